```python
import jax
import jax.numpy as jnp
from jax import lax
import numpy as np

D_MODEL = 1024
BATCH = 32
SEQ = 2048
DEPTH = 1

CTX_LEN = 256
GRID_W = 64
N_SUBLAYERS = 3
D_FF = 2816
FFN_HALF = 0.5
GLA_HEADS = 4
GLA_DK = 64
GLA_DV = 128
GLA_GATE_RANK = 16
GLA_GATE_NORMALIZER = 16.0
HGRN_HEADS = 4
HGRN_DK = 128
HGRN_DV = 128
GLA_QK = GLA_HEADS * GLA_DK
GLA_V = GLA_HEADS * GLA_DV
HGRN_K = HGRN_HEADS * HGRN_DK
HGRN_V = HGRN_HEADS * HGRN_DV
MIX_WIDTH = GLA_V + HGRN_V
IN_SPLITS = (GLA_QK, GLA_QK, GLA_V, GLA_V, GLA_GATE_RANK, GLA_GATE_RANK, HGRN_K, HGRN_K, HGRN_K, HGRN_V, HGRN_V)
IN_WIDTH = sum(IN_SPLITS)
CHUNK = 32
LN_EPS = 1e-5
NORM_EPS = 1e-6
POS_THETA = 10000.0
DN_ALPHA = (2.0 * DEPTH) ** 0.25
DN_BETA = (8.0 * DEPTH) ** -0.25

kernel_name = "hybrid_gla_hgrn2_dit_layer"


def layer_norm(x, gain=None, bias=None):
    xf = x.astype(jnp.float32)
    mu = jnp.mean(xf, axis=-1, keepdims=True)
    var = jnp.mean(jnp.square(xf - mu), axis=-1, keepdims=True)
    y = (xf - mu) * lax.rsqrt(var + LN_EPS)
    if gain is not None:
        y = y * gain.astype(jnp.float32) + bias.astype(jnp.float32)
    return y.astype(x.dtype)


def modulate(x, m, i):
    return layer_norm(x) * (1.0 + m[:, 3 * i][:, None, :]) + m[:, 3 * i + 1][:, None, :]


def residual_post_norm(x, y, gate, weight, gain, bias):
    return layer_norm(DN_ALPHA * x + weight * gate[:, None, :] * y, gain, bias)


def swiglu(h, w_in, w_out):
    gate, up = jnp.split(h @ w_in, 2, axis=-1)
    return (jax.nn.silu(gate) * up) @ w_out


def sincos_2d(rows, width, dim):
    r = jnp.repeat(jnp.arange(rows), width)
    col = jnp.tile(jnp.arange(width), rows)
    quarter = dim // 4
    omega = 1.0 / POS_THETA ** (jnp.arange(quarter, dtype=jnp.float32) / quarter)

    def emb(p):
        a = p.astype(jnp.float32)[:, None] * omega[None, :]
        return jnp.concatenate([jnp.sin(a), jnp.cos(a)], axis=-1)

    return jnp.concatenate([emb(r), emb(col)], axis=-1)


def to_heads(a, n_heads):
    b, t, _ = a.shape
    return a.reshape(b, t, n_heads, -1).transpose(0, 2, 1, 3)


def from_heads(a):
    b, h, t, d = a.shape
    return a.transpose(0, 2, 1, 3).reshape(b, t, h * d)


def chunk_gated_linear_attention(q, k, v, log_f, s0):
    bsz, nh, t, _ = q.shape
    dv = v.shape[-1]
    n_chunks = t // CHUNK

    def split(a):
        return a.astype(jnp.float32).reshape(bsz, nh, n_chunks, CHUNK, a.shape[-1]).transpose(2, 0, 1, 3, 4)

    lower_tri = jnp.tril(jnp.ones((CHUNK, CHUNK), dtype=bool))

    def step(s, blk):
        qb, kb, vb, gb = blk
        b = jnp.cumsum(gb, axis=-2)
        b_last = b[..., -1:, :]
        q_dec = qb * jnp.exp(b)
        k_inv = kb * jnp.exp(-b)
        k_tail = kb * jnp.exp(b_last - b)
        att = jnp.where(lower_tri, jnp.einsum("bhik,bhjk->bhij", q_dec, k_inv), 0.0)
        o = jnp.einsum("bhij,bhjv->bhiv", att, vb) + jnp.einsum("bhik,bhkv->bhiv", q_dec, s)
        s_new = jnp.swapaxes(jnp.exp(b_last), -1, -2) * s + jnp.einsum("bhjk,bhjv->bhkv", k_tail, vb)
        return s_new, o

    s_final, o = lax.scan(step, s0.astype(jnp.float32), (split(q), split(k), split(v), split(log_f)))
    return o.transpose(1, 2, 0, 3, 4).reshape(bsz, nh, t, dv), s_final


def bidirectional_scan(lat, ctx):
    q, k_f, k_b, v, g_f, g_b = lat
    qc, kc_f, kc_b, vc, gc_f, gc_b = ctx

    def flip(a):
        return jnp.flip(a, axis=2)

    zero = jnp.zeros(qc.shape[:2] + (qc.shape[-1], vc.shape[-1]), jnp.float32)
    oc_f, sc_f = chunk_gated_linear_attention(qc, kc_f, vc, gc_f, zero)
    oc_b, sc_b = chunk_gated_linear_attention(flip(qc), flip(kc_b), flip(vc), flip(gc_b), zero)
    o_f, _ = chunk_gated_linear_attention(q, k_f, v, g_f, sc_f)
    o_b, _ = chunk_gated_linear_attention(flip(q), flip(k_b), flip(v), flip(g_b), sc_b)
    return o_f + flip(o_b), oc_f + flip(oc_b)


def mixer_features(h, w_in, a2_f, a2_b, a_bias_f, a_bias_b, lb_f, lb_b):
    split_points = np.cumsum(IN_SPLITS)[:-1].tolist()
    (g_q, g_k, g_v, g_gate, g_lr_f, g_lr_b, r_q, r_f_f, r_f_b, r_i, r_gate) = jnp.split(h @ w_in, split_points, axis=-1)

    def gla_log_decay(lr, a2, bias):
        return to_heads(jax.nn.log_sigmoid((lr @ a2 + bias).astype(jnp.float32)) / GLA_GATE_NORMALIZER, GLA_HEADS)

    def hgrn_forget(z, lb):
        f = lb + (1.0 - lb) * jax.nn.sigmoid(z.astype(jnp.float32))
        return to_heads(jnp.log(f), HGRN_HEADS), to_heads(1.0 - f, HGRN_HEADS)

    k_gla = to_heads(g_k, GLA_HEADS)
    gla = (to_heads(g_q * GLA_DK ** -0.5, GLA_HEADS), k_gla, k_gla, to_heads(g_v, GLA_HEADS),
           gla_log_decay(g_lr_f, a2_f, a_bias_f), gla_log_decay(g_lr_b, a2_b, a_bias_b))
    logf_f, k_f = hgrn_forget(r_f_f, lb_f)
    logf_b, k_b = hgrn_forget(r_f_b, lb_b)
    hgrn = (to_heads(jax.nn.silu(r_q) * HGRN_DK ** -0.5, HGRN_HEADS), k_f, k_b, to_heads(r_i, HGRN_HEADS), logf_f, logf_b)
    return gla, hgrn, g_gate, r_gate


def gated_head_norm(o, gain, gate):
    o = o * lax.rsqrt(jnp.mean(jnp.square(o), axis=-1, keepdims=True) + NORM_EPS) * gain.astype(jnp.float32)
    return from_heads(o) * jax.nn.silu(gate.astype(jnp.float32))


def token_mixer(h, hc, with_ctx_output, w_in, a2_f, a2_b, a_bias_f, a_bias_b, lb_f, lb_b, gla_gain, hgrn_gain, w_out):
    gla, hgrn, gla_gate, hgrn_gate = mixer_features(h, w_in, a2_f, a2_b, a_bias_f, a_bias_b, lb_f, lb_b)
    gla_c, hgrn_c, gla_gate_c, hgrn_gate_c = mixer_features(hc, w_in, a2_f, a2_b, a_bias_f, a_bias_b, lb_f, lb_b)
    o_gla, oc_gla = bidirectional_scan(gla, gla_c)
    o_hgrn, oc_hgrn = bidirectional_scan(hgrn, hgrn_c)

    def project(o1, g1, o2, g2):
        merged = jnp.concatenate([gated_head_norm(o1, gla_gain, g1), gated_head_norm(o2, hgrn_gain, g2)], axis=-1)
        return merged.astype(w_out.dtype) @ w_out

    y = project(o_gla, gla_gate, o_hgrn, hgrn_gate)
    yc = project(oc_gla, gla_gate_c, oc_hgrn, hgrn_gate_c) if with_ctx_output else None
    return y, yc


def trunk_layer(x, xc, m, mc, update_ctx, ln_gain, ln_bias, ffn1_w_in, ffn1_w_out, w_mix_in,
                a2_f, a2_b, a_bias_f, a_bias_b, lb_f, lb_b, gla_gain, hgrn_gain, w_mix_out, ffn2_w_in, ffn2_w_out):
    x = residual_post_norm(x, swiglu(modulate(x, m, 0), ffn1_w_in, ffn1_w_out), m[:, 2], FFN_HALF, ln_gain[0], ln_bias[0])
    xc = residual_post_norm(xc, swiglu(modulate(xc, mc, 0), ffn1_w_in, ffn1_w_out), mc[:, 2], FFN_HALF, ln_gain[0], ln_bias[0])
    y, yc = token_mixer(modulate(x, m, 1), modulate(xc, mc, 1), update_ctx, w_mix_in, a2_f, a2_b, a_bias_f, a_bias_b,
                        lb_f, lb_b, gla_gain, hgrn_gain, w_mix_out)
    x = residual_post_norm(x, y, m[:, 5], 1.0, ln_gain[1], ln_bias[1])
    x = residual_post_norm(x, swiglu(modulate(x, m, 2), ffn2_w_in, ffn2_w_out), m[:, 8], FFN_HALF, ln_gain[2], ln_bias[2])
    if update_ctx:
        xc = residual_post_norm(xc, yc, mc[:, 5], 1.0, ln_gain[1], ln_bias[1])
        xc = residual_post_norm(xc, swiglu(modulate(xc, mc, 2), ffn2_w_in, ffn2_w_out), mc[:, 8], FFN_HALF, ln_gain[2], ln_bias[2])
    return x, xc


def _fwd_setup_inputs(seed: int = 0) -> dict:
    key = jax.random.key(seed)
    ks = jax.random.split(key, 24)
    f32 = jnp.float32

    def w(k, shape, fan_in, scale=1.0):
        return jax.random.normal(k, shape, f32) * (scale * fan_in ** -0.5)

    def noise(k, shape, scale):
        return jax.random.normal(k, shape, f32) * scale

    n_mod = 3 * N_SUBLAYERS
    return {
        "x": jax.random.normal(ks[0], (BATCH, SEQ, D_MODEL), f32),
        "c": jax.random.normal(ks[1], (BATCH, D_MODEL), f32),
        "ctx": jax.random.normal(ks[2], (BATCH, CTX_LEN, D_MODEL), f32),
        "c_ctx": jax.random.normal(ks[3], (D_MODEL,), f32),
        "w_ada": w(ks[4], (DEPTH, D_MODEL, n_mod * D_MODEL), D_MODEL, 0.5),
        "b_ada": noise(ks[5], (DEPTH, n_mod * D_MODEL), 0.02),
        "ln_gain": 1.0 + noise(ks[6], (DEPTH, N_SUBLAYERS, D_MODEL), 0.05),
        "ln_bias": noise(ks[7], (DEPTH, N_SUBLAYERS, D_MODEL), 0.02),
        "ffn1_w_in": w(ks[8], (DEPTH, D_MODEL, 2 * D_FF), D_MODEL),
        "ffn1_w_out": w(ks[9], (DEPTH, D_FF, D_MODEL), D_FF, DN_BETA),
        "w_mix_in": w(ks[10], (DEPTH, D_MODEL, IN_WIDTH), D_MODEL),
        "gla_a2_fwd": w(ks[11], (DEPTH, GLA_GATE_RANK, GLA_QK), GLA_GATE_RANK),
        "gla_a2_bwd": w(ks[12], (DEPTH, GLA_GATE_RANK, GLA_QK), GLA_GATE_RANK),
        "gla_a_bias_fwd": noise(ks[13], (DEPTH, GLA_QK), 0.1),
        "gla_a_bias_bwd": noise(ks[14], (DEPTH, GLA_QK), 0.1),
        "hgrn_lb_logits": noise(ks[15], (2, DEPTH + 1, HGRN_K), 0.1),
        "gla_norm_gain": 1.0 + noise(ks[16], (DEPTH, GLA_DV), 0.05),
        "hgrn_norm_gain": 1.0 + noise(ks[17], (DEPTH, HGRN_DV), 0.05),
        "w_mix_out": w(ks[18], (DEPTH, MIX_WIDTH, D_MODEL), MIX_WIDTH, DN_BETA),
        "ffn2_w_in": w(ks[19], (DEPTH, D_MODEL, 2 * D_FF), D_MODEL),
        "ffn2_w_out": w(ks[20], (DEPTH, D_FF, D_MODEL), D_FF, DN_BETA),
    }


def _fwd_reference(x, c, ctx, c_ctx, w_ada, b_ada, ln_gain, ln_bias, ffn1_w_in, ffn1_w_out, w_mix_in,
              gla_a2_fwd, gla_a2_bwd, gla_a_bias_fwd, gla_a_bias_bwd, hgrn_lb_logits,
              gla_norm_gain, hgrn_norm_gain, w_mix_out, ffn2_w_in, ffn2_w_out):
    bsz, n_tok, d = x.shape
    rows = n_tok // GRID_W
    x = x + sincos_2d(rows, GRID_W, d).astype(x.dtype)[None]
    xc = ctx
    lb = jnp.cumsum(jax.nn.softmax(hgrn_lb_logits.astype(jnp.float32), axis=1), axis=1)
    n_mod = 3 * N_SUBLAYERS
    for l in range(DEPTH):
        m = (jax.nn.silu(c) @ w_ada[l] + b_ada[l]).reshape(bsz, n_mod, d)
        mc = (jax.nn.silu(c_ctx) @ w_ada[l] + b_ada[l]).reshape(1, n_mod, d)
        x, xc = trunk_layer(x, xc, m, mc, l < DEPTH - 1, ln_gain[l], ln_bias[l], ffn1_w_in[l], ffn1_w_out[l], w_mix_in[l],
                            gla_a2_fwd[l], gla_a2_bwd[l], gla_a_bias_fwd[l], gla_a_bias_bwd[l], lb[0, l], lb[1, l],
                            gla_norm_gain[l], hgrn_norm_gain[l], w_mix_out[l], ffn2_w_in[l], ffn2_w_out[l])
    return x


import jax as _jax
import jax.numpy as _jnp

TWIN_FORMAT = 'train_step'
FWD_PARAMS = ['x', 'c', 'ctx', 'c_ctx', 'w_ada', 'b_ada', 'ln_gain', 'ln_bias', 'ffn1_w_in', 'ffn1_w_out', 'w_mix_in', 'gla_a2_fwd', 'gla_a2_bwd', 'gla_a_bias_fwd', 'gla_a_bias_bwd', 'hgrn_lb_logits', 'gla_norm_gain', 'hgrn_norm_gain', 'w_mix_out', 'ffn2_w_in', 'ffn2_w_out']
TWIN_WEIGHTS = ['c_ctx', 'w_ada', 'b_ada', 'ln_gain', 'ln_bias', 'ffn1_w_in', 'ffn1_w_out', 'w_mix_in', 'gla_a2_fwd', 'gla_a2_bwd', 'gla_a_bias_fwd', 'gla_a_bias_bwd', 'hgrn_lb_logits', 'gla_norm_gain', 'hgrn_norm_gain', 'w_mix_out', 'ffn2_w_in', 'ffn2_w_out']
TWIN_DIFF_INPUT = 'x'
TWIN_INPUTS = ['x', 'c', 'ctx', 'c_ctx', 'w_ada', 'b_ada', 'ln_gain', 'ln_bias', 'ffn1_w_in', 'ffn1_w_out', 'w_mix_in', 'gla_a2_fwd', 'gla_a2_bwd', 'gla_a_bias_fwd', 'gla_a_bias_bwd', 'hgrn_lb_logits', 'gla_norm_gain', 'hgrn_norm_gain', 'w_mix_out', 'ffn2_w_in', 'ffn2_w_out', 'loss_target', 'm_c_ctx', 'm_w_ada', 'm_b_ada', 'm_ln_gain', 'm_ln_bias', 'm_ffn1_w_in', 'm_ffn1_w_out', 'm_w_mix_in', 'm_gla_a2_fwd', 'm_gla_a2_bwd', 'm_gla_a_bias_fwd', 'm_gla_a_bias_bwd', 'm_hgrn_lb_logits', 'm_gla_norm_gain', 'm_hgrn_norm_gain', 'm_w_mix_out', 'm_ffn2_w_in', 'm_ffn2_w_out', 'v_c_ctx', 'v_w_ada', 'v_b_ada', 'v_ln_gain', 'v_ln_bias', 'v_ffn1_w_in', 'v_ffn1_w_out', 'v_w_mix_in', 'v_gla_a2_fwd', 'v_gla_a2_bwd', 'v_gla_a_bias_fwd', 'v_gla_a_bias_bwd', 'v_hgrn_lb_logits', 'v_gla_norm_gain', 'v_hgrn_norm_gain', 'v_w_mix_out', 'v_ffn2_w_in', 'v_ffn2_w_out']
TWIN_OUTPUTS = ['loss', 'grad_x', 'grad_c_ctx', 'grad_w_ada', 'grad_b_ada', 'grad_ln_gain', 'grad_ln_bias', 'grad_ffn1_w_in', 'grad_ffn1_w_out', 'grad_w_mix_in', 'grad_gla_a2_fwd', 'grad_gla_a2_bwd', 'grad_gla_a_bias_fwd', 'grad_gla_a_bias_bwd', 'grad_hgrn_lb_logits', 'grad_gla_norm_gain', 'grad_hgrn_norm_gain', 'grad_w_mix_out', 'grad_ffn2_w_in', 'grad_ffn2_w_out', 'delta_c_ctx', 'delta_w_ada', 'delta_b_ada', 'delta_ln_gain', 'delta_ln_bias', 'delta_ffn1_w_in', 'delta_ffn1_w_out', 'delta_w_mix_in', 'delta_gla_a2_fwd', 'delta_gla_a2_bwd', 'delta_gla_a_bias_fwd', 'delta_gla_a_bias_bwd', 'delta_hgrn_lb_logits', 'delta_gla_norm_gain', 'delta_hgrn_norm_gain', 'delta_w_mix_out', 'delta_ffn2_w_in', 'delta_ffn2_w_out', 'new_m_c_ctx', 'new_m_w_ada', 'new_m_b_ada', 'new_m_ln_gain', 'new_m_ln_bias', 'new_m_ffn1_w_in', 'new_m_ffn1_w_out', 'new_m_w_mix_in', 'new_m_gla_a2_fwd', 'new_m_gla_a2_bwd', 'new_m_gla_a_bias_fwd', 'new_m_gla_a_bias_bwd', 'new_m_hgrn_lb_logits', 'new_m_gla_norm_gain', 'new_m_hgrn_norm_gain', 'new_m_w_mix_out', 'new_m_ffn2_w_in', 'new_m_ffn2_w_out', 'new_v_c_ctx', 'new_v_w_ada', 'new_v_b_ada', 'new_v_ln_gain', 'new_v_ln_bias', 'new_v_ffn1_w_in', 'new_v_ffn1_w_out', 'new_v_w_mix_in', 'new_v_gla_a2_fwd', 'new_v_gla_a2_bwd', 'new_v_gla_a_bias_fwd', 'new_v_gla_a_bias_bwd', 'new_v_hgrn_lb_logits', 'new_v_gla_norm_gain', 'new_v_hgrn_norm_gain', 'new_v_w_mix_out', 'new_v_ffn2_w_in', 'new_v_ffn2_w_out']
TWIN_LEAF_KINDS = {'loss': 'loss', 'grad_x': 'grad_x', 'grad_c_ctx': 'grad_w', 'grad_w_ada': 'grad_w', 'grad_b_ada': 'grad_w', 'grad_ln_gain': 'grad_w', 'grad_ln_bias': 'grad_w', 'grad_ffn1_w_in': 'grad_w', 'grad_ffn1_w_out': 'grad_w', 'grad_w_mix_in': 'grad_w', 'grad_gla_a2_fwd': 'grad_w', 'grad_gla_a2_bwd': 'grad_w', 'grad_gla_a_bias_fwd': 'grad_w', 'grad_gla_a_bias_bwd': 'grad_w', 'grad_hgrn_lb_logits': 'grad_w', 'grad_gla_norm_gain': 'grad_w', 'grad_hgrn_norm_gain': 'grad_w', 'grad_w_mix_out': 'grad_w', 'grad_ffn2_w_in': 'grad_w', 'grad_ffn2_w_out': 'grad_w', 'delta_c_ctx': 'delta_w', 'delta_w_ada': 'delta_w', 'delta_b_ada': 'delta_w', 'delta_ln_gain': 'delta_w', 'delta_ln_bias': 'delta_w', 'delta_ffn1_w_in': 'delta_w', 'delta_ffn1_w_out': 'delta_w', 'delta_w_mix_in': 'delta_w', 'delta_gla_a2_fwd': 'delta_w', 'delta_gla_a2_bwd': 'delta_w', 'delta_gla_a_bias_fwd': 'delta_w', 'delta_gla_a_bias_bwd': 'delta_w', 'delta_hgrn_lb_logits': 'delta_w', 'delta_gla_norm_gain': 'delta_w', 'delta_hgrn_norm_gain': 'delta_w', 'delta_w_mix_out': 'delta_w', 'delta_ffn2_w_in': 'delta_w', 'delta_ffn2_w_out': 'delta_w', 'new_m_c_ctx': 'new_m', 'new_m_w_ada': 'new_m', 'new_m_b_ada': 'new_m', 'new_m_ln_gain': 'new_m', 'new_m_ln_bias': 'new_m', 'new_m_ffn1_w_in': 'new_m', 'new_m_ffn1_w_out': 'new_m', 'new_m_w_mix_in': 'new_m', 'new_m_gla_a2_fwd': 'new_m', 'new_m_gla_a2_bwd': 'new_m', 'new_m_gla_a_bias_fwd': 'new_m', 'new_m_gla_a_bias_bwd': 'new_m', 'new_m_hgrn_lb_logits': 'new_m', 'new_m_gla_norm_gain': 'new_m', 'new_m_hgrn_norm_gain': 'new_m', 'new_m_w_mix_out': 'new_m', 'new_m_ffn2_w_in': 'new_m', 'new_m_ffn2_w_out': 'new_m', 'new_v_c_ctx': 'new_v', 'new_v_w_ada': 'new_v', 'new_v_b_ada': 'new_v', 'new_v_ln_gain': 'new_v', 'new_v_ln_bias': 'new_v', 'new_v_ffn1_w_in': 'new_v', 'new_v_ffn1_w_out': 'new_v', 'new_v_w_mix_in': 'new_v', 'new_v_gla_a2_fwd': 'new_v', 'new_v_gla_a2_bwd': 'new_v', 'new_v_gla_a_bias_fwd': 'new_v', 'new_v_gla_a_bias_bwd': 'new_v', 'new_v_hgrn_lb_logits': 'new_v', 'new_v_gla_norm_gain': 'new_v', 'new_v_hgrn_norm_gain': 'new_v', 'new_v_w_mix_out': 'new_v', 'new_v_ffn2_w_in': 'new_v', 'new_v_ffn2_w_out': 'new_v'}


def _forward(args):
    return _fwd_reference(*[args[k] for k in FWD_PARAMS])


def _output_shape():
    out = _jax.eval_shape(lambda: _forward(_fwd_setup_inputs(0)))
    return out.shape, out.dtype

N_MICROBATCH = 1
ADAM_LR = 0.001
ADAM_B1 = 0.9
ADAM_B2 = 0.999
ADAM_EPS = 1e-08
ADAM_WD = 0.01
ADAM_STEP = 10
PER_EXAMPLE_BATCH_AXIS = {'x': 0, 'c': 0, 'ctx': 0, 'loss_target': 0}
SHARED_INPUTS = []
_WEIGHT_DTYPES = {'c_ctx': _jnp.float32, 'w_ada': _jnp.float32, 'b_ada': _jnp.float32, 'ln_gain': _jnp.float32, 'ln_bias': _jnp.float32, 'ffn1_w_in': _jnp.float32, 'ffn1_w_out': _jnp.float32, 'w_mix_in': _jnp.float32, 'gla_a2_fwd': _jnp.float32, 'gla_a2_bwd': _jnp.float32, 'gla_a_bias_fwd': _jnp.float32, 'gla_a_bias_bwd': _jnp.float32, 'hgrn_lb_logits': _jnp.float32, 'gla_norm_gain': _jnp.float32, 'hgrn_norm_gain': _jnp.float32, 'w_mix_out': _jnp.float32, 'ffn2_w_in': _jnp.float32, 'ffn2_w_out': _jnp.float32}
MOMENT_SCALE = {'c_ctx': 2.637056e-03, 'w_ada': 3.908899e-02, 'b_ada': 7.412706e-02, 'ln_gain': 3.821185e+01, 'ln_bias': 1.286143e+01, 'ffn1_w_in': 8.175550e-03, 'ffn1_w_out': 2.305780e-02, 'w_mix_in': 3.296512e-02, 'gla_a2_fwd': 9.445676e-03, 'gla_a2_bwd': 1.706317e-02, 'gla_a_bias_fwd': 1.901995e-02, 'gla_a_bias_bwd': 1.890488e-02, 'hgrn_lb_logits': 8.260948e-04, 'gla_norm_gain': 5.187735e-02, 'hgrn_norm_gain': 5.206687e-02, 'w_mix_out': 5.524981e-02, 'ffn2_w_in': 9.839201e-03, 'ffn2_w_out': 2.785884e-02}


def _to_microbatches(a, axis):
    t = _jnp.moveaxis(a, axis, 0)
    t = t.reshape((N_MICROBATCH, t.shape[0] // N_MICROBATCH) + t.shape[1:])
    return _jnp.moveaxis(t, 1, axis + 1)


def setup_inputs(seed: int = 0) -> dict:
    inp = _fwd_setup_inputs(seed)
    key = _jax.random.fold_in(_jax.random.key(seed), 7919)
    shape, _ = _output_shape()
    out = dict(inp)
    out["loss_target"] = _jax.random.normal(_jax.random.fold_in(key, 0), shape, _jnp.float32)
    for i, name in enumerate(TWIN_WEIGHTS):
        w = inp[name].astype(_jnp.float32)
        if MOMENT_SCALE is None:
            s = _jnp.sqrt(_jnp.mean(_jnp.square(w)) + 1e-30)
        else:
            s = MOMENT_SCALE[name]
        km, kv = _jax.random.split(_jax.random.fold_in(key, i + 1))
        out[name] = w
        out["m_" + name] = s * _jax.random.normal(km, w.shape, _jnp.float32)
        out["v_" + name] = (s * s) * _jax.random.uniform(kv, w.shape, _jnp.float32, 0.5, 1.5)
    if N_MICROBATCH > 1:
        for name, axis in PER_EXAMPLE_BATCH_AXIS.items():
            out[name] = _to_microbatches(out[name], axis)
    return {'x': out['x'], 'c': out['c'], 'ctx': out['ctx'], 'c_ctx': out['c_ctx'], 'w_ada': out['w_ada'], 'b_ada': out['b_ada'], 'ln_gain': out['ln_gain'], 'ln_bias': out['ln_bias'], 'ffn1_w_in': out['ffn1_w_in'], 'ffn1_w_out': out['ffn1_w_out'], 'w_mix_in': out['w_mix_in'], 'gla_a2_fwd': out['gla_a2_fwd'], 'gla_a2_bwd': out['gla_a2_bwd'], 'gla_a_bias_fwd': out['gla_a_bias_fwd'], 'gla_a_bias_bwd': out['gla_a_bias_bwd'], 'hgrn_lb_logits': out['hgrn_lb_logits'], 'gla_norm_gain': out['gla_norm_gain'], 'hgrn_norm_gain': out['hgrn_norm_gain'], 'w_mix_out': out['w_mix_out'], 'ffn2_w_in': out['ffn2_w_in'], 'ffn2_w_out': out['ffn2_w_out'], 'loss_target': out['loss_target'], 'm_c_ctx': out['m_c_ctx'], 'm_w_ada': out['m_w_ada'], 'm_b_ada': out['m_b_ada'], 'm_ln_gain': out['m_ln_gain'], 'm_ln_bias': out['m_ln_bias'], 'm_ffn1_w_in': out['m_ffn1_w_in'], 'm_ffn1_w_out': out['m_ffn1_w_out'], 'm_w_mix_in': out['m_w_mix_in'], 'm_gla_a2_fwd': out['m_gla_a2_fwd'], 'm_gla_a2_bwd': out['m_gla_a2_bwd'], 'm_gla_a_bias_fwd': out['m_gla_a_bias_fwd'], 'm_gla_a_bias_bwd': out['m_gla_a_bias_bwd'], 'm_hgrn_lb_logits': out['m_hgrn_lb_logits'], 'm_gla_norm_gain': out['m_gla_norm_gain'], 'm_hgrn_norm_gain': out['m_hgrn_norm_gain'], 'm_w_mix_out': out['m_w_mix_out'], 'm_ffn2_w_in': out['m_ffn2_w_in'], 'm_ffn2_w_out': out['m_ffn2_w_out'], 'v_c_ctx': out['v_c_ctx'], 'v_w_ada': out['v_w_ada'], 'v_b_ada': out['v_b_ada'], 'v_ln_gain': out['v_ln_gain'], 'v_ln_bias': out['v_ln_bias'], 'v_ffn1_w_in': out['v_ffn1_w_in'], 'v_ffn1_w_out': out['v_ffn1_w_out'], 'v_w_mix_in': out['v_w_mix_in'], 'v_gla_a2_fwd': out['v_gla_a2_fwd'], 'v_gla_a2_bwd': out['v_gla_a2_bwd'], 'v_gla_a_bias_fwd': out['v_gla_a_bias_fwd'], 'v_gla_a_bias_bwd': out['v_gla_a_bias_bwd'], 'v_hgrn_lb_logits': out['v_hgrn_lb_logits'], 'v_gla_norm_gain': out['v_gla_norm_gain'], 'v_hgrn_norm_gain': out['v_hgrn_norm_gain'], 'v_w_mix_out': out['v_w_mix_out'], 'v_ffn2_w_in': out['v_ffn2_w_in'], 'v_ffn2_w_out': out['v_ffn2_w_out']}


def _loss(weights, diff, rest, loss_target):
    with _jax.named_scope("forward"):
        args = {**rest, TWIN_DIFF_INPUT: diff, **{k: w.astype(_WEIGHT_DTYPES[k]) for k, w in weights.items()}}
        y = _forward(args)
    with _jax.named_scope("loss_head"):
        err = _jnp.square(y.astype(_jnp.float32) - loss_target)
        return 0.5 * _jnp.sum(_jnp.mean(err, axis=-1)) if err.ndim else 0.5 * err


def _adamw(w, g, m, v):
    m = ADAM_B1 * m + (1.0 - ADAM_B1) * g
    v = ADAM_B2 * v + (1.0 - ADAM_B2) * _jnp.square(g)
    m_hat = m / (1.0 - ADAM_B1 ** ADAM_STEP)
    v_hat = v / (1.0 - ADAM_B2 ** ADAM_STEP)
    delta = -ADAM_LR * (m_hat / (_jnp.sqrt(v_hat) + ADAM_EPS) + ADAM_WD * w)
    return delta, m, v


def reference(x, c, ctx, c_ctx, w_ada, b_ada, ln_gain, ln_bias, ffn1_w_in, ffn1_w_out, w_mix_in, gla_a2_fwd, gla_a2_bwd, gla_a_bias_fwd, gla_a_bias_bwd, hgrn_lb_logits, gla_norm_gain, hgrn_norm_gain, w_mix_out, ffn2_w_in, ffn2_w_out, loss_target, m_c_ctx, m_w_ada, m_b_ada, m_ln_gain, m_ln_bias, m_ffn1_w_in, m_ffn1_w_out, m_w_mix_in, m_gla_a2_fwd, m_gla_a2_bwd, m_gla_a_bias_fwd, m_gla_a_bias_bwd, m_hgrn_lb_logits, m_gla_norm_gain, m_hgrn_norm_gain, m_w_mix_out, m_ffn2_w_in, m_ffn2_w_out, v_c_ctx, v_w_ada, v_b_ada, v_ln_gain, v_ln_bias, v_ffn1_w_in, v_ffn1_w_out, v_w_mix_in, v_gla_a2_fwd, v_gla_a2_bwd, v_gla_a_bias_fwd, v_gla_a_bias_bwd, v_hgrn_lb_logits, v_gla_norm_gain, v_hgrn_norm_gain, v_w_mix_out, v_ffn2_w_in, v_ffn2_w_out):
    given = dict(x=x, c=c, ctx=ctx, c_ctx=c_ctx, w_ada=w_ada, b_ada=b_ada, ln_gain=ln_gain, ln_bias=ln_bias, ffn1_w_in=ffn1_w_in, ffn1_w_out=ffn1_w_out, w_mix_in=w_mix_in, gla_a2_fwd=gla_a2_fwd, gla_a2_bwd=gla_a2_bwd, gla_a_bias_fwd=gla_a_bias_fwd, gla_a_bias_bwd=gla_a_bias_bwd, hgrn_lb_logits=hgrn_lb_logits, gla_norm_gain=gla_norm_gain, hgrn_norm_gain=hgrn_norm_gain, w_mix_out=w_mix_out, ffn2_w_in=ffn2_w_in, ffn2_w_out=ffn2_w_out, loss_target=loss_target, m_c_ctx=m_c_ctx, m_w_ada=m_w_ada, m_b_ada=m_b_ada, m_ln_gain=m_ln_gain, m_ln_bias=m_ln_bias, m_ffn1_w_in=m_ffn1_w_in, m_ffn1_w_out=m_ffn1_w_out, m_w_mix_in=m_w_mix_in, m_gla_a2_fwd=m_gla_a2_fwd, m_gla_a2_bwd=m_gla_a2_bwd, m_gla_a_bias_fwd=m_gla_a_bias_fwd, m_gla_a_bias_bwd=m_gla_a_bias_bwd, m_hgrn_lb_logits=m_hgrn_lb_logits, m_gla_norm_gain=m_gla_norm_gain, m_hgrn_norm_gain=m_hgrn_norm_gain, m_w_mix_out=m_w_mix_out, m_ffn2_w_in=m_ffn2_w_in, m_ffn2_w_out=m_ffn2_w_out, v_c_ctx=v_c_ctx, v_w_ada=v_w_ada, v_b_ada=v_b_ada, v_ln_gain=v_ln_gain, v_ln_bias=v_ln_bias, v_ffn1_w_in=v_ffn1_w_in, v_ffn1_w_out=v_ffn1_w_out, v_w_mix_in=v_w_mix_in, v_gla_a2_fwd=v_gla_a2_fwd, v_gla_a2_bwd=v_gla_a2_bwd, v_gla_a_bias_fwd=v_gla_a_bias_fwd, v_gla_a_bias_bwd=v_gla_a_bias_bwd, v_hgrn_lb_logits=v_hgrn_lb_logits, v_gla_norm_gain=v_gla_norm_gain, v_hgrn_norm_gain=v_hgrn_norm_gain, v_w_mix_out=v_w_mix_out, v_ffn2_w_in=v_ffn2_w_in, v_ffn2_w_out=v_ffn2_w_out)
    weights = {n: given[n] for n in TWIN_WEIGHTS}
    shared = {n: given[n] for n in SHARED_INPUTS}
    per_example = {n: given[n] for n in ['x', 'c', 'ctx']}
    grad_fn = _jax.value_and_grad(_loss, argnums=(0, 1))

    def one_microbatch(ex, loss_target):
        ex = dict(ex)
        diff = ex.pop(TWIN_DIFF_INPUT)
        return grad_fn(weights, diff, {**shared, **ex}, loss_target)

    if N_MICROBATCH == 1:
        loss, (grad_w, grad_x) = one_microbatch(per_example, given["loss_target"])
    else:
        def body(carry, xs):
            loss_sum, grad_sum = carry
            l_k, (gw_k, gx_k) = one_microbatch(xs[0], xs[1])
            with _jax.named_scope("update"):
                return (loss_sum + l_k, _jax.tree.map(_jnp.add, grad_sum, gw_k)), gx_k

        init = (_jnp.zeros((), _jnp.float32), _jax.tree.map(_jnp.zeros_like, weights))
        (loss, grad_w), grad_x = _jax.lax.scan(body, init, (per_example, given["loss_target"]))
    with _jax.named_scope("update"):
        delta_w, new_m, new_v = {}, {}, {}
        for n in TWIN_WEIGHTS:
            delta_w[n], new_m[n], new_v[n] = _adamw(weights[n], grad_w[n], given["m_" + n], given["v_" + n])
    return (loss, grad_x, *[grad_w[n] for n in TWIN_WEIGHTS], *[delta_w[n] for n in TWIN_WEIGHTS],
            *[new_m[n] for n in TWIN_WEIGHTS], *[new_v[n] for n in TWIN_WEIGHTS])
```

```python
import functools

import numpy as np
import jax
import jax.numpy as jnp
from jax import lax
from jax.experimental import pallas as pl
from jax.experimental.pallas import tpu as pltpu

F32 = jnp.float32
BF16 = jnp.bfloat16
MESH = pl.DeviceIdType.MESH

D = 1024
DFF = 2816
TM = 256
CH = 32
NCB = TM // CH
HD = 128
NH = 8
LN_EPS = 1e-5
NORM_EPS = 1e-6
ALPHA = 2.0 ** 0.25
GATE_NORM = 16.0
GLA_DK = 64
N_MOD = 9
VMEM_LIMIT = 52 * 1024 * 1024

MIXP = 5120
GG, RG, GQ, GK, GV, RQ, RFF, RFB, RI, LR = 0, 512, 1024, 1536, 2048, 2560, 3072, 3584, 4096, 4608
IN_SPLITS = (256, 256, 512, 512, 16, 16, 512, 512, 512, 512, 512)

ADAM_LR, ADAM_B1, ADAM_B2, ADAM_EPS, ADAM_WD, ADAM_STEP = 0.001, 0.9, 0.999, 1e-08, 0.01, 10


def _cp(n_axes):
    return pltpu.CompilerParams(dimension_semantics=("arbitrary",) * n_axes, vmem_limit_bytes=VMEM_LIMIT)


def _rowmap(stride, off):
    return lambda b, i: (b * stride + off + i, 0)


def _mmap(comb):
    if comb:
        return lambda b, i: (b, jnp.minimum(i, 1), 0, 0)
    return lambda b, i: (b, 1, 0, 0)


def _ln(x):
    mu = jnp.mean(x, axis=-1, keepdims=True)
    xc = x - mu
    var = jnp.mean(xc * xc, axis=-1, keepdims=True)
    r = lax.rsqrt(var + LN_EPS)
    return xc * r, r


def _ln_bwd(dxh, xh, r):
    return r * (dxh - jnp.mean(dxh, axis=-1, keepdims=True) - xh * jnp.mean(dxh * xh, axis=-1, keepdims=True))


def _sigmoid(x):
    return 1.0 / (1.0 + jnp.exp(-x))


def _rsum(x):
    return jnp.sum(x, axis=0, keepdims=True)


def _dot(a, b):
    return jnp.dot(a, b, preferred_element_type=F32)


def _dot_nt(a, b):
    return lax.dot_general(a, b, (((1,), (1,)), ((), ())), preferred_element_type=F32)


def _dot_tn(a, b):
    return lax.dot_general(a, b, (((0,), (0,)), ((), ())), preferred_element_type=F32)


def _embed(x, ctx, pos):
    B, T, _ = x.shape
    nt = 1 + T // TM

    def body(x_ref, c_ref, p_ref, o_ref):
        i = pl.program_id(1)

        @pl.when(i == 0)
        def _():
            o_ref[...] = c_ref[0]

        @pl.when(i > 0)
        def _():
            o_ref[...] = x_ref[0] + p_ref[...]

    return pl.pallas_call(
        body, name="embed", grid=(B, nt),
        in_specs=[pl.BlockSpec((1, TM, D), lambda b, i: (b, jnp.maximum(i - 1, 0), 0)),
                  pl.BlockSpec((1, TM, D), lambda b, i: (b, 0, 0)),
                  pl.BlockSpec((TM, D), lambda b, i: (jnp.maximum(i - 1, 0), 0))],
        out_specs=pl.BlockSpec((TM, D), lambda b, i: (b * nt + i, 0)),
        out_shape=jax.ShapeDtypeStruct((B * nt * TM, D), F32),
        compiler_params=_cp(2))(x, ctx, pos)


def _lnmod(X, lx, mvec, comb, sub, B, nt, name):
    def body(x_ref, m_ref, o_ref):
        xh, _ = _ln(x_ref[...])
        sh = m_ref[0, 0, 3 * sub:3 * sub + 1, :]
        sc = m_ref[0, 0, 3 * sub + 1:3 * sub + 2, :]
        o_ref[...] = (xh * (1.0 + sh) + sc).astype(BF16)

    return pl.pallas_call(
        body, name=name, grid=(B, nt),
        in_specs=[pl.BlockSpec((TM, D), _rowmap(*lx)), pl.BlockSpec((1, 1, N_MOD, D), _mmap(comb))],
        out_specs=pl.BlockSpec((TM, D), _rowmap(nt, 0)),
        out_shape=jax.ShapeDtypeStruct((B * nt * TM, D), BF16),
        compiler_params=_cp(2))(X, mvec)


def _lnmod_bwd(dh, X, lx, mvec, comb, sub, B, nt, name, dres=None, ldres=None, dres_zero_ctx=False, to_x=None):
    has_res = dres is not None

    def body(*refs):
        if has_res:
            dh_ref, x_ref, m_ref, dr_ref, dx_ref, dm_ref = refs
        else:
            dh_ref, x_ref, m_ref, dx_ref, dm_ref = refs
        i = pl.program_id(1)
        xh, r = _ln(x_ref[...])
        sh = m_ref[0, 0, 3 * sub:3 * sub + 1, :]
        dhv = dh_ref[...]
        dx = _ln_bwd(dhv * (1.0 + sh), xh, r)
        if has_res:
            dr = dr_ref[...]
            if dres_zero_ctx:
                dr = jnp.where(i > 0, dr, 0.0)
            dx = dx + dr

        init = (i == 0) | (i == 1) if comb else (i == 0)

        @pl.when(init)
        def _():
            dm_ref[...] = jnp.zeros_like(dm_ref)

        dm_ref[0, 0, 0:1, :] += _rsum(dhv * xh)
        dm_ref[0, 0, 1:2, :] += _rsum(dhv)
        if to_x is None:
            dx_ref[...] = dx
        else:
            @pl.when(i > 0)
            def _():
                dx_ref[0] = dx

    in_specs = [pl.BlockSpec((TM, D), _rowmap(nt, 0)), pl.BlockSpec((TM, D), _rowmap(*lx)),
                pl.BlockSpec((1, 1, N_MOD, D), _mmap(comb))]
    args = [dh, X, mvec]
    if has_res:
        if dres_zero_ctx:
            s, o = ldres
            in_specs.append(pl.BlockSpec((TM, D), lambda b, i: (b * s + o + jnp.maximum(i - 1, 0), 0)))
        else:
            in_specs.append(pl.BlockSpec((TM, D), _rowmap(*ldres)))
        args.append(dres)
    if to_x is None:
        dx_spec = pl.BlockSpec((TM, D), _rowmap(nt, 0))
        dx_shape = jax.ShapeDtypeStruct((B * nt * TM, D), F32)
    else:
        dx_spec = pl.BlockSpec((1, TM, D), lambda b, i: (b, jnp.maximum(i - 1, 0), 0))
        dx_shape = jax.ShapeDtypeStruct(to_x, F32)
    return pl.pallas_call(
        body, name=name, grid=(B, nt), in_specs=in_specs,
        out_specs=[dx_spec, pl.BlockSpec((1, 1, 2, D), _mmap(comb))],
        out_shape=[dx_shape, jax.ShapeDtypeStruct((B, 2, 2, D), F32)],
        compiler_params=_cp(2))(*args)


def _resid(X, lx, Y, mvec, comb, sub, w, gb, B, nt, name):
    def body(x_ref, y_ref, m_ref, gb_ref, o_ref):
        g = m_ref[0, 0, 3 * sub + 2:3 * sub + 3, :]
        z = ALPHA * x_ref[...] + (w * g) * y_ref[...]
        zh, _ = _ln(z)
        o_ref[...] = zh * gb_ref[0:1, :] + gb_ref[1:2, :]

    return pl.pallas_call(
        body, name=name, grid=(B, nt),
        in_specs=[pl.BlockSpec((TM, D), _rowmap(*lx)), pl.BlockSpec((TM, D), _rowmap(nt, 0)),
                  pl.BlockSpec((1, 1, N_MOD, D), _mmap(comb)), pl.BlockSpec((2, D), lambda b, i: (0, 0))],
        out_specs=pl.BlockSpec((TM, D), _rowmap(nt, 0)),
        out_shape=jax.ShapeDtypeStruct((B * nt * TM, D), F32),
        compiler_params=_cp(2))(X, Y, mvec, gb)


def _resid_bwd(dxn, X, lx, Y, mvec, comb, sub, w, gb, B, nt, name):
    def body(d_ref, x_ref, y_ref, m_ref, gb_ref, dx_ref, dy_ref, dg_ref, dgb_ref):
        b_, i = pl.program_id(0), pl.program_id(1)
        g = m_ref[0, 0, 3 * sub + 2:3 * sub + 3, :]
        y = y_ref[...]
        z = ALPHA * x_ref[...] + (w * g) * y
        zh, r = _ln(z)
        do = d_ref[...]
        dz = _ln_bwd(do * gb_ref[0:1, :], zh, r)
        dx_ref[...] = ALPHA * dz
        dy_ref[...] = ((w * g) * dz).astype(BF16)

        @pl.when((b_ == 0) & (i == 0))
        def _():
            dgb_ref[...] = jnp.zeros_like(dgb_ref)

        dgb_ref[0:1, :] += _rsum(do * zh)
        dgb_ref[1:2, :] += _rsum(do)

        init = (i == 0) | (i == 1) if comb else (i == 0)

        @pl.when(init)
        def _():
            dg_ref[...] = jnp.zeros_like(dg_ref)

        dg_ref[0, 0] += w * _rsum(dz * y)

    return pl.pallas_call(
        body, name=name, grid=(B, nt),
        in_specs=[pl.BlockSpec((TM, D), _rowmap(nt, 0)), pl.BlockSpec((TM, D), _rowmap(*lx)),
                  pl.BlockSpec((TM, D), _rowmap(nt, 0)), pl.BlockSpec((1, 1, N_MOD, D), _mmap(comb)),
                  pl.BlockSpec((2, D), lambda b, i: (0, 0))],
        out_specs=[pl.BlockSpec((TM, D), _rowmap(nt, 0)), pl.BlockSpec((TM, D), _rowmap(nt, 0)),
                   pl.BlockSpec((1, 1, 1, D), _mmap(comb)), pl.BlockSpec((2, D), lambda b, i: (0, 0))],
        out_shape=[jax.ShapeDtypeStruct((B * nt * TM, D), F32), jax.ShapeDtypeStruct((B * nt * TM, D), BF16),
                   jax.ShapeDtypeStruct((B, 2, 1, D), F32), jax.ShapeDtypeStruct((2, D), F32)],
        compiler_params=_cp(2))(dxn, X, Y, mvec, gb)


def _swiglu(u, name):
    M = u.shape[0]

    def body(u_ref, a_ref):
        g = u_ref[:, :DFF]
        a_ref[...] = (g * _sigmoid(g) * u_ref[:, DFF:]).astype(BF16)

    return pl.pallas_call(
        body, name=name, grid=(M // TM,),
        in_specs=[pl.BlockSpec((TM, 2 * DFF), lambda i: (i, 0))],
        out_specs=pl.BlockSpec((TM, DFF), lambda i: (i, 0)),
        out_shape=jax.ShapeDtypeStruct((M, DFF), BF16), compiler_params=_cp(1))(u)


def _swiglu_bwd(u, da, name):
    M = u.shape[0]

    def body(u_ref, da_ref, du_ref):
        g = u_ref[:, :DFF]
        up = u_ref[:, DFF:]
        dav = da_ref[...]
        s = _sigmoid(g)
        du_ref[:, :DFF] = (dav * up * (s * (1.0 + g * (1.0 - s)))).astype(BF16)
        du_ref[:, DFF:] = (dav * (g * s)).astype(BF16)

    return pl.pallas_call(
        body, name=name, grid=(M // TM,),
        in_specs=[pl.BlockSpec((TM, 2 * DFF), lambda i: (i, 0)), pl.BlockSpec((TM, DFF), lambda i: (i, 0))],
        out_specs=pl.BlockSpec((TM, 2 * DFF), lambda i: (i, 0)),
        out_shape=jax.ShapeDtypeStruct((M, 2 * DFF), BF16), compiler_params=_cp(1))(u, da)


def _loss_grad(x3, tgt, B, nt):
    def body(x_ref, t_ref, d_ref, l_ref):
        b_, i = pl.program_id(0), pl.program_id(1)
        e = x_ref[...] - t_ref[0]
        d_ref[...] = e * (1.0 / D)

        @pl.when((b_ == 0) & (i == 0))
        def _():
            l_ref[...] = jnp.zeros_like(l_ref)

        l_ref[...] += _rsum(e * e)

    return pl.pallas_call(
        body, name="loss_grad", grid=(B, nt),
        in_specs=[pl.BlockSpec((TM, D), _rowmap(nt, 0)), pl.BlockSpec((1, TM, D), lambda b, i: (b, i, 0))],
        out_specs=[pl.BlockSpec((TM, D), _rowmap(nt, 0)), pl.BlockSpec((1, D), lambda b, i: (0, 0))],
        out_shape=[jax.ShapeDtypeStruct((B * nt * TM, D), F32), jax.ShapeDtypeStruct((1, D), F32)],
        compiler_params=_cp(2))(x3, tgt)


def _mm_nn(A, la, W, B, nt, out_dtype, name):
    K, N = W.shape

    def body(a_ref, w_ref, o_ref):
        o_ref[...] = _dot(a_ref[...], w_ref[...]).astype(out_dtype)

    return pl.pallas_call(
        body, name=name, grid=(B, nt),
        in_specs=[pl.BlockSpec((TM, K), _rowmap(*la)), pl.BlockSpec((K, N), lambda b, i: (0, 0))],
        out_specs=pl.BlockSpec((TM, N), _rowmap(nt, 0)),
        out_shape=jax.ShapeDtypeStruct((B * nt * TM, N), out_dtype), compiler_params=_cp(2))(A, W)


def _mm_nt(A, W, name):
    M, N = A.shape
    K = W.shape[0]

    def body(a_ref, w_ref, o_ref):
        o_ref[...] = _dot_nt(a_ref[...], w_ref[...])

    return pl.pallas_call(
        body, name=name, grid=(M // TM,),
        in_specs=[pl.BlockSpec((TM, N), lambda i: (i, 0)), pl.BlockSpec((K, N), lambda i: (0, 0))],
        out_specs=pl.BlockSpec((TM, K), lambda i: (i, 0)),
        out_shape=jax.ShapeDtypeStruct((M, K), F32), compiler_params=_cp(1))(A, W)


def _mm_tn(A, G, name, tk=512, tn=512):
    M, K = A.shape
    N = G.shape[1]

    def body(a_ref, g_ref, o_ref):
        @pl.when(pl.program_id(1) == 0)
        def _():
            o_ref[...] = jnp.zeros_like(o_ref)

        o_ref[...] += _dot_tn(a_ref[...], g_ref[...])

    return pl.pallas_call(
        body, name=name, grid=(N // tn, M // tk),
        in_specs=[pl.BlockSpec((tk, K), lambda n, k: (k, 0)), pl.BlockSpec((tk, tn), lambda n, k: (k, n))],
        out_specs=pl.BlockSpec((K, tn), lambda n, k: (0, n)),
        out_shape=jax.ShapeDtypeStruct((K, N), F32), compiler_params=_cp(2))(A, G)


def _logsig(z):
    return jnp.minimum(z, 0.0) - jnp.log(1.0 + jnp.exp(-jnp.abs(z)))


def _features(F, a2p, biasp, lbp, B):
    nt = F.shape[0] // (B * TM)
    nj = nt + 1
    R2 = nj * TM

    def body(f_ref, a2_ref, bias_ref, lb_ref, q_ref, k_ref, v_ref, g_ref):
        lr = f_ref[:, LR:LR + HD].astype(BF16)
        lane = lax.broadcasted_iota(jnp.int32, (1, 4 * HD), 1)
        keep = (lane & (HD - 1)) < GLA_DK
        for d in range(2):
            z = _dot(lr, a2_ref[d]) + bias_ref[d:d + 1, :]
            gl = jnp.where(keep, _logsig(z) * (1.0 / GATE_NORM), 0.0)
            for h in range(4):
                g_ref[d, 0, h] = gl[:, h * HD:(h + 1) * HD]
        for h in range(4):
            q_ref[0, h] = f_ref[:, GQ + h * HD:GQ + (h + 1) * HD] * (GLA_DK ** -0.5)
            kk = f_ref[:, GK + h * HD:GK + (h + 1) * HD]
            k_ref[0, 0, h] = kk
            k_ref[1, 0, h] = kk
            v_ref[0, h] = f_ref[:, GV + h * HD:GV + (h + 1) * HD]
        for h in range(4):
            sl = slice(h * HD, (h + 1) * HD)
            rq = f_ref[:, RQ + h * HD:RQ + (h + 1) * HD]
            q_ref[0, 4 + h] = rq * _sigmoid(rq) * (HD ** -0.5)
            v_ref[0, 4 + h] = f_ref[:, RI + h * HD:RI + (h + 1) * HD]
            for d, off in ((0, RFF), (1, RFB)):
                lb = lb_ref[d:d + 1, sl]
                f = lb + (1.0 - lb) * _sigmoid(f_ref[:, off + h * HD:off + (h + 1) * HD])
                g_ref[d, 0, 4 + h] = jnp.log(f)
                k_ref[d, 0, 4 + h] = 1.0 - f

    fmap = lambda b, j: (b * nt + jnp.where(j == nt, 0, j), 0)
    one = pl.BlockSpec((1, NH, TM, HD), lambda b, j: (b, 0, j, 0))
    two = pl.BlockSpec((2, 1, NH, TM, HD), lambda b, j: (0, b, 0, j, 0))
    s1 = jax.ShapeDtypeStruct((B, NH, R2, HD), F32)
    s2 = jax.ShapeDtypeStruct((2, B, NH, R2, HD), F32)
    return pl.pallas_call(
        body, name="mix_features", grid=(B, nj),
        in_specs=[pl.BlockSpec((TM, MIXP), fmap), pl.BlockSpec((2, HD, 4 * HD), lambda b, j: (0, 0, 0)),
                  pl.BlockSpec((2, 4 * HD), lambda b, j: (0, 0)), pl.BlockSpec((2, 4 * HD), lambda b, j: (0, 0))],
        out_specs=[one, two, one, two], out_shape=[s1, s2, s1, s2],
        compiler_params=_cp(2))(F, a2p, biasp, lbp)


def _features_bwd(F, a2p, biasp, lbp, dQ0, dQ1, dK0, dK1, dV0, dV1, dG0, dG1, dgates, B):
    nt = F.shape[0] // (B * TM)

    def body(f_ref, a2_ref, bias_ref, lb_ref, dq0, dq1, dk0, dk1, dv0, dv1, dg0, dg1, dgt_ref,
             df_ref, da2_ref, dbias_ref, dlb_ref):
        b_, i = pl.program_id(0), pl.program_id(1)

        @pl.when((b_ == 0) & (i == 0))
        def _():
            da2_ref[...] = jnp.zeros_like(da2_ref)
            dbias_ref[...] = jnp.zeros_like(dbias_ref)
            dlb_ref[...] = jnp.zeros_like(dlb_ref)

        df_ref[:, 0:2 * 4 * HD] = jnp.where(i > 0, dgt_ref[...], 0.0).astype(BF16)
        df_ref[:, LR + HD:] = jnp.zeros((TM, MIXP - LR - HD), BF16)
        lr = f_ref[:, LR:LR + HD].astype(BF16)
        lane = lax.broadcasted_iota(jnp.int32, (1, 4 * HD), 1)
        keep = (lane & (HD - 1)) < GLA_DK
        dlr = jnp.zeros((TM, HD), F32)
        dgs = (dg0, dg1)
        dks = (dk0, dk1)
        for d in range(2):
            z = _dot(lr, a2_ref[d]) + bias_ref[d:d + 1, :]
            dgl = jnp.concatenate([dgs[d][0, h] for h in range(4)], axis=1)
            dz = jnp.where(keep, dgl * (1.0 / GATE_NORM) * (1.0 - _sigmoid(z)), 0.0)
            dzb = dz.astype(BF16)
            dlr = dlr + _dot_nt(dzb, a2_ref[d])
            da2_ref[d] += _dot_tn(lr, dzb)
            dbias_ref[d:d + 1, :] += _rsum(dz)
        df_ref[:, LR:LR + HD] = dlr.astype(BF16)
        for h in range(4):
            df_ref[:, GQ + h * HD:GQ + (h + 1) * HD] = ((dq0[0, h] + dq1[0, h]) * (GLA_DK ** -0.5)).astype(BF16)
            df_ref[:, GK + h * HD:GK + (h + 1) * HD] = (dk0[0, h] + dk1[0, h]).astype(BF16)
            df_ref[:, GV + h * HD:GV + (h + 1) * HD] = (dv0[0, h] + dv1[0, h]).astype(BF16)
        for h in range(4):
            sl = slice(h * HD, (h + 1) * HD)
            rq = f_ref[:, RQ + h * HD:RQ + (h + 1) * HD]
            s = _sigmoid(rq)
            dqh = dq0[0, 4 + h] + dq1[0, 4 + h]
            df_ref[:, RQ + h * HD:RQ + (h + 1) * HD] = (dqh * (HD ** -0.5) * (s * (1.0 + rq * (1.0 - s)))).astype(BF16)
            df_ref[:, RI + h * HD:RI + (h + 1) * HD] = (dv0[0, 4 + h] + dv1[0, 4 + h]).astype(BF16)
            for d, off in ((0, RFF), (1, RFB)):
                lb = lb_ref[d:d + 1, sl]
                sg = _sigmoid(f_ref[:, off + h * HD:off + (h + 1) * HD])
                f = lb + (1.0 - lb) * sg
                dff = dgs[d][0, 4 + h] / f - dks[d][0, 4 + h]
                df_ref[:, off + h * HD:off + (h + 1) * HD] = (dff * (1.0 - lb) * sg * (1.0 - sg)).astype(BF16)
                dlb_ref[d:d + 1, sl] += _rsum(dff * (1.0 - sg))

    m0 = lambda b, i: (b, 0, i, 0)
    m1 = lambda b, i: (b, 0, jnp.where(i == 0, nt, i), 0)
    one = lambda m: pl.BlockSpec((1, NH, TM, HD), m)
    return pl.pallas_call(
        body, name="mix_features_bwd", grid=(B, nt),
        in_specs=[pl.BlockSpec((TM, MIXP), _rowmap(nt, 0)), pl.BlockSpec((2, HD, 4 * HD), lambda b, i: (0, 0, 0)),
                  pl.BlockSpec((2, 4 * HD), lambda b, i: (0, 0)), pl.BlockSpec((2, 4 * HD), lambda b, i: (0, 0)),
                  one(m0), one(m1), one(m0), one(m1), one(m0), one(m1), one(m0), one(m1),
                  pl.BlockSpec((TM, D), lambda b, i: (b * (nt - 1) + jnp.maximum(i - 1, 0), 0))],
        out_specs=[pl.BlockSpec((TM, MIXP), _rowmap(nt, 0)), pl.BlockSpec((2, HD, 4 * HD), lambda b, i: (0, 0, 0)),
                   pl.BlockSpec((2, 4 * HD), lambda b, i: (0, 0)), pl.BlockSpec((2, 4 * HD), lambda b, i: (0, 0))],
        out_shape=[jax.ShapeDtypeStruct((B * nt * TM, MIXP), BF16), jax.ShapeDtypeStruct((2, HD, 4 * HD), F32),
                   jax.ShapeDtypeStruct((2, 4 * HD), F32), jax.ShapeDtypeStruct((2, 4 * HD), F32)],
        compiler_params=_cp(2))(F, a2p, biasp, lbp, dQ0, dQ1, dK0, dK1, dV0, dV1, dG0, dG1, dgates)


def _chunk_scan(x, rin, fwd, inclusive=True):
    acc = x
    sft = 1
    while sft < CH:
        if fwd:
            acc = acc + jnp.where(rin >= sft, pltpu.roll(acc, sft, 0), 0.0)
        else:
            acc = acc + jnp.where(rin < CH - sft, pltpu.roll(acc, TM - sft, 0), 0.0)
        sft *= 2
    return acc if inclusive else acc - x


def _scan_common(q, k, v, g, rev):
    rin = lax.broadcasted_iota(jnp.int32, (TM, HD), 0) & (CH - 1)
    b = _chunk_scan(g, rin, not rev)
    xx = _chunk_scan(g, rin, rev, inclusive=False)
    eb = jnp.exp(b)
    qd = q * eb
    ki = k * jnp.exp(-b)
    kt = k * jnp.exp(xx)
    ri = lax.broadcasted_iota(jnp.int32, (TM, TM), 0)
    ci = lax.broadcasted_iota(jnp.int32, (TM, TM), 1)
    same = (ri >> 5) == (ci >> 5)
    lo = same & (ri >= ci)
    up = same & (ri <= ci)
    mask, maskT = (up, lo) if rev else (lo, up)
    re = lax.broadcasted_iota(jnp.int32, (TM, NCB * HD), 0) >> 5
    ce = lax.broadcasted_iota(jnp.int32, (TM, NCB * HD), 1) >> 7
    mexp = re == ce
    return rin, b, xx, eb, qd, ki, kt, mask, maskT, mexp


def _expand(xb, mexp):
    return jnp.where(mexp, jnp.concatenate([xb] * NCB, axis=1), jnp.zeros((), xb.dtype))


def _own(x, mexp):
    xm = jnp.where(mexp, x, 0.0)
    acc = xm[:, 0:HD]
    for n in range(1, NCB):
        acc = acc + xm[:, n * HD:(n + 1) * HD]
    return acc


def _state_pass(s0, eb, ut, rev):
    order = range(NCB - 1, -1, -1) if rev else range(NCB)
    states = [None] * NCB
    s = s0
    for n in order:
        row = n * CH if rev else n * CH + CH - 1
        states[n] = s
        s = eb[row:row + 1, :] * s + ut[:, n * HD:(n + 1) * HD]
    return jnp.concatenate(states, axis=1), s


def _scan_fwd(Q, K, V, G, rev, B):
    nb = Q.shape[2] // TM - 1
    d = 1 if rev else 0
    rmap = (lambda s: nb - s) if rev else (lambda s: s)

    def body(q_ref, k_ref, v_ref, g_ref, o_ref, st_ref, s_scr):
        @pl.when(pl.program_id(2) == 0)
        def _():
            s_scr[...] = jnp.zeros_like(s_scr)

        s0 = s_scr[...]
        st_ref[0, 0, 0] = s0
        v = v_ref[0, 0]
        _, _, _, eb, qd, ki, kt, mask, _, mexp = _scan_common(q_ref[0, 0], k_ref[0, 0, 0], v, g_ref[0, 0, 0], rev)
        qb, kib, vb = qd.astype(BF16), ki.astype(BF16), v.astype(BF16)
        a = jnp.where(mask, _dot_nt(qb, kib), 0.0)
        ut = _dot_tn(vb, _expand(kt.astype(BF16), mexp))
        sstack, s_new = _state_pass(s0, eb, ut, rev)
        s_scr[...] = s_new
        o_ref[0, 0] = _dot(a.astype(BF16), vb) + _dot_nt(_expand(qb, mexp), sstack.astype(BF16))

    one = pl.BlockSpec((1, 1, TM, HD), lambda b, h, s: (b, h, rmap(s), 0))
    two = pl.BlockSpec((1, 1, 1, TM, HD), lambda b, h, s: (d, b, h, rmap(s), 0))
    return pl.pallas_call(
        body, name="scan_fwd_rev" if rev else "scan_fwd", grid=(B, NH, nb),
        in_specs=[one, two, one, two],
        out_specs=[one, pl.BlockSpec((1, 1, 1, HD, HD), lambda b, h, s: (b, h, s, 0, 0))],
        out_shape=[jax.ShapeDtypeStruct(Q.shape, F32), jax.ShapeDtypeStruct((B, NH, nb, HD, HD), F32)],
        scratch_shapes=[pltpu.VMEM((HD, HD), F32)],
        compiler_params=_cp(3))(Q, K, V, G)


def _scan_bwd(Q, K, V, G, St, dO, rev, B):
    nb = Q.shape[2] // TM - 1
    d = 1 if rev else 0
    smap = lambda t: nb - 1 - t
    rmap = (lambda t: nb - smap(t)) if rev else smap

    def body(q_ref, k_ref, v_ref, g_ref, st_ref, do_ref, dq_ref, dk_ref, dv_ref, dg_ref, ds_scr):
        t = pl.program_id(2)

        @pl.when(t == 0)
        def _():
            ds_scr[...] = jnp.zeros_like(ds_scr)

        is_lat = smap(t) >= 1
        v = v_ref[0, 0]
        rin, b, xx, eb, qd, ki, kt, mask, maskT, mexp = _scan_common(
            q_ref[0, 0], k_ref[0, 0, 0], v, g_ref[0, 0, 0], rev)
        qb, kib, ktb, vb = qd.astype(BF16), ki.astype(BF16), kt.astype(BF16), v.astype(BF16)
        dob = jnp.where(is_lat, do_ref[0, 0], 0.0).astype(BF16)
        kt_exp = _expand(ktb, mexp)
        ut = _dot_tn(vb, kt_exp)
        sstack, _ = _state_pass(st_ref[0, 0, 0], eb, ut, rev)
        gt = _dot_tn(dob, _expand(qb, mexp))
        order = range(NCB) if rev else range(NCB - 1, -1, -1)
        dsp = [None] * NCB
        t2 = [None] * NCB
        dsc = ds_scr[...]
        for n in order:
            row = n * CH if rev else n * CH + CH - 1
            ebl = eb[row:row + 1, :]
            dsp[n] = dsc
            t2[n] = jnp.broadcast_to(ebl * _rsum(sstack[:, n * HD:(n + 1) * HD] * dsc), (CH, HD))
            dsc = gt[:, n * HD:(n + 1) * HD] + ebl * dsc
        ds_scr[...] = dsc
        dspb = jnp.concatenate(dsp, axis=1).astype(BF16)
        da = jnp.where(mask, _dot_nt(dob, vb), 0.0).astype(BF16)
        dat = jnp.where(maskT, _dot_nt(vb, dob), 0.0).astype(BF16)
        at = jnp.where(maskT, _dot_nt(kib, qb), 0.0).astype(BF16)
        dqd = _dot(da, kib) + _own(_dot(dob, sstack.astype(BF16)), mexp)
        dki = _dot(dat, qb)
        dv_ref[0, 0] = _dot(at, dob) + _dot_nt(kt_exp, dspb)
        dkt = _own(_dot(vb, dspb), mexp)
        z = dkt * kt
        db = dqd * qd - dki * ki
        dq_ref[0, 0] = dqd * eb
        dk_ref[0, 0] = dki * jnp.exp(-b) + dkt * jnp.exp(xx)
        dg_ref[0, 0] = (_chunk_scan(db, rin, rev) + _chunk_scan(z, rin, not rev, inclusive=False)
                        + jnp.concatenate(t2, axis=0))

    one = pl.BlockSpec((1, 1, TM, HD), lambda b, h, t: (b, h, rmap(t), 0))
    two = pl.BlockSpec((1, 1, 1, TM, HD), lambda b, h, t: (d, b, h, rmap(t), 0))
    lat = pl.BlockSpec((1, 1, TM, HD), lambda b, h, t: (b, h, jnp.clip(rmap(t) - 1, 0, nb - 2), 0))
    shp = jax.ShapeDtypeStruct(Q.shape, F32)
    return pl.pallas_call(
        body, name="scan_bwd_rev" if rev else "scan_bwd", grid=(B, NH, nb),
        in_specs=[one, two, one, two, pl.BlockSpec((1, 1, 1, HD, HD), lambda b, h, t: (b, h, smap(t), 0, 0)), lat],
        out_specs=[one, one, one, one], out_shape=[shp, shp, shp, shp],
        scratch_shapes=[pltpu.VMEM((HD, HD), F32)],
        compiler_params=_cp(3))(Q, K, V, G, St, dO)


def _gnorm(O0, O1, F, gains, B, ntl):
    nt = ntl + 1

    def body(o0_ref, o1_ref, f_ref, gn_ref, m_ref):
        for h in range(NH):
            o = o0_ref[0, h] + o1_ref[0, h]
            r = lax.rsqrt(jnp.mean(o * o, axis=-1, keepdims=True) + NORM_EPS)
            gn = gn_ref[0:1, :] if h < 4 else gn_ref[1:2, :]
            gt = f_ref[:, h * HD:(h + 1) * HD]
            m_ref[:, h * HD:(h + 1) * HD] = (o * r * gn * (gt * _sigmoid(gt))).astype(BF16)

    ospec = pl.BlockSpec((1, NH, TM, HD), lambda b, i: (b, 0, i + 1, 0))
    return pl.pallas_call(
        body, name="gated_norm", grid=(B, ntl),
        in_specs=[ospec, ospec, pl.BlockSpec((TM, D), lambda b, i: (b * nt + 1 + i, 0)),
                  pl.BlockSpec((2, HD), lambda b, i: (0, 0))],
        out_specs=pl.BlockSpec((TM, D), _rowmap(ntl, 0)),
        out_shape=jax.ShapeDtypeStruct((B * ntl * TM, D), BF16), compiler_params=_cp(2))(O0, O1, F, gains)


def _gnorm_bwd(dM, O0, O1, F, gains, B, ntl):
    nt = ntl + 1

    def body(dm_ref, o0_ref, o1_ref, f_ref, gn_ref, do_ref, dgt_ref, dgn_ref):
        b_, i = pl.program_id(0), pl.program_id(1)

        @pl.when((b_ == 0) & (i == 0))
        def _():
            dgn_ref[...] = jnp.zeros_like(dgn_ref)

        for h in range(NH):
            o = o0_ref[0, h] + o1_ref[0, h]
            r = lax.rsqrt(jnp.mean(o * o, axis=-1, keepdims=True) + NORM_EPS)
            y = o * r
            gn = gn_ref[0:1, :] if h < 4 else gn_ref[1:2, :]
            gt = f_ref[:, h * HD:(h + 1) * HD]
            s = _sigmoid(gt)
            dm = dm_ref[:, h * HD:(h + 1) * HD]
            don = dm * (gt * s)
            dgt_ref[:, h * HD:(h + 1) * HD] = dm * (y * gn) * (s * (1.0 + gt * (1.0 - s)))
            row = 0 if h < 4 else 1
            dgn_ref[row:row + 1, :] += _rsum(don * y)
            dy = don * gn
            do_ref[0, h] = r * (dy - y * jnp.mean(dy * y, axis=-1, keepdims=True))

    ospec = pl.BlockSpec((1, NH, TM, HD), lambda b, i: (b, 0, i + 1, 0))
    return pl.pallas_call(
        body, name="gated_norm_bwd", grid=(B, ntl),
        in_specs=[pl.BlockSpec((TM, D), _rowmap(ntl, 0)), ospec, ospec,
                  pl.BlockSpec((TM, D), lambda b, i: (b * nt + 1 + i, 0)), pl.BlockSpec((2, HD), lambda b, i: (0, 0))],
        out_specs=[pl.BlockSpec((1, NH, TM, HD), lambda b, i: (b, 0, i, 0)), pl.BlockSpec((TM, D), _rowmap(ntl, 0)),
                   pl.BlockSpec((2, HD), lambda b, i: (0, 0))],
        out_shape=[jax.ShapeDtypeStruct((B, NH, ntl * TM, HD), F32), jax.ShapeDtypeStruct((B * ntl * TM, D), F32),
                   jax.ShapeDtypeStruct((2, HD), F32)],
        compiler_params=_cp(2))(dM, O0, O1, F, gains)


def _sincos_2d(rows, width, dim):
    r = jnp.repeat(jnp.arange(rows), width)
    col = jnp.tile(jnp.arange(width), rows)
    quarter = dim // 4
    omega = 1.0 / 10000.0 ** (jnp.arange(quarter, dtype=F32) / quarter)

    def emb(p):
        a = p.astype(F32)[:, None] * omega[None, :]
        return jnp.concatenate([jnp.sin(a), jnp.cos(a)], axis=-1)

    return jnp.concatenate([emb(r), emb(col)], axis=-1)


def _pad_heads(w):
    k = w.shape[0]
    return jnp.pad(w.reshape(k, 4, GLA_DK), ((0, 0), (0, 0), (0, HD - GLA_DK))).reshape(k, 4 * HD)


def _unpad_heads(w):
    k = w.shape[0]
    return w.reshape(k, 4, HD)[:, :, :GLA_DK].reshape(k, 4 * GLA_DK)


def _mix_in_to_padded(w):
    o = np.cumsum((0,) + IN_SPLITS)
    p = [w[:, o[i]:o[i + 1]] for i in range(len(IN_SPLITS))]
    lr = jnp.pad(jnp.concatenate([p[4], p[5]], axis=1), ((0, 0), (0, HD - 32)))
    pad = jnp.zeros((w.shape[0], MIXP - LR - HD), w.dtype)
    return jnp.concatenate([p[3], p[10], _pad_heads(p[0]), _pad_heads(p[1]), p[2], p[6], p[7], p[8], p[9], lr, pad], axis=1)


def _mix_in_from_padded(g):
    s = lambda a: g[:, a:a + 512]
    lr = g[:, LR:LR + 32]
    return jnp.concatenate([_unpad_heads(s(GQ)), _unpad_heads(s(GK)), s(GV), s(GG), lr[:, :16], lr[:, 16:], s(RQ),
                            s(RFF), s(RFB), s(RI), s(RG)], axis=1)


def _local_step(x, ctx, tgt, mvec, w1i, w1o, wmp, wmo, w2i, w2o, ln_gain, ln_bias, a2f, a2b, abf, abb, lb, gng, gnh):
    B, T, _ = x.shape
    assert ctx.shape[1] == TM and T % TM == 0
    ntl = T // TM
    nt = ntl + 1
    C, L, CL = (nt, 0), (ntl, 0), (nt, 1)
    pos = _sincos_2d(T // 64, 64, D)
    gbs = [jnp.stack([ln_gain[i], ln_bias[i]]) for i in range(3)]
    a2p = jnp.zeros((2, HD, 4 * HD), F32)
    a2p = a2p.at[0, 0:16].set(_pad_heads(a2f)).at[1, 16:32].set(_pad_heads(a2b)).astype(BF16)
    biasp = jnp.concatenate([_pad_heads(abf.reshape(1, -1)), _pad_heads(abb.reshape(1, -1))], axis=0)
    gains = jnp.concatenate([gng.reshape(1, HD), gnh.reshape(1, HD)], axis=0)

    X0 = _embed(x, ctx, pos)
    h0 = _lnmod(X0, C, mvec, True, 0, B, nt, "lnmod0")
    u0 = _mm_nn(h0, C, w1i, B, nt, F32, "ffn1_in")
    a0 = _swiglu(u0, "swiglu1")
    y0 = _mm_nn(a0, C, w1o, B, nt, F32, "ffn1_out")
    X1 = _resid(X0, C, y0, mvec, True, 0, 0.5, gbs[0], B, nt, "resid0")
    h1 = _lnmod(X1, C, mvec, True, 1, B, nt, "lnmod1")
    Fm = _mm_nn(h1, C, wmp, B, nt, F32, "mix_in")
    Q, K, V, G = _features(Fm, a2p, biasp, lb, B)
    O0, S0 = _scan_fwd(Q, K, V, G, False, B)
    O1, S1 = _scan_fwd(Q, K, V, G, True, B)
    merged = _gnorm(O0, O1, Fm, gains, B, ntl)
    y1 = _mm_nn(merged, L, wmo, B, ntl, F32, "mix_out")
    X2 = _resid(X1, CL, y1, mvec, False, 1, 1.0, gbs[1], B, ntl, "resid1")
    h2 = _lnmod(X2, L, mvec, False, 2, B, ntl, "lnmod2")
    u2 = _mm_nn(h2, L, w2i, B, ntl, F32, "ffn2_in")
    a2 = _swiglu(u2, "swiglu2")
    y2 = _mm_nn(a2, L, w2o, B, ntl, F32, "ffn2_out")
    X3 = _resid(X2, L, y2, mvec, False, 2, 0.5, gbs[2], B, ntl, "resid2")
    dX3, lsum = _loss_grad(X3, tgt, B, ntl)
    loss = (0.5 / D) * jnp.sum(lsum)

    dx2r, dy2, dgate2, dgb2 = _resid_bwd(dX3, X2, L, y2, mvec, False, 2, 0.5, gbs[2], B, ntl, "resid2_bwd")
    da2 = _mm_nt(dy2, w2o, "ffn2_out_dx")
    g_w2o = _mm_tn(a2, dy2, "ffn2_out_dw")
    du2 = _swiglu_bwd(u2, da2, "swiglu2_bwd")
    dh2 = _mm_nt(du2, w2i, "ffn2_in_dx")
    g_w2i = _mm_tn(h2, du2, "ffn2_in_dw")
    dX2, dss2 = _lnmod_bwd(dh2, X2, L, mvec, False, 2, B, ntl, "lnmod2_bwd", dres=dx2r, ldres=L)
    dx1r, dy1, dgate1, dgb1 = _resid_bwd(dX2, X1, CL, y1, mvec, False, 1, 1.0, gbs[1], B, ntl, "resid1_bwd")
    dmerged = _mm_nt(dy1, wmo, "mix_out_dx")
    g_wmo = _mm_tn(merged, dy1, "mix_out_dw")
    dO, dgates, dgains = _gnorm_bwd(dmerged, O0, O1, Fm, gains, B, ntl)
    dQ0, dK0, dV0, dG0 = _scan_bwd(Q, K, V, G, S0, dO, False, B)
    dQ1, dK1, dV1, dG1 = _scan_bwd(Q, K, V, G, S1, dO, True, B)
    dF, da2p, dbiasp, dlb = _features_bwd(Fm, a2p, biasp, lb, dQ0, dQ1, dK0, dK1, dV0, dV1, dG0, dG1, dgates, B)
    dh1 = _mm_nt(dF, wmp, "mix_in_dx")
    g_wmp = _mm_tn(h1, dF, "mix_in_dw")
    dX1, dss1 = _lnmod_bwd(dh1, X1, C, mvec, True, 1, B, nt, "lnmod1_bwd", dres=dx1r, ldres=L, dres_zero_ctx=True)
    dx0r, dy0, dgate0, dgb0 = _resid_bwd(dX1, X0, C, y0, mvec, True, 0, 0.5, gbs[0], B, nt, "resid0_bwd")
    da0 = _mm_nt(dy0, w1o, "ffn1_out_dx")
    g_w1o = _mm_tn(a0, dy0, "ffn1_out_dw")
    du0 = _swiglu_bwd(u0, da0, "swiglu1_bwd")
    dh0 = _mm_nt(du0, w1i, "ffn1_in_dx")
    g_w1i = _mm_tn(h0, du0, "ffn1_in_dw")
    grad_x, dss0 = _lnmod_bwd(dh0, X0, C, mvec, True, 0, B, nt, "lnmod0_bwd", dres=dx0r, ldres=C, to_x=x.shape)

    zero_ctx = lambda a: a.at[:, 0].set(0.0)
    dm = jnp.concatenate([dss0, dgate0, dss1, zero_ctx(dgate1), zero_ctx(dss2), zero_ctx(dgate2)], axis=2)
    small = dict(
        ln_gain=jnp.stack([dgb0[0], dgb1[0], dgb2[0]]), ln_bias=jnp.stack([dgb0[1], dgb1[1], dgb2[1]]),
        a2f=_unpad_heads(da2p[0, 0:16]), a2b=_unpad_heads(da2p[1, 16:32]),
        abf=_unpad_heads(dbiasp[0:1]), abb=_unpad_heads(dbiasp[1:2]), lb=dlb, gng=dgains[0], gnh=dgains[1])
    big = dict(w1i=g_w1i, w1o=g_w1o, wmp=g_wmp, wmo=g_wmo, w2i=g_w2i, w2o=g_w2o)
    return loss, grad_x, big, dm, small


def _small_allgather(xs, name):
    r, n = xs.shape

    def body(x_ref, out_ref, send_sems, recv_sems, local_sem):
        x, y, c = lax.axis_index("x"), lax.axis_index("y"), lax.axis_index("c")
        me, sibling = (x, y, c), (x, y, 1 - c)
        chips = [(1 - x, y), (x, 1 - y), (1 - x, 1 - y)]

        def rows(px, py, pc):
            return out_ref.at[pl.ds((4 * px + 2 * py + pc) * r, r), :]

        def copy(k, block, to, src=None):
            return pltpu.make_async_remote_copy(
                src_ref=rows(*block) if src is None else src, dst_ref=rows(*block),
                send_sem=send_sems.at[k], recv_sem=recv_sems.at[k], device_id=to, device_id_type=MESH)

        mine = pltpu.make_async_copy(x_ref, rows(*me), local_sem)
        mine.start()
        first = [copy(0, me, sibling, src=x_ref)]
        first += [copy(1 + j, me, (*chip, c), src=x_ref) for j, chip in enumerate(chips)]
        for cp in first:
            cp.start()
        passed = [copy(4 + j, (*chip, c), sibling) for j, chip in enumerate(chips)]
        for j, chip in enumerate(chips):
            copy(1 + j, (*chip, c), me).wait_recv()
            passed[j].start()
        copy(0, sibling, me).wait_recv()
        for j, chip in enumerate(chips):
            copy(4 + j, (*chip, 1 - c), me).wait_recv()
        for cp in first + passed:
            cp.wait_send()
        mine.wait()

    out = pl.pallas_call(
        body, name=name,
        out_shape=jax.ShapeDtypeStruct((8 * r, n), xs.dtype),
        in_specs=[pl.BlockSpec(memory_space=pltpu.VMEM)],
        out_specs=pl.BlockSpec(memory_space=pltpu.VMEM),
        scratch_shapes=[pltpu.SemaphoreType.DMA((7,)), pltpu.SemaphoreType.DMA((7,)), pltpu.SemaphoreType.DMA],
        compiler_params=pltpu.CompilerParams(vmem_limit_bytes=VMEM_LIMIT))(xs)
    return out.reshape(8, r, n)


def _gather_flat(v, name):
    n = v.shape[0]
    npad = -(-n // 1024) * 1024
    g = _small_allgather(jnp.pad(v, (0, npad - n)).reshape(8, npad // 8), name)
    return g.reshape(8, npad)[:, :n]


def _big_allgather(blk):
    R = blk.shape[0]

    def body(x_ref, out_ref, send_sems, recv_sems, local_sem):
        x, y, c = lax.axis_index("x"), lax.axis_index("y"), lax.axis_index("c")
        me, sibling = (x, y, c), (x, y, 1 - c)
        chips = [(1 - x, y), (x, 1 - y), (1 - x, 1 - y)]

        def slot(px, py, pc):
            return out_ref.at[4 * px + 2 * py + pc]

        def copy(k, block, to, src=None):
            return pltpu.make_async_remote_copy(
                src_ref=slot(*block) if src is None else src, dst_ref=slot(*block),
                send_sem=send_sems.at[k], recv_sem=recv_sems.at[k], device_id=to, device_id_type=MESH)

        mine = pltpu.make_async_copy(x_ref, slot(*me), local_sem)
        mine.start()
        first = [copy(0, me, sibling, src=x_ref)]
        first += [copy(1 + j, me, (*chip, c), src=x_ref) for j, chip in enumerate(chips)]
        for cp in first:
            cp.start()
        passed = [copy(4 + j, (*chip, c), sibling) for j, chip in enumerate(chips)]
        for j, chip in enumerate(chips):
            copy(1 + j, (*chip, c), me).wait_recv()
            passed[j].start()
        copy(0, sibling, me).wait_recv()
        for j, chip in enumerate(chips):
            copy(4 + j, (*chip, 1 - c), me).wait_recv()
        for cp in first + passed:
            cp.wait_send()
        mine.wait()

    return pl.pallas_call(
        body, name="weight_allgather",
        out_shape=jax.ShapeDtypeStruct((8, R, 128), blk.dtype),
        in_specs=[pl.BlockSpec(memory_space=pl.ANY)], out_specs=pl.BlockSpec(memory_space=pl.ANY),
        scratch_shapes=[pltpu.SemaphoreType.DMA((7,)), pltpu.SemaphoreType.DMA((7,)), pltpu.SemaphoreType.DMA],
    )(blk)


def _rs_pair_exchange(g8):
    R = g8.shape[1]

    def body(g_ref, r_ref, send_sems, recv_sems):
        x, y, c = lax.axis_index("x"), lax.axis_index("y"), lax.axis_index("c")

        def copy(j):
            return pltpu.make_async_remote_copy(
                src_ref=g_ref.at[2 * j + 1 - c], dst_ref=r_ref.at[j], send_sem=send_sems.at[j],
                recv_sem=recv_sems.at[j], device_id=(x, y, 1 - c), device_id_type=MESH)

        cps = [copy(j) for j in range(4)]
        for cp in cps:
            cp.start()
        for cp in cps:
            cp.wait_recv()
        for cp in cps:
            cp.wait_send()

    return pl.pallas_call(
        body, name="grad_pair_exchange", out_shape=jax.ShapeDtypeStruct((4, R, 128), g8.dtype),
        in_specs=[pl.BlockSpec(memory_space=pl.ANY)], out_specs=pl.BlockSpec(memory_space=pl.ANY),
        scratch_shapes=[pltpu.SemaphoreType.DMA((4,)), pltpu.SemaphoreType.DMA((4,))])(g8)


def _rs_chip_exchange(h4):
    R = h4.shape[1]

    def body(h_ref, r_ref, send_sems, recv_sems):
        x, y, c = lax.axis_index("x"), lax.axis_index("y"), lax.axis_index("c")
        chips = [(1 - x, y), (x, 1 - y), (1 - x, 1 - y)]

        def copy(k):
            px, py = chips[k]
            return pltpu.make_async_remote_copy(
                src_ref=h_ref.at[2 * px + py], dst_ref=r_ref.at[k], send_sem=send_sems.at[k],
                recv_sem=recv_sems.at[k], device_id=(px, py, c), device_id_type=MESH)

        cps = [copy(k) for k in range(3)]
        for cp in cps:
            cp.start()
        for cp in cps:
            cp.wait_recv()
        for cp in cps:
            cp.wait_send()

    return pl.pallas_call(
        body, name="grad_chip_exchange", out_shape=jax.ShapeDtypeStruct((3, R, 128), h4.dtype),
        in_specs=[pl.BlockSpec(memory_space=pl.ANY)], out_specs=pl.BlockSpec(memory_space=pl.ANY),
        scratch_shapes=[pltpu.SemaphoreType.DMA((3,)), pltpu.SemaphoreType.DMA((3,))])(h4)


def _rs_pair_share(fin):
    R = fin.shape[0]

    def body(f_ref, o_ref, send_sem, recv_sem, local_sem):
        x, y, c = lax.axis_index("x"), lax.axis_index("y"), lax.axis_index("c")
        mine = pltpu.make_async_copy(f_ref, o_ref.at[c], local_sem)
        mine.start()
        cp = pltpu.make_async_remote_copy(
            src_ref=f_ref, dst_ref=o_ref.at[c], send_sem=send_sem, recv_sem=recv_sem,
            device_id=(x, y, 1 - c), device_id_type=MESH)
        cp.start()
        pltpu.make_async_remote_copy(
            src_ref=f_ref, dst_ref=o_ref.at[1 - c], send_sem=send_sem, recv_sem=recv_sem,
            device_id=(x, y, 1 - c), device_id_type=MESH).wait_recv()
        cp.wait_send()
        mine.wait()

    return pl.pallas_call(
        body, name="grad_pair_share", out_shape=jax.ShapeDtypeStruct((2, R, 128), fin.dtype),
        in_specs=[pl.BlockSpec(memory_space=pl.ANY)], out_specs=pl.BlockSpec(memory_space=pl.ANY),
        scratch_shapes=[pltpu.SemaphoreType.DMA, pltpu.SemaphoreType.DMA, pltpu.SemaphoreType.DMA])(fin)


def _row_block(R, cap=2048):
    best = 8
    for rb in range(8, cap + 1, 8):
        if R % rb == 0:
            best = rb
    return best


def _rs_add_pair(g8, r4, c):
    R = g8.shape[1]
    rb = _row_block(R)

    def body(c_ref, g_ref, r_ref, o_ref):
        o_ref[...] = g_ref[...] + r_ref[...]

    spec = pl.BlockSpec((1, rb, 128), lambda j, i, c_ref: (j, i, 0))
    return pl.pallas_call(
        body, name="grad_pair_add",
        grid_spec=pltpu.PrefetchScalarGridSpec(
            num_scalar_prefetch=1, grid=(4, R // rb),
            in_specs=[pl.BlockSpec((1, rb, 128), lambda j, i, c_ref: (2 * j + c_ref[0], i, 0)), spec],
            out_specs=spec),
        out_shape=jax.ShapeDtypeStruct((4, R, 128), F32), compiler_params=_cp(2))(c, g8, r4)


def _rs_add_chips(h4, r3, j):
    R = h4.shape[1]
    rb = _row_block(R)

    def body(j_ref, h_ref, r_ref, o_ref):
        o_ref[...] = ((h_ref[0] + r_ref[0]) + r_ref[1]) + r_ref[2]

    return pl.pallas_call(
        body, name="grad_chip_add",
        grid_spec=pltpu.PrefetchScalarGridSpec(
            num_scalar_prefetch=1, grid=(R // rb,),
            in_specs=[pl.BlockSpec((1, rb, 128), lambda i, j_ref: (j_ref[0], i, 0)),
                      pl.BlockSpec((3, rb, 128), lambda i, j_ref: (0, i, 0))],
            out_specs=pl.BlockSpec((rb, 128), lambda i, j_ref: (i, 0))),
        out_shape=jax.ShapeDtypeStruct((R, 128), F32), compiler_params=_cp(1))(j, h4, r3)


def _sum8(g):
    n = g.shape[1]

    def body(g_ref, o_ref):
        acc = g_ref[0:1, :]
        for k in range(1, 8):
            acc = acc + g_ref[k:k + 1, :]
        o_ref[...] = acc

    return pl.pallas_call(body, name="sum_devices", out_shape=jax.ShapeDtypeStruct((1, n), F32),
                          compiler_params=pltpu.CompilerParams(vmem_limit_bytes=VMEM_LIMIT))(g)


ADA_ROWS = 64


def _ada_fwd(cs, w, b):
    n = w.shape[1]

    def body(c_ref, w_ref, b_ref, o_ref):
        cv = c_ref[...]
        s = (cv * _sigmoid(cv)).astype(BF16)
        o_ref[...] = _dot(s, w_ref[...].astype(BF16)) + b_ref[...]

    return pl.pallas_call(body, name="ada_fwd", out_shape=jax.ShapeDtypeStruct((ADA_ROWS, n), F32),
                          compiler_params=pltpu.CompilerParams(vmem_limit_bytes=VMEM_LIMIT))(cs, w, b)


def _ada_bwd(cs, w, dm):
    n = w.shape[1]

    def body(c_ref, w_ref, dm_ref, gw_ref, dc_ref):
        cv = c_ref[...]
        s = (cv * _sigmoid(cv)).astype(BF16)
        gw_ref[...] = _dot_tn(s, dm_ref[...].astype(BF16))
        dc_ref[...] = _dot_nt(dm_ref[32:40, :].astype(BF16), w_ref[...].astype(BF16))

    return pl.pallas_call(
        body, name="ada_bwd",
        out_shape=[jax.ShapeDtypeStruct((D, n), F32), jax.ShapeDtypeStruct((8, D), F32)],
        compiler_params=pltpu.CompilerParams(vmem_limit_bytes=VMEM_LIMIT))(cs, w, dm)


def _adamw(w, g, m, v, name):
    r, c = w.shape
    rb = r
    if r % 8 == 0 and r * c * 4 > (1 << 20):
        rb = 8
        for cand in range(8, r, 8):
            if r % cand == 0 and cand * c * 4 <= (1 << 20):
                rb = cand

    def body(w_ref, g_ref, m_ref, v_ref, d_ref, nm_ref, nv_ref):
        gv = g_ref[...]
        mn = ADAM_B1 * m_ref[...] + (1.0 - ADAM_B1) * gv
        vn = ADAM_B2 * v_ref[...] + (1.0 - ADAM_B2) * (gv * gv)
        m_hat = mn / (1.0 - ADAM_B1 ** ADAM_STEP)
        v_hat = vn / (1.0 - ADAM_B2 ** ADAM_STEP)
        d_ref[...] = -ADAM_LR * (m_hat / (jnp.sqrt(v_hat) + ADAM_EPS) + ADAM_WD * w_ref[...])
        nm_ref[...] = mn
        nv_ref[...] = vn

    spec = pl.BlockSpec((rb, c), lambda i: (i, 0))
    shp = jax.ShapeDtypeStruct((r, c), F32)
    return pl.pallas_call(body, name=name, grid=(r // rb,), in_specs=[spec] * 4, out_specs=[spec] * 3,
                          out_shape=[shp] * 3, compiler_params=_cp(1))(w, g, m, v)


BIG = ("ffn1_w_in", "ffn1_w_out", "w_mix_in", "w_mix_out", "ffn2_w_in", "ffn2_w_out")
COL_SHARDED = {"ffn1_w_in": True, "ffn1_w_out": False, "w_mix_in": True, "w_mix_out": False,
               "ffn2_w_in": True, "ffn2_w_out": False}


def _my_block(w, col, c):
    half = w.shape[0] // 2
    return lax.dynamic_slice_in_dim(w, c * half, half, axis=0)


def _unpack_full(g8, shapes):
    out, off = {}, 0
    for name in BIG:
        r, n = shapes[name]
        size = (r // 2) * n
        piece = g8[:, off:off + size]
        off += size
        if COL_SHARDED[name]:
            out[name] = piece.reshape(4, 2, r // 2, n).transpose(1, 2, 0, 3).reshape(r, 4 * n)
        else:
            out[name] = piece.reshape(4 * r, n)
    return out


def _pack_blocks(full, shapes):
    pieces = []
    for name in BIG:
        r, n = shapes[name]
        g = full[name]
        if COL_SHARDED[name]:
            pieces.append(g.reshape(2, r // 2, 4, n).transpose(2, 0, 1, 3).reshape(8, (r // 2) * n))
        else:
            pieces.append(g.reshape(8, (r // 2) * n))
    return jnp.concatenate(pieces, axis=1)


def _unpack_shard(s2, shapes):
    out, off = {}, 0
    for name in BIG:
        r, n = shapes[name]
        size = (r // 2) * n
        out[name] = s2[:, off:off + size].reshape(r, n)
        off += size
    return out


def _lower_bounds(logits):
    return jnp.cumsum(jax.nn.softmax(logits.astype(F32), axis=1), axis=1)[:, 0]


def kernel(x, c, ctx, c_ctx, w_ada, b_ada, ln_gain, ln_bias, ffn1_w_in, ffn1_w_out, w_mix_in, gla_a2_fwd, gla_a2_bwd, gla_a_bias_fwd, gla_a_bias_bwd, hgrn_lb_logits, gla_norm_gain, hgrn_norm_gain, w_mix_out, ffn2_w_in, ffn2_w_out, loss_target, m_c_ctx, m_w_ada, m_b_ada, m_ln_gain, m_ln_bias, m_ffn1_w_in, m_ffn1_w_out, m_w_mix_in, m_gla_a2_fwd, m_gla_a2_bwd, m_gla_a_bias_fwd, m_gla_a_bias_bwd, m_hgrn_lb_logits, m_gla_norm_gain, m_hgrn_norm_gain, m_w_mix_out, m_ffn2_w_in, m_ffn2_w_out, v_c_ctx, v_w_ada, v_b_ada, v_ln_gain, v_ln_bias, v_ffn1_w_in, v_ffn1_w_out, v_w_mix_in, v_gla_a2_fwd, v_gla_a2_bwd, v_gla_a_bias_fwd, v_gla_a_bias_bwd, v_hgrn_lb_logits, v_gla_norm_gain, v_hgrn_norm_gain, v_w_mix_out, v_ffn2_w_in, v_ffn2_w_out):
    xi, yi, ci = lax.axis_index("x"), lax.axis_index("y"), lax.axis_index("c")
    chip = 2 * xi + yi
    dev = 2 * chip + ci
    B = x.shape[0]
    weights = dict(ffn1_w_in=ffn1_w_in[0], ffn1_w_out=ffn1_w_out[0], w_mix_in=w_mix_in[0], w_mix_out=w_mix_out[0],
                   ffn2_w_in=ffn2_w_in[0], ffn2_w_out=ffn2_w_out[0])
    shapes = {k: v.shape for k, v in weights.items()}

    mine = jnp.concatenate([c.reshape(-1), ln_gain.reshape(-1), ln_bias.reshape(-1), gla_a2_fwd.reshape(-1),
                            gla_a2_bwd.reshape(-1), hgrn_lb_logits.reshape(-1)])
    g1 = _gather_flat(mine, "gather_cond")
    nc = B * D
    c_all = g1[:, :nc].reshape(8 * B, D)
    per_chip = g1[0::2, nc:]
    o = 0

    def take(shape, axis):
        nonlocal o
        n = int(np.prod(shape))
        parts = per_chip[:, o:o + n].reshape((4,) + shape)
        o += n
        return jnp.concatenate([parts[j] for j in range(4)], axis=axis)

    ln_gain_f = take((3, 256), 1)
    ln_bias_f = take((3, 256), 1)
    a2f_f = take((16, 64), 1)
    a2b_f = take((16, 64), 1)
    lbl_f = take((2, 2, 128), 2)
    lb, lb_vjp = jax.vjp(_lower_bounds, lbl_f)

    blk = jnp.concatenate([_my_block(weights[k], COL_SHARDED[k], ci).astype(BF16).reshape(-1) for k in BIG])
    L = blk.shape[0]
    full = _unpack_full(_big_allgather(blk.reshape(L // 128, 128)).reshape(8, L), shapes)
    wmp = _mix_in_to_padded(full["w_mix_in"])

    cs = jnp.concatenate([c_all, c_ctx.reshape(1, D), jnp.zeros((ADA_ROWS - 8 * B - 1, D), F32)], axis=0)
    ncol = w_ada.shape[2]
    b_cols = lax.dynamic_slice_in_dim(b_ada, chip * ncol, ncol, axis=1)
    m_cols = _ada_fwd(cs, w_ada[0], b_cols)
    g2 = _small_allgather(m_cols, "gather_mod")[0::2]
    m_all = jnp.concatenate([g2[j] for j in range(4)], axis=1)
    m_lat = lax.dynamic_slice_in_dim(m_all, dev * B, B, axis=0).reshape(B, 1, N_MOD, D)
    m_ctx = jnp.broadcast_to(m_all[8 * B].reshape(1, 1, N_MOD, D), (B, 1, N_MOD, D))
    mvec = jnp.concatenate([m_ctx, m_lat], axis=1)

    loss_l, grad_x, big, dm, small = _local_step(
        x, ctx, loss_target, mvec, full["ffn1_w_in"], full["ffn1_w_out"], wmp, full["w_mix_out"], full["ffn2_w_in"],
        full["ffn2_w_out"], ln_gain_f, ln_bias_f, a2f_f, a2b_f, gla_a_bias_fwd, gla_a_bias_bwd, lb,
        gla_norm_gain, hgrn_norm_gain)
    loss = lax.psum(loss_l, ("x", "y", "c"))

    dm_lat = dm[:, 1].reshape(B, N_MOD * D)
    dm_ctx = jnp.sum(dm[:, 0], axis=0).reshape(N_MOD * D)
    keys = ("ln_gain", "ln_bias", "a2f", "a2b", "abf", "abb", "lb", "gng", "gnh")
    flat = jnp.concatenate([dm_lat.reshape(-1), dm_ctx] + [small[k].reshape(-1) for k in keys])
    g3 = _gather_flat(flat, "gather_small_grads")
    nlat = B * N_MOD * D
    dm_all = g3[:, :nlat].reshape(8 * B, N_MOD * D)
    tot = _sum8(g3[:, nlat:])[0]
    dmc_tot = tot[:N_MOD * D]
    o = N_MOD * D
    sg = {}
    for k in keys:
        n = int(np.prod(small[k].shape))
        sg[k] = tot[o:o + n].reshape(small[k].shape)
        o += n
    dm_rows = jnp.concatenate([dm_all, dmc_tot.reshape(1, -1), jnp.zeros((ADA_ROWS - 8 * B - 1, N_MOD * D), F32)], axis=0)
    g_b_ada = (jnp.sum(dm_all, axis=0) + dmc_tot).reshape(1, N_MOD * D)
    g_w_ada, dcc = _ada_bwd(cs, w_ada[0], lax.dynamic_slice_in_dim(dm_rows, chip * ncol, ncol, axis=1))
    g4 = _gather_flat(dcc[0], "gather_cctx")
    dsilu = ((g4[0] + g4[2]) + g4[4]) + g4[6]
    sc = _sigmoid(c_ctx)
    g_c_ctx = dsilu * (sc * (1.0 + c_ctx * (1.0 - sc)))
    (g_lbl,) = lb_vjp(sg["lb"])

    def cols(a, n, axis):
        return lax.dynamic_slice_in_dim(a, chip * n, n, axis=axis)

    big_full = dict(ffn1_w_in=big["w1i"], ffn1_w_out=big["w1o"], w_mix_in=_mix_in_from_padded(big["wmp"]),
                    w_mix_out=big["wmo"], ffn2_w_in=big["w2i"], ffn2_w_out=big["w2o"])
    g8 = _pack_blocks(big_full, shapes).reshape(8, L // 128, 128)
    cvec, jvec = ci.reshape(1).astype(jnp.int32), chip.reshape(1).astype(jnp.int32)
    h4 = _rs_add_pair(g8, _rs_pair_exchange(g8), cvec)
    fin = _rs_add_chips(h4, _rs_chip_exchange(h4), jvec)
    gsh = _unpack_shard(_rs_pair_share(fin).reshape(2, L), shapes)

    grads = dict(
        c_ctx=g_c_ctx, w_ada=g_w_ada[None], b_ada=g_b_ada, ln_gain=cols(sg["ln_gain"], 256, 1)[None],
        ln_bias=cols(sg["ln_bias"], 256, 1)[None], ffn1_w_in=gsh["ffn1_w_in"][None], ffn1_w_out=gsh["ffn1_w_out"][None],
        w_mix_in=gsh["w_mix_in"][None], gla_a2_fwd=cols(sg["a2f"], 64, 1)[None], gla_a2_bwd=cols(sg["a2b"], 64, 1)[None],
        gla_a_bias_fwd=sg["abf"], gla_a_bias_bwd=sg["abb"], hgrn_lb_logits=cols(g_lbl, 128, 2),
        gla_norm_gain=sg["gng"].reshape(1, HD), hgrn_norm_gain=sg["gnh"].reshape(1, HD),
        w_mix_out=gsh["w_mix_out"][None], ffn2_w_in=gsh["ffn2_w_in"][None], ffn2_w_out=gsh["ffn2_w_out"][None])
    params = dict(
        c_ctx=(c_ctx, m_c_ctx, v_c_ctx), w_ada=(w_ada, m_w_ada, v_w_ada), b_ada=(b_ada, m_b_ada, v_b_ada),
        ln_gain=(ln_gain, m_ln_gain, v_ln_gain), ln_bias=(ln_bias, m_ln_bias, v_ln_bias),
        ffn1_w_in=(ffn1_w_in, m_ffn1_w_in, v_ffn1_w_in), ffn1_w_out=(ffn1_w_out, m_ffn1_w_out, v_ffn1_w_out),
        w_mix_in=(w_mix_in, m_w_mix_in, v_w_mix_in), gla_a2_fwd=(gla_a2_fwd, m_gla_a2_fwd, v_gla_a2_fwd),
        gla_a2_bwd=(gla_a2_bwd, m_gla_a2_bwd, v_gla_a2_bwd),
        gla_a_bias_fwd=(gla_a_bias_fwd, m_gla_a_bias_fwd, v_gla_a_bias_fwd),
        gla_a_bias_bwd=(gla_a_bias_bwd, m_gla_a_bias_bwd, v_gla_a_bias_bwd),
        hgrn_lb_logits=(hgrn_lb_logits, m_hgrn_lb_logits, v_hgrn_lb_logits),
        gla_norm_gain=(gla_norm_gain, m_gla_norm_gain, v_gla_norm_gain),
        hgrn_norm_gain=(hgrn_norm_gain, m_hgrn_norm_gain, v_hgrn_norm_gain),
        w_mix_out=(w_mix_out, m_w_mix_out, v_w_mix_out), ffn2_w_in=(ffn2_w_in, m_ffn2_w_in, v_ffn2_w_in),
        ffn2_w_out=(ffn2_w_out, m_ffn2_w_out, v_ffn2_w_out))
    order = list(params.keys())
    big_names = ("w_ada",) + BIG
    upd = {}
    for k in big_names:
        w_, m_, v_ = params[k]
        s2 = w_.shape[-2:]
        d_, nm_, nv_ = _adamw(w_.reshape(s2), grads[k].reshape(s2), m_.reshape(s2), v_.reshape(s2), "adamw_" + k)
        upd[k] = (d_.reshape(w_.shape), nm_.reshape(w_.shape), nv_.reshape(w_.shape))
    small_names = [k for k in order if k not in big_names]
    sizes = [int(np.prod(params[k][0].shape)) for k in small_names]
    tot_n = sum(sizes)
    npad = -(-tot_n // 1024) * 1024

    def packed(get):
        flat_ = jnp.concatenate([get(k).reshape(-1) for k in small_names])
        return jnp.pad(flat_, (0, npad - tot_n)).reshape(8, npad // 8)

    d_s, nm_s, nv_s = _adamw(packed(lambda k: params[k][0]), packed(lambda k: grads[k]),
                             packed(lambda k: params[k][1]), packed(lambda k: params[k][2]), "adamw_small")
    o = 0
    for k, n in zip(small_names, sizes):
        shp = params[k][0].shape
        upd[k] = tuple(a.reshape(-1)[o:o + n].reshape(shp) for a in (d_s, nm_s, nv_s))
        o += n

    return (loss, grad_x, *[grads[k].reshape(params[k][0].shape) for k in order], *[upd[k][0] for k in order],
            *[upd[k][1] for k in order], *[upd[k][2] for k in order])
```

```python
import functools

import numpy as np
import jax
import jax.numpy as jnp
from jax import lax
from jax.experimental import pallas as pl
from jax.experimental.pallas import tpu as pltpu

F32 = jnp.float32
BF16 = jnp.bfloat16
MESH = pl.DeviceIdType.MESH

D = 1024
DFF = 2816
TM = 256
CH = 32
NCB = TM // CH
HD = 128
NH = 8
LN_EPS = 1e-5
NORM_EPS = 1e-6
ALPHA = 2.0 ** 0.25
GATE_NORM = 16.0
GLA_DK = 64
N_MOD = 9
VMEM_LIMIT = 52 * 1024 * 1024

MIXP = 5120
GG, RG, GQ, GK, GV, RQ, RFF, RFB, RI, LR = 0, 512, 1024, 1536, 2048, 2560, 3072, 3584, 4096, 4608
IN_SPLITS = (256, 256, 512, 512, 16, 16, 512, 512, 512, 512, 512)

ADAM_LR, ADAM_B1, ADAM_B2, ADAM_EPS, ADAM_WD, ADAM_STEP = 0.001, 0.9, 0.999, 1e-08, 0.01, 10


def _cp(n_axes):
    return pltpu.CompilerParams(dimension_semantics=("arbitrary",) * n_axes, vmem_limit_bytes=VMEM_LIMIT)


def _rowmap(stride, off):
    return lambda b, i: (b * stride + off + i, 0)


def _mmap(comb):
    if comb:
        return lambda b, i: (b, jnp.minimum(i, 1), 0, 0)
    return lambda b, i: (b, 1, 0, 0)


def _ln(x):
    mu = jnp.mean(x, axis=-1, keepdims=True)
    xc = x - mu
    var = jnp.mean(xc * xc, axis=-1, keepdims=True)
    r = lax.rsqrt(var + LN_EPS)
    return xc * r, r


def _ln_bwd(dxh, xh, r):
    return r * (dxh - jnp.mean(dxh, axis=-1, keepdims=True) - xh * jnp.mean(dxh * xh, axis=-1, keepdims=True))


def _sigmoid(x):
    return 1.0 / (1.0 + jnp.exp(-x))


def _rsum(x):
    return jnp.sum(x, axis=0, keepdims=True)


def _dot(a, b):
    return jnp.dot(a, b, preferred_element_type=F32)


def _dot_nt(a, b):
    return lax.dot_general(a, b, (((1,), (1,)), ((), ())), preferred_element_type=F32)


def _dot_tn(a, b):
    return lax.dot_general(a, b, (((0,), (0,)), ((), ())), preferred_element_type=F32)


def _embed(x, ctx, pos):
    B, T, _ = x.shape
    nt = 1 + T // TM

    def body(x_ref, c_ref, p_ref, o_ref):
        i = pl.program_id(1)

        @pl.when(i == 0)
        def _():
            o_ref[...] = c_ref[0]

        @pl.when(i > 0)
        def _():
            o_ref[...] = x_ref[0] + p_ref[...]

    return pl.pallas_call(
        body, name="embed", grid=(B, nt),
        in_specs=[pl.BlockSpec((1, TM, D), lambda b, i: (b, jnp.maximum(i - 1, 0), 0)),
                  pl.BlockSpec((1, TM, D), lambda b, i: (b, 0, 0)),
                  pl.BlockSpec((TM, D), lambda b, i: (jnp.maximum(i - 1, 0), 0))],
        out_specs=pl.BlockSpec((TM, D), lambda b, i: (b * nt + i, 0)),
        out_shape=jax.ShapeDtypeStruct((B * nt * TM, D), F32),
        compiler_params=_cp(2))(x, ctx, pos)


def _lnmod(X, lx, mvec, comb, sub, B, nt, name):
    def body(x_ref, m_ref, o_ref):
        xh, _ = _ln(x_ref[...])
        sh = m_ref[0, 0, 3 * sub:3 * sub + 1, :]
        sc = m_ref[0, 0, 3 * sub + 1:3 * sub + 2, :]
        o_ref[...] = (xh * (1.0 + sh) + sc).astype(BF16)

    return pl.pallas_call(
        body, name=name, grid=(B, nt),
        in_specs=[pl.BlockSpec((TM, D), _rowmap(*lx)), pl.BlockSpec((1, 1, N_MOD, D), _mmap(comb))],
        out_specs=pl.BlockSpec((TM, D), _rowmap(nt, 0)),
        out_shape=jax.ShapeDtypeStruct((B * nt * TM, D), BF16),
        compiler_params=_cp(2))(X, mvec)


def _lnmod_bwd(dh, X, lx, mvec, comb, sub, B, nt, name, dres=None, ldres=None, dres_zero_ctx=False, to_x=None):
    has_res = dres is not None

    def body(*refs):
        if has_res:
            dh_ref, x_ref, m_ref, dr_ref, dx_ref, dm_ref = refs
        else:
            dh_ref, x_ref, m_ref, dx_ref, dm_ref = refs
        i = pl.program_id(1)
        xh, r = _ln(x_ref[...])
        sh = m_ref[0, 0, 3 * sub:3 * sub + 1, :]
        dhv = dh_ref[...]
        dx = _ln_bwd(dhv * (1.0 + sh), xh, r)
        if has_res:
            dr = dr_ref[...]
            if dres_zero_ctx:
                dr = jnp.where(i > 0, dr, 0.0)
            dx = dx + dr

        init = (i == 0) | (i == 1) if comb else (i == 0)

        @pl.when(init)
        def _():
            dm_ref[...] = jnp.zeros_like(dm_ref)

        dm_ref[0, 0, 0:1, :] += _rsum(dhv * xh)
        dm_ref[0, 0, 1:2, :] += _rsum(dhv)
        if to_x is None:
            dx_ref[...] = dx
        else:
            @pl.when(i > 0)
            def _():
                dx_ref[0] = dx

    in_specs = [pl.BlockSpec((TM, D), _rowmap(nt, 0)), pl.BlockSpec((TM, D), _rowmap(*lx)),
                pl.BlockSpec((1, 1, N_MOD, D), _mmap(comb))]
    args = [dh, X, mvec]
    if has_res:
        if dres_zero_ctx:
            s, o = ldres
            in_specs.append(pl.BlockSpec((TM, D), lambda b, i: (b * s + o + jnp.maximum(i - 1, 0), 0)))
        else:
            in_specs.append(pl.BlockSpec((TM, D), _rowmap(*ldres)))
        args.append(dres)
    if to_x is None:
        dx_spec = pl.BlockSpec((TM, D), _rowmap(nt, 0))
        dx_shape = jax.ShapeDtypeStruct((B * nt * TM, D), F32)
    else:
        dx_spec = pl.BlockSpec((1, TM, D), lambda b, i: (b, jnp.maximum(i - 1, 0), 0))
        dx_shape = jax.ShapeDtypeStruct(to_x, F32)
    return pl.pallas_call(
        body, name=name, grid=(B, nt), in_specs=in_specs,
        out_specs=[dx_spec, pl.BlockSpec((1, 1, 2, D), _mmap(comb))],
        out_shape=[dx_shape, jax.ShapeDtypeStruct((B, 2, 2, D), F32)],
        compiler_params=_cp(2))(*args)


def _resid(X, lx, Y, mvec, comb, sub, w, gb, B, nt, name):
    def body(x_ref, y_ref, m_ref, gb_ref, o_ref):
        g = m_ref[0, 0, 3 * sub + 2:3 * sub + 3, :]
        z = ALPHA * x_ref[...] + (w * g) * y_ref[...]
        zh, _ = _ln(z)
        o_ref[...] = zh * gb_ref[0:1, :] + gb_ref[1:2, :]

    return pl.pallas_call(
        body, name=name, grid=(B, nt),
        in_specs=[pl.BlockSpec((TM, D), _rowmap(*lx)), pl.BlockSpec((TM, D), _rowmap(nt, 0)),
                  pl.BlockSpec((1, 1, N_MOD, D), _mmap(comb)), pl.BlockSpec((2, D), lambda b, i: (0, 0))],
        out_specs=pl.BlockSpec((TM, D), _rowmap(nt, 0)),
        out_shape=jax.ShapeDtypeStruct((B * nt * TM, D), F32),
        compiler_params=_cp(2))(X, Y, mvec, gb)


def _resid_bwd(dxn, X, lx, Y, mvec, comb, sub, w, gb, B, nt, name):
    def body(d_ref, x_ref, y_ref, m_ref, gb_ref, dx_ref, dy_ref, dg_ref, dgb_ref):
        b_, i = pl.program_id(0), pl.program_id(1)
        g = m_ref[0, 0, 3 * sub + 2:3 * sub + 3, :]
        y = y_ref[...]
        z = ALPHA * x_ref[...] + (w * g) * y
        zh, r = _ln(z)
        do = d_ref[...]
        dz = _ln_bwd(do * gb_ref[0:1, :], zh, r)
        dx_ref[...] = ALPHA * dz
        dy_ref[...] = ((w * g) * dz).astype(BF16)

        @pl.when((b_ == 0) & (i == 0))
        def _():
            dgb_ref[...] = jnp.zeros_like(dgb_ref)

        dgb_ref[0:1, :] += _rsum(do * zh)
        dgb_ref[1:2, :] += _rsum(do)

        init = (i == 0) | (i == 1) if comb else (i == 0)

        @pl.when(init)
        def _():
            dg_ref[...] = jnp.zeros_like(dg_ref)

        dg_ref[0, 0] += w * _rsum(dz * y)

    return pl.pallas_call(
        body, name=name, grid=(B, nt),
        in_specs=[pl.BlockSpec((TM, D), _rowmap(nt, 0)), pl.BlockSpec((TM, D), _rowmap(*lx)),
                  pl.BlockSpec((TM, D), _rowmap(nt, 0)), pl.BlockSpec((1, 1, N_MOD, D), _mmap(comb)),
                  pl.BlockSpec((2, D), lambda b, i: (0, 0))],
        out_specs=[pl.BlockSpec((TM, D), _rowmap(nt, 0)), pl.BlockSpec((TM, D), _rowmap(nt, 0)),
                   pl.BlockSpec((1, 1, 1, D), _mmap(comb)), pl.BlockSpec((2, D), lambda b, i: (0, 0))],
        out_shape=[jax.ShapeDtypeStruct((B * nt * TM, D), F32), jax.ShapeDtypeStruct((B * nt * TM, D), BF16),
                   jax.ShapeDtypeStruct((B, 2, 1, D), F32), jax.ShapeDtypeStruct((2, D), F32)],
        compiler_params=_cp(2))(dxn, X, Y, mvec, gb)


def _swiglu_bwd(u, da, name):
    M = u.shape[0]

    def body(u_ref, da_ref, du_ref):
        g = u_ref[:, :DFF]
        up = u_ref[:, DFF:]
        dav = da_ref[...]
        s = _sigmoid(g)
        du_ref[:, :DFF] = (dav * up * (s * (1.0 + g * (1.0 - s)))).astype(BF16)
        du_ref[:, DFF:] = (dav * (g * s)).astype(BF16)

    return pl.pallas_call(
        body, name=name, grid=(M // TM,),
        in_specs=[pl.BlockSpec((TM, 2 * DFF), lambda i: (i, 0)), pl.BlockSpec((TM, DFF), lambda i: (i, 0))],
        out_specs=pl.BlockSpec((TM, 2 * DFF), lambda i: (i, 0)),
        out_shape=jax.ShapeDtypeStruct((M, 2 * DFF), BF16), compiler_params=_cp(1))(u, da)


def _loss_grad(x3, tgt, B, nt):
    def body(x_ref, t_ref, d_ref, l_ref):
        b_, i = pl.program_id(0), pl.program_id(1)
        e = x_ref[...] - t_ref[0]
        d_ref[...] = e * (1.0 / D)

        @pl.when((b_ == 0) & (i == 0))
        def _():
            l_ref[...] = jnp.zeros_like(l_ref)

        l_ref[...] += _rsum(e * e)

    return pl.pallas_call(
        body, name="loss_grad", grid=(B, nt),
        in_specs=[pl.BlockSpec((TM, D), _rowmap(nt, 0)), pl.BlockSpec((1, TM, D), lambda b, i: (b, i, 0))],
        out_specs=[pl.BlockSpec((TM, D), _rowmap(nt, 0)), pl.BlockSpec((1, D), lambda b, i: (0, 0))],
        out_shape=[jax.ShapeDtypeStruct((B * nt * TM, D), F32), jax.ShapeDtypeStruct((1, D), F32)],
        compiler_params=_cp(2))(x3, tgt)


def _wspec(W, nidx):
    zeros = (0,) * W.ndim
    if nidx == 1:
        return pl.BlockSpec(W.shape, lambda i: zeros)
    return pl.BlockSpec(W.shape, lambda b, i: zeros)


def _mm_nn(A, la, W, B, nt, out_dtype, name):
    K, N = W.shape

    def body(a_ref, w_ref, o_ref):
        o_ref[...] = _dot(a_ref[...], w_ref[...]).astype(out_dtype)

    return pl.pallas_call(
        body, name=name, grid=(B, nt),
        in_specs=[pl.BlockSpec((TM, K), _rowmap(*la)), _wspec(W, 2)],
        out_specs=pl.BlockSpec((TM, N), _rowmap(nt, 0)),
        out_shape=jax.ShapeDtypeStruct((B * nt * TM, N), out_dtype), compiler_params=_cp(2))(A, W)


def _ffn_in(A, la, W3, B, nt, name):
    K, n = W3.shape[1:]

    def body(a_ref, w_ref, u_ref, s_ref):
        a = a_ref[...]
        for j in range(2):
            g = _dot(a, w_ref[j])
            up = _dot(a, w_ref[j + 2])
            u_ref[:, j * n:(j + 1) * n] = g
            u_ref[:, (j + 2) * n:(j + 3) * n] = up
            s_ref[:, j * n:(j + 1) * n] = (g * _sigmoid(g) * up).astype(BF16)

    rows = B * nt * TM
    return pl.pallas_call(
        body, name=name, grid=(B, nt),
        in_specs=[pl.BlockSpec((TM, K), _rowmap(*la)), _wspec(W3, 2)],
        out_specs=[pl.BlockSpec((TM, 4 * n), _rowmap(nt, 0)), pl.BlockSpec((TM, 2 * n), _rowmap(nt, 0))],
        out_shape=[jax.ShapeDtypeStruct((rows, 4 * n), F32), jax.ShapeDtypeStruct((rows, 2 * n), BF16)],
        compiler_params=_cp(2))(A, W3)


def _mm_nt(A, W, name):
    M, N = A.shape
    K = W.shape[-2]

    def body(a_ref, w_ref, o_ref):
        if W.ndim == 3:
            n = W.shape[-1]
            acc = _dot_nt(a_ref[:, 0:n], w_ref[0])
            for j in range(1, 4):
                acc = acc + _dot_nt(a_ref[:, j * n:(j + 1) * n], w_ref[j])
            o_ref[...] = acc
        else:
            o_ref[...] = _dot_nt(a_ref[...], w_ref[...])

    return pl.pallas_call(
        body, name=name, grid=(M // TM,),
        in_specs=[pl.BlockSpec((TM, N), lambda i: (i, 0)), _wspec(W, 1)],
        out_specs=pl.BlockSpec((TM, K), lambda i: (i, 0)),
        out_shape=jax.ShapeDtypeStruct((M, K), F32), compiler_params=_cp(1))(A, W)


def _mm_tn(A, G, name, tk=512, tn=512, shards=None):
    M, K = A.shape
    N = G.shape[1]
    if shards:
        tn = N // shards

    def body(a_ref, g_ref, o_ref):
        @pl.when(pl.program_id(1) == 0)
        def _():
            o_ref[...] = jnp.zeros_like(o_ref)

        upd = _dot_tn(a_ref[...], g_ref[...])
        if shards:
            o_ref[0] += upd
        else:
            o_ref[...] += upd

    if shards:
        out_spec = pl.BlockSpec((1, K, tn), lambda n, k: (n, 0, 0))
        out_shape = jax.ShapeDtypeStruct((shards, K, tn), F32)
    else:
        out_spec = pl.BlockSpec((K, tn), lambda n, k: (0, n))
        out_shape = jax.ShapeDtypeStruct((K, N), F32)
    return pl.pallas_call(
        body, name=name, grid=(N // tn, M // tk),
        in_specs=[pl.BlockSpec((tk, K), lambda n, k: (k, 0)), pl.BlockSpec((tk, tn), lambda n, k: (k, n))],
        out_specs=out_spec, out_shape=out_shape, compiler_params=_cp(2))(A, G)


def _logsig(z):
    return jnp.minimum(z, 0.0) - jnp.log(1.0 + jnp.exp(-jnp.abs(z)))


def _features(F, a2p, biasp, lbp, B):
    nt = F.shape[0] // (B * TM)
    nj = nt + 1
    R2 = nj * TM

    def body(f_ref, a2_ref, bias_ref, lb_ref, q_ref, k_ref, v_ref, g_ref):
        lr = f_ref[:, LR:LR + HD].astype(BF16)
        lane = lax.broadcasted_iota(jnp.int32, (1, 4 * HD), 1)
        keep = (lane & (HD - 1)) < GLA_DK
        for d in range(2):
            z = _dot(lr, a2_ref[d]) + bias_ref[d:d + 1, :]
            gl = jnp.where(keep, _logsig(z) * (1.0 / GATE_NORM), 0.0)
            for h in range(4):
                g_ref[d, 0, h] = gl[:, h * HD:(h + 1) * HD]
        for h in range(4):
            q_ref[0, h] = f_ref[:, GQ + h * HD:GQ + (h + 1) * HD] * (GLA_DK ** -0.5)
            kk = f_ref[:, GK + h * HD:GK + (h + 1) * HD]
            k_ref[0, 0, h] = kk
            k_ref[1, 0, h] = kk
            v_ref[0, h] = f_ref[:, GV + h * HD:GV + (h + 1) * HD]
        for h in range(4):
            sl = slice(h * HD, (h + 1) * HD)
            rq = f_ref[:, RQ + h * HD:RQ + (h + 1) * HD]
            q_ref[0, 4 + h] = rq * _sigmoid(rq) * (HD ** -0.5)
            v_ref[0, 4 + h] = f_ref[:, RI + h * HD:RI + (h + 1) * HD]
            for d, off in ((0, RFF), (1, RFB)):
                lb = lb_ref[d:d + 1, sl]
                f = lb + (1.0 - lb) * _sigmoid(f_ref[:, off + h * HD:off + (h + 1) * HD])
                g_ref[d, 0, 4 + h] = jnp.log(f)
                k_ref[d, 0, 4 + h] = 1.0 - f

    fmap = lambda b, j: (b * nt + jnp.where(j == nt, 0, j), 0)
    one = pl.BlockSpec((1, NH, TM, HD), lambda b, j: (b, 0, j, 0))
    two = pl.BlockSpec((2, 1, NH, TM, HD), lambda b, j: (0, b, 0, j, 0))
    s1 = jax.ShapeDtypeStruct((B, NH, R2, HD), F32)
    s2 = jax.ShapeDtypeStruct((2, B, NH, R2, HD), F32)
    return pl.pallas_call(
        body, name="mix_features", grid=(B, nj),
        in_specs=[pl.BlockSpec((TM, MIXP), fmap), pl.BlockSpec((2, HD, 4 * HD), lambda b, j: (0, 0, 0)),
                  pl.BlockSpec((2, 4 * HD), lambda b, j: (0, 0)), pl.BlockSpec((2, 4 * HD), lambda b, j: (0, 0))],
        out_specs=[one, two, one, two], out_shape=[s1, s2, s1, s2],
        compiler_params=_cp(2))(F, a2p, biasp, lbp)


def _features_bwd(F, a2p, biasp, lbp, dQ0, dQ1, dK0, dK1, dV0, dV1, dG0, dG1, dgates, B):
    nt = F.shape[0] // (B * TM)

    def body(f_ref, a2_ref, bias_ref, lb_ref, dq0, dq1, dk0, dk1, dv0, dv1, dg0, dg1, dgt_ref,
             df_ref, da2_ref, dbias_ref, dlb_ref):
        b_, i = pl.program_id(0), pl.program_id(1)

        @pl.when((b_ == 0) & (i == 0))
        def _():
            da2_ref[...] = jnp.zeros_like(da2_ref)
            dbias_ref[...] = jnp.zeros_like(dbias_ref)
            dlb_ref[...] = jnp.zeros_like(dlb_ref)

        df_ref[:, 0:2 * 4 * HD] = jnp.where(i > 0, dgt_ref[...], 0.0).astype(BF16)
        df_ref[:, LR + HD:] = jnp.zeros((TM, MIXP - LR - HD), BF16)
        lr = f_ref[:, LR:LR + HD].astype(BF16)
        lane = lax.broadcasted_iota(jnp.int32, (1, 4 * HD), 1)
        keep = (lane & (HD - 1)) < GLA_DK
        dlr = jnp.zeros((TM, HD), F32)
        dgs = (dg0, dg1)
        dks = (dk0, dk1)
        for d in range(2):
            z = _dot(lr, a2_ref[d]) + bias_ref[d:d + 1, :]
            dgl = jnp.concatenate([dgs[d][0, h] for h in range(4)], axis=1)
            dz = jnp.where(keep, dgl * (1.0 / GATE_NORM) * (1.0 - _sigmoid(z)), 0.0)
            dzb = dz.astype(BF16)
            dlr = dlr + _dot_nt(dzb, a2_ref[d])
            da2_ref[d] += _dot_tn(lr, dzb)
            dbias_ref[d:d + 1, :] += _rsum(dz)
        df_ref[:, LR:LR + HD] = dlr.astype(BF16)
        for h in range(4):
            df_ref[:, GQ + h * HD:GQ + (h + 1) * HD] = ((dq0[0, h] + dq1[0, h]) * (GLA_DK ** -0.5)).astype(BF16)
            df_ref[:, GK + h * HD:GK + (h + 1) * HD] = (dk0[0, h] + dk1[0, h]).astype(BF16)
            df_ref[:, GV + h * HD:GV + (h + 1) * HD] = (dv0[0, h] + dv1[0, h]).astype(BF16)
        for h in range(4):
            sl = slice(h * HD, (h + 1) * HD)
            rq = f_ref[:, RQ + h * HD:RQ + (h + 1) * HD]
            s = _sigmoid(rq)
            dqh = dq0[0, 4 + h] + dq1[0, 4 + h]
            df_ref[:, RQ + h * HD:RQ + (h + 1) * HD] = (dqh * (HD ** -0.5) * (s * (1.0 + rq * (1.0 - s)))).astype(BF16)
            df_ref[:, RI + h * HD:RI + (h + 1) * HD] = (dv0[0, 4 + h] + dv1[0, 4 + h]).astype(BF16)
            for d, off in ((0, RFF), (1, RFB)):
                lb = lb_ref[d:d + 1, sl]
                sg = _sigmoid(f_ref[:, off + h * HD:off + (h + 1) * HD])
                f = lb + (1.0 - lb) * sg
                dff = dgs[d][0, 4 + h] / f - dks[d][0, 4 + h]
                df_ref[:, off + h * HD:off + (h + 1) * HD] = (dff * (1.0 - lb) * sg * (1.0 - sg)).astype(BF16)
                dlb_ref[d:d + 1, sl] += _rsum(dff * (1.0 - sg))

    m0 = lambda b, i: (b, 0, i, 0)
    m1 = lambda b, i: (b, 0, jnp.where(i == 0, nt, i), 0)
    one = lambda m: pl.BlockSpec((1, NH, TM, HD), m)
    return pl.pallas_call(
        body, name="mix_features_bwd", grid=(B, nt),
        in_specs=[pl.BlockSpec((TM, MIXP), _rowmap(nt, 0)), pl.BlockSpec((2, HD, 4 * HD), lambda b, i: (0, 0, 0)),
                  pl.BlockSpec((2, 4 * HD), lambda b, i: (0, 0)), pl.BlockSpec((2, 4 * HD), lambda b, i: (0, 0)),
                  one(m0), one(m1), one(m0), one(m1), one(m0), one(m1), one(m0), one(m1),
                  pl.BlockSpec((TM, D), lambda b, i: (b * (nt - 1) + jnp.maximum(i - 1, 0), 0))],
        out_specs=[pl.BlockSpec((TM, MIXP), _rowmap(nt, 0)), pl.BlockSpec((2, HD, 4 * HD), lambda b, i: (0, 0, 0)),
                   pl.BlockSpec((2, 4 * HD), lambda b, i: (0, 0)), pl.BlockSpec((2, 4 * HD), lambda b, i: (0, 0))],
        out_shape=[jax.ShapeDtypeStruct((B * nt * TM, MIXP), BF16), jax.ShapeDtypeStruct((2, HD, 4 * HD), F32),
                   jax.ShapeDtypeStruct((2, 4 * HD), F32), jax.ShapeDtypeStruct((2, 4 * HD), F32)],
        compiler_params=_cp(2))(F, a2p, biasp, lbp, dQ0, dQ1, dK0, dK1, dV0, dV1, dG0, dG1, dgates)


def _chunk_scan(x, rin, fwd, inclusive=True):
    acc = x
    sft = 1
    while sft < CH:
        if fwd:
            acc = acc + jnp.where(rin >= sft, pltpu.roll(acc, sft, 0), 0.0)
        else:
            acc = acc + jnp.where(rin < CH - sft, pltpu.roll(acc, TM - sft, 0), 0.0)
        sft *= 2
    return acc if inclusive else acc - x


def _scan_common(q, k, v, g, rev):
    rin = lax.broadcasted_iota(jnp.int32, (TM, HD), 0) & (CH - 1)
    b = _chunk_scan(g, rin, not rev)
    xx = _chunk_scan(g, rin, rev, inclusive=False)
    eb = jnp.exp(b)
    qd = q * eb
    ki = k * jnp.exp(-b)
    kt = k * jnp.exp(xx)
    ri = lax.broadcasted_iota(jnp.int32, (TM, TM), 0)
    ci = lax.broadcasted_iota(jnp.int32, (TM, TM), 1)
    same = (ri >> 5) == (ci >> 5)
    lo = same & (ri >= ci)
    up = same & (ri <= ci)
    mask, maskT = (up, lo) if rev else (lo, up)
    re = lax.broadcasted_iota(jnp.int32, (TM, NCB * HD), 0) >> 5
    ce = lax.broadcasted_iota(jnp.int32, (TM, NCB * HD), 1) >> 7
    mexp = re == ce
    return rin, b, xx, eb, qd, ki, kt, mask, maskT, mexp


def _expand(xb, mexp):
    return jnp.where(mexp, jnp.concatenate([xb] * NCB, axis=1), jnp.zeros((), xb.dtype))


def _own(x, mexp):
    xm = jnp.where(mexp, x, 0.0)
    acc = xm[:, 0:HD]
    for n in range(1, NCB):
        acc = acc + xm[:, n * HD:(n + 1) * HD]
    return acc


def _state_pass(s0, eb, ut, rev):
    order = range(NCB - 1, -1, -1) if rev else range(NCB)
    states = [None] * NCB
    s = s0
    for n in order:
        row = n * CH if rev else n * CH + CH - 1
        states[n] = s
        s = eb[row:row + 1, :] * s + ut[:, n * HD:(n + 1) * HD]
    return jnp.concatenate(states, axis=1), s


def _scan_fwd(Q, K, V, G, rev, B):
    nb = Q.shape[2] // TM - 1
    d = 1 if rev else 0
    rmap = (lambda s: nb - s) if rev else (lambda s: s)

    def body(q_ref, k_ref, v_ref, g_ref, o_ref, st_ref, s_scr):
        @pl.when(pl.program_id(2) == 0)
        def _():
            s_scr[...] = jnp.zeros_like(s_scr)

        s0 = s_scr[...]
        st_ref[0, 0, 0] = s0
        v = v_ref[0, 0]
        _, _, _, eb, qd, ki, kt, mask, _, mexp = _scan_common(q_ref[0, 0], k_ref[0, 0, 0], v, g_ref[0, 0, 0], rev)
        qb, kib, vb = qd.astype(BF16), ki.astype(BF16), v.astype(BF16)
        a = jnp.where(mask, _dot_nt(qb, kib), 0.0)
        ut = _dot_tn(vb, _expand(kt.astype(BF16), mexp))
        sstack, s_new = _state_pass(s0, eb, ut, rev)
        s_scr[...] = s_new
        o_ref[0, 0] = _dot(a.astype(BF16), vb) + _dot_nt(_expand(qb, mexp), sstack.astype(BF16))

    one = pl.BlockSpec((1, 1, TM, HD), lambda b, h, s: (b, h, rmap(s), 0))
    two = pl.BlockSpec((1, 1, 1, TM, HD), lambda b, h, s: (d, b, h, rmap(s), 0))
    return pl.pallas_call(
        body, name="scan_fwd_rev" if rev else "scan_fwd", grid=(B, NH, nb),
        in_specs=[one, two, one, two],
        out_specs=[one, pl.BlockSpec((1, 1, 1, HD, HD), lambda b, h, s: (b, h, s, 0, 0))],
        out_shape=[jax.ShapeDtypeStruct(Q.shape, F32), jax.ShapeDtypeStruct((B, NH, nb, HD, HD), F32)],
        scratch_shapes=[pltpu.VMEM((HD, HD), F32)],
        compiler_params=_cp(3))(Q, K, V, G)


def _scan_bwd(Q, K, V, G, St, dO, rev, B):
    nb = Q.shape[2] // TM - 1
    d = 1 if rev else 0
    smap = lambda t: nb - 1 - t
    rmap = (lambda t: nb - smap(t)) if rev else smap

    def body(q_ref, k_ref, v_ref, g_ref, st_ref, do_ref, dq_ref, dk_ref, dv_ref, dg_ref, ds_scr):
        t = pl.program_id(2)

        @pl.when(t == 0)
        def _():
            ds_scr[...] = jnp.zeros_like(ds_scr)

        is_lat = smap(t) >= 1
        v = v_ref[0, 0]
        rin, b, xx, eb, qd, ki, kt, mask, maskT, mexp = _scan_common(
            q_ref[0, 0], k_ref[0, 0, 0], v, g_ref[0, 0, 0], rev)
        qb, kib, ktb, vb = qd.astype(BF16), ki.astype(BF16), kt.astype(BF16), v.astype(BF16)
        dob = jnp.where(is_lat, do_ref[0, 0], 0.0).astype(BF16)
        kt_exp = _expand(ktb, mexp)
        ut = _dot_tn(vb, kt_exp)
        sstack, _ = _state_pass(st_ref[0, 0, 0], eb, ut, rev)
        gt = _dot_tn(dob, _expand(qb, mexp))
        order = range(NCB) if rev else range(NCB - 1, -1, -1)
        dsp = [None] * NCB
        t2 = [None] * NCB
        dsc = ds_scr[...]
        for n in order:
            row = n * CH if rev else n * CH + CH - 1
            ebl = eb[row:row + 1, :]
            dsp[n] = dsc
            t2[n] = jnp.broadcast_to(ebl * _rsum(sstack[:, n * HD:(n + 1) * HD] * dsc), (CH, HD))
            dsc = gt[:, n * HD:(n + 1) * HD] + ebl * dsc
        ds_scr[...] = dsc
        dspb = jnp.concatenate(dsp, axis=1).astype(BF16)
        da = jnp.where(mask, _dot_nt(dob, vb), 0.0).astype(BF16)
        dat = jnp.where(maskT, _dot_nt(vb, dob), 0.0).astype(BF16)
        at = jnp.where(maskT, _dot_nt(kib, qb), 0.0).astype(BF16)
        dqd = _dot(da, kib) + _own(_dot(dob, sstack.astype(BF16)), mexp)
        dki = _dot(dat, qb)
        dv_ref[0, 0] = _dot(at, dob) + _dot_nt(kt_exp, dspb)
        dkt = _own(_dot(vb, dspb), mexp)
        z = dkt * kt
        db = dqd * qd - dki * ki
        dq_ref[0, 0] = dqd * eb
        dk_ref[0, 0] = dki * jnp.exp(-b) + dkt * jnp.exp(xx)
        dg_ref[0, 0] = (_chunk_scan(db, rin, rev) + _chunk_scan(z, rin, not rev, inclusive=False)
                        + jnp.concatenate(t2, axis=0))

    one = pl.BlockSpec((1, 1, TM, HD), lambda b, h, t: (b, h, rmap(t), 0))
    two = pl.BlockSpec((1, 1, 1, TM, HD), lambda b, h, t: (d, b, h, rmap(t), 0))
    lat = pl.BlockSpec((1, 1, TM, HD), lambda b, h, t: (b, h, jnp.clip(rmap(t) - 1, 0, nb - 2), 0))
    shp = jax.ShapeDtypeStruct(Q.shape, F32)
    return pl.pallas_call(
        body, name="scan_bwd_rev" if rev else "scan_bwd", grid=(B, NH, nb),
        in_specs=[one, two, one, two, pl.BlockSpec((1, 1, 1, HD, HD), lambda b, h, t: (b, h, smap(t), 0, 0)), lat],
        out_specs=[one, one, one, one], out_shape=[shp, shp, shp, shp],
        scratch_shapes=[pltpu.VMEM((HD, HD), F32)],
        compiler_params=_cp(3))(Q, K, V, G, St, dO)


def _gnorm(O0, O1, F, gains, B, ntl):
    nt = ntl + 1

    def body(o0_ref, o1_ref, f_ref, gn_ref, m_ref):
        for h in range(NH):
            o = o0_ref[0, h] + o1_ref[0, h]
            r = lax.rsqrt(jnp.mean(o * o, axis=-1, keepdims=True) + NORM_EPS)
            gn = gn_ref[0:1, :] if h < 4 else gn_ref[1:2, :]
            gt = f_ref[:, h * HD:(h + 1) * HD]
            m_ref[:, h * HD:(h + 1) * HD] = (o * r * gn * (gt * _sigmoid(gt))).astype(BF16)

    ospec = pl.BlockSpec((1, NH, TM, HD), lambda b, i: (b, 0, i + 1, 0))
    return pl.pallas_call(
        body, name="gated_norm", grid=(B, ntl),
        in_specs=[ospec, ospec, pl.BlockSpec((TM, D), lambda b, i: (b * nt + 1 + i, 0)),
                  pl.BlockSpec((2, HD), lambda b, i: (0, 0))],
        out_specs=pl.BlockSpec((TM, D), _rowmap(ntl, 0)),
        out_shape=jax.ShapeDtypeStruct((B * ntl * TM, D), BF16), compiler_params=_cp(2))(O0, O1, F, gains)


def _gnorm_bwd(dM, O0, O1, F, gains, B, ntl):
    nt = ntl + 1

    def body(dm_ref, o0_ref, o1_ref, f_ref, gn_ref, do_ref, dgt_ref, dgn_ref):
        b_, i = pl.program_id(0), pl.program_id(1)

        @pl.when((b_ == 0) & (i == 0))
        def _():
            dgn_ref[...] = jnp.zeros_like(dgn_ref)

        for h in range(NH):
            o = o0_ref[0, h] + o1_ref[0, h]
            r = lax.rsqrt(jnp.mean(o * o, axis=-1, keepdims=True) + NORM_EPS)
            y = o * r
            gn = gn_ref[0:1, :] if h < 4 else gn_ref[1:2, :]
            gt = f_ref[:, h * HD:(h + 1) * HD]
            s = _sigmoid(gt)
            dm = dm_ref[:, h * HD:(h + 1) * HD]
            don = dm * (gt * s)
            dgt_ref[:, h * HD:(h + 1) * HD] = dm * (y * gn) * (s * (1.0 + gt * (1.0 - s)))
            row = 0 if h < 4 else 1
            dgn_ref[row:row + 1, :] += _rsum(don * y)
            dy = don * gn
            do_ref[0, h] = r * (dy - y * jnp.mean(dy * y, axis=-1, keepdims=True))

    ospec = pl.BlockSpec((1, NH, TM, HD), lambda b, i: (b, 0, i + 1, 0))
    return pl.pallas_call(
        body, name="gated_norm_bwd", grid=(B, ntl),
        in_specs=[pl.BlockSpec((TM, D), _rowmap(ntl, 0)), ospec, ospec,
                  pl.BlockSpec((TM, D), lambda b, i: (b * nt + 1 + i, 0)), pl.BlockSpec((2, HD), lambda b, i: (0, 0))],
        out_specs=[pl.BlockSpec((1, NH, TM, HD), lambda b, i: (b, 0, i, 0)), pl.BlockSpec((TM, D), _rowmap(ntl, 0)),
                   pl.BlockSpec((2, HD), lambda b, i: (0, 0))],
        out_shape=[jax.ShapeDtypeStruct((B, NH, ntl * TM, HD), F32), jax.ShapeDtypeStruct((B * ntl * TM, D), F32),
                   jax.ShapeDtypeStruct((2, HD), F32)],
        compiler_params=_cp(2))(dM, O0, O1, F, gains)


def _sincos_2d(rows, width, dim):
    r = jnp.repeat(jnp.arange(rows), width)
    col = jnp.tile(jnp.arange(width), rows)
    quarter = dim // 4
    omega = 1.0 / 10000.0 ** (jnp.arange(quarter, dtype=F32) / quarter)

    def emb(p):
        a = p.astype(F32)[:, None] * omega[None, :]
        return jnp.concatenate([jnp.sin(a), jnp.cos(a)], axis=-1)

    return jnp.concatenate([emb(r), emb(col)], axis=-1)


def _pad_heads(w):
    k = w.shape[0]
    return jnp.pad(w.reshape(k, 4, GLA_DK), ((0, 0), (0, 0), (0, HD - GLA_DK))).reshape(k, 4 * HD)


def _unpad_heads(w):
    k = w.shape[0]
    return w.reshape(k, 4, HD)[:, :, :GLA_DK].reshape(k, 4 * GLA_DK)


MIX_N = 1032
MIX_NP = 1152
_SEGS = ([(64 * h, 64, GQ + HD * h) for h in range(4)] + [(256 + 64 * h, 64, GK + HD * h) for h in range(4)]
         + [(512, 512, GV), (1024, 512, GG), (1536, 32, LR), (1568, 512, RQ), (2080, 512, RFF), (2592, 512, RFB),
            (3104, 512, RI), (3616, 512, RG)])


def _mix_in_to_padded(ps):
    k = ps.shape[1]
    parts, pos = [], 0
    for g0, ln, s0 in sorted(_SEGS, key=lambda s: s[2]):
        if s0 > pos:
            parts.append(jnp.zeros((k, s0 - pos), ps.dtype))
        for j in range(4):
            lo, hi = max(g0, j * MIX_N), min(g0 + ln, (j + 1) * MIX_N)
            if lo < hi:
                parts.append(ps[j][:, lo - j * MIX_N:hi - j * MIX_N])
        pos = s0 + ln
    parts.append(jnp.zeros((k, MIXP - pos), ps.dtype))
    return jnp.concatenate(parts, axis=1)


def _mix_in_from_padded(g):
    k = g.shape[0]
    shards = []
    for j in range(4):
        parts = []
        for g0, ln, s0 in sorted(_SEGS):
            lo, hi = max(g0, j * MIX_N), min(g0 + ln, (j + 1) * MIX_N)
            if lo < hi:
                parts.append(g[:, s0 + lo - g0:s0 + hi - g0])
        parts.append(jnp.zeros((k, MIX_NP - MIX_N), g.dtype))
        shards.append(jnp.concatenate(parts, axis=1))
    return jnp.stack(shards)


def _local_step(x, ctx, tgt, mvec, w1i, w1o, wmp, wmo, w2i, w2o, ln_gain, ln_bias, a2f, a2b, abf, abb, lb, gng, gnh):
    B, T, _ = x.shape
    assert ctx.shape[1] == TM and T % TM == 0
    ntl = T // TM
    nt = ntl + 1
    C, L, CL = (nt, 0), (ntl, 0), (nt, 1)
    pos = _sincos_2d(T // 64, 64, D)
    gbs = [jnp.stack([ln_gain[i], ln_bias[i]]) for i in range(3)]
    a2p = jnp.zeros((2, HD, 4 * HD), F32)
    a2p = a2p.at[0, 0:16].set(_pad_heads(a2f)).at[1, 16:32].set(_pad_heads(a2b)).astype(BF16)
    biasp = jnp.concatenate([_pad_heads(abf.reshape(1, -1)), _pad_heads(abb.reshape(1, -1))], axis=0)
    gains = jnp.concatenate([gng.reshape(1, HD), gnh.reshape(1, HD)], axis=0)

    X0 = _embed(x, ctx, pos)
    h0 = _lnmod(X0, C, mvec, True, 0, B, nt, "lnmod0")
    u0, a0 = _ffn_in(h0, C, w1i, B, nt, "ffn1_in")
    y0 = _mm_nn(a0, C, w1o, B, nt, F32, "ffn1_out")
    X1 = _resid(X0, C, y0, mvec, True, 0, 0.5, gbs[0], B, nt, "resid0")
    h1 = _lnmod(X1, C, mvec, True, 1, B, nt, "lnmod1")
    Fm = _mm_nn(h1, C, wmp, B, nt, F32, "mix_in")
    Q, K, V, G = _features(Fm, a2p, biasp, lb, B)
    O0, S0 = _scan_fwd(Q, K, V, G, False, B)
    O1, S1 = _scan_fwd(Q, K, V, G, True, B)
    merged = _gnorm(O0, O1, Fm, gains, B, ntl)
    y1 = _mm_nn(merged, L, wmo, B, ntl, F32, "mix_out")
    X2 = _resid(X1, CL, y1, mvec, False, 1, 1.0, gbs[1], B, ntl, "resid1")
    h2 = _lnmod(X2, L, mvec, False, 2, B, ntl, "lnmod2")
    u2, a2 = _ffn_in(h2, L, w2i, B, ntl, "ffn2_in")
    y2 = _mm_nn(a2, L, w2o, B, ntl, F32, "ffn2_out")
    X3 = _resid(X2, L, y2, mvec, False, 2, 0.5, gbs[2], B, ntl, "resid2")
    dX3, lsum = _loss_grad(X3, tgt, B, ntl)
    loss = (0.5 / D) * jnp.sum(lsum)

    dx2r, dy2, dgate2, dgb2 = _resid_bwd(dX3, X2, L, y2, mvec, False, 2, 0.5, gbs[2], B, ntl, "resid2_bwd")
    da2 = _mm_nt(dy2, w2o, "ffn2_out_dx")
    g_w2o = _mm_tn(a2, dy2, "ffn2_out_dw")
    du2 = _swiglu_bwd(u2, da2, "swiglu2_bwd")
    dh2 = _mm_nt(du2, w2i, "ffn2_in_dx")
    g_w2i = _mm_tn(h2, du2, "ffn2_in_dw", shards=4)
    dX2, dss2 = _lnmod_bwd(dh2, X2, L, mvec, False, 2, B, ntl, "lnmod2_bwd", dres=dx2r, ldres=L)
    dx1r, dy1, dgate1, dgb1 = _resid_bwd(dX2, X1, CL, y1, mvec, False, 1, 1.0, gbs[1], B, ntl, "resid1_bwd")
    dmerged = _mm_nt(dy1, wmo, "mix_out_dx")
    g_wmo = _mm_tn(merged, dy1, "mix_out_dw")
    dO, dgates, dgains = _gnorm_bwd(dmerged, O0, O1, Fm, gains, B, ntl)
    dQ0, dK0, dV0, dG0 = _scan_bwd(Q, K, V, G, S0, dO, False, B)
    dQ1, dK1, dV1, dG1 = _scan_bwd(Q, K, V, G, S1, dO, True, B)
    dF, da2p, dbiasp, dlb = _features_bwd(Fm, a2p, biasp, lb, dQ0, dQ1, dK0, dK1, dV0, dV1, dG0, dG1, dgates, B)
    dh1 = _mm_nt(dF, wmp, "mix_in_dx")
    g_wmp = _mm_tn(h1, dF, "mix_in_dw")
    dX1, dss1 = _lnmod_bwd(dh1, X1, C, mvec, True, 1, B, nt, "lnmod1_bwd", dres=dx1r, ldres=L, dres_zero_ctx=True)
    dx0r, dy0, dgate0, dgb0 = _resid_bwd(dX1, X0, C, y0, mvec, True, 0, 0.5, gbs[0], B, nt, "resid0_bwd")
    da0 = _mm_nt(dy0, w1o, "ffn1_out_dx")
    g_w1o = _mm_tn(a0, dy0, "ffn1_out_dw")
    du0 = _swiglu_bwd(u0, da0, "swiglu1_bwd")
    dh0 = _mm_nt(du0, w1i, "ffn1_in_dx")
    g_w1i = _mm_tn(h0, du0, "ffn1_in_dw", shards=4)
    grad_x, dss0 = _lnmod_bwd(dh0, X0, C, mvec, True, 0, B, nt, "lnmod0_bwd", dres=dx0r, ldres=C, to_x=x.shape)

    zero_ctx = lambda a: a.at[:, 0].set(0.0)
    dm = jnp.concatenate([dss0, dgate0, dss1, zero_ctx(dgate1), zero_ctx(dss2), zero_ctx(dgate2)], axis=2)
    small = dict(
        ln_gain=jnp.stack([dgb0[0], dgb1[0], dgb2[0]]), ln_bias=jnp.stack([dgb0[1], dgb1[1], dgb2[1]]),
        a2f=_unpad_heads(da2p[0, 0:16]), a2b=_unpad_heads(da2p[1, 16:32]),
        abf=_unpad_heads(dbiasp[0:1]), abb=_unpad_heads(dbiasp[1:2]), lb=dlb, gng=dgains[0], gnh=dgains[1])
    big = dict(w1i=g_w1i, w1o=g_w1o, wmp=g_wmp, wmo=g_wmo, w2i=g_w2i, w2o=g_w2o)
    return loss, grad_x, big, dm, small


def _small_allgather(xs, name):
    r, n = xs.shape

    def body(x_ref, out_ref, send_sems, recv_sems, local_sem):
        x, y, c = lax.axis_index("x"), lax.axis_index("y"), lax.axis_index("c")
        me, sibling = (x, y, c), (x, y, 1 - c)
        chips = [(1 - x, y), (x, 1 - y), (1 - x, 1 - y)]

        def rows(px, py, pc):
            return out_ref.at[pl.ds((4 * px + 2 * py + pc) * r, r), :]

        def copy(k, block, to, src=None):
            return pltpu.make_async_remote_copy(
                src_ref=rows(*block) if src is None else src, dst_ref=rows(*block),
                send_sem=send_sems.at[k], recv_sem=recv_sems.at[k], device_id=to, device_id_type=MESH)

        mine = pltpu.make_async_copy(x_ref, rows(*me), local_sem)
        mine.start()
        first = [copy(0, me, sibling, src=x_ref)]
        first += [copy(1 + j, me, (*chip, c), src=x_ref) for j, chip in enumerate(chips)]
        for cp in first:
            cp.start()
        passed = [copy(4 + j, (*chip, c), sibling) for j, chip in enumerate(chips)]
        for j, chip in enumerate(chips):
            copy(1 + j, (*chip, c), me).wait_recv()
            passed[j].start()
        copy(0, sibling, me).wait_recv()
        for j, chip in enumerate(chips):
            copy(4 + j, (*chip, 1 - c), me).wait_recv()
        for cp in first + passed:
            cp.wait_send()
        mine.wait()

    out = pl.pallas_call(
        body, name=name,
        out_shape=jax.ShapeDtypeStruct((8 * r, n), xs.dtype),
        in_specs=[pl.BlockSpec(memory_space=pltpu.VMEM)],
        out_specs=pl.BlockSpec(memory_space=pltpu.VMEM),
        scratch_shapes=[pltpu.SemaphoreType.DMA((7,)), pltpu.SemaphoreType.DMA((7,)), pltpu.SemaphoreType.DMA],
        compiler_params=pltpu.CompilerParams(vmem_limit_bytes=VMEM_LIMIT))(xs)
    return out.reshape(8, r, n)


def _gather_flat(v, name):
    n = v.shape[0]
    npad = -(-n // 1024) * 1024
    g = _small_allgather(jnp.pad(v, (0, npad - n)).reshape(8, npad // 8), name)
    return g.reshape(8, npad)[:, :n]


def _big_allgather(blks, name):
    n = len(blks)

    def body(*refs):
        xs, outs = refs[:n], refs[n:2 * n]
        send_sems, recv_sems, local_sems = refs[2 * n:]
        x, y, c = lax.axis_index("x"), lax.axis_index("y"), lax.axis_index("c")
        me, sibling = (x, y, c), (x, y, 1 - c)
        chips = [(1 - x, y), (x, 1 - y), (1 - x, 1 - y)]

        def copy(w, k, block, to, own=False):
            px, py, pc = block
            slot = outs[w].at[4 * px + 2 * py + pc]
            return pltpu.make_async_remote_copy(
                src_ref=xs[w] if own else slot, dst_ref=slot, send_sem=send_sems.at[7 * w + k],
                recv_sem=recv_sems.at[7 * w + k], device_id=to, device_id_type=MESH)

        mine = [pltpu.make_async_copy(xs[w], outs[w].at[4 * x + 2 * y + c], local_sems.at[w]) for w in range(n)]
        for cp in mine:
            cp.start()
        first = []
        for w in range(n):
            first.append(copy(w, 0, me, sibling, own=True))
            first += [copy(w, 1 + j, me, (*chip, c), own=True) for j, chip in enumerate(chips)]
        for cp in first:
            cp.start()
        passed = []
        for w in range(n):
            for j, chip in enumerate(chips):
                copy(w, 1 + j, (*chip, c), me).wait_recv()
                cp = copy(w, 4 + j, (*chip, c), sibling)
                cp.start()
                passed.append(cp)
        for w in range(n):
            copy(w, 0, sibling, me).wait_recv()
            for j, chip in enumerate(chips):
                copy(w, 4 + j, (*chip, 1 - c), me).wait_recv()
        for cp in first + passed:
            cp.wait_send()
        for cp in mine:
            cp.wait()

    any_spec = pl.BlockSpec(memory_space=pl.ANY)
    return pl.pallas_call(
        body, name=name,
        out_shape=[jax.ShapeDtypeStruct((8,) + b.shape, b.dtype) for b in blks],
        in_specs=[any_spec] * n, out_specs=[any_spec] * n,
        scratch_shapes=[pltpu.SemaphoreType.DMA((7 * n,)), pltpu.SemaphoreType.DMA((7 * n,)),
                        pltpu.SemaphoreType.DMA((n,))],
    )(*blks)


def _rs_pair_exchange(g8s, name):
    n = len(g8s)

    def body(*refs):
        gs, rs = refs[:n], refs[n:2 * n]
        send_sems, recv_sems = refs[2 * n:]
        x, y, c = lax.axis_index("x"), lax.axis_index("y"), lax.axis_index("c")
        cps = [pltpu.make_async_remote_copy(
            src_ref=gs[w].at[2 * j + 1 - c], dst_ref=rs[w].at[j], send_sem=send_sems.at[4 * w + j],
            recv_sem=recv_sems.at[4 * w + j], device_id=(x, y, 1 - c), device_id_type=MESH)
            for w in range(n) for j in range(4)]
        for cp in cps:
            cp.start()
        for cp in cps:
            cp.wait_recv()
        for cp in cps:
            cp.wait_send()

    any_spec = pl.BlockSpec(memory_space=pl.ANY)
    return pl.pallas_call(
        body, name=name,
        out_shape=[jax.ShapeDtypeStruct((4,) + g.shape[1:], g.dtype) for g in g8s],
        in_specs=[any_spec] * n, out_specs=[any_spec] * n,
        scratch_shapes=[pltpu.SemaphoreType.DMA((4 * n,)), pltpu.SemaphoreType.DMA((4 * n,))])(*g8s)


def _rs_chip_exchange(h4s, name):
    n = len(h4s)

    def body(*refs):
        hs, rs = refs[:n], refs[n:2 * n]
        send_sems, recv_sems = refs[2 * n:]
        x, y, c = lax.axis_index("x"), lax.axis_index("y"), lax.axis_index("c")
        chips = [(1 - x, y), (x, 1 - y), (1 - x, 1 - y)]
        cps = [pltpu.make_async_remote_copy(
            src_ref=hs[w].at[2 * px + py], dst_ref=rs[w].at[k], send_sem=send_sems.at[3 * w + k],
            recv_sem=recv_sems.at[3 * w + k], device_id=(px, py, c), device_id_type=MESH)
            for w in range(n) for k, (px, py) in enumerate(chips)]
        for cp in cps:
            cp.start()
        for cp in cps:
            cp.wait_recv()
        for cp in cps:
            cp.wait_send()

    any_spec = pl.BlockSpec(memory_space=pl.ANY)
    return pl.pallas_call(
        body, name=name,
        out_shape=[jax.ShapeDtypeStruct((3,) + h.shape[1:], h.dtype) for h in h4s],
        in_specs=[any_spec] * n, out_specs=[any_spec] * n,
        scratch_shapes=[pltpu.SemaphoreType.DMA((3 * n,)), pltpu.SemaphoreType.DMA((3 * n,))])(*h4s)


def _rs_pair_share(fins, name):
    n = len(fins)

    def body(*refs):
        fs, outs = refs[:n], refs[n:2 * n]
        send_sems, recv_sems, local_sems = refs[2 * n:]
        x, y, c = lax.axis_index("x"), lax.axis_index("y"), lax.axis_index("c")
        mine = [pltpu.make_async_copy(fs[w], outs[w].at[c], local_sems.at[w]) for w in range(n)]
        for cp in mine:
            cp.start()

        def copy(w, slot):
            return pltpu.make_async_remote_copy(
                src_ref=fs[w], dst_ref=outs[w].at[slot], send_sem=send_sems.at[w], recv_sem=recv_sems.at[w],
                device_id=(x, y, 1 - c), device_id_type=MESH)

        sends = [copy(w, c) for w in range(n)]
        for cp in sends:
            cp.start()
        for w in range(n):
            copy(w, 1 - c).wait_recv()
        for cp in sends:
            cp.wait_send()
        for cp in mine:
            cp.wait()

    any_spec = pl.BlockSpec(memory_space=pl.ANY)
    return pl.pallas_call(
        body, name=name,
        out_shape=[jax.ShapeDtypeStruct((2,) + f.shape, f.dtype) for f in fins],
        in_specs=[any_spec] * n, out_specs=[any_spec] * n,
        scratch_shapes=[pltpu.SemaphoreType.DMA((n,)), pltpu.SemaphoreType.DMA((n,)), pltpu.SemaphoreType.DMA((n,))])(*fins)


def _row_block(R, cap=2048):
    best = 16
    for rb in range(16, cap + 1, 16):
        if R % rb == 0:
            best = rb
    return best


def _rs_add_pair(g8, r4, c, name):
    R = g8.shape[1]
    rb = _row_block(R)

    def body(c_ref, g_ref, r_ref, o_ref):
        o_ref[...] = (g_ref[...] + r_ref[...]).astype(BF16)

    spec = pl.BlockSpec((1, rb, 128), lambda j, i, c_ref: (j, i, 0))
    return pl.pallas_call(
        body, name=name,
        grid_spec=pltpu.PrefetchScalarGridSpec(
            num_scalar_prefetch=1, grid=(4, R // rb),
            in_specs=[pl.BlockSpec((1, rb, 128), lambda j, i, c_ref: (2 * j + c_ref[0], i, 0)), spec],
            out_specs=spec),
        out_shape=jax.ShapeDtypeStruct((4, R, 128), BF16), compiler_params=_cp(2))(c, g8, r4)


def _rs_add_chips(g8, r4, r3, cj, name):
    R = g8.shape[1]
    rb = _row_block(R)

    def body(cj_ref, g_ref, p_ref, r_ref, o_ref):
        own = g_ref[0] + p_ref[0]
        o_ref[...] = ((own + r_ref[0].astype(F32)) + r_ref[1].astype(F32)) + r_ref[2].astype(F32)

    return pl.pallas_call(
        body, name=name,
        grid_spec=pltpu.PrefetchScalarGridSpec(
            num_scalar_prefetch=1, grid=(R // rb,),
            in_specs=[pl.BlockSpec((1, rb, 128), lambda i, cj_ref: (2 * cj_ref[1] + cj_ref[0], i, 0)),
                      pl.BlockSpec((1, rb, 128), lambda i, cj_ref: (cj_ref[1], i, 0)),
                      pl.BlockSpec((3, rb, 128), lambda i, cj_ref: (0, i, 0))],
            out_specs=pl.BlockSpec((rb, 128), lambda i, cj_ref: (i, 0))),
        out_shape=jax.ShapeDtypeStruct((R, 128), F32), compiler_params=_cp(1))(cj, g8, r4, r3)


def _sum8(g):
    n = g.shape[1]

    def body(g_ref, o_ref):
        acc = g_ref[0:1, :]
        for k in range(1, 8):
            acc = acc + g_ref[k:k + 1, :]
        o_ref[...] = acc

    return pl.pallas_call(body, name="sum_devices", out_shape=jax.ShapeDtypeStruct((1, n), F32),
                          compiler_params=pltpu.CompilerParams(vmem_limit_bytes=VMEM_LIMIT))(g)


ADA_ROWS = 64


def _ada_fwd(cs, w, b):
    n = w.shape[1]

    def body(c_ref, w_ref, b_ref, o_ref):
        cv = c_ref[...]
        s = (cv * _sigmoid(cv)).astype(BF16)
        o_ref[...] = _dot(s, w_ref[...].astype(BF16)) + b_ref[...]

    return pl.pallas_call(body, name="ada_fwd", out_shape=jax.ShapeDtypeStruct((ADA_ROWS, n), F32),
                          compiler_params=pltpu.CompilerParams(vmem_limit_bytes=VMEM_LIMIT))(cs, w, b)


def _ada_bwd(cs, w, dm):
    n = w.shape[1]

    def body(c_ref, w_ref, dm_ref, gw_ref, dc_ref):
        cv = c_ref[...]
        s = (cv * _sigmoid(cv)).astype(BF16)
        gw_ref[...] = _dot_tn(s, dm_ref[...].astype(BF16))
        dc_ref[...] = _dot_nt(dm_ref[32:40, :].astype(BF16), w_ref[...].astype(BF16))

    return pl.pallas_call(
        body, name="ada_bwd",
        out_shape=[jax.ShapeDtypeStruct((D, n), F32), jax.ShapeDtypeStruct((8, D), F32)],
        compiler_params=pltpu.CompilerParams(vmem_limit_bytes=VMEM_LIMIT))(cs, w, dm)


def _adamw(w, g, m, v, name):
    r, c = w.shape
    rb = r
    if r % 8 == 0 and r * c * 4 > (1 << 20):
        rb = 8
        for cand in range(8, r, 8):
            if r % cand == 0 and cand * c * 4 <= (1 << 20):
                rb = cand

    def body(w_ref, g_ref, m_ref, v_ref, d_ref, nm_ref, nv_ref):
        gv = g_ref[...]
        mn = ADAM_B1 * m_ref[...] + (1.0 - ADAM_B1) * gv
        vn = ADAM_B2 * v_ref[...] + (1.0 - ADAM_B2) * (gv * gv)
        m_hat = mn / (1.0 - ADAM_B1 ** ADAM_STEP)
        v_hat = vn / (1.0 - ADAM_B2 ** ADAM_STEP)
        d_ref[...] = -ADAM_LR * (m_hat / (jnp.sqrt(v_hat) + ADAM_EPS) + ADAM_WD * w_ref[...])
        nm_ref[...] = mn
        nv_ref[...] = vn

    spec = pl.BlockSpec((rb, c), lambda i: (i, 0))
    shp = jax.ShapeDtypeStruct((r, c), F32)
    return pl.pallas_call(body, name=name, grid=(r // rb,), in_specs=[spec] * 4, out_specs=[spec] * 3,
                          out_shape=[shp] * 3, compiler_params=_cp(1))(w, g, m, v)


BIG = ("ffn1_w_in", "ffn1_w_out", "w_mix_in", "w_mix_out", "ffn2_w_in", "ffn2_w_out")


def _half_rows(w, c):
    half = w.shape[0] // 2
    return lax.dynamic_slice_in_dim(w, c * half, half, axis=0).reshape(-1, 128)


def _lower_bounds(logits):
    return jnp.cumsum(jax.nn.softmax(logits.astype(F32), axis=1), axis=1)[:, 0]


def kernel(x, c, ctx, c_ctx, w_ada, b_ada, ln_gain, ln_bias, ffn1_w_in, ffn1_w_out, w_mix_in, gla_a2_fwd, gla_a2_bwd, gla_a_bias_fwd, gla_a_bias_bwd, hgrn_lb_logits, gla_norm_gain, hgrn_norm_gain, w_mix_out, ffn2_w_in, ffn2_w_out, loss_target, m_c_ctx, m_w_ada, m_b_ada, m_ln_gain, m_ln_bias, m_ffn1_w_in, m_ffn1_w_out, m_w_mix_in, m_gla_a2_fwd, m_gla_a2_bwd, m_gla_a_bias_fwd, m_gla_a_bias_bwd, m_hgrn_lb_logits, m_gla_norm_gain, m_hgrn_norm_gain, m_w_mix_out, m_ffn2_w_in, m_ffn2_w_out, v_c_ctx, v_w_ada, v_b_ada, v_ln_gain, v_ln_bias, v_ffn1_w_in, v_ffn1_w_out, v_w_mix_in, v_gla_a2_fwd, v_gla_a2_bwd, v_gla_a_bias_fwd, v_gla_a_bias_bwd, v_hgrn_lb_logits, v_gla_norm_gain, v_hgrn_norm_gain, v_w_mix_out, v_ffn2_w_in, v_ffn2_w_out):
    xi, yi, ci = lax.axis_index("x"), lax.axis_index("y"), lax.axis_index("c")
    chip = 2 * xi + yi
    dev = 2 * chip + ci
    B = x.shape[0]
    weights = dict(ffn1_w_in=ffn1_w_in[0], ffn1_w_out=ffn1_w_out[0], w_mix_in=w_mix_in[0], w_mix_out=w_mix_out[0],
                   ffn2_w_in=ffn2_w_in[0], ffn2_w_out=ffn2_w_out[0])

    mine = jnp.concatenate([c.reshape(-1), ln_gain.reshape(-1), ln_bias.reshape(-1), gla_a2_fwd.reshape(-1),
                            gla_a2_bwd.reshape(-1), hgrn_lb_logits.reshape(-1)])
    g1 = _gather_flat(mine, "gather_cond")
    nc = B * D
    c_all = g1[:, :nc].reshape(8 * B, D)
    per_chip = g1[0::2, nc:]
    o = 0

    def take(shape, axis):
        nonlocal o
        n = int(np.prod(shape))
        parts = per_chip[:, o:o + n].reshape((4,) + shape)
        o += n
        return jnp.concatenate([parts[j] for j in range(4)], axis=axis)

    ln_gain_f = take((3, 256), 1)
    ln_bias_f = take((3, 256), 1)
    a2f_f = take((16, 64), 1)
    a2b_f = take((16, 64), 1)
    lbl_f = take((2, 2, 128), 2)
    lb, lb_vjp = jax.vjp(_lower_bounds, lbl_f)

    shards = dict(weights, w_mix_in=jnp.pad(weights["w_mix_in"], ((0, 0), (0, MIX_NP - MIX_N))))
    blks = [_half_rows(shards[k], ci).astype(BF16) for k in BIG]
    full = dict(zip(BIG, _big_allgather(blks, "weight_allgather")))
    w1i, w2i = (full[k].reshape((4,) + shards[k].shape) for k in ("ffn1_w_in", "ffn2_w_in"))
    w1o, wmo, w2o = (full[k].reshape(-1, D) for k in ("ffn1_w_out", "w_mix_out", "ffn2_w_out"))
    wmp = _mix_in_to_padded(full["w_mix_in"].reshape(4, D, MIX_NP))

    cs = jnp.concatenate([c_all, c_ctx.reshape(1, D), jnp.zeros((ADA_ROWS - 8 * B - 1, D), F32)], axis=0)
    ncol = w_ada.shape[2]
    b_cols = lax.dynamic_slice_in_dim(b_ada, chip * ncol, ncol, axis=1)
    m_cols = _ada_fwd(cs, w_ada[0], b_cols)
    g2 = _small_allgather(m_cols, "gather_mod")[0::2]
    m_all = jnp.concatenate([g2[j] for j in range(4)], axis=1)
    m_lat = lax.dynamic_slice_in_dim(m_all, dev * B, B, axis=0).reshape(B, 1, N_MOD, D)
    m_ctx = jnp.broadcast_to(m_all[8 * B].reshape(1, 1, N_MOD, D), (B, 1, N_MOD, D))
    mvec = jnp.concatenate([m_ctx, m_lat], axis=1)

    loss_l, grad_x, big, dm, small = _local_step(
        x, ctx, loss_target, mvec, w1i, w1o, wmp, wmo, w2i, w2o, ln_gain_f, ln_bias_f, a2f_f, a2b_f,
        gla_a_bias_fwd, gla_a_bias_bwd, lb, gla_norm_gain, hgrn_norm_gain)
    loss = lax.psum(loss_l, ("x", "y", "c"))

    dm_lat = dm[:, 1].reshape(B, N_MOD * D)
    dm_ctx = jnp.sum(dm[:, 0], axis=0).reshape(N_MOD * D)
    keys = ("ln_gain", "ln_bias", "a2f", "a2b", "abf", "abb", "lb", "gng", "gnh")
    flat = jnp.concatenate([dm_lat.reshape(-1), dm_ctx] + [small[k].reshape(-1) for k in keys])
    g3 = _gather_flat(flat, "gather_small_grads")
    nlat = B * N_MOD * D
    dm_all = g3[:, :nlat].reshape(8 * B, N_MOD * D)
    tot = _sum8(g3[:, nlat:])[0]
    dmc_tot = tot[:N_MOD * D]
    o = N_MOD * D
    sg = {}
    for k in keys:
        n = int(np.prod(small[k].shape))
        sg[k] = tot[o:o + n].reshape(small[k].shape)
        o += n
    dm_rows = jnp.concatenate([dm_all, dmc_tot.reshape(1, -1), jnp.zeros((ADA_ROWS - 8 * B - 1, N_MOD * D), F32)], axis=0)
    g_b_ada = (jnp.sum(dm_all, axis=0) + dmc_tot).reshape(1, N_MOD * D)
    g_w_ada, dcc = _ada_bwd(cs, w_ada[0], lax.dynamic_slice_in_dim(dm_rows, chip * ncol, ncol, axis=1))
    g4 = _gather_flat(dcc[0], "gather_cctx")
    dsilu = ((g4[0] + g4[2]) + g4[4]) + g4[6]
    sc = _sigmoid(c_ctx)
    g_c_ctx = dsilu * (sc * (1.0 + c_ctx * (1.0 - sc)))
    (g_lbl,) = lb_vjp(sg["lb"])

    def cols(a, n, axis):
        return lax.dynamic_slice_in_dim(a, chip * n, n, axis=axis)

    big_full = dict(ffn1_w_in=big["w1i"], ffn1_w_out=big["w1o"], w_mix_in=_mix_in_from_padded(big["wmp"]),
                    w_mix_out=big["wmo"], ffn2_w_in=big["w2i"], ffn2_w_out=big["w2o"])
    g8s = [big_full[k].reshape(8, -1, 128) for k in BIG]
    cvec = ci.reshape(1).astype(jnp.int32)
    cjvec = jnp.stack([ci, chip]).astype(jnp.int32)
    r4s = _rs_pair_exchange(g8s, "grad_pair_exchange")
    h4s = [_rs_add_pair(g, r, cvec, "grad_pair_add_" + k) for k, g, r in zip(BIG, g8s, r4s)]
    r3s = _rs_chip_exchange(h4s, "grad_chip_exchange")
    fins = [_rs_add_chips(g, r4, r3, cjvec, "grad_chip_add_" + k) for k, g, r4, r3 in zip(BIG, g8s, r4s, r3s)]
    gsh = {k: o2.reshape(shards[k].shape)[:, :weights[k].shape[1]]
           for k, o2 in zip(BIG, _rs_pair_share(fins, "grad_pair_share"))}

    grads = dict(
        c_ctx=g_c_ctx, w_ada=g_w_ada[None], b_ada=g_b_ada, ln_gain=cols(sg["ln_gain"], 256, 1)[None],
        ln_bias=cols(sg["ln_bias"], 256, 1)[None], ffn1_w_in=gsh["ffn1_w_in"][None], ffn1_w_out=gsh["ffn1_w_out"][None],
        w_mix_in=gsh["w_mix_in"][None], gla_a2_fwd=cols(sg["a2f"], 64, 1)[None], gla_a2_bwd=cols(sg["a2b"], 64, 1)[None],
        gla_a_bias_fwd=sg["abf"], gla_a_bias_bwd=sg["abb"], hgrn_lb_logits=cols(g_lbl, 128, 2),
        gla_norm_gain=sg["gng"].reshape(1, HD), hgrn_norm_gain=sg["gnh"].reshape(1, HD),
        w_mix_out=gsh["w_mix_out"][None], ffn2_w_in=gsh["ffn2_w_in"][None], ffn2_w_out=gsh["ffn2_w_out"][None])
    params = dict(
        c_ctx=(c_ctx, m_c_ctx, v_c_ctx), w_ada=(w_ada, m_w_ada, v_w_ada), b_ada=(b_ada, m_b_ada, v_b_ada),
        ln_gain=(ln_gain, m_ln_gain, v_ln_gain), ln_bias=(ln_bias, m_ln_bias, v_ln_bias),
        ffn1_w_in=(ffn1_w_in, m_ffn1_w_in, v_ffn1_w_in), ffn1_w_out=(ffn1_w_out, m_ffn1_w_out, v_ffn1_w_out),
        w_mix_in=(w_mix_in, m_w_mix_in, v_w_mix_in), gla_a2_fwd=(gla_a2_fwd, m_gla_a2_fwd, v_gla_a2_fwd),
        gla_a2_bwd=(gla_a2_bwd, m_gla_a2_bwd, v_gla_a2_bwd),
        gla_a_bias_fwd=(gla_a_bias_fwd, m_gla_a_bias_fwd, v_gla_a_bias_fwd),
        gla_a_bias_bwd=(gla_a_bias_bwd, m_gla_a_bias_bwd, v_gla_a_bias_bwd),
        hgrn_lb_logits=(hgrn_lb_logits, m_hgrn_lb_logits, v_hgrn_lb_logits),
        gla_norm_gain=(gla_norm_gain, m_gla_norm_gain, v_gla_norm_gain),
        hgrn_norm_gain=(hgrn_norm_gain, m_hgrn_norm_gain, v_hgrn_norm_gain),
        w_mix_out=(w_mix_out, m_w_mix_out, v_w_mix_out), ffn2_w_in=(ffn2_w_in, m_ffn2_w_in, v_ffn2_w_in),
        ffn2_w_out=(ffn2_w_out, m_ffn2_w_out, v_ffn2_w_out))
    order = list(params.keys())
    big_names = ("w_ada",) + BIG
    upd = {}
    for k in big_names:
        w_, m_, v_ = params[k]
        s2 = w_.shape[-2:]
        d_, nm_, nv_ = _adamw(w_.reshape(s2), grads[k].reshape(s2), m_.reshape(s2), v_.reshape(s2), "adamw_" + k)
        upd[k] = (d_.reshape(w_.shape), nm_.reshape(w_.shape), nv_.reshape(w_.shape))
    small_names = [k for k in order if k not in big_names]
    sizes = [int(np.prod(params[k][0].shape)) for k in small_names]
    tot_n = sum(sizes)
    npad = -(-tot_n // 1024) * 1024

    def packed(get):
        flat_ = jnp.concatenate([get(k).reshape(-1) for k in small_names])
        return jnp.pad(flat_, (0, npad - tot_n)).reshape(8, npad // 8)

    d_s, nm_s, nv_s = _adamw(packed(lambda k: params[k][0]), packed(lambda k: grads[k]),
                             packed(lambda k: params[k][1]), packed(lambda k: params[k][2]), "adamw_small")
    o = 0
    for k, n in zip(small_names, sizes):
        shp = params[k][0].shape
        upd[k] = tuple(a.reshape(-1)[o:o + n].reshape(shp) for a in (d_s, nm_s, nv_s))
        o += n

    return (loss, grad_x, *[grads[k].reshape(params[k][0].shape) for k in order], *[upd[k][0] for k in order],
            *[upd[k][1] for k in order], *[upd[k][2] for k in order])
```

```python
import functools

import numpy as np
import jax
import jax.numpy as jnp
from jax import lax
from jax.experimental import pallas as pl
from jax.experimental.pallas import tpu as pltpu

F32 = jnp.float32
BF16 = jnp.bfloat16
MESH = pl.DeviceIdType.MESH

D = 1024
DFF = 2816
TM = 256
CH = 32
NCB = TM // CH
SB = 128
CSB = SB // CH
NSB = TM // SB
HP = 4
HD = 128
NH = 8
LN_EPS = 1e-5
NORM_EPS = 1e-6
ALPHA = 2.0 ** 0.25
GATE_NORM = 16.0
GLA_DK = 64
N_MOD = 9
VMEM_LIMIT = 52 * 1024 * 1024

MIXP = 5120
GG, RG, GQ, GK, GV, RQ, RFF, RFB, RI, LR = 0, 512, 1024, 1536, 2048, 2560, 3072, 3584, 4096, 4608
IN_SPLITS = (256, 256, 512, 512, 16, 16, 512, 512, 512, 512, 512)

ADAM_LR, ADAM_B1, ADAM_B2, ADAM_EPS, ADAM_WD, ADAM_STEP = 0.001, 0.9, 0.999, 1e-08, 0.01, 10


def _cp(n_axes):
    return pltpu.CompilerParams(dimension_semantics=("arbitrary",) * n_axes, vmem_limit_bytes=VMEM_LIMIT)


def _rowmap(stride, off):
    return lambda b, i: (b * stride + off + i, 0)


def _mmap(comb):
    if comb:
        return lambda b, i: (b, jnp.minimum(i, 1), 0, 0)
    return lambda b, i: (b, 1, 0, 0)


def _ln(x):
    mu = jnp.mean(x, axis=-1, keepdims=True)
    xc = x - mu
    var = jnp.mean(xc * xc, axis=-1, keepdims=True)
    r = lax.rsqrt(var + LN_EPS)
    return xc * r, r


def _ln_bwd(dxh, xh, r):
    return r * (dxh - jnp.mean(dxh, axis=-1, keepdims=True) - xh * jnp.mean(dxh * xh, axis=-1, keepdims=True))


def _sigmoid(x):
    return 1.0 / (1.0 + jnp.exp(-x))


def _rsum(x):
    return jnp.sum(x, axis=0, keepdims=True)


def _dot(a, b):
    return jnp.dot(a, b, preferred_element_type=F32)


def _dot_nt(a, b):
    return lax.dot_general(a, b, (((1,), (1,)), ((), ())), preferred_element_type=F32)


def _dot_tn(a, b):
    return lax.dot_general(a, b, (((0,), (0,)), ((), ())), preferred_element_type=F32)


def _embed(x, ctx, pos):
    B, T, _ = x.shape
    nt = 1 + T // TM

    def body(x_ref, c_ref, p_ref, o_ref):
        i = pl.program_id(1)

        @pl.when(i == 0)
        def _():
            o_ref[...] = c_ref[0]

        @pl.when(i > 0)
        def _():
            o_ref[...] = x_ref[0] + p_ref[...]

    return pl.pallas_call(
        body, name="embed", grid=(B, nt),
        in_specs=[pl.BlockSpec((1, TM, D), lambda b, i: (b, jnp.maximum(i - 1, 0), 0)),
                  pl.BlockSpec((1, TM, D), lambda b, i: (b, 0, 0)),
                  pl.BlockSpec((TM, D), lambda b, i: (jnp.maximum(i - 1, 0), 0))],
        out_specs=pl.BlockSpec((TM, D), lambda b, i: (b * nt + i, 0)),
        out_shape=jax.ShapeDtypeStruct((B * nt * TM, D), F32),
        compiler_params=_cp(2))(x, ctx, pos)


def _lnmod(X, lx, mvec, comb, sub, B, nt, name):
    def body(x_ref, m_ref, o_ref):
        xh, _ = _ln(x_ref[...])
        sh = m_ref[0, 0, 3 * sub:3 * sub + 1, :]
        sc = m_ref[0, 0, 3 * sub + 1:3 * sub + 2, :]
        o_ref[...] = (xh * (1.0 + sh) + sc).astype(BF16)

    return pl.pallas_call(
        body, name=name, grid=(B, nt),
        in_specs=[pl.BlockSpec((TM, D), _rowmap(*lx)), pl.BlockSpec((1, 1, N_MOD, D), _mmap(comb))],
        out_specs=pl.BlockSpec((TM, D), _rowmap(nt, 0)),
        out_shape=jax.ShapeDtypeStruct((B * nt * TM, D), BF16),
        compiler_params=_cp(2))(X, mvec)


def _lnmod_bwd(dh, X, lx, mvec, comb, sub, B, nt, name, dres=None, ldres=None, dres_zero_ctx=False, to_x=None):
    has_res = dres is not None

    def body(*refs):
        if has_res:
            dh_ref, x_ref, m_ref, dr_ref, dx_ref, dm_ref = refs
        else:
            dh_ref, x_ref, m_ref, dx_ref, dm_ref = refs
        i = pl.program_id(1)
        xh, r = _ln(x_ref[...])
        sh = m_ref[0, 0, 3 * sub:3 * sub + 1, :]
        dhv = dh_ref[...]
        dx = _ln_bwd(dhv * (1.0 + sh), xh, r)
        if has_res:
            dr = dr_ref[...]
            if dres_zero_ctx:
                dr = jnp.where(i > 0, dr, 0.0)
            dx = dx + dr

        init = (i == 0) | (i == 1) if comb else (i == 0)

        @pl.when(init)
        def _():
            dm_ref[...] = jnp.zeros_like(dm_ref)

        dm_ref[0, 0, 0:1, :] += _rsum(dhv * xh)
        dm_ref[0, 0, 1:2, :] += _rsum(dhv)
        if to_x is None:
            dx_ref[...] = dx
        else:
            @pl.when(i > 0)
            def _():
                dx_ref[0] = dx

    in_specs = [pl.BlockSpec((TM, D), _rowmap(nt, 0)), pl.BlockSpec((TM, D), _rowmap(*lx)),
                pl.BlockSpec((1, 1, N_MOD, D), _mmap(comb))]
    args = [dh, X, mvec]
    if has_res:
        if dres_zero_ctx:
            s, o = ldres
            in_specs.append(pl.BlockSpec((TM, D), lambda b, i: (b * s + o + jnp.maximum(i - 1, 0), 0)))
        else:
            in_specs.append(pl.BlockSpec((TM, D), _rowmap(*ldres)))
        args.append(dres)
    if to_x is None:
        dx_spec = pl.BlockSpec((TM, D), _rowmap(nt, 0))
        dx_shape = jax.ShapeDtypeStruct((B * nt * TM, D), F32)
    else:
        dx_spec = pl.BlockSpec((1, TM, D), lambda b, i: (b, jnp.maximum(i - 1, 0), 0))
        dx_shape = jax.ShapeDtypeStruct(to_x, F32)
    return pl.pallas_call(
        body, name=name, grid=(B, nt), in_specs=in_specs,
        out_specs=[dx_spec, pl.BlockSpec((1, 1, 2, D), _mmap(comb))],
        out_shape=[dx_shape, jax.ShapeDtypeStruct((B, 2, 2, D), F32)],
        compiler_params=_cp(2))(*args)


def _resid(X, lx, Y, mvec, comb, sub, w, gb, B, nt, name):
    def body(x_ref, y_ref, m_ref, gb_ref, o_ref):
        g = m_ref[0, 0, 3 * sub + 2:3 * sub + 3, :]
        z = ALPHA * x_ref[...] + (w * g) * y_ref[...]
        zh, _ = _ln(z)
        o_ref[...] = zh * gb_ref[0:1, :] + gb_ref[1:2, :]

    return pl.pallas_call(
        body, name=name, grid=(B, nt),
        in_specs=[pl.BlockSpec((TM, D), _rowmap(*lx)), pl.BlockSpec((TM, D), _rowmap(nt, 0)),
                  pl.BlockSpec((1, 1, N_MOD, D), _mmap(comb)), pl.BlockSpec((2, D), lambda b, i: (0, 0))],
        out_specs=pl.BlockSpec((TM, D), _rowmap(nt, 0)),
        out_shape=jax.ShapeDtypeStruct((B * nt * TM, D), F32),
        compiler_params=_cp(2))(X, Y, mvec, gb)


def _resid_bwd(dxn, X, lx, Y, mvec, comb, sub, w, gb, B, nt, name):
    def body(d_ref, x_ref, y_ref, m_ref, gb_ref, dx_ref, dy_ref, dg_ref, dgb_ref):
        b_, i = pl.program_id(0), pl.program_id(1)
        g = m_ref[0, 0, 3 * sub + 2:3 * sub + 3, :]
        y = y_ref[...]
        z = ALPHA * x_ref[...] + (w * g) * y
        zh, r = _ln(z)
        do = d_ref[...]
        dz = _ln_bwd(do * gb_ref[0:1, :], zh, r)
        dx_ref[...] = ALPHA * dz
        dy_ref[...] = ((w * g) * dz).astype(BF16)

        @pl.when((b_ == 0) & (i == 0))
        def _():
            dgb_ref[...] = jnp.zeros_like(dgb_ref)

        dgb_ref[0:1, :] += _rsum(do * zh)
        dgb_ref[1:2, :] += _rsum(do)

        init = (i == 0) | (i == 1) if comb else (i == 0)

        @pl.when(init)
        def _():
            dg_ref[...] = jnp.zeros_like(dg_ref)

        dg_ref[0, 0] += w * _rsum(dz * y)

    return pl.pallas_call(
        body, name=name, grid=(B, nt),
        in_specs=[pl.BlockSpec((TM, D), _rowmap(nt, 0)), pl.BlockSpec((TM, D), _rowmap(*lx)),
                  pl.BlockSpec((TM, D), _rowmap(nt, 0)), pl.BlockSpec((1, 1, N_MOD, D), _mmap(comb)),
                  pl.BlockSpec((2, D), lambda b, i: (0, 0))],
        out_specs=[pl.BlockSpec((TM, D), _rowmap(nt, 0)), pl.BlockSpec((TM, D), _rowmap(nt, 0)),
                   pl.BlockSpec((1, 1, 1, D), _mmap(comb)), pl.BlockSpec((2, D), lambda b, i: (0, 0))],
        out_shape=[jax.ShapeDtypeStruct((B * nt * TM, D), F32), jax.ShapeDtypeStruct((B * nt * TM, D), BF16),
                   jax.ShapeDtypeStruct((B, 2, 1, D), F32), jax.ShapeDtypeStruct((2, D), F32)],
        compiler_params=_cp(2))(dxn, X, Y, mvec, gb)


def _ffn_out_dx(dy, W, u, name):
    M = dy.shape[0]
    half = DFF // 2

    def body(dy_ref, w_ref, u_ref, du_ref):
        dyv = dy_ref[...]
        for j in range(2):
            lo, hi = j * half, (j + 1) * half
            da = _dot_nt(dyv, w_ref[lo:hi, :])
            g = u_ref[:, lo:hi].astype(F32)
            up = u_ref[:, DFF + lo:DFF + hi].astype(F32)
            s = _sigmoid(g)
            du_ref[:, lo:hi] = (da * up * (s * (1.0 + g * (1.0 - s)))).astype(BF16)
            du_ref[:, DFF + lo:DFF + hi] = (da * (g * s)).astype(BF16)

    return pl.pallas_call(
        body, name=name, grid=(M // TM,),
        in_specs=[pl.BlockSpec((TM, D), lambda i: (i, 0)), _wspec(W, 1), pl.BlockSpec((TM, 2 * DFF), lambda i: (i, 0))],
        out_specs=pl.BlockSpec((TM, 2 * DFF), lambda i: (i, 0)),
        out_shape=jax.ShapeDtypeStruct((M, 2 * DFF), BF16), compiler_params=_cp(1))(dy, W, u)


def _loss_grad(x3, tgt, B, nt):
    def body(x_ref, t_ref, d_ref, l_ref):
        b_, i = pl.program_id(0), pl.program_id(1)
        e = x_ref[...] - t_ref[0]
        d_ref[...] = e * (1.0 / D)

        @pl.when((b_ == 0) & (i == 0))
        def _():
            l_ref[...] = jnp.zeros_like(l_ref)

        l_ref[...] += _rsum(e * e)

    return pl.pallas_call(
        body, name="loss_grad", grid=(B, nt),
        in_specs=[pl.BlockSpec((TM, D), _rowmap(nt, 0)), pl.BlockSpec((1, TM, D), lambda b, i: (b, i, 0))],
        out_specs=[pl.BlockSpec((TM, D), _rowmap(nt, 0)), pl.BlockSpec((1, D), lambda b, i: (0, 0))],
        out_shape=[jax.ShapeDtypeStruct((B * nt * TM, D), F32), jax.ShapeDtypeStruct((1, D), F32)],
        compiler_params=_cp(2))(x3, tgt)


def _wspec(W, nidx):
    zeros = (0,) * W.ndim
    if nidx == 1:
        return pl.BlockSpec(W.shape, lambda i: zeros)
    return pl.BlockSpec(W.shape, lambda b, i: zeros)


def _mm_nn(A, la, W, B, nt, out_dtype, name):
    K, N = W.shape

    def body(a_ref, w_ref, o_ref):
        o_ref[...] = _dot(a_ref[...], w_ref[...]).astype(out_dtype)

    return pl.pallas_call(
        body, name=name, grid=(B, nt),
        in_specs=[pl.BlockSpec((TM, K), _rowmap(*la)), _wspec(W, 2)],
        out_specs=pl.BlockSpec((TM, N), _rowmap(nt, 0)),
        out_shape=jax.ShapeDtypeStruct((B * nt * TM, N), out_dtype), compiler_params=_cp(2))(A, W)


def _ffn_in(A, la, W3, B, nt, name):
    K, n = W3.shape[1:]

    def body(a_ref, w_ref, u_ref, s_ref):
        a = a_ref[...]
        for j in range(2):
            g = _dot(a, w_ref[j])
            up = _dot(a, w_ref[j + 2])
            u_ref[:, j * n:(j + 1) * n] = g.astype(BF16)
            u_ref[:, (j + 2) * n:(j + 3) * n] = up.astype(BF16)
            s_ref[:, j * n:(j + 1) * n] = (g * _sigmoid(g) * up).astype(BF16)

    rows = B * nt * TM
    return pl.pallas_call(
        body, name=name, grid=(B, nt),
        in_specs=[pl.BlockSpec((TM, K), _rowmap(*la)), _wspec(W3, 2)],
        out_specs=[pl.BlockSpec((TM, 4 * n), _rowmap(nt, 0)), pl.BlockSpec((TM, 2 * n), _rowmap(nt, 0))],
        out_shape=[jax.ShapeDtypeStruct((rows, 4 * n), BF16), jax.ShapeDtypeStruct((rows, 2 * n), BF16)],
        compiler_params=_cp(2))(A, W3)


def _mm_nt(A, W, name):
    M, N = A.shape
    K = W.shape[-2]

    def body(a_ref, w_ref, o_ref):
        if W.ndim == 3:
            n = W.shape[-1]
            acc = _dot_nt(a_ref[:, 0:n], w_ref[0])
            for j in range(1, 4):
                acc = acc + _dot_nt(a_ref[:, j * n:(j + 1) * n], w_ref[j])
            o_ref[...] = acc
        else:
            o_ref[...] = _dot_nt(a_ref[...], w_ref[...])

    return pl.pallas_call(
        body, name=name, grid=(M // TM,),
        in_specs=[pl.BlockSpec((TM, N), lambda i: (i, 0)), _wspec(W, 1)],
        out_specs=pl.BlockSpec((TM, K), lambda i: (i, 0)),
        out_shape=jax.ShapeDtypeStruct((M, K), F32), compiler_params=_cp(1))(A, W)


def _mm_tn(A, G, name, tk=512, tn=512, shards=None):
    M, K = A.shape
    N = G.shape[1]
    if shards:
        tn = N // shards

    def body(a_ref, g_ref, o_ref):
        @pl.when(pl.program_id(1) == 0)
        def _():
            o_ref[...] = jnp.zeros_like(o_ref)

        upd = _dot_tn(a_ref[...], g_ref[...])
        if shards:
            o_ref[0] += upd
        else:
            o_ref[...] += upd

    if shards:
        out_spec = pl.BlockSpec((1, K, tn), lambda n, k: (n, 0, 0))
        out_shape = jax.ShapeDtypeStruct((shards, K, tn), F32)
    else:
        out_spec = pl.BlockSpec((K, tn), lambda n, k: (0, n))
        out_shape = jax.ShapeDtypeStruct((K, N), F32)
    return pl.pallas_call(
        body, name=name, grid=(N // tn, M // tk),
        in_specs=[pl.BlockSpec((tk, K), lambda n, k: (k, 0)), pl.BlockSpec((tk, tn), lambda n, k: (k, n))],
        out_specs=out_spec, out_shape=out_shape, compiler_params=_cp(2))(A, G)


def _logsig(z):
    return jnp.minimum(z, 0.0) - jnp.log(1.0 + jnp.exp(-jnp.abs(z)))


def _features(F, a2p, biasp, lbp, B):
    nt = F.shape[0] // (B * TM)
    nj = nt + 1
    R2 = nj * TM

    def body(f_ref, a2_ref, bias_ref, lb_ref, q_ref, k_ref, v_ref, g_ref):
        lr = f_ref[:, LR:LR + HD].astype(BF16)
        lane = lax.broadcasted_iota(jnp.int32, (1, 4 * HD), 1)
        keep = (lane & (HD - 1)) < GLA_DK
        for d in range(2):
            z = _dot(lr, a2_ref[d]) + bias_ref[d:d + 1, :]
            gl = jnp.where(keep, _logsig(z) * (1.0 / GATE_NORM), 0.0)
            for h in range(4):
                g_ref[d, 0, h] = gl[:, h * HD:(h + 1) * HD]
        for h in range(4):
            q_ref[0, h] = f_ref[:, GQ + h * HD:GQ + (h + 1) * HD] * (GLA_DK ** -0.5)
            kk = f_ref[:, GK + h * HD:GK + (h + 1) * HD]
            k_ref[0, 0, h] = kk
            k_ref[1, 0, h] = kk
            v_ref[0, h] = f_ref[:, GV + h * HD:GV + (h + 1) * HD]
        for h in range(4):
            sl = slice(h * HD, (h + 1) * HD)
            rq = f_ref[:, RQ + h * HD:RQ + (h + 1) * HD]
            q_ref[0, 4 + h] = rq * _sigmoid(rq) * (HD ** -0.5)
            v_ref[0, 4 + h] = f_ref[:, RI + h * HD:RI + (h + 1) * HD]
            for d, off in ((0, RFF), (1, RFB)):
                lb = lb_ref[d:d + 1, sl]
                f = lb + (1.0 - lb) * _sigmoid(f_ref[:, off + h * HD:off + (h + 1) * HD])
                g_ref[d, 0, 4 + h] = jnp.log(f)
                k_ref[d, 0, 4 + h] = 1.0 - f

    fmap = lambda b, j: (b * nt + jnp.where(j == nt, 0, j), 0)
    one = pl.BlockSpec((1, NH, TM, HD), lambda b, j: (b, 0, j, 0))
    two = pl.BlockSpec((2, 1, NH, TM, HD), lambda b, j: (0, b, 0, j, 0))
    s1 = jax.ShapeDtypeStruct((B, NH, R2, HD), F32)
    s2 = jax.ShapeDtypeStruct((2, B, NH, R2, HD), F32)
    return pl.pallas_call(
        body, name="mix_features", grid=(B, nj),
        in_specs=[pl.BlockSpec((TM, MIXP), fmap), pl.BlockSpec((2, HD, 4 * HD), lambda b, j: (0, 0, 0)),
                  pl.BlockSpec((2, 4 * HD), lambda b, j: (0, 0)), pl.BlockSpec((2, 4 * HD), lambda b, j: (0, 0))],
        out_specs=[one, two, one, two], out_shape=[s1, s2, s1, s2],
        compiler_params=_cp(2))(F, a2p, biasp, lbp)


def _features_bwd(F, a2p, biasp, lbp, dQ0, dQ1, dK0, dK1, dV0, dV1, dG0, dG1, dgates, B):
    nt = F.shape[0] // (B * TM)

    def body(f_ref, a2_ref, bias_ref, lb_ref, dq0, dq1, dk0, dk1, dv0, dv1, dg0, dg1, dgt_ref,
             df_ref, da2_ref, dbias_ref, dlb_ref):
        b_, i = pl.program_id(0), pl.program_id(1)

        @pl.when((b_ == 0) & (i == 0))
        def _():
            da2_ref[...] = jnp.zeros_like(da2_ref)
            dbias_ref[...] = jnp.zeros_like(dbias_ref)
            dlb_ref[...] = jnp.zeros_like(dlb_ref)

        df_ref[:, 0:2 * 4 * HD] = jnp.where(i > 0, dgt_ref[...], 0.0).astype(BF16)
        df_ref[:, LR + HD:] = jnp.zeros((TM, MIXP - LR - HD), BF16)
        lr = f_ref[:, LR:LR + HD].astype(BF16)
        lane = lax.broadcasted_iota(jnp.int32, (1, 4 * HD), 1)
        keep = (lane & (HD - 1)) < GLA_DK
        dlr = jnp.zeros((TM, HD), F32)
        dgs = (dg0, dg1)
        dks = (dk0, dk1)
        for d in range(2):
            z = _dot(lr, a2_ref[d]) + bias_ref[d:d + 1, :]
            dgl = jnp.concatenate([dgs[d][0, h] for h in range(4)], axis=1)
            dz = jnp.where(keep, dgl * (1.0 / GATE_NORM) * (1.0 - _sigmoid(z)), 0.0)
            dzb = dz.astype(BF16)
            dlr = dlr + _dot_nt(dzb, a2_ref[d])
            da2_ref[d] += _dot_tn(lr, dzb)
            dbias_ref[d:d + 1, :] += _rsum(dz)
        df_ref[:, LR:LR + HD] = dlr.astype(BF16)
        for h in range(4):
            df_ref[:, GQ + h * HD:GQ + (h + 1) * HD] = ((dq0[0, h] + dq1[0, h]) * (GLA_DK ** -0.5)).astype(BF16)
            df_ref[:, GK + h * HD:GK + (h + 1) * HD] = (dk0[0, h] + dk1[0, h]).astype(BF16)
            df_ref[:, GV + h * HD:GV + (h + 1) * HD] = (dv0[0, h] + dv1[0, h]).astype(BF16)
        for h in range(4):
            sl = slice(h * HD, (h + 1) * HD)
            rq = f_ref[:, RQ + h * HD:RQ + (h + 1) * HD]
            s = _sigmoid(rq)
            dqh = dq0[0, 4 + h] + dq1[0, 4 + h]
            df_ref[:, RQ + h * HD:RQ + (h + 1) * HD] = (dqh * (HD ** -0.5) * (s * (1.0 + rq * (1.0 - s)))).astype(BF16)
            df_ref[:, RI + h * HD:RI + (h + 1) * HD] = (dv0[0, 4 + h] + dv1[0, 4 + h]).astype(BF16)
            for d, off in ((0, RFF), (1, RFB)):
                lb = lb_ref[d:d + 1, sl]
                sg = _sigmoid(f_ref[:, off + h * HD:off + (h + 1) * HD])
                f = lb + (1.0 - lb) * sg
                dff = dgs[d][0, 4 + h] / f - dks[d][0, 4 + h]
                df_ref[:, off + h * HD:off + (h + 1) * HD] = (dff * (1.0 - lb) * sg * (1.0 - sg)).astype(BF16)
                dlb_ref[d:d + 1, sl] += _rsum(dff * (1.0 - sg))

    m0 = lambda b, i: (b, 0, i, 0)
    m1 = lambda b, i: (b, 0, jnp.where(i == 0, nt, i), 0)
    one = lambda m: pl.BlockSpec((1, NH, TM, HD), m)
    return pl.pallas_call(
        body, name="mix_features_bwd", grid=(B, nt),
        in_specs=[pl.BlockSpec((TM, MIXP), _rowmap(nt, 0)), pl.BlockSpec((2, HD, 4 * HD), lambda b, i: (0, 0, 0)),
                  pl.BlockSpec((2, 4 * HD), lambda b, i: (0, 0)), pl.BlockSpec((2, 4 * HD), lambda b, i: (0, 0)),
                  one(m0), one(m1), one(m0), one(m1), one(m0), one(m1), one(m0), one(m1),
                  pl.BlockSpec((TM, D), lambda b, i: (b * (nt - 1) + jnp.maximum(i - 1, 0), 0))],
        out_specs=[pl.BlockSpec((TM, MIXP), _rowmap(nt, 0)), pl.BlockSpec((2, HD, 4 * HD), lambda b, i: (0, 0, 0)),
                   pl.BlockSpec((2, 4 * HD), lambda b, i: (0, 0)), pl.BlockSpec((2, 4 * HD), lambda b, i: (0, 0))],
        out_shape=[jax.ShapeDtypeStruct((B * nt * TM, MIXP), BF16), jax.ShapeDtypeStruct((2, HD, 4 * HD), F32),
                   jax.ShapeDtypeStruct((2, 4 * HD), F32), jax.ShapeDtypeStruct((2, 4 * HD), F32)],
        compiler_params=_cp(2))(F, a2p, biasp, lbp, dQ0, dQ1, dK0, dK1, dV0, dV1, dG0, dG1, dgates)


def _chunk_scan(x, rin, fwd, inclusive=True):
    acc = x
    sft = 1
    while sft < CH:
        if fwd:
            acc = acc + jnp.where(rin >= sft, pltpu.roll(acc, sft, 0), 0.0)
        else:
            acc = acc + jnp.where(rin < CH - sft, pltpu.roll(acc, TM - sft, 0), 0.0)
        sft *= 2
    return acc if inclusive else acc - x


def _scan_common(q, k, v, g, rev):
    rin = lax.broadcasted_iota(jnp.int32, (TM, HD), 0) & (CH - 1)
    b = _chunk_scan(g, rin, not rev)
    xx = _chunk_scan(g, rin, rev, inclusive=False)
    eb = jnp.exp(b)
    qd = q * eb
    ki = k * jnp.exp(-b)
    kt = k * jnp.exp(xx)
    ri = lax.broadcasted_iota(jnp.int32, (SB, SB), 0)
    ci = lax.broadcasted_iota(jnp.int32, (SB, SB), 1)
    same = (ri >> 5) == (ci >> 5)
    lo = same & (ri >= ci)
    up = same & (ri <= ci)
    mask, maskT = (up, lo) if rev else (lo, up)
    re = lax.broadcasted_iota(jnp.int32, (SB, CSB * HD), 0) >> 5
    ce = lax.broadcasted_iota(jnp.int32, (SB, CSB * HD), 1) >> 7
    mexp = re == ce
    return rin, b, xx, eb, qd, ki, kt, mask, maskT, mexp


def _sub(x, s):
    return x[s * SB:(s + 1) * SB]


def _expand(xb, mexp):
    return jnp.where(mexp, jnp.concatenate([xb] * CSB, axis=1), jnp.zeros((), xb.dtype))


def _own(x, mexp):
    xm = jnp.where(mexp, x, 0.0)
    acc = xm[:, 0:HD]
    for n in range(1, CSB):
        acc = acc + xm[:, n * HD:(n + 1) * HD]
    return acc


def _stack(per_chunk, s):
    return jnp.concatenate(per_chunk[s * CSB:(s + 1) * CSB], axis=1)


def _state_pass(s0, eb, uts, rev):
    order = range(NCB - 1, -1, -1) if rev else range(NCB)
    states = [None] * NCB
    s = s0
    for n in order:
        row = n * CH if rev else n * CH + CH - 1
        states[n] = s
        s = eb[row:row + 1, :] * s + uts[n // CSB][:, (n % CSB) * HD:(n % CSB + 1) * HD]
    return states, s


def _scan_fwd(Q, K, V, G, rev, B):
    nb = Q.shape[2] // TM - 1
    d = 1 if rev else 0
    rmap = (lambda s: nb - s) if rev else (lambda s: s)

    def body(q_ref, k_ref, v_ref, g_ref, o_ref, st_ref, s_scr):
        @pl.when(pl.program_id(2) == 0)
        def _():
            s_scr[...] = jnp.zeros_like(s_scr)

        for p in range(HP):
            s0 = s_scr[p]
            st_ref[0, p, 0] = s0
            v = v_ref[0, p]
            _, _, _, eb, qd, ki, kt, mask, _, mexp = _scan_common(q_ref[0, p], k_ref[0, 0, p], v, g_ref[0, 0, p], rev)
            qb, kib, ktb, vb = qd.astype(BF16), ki.astype(BF16), kt.astype(BF16), v.astype(BF16)
            uts = [_dot_tn(_sub(vb, s), _expand(_sub(ktb, s), mexp)) for s in range(NSB)]
            states, s_new = _state_pass(s0, eb, uts, rev)
            s_scr[p] = s_new
            for s in range(NSB):
                a = jnp.where(mask, _dot_nt(_sub(qb, s), _sub(kib, s)), 0.0)
                o_ref[0, p, s * SB:(s + 1) * SB, :] = (
                    _dot(a.astype(BF16), _sub(vb, s))
                    + _dot_nt(_expand(_sub(qb, s), mexp), _stack(states, s).astype(BF16)))

    one = pl.BlockSpec((1, HP, TM, HD), lambda b, h, s: (b, h, rmap(s), 0))
    two = pl.BlockSpec((1, 1, HP, TM, HD), lambda b, h, s: (d, b, h, rmap(s), 0))
    return pl.pallas_call(
        body, name="scan_fwd_rev" if rev else "scan_fwd", grid=(B, NH // HP, nb),
        in_specs=[one, two, one, two],
        out_specs=[one, pl.BlockSpec((1, HP, 1, HD, HD), lambda b, h, s: (b, h, s, 0, 0))],
        out_shape=[jax.ShapeDtypeStruct(Q.shape, F32), jax.ShapeDtypeStruct((B, NH, nb, HD, HD), F32)],
        scratch_shapes=[pltpu.VMEM((HP, HD, HD), F32)],
        compiler_params=_cp(3))(Q, K, V, G)


def _scan_bwd(Q, K, V, G, St, dO, rev, B):
    nb = Q.shape[2] // TM - 1
    d = 1 if rev else 0
    smap = lambda t: nb - 1 - t
    rmap = (lambda t: nb - smap(t)) if rev else smap

    def body(q_ref, k_ref, v_ref, g_ref, st_ref, do_ref, dq_ref, dk_ref, dv_ref, dg_ref, ds_scr):
        t = pl.program_id(2)

        @pl.when(t == 0)
        def _():
            ds_scr[...] = jnp.zeros_like(ds_scr)

        is_lat = smap(t) >= 1
        for p in range(HP):
            v = v_ref[0, p]
            rin, b, xx, eb, qd, ki, kt, mask, maskT, mexp = _scan_common(
                q_ref[0, p], k_ref[0, 0, p], v, g_ref[0, 0, p], rev)
            qb, kib, ktb, vb = qd.astype(BF16), ki.astype(BF16), kt.astype(BF16), v.astype(BF16)
            dob = jnp.where(is_lat, do_ref[0, p], 0.0).astype(BF16)
            kt_exps = [_expand(_sub(ktb, s), mexp) for s in range(NSB)]
            uts = [_dot_tn(_sub(vb, s), kt_exps[s]) for s in range(NSB)]
            states, _ = _state_pass(st_ref[0, p, 0], eb, uts, rev)
            gts = [_dot_tn(_sub(dob, s), _expand(_sub(qb, s), mexp)) for s in range(NSB)]
            order = range(NCB) if rev else range(NCB - 1, -1, -1)
            dsp = [None] * NCB
            t2 = [None] * NCB
            dsc = ds_scr[p]
            for n in order:
                row = n * CH if rev else n * CH + CH - 1
                ebl = eb[row:row + 1, :]
                dsp[n] = dsc
                t2[n] = jnp.broadcast_to(ebl * _rsum(states[n] * dsc), (CH, HD))
                dsc = gts[n // CSB][:, (n % CSB) * HD:(n % CSB + 1) * HD] + ebl * dsc
            ds_scr[p] = dsc
            dqds, dkis, dkts = [], [], []
            for s in range(NSB):
                q_s, ki_s, v_s, do_s = _sub(qb, s), _sub(kib, s), _sub(vb, s), _sub(dob, s)
                dspb = _stack(dsp, s).astype(BF16)
                da = jnp.where(mask, _dot_nt(do_s, v_s), 0.0).astype(BF16)
                dat = jnp.where(maskT, _dot_nt(v_s, do_s), 0.0).astype(BF16)
                at = jnp.where(maskT, _dot_nt(ki_s, q_s), 0.0).astype(BF16)
                dqds.append(_dot(da, ki_s) + _own(_dot(do_s, _stack(states, s).astype(BF16)), mexp))
                dkis.append(_dot(dat, q_s))
                dv_ref[0, p, s * SB:(s + 1) * SB, :] = _dot(at, do_s) + _dot_nt(kt_exps[s], dspb)
                dkts.append(_own(_dot(v_s, dspb), mexp))
            dqd, dki, dkt = (jnp.concatenate(parts, axis=0) for parts in (dqds, dkis, dkts))
            z = dkt * kt
            db = dqd * qd - dki * ki
            dq_ref[0, p] = dqd * eb
            dk_ref[0, p] = dki * jnp.exp(-b) + dkt * jnp.exp(xx)
            dg_ref[0, p] = (_chunk_scan(db, rin, rev) + _chunk_scan(z, rin, not rev, inclusive=False)
                            + jnp.concatenate(t2, axis=0))

    one = pl.BlockSpec((1, HP, TM, HD), lambda b, h, t: (b, h, rmap(t), 0))
    two = pl.BlockSpec((1, 1, HP, TM, HD), lambda b, h, t: (d, b, h, rmap(t), 0))
    lat = pl.BlockSpec((1, HP, TM, HD), lambda b, h, t: (b, h, jnp.clip(rmap(t) - 1, 0, nb - 2), 0))
    shp = jax.ShapeDtypeStruct(Q.shape, F32)
    return pl.pallas_call(
        body, name="scan_bwd_rev" if rev else "scan_bwd", grid=(B, NH // HP, nb),
        in_specs=[one, two, one, two, pl.BlockSpec((1, HP, 1, HD, HD), lambda b, h, t: (b, h, smap(t), 0, 0)), lat],
        out_specs=[one, one, one, one], out_shape=[shp, shp, shp, shp],
        scratch_shapes=[pltpu.VMEM((HP, HD, HD), F32)],
        compiler_params=_cp(3))(Q, K, V, G, St, dO)


def _gnorm(O0, O1, F, gains, B, ntl):
    nt = ntl + 1

    def body(o0_ref, o1_ref, f_ref, gn_ref, m_ref):
        for h in range(NH):
            o = o0_ref[0, h] + o1_ref[0, h]
            r = lax.rsqrt(jnp.mean(o * o, axis=-1, keepdims=True) + NORM_EPS)
            gn = gn_ref[0:1, :] if h < 4 else gn_ref[1:2, :]
            gt = f_ref[:, h * HD:(h + 1) * HD]
            m_ref[:, h * HD:(h + 1) * HD] = (o * r * gn * (gt * _sigmoid(gt))).astype(BF16)

    ospec = pl.BlockSpec((1, NH, TM, HD), lambda b, i: (b, 0, i + 1, 0))
    return pl.pallas_call(
        body, name="gated_norm", grid=(B, ntl),
        in_specs=[ospec, ospec, pl.BlockSpec((TM, D), lambda b, i: (b * nt + 1 + i, 0)),
                  pl.BlockSpec((2, HD), lambda b, i: (0, 0))],
        out_specs=pl.BlockSpec((TM, D), _rowmap(ntl, 0)),
        out_shape=jax.ShapeDtypeStruct((B * ntl * TM, D), BF16), compiler_params=_cp(2))(O0, O1, F, gains)


def _gnorm_bwd(dM, O0, O1, F, gains, B, ntl):
    nt = ntl + 1

    def body(dm_ref, o0_ref, o1_ref, f_ref, gn_ref, do_ref, dgt_ref, dgn_ref):
        b_, i = pl.program_id(0), pl.program_id(1)

        @pl.when((b_ == 0) & (i == 0))
        def _():
            dgn_ref[...] = jnp.zeros_like(dgn_ref)

        for h in range(NH):
            o = o0_ref[0, h] + o1_ref[0, h]
            r = lax.rsqrt(jnp.mean(o * o, axis=-1, keepdims=True) + NORM_EPS)
            y = o * r
            gn = gn_ref[0:1, :] if h < 4 else gn_ref[1:2, :]
            gt = f_ref[:, h * HD:(h + 1) * HD]
            s = _sigmoid(gt)
            dm = dm_ref[:, h * HD:(h + 1) * HD]
            don = dm * (gt * s)
            dgt_ref[:, h * HD:(h + 1) * HD] = dm * (y * gn) * (s * (1.0 + gt * (1.0 - s)))
            row = 0 if h < 4 else 1
            dgn_ref[row:row + 1, :] += _rsum(don * y)
            dy = don * gn
            do_ref[0, h] = r * (dy - y * jnp.mean(dy * y, axis=-1, keepdims=True))

    ospec = pl.BlockSpec((1, NH, TM, HD), lambda b, i: (b, 0, i + 1, 0))
    return pl.pallas_call(
        body, name="gated_norm_bwd", grid=(B, ntl),
        in_specs=[pl.BlockSpec((TM, D), _rowmap(ntl, 0)), ospec, ospec,
                  pl.BlockSpec((TM, D), lambda b, i: (b * nt + 1 + i, 0)), pl.BlockSpec((2, HD), lambda b, i: (0, 0))],
        out_specs=[pl.BlockSpec((1, NH, TM, HD), lambda b, i: (b, 0, i, 0)), pl.BlockSpec((TM, D), _rowmap(ntl, 0)),
                   pl.BlockSpec((2, HD), lambda b, i: (0, 0))],
        out_shape=[jax.ShapeDtypeStruct((B, NH, ntl * TM, HD), F32), jax.ShapeDtypeStruct((B * ntl * TM, D), F32),
                   jax.ShapeDtypeStruct((2, HD), F32)],
        compiler_params=_cp(2))(dM, O0, O1, F, gains)


def _sincos_2d(rows, width, dim):
    r = jnp.repeat(jnp.arange(rows), width)
    col = jnp.tile(jnp.arange(width), rows)
    quarter = dim // 4
    omega = 1.0 / 10000.0 ** (jnp.arange(quarter, dtype=F32) / quarter)

    def emb(p):
        a = p.astype(F32)[:, None] * omega[None, :]
        return jnp.concatenate([jnp.sin(a), jnp.cos(a)], axis=-1)

    return jnp.concatenate([emb(r), emb(col)], axis=-1)


def _pad_heads(w):
    k = w.shape[0]
    return jnp.pad(w.reshape(k, 4, GLA_DK), ((0, 0), (0, 0), (0, HD - GLA_DK))).reshape(k, 4 * HD)


def _unpad_heads(w):
    k = w.shape[0]
    return w.reshape(k, 4, HD)[:, :, :GLA_DK].reshape(k, 4 * GLA_DK)


MIX_N = 1032
MIX_NP = 1152
_SEGS = ([(64 * h, 64, GQ + HD * h) for h in range(4)] + [(256 + 64 * h, 64, GK + HD * h) for h in range(4)]
         + [(512, 512, GV), (1024, 512, GG), (1536, 32, LR), (1568, 512, RQ), (2080, 512, RFF), (2592, 512, RFB),
            (3104, 512, RI), (3616, 512, RG)])


def _mix_in_to_padded(ps):
    k = ps.shape[1]
    parts, pos = [], 0
    for g0, ln, s0 in sorted(_SEGS, key=lambda s: s[2]):
        if s0 > pos:
            parts.append(jnp.zeros((k, s0 - pos), ps.dtype))
        for j in range(4):
            lo, hi = max(g0, j * MIX_N), min(g0 + ln, (j + 1) * MIX_N)
            if lo < hi:
                parts.append(ps[j][:, lo - j * MIX_N:hi - j * MIX_N])
        pos = s0 + ln
    parts.append(jnp.zeros((k, MIXP - pos), ps.dtype))
    return jnp.concatenate(parts, axis=1)


def _mix_in_from_padded(g):
    k = g.shape[0]
    shards = []
    for j in range(4):
        parts = []
        for g0, ln, s0 in sorted(_SEGS):
            lo, hi = max(g0, j * MIX_N), min(g0 + ln, (j + 1) * MIX_N)
            if lo < hi:
                parts.append(g[:, s0 + lo - g0:s0 + hi - g0])
        parts.append(jnp.zeros((k, MIX_NP - MIX_N), g.dtype))
        shards.append(jnp.concatenate(parts, axis=1))
    return jnp.stack(shards)


def _local_step(x, ctx, tgt, mvec, w1i, w1o, wmp, wmo, w2i, w2o, ln_gain, ln_bias, a2f, a2b, abf, abb, lb, gng, gnh):
    B, T, _ = x.shape
    assert ctx.shape[1] == TM and T % TM == 0
    ntl = T // TM
    nt = ntl + 1
    C, L, CL = (nt, 0), (ntl, 0), (nt, 1)
    pos = _sincos_2d(T // 64, 64, D)
    gbs = [jnp.stack([ln_gain[i], ln_bias[i]]) for i in range(3)]
    a2p = jnp.zeros((2, HD, 4 * HD), F32)
    a2p = a2p.at[0, 0:16].set(_pad_heads(a2f)).at[1, 16:32].set(_pad_heads(a2b)).astype(BF16)
    biasp = jnp.concatenate([_pad_heads(abf.reshape(1, -1)), _pad_heads(abb.reshape(1, -1))], axis=0)
    gains = jnp.concatenate([gng.reshape(1, HD), gnh.reshape(1, HD)], axis=0)

    X0 = _embed(x, ctx, pos)
    h0 = _lnmod(X0, C, mvec, True, 0, B, nt, "lnmod0")
    u0, a0 = _ffn_in(h0, C, w1i, B, nt, "ffn1_in")
    y0 = _mm_nn(a0, C, w1o, B, nt, F32, "ffn1_out")
    X1 = _resid(X0, C, y0, mvec, True, 0, 0.5, gbs[0], B, nt, "resid0")
    h1 = _lnmod(X1, C, mvec, True, 1, B, nt, "lnmod1")
    Fm = _mm_nn(h1, C, wmp, B, nt, F32, "mix_in")
    Q, K, V, G = _features(Fm, a2p, biasp, lb, B)
    O0, S0 = _scan_fwd(Q, K, V, G, False, B)
    O1, S1 = _scan_fwd(Q, K, V, G, True, B)
    merged = _gnorm(O0, O1, Fm, gains, B, ntl)
    y1 = _mm_nn(merged, L, wmo, B, ntl, F32, "mix_out")
    X2 = _resid(X1, CL, y1, mvec, False, 1, 1.0, gbs[1], B, ntl, "resid1")
    h2 = _lnmod(X2, L, mvec, False, 2, B, ntl, "lnmod2")
    u2, a2 = _ffn_in(h2, L, w2i, B, ntl, "ffn2_in")
    y2 = _mm_nn(a2, L, w2o, B, ntl, F32, "ffn2_out")
    X3 = _resid(X2, L, y2, mvec, False, 2, 0.5, gbs[2], B, ntl, "resid2")
    dX3, lsum = _loss_grad(X3, tgt, B, ntl)
    loss = (0.5 / D) * jnp.sum(lsum)

    dx2r, dy2, dgate2, dgb2 = _resid_bwd(dX3, X2, L, y2, mvec, False, 2, 0.5, gbs[2], B, ntl, "resid2_bwd")
    du2 = _ffn_out_dx(dy2, w2o, u2, "ffn2_out_dx")
    g_w2o = _mm_tn(a2, dy2, "ffn2_out_dw")
    dh2 = _mm_nt(du2, w2i, "ffn2_in_dx")
    g_w2i = _mm_tn(h2, du2, "ffn2_in_dw", shards=4)
    dX2, dss2 = _lnmod_bwd(dh2, X2, L, mvec, False, 2, B, ntl, "lnmod2_bwd", dres=dx2r, ldres=L)
    dx1r, dy1, dgate1, dgb1 = _resid_bwd(dX2, X1, CL, y1, mvec, False, 1, 1.0, gbs[1], B, ntl, "resid1_bwd")
    dmerged = _mm_nt(dy1, wmo, "mix_out_dx")
    g_wmo = _mm_tn(merged, dy1, "mix_out_dw")
    dO, dgates, dgains = _gnorm_bwd(dmerged, O0, O1, Fm, gains, B, ntl)
    dQ0, dK0, dV0, dG0 = _scan_bwd(Q, K, V, G, S0, dO, False, B)
    dQ1, dK1, dV1, dG1 = _scan_bwd(Q, K, V, G, S1, dO, True, B)
    dF, da2p, dbiasp, dlb = _features_bwd(Fm, a2p, biasp, lb, dQ0, dQ1, dK0, dK1, dV0, dV1, dG0, dG1, dgates, B)
    dh1 = _mm_nt(dF, wmp, "mix_in_dx")
    g_wmp = _mm_tn(h1, dF, "mix_in_dw")
    dX1, dss1 = _lnmod_bwd(dh1, X1, C, mvec, True, 1, B, nt, "lnmod1_bwd", dres=dx1r, ldres=L, dres_zero_ctx=True)
    dx0r, dy0, dgate0, dgb0 = _resid_bwd(dX1, X0, C, y0, mvec, True, 0, 0.5, gbs[0], B, nt, "resid0_bwd")
    du0 = _ffn_out_dx(dy0, w1o, u0, "ffn1_out_dx")
    g_w1o = _mm_tn(a0, dy0, "ffn1_out_dw")
    dh0 = _mm_nt(du0, w1i, "ffn1_in_dx")
    g_w1i = _mm_tn(h0, du0, "ffn1_in_dw", shards=4)
    grad_x, dss0 = _lnmod_bwd(dh0, X0, C, mvec, True, 0, B, nt, "lnmod0_bwd", dres=dx0r, ldres=C, to_x=x.shape)

    zero_ctx = lambda a: a.at[:, 0].set(0.0)
    dm = jnp.concatenate([dss0, dgate0, dss1, zero_ctx(dgate1), zero_ctx(dss2), zero_ctx(dgate2)], axis=2)
    small = dict(
        ln_gain=jnp.stack([dgb0[0], dgb1[0], dgb2[0]]), ln_bias=jnp.stack([dgb0[1], dgb1[1], dgb2[1]]),
        a2f=_unpad_heads(da2p[0, 0:16]), a2b=_unpad_heads(da2p[1, 16:32]),
        abf=_unpad_heads(dbiasp[0:1]), abb=_unpad_heads(dbiasp[1:2]), lb=dlb, gng=dgains[0], gnh=dgains[1])
    big = dict(w1i=g_w1i, w1o=g_w1o, wmp=g_wmp, wmo=g_wmo, w2i=g_w2i, w2o=g_w2o)
    return loss, grad_x, big, dm, small


def _small_allgather(xs, name):
    r, n = xs.shape

    def body(x_ref, out_ref, send_sems, recv_sems, local_sem):
        x, y, c = lax.axis_index("x"), lax.axis_index("y"), lax.axis_index("c")
        me, sibling = (x, y, c), (x, y, 1 - c)
        chips = [(1 - x, y), (x, 1 - y), (1 - x, 1 - y)]

        def rows(px, py, pc):
            return out_ref.at[pl.ds((4 * px + 2 * py + pc) * r, r), :]

        def copy(k, block, to, src=None):
            return pltpu.make_async_remote_copy(
                src_ref=rows(*block) if src is None else src, dst_ref=rows(*block),
                send_sem=send_sems.at[k], recv_sem=recv_sems.at[k], device_id=to, device_id_type=MESH)

        mine = pltpu.make_async_copy(x_ref, rows(*me), local_sem)
        mine.start()
        first = [copy(0, me, sibling, src=x_ref)]
        first += [copy(1 + j, me, (*chip, c), src=x_ref) for j, chip in enumerate(chips)]
        for cp in first:
            cp.start()
        passed = [copy(4 + j, (*chip, c), sibling) for j, chip in enumerate(chips)]
        for j, chip in enumerate(chips):
            copy(1 + j, (*chip, c), me).wait_recv()
            passed[j].start()
        copy(0, sibling, me).wait_recv()
        for j, chip in enumerate(chips):
            copy(4 + j, (*chip, 1 - c), me).wait_recv()
        for cp in first + passed:
            cp.wait_send()
        mine.wait()

    out = pl.pallas_call(
        body, name=name,
        out_shape=jax.ShapeDtypeStruct((8 * r, n), xs.dtype),
        in_specs=[pl.BlockSpec(memory_space=pltpu.VMEM)],
        out_specs=pl.BlockSpec(memory_space=pltpu.VMEM),
        scratch_shapes=[pltpu.SemaphoreType.DMA((7,)), pltpu.SemaphoreType.DMA((7,)), pltpu.SemaphoreType.DMA],
        compiler_params=pltpu.CompilerParams(vmem_limit_bytes=VMEM_LIMIT))(xs)
    return out.reshape(8, r, n)


def _gather_flat(v, name):
    n = v.shape[0]
    npad = -(-n // 1024) * 1024
    g = _small_allgather(jnp.pad(v, (0, npad - n)).reshape(8, npad // 8), name)
    return g.reshape(8, npad)[:, :n]


def _big_allgather(blks, name):
    n = len(blks)

    def body(*refs):
        xs, outs = refs[:n], refs[n:2 * n]
        send_sems, recv_sems = refs[2 * n:]
        x, y, c = lax.axis_index("x"), lax.axis_index("y"), lax.axis_index("c")
        me, sibling = (x, y, c), (x, y, 1 - c)
        chips = [(1 - x, y), (x, 1 - y), (1 - x, 1 - y)]

        def copy(w, k, block, to, own=False):
            px, py, pc = block
            slot = outs[w].at[4 * px + 2 * py + pc]
            return pltpu.make_async_remote_copy(
                src_ref=xs[w] if own else slot, dst_ref=slot, send_sem=send_sems.at[7 * w + k],
                recv_sem=recv_sems.at[7 * w + k], device_id=to, device_id_type=MESH)

        first = []
        for w in range(n):
            first.append(copy(w, 0, me, sibling, own=True))
            first += [copy(w, 1 + j, me, (*chip, c), own=True) for j, chip in enumerate(chips)]
        for cp in first:
            cp.start()
        passed = []
        for w in range(n):
            for j, chip in enumerate(chips):
                copy(w, 1 + j, (*chip, c), me).wait_recv()
                cp = copy(w, 4 + j, (*chip, c), sibling)
                cp.start()
                passed.append(cp)
        for w in range(n):
            copy(w, 0, sibling, me).wait_recv()
            for j, chip in enumerate(chips):
                copy(w, 4 + j, (*chip, 1 - c), me).wait_recv()
        for cp in first + passed:
            cp.wait_send()

    any_spec = pl.BlockSpec(memory_space=pl.ANY)
    return pl.pallas_call(
        body, name=name,
        out_shape=[jax.ShapeDtypeStruct((8,) + b.shape, b.dtype) for b in blks],
        in_specs=[any_spec] * n, out_specs=[any_spec] * n,
        scratch_shapes=[pltpu.SemaphoreType.DMA((7 * n,)), pltpu.SemaphoreType.DMA((7 * n,))],
    )(*blks)


def _rs_pair_exchange(g8s, name):
    n = len(g8s)

    def body(*refs):
        gs, rs = refs[:n], refs[n:2 * n]
        send_sems, recv_sems = refs[2 * n:]
        x, y, c = lax.axis_index("x"), lax.axis_index("y"), lax.axis_index("c")
        cps = [pltpu.make_async_remote_copy(
            src_ref=gs[w].at[2 * j + 1 - c], dst_ref=rs[w].at[j], send_sem=send_sems.at[4 * w + j],
            recv_sem=recv_sems.at[4 * w + j], device_id=(x, y, 1 - c), device_id_type=MESH)
            for w in range(n) for j in range(4)]
        for cp in cps:
            cp.start()
        for cp in cps:
            cp.wait_recv()
        for cp in cps:
            cp.wait_send()

    any_spec = pl.BlockSpec(memory_space=pl.ANY)
    return pl.pallas_call(
        body, name=name,
        out_shape=[jax.ShapeDtypeStruct((4,) + g.shape[1:], g.dtype) for g in g8s],
        in_specs=[any_spec] * n, out_specs=[any_spec] * n,
        scratch_shapes=[pltpu.SemaphoreType.DMA((4 * n,)), pltpu.SemaphoreType.DMA((4 * n,))])(*g8s)


def _rs_chip_exchange(h4s, name):
    n = len(h4s)

    def body(*refs):
        hs, rs = refs[:n], refs[n:2 * n]
        send_sems, recv_sems = refs[2 * n:]
        x, y, c = lax.axis_index("x"), lax.axis_index("y"), lax.axis_index("c")
        chips = [(1 - x, y), (x, 1 - y), (1 - x, 1 - y)]
        cps = [pltpu.make_async_remote_copy(
            src_ref=hs[w].at[2 * px + py], dst_ref=rs[w].at[k], send_sem=send_sems.at[3 * w + k],
            recv_sem=recv_sems.at[3 * w + k], device_id=(px, py, c), device_id_type=MESH)
            for w in range(n) for k, (px, py) in enumerate(chips)]
        for cp in cps:
            cp.start()
        for cp in cps:
            cp.wait_recv()
        for cp in cps:
            cp.wait_send()

    any_spec = pl.BlockSpec(memory_space=pl.ANY)
    return pl.pallas_call(
        body, name=name,
        out_shape=[jax.ShapeDtypeStruct((3,) + h.shape[1:], h.dtype) for h in h4s],
        in_specs=[any_spec] * n, out_specs=[any_spec] * n,
        scratch_shapes=[pltpu.SemaphoreType.DMA((3 * n,)), pltpu.SemaphoreType.DMA((3 * n,))])(*h4s)


def _rs_pair_share(fins, name):
    n = len(fins)

    def body(*refs):
        fs, outs = refs[:n], refs[n:2 * n]
        send_sems, recv_sems = refs[2 * n:]
        x, y, c = lax.axis_index("x"), lax.axis_index("y"), lax.axis_index("c")
        cps = [pltpu.make_async_remote_copy(
            src_ref=fs[w], dst_ref=outs[w], send_sem=send_sems.at[w], recv_sem=recv_sems.at[w],
            device_id=(x, y, 1 - c), device_id_type=MESH) for w in range(n)]
        for cp in cps:
            cp.start()
        for cp in cps:
            cp.wait_recv()
        for cp in cps:
            cp.wait_send()

    any_spec = pl.BlockSpec(memory_space=pl.ANY)
    return pl.pallas_call(
        body, name=name,
        out_shape=[jax.ShapeDtypeStruct(f.shape, f.dtype) for f in fins],
        in_specs=[any_spec] * n, out_specs=[any_spec] * n,
        scratch_shapes=[pltpu.SemaphoreType.DMA((n,)), pltpu.SemaphoreType.DMA((n,))])(*fins)


def _row_block(R, cap=2048):
    best = 16
    for rb in range(16, cap + 1, 16):
        if R % rb == 0:
            best = rb
    return best


def _rs_add_pair(g8, r4, c, name):
    R = g8.shape[1]
    rb = _row_block(R)

    def body(c_ref, g_ref, r_ref, o_ref):
        o_ref[...] = (g_ref[...] + r_ref[...]).astype(BF16)

    spec = pl.BlockSpec((1, rb, 128), lambda j, i, c_ref: (j, i, 0))
    return pl.pallas_call(
        body, name=name,
        grid_spec=pltpu.PrefetchScalarGridSpec(
            num_scalar_prefetch=1, grid=(4, R // rb),
            in_specs=[pl.BlockSpec((1, rb, 128), lambda j, i, c_ref: (2 * j + c_ref[0], i, 0)), spec],
            out_specs=spec),
        out_shape=jax.ShapeDtypeStruct((4, R, 128), BF16), compiler_params=_cp(2))(c, g8, r4)


def _rs_add_chips(g8, r4, r3, cj, name):
    R = g8.shape[1]
    rb = _row_block(R)

    def body(cj_ref, g_ref, p_ref, r_ref, o_ref):
        own = g_ref[0] + p_ref[0]
        o_ref[...] = ((own + r_ref[0].astype(F32)) + r_ref[1].astype(F32)) + r_ref[2].astype(F32)

    return pl.pallas_call(
        body, name=name,
        grid_spec=pltpu.PrefetchScalarGridSpec(
            num_scalar_prefetch=1, grid=(R // rb,),
            in_specs=[pl.BlockSpec((1, rb, 128), lambda i, cj_ref: (2 * cj_ref[1] + cj_ref[0], i, 0)),
                      pl.BlockSpec((1, rb, 128), lambda i, cj_ref: (cj_ref[1], i, 0)),
                      pl.BlockSpec((3, rb, 128), lambda i, cj_ref: (0, i, 0))],
            out_specs=pl.BlockSpec((rb, 128), lambda i, cj_ref: (i, 0))),
        out_shape=jax.ShapeDtypeStruct((R, 128), F32), compiler_params=_cp(1))(cj, g8, r4, r3)


def _sum8(g):
    n = g.shape[1]

    def body(g_ref, o_ref):
        acc = g_ref[0:1, :]
        for k in range(1, 8):
            acc = acc + g_ref[k:k + 1, :]
        o_ref[...] = acc

    return pl.pallas_call(body, name="sum_devices", out_shape=jax.ShapeDtypeStruct((1, n), F32),
                          compiler_params=pltpu.CompilerParams(vmem_limit_bytes=VMEM_LIMIT))(g)


ADA_ROWS = 64


def _ada_fwd(cs, w, b):
    n = w.shape[1]

    def body(c_ref, w_ref, b_ref, o_ref):
        cv = c_ref[...]
        s = (cv * _sigmoid(cv)).astype(BF16)
        o_ref[...] = _dot(s, w_ref[...].astype(BF16)) + b_ref[...]

    return pl.pallas_call(body, name="ada_fwd", out_shape=jax.ShapeDtypeStruct((ADA_ROWS, n), F32),
                          compiler_params=pltpu.CompilerParams(vmem_limit_bytes=VMEM_LIMIT))(cs, w, b)


def _ada_bwd(cs, w, dm):
    n = w.shape[1]

    def body(c_ref, w_ref, dm_ref, gw_ref, dc_ref):
        cv = c_ref[...]
        s = (cv * _sigmoid(cv)).astype(BF16)
        gw_ref[...] = _dot_tn(s, dm_ref[...].astype(BF16))
        dc_ref[...] = _dot_nt(dm_ref[32:40, :].astype(BF16), w_ref[...].astype(BF16))

    return pl.pallas_call(
        body, name="ada_bwd",
        out_shape=[jax.ShapeDtypeStruct((D, n), F32), jax.ShapeDtypeStruct((8, D), F32)],
        compiler_params=pltpu.CompilerParams(vmem_limit_bytes=VMEM_LIMIT))(cs, w, dm)


def _adamw(w, g, m, v, name):
    r, c = w.shape
    rb = r
    if r % 8 == 0 and r * c * 4 > (1 << 20):
        rb = 8
        for cand in range(8, r, 8):
            if r % cand == 0 and cand * c * 4 <= (1 << 20):
                rb = cand

    def body(w_ref, g_ref, m_ref, v_ref, d_ref, nm_ref, nv_ref):
        gv = g_ref[...]
        mn = ADAM_B1 * m_ref[...] + (1.0 - ADAM_B1) * gv
        vn = ADAM_B2 * v_ref[...] + (1.0 - ADAM_B2) * (gv * gv)
        m_hat = mn / (1.0 - ADAM_B1 ** ADAM_STEP)
        v_hat = vn / (1.0 - ADAM_B2 ** ADAM_STEP)
        d_ref[...] = -ADAM_LR * (m_hat / (jnp.sqrt(v_hat) + ADAM_EPS) + ADAM_WD * w_ref[...])
        nm_ref[...] = mn
        nv_ref[...] = vn

    spec = pl.BlockSpec((rb, c), lambda i: (i, 0))
    shp = jax.ShapeDtypeStruct((r, c), F32)
    return pl.pallas_call(body, name=name, grid=(r // rb,), in_specs=[spec] * 4, out_specs=[spec] * 3,
                          out_shape=[shp] * 3, compiler_params=_cp(1))(w, g, m, v)


BIG = ("ffn1_w_in", "ffn1_w_out", "w_mix_in", "w_mix_out", "ffn2_w_in", "ffn2_w_out")


def _half_rows(w, c):
    half = w.shape[0] // 2
    return lax.dynamic_slice_in_dim(w, c * half, half, axis=0).reshape(-1, 128)


def _lower_bounds(logits):
    return jnp.cumsum(jax.nn.softmax(logits.astype(F32), axis=1), axis=1)[:, 0]


def kernel(x, c, ctx, c_ctx, w_ada, b_ada, ln_gain, ln_bias, ffn1_w_in, ffn1_w_out, w_mix_in, gla_a2_fwd, gla_a2_bwd, gla_a_bias_fwd, gla_a_bias_bwd, hgrn_lb_logits, gla_norm_gain, hgrn_norm_gain, w_mix_out, ffn2_w_in, ffn2_w_out, loss_target, m_c_ctx, m_w_ada, m_b_ada, m_ln_gain, m_ln_bias, m_ffn1_w_in, m_ffn1_w_out, m_w_mix_in, m_gla_a2_fwd, m_gla_a2_bwd, m_gla_a_bias_fwd, m_gla_a_bias_bwd, m_hgrn_lb_logits, m_gla_norm_gain, m_hgrn_norm_gain, m_w_mix_out, m_ffn2_w_in, m_ffn2_w_out, v_c_ctx, v_w_ada, v_b_ada, v_ln_gain, v_ln_bias, v_ffn1_w_in, v_ffn1_w_out, v_w_mix_in, v_gla_a2_fwd, v_gla_a2_bwd, v_gla_a_bias_fwd, v_gla_a_bias_bwd, v_hgrn_lb_logits, v_gla_norm_gain, v_hgrn_norm_gain, v_w_mix_out, v_ffn2_w_in, v_ffn2_w_out):
    xi, yi, ci = lax.axis_index("x"), lax.axis_index("y"), lax.axis_index("c")
    chip = 2 * xi + yi
    dev = 2 * chip + ci
    B = x.shape[0]
    weights = dict(ffn1_w_in=ffn1_w_in[0], ffn1_w_out=ffn1_w_out[0], w_mix_in=w_mix_in[0], w_mix_out=w_mix_out[0],
                   ffn2_w_in=ffn2_w_in[0], ffn2_w_out=ffn2_w_out[0])

    mine = jnp.concatenate([c.reshape(-1), ln_gain.reshape(-1), ln_bias.reshape(-1), gla_a2_fwd.reshape(-1),
                            gla_a2_bwd.reshape(-1), hgrn_lb_logits.reshape(-1)])
    g1 = _gather_flat(mine, "gather_cond")
    nc = B * D
    c_all = g1[:, :nc].reshape(8 * B, D)
    per_chip = g1[0::2, nc:]
    o = 0

    def take(shape, axis):
        nonlocal o
        n = int(np.prod(shape))
        parts = per_chip[:, o:o + n].reshape((4,) + shape)
        o += n
        return jnp.concatenate([parts[j] for j in range(4)], axis=axis)

    ln_gain_f = take((3, 256), 1)
    ln_bias_f = take((3, 256), 1)
    a2f_f = take((16, 64), 1)
    a2b_f = take((16, 64), 1)
    lbl_f = take((2, 2, 128), 2)
    lb, lb_vjp = jax.vjp(_lower_bounds, lbl_f)

    shards = dict(weights, w_mix_in=jnp.pad(weights["w_mix_in"], ((0, 0), (0, MIX_NP - MIX_N))))
    blks = [_half_rows(shards[k], ci).astype(BF16) for k in BIG]
    full = {k: lax.dynamic_update_index_in_dim(g, own, dev, 0)
            for k, own, g in zip(BIG, blks, _big_allgather(blks, "weight_allgather"))}
    w1i, w2i = (full[k].reshape((4,) + shards[k].shape) for k in ("ffn1_w_in", "ffn2_w_in"))
    w1o, wmo, w2o = (full[k].reshape(-1, D) for k in ("ffn1_w_out", "w_mix_out", "ffn2_w_out"))
    wmp = _mix_in_to_padded(full["w_mix_in"].reshape(4, D, MIX_NP))

    cs = jnp.concatenate([c_all, c_ctx.reshape(1, D), jnp.zeros((ADA_ROWS - 8 * B - 1, D), F32)], axis=0)
    ncol = w_ada.shape[2]
    b_cols = lax.dynamic_slice_in_dim(b_ada, chip * ncol, ncol, axis=1)
    m_cols = _ada_fwd(cs, w_ada[0], b_cols)
    g2 = _small_allgather(m_cols, "gather_mod")[0::2]
    m_all = jnp.concatenate([g2[j] for j in range(4)], axis=1)
    m_lat = lax.dynamic_slice_in_dim(m_all, dev * B, B, axis=0).reshape(B, 1, N_MOD, D)
    m_ctx = jnp.broadcast_to(m_all[8 * B].reshape(1, 1, N_MOD, D), (B, 1, N_MOD, D))
    mvec = jnp.concatenate([m_ctx, m_lat], axis=1)

    loss_l, grad_x, big, dm, small = _local_step(
        x, ctx, loss_target, mvec, w1i, w1o, wmp, wmo, w2i, w2o, ln_gain_f, ln_bias_f, a2f_f, a2b_f,
        gla_a_bias_fwd, gla_a_bias_bwd, lb, gla_norm_gain, hgrn_norm_gain)
    loss = lax.psum(loss_l, ("x", "y", "c"))

    dm_lat = dm[:, 1].reshape(B, N_MOD * D)
    dm_ctx = jnp.sum(dm[:, 0], axis=0).reshape(N_MOD * D)
    keys = ("ln_gain", "ln_bias", "a2f", "a2b", "abf", "abb", "lb", "gng", "gnh")
    flat = jnp.concatenate([dm_lat.reshape(-1), dm_ctx] + [small[k].reshape(-1) for k in keys])
    g3 = _gather_flat(flat, "gather_small_grads")
    nlat = B * N_MOD * D
    dm_all = g3[:, :nlat].reshape(8 * B, N_MOD * D)
    tot = _sum8(g3[:, nlat:])[0]
    dmc_tot = tot[:N_MOD * D]
    o = N_MOD * D
    sg = {}
    for k in keys:
        n = int(np.prod(small[k].shape))
        sg[k] = tot[o:o + n].reshape(small[k].shape)
        o += n
    dm_rows = jnp.concatenate([dm_all, dmc_tot.reshape(1, -1), jnp.zeros((ADA_ROWS - 8 * B - 1, N_MOD * D), F32)], axis=0)
    g_b_ada = (jnp.sum(dm_all, axis=0) + dmc_tot).reshape(1, N_MOD * D)
    g_w_ada, dcc = _ada_bwd(cs, w_ada[0], lax.dynamic_slice_in_dim(dm_rows, chip * ncol, ncol, axis=1))
    g4 = _gather_flat(dcc[0], "gather_cctx")
    dsilu = ((g4[0] + g4[2]) + g4[4]) + g4[6]
    sc = _sigmoid(c_ctx)
    g_c_ctx = dsilu * (sc * (1.0 + c_ctx * (1.0 - sc)))
    (g_lbl,) = lb_vjp(sg["lb"])

    def cols(a, n, axis):
        return lax.dynamic_slice_in_dim(a, chip * n, n, axis=axis)

    big_full = dict(ffn1_w_in=big["w1i"], ffn1_w_out=big["w1o"], w_mix_in=_mix_in_from_padded(big["wmp"]),
                    w_mix_out=big["wmo"], ffn2_w_in=big["w2i"], ffn2_w_out=big["w2o"])
    g8s = [big_full[k].reshape(8, -1, 128) for k in BIG]
    cvec = ci.reshape(1).astype(jnp.int32)
    cjvec = jnp.stack([ci, chip]).astype(jnp.int32)
    r4s = _rs_pair_exchange(g8s, "grad_pair_exchange")
    h4s = [_rs_add_pair(g, r, cvec, "grad_pair_add_" + k) for k, g, r in zip(BIG, g8s, r4s)]
    r3s = _rs_chip_exchange(h4s, "grad_chip_exchange")
    fins = [_rs_add_chips(g, r4, r3, cjvec, "grad_chip_add_" + k) for k, g, r4, r3 in zip(BIG, g8s, r4s, r3s)]
    gsh = {}
    for k, own, sib in zip(BIG, fins, _rs_pair_share(fins, "grad_pair_share")):
        both = jnp.where(ci == 0, jnp.stack([own, sib]), jnp.stack([sib, own]))
        gsh[k] = both.reshape(shards[k].shape)[:, :weights[k].shape[1]]

    grads = dict(
        c_ctx=g_c_ctx, w_ada=g_w_ada[None], b_ada=g_b_ada, ln_gain=cols(sg["ln_gain"], 256, 1)[None],
        ln_bias=cols(sg["ln_bias"], 256, 1)[None], ffn1_w_in=gsh["ffn1_w_in"][None], ffn1_w_out=gsh["ffn1_w_out"][None],
        w_mix_in=gsh["w_mix_in"][None], gla_a2_fwd=cols(sg["a2f"], 64, 1)[None], gla_a2_bwd=cols(sg["a2b"], 64, 1)[None],
        gla_a_bias_fwd=sg["abf"], gla_a_bias_bwd=sg["abb"], hgrn_lb_logits=cols(g_lbl, 128, 2),
        gla_norm_gain=sg["gng"].reshape(1, HD), hgrn_norm_gain=sg["gnh"].reshape(1, HD),
        w_mix_out=gsh["w_mix_out"][None], ffn2_w_in=gsh["ffn2_w_in"][None], ffn2_w_out=gsh["ffn2_w_out"][None])
    params = dict(
        c_ctx=(c_ctx, m_c_ctx, v_c_ctx), w_ada=(w_ada, m_w_ada, v_w_ada), b_ada=(b_ada, m_b_ada, v_b_ada),
        ln_gain=(ln_gain, m_ln_gain, v_ln_gain), ln_bias=(ln_bias, m_ln_bias, v_ln_bias),
        ffn1_w_in=(ffn1_w_in, m_ffn1_w_in, v_ffn1_w_in), ffn1_w_out=(ffn1_w_out, m_ffn1_w_out, v_ffn1_w_out),
        w_mix_in=(w_mix_in, m_w_mix_in, v_w_mix_in), gla_a2_fwd=(gla_a2_fwd, m_gla_a2_fwd, v_gla_a2_fwd),
        gla_a2_bwd=(gla_a2_bwd, m_gla_a2_bwd, v_gla_a2_bwd),
        gla_a_bias_fwd=(gla_a_bias_fwd, m_gla_a_bias_fwd, v_gla_a_bias_fwd),
        gla_a_bias_bwd=(gla_a_bias_bwd, m_gla_a_bias_bwd, v_gla_a_bias_bwd),
        hgrn_lb_logits=(hgrn_lb_logits, m_hgrn_lb_logits, v_hgrn_lb_logits),
        gla_norm_gain=(gla_norm_gain, m_gla_norm_gain, v_gla_norm_gain),
        hgrn_norm_gain=(hgrn_norm_gain, m_hgrn_norm_gain, v_hgrn_norm_gain),
        w_mix_out=(w_mix_out, m_w_mix_out, v_w_mix_out), ffn2_w_in=(ffn2_w_in, m_ffn2_w_in, v_ffn2_w_in),
        ffn2_w_out=(ffn2_w_out, m_ffn2_w_out, v_ffn2_w_out))
    order = list(params.keys())
    big_names = ("w_ada",) + BIG
    upd = {}
    for k in big_names:
        w_, m_, v_ = params[k]
        s2 = w_.shape[-2:]
        d_, nm_, nv_ = _adamw(w_.reshape(s2), grads[k].reshape(s2), m_.reshape(s2), v_.reshape(s2), "adamw_" + k)
        upd[k] = (d_.reshape(w_.shape), nm_.reshape(w_.shape), nv_.reshape(w_.shape))
    small_names = [k for k in order if k not in big_names]
    sizes = [int(np.prod(params[k][0].shape)) for k in small_names]
    tot_n = sum(sizes)
    npad = -(-tot_n // 1024) * 1024

    def packed(get):
        flat_ = jnp.concatenate([get(k).reshape(-1) for k in small_names])
        return jnp.pad(flat_, (0, npad - tot_n)).reshape(8, npad // 8)

    d_s, nm_s, nv_s = _adamw(packed(lambda k: params[k][0]), packed(lambda k: grads[k]),
                             packed(lambda k: params[k][1]), packed(lambda k: params[k][2]), "adamw_small")
    o = 0
    for k, n in zip(small_names, sizes):
        shp = params[k][0].shape
        upd[k] = tuple(a.reshape(-1)[o:o + n].reshape(shp) for a in (d_s, nm_s, nv_s))
        o += n

    return (loss, grad_x, *[grads[k].reshape(params[k][0].shape) for k in order], *[upd[k][0] for k in order],
            *[upd[k][1] for k in order], *[upd[k][2] for k in order])
```

```python
import functools

import numpy as np
import jax
import jax.numpy as jnp
from jax import lax
from jax.experimental import pallas as pl
from jax.experimental.pallas import tpu as pltpu

F32 = jnp.float32
BF16 = jnp.bfloat16
MESH = pl.DeviceIdType.MESH

D = 1024
DFF = 2816
TM = 256
CH = 32
NCB = TM // CH
SB = 128
CSB = SB // CH
NSB = TM // SB
HP = 4
HD = 128
NH = 8
LN_EPS = 1e-5
NORM_EPS = 1e-6
ALPHA = 2.0 ** 0.25
GATE_NORM = 16.0
GLA_DK = 64
N_MOD = 9
VMEM_LIMIT = 52 * 1024 * 1024

MIXP = 5120
GG, RG, GQ, GK, GV, RQ, RFF, RFB, RI, LR = 0, 512, 1024, 1536, 2048, 2560, 3072, 3584, 4096, 4608
IN_SPLITS = (256, 256, 512, 512, 16, 16, 512, 512, 512, 512, 512)

ADAM_LR, ADAM_B1, ADAM_B2, ADAM_EPS, ADAM_WD, ADAM_STEP = 0.001, 0.9, 0.999, 1e-08, 0.01, 10


def _cp(n_axes):
    return pltpu.CompilerParams(dimension_semantics=("arbitrary",) * n_axes, vmem_limit_bytes=VMEM_LIMIT)


def _rowmap(stride, off):
    return lambda b, i: (b * stride + off + i, 0)


def _mmap(comb):
    if comb:
        return lambda b, i: (b, jnp.minimum(i, 1), 0, 0)
    return lambda b, i: (b, 1, 0, 0)


def _ln(x):
    mu = jnp.mean(x, axis=-1, keepdims=True)
    xc = x - mu
    var = jnp.mean(xc * xc, axis=-1, keepdims=True)
    r = lax.rsqrt(var + LN_EPS)
    return xc * r, r


def _ln_bwd(dxh, xh, r):
    return r * (dxh - jnp.mean(dxh, axis=-1, keepdims=True) - xh * jnp.mean(dxh * xh, axis=-1, keepdims=True))


def _sigmoid(x):
    return 1.0 / (1.0 + jnp.exp(-x))


def _rsum(x):
    return jnp.sum(x, axis=0, keepdims=True)


def _dot(a, b):
    return jnp.dot(a, b, preferred_element_type=F32)


def _dot_nt(a, b):
    return lax.dot_general(a, b, (((1,), (1,)), ((), ())), preferred_element_type=F32)


def _dot_tn(a, b):
    return lax.dot_general(a, b, (((0,), (0,)), ((), ())), preferred_element_type=F32)


def _modulate(xv, m_ref, sub):
    xh, _ = _ln(xv)
    sh = m_ref[0, 0, 3 * sub:3 * sub + 1, :]
    sc = m_ref[0, 0, 3 * sub + 1:3 * sub + 2, :]
    return (xh * (1.0 + sh) + sc).astype(BF16)


def _embed_lnmod(x, ctx, pos, mvec):
    B, T, _ = x.shape
    nt = 1 + T // TM

    def body(x_ref, c_ref, p_ref, m_ref, o_ref, h_ref):
        i = pl.program_id(1)

        @pl.when(i == 0)
        def _():
            o_ref[...] = c_ref[0]

        @pl.when(i > 0)
        def _():
            o_ref[...] = x_ref[0] + p_ref[...]

        h_ref[...] = _modulate(o_ref[...], m_ref, 0)

    rows = pl.BlockSpec((TM, D), lambda b, i: (b * nt + i, 0))
    return pl.pallas_call(
        body, name="embed_lnmod0", grid=(B, nt),
        in_specs=[pl.BlockSpec((1, TM, D), lambda b, i: (b, jnp.maximum(i - 1, 0), 0)),
                  pl.BlockSpec((1, TM, D), lambda b, i: (b, 0, 0)),
                  pl.BlockSpec((TM, D), lambda b, i: (jnp.maximum(i - 1, 0), 0)),
                  pl.BlockSpec((1, 1, N_MOD, D), _mmap(True))],
        out_specs=[rows, rows],
        out_shape=[jax.ShapeDtypeStruct((B * nt * TM, D), F32), jax.ShapeDtypeStruct((B * nt * TM, D), BF16)],
        compiler_params=_cp(2))(x, ctx, pos, mvec)


def _lnmod0_bwd(dh, X, mvec, dres, x_shape, B, nt):
    def body(dh_ref, x_ref, m_ref, dr_ref, dx_ref, dm_ref):
        i = pl.program_id(1)
        xh, r = _ln(x_ref[...])
        sh = m_ref[0, 0, 0:1, :]
        dhv = dh_ref[...]

        @pl.when((i == 0) | (i == 1))
        def _():
            dm_ref[...] = jnp.zeros_like(dm_ref)

        dm_ref[0, 0, 0:1, :] += _rsum(dhv * xh)
        dm_ref[0, 0, 1:2, :] += _rsum(dhv)

        @pl.when(i > 0)
        def _():
            dx_ref[0] = _ln_bwd(dhv * (1.0 + sh), xh, r) + dr_ref[...]

    rows = pl.BlockSpec((TM, D), _rowmap(nt, 0))
    return pl.pallas_call(
        body, name="lnmod0_bwd", grid=(B, nt),
        in_specs=[rows, rows, pl.BlockSpec((1, 1, N_MOD, D), _mmap(True)), rows],
        out_specs=[pl.BlockSpec((1, TM, D), lambda b, i: (b, jnp.maximum(i - 1, 0), 0)),
                   pl.BlockSpec((1, 1, 2, D), _mmap(True))],
        out_shape=[jax.ShapeDtypeStruct(x_shape, F32), jax.ShapeDtypeStruct((B, 2, 2, D), F32)],
        compiler_params=_cp(2))(dh, X, mvec, dres)


def _resid_fwd(x_ref, y_ref, m_ref, gb_ref, sub, w):
    wg = w * m_ref[0, 0, 3 * sub + 2:3 * sub + 3, :]
    y = y_ref[...]
    zh, r = _ln(ALPHA * x_ref[...] + wg * y)
    return y, wg, zh, r


def _resid_grads(do, y, wg, zh, r, w, gb_ref, comb, dx_ref, dy_ref, dg_ref, dgb_ref):
    b_, i = pl.program_id(0), pl.program_id(1)
    dz = _ln_bwd(do * gb_ref[0:1, :], zh, r)
    dx_ref[...] = ALPHA * dz
    dy_ref[...] = (wg * dz).astype(BF16)

    @pl.when((b_ == 0) & (i == 0))
    def _():
        dgb_ref[...] = jnp.zeros_like(dgb_ref)

    dgb_ref[0:1, :] += _rsum(do * zh)
    dgb_ref[1:2, :] += _rsum(do)

    init = (i == 0) | (i == 1) if comb else (i == 0)

    @pl.when(init)
    def _():
        dg_ref[...] = jnp.zeros_like(dg_ref)

    dg_ref[0, 0] += w * _rsum(dz * y)


def _resid_lnmod(X, lx, Y, mvec, comb, sub, w, gb, B, nt, name):
    def body(x_ref, y_ref, m_ref, gb_ref, o_ref, h_ref):
        _, _, zh, _ = _resid_fwd(x_ref, y_ref, m_ref, gb_ref, sub, w)
        xn = zh * gb_ref[0:1, :] + gb_ref[1:2, :]
        o_ref[...] = xn
        h_ref[...] = _modulate(xn, m_ref, sub + 1)

    rows = pl.BlockSpec((TM, D), _rowmap(nt, 0))
    return pl.pallas_call(
        body, name=name, grid=(B, nt),
        in_specs=[pl.BlockSpec((TM, D), _rowmap(*lx)), rows,
                  pl.BlockSpec((1, 1, N_MOD, D), _mmap(comb)), pl.BlockSpec((2, D), lambda b, i: (0, 0))],
        out_specs=[rows, rows],
        out_shape=[jax.ShapeDtypeStruct((B * nt * TM, D), F32), jax.ShapeDtypeStruct((B * nt * TM, D), BF16)],
        compiler_params=_cp(2))(X, Y, mvec, gb)


def _resid_out_shapes(B, nt, comb):
    rows = pl.BlockSpec((TM, D), _rowmap(nt, 0))
    specs = [rows, rows, pl.BlockSpec((1, 1, 1, D), _mmap(comb)), pl.BlockSpec((2, D), lambda b, i: (0, 0))]
    shapes = [jax.ShapeDtypeStruct((B * nt * TM, D), F32), jax.ShapeDtypeStruct((B * nt * TM, D), BF16),
              jax.ShapeDtypeStruct((B, 2, 1, D), F32), jax.ShapeDtypeStruct((2, D), F32)]
    return specs, shapes


def _tail(X, Y, mvec, gb, tgt, sub, w, B, nt):
    def body(x_ref, y_ref, m_ref, gb_ref, t_ref, dx_ref, dy_ref, dg_ref, dgb_ref, l_ref):
        y, wg, zh, r = _resid_fwd(x_ref, y_ref, m_ref, gb_ref, sub, w)
        e = (zh * gb_ref[0:1, :] + gb_ref[1:2, :]) - t_ref[0]

        @pl.when((pl.program_id(0) == 0) & (pl.program_id(1) == 0))
        def _():
            l_ref[...] = jnp.zeros_like(l_ref)

        l_ref[...] += _rsum(e * e)
        _resid_grads(e * (1.0 / D), y, wg, zh, r, w, gb_ref, False, dx_ref, dy_ref, dg_ref, dgb_ref)

    rows = pl.BlockSpec((TM, D), _rowmap(nt, 0))
    specs, shapes = _resid_out_shapes(B, nt, False)
    return pl.pallas_call(
        body, name="resid2_loss_bwd", grid=(B, nt),
        in_specs=[rows, rows, pl.BlockSpec((1, 1, N_MOD, D), _mmap(False)), pl.BlockSpec((2, D), lambda b, i: (0, 0)),
                  pl.BlockSpec((1, TM, D), lambda b, i: (b, i, 0))],
        out_specs=specs + [pl.BlockSpec((1, D), lambda b, i: (0, 0))],
        out_shape=shapes + [jax.ShapeDtypeStruct((1, D), F32)],
        compiler_params=_cp(2))(X, Y, mvec, gb, tgt)


def _lnmod_resid_bwd(dh, Xi, lxi, dres, Xp, lxp, Yp, mvec, comb, sub, w, gb, B, nt, name):
    ntl = nt - 1 if comb else nt

    def body(dh_ref, xi_ref, dr_ref, xp_ref, yp_ref, m_ref, gb_ref, dx_ref, dy_ref, dg_ref, dgb_ref, dm_ref):
        i = pl.program_id(1)
        xh, r = _ln(xi_ref[...])
        sh = m_ref[0, 0, 3 * sub:3 * sub + 1, :]
        dhv = dh_ref[...]
        dr = dr_ref[...]
        if comb:
            dr = jnp.where(i > 0, dr, 0.0)
        do = _ln_bwd(dhv * (1.0 + sh), xh, r) + dr

        init = (i == 0) | (i == 1) if comb else (i == 0)

        @pl.when(init)
        def _():
            dm_ref[...] = jnp.zeros_like(dm_ref)

        dm_ref[0, 0, 0:1, :] += _rsum(dhv * xh)
        dm_ref[0, 0, 1:2, :] += _rsum(dhv)
        y, wg, zh, r2 = _resid_fwd(xp_ref, yp_ref, m_ref, gb_ref, sub - 1, w)
        _resid_grads(do, y, wg, zh, r2, w, gb_ref, comb, dx_ref, dy_ref, dg_ref, dgb_ref)

    rows = pl.BlockSpec((TM, D), _rowmap(nt, 0))
    if comb:
        dres_spec = pl.BlockSpec((TM, D), lambda b, i: (b * ntl + jnp.maximum(i - 1, 0), 0))
    else:
        dres_spec = rows
    specs, shapes = _resid_out_shapes(B, nt, comb)
    return pl.pallas_call(
        body, name=name, grid=(B, nt),
        in_specs=[rows, pl.BlockSpec((TM, D), _rowmap(*lxi)), dres_spec, pl.BlockSpec((TM, D), _rowmap(*lxp)), rows,
                  pl.BlockSpec((1, 1, N_MOD, D), _mmap(comb)), pl.BlockSpec((2, D), lambda b, i: (0, 0))],
        out_specs=specs + [pl.BlockSpec((1, 1, 2, D), _mmap(comb))],
        out_shape=shapes + [jax.ShapeDtypeStruct((B, 2, 2, D), F32)],
        compiler_params=_cp(2))(dh, Xi, dres, Xp, Yp, mvec, gb)


def _ffn_out_dx(dy, W, u, name):
    M = dy.shape[0]
    half = DFF // 2

    def body(dy_ref, w_ref, u_ref, du_ref):
        dyv = dy_ref[...]
        for j in range(2):
            lo, hi = j * half, (j + 1) * half
            da = _dot_nt(dyv, w_ref[lo:hi, :])
            g = u_ref[:, lo:hi].astype(F32)
            up = u_ref[:, DFF + lo:DFF + hi].astype(F32)
            s = _sigmoid(g)
            du_ref[:, lo:hi] = (da * up * (s * (1.0 + g * (1.0 - s)))).astype(BF16)
            du_ref[:, DFF + lo:DFF + hi] = (da * (g * s)).astype(BF16)

    return pl.pallas_call(
        body, name=name, grid=(M // TM,),
        in_specs=[pl.BlockSpec((TM, D), lambda i: (i, 0)), _wspec(W, 1), pl.BlockSpec((TM, 2 * DFF), lambda i: (i, 0))],
        out_specs=pl.BlockSpec((TM, 2 * DFF), lambda i: (i, 0)),
        out_shape=jax.ShapeDtypeStruct((M, 2 * DFF), BF16), compiler_params=_cp(1))(dy, W, u)


def _wspec(W, nidx):
    zeros = (0,) * W.ndim
    if nidx == 1:
        return pl.BlockSpec(W.shape, lambda i: zeros)
    return pl.BlockSpec(W.shape, lambda b, i: zeros)


def _mm_nn(A, la, W, B, nt, out_dtype, name):
    K, N = W.shape

    def body(a_ref, w_ref, o_ref):
        o_ref[...] = _dot(a_ref[...], w_ref[...]).astype(out_dtype)

    return pl.pallas_call(
        body, name=name, grid=(B, nt),
        in_specs=[pl.BlockSpec((TM, K), _rowmap(*la)), _wspec(W, 2)],
        out_specs=pl.BlockSpec((TM, N), _rowmap(nt, 0)),
        out_shape=jax.ShapeDtypeStruct((B * nt * TM, N), out_dtype), compiler_params=_cp(2))(A, W)


def _ffn_in(A, la, W3, B, nt, name):
    K, n = W3.shape[1:]

    def body(a_ref, w_ref, u_ref, s_ref):
        a = a_ref[...]
        for j in range(2):
            g = _dot(a, w_ref[j])
            up = _dot(a, w_ref[j + 2])
            u_ref[:, j * n:(j + 1) * n] = g.astype(BF16)
            u_ref[:, (j + 2) * n:(j + 3) * n] = up.astype(BF16)
            s_ref[:, j * n:(j + 1) * n] = (g * _sigmoid(g) * up).astype(BF16)

    rows = B * nt * TM
    return pl.pallas_call(
        body, name=name, grid=(B, nt),
        in_specs=[pl.BlockSpec((TM, K), _rowmap(*la)), _wspec(W3, 2)],
        out_specs=[pl.BlockSpec((TM, 4 * n), _rowmap(nt, 0)), pl.BlockSpec((TM, 2 * n), _rowmap(nt, 0))],
        out_shape=[jax.ShapeDtypeStruct((rows, 4 * n), BF16), jax.ShapeDtypeStruct((rows, 2 * n), BF16)],
        compiler_params=_cp(2))(A, W3)


def _mm_nt(A, W, name):
    M, N = A.shape
    K = W.shape[-2]

    def body(a_ref, w_ref, o_ref):
        if W.ndim == 3:
            n = W.shape[-1]
            acc = _dot_nt(a_ref[:, 0:n], w_ref[0])
            for j in range(1, 4):
                acc = acc + _dot_nt(a_ref[:, j * n:(j + 1) * n], w_ref[j])
            o_ref[...] = acc
        else:
            o_ref[...] = _dot_nt(a_ref[...], w_ref[...])

    return pl.pallas_call(
        body, name=name, grid=(M // TM,),
        in_specs=[pl.BlockSpec((TM, N), lambda i: (i, 0)), _wspec(W, 1)],
        out_specs=pl.BlockSpec((TM, K), lambda i: (i, 0)),
        out_shape=jax.ShapeDtypeStruct((M, K), F32), compiler_params=_cp(1))(A, W)


def _mm_tn(A, G, name, tk=512, tn=512, shards=None):
    M, K = A.shape
    N = G.shape[1]
    if shards:
        tn = N // shards

    def body(a_ref, g_ref, o_ref):
        @pl.when(pl.program_id(1) == 0)
        def _():
            o_ref[...] = jnp.zeros_like(o_ref)

        upd = _dot_tn(a_ref[...], g_ref[...])
        if shards:
            o_ref[0] += upd
        else:
            o_ref[...] += upd

    if shards:
        out_spec = pl.BlockSpec((1, K, tn), lambda n, k: (n, 0, 0))
        out_shape = jax.ShapeDtypeStruct((shards, K, tn), F32)
    else:
        out_spec = pl.BlockSpec((K, tn), lambda n, k: (0, n))
        out_shape = jax.ShapeDtypeStruct((K, N), F32)
    return pl.pallas_call(
        body, name=name, grid=(N // tn, M // tk),
        in_specs=[pl.BlockSpec((tk, K), lambda n, k: (k, 0)), pl.BlockSpec((tk, tn), lambda n, k: (k, n))],
        out_specs=out_spec, out_shape=out_shape, compiler_params=_cp(2))(A, G)


def _logsig(z):
    return jnp.minimum(z, 0.0) - jnp.log(1.0 + jnp.exp(-jnp.abs(z)))


def _features(F, a2p, biasp, lbp, B):
    nt = F.shape[0] // (B * TM)
    nj = nt + 1
    R2 = nj * TM

    def body(f_ref, a2_ref, bias_ref, lb_ref, q_ref, k_ref, v_ref, g_ref):
        lr = f_ref[:, LR:LR + HD].astype(BF16)
        lane = lax.broadcasted_iota(jnp.int32, (1, 4 * HD), 1)
        keep = (lane & (HD - 1)) < GLA_DK
        for d in range(2):
            z = _dot(lr, a2_ref[d]) + bias_ref[d:d + 1, :]
            gl = jnp.where(keep, _logsig(z) * (1.0 / GATE_NORM), 0.0)
            for h in range(4):
                g_ref[d, 0, h] = gl[:, h * HD:(h + 1) * HD]
        for h in range(4):
            q_ref[0, h] = f_ref[:, GQ + h * HD:GQ + (h + 1) * HD] * (GLA_DK ** -0.5)
            kk = f_ref[:, GK + h * HD:GK + (h + 1) * HD]
            k_ref[0, 0, h] = kk
            k_ref[1, 0, h] = kk
            v_ref[0, h] = f_ref[:, GV + h * HD:GV + (h + 1) * HD].astype(BF16)
        for h in range(4):
            sl = slice(h * HD, (h + 1) * HD)
            rq = f_ref[:, RQ + h * HD:RQ + (h + 1) * HD]
            q_ref[0, 4 + h] = rq * _sigmoid(rq) * (HD ** -0.5)
            v_ref[0, 4 + h] = f_ref[:, RI + h * HD:RI + (h + 1) * HD].astype(BF16)
            for d, off in ((0, RFF), (1, RFB)):
                lb = lb_ref[d:d + 1, sl]
                f = lb + (1.0 - lb) * _sigmoid(f_ref[:, off + h * HD:off + (h + 1) * HD])
                g_ref[d, 0, 4 + h] = jnp.log(f)
                k_ref[d, 0, 4 + h] = 1.0 - f

    fmap = lambda b, j: (b * nt + jnp.where(j == nt, 0, j), 0)
    one = pl.BlockSpec((1, NH, TM, HD), lambda b, j: (b, 0, j, 0))
    two = pl.BlockSpec((2, 1, NH, TM, HD), lambda b, j: (0, b, 0, j, 0))
    s1 = jax.ShapeDtypeStruct((B, NH, R2, HD), F32)
    s2 = jax.ShapeDtypeStruct((2, B, NH, R2, HD), F32)
    return pl.pallas_call(
        body, name="mix_features", grid=(B, nj),
        in_specs=[pl.BlockSpec((TM, MIXP), fmap), pl.BlockSpec((2, HD, 4 * HD), lambda b, j: (0, 0, 0)),
                  pl.BlockSpec((2, 4 * HD), lambda b, j: (0, 0)), pl.BlockSpec((2, 4 * HD), lambda b, j: (0, 0))],
        out_specs=[one, two, one, two], out_shape=[s1, s2, jax.ShapeDtypeStruct(s1.shape, BF16), s2],
        compiler_params=_cp(2))(F, a2p, biasp, lbp)


def _features_bwd(F, a2p, biasp, lbp, dQ0, dQ1, dK0, dK1, dV0, dV1, dG0, dG1, dgates, B):
    nt = F.shape[0] // (B * TM)

    def body(f_ref, a2_ref, bias_ref, lb_ref, dq0, dq1, dk0, dk1, dv0, dv1, dg0, dg1, dgt_ref,
             df_ref, da2_ref, dbias_ref, dlb_ref):
        b_, i = pl.program_id(0), pl.program_id(1)

        @pl.when((b_ == 0) & (i == 0))
        def _():
            da2_ref[...] = jnp.zeros_like(da2_ref)
            dbias_ref[...] = jnp.zeros_like(dbias_ref)
            dlb_ref[...] = jnp.zeros_like(dlb_ref)

        df_ref[:, 0:2 * 4 * HD] = jnp.where(i > 0, dgt_ref[...], 0.0).astype(BF16)
        df_ref[:, LR + HD:] = jnp.zeros((TM, MIXP - LR - HD), BF16)
        lr = f_ref[:, LR:LR + HD].astype(BF16)
        lane = lax.broadcasted_iota(jnp.int32, (1, 4 * HD), 1)
        keep = (lane & (HD - 1)) < GLA_DK
        dlr = jnp.zeros((TM, HD), F32)
        dgs = (dg0, dg1)
        dks = (dk0, dk1)
        for d in range(2):
            z = _dot(lr, a2_ref[d]) + bias_ref[d:d + 1, :]
            dgl = jnp.concatenate([dgs[d][0, h] for h in range(4)], axis=1)
            dz = jnp.where(keep, dgl * (1.0 / GATE_NORM) * (1.0 - _sigmoid(z)), 0.0)
            dzb = dz.astype(BF16)
            dlr = dlr + _dot_nt(dzb, a2_ref[d])
            da2_ref[d] += _dot_tn(lr, dzb)
            dbias_ref[d:d + 1, :] += _rsum(dz)
        df_ref[:, LR:LR + HD] = dlr.astype(BF16)
        for h in range(4):
            df_ref[:, GQ + h * HD:GQ + (h + 1) * HD] = ((dq0[0, h] + dq1[0, h]) * (GLA_DK ** -0.5)).astype(BF16)
            df_ref[:, GK + h * HD:GK + (h + 1) * HD] = (dk0[0, h] + dk1[0, h]).astype(BF16)
            df_ref[:, GV + h * HD:GV + (h + 1) * HD] = (dv0[0, h] + dv1[0, h]).astype(BF16)
        for h in range(4):
            sl = slice(h * HD, (h + 1) * HD)
            rq = f_ref[:, RQ + h * HD:RQ + (h + 1) * HD]
            s = _sigmoid(rq)
            dqh = dq0[0, 4 + h] + dq1[0, 4 + h]
            df_ref[:, RQ + h * HD:RQ + (h + 1) * HD] = (dqh * (HD ** -0.5) * (s * (1.0 + rq * (1.0 - s)))).astype(BF16)
            df_ref[:, RI + h * HD:RI + (h + 1) * HD] = (dv0[0, 4 + h] + dv1[0, 4 + h]).astype(BF16)
            for d, off in ((0, RFF), (1, RFB)):
                lb = lb_ref[d:d + 1, sl]
                sg = _sigmoid(f_ref[:, off + h * HD:off + (h + 1) * HD])
                f = lb + (1.0 - lb) * sg
                dff = dgs[d][0, 4 + h] / f - dks[d][0, 4 + h]
                df_ref[:, off + h * HD:off + (h + 1) * HD] = (dff * (1.0 - lb) * sg * (1.0 - sg)).astype(BF16)
                dlb_ref[d:d + 1, sl] += _rsum(dff * (1.0 - sg))

    m0 = lambda b, i: (b, 0, i, 0)
    m1 = lambda b, i: (b, 0, jnp.where(i == 0, nt, i), 0)
    one = lambda m: pl.BlockSpec((1, NH, TM, HD), m)
    return pl.pallas_call(
        body, name="mix_features_bwd", grid=(B, nt),
        in_specs=[pl.BlockSpec((TM, MIXP), _rowmap(nt, 0)), pl.BlockSpec((2, HD, 4 * HD), lambda b, i: (0, 0, 0)),
                  pl.BlockSpec((2, 4 * HD), lambda b, i: (0, 0)), pl.BlockSpec((2, 4 * HD), lambda b, i: (0, 0)),
                  one(m0), one(m1), one(m0), one(m1), one(m0), one(m1), one(m0), one(m1),
                  pl.BlockSpec((TM, D), lambda b, i: (b * (nt - 1) + jnp.maximum(i - 1, 0), 0))],
        out_specs=[pl.BlockSpec((TM, MIXP), _rowmap(nt, 0)), pl.BlockSpec((2, HD, 4 * HD), lambda b, i: (0, 0, 0)),
                   pl.BlockSpec((2, 4 * HD), lambda b, i: (0, 0)), pl.BlockSpec((2, 4 * HD), lambda b, i: (0, 0))],
        out_shape=[jax.ShapeDtypeStruct((B * nt * TM, MIXP), BF16), jax.ShapeDtypeStruct((2, HD, 4 * HD), F32),
                   jax.ShapeDtypeStruct((2, 4 * HD), F32), jax.ShapeDtypeStruct((2, 4 * HD), F32)],
        compiler_params=_cp(2))(F, a2p, biasp, lbp, dQ0, dQ1, dK0, dK1, dV0, dV1, dG0, dG1, dgates)


def _chunk_scan(x, rin, fwd, inclusive=True):
    acc = x
    sft = 1
    while sft < CH:
        if fwd:
            acc = acc + jnp.where(rin >= sft, pltpu.roll(acc, sft, 0), 0.0)
        else:
            acc = acc + jnp.where(rin < CH - sft, pltpu.roll(acc, TM - sft, 0), 0.0)
        sft *= 2
    return acc if inclusive else acc - x


def _scan_common(q, k, v, g, rev):
    rin = lax.broadcasted_iota(jnp.int32, (TM, HD), 0) & (CH - 1)
    b = _chunk_scan(g, rin, not rev)
    xx = _chunk_scan(g, rin, rev, inclusive=False)
    eb = jnp.exp(b)
    qd = q * eb
    ki = k * jnp.exp(-b)
    kt = k * jnp.exp(xx)
    ri = lax.broadcasted_iota(jnp.int32, (SB, SB), 0)
    ci = lax.broadcasted_iota(jnp.int32, (SB, SB), 1)
    same = (ri >> 5) == (ci >> 5)
    lo = same & (ri >= ci)
    up = same & (ri <= ci)
    mask, maskT = (up, lo) if rev else (lo, up)
    re = lax.broadcasted_iota(jnp.int32, (SB, CSB * HD), 0) >> 5
    ce = lax.broadcasted_iota(jnp.int32, (SB, CSB * HD), 1) >> 7
    mexp = re == ce
    return rin, b, xx, eb, qd, ki, kt, mask, maskT, mexp


def _sub(x, s):
    return x[s * SB:(s + 1) * SB]


def _expand(xb, mexp):
    return jnp.where(mexp, jnp.concatenate([xb] * CSB, axis=1), jnp.zeros((), xb.dtype))


def _own(x, mexp):
    xm = jnp.where(mexp, x, 0.0)
    acc = xm[:, 0:HD]
    for n in range(1, CSB):
        acc = acc + xm[:, n * HD:(n + 1) * HD]
    return acc


def _stack(per_chunk, s):
    return jnp.concatenate(per_chunk[s * CSB:(s + 1) * CSB], axis=1)


def _state_pass(s0, eb, uts, rev):
    order = range(NCB - 1, -1, -1) if rev else range(NCB)
    states = [None] * NCB
    s = s0
    for n in order:
        row = n * CH if rev else n * CH + CH - 1
        states[n] = s
        s = eb[row:row + 1, :] * s + uts[n // CSB][:, (n % CSB) * HD:(n % CSB + 1) * HD]
    return states, s


def _scan_fwd(Q, K, V, G, rev, B):
    nb = Q.shape[2] // TM - 1
    d = 1 if rev else 0
    rmap = (lambda s: nb - s) if rev else (lambda s: s)

    def body(q_ref, k_ref, v_ref, g_ref, o_ref, st_ref, s_scr):
        @pl.when(pl.program_id(2) == 0)
        def _():
            s_scr[...] = jnp.zeros_like(s_scr)

        for p in range(HP):
            s0 = s_scr[p]
            st_ref[0, p, 0] = s0
            v = v_ref[0, p]
            _, _, _, eb, qd, ki, kt, mask, _, mexp = _scan_common(q_ref[0, p], k_ref[0, 0, p], v, g_ref[0, 0, p], rev)
            qb, kib, ktb, vb = qd.astype(BF16), ki.astype(BF16), kt.astype(BF16), v.astype(BF16)
            uts = [_dot_tn(_sub(vb, s), _expand(_sub(ktb, s), mexp)) for s in range(NSB)]
            states, s_new = _state_pass(s0, eb, uts, rev)
            s_scr[p] = s_new
            for s in range(NSB):
                a = jnp.where(mask, _dot_nt(_sub(qb, s), _sub(kib, s)), 0.0)
                o_ref[0, p, s * SB:(s + 1) * SB, :] = (
                    _dot(a.astype(BF16), _sub(vb, s))
                    + _dot_nt(_expand(_sub(qb, s), mexp), _stack(states, s).astype(BF16)))

    one = pl.BlockSpec((1, HP, TM, HD), lambda b, h, s: (b, h, rmap(s), 0))
    two = pl.BlockSpec((1, 1, HP, TM, HD), lambda b, h, s: (d, b, h, rmap(s), 0))
    return pl.pallas_call(
        body, name="scan_fwd_rev" if rev else "scan_fwd", grid=(B, NH // HP, nb),
        in_specs=[one, two, one, two],
        out_specs=[one, pl.BlockSpec((1, HP, 1, HD, HD), lambda b, h, s: (b, h, s, 0, 0))],
        out_shape=[jax.ShapeDtypeStruct(Q.shape, F32), jax.ShapeDtypeStruct((B, NH, nb, HD, HD), F32)],
        scratch_shapes=[pltpu.VMEM((HP, HD, HD), F32)],
        compiler_params=_cp(3))(Q, K, V, G)


def _scan_bwd(Q, K, V, G, St, dO, rev, B):
    nb = Q.shape[2] // TM - 1
    d = 1 if rev else 0
    smap = lambda t: nb - 1 - t
    rmap = (lambda t: nb - smap(t)) if rev else smap

    def body(q_ref, k_ref, v_ref, g_ref, st_ref, do_ref, dq_ref, dk_ref, dv_ref, dg_ref, ds_scr):
        t = pl.program_id(2)

        @pl.when(t == 0)
        def _():
            ds_scr[...] = jnp.zeros_like(ds_scr)

        is_lat = smap(t) >= 1
        for p in range(HP):
            v = v_ref[0, p]
            rin, b, xx, eb, qd, ki, kt, mask, maskT, mexp = _scan_common(
                q_ref[0, p], k_ref[0, 0, p], v, g_ref[0, 0, p], rev)
            qb, kib, ktb, vb = qd.astype(BF16), ki.astype(BF16), kt.astype(BF16), v.astype(BF16)
            dob = jnp.where(is_lat, do_ref[0, p], 0.0).astype(BF16)
            kt_exps = [_expand(_sub(ktb, s), mexp) for s in range(NSB)]
            uts = [_dot_tn(_sub(vb, s), kt_exps[s]) for s in range(NSB)]
            states, _ = _state_pass(st_ref[0, p, 0], eb, uts, rev)
            gts = [_dot_tn(_sub(dob, s), _expand(_sub(qb, s), mexp)) for s in range(NSB)]
            order = range(NCB) if rev else range(NCB - 1, -1, -1)
            dsp = [None] * NCB
            t2 = [None] * NCB
            dsc = ds_scr[p]
            for n in order:
                row = n * CH if rev else n * CH + CH - 1
                ebl = eb[row:row + 1, :]
                dsp[n] = dsc
                t2[n] = jnp.broadcast_to(ebl * _rsum(states[n] * dsc), (CH, HD))
                dsc = gts[n // CSB][:, (n % CSB) * HD:(n % CSB + 1) * HD] + ebl * dsc
            ds_scr[p] = dsc
            dqds, dkis, dkts = [], [], []
            for s in range(NSB):
                q_s, ki_s, v_s, do_s = _sub(qb, s), _sub(kib, s), _sub(vb, s), _sub(dob, s)
                dspb = _stack(dsp, s).astype(BF16)
                da = jnp.where(mask, _dot_nt(do_s, v_s), 0.0).astype(BF16)
                dat = jnp.where(maskT, _dot_nt(v_s, do_s), 0.0).astype(BF16)
                at = jnp.where(maskT, _dot_nt(ki_s, q_s), 0.0).astype(BF16)
                dqds.append(_dot(da, ki_s) + _own(_dot(do_s, _stack(states, s).astype(BF16)), mexp))
                dkis.append(_dot(dat, q_s))
                dv_ref[0, p, s * SB:(s + 1) * SB, :] = _dot(at, do_s) + _dot_nt(kt_exps[s], dspb)
                dkts.append(_own(_dot(v_s, dspb), mexp))
            dqd, dki, dkt = (jnp.concatenate(parts, axis=0) for parts in (dqds, dkis, dkts))
            z = dkt * kt
            db = dqd * qd - dki * ki
            dq_ref[0, p] = dqd * eb
            dk_ref[0, p] = dki * jnp.exp(-b) + dkt * jnp.exp(xx)
            dg_ref[0, p] = (_chunk_scan(db, rin, rev) + _chunk_scan(z, rin, not rev, inclusive=False)
                            + jnp.concatenate(t2, axis=0))

    one = pl.BlockSpec((1, HP, TM, HD), lambda b, h, t: (b, h, rmap(t), 0))
    two = pl.BlockSpec((1, 1, HP, TM, HD), lambda b, h, t: (d, b, h, rmap(t), 0))
    lat = pl.BlockSpec((1, HP, TM, HD), lambda b, h, t: (b, h, jnp.clip(rmap(t) - 1, 0, nb - 2), 0))
    shp = jax.ShapeDtypeStruct(Q.shape, F32)
    return pl.pallas_call(
        body, name="scan_bwd_rev" if rev else "scan_bwd", grid=(B, NH // HP, nb),
        in_specs=[one, two, one, two, pl.BlockSpec((1, HP, 1, HD, HD), lambda b, h, t: (b, h, smap(t), 0, 0)), lat],
        out_specs=[one, one, one, one], out_shape=[shp, shp, shp, shp],
        scratch_shapes=[pltpu.VMEM((HP, HD, HD), F32)],
        compiler_params=_cp(3))(Q, K, V, G, St, dO)


def _gnorm(O0, O1, F, gains, B, ntl):
    nt = ntl + 1

    def body(o0_ref, o1_ref, f_ref, gn_ref, m_ref):
        for h in range(NH):
            o = o0_ref[0, h] + o1_ref[0, h]
            r = lax.rsqrt(jnp.mean(o * o, axis=-1, keepdims=True) + NORM_EPS)
            gn = gn_ref[0:1, :] if h < 4 else gn_ref[1:2, :]
            gt = f_ref[:, h * HD:(h + 1) * HD]
            m_ref[:, h * HD:(h + 1) * HD] = (o * r * gn * (gt * _sigmoid(gt))).astype(BF16)

    ospec = pl.BlockSpec((1, NH, TM, HD), lambda b, i: (b, 0, i + 1, 0))
    return pl.pallas_call(
        body, name="gated_norm", grid=(B, ntl),
        in_specs=[ospec, ospec, pl.BlockSpec((TM, D), lambda b, i: (b * nt + 1 + i, 0)),
                  pl.BlockSpec((2, HD), lambda b, i: (0, 0))],
        out_specs=pl.BlockSpec((TM, D), _rowmap(ntl, 0)),
        out_shape=jax.ShapeDtypeStruct((B * ntl * TM, D), BF16), compiler_params=_cp(2))(O0, O1, F, gains)


def _gnorm_bwd(dM, O0, O1, F, gains, B, ntl):
    nt = ntl + 1

    def body(dm_ref, o0_ref, o1_ref, f_ref, gn_ref, do_ref, dgt_ref, dgn_ref):
        b_, i = pl.program_id(0), pl.program_id(1)

        @pl.when((b_ == 0) & (i == 0))
        def _():
            dgn_ref[...] = jnp.zeros_like(dgn_ref)

        for h in range(NH):
            o = o0_ref[0, h] + o1_ref[0, h]
            r = lax.rsqrt(jnp.mean(o * o, axis=-1, keepdims=True) + NORM_EPS)
            y = o * r
            gn = gn_ref[0:1, :] if h < 4 else gn_ref[1:2, :]
            gt = f_ref[:, h * HD:(h + 1) * HD]
            s = _sigmoid(gt)
            dm = dm_ref[:, h * HD:(h + 1) * HD]
            don = dm * (gt * s)
            dgt_ref[:, h * HD:(h + 1) * HD] = dm * (y * gn) * (s * (1.0 + gt * (1.0 - s)))
            row = 0 if h < 4 else 1
            dgn_ref[row:row + 1, :] += _rsum(don * y)
            dy = don * gn
            do_ref[0, h] = r * (dy - y * jnp.mean(dy * y, axis=-1, keepdims=True))

    ospec = pl.BlockSpec((1, NH, TM, HD), lambda b, i: (b, 0, i + 1, 0))
    return pl.pallas_call(
        body, name="gated_norm_bwd", grid=(B, ntl),
        in_specs=[pl.BlockSpec((TM, D), _rowmap(ntl, 0)), ospec, ospec,
                  pl.BlockSpec((TM, D), lambda b, i: (b * nt + 1 + i, 0)), pl.BlockSpec((2, HD), lambda b, i: (0, 0))],
        out_specs=[pl.BlockSpec((1, NH, TM, HD), lambda b, i: (b, 0, i, 0)), pl.BlockSpec((TM, D), _rowmap(ntl, 0)),
                   pl.BlockSpec((2, HD), lambda b, i: (0, 0))],
        out_shape=[jax.ShapeDtypeStruct((B, NH, ntl * TM, HD), F32), jax.ShapeDtypeStruct((B * ntl * TM, D), F32),
                   jax.ShapeDtypeStruct((2, HD), F32)],
        compiler_params=_cp(2))(dM, O0, O1, F, gains)


def _sincos_2d(rows, width, dim):
    r = jnp.repeat(jnp.arange(rows), width)
    col = jnp.tile(jnp.arange(width), rows)
    quarter = dim // 4
    omega = 1.0 / 10000.0 ** (jnp.arange(quarter, dtype=F32) / quarter)

    def emb(p):
        a = p.astype(F32)[:, None] * omega[None, :]
        return jnp.concatenate([jnp.sin(a), jnp.cos(a)], axis=-1)

    return jnp.concatenate([emb(r), emb(col)], axis=-1)


def _pad_heads(w):
    k = w.shape[0]
    return jnp.pad(w.reshape(k, 4, GLA_DK), ((0, 0), (0, 0), (0, HD - GLA_DK))).reshape(k, 4 * HD)


def _unpad_heads(w):
    k = w.shape[0]
    return w.reshape(k, 4, HD)[:, :, :GLA_DK].reshape(k, 4 * GLA_DK)


MIX_N = 1032
MIX_NP = 1152
_SEGS = ([(64 * h, 64, GQ + HD * h) for h in range(4)] + [(256 + 64 * h, 64, GK + HD * h) for h in range(4)]
         + [(512, 512, GV), (1024, 512, GG), (1536, 32, LR), (1568, 512, RQ), (2080, 512, RFF), (2592, 512, RFB),
            (3104, 512, RI), (3616, 512, RG)])


def _mix_in_to_padded(ps):
    k = ps.shape[1]
    parts, pos = [], 0
    for g0, ln, s0 in sorted(_SEGS, key=lambda s: s[2]):
        if s0 > pos:
            parts.append(jnp.zeros((k, s0 - pos), ps.dtype))
        for j in range(4):
            lo, hi = max(g0, j * MIX_N), min(g0 + ln, (j + 1) * MIX_N)
            if lo < hi:
                parts.append(ps[j][:, lo - j * MIX_N:hi - j * MIX_N])
        pos = s0 + ln
    parts.append(jnp.zeros((k, MIXP - pos), ps.dtype))
    return jnp.concatenate(parts, axis=1)


def _mix_in_from_padded(g):
    k = g.shape[0]
    shards = []
    for j in range(4):
        parts = []
        for g0, ln, s0 in sorted(_SEGS):
            lo, hi = max(g0, j * MIX_N), min(g0 + ln, (j + 1) * MIX_N)
            if lo < hi:
                parts.append(g[:, s0 + lo - g0:s0 + hi - g0])
        parts.append(jnp.zeros((k, MIX_NP - MIX_N), g.dtype))
        shards.append(jnp.concatenate(parts, axis=1))
    return jnp.stack(shards)


def _local_step(x, ctx, tgt, mvec, w1i, w1o, wmp, wmo, w2i, w2o, ln_gain, ln_bias, a2f, a2b, abf, abb, lb, gng, gnh):
    B, T, _ = x.shape
    assert ctx.shape[1] == TM and T % TM == 0
    ntl = T // TM
    nt = ntl + 1
    C, L, CL = (nt, 0), (ntl, 0), (nt, 1)
    pos = _sincos_2d(T // 64, 64, D)
    gbs = [jnp.stack([ln_gain[i], ln_bias[i]]) for i in range(3)]
    a2p = jnp.zeros((2, HD, 4 * HD), F32)
    a2p = a2p.at[0, 0:16].set(_pad_heads(a2f)).at[1, 16:32].set(_pad_heads(a2b)).astype(BF16)
    biasp = jnp.concatenate([_pad_heads(abf.reshape(1, -1)), _pad_heads(abb.reshape(1, -1))], axis=0)
    gains = jnp.concatenate([gng.reshape(1, HD), gnh.reshape(1, HD)], axis=0)

    X0, h0 = _embed_lnmod(x, ctx, pos, mvec)
    u0, a0 = _ffn_in(h0, C, w1i, B, nt, "ffn1_in")
    y0 = _mm_nn(a0, C, w1o, B, nt, F32, "ffn1_out")
    X1, h1 = _resid_lnmod(X0, C, y0, mvec, True, 0, 0.5, gbs[0], B, nt, "resid0_lnmod1")
    Fm = _mm_nn(h1, C, wmp, B, nt, F32, "mix_in")
    Q, K, V, G = _features(Fm, a2p, biasp, lb, B)
    O0, S0 = _scan_fwd(Q, K, V, G, False, B)
    O1, S1 = _scan_fwd(Q, K, V, G, True, B)
    merged = _gnorm(O0, O1, Fm, gains, B, ntl)
    y1 = _mm_nn(merged, L, wmo, B, ntl, F32, "mix_out")
    X2, h2 = _resid_lnmod(X1, CL, y1, mvec, False, 1, 1.0, gbs[1], B, ntl, "resid1_lnmod2")
    u2, a2 = _ffn_in(h2, L, w2i, B, ntl, "ffn2_in")
    y2 = _mm_nn(a2, L, w2o, B, ntl, F32, "ffn2_out")

    dx2r, dy2, dgate2, dgb2, lsum = _tail(X2, y2, mvec, gbs[2], tgt, 2, 0.5, B, ntl)
    loss = (0.5 / D) * jnp.sum(lsum)
    du2 = _ffn_out_dx(dy2, w2o, u2, "ffn2_out_dx")
    g_w2o = _mm_tn(a2, dy2, "ffn2_out_dw")
    dh2 = _mm_nt(du2, w2i, "ffn2_in_dx")
    g_w2i = _mm_tn(h2, du2, "ffn2_in_dw", shards=4)
    dx1r, dy1, dgate1, dgb1, dss2 = _lnmod_resid_bwd(dh2, X2, L, dx2r, X1, CL, y1, mvec, False, 2, 1.0, gbs[1],
                                                     B, ntl, "lnmod2_resid1_bwd")
    dmerged = _mm_nt(dy1, wmo, "mix_out_dx")
    g_wmo = _mm_tn(merged, dy1, "mix_out_dw")
    dO, dgates, dgains = _gnorm_bwd(dmerged, O0, O1, Fm, gains, B, ntl)
    dQ0, dK0, dV0, dG0 = _scan_bwd(Q, K, V, G, S0, dO, False, B)
    dQ1, dK1, dV1, dG1 = _scan_bwd(Q, K, V, G, S1, dO, True, B)
    dF, da2p, dbiasp, dlb = _features_bwd(Fm, a2p, biasp, lb, dQ0, dQ1, dK0, dK1, dV0, dV1, dG0, dG1, dgates, B)
    dh1 = _mm_nt(dF, wmp, "mix_in_dx")
    g_wmp = _mm_tn(h1, dF, "mix_in_dw", tn=MIXP // 4)
    dx0r, dy0, dgate0, dgb0, dss1 = _lnmod_resid_bwd(dh1, X1, C, dx1r, X0, C, y0, mvec, True, 1, 0.5, gbs[0],
                                                     B, nt, "lnmod1_resid0_bwd")
    du0 = _ffn_out_dx(dy0, w1o, u0, "ffn1_out_dx")
    g_w1o = _mm_tn(a0, dy0, "ffn1_out_dw")
    dh0 = _mm_nt(du0, w1i, "ffn1_in_dx")
    g_w1i = _mm_tn(h0, du0, "ffn1_in_dw", shards=4)
    grad_x, dss0 = _lnmod0_bwd(dh0, X0, mvec, dx0r, x.shape, B, nt)

    zero_ctx = lambda a: a.at[:, 0].set(0.0)
    dm = jnp.concatenate([dss0, dgate0, dss1, zero_ctx(dgate1), zero_ctx(dss2), zero_ctx(dgate2)], axis=2)
    small = dict(
        ln_gain=jnp.stack([dgb0[0], dgb1[0], dgb2[0]]), ln_bias=jnp.stack([dgb0[1], dgb1[1], dgb2[1]]),
        a2f=_unpad_heads(da2p[0, 0:16]), a2b=_unpad_heads(da2p[1, 16:32]),
        abf=_unpad_heads(dbiasp[0:1]), abb=_unpad_heads(dbiasp[1:2]), lb=dlb, gng=dgains[0], gnh=dgains[1])
    big = dict(w1i=g_w1i, w1o=g_w1o, wmp=g_wmp, wmo=g_wmo, w2i=g_w2i, w2o=g_w2o)
    return loss, grad_x, big, dm, small


def _small_allgather(xs, name):
    r, n = xs.shape

    def body(x_ref, out_ref, send_sems, recv_sems, local_sem):
        x, y, c = lax.axis_index("x"), lax.axis_index("y"), lax.axis_index("c")
        me, sibling = (x, y, c), (x, y, 1 - c)
        chips = [(1 - x, y), (x, 1 - y), (1 - x, 1 - y)]

        def rows(px, py, pc):
            return out_ref.at[pl.ds((4 * px + 2 * py + pc) * r, r), :]

        def copy(k, block, to, src=None):
            return pltpu.make_async_remote_copy(
                src_ref=rows(*block) if src is None else src, dst_ref=rows(*block),
                send_sem=send_sems.at[k], recv_sem=recv_sems.at[k], device_id=to, device_id_type=MESH)

        mine = pltpu.make_async_copy(x_ref, rows(*me), local_sem)
        mine.start()
        first = [copy(0, me, sibling, src=x_ref)]
        first += [copy(1 + j, me, (*chip, c), src=x_ref) for j, chip in enumerate(chips)]
        for cp in first:
            cp.start()
        passed = [copy(4 + j, (*chip, c), sibling) for j, chip in enumerate(chips)]
        for j, chip in enumerate(chips):
            copy(1 + j, (*chip, c), me).wait_recv()
            passed[j].start()
        copy(0, sibling, me).wait_recv()
        for j, chip in enumerate(chips):
            copy(4 + j, (*chip, 1 - c), me).wait_recv()
        for cp in first + passed:
            cp.wait_send()
        mine.wait()

    out = pl.pallas_call(
        body, name=name,
        out_shape=jax.ShapeDtypeStruct((8 * r, n), xs.dtype),
        in_specs=[pl.BlockSpec(memory_space=pltpu.VMEM)],
        out_specs=pl.BlockSpec(memory_space=pltpu.VMEM),
        scratch_shapes=[pltpu.SemaphoreType.DMA((7,)), pltpu.SemaphoreType.DMA((7,)), pltpu.SemaphoreType.DMA],
        compiler_params=pltpu.CompilerParams(vmem_limit_bytes=VMEM_LIMIT))(xs)
    return out.reshape(8, r, n)


def _gather_flat(v, name):
    n = v.shape[0]
    npad = -(-n // 1024) * 1024
    g = _small_allgather(jnp.pad(v, (0, npad - n)).reshape(8, npad // 8), name)
    return g.reshape(8, npad)[:, :n]


def _big_allgather(blks, name):
    n = len(blks)

    def body(*refs):
        xs, outs = refs[:n], refs[n:2 * n]
        send_sems, recv_sems = refs[2 * n:]
        x, y, c = lax.axis_index("x"), lax.axis_index("y"), lax.axis_index("c")
        me, sibling = (x, y, c), (x, y, 1 - c)
        chips = [(1 - x, y), (x, 1 - y), (1 - x, 1 - y)]

        def copy(w, k, block, to, own=False):
            px, py, pc = block
            slot = outs[w].at[4 * px + 2 * py + pc]
            return pltpu.make_async_remote_copy(
                src_ref=xs[w] if own else slot, dst_ref=slot, send_sem=send_sems.at[7 * w + k],
                recv_sem=recv_sems.at[7 * w + k], device_id=to, device_id_type=MESH)

        first = []
        for w in range(n):
            first.append(copy(w, 0, me, sibling, own=True))
            first += [copy(w, 1 + j, me, (*chip, c), own=True) for j, chip in enumerate(chips)]
        for cp in first:
            cp.start()
        passed = []
        for w in range(n):
            for j, chip in enumerate(chips):
                copy(w, 1 + j, (*chip, c), me).wait_recv()
                cp = copy(w, 4 + j, (*chip, c), sibling)
                cp.start()
                passed.append(cp)
        for w in range(n):
            copy(w, 0, sibling, me).wait_recv()
            for j, chip in enumerate(chips):
                copy(w, 4 + j, (*chip, 1 - c), me).wait_recv()
        for cp in first + passed:
            cp.wait_send()

    any_spec = pl.BlockSpec(memory_space=pl.ANY)
    return pl.pallas_call(
        body, name=name,
        out_shape=[jax.ShapeDtypeStruct((8,) + b.shape, b.dtype) for b in blks],
        in_specs=[any_spec] * n, out_specs=[any_spec] * n,
        scratch_shapes=[pltpu.SemaphoreType.DMA((7 * n,)), pltpu.SemaphoreType.DMA((7 * n,))],
    )(*blks)


def _rs_pair_exchange(g8s, name):
    n = len(g8s)

    def body(*refs):
        gs, rs = refs[:n], refs[n:2 * n]
        send_sems, recv_sems = refs[2 * n:]
        x, y, c = lax.axis_index("x"), lax.axis_index("y"), lax.axis_index("c")
        cps = [pltpu.make_async_remote_copy(
            src_ref=gs[w].at[2 * j + 1 - c], dst_ref=rs[w].at[j], send_sem=send_sems.at[4 * w + j],
            recv_sem=recv_sems.at[4 * w + j], device_id=(x, y, 1 - c), device_id_type=MESH)
            for w in range(n) for j in range(4)]
        for cp in cps:
            cp.start()
        for cp in cps:
            cp.wait_recv()
        for cp in cps:
            cp.wait_send()

    any_spec = pl.BlockSpec(memory_space=pl.ANY)
    return pl.pallas_call(
        body, name=name,
        out_shape=[jax.ShapeDtypeStruct((4,) + g.shape[1:], g.dtype) for g in g8s],
        in_specs=[any_spec] * n, out_specs=[any_spec] * n,
        scratch_shapes=[pltpu.SemaphoreType.DMA((4 * n,)), pltpu.SemaphoreType.DMA((4 * n,))])(*g8s)


def _rs_chip_exchange(h4s, name):
    n = len(h4s)

    def body(*refs):
        hs, rs = refs[:n], refs[n:2 * n]
        send_sems, recv_sems = refs[2 * n:]
        x, y, c = lax.axis_index("x"), lax.axis_index("y"), lax.axis_index("c")
        chips = [(1 - x, y), (x, 1 - y), (1 - x, 1 - y)]
        cps = [pltpu.make_async_remote_copy(
            src_ref=hs[w].at[2 * px + py], dst_ref=rs[w].at[k], send_sem=send_sems.at[3 * w + k],
            recv_sem=recv_sems.at[3 * w + k], device_id=(px, py, c), device_id_type=MESH)
            for w in range(n) for k, (px, py) in enumerate(chips)]
        for cp in cps:
            cp.start()
        for cp in cps:
            cp.wait_recv()
        for cp in cps:
            cp.wait_send()

    any_spec = pl.BlockSpec(memory_space=pl.ANY)
    return pl.pallas_call(
        body, name=name,
        out_shape=[jax.ShapeDtypeStruct((3,) + h.shape[1:], h.dtype) for h in h4s],
        in_specs=[any_spec] * n, out_specs=[any_spec] * n,
        scratch_shapes=[pltpu.SemaphoreType.DMA((3 * n,)), pltpu.SemaphoreType.DMA((3 * n,))])(*h4s)


def _rs_pair_share(fins, name):
    n = len(fins)

    def body(*refs):
        fs, outs = refs[:n], refs[n:2 * n]
        send_sems, recv_sems = refs[2 * n:]
        x, y, c = lax.axis_index("x"), lax.axis_index("y"), lax.axis_index("c")
        cps = [pltpu.make_async_remote_copy(
            src_ref=fs[w], dst_ref=outs[w], send_sem=send_sems.at[w], recv_sem=recv_sems.at[w],
            device_id=(x, y, 1 - c), device_id_type=MESH) for w in range(n)]
        for cp in cps:
            cp.start()
        for cp in cps:
            cp.wait_recv()
        for cp in cps:
            cp.wait_send()

    any_spec = pl.BlockSpec(memory_space=pl.ANY)
    return pl.pallas_call(
        body, name=name,
        out_shape=[jax.ShapeDtypeStruct(f.shape, f.dtype) for f in fins],
        in_specs=[any_spec] * n, out_specs=[any_spec] * n,
        scratch_shapes=[pltpu.SemaphoreType.DMA((n,)), pltpu.SemaphoreType.DMA((n,))])(*fins)


def _rs_add_pair(g8, r4, c, name):
    R, n = g8.shape[1:]
    rb = R // 2

    def body(c_ref, g_ref, r_ref, o_ref):
        o_ref[...] = (g_ref[...] + r_ref[...]).astype(BF16)

    spec = pl.BlockSpec((1, rb, n), lambda j, i, c_ref: (j, i, 0))
    return pl.pallas_call(
        body, name=name,
        grid_spec=pltpu.PrefetchScalarGridSpec(
            num_scalar_prefetch=1, grid=(4, R // rb),
            in_specs=[pl.BlockSpec((1, rb, n), lambda j, i, c_ref: (2 * j + c_ref[0], i, 0)), spec],
            out_specs=spec),
        out_shape=jax.ShapeDtypeStruct((4, R, n), BF16), compiler_params=_cp(2))(c, g8, r4)


def _rs_add_chips(g8, r4, r3, cj, name):
    R, n = g8.shape[1:]
    rb = R // 2

    def body(cj_ref, g_ref, p_ref, r_ref, o_ref):
        own = g_ref[0] + p_ref[0]
        o_ref[...] = ((own + r_ref[0].astype(F32)) + r_ref[1].astype(F32)) + r_ref[2].astype(F32)

    return pl.pallas_call(
        body, name=name,
        grid_spec=pltpu.PrefetchScalarGridSpec(
            num_scalar_prefetch=1, grid=(R // rb,),
            in_specs=[pl.BlockSpec((1, rb, n), lambda i, cj_ref: (2 * cj_ref[1] + cj_ref[0], i, 0)),
                      pl.BlockSpec((1, rb, n), lambda i, cj_ref: (cj_ref[1], i, 0)),
                      pl.BlockSpec((3, rb, n), lambda i, cj_ref: (0, i, 0))],
            out_specs=pl.BlockSpec((rb, n), lambda i, cj_ref: (i, 0))),
        out_shape=jax.ShapeDtypeStruct((R, n), F32), compiler_params=_cp(1))(cj, g8, r4, r3)


def _sum8(g):
    n = g.shape[1]

    def body(g_ref, o_ref):
        acc = g_ref[0:1, :]
        for k in range(1, 8):
            acc = acc + g_ref[k:k + 1, :]
        o_ref[...] = acc

    return pl.pallas_call(body, name="sum_devices", out_shape=jax.ShapeDtypeStruct((1, n), F32),
                          compiler_params=pltpu.CompilerParams(vmem_limit_bytes=VMEM_LIMIT))(g)


ADA_ROWS = 64


def _ada_fwd(cs, w, b):
    n = w.shape[1]

    def body(c_ref, w_ref, b_ref, o_ref):
        cv = c_ref[...]
        s = (cv * _sigmoid(cv)).astype(BF16)
        o_ref[...] = _dot(s, w_ref[...].astype(BF16)) + b_ref[...]

    return pl.pallas_call(body, name="ada_fwd", out_shape=jax.ShapeDtypeStruct((ADA_ROWS, n), F32),
                          compiler_params=pltpu.CompilerParams(vmem_limit_bytes=VMEM_LIMIT))(cs, w, b)


def _ada_bwd(cs, w, dm):
    n = w.shape[1]

    def body(c_ref, w_ref, dm_ref, gw_ref, dc_ref):
        cv = c_ref[...]
        s = (cv * _sigmoid(cv)).astype(BF16)
        gw_ref[...] = _dot_tn(s, dm_ref[...].astype(BF16))
        dc_ref[...] = _dot_nt(dm_ref[32:40, :].astype(BF16), w_ref[...].astype(BF16))

    return pl.pallas_call(
        body, name="ada_bwd",
        out_shape=[jax.ShapeDtypeStruct((D, n), F32), jax.ShapeDtypeStruct((8, D), F32)],
        compiler_params=pltpu.CompilerParams(vmem_limit_bytes=VMEM_LIMIT))(cs, w, dm)


def _adamw(w, g, m, v, name):
    r, c = w.shape
    rb = r
    if r % 8 == 0 and r * c * 4 > (1 << 20):
        rb = 8
        for cand in range(8, r, 8):
            if r % cand == 0 and cand * c * 4 <= (1 << 20):
                rb = cand

    def body(w_ref, g_ref, m_ref, v_ref, d_ref, nm_ref, nv_ref):
        gv = g_ref[...]
        mn = ADAM_B1 * m_ref[...] + (1.0 - ADAM_B1) * gv
        vn = ADAM_B2 * v_ref[...] + (1.0 - ADAM_B2) * (gv * gv)
        m_hat = mn / (1.0 - ADAM_B1 ** ADAM_STEP)
        v_hat = vn / (1.0 - ADAM_B2 ** ADAM_STEP)
        d_ref[...] = -ADAM_LR * (m_hat / (jnp.sqrt(v_hat) + ADAM_EPS) + ADAM_WD * w_ref[...])
        nm_ref[...] = mn
        nv_ref[...] = vn

    spec = pl.BlockSpec((rb, c), lambda i: (i, 0))
    shp = jax.ShapeDtypeStruct((r, c), F32)
    return pl.pallas_call(body, name=name, grid=(r // rb,), in_specs=[spec] * 4, out_specs=[spec] * 3,
                          out_shape=[shp] * 3, compiler_params=_cp(1))(w, g, m, v)


BIG = ("ffn1_w_in", "ffn1_w_out", "w_mix_in", "w_mix_out", "ffn2_w_in", "ffn2_w_out")


def _half_rows(w, c):
    half = w.shape[0] // 2
    return lax.dynamic_slice_in_dim(w, c * half, half, axis=0)


def _lower_bounds(logits):
    return jnp.cumsum(jax.nn.softmax(logits.astype(F32), axis=1), axis=1)[:, 0]


def kernel(x, c, ctx, c_ctx, w_ada, b_ada, ln_gain, ln_bias, ffn1_w_in, ffn1_w_out, w_mix_in, gla_a2_fwd, gla_a2_bwd, gla_a_bias_fwd, gla_a_bias_bwd, hgrn_lb_logits, gla_norm_gain, hgrn_norm_gain, w_mix_out, ffn2_w_in, ffn2_w_out, loss_target, m_c_ctx, m_w_ada, m_b_ada, m_ln_gain, m_ln_bias, m_ffn1_w_in, m_ffn1_w_out, m_w_mix_in, m_gla_a2_fwd, m_gla_a2_bwd, m_gla_a_bias_fwd, m_gla_a_bias_bwd, m_hgrn_lb_logits, m_gla_norm_gain, m_hgrn_norm_gain, m_w_mix_out, m_ffn2_w_in, m_ffn2_w_out, v_c_ctx, v_w_ada, v_b_ada, v_ln_gain, v_ln_bias, v_ffn1_w_in, v_ffn1_w_out, v_w_mix_in, v_gla_a2_fwd, v_gla_a2_bwd, v_gla_a_bias_fwd, v_gla_a_bias_bwd, v_hgrn_lb_logits, v_gla_norm_gain, v_hgrn_norm_gain, v_w_mix_out, v_ffn2_w_in, v_ffn2_w_out):
    xi, yi, ci = lax.axis_index("x"), lax.axis_index("y"), lax.axis_index("c")
    chip = 2 * xi + yi
    dev = 2 * chip + ci
    B = x.shape[0]
    weights = dict(ffn1_w_in=ffn1_w_in[0], ffn1_w_out=ffn1_w_out[0], w_mix_in=w_mix_in[0], w_mix_out=w_mix_out[0],
                   ffn2_w_in=ffn2_w_in[0], ffn2_w_out=ffn2_w_out[0])

    mine = jnp.concatenate([c.reshape(-1), ln_gain.reshape(-1), ln_bias.reshape(-1), gla_a2_fwd.reshape(-1),
                            gla_a2_bwd.reshape(-1), hgrn_lb_logits.reshape(-1)])
    g1 = _gather_flat(mine, "gather_cond")
    nc = B * D
    c_all = g1[:, :nc].reshape(8 * B, D)
    per_chip = g1[0::2, nc:]
    o = 0

    def take(shape, axis):
        nonlocal o
        n = int(np.prod(shape))
        parts = per_chip[:, o:o + n].reshape((4,) + shape)
        o += n
        return jnp.concatenate([parts[j] for j in range(4)], axis=axis)

    ln_gain_f = take((3, 256), 1)
    ln_bias_f = take((3, 256), 1)
    a2f_f = take((16, 64), 1)
    a2b_f = take((16, 64), 1)
    lbl_f = take((2, 2, 128), 2)
    lb, lb_vjp = jax.vjp(_lower_bounds, lbl_f)

    shards = dict(weights, w_mix_in=jnp.pad(weights["w_mix_in"], ((0, 0), (0, MIX_NP - MIX_N))))
    blks = [_half_rows(shards[k], ci).astype(BF16) for k in BIG]
    full = {k: lax.dynamic_update_index_in_dim(g, own, dev, 0)
            for k, own, g in zip(BIG, blks, _big_allgather(blks, "weight_allgather"))}
    w1i, w2i = (full[k].reshape((4,) + shards[k].shape) for k in ("ffn1_w_in", "ffn2_w_in"))
    w1o, wmo, w2o = (full[k].reshape(-1, D) for k in ("ffn1_w_out", "w_mix_out", "ffn2_w_out"))
    wmp = _mix_in_to_padded(full["w_mix_in"].reshape(4, D, MIX_NP))

    cs = jnp.concatenate([c_all, c_ctx.reshape(1, D), jnp.zeros((ADA_ROWS - 8 * B - 1, D), F32)], axis=0)
    ncol = w_ada.shape[2]
    b_cols = lax.dynamic_slice_in_dim(b_ada, chip * ncol, ncol, axis=1)
    m_cols = _ada_fwd(cs, w_ada[0], b_cols)
    g2 = _small_allgather(m_cols, "gather_mod")[0::2]
    m_all = jnp.concatenate([g2[j] for j in range(4)], axis=1)
    m_lat = lax.dynamic_slice_in_dim(m_all, dev * B, B, axis=0).reshape(B, 1, N_MOD, D)
    m_ctx = jnp.broadcast_to(m_all[8 * B].reshape(1, 1, N_MOD, D), (B, 1, N_MOD, D))
    mvec = jnp.concatenate([m_ctx, m_lat], axis=1)

    loss_l, grad_x, big, dm, small = _local_step(
        x, ctx, loss_target, mvec, w1i, w1o, wmp, wmo, w2i, w2o, ln_gain_f, ln_bias_f, a2f_f, a2b_f,
        gla_a_bias_fwd, gla_a_bias_bwd, lb, gla_norm_gain, hgrn_norm_gain)
    loss = lax.psum(loss_l, ("x", "y", "c"))

    dm_lat = dm[:, 1].reshape(B, N_MOD * D)
    dm_ctx = jnp.sum(dm[:, 0], axis=0).reshape(N_MOD * D)
    keys = ("ln_gain", "ln_bias", "a2f", "a2b", "abf", "abb", "lb", "gng", "gnh")
    flat = jnp.concatenate([dm_lat.reshape(-1), dm_ctx] + [small[k].reshape(-1) for k in keys])
    g3 = _gather_flat(flat, "gather_small_grads")
    nlat = B * N_MOD * D
    dm_all = g3[:, :nlat].reshape(8 * B, N_MOD * D)
    tot = _sum8(g3[:, nlat:])[0]
    dmc_tot = tot[:N_MOD * D]
    o = N_MOD * D
    sg = {}
    for k in keys:
        n = int(np.prod(small[k].shape))
        sg[k] = tot[o:o + n].reshape(small[k].shape)
        o += n
    dm_rows = jnp.concatenate([dm_all, dmc_tot.reshape(1, -1), jnp.zeros((ADA_ROWS - 8 * B - 1, N_MOD * D), F32)], axis=0)
    g_b_ada = (jnp.sum(dm_all, axis=0) + dmc_tot).reshape(1, N_MOD * D)
    g_w_ada, dcc = _ada_bwd(cs, w_ada[0], lax.dynamic_slice_in_dim(dm_rows, chip * ncol, ncol, axis=1))
    g4 = _gather_flat(dcc[0], "gather_cctx")
    dsilu = ((g4[0] + g4[2]) + g4[4]) + g4[6]
    sc = _sigmoid(c_ctx)
    g_c_ctx = dsilu * (sc * (1.0 + c_ctx * (1.0 - sc)))
    (g_lbl,) = lb_vjp(sg["lb"])

    def cols(a, n, axis):
        return lax.dynamic_slice_in_dim(a, chip * n, n, axis=axis)

    big_full = dict(ffn1_w_in=big["w1i"], ffn1_w_out=big["w1o"], w_mix_in=_mix_in_from_padded(big["wmp"]),
                    w_mix_out=big["wmo"], ffn2_w_in=big["w2i"], ffn2_w_out=big["w2o"])
    g8s = [big_full[k].reshape((8, shards[k].shape[0] // 2, shards[k].shape[1])) for k in BIG]
    cvec = ci.reshape(1).astype(jnp.int32)
    cjvec = jnp.stack([ci, chip]).astype(jnp.int32)
    r4s = _rs_pair_exchange(g8s, "grad_pair_exchange")
    h4s = [_rs_add_pair(g, r, cvec, "grad_pair_add_" + k) for k, g, r in zip(BIG, g8s, r4s)]
    r3s = _rs_chip_exchange(h4s, "grad_chip_exchange")
    fins = [_rs_add_chips(g, r4, r3, cjvec, "grad_chip_add_" + k) for k, g, r4, r3 in zip(BIG, g8s, r4s, r3s)]
    gsh = {}
    for k, own, sib in zip(BIG, fins, _rs_pair_share(fins, "grad_pair_share")):
        both = jnp.where(ci == 0, jnp.stack([own, sib]), jnp.stack([sib, own]))
        gsh[k] = both.reshape(shards[k].shape)[:, :weights[k].shape[1]]

    grads = dict(
        c_ctx=g_c_ctx, w_ada=g_w_ada[None], b_ada=g_b_ada, ln_gain=cols(sg["ln_gain"], 256, 1)[None],
        ln_bias=cols(sg["ln_bias"], 256, 1)[None], ffn1_w_in=gsh["ffn1_w_in"][None], ffn1_w_out=gsh["ffn1_w_out"][None],
        w_mix_in=gsh["w_mix_in"][None], gla_a2_fwd=cols(sg["a2f"], 64, 1)[None], gla_a2_bwd=cols(sg["a2b"], 64, 1)[None],
        gla_a_bias_fwd=sg["abf"], gla_a_bias_bwd=sg["abb"], hgrn_lb_logits=cols(g_lbl, 128, 2),
        gla_norm_gain=sg["gng"].reshape(1, HD), hgrn_norm_gain=sg["gnh"].reshape(1, HD),
        w_mix_out=gsh["w_mix_out"][None], ffn2_w_in=gsh["ffn2_w_in"][None], ffn2_w_out=gsh["ffn2_w_out"][None])
    params = dict(
        c_ctx=(c_ctx, m_c_ctx, v_c_ctx), w_ada=(w_ada, m_w_ada, v_w_ada), b_ada=(b_ada, m_b_ada, v_b_ada),
        ln_gain=(ln_gain, m_ln_gain, v_ln_gain), ln_bias=(ln_bias, m_ln_bias, v_ln_bias),
        ffn1_w_in=(ffn1_w_in, m_ffn1_w_in, v_ffn1_w_in), ffn1_w_out=(ffn1_w_out, m_ffn1_w_out, v_ffn1_w_out),
        w_mix_in=(w_mix_in, m_w_mix_in, v_w_mix_in), gla_a2_fwd=(gla_a2_fwd, m_gla_a2_fwd, v_gla_a2_fwd),
        gla_a2_bwd=(gla_a2_bwd, m_gla_a2_bwd, v_gla_a2_bwd),
        gla_a_bias_fwd=(gla_a_bias_fwd, m_gla_a_bias_fwd, v_gla_a_bias_fwd),
        gla_a_bias_bwd=(gla_a_bias_bwd, m_gla_a_bias_bwd, v_gla_a_bias_bwd),
        hgrn_lb_logits=(hgrn_lb_logits, m_hgrn_lb_logits, v_hgrn_lb_logits),
        gla_norm_gain=(gla_norm_gain, m_gla_norm_gain, v_gla_norm_gain),
        hgrn_norm_gain=(hgrn_norm_gain, m_hgrn_norm_gain, v_hgrn_norm_gain),
        w_mix_out=(w_mix_out, m_w_mix_out, v_w_mix_out), ffn2_w_in=(ffn2_w_in, m_ffn2_w_in, v_ffn2_w_in),
        ffn2_w_out=(ffn2_w_out, m_ffn2_w_out, v_ffn2_w_out))
    order = list(params.keys())
    big_names = ("w_ada",) + BIG
    upd = {}
    for k in big_names:
        w_, m_, v_ = params[k]
        s2 = w_.shape[-2:]
        d_, nm_, nv_ = _adamw(w_.reshape(s2), grads[k].reshape(s2), m_.reshape(s2), v_.reshape(s2), "adamw_" + k)
        upd[k] = (d_.reshape(w_.shape), nm_.reshape(w_.shape), nv_.reshape(w_.shape))
    small_names = [k for k in order if k not in big_names]
    sizes = [int(np.prod(params[k][0].shape)) for k in small_names]
    tot_n = sum(sizes)
    npad = -(-tot_n // 1024) * 1024

    def packed(get):
        flat_ = jnp.concatenate([get(k).reshape(-1) for k in small_names])
        return jnp.pad(flat_, (0, npad - tot_n)).reshape(8, npad // 8)

    d_s, nm_s, nv_s = _adamw(packed(lambda k: params[k][0]), packed(lambda k: grads[k]),
                             packed(lambda k: params[k][1]), packed(lambda k: params[k][2]), "adamw_small")
    o = 0
    for k, n in zip(small_names, sizes):
        shp = params[k][0].shape
        upd[k] = tuple(a.reshape(-1)[o:o + n].reshape(shp) for a in (d_s, nm_s, nv_s))
        o += n

    return (loss, grad_x, *[grads[k].reshape(params[k][0].shape) for k in order], *[upd[k][0] for k in order],
            *[upd[k][1] for k in order], *[upd[k][2] for k in order])
```

```python
import functools

import numpy as np
import jax
import jax.numpy as jnp
from jax import lax
from jax.experimental import pallas as pl
from jax.experimental.pallas import tpu as pltpu

F32 = jnp.float32
BF16 = jnp.bfloat16
MESH = pl.DeviceIdType.MESH

D = 1024
DFF = 2816
TM = 256
CH = 32
NCB = TM // CH
SB = 128
CSB = SB // CH
NSB = TM // SB
HP = 4
HD = 128
NH = 8
LN_EPS = 1e-5
NORM_EPS = 1e-6
ALPHA = 2.0 ** 0.25
GATE_NORM = 16.0
GLA_DK = 64
N_MOD = 9
VMEM_LIMIT = 52 * 1024 * 1024

MIXP = 5120
GG, RG, GQ, GK, GV, RQ, RFF, RFB, RI, LR = 0, 512, 1024, 1536, 2048, 2560, 3072, 3584, 4096, 4608
IN_SPLITS = (256, 256, 512, 512, 16, 16, 512, 512, 512, 512, 512)

ADAM_LR, ADAM_B1, ADAM_B2, ADAM_EPS, ADAM_WD, ADAM_STEP = 0.001, 0.9, 0.999, 1e-08, 0.01, 10


def _cp(n_axes):
    return pltpu.CompilerParams(dimension_semantics=("arbitrary",) * n_axes, vmem_limit_bytes=VMEM_LIMIT)


def _rowmap(stride, off):
    return lambda b, i: (b * stride + off + i, 0)


def _mmap(comb):
    if comb:
        return lambda b, i: (b, jnp.minimum(i, 1), 0, 0)
    return lambda b, i: (b, 1, 0, 0)


def _ln(x):
    mu = jnp.mean(x, axis=-1, keepdims=True)
    xc = x - mu
    var = jnp.mean(xc * xc, axis=-1, keepdims=True)
    r = lax.rsqrt(var + LN_EPS)
    return xc * r, r


def _ln_bwd(dxh, xh, r):
    return r * (dxh - jnp.mean(dxh, axis=-1, keepdims=True) - xh * jnp.mean(dxh * xh, axis=-1, keepdims=True))


def _sigmoid(x):
    return 1.0 / (1.0 + jnp.exp(-x))


def _rsum(x):
    return jnp.sum(x, axis=0, keepdims=True)


def _dot(a, b):
    return jnp.dot(a, b, preferred_element_type=F32)


def _dot_nt(a, b):
    return lax.dot_general(a, b, (((1,), (1,)), ((), ())), preferred_element_type=F32)


def _dot_tn(a, b):
    return lax.dot_general(a, b, (((0,), (0,)), ((), ())), preferred_element_type=F32)


def _modulate(xv, m_ref, sub):
    xh, _ = _ln(xv)
    sh = m_ref[0, 0, 3 * sub:3 * sub + 1, :]
    sc = m_ref[0, 0, 3 * sub + 1:3 * sub + 2, :]
    return (xh * (1.0 + sh) + sc).astype(BF16)


def _embed_lnmod(x, ctx, pos, mvec):
    B, T, _ = x.shape
    nt = 1 + T // TM

    def body(x_ref, c_ref, p_ref, m_ref, o_ref, h_ref):
        i = pl.program_id(1)

        @pl.when(i == 0)
        def _():
            o_ref[...] = c_ref[0]

        @pl.when(i > 0)
        def _():
            o_ref[...] = x_ref[0] + p_ref[...]

        h_ref[...] = _modulate(o_ref[...], m_ref, 0)

    rows = pl.BlockSpec((TM, D), lambda b, i: (b * nt + i, 0))
    return pl.pallas_call(
        body, name="embed_lnmod0", grid=(B, nt),
        in_specs=[pl.BlockSpec((1, TM, D), lambda b, i: (b, jnp.maximum(i - 1, 0), 0)),
                  pl.BlockSpec((1, TM, D), lambda b, i: (b, 0, 0)),
                  pl.BlockSpec((TM, D), lambda b, i: (jnp.maximum(i - 1, 0), 0)),
                  pl.BlockSpec((1, 1, N_MOD, D), _mmap(True))],
        out_specs=[rows, rows],
        out_shape=[jax.ShapeDtypeStruct((B * nt * TM, D), F32), jax.ShapeDtypeStruct((B * nt * TM, D), BF16)],
        compiler_params=_cp(2))(x, ctx, pos, mvec)


def _lnmod0_bwd(dh, X, mvec, dres, x_shape, B, nt):
    def body(dh_ref, x_ref, m_ref, dr_ref, dx_ref, dm_ref):
        i = pl.program_id(1)
        xh, r = _ln(x_ref[...])
        sh = m_ref[0, 0, 0:1, :]
        dhv = dh_ref[...]

        @pl.when((i == 0) | (i == 1))
        def _():
            dm_ref[...] = jnp.zeros_like(dm_ref)

        dm_ref[0, 0, 0:1, :] += _rsum(dhv * xh)
        dm_ref[0, 0, 1:2, :] += _rsum(dhv)

        @pl.when(i > 0)
        def _():
            dx_ref[0] = _ln_bwd(dhv * (1.0 + sh), xh, r) + dr_ref[...]

    rows = pl.BlockSpec((TM, D), _rowmap(nt, 0))
    return pl.pallas_call(
        body, name="lnmod0_bwd", grid=(B, nt),
        in_specs=[rows, rows, pl.BlockSpec((1, 1, N_MOD, D), _mmap(True)), rows],
        out_specs=[pl.BlockSpec((1, TM, D), lambda b, i: (b, jnp.maximum(i - 1, 0), 0)),
                   pl.BlockSpec((1, 1, 2, D), _mmap(True))],
        out_shape=[jax.ShapeDtypeStruct(x_shape, F32), jax.ShapeDtypeStruct((B, 2, 2, D), F32)],
        compiler_params=_cp(2))(dh, X, mvec, dres)


def _resid_fwd(x_ref, y_ref, m_ref, gb_ref, sub, w):
    wg = w * m_ref[0, 0, 3 * sub + 2:3 * sub + 3, :]
    y = y_ref[...]
    zh, r = _ln(ALPHA * x_ref[...] + wg * y)
    return y, wg, zh, r


def _resid_grads(do, y, wg, zh, r, w, gb_ref, comb, dx_ref, dy_ref, dg_ref, dgb_ref):
    b_, i = pl.program_id(0), pl.program_id(1)
    dz = _ln_bwd(do * gb_ref[0:1, :], zh, r)
    dx_ref[...] = ALPHA * dz
    dy_ref[...] = (wg * dz).astype(BF16)

    @pl.when((b_ == 0) & (i == 0))
    def _():
        dgb_ref[...] = jnp.zeros_like(dgb_ref)

    dgb_ref[0:1, :] += _rsum(do * zh)
    dgb_ref[1:2, :] += _rsum(do)

    init = (i == 0) | (i == 1) if comb else (i == 0)

    @pl.when(init)
    def _():
        dg_ref[...] = jnp.zeros_like(dg_ref)

    dg_ref[0, 0] += w * _rsum(dz * y)


def _resid_lnmod(X, lx, Y, mvec, comb, sub, w, gb, B, nt, name):
    def body(x_ref, y_ref, m_ref, gb_ref, o_ref, h_ref):
        _, _, zh, _ = _resid_fwd(x_ref, y_ref, m_ref, gb_ref, sub, w)
        xn = zh * gb_ref[0:1, :] + gb_ref[1:2, :]
        o_ref[...] = xn
        h_ref[...] = _modulate(xn, m_ref, sub + 1)

    rows = pl.BlockSpec((TM, D), _rowmap(nt, 0))
    return pl.pallas_call(
        body, name=name, grid=(B, nt),
        in_specs=[pl.BlockSpec((TM, D), _rowmap(*lx)), rows,
                  pl.BlockSpec((1, 1, N_MOD, D), _mmap(comb)), pl.BlockSpec((2, D), lambda b, i: (0, 0))],
        out_specs=[rows, rows],
        out_shape=[jax.ShapeDtypeStruct((B * nt * TM, D), F32), jax.ShapeDtypeStruct((B * nt * TM, D), BF16)],
        compiler_params=_cp(2))(X, Y, mvec, gb)


def _resid_out_shapes(B, nt, comb):
    rows = pl.BlockSpec((TM, D), _rowmap(nt, 0))
    specs = [rows, rows, pl.BlockSpec((1, 1, 1, D), _mmap(comb)), pl.BlockSpec((2, D), lambda b, i: (0, 0))]
    shapes = [jax.ShapeDtypeStruct((B * nt * TM, D), F32), jax.ShapeDtypeStruct((B * nt * TM, D), BF16),
              jax.ShapeDtypeStruct((B, 2, 1, D), F32), jax.ShapeDtypeStruct((2, D), F32)]
    return specs, shapes


def _tail(X, Y, mvec, gb, tgt, sub, w, B, nt):
    def body(x_ref, y_ref, m_ref, gb_ref, t_ref, dx_ref, dy_ref, dg_ref, dgb_ref, l_ref):
        y, wg, zh, r = _resid_fwd(x_ref, y_ref, m_ref, gb_ref, sub, w)
        e = (zh * gb_ref[0:1, :] + gb_ref[1:2, :]) - t_ref[0]

        @pl.when((pl.program_id(0) == 0) & (pl.program_id(1) == 0))
        def _():
            l_ref[...] = jnp.zeros_like(l_ref)

        l_ref[...] += _rsum(e * e)
        _resid_grads(e * (1.0 / D), y, wg, zh, r, w, gb_ref, False, dx_ref, dy_ref, dg_ref, dgb_ref)

    rows = pl.BlockSpec((TM, D), _rowmap(nt, 0))
    specs, shapes = _resid_out_shapes(B, nt, False)
    return pl.pallas_call(
        body, name="resid2_loss_bwd", grid=(B, nt),
        in_specs=[rows, rows, pl.BlockSpec((1, 1, N_MOD, D), _mmap(False)), pl.BlockSpec((2, D), lambda b, i: (0, 0)),
                  pl.BlockSpec((1, TM, D), lambda b, i: (b, i, 0))],
        out_specs=specs + [pl.BlockSpec((1, D), lambda b, i: (0, 0))],
        out_shape=shapes + [jax.ShapeDtypeStruct((1, D), F32)],
        compiler_params=_cp(2))(X, Y, mvec, gb, tgt)


def _lnmod_resid_bwd(dh, Xi, lxi, dres, Xp, lxp, Yp, mvec, comb, sub, w, gb, B, nt, name):
    ntl = nt - 1 if comb else nt

    def body(dh_ref, xi_ref, dr_ref, xp_ref, yp_ref, m_ref, gb_ref, dx_ref, dy_ref, dg_ref, dgb_ref, dm_ref):
        i = pl.program_id(1)
        xh, r = _ln(xi_ref[...])
        sh = m_ref[0, 0, 3 * sub:3 * sub + 1, :]
        dhv = dh_ref[...]
        dr = dr_ref[...]
        if comb:
            dr = jnp.where(i > 0, dr, 0.0)
        do = _ln_bwd(dhv * (1.0 + sh), xh, r) + dr

        init = (i == 0) | (i == 1) if comb else (i == 0)

        @pl.when(init)
        def _():
            dm_ref[...] = jnp.zeros_like(dm_ref)

        dm_ref[0, 0, 0:1, :] += _rsum(dhv * xh)
        dm_ref[0, 0, 1:2, :] += _rsum(dhv)
        y, wg, zh, r2 = _resid_fwd(xp_ref, yp_ref, m_ref, gb_ref, sub - 1, w)
        _resid_grads(do, y, wg, zh, r2, w, gb_ref, comb, dx_ref, dy_ref, dg_ref, dgb_ref)

    rows = pl.BlockSpec((TM, D), _rowmap(nt, 0))
    if comb:
        dres_spec = pl.BlockSpec((TM, D), lambda b, i: (b * ntl + jnp.maximum(i - 1, 0), 0))
    else:
        dres_spec = rows
    specs, shapes = _resid_out_shapes(B, nt, comb)
    return pl.pallas_call(
        body, name=name, grid=(B, nt),
        in_specs=[rows, pl.BlockSpec((TM, D), _rowmap(*lxi)), dres_spec, pl.BlockSpec((TM, D), _rowmap(*lxp)), rows,
                  pl.BlockSpec((1, 1, N_MOD, D), _mmap(comb)), pl.BlockSpec((2, D), lambda b, i: (0, 0))],
        out_specs=specs + [pl.BlockSpec((1, 1, 2, D), _mmap(comb))],
        out_shape=shapes + [jax.ShapeDtypeStruct((B, 2, 2, D), F32)],
        compiler_params=_cp(2))(dh, Xi, dres, Xp, Yp, mvec, gb)


def _ffn_out_dx(dy, W, u, name):
    M = dy.shape[0]
    half = DFF // 2

    def body(dy_ref, w_ref, u_ref, du_ref):
        dyv = dy_ref[...]
        for j in range(2):
            lo, hi = j * half, (j + 1) * half
            da = _dot_nt(dyv, w_ref[lo:hi, :])
            g = u_ref[:, lo:hi].astype(F32)
            up = u_ref[:, DFF + lo:DFF + hi].astype(F32)
            s = _sigmoid(g)
            du_ref[:, lo:hi] = (da * up * (s * (1.0 + g * (1.0 - s)))).astype(BF16)
            du_ref[:, DFF + lo:DFF + hi] = (da * (g * s)).astype(BF16)

    return pl.pallas_call(
        body, name=name, grid=(M // TM,),
        in_specs=[pl.BlockSpec((TM, D), lambda i: (i, 0)), _wspec(W, 1), pl.BlockSpec((TM, 2 * DFF), lambda i: (i, 0))],
        out_specs=pl.BlockSpec((TM, 2 * DFF), lambda i: (i, 0)),
        out_shape=jax.ShapeDtypeStruct((M, 2 * DFF), BF16), compiler_params=_cp(1))(dy, W, u)


def _wspec(W, nidx):
    zeros = (0,) * W.ndim
    if nidx == 1:
        return pl.BlockSpec(W.shape, lambda i: zeros)
    return pl.BlockSpec(W.shape, lambda b, i: zeros)


def _mm_nn(A, la, W, B, nt, out_dtype, name):
    K, N = W.shape

    def body(a_ref, w_ref, o_ref):
        o_ref[...] = _dot(a_ref[...], w_ref[...]).astype(out_dtype)

    return pl.pallas_call(
        body, name=name, grid=(B, nt),
        in_specs=[pl.BlockSpec((TM, K), _rowmap(*la)), _wspec(W, 2)],
        out_specs=pl.BlockSpec((TM, N), _rowmap(nt, 0)),
        out_shape=jax.ShapeDtypeStruct((B * nt * TM, N), out_dtype), compiler_params=_cp(2))(A, W)


def _ffn_in(A, la, W3, B, nt, name):
    K, n = W3.shape[1:]

    def body(a_ref, w_ref, u_ref, s_ref):
        a = a_ref[...]
        for j in range(2):
            g = _dot(a, w_ref[j])
            up = _dot(a, w_ref[j + 2])
            u_ref[:, j * n:(j + 1) * n] = g.astype(BF16)
            u_ref[:, (j + 2) * n:(j + 3) * n] = up.astype(BF16)
            s_ref[:, j * n:(j + 1) * n] = (g * _sigmoid(g) * up).astype(BF16)

    rows = B * nt * TM
    return pl.pallas_call(
        body, name=name, grid=(B, nt),
        in_specs=[pl.BlockSpec((TM, K), _rowmap(*la)), _wspec(W3, 2)],
        out_specs=[pl.BlockSpec((TM, 4 * n), _rowmap(nt, 0)), pl.BlockSpec((TM, 2 * n), _rowmap(nt, 0))],
        out_shape=[jax.ShapeDtypeStruct((rows, 4 * n), BF16), jax.ShapeDtypeStruct((rows, 2 * n), BF16)],
        compiler_params=_cp(2))(A, W3)


def _mm_nt(A, W, name, dep=None):
    M, N = A.shape
    K = W.shape[-2]

    def body(a_ref, w_ref, *rest):
        o_ref = rest[-1]
        if W.ndim == 3:
            n = W.shape[-1]
            acc = _dot_nt(a_ref[:, 0:n], w_ref[0])
            for j in range(1, 4):
                acc = acc + _dot_nt(a_ref[:, j * n:(j + 1) * n], w_ref[j])
            o_ref[...] = acc
        else:
            o_ref[...] = _dot_nt(a_ref[...], w_ref[...])

    deps = [] if dep is None else [dep]
    return pl.pallas_call(
        body, name=name, grid=(M // TM,),
        in_specs=[pl.BlockSpec((TM, N), lambda i: (i, 0)), _wspec(W, 1)] + [_wspec(d, 1) for d in deps],
        out_specs=pl.BlockSpec((TM, K), lambda i: (i, 0)),
        out_shape=jax.ShapeDtypeStruct((M, K), F32), compiler_params=_cp(1))(A, W, *deps)


def _mm_tn(A, G, name, tk=512, tn=512, shards=None):
    M, K = A.shape
    N = G.shape[1]
    if shards:
        tn = N // shards

    def body(a_ref, g_ref, o_ref):
        @pl.when(pl.program_id(1) == 0)
        def _():
            o_ref[...] = jnp.zeros_like(o_ref)

        upd = _dot_tn(a_ref[...], g_ref[...])
        if shards:
            o_ref[0] += upd
        else:
            o_ref[...] += upd

    if shards:
        out_spec = pl.BlockSpec((1, K, tn), lambda n, k: (n, 0, 0))
        out_shape = jax.ShapeDtypeStruct((shards, K, tn), F32)
    else:
        out_spec = pl.BlockSpec((K, tn), lambda n, k: (0, n))
        out_shape = jax.ShapeDtypeStruct((K, N), F32)
    return pl.pallas_call(
        body, name=name, grid=(N // tn, M // tk),
        in_specs=[pl.BlockSpec((tk, K), lambda n, k: (k, 0)), pl.BlockSpec((tk, tn), lambda n, k: (k, n))],
        out_specs=out_spec, out_shape=out_shape, compiler_params=_cp(2))(A, G)


def _logsig(z):
    return jnp.minimum(z, 0.0) - jnp.log(1.0 + jnp.exp(-jnp.abs(z)))


def _features(F, a2p, biasp, lbp, B):
    nt = F.shape[0] // (B * TM)
    nj = nt + 1
    R2 = nj * TM

    def body(f_ref, a2_ref, bias_ref, lb_ref, q_ref, k_ref, v_ref, g_ref):
        lr = f_ref[:, LR:LR + HD].astype(BF16)
        lane = lax.broadcasted_iota(jnp.int32, (1, 4 * HD), 1)
        keep = (lane & (HD - 1)) < GLA_DK
        for d in range(2):
            z = _dot(lr, a2_ref[d]) + bias_ref[d:d + 1, :]
            gl = jnp.where(keep, _logsig(z) * (1.0 / GATE_NORM), 0.0)
            for h in range(4):
                g_ref[d, 0, h] = gl[:, h * HD:(h + 1) * HD]
        for h in range(4):
            q_ref[0, h] = f_ref[:, GQ + h * HD:GQ + (h + 1) * HD] * (GLA_DK ** -0.5)
            kk = f_ref[:, GK + h * HD:GK + (h + 1) * HD]
            k_ref[0, 0, h] = kk
            k_ref[1, 0, h] = kk
            v_ref[0, h] = f_ref[:, GV + h * HD:GV + (h + 1) * HD].astype(BF16)
        for h in range(4):
            sl = slice(h * HD, (h + 1) * HD)
            rq = f_ref[:, RQ + h * HD:RQ + (h + 1) * HD]
            q_ref[0, 4 + h] = rq * _sigmoid(rq) * (HD ** -0.5)
            v_ref[0, 4 + h] = f_ref[:, RI + h * HD:RI + (h + 1) * HD].astype(BF16)
            for d, off in ((0, RFF), (1, RFB)):
                lb = lb_ref[d:d + 1, sl]
                f = lb + (1.0 - lb) * _sigmoid(f_ref[:, off + h * HD:off + (h + 1) * HD])
                g_ref[d, 0, 4 + h] = jnp.log(f)
                k_ref[d, 0, 4 + h] = 1.0 - f

    fmap = lambda b, j: (b * nt + jnp.where(j == nt, 0, j), 0)
    one = pl.BlockSpec((1, NH, TM, HD), lambda b, j: (b, 0, j, 0))
    two = pl.BlockSpec((2, 1, NH, TM, HD), lambda b, j: (0, b, 0, j, 0))
    s1 = jax.ShapeDtypeStruct((B, NH, R2, HD), F32)
    s2 = jax.ShapeDtypeStruct((2, B, NH, R2, HD), F32)
    return pl.pallas_call(
        body, name="mix_features", grid=(B, nj),
        in_specs=[pl.BlockSpec((TM, MIXP), fmap), pl.BlockSpec((2, HD, 4 * HD), lambda b, j: (0, 0, 0)),
                  pl.BlockSpec((2, 4 * HD), lambda b, j: (0, 0)), pl.BlockSpec((2, 4 * HD), lambda b, j: (0, 0))],
        out_specs=[one, two, one, two], out_shape=[s1, s2, jax.ShapeDtypeStruct(s1.shape, BF16), s2],
        compiler_params=_cp(2))(F, a2p, biasp, lbp)


def _features_bwd(F, a2p, biasp, lbp, dQ0, dQ1, dK0, dK1, dV0, dV1, dG0, dG1, dgates, B):
    nt = F.shape[0] // (B * TM)

    def body(f_ref, a2_ref, bias_ref, lb_ref, dq0, dq1, dk0, dk1, dv0, dv1, dg0, dg1, dgt_ref,
             df_ref, da2_ref, dbias_ref, dlb_ref):
        b_, i = pl.program_id(0), pl.program_id(1)

        @pl.when((b_ == 0) & (i == 0))
        def _():
            da2_ref[...] = jnp.zeros_like(da2_ref)
            dbias_ref[...] = jnp.zeros_like(dbias_ref)
            dlb_ref[...] = jnp.zeros_like(dlb_ref)

        df_ref[:, 0:2 * 4 * HD] = jnp.where(i > 0, dgt_ref[...], 0.0).astype(BF16)
        df_ref[:, LR + HD:] = jnp.zeros((TM, MIXP - LR - HD), BF16)
        lr = f_ref[:, LR:LR + HD].astype(BF16)
        lane = lax.broadcasted_iota(jnp.int32, (1, 4 * HD), 1)
        keep = (lane & (HD - 1)) < GLA_DK
        dlr = jnp.zeros((TM, HD), F32)
        dgs = (dg0, dg1)
        dks = (dk0, dk1)
        for d in range(2):
            z = _dot(lr, a2_ref[d]) + bias_ref[d:d + 1, :]
            dgl = jnp.concatenate([dgs[d][0, h] for h in range(4)], axis=1)
            dz = jnp.where(keep, dgl * (1.0 / GATE_NORM) * (1.0 - _sigmoid(z)), 0.0)
            dzb = dz.astype(BF16)
            dlr = dlr + _dot_nt(dzb, a2_ref[d])
            da2_ref[d] += _dot_tn(lr, dzb)
            dbias_ref[d:d + 1, :] += _rsum(dz)
        df_ref[:, LR:LR + HD] = dlr.astype(BF16)
        for h in range(4):
            df_ref[:, GQ + h * HD:GQ + (h + 1) * HD] = ((dq0[0, h] + dq1[0, h]) * (GLA_DK ** -0.5)).astype(BF16)
            df_ref[:, GK + h * HD:GK + (h + 1) * HD] = (dk0[0, h] + dk1[0, h]).astype(BF16)
            df_ref[:, GV + h * HD:GV + (h + 1) * HD] = (dv0[0, h] + dv1[0, h]).astype(BF16)
        for h in range(4):
            sl = slice(h * HD, (h + 1) * HD)
            rq = f_ref[:, RQ + h * HD:RQ + (h + 1) * HD]
            s = _sigmoid(rq)
            dqh = dq0[0, 4 + h] + dq1[0, 4 + h]
            df_ref[:, RQ + h * HD:RQ + (h + 1) * HD] = (dqh * (HD ** -0.5) * (s * (1.0 + rq * (1.0 - s)))).astype(BF16)
            df_ref[:, RI + h * HD:RI + (h + 1) * HD] = (dv0[0, 4 + h] + dv1[0, 4 + h]).astype(BF16)
            for d, off in ((0, RFF), (1, RFB)):
                lb = lb_ref[d:d + 1, sl]
                sg = _sigmoid(f_ref[:, off + h * HD:off + (h + 1) * HD])
                f = lb + (1.0 - lb) * sg
                dff = dgs[d][0, 4 + h] / f - dks[d][0, 4 + h]
                df_ref[:, off + h * HD:off + (h + 1) * HD] = (dff * (1.0 - lb) * sg * (1.0 - sg)).astype(BF16)
                dlb_ref[d:d + 1, sl] += _rsum(dff * (1.0 - sg))

    m0 = lambda b, i: (b, 0, i, 0)
    m1 = lambda b, i: (b, 0, jnp.where(i == 0, nt, i), 0)
    one = lambda m: pl.BlockSpec((1, NH, TM, HD), m)
    return pl.pallas_call(
        body, name="mix_features_bwd", grid=(B, nt),
        in_specs=[pl.BlockSpec((TM, MIXP), _rowmap(nt, 0)), pl.BlockSpec((2, HD, 4 * HD), lambda b, i: (0, 0, 0)),
                  pl.BlockSpec((2, 4 * HD), lambda b, i: (0, 0)), pl.BlockSpec((2, 4 * HD), lambda b, i: (0, 0)),
                  one(m0), one(m1), one(m0), one(m1), one(m0), one(m1), one(m0), one(m1),
                  pl.BlockSpec((TM, D), lambda b, i: (b * (nt - 1) + jnp.maximum(i - 1, 0), 0))],
        out_specs=[pl.BlockSpec((TM, MIXP), _rowmap(nt, 0)), pl.BlockSpec((2, HD, 4 * HD), lambda b, i: (0, 0, 0)),
                   pl.BlockSpec((2, 4 * HD), lambda b, i: (0, 0)), pl.BlockSpec((2, 4 * HD), lambda b, i: (0, 0))],
        out_shape=[jax.ShapeDtypeStruct((B * nt * TM, MIXP), BF16), jax.ShapeDtypeStruct((2, HD, 4 * HD), F32),
                   jax.ShapeDtypeStruct((2, 4 * HD), F32), jax.ShapeDtypeStruct((2, 4 * HD), F32)],
        compiler_params=_cp(2))(F, a2p, biasp, lbp, dQ0, dQ1, dK0, dK1, dV0, dV1, dG0, dG1, dgates)


def _chunk_scan(x, rin, fwd, inclusive=True):
    acc = x
    sft = 1
    while sft < CH:
        if fwd:
            acc = acc + jnp.where(rin >= sft, pltpu.roll(acc, sft, 0), 0.0)
        else:
            acc = acc + jnp.where(rin < CH - sft, pltpu.roll(acc, TM - sft, 0), 0.0)
        sft *= 2
    return acc if inclusive else acc - x


def _scan_common(q, k, v, g, rev):
    rin = lax.broadcasted_iota(jnp.int32, (TM, HD), 0) & (CH - 1)
    b = _chunk_scan(g, rin, not rev)
    xx = _chunk_scan(g, rin, rev, inclusive=False)
    eb = jnp.exp(b)
    qd = q * eb
    ki = k * jnp.exp(-b)
    kt = k * jnp.exp(xx)
    ri = lax.broadcasted_iota(jnp.int32, (SB, SB), 0)
    ci = lax.broadcasted_iota(jnp.int32, (SB, SB), 1)
    same = (ri >> 5) == (ci >> 5)
    lo = same & (ri >= ci)
    up = same & (ri <= ci)
    mask, maskT = (up, lo) if rev else (lo, up)
    re = lax.broadcasted_iota(jnp.int32, (SB, CSB * HD), 0) >> 5
    ce = lax.broadcasted_iota(jnp.int32, (SB, CSB * HD), 1) >> 7
    mexp = re == ce
    return rin, b, xx, eb, qd, ki, kt, mask, maskT, mexp


def _sub(x, s):
    return x[s * SB:(s + 1) * SB]


def _expand(xb, mexp):
    return jnp.where(mexp, jnp.concatenate([xb] * CSB, axis=1), jnp.zeros((), xb.dtype))


def _own(x, mexp):
    xm = jnp.where(mexp, x, 0.0)
    acc = xm[:, 0:HD]
    for n in range(1, CSB):
        acc = acc + xm[:, n * HD:(n + 1) * HD]
    return acc


def _stack(per_chunk, s):
    return jnp.concatenate(per_chunk[s * CSB:(s + 1) * CSB], axis=1)


def _state_pass(s0, eb, uts, rev):
    order = range(NCB - 1, -1, -1) if rev else range(NCB)
    states = [None] * NCB
    s = s0
    for n in order:
        row = n * CH if rev else n * CH + CH - 1
        states[n] = s
        s = eb[row:row + 1, :] * s + uts[n // CSB][:, (n % CSB) * HD:(n % CSB + 1) * HD]
    return states, s


def _scan_fwd(Q, K, V, G, rev, B):
    nb = Q.shape[2] // TM - 1
    d = 1 if rev else 0
    rmap = (lambda s: nb - s) if rev else (lambda s: s)

    def body(q_ref, k_ref, v_ref, g_ref, o_ref, st_ref, s_scr):
        @pl.when(pl.program_id(2) == 0)
        def _():
            s_scr[...] = jnp.zeros_like(s_scr)

        for p in range(HP):
            s0 = s_scr[p]
            st_ref[0, p, 0] = s0
            v = v_ref[0, p]
            _, _, _, eb, qd, ki, kt, mask, _, mexp = _scan_common(q_ref[0, p], k_ref[0, 0, p], v, g_ref[0, 0, p], rev)
            qb, kib, ktb, vb = qd.astype(BF16), ki.astype(BF16), kt.astype(BF16), v.astype(BF16)
            uts = [_dot_tn(_sub(vb, s), _expand(_sub(ktb, s), mexp)) for s in range(NSB)]
            states, s_new = _state_pass(s0, eb, uts, rev)
            s_scr[p] = s_new
            for s in range(NSB):
                a = jnp.where(mask, _dot_nt(_sub(qb, s), _sub(kib, s)), 0.0)
                o_ref[0, p, s * SB:(s + 1) * SB, :] = (
                    _dot(a.astype(BF16), _sub(vb, s))
                    + _dot_nt(_expand(_sub(qb, s), mexp), _stack(states, s).astype(BF16)))

    one = pl.BlockSpec((1, HP, TM, HD), lambda b, h, s: (b, h, rmap(s), 0))
    two = pl.BlockSpec((1, 1, HP, TM, HD), lambda b, h, s: (d, b, h, rmap(s), 0))
    return pl.pallas_call(
        body, name="scan_fwd_rev" if rev else "scan_fwd", grid=(B, NH // HP, nb),
        in_specs=[one, two, one, two],
        out_specs=[one, pl.BlockSpec((1, HP, 1, HD, HD), lambda b, h, s: (b, h, s, 0, 0))],
        out_shape=[jax.ShapeDtypeStruct(Q.shape, F32), jax.ShapeDtypeStruct((B, NH, nb, HD, HD), F32)],
        scratch_shapes=[pltpu.VMEM((HP, HD, HD), F32)],
        compiler_params=_cp(3))(Q, K, V, G)


def _scan_bwd(Q, K, V, G, St, dO, rev, B):
    nb = Q.shape[2] // TM - 1
    d = 1 if rev else 0
    smap = lambda t: nb - 1 - t
    rmap = (lambda t: nb - smap(t)) if rev else smap

    def body(q_ref, k_ref, v_ref, g_ref, st_ref, do_ref, dq_ref, dk_ref, dv_ref, dg_ref, ds_scr):
        t = pl.program_id(2)

        @pl.when(t == 0)
        def _():
            ds_scr[...] = jnp.zeros_like(ds_scr)

        is_lat = smap(t) >= 1
        for p in range(HP):
            v = v_ref[0, p]
            rin, b, xx, eb, qd, ki, kt, mask, maskT, mexp = _scan_common(
                q_ref[0, p], k_ref[0, 0, p], v, g_ref[0, 0, p], rev)
            qb, kib, ktb, vb = qd.astype(BF16), ki.astype(BF16), kt.astype(BF16), v.astype(BF16)
            dob = jnp.where(is_lat, do_ref[0, p], 0.0).astype(BF16)
            kt_exps = [_expand(_sub(ktb, s), mexp) for s in range(NSB)]
            uts = [_dot_tn(_sub(vb, s), kt_exps[s]) for s in range(NSB)]
            states, _ = _state_pass(st_ref[0, p, 0], eb, uts, rev)
            gts = [_dot_tn(_sub(dob, s), _expand(_sub(qb, s), mexp)) for s in range(NSB)]
            order = range(NCB) if rev else range(NCB - 1, -1, -1)
            dsp = [None] * NCB
            t2 = [None] * NCB
            dsc = ds_scr[p]
            for n in order:
                row = n * CH if rev else n * CH + CH - 1
                ebl = eb[row:row + 1, :]
                dsp[n] = dsc
                t2[n] = jnp.broadcast_to(ebl * _rsum(states[n] * dsc), (CH, HD))
                dsc = gts[n // CSB][:, (n % CSB) * HD:(n % CSB + 1) * HD] + ebl * dsc
            ds_scr[p] = dsc
            dqds, dkis, dkts = [], [], []
            for s in range(NSB):
                q_s, ki_s, v_s, do_s = _sub(qb, s), _sub(kib, s), _sub(vb, s), _sub(dob, s)
                dspb = _stack(dsp, s).astype(BF16)
                da = jnp.where(mask, _dot_nt(do_s, v_s), 0.0).astype(BF16)
                dat = jnp.where(maskT, _dot_nt(v_s, do_s), 0.0).astype(BF16)
                at = jnp.where(maskT, _dot_nt(ki_s, q_s), 0.0).astype(BF16)
                dqds.append(_dot(da, ki_s) + _own(_dot(do_s, _stack(states, s).astype(BF16)), mexp))
                dkis.append(_dot(dat, q_s))
                dv_ref[0, p, s * SB:(s + 1) * SB, :] = _dot(at, do_s) + _dot_nt(kt_exps[s], dspb)
                dkts.append(_own(_dot(v_s, dspb), mexp))
            dqd, dki, dkt = (jnp.concatenate(parts, axis=0) for parts in (dqds, dkis, dkts))
            z = dkt * kt
            db = dqd * qd - dki * ki
            dq_ref[0, p] = dqd * eb
            dk_ref[0, p] = dki * jnp.exp(-b) + dkt * jnp.exp(xx)
            dg_ref[0, p] = (_chunk_scan(db, rin, rev) + _chunk_scan(z, rin, not rev, inclusive=False)
                            + jnp.concatenate(t2, axis=0))

    one = pl.BlockSpec((1, HP, TM, HD), lambda b, h, t: (b, h, rmap(t), 0))
    two = pl.BlockSpec((1, 1, HP, TM, HD), lambda b, h, t: (d, b, h, rmap(t), 0))
    lat = pl.BlockSpec((1, HP, TM, HD), lambda b, h, t: (b, h, jnp.clip(rmap(t) - 1, 0, nb - 2), 0))
    shp = jax.ShapeDtypeStruct(Q.shape, F32)
    return pl.pallas_call(
        body, name="scan_bwd_rev" if rev else "scan_bwd", grid=(B, NH // HP, nb),
        in_specs=[one, two, one, two, pl.BlockSpec((1, HP, 1, HD, HD), lambda b, h, t: (b, h, smap(t), 0, 0)), lat],
        out_specs=[one, one, one, one], out_shape=[shp, shp, shp, shp],
        scratch_shapes=[pltpu.VMEM((HP, HD, HD), F32)],
        compiler_params=_cp(3))(Q, K, V, G, St, dO)


def _gnorm(O0, O1, F, gains, B, ntl):
    nt = ntl + 1

    def body(o0_ref, o1_ref, f_ref, gn_ref, m_ref):
        for h in range(NH):
            o = o0_ref[0, h] + o1_ref[0, h]
            r = lax.rsqrt(jnp.mean(o * o, axis=-1, keepdims=True) + NORM_EPS)
            gn = gn_ref[0:1, :] if h < 4 else gn_ref[1:2, :]
            gt = f_ref[:, h * HD:(h + 1) * HD]
            m_ref[:, h * HD:(h + 1) * HD] = (o * r * gn * (gt * _sigmoid(gt))).astype(BF16)

    ospec = pl.BlockSpec((1, NH, TM, HD), lambda b, i: (b, 0, i + 1, 0))
    return pl.pallas_call(
        body, name="gated_norm", grid=(B, ntl),
        in_specs=[ospec, ospec, pl.BlockSpec((TM, D), lambda b, i: (b * nt + 1 + i, 0)),
                  pl.BlockSpec((2, HD), lambda b, i: (0, 0))],
        out_specs=pl.BlockSpec((TM, D), _rowmap(ntl, 0)),
        out_shape=jax.ShapeDtypeStruct((B * ntl * TM, D), BF16), compiler_params=_cp(2))(O0, O1, F, gains)


def _gnorm_bwd(dM, O0, O1, F, gains, B, ntl):
    nt = ntl + 1

    def body(dm_ref, o0_ref, o1_ref, f_ref, gn_ref, do_ref, dgt_ref, dgn_ref):
        b_, i = pl.program_id(0), pl.program_id(1)

        @pl.when((b_ == 0) & (i == 0))
        def _():
            dgn_ref[...] = jnp.zeros_like(dgn_ref)

        for h in range(NH):
            o = o0_ref[0, h] + o1_ref[0, h]
            r = lax.rsqrt(jnp.mean(o * o, axis=-1, keepdims=True) + NORM_EPS)
            y = o * r
            gn = gn_ref[0:1, :] if h < 4 else gn_ref[1:2, :]
            gt = f_ref[:, h * HD:(h + 1) * HD]
            s = _sigmoid(gt)
            dm = dm_ref[:, h * HD:(h + 1) * HD]
            don = dm * (gt * s)
            dgt_ref[:, h * HD:(h + 1) * HD] = dm * (y * gn) * (s * (1.0 + gt * (1.0 - s)))
            row = 0 if h < 4 else 1
            dgn_ref[row:row + 1, :] += _rsum(don * y)
            dy = don * gn
            do_ref[0, h] = r * (dy - y * jnp.mean(dy * y, axis=-1, keepdims=True))

    ospec = pl.BlockSpec((1, NH, TM, HD), lambda b, i: (b, 0, i + 1, 0))
    return pl.pallas_call(
        body, name="gated_norm_bwd", grid=(B, ntl),
        in_specs=[pl.BlockSpec((TM, D), _rowmap(ntl, 0)), ospec, ospec,
                  pl.BlockSpec((TM, D), lambda b, i: (b * nt + 1 + i, 0)), pl.BlockSpec((2, HD), lambda b, i: (0, 0))],
        out_specs=[pl.BlockSpec((1, NH, TM, HD), lambda b, i: (b, 0, i, 0)), pl.BlockSpec((TM, D), _rowmap(ntl, 0)),
                   pl.BlockSpec((2, HD), lambda b, i: (0, 0))],
        out_shape=[jax.ShapeDtypeStruct((B, NH, ntl * TM, HD), F32), jax.ShapeDtypeStruct((B * ntl * TM, D), F32),
                   jax.ShapeDtypeStruct((2, HD), F32)],
        compiler_params=_cp(2))(dM, O0, O1, F, gains)


def _sincos_2d(rows, width, dim):
    r = jnp.repeat(jnp.arange(rows), width)
    col = jnp.tile(jnp.arange(width), rows)
    quarter = dim // 4
    omega = 1.0 / 10000.0 ** (jnp.arange(quarter, dtype=F32) / quarter)

    def emb(p):
        a = p.astype(F32)[:, None] * omega[None, :]
        return jnp.concatenate([jnp.sin(a), jnp.cos(a)], axis=-1)

    return jnp.concatenate([emb(r), emb(col)], axis=-1)


def _pad_heads(w):
    k = w.shape[0]
    return jnp.pad(w.reshape(k, 4, GLA_DK), ((0, 0), (0, 0), (0, HD - GLA_DK))).reshape(k, 4 * HD)


def _unpad_heads(w):
    k = w.shape[0]
    return w.reshape(k, 4, HD)[:, :, :GLA_DK].reshape(k, 4 * GLA_DK)


MIX_N = 1032
MIX_NP = 1152
_SEGS = ([(64 * h, 64, GQ + HD * h) for h in range(4)] + [(256 + 64 * h, 64, GK + HD * h) for h in range(4)]
         + [(512, 512, GV), (1024, 512, GG), (1536, 32, LR), (1568, 512, RQ), (2080, 512, RFF), (2592, 512, RFB),
            (3104, 512, RI), (3616, 512, RG)])


def _mix_in_to_padded(ps):
    k = ps.shape[1]
    parts, pos = [], 0
    for g0, ln, s0 in sorted(_SEGS, key=lambda s: s[2]):
        if s0 > pos:
            parts.append(jnp.zeros((k, s0 - pos), ps.dtype))
        for j in range(4):
            lo, hi = max(g0, j * MIX_N), min(g0 + ln, (j + 1) * MIX_N)
            if lo < hi:
                parts.append(ps[j][:, lo - j * MIX_N:hi - j * MIX_N])
        pos = s0 + ln
    parts.append(jnp.zeros((k, MIXP - pos), ps.dtype))
    return jnp.concatenate(parts, axis=1)


def _mix_in_from_padded(g):
    k = g.shape[0]
    shards = []
    for j in range(4):
        parts = []
        for g0, ln, s0 in sorted(_SEGS):
            lo, hi = max(g0, j * MIX_N), min(g0 + ln, (j + 1) * MIX_N)
            if lo < hi:
                parts.append(g[:, s0 + lo - g0:s0 + hi - g0])
        parts.append(jnp.zeros((k, MIX_NP - MIX_N), g.dtype))
        shards.append(jnp.concatenate(parts, axis=1))
    return jnp.stack(shards)


def _local_step(x, ctx, tgt, mvec, w1i, w1o, wmp, wmo, w2i, w2o, ln_gain, ln_bias, a2f, a2b, abf, abb, lb, gng, gnh,
                on_grads):
    B, T, _ = x.shape
    assert ctx.shape[1] == TM and T % TM == 0
    ntl = T // TM
    nt = ntl + 1
    C, L, CL = (nt, 0), (ntl, 0), (nt, 1)
    pos = _sincos_2d(T // 64, 64, D)
    gbs = [jnp.stack([ln_gain[i], ln_bias[i]]) for i in range(3)]
    a2p = jnp.zeros((2, HD, 4 * HD), F32)
    a2p = a2p.at[0, 0:16].set(_pad_heads(a2f)).at[1, 16:32].set(_pad_heads(a2b)).astype(BF16)
    biasp = jnp.concatenate([_pad_heads(abf.reshape(1, -1)), _pad_heads(abb.reshape(1, -1))], axis=0)
    gains = jnp.concatenate([gng.reshape(1, HD), gnh.reshape(1, HD)], axis=0)

    X0, h0 = _embed_lnmod(x, ctx, pos, mvec)
    u0, a0 = _ffn_in(h0, C, w1i, B, nt, "ffn1_in")
    y0 = _mm_nn(a0, C, w1o, B, nt, F32, "ffn1_out")
    X1, h1 = _resid_lnmod(X0, C, y0, mvec, True, 0, 0.5, gbs[0], B, nt, "resid0_lnmod1")
    Fm = _mm_nn(h1, C, wmp, B, nt, F32, "mix_in")
    Q, K, V, G = _features(Fm, a2p, biasp, lb, B)
    O0, S0 = _scan_fwd(Q, K, V, G, False, B)
    O1, S1 = _scan_fwd(Q, K, V, G, True, B)
    merged = _gnorm(O0, O1, Fm, gains, B, ntl)
    y1 = _mm_nn(merged, L, wmo, B, ntl, F32, "mix_out")
    X2, h2 = _resid_lnmod(X1, CL, y1, mvec, False, 1, 1.0, gbs[1], B, ntl, "resid1_lnmod2")
    u2, a2 = _ffn_in(h2, L, w2i, B, ntl, "ffn2_in")
    y2 = _mm_nn(a2, L, w2o, B, ntl, F32, "ffn2_out")

    dx2r, dy2, dgate2, dgb2, lsum = _tail(X2, y2, mvec, gbs[2], tgt, 2, 0.5, B, ntl)
    loss = (0.5 / D) * jnp.sum(lsum)
    du2 = _ffn_out_dx(dy2, w2o, u2, "ffn2_out_dx")
    g_w2o = _mm_tn(a2, dy2, "ffn2_out_dw")
    dh2 = _mm_nt(du2, w2i, "ffn2_in_dx")
    g_w2i = _mm_tn(h2, du2, "ffn2_in_dw", shards=4)
    tok = on_grads("ffn2", (g_w2i, g_w2o))
    dx1r, dy1, dgate1, dgb1, dss2 = _lnmod_resid_bwd(dh2, X2, L, dx2r, X1, CL, y1, mvec, False, 2, 1.0,
                                                     gbs[1] + tok[0, 0], B, ntl, "lnmod2_resid1_bwd")
    dmerged = _mm_nt(dy1, wmo, "mix_out_dx")
    g_wmo = _mm_tn(merged, dy1, "mix_out_dw")
    dO, dgates, dgains = _gnorm_bwd(dmerged, O0, O1, Fm, gains, B, ntl)
    dQ0, dK0, dV0, dG0 = _scan_bwd(Q, K, V, G, S0, dO, False, B)
    dQ1, dK1, dV1, dG1 = _scan_bwd(Q, K, V, G, S1, dO, True, B)
    dF, da2p, dbiasp, dlb = _features_bwd(Fm, a2p, biasp, lb, dQ0, dQ1, dK0, dK1, dV0, dV1, dG0, dG1, dgates, B)
    dh1 = _mm_nt(dF, wmp, "mix_in_dx")
    g_wmp = _mm_tn(h1, dF, "mix_in_dw", tn=MIXP // 4)
    tok = on_grads("mix", (g_wmp, g_wmo))
    dx0r, dy0, dgate0, dgb0, dss1 = _lnmod_resid_bwd(dh1, X1, C, dx1r, X0, C, y0, mvec, True, 1, 0.5,
                                                     gbs[0] + tok[0, 0], B, nt, "lnmod1_resid0_bwd")
    du0 = _ffn_out_dx(dy0, w1o, u0, "ffn1_out_dx")
    g_w1o = _mm_tn(a0, dy0, "ffn1_out_dw")
    g_w1i = _mm_tn(h0, du0, "ffn1_in_dw", shards=4)
    tok = on_grads("ffn1", (g_w1i, g_w1o))
    dh0 = _mm_nt(du0, w1i, "ffn1_in_dx", dep=tok)
    grad_x, dss0 = _lnmod0_bwd(dh0, X0, mvec, dx0r, x.shape, B, nt)

    zero_ctx = lambda a: a.at[:, 0].set(0.0)
    dm = jnp.concatenate([dss0, dgate0, dss1, zero_ctx(dgate1), zero_ctx(dss2), zero_ctx(dgate2)], axis=2)
    small = dict(
        ln_gain=jnp.stack([dgb0[0], dgb1[0], dgb2[0]]), ln_bias=jnp.stack([dgb0[1], dgb1[1], dgb2[1]]),
        a2f=_unpad_heads(da2p[0, 0:16]), a2b=_unpad_heads(da2p[1, 16:32]),
        abf=_unpad_heads(dbiasp[0:1]), abb=_unpad_heads(dbiasp[1:2]), lb=dlb, gng=dgains[0], gnh=dgains[1])
    return loss, grad_x, dm, small


def _small_allgather(xs, name):
    r, n = xs.shape

    def body(x_ref, out_ref, send_sems, recv_sems, local_sem):
        x, y, c = lax.axis_index("x"), lax.axis_index("y"), lax.axis_index("c")
        me, sibling = (x, y, c), (x, y, 1 - c)
        chips = [(1 - x, y), (x, 1 - y), (1 - x, 1 - y)]

        def rows(px, py, pc):
            return out_ref.at[pl.ds((4 * px + 2 * py + pc) * r, r), :]

        def copy(k, block, to, src=None):
            return pltpu.make_async_remote_copy(
                src_ref=rows(*block) if src is None else src, dst_ref=rows(*block),
                send_sem=send_sems.at[k], recv_sem=recv_sems.at[k], device_id=to, device_id_type=MESH)

        mine = pltpu.make_async_copy(x_ref, rows(*me), local_sem)
        mine.start()
        first = [copy(0, me, sibling, src=x_ref)]
        first += [copy(1 + j, me, (*chip, c), src=x_ref) for j, chip in enumerate(chips)]
        for cp in first:
            cp.start()
        passed = [copy(4 + j, (*chip, c), sibling) for j, chip in enumerate(chips)]
        for j, chip in enumerate(chips):
            copy(1 + j, (*chip, c), me).wait_recv()
            passed[j].start()
        copy(0, sibling, me).wait_recv()
        for j, chip in enumerate(chips):
            copy(4 + j, (*chip, 1 - c), me).wait_recv()
        for cp in first + passed:
            cp.wait_send()
        mine.wait()

    out = pl.pallas_call(
        body, name=name,
        out_shape=jax.ShapeDtypeStruct((8 * r, n), xs.dtype),
        in_specs=[pl.BlockSpec(memory_space=pltpu.VMEM)],
        out_specs=pl.BlockSpec(memory_space=pltpu.VMEM),
        scratch_shapes=[pltpu.SemaphoreType.DMA((7,)), pltpu.SemaphoreType.DMA((7,)), pltpu.SemaphoreType.DMA],
        compiler_params=pltpu.CompilerParams(vmem_limit_bytes=VMEM_LIMIT))(xs)
    return out.reshape(8, r, n)


def _gather_flat(v, name):
    n = v.shape[0]
    npad = -(-n // 1024) * 1024
    g = _small_allgather(jnp.pad(v, (0, npad - n)).reshape(8, npad // 8), name)
    return g.reshape(8, npad)[:, :n]


def _big_allgather(blks, name):
    n = len(blks)

    def body(*refs):
        xs, outs = refs[:n], refs[n:2 * n]
        send_sems, recv_sems = refs[2 * n:]
        x, y, c = lax.axis_index("x"), lax.axis_index("y"), lax.axis_index("c")
        me, sibling = (x, y, c), (x, y, 1 - c)
        chips = [(1 - x, y), (x, 1 - y), (1 - x, 1 - y)]

        def copy(w, k, block, to, own=False):
            px, py, pc = block
            slot = outs[w].at[4 * px + 2 * py + pc]
            return pltpu.make_async_remote_copy(
                src_ref=xs[w] if own else slot, dst_ref=slot, send_sem=send_sems.at[7 * w + k],
                recv_sem=recv_sems.at[7 * w + k], device_id=to, device_id_type=MESH)

        first = []
        for w in range(n):
            first.append(copy(w, 0, me, sibling, own=True))
            first += [copy(w, 1 + j, me, (*chip, c), own=True) for j, chip in enumerate(chips)]
        for cp in first:
            cp.start()
        passed = []
        for w in range(n):
            for j, chip in enumerate(chips):
                copy(w, 1 + j, (*chip, c), me).wait_recv()
                cp = copy(w, 4 + j, (*chip, c), sibling)
                cp.start()
                passed.append(cp)
        for w in range(n):
            copy(w, 0, sibling, me).wait_recv()
            for j, chip in enumerate(chips):
                copy(w, 4 + j, (*chip, 1 - c), me).wait_recv()
        for cp in first + passed:
            cp.wait_send()

    any_spec = pl.BlockSpec(memory_space=pl.ANY)
    return pl.pallas_call(
        body, name=name,
        out_shape=[jax.ShapeDtypeStruct((8,) + b.shape, b.dtype) for b in blks],
        in_specs=[any_spec] * n, out_specs=[any_spec] * n,
        scratch_shapes=[pltpu.SemaphoreType.DMA((7 * n,)), pltpu.SemaphoreType.DMA((7 * n,))],
    )(*blks)


def _rs_pair_exchange(g8s, name):
    n = len(g8s)

    def body(*refs):
        gs, rs = refs[:n], refs[n:2 * n]
        send_sems, recv_sems = refs[2 * n:]
        x, y, c = lax.axis_index("x"), lax.axis_index("y"), lax.axis_index("c")
        cps = [pltpu.make_async_remote_copy(
            src_ref=gs[w].at[2 * j + 1 - c], dst_ref=rs[w].at[j], send_sem=send_sems.at[4 * w + j],
            recv_sem=recv_sems.at[4 * w + j], device_id=(x, y, 1 - c), device_id_type=MESH)
            for w in range(n) for j in range(4)]
        for cp in cps:
            cp.start()
        for cp in cps:
            cp.wait_recv()
        for cp in cps:
            cp.wait_send()

    any_spec = pl.BlockSpec(memory_space=pl.ANY)
    return pl.pallas_call(
        body, name=name,
        out_shape=[jax.ShapeDtypeStruct((4,) + g.shape[1:], g.dtype) for g in g8s],
        in_specs=[any_spec] * n, out_specs=[any_spec] * n,
        scratch_shapes=[pltpu.SemaphoreType.DMA((4 * n,)), pltpu.SemaphoreType.DMA((4 * n,))])(*g8s)


HBM_SPEC = pl.BlockSpec(memory_space=pltpu.HBM)
SEM_SPEC = pl.BlockSpec(memory_space=pltpu.SEMAPHORE)
DATAFLOW = pltpu.SideEffectType.DATAFLOW_SIDE_EFFECTING


def _chip_copies(hs, ls, send_sems, recv_sems):
    x, y, c = lax.axis_index("x"), lax.axis_index("y"), lax.axis_index("c")
    chips = [(1 - x, y), (x, 1 - y), (1 - x, 1 - y)]
    return [pltpu.make_async_remote_copy(
        src_ref=hs[w].at[2 * px + py], dst_ref=ls[w].at[k], send_sem=send_sems[3 * w + k],
        recv_sem=recv_sems[3 * w + k], device_id=(px, py, c), device_id_type=MESH)
        for w in range(len(hs)) for k, (px, py) in enumerate(chips)]


def _rs_chip_exchange_start(h4s, name):
    n = len(h4s)
    lands = [lax.empty((3,) + h.shape[1:], h.dtype) for h in h4s]

    def body(*refs):
        hs, ls, outs = refs[:n], refs[n:2 * n], refs[2 * n:]
        for cp in _chip_copies(hs, ls, outs[:3 * n], outs[3 * n:6 * n]):
            cp.start()
        token = outs[8 * n]
        token[...] = jnp.zeros_like(token)

    outs = pl.pallas_call(
        body, name=name,
        out_shape=([pltpu.SemaphoreType.DMA(())] * (6 * n) + [pltpu.HBM(h.shape, h.dtype) for h in h4s]
                   + [pltpu.HBM(l.shape, l.dtype) for l in lands] + [jax.ShapeDtypeStruct((8, 128), F32)]),
        in_specs=[HBM_SPEC] * (2 * n),
        out_specs=[SEM_SPEC] * (6 * n) + [HBM_SPEC] * (2 * n) + [pl.BlockSpec(memory_space=pltpu.VMEM)],
        input_output_aliases={w: 6 * n + w for w in range(2 * n)},
        compiler_params=pltpu.CompilerParams(has_side_effects=DATAFLOW),
    )(*[pltpu.with_memory_space_constraint(a, pltpu.HBM) for a in h4s + lands])
    return outs[:6 * n], outs[6 * n:7 * n], outs[7 * n:8 * n], outs[8 * n]


def _rs_chip_exchange_wait(sems, h_thru, l_thru, after, name):
    n = len(h_thru)

    def body(*refs):
        hs, ls, ss = refs[:n], refs[n:2 * n], refs[2 * n:8 * n]
        for cp in _chip_copies(hs, ls, ss[:3 * n], ss[3 * n:]):
            cp.wait_send()
            cp.wait_recv()

    outs = pl.pallas_call(
        body, name=name,
        out_shape=[pltpu.HBM(a.shape, a.dtype) for a in h_thru + l_thru],
        in_specs=[HBM_SPEC] * (2 * n) + [SEM_SPEC] * (6 * n) + [pl.BlockSpec(memory_space=pl.ANY)],
        out_specs=[HBM_SPEC] * (2 * n),
        input_output_aliases={w: w for w in range(2 * n)},
        compiler_params=pltpu.CompilerParams(has_side_effects=DATAFLOW),
    )(*h_thru, *l_thru, *sems, after)
    return outs[n:]


def _rs_pair_share(fins, name):
    n = len(fins)

    def body(*refs):
        fs, outs = refs[:n], refs[n:2 * n]
        send_sems, recv_sems = refs[2 * n:]
        x, y, c = lax.axis_index("x"), lax.axis_index("y"), lax.axis_index("c")
        cps = [pltpu.make_async_remote_copy(
            src_ref=fs[w], dst_ref=outs[w], send_sem=send_sems.at[w], recv_sem=recv_sems.at[w],
            device_id=(x, y, 1 - c), device_id_type=MESH) for w in range(n)]
        for cp in cps:
            cp.start()
        for cp in cps:
            cp.wait_recv()
        for cp in cps:
            cp.wait_send()

    any_spec = pl.BlockSpec(memory_space=pl.ANY)
    return pl.pallas_call(
        body, name=name,
        out_shape=[jax.ShapeDtypeStruct(f.shape, f.dtype) for f in fins],
        in_specs=[any_spec] * n, out_specs=[any_spec] * n,
        scratch_shapes=[pltpu.SemaphoreType.DMA((n,)), pltpu.SemaphoreType.DMA((n,))])(*fins)


def _rs_add_pair(g8, r4, c, name):
    R, n = g8.shape[1:]
    rb = R // 2

    def body(c_ref, g_ref, r_ref, o_ref):
        o_ref[...] = (g_ref[...] + r_ref[...]).astype(BF16)

    spec = pl.BlockSpec((1, rb, n), lambda j, i, c_ref: (j, i, 0))
    return pl.pallas_call(
        body, name=name,
        grid_spec=pltpu.PrefetchScalarGridSpec(
            num_scalar_prefetch=1, grid=(4, R // rb),
            in_specs=[pl.BlockSpec((1, rb, n), lambda j, i, c_ref: (2 * j + c_ref[0], i, 0)), spec],
            out_specs=spec),
        out_shape=jax.ShapeDtypeStruct((4, R, n), BF16), compiler_params=_cp(2))(c, g8, r4)


def _rs_add_chips(g8, r4, r3, cj, name):
    R, n = g8.shape[1:]
    rb = R // 2

    def body(cj_ref, g_ref, p_ref, r_ref, o_ref):
        own = g_ref[0] + p_ref[0]
        o_ref[...] = ((own + r_ref[0].astype(F32)) + r_ref[1].astype(F32)) + r_ref[2].astype(F32)

    return pl.pallas_call(
        body, name=name,
        grid_spec=pltpu.PrefetchScalarGridSpec(
            num_scalar_prefetch=1, grid=(R // rb,),
            in_specs=[pl.BlockSpec((1, rb, n), lambda i, cj_ref: (2 * cj_ref[1] + cj_ref[0], i, 0)),
                      pl.BlockSpec((1, rb, n), lambda i, cj_ref: (cj_ref[1], i, 0)),
                      pl.BlockSpec((3, rb, n), lambda i, cj_ref: (0, i, 0))],
            out_specs=pl.BlockSpec((rb, n), lambda i, cj_ref: (i, 0))),
        out_shape=jax.ShapeDtypeStruct((R, n), F32), compiler_params=_cp(1))(cj, g8, r4, r3)


def _sum8(g):
    n = g.shape[1]

    def body(g_ref, o_ref):
        acc = g_ref[0:1, :]
        for k in range(1, 8):
            acc = acc + g_ref[k:k + 1, :]
        o_ref[...] = acc

    return pl.pallas_call(body, name="sum_devices", out_shape=jax.ShapeDtypeStruct((1, n), F32),
                          compiler_params=pltpu.CompilerParams(vmem_limit_bytes=VMEM_LIMIT))(g)


ADA_ROWS = 64


def _ada_fwd(cs, w, b):
    n = w.shape[1]

    def body(c_ref, w_ref, b_ref, o_ref):
        cv = c_ref[...]
        s = (cv * _sigmoid(cv)).astype(BF16)
        o_ref[...] = _dot(s, w_ref[...].astype(BF16)) + b_ref[...]

    return pl.pallas_call(body, name="ada_fwd", out_shape=jax.ShapeDtypeStruct((ADA_ROWS, n), F32),
                          compiler_params=pltpu.CompilerParams(vmem_limit_bytes=VMEM_LIMIT))(cs, w, b)


def _ada_bwd(cs, w, dm):
    n = w.shape[1]

    def body(c_ref, w_ref, dm_ref, gw_ref, dc_ref):
        cv = c_ref[...]
        s = (cv * _sigmoid(cv)).astype(BF16)
        gw_ref[...] = _dot_tn(s, dm_ref[...].astype(BF16))
        dc_ref[...] = _dot_nt(dm_ref[32:40, :].astype(BF16), w_ref[...].astype(BF16))

    return pl.pallas_call(
        body, name="ada_bwd",
        out_shape=[jax.ShapeDtypeStruct((D, n), F32), jax.ShapeDtypeStruct((8, D), F32)],
        compiler_params=pltpu.CompilerParams(vmem_limit_bytes=VMEM_LIMIT))(cs, w, dm)


def _adamw(w, g, m, v, name):
    r, c = w.shape
    rb = r
    if r % 8 == 0 and r * c * 4 > (1 << 20):
        rb = 8
        for cand in range(8, r, 8):
            if r % cand == 0 and cand * c * 4 <= (1 << 20):
                rb = cand

    def body(w_ref, g_ref, m_ref, v_ref, d_ref, nm_ref, nv_ref):
        gv = g_ref[...]
        mn = ADAM_B1 * m_ref[...] + (1.0 - ADAM_B1) * gv
        vn = ADAM_B2 * v_ref[...] + (1.0 - ADAM_B2) * (gv * gv)
        m_hat = mn / (1.0 - ADAM_B1 ** ADAM_STEP)
        v_hat = vn / (1.0 - ADAM_B2 ** ADAM_STEP)
        d_ref[...] = -ADAM_LR * (m_hat / (jnp.sqrt(v_hat) + ADAM_EPS) + ADAM_WD * w_ref[...])
        nm_ref[...] = mn
        nv_ref[...] = vn

    spec = pl.BlockSpec((rb, c), lambda i: (i, 0))
    shp = jax.ShapeDtypeStruct((r, c), F32)
    return pl.pallas_call(body, name=name, grid=(r // rb,), in_specs=[spec] * 4, out_specs=[spec] * 3,
                          out_shape=[shp] * 3, compiler_params=_cp(1))(w, g, m, v)


BIG = ("ffn1_w_in", "ffn1_w_out", "w_mix_in", "w_mix_out", "ffn2_w_in", "ffn2_w_out")


def _half_rows(w, c):
    half = w.shape[0] // 2
    return lax.dynamic_slice_in_dim(w, c * half, half, axis=0)


def _lower_bounds(logits):
    return jnp.cumsum(jax.nn.softmax(logits.astype(F32), axis=1), axis=1)[:, 0]


def kernel(x, c, ctx, c_ctx, w_ada, b_ada, ln_gain, ln_bias, ffn1_w_in, ffn1_w_out, w_mix_in, gla_a2_fwd, gla_a2_bwd, gla_a_bias_fwd, gla_a_bias_bwd, hgrn_lb_logits, gla_norm_gain, hgrn_norm_gain, w_mix_out, ffn2_w_in, ffn2_w_out, loss_target, m_c_ctx, m_w_ada, m_b_ada, m_ln_gain, m_ln_bias, m_ffn1_w_in, m_ffn1_w_out, m_w_mix_in, m_gla_a2_fwd, m_gla_a2_bwd, m_gla_a_bias_fwd, m_gla_a_bias_bwd, m_hgrn_lb_logits, m_gla_norm_gain, m_hgrn_norm_gain, m_w_mix_out, m_ffn2_w_in, m_ffn2_w_out, v_c_ctx, v_w_ada, v_b_ada, v_ln_gain, v_ln_bias, v_ffn1_w_in, v_ffn1_w_out, v_w_mix_in, v_gla_a2_fwd, v_gla_a2_bwd, v_gla_a_bias_fwd, v_gla_a_bias_bwd, v_hgrn_lb_logits, v_gla_norm_gain, v_hgrn_norm_gain, v_w_mix_out, v_ffn2_w_in, v_ffn2_w_out):
    xi, yi, ci = lax.axis_index("x"), lax.axis_index("y"), lax.axis_index("c")
    chip = 2 * xi + yi
    dev = 2 * chip + ci
    B = x.shape[0]
    weights = dict(ffn1_w_in=ffn1_w_in[0], ffn1_w_out=ffn1_w_out[0], w_mix_in=w_mix_in[0], w_mix_out=w_mix_out[0],
                   ffn2_w_in=ffn2_w_in[0], ffn2_w_out=ffn2_w_out[0])

    mine = jnp.concatenate([c.reshape(-1), ln_gain.reshape(-1), ln_bias.reshape(-1), gla_a2_fwd.reshape(-1),
                            gla_a2_bwd.reshape(-1), hgrn_lb_logits.reshape(-1)])
    g1 = _gather_flat(mine, "gather_cond")
    nc = B * D
    c_all = g1[:, :nc].reshape(8 * B, D)
    per_chip = g1[0::2, nc:]
    o = 0

    def take(shape, axis):
        nonlocal o
        n = int(np.prod(shape))
        parts = per_chip[:, o:o + n].reshape((4,) + shape)
        o += n
        return jnp.concatenate([parts[j] for j in range(4)], axis=axis)

    ln_gain_f = take((3, 256), 1)
    ln_bias_f = take((3, 256), 1)
    a2f_f = take((16, 64), 1)
    a2b_f = take((16, 64), 1)
    lbl_f = take((2, 2, 128), 2)
    lb, lb_vjp = jax.vjp(_lower_bounds, lbl_f)

    shards = dict(weights, w_mix_in=jnp.pad(weights["w_mix_in"], ((0, 0), (0, MIX_NP - MIX_N))))
    blks = [_half_rows(shards[k], ci).astype(BF16) for k in BIG]
    full = {k: lax.dynamic_update_index_in_dim(g, own, dev, 0)
            for k, own, g in zip(BIG, blks, _big_allgather(blks, "weight_allgather"))}
    w1i, w2i = (full[k].reshape((4,) + shards[k].shape) for k in ("ffn1_w_in", "ffn2_w_in"))
    w1o, wmo, w2o = (full[k].reshape(-1, D) for k in ("ffn1_w_out", "w_mix_out", "ffn2_w_out"))
    wmp = _mix_in_to_padded(full["w_mix_in"].reshape(4, D, MIX_NP))

    cs = jnp.concatenate([c_all, c_ctx.reshape(1, D), jnp.zeros((ADA_ROWS - 8 * B - 1, D), F32)], axis=0)
    ncol = w_ada.shape[2]
    b_cols = lax.dynamic_slice_in_dim(b_ada, chip * ncol, ncol, axis=1)
    m_cols = _ada_fwd(cs, w_ada[0], b_cols)
    g2 = _small_allgather(m_cols, "gather_mod")[0::2]
    m_all = jnp.concatenate([g2[j] for j in range(4)], axis=1)
    m_lat = lax.dynamic_slice_in_dim(m_all, dev * B, B, axis=0).reshape(B, 1, N_MOD, D)
    m_ctx = jnp.broadcast_to(m_all[8 * B].reshape(1, 1, N_MOD, D), (B, 1, N_MOD, D))
    mvec = jnp.concatenate([m_ctx, m_lat], axis=1)

    cvec = ci.reshape(1).astype(jnp.int32)
    cjvec = jnp.stack([ci, chip]).astype(jnp.int32)
    groups = dict(ffn2=("ffn2_w_in", "ffn2_w_out"), mix=("w_mix_in", "w_mix_out"), ffn1=("ffn1_w_in", "ffn1_w_out"))
    in_flight = {}

    def on_grads(group, gs):
        names = groups[group]
        if group == "mix":
            gs = (_mix_in_from_padded(gs[0]), gs[1])
        g8s = [g.reshape((8, shards[k].shape[0] // 2, shards[k].shape[1])) for k, g in zip(names, gs)]
        r4s = _rs_pair_exchange(g8s, "grad_pair_exchange_" + group)
        h4s = [_rs_add_pair(g, r, cvec, "grad_pair_add_" + k) for k, g, r in zip(names, g8s, r4s)]
        sems, h_thru, l_thru, token = _rs_chip_exchange_start(h4s, "grad_chip_start_" + group)
        in_flight[group] = (g8s, r4s, sems, h_thru, l_thru)
        return token

    loss_l, grad_x, dm, small = _local_step(
        x, ctx, loss_target, mvec, w1i, w1o, wmp, wmo, w2i, w2o, ln_gain_f, ln_bias_f, a2f_f, a2b_f,
        gla_a_bias_fwd, gla_a_bias_bwd, lb, gla_norm_gain, hgrn_norm_gain, on_grads)
    loss = lax.psum(loss_l, ("x", "y", "c"))

    dm_lat = dm[:, 1].reshape(B, N_MOD * D)
    dm_ctx = jnp.sum(dm[:, 0], axis=0).reshape(N_MOD * D)
    keys = ("ln_gain", "ln_bias", "a2f", "a2b", "abf", "abb", "lb", "gng", "gnh")
    flat = jnp.concatenate([dm_lat.reshape(-1), dm_ctx] + [small[k].reshape(-1) for k in keys])
    g3 = _gather_flat(flat, "gather_small_grads")
    nlat = B * N_MOD * D
    dm_all = g3[:, :nlat].reshape(8 * B, N_MOD * D)
    tot = _sum8(g3[:, nlat:])[0]
    dmc_tot = tot[:N_MOD * D]
    o = N_MOD * D
    sg = {}
    for k in keys:
        n = int(np.prod(small[k].shape))
        sg[k] = tot[o:o + n].reshape(small[k].shape)
        o += n
    dm_rows = jnp.concatenate([dm_all, dmc_tot.reshape(1, -1), jnp.zeros((ADA_ROWS - 8 * B - 1, N_MOD * D), F32)], axis=0)
    g_b_ada = (jnp.sum(dm_all, axis=0) + dmc_tot).reshape(1, N_MOD * D)
    g_w_ada, dcc = _ada_bwd(cs, w_ada[0], lax.dynamic_slice_in_dim(dm_rows, chip * ncol, ncol, axis=1))
    g4 = _gather_flat(dcc[0], "gather_cctx")
    dsilu = ((g4[0] + g4[2]) + g4[4]) + g4[6]
    sc = _sigmoid(c_ctx)
    g_c_ctx = dsilu * (sc * (1.0 + c_ctx * (1.0 - sc)))
    (g_lbl,) = lb_vjp(sg["lb"])

    def cols(a, n, axis):
        return lax.dynamic_slice_in_dim(a, chip * n, n, axis=axis)

    fin = {}
    for group in ("ffn2", "mix", "ffn1"):
        g8s, r4s, sems, h_thru, l_thru = in_flight[group]
        r3s = _rs_chip_exchange_wait(sems, h_thru, l_thru, g_w_ada, "grad_chip_wait_" + group)
        for k, g, r4, r3 in zip(groups[group], g8s, r4s, r3s):
            fin[k] = _rs_add_chips(g, r4, r3, cjvec, "grad_chip_add_" + k)
    fins = [fin[k] for k in BIG]
    gsh = {}
    for k, own, sib in zip(BIG, fins, _rs_pair_share(fins, "grad_pair_share")):
        both = jnp.where(ci == 0, jnp.stack([own, sib]), jnp.stack([sib, own]))
        gsh[k] = both.reshape(shards[k].shape)[:, :weights[k].shape[1]]

    grads = dict(
        c_ctx=g_c_ctx, w_ada=g_w_ada[None], b_ada=g_b_ada, ln_gain=cols(sg["ln_gain"], 256, 1)[None],
        ln_bias=cols(sg["ln_bias"], 256, 1)[None], ffn1_w_in=gsh["ffn1_w_in"][None], ffn1_w_out=gsh["ffn1_w_out"][None],
        w_mix_in=gsh["w_mix_in"][None], gla_a2_fwd=cols(sg["a2f"], 64, 1)[None], gla_a2_bwd=cols(sg["a2b"], 64, 1)[None],
        gla_a_bias_fwd=sg["abf"], gla_a_bias_bwd=sg["abb"], hgrn_lb_logits=cols(g_lbl, 128, 2),
        gla_norm_gain=sg["gng"].reshape(1, HD), hgrn_norm_gain=sg["gnh"].reshape(1, HD),
        w_mix_out=gsh["w_mix_out"][None], ffn2_w_in=gsh["ffn2_w_in"][None], ffn2_w_out=gsh["ffn2_w_out"][None])
    params = dict(
        c_ctx=(c_ctx, m_c_ctx, v_c_ctx), w_ada=(w_ada, m_w_ada, v_w_ada), b_ada=(b_ada, m_b_ada, v_b_ada),
        ln_gain=(ln_gain, m_ln_gain, v_ln_gain), ln_bias=(ln_bias, m_ln_bias, v_ln_bias),
        ffn1_w_in=(ffn1_w_in, m_ffn1_w_in, v_ffn1_w_in), ffn1_w_out=(ffn1_w_out, m_ffn1_w_out, v_ffn1_w_out),
        w_mix_in=(w_mix_in, m_w_mix_in, v_w_mix_in), gla_a2_fwd=(gla_a2_fwd, m_gla_a2_fwd, v_gla_a2_fwd),
        gla_a2_bwd=(gla_a2_bwd, m_gla_a2_bwd, v_gla_a2_bwd),
        gla_a_bias_fwd=(gla_a_bias_fwd, m_gla_a_bias_fwd, v_gla_a_bias_fwd),
        gla_a_bias_bwd=(gla_a_bias_bwd, m_gla_a_bias_bwd, v_gla_a_bias_bwd),
        hgrn_lb_logits=(hgrn_lb_logits, m_hgrn_lb_logits, v_hgrn_lb_logits),
        gla_norm_gain=(gla_norm_gain, m_gla_norm_gain, v_gla_norm_gain),
        hgrn_norm_gain=(hgrn_norm_gain, m_hgrn_norm_gain, v_hgrn_norm_gain),
        w_mix_out=(w_mix_out, m_w_mix_out, v_w_mix_out), ffn2_w_in=(ffn2_w_in, m_ffn2_w_in, v_ffn2_w_in),
        ffn2_w_out=(ffn2_w_out, m_ffn2_w_out, v_ffn2_w_out))
    order = list(params.keys())
    big_names = ("w_ada",) + BIG
    upd = {}
    for k in big_names:
        w_, m_, v_ = params[k]
        s2 = w_.shape[-2:]
        d_, nm_, nv_ = _adamw(w_.reshape(s2), grads[k].reshape(s2), m_.reshape(s2), v_.reshape(s2), "adamw_" + k)
        upd[k] = (d_.reshape(w_.shape), nm_.reshape(w_.shape), nv_.reshape(w_.shape))
    small_names = [k for k in order if k not in big_names]
    sizes = [int(np.prod(params[k][0].shape)) for k in small_names]
    tot_n = sum(sizes)
    npad = -(-tot_n // 1024) * 1024

    def packed(get):
        flat_ = jnp.concatenate([get(k).reshape(-1) for k in small_names])
        return jnp.pad(flat_, (0, npad - tot_n)).reshape(8, npad // 8)

    d_s, nm_s, nv_s = _adamw(packed(lambda k: params[k][0]), packed(lambda k: grads[k]),
                             packed(lambda k: params[k][1]), packed(lambda k: params[k][2]), "adamw_small")
    o = 0
    for k, n in zip(small_names, sizes):
        shp = params[k][0].shape
        upd[k] = tuple(a.reshape(-1)[o:o + n].reshape(shp) for a in (d_s, nm_s, nv_s))
        o += n

    return (loss, grad_x, *[grads[k].reshape(params[k][0].shape) for k in order], *[upd[k][0] for k in order],
            *[upd[k][1] for k in order], *[upd[k][2] for k in order])
```

```python
import functools

import numpy as np
import jax
import jax.numpy as jnp
from jax import lax
from jax.experimental import pallas as pl
from jax.experimental.pallas import tpu as pltpu

F32 = jnp.float32
BF16 = jnp.bfloat16
MESH = pl.DeviceIdType.MESH

D = 1024
DFF = 2816
TM = 256
CH = 32
NCB = TM // CH
SB = 128
CSB = SB // CH
NSB = TM // SB
HP = 4
HD = 128
NH = 8
LN_EPS = 1e-5
NORM_EPS = 1e-6
ALPHA = 2.0 ** 0.25
GATE_NORM = 16.0
GLA_DK = 64
N_MOD = 9
VMEM_LIMIT = 52 * 1024 * 1024

MIXP = 5120
GG, RG, GQ, GK, GV, RQ, RFF, RFB, RI, LR = 0, 512, 1024, 1536, 2048, 2560, 3072, 3584, 4096, 4608
IN_SPLITS = (256, 256, 512, 512, 16, 16, 512, 512, 512, 512, 512)

ADAM_LR, ADAM_B1, ADAM_B2, ADAM_EPS, ADAM_WD, ADAM_STEP = 0.001, 0.9, 0.999, 1e-08, 0.01, 10


def _cp(n_axes):
    return pltpu.CompilerParams(dimension_semantics=("arbitrary",) * n_axes, vmem_limit_bytes=VMEM_LIMIT)


def _rowmap(stride, off):
    return lambda b, i: (b * stride + off + i, 0)


def _mmap(comb):
    if comb:
        return lambda b, i: (b, jnp.minimum(i, 1), 0, 0)
    return lambda b, i: (b, 1, 0, 0)


def _ln(x):
    mu = jnp.mean(x, axis=-1, keepdims=True)
    xc = x - mu
    var = jnp.mean(xc * xc, axis=-1, keepdims=True)
    r = lax.rsqrt(var + LN_EPS)
    return xc * r, r


def _ln_bwd(dxh, xh, r):
    return r * (dxh - jnp.mean(dxh, axis=-1, keepdims=True) - xh * jnp.mean(dxh * xh, axis=-1, keepdims=True))


def _sigmoid(x):
    return 1.0 / (1.0 + jnp.exp(-x))


def _rsum(x):
    return jnp.sum(x, axis=0, keepdims=True)


def _dot(a, b):
    return jnp.dot(a, b, preferred_element_type=F32)


def _dot_nt(a, b):
    return lax.dot_general(a, b, (((1,), (1,)), ((), ())), preferred_element_type=F32)


def _dot_tn(a, b):
    return lax.dot_general(a, b, (((0,), (0,)), ((), ())), preferred_element_type=F32)


def _modulate(xv, m_ref, sub):
    xh, _ = _ln(xv)
    sh = m_ref[0, 0, 3 * sub:3 * sub + 1, :]
    sc = m_ref[0, 0, 3 * sub + 1:3 * sub + 2, :]
    return (xh * (1.0 + sh) + sc).astype(BF16)


def _embed_lnmod(x, ctx, pos, mvec):
    B, T, _ = x.shape
    nt = 1 + T // TM

    def body(x_ref, c_ref, p_ref, m_ref, o_ref, h_ref):
        i = pl.program_id(1)

        @pl.when(i == 0)
        def _():
            o_ref[...] = c_ref[0]

        @pl.when(i > 0)
        def _():
            o_ref[...] = x_ref[0] + p_ref[...]

        h_ref[...] = _modulate(o_ref[...], m_ref, 0)

    rows = pl.BlockSpec((TM, D), lambda b, i: (b * nt + i, 0))
    return pl.pallas_call(
        body, name="embed_lnmod0", grid=(B, nt),
        in_specs=[pl.BlockSpec((1, TM, D), lambda b, i: (b, jnp.maximum(i - 1, 0), 0)),
                  pl.BlockSpec((1, TM, D), lambda b, i: (b, 0, 0)),
                  pl.BlockSpec((TM, D), lambda b, i: (jnp.maximum(i - 1, 0), 0)),
                  pl.BlockSpec((1, 1, N_MOD, D), _mmap(True))],
        out_specs=[rows, rows],
        out_shape=[jax.ShapeDtypeStruct((B * nt * TM, D), F32), jax.ShapeDtypeStruct((B * nt * TM, D), BF16)],
        compiler_params=_cp(2))(x, ctx, pos, mvec)


def _lnmod0_bwd(dh, X, mvec, dres, x_shape, B, nt):
    def body(dh_ref, x_ref, m_ref, dr_ref, dx_ref, dm_ref):
        i = pl.program_id(1)
        xh, r = _ln(x_ref[...])
        sh = m_ref[0, 0, 0:1, :]
        dhv = dh_ref[...]

        @pl.when((i == 0) | (i == 1))
        def _():
            dm_ref[...] = jnp.zeros_like(dm_ref)

        dm_ref[0, 0, 0:1, :] += _rsum(dhv * xh)
        dm_ref[0, 0, 1:2, :] += _rsum(dhv)

        @pl.when(i > 0)
        def _():
            dx_ref[0] = _ln_bwd(dhv * (1.0 + sh), xh, r) + dr_ref[...]

    rows = pl.BlockSpec((TM, D), _rowmap(nt, 0))
    return pl.pallas_call(
        body, name="lnmod0_bwd", grid=(B, nt),
        in_specs=[rows, rows, pl.BlockSpec((1, 1, N_MOD, D), _mmap(True)), rows],
        out_specs=[pl.BlockSpec((1, TM, D), lambda b, i: (b, jnp.maximum(i - 1, 0), 0)),
                   pl.BlockSpec((1, 1, 2, D), _mmap(True))],
        out_shape=[jax.ShapeDtypeStruct(x_shape, F32), jax.ShapeDtypeStruct((B, 2, 2, D), F32)],
        compiler_params=_cp(2))(dh, X, mvec, dres)


def _resid_fwd(x_ref, y_ref, m_ref, gb_ref, sub, w):
    wg = w * m_ref[0, 0, 3 * sub + 2:3 * sub + 3, :]
    y = y_ref[...]
    zh, r = _ln(ALPHA * x_ref[...] + wg * y)
    return y, wg, zh, r


def _resid_grads(do, y, wg, zh, r, w, gb_ref, comb, dx_ref, dy_ref, dg_ref, dgb_ref):
    b_, i = pl.program_id(0), pl.program_id(1)
    dz = _ln_bwd(do * gb_ref[0:1, :], zh, r)
    dx_ref[...] = ALPHA * dz
    dy_ref[...] = (wg * dz).astype(BF16)

    @pl.when((b_ == 0) & (i == 0))
    def _():
        dgb_ref[...] = jnp.zeros_like(dgb_ref)

    dgb_ref[0:1, :] += _rsum(do * zh)
    dgb_ref[1:2, :] += _rsum(do)

    init = (i == 0) | (i == 1) if comb else (i == 0)

    @pl.when(init)
    def _():
        dg_ref[...] = jnp.zeros_like(dg_ref)

    dg_ref[0, 0] += w * _rsum(dz * y)


def _resid_lnmod(X, lx, Y, mvec, comb, sub, w, gb, B, nt, name):
    def body(x_ref, y_ref, m_ref, gb_ref, o_ref, h_ref):
        _, _, zh, _ = _resid_fwd(x_ref, y_ref, m_ref, gb_ref, sub, w)
        xn = zh * gb_ref[0:1, :] + gb_ref[1:2, :]
        o_ref[...] = xn
        h_ref[...] = _modulate(xn, m_ref, sub + 1)

    rows = pl.BlockSpec((TM, D), _rowmap(nt, 0))
    return pl.pallas_call(
        body, name=name, grid=(B, nt),
        in_specs=[pl.BlockSpec((TM, D), _rowmap(*lx)), rows,
                  pl.BlockSpec((1, 1, N_MOD, D), _mmap(comb)), pl.BlockSpec((2, D), lambda b, i: (0, 0))],
        out_specs=[rows, rows],
        out_shape=[jax.ShapeDtypeStruct((B * nt * TM, D), F32), jax.ShapeDtypeStruct((B * nt * TM, D), BF16)],
        compiler_params=_cp(2))(X, Y, mvec, gb)


def _resid_out_shapes(B, nt, comb):
    rows = pl.BlockSpec((TM, D), _rowmap(nt, 0))
    specs = [rows, rows, pl.BlockSpec((1, 1, 1, D), _mmap(comb)), pl.BlockSpec((2, D), lambda b, i: (0, 0))]
    shapes = [jax.ShapeDtypeStruct((B * nt * TM, D), F32), jax.ShapeDtypeStruct((B * nt * TM, D), BF16),
              jax.ShapeDtypeStruct((B, 2, 1, D), F32), jax.ShapeDtypeStruct((2, D), F32)]
    return specs, shapes


def _tail(X, Y, mvec, gb, tgt, sub, w, B, nt):
    def body(x_ref, y_ref, m_ref, gb_ref, t_ref, dx_ref, dy_ref, dg_ref, dgb_ref, l_ref):
        y, wg, zh, r = _resid_fwd(x_ref, y_ref, m_ref, gb_ref, sub, w)
        e = (zh * gb_ref[0:1, :] + gb_ref[1:2, :]) - t_ref[0]

        @pl.when((pl.program_id(0) == 0) & (pl.program_id(1) == 0))
        def _():
            l_ref[...] = jnp.zeros_like(l_ref)

        l_ref[...] += _rsum(e * e)
        _resid_grads(e * (1.0 / D), y, wg, zh, r, w, gb_ref, False, dx_ref, dy_ref, dg_ref, dgb_ref)

    rows = pl.BlockSpec((TM, D), _rowmap(nt, 0))
    specs, shapes = _resid_out_shapes(B, nt, False)
    return pl.pallas_call(
        body, name="resid2_loss_bwd", grid=(B, nt),
        in_specs=[rows, rows, pl.BlockSpec((1, 1, N_MOD, D), _mmap(False)), pl.BlockSpec((2, D), lambda b, i: (0, 0)),
                  pl.BlockSpec((1, TM, D), lambda b, i: (b, i, 0))],
        out_specs=specs + [pl.BlockSpec((1, D), lambda b, i: (0, 0))],
        out_shape=shapes + [jax.ShapeDtypeStruct((1, D), F32)],
        compiler_params=_cp(2))(X, Y, mvec, gb, tgt)


def _lnmod_resid_bwd(dh, Xi, lxi, dres, Xp, lxp, Yp, mvec, comb, sub, w, gb, B, nt, name):
    ntl = nt - 1 if comb else nt

    def body(dh_ref, xi_ref, dr_ref, xp_ref, yp_ref, m_ref, gb_ref, dx_ref, dy_ref, dg_ref, dgb_ref, dm_ref):
        i = pl.program_id(1)
        xh, r = _ln(xi_ref[...])
        sh = m_ref[0, 0, 3 * sub:3 * sub + 1, :]
        dhv = dh_ref[...]
        dr = dr_ref[...]
        if comb:
            dr = jnp.where(i > 0, dr, 0.0)
        do = _ln_bwd(dhv * (1.0 + sh), xh, r) + dr

        init = (i == 0) | (i == 1) if comb else (i == 0)

        @pl.when(init)
        def _():
            dm_ref[...] = jnp.zeros_like(dm_ref)

        dm_ref[0, 0, 0:1, :] += _rsum(dhv * xh)
        dm_ref[0, 0, 1:2, :] += _rsum(dhv)
        y, wg, zh, r2 = _resid_fwd(xp_ref, yp_ref, m_ref, gb_ref, sub - 1, w)
        _resid_grads(do, y, wg, zh, r2, w, gb_ref, comb, dx_ref, dy_ref, dg_ref, dgb_ref)

    rows = pl.BlockSpec((TM, D), _rowmap(nt, 0))
    if comb:
        dres_spec = pl.BlockSpec((TM, D), lambda b, i: (b * ntl + jnp.maximum(i - 1, 0), 0))
    else:
        dres_spec = rows
    specs, shapes = _resid_out_shapes(B, nt, comb)
    return pl.pallas_call(
        body, name=name, grid=(B, nt),
        in_specs=[rows, pl.BlockSpec((TM, D), _rowmap(*lxi)), dres_spec, pl.BlockSpec((TM, D), _rowmap(*lxp)), rows,
                  pl.BlockSpec((1, 1, N_MOD, D), _mmap(comb)), pl.BlockSpec((2, D), lambda b, i: (0, 0))],
        out_specs=specs + [pl.BlockSpec((1, 1, 2, D), _mmap(comb))],
        out_shape=shapes + [jax.ShapeDtypeStruct((B, 2, 2, D), F32)],
        compiler_params=_cp(2))(dh, Xi, dres, Xp, Yp, mvec, gb)


def _ffn_out_dx(dy, W, u, name):
    M = dy.shape[0]
    half = DFF // 2

    def body(dy_ref, w_ref, u_ref, du_ref):
        dyv = dy_ref[...]
        for j in range(2):
            lo, hi = j * half, (j + 1) * half
            da = _dot_nt(dyv, w_ref[lo:hi, :])
            g = u_ref[:, lo:hi].astype(F32)
            up = u_ref[:, DFF + lo:DFF + hi].astype(F32)
            s = _sigmoid(g)
            du_ref[:, lo:hi] = (da * up * (s * (1.0 + g * (1.0 - s)))).astype(BF16)
            du_ref[:, DFF + lo:DFF + hi] = (da * (g * s)).astype(BF16)

    return pl.pallas_call(
        body, name=name, grid=(M // TM,),
        in_specs=[pl.BlockSpec((TM, D), lambda i: (i, 0)), _wspec(W, 1), pl.BlockSpec((TM, 2 * DFF), lambda i: (i, 0))],
        out_specs=pl.BlockSpec((TM, 2 * DFF), lambda i: (i, 0)),
        out_shape=jax.ShapeDtypeStruct((M, 2 * DFF), BF16), compiler_params=_cp(1))(dy, W, u)


def _wspec(W, nidx):
    zeros = (0,) * W.ndim
    if nidx == 1:
        return pl.BlockSpec(W.shape, lambda i: zeros)
    return pl.BlockSpec(W.shape, lambda b, i: zeros)


def _mm_nn(A, la, W, B, nt, out_dtype, name):
    K, N = W.shape

    def body(a_ref, w_ref, o_ref):
        o_ref[...] = _dot(a_ref[...], w_ref[...]).astype(out_dtype)

    return pl.pallas_call(
        body, name=name, grid=(B, nt),
        in_specs=[pl.BlockSpec((TM, K), _rowmap(*la)), _wspec(W, 2)],
        out_specs=pl.BlockSpec((TM, N), _rowmap(nt, 0)),
        out_shape=jax.ShapeDtypeStruct((B * nt * TM, N), out_dtype), compiler_params=_cp(2))(A, W)


def _ffn_in(A, la, W3, B, nt, name):
    K, n = W3.shape[1:]

    def body(a_ref, w_ref, u_ref, s_ref):
        a = a_ref[...]
        for j in range(2):
            g = _dot(a, w_ref[j])
            up = _dot(a, w_ref[j + 2])
            u_ref[:, j * n:(j + 1) * n] = g.astype(BF16)
            u_ref[:, (j + 2) * n:(j + 3) * n] = up.astype(BF16)
            s_ref[:, j * n:(j + 1) * n] = (g * _sigmoid(g) * up).astype(BF16)

    rows = B * nt * TM
    return pl.pallas_call(
        body, name=name, grid=(B, nt),
        in_specs=[pl.BlockSpec((TM, K), _rowmap(*la)), _wspec(W3, 2)],
        out_specs=[pl.BlockSpec((TM, 4 * n), _rowmap(nt, 0)), pl.BlockSpec((TM, 2 * n), _rowmap(nt, 0))],
        out_shape=[jax.ShapeDtypeStruct((rows, 4 * n), BF16), jax.ShapeDtypeStruct((rows, 2 * n), BF16)],
        compiler_params=_cp(2))(A, W3)


def _mm_nt(A, W, name, dep=None):
    M, N = A.shape
    K = W.shape[-2]

    def body(a_ref, w_ref, *rest):
        o_ref = rest[-1]
        if W.ndim == 3:
            n = W.shape[-1]
            acc = _dot_nt(a_ref[:, 0:n], w_ref[0])
            for j in range(1, 4):
                acc = acc + _dot_nt(a_ref[:, j * n:(j + 1) * n], w_ref[j])
            o_ref[...] = acc
        else:
            o_ref[...] = _dot_nt(a_ref[...], w_ref[...])

    deps = [] if dep is None else [dep]
    return pl.pallas_call(
        body, name=name, grid=(M // TM,),
        in_specs=[pl.BlockSpec((TM, N), lambda i: (i, 0)), _wspec(W, 1)] + [_wspec(d, 1) for d in deps],
        out_specs=pl.BlockSpec((TM, K), lambda i: (i, 0)),
        out_shape=jax.ShapeDtypeStruct((M, K), F32), compiler_params=_cp(1))(A, W, *deps)


def _mm_tn(A, G, name, tk=512, tn=512, shards=None):
    M, K = A.shape
    N = G.shape[1]
    if shards:
        tn = N // shards

    def body(a_ref, g_ref, o_ref):
        @pl.when(pl.program_id(1) == 0)
        def _():
            o_ref[...] = jnp.zeros_like(o_ref)

        upd = _dot_tn(a_ref[...], g_ref[...])
        if shards:
            o_ref[0] += upd
        else:
            o_ref[...] += upd

    if shards:
        out_spec = pl.BlockSpec((1, K, tn), lambda n, k: (n, 0, 0))
        out_shape = jax.ShapeDtypeStruct((shards, K, tn), F32)
    else:
        out_spec = pl.BlockSpec((K, tn), lambda n, k: (0, n))
        out_shape = jax.ShapeDtypeStruct((K, N), F32)
    return pl.pallas_call(
        body, name=name, grid=(N // tn, M // tk),
        in_specs=[pl.BlockSpec((tk, K), lambda n, k: (k, 0)), pl.BlockSpec((tk, tn), lambda n, k: (k, n))],
        out_specs=out_spec, out_shape=out_shape, compiler_params=_cp(2))(A, G)


def _logsig(z):
    return jnp.minimum(z, 0.0) - jnp.log(1.0 + jnp.exp(-jnp.abs(z)))


def _features(F, a2p, biasp, lbp, B):
    nt = F.shape[0] // (B * TM)
    nj = nt + 1
    R2 = nj * TM

    def body(f_ref, a2_ref, bias_ref, lb_ref, q_ref, k_ref, v_ref, g_ref):
        lr = f_ref[:, LR:LR + HD].astype(BF16)
        lane = lax.broadcasted_iota(jnp.int32, (1, 4 * HD), 1)
        keep = (lane & (HD - 1)) < GLA_DK
        for d in range(2):
            z = _dot(lr, a2_ref[d]) + bias_ref[d:d + 1, :]
            gl = jnp.where(keep, _logsig(z) * (1.0 / GATE_NORM), 0.0)
            for h in range(4):
                g_ref[d, 0, h] = gl[:, h * HD:(h + 1) * HD]
        for h in range(4):
            q_ref[0, h] = f_ref[:, GQ + h * HD:GQ + (h + 1) * HD] * (GLA_DK ** -0.5)
            kk = f_ref[:, GK + h * HD:GK + (h + 1) * HD]
            k_ref[0, 0, h] = kk
            k_ref[1, 0, h] = kk
            v_ref[0, h] = f_ref[:, GV + h * HD:GV + (h + 1) * HD].astype(BF16)
        for h in range(4):
            sl = slice(h * HD, (h + 1) * HD)
            rq = f_ref[:, RQ + h * HD:RQ + (h + 1) * HD]
            q_ref[0, 4 + h] = rq * _sigmoid(rq) * (HD ** -0.5)
            v_ref[0, 4 + h] = f_ref[:, RI + h * HD:RI + (h + 1) * HD].astype(BF16)
            for d, off in ((0, RFF), (1, RFB)):
                lb = lb_ref[d:d + 1, sl]
                f = lb + (1.0 - lb) * _sigmoid(f_ref[:, off + h * HD:off + (h + 1) * HD])
                g_ref[d, 0, 4 + h] = jnp.log(f)
                k_ref[d, 0, 4 + h] = 1.0 - f

    fmap = lambda b, j: (b * nt + jnp.where(j == nt, 0, j), 0)
    one = pl.BlockSpec((1, NH, TM, HD), lambda b, j: (b, 0, j, 0))
    two = pl.BlockSpec((2, 1, NH, TM, HD), lambda b, j: (0, b, 0, j, 0))
    s1 = jax.ShapeDtypeStruct((B, NH, R2, HD), F32)
    s2 = jax.ShapeDtypeStruct((2, B, NH, R2, HD), F32)
    return pl.pallas_call(
        body, name="mix_features", grid=(B, nj),
        in_specs=[pl.BlockSpec((TM, MIXP), fmap), pl.BlockSpec((2, HD, 4 * HD), lambda b, j: (0, 0, 0)),
                  pl.BlockSpec((2, 4 * HD), lambda b, j: (0, 0)), pl.BlockSpec((2, 4 * HD), lambda b, j: (0, 0))],
        out_specs=[one, two, one, two], out_shape=[s1, s2, jax.ShapeDtypeStruct(s1.shape, BF16), s2],
        compiler_params=_cp(2))(F, a2p, biasp, lbp)


def _features_bwd(F, a2p, biasp, lbp, dQ0, dQ1, dK0, dK1, dV0, dV1, dG0, dG1, dgates, B):
    nt = F.shape[0] // (B * TM)

    def body(f_ref, a2_ref, bias_ref, lb_ref, dq0, dq1, dk0, dk1, dv0, dv1, dg0, dg1, dgt_ref,
             df_ref, da2_ref, dbias_ref, dlb_ref):
        b_, i = pl.program_id(0), pl.program_id(1)

        @pl.when((b_ == 0) & (i == 0))
        def _():
            da2_ref[...] = jnp.zeros_like(da2_ref)
            dbias_ref[...] = jnp.zeros_like(dbias_ref)
            dlb_ref[...] = jnp.zeros_like(dlb_ref)

        df_ref[:, 0:2 * 4 * HD] = jnp.where(i > 0, dgt_ref[...], 0.0).astype(BF16)
        df_ref[:, LR + HD:] = jnp.zeros((TM, MIXP - LR - HD), BF16)
        lr = f_ref[:, LR:LR + HD].astype(BF16)
        lane = lax.broadcasted_iota(jnp.int32, (1, 4 * HD), 1)
        keep = (lane & (HD - 1)) < GLA_DK
        dlr = jnp.zeros((TM, HD), F32)
        dgs = (dg0, dg1)
        dks = (dk0, dk1)
        for d in range(2):
            z = _dot(lr, a2_ref[d]) + bias_ref[d:d + 1, :]
            dgl = jnp.concatenate([dgs[d][0, h] for h in range(4)], axis=1)
            dz = jnp.where(keep, dgl * (1.0 / GATE_NORM) * (1.0 - _sigmoid(z)), 0.0)
            dzb = dz.astype(BF16)
            dlr = dlr + _dot_nt(dzb, a2_ref[d])
            da2_ref[d] += _dot_tn(lr, dzb)
            dbias_ref[d:d + 1, :] += _rsum(dz)
        df_ref[:, LR:LR + HD] = dlr.astype(BF16)
        for h in range(4):
            df_ref[:, GQ + h * HD:GQ + (h + 1) * HD] = ((dq0[0, h] + dq1[0, h]) * (GLA_DK ** -0.5)).astype(BF16)
            df_ref[:, GK + h * HD:GK + (h + 1) * HD] = (dk0[0, h] + dk1[0, h]).astype(BF16)
            df_ref[:, GV + h * HD:GV + (h + 1) * HD] = (dv0[0, h] + dv1[0, h]).astype(BF16)
        for h in range(4):
            sl = slice(h * HD, (h + 1) * HD)
            rq = f_ref[:, RQ + h * HD:RQ + (h + 1) * HD]
            s = _sigmoid(rq)
            dqh = dq0[0, 4 + h] + dq1[0, 4 + h]
            df_ref[:, RQ + h * HD:RQ + (h + 1) * HD] = (dqh * (HD ** -0.5) * (s * (1.0 + rq * (1.0 - s)))).astype(BF16)
            df_ref[:, RI + h * HD:RI + (h + 1) * HD] = (dv0[0, 4 + h] + dv1[0, 4 + h]).astype(BF16)
            for d, off in ((0, RFF), (1, RFB)):
                lb = lb_ref[d:d + 1, sl]
                sg = _sigmoid(f_ref[:, off + h * HD:off + (h + 1) * HD])
                f = lb + (1.0 - lb) * sg
                dff = dgs[d][0, 4 + h] / f - dks[d][0, 4 + h]
                df_ref[:, off + h * HD:off + (h + 1) * HD] = (dff * (1.0 - lb) * sg * (1.0 - sg)).astype(BF16)
                dlb_ref[d:d + 1, sl] += _rsum(dff * (1.0 - sg))

    m0 = lambda b, i: (b, 0, i, 0)
    m1 = lambda b, i: (b, 0, jnp.where(i == 0, nt, i), 0)
    one = lambda m: pl.BlockSpec((1, NH, TM, HD), m)
    return pl.pallas_call(
        body, name="mix_features_bwd", grid=(B, nt),
        in_specs=[pl.BlockSpec((TM, MIXP), _rowmap(nt, 0)), pl.BlockSpec((2, HD, 4 * HD), lambda b, i: (0, 0, 0)),
                  pl.BlockSpec((2, 4 * HD), lambda b, i: (0, 0)), pl.BlockSpec((2, 4 * HD), lambda b, i: (0, 0)),
                  one(m0), one(m1), one(m0), one(m1), one(m0), one(m1), one(m0), one(m1),
                  pl.BlockSpec((TM, D), lambda b, i: (b * (nt - 1) + jnp.maximum(i - 1, 0), 0))],
        out_specs=[pl.BlockSpec((TM, MIXP), _rowmap(nt, 0)), pl.BlockSpec((2, HD, 4 * HD), lambda b, i: (0, 0, 0)),
                   pl.BlockSpec((2, 4 * HD), lambda b, i: (0, 0)), pl.BlockSpec((2, 4 * HD), lambda b, i: (0, 0))],
        out_shape=[jax.ShapeDtypeStruct((B * nt * TM, MIXP), BF16), jax.ShapeDtypeStruct((2, HD, 4 * HD), F32),
                   jax.ShapeDtypeStruct((2, 4 * HD), F32), jax.ShapeDtypeStruct((2, 4 * HD), F32)],
        compiler_params=_cp(2))(F, a2p, biasp, lbp, dQ0, dQ1, dK0, dK1, dV0, dV1, dG0, dG1, dgates)


def _chunk_scan(x, rin, fwd, inclusive=True):
    acc = x
    sft = 1
    while sft < CH:
        if fwd:
            acc = acc + jnp.where(rin >= sft, pltpu.roll(acc, sft, 0), 0.0)
        else:
            acc = acc + jnp.where(rin < CH - sft, pltpu.roll(acc, TM - sft, 0), 0.0)
        sft *= 2
    return acc if inclusive else acc - x


def _scan_common(q, k, v, g, rev):
    rin = lax.broadcasted_iota(jnp.int32, (TM, HD), 0) & (CH - 1)
    b = _chunk_scan(g, rin, not rev)
    xx = _chunk_scan(g, rin, rev, inclusive=False)
    eb = jnp.exp(b)
    qd = q * eb
    ki = k * jnp.exp(-b)
    kt = k * jnp.exp(xx)
    ri = lax.broadcasted_iota(jnp.int32, (SB, SB), 0)
    ci = lax.broadcasted_iota(jnp.int32, (SB, SB), 1)
    same = (ri >> 5) == (ci >> 5)
    lo = same & (ri >= ci)
    up = same & (ri <= ci)
    mask, maskT = (up, lo) if rev else (lo, up)
    re = lax.broadcasted_iota(jnp.int32, (SB, CSB * HD), 0) >> 5
    ce = lax.broadcasted_iota(jnp.int32, (SB, CSB * HD), 1) >> 7
    mexp = re == ce
    return rin, b, xx, eb, qd, ki, kt, mask, maskT, mexp


def _sub(x, s):
    return x[s * SB:(s + 1) * SB]


def _expand(xb, mexp):
    return jnp.where(mexp, jnp.concatenate([xb] * CSB, axis=1), jnp.zeros((), xb.dtype))


def _own(x, mexp):
    xm = jnp.where(mexp, x, 0.0)
    acc = xm[:, 0:HD]
    for n in range(1, CSB):
        acc = acc + xm[:, n * HD:(n + 1) * HD]
    return acc


def _stack(per_chunk, s):
    return jnp.concatenate(per_chunk[s * CSB:(s + 1) * CSB], axis=1)


def _state_pass(s0, eb, uts, rev):
    order = range(NCB - 1, -1, -1) if rev else range(NCB)
    states = [None] * NCB
    s = s0
    for n in order:
        row = n * CH if rev else n * CH + CH - 1
        states[n] = s
        s = eb[row:row + 1, :] * s + uts[n // CSB][:, (n % CSB) * HD:(n % CSB + 1) * HD]
    return states, s


def _scan_fwd(Q, K, V, G, rev, B):
    nb = Q.shape[2] // TM - 1
    d = 1 if rev else 0
    rmap = (lambda s: nb - s) if rev else (lambda s: s)

    def body(q_ref, k_ref, v_ref, g_ref, o_ref, st_ref, s_scr):
        @pl.when(pl.program_id(2) == 0)
        def _():
            s_scr[...] = jnp.zeros_like(s_scr)

        for p in range(HP):
            s0 = s_scr[p]
            st_ref[0, p, 0] = s0
            v = v_ref[0, p]
            _, _, _, eb, qd, ki, kt, mask, _, mexp = _scan_common(q_ref[0, p], k_ref[0, 0, p], v, g_ref[0, 0, p], rev)
            qb, kib, ktb, vb = qd.astype(BF16), ki.astype(BF16), kt.astype(BF16), v.astype(BF16)
            uts = [_dot_tn(_sub(vb, s), _expand(_sub(ktb, s), mexp)) for s in range(NSB)]
            states, s_new = _state_pass(s0, eb, uts, rev)
            s_scr[p] = s_new
            for s in range(NSB):
                a = jnp.where(mask, _dot_nt(_sub(qb, s), _sub(kib, s)), 0.0)
                o_ref[0, p, s * SB:(s + 1) * SB, :] = (
                    _dot(a.astype(BF16), _sub(vb, s))
                    + _dot_nt(_expand(_sub(qb, s), mexp), _stack(states, s).astype(BF16)))

    one = pl.BlockSpec((1, HP, TM, HD), lambda b, h, s: (b, h, rmap(s), 0))
    two = pl.BlockSpec((1, 1, HP, TM, HD), lambda b, h, s: (d, b, h, rmap(s), 0))
    return pl.pallas_call(
        body, name="scan_fwd_rev" if rev else "scan_fwd", grid=(B, NH // HP, nb),
        in_specs=[one, two, one, two],
        out_specs=[one, pl.BlockSpec((1, HP, 1, HD, HD), lambda b, h, s: (b, h, s, 0, 0))],
        out_shape=[jax.ShapeDtypeStruct(Q.shape, F32), jax.ShapeDtypeStruct((B, NH, nb, HD, HD), F32)],
        scratch_shapes=[pltpu.VMEM((HP, HD, HD), F32)],
        compiler_params=_cp(3))(Q, K, V, G)


def _scan_bwd(Q, K, V, G, St, dO, rev, B):
    nb = Q.shape[2] // TM - 1
    d = 1 if rev else 0
    smap = lambda t: nb - 1 - t
    rmap = (lambda t: nb - smap(t)) if rev else smap

    def body(q_ref, k_ref, v_ref, g_ref, st_ref, do_ref, dq_ref, dk_ref, dv_ref, dg_ref, ds_scr):
        t = pl.program_id(2)

        @pl.when(t == 0)
        def _():
            ds_scr[...] = jnp.zeros_like(ds_scr)

        is_lat = smap(t) >= 1
        for p in range(HP):
            v = v_ref[0, p]
            rin, b, xx, eb, qd, ki, kt, mask, maskT, mexp = _scan_common(
                q_ref[0, p], k_ref[0, 0, p], v, g_ref[0, 0, p], rev)
            qb, kib, ktb, vb = qd.astype(BF16), ki.astype(BF16), kt.astype(BF16), v.astype(BF16)
            dob = jnp.where(is_lat, do_ref[0, p], 0.0).astype(BF16)
            kt_exps = [_expand(_sub(ktb, s), mexp) for s in range(NSB)]
            uts = [_dot_tn(_sub(vb, s), kt_exps[s]) for s in range(NSB)]
            states, _ = _state_pass(st_ref[0, p, 0], eb, uts, rev)
            gts = [_dot_tn(_sub(dob, s), _expand(_sub(qb, s), mexp)) for s in range(NSB)]
            order = range(NCB) if rev else range(NCB - 1, -1, -1)
            dsp = [None] * NCB
            t2 = [None] * NCB
            dsc = ds_scr[p]
            for n in order:
                row = n * CH if rev else n * CH + CH - 1
                ebl = eb[row:row + 1, :]
                dsp[n] = dsc
                t2[n] = jnp.broadcast_to(ebl * _rsum(states[n] * dsc), (CH, HD))
                dsc = gts[n // CSB][:, (n % CSB) * HD:(n % CSB + 1) * HD] + ebl * dsc
            ds_scr[p] = dsc
            dqds, dkis, dkts = [], [], []
            for s in range(NSB):
                q_s, ki_s, v_s, do_s = _sub(qb, s), _sub(kib, s), _sub(vb, s), _sub(dob, s)
                dspb = _stack(dsp, s).astype(BF16)
                da = jnp.where(mask, _dot_nt(do_s, v_s), 0.0).astype(BF16)
                dat = jnp.where(maskT, _dot_nt(v_s, do_s), 0.0).astype(BF16)
                at = jnp.where(maskT, _dot_nt(ki_s, q_s), 0.0).astype(BF16)
                dqds.append(_dot(da, ki_s) + _own(_dot(do_s, _stack(states, s).astype(BF16)), mexp))
                dkis.append(_dot(dat, q_s))
                dv_ref[0, p, s * SB:(s + 1) * SB, :] = _dot(at, do_s) + _dot_nt(kt_exps[s], dspb)
                dkts.append(_own(_dot(v_s, dspb), mexp))
            dqd, dki, dkt = (jnp.concatenate(parts, axis=0) for parts in (dqds, dkis, dkts))
            z = dkt * kt
            db = dqd * qd - dki * ki
            dq_ref[0, p] = dqd * eb
            dk_ref[0, p] = dki * jnp.exp(-b) + dkt * jnp.exp(xx)
            dg_ref[0, p] = (_chunk_scan(db, rin, rev) + _chunk_scan(z, rin, not rev, inclusive=False)
                            + jnp.concatenate(t2, axis=0))

    one = pl.BlockSpec((1, HP, TM, HD), lambda b, h, t: (b, h, rmap(t), 0))
    two = pl.BlockSpec((1, 1, HP, TM, HD), lambda b, h, t: (d, b, h, rmap(t), 0))
    lat = pl.BlockSpec((1, HP, TM, HD), lambda b, h, t: (b, h, jnp.clip(rmap(t) - 1, 0, nb - 2), 0))
    shp = jax.ShapeDtypeStruct(Q.shape, F32)
    return pl.pallas_call(
        body, name="scan_bwd_rev" if rev else "scan_bwd", grid=(B, NH // HP, nb),
        in_specs=[one, two, one, two, pl.BlockSpec((1, HP, 1, HD, HD), lambda b, h, t: (b, h, smap(t), 0, 0)), lat],
        out_specs=[one, one, one, one], out_shape=[shp, shp, shp, shp],
        scratch_shapes=[pltpu.VMEM((HP, HD, HD), F32)],
        compiler_params=_cp(3))(Q, K, V, G, St, dO)


def _gnorm(O0, O1, F, gains, B, ntl):
    nt = ntl + 1

    def body(o0_ref, o1_ref, f_ref, gn_ref, m_ref):
        for h in range(NH):
            o = o0_ref[0, h] + o1_ref[0, h]
            r = lax.rsqrt(jnp.mean(o * o, axis=-1, keepdims=True) + NORM_EPS)
            gn = gn_ref[0:1, :] if h < 4 else gn_ref[1:2, :]
            gt = f_ref[:, h * HD:(h + 1) * HD]
            m_ref[:, h * HD:(h + 1) * HD] = (o * r * gn * (gt * _sigmoid(gt))).astype(BF16)

    ospec = pl.BlockSpec((1, NH, TM, HD), lambda b, i: (b, 0, i + 1, 0))
    return pl.pallas_call(
        body, name="gated_norm", grid=(B, ntl),
        in_specs=[ospec, ospec, pl.BlockSpec((TM, D), lambda b, i: (b * nt + 1 + i, 0)),
                  pl.BlockSpec((2, HD), lambda b, i: (0, 0))],
        out_specs=pl.BlockSpec((TM, D), _rowmap(ntl, 0)),
        out_shape=jax.ShapeDtypeStruct((B * ntl * TM, D), BF16), compiler_params=_cp(2))(O0, O1, F, gains)


def _gnorm_bwd(dM, O0, O1, F, gains, B, ntl):
    nt = ntl + 1

    def body(dm_ref, o0_ref, o1_ref, f_ref, gn_ref, do_ref, dgt_ref, dgn_ref):
        b_, i = pl.program_id(0), pl.program_id(1)

        @pl.when((b_ == 0) & (i == 0))
        def _():
            dgn_ref[...] = jnp.zeros_like(dgn_ref)

        for h in range(NH):
            o = o0_ref[0, h] + o1_ref[0, h]
            r = lax.rsqrt(jnp.mean(o * o, axis=-1, keepdims=True) + NORM_EPS)
            y = o * r
            gn = gn_ref[0:1, :] if h < 4 else gn_ref[1:2, :]
            gt = f_ref[:, h * HD:(h + 1) * HD]
            s = _sigmoid(gt)
            dm = dm_ref[:, h * HD:(h + 1) * HD]
            don = dm * (gt * s)
            dgt_ref[:, h * HD:(h + 1) * HD] = dm * (y * gn) * (s * (1.0 + gt * (1.0 - s)))
            row = 0 if h < 4 else 1
            dgn_ref[row:row + 1, :] += _rsum(don * y)
            dy = don * gn
            do_ref[0, h] = r * (dy - y * jnp.mean(dy * y, axis=-1, keepdims=True))

    ospec = pl.BlockSpec((1, NH, TM, HD), lambda b, i: (b, 0, i + 1, 0))
    return pl.pallas_call(
        body, name="gated_norm_bwd", grid=(B, ntl),
        in_specs=[pl.BlockSpec((TM, D), _rowmap(ntl, 0)), ospec, ospec,
                  pl.BlockSpec((TM, D), lambda b, i: (b * nt + 1 + i, 0)), pl.BlockSpec((2, HD), lambda b, i: (0, 0))],
        out_specs=[pl.BlockSpec((1, NH, TM, HD), lambda b, i: (b, 0, i, 0)), pl.BlockSpec((TM, D), _rowmap(ntl, 0)),
                   pl.BlockSpec((2, HD), lambda b, i: (0, 0))],
        out_shape=[jax.ShapeDtypeStruct((B, NH, ntl * TM, HD), F32), jax.ShapeDtypeStruct((B * ntl * TM, D), F32),
                   jax.ShapeDtypeStruct((2, HD), F32)],
        compiler_params=_cp(2))(dM, O0, O1, F, gains)


def _sincos_2d(rows, width, dim):
    r = jnp.repeat(jnp.arange(rows), width)
    col = jnp.tile(jnp.arange(width), rows)
    quarter = dim // 4
    omega = 1.0 / 10000.0 ** (jnp.arange(quarter, dtype=F32) / quarter)

    def emb(p):
        a = p.astype(F32)[:, None] * omega[None, :]
        return jnp.concatenate([jnp.sin(a), jnp.cos(a)], axis=-1)

    return jnp.concatenate([emb(r), emb(col)], axis=-1)


def _pad_heads(w):
    k = w.shape[0]
    return jnp.pad(w.reshape(k, 4, GLA_DK), ((0, 0), (0, 0), (0, HD - GLA_DK))).reshape(k, 4 * HD)


def _unpad_heads(w):
    k = w.shape[0]
    return w.reshape(k, 4, HD)[:, :, :GLA_DK].reshape(k, 4 * GLA_DK)


MIX_N = 1032
MIX_NP = 1152
_SEGS = ([(64 * h, 64, GQ + HD * h) for h in range(4)] + [(256 + 64 * h, 64, GK + HD * h) for h in range(4)]
         + [(512, 512, GV), (1024, 512, GG), (1536, 32, LR), (1568, 512, RQ), (2080, 512, RFF), (2592, 512, RFB),
            (3104, 512, RI), (3616, 512, RG)])


def _mix_in_to_padded(ps):
    k = ps.shape[1]
    parts, pos = [], 0
    for g0, ln, s0 in sorted(_SEGS, key=lambda s: s[2]):
        if s0 > pos:
            parts.append(jnp.zeros((k, s0 - pos), ps.dtype))
        for j in range(4):
            lo, hi = max(g0, j * MIX_N), min(g0 + ln, (j + 1) * MIX_N)
            if lo < hi:
                parts.append(ps[j][:, lo - j * MIX_N:hi - j * MIX_N])
        pos = s0 + ln
    parts.append(jnp.zeros((k, MIXP - pos), ps.dtype))
    return jnp.concatenate(parts, axis=1)


def _mix_in_from_padded(g):
    k = g.shape[0]
    shards = []
    for j in range(4):
        parts = []
        for g0, ln, s0 in sorted(_SEGS):
            lo, hi = max(g0, j * MIX_N), min(g0 + ln, (j + 1) * MIX_N)
            if lo < hi:
                parts.append(g[:, s0 + lo - g0:s0 + hi - g0])
        parts.append(jnp.zeros((k, MIX_NP - MIX_N), g.dtype))
        shards.append(jnp.concatenate(parts, axis=1))
    return jnp.stack(shards)


def _local_step(x, ctx, tgt, mvec, weights_for, ln_gain, ln_bias, a2f, a2b, abf, abb, lb, gng, gnh, on_grads):
    B, T, _ = x.shape
    assert ctx.shape[1] == TM and T % TM == 0
    ntl = T // TM
    nt = ntl + 1
    C, L, CL = (nt, 0), (ntl, 0), (nt, 1)
    pos = _sincos_2d(T // 64, 64, D)
    gbs = [jnp.stack([ln_gain[i], ln_bias[i]]) for i in range(3)]
    a2p = jnp.zeros((2, HD, 4 * HD), F32)
    a2p = a2p.at[0, 0:16].set(_pad_heads(a2f)).at[1, 16:32].set(_pad_heads(a2b)).astype(BF16)
    biasp = jnp.concatenate([_pad_heads(abf.reshape(1, -1)), _pad_heads(abb.reshape(1, -1))], axis=0)
    gains = jnp.concatenate([gng.reshape(1, HD), gnh.reshape(1, HD)], axis=0)

    X0, h0 = _embed_lnmod(x, ctx, pos, mvec)
    w1i, w1o = weights_for("ffn1", h0)
    u0, a0 = _ffn_in(h0, C, w1i, B, nt, "ffn1_in")
    wmp, wmo = weights_for("mix", a0)
    y0 = _mm_nn(a0, C, w1o, B, nt, F32, "ffn1_out")
    X1, h1 = _resid_lnmod(X0, C, y0, mvec, True, 0, 0.5, gbs[0], B, nt, "resid0_lnmod1")
    Fm = _mm_nn(h1, C, wmp, B, nt, F32, "mix_in")
    w2i, w2o = weights_for("ffn2", Fm)
    Q, K, V, G = _features(Fm, a2p, biasp, lb, B)
    O0, S0 = _scan_fwd(Q, K, V, G, False, B)
    O1, S1 = _scan_fwd(Q, K, V, G, True, B)
    merged = _gnorm(O0, O1, Fm, gains, B, ntl)
    y1 = _mm_nn(merged, L, wmo, B, ntl, F32, "mix_out")
    X2, h2 = _resid_lnmod(X1, CL, y1, mvec, False, 1, 1.0, gbs[1], B, ntl, "resid1_lnmod2")
    u2, a2 = _ffn_in(h2, L, w2i, B, ntl, "ffn2_in")
    y2 = _mm_nn(a2, L, w2o, B, ntl, F32, "ffn2_out")

    dx2r, dy2, dgate2, dgb2, lsum = _tail(X2, y2, mvec, gbs[2], tgt, 2, 0.5, B, ntl)
    loss = (0.5 / D) * jnp.sum(lsum)
    du2 = _ffn_out_dx(dy2, w2o, u2, "ffn2_out_dx")
    g_w2o = _mm_tn(a2, dy2, "ffn2_out_dw")
    dh2 = _mm_nt(du2, w2i, "ffn2_in_dx")
    g_w2i = _mm_tn(h2, du2, "ffn2_in_dw", shards=4)
    tok = on_grads("ffn2", (g_w2i, g_w2o))
    dx1r, dy1, dgate1, dgb1, dss2 = _lnmod_resid_bwd(dh2, X2, L, dx2r, X1, CL, y1, mvec, False, 2, 1.0,
                                                     gbs[1] + tok[0, 0], B, ntl, "lnmod2_resid1_bwd")
    dmerged = _mm_nt(dy1, wmo, "mix_out_dx")
    g_wmo = _mm_tn(merged, dy1, "mix_out_dw")
    dO, dgates, dgains = _gnorm_bwd(dmerged, O0, O1, Fm, gains, B, ntl)
    dQ0, dK0, dV0, dG0 = _scan_bwd(Q, K, V, G, S0, dO, False, B)
    dQ1, dK1, dV1, dG1 = _scan_bwd(Q, K, V, G, S1, dO, True, B)
    dF, da2p, dbiasp, dlb = _features_bwd(Fm, a2p, biasp, lb, dQ0, dQ1, dK0, dK1, dV0, dV1, dG0, dG1, dgates, B)
    dh1 = _mm_nt(dF, wmp, "mix_in_dx")
    g_wmp = _mm_tn(h1, dF, "mix_in_dw", tn=MIXP // 4)
    tok = on_grads("mix", (g_wmp, g_wmo))
    dx0r, dy0, dgate0, dgb0, dss1 = _lnmod_resid_bwd(dh1, X1, C, dx1r, X0, C, y0, mvec, True, 1, 0.5,
                                                     gbs[0] + tok[0, 0], B, nt, "lnmod1_resid0_bwd")
    du0 = _ffn_out_dx(dy0, w1o, u0, "ffn1_out_dx")
    g_w1o = _mm_tn(a0, dy0, "ffn1_out_dw")
    g_w1i = _mm_tn(h0, du0, "ffn1_in_dw", shards=4)
    tok = on_grads("ffn1", (g_w1i, g_w1o))
    dh0 = _mm_nt(du0, w1i, "ffn1_in_dx", dep=tok)
    grad_x, dss0 = _lnmod0_bwd(dh0, X0, mvec, dx0r, x.shape, B, nt)

    zero_ctx = lambda a: a.at[:, 0].set(0.0)
    dm = jnp.concatenate([dss0, dgate0, dss1, zero_ctx(dgate1), zero_ctx(dss2), zero_ctx(dgate2)], axis=2)
    small = dict(
        ln_gain=jnp.stack([dgb0[0], dgb1[0], dgb2[0]]), ln_bias=jnp.stack([dgb0[1], dgb1[1], dgb2[1]]),
        a2f=_unpad_heads(da2p[0, 0:16]), a2b=_unpad_heads(da2p[1, 16:32]),
        abf=_unpad_heads(dbiasp[0:1]), abb=_unpad_heads(dbiasp[1:2]), lb=dlb, gng=dgains[0], gnh=dgains[1])
    return loss, grad_x, dm, small


def _small_allgather(xs, name):
    r, n = xs.shape

    def body(x_ref, out_ref, send_sems, recv_sems, local_sem):
        x, y, c = lax.axis_index("x"), lax.axis_index("y"), lax.axis_index("c")
        me, sibling = (x, y, c), (x, y, 1 - c)
        chips = [(1 - x, y), (x, 1 - y), (1 - x, 1 - y)]

        def rows(px, py, pc):
            return out_ref.at[pl.ds((4 * px + 2 * py + pc) * r, r), :]

        def copy(k, block, to, src=None):
            return pltpu.make_async_remote_copy(
                src_ref=rows(*block) if src is None else src, dst_ref=rows(*block),
                send_sem=send_sems.at[k], recv_sem=recv_sems.at[k], device_id=to, device_id_type=MESH)

        mine = pltpu.make_async_copy(x_ref, rows(*me), local_sem)
        mine.start()
        first = [copy(0, me, sibling, src=x_ref)]
        first += [copy(1 + j, me, (*chip, c), src=x_ref) for j, chip in enumerate(chips)]
        for cp in first:
            cp.start()
        passed = [copy(4 + j, (*chip, c), sibling) for j, chip in enumerate(chips)]
        for j, chip in enumerate(chips):
            copy(1 + j, (*chip, c), me).wait_recv()
            passed[j].start()
        copy(0, sibling, me).wait_recv()
        for j, chip in enumerate(chips):
            copy(4 + j, (*chip, 1 - c), me).wait_recv()
        for cp in first + passed:
            cp.wait_send()
        mine.wait()

    out = pl.pallas_call(
        body, name=name,
        out_shape=jax.ShapeDtypeStruct((8 * r, n), xs.dtype),
        in_specs=[pl.BlockSpec(memory_space=pltpu.VMEM)],
        out_specs=pl.BlockSpec(memory_space=pltpu.VMEM),
        scratch_shapes=[pltpu.SemaphoreType.DMA((7,)), pltpu.SemaphoreType.DMA((7,)), pltpu.SemaphoreType.DMA],
        compiler_params=pltpu.CompilerParams(vmem_limit_bytes=VMEM_LIMIT))(xs)
    return out.reshape(8, r, n)


def _gather_flat(v, name):
    n = v.shape[0]
    npad = -(-n // 1024) * 1024
    g = _small_allgather(jnp.pad(v, (0, npad - n)).reshape(8, npad // 8), name)
    return g.reshape(8, npad)[:, :n]


HBM_SPEC = pl.BlockSpec(memory_space=pltpu.HBM)
SEM_SPEC = pl.BlockSpec(memory_space=pltpu.SEMAPHORE)
DATAFLOW = pltpu.SideEffectType.DATAFLOW_SIDE_EFFECTING


def _gather_copies(xs, outs, send_sems, recv_sems):
    x, y, c = lax.axis_index("x"), lax.axis_index("y"), lax.axis_index("c")
    dests = [(x, y, 1 - c), (1 - x, y, c), (x, 1 - y, c), (1 - x, 1 - y, c)]
    return [pltpu.make_async_remote_copy(
        src_ref=xs[w], dst_ref=outs[w].at[4 * x + 2 * y + c], send_sem=send_sems[4 * w + k],
        recv_sem=recv_sems[4 * w + k], device_id=dests[k], device_id_type=MESH)
        for w in range(len(xs)) for k in range(4)]


def _gather_start(blks, name):
    n = len(blks)
    lands = [lax.empty((8,) + b.shape, b.dtype) for b in blks]

    def body(*refs):
        xs, ls, outs = refs[:n], refs[n:2 * n], refs[2 * n:]
        for cp in _gather_copies(xs, ls, outs[:4 * n], outs[4 * n:8 * n]):
            cp.start()
        token = outs[10 * n]
        token[...] = jnp.zeros_like(token)

    outs = pl.pallas_call(
        body, name=name,
        out_shape=([pltpu.SemaphoreType.DMA(())] * (8 * n) + [pltpu.HBM(a.shape, a.dtype) for a in blks + lands]
                   + [jax.ShapeDtypeStruct((8, 128), F32)]),
        in_specs=[HBM_SPEC] * (2 * n),
        out_specs=[SEM_SPEC] * (8 * n) + [HBM_SPEC] * (2 * n) + [pl.BlockSpec(memory_space=pltpu.VMEM)],
        input_output_aliases={w: 8 * n + w for w in range(2 * n)},
        compiler_params=pltpu.CompilerParams(has_side_effects=DATAFLOW),
    )(*[pltpu.with_memory_space_constraint(a, pltpu.HBM) for a in blks + lands])
    return outs[:8 * n], outs[8 * n:9 * n], outs[9 * n:10 * n], outs[10 * n]


def _gather_wait(sems, x_thru, l_thru, after, name):
    n = len(x_thru)

    def body(*refs):
        xs, ls, ss = refs[:n], refs[n:2 * n], refs[2 * n:10 * n]
        for cp in _gather_copies(xs, ls, ss[:4 * n], ss[4 * n:]):
            cp.wait_send()
            cp.wait_recv()

    outs = pl.pallas_call(
        body, name=name,
        out_shape=[pltpu.HBM(a.shape, a.dtype) for a in x_thru + l_thru],
        in_specs=[HBM_SPEC] * (2 * n) + [SEM_SPEC] * (8 * n) + [pl.BlockSpec(memory_space=pl.ANY)],
        out_specs=[HBM_SPEC] * (2 * n),
        input_output_aliases={w: w for w in range(2 * n)},
        compiler_params=pltpu.CompilerParams(has_side_effects=DATAFLOW),
    )(*x_thru, *l_thru, *sems, after)
    return outs[n:]


def _gather_forward(gathered, name):
    n = len(gathered)

    def body(*refs):
        outs = refs[n:2 * n]
        send_sems, recv_sems = refs[2 * n:]
        x, y, c = lax.axis_index("x"), lax.axis_index("y"), lax.axis_index("c")
        chips = [(1 - x, y), (x, 1 - y), (1 - x, 1 - y)]

        def copy(w, j, pc):
            px, py = chips[j]
            slot = outs[w].at[4 * px + 2 * py + pc]
            return pltpu.make_async_remote_copy(
                src_ref=slot, dst_ref=slot, send_sem=send_sems.at[3 * w + j], recv_sem=recv_sems.at[3 * w + j],
                device_id=(x, y, 1 - c), device_id_type=MESH)

        sends = [copy(w, j, c) for w in range(n) for j in range(3)]
        for cp in sends:
            cp.start()
        for w in range(n):
            for j in range(3):
                copy(w, j, 1 - c).wait_recv()
        for cp in sends:
            cp.wait_send()

    any_spec = pl.BlockSpec(memory_space=pl.ANY)
    return pl.pallas_call(
        body, name=name,
        out_shape=[jax.ShapeDtypeStruct(g.shape, g.dtype) for g in gathered],
        in_specs=[any_spec] * n, out_specs=[any_spec] * n,
        input_output_aliases={w: w for w in range(n)},
        scratch_shapes=[pltpu.SemaphoreType.DMA((3 * n,)), pltpu.SemaphoreType.DMA((3 * n,))],
    )(*gathered)


def _rs_pair_exchange(g8s, name):
    n = len(g8s)

    def body(*refs):
        gs, rs = refs[:n], refs[n:2 * n]
        send_sems, recv_sems = refs[2 * n:]
        x, y, c = lax.axis_index("x"), lax.axis_index("y"), lax.axis_index("c")
        cps = [pltpu.make_async_remote_copy(
            src_ref=gs[w].at[2 * j + 1 - c], dst_ref=rs[w].at[j], send_sem=send_sems.at[4 * w + j],
            recv_sem=recv_sems.at[4 * w + j], device_id=(x, y, 1 - c), device_id_type=MESH)
            for w in range(n) for j in range(4)]
        for cp in cps:
            cp.start()
        for cp in cps:
            cp.wait_recv()
        for cp in cps:
            cp.wait_send()

    any_spec = pl.BlockSpec(memory_space=pl.ANY)
    return pl.pallas_call(
        body, name=name,
        out_shape=[jax.ShapeDtypeStruct((4,) + g.shape[1:], g.dtype) for g in g8s],
        in_specs=[any_spec] * n, out_specs=[any_spec] * n,
        scratch_shapes=[pltpu.SemaphoreType.DMA((4 * n,)), pltpu.SemaphoreType.DMA((4 * n,))])(*g8s)


def _chip_copies(hs, ls, send_sems, recv_sems):
    x, y, c = lax.axis_index("x"), lax.axis_index("y"), lax.axis_index("c")
    chips = [(1 - x, y), (x, 1 - y), (1 - x, 1 - y)]
    return [pltpu.make_async_remote_copy(
        src_ref=hs[w].at[2 * px + py], dst_ref=ls[w].at[k], send_sem=send_sems[3 * w + k],
        recv_sem=recv_sems[3 * w + k], device_id=(px, py, c), device_id_type=MESH)
        for w in range(len(hs)) for k, (px, py) in enumerate(chips)]


def _rs_chip_exchange_start(h4s, name):
    n = len(h4s)
    lands = [lax.empty((3,) + h.shape[1:], h.dtype) for h in h4s]

    def body(*refs):
        hs, ls, outs = refs[:n], refs[n:2 * n], refs[2 * n:]
        for cp in _chip_copies(hs, ls, outs[:3 * n], outs[3 * n:6 * n]):
            cp.start()
        token = outs[8 * n]
        token[...] = jnp.zeros_like(token)

    outs = pl.pallas_call(
        body, name=name,
        out_shape=([pltpu.SemaphoreType.DMA(())] * (6 * n) + [pltpu.HBM(h.shape, h.dtype) for h in h4s]
                   + [pltpu.HBM(l.shape, l.dtype) for l in lands] + [jax.ShapeDtypeStruct((8, 128), F32)]),
        in_specs=[HBM_SPEC] * (2 * n),
        out_specs=[SEM_SPEC] * (6 * n) + [HBM_SPEC] * (2 * n) + [pl.BlockSpec(memory_space=pltpu.VMEM)],
        input_output_aliases={w: 6 * n + w for w in range(2 * n)},
        compiler_params=pltpu.CompilerParams(has_side_effects=DATAFLOW),
    )(*[pltpu.with_memory_space_constraint(a, pltpu.HBM) for a in h4s + lands])
    return outs[:6 * n], outs[6 * n:7 * n], outs[7 * n:8 * n], outs[8 * n]


def _rs_chip_exchange_wait(sems, h_thru, l_thru, after, name):
    n = len(h_thru)

    def body(*refs):
        hs, ls, ss = refs[:n], refs[n:2 * n], refs[2 * n:8 * n]
        for cp in _chip_copies(hs, ls, ss[:3 * n], ss[3 * n:]):
            cp.wait_send()
            cp.wait_recv()

    outs = pl.pallas_call(
        body, name=name,
        out_shape=[pltpu.HBM(a.shape, a.dtype) for a in h_thru + l_thru],
        in_specs=[HBM_SPEC] * (2 * n) + [SEM_SPEC] * (6 * n) + [pl.BlockSpec(memory_space=pl.ANY)],
        out_specs=[HBM_SPEC] * (2 * n),
        input_output_aliases={w: w for w in range(2 * n)},
        compiler_params=pltpu.CompilerParams(has_side_effects=DATAFLOW),
    )(*h_thru, *l_thru, *sems, after)
    return outs[n:]


def _rs_pair_share(fins, name):
    n = len(fins)

    def body(*refs):
        fs, outs = refs[:n], refs[n:2 * n]
        send_sems, recv_sems = refs[2 * n:]
        x, y, c = lax.axis_index("x"), lax.axis_index("y"), lax.axis_index("c")
        cps = [pltpu.make_async_remote_copy(
            src_ref=fs[w], dst_ref=outs[w], send_sem=send_sems.at[w], recv_sem=recv_sems.at[w],
            device_id=(x, y, 1 - c), device_id_type=MESH) for w in range(n)]
        for cp in cps:
            cp.start()
        for cp in cps:
            cp.wait_recv()
        for cp in cps:
            cp.wait_send()

    any_spec = pl.BlockSpec(memory_space=pl.ANY)
    return pl.pallas_call(
        body, name=name,
        out_shape=[jax.ShapeDtypeStruct(f.shape, f.dtype) for f in fins],
        in_specs=[any_spec] * n, out_specs=[any_spec] * n,
        scratch_shapes=[pltpu.SemaphoreType.DMA((n,)), pltpu.SemaphoreType.DMA((n,))])(*fins)


def _rs_add_pair(g8, r4, c, name):
    R, n = g8.shape[1:]
    rb = R // 2

    def body(c_ref, g_ref, r_ref, o_ref):
        o_ref[...] = (g_ref[...] + r_ref[...]).astype(BF16)

    spec = pl.BlockSpec((1, rb, n), lambda j, i, c_ref: (j, i, 0))
    return pl.pallas_call(
        body, name=name,
        grid_spec=pltpu.PrefetchScalarGridSpec(
            num_scalar_prefetch=1, grid=(4, R // rb),
            in_specs=[pl.BlockSpec((1, rb, n), lambda j, i, c_ref: (2 * j + c_ref[0], i, 0)), spec],
            out_specs=spec),
        out_shape=jax.ShapeDtypeStruct((4, R, n), BF16), compiler_params=_cp(2))(c, g8, r4)


def _rs_add_chips(g8, r4, r3, cj, name):
    R, n = g8.shape[1:]
    rb = R // 2

    def body(cj_ref, g_ref, p_ref, r_ref, o_ref):
        own = g_ref[0] + p_ref[0]
        o_ref[...] = ((own + r_ref[0].astype(F32)) + r_ref[1].astype(F32)) + r_ref[2].astype(F32)

    return pl.pallas_call(
        body, name=name,
        grid_spec=pltpu.PrefetchScalarGridSpec(
            num_scalar_prefetch=1, grid=(R // rb,),
            in_specs=[pl.BlockSpec((1, rb, n), lambda i, cj_ref: (2 * cj_ref[1] + cj_ref[0], i, 0)),
                      pl.BlockSpec((1, rb, n), lambda i, cj_ref: (cj_ref[1], i, 0)),
                      pl.BlockSpec((3, rb, n), lambda i, cj_ref: (0, i, 0))],
            out_specs=pl.BlockSpec((rb, n), lambda i, cj_ref: (i, 0))),
        out_shape=jax.ShapeDtypeStruct((R, n), F32), compiler_params=_cp(1))(cj, g8, r4, r3)


def _sum8(g):
    n = g.shape[1]

    def body(g_ref, o_ref):
        acc = g_ref[0:1, :]
        for k in range(1, 8):
            acc = acc + g_ref[k:k + 1, :]
        o_ref[...] = acc

    return pl.pallas_call(body, name="sum_devices", out_shape=jax.ShapeDtypeStruct((1, n), F32),
                          compiler_params=pltpu.CompilerParams(vmem_limit_bytes=VMEM_LIMIT))(g)


ADA_ROWS = 64


def _ada_fwd(cs, w, b):
    n = w.shape[1]

    def body(c_ref, w_ref, b_ref, o_ref):
        cv = c_ref[...]
        s = (cv * _sigmoid(cv)).astype(BF16)
        o_ref[...] = _dot(s, w_ref[...].astype(BF16)) + b_ref[...]

    return pl.pallas_call(body, name="ada_fwd", out_shape=jax.ShapeDtypeStruct((ADA_ROWS, n), F32),
                          compiler_params=pltpu.CompilerParams(vmem_limit_bytes=VMEM_LIMIT))(cs, w, b)


def _ada_bwd(cs, w, dm):
    n = w.shape[1]

    def body(c_ref, w_ref, dm_ref, gw_ref, dc_ref):
        cv = c_ref[...]
        s = (cv * _sigmoid(cv)).astype(BF16)
        gw_ref[...] = _dot_tn(s, dm_ref[...].astype(BF16))
        dc_ref[...] = _dot_nt(dm_ref[32:40, :].astype(BF16), w_ref[...].astype(BF16))

    return pl.pallas_call(
        body, name="ada_bwd",
        out_shape=[jax.ShapeDtypeStruct((D, n), F32), jax.ShapeDtypeStruct((8, D), F32)],
        compiler_params=pltpu.CompilerParams(vmem_limit_bytes=VMEM_LIMIT))(cs, w, dm)


def _adamw(w, g, m, v, name):
    r, c = w.shape
    rb = r
    if r % 8 == 0 and r * c * 4 > (1 << 20):
        rb = 8
        for cand in range(8, r, 8):
            if r % cand == 0 and cand * c * 4 <= (1 << 20):
                rb = cand

    def body(w_ref, g_ref, m_ref, v_ref, d_ref, nm_ref, nv_ref):
        gv = g_ref[...]
        mn = ADAM_B1 * m_ref[...] + (1.0 - ADAM_B1) * gv
        vn = ADAM_B2 * v_ref[...] + (1.0 - ADAM_B2) * (gv * gv)
        m_hat = mn / (1.0 - ADAM_B1 ** ADAM_STEP)
        v_hat = vn / (1.0 - ADAM_B2 ** ADAM_STEP)
        d_ref[...] = -ADAM_LR * (m_hat / (jnp.sqrt(v_hat) + ADAM_EPS) + ADAM_WD * w_ref[...])
        nm_ref[...] = mn
        nv_ref[...] = vn

    spec = pl.BlockSpec((rb, c), lambda i: (i, 0))
    shp = jax.ShapeDtypeStruct((r, c), F32)
    return pl.pallas_call(body, name=name, grid=(r // rb,), in_specs=[spec] * 4, out_specs=[spec] * 3,
                          out_shape=[shp] * 3, compiler_params=_cp(1))(w, g, m, v)


BIG = ("ffn1_w_in", "ffn1_w_out", "w_mix_in", "w_mix_out", "ffn2_w_in", "ffn2_w_out")


def _half_rows(w, c):
    half = w.shape[0] // 2
    return lax.dynamic_slice_in_dim(w, c * half, half, axis=0)


def _lower_bounds(logits):
    return jnp.cumsum(jax.nn.softmax(logits.astype(F32), axis=1), axis=1)[:, 0]


def kernel(x, c, ctx, c_ctx, w_ada, b_ada, ln_gain, ln_bias, ffn1_w_in, ffn1_w_out, w_mix_in, gla_a2_fwd, gla_a2_bwd, gla_a_bias_fwd, gla_a_bias_bwd, hgrn_lb_logits, gla_norm_gain, hgrn_norm_gain, w_mix_out, ffn2_w_in, ffn2_w_out, loss_target, m_c_ctx, m_w_ada, m_b_ada, m_ln_gain, m_ln_bias, m_ffn1_w_in, m_ffn1_w_out, m_w_mix_in, m_gla_a2_fwd, m_gla_a2_bwd, m_gla_a_bias_fwd, m_gla_a_bias_bwd, m_hgrn_lb_logits, m_gla_norm_gain, m_hgrn_norm_gain, m_w_mix_out, m_ffn2_w_in, m_ffn2_w_out, v_c_ctx, v_w_ada, v_b_ada, v_ln_gain, v_ln_bias, v_ffn1_w_in, v_ffn1_w_out, v_w_mix_in, v_gla_a2_fwd, v_gla_a2_bwd, v_gla_a_bias_fwd, v_gla_a_bias_bwd, v_hgrn_lb_logits, v_gla_norm_gain, v_hgrn_norm_gain, v_w_mix_out, v_ffn2_w_in, v_ffn2_w_out):
    xi, yi, ci = lax.axis_index("x"), lax.axis_index("y"), lax.axis_index("c")
    chip = 2 * xi + yi
    dev = 2 * chip + ci
    B = x.shape[0]
    weights = dict(ffn1_w_in=ffn1_w_in[0], ffn1_w_out=ffn1_w_out[0], w_mix_in=w_mix_in[0], w_mix_out=w_mix_out[0],
                   ffn2_w_in=ffn2_w_in[0], ffn2_w_out=ffn2_w_out[0])

    groups = dict(ffn2=("ffn2_w_in", "ffn2_w_out"), mix=("w_mix_in", "w_mix_out"), ffn1=("ffn1_w_in", "ffn1_w_out"))
    shards = dict(weights, w_mix_in=jnp.pad(weights["w_mix_in"], ((0, 0), (0, MIX_NP - MIX_N))))
    blks = {k: _half_rows(shards[k], ci).astype(BF16) for k in BIG}
    gathering = {}
    token = jnp.zeros((8, 128), F32)
    for group in ("ffn1", "mix", "ffn2"):
        after_prev = token[0, 0].astype(BF16)
        sems, x_thru, l_thru, token = _gather_start([blks[k] + after_prev for k in groups[group]],
                                                    "weight_gather_start_" + group)
        gathering[group] = (sems, x_thru, l_thru)

    def weights_for(group, after):
        names = groups[group]
        got = _gather_wait(*gathering[group], after, "weight_gather_wait_" + group)
        got = _gather_forward(got, "weight_gather_forward_" + group)
        w_in, w_out = (lax.dynamic_update_index_in_dim(g, blks[k], dev, 0) for k, g in zip(names, got))
        if group == "mix":
            return _mix_in_to_padded(w_in.reshape(4, D, MIX_NP)), w_out.reshape(-1, D)
        return w_in.reshape((4,) + shards[names[0]].shape), w_out.reshape(-1, D)

    mine = jnp.concatenate([c.reshape(-1), ln_gain.reshape(-1), ln_bias.reshape(-1), gla_a2_fwd.reshape(-1),
                            gla_a2_bwd.reshape(-1), hgrn_lb_logits.reshape(-1)]) + token[0, 0]
    g1 = _gather_flat(mine, "gather_cond")
    nc = B * D
    c_all = g1[:, :nc].reshape(8 * B, D)
    per_chip = g1[0::2, nc:]
    o = 0

    def take(shape, axis):
        nonlocal o
        n = int(np.prod(shape))
        parts = per_chip[:, o:o + n].reshape((4,) + shape)
        o += n
        return jnp.concatenate([parts[j] for j in range(4)], axis=axis)

    ln_gain_f = take((3, 256), 1)
    ln_bias_f = take((3, 256), 1)
    a2f_f = take((16, 64), 1)
    a2b_f = take((16, 64), 1)
    lbl_f = take((2, 2, 128), 2)
    lb, lb_vjp = jax.vjp(_lower_bounds, lbl_f)

    cs = jnp.concatenate([c_all, c_ctx.reshape(1, D), jnp.zeros((ADA_ROWS - 8 * B - 1, D), F32)], axis=0)
    ncol = w_ada.shape[2]
    b_cols = lax.dynamic_slice_in_dim(b_ada, chip * ncol, ncol, axis=1)
    m_cols = _ada_fwd(cs, w_ada[0], b_cols)
    g2 = _small_allgather(m_cols, "gather_mod")[0::2]
    m_all = jnp.concatenate([g2[j] for j in range(4)], axis=1)
    m_lat = lax.dynamic_slice_in_dim(m_all, dev * B, B, axis=0).reshape(B, 1, N_MOD, D)
    m_ctx = jnp.broadcast_to(m_all[8 * B].reshape(1, 1, N_MOD, D), (B, 1, N_MOD, D))
    mvec = jnp.concatenate([m_ctx, m_lat], axis=1)

    cvec = ci.reshape(1).astype(jnp.int32)
    cjvec = jnp.stack([ci, chip]).astype(jnp.int32)
    in_flight = {}

    def on_grads(group, gs):
        names = groups[group]
        if group == "mix":
            gs = (_mix_in_from_padded(gs[0]), gs[1])
        g8s = [g.reshape((8, shards[k].shape[0] // 2, shards[k].shape[1])) for k, g in zip(names, gs)]
        r4s = _rs_pair_exchange(g8s, "grad_pair_exchange_" + group)
        h4s = [_rs_add_pair(g, r, cvec, "grad_pair_add_" + k) for k, g, r in zip(names, g8s, r4s)]
        sems, h_thru, l_thru, token = _rs_chip_exchange_start(h4s, "grad_chip_start_" + group)
        in_flight[group] = (g8s, r4s, sems, h_thru, l_thru)
        return token

    loss_l, grad_x, dm, small = _local_step(
        x, ctx, loss_target, mvec, weights_for, ln_gain_f, ln_bias_f, a2f_f, a2b_f,
        gla_a_bias_fwd, gla_a_bias_bwd, lb, gla_norm_gain, hgrn_norm_gain, on_grads)
    loss = lax.psum(loss_l, ("x", "y", "c"))

    dm_lat = dm[:, 1].reshape(B, N_MOD * D)
    dm_ctx = jnp.sum(dm[:, 0], axis=0).reshape(N_MOD * D)
    keys = ("ln_gain", "ln_bias", "a2f", "a2b", "abf", "abb", "lb", "gng", "gnh")
    flat = jnp.concatenate([dm_lat.reshape(-1), dm_ctx] + [small[k].reshape(-1) for k in keys])
    g3 = _gather_flat(flat, "gather_small_grads")
    nlat = B * N_MOD * D
    dm_all = g3[:, :nlat].reshape(8 * B, N_MOD * D)
    tot = _sum8(g3[:, nlat:])[0]
    dmc_tot = tot[:N_MOD * D]
    o = N_MOD * D
    sg = {}
    for k in keys:
        n = int(np.prod(small[k].shape))
        sg[k] = tot[o:o + n].reshape(small[k].shape)
        o += n
    dm_rows = jnp.concatenate([dm_all, dmc_tot.reshape(1, -1), jnp.zeros((ADA_ROWS - 8 * B - 1, N_MOD * D), F32)], axis=0)
    g_b_ada = (jnp.sum(dm_all, axis=0) + dmc_tot).reshape(1, N_MOD * D)
    g_w_ada, dcc = _ada_bwd(cs, w_ada[0], lax.dynamic_slice_in_dim(dm_rows, chip * ncol, ncol, axis=1))
    g4 = _gather_flat(dcc[0], "gather_cctx")
    dsilu = ((g4[0] + g4[2]) + g4[4]) + g4[6]
    sc = _sigmoid(c_ctx)
    g_c_ctx = dsilu * (sc * (1.0 + c_ctx * (1.0 - sc)))
    (g_lbl,) = lb_vjp(sg["lb"])

    def cols(a, n, axis):
        return lax.dynamic_slice_in_dim(a, chip * n, n, axis=axis)

    fin = {}
    for group in ("ffn2", "mix", "ffn1"):
        g8s, r4s, sems, h_thru, l_thru = in_flight[group]
        r3s = _rs_chip_exchange_wait(sems, h_thru, l_thru, g_w_ada, "grad_chip_wait_" + group)
        for k, g, r4, r3 in zip(groups[group], g8s, r4s, r3s):
            fin[k] = _rs_add_chips(g, r4, r3, cjvec, "grad_chip_add_" + k)
    fins = [fin[k] for k in BIG]
    gsh = {}
    for k, own, sib in zip(BIG, fins, _rs_pair_share(fins, "grad_pair_share")):
        both = jnp.where(ci == 0, jnp.stack([own, sib]), jnp.stack([sib, own]))
        gsh[k] = both.reshape(shards[k].shape)[:, :weights[k].shape[1]]

    grads = dict(
        c_ctx=g_c_ctx, w_ada=g_w_ada[None], b_ada=g_b_ada, ln_gain=cols(sg["ln_gain"], 256, 1)[None],
        ln_bias=cols(sg["ln_bias"], 256, 1)[None], ffn1_w_in=gsh["ffn1_w_in"][None], ffn1_w_out=gsh["ffn1_w_out"][None],
        w_mix_in=gsh["w_mix_in"][None], gla_a2_fwd=cols(sg["a2f"], 64, 1)[None], gla_a2_bwd=cols(sg["a2b"], 64, 1)[None],
        gla_a_bias_fwd=sg["abf"], gla_a_bias_bwd=sg["abb"], hgrn_lb_logits=cols(g_lbl, 128, 2),
        gla_norm_gain=sg["gng"].reshape(1, HD), hgrn_norm_gain=sg["gnh"].reshape(1, HD),
        w_mix_out=gsh["w_mix_out"][None], ffn2_w_in=gsh["ffn2_w_in"][None], ffn2_w_out=gsh["ffn2_w_out"][None])
    params = dict(
        c_ctx=(c_ctx, m_c_ctx, v_c_ctx), w_ada=(w_ada, m_w_ada, v_w_ada), b_ada=(b_ada, m_b_ada, v_b_ada),
        ln_gain=(ln_gain, m_ln_gain, v_ln_gain), ln_bias=(ln_bias, m_ln_bias, v_ln_bias),
        ffn1_w_in=(ffn1_w_in, m_ffn1_w_in, v_ffn1_w_in), ffn1_w_out=(ffn1_w_out, m_ffn1_w_out, v_ffn1_w_out),
        w_mix_in=(w_mix_in, m_w_mix_in, v_w_mix_in), gla_a2_fwd=(gla_a2_fwd, m_gla_a2_fwd, v_gla_a2_fwd),
        gla_a2_bwd=(gla_a2_bwd, m_gla_a2_bwd, v_gla_a2_bwd),
        gla_a_bias_fwd=(gla_a_bias_fwd, m_gla_a_bias_fwd, v_gla_a_bias_fwd),
        gla_a_bias_bwd=(gla_a_bias_bwd, m_gla_a_bias_bwd, v_gla_a_bias_bwd),
        hgrn_lb_logits=(hgrn_lb_logits, m_hgrn_lb_logits, v_hgrn_lb_logits),
        gla_norm_gain=(gla_norm_gain, m_gla_norm_gain, v_gla_norm_gain),
        hgrn_norm_gain=(hgrn_norm_gain, m_hgrn_norm_gain, v_hgrn_norm_gain),
        w_mix_out=(w_mix_out, m_w_mix_out, v_w_mix_out), ffn2_w_in=(ffn2_w_in, m_ffn2_w_in, v_ffn2_w_in),
        ffn2_w_out=(ffn2_w_out, m_ffn2_w_out, v_ffn2_w_out))
    order = list(params.keys())
    big_names = ("w_ada",) + BIG
    upd = {}
    for k in big_names:
        w_, m_, v_ = params[k]
        s2 = w_.shape[-2:]
        d_, nm_, nv_ = _adamw(w_.reshape(s2), grads[k].reshape(s2), m_.reshape(s2), v_.reshape(s2), "adamw_" + k)
        upd[k] = (d_.reshape(w_.shape), nm_.reshape(w_.shape), nv_.reshape(w_.shape))
    small_names = [k for k in order if k not in big_names]
    sizes = [int(np.prod(params[k][0].shape)) for k in small_names]
    tot_n = sum(sizes)
    npad = -(-tot_n // 1024) * 1024

    def packed(get):
        flat_ = jnp.concatenate([get(k).reshape(-1) for k in small_names])
        return jnp.pad(flat_, (0, npad - tot_n)).reshape(8, npad // 8)

    d_s, nm_s, nv_s = _adamw(packed(lambda k: params[k][0]), packed(lambda k: grads[k]),
                             packed(lambda k: params[k][1]), packed(lambda k: params[k][2]), "adamw_small")
    o = 0
    for k, n in zip(small_names, sizes):
        shp = params[k][0].shape
        upd[k] = tuple(a.reshape(-1)[o:o + n].reshape(shp) for a in (d_s, nm_s, nv_s))
        o += n

    return (loss, grad_x, *[grads[k].reshape(params[k][0].shape) for k in order], *[upd[k][0] for k in order],
            *[upd[k][1] for k in order], *[upd[k][2] for k in order])
```

```python
import functools

import numpy as np
import jax
import jax.numpy as jnp
from jax import lax
from jax.experimental import pallas as pl
from jax.experimental.pallas import tpu as pltpu

F32 = jnp.float32
BF16 = jnp.bfloat16
MESH = pl.DeviceIdType.MESH

D = 1024
DFF = 2816
TM = 256
CH = 32
NCB = TM // CH
SB = 128
CSB = SB // CH
NSB = TM // SB
HP = 4
HD = 128
NH = 8
LN_EPS = 1e-5
NORM_EPS = 1e-6
ALPHA = 2.0 ** 0.25
GATE_NORM = 16.0
GLA_DK = 64
N_MOD = 9
VMEM_LIMIT = 52 * 1024 * 1024

MIXP = 5120
GG, RG, GQ, GK, GV, RQ, RFF, RFB, RI, LR = 0, 512, 1024, 1536, 2048, 2560, 3072, 3584, 4096, 4608
IN_SPLITS = (256, 256, 512, 512, 16, 16, 512, 512, 512, 512, 512)

ADAM_LR, ADAM_B1, ADAM_B2, ADAM_EPS, ADAM_WD, ADAM_STEP = 0.001, 0.9, 0.999, 1e-08, 0.01, 10


def _cp(n_axes):
    return pltpu.CompilerParams(dimension_semantics=("arbitrary",) * n_axes, vmem_limit_bytes=VMEM_LIMIT)


def _rowmap(stride, off):
    return lambda b, i: (b * stride + off + i, 0)


def _mmap(comb):
    if comb:
        return lambda b, i: (b, jnp.minimum(i, 1), 0, 0)
    return lambda b, i: (b, 1, 0, 0)


def _ln(x):
    mu = jnp.mean(x, axis=-1, keepdims=True)
    xc = x - mu
    var = jnp.mean(xc * xc, axis=-1, keepdims=True)
    r = lax.rsqrt(var + LN_EPS)
    return xc * r, r


def _ln_bwd(dxh, xh, r):
    return r * (dxh - jnp.mean(dxh, axis=-1, keepdims=True) - xh * jnp.mean(dxh * xh, axis=-1, keepdims=True))


def _sigmoid(x):
    return 1.0 / (1.0 + jnp.exp(-x))


def _rsum(x):
    return jnp.sum(x, axis=0, keepdims=True)


def _dot(a, b):
    return jnp.dot(a, b, preferred_element_type=F32)


def _dot_nt(a, b):
    return lax.dot_general(a, b, (((1,), (1,)), ((), ())), preferred_element_type=F32)


def _dot_tn(a, b):
    return lax.dot_general(a, b, (((0,), (0,)), ((), ())), preferred_element_type=F32)


def _modulate(xv, m_ref, sub):
    xh, _ = _ln(xv)
    sh = m_ref[0, 0, 3 * sub:3 * sub + 1, :]
    sc = m_ref[0, 0, 3 * sub + 1:3 * sub + 2, :]
    return (xh * (1.0 + sh) + sc).astype(BF16)


def _embed_lnmod(x, ctx, pos, mvec):
    B, T, _ = x.shape
    nt = 1 + T // TM

    def body(x_ref, c_ref, p_ref, m_ref, o_ref, h_ref):
        i = pl.program_id(1)

        @pl.when(i == 0)
        def _():
            o_ref[...] = c_ref[0]

        @pl.when(i > 0)
        def _():
            o_ref[...] = x_ref[0] + p_ref[...]

        h_ref[...] = _modulate(o_ref[...], m_ref, 0)

    rows = pl.BlockSpec((TM, D), lambda b, i: (b * nt + i, 0))
    return pl.pallas_call(
        body, name="embed_lnmod0", grid=(B, nt),
        in_specs=[pl.BlockSpec((1, TM, D), lambda b, i: (b, jnp.maximum(i - 1, 0), 0)),
                  pl.BlockSpec((1, TM, D), lambda b, i: (b, 0, 0)),
                  pl.BlockSpec((TM, D), lambda b, i: (jnp.maximum(i - 1, 0), 0)),
                  pl.BlockSpec((1, 1, N_MOD, D), _mmap(True))],
        out_specs=[rows, rows],
        out_shape=[jax.ShapeDtypeStruct((B * nt * TM, D), F32), jax.ShapeDtypeStruct((B * nt * TM, D), BF16)],
        compiler_params=_cp(2))(x, ctx, pos, mvec)


def _lnmod0_bwd(dh, X, mvec, dres, x_shape, B, nt):
    def body(dh_ref, x_ref, m_ref, dr_ref, dx_ref, dm_ref):
        i = pl.program_id(1)
        xh, r = _ln(x_ref[...])
        sh = m_ref[0, 0, 0:1, :]
        dhv = dh_ref[...]

        @pl.when((i == 0) | (i == 1))
        def _():
            dm_ref[...] = jnp.zeros_like(dm_ref)

        dm_ref[0, 0, 0:1, :] += _rsum(dhv * xh)
        dm_ref[0, 0, 1:2, :] += _rsum(dhv)

        @pl.when(i > 0)
        def _():
            dx_ref[0] = _ln_bwd(dhv * (1.0 + sh), xh, r) + dr_ref[...]

    rows = pl.BlockSpec((TM, D), _rowmap(nt, 0))
    return pl.pallas_call(
        body, name="lnmod0_bwd", grid=(B, nt),
        in_specs=[rows, rows, pl.BlockSpec((1, 1, N_MOD, D), _mmap(True)), rows],
        out_specs=[pl.BlockSpec((1, TM, D), lambda b, i: (b, jnp.maximum(i - 1, 0), 0)),
                   pl.BlockSpec((1, 1, 2, D), _mmap(True))],
        out_shape=[jax.ShapeDtypeStruct(x_shape, F32), jax.ShapeDtypeStruct((B, 2, 2, D), F32)],
        compiler_params=_cp(2))(dh, X, mvec, dres)


def _resid_fwd(x_ref, y_ref, m_ref, gb_ref, sub, w):
    wg = w * m_ref[0, 0, 3 * sub + 2:3 * sub + 3, :]
    y = y_ref[...]
    zh, r = _ln(ALPHA * x_ref[...] + wg * y)
    return y, wg, zh, r


def _resid_grads(do, y, wg, zh, r, w, gb_ref, comb, dx_ref, dy_ref, dg_ref, dgb_ref):
    b_, i = pl.program_id(0), pl.program_id(1)
    dz = _ln_bwd(do * gb_ref[0:1, :], zh, r)
    dx_ref[...] = ALPHA * dz
    dy_ref[...] = (wg * dz).astype(BF16)

    @pl.when((b_ == 0) & (i == 0))
    def _():
        dgb_ref[...] = jnp.zeros_like(dgb_ref)

    dgb_ref[0:1, :] += _rsum(do * zh)
    dgb_ref[1:2, :] += _rsum(do)

    init = (i == 0) | (i == 1) if comb else (i == 0)

    @pl.when(init)
    def _():
        dg_ref[...] = jnp.zeros_like(dg_ref)

    dg_ref[0, 0] += w * _rsum(dz * y)


def _resid_lnmod(X, lx, Y, mvec, comb, sub, w, gb, B, nt, name):
    def body(x_ref, y_ref, m_ref, gb_ref, o_ref, h_ref):
        _, _, zh, _ = _resid_fwd(x_ref, y_ref, m_ref, gb_ref, sub, w)
        xn = zh * gb_ref[0:1, :] + gb_ref[1:2, :]
        o_ref[...] = xn
        h_ref[...] = _modulate(xn, m_ref, sub + 1)

    rows = pl.BlockSpec((TM, D), _rowmap(nt, 0))
    return pl.pallas_call(
        body, name=name, grid=(B, nt),
        in_specs=[pl.BlockSpec((TM, D), _rowmap(*lx)), rows,
                  pl.BlockSpec((1, 1, N_MOD, D), _mmap(comb)), pl.BlockSpec((2, D), lambda b, i: (0, 0))],
        out_specs=[rows, rows],
        out_shape=[jax.ShapeDtypeStruct((B * nt * TM, D), F32), jax.ShapeDtypeStruct((B * nt * TM, D), BF16)],
        compiler_params=_cp(2))(X, Y, mvec, gb)


def _resid_out_shapes(B, nt, comb):
    rows = pl.BlockSpec((TM, D), _rowmap(nt, 0))
    specs = [rows, rows, pl.BlockSpec((1, 1, 1, D), _mmap(comb)), pl.BlockSpec((2, D), lambda b, i: (0, 0))]
    shapes = [jax.ShapeDtypeStruct((B * nt * TM, D), F32), jax.ShapeDtypeStruct((B * nt * TM, D), BF16),
              jax.ShapeDtypeStruct((B, 2, 1, D), F32), jax.ShapeDtypeStruct((2, D), F32)]
    return specs, shapes


def _tail(X, Y, mvec, gb, tgt, sub, w, B, nt):
    def body(x_ref, y_ref, m_ref, gb_ref, t_ref, dx_ref, dy_ref, dg_ref, dgb_ref, l_ref):
        y, wg, zh, r = _resid_fwd(x_ref, y_ref, m_ref, gb_ref, sub, w)
        e = (zh * gb_ref[0:1, :] + gb_ref[1:2, :]) - t_ref[0]

        @pl.when((pl.program_id(0) == 0) & (pl.program_id(1) == 0))
        def _():
            l_ref[...] = jnp.zeros_like(l_ref)

        l_ref[...] += _rsum(e * e)
        _resid_grads(e * (1.0 / D), y, wg, zh, r, w, gb_ref, False, dx_ref, dy_ref, dg_ref, dgb_ref)

    rows = pl.BlockSpec((TM, D), _rowmap(nt, 0))
    specs, shapes = _resid_out_shapes(B, nt, False)
    return pl.pallas_call(
        body, name="resid2_loss_bwd", grid=(B, nt),
        in_specs=[rows, rows, pl.BlockSpec((1, 1, N_MOD, D), _mmap(False)), pl.BlockSpec((2, D), lambda b, i: (0, 0)),
                  pl.BlockSpec((1, TM, D), lambda b, i: (b, i, 0))],
        out_specs=specs + [pl.BlockSpec((1, D), lambda b, i: (0, 0))],
        out_shape=shapes + [jax.ShapeDtypeStruct((1, D), F32)],
        compiler_params=_cp(2))(X, Y, mvec, gb, tgt)


def _lnmod_resid_bwd(dh, Xi, lxi, dres, Xp, lxp, Yp, mvec, comb, sub, w, gb, B, nt, name):
    ntl = nt - 1 if comb else nt

    def body(dh_ref, xi_ref, dr_ref, xp_ref, yp_ref, m_ref, gb_ref, dx_ref, dy_ref, dg_ref, dgb_ref, dm_ref):
        i = pl.program_id(1)
        xh, r = _ln(xi_ref[...])
        sh = m_ref[0, 0, 3 * sub:3 * sub + 1, :]
        dhv = dh_ref[...]
        dr = dr_ref[...]
        if comb:
            dr = jnp.where(i > 0, dr, 0.0)
        do = _ln_bwd(dhv * (1.0 + sh), xh, r) + dr

        init = (i == 0) | (i == 1) if comb else (i == 0)

        @pl.when(init)
        def _():
            dm_ref[...] = jnp.zeros_like(dm_ref)

        dm_ref[0, 0, 0:1, :] += _rsum(dhv * xh)
        dm_ref[0, 0, 1:2, :] += _rsum(dhv)
        y, wg, zh, r2 = _resid_fwd(xp_ref, yp_ref, m_ref, gb_ref, sub - 1, w)
        _resid_grads(do, y, wg, zh, r2, w, gb_ref, comb, dx_ref, dy_ref, dg_ref, dgb_ref)

    rows = pl.BlockSpec((TM, D), _rowmap(nt, 0))
    if comb:
        dres_spec = pl.BlockSpec((TM, D), lambda b, i: (b * ntl + jnp.maximum(i - 1, 0), 0))
    else:
        dres_spec = rows
    specs, shapes = _resid_out_shapes(B, nt, comb)
    return pl.pallas_call(
        body, name=name, grid=(B, nt),
        in_specs=[rows, pl.BlockSpec((TM, D), _rowmap(*lxi)), dres_spec, pl.BlockSpec((TM, D), _rowmap(*lxp)), rows,
                  pl.BlockSpec((1, 1, N_MOD, D), _mmap(comb)), pl.BlockSpec((2, D), lambda b, i: (0, 0))],
        out_specs=specs + [pl.BlockSpec((1, 1, 2, D), _mmap(comb))],
        out_shape=shapes + [jax.ShapeDtypeStruct((B, 2, 2, D), F32)],
        compiler_params=_cp(2))(dh, Xi, dres, Xp, Yp, mvec, gb)


def _ffn_out_dx(dy, W, u, name, dep=None):
    M = dy.shape[0]
    half = DFF // 2
    deps = [] if dep is None else [dep]

    def body(dy_ref, w_ref, u_ref, *rest):
        du_ref = rest[-1]
        dyv = dy_ref[...]
        for j in range(2):
            lo, hi = j * half, (j + 1) * half
            da = _dot_nt(dyv, w_ref[lo:hi, :])
            g = u_ref[:, lo:hi].astype(F32)
            up = u_ref[:, DFF + lo:DFF + hi].astype(F32)
            s = _sigmoid(g)
            du_ref[:, lo:hi] = (da * up * (s * (1.0 + g * (1.0 - s)))).astype(BF16)
            du_ref[:, DFF + lo:DFF + hi] = (da * (g * s)).astype(BF16)

    return pl.pallas_call(
        body, name=name, grid=(M // TM,),
        in_specs=[pl.BlockSpec((TM, D), lambda i: (i, 0)), _wspec(W, 1), pl.BlockSpec((TM, 2 * DFF), lambda i: (i, 0))]
        + [_wspec(d, 1) for d in deps],
        out_specs=pl.BlockSpec((TM, 2 * DFF), lambda i: (i, 0)),
        out_shape=jax.ShapeDtypeStruct((M, 2 * DFF), BF16), compiler_params=_cp(1))(dy, W, u, *deps)


def _wspec(W, nidx):
    zeros = (0,) * W.ndim
    if nidx == 1:
        return pl.BlockSpec(W.shape, lambda i: zeros)
    return pl.BlockSpec(W.shape, lambda b, i: zeros)


def _mm_nn(A, la, W, B, nt, out_dtype, name):
    K, N = W.shape

    def body(a_ref, w_ref, o_ref):
        o_ref[...] = _dot(a_ref[...], w_ref[...]).astype(out_dtype)

    return pl.pallas_call(
        body, name=name, grid=(B, nt),
        in_specs=[pl.BlockSpec((TM, K), _rowmap(*la)), _wspec(W, 2)],
        out_specs=pl.BlockSpec((TM, N), _rowmap(nt, 0)),
        out_shape=jax.ShapeDtypeStruct((B * nt * TM, N), out_dtype), compiler_params=_cp(2))(A, W)


def _ffn_in(A, la, W3, B, nt, name):
    K, n = W3.shape[1:]

    def body(a_ref, w_ref, u_ref, s_ref):
        a = a_ref[...]
        for j in range(2):
            g = _dot(a, w_ref[j])
            up = _dot(a, w_ref[j + 2])
            u_ref[:, j * n:(j + 1) * n] = g.astype(BF16)
            u_ref[:, (j + 2) * n:(j + 3) * n] = up.astype(BF16)
            s_ref[:, j * n:(j + 1) * n] = (g * _sigmoid(g) * up).astype(BF16)

    rows = B * nt * TM
    return pl.pallas_call(
        body, name=name, grid=(B, nt),
        in_specs=[pl.BlockSpec((TM, K), _rowmap(*la)), _wspec(W3, 2)],
        out_specs=[pl.BlockSpec((TM, 4 * n), _rowmap(nt, 0)), pl.BlockSpec((TM, 2 * n), _rowmap(nt, 0))],
        out_shape=[jax.ShapeDtypeStruct((rows, 4 * n), BF16), jax.ShapeDtypeStruct((rows, 2 * n), BF16)],
        compiler_params=_cp(2))(A, W3)


def _mm_nt(A, W, name, dep=None):
    M, N = A.shape
    K = W.shape[-2]

    def body(a_ref, w_ref, *rest):
        o_ref = rest[-1]
        if W.ndim == 3:
            n = W.shape[-1]
            acc = _dot_nt(a_ref[:, 0:n], w_ref[0])
            for j in range(1, 4):
                acc = acc + _dot_nt(a_ref[:, j * n:(j + 1) * n], w_ref[j])
            o_ref[...] = acc
        else:
            o_ref[...] = _dot_nt(a_ref[...], w_ref[...])

    deps = [] if dep is None else [dep]
    return pl.pallas_call(
        body, name=name, grid=(M // TM,),
        in_specs=[pl.BlockSpec((TM, N), lambda i: (i, 0)), _wspec(W, 1)] + [_wspec(d, 1) for d in deps],
        out_specs=pl.BlockSpec((TM, K), lambda i: (i, 0)),
        out_shape=jax.ShapeDtypeStruct((M, K), F32), compiler_params=_cp(1))(A, W, *deps)


def _mm_tn(A, G, name, tk=512, tn=512, shards=None):
    M, K = A.shape
    N = G.shape[1]
    if shards:
        tn = N // shards

    def body(a_ref, g_ref, o_ref):
        @pl.when(pl.program_id(1) == 0)
        def _():
            o_ref[...] = jnp.zeros_like(o_ref)

        upd = _dot_tn(a_ref[...], g_ref[...])
        if shards:
            o_ref[0] += upd
        else:
            o_ref[...] += upd

    if shards:
        out_spec = pl.BlockSpec((1, K, tn), lambda n, k: (n, 0, 0))
        out_shape = jax.ShapeDtypeStruct((shards, K, tn), F32)
    else:
        out_spec = pl.BlockSpec((K, tn), lambda n, k: (0, n))
        out_shape = jax.ShapeDtypeStruct((K, N), F32)
    return pl.pallas_call(
        body, name=name, grid=(N // tn, M // tk),
        in_specs=[pl.BlockSpec((tk, K), lambda n, k: (k, 0)), pl.BlockSpec((tk, tn), lambda n, k: (k, n))],
        out_specs=out_spec, out_shape=out_shape, compiler_params=_cp(2))(A, G)


def _logsig(z):
    return jnp.minimum(z, 0.0) - jnp.log(1.0 + jnp.exp(-jnp.abs(z)))


def _features(F, a2p, biasp, lbp, B):
    nt = F.shape[0] // (B * TM)
    nj = nt + 1
    R2 = nj * TM

    def body(f_ref, a2_ref, bias_ref, lb_ref, q_ref, k_ref, v_ref, g_ref):
        lr = f_ref[:, LR:LR + HD].astype(BF16)
        lane = lax.broadcasted_iota(jnp.int32, (1, 4 * HD), 1)
        keep = (lane & (HD - 1)) < GLA_DK
        for d in range(2):
            z = _dot(lr, a2_ref[d]) + bias_ref[d:d + 1, :]
            gl = jnp.where(keep, _logsig(z) * (1.0 / GATE_NORM), 0.0)
            for h in range(4):
                g_ref[d, 0, h] = gl[:, h * HD:(h + 1) * HD]
        for h in range(4):
            q_ref[0, h] = f_ref[:, GQ + h * HD:GQ + (h + 1) * HD] * (GLA_DK ** -0.5)
            kk = f_ref[:, GK + h * HD:GK + (h + 1) * HD]
            k_ref[0, 0, h] = kk
            k_ref[1, 0, h] = kk
            v_ref[0, h] = f_ref[:, GV + h * HD:GV + (h + 1) * HD].astype(BF16)
        for h in range(4):
            sl = slice(h * HD, (h + 1) * HD)
            rq = f_ref[:, RQ + h * HD:RQ + (h + 1) * HD]
            q_ref[0, 4 + h] = rq * _sigmoid(rq) * (HD ** -0.5)
            v_ref[0, 4 + h] = f_ref[:, RI + h * HD:RI + (h + 1) * HD].astype(BF16)
            for d, off in ((0, RFF), (1, RFB)):
                lb = lb_ref[d:d + 1, sl]
                f = lb + (1.0 - lb) * _sigmoid(f_ref[:, off + h * HD:off + (h + 1) * HD])
                g_ref[d, 0, 4 + h] = jnp.log(f)
                k_ref[d, 0, 4 + h] = 1.0 - f

    fmap = lambda b, j: (b * nt + jnp.where(j == nt, 0, j), 0)
    one = pl.BlockSpec((1, NH, TM, HD), lambda b, j: (b, 0, j, 0))
    two = pl.BlockSpec((2, 1, NH, TM, HD), lambda b, j: (0, b, 0, j, 0))
    s1 = jax.ShapeDtypeStruct((B, NH, R2, HD), F32)
    s2 = jax.ShapeDtypeStruct((2, B, NH, R2, HD), F32)
    return pl.pallas_call(
        body, name="mix_features", grid=(B, nj),
        in_specs=[pl.BlockSpec((TM, MIXP), fmap), pl.BlockSpec((2, HD, 4 * HD), lambda b, j: (0, 0, 0)),
                  pl.BlockSpec((2, 4 * HD), lambda b, j: (0, 0)), pl.BlockSpec((2, 4 * HD), lambda b, j: (0, 0))],
        out_specs=[one, two, one, two], out_shape=[s1, s2, jax.ShapeDtypeStruct(s1.shape, BF16), s2],
        compiler_params=_cp(2))(F, a2p, biasp, lbp)


def _features_bwd(F, a2p, biasp, lbp, dQ0, dQ1, dK0, dK1, dV0, dV1, dG0, dG1, dgates, B):
    nt = F.shape[0] // (B * TM)

    def body(f_ref, a2_ref, bias_ref, lb_ref, dq0, dq1, dk0, dk1, dv0, dv1, dg0, dg1, dgt_ref,
             df_ref, da2_ref, dbias_ref, dlb_ref):
        b_, i = pl.program_id(0), pl.program_id(1)

        @pl.when((b_ == 0) & (i == 0))
        def _():
            da2_ref[...] = jnp.zeros_like(da2_ref)
            dbias_ref[...] = jnp.zeros_like(dbias_ref)
            dlb_ref[...] = jnp.zeros_like(dlb_ref)

        df_ref[:, 0:2 * 4 * HD] = jnp.where(i > 0, dgt_ref[...], 0.0).astype(BF16)
        df_ref[:, LR + HD:] = jnp.zeros((TM, MIXP - LR - HD), BF16)
        lr = f_ref[:, LR:LR + HD].astype(BF16)
        lane = lax.broadcasted_iota(jnp.int32, (1, 4 * HD), 1)
        keep = (lane & (HD - 1)) < GLA_DK
        dlr = jnp.zeros((TM, HD), F32)
        dgs = (dg0, dg1)
        dks = (dk0, dk1)
        for d in range(2):
            z = _dot(lr, a2_ref[d]) + bias_ref[d:d + 1, :]
            dgl = jnp.concatenate([dgs[d][0, h] for h in range(4)], axis=1)
            dz = jnp.where(keep, dgl * (1.0 / GATE_NORM) * (1.0 - _sigmoid(z)), 0.0)
            dzb = dz.astype(BF16)
            dlr = dlr + _dot_nt(dzb, a2_ref[d])
            da2_ref[d] += _dot_tn(lr, dzb)
            dbias_ref[d:d + 1, :] += _rsum(dz)
        df_ref[:, LR:LR + HD] = dlr.astype(BF16)
        for h in range(4):
            df_ref[:, GQ + h * HD:GQ + (h + 1) * HD] = ((dq0[0, h] + dq1[0, h]) * (GLA_DK ** -0.5)).astype(BF16)
            df_ref[:, GK + h * HD:GK + (h + 1) * HD] = (dk0[0, h] + dk1[0, h]).astype(BF16)
            df_ref[:, GV + h * HD:GV + (h + 1) * HD] = (dv0[0, h] + dv1[0, h]).astype(BF16)
        for h in range(4):
            sl = slice(h * HD, (h + 1) * HD)
            rq = f_ref[:, RQ + h * HD:RQ + (h + 1) * HD]
            s = _sigmoid(rq)
            dqh = dq0[0, 4 + h] + dq1[0, 4 + h]
            df_ref[:, RQ + h * HD:RQ + (h + 1) * HD] = (dqh * (HD ** -0.5) * (s * (1.0 + rq * (1.0 - s)))).astype(BF16)
            df_ref[:, RI + h * HD:RI + (h + 1) * HD] = (dv0[0, 4 + h] + dv1[0, 4 + h]).astype(BF16)
            for d, off in ((0, RFF), (1, RFB)):
                lb = lb_ref[d:d + 1, sl]
                sg = _sigmoid(f_ref[:, off + h * HD:off + (h + 1) * HD])
                f = lb + (1.0 - lb) * sg
                dff = dgs[d][0, 4 + h] / f - dks[d][0, 4 + h]
                df_ref[:, off + h * HD:off + (h + 1) * HD] = (dff * (1.0 - lb) * sg * (1.0 - sg)).astype(BF16)
                dlb_ref[d:d + 1, sl] += _rsum(dff * (1.0 - sg))

    m0 = lambda b, i: (b, 0, i, 0)
    m1 = lambda b, i: (b, 0, jnp.where(i == 0, nt, i), 0)
    one = lambda m: pl.BlockSpec((1, NH, TM, HD), m)
    return pl.pallas_call(
        body, name="mix_features_bwd", grid=(B, nt),
        in_specs=[pl.BlockSpec((TM, MIXP), _rowmap(nt, 0)), pl.BlockSpec((2, HD, 4 * HD), lambda b, i: (0, 0, 0)),
                  pl.BlockSpec((2, 4 * HD), lambda b, i: (0, 0)), pl.BlockSpec((2, 4 * HD), lambda b, i: (0, 0)),
                  one(m0), one(m1), one(m0), one(m1), one(m0), one(m1), one(m0), one(m1),
                  pl.BlockSpec((TM, D), lambda b, i: (b * (nt - 1) + jnp.maximum(i - 1, 0), 0))],
        out_specs=[pl.BlockSpec((TM, MIXP), _rowmap(nt, 0)), pl.BlockSpec((2, HD, 4 * HD), lambda b, i: (0, 0, 0)),
                   pl.BlockSpec((2, 4 * HD), lambda b, i: (0, 0)), pl.BlockSpec((2, 4 * HD), lambda b, i: (0, 0))],
        out_shape=[jax.ShapeDtypeStruct((B * nt * TM, MIXP), BF16), jax.ShapeDtypeStruct((2, HD, 4 * HD), F32),
                   jax.ShapeDtypeStruct((2, 4 * HD), F32), jax.ShapeDtypeStruct((2, 4 * HD), F32)],
        compiler_params=_cp(2))(F, a2p, biasp, lbp, dQ0, dQ1, dK0, dK1, dV0, dV1, dG0, dG1, dgates)


def _chunk_scan(x, rin, fwd, inclusive=True):
    acc = x
    sft = 1
    while sft < CH:
        if fwd:
            acc = acc + jnp.where(rin >= sft, pltpu.roll(acc, sft, 0), 0.0)
        else:
            acc = acc + jnp.where(rin < CH - sft, pltpu.roll(acc, TM - sft, 0), 0.0)
        sft *= 2
    return acc if inclusive else acc - x


def _scan_common(q, k, v, g, rev):
    rin = lax.broadcasted_iota(jnp.int32, (TM, HD), 0) & (CH - 1)
    b = _chunk_scan(g, rin, not rev)
    xx = _chunk_scan(g, rin, rev, inclusive=False)
    eb = jnp.exp(b)
    qd = q * eb
    ki = k * jnp.exp(-b)
    kt = k * jnp.exp(xx)
    ri = lax.broadcasted_iota(jnp.int32, (SB, SB), 0)
    ci = lax.broadcasted_iota(jnp.int32, (SB, SB), 1)
    same = (ri >> 5) == (ci >> 5)
    lo = same & (ri >= ci)
    up = same & (ri <= ci)
    mask, maskT = (up, lo) if rev else (lo, up)
    re = lax.broadcasted_iota(jnp.int32, (SB, CSB * HD), 0) >> 5
    ce = lax.broadcasted_iota(jnp.int32, (SB, CSB * HD), 1) >> 7
    mexp = re == ce
    return rin, b, xx, eb, qd, ki, kt, mask, maskT, mexp


def _sub(x, s):
    return x[s * SB:(s + 1) * SB]


def _expand(xb, mexp):
    return jnp.where(mexp, jnp.concatenate([xb] * CSB, axis=1), jnp.zeros((), xb.dtype))


def _own(x, mexp):
    xm = jnp.where(mexp, x, 0.0)
    acc = xm[:, 0:HD]
    for n in range(1, CSB):
        acc = acc + xm[:, n * HD:(n + 1) * HD]
    return acc


def _stack(per_chunk, s):
    return jnp.concatenate(per_chunk[s * CSB:(s + 1) * CSB], axis=1)


def _state_pass(s0, eb, uts, rev):
    order = range(NCB - 1, -1, -1) if rev else range(NCB)
    states = [None] * NCB
    s = s0
    for n in order:
        row = n * CH if rev else n * CH + CH - 1
        states[n] = s
        s = eb[row:row + 1, :] * s + uts[n // CSB][:, (n % CSB) * HD:(n % CSB + 1) * HD]
    return states, s


def _scan_fwd(Q, K, V, G, rev, B):
    nb = Q.shape[2] // TM - 1
    d = 1 if rev else 0
    rmap = (lambda s: nb - s) if rev else (lambda s: s)

    def body(q_ref, k_ref, v_ref, g_ref, o_ref, st_ref, s_scr):
        @pl.when(pl.program_id(2) == 0)
        def _():
            s_scr[...] = jnp.zeros_like(s_scr)

        for p in range(HP):
            s0 = s_scr[p]
            st_ref[0, p, 0] = s0
            v = v_ref[0, p]
            _, _, _, eb, qd, ki, kt, mask, _, mexp = _scan_common(q_ref[0, p], k_ref[0, 0, p], v, g_ref[0, 0, p], rev)
            qb, kib, ktb, vb = qd.astype(BF16), ki.astype(BF16), kt.astype(BF16), v.astype(BF16)
            uts = [_dot_tn(_sub(vb, s), _expand(_sub(ktb, s), mexp)) for s in range(NSB)]
            states, s_new = _state_pass(s0, eb, uts, rev)
            s_scr[p] = s_new
            for s in range(NSB):
                a = jnp.where(mask, _dot_nt(_sub(qb, s), _sub(kib, s)), 0.0)
                o_ref[0, p, s * SB:(s + 1) * SB, :] = (
                    _dot(a.astype(BF16), _sub(vb, s))
                    + _dot_nt(_expand(_sub(qb, s), mexp), _stack(states, s).astype(BF16)))

    one = pl.BlockSpec((1, HP, TM, HD), lambda b, h, s: (b, h, rmap(s), 0))
    two = pl.BlockSpec((1, 1, HP, TM, HD), lambda b, h, s: (d, b, h, rmap(s), 0))
    return pl.pallas_call(
        body, name="scan_fwd_rev" if rev else "scan_fwd", grid=(B, NH // HP, nb),
        in_specs=[one, two, one, two],
        out_specs=[one, pl.BlockSpec((1, HP, 1, HD, HD), lambda b, h, s: (b, h, s, 0, 0))],
        out_shape=[jax.ShapeDtypeStruct(Q.shape, F32), jax.ShapeDtypeStruct((B, NH, nb, HD, HD), F32)],
        scratch_shapes=[pltpu.VMEM((HP, HD, HD), F32)],
        compiler_params=_cp(3))(Q, K, V, G)


def _scan_bwd(Q, K, V, G, St, dO, rev, B):
    nb = Q.shape[2] // TM - 1
    d = 1 if rev else 0
    smap = lambda t: nb - 1 - t
    rmap = (lambda t: nb - smap(t)) if rev else smap

    def body(q_ref, k_ref, v_ref, g_ref, st_ref, do_ref, dq_ref, dk_ref, dv_ref, dg_ref, ds_scr):
        t = pl.program_id(2)

        @pl.when(t == 0)
        def _():
            ds_scr[...] = jnp.zeros_like(ds_scr)

        is_lat = smap(t) >= 1
        for p in range(HP):
            v = v_ref[0, p]
            rin, b, xx, eb, qd, ki, kt, mask, maskT, mexp = _scan_common(
                q_ref[0, p], k_ref[0, 0, p], v, g_ref[0, 0, p], rev)
            qb, kib, ktb, vb = qd.astype(BF16), ki.astype(BF16), kt.astype(BF16), v.astype(BF16)
            dob = jnp.where(is_lat, do_ref[0, p], 0.0).astype(BF16)
            kt_exps = [_expand(_sub(ktb, s), mexp) for s in range(NSB)]
            uts = [_dot_tn(_sub(vb, s), kt_exps[s]) for s in range(NSB)]
            states, _ = _state_pass(st_ref[0, p, 0], eb, uts, rev)
            gts = [_dot_tn(_sub(dob, s), _expand(_sub(qb, s), mexp)) for s in range(NSB)]
            order = range(NCB) if rev else range(NCB - 1, -1, -1)
            dsp = [None] * NCB
            t2 = [None] * NCB
            dsc = ds_scr[p]
            for n in order:
                row = n * CH if rev else n * CH + CH - 1
                ebl = eb[row:row + 1, :]
                dsp[n] = dsc
                t2[n] = jnp.broadcast_to(ebl * _rsum(states[n] * dsc), (CH, HD))
                dsc = gts[n // CSB][:, (n % CSB) * HD:(n % CSB + 1) * HD] + ebl * dsc
            ds_scr[p] = dsc
            dqds, dkis, dkts = [], [], []
            for s in range(NSB):
                q_s, ki_s, v_s, do_s = _sub(qb, s), _sub(kib, s), _sub(vb, s), _sub(dob, s)
                dspb = _stack(dsp, s).astype(BF16)
                da = jnp.where(mask, _dot_nt(do_s, v_s), 0.0).astype(BF16)
                dat = jnp.where(maskT, _dot_nt(v_s, do_s), 0.0).astype(BF16)
                at = jnp.where(maskT, _dot_nt(ki_s, q_s), 0.0).astype(BF16)
                dqds.append(_dot(da, ki_s) + _own(_dot(do_s, _stack(states, s).astype(BF16)), mexp))
                dkis.append(_dot(dat, q_s))
                dv_ref[0, p, s * SB:(s + 1) * SB, :] = _dot(at, do_s) + _dot_nt(kt_exps[s], dspb)
                dkts.append(_own(_dot(v_s, dspb), mexp))
            dqd, dki, dkt = (jnp.concatenate(parts, axis=0) for parts in (dqds, dkis, dkts))
            z = dkt * kt
            db = dqd * qd - dki * ki
            dq_ref[0, p] = dqd * eb
            dk_ref[0, p] = dki * jnp.exp(-b) + dkt * jnp.exp(xx)
            dg_ref[0, p] = (_chunk_scan(db, rin, rev) + _chunk_scan(z, rin, not rev, inclusive=False)
                            + jnp.concatenate(t2, axis=0))

    one = pl.BlockSpec((1, HP, TM, HD), lambda b, h, t: (b, h, rmap(t), 0))
    two = pl.BlockSpec((1, 1, HP, TM, HD), lambda b, h, t: (d, b, h, rmap(t), 0))
    lat = pl.BlockSpec((1, HP, TM, HD), lambda b, h, t: (b, h, jnp.clip(rmap(t) - 1, 0, nb - 2), 0))
    shp = jax.ShapeDtypeStruct(Q.shape, F32)
    return pl.pallas_call(
        body, name="scan_bwd_rev" if rev else "scan_bwd", grid=(B, NH // HP, nb),
        in_specs=[one, two, one, two, pl.BlockSpec((1, HP, 1, HD, HD), lambda b, h, t: (b, h, smap(t), 0, 0)), lat],
        out_specs=[one, one, one, one], out_shape=[shp, shp, shp, shp],
        scratch_shapes=[pltpu.VMEM((HP, HD, HD), F32)],
        compiler_params=_cp(3))(Q, K, V, G, St, dO)


def _gnorm(O0, O1, F, gains, B, ntl):
    nt = ntl + 1

    def body(o0_ref, o1_ref, f_ref, gn_ref, m_ref):
        for h in range(NH):
            o = o0_ref[0, h] + o1_ref[0, h]
            r = lax.rsqrt(jnp.mean(o * o, axis=-1, keepdims=True) + NORM_EPS)
            gn = gn_ref[0:1, :] if h < 4 else gn_ref[1:2, :]
            gt = f_ref[:, h * HD:(h + 1) * HD]
            m_ref[:, h * HD:(h + 1) * HD] = (o * r * gn * (gt * _sigmoid(gt))).astype(BF16)

    ospec = pl.BlockSpec((1, NH, TM, HD), lambda b, i: (b, 0, i + 1, 0))
    return pl.pallas_call(
        body, name="gated_norm", grid=(B, ntl),
        in_specs=[ospec, ospec, pl.BlockSpec((TM, D), lambda b, i: (b * nt + 1 + i, 0)),
                  pl.BlockSpec((2, HD), lambda b, i: (0, 0))],
        out_specs=pl.BlockSpec((TM, D), _rowmap(ntl, 0)),
        out_shape=jax.ShapeDtypeStruct((B * ntl * TM, D), BF16), compiler_params=_cp(2))(O0, O1, F, gains)


def _gnorm_bwd(dM, O0, O1, F, gains, B, ntl):
    nt = ntl + 1

    def body(dm_ref, o0_ref, o1_ref, f_ref, gn_ref, do_ref, dgt_ref, dgn_ref):
        b_, i = pl.program_id(0), pl.program_id(1)

        @pl.when((b_ == 0) & (i == 0))
        def _():
            dgn_ref[...] = jnp.zeros_like(dgn_ref)

        for h in range(NH):
            o = o0_ref[0, h] + o1_ref[0, h]
            r = lax.rsqrt(jnp.mean(o * o, axis=-1, keepdims=True) + NORM_EPS)
            y = o * r
            gn = gn_ref[0:1, :] if h < 4 else gn_ref[1:2, :]
            gt = f_ref[:, h * HD:(h + 1) * HD]
            s = _sigmoid(gt)
            dm = dm_ref[:, h * HD:(h + 1) * HD]
            don = dm * (gt * s)
            dgt_ref[:, h * HD:(h + 1) * HD] = dm * (y * gn) * (s * (1.0 + gt * (1.0 - s)))
            row = 0 if h < 4 else 1
            dgn_ref[row:row + 1, :] += _rsum(don * y)
            dy = don * gn
            do_ref[0, h] = r * (dy - y * jnp.mean(dy * y, axis=-1, keepdims=True))

    ospec = pl.BlockSpec((1, NH, TM, HD), lambda b, i: (b, 0, i + 1, 0))
    return pl.pallas_call(
        body, name="gated_norm_bwd", grid=(B, ntl),
        in_specs=[pl.BlockSpec((TM, D), _rowmap(ntl, 0)), ospec, ospec,
                  pl.BlockSpec((TM, D), lambda b, i: (b * nt + 1 + i, 0)), pl.BlockSpec((2, HD), lambda b, i: (0, 0))],
        out_specs=[pl.BlockSpec((1, NH, TM, HD), lambda b, i: (b, 0, i, 0)), pl.BlockSpec((TM, D), _rowmap(ntl, 0)),
                   pl.BlockSpec((2, HD), lambda b, i: (0, 0))],
        out_shape=[jax.ShapeDtypeStruct((B, NH, ntl * TM, HD), F32), jax.ShapeDtypeStruct((B * ntl * TM, D), F32),
                   jax.ShapeDtypeStruct((2, HD), F32)],
        compiler_params=_cp(2))(dM, O0, O1, F, gains)


def _sincos_2d(rows, width, dim):
    r = jnp.repeat(jnp.arange(rows), width)
    col = jnp.tile(jnp.arange(width), rows)
    quarter = dim // 4
    omega = 1.0 / 10000.0 ** (jnp.arange(quarter, dtype=F32) / quarter)

    def emb(p):
        a = p.astype(F32)[:, None] * omega[None, :]
        return jnp.concatenate([jnp.sin(a), jnp.cos(a)], axis=-1)

    return jnp.concatenate([emb(r), emb(col)], axis=-1)


def _pad_heads(w):
    k = w.shape[0]
    return jnp.pad(w.reshape(k, 4, GLA_DK), ((0, 0), (0, 0), (0, HD - GLA_DK))).reshape(k, 4 * HD)


def _unpad_heads(w):
    k = w.shape[0]
    return w.reshape(k, 4, HD)[:, :, :GLA_DK].reshape(k, 4 * GLA_DK)


MIX_N = 1032
MIX_NP = 1152
_SEGS = ([(64 * h, 64, GQ + HD * h) for h in range(4)] + [(256 + 64 * h, 64, GK + HD * h) for h in range(4)]
         + [(512, 512, GV), (1024, 512, GG), (1536, 32, LR), (1568, 512, RQ), (2080, 512, RFF), (2592, 512, RFB),
            (3104, 512, RI), (3616, 512, RG)])


def _mix_in_to_padded(ps):
    k = ps.shape[1]
    parts, pos = [], 0
    for g0, ln, s0 in sorted(_SEGS, key=lambda s: s[2]):
        if s0 > pos:
            parts.append(jnp.zeros((k, s0 - pos), ps.dtype))
        for j in range(4):
            lo, hi = max(g0, j * MIX_N), min(g0 + ln, (j + 1) * MIX_N)
            if lo < hi:
                parts.append(ps[j][:, lo - j * MIX_N:hi - j * MIX_N])
        pos = s0 + ln
    parts.append(jnp.zeros((k, MIXP - pos), ps.dtype))
    return jnp.concatenate(parts, axis=1)


def _mix_in_from_padded(g):
    k = g.shape[0]
    shards = []
    for j in range(4):
        parts = []
        for g0, ln, s0 in sorted(_SEGS):
            lo, hi = max(g0, j * MIX_N), min(g0 + ln, (j + 1) * MIX_N)
            if lo < hi:
                parts.append(g[:, s0 + lo - g0:s0 + hi - g0])
        parts.append(jnp.zeros((k, MIX_NP - MIX_N), g.dtype))
        shards.append(jnp.concatenate(parts, axis=1))
    return jnp.stack(shards)


def _local_step(x, ctx, tgt, mvec, weights_for, ln_gain, ln_bias, a2f, a2b, abf, abb, lb, gng, gnh, on_grads, on_sent):
    B, T, _ = x.shape
    assert ctx.shape[1] == TM and T % TM == 0
    ntl = T // TM
    nt = ntl + 1
    C, L, CL = (nt, 0), (ntl, 0), (nt, 1)
    pos = _sincos_2d(T // 64, 64, D)
    gbs = [jnp.stack([ln_gain[i], ln_bias[i]]) for i in range(3)]
    a2p = jnp.zeros((2, HD, 4 * HD), F32)
    a2p = a2p.at[0, 0:16].set(_pad_heads(a2f)).at[1, 16:32].set(_pad_heads(a2b)).astype(BF16)
    biasp = jnp.concatenate([_pad_heads(abf.reshape(1, -1)), _pad_heads(abb.reshape(1, -1))], axis=0)
    gains = jnp.concatenate([gng.reshape(1, HD), gnh.reshape(1, HD)], axis=0)

    X0, h0 = _embed_lnmod(x, ctx, pos, mvec)
    w1i, w1o = weights_for("ffn1", h0)
    u0, a0 = _ffn_in(h0, C, w1i, B, nt, "ffn1_in")
    wmp, wmo = weights_for("mix", a0)
    y0 = _mm_nn(a0, C, w1o, B, nt, F32, "ffn1_out")
    X1, h1 = _resid_lnmod(X0, C, y0, mvec, True, 0, 0.5, gbs[0], B, nt, "resid0_lnmod1")
    Fm = _mm_nn(h1, C, wmp, B, nt, F32, "mix_in")
    w2i, w2o = weights_for("ffn2", Fm)
    Q, K, V, G = _features(Fm, a2p, biasp, lb, B)
    O0, S0 = _scan_fwd(Q, K, V, G, False, B)
    O1, S1 = _scan_fwd(Q, K, V, G, True, B)
    merged = _gnorm(O0, O1, Fm, gains, B, ntl)
    y1 = _mm_nn(merged, L, wmo, B, ntl, F32, "mix_out")
    X2, h2 = _resid_lnmod(X1, CL, y1, mvec, False, 1, 1.0, gbs[1], B, ntl, "resid1_lnmod2")
    u2, a2 = _ffn_in(h2, L, w2i, B, ntl, "ffn2_in")
    y2 = _mm_nn(a2, L, w2o, B, ntl, F32, "ffn2_out")

    dx2r, dy2, dgate2, dgb2, lsum = _tail(X2, y2, mvec, gbs[2], tgt, 2, 0.5, B, ntl)
    loss = (0.5 / D) * jnp.sum(lsum)
    du2 = _ffn_out_dx(dy2, w2o, u2, "ffn2_out_dx")
    g_w2o = _mm_tn(a2, dy2, "ffn2_out_dw")
    dh2 = _mm_nt(du2, w2i, "ffn2_in_dx")
    g_w2i = _mm_tn(h2, du2, "ffn2_in_dw", shards=4)
    tok = on_grads("ffn2", (g_w2i, g_w2o))
    dx1r, dy1, dgate1, dgb1, dss2 = _lnmod_resid_bwd(dh2, X2, L, dx2r, X1, CL, y1, mvec, False, 2, 1.0,
                                                     gbs[1] + tok[0, 0], B, ntl, "lnmod2_resid1_bwd")
    tok = on_sent("ffn2", dy1)
    dmerged = _mm_nt(dy1, wmo, "mix_out_dx", dep=tok)
    g_wmo = _mm_tn(merged, dy1, "mix_out_dw")
    dO, dgates, dgains = _gnorm_bwd(dmerged, O0, O1, Fm, gains, B, ntl)
    dQ0, dK0, dV0, dG0 = _scan_bwd(Q, K, V, G, S0, dO, False, B)
    dQ1, dK1, dV1, dG1 = _scan_bwd(Q, K, V, G, S1, dO, True, B)
    dF, da2p, dbiasp, dlb = _features_bwd(Fm, a2p, biasp, lb, dQ0, dQ1, dK0, dK1, dV0, dV1, dG0, dG1, dgates, B)
    dh1 = _mm_nt(dF, wmp, "mix_in_dx")
    g_wmp = _mm_tn(h1, dF, "mix_in_dw", tn=MIXP // 4)
    tok = on_grads("mix", (g_wmp, g_wmo))
    dx0r, dy0, dgate0, dgb0, dss1 = _lnmod_resid_bwd(dh1, X1, C, dx1r, X0, C, y0, mvec, True, 1, 0.5,
                                                     gbs[0] + tok[0, 0], B, nt, "lnmod1_resid0_bwd")
    tok = on_sent("mix", dy0)
    du0 = _ffn_out_dx(dy0, w1o, u0, "ffn1_out_dx", dep=tok)
    g_w1o = _mm_tn(a0, dy0, "ffn1_out_dw")
    g_w1i = _mm_tn(h0, du0, "ffn1_in_dw", shards=4)
    tok = on_grads("ffn1", (g_w1i, g_w1o))
    dh0 = _mm_nt(du0, w1i, "ffn1_in_dx", dep=tok)
    tok = on_sent("ffn1", dh0)
    grad_x, dss0 = _lnmod0_bwd(dh0, X0, mvec + tok[0, 0], dx0r, x.shape, B, nt)

    zero_ctx = lambda a: a.at[:, 0].set(0.0)
    dm = jnp.concatenate([dss0, dgate0, dss1, zero_ctx(dgate1), zero_ctx(dss2), zero_ctx(dgate2)], axis=2)
    small = dict(
        ln_gain=jnp.stack([dgb0[0], dgb1[0], dgb2[0]]), ln_bias=jnp.stack([dgb0[1], dgb1[1], dgb2[1]]),
        a2f=_unpad_heads(da2p[0, 0:16]), a2b=_unpad_heads(da2p[1, 16:32]),
        abf=_unpad_heads(dbiasp[0:1]), abb=_unpad_heads(dbiasp[1:2]), lb=dlb, gng=dgains[0], gnh=dgains[1])
    return loss, grad_x, dm, small


def _small_allgather(xs, name):
    r, n = xs.shape

    def body(x_ref, out_ref, send_sems, recv_sems, local_sem):
        x, y, c = lax.axis_index("x"), lax.axis_index("y"), lax.axis_index("c")
        me, sibling = (x, y, c), (x, y, 1 - c)
        chips = [(1 - x, y), (x, 1 - y), (1 - x, 1 - y)]

        def rows(px, py, pc):
            return out_ref.at[pl.ds((4 * px + 2 * py + pc) * r, r), :]

        def copy(k, block, to, src=None):
            return pltpu.make_async_remote_copy(
                src_ref=rows(*block) if src is None else src, dst_ref=rows(*block),
                send_sem=send_sems.at[k], recv_sem=recv_sems.at[k], device_id=to, device_id_type=MESH)

        mine = pltpu.make_async_copy(x_ref, rows(*me), local_sem)
        mine.start()
        first = [copy(0, me, sibling, src=x_ref)]
        first += [copy(1 + j, me, (*chip, c), src=x_ref) for j, chip in enumerate(chips)]
        for cp in first:
            cp.start()
        passed = [copy(4 + j, (*chip, c), sibling) for j, chip in enumerate(chips)]
        for j, chip in enumerate(chips):
            copy(1 + j, (*chip, c), me).wait_recv()
            passed[j].start()
        copy(0, sibling, me).wait_recv()
        for j, chip in enumerate(chips):
            copy(4 + j, (*chip, 1 - c), me).wait_recv()
        for cp in first + passed:
            cp.wait_send()
        mine.wait()

    out = pl.pallas_call(
        body, name=name,
        out_shape=jax.ShapeDtypeStruct((8 * r, n), xs.dtype),
        in_specs=[pl.BlockSpec(memory_space=pltpu.VMEM)],
        out_specs=pl.BlockSpec(memory_space=pltpu.VMEM),
        scratch_shapes=[pltpu.SemaphoreType.DMA((7,)), pltpu.SemaphoreType.DMA((7,)), pltpu.SemaphoreType.DMA],
        compiler_params=pltpu.CompilerParams(vmem_limit_bytes=VMEM_LIMIT))(xs)
    return out.reshape(8, r, n)


def _gather_flat(v, name):
    n = v.shape[0]
    npad = -(-n // 1024) * 1024
    g = _small_allgather(jnp.pad(v, (0, npad - n)).reshape(8, npad // 8), name)
    return g.reshape(8, npad)[:, :n]


HBM_SPEC = pl.BlockSpec(memory_space=pltpu.HBM)
SEM_SPEC = pl.BlockSpec(memory_space=pltpu.SEMAPHORE)
DATAFLOW = pltpu.SideEffectType.DATAFLOW_SIDE_EFFECTING


def _gather_copies(xs, outs, send_sems, recv_sems):
    x, y, c = lax.axis_index("x"), lax.axis_index("y"), lax.axis_index("c")
    dests = [(x, y, 1 - c), (1 - x, y, c), (x, 1 - y, c), (1 - x, 1 - y, c)]
    return [pltpu.make_async_remote_copy(
        src_ref=xs[w], dst_ref=outs[w].at[4 * x + 2 * y + c], send_sem=send_sems[4 * w + k],
        recv_sem=recv_sems[4 * w + k], device_id=dests[k], device_id_type=MESH)
        for w in range(len(xs)) for k in range(4)]


def _split_start(copies, per_w, srcs, land_lead, after, name):
    n = len(srcs)
    m = per_w * n
    lands = [lax.empty((land_lead,) + s.shape[-2:], s.dtype) for s in srcs]
    deps = [] if after is None else [after]

    def body(*refs):
        xs, ls, outs = refs[:n], refs[n:2 * n], refs[2 * n + len(deps):]
        for cp in copies(xs, ls, outs[:m], outs[m:2 * m]):
            cp.start()
        token = outs[2 * m + 2 * n]
        token[...] = jnp.zeros_like(token)

    outs = pl.pallas_call(
        body, name=name,
        out_shape=([pltpu.SemaphoreType.DMA(())] * (2 * m) + [pltpu.HBM(a.shape, a.dtype) for a in srcs + lands]
                   + [jax.ShapeDtypeStruct((8, 128), F32)]),
        in_specs=[HBM_SPEC] * (2 * n) + [pl.BlockSpec(memory_space=pl.ANY)] * len(deps),
        out_specs=[SEM_SPEC] * (2 * m) + [HBM_SPEC] * (2 * n) + [pl.BlockSpec(memory_space=pltpu.VMEM)],
        input_output_aliases={w: 2 * m + w for w in range(2 * n)},
        compiler_params=pltpu.CompilerParams(has_side_effects=DATAFLOW),
    )(*[pltpu.with_memory_space_constraint(a, pltpu.HBM) for a in srcs + lands], *deps)
    return outs[:2 * m], outs[2 * m:2 * m + n], outs[2 * m + n:2 * m + 2 * n], outs[2 * m + 2 * n]


def _split_wait(copies, per_w, sems, x_thru, l_thru, after, name):
    n = len(x_thru)
    m = per_w * n

    def body(*refs):
        xs, ls, ss = refs[:n], refs[n:2 * n], refs[2 * n:2 * n + 2 * m]
        for cp in copies(xs, ls, ss[:m], ss[m:]):
            cp.wait_send()
            cp.wait_recv()

    outs = pl.pallas_call(
        body, name=name,
        out_shape=[pltpu.HBM(a.shape, a.dtype) for a in list(x_thru) + list(l_thru)],
        in_specs=[HBM_SPEC] * (2 * n) + [SEM_SPEC] * (2 * m) + [pl.BlockSpec(memory_space=pl.ANY)],
        out_specs=[HBM_SPEC] * (2 * n),
        input_output_aliases={w: w for w in range(2 * n)},
        compiler_params=pltpu.CompilerParams(has_side_effects=DATAFLOW),
    )(*x_thru, *l_thru, *sems, after)
    return outs[:n], outs[n:]


def _gather_forward(gathered, name):
    n = len(gathered)

    def body(*refs):
        outs = refs[n:2 * n]
        send_sems, recv_sems = refs[2 * n:]
        x, y, c = lax.axis_index("x"), lax.axis_index("y"), lax.axis_index("c")
        chips = [(1 - x, y), (x, 1 - y), (1 - x, 1 - y)]

        def copy(w, j, pc):
            px, py = chips[j]
            slot = outs[w].at[4 * px + 2 * py + pc]
            return pltpu.make_async_remote_copy(
                src_ref=slot, dst_ref=slot, send_sem=send_sems.at[3 * w + j], recv_sem=recv_sems.at[3 * w + j],
                device_id=(x, y, 1 - c), device_id_type=MESH)

        sends = [copy(w, j, c) for w in range(n) for j in range(3)]
        for cp in sends:
            cp.start()
        for w in range(n):
            for j in range(3):
                copy(w, j, 1 - c).wait_recv()
        for cp in sends:
            cp.wait_send()

    any_spec = pl.BlockSpec(memory_space=pl.ANY)
    return pl.pallas_call(
        body, name=name,
        out_shape=[jax.ShapeDtypeStruct(g.shape, g.dtype) for g in gathered],
        in_specs=[any_spec] * n, out_specs=[any_spec] * n,
        input_output_aliases={w: w for w in range(n)},
        scratch_shapes=[pltpu.SemaphoreType.DMA((3 * n,)), pltpu.SemaphoreType.DMA((3 * n,))],
    )(*gathered)


def _pair_copies(gs, ls, send_sems, recv_sems):
    x, y, c = lax.axis_index("x"), lax.axis_index("y"), lax.axis_index("c")
    return [pltpu.make_async_remote_copy(
        src_ref=gs[w].at[2 * j + 1 - c], dst_ref=ls[w].at[j], send_sem=send_sems[4 * w + j],
        recv_sem=recv_sems[4 * w + j], device_id=(x, y, 1 - c), device_id_type=MESH)
        for w in range(len(gs)) for j in range(4)]


def _chip_copies(hs, ls, send_sems, recv_sems):
    x, y, c = lax.axis_index("x"), lax.axis_index("y"), lax.axis_index("c")
    chips = [(1 - x, y), (x, 1 - y), (1 - x, 1 - y)]
    return [pltpu.make_async_remote_copy(
        src_ref=hs[w].at[2 * px + py], dst_ref=ls[w].at[k], send_sem=send_sems[3 * w + k],
        recv_sem=recv_sems[3 * w + k], device_id=(px, py, c), device_id_type=MESH)
        for w in range(len(hs)) for k, (px, py) in enumerate(chips)]


def _rs_pair_share(fins, name):
    n = len(fins)

    def body(*refs):
        fs, outs = refs[:n], refs[n:2 * n]
        send_sems, recv_sems = refs[2 * n:]
        x, y, c = lax.axis_index("x"), lax.axis_index("y"), lax.axis_index("c")
        cps = [pltpu.make_async_remote_copy(
            src_ref=fs[w], dst_ref=outs[w], send_sem=send_sems.at[w], recv_sem=recv_sems.at[w],
            device_id=(x, y, 1 - c), device_id_type=MESH) for w in range(n)]
        for cp in cps:
            cp.start()
        for cp in cps:
            cp.wait_recv()
        for cp in cps:
            cp.wait_send()

    any_spec = pl.BlockSpec(memory_space=pl.ANY)
    return pl.pallas_call(
        body, name=name,
        out_shape=[jax.ShapeDtypeStruct(f.shape, f.dtype) for f in fins],
        in_specs=[any_spec] * n, out_specs=[any_spec] * n,
        scratch_shapes=[pltpu.SemaphoreType.DMA((n,)), pltpu.SemaphoreType.DMA((n,))])(*fins)


def _rs_add_pair(g8, r4, c, name):
    R, n = g8.shape[1:]
    rb = R // 2

    def body(c_ref, g_ref, r_ref, o_ref):
        o_ref[...] = (g_ref[...] + r_ref[...]).astype(BF16)

    spec = pl.BlockSpec((1, rb, n), lambda j, i, c_ref: (j, i, 0))
    return pl.pallas_call(
        body, name=name,
        grid_spec=pltpu.PrefetchScalarGridSpec(
            num_scalar_prefetch=1, grid=(4, R // rb),
            in_specs=[pl.BlockSpec((1, rb, n), lambda j, i, c_ref: (2 * j + c_ref[0], i, 0)), spec],
            out_specs=spec),
        out_shape=jax.ShapeDtypeStruct((4, R, n), BF16), compiler_params=_cp(2))(c, g8, r4)


def _rs_add_chips(g8, r4, r3, cj, name):
    R, n = g8.shape[1:]
    rb = R // 2

    def body(cj_ref, g_ref, p_ref, r_ref, o_ref):
        own = g_ref[0] + p_ref[0]
        o_ref[...] = ((own + r_ref[0].astype(F32)) + r_ref[1].astype(F32)) + r_ref[2].astype(F32)

    return pl.pallas_call(
        body, name=name,
        grid_spec=pltpu.PrefetchScalarGridSpec(
            num_scalar_prefetch=1, grid=(R // rb,),
            in_specs=[pl.BlockSpec((1, rb, n), lambda i, cj_ref: (2 * cj_ref[1] + cj_ref[0], i, 0)),
                      pl.BlockSpec((1, rb, n), lambda i, cj_ref: (cj_ref[1], i, 0)),
                      pl.BlockSpec((3, rb, n), lambda i, cj_ref: (0, i, 0))],
            out_specs=pl.BlockSpec((rb, n), lambda i, cj_ref: (i, 0))),
        out_shape=jax.ShapeDtypeStruct((R, n), F32), compiler_params=_cp(1))(cj, g8, r4, r3)


def _sum8(g):
    n = g.shape[1]

    def body(g_ref, o_ref):
        acc = g_ref[0:1, :]
        for k in range(1, 8):
            acc = acc + g_ref[k:k + 1, :]
        o_ref[...] = acc

    return pl.pallas_call(body, name="sum_devices", out_shape=jax.ShapeDtypeStruct((1, n), F32),
                          compiler_params=pltpu.CompilerParams(vmem_limit_bytes=VMEM_LIMIT))(g)


ADA_ROWS = 64


def _ada_fwd(cs, w, b):
    n = w.shape[1]

    def body(c_ref, w_ref, b_ref, o_ref):
        cv = c_ref[...]
        s = (cv * _sigmoid(cv)).astype(BF16)
        o_ref[...] = _dot(s, w_ref[...].astype(BF16)) + b_ref[...]

    return pl.pallas_call(body, name="ada_fwd", out_shape=jax.ShapeDtypeStruct((ADA_ROWS, n), F32),
                          compiler_params=pltpu.CompilerParams(vmem_limit_bytes=VMEM_LIMIT))(cs, w, b)


def _ada_bwd(cs, w, dm):
    n = w.shape[1]

    def body(c_ref, w_ref, dm_ref, gw_ref, dc_ref):
        cv = c_ref[...]
        s = (cv * _sigmoid(cv)).astype(BF16)
        gw_ref[...] = _dot_tn(s, dm_ref[...].astype(BF16))
        dc_ref[...] = _dot_nt(dm_ref[32:40, :].astype(BF16), w_ref[...].astype(BF16))

    return pl.pallas_call(
        body, name="ada_bwd",
        out_shape=[jax.ShapeDtypeStruct((D, n), F32), jax.ShapeDtypeStruct((8, D), F32)],
        compiler_params=pltpu.CompilerParams(vmem_limit_bytes=VMEM_LIMIT))(cs, w, dm)


def _adamw(w, g, m, v, name):
    r, c = w.shape
    rb = r
    if r % 8 == 0 and r * c * 4 > (1 << 20):
        rb = 8
        for cand in range(8, r, 8):
            if r % cand == 0 and cand * c * 4 <= (1 << 20):
                rb = cand

    def body(w_ref, g_ref, m_ref, v_ref, d_ref, nm_ref, nv_ref):
        gv = g_ref[...]
        mn = ADAM_B1 * m_ref[...] + (1.0 - ADAM_B1) * gv
        vn = ADAM_B2 * v_ref[...] + (1.0 - ADAM_B2) * (gv * gv)
        m_hat = mn / (1.0 - ADAM_B1 ** ADAM_STEP)
        v_hat = vn / (1.0 - ADAM_B2 ** ADAM_STEP)
        d_ref[...] = -ADAM_LR * (m_hat / (jnp.sqrt(v_hat) + ADAM_EPS) + ADAM_WD * w_ref[...])
        nm_ref[...] = mn
        nv_ref[...] = vn

    spec = pl.BlockSpec((rb, c), lambda i: (i, 0))
    shp = jax.ShapeDtypeStruct((r, c), F32)
    return pl.pallas_call(body, name=name, grid=(r // rb,), in_specs=[spec] * 4, out_specs=[spec] * 3,
                          out_shape=[shp] * 3, compiler_params=_cp(1))(w, g, m, v)


BIG = ("ffn1_w_in", "ffn1_w_out", "w_mix_in", "w_mix_out", "ffn2_w_in", "ffn2_w_out")


def _half_rows(w, c):
    half = w.shape[0] // 2
    return lax.dynamic_slice_in_dim(w, c * half, half, axis=0)


def _lower_bounds(logits):
    return jnp.cumsum(jax.nn.softmax(logits.astype(F32), axis=1), axis=1)[:, 0]


def kernel(x, c, ctx, c_ctx, w_ada, b_ada, ln_gain, ln_bias, ffn1_w_in, ffn1_w_out, w_mix_in, gla_a2_fwd, gla_a2_bwd, gla_a_bias_fwd, gla_a_bias_bwd, hgrn_lb_logits, gla_norm_gain, hgrn_norm_gain, w_mix_out, ffn2_w_in, ffn2_w_out, loss_target, m_c_ctx, m_w_ada, m_b_ada, m_ln_gain, m_ln_bias, m_ffn1_w_in, m_ffn1_w_out, m_w_mix_in, m_gla_a2_fwd, m_gla_a2_bwd, m_gla_a_bias_fwd, m_gla_a_bias_bwd, m_hgrn_lb_logits, m_gla_norm_gain, m_hgrn_norm_gain, m_w_mix_out, m_ffn2_w_in, m_ffn2_w_out, v_c_ctx, v_w_ada, v_b_ada, v_ln_gain, v_ln_bias, v_ffn1_w_in, v_ffn1_w_out, v_w_mix_in, v_gla_a2_fwd, v_gla_a2_bwd, v_gla_a_bias_fwd, v_gla_a_bias_bwd, v_hgrn_lb_logits, v_gla_norm_gain, v_hgrn_norm_gain, v_w_mix_out, v_ffn2_w_in, v_ffn2_w_out):
    xi, yi, ci = lax.axis_index("x"), lax.axis_index("y"), lax.axis_index("c")
    chip = 2 * xi + yi
    dev = 2 * chip + ci
    B = x.shape[0]
    weights = dict(ffn1_w_in=ffn1_w_in[0], ffn1_w_out=ffn1_w_out[0], w_mix_in=w_mix_in[0], w_mix_out=w_mix_out[0],
                   ffn2_w_in=ffn2_w_in[0], ffn2_w_out=ffn2_w_out[0])

    mine = jnp.concatenate([c.reshape(-1), ln_gain.reshape(-1), ln_bias.reshape(-1), gla_a2_fwd.reshape(-1),
                            gla_a2_bwd.reshape(-1), hgrn_lb_logits.reshape(-1)])
    g1 = _gather_flat(mine, "gather_cond")
    nc = B * D
    c_all = g1[:, :nc].reshape(8 * B, D)
    per_chip = g1[0::2, nc:]
    o = 0

    def take(shape, axis):
        nonlocal o
        n = int(np.prod(shape))
        parts = per_chip[:, o:o + n].reshape((4,) + shape)
        o += n
        return jnp.concatenate([parts[j] for j in range(4)], axis=axis)

    ln_gain_f = take((3, 256), 1)
    ln_bias_f = take((3, 256), 1)
    a2f_f = take((16, 64), 1)
    a2b_f = take((16, 64), 1)
    lbl_f = take((2, 2, 128), 2)
    lb, lb_vjp = jax.vjp(_lower_bounds, lbl_f)

    cs = jnp.concatenate([c_all, c_ctx.reshape(1, D), jnp.zeros((ADA_ROWS - 8 * B - 1, D), F32)], axis=0)
    ncol = w_ada.shape[2]
    b_cols = lax.dynamic_slice_in_dim(b_ada, chip * ncol, ncol, axis=1)
    m_cols = _ada_fwd(cs, w_ada[0], b_cols)
    g2 = _small_allgather(m_cols, "gather_mod")[0::2]
    m_all = jnp.concatenate([g2[j] for j in range(4)], axis=1)
    m_lat = lax.dynamic_slice_in_dim(m_all, dev * B, B, axis=0).reshape(B, 1, N_MOD, D)
    m_ctx = jnp.broadcast_to(m_all[8 * B].reshape(1, 1, N_MOD, D), (B, 1, N_MOD, D))

    groups = dict(ffn2=("ffn2_w_in", "ffn2_w_out"), mix=("w_mix_in", "w_mix_out"), ffn1=("ffn1_w_in", "ffn1_w_out"))
    shards = dict(weights, w_mix_in=jnp.pad(weights["w_mix_in"], ((0, 0), (0, MIX_NP - MIX_N))))
    blks = {k: _half_rows(shards[k], ci).astype(BF16) for k in BIG}
    gathering = {}
    token = m_all
    for group in ("ffn1", "mix", "ffn2"):
        sems, x_thru, l_thru, token = _split_start(_gather_copies, 4, [blks[k] for k in groups[group]], 8, token,
                                                   "weight_gather_start_" + group)
        gathering[group] = (sems, x_thru, l_thru)
    mvec = jnp.concatenate([m_ctx, m_lat], axis=1) + token[0, 0]

    def weights_for(group, after):
        names = groups[group]
        _, got = _split_wait(_gather_copies, 4, *gathering[group], after, "weight_gather_wait_" + group)
        got = _gather_forward(got, "weight_gather_forward_" + group)
        w_in, w_out = (lax.dynamic_update_index_in_dim(g, blks[k], dev, 0) for k, g in zip(names, got))
        if group == "mix":
            return _mix_in_to_padded(w_in.reshape(4, D, MIX_NP)), w_out.reshape(-1, D)
        return w_in.reshape((4,) + shards[names[0]].shape), w_out.reshape(-1, D)

    cvec = ci.reshape(1).astype(jnp.int32)
    cjvec = jnp.stack([ci, chip]).astype(jnp.int32)
    in_flight = {}

    def on_grads(group, gs):
        names = groups[group]
        if group == "mix":
            gs = (_mix_in_from_padded(gs[0]), gs[1])
        g8s = [g.reshape((8, shards[k].shape[0] // 2, shards[k].shape[1])) for k, g in zip(names, gs)]
        sems, g_thru, l_thru, token = _split_start(_pair_copies, 4, g8s, 4, None, "grad_pair_start_" + group)
        in_flight[group] = (sems, g_thru, l_thru)
        return token

    def on_sent(group, after):
        g8s, r4s = _split_wait(_pair_copies, 4, *in_flight[group], after, "grad_pair_wait_" + group)
        h4s = [_rs_add_pair(g, r, cvec, "grad_pair_add_" + k) for k, g, r in zip(groups[group], g8s, r4s)]
        sems, h_thru, l_thru, token = _split_start(_chip_copies, 3, h4s, 3, None, "grad_chip_start_" + group)
        in_flight[group] = (g8s, r4s, sems, h_thru, l_thru)
        return token

    loss_l, grad_x, dm, small = _local_step(
        x, ctx, loss_target, mvec, weights_for, ln_gain_f, ln_bias_f, a2f_f, a2b_f,
        gla_a_bias_fwd, gla_a_bias_bwd, lb, gla_norm_gain, hgrn_norm_gain, on_grads, on_sent)
    loss = lax.psum(loss_l, ("x", "y", "c"))

    dm_lat = dm[:, 1].reshape(B, N_MOD * D)
    dm_ctx = jnp.sum(dm[:, 0], axis=0).reshape(N_MOD * D)
    keys = ("ln_gain", "ln_bias", "a2f", "a2b", "abf", "abb", "lb", "gng", "gnh")
    flat = jnp.concatenate([dm_lat.reshape(-1), dm_ctx] + [small[k].reshape(-1) for k in keys])
    g3 = _gather_flat(flat, "gather_small_grads")
    nlat = B * N_MOD * D
    dm_all = g3[:, :nlat].reshape(8 * B, N_MOD * D)
    tot = _sum8(g3[:, nlat:])[0]
    dmc_tot = tot[:N_MOD * D]
    o = N_MOD * D
    sg = {}
    for k in keys:
        n = int(np.prod(small[k].shape))
        sg[k] = tot[o:o + n].reshape(small[k].shape)
        o += n
    dm_rows = jnp.concatenate([dm_all, dmc_tot.reshape(1, -1), jnp.zeros((ADA_ROWS - 8 * B - 1, N_MOD * D), F32)], axis=0)
    g_b_ada = (jnp.sum(dm_all, axis=0) + dmc_tot).reshape(1, N_MOD * D)
    g_w_ada, dcc = _ada_bwd(cs, w_ada[0], lax.dynamic_slice_in_dim(dm_rows, chip * ncol, ncol, axis=1))
    g4 = _gather_flat(dcc[0], "gather_cctx")
    dsilu = ((g4[0] + g4[2]) + g4[4]) + g4[6]
    sc = _sigmoid(c_ctx)
    g_c_ctx = dsilu * (sc * (1.0 + c_ctx * (1.0 - sc)))
    (g_lbl,) = lb_vjp(sg["lb"])

    def cols(a, n, axis):
        return lax.dynamic_slice_in_dim(a, chip * n, n, axis=axis)

    fin = {}
    for group in ("ffn2", "mix", "ffn1"):
        g8s, r4s, sems, h_thru, l_thru = in_flight[group]
        _, r3s = _split_wait(_chip_copies, 3, sems, h_thru, l_thru, g_w_ada, "grad_chip_wait_" + group)
        for k, g, r4, r3 in zip(groups[group], g8s, r4s, r3s):
            fin[k] = _rs_add_chips(g, r4, r3, cjvec, "grad_chip_add_" + k)
    fins = [fin[k] for k in BIG]
    gsh = {}
    for k, own, sib in zip(BIG, fins, _rs_pair_share(fins, "grad_pair_share")):
        both = jnp.where(ci == 0, jnp.stack([own, sib]), jnp.stack([sib, own]))
        gsh[k] = both.reshape(shards[k].shape)[:, :weights[k].shape[1]]

    grads = dict(
        c_ctx=g_c_ctx, w_ada=g_w_ada[None], b_ada=g_b_ada, ln_gain=cols(sg["ln_gain"], 256, 1)[None],
        ln_bias=cols(sg["ln_bias"], 256, 1)[None], ffn1_w_in=gsh["ffn1_w_in"][None], ffn1_w_out=gsh["ffn1_w_out"][None],
        w_mix_in=gsh["w_mix_in"][None], gla_a2_fwd=cols(sg["a2f"], 64, 1)[None], gla_a2_bwd=cols(sg["a2b"], 64, 1)[None],
        gla_a_bias_fwd=sg["abf"], gla_a_bias_bwd=sg["abb"], hgrn_lb_logits=cols(g_lbl, 128, 2),
        gla_norm_gain=sg["gng"].reshape(1, HD), hgrn_norm_gain=sg["gnh"].reshape(1, HD),
        w_mix_out=gsh["w_mix_out"][None], ffn2_w_in=gsh["ffn2_w_in"][None], ffn2_w_out=gsh["ffn2_w_out"][None])
    params = dict(
        c_ctx=(c_ctx, m_c_ctx, v_c_ctx), w_ada=(w_ada, m_w_ada, v_w_ada), b_ada=(b_ada, m_b_ada, v_b_ada),
        ln_gain=(ln_gain, m_ln_gain, v_ln_gain), ln_bias=(ln_bias, m_ln_bias, v_ln_bias),
        ffn1_w_in=(ffn1_w_in, m_ffn1_w_in, v_ffn1_w_in), ffn1_w_out=(ffn1_w_out, m_ffn1_w_out, v_ffn1_w_out),
        w_mix_in=(w_mix_in, m_w_mix_in, v_w_mix_in), gla_a2_fwd=(gla_a2_fwd, m_gla_a2_fwd, v_gla_a2_fwd),
        gla_a2_bwd=(gla_a2_bwd, m_gla_a2_bwd, v_gla_a2_bwd),
        gla_a_bias_fwd=(gla_a_bias_fwd, m_gla_a_bias_fwd, v_gla_a_bias_fwd),
        gla_a_bias_bwd=(gla_a_bias_bwd, m_gla_a_bias_bwd, v_gla_a_bias_bwd),
        hgrn_lb_logits=(hgrn_lb_logits, m_hgrn_lb_logits, v_hgrn_lb_logits),
        gla_norm_gain=(gla_norm_gain, m_gla_norm_gain, v_gla_norm_gain),
        hgrn_norm_gain=(hgrn_norm_gain, m_hgrn_norm_gain, v_hgrn_norm_gain),
        w_mix_out=(w_mix_out, m_w_mix_out, v_w_mix_out), ffn2_w_in=(ffn2_w_in, m_ffn2_w_in, v_ffn2_w_in),
        ffn2_w_out=(ffn2_w_out, m_ffn2_w_out, v_ffn2_w_out))
    order = list(params.keys())
    big_names = ("w_ada",) + BIG
    upd = {}
    for k in big_names:
        w_, m_, v_ = params[k]
        s2 = w_.shape[-2:]
        d_, nm_, nv_ = _adamw(w_.reshape(s2), grads[k].reshape(s2), m_.reshape(s2), v_.reshape(s2), "adamw_" + k)
        upd[k] = (d_.reshape(w_.shape), nm_.reshape(w_.shape), nv_.reshape(w_.shape))
    small_names = [k for k in order if k not in big_names]
    sizes = [int(np.prod(params[k][0].shape)) for k in small_names]
    tot_n = sum(sizes)
    npad = -(-tot_n // 1024) * 1024

    def packed(get):
        flat_ = jnp.concatenate([get(k).reshape(-1) for k in small_names])
        return jnp.pad(flat_, (0, npad - tot_n)).reshape(8, npad // 8)

    d_s, nm_s, nv_s = _adamw(packed(lambda k: params[k][0]), packed(lambda k: grads[k]),
                             packed(lambda k: params[k][1]), packed(lambda k: params[k][2]), "adamw_small")
    o = 0
    for k, n in zip(small_names, sizes):
        shp = params[k][0].shape
        upd[k] = tuple(a.reshape(-1)[o:o + n].reshape(shp) for a in (d_s, nm_s, nv_s))
        o += n

    return (loss, grad_x, *[grads[k].reshape(params[k][0].shape) for k in order], *[upd[k][0] for k in order],
            *[upd[k][1] for k in order], *[upd[k][2] for k in order])
```

```python
import functools

import numpy as np
import jax
import jax.numpy as jnp
from jax import lax
from jax.experimental import pallas as pl
from jax.experimental.pallas import tpu as pltpu

F32 = jnp.float32
BF16 = jnp.bfloat16
MESH = pl.DeviceIdType.MESH

D = 1024
DFF = 2816
TM = 256
CH = 32
NCB = TM // CH
SB = 128
CSB = SB // CH
NSB = TM // SB
HP = 4
HD = 128
NH = 8
LN_EPS = 1e-5
NORM_EPS = 1e-6
ALPHA = 2.0 ** 0.25
GATE_NORM = 16.0
GLA_DK = 64
N_MOD = 9
VMEM_LIMIT = 52 * 1024 * 1024

MIXP = 5120
GG, RG, GQ, GK, GV, RQ, RFF, RFB, RI, LR = 0, 512, 1024, 1536, 2048, 2560, 3072, 3584, 4096, 4608
IN_SPLITS = (256, 256, 512, 512, 16, 16, 512, 512, 512, 512, 512)

ADAM_LR, ADAM_B1, ADAM_B2, ADAM_EPS, ADAM_WD, ADAM_STEP = 0.001, 0.9, 0.999, 1e-08, 0.01, 10


def _cp(n_axes):
    return pltpu.CompilerParams(dimension_semantics=("arbitrary",) * n_axes, vmem_limit_bytes=VMEM_LIMIT)


def _rowmap(stride, off):
    return lambda b, i: (b * stride + off + i, 0)


def _mmap(comb):
    if comb:
        return lambda b, i: (b, jnp.minimum(i, 1), 0, 0)
    return lambda b, i: (b, 1, 0, 0)


def _ln(x):
    mu = jnp.mean(x, axis=-1, keepdims=True)
    xc = x - mu
    var = jnp.mean(xc * xc, axis=-1, keepdims=True)
    r = lax.rsqrt(var + LN_EPS)
    return xc * r, r


def _ln_bwd(dxh, xh, r):
    return r * (dxh - jnp.mean(dxh, axis=-1, keepdims=True) - xh * jnp.mean(dxh * xh, axis=-1, keepdims=True))


def _sigmoid(x):
    return 1.0 / (1.0 + jnp.exp(-x))


def _rsum(x):
    return jnp.sum(x, axis=0, keepdims=True)


def _dot(a, b):
    return jnp.dot(a, b, preferred_element_type=F32)


def _dot_nt(a, b):
    return lax.dot_general(a, b, (((1,), (1,)), ((), ())), preferred_element_type=F32)


def _dot_tn(a, b):
    return lax.dot_general(a, b, (((0,), (0,)), ((), ())), preferred_element_type=F32)


def _modulate(xv, m_ref, sub):
    xh, _ = _ln(xv)
    sh = m_ref[0, 0, 3 * sub:3 * sub + 1, :]
    sc = m_ref[0, 0, 3 * sub + 1:3 * sub + 2, :]
    return (xh * (1.0 + sh) + sc).astype(BF16)


def _embed_lnmod(x, ctx, pos, mvec):
    B, T, _ = x.shape
    nt = 1 + T // TM

    def body(x_ref, c_ref, p_ref, m_ref, o_ref, h_ref):
        i = pl.program_id(1)

        @pl.when(i == 0)
        def _():
            o_ref[...] = c_ref[0]

        @pl.when(i > 0)
        def _():
            o_ref[...] = x_ref[0] + p_ref[...]

        h_ref[...] = _modulate(o_ref[...], m_ref, 0)

    rows = pl.BlockSpec((TM, D), lambda b, i: (b * nt + i, 0))
    return pl.pallas_call(
        body, name="embed_lnmod0", grid=(B, nt),
        in_specs=[pl.BlockSpec((1, TM, D), lambda b, i: (b, jnp.maximum(i - 1, 0), 0)),
                  pl.BlockSpec((1, TM, D), lambda b, i: (b, 0, 0)),
                  pl.BlockSpec((TM, D), lambda b, i: (jnp.maximum(i - 1, 0), 0)),
                  pl.BlockSpec((1, 1, N_MOD, D), _mmap(True))],
        out_specs=[rows, rows],
        out_shape=[jax.ShapeDtypeStruct((B * nt * TM, D), F32), jax.ShapeDtypeStruct((B * nt * TM, D), BF16)],
        compiler_params=_cp(2))(x, ctx, pos, mvec)


def _lnmod0_bwd(dh, X, mvec, dres, x_shape, B, nt):
    def body(dh_ref, x_ref, m_ref, dr_ref, dx_ref, dm_ref):
        i = pl.program_id(1)
        xh, r = _ln(x_ref[...])
        sh = m_ref[0, 0, 0:1, :]
        dhv = dh_ref[...]

        @pl.when((i == 0) | (i == 1))
        def _():
            dm_ref[...] = jnp.zeros_like(dm_ref)

        dm_ref[0, 0, 0:1, :] += _rsum(dhv * xh)
        dm_ref[0, 0, 1:2, :] += _rsum(dhv)

        @pl.when(i > 0)
        def _():
            dx_ref[0] = _ln_bwd(dhv * (1.0 + sh), xh, r) + dr_ref[...]

    rows = pl.BlockSpec((TM, D), _rowmap(nt, 0))
    return pl.pallas_call(
        body, name="lnmod0_bwd", grid=(B, nt),
        in_specs=[rows, rows, pl.BlockSpec((1, 1, N_MOD, D), _mmap(True)), rows],
        out_specs=[pl.BlockSpec((1, TM, D), lambda b, i: (b, jnp.maximum(i - 1, 0), 0)),
                   pl.BlockSpec((1, 1, 2, D), _mmap(True))],
        out_shape=[jax.ShapeDtypeStruct(x_shape, F32), jax.ShapeDtypeStruct((B, 2, 2, D), F32)],
        compiler_params=_cp(2))(dh, X, mvec, dres)


def _resid_fwd(x_ref, y_ref, m_ref, gb_ref, sub, w):
    wg = w * m_ref[0, 0, 3 * sub + 2:3 * sub + 3, :]
    y = y_ref[...]
    zh, r = _ln(ALPHA * x_ref[...] + wg * y)
    return y, wg, zh, r


def _resid_grads(do, y, wg, zh, r, w, gb_ref, comb, dx_ref, dy_ref, dg_ref, dgb_ref):
    b_, i = pl.program_id(0), pl.program_id(1)
    dz = _ln_bwd(do * gb_ref[0:1, :], zh, r)
    dx_ref[...] = ALPHA * dz
    dy_ref[...] = (wg * dz).astype(BF16)

    @pl.when((b_ == 0) & (i == 0))
    def _():
        dgb_ref[...] = jnp.zeros_like(dgb_ref)

    dgb_ref[0:1, :] += _rsum(do * zh)
    dgb_ref[1:2, :] += _rsum(do)

    init = (i == 0) | (i == 1) if comb else (i == 0)

    @pl.when(init)
    def _():
        dg_ref[...] = jnp.zeros_like(dg_ref)

    dg_ref[0, 0] += w * _rsum(dz * y)


def _resid_lnmod(X, lx, Y, mvec, comb, sub, w, gb, B, nt, name):
    def body(x_ref, y_ref, m_ref, gb_ref, o_ref, h_ref):
        _, _, zh, _ = _resid_fwd(x_ref, y_ref, m_ref, gb_ref, sub, w)
        xn = zh * gb_ref[0:1, :] + gb_ref[1:2, :]
        o_ref[...] = xn
        h_ref[...] = _modulate(xn, m_ref, sub + 1)

    rows = pl.BlockSpec((TM, D), _rowmap(nt, 0))
    return pl.pallas_call(
        body, name=name, grid=(B, nt),
        in_specs=[pl.BlockSpec((TM, D), _rowmap(*lx)), rows,
                  pl.BlockSpec((1, 1, N_MOD, D), _mmap(comb)), pl.BlockSpec((2, D), lambda b, i: (0, 0))],
        out_specs=[rows, rows],
        out_shape=[jax.ShapeDtypeStruct((B * nt * TM, D), F32), jax.ShapeDtypeStruct((B * nt * TM, D), BF16)],
        compiler_params=_cp(2))(X, Y, mvec, gb)


def _resid_out_shapes(B, nt, comb):
    rows = pl.BlockSpec((TM, D), _rowmap(nt, 0))
    specs = [rows, rows, pl.BlockSpec((1, 1, 1, D), _mmap(comb)), pl.BlockSpec((2, D), lambda b, i: (0, 0))]
    shapes = [jax.ShapeDtypeStruct((B * nt * TM, D), F32), jax.ShapeDtypeStruct((B * nt * TM, D), BF16),
              jax.ShapeDtypeStruct((B, 2, 1, D), F32), jax.ShapeDtypeStruct((2, D), F32)]
    return specs, shapes


def _tail(X, Y, mvec, gb, tgt, sub, w, B, nt):
    def body(x_ref, y_ref, m_ref, gb_ref, t_ref, dx_ref, dy_ref, dg_ref, dgb_ref, l_ref):
        y, wg, zh, r = _resid_fwd(x_ref, y_ref, m_ref, gb_ref, sub, w)
        e = (zh * gb_ref[0:1, :] + gb_ref[1:2, :]) - t_ref[0]

        @pl.when((pl.program_id(0) == 0) & (pl.program_id(1) == 0))
        def _():
            l_ref[...] = jnp.zeros_like(l_ref)

        l_ref[...] += _rsum(e * e)
        _resid_grads(e * (1.0 / D), y, wg, zh, r, w, gb_ref, False, dx_ref, dy_ref, dg_ref, dgb_ref)

    rows = pl.BlockSpec((TM, D), _rowmap(nt, 0))
    specs, shapes = _resid_out_shapes(B, nt, False)
    return pl.pallas_call(
        body, name="resid2_loss_bwd", grid=(B, nt),
        in_specs=[rows, rows, pl.BlockSpec((1, 1, N_MOD, D), _mmap(False)), pl.BlockSpec((2, D), lambda b, i: (0, 0)),
                  pl.BlockSpec((1, TM, D), lambda b, i: (b, i, 0))],
        out_specs=specs + [pl.BlockSpec((1, D), lambda b, i: (0, 0))],
        out_shape=shapes + [jax.ShapeDtypeStruct((1, D), F32)],
        compiler_params=_cp(2))(X, Y, mvec, gb, tgt)


def _lnmod_resid_bwd(dh, Xi, lxi, dres, Xp, lxp, Yp, mvec, comb, sub, w, gb, B, nt, name):
    ntl = nt - 1 if comb else nt

    def body(dh_ref, xi_ref, dr_ref, xp_ref, yp_ref, m_ref, gb_ref, dx_ref, dy_ref, dg_ref, dgb_ref, dm_ref):
        i = pl.program_id(1)
        xh, r = _ln(xi_ref[...])
        sh = m_ref[0, 0, 3 * sub:3 * sub + 1, :]
        dhv = dh_ref[...]
        dr = dr_ref[...]
        if comb:
            dr = jnp.where(i > 0, dr, 0.0)
        do = _ln_bwd(dhv * (1.0 + sh), xh, r) + dr

        init = (i == 0) | (i == 1) if comb else (i == 0)

        @pl.when(init)
        def _():
            dm_ref[...] = jnp.zeros_like(dm_ref)

        dm_ref[0, 0, 0:1, :] += _rsum(dhv * xh)
        dm_ref[0, 0, 1:2, :] += _rsum(dhv)
        y, wg, zh, r2 = _resid_fwd(xp_ref, yp_ref, m_ref, gb_ref, sub - 1, w)
        _resid_grads(do, y, wg, zh, r2, w, gb_ref, comb, dx_ref, dy_ref, dg_ref, dgb_ref)

    rows = pl.BlockSpec((TM, D), _rowmap(nt, 0))
    if comb:
        dres_spec = pl.BlockSpec((TM, D), lambda b, i: (b * ntl + jnp.maximum(i - 1, 0), 0))
    else:
        dres_spec = rows
    specs, shapes = _resid_out_shapes(B, nt, comb)
    return pl.pallas_call(
        body, name=name, grid=(B, nt),
        in_specs=[rows, pl.BlockSpec((TM, D), _rowmap(*lxi)), dres_spec, pl.BlockSpec((TM, D), _rowmap(*lxp)), rows,
                  pl.BlockSpec((1, 1, N_MOD, D), _mmap(comb)), pl.BlockSpec((2, D), lambda b, i: (0, 0))],
        out_specs=specs + [pl.BlockSpec((1, 1, 2, D), _mmap(comb))],
        out_shape=shapes + [jax.ShapeDtypeStruct((B, 2, 2, D), F32)],
        compiler_params=_cp(2))(dh, Xi, dres, Xp, Yp, mvec, gb)


def _ffn_out_dx(dy, W, u, name, dep=None):
    M = dy.shape[0]
    half = DFF // 2
    deps = [] if dep is None else [dep]

    def body(dy_ref, w_ref, u_ref, *rest):
        du_ref = rest[-1]
        dyv = dy_ref[...]
        for j in range(2):
            lo, hi = j * half, (j + 1) * half
            da = _dot_nt(dyv, w_ref[lo:hi, :])
            g = u_ref[:, lo:hi].astype(F32)
            up = u_ref[:, DFF + lo:DFF + hi].astype(F32)
            s = _sigmoid(g)
            du_ref[:, lo:hi] = (da * up * (s * (1.0 + g * (1.0 - s)))).astype(BF16)
            du_ref[:, DFF + lo:DFF + hi] = (da * (g * s)).astype(BF16)

    return pl.pallas_call(
        body, name=name, grid=(M // TM,),
        in_specs=[pl.BlockSpec((TM, D), lambda i: (i, 0)), _wspec(W, 1), pl.BlockSpec((TM, 2 * DFF), lambda i: (i, 0))]
        + [_wspec(d, 1) for d in deps],
        out_specs=pl.BlockSpec((TM, 2 * DFF), lambda i: (i, 0)),
        out_shape=jax.ShapeDtypeStruct((M, 2 * DFF), BF16), compiler_params=_cp(1))(dy, W, u, *deps)


def _wspec(W, nidx):
    zeros = (0,) * W.ndim
    if nidx == 1:
        return pl.BlockSpec(W.shape, lambda i: zeros)
    return pl.BlockSpec(W.shape, lambda b, i: zeros)


def _mm_nn(A, la, W, B, nt, out_dtype, name):
    K, N = W.shape

    def body(a_ref, w_ref, o_ref):
        o_ref[...] = _dot(a_ref[...], w_ref[...]).astype(out_dtype)

    return pl.pallas_call(
        body, name=name, grid=(B, nt),
        in_specs=[pl.BlockSpec((TM, K), _rowmap(*la)), _wspec(W, 2)],
        out_specs=pl.BlockSpec((TM, N), _rowmap(nt, 0)),
        out_shape=jax.ShapeDtypeStruct((B * nt * TM, N), out_dtype), compiler_params=_cp(2))(A, W)


def _ffn_in(A, la, W3, B, nt, name):
    K, n = W3.shape[1:]

    def body(a_ref, w_ref, u_ref, s_ref):
        a = a_ref[...]
        for j in range(2):
            g = _dot(a, w_ref[j])
            up = _dot(a, w_ref[j + 2])
            u_ref[:, j * n:(j + 1) * n] = g.astype(BF16)
            u_ref[:, (j + 2) * n:(j + 3) * n] = up.astype(BF16)
            s_ref[:, j * n:(j + 1) * n] = (g * _sigmoid(g) * up).astype(BF16)

    rows = B * nt * TM
    return pl.pallas_call(
        body, name=name, grid=(B, nt),
        in_specs=[pl.BlockSpec((TM, K), _rowmap(*la)), _wspec(W3, 2)],
        out_specs=[pl.BlockSpec((TM, 4 * n), _rowmap(nt, 0)), pl.BlockSpec((TM, 2 * n), _rowmap(nt, 0))],
        out_shape=[jax.ShapeDtypeStruct((rows, 4 * n), BF16), jax.ShapeDtypeStruct((rows, 2 * n), BF16)],
        compiler_params=_cp(2))(A, W3)


def _mm_nt(A, W, name, dep=None):
    M, N = A.shape
    K = W.shape[-2]

    def body(a_ref, w_ref, *rest):
        o_ref = rest[-1]
        if W.ndim == 3:
            n = W.shape[-1]
            acc = _dot_nt(a_ref[:, 0:n], w_ref[0])
            for j in range(1, 4):
                acc = acc + _dot_nt(a_ref[:, j * n:(j + 1) * n], w_ref[j])
            o_ref[...] = acc
        else:
            o_ref[...] = _dot_nt(a_ref[...], w_ref[...])

    deps = [] if dep is None else [dep]
    return pl.pallas_call(
        body, name=name, grid=(M // TM,),
        in_specs=[pl.BlockSpec((TM, N), lambda i: (i, 0)), _wspec(W, 1)] + [_wspec(d, 1) for d in deps],
        out_specs=pl.BlockSpec((TM, K), lambda i: (i, 0)),
        out_shape=jax.ShapeDtypeStruct((M, K), F32), compiler_params=_cp(1))(A, W, *deps)


def _mm_tn(A, G, name, tk=512, tn=512, shards=None):
    M, K = A.shape
    N = G.shape[1]
    if shards:
        tn = N // shards

    def body(a_ref, g_ref, o_ref):
        @pl.when(pl.program_id(1) == 0)
        def _():
            o_ref[...] = jnp.zeros_like(o_ref)

        upd = _dot_tn(a_ref[...], g_ref[...])
        if shards:
            o_ref[0] += upd
        else:
            o_ref[...] += upd

    if shards:
        out_spec = pl.BlockSpec((1, K, tn), lambda n, k: (n, 0, 0))
        out_shape = jax.ShapeDtypeStruct((shards, K, tn), F32)
    else:
        out_spec = pl.BlockSpec((K, tn), lambda n, k: (0, n))
        out_shape = jax.ShapeDtypeStruct((K, N), F32)
    return pl.pallas_call(
        body, name=name, grid=(N // tn, M // tk),
        in_specs=[pl.BlockSpec((tk, K), lambda n, k: (k, 0)), pl.BlockSpec((tk, tn), lambda n, k: (k, n))],
        out_specs=out_spec, out_shape=out_shape, compiler_params=_cp(2))(A, G)


def _logsig(z):
    return jnp.minimum(z, 0.0) - jnp.log(1.0 + jnp.exp(-jnp.abs(z)))


def _features(F, a2p, biasp, lbp, B):
    nt = F.shape[0] // (B * TM)
    nj = nt + 1
    R2 = nj * TM

    def body(f_ref, a2_ref, bias_ref, lb_ref, q_ref, k_ref, v_ref, g_ref):
        lr = f_ref[:, LR:LR + HD].astype(BF16)
        lane = lax.broadcasted_iota(jnp.int32, (1, 4 * HD), 1)
        keep = (lane & (HD - 1)) < GLA_DK
        for d in range(2):
            z = _dot(lr, a2_ref[d]) + bias_ref[d:d + 1, :]
            gl = jnp.where(keep, _logsig(z) * (1.0 / GATE_NORM), 0.0)
            for h in range(4):
                g_ref[d, 0, h] = gl[:, h * HD:(h + 1) * HD]
        for h in range(4):
            q_ref[0, h] = f_ref[:, GQ + h * HD:GQ + (h + 1) * HD] * (GLA_DK ** -0.5)
            kk = f_ref[:, GK + h * HD:GK + (h + 1) * HD]
            k_ref[0, 0, h] = kk
            k_ref[1, 0, h] = kk
            v_ref[0, h] = f_ref[:, GV + h * HD:GV + (h + 1) * HD].astype(BF16)
        for h in range(4):
            sl = slice(h * HD, (h + 1) * HD)
            rq = f_ref[:, RQ + h * HD:RQ + (h + 1) * HD]
            q_ref[0, 4 + h] = rq * _sigmoid(rq) * (HD ** -0.5)
            v_ref[0, 4 + h] = f_ref[:, RI + h * HD:RI + (h + 1) * HD].astype(BF16)
            for d, off in ((0, RFF), (1, RFB)):
                lb = lb_ref[d:d + 1, sl]
                f = lb + (1.0 - lb) * _sigmoid(f_ref[:, off + h * HD:off + (h + 1) * HD])
                g_ref[d, 0, 4 + h] = jnp.log(f)
                k_ref[d, 0, 4 + h] = 1.0 - f

    fmap = lambda b, j: (b * nt + jnp.where(j == nt, 0, j), 0)
    one = pl.BlockSpec((1, NH, TM, HD), lambda b, j: (b, 0, j, 0))
    two = pl.BlockSpec((2, 1, NH, TM, HD), lambda b, j: (0, b, 0, j, 0))
    s1 = jax.ShapeDtypeStruct((B, NH, R2, HD), F32)
    s2 = jax.ShapeDtypeStruct((2, B, NH, R2, HD), F32)
    return pl.pallas_call(
        body, name="mix_features", grid=(B, nj),
        in_specs=[pl.BlockSpec((TM, MIXP), fmap), pl.BlockSpec((2, HD, 4 * HD), lambda b, j: (0, 0, 0)),
                  pl.BlockSpec((2, 4 * HD), lambda b, j: (0, 0)), pl.BlockSpec((2, 4 * HD), lambda b, j: (0, 0))],
        out_specs=[one, two, one, two], out_shape=[s1, s2, jax.ShapeDtypeStruct(s1.shape, BF16), s2],
        compiler_params=_cp(2))(F, a2p, biasp, lbp)


def _features_bwd(F, a2p, biasp, lbp, dQ0, dQ1, dK0, dK1, dV0, dV1, dG0, dG1, dgates, B):
    nt = F.shape[0] // (B * TM)

    def body(f_ref, a2_ref, bias_ref, lb_ref, dq0, dq1, dk0, dk1, dv0, dv1, dg0, dg1, dgt_ref,
             df_ref, da2_ref, dbias_ref, dlb_ref):
        b_, i = pl.program_id(0), pl.program_id(1)

        @pl.when((b_ == 0) & (i == 0))
        def _():
            da2_ref[...] = jnp.zeros_like(da2_ref)
            dbias_ref[...] = jnp.zeros_like(dbias_ref)
            dlb_ref[...] = jnp.zeros_like(dlb_ref)

        df_ref[:, 0:2 * 4 * HD] = jnp.where(i > 0, dgt_ref[...], 0.0).astype(BF16)
        df_ref[:, LR + HD:] = jnp.zeros((TM, MIXP - LR - HD), BF16)
        lr = f_ref[:, LR:LR + HD].astype(BF16)
        lane = lax.broadcasted_iota(jnp.int32, (1, 4 * HD), 1)
        keep = (lane & (HD - 1)) < GLA_DK
        dlr = jnp.zeros((TM, HD), F32)
        dgs = (dg0, dg1)
        dks = (dk0, dk1)
        rd = lambda ref, h: ref[0, h].astype(F32)
        for d in range(2):
            z = _dot(lr, a2_ref[d]) + bias_ref[d:d + 1, :]
            dgl = jnp.concatenate([rd(dgs[d], h) for h in range(4)], axis=1)
            dz = jnp.where(keep, dgl * (1.0 / GATE_NORM) * (1.0 - _sigmoid(z)), 0.0)
            dzb = dz.astype(BF16)
            dlr = dlr + _dot_nt(dzb, a2_ref[d])
            da2_ref[d] += _dot_tn(lr, dzb)
            dbias_ref[d:d + 1, :] += _rsum(dz)
        df_ref[:, LR:LR + HD] = dlr.astype(BF16)
        for h in range(4):
            df_ref[:, GQ + h * HD:GQ + (h + 1) * HD] = ((rd(dq0, h) + rd(dq1, h)) * (GLA_DK ** -0.5)).astype(BF16)
            df_ref[:, GK + h * HD:GK + (h + 1) * HD] = (rd(dk0, h) + rd(dk1, h)).astype(BF16)
            df_ref[:, GV + h * HD:GV + (h + 1) * HD] = (rd(dv0, h) + rd(dv1, h)).astype(BF16)
        for h in range(4):
            sl = slice(h * HD, (h + 1) * HD)
            rq = f_ref[:, RQ + h * HD:RQ + (h + 1) * HD]
            s = _sigmoid(rq)
            dqh = rd(dq0, 4 + h) + rd(dq1, 4 + h)
            df_ref[:, RQ + h * HD:RQ + (h + 1) * HD] = (dqh * (HD ** -0.5) * (s * (1.0 + rq * (1.0 - s)))).astype(BF16)
            df_ref[:, RI + h * HD:RI + (h + 1) * HD] = (rd(dv0, 4 + h) + rd(dv1, 4 + h)).astype(BF16)
            for d, off in ((0, RFF), (1, RFB)):
                lb = lb_ref[d:d + 1, sl]
                sg = _sigmoid(f_ref[:, off + h * HD:off + (h + 1) * HD])
                f = lb + (1.0 - lb) * sg
                dff = rd(dgs[d], 4 + h) / f - rd(dks[d], 4 + h)
                df_ref[:, off + h * HD:off + (h + 1) * HD] = (dff * (1.0 - lb) * sg * (1.0 - sg)).astype(BF16)
                dlb_ref[d:d + 1, sl] += _rsum(dff * (1.0 - sg))

    m0 = lambda b, i: (b, 0, i, 0)
    m1 = lambda b, i: (b, 0, jnp.where(i == 0, nt, i), 0)
    one = lambda m: pl.BlockSpec((1, NH, TM, HD), m)
    return pl.pallas_call(
        body, name="mix_features_bwd", grid=(B, nt),
        in_specs=[pl.BlockSpec((TM, MIXP), _rowmap(nt, 0)), pl.BlockSpec((2, HD, 4 * HD), lambda b, i: (0, 0, 0)),
                  pl.BlockSpec((2, 4 * HD), lambda b, i: (0, 0)), pl.BlockSpec((2, 4 * HD), lambda b, i: (0, 0)),
                  one(m0), one(m1), one(m0), one(m1), one(m0), one(m1), one(m0), one(m1),
                  pl.BlockSpec((TM, D), lambda b, i: (b * (nt - 1) + jnp.maximum(i - 1, 0), 0))],
        out_specs=[pl.BlockSpec((TM, MIXP), _rowmap(nt, 0)), pl.BlockSpec((2, HD, 4 * HD), lambda b, i: (0, 0, 0)),
                   pl.BlockSpec((2, 4 * HD), lambda b, i: (0, 0)), pl.BlockSpec((2, 4 * HD), lambda b, i: (0, 0))],
        out_shape=[jax.ShapeDtypeStruct((B * nt * TM, MIXP), BF16), jax.ShapeDtypeStruct((2, HD, 4 * HD), F32),
                   jax.ShapeDtypeStruct((2, 4 * HD), F32), jax.ShapeDtypeStruct((2, 4 * HD), F32)],
        compiler_params=_cp(2))(F, a2p, biasp, lbp, dQ0, dQ1, dK0, dK1, dV0, dV1, dG0, dG1, dgates)


def _chunk_scan(x, rin, fwd):
    acc = x
    sft = 1
    while sft < CH:
        if fwd:
            acc = acc + jnp.where(rin >= sft, pltpu.roll(acc, sft, 0), 0.0)
        else:
            acc = acc + jnp.where(rin < CH - sft, pltpu.roll(acc, TM - sft, 0), 0.0)
        sft *= 2
    return acc


def _chunk_total(x):
    t = jnp.sum(x.reshape(NCB, CH, HD), axis=1, keepdims=True)
    return jnp.broadcast_to(t, (NCB, CH, HD)).reshape(TM, HD)


def _scan_masks(rev):
    rin = lax.broadcasted_iota(jnp.int32, (TM, HD), 0) & (CH - 1)
    ri = lax.broadcasted_iota(jnp.int32, (SB, SB), 0)
    ci = lax.broadcasted_iota(jnp.int32, (SB, SB), 1)
    same = (ri >> 5) == (ci >> 5)
    lo = same & (ri >= ci)
    up = same & (ri <= ci)
    mask, maskT = (up, lo) if rev else (lo, up)
    re = lax.broadcasted_iota(jnp.int32, (SB, CSB * HD), 0) >> 5
    ce = lax.broadcasted_iota(jnp.int32, (SB, CSB * HD), 1) >> 7
    return rin, mask, maskT, re == ce


def _scan_decay(q, k, g, rin, rev):
    b = _chunk_scan(g, rin, not rev)
    xx = _chunk_total(g) - b
    eb = jnp.exp(b)
    return b, xx, eb, q * eb, k * jnp.exp(-b), k * jnp.exp(xx)


def _sub(x, s):
    return x[s * SB:(s + 1) * SB]


def _expand(xb, mexp):
    return jnp.where(mexp, jnp.concatenate([xb] * CSB, axis=1), jnp.zeros((), xb.dtype))


def _own(x, mexp):
    xm = jnp.where(mexp, x, 0.0)
    acc = xm[:, 0:HD]
    for n in range(1, CSB):
        acc = acc + xm[:, n * HD:(n + 1) * HD]
    return acc


def _stack(per_chunk, s):
    return jnp.concatenate(per_chunk[s * CSB:(s + 1) * CSB], axis=1)


def _state_pass(s0, eb, uts, rev):
    order = range(NCB - 1, -1, -1) if rev else range(NCB)
    states = [None] * NCB
    s = s0
    for n in order:
        row = n * CH if rev else n * CH + CH - 1
        states[n] = s
        s = eb[row:row + 1, :] * s + uts[n // CSB][:, (n % CSB) * HD:(n % CSB + 1) * HD]
    return states, s


def _scan_fwd(Q, K, V, G, rev, B):
    nb = Q.shape[2] // TM - 1
    d = 1 if rev else 0
    rmap = (lambda s: nb - s) if rev else (lambda s: s)

    def body(q_ref, k_ref, v_ref, g_ref, o_ref, st_ref, s_scr):
        @pl.when(pl.program_id(2) == 0)
        def _():
            s_scr[...] = jnp.zeros_like(s_scr)

        rin, mask, _, mexp = _scan_masks(rev)
        for p in range(HP):
            s0 = s_scr[p]
            st_ref[0, p, 0] = s0
            _, _, eb, qd, ki, kt = _scan_decay(q_ref[0, p], k_ref[0, 0, p], g_ref[0, 0, p], rin, rev)
            qb, kib, ktb, vb = qd.astype(BF16), ki.astype(BF16), kt.astype(BF16), v_ref[0, p]
            uts = [_dot_tn(_sub(vb, s), _expand(_sub(ktb, s), mexp)) for s in range(NSB)]
            states, s_new = _state_pass(s0, eb, uts, rev)
            s_scr[p] = s_new
            for s in range(NSB):
                a = jnp.where(mask, _dot_nt(_sub(qb, s), _sub(kib, s)), 0.0)
                o_ref[0, p, s * SB:(s + 1) * SB, :] = (
                    _dot(a.astype(BF16), _sub(vb, s))
                    + _dot_nt(_expand(_sub(qb, s), mexp), _stack(states, s).astype(BF16)))

    one = pl.BlockSpec((1, HP, TM, HD), lambda b, h, s: (b, h, rmap(s), 0))
    two = pl.BlockSpec((1, 1, HP, TM, HD), lambda b, h, s: (d, b, h, rmap(s), 0))
    return pl.pallas_call(
        body, name="scan_fwd_rev" if rev else "scan_fwd", grid=(B, NH // HP, nb),
        in_specs=[one, two, one, two],
        out_specs=[one, pl.BlockSpec((1, HP, 1, HD, HD), lambda b, h, s: (b, h, s, 0, 0))],
        out_shape=[jax.ShapeDtypeStruct(Q.shape, F32), jax.ShapeDtypeStruct((B, NH, nb, HD, HD), F32)],
        scratch_shapes=[pltpu.VMEM((HP, HD, HD), F32)],
        compiler_params=_cp(3))(Q, K, V, G)


def _scan_bwd(Q, K, V, G, St, dO, rev, B):
    nb = Q.shape[2] // TM - 1
    d = 1 if rev else 0
    smap = lambda t: nb - 1 - t
    rmap = (lambda t: nb - smap(t)) if rev else smap

    def body(q_ref, k_ref, v_ref, g_ref, st_ref, do_ref, dq_ref, dk_ref, dv_ref, dg_ref, ds_scr):
        t = pl.program_id(2)

        @pl.when(t == 0)
        def _():
            ds_scr[...] = jnp.zeros_like(ds_scr)

        is_lat = smap(t) >= 1
        rin, mask, maskT, mexp = _scan_masks(rev)
        for p in range(HP):
            b, xx, eb, qd, ki, kt = _scan_decay(q_ref[0, p], k_ref[0, 0, p], g_ref[0, 0, p], rin, rev)
            qb, kib, ktb, vb = qd.astype(BF16), ki.astype(BF16), kt.astype(BF16), v_ref[0, p]
            dob = jnp.where(is_lat, do_ref[0, p], 0.0).astype(BF16)
            kt_exps = [_expand(_sub(ktb, s), mexp) for s in range(NSB)]
            uts = [_dot_tn(_sub(vb, s), kt_exps[s]) for s in range(NSB)]
            states, _ = _state_pass(st_ref[0, p, 0], eb, uts, rev)
            gts = [_dot_tn(_sub(dob, s), _expand(_sub(qb, s), mexp)) for s in range(NSB)]
            order = range(NCB) if rev else range(NCB - 1, -1, -1)
            dsp = [None] * NCB
            t2 = [None] * NCB
            dsc = ds_scr[p]
            for n in order:
                row = n * CH if rev else n * CH + CH - 1
                ebl = eb[row:row + 1, :]
                dsp[n] = dsc
                t2[n] = jnp.broadcast_to(ebl * _rsum(states[n] * dsc), (CH, HD))
                dsc = gts[n // CSB][:, (n % CSB) * HD:(n % CSB + 1) * HD] + ebl * dsc
            ds_scr[p] = dsc
            dqds, dkis, dkts = [], [], []
            for s in range(NSB):
                q_s, ki_s, v_s, do_s = _sub(qb, s), _sub(kib, s), _sub(vb, s), _sub(dob, s)
                dspb = _stack(dsp, s).astype(BF16)
                da = jnp.where(mask, _dot_nt(do_s, v_s), 0.0).astype(BF16)
                dat = jnp.where(maskT, _dot_nt(v_s, do_s), 0.0).astype(BF16)
                at = jnp.where(maskT, _dot_nt(ki_s, q_s), 0.0).astype(BF16)
                dqds.append(_dot(da, ki_s) + _own(_dot(do_s, _stack(states, s).astype(BF16)), mexp))
                dkis.append(_dot(dat, q_s))
                dv_ref[0, p, s * SB:(s + 1) * SB, :] = (_dot(at, do_s) + _dot_nt(kt_exps[s], dspb)).astype(BF16)
                dkts.append(_own(_dot(v_s, dspb), mexp))
            dqd, dki, dkt = (jnp.concatenate(parts, axis=0) for parts in (dqds, dkis, dkts))
            z = dkt * kt
            db = dqd * qd - dki * ki
            dq_ref[0, p] = (dqd * eb).astype(BF16)
            dk_ref[0, p] = (dki * jnp.exp(-b) + dkt * jnp.exp(xx)).astype(BF16)
            dg_ref[0, p] = (_chunk_total(db) + (db - z) + _chunk_scan(z - db, rin, not rev)
                            + jnp.concatenate(t2, axis=0)).astype(BF16)

    one = pl.BlockSpec((1, HP, TM, HD), lambda b, h, t: (b, h, rmap(t), 0))
    two = pl.BlockSpec((1, 1, HP, TM, HD), lambda b, h, t: (d, b, h, rmap(t), 0))
    lat = pl.BlockSpec((1, HP, TM, HD), lambda b, h, t: (b, h, jnp.clip(rmap(t) - 1, 0, nb - 2), 0))
    shp = jax.ShapeDtypeStruct(Q.shape, BF16)
    return pl.pallas_call(
        body, name="scan_bwd_rev" if rev else "scan_bwd", grid=(B, NH // HP, nb),
        in_specs=[one, two, one, two, pl.BlockSpec((1, HP, 1, HD, HD), lambda b, h, t: (b, h, smap(t), 0, 0)), lat],
        out_specs=[one, one, one, one], out_shape=[shp, shp, shp, shp],
        scratch_shapes=[pltpu.VMEM((HP, HD, HD), F32)],
        compiler_params=_cp(3))(Q, K, V, G, St, dO)


def _gnorm(O0, O1, F, gains, B, ntl):
    nt = ntl + 1

    def body(o0_ref, o1_ref, f_ref, gn_ref, m_ref):
        for h in range(NH):
            o = o0_ref[0, h] + o1_ref[0, h]
            r = lax.rsqrt(jnp.mean(o * o, axis=-1, keepdims=True) + NORM_EPS)
            gn = gn_ref[0:1, :] if h < 4 else gn_ref[1:2, :]
            gt = f_ref[:, h * HD:(h + 1) * HD]
            m_ref[:, h * HD:(h + 1) * HD] = (o * r * gn * (gt * _sigmoid(gt))).astype(BF16)

    ospec = pl.BlockSpec((1, NH, TM, HD), lambda b, i: (b, 0, i + 1, 0))
    return pl.pallas_call(
        body, name="gated_norm", grid=(B, ntl),
        in_specs=[ospec, ospec, pl.BlockSpec((TM, D), lambda b, i: (b * nt + 1 + i, 0)),
                  pl.BlockSpec((2, HD), lambda b, i: (0, 0))],
        out_specs=pl.BlockSpec((TM, D), _rowmap(ntl, 0)),
        out_shape=jax.ShapeDtypeStruct((B * ntl * TM, D), BF16), compiler_params=_cp(2))(O0, O1, F, gains)


def _gnorm_bwd(dM, O0, O1, F, gains, B, ntl):
    nt = ntl + 1

    def body(dm_ref, o0_ref, o1_ref, f_ref, gn_ref, do_ref, dgt_ref, dgn_ref):
        b_, i = pl.program_id(0), pl.program_id(1)

        @pl.when((b_ == 0) & (i == 0))
        def _():
            dgn_ref[...] = jnp.zeros_like(dgn_ref)

        for h in range(NH):
            o = o0_ref[0, h] + o1_ref[0, h]
            r = lax.rsqrt(jnp.mean(o * o, axis=-1, keepdims=True) + NORM_EPS)
            y = o * r
            gn = gn_ref[0:1, :] if h < 4 else gn_ref[1:2, :]
            gt = f_ref[:, h * HD:(h + 1) * HD]
            s = _sigmoid(gt)
            dm = dm_ref[:, h * HD:(h + 1) * HD]
            don = dm * (gt * s)
            dgt_ref[:, h * HD:(h + 1) * HD] = dm * (y * gn) * (s * (1.0 + gt * (1.0 - s)))
            row = 0 if h < 4 else 1
            dgn_ref[row:row + 1, :] += _rsum(don * y)
            dy = don * gn
            do_ref[0, h] = r * (dy - y * jnp.mean(dy * y, axis=-1, keepdims=True))

    ospec = pl.BlockSpec((1, NH, TM, HD), lambda b, i: (b, 0, i + 1, 0))
    return pl.pallas_call(
        body, name="gated_norm_bwd", grid=(B, ntl),
        in_specs=[pl.BlockSpec((TM, D), _rowmap(ntl, 0)), ospec, ospec,
                  pl.BlockSpec((TM, D), lambda b, i: (b * nt + 1 + i, 0)), pl.BlockSpec((2, HD), lambda b, i: (0, 0))],
        out_specs=[pl.BlockSpec((1, NH, TM, HD), lambda b, i: (b, 0, i, 0)), pl.BlockSpec((TM, D), _rowmap(ntl, 0)),
                   pl.BlockSpec((2, HD), lambda b, i: (0, 0))],
        out_shape=[jax.ShapeDtypeStruct((B, NH, ntl * TM, HD), F32), jax.ShapeDtypeStruct((B * ntl * TM, D), F32),
                   jax.ShapeDtypeStruct((2, HD), F32)],
        compiler_params=_cp(2))(dM, O0, O1, F, gains)


def _sincos_2d(rows, width, dim):
    quarter = dim // 4
    omega = 1.0 / 10000.0 ** (jnp.arange(quarter, dtype=F32) / quarter)

    def emb(n):
        a = jnp.arange(n).astype(F32)[:, None] * omega[None, :]
        return jnp.concatenate([jnp.sin(a), jnp.cos(a)], axis=-1)

    er = jnp.broadcast_to(emb(rows)[:, None, :], (rows, width, dim // 2))
    ec = jnp.broadcast_to(emb(width)[None, :, :], (rows, width, dim // 2))
    return jnp.concatenate([er, ec], axis=-1).reshape(rows * width, dim)


def _pad_heads(w):
    k = w.shape[0]
    return jnp.pad(w.reshape(k, 4, GLA_DK), ((0, 0), (0, 0), (0, HD - GLA_DK))).reshape(k, 4 * HD)


def _unpad_heads(w):
    k = w.shape[0]
    return w.reshape(k, 4, HD)[:, :, :GLA_DK].reshape(k, 4 * GLA_DK)


MIX_N = 1032
MIX_NP = 1152
_SEGS = ([(64 * h, 64, GQ + HD * h) for h in range(4)] + [(256 + 64 * h, 64, GK + HD * h) for h in range(4)]
         + [(512, 512, GV), (1024, 512, GG), (1536, 32, LR), (1568, 512, RQ), (2080, 512, RFF), (2592, 512, RFB),
            (3104, 512, RI), (3616, 512, RG)])


def _mix_in_to_padded(ps):
    k = ps.shape[1]
    parts, pos = [], 0
    for g0, ln, s0 in sorted(_SEGS, key=lambda s: s[2]):
        if s0 > pos:
            parts.append(jnp.zeros((k, s0 - pos), ps.dtype))
        for j in range(4):
            lo, hi = max(g0, j * MIX_N), min(g0 + ln, (j + 1) * MIX_N)
            if lo < hi:
                parts.append(ps[j][:, lo - j * MIX_N:hi - j * MIX_N])
        pos = s0 + ln
    parts.append(jnp.zeros((k, MIXP - pos), ps.dtype))
    return jnp.concatenate(parts, axis=1)


def _mix_in_from_padded(g):
    k = g.shape[0]
    shards = []
    for j in range(4):
        parts = []
        for g0, ln, s0 in sorted(_SEGS):
            lo, hi = max(g0, j * MIX_N), min(g0 + ln, (j + 1) * MIX_N)
            if lo < hi:
                parts.append(g[:, s0 + lo - g0:s0 + hi - g0])
        parts.append(jnp.zeros((k, MIX_NP - MIX_N), g.dtype))
        shards.append(jnp.concatenate(parts, axis=1))
    return jnp.stack(shards)


def _local_step(x, ctx, tgt, mvec, weights_for, ln_gain, ln_bias, a2f, a2b, abf, abb, lb, gng, gnh, on_grads, on_sent):
    B, T, _ = x.shape
    assert ctx.shape[1] == TM and T % TM == 0
    ntl = T // TM
    nt = ntl + 1
    C, L, CL = (nt, 0), (ntl, 0), (nt, 1)
    pos = _sincos_2d(T // 64, 64, D)
    gbs = [jnp.stack([ln_gain[i], ln_bias[i]]) for i in range(3)]
    a2p = jnp.zeros((2, HD, 4 * HD), F32)
    a2p = a2p.at[0, 0:16].set(_pad_heads(a2f)).at[1, 16:32].set(_pad_heads(a2b)).astype(BF16)
    biasp = jnp.concatenate([_pad_heads(abf.reshape(1, -1)), _pad_heads(abb.reshape(1, -1))], axis=0)
    gains = jnp.concatenate([gng.reshape(1, HD), gnh.reshape(1, HD)], axis=0)

    X0, h0 = _embed_lnmod(x, ctx, pos, mvec)
    w1i, w1o = weights_for("ffn1", h0)
    u0, a0 = _ffn_in(h0, C, w1i, B, nt, "ffn1_in")
    wmp, wmo = weights_for("mix", a0)
    y0 = _mm_nn(a0, C, w1o, B, nt, F32, "ffn1_out")
    X1, h1 = _resid_lnmod(X0, C, y0, mvec, True, 0, 0.5, gbs[0], B, nt, "resid0_lnmod1")
    Fm = _mm_nn(h1, C, wmp, B, nt, F32, "mix_in")
    w2i, w2o = weights_for("ffn2", Fm)
    Q, K, V, G = _features(Fm, a2p, biasp, lb, B)
    O0, S0 = _scan_fwd(Q, K, V, G, False, B)
    O1, S1 = _scan_fwd(Q, K, V, G, True, B)
    merged = _gnorm(O0, O1, Fm, gains, B, ntl)
    y1 = _mm_nn(merged, L, wmo, B, ntl, F32, "mix_out")
    X2, h2 = _resid_lnmod(X1, CL, y1, mvec, False, 1, 1.0, gbs[1], B, ntl, "resid1_lnmod2")
    u2, a2 = _ffn_in(h2, L, w2i, B, ntl, "ffn2_in")
    y2 = _mm_nn(a2, L, w2o, B, ntl, F32, "ffn2_out")

    dx2r, dy2, dgate2, dgb2, lsum = _tail(X2, y2, mvec, gbs[2], tgt, 2, 0.5, B, ntl)
    loss = (0.5 / D) * jnp.sum(lsum)
    du2 = _ffn_out_dx(dy2, w2o, u2, "ffn2_out_dx")
    g_w2o = _mm_tn(a2, dy2, "ffn2_out_dw")
    dh2 = _mm_nt(du2, w2i, "ffn2_in_dx")
    g_w2i = _mm_tn(h2, du2, "ffn2_in_dw", shards=4)
    tok = on_grads("ffn2", (g_w2i, g_w2o))
    dx1r, dy1, dgate1, dgb1, dss2 = _lnmod_resid_bwd(dh2, X2, L, dx2r, X1, CL, y1, mvec, False, 2, 1.0,
                                                     gbs[1] + tok[0, 0], B, ntl, "lnmod2_resid1_bwd")
    tok = on_sent("ffn2", dy1)
    dmerged = _mm_nt(dy1, wmo, "mix_out_dx", dep=tok)
    g_wmo = _mm_tn(merged, dy1, "mix_out_dw")
    dO, dgates, dgains = _gnorm_bwd(dmerged, O0, O1, Fm, gains, B, ntl)
    dQ0, dK0, dV0, dG0 = _scan_bwd(Q, K, V, G, S0, dO, False, B)
    dQ1, dK1, dV1, dG1 = _scan_bwd(Q, K, V, G, S1, dO, True, B)
    dF, da2p, dbiasp, dlb = _features_bwd(Fm, a2p, biasp, lb, dQ0, dQ1, dK0, dK1, dV0, dV1, dG0, dG1, dgates, B)
    dh1 = _mm_nt(dF, wmp, "mix_in_dx")
    g_wmp = _mm_tn(h1, dF, "mix_in_dw", tn=MIXP // 4)
    tok = on_grads("mix", (g_wmp, g_wmo))
    dx0r, dy0, dgate0, dgb0, dss1 = _lnmod_resid_bwd(dh1, X1, C, dx1r, X0, C, y0, mvec, True, 1, 0.5,
                                                     gbs[0] + tok[0, 0], B, nt, "lnmod1_resid0_bwd")
    tok = on_sent("mix", dy0)
    du0 = _ffn_out_dx(dy0, w1o, u0, "ffn1_out_dx", dep=tok)
    g_w1o = _mm_tn(a0, dy0, "ffn1_out_dw")
    g_w1i = _mm_tn(h0, du0, "ffn1_in_dw", shards=4)
    tok = on_grads("ffn1", (g_w1i, g_w1o))
    dh0 = _mm_nt(du0, w1i, "ffn1_in_dx", dep=tok)
    tok = on_sent("ffn1", dh0)
    grad_x, dss0 = _lnmod0_bwd(dh0, X0, mvec + tok[0, 0], dx0r, x.shape, B, nt)

    zero_ctx = lambda a: a.at[:, 0].set(0.0)
    dm = jnp.concatenate([dss0, dgate0, dss1, zero_ctx(dgate1), zero_ctx(dss2), zero_ctx(dgate2)], axis=2)
    small = dict(
        ln_gain=jnp.stack([dgb0[0], dgb1[0], dgb2[0]]), ln_bias=jnp.stack([dgb0[1], dgb1[1], dgb2[1]]),
        a2f=_unpad_heads(da2p[0, 0:16]), a2b=_unpad_heads(da2p[1, 16:32]),
        abf=_unpad_heads(dbiasp[0:1]), abb=_unpad_heads(dbiasp[1:2]), lb=dlb, gng=dgains[0], gnh=dgains[1])
    return loss, grad_x, dm, small


def _small_allgather(xs, name):
    r, n = xs.shape

    def body(x_ref, out_ref, send_sems, recv_sems, local_sem):
        x, y, c = lax.axis_index("x"), lax.axis_index("y"), lax.axis_index("c")
        me, sibling = (x, y, c), (x, y, 1 - c)
        chips = [(1 - x, y), (x, 1 - y), (1 - x, 1 - y)]

        def rows(px, py, pc):
            return out_ref.at[pl.ds((4 * px + 2 * py + pc) * r, r), :]

        def copy(k, block, to, src=None):
            return pltpu.make_async_remote_copy(
                src_ref=rows(*block) if src is None else src, dst_ref=rows(*block),
                send_sem=send_sems.at[k], recv_sem=recv_sems.at[k], device_id=to, device_id_type=MESH)

        mine = pltpu.make_async_copy(x_ref, rows(*me), local_sem)
        mine.start()
        first = [copy(0, me, sibling, src=x_ref)]
        first += [copy(1 + j, me, (*chip, c), src=x_ref) for j, chip in enumerate(chips)]
        for cp in first:
            cp.start()
        passed = [copy(4 + j, (*chip, c), sibling) for j, chip in enumerate(chips)]
        for j, chip in enumerate(chips):
            copy(1 + j, (*chip, c), me).wait_recv()
            passed[j].start()
        copy(0, sibling, me).wait_recv()
        for j, chip in enumerate(chips):
            copy(4 + j, (*chip, 1 - c), me).wait_recv()
        for cp in first + passed:
            cp.wait_send()
        mine.wait()

    out = pl.pallas_call(
        body, name=name,
        out_shape=jax.ShapeDtypeStruct((8 * r, n), xs.dtype),
        in_specs=[pl.BlockSpec(memory_space=pltpu.VMEM)],
        out_specs=pl.BlockSpec(memory_space=pltpu.VMEM),
        scratch_shapes=[pltpu.SemaphoreType.DMA((7,)), pltpu.SemaphoreType.DMA((7,)), pltpu.SemaphoreType.DMA],
        compiler_params=pltpu.CompilerParams(vmem_limit_bytes=VMEM_LIMIT))(xs)
    return out.reshape(8, r, n)


def _gather_flat(v, name):
    n = v.shape[0]
    npad = -(-n // 1024) * 1024
    g = _small_allgather(jnp.pad(v, (0, npad - n)).reshape(8, npad // 8), name)
    return g.reshape(8, npad)[:, :n]


HBM_SPEC = pl.BlockSpec(memory_space=pltpu.HBM)
SEM_SPEC = pl.BlockSpec(memory_space=pltpu.SEMAPHORE)
DATAFLOW = pltpu.SideEffectType.DATAFLOW_SIDE_EFFECTING


def _gather_copies(xs, outs, send_sems, recv_sems):
    x, y, c = lax.axis_index("x"), lax.axis_index("y"), lax.axis_index("c")
    dests = [(x, y, 1 - c), (1 - x, y, c), (x, 1 - y, c), (1 - x, 1 - y, c)]
    return [pltpu.make_async_remote_copy(
        src_ref=xs[w], dst_ref=outs[w].at[4 * x + 2 * y + c], send_sem=send_sems[4 * w + k],
        recv_sem=recv_sems[4 * w + k], device_id=dests[k], device_id_type=MESH)
        for w in range(len(xs)) for k in range(4)]


def _split_start(copies, per_w, srcs, land_lead, after, name):
    n = len(srcs)
    m = per_w * n
    lands = [lax.empty((land_lead,) + s.shape[-2:], s.dtype) for s in srcs]
    deps = [] if after is None else [after]

    def body(*refs):
        xs, ls, outs = refs[:n], refs[n:2 * n], refs[2 * n + len(deps):]
        for cp in copies(xs, ls, outs[:m], outs[m:2 * m]):
            cp.start()
        token = outs[2 * m + 2 * n]
        token[...] = jnp.zeros_like(token)

    outs = pl.pallas_call(
        body, name=name,
        out_shape=([pltpu.SemaphoreType.DMA(())] * (2 * m) + [pltpu.HBM(a.shape, a.dtype) for a in srcs + lands]
                   + [jax.ShapeDtypeStruct((8, 128), F32)]),
        in_specs=[HBM_SPEC] * (2 * n) + [pl.BlockSpec(memory_space=pl.ANY)] * len(deps),
        out_specs=[SEM_SPEC] * (2 * m) + [HBM_SPEC] * (2 * n) + [pl.BlockSpec(memory_space=pltpu.VMEM)],
        input_output_aliases={w: 2 * m + w for w in range(2 * n)},
        compiler_params=pltpu.CompilerParams(has_side_effects=DATAFLOW),
    )(*[pltpu.with_memory_space_constraint(a, pltpu.HBM) for a in srcs + lands], *deps)
    return outs[:2 * m], outs[2 * m:2 * m + n], outs[2 * m + n:2 * m + 2 * n], outs[2 * m + 2 * n]


def _split_wait(copies, per_w, sems, x_thru, l_thru, after, name):
    n = len(x_thru)
    m = per_w * n

    def body(*refs):
        xs, ls, ss = refs[:n], refs[n:2 * n], refs[2 * n:2 * n + 2 * m]
        for cp in copies(xs, ls, ss[:m], ss[m:]):
            cp.wait_send()
            cp.wait_recv()

    outs = pl.pallas_call(
        body, name=name,
        out_shape=[pltpu.HBM(a.shape, a.dtype) for a in list(x_thru) + list(l_thru)],
        in_specs=[HBM_SPEC] * (2 * n) + [SEM_SPEC] * (2 * m) + [pl.BlockSpec(memory_space=pl.ANY)],
        out_specs=[HBM_SPEC] * (2 * n),
        input_output_aliases={w: w for w in range(2 * n)},
        compiler_params=pltpu.CompilerParams(has_side_effects=DATAFLOW),
    )(*x_thru, *l_thru, *sems, after)
    return outs[:n], outs[n:]


def _gather_forward(gathered, name):
    n = len(gathered)

    def body(*refs):
        outs = refs[n:2 * n]
        send_sems, recv_sems = refs[2 * n:]
        x, y, c = lax.axis_index("x"), lax.axis_index("y"), lax.axis_index("c")
        chips = [(1 - x, y), (x, 1 - y), (1 - x, 1 - y)]

        def copy(w, j, pc):
            px, py = chips[j]
            slot = outs[w].at[4 * px + 2 * py + pc]
            return pltpu.make_async_remote_copy(
                src_ref=slot, dst_ref=slot, send_sem=send_sems.at[3 * w + j], recv_sem=recv_sems.at[3 * w + j],
                device_id=(x, y, 1 - c), device_id_type=MESH)

        sends = [copy(w, j, c) for w in range(n) for j in range(3)]
        for cp in sends:
            cp.start()
        for w in range(n):
            for j in range(3):
                copy(w, j, 1 - c).wait_recv()
        for cp in sends:
            cp.wait_send()

    any_spec = pl.BlockSpec(memory_space=pl.ANY)
    return pl.pallas_call(
        body, name=name,
        out_shape=[jax.ShapeDtypeStruct(g.shape, g.dtype) for g in gathered],
        in_specs=[any_spec] * n, out_specs=[any_spec] * n,
        input_output_aliases={w: w for w in range(n)},
        scratch_shapes=[pltpu.SemaphoreType.DMA((3 * n,)), pltpu.SemaphoreType.DMA((3 * n,))],
    )(*gathered)


def _pair_copies(gs, ls, send_sems, recv_sems):
    x, y, c = lax.axis_index("x"), lax.axis_index("y"), lax.axis_index("c")
    return [pltpu.make_async_remote_copy(
        src_ref=gs[w].at[2 * j + 1 - c], dst_ref=ls[w].at[j], send_sem=send_sems[4 * w + j],
        recv_sem=recv_sems[4 * w + j], device_id=(x, y, 1 - c), device_id_type=MESH)
        for w in range(len(gs)) for j in range(4)]


def _chip_copies(hs, ls, send_sems, recv_sems):
    x, y, c = lax.axis_index("x"), lax.axis_index("y"), lax.axis_index("c")
    chips = [(1 - x, y), (x, 1 - y), (1 - x, 1 - y)]
    return [pltpu.make_async_remote_copy(
        src_ref=hs[w].at[2 * px + py], dst_ref=ls[w].at[k], send_sem=send_sems[3 * w + k],
        recv_sem=recv_sems[3 * w + k], device_id=(px, py, c), device_id_type=MESH)
        for w in range(len(hs)) for k, (px, py) in enumerate(chips)]


def _rs_pair_share(fins, name):
    n = len(fins)

    def body(*refs):
        fs, outs = refs[:n], refs[n:2 * n]
        send_sems, recv_sems = refs[2 * n:]
        x, y, c = lax.axis_index("x"), lax.axis_index("y"), lax.axis_index("c")
        cps = [pltpu.make_async_remote_copy(
            src_ref=fs[w], dst_ref=outs[w], send_sem=send_sems.at[w], recv_sem=recv_sems.at[w],
            device_id=(x, y, 1 - c), device_id_type=MESH) for w in range(n)]
        for cp in cps:
            cp.start()
        for cp in cps:
            cp.wait_recv()
        for cp in cps:
            cp.wait_send()

    any_spec = pl.BlockSpec(memory_space=pl.ANY)
    return pl.pallas_call(
        body, name=name,
        out_shape=[jax.ShapeDtypeStruct(f.shape, f.dtype) for f in fins],
        in_specs=[any_spec] * n, out_specs=[any_spec] * n,
        scratch_shapes=[pltpu.SemaphoreType.DMA((n,)), pltpu.SemaphoreType.DMA((n,))])(*fins)


def _rs_add_pair(g8, r4, c, name):
    R, n = g8.shape[1:]
    rb = R // 2

    def body(c_ref, g_ref, r_ref, o_ref):
        o_ref[...] = (g_ref[...] + r_ref[...]).astype(BF16)

    spec = pl.BlockSpec((1, rb, n), lambda j, i, c_ref: (j, i, 0))
    return pl.pallas_call(
        body, name=name,
        grid_spec=pltpu.PrefetchScalarGridSpec(
            num_scalar_prefetch=1, grid=(4, R // rb),
            in_specs=[pl.BlockSpec((1, rb, n), lambda j, i, c_ref: (2 * j + c_ref[0], i, 0)), spec],
            out_specs=spec),
        out_shape=jax.ShapeDtypeStruct((4, R, n), BF16), compiler_params=_cp(2))(c, g8, r4)


def _rs_add_chips(g8, r4, r3, cj, name):
    R, n = g8.shape[1:]
    rb = R // 2

    def body(cj_ref, g_ref, p_ref, r_ref, o_ref):
        own = g_ref[0] + p_ref[0]
        o_ref[...] = ((own + r_ref[0].astype(F32)) + r_ref[1].astype(F32)) + r_ref[2].astype(F32)

    return pl.pallas_call(
        body, name=name,
        grid_spec=pltpu.PrefetchScalarGridSpec(
            num_scalar_prefetch=1, grid=(R // rb,),
            in_specs=[pl.BlockSpec((1, rb, n), lambda i, cj_ref: (2 * cj_ref[1] + cj_ref[0], i, 0)),
                      pl.BlockSpec((1, rb, n), lambda i, cj_ref: (cj_ref[1], i, 0)),
                      pl.BlockSpec((3, rb, n), lambda i, cj_ref: (0, i, 0))],
            out_specs=pl.BlockSpec((rb, n), lambda i, cj_ref: (i, 0))),
        out_shape=jax.ShapeDtypeStruct((R, n), F32), compiler_params=_cp(1))(cj, g8, r4, r3)


def _sum8(g):
    n = g.shape[1]

    def body(g_ref, o_ref):
        acc = g_ref[0:1, :]
        for k in range(1, 8):
            acc = acc + g_ref[k:k + 1, :]
        o_ref[...] = acc

    return pl.pallas_call(body, name="sum_devices", out_shape=jax.ShapeDtypeStruct((1, n), F32),
                          compiler_params=pltpu.CompilerParams(vmem_limit_bytes=VMEM_LIMIT))(g)


ADA_ROWS = 64


def _ada_fwd(cs, w, b):
    n = w.shape[1]

    def body(c_ref, w_ref, b_ref, o_ref):
        cv = c_ref[...]
        s = (cv * _sigmoid(cv)).astype(BF16)
        o_ref[...] = _dot(s, w_ref[...].astype(BF16)) + b_ref[...]

    return pl.pallas_call(body, name="ada_fwd", out_shape=jax.ShapeDtypeStruct((ADA_ROWS, n), F32),
                          compiler_params=pltpu.CompilerParams(vmem_limit_bytes=VMEM_LIMIT))(cs, w, b)


def _ada_bwd(cs, w, dm):
    n = w.shape[1]

    def body(c_ref, w_ref, dm_ref, gw_ref, dc_ref):
        cv = c_ref[...]
        s = (cv * _sigmoid(cv)).astype(BF16)
        gw_ref[...] = _dot_tn(s, dm_ref[...].astype(BF16))
        dc_ref[...] = _dot_nt(dm_ref[32:40, :].astype(BF16), w_ref[...].astype(BF16))

    return pl.pallas_call(
        body, name="ada_bwd",
        out_shape=[jax.ShapeDtypeStruct((D, n), F32), jax.ShapeDtypeStruct((8, D), F32)],
        compiler_params=pltpu.CompilerParams(vmem_limit_bytes=VMEM_LIMIT))(cs, w, dm)


def _adamw(w, g, m, v, name):
    r, c = w.shape
    rb = r
    if r % 8 == 0 and r * c * 4 > (1 << 20):
        rb = 8
        for cand in range(8, r, 8):
            if r % cand == 0 and cand * c * 4 <= (1 << 20):
                rb = cand

    def body(w_ref, g_ref, m_ref, v_ref, d_ref, nm_ref, nv_ref):
        gv = g_ref[...]
        mn = ADAM_B1 * m_ref[...] + (1.0 - ADAM_B1) * gv
        vn = ADAM_B2 * v_ref[...] + (1.0 - ADAM_B2) * (gv * gv)
        m_hat = mn / (1.0 - ADAM_B1 ** ADAM_STEP)
        v_hat = vn / (1.0 - ADAM_B2 ** ADAM_STEP)
        d_ref[...] = -ADAM_LR * (m_hat / (jnp.sqrt(v_hat) + ADAM_EPS) + ADAM_WD * w_ref[...])
        nm_ref[...] = mn
        nv_ref[...] = vn

    spec = pl.BlockSpec((rb, c), lambda i: (i, 0))
    shp = jax.ShapeDtypeStruct((r, c), F32)
    return pl.pallas_call(body, name=name, grid=(r // rb,), in_specs=[spec] * 4, out_specs=[spec] * 3,
                          out_shape=[shp] * 3, compiler_params=_cp(1))(w, g, m, v)


BIG = ("ffn1_w_in", "ffn1_w_out", "w_mix_in", "w_mix_out", "ffn2_w_in", "ffn2_w_out")


def _half_rows(w, c):
    half = w.shape[0] // 2
    return lax.dynamic_slice_in_dim(w, c * half, half, axis=0)


def _lower_bounds(logits):
    return jnp.cumsum(jax.nn.softmax(logits.astype(F32), axis=1), axis=1)[:, 0]


def kernel(x, c, ctx, c_ctx, w_ada, b_ada, ln_gain, ln_bias, ffn1_w_in, ffn1_w_out, w_mix_in, gla_a2_fwd, gla_a2_bwd, gla_a_bias_fwd, gla_a_bias_bwd, hgrn_lb_logits, gla_norm_gain, hgrn_norm_gain, w_mix_out, ffn2_w_in, ffn2_w_out, loss_target, m_c_ctx, m_w_ada, m_b_ada, m_ln_gain, m_ln_bias, m_ffn1_w_in, m_ffn1_w_out, m_w_mix_in, m_gla_a2_fwd, m_gla_a2_bwd, m_gla_a_bias_fwd, m_gla_a_bias_bwd, m_hgrn_lb_logits, m_gla_norm_gain, m_hgrn_norm_gain, m_w_mix_out, m_ffn2_w_in, m_ffn2_w_out, v_c_ctx, v_w_ada, v_b_ada, v_ln_gain, v_ln_bias, v_ffn1_w_in, v_ffn1_w_out, v_w_mix_in, v_gla_a2_fwd, v_gla_a2_bwd, v_gla_a_bias_fwd, v_gla_a_bias_bwd, v_hgrn_lb_logits, v_gla_norm_gain, v_hgrn_norm_gain, v_w_mix_out, v_ffn2_w_in, v_ffn2_w_out):
    xi, yi, ci = lax.axis_index("x"), lax.axis_index("y"), lax.axis_index("c")
    chip = 2 * xi + yi
    dev = 2 * chip + ci
    B = x.shape[0]
    weights = dict(ffn1_w_in=ffn1_w_in[0], ffn1_w_out=ffn1_w_out[0], w_mix_in=w_mix_in[0], w_mix_out=w_mix_out[0],
                   ffn2_w_in=ffn2_w_in[0], ffn2_w_out=ffn2_w_out[0])

    mine = jnp.concatenate([c.reshape(-1), ln_gain.reshape(-1), ln_bias.reshape(-1), gla_a2_fwd.reshape(-1),
                            gla_a2_bwd.reshape(-1), hgrn_lb_logits.reshape(-1)])
    g1 = _gather_flat(mine, "gather_cond")
    nc = B * D
    c_all = g1[:, :nc].reshape(8 * B, D)
    per_chip = g1[0::2, nc:]
    o = 0

    def take(shape, axis):
        nonlocal o
        n = int(np.prod(shape))
        parts = per_chip[:, o:o + n].reshape((4,) + shape)
        o += n
        return jnp.concatenate([parts[j] for j in range(4)], axis=axis)

    ln_gain_f = take((3, 256), 1)
    ln_bias_f = take((3, 256), 1)
    a2f_f = take((16, 64), 1)
    a2b_f = take((16, 64), 1)
    lbl_f = take((2, 2, 128), 2)
    lb, lb_vjp = jax.vjp(_lower_bounds, lbl_f)

    cs = jnp.concatenate([c_all, c_ctx.reshape(1, D), jnp.zeros((ADA_ROWS - 8 * B - 1, D), F32)], axis=0)
    ncol = w_ada.shape[2]
    b_cols = lax.dynamic_slice_in_dim(b_ada, chip * ncol, ncol, axis=1)
    m_cols = _ada_fwd(cs, w_ada[0], b_cols)
    g2 = _small_allgather(m_cols, "gather_mod")[0::2]
    m_all = jnp.concatenate([g2[j] for j in range(4)], axis=1)
    m_lat = lax.dynamic_slice_in_dim(m_all, dev * B, B, axis=0).reshape(B, 1, N_MOD, D)
    m_ctx = jnp.broadcast_to(m_all[8 * B].reshape(1, 1, N_MOD, D), (B, 1, N_MOD, D))

    groups = dict(ffn2=("ffn2_w_in", "ffn2_w_out"), mix=("w_mix_in", "w_mix_out"), ffn1=("ffn1_w_in", "ffn1_w_out"))
    shards = dict(weights, w_mix_in=jnp.pad(weights["w_mix_in"], ((0, 0), (0, MIX_NP - MIX_N))))
    blks = {k: _half_rows(shards[k], ci).astype(BF16) for k in BIG}
    gathering = {}
    token = m_all
    for group in ("ffn1", "mix", "ffn2"):
        sems, x_thru, l_thru, token = _split_start(_gather_copies, 4, [blks[k] for k in groups[group]], 8, token,
                                                   "weight_gather_start_" + group)
        gathering[group] = (sems, x_thru, l_thru)
    mvec = jnp.concatenate([m_ctx, m_lat], axis=1) + token[0, 0]

    def weights_for(group, after):
        names = groups[group]
        _, got = _split_wait(_gather_copies, 4, *gathering[group], after, "weight_gather_wait_" + group)
        got = _gather_forward(got, "weight_gather_forward_" + group)
        w_in, w_out = (lax.dynamic_update_index_in_dim(g, blks[k], dev, 0) for k, g in zip(names, got))
        if group == "mix":
            return _mix_in_to_padded(w_in.reshape(4, D, MIX_NP)), w_out.reshape(-1, D)
        return w_in.reshape((4,) + shards[names[0]].shape), w_out.reshape(-1, D)

    cvec = ci.reshape(1).astype(jnp.int32)
    cjvec = jnp.stack([ci, chip]).astype(jnp.int32)
    in_flight = {}

    def on_grads(group, gs):
        names = groups[group]
        if group == "mix":
            gs = (_mix_in_from_padded(gs[0]), gs[1])
        g8s = [g.reshape((8, shards[k].shape[0] // 2, shards[k].shape[1])) for k, g in zip(names, gs)]
        sems, g_thru, l_thru, token = _split_start(_pair_copies, 4, g8s, 4, None, "grad_pair_start_" + group)
        in_flight[group] = (sems, g_thru, l_thru)
        return token

    def on_sent(group, after):
        g8s, r4s = _split_wait(_pair_copies, 4, *in_flight[group], after, "grad_pair_wait_" + group)
        h4s = [_rs_add_pair(g, r, cvec, "grad_pair_add_" + k) for k, g, r in zip(groups[group], g8s, r4s)]
        sems, h_thru, l_thru, token = _split_start(_chip_copies, 3, h4s, 3, None, "grad_chip_start_" + group)
        in_flight[group] = (g8s, r4s, sems, h_thru, l_thru)
        return token

    loss_l, grad_x, dm, small = _local_step(
        x, ctx, loss_target, mvec, weights_for, ln_gain_f, ln_bias_f, a2f_f, a2b_f,
        gla_a_bias_fwd, gla_a_bias_bwd, lb, gla_norm_gain, hgrn_norm_gain, on_grads, on_sent)
    loss = lax.psum(loss_l, ("x", "y", "c"))

    dm_lat = dm[:, 1].reshape(B, N_MOD * D)
    dm_ctx = jnp.sum(dm[:, 0], axis=0).reshape(N_MOD * D)
    keys = ("ln_gain", "ln_bias", "a2f", "a2b", "abf", "abb", "lb", "gng", "gnh")
    flat = jnp.concatenate([dm_lat.reshape(-1), dm_ctx] + [small[k].reshape(-1) for k in keys])
    g3 = _gather_flat(flat, "gather_small_grads")
    nlat = B * N_MOD * D
    dm_all = g3[:, :nlat].reshape(8 * B, N_MOD * D)
    tot = _sum8(g3[:, nlat:])[0]
    dmc_tot = tot[:N_MOD * D]
    o = N_MOD * D
    sg = {}
    for k in keys:
        n = int(np.prod(small[k].shape))
        sg[k] = tot[o:o + n].reshape(small[k].shape)
        o += n
    dm_rows = jnp.concatenate([dm_all, dmc_tot.reshape(1, -1), jnp.zeros((ADA_ROWS - 8 * B - 1, N_MOD * D), F32)], axis=0)
    g_b_ada = (jnp.sum(dm_all, axis=0) + dmc_tot).reshape(1, N_MOD * D)
    g_w_ada, dcc = _ada_bwd(cs, w_ada[0], lax.dynamic_slice_in_dim(dm_rows, chip * ncol, ncol, axis=1))
    g4 = _gather_flat(dcc[0], "gather_cctx")
    dsilu = ((g4[0] + g4[2]) + g4[4]) + g4[6]
    sc = _sigmoid(c_ctx)
    g_c_ctx = dsilu * (sc * (1.0 + c_ctx * (1.0 - sc)))
    (g_lbl,) = lb_vjp(sg["lb"])

    def cols(a, n, axis):
        return lax.dynamic_slice_in_dim(a, chip * n, n, axis=axis)

    fin = {}
    for group in ("ffn2", "mix", "ffn1"):
        g8s, r4s, sems, h_thru, l_thru = in_flight[group]
        _, r3s = _split_wait(_chip_copies, 3, sems, h_thru, l_thru, g_w_ada, "grad_chip_wait_" + group)
        for k, g, r4, r3 in zip(groups[group], g8s, r4s, r3s):
            fin[k] = _rs_add_chips(g, r4, r3, cjvec, "grad_chip_add_" + k)
    fins = [fin[k] for k in BIG]
    gsh = {}
    for k, own, sib in zip(BIG, fins, _rs_pair_share(fins, "grad_pair_share")):
        both = jnp.where(ci == 0, jnp.stack([own, sib]), jnp.stack([sib, own]))
        gsh[k] = both.reshape(shards[k].shape)[:, :weights[k].shape[1]]

    grads = dict(
        c_ctx=g_c_ctx, w_ada=g_w_ada[None], b_ada=g_b_ada, ln_gain=cols(sg["ln_gain"], 256, 1)[None],
        ln_bias=cols(sg["ln_bias"], 256, 1)[None], ffn1_w_in=gsh["ffn1_w_in"][None], ffn1_w_out=gsh["ffn1_w_out"][None],
        w_mix_in=gsh["w_mix_in"][None], gla_a2_fwd=cols(sg["a2f"], 64, 1)[None], gla_a2_bwd=cols(sg["a2b"], 64, 1)[None],
        gla_a_bias_fwd=sg["abf"], gla_a_bias_bwd=sg["abb"], hgrn_lb_logits=cols(g_lbl, 128, 2),
        gla_norm_gain=sg["gng"].reshape(1, HD), hgrn_norm_gain=sg["gnh"].reshape(1, HD),
        w_mix_out=gsh["w_mix_out"][None], ffn2_w_in=gsh["ffn2_w_in"][None], ffn2_w_out=gsh["ffn2_w_out"][None])
    params = dict(
        c_ctx=(c_ctx, m_c_ctx, v_c_ctx), w_ada=(w_ada, m_w_ada, v_w_ada), b_ada=(b_ada, m_b_ada, v_b_ada),
        ln_gain=(ln_gain, m_ln_gain, v_ln_gain), ln_bias=(ln_bias, m_ln_bias, v_ln_bias),
        ffn1_w_in=(ffn1_w_in, m_ffn1_w_in, v_ffn1_w_in), ffn1_w_out=(ffn1_w_out, m_ffn1_w_out, v_ffn1_w_out),
        w_mix_in=(w_mix_in, m_w_mix_in, v_w_mix_in), gla_a2_fwd=(gla_a2_fwd, m_gla_a2_fwd, v_gla_a2_fwd),
        gla_a2_bwd=(gla_a2_bwd, m_gla_a2_bwd, v_gla_a2_bwd),
        gla_a_bias_fwd=(gla_a_bias_fwd, m_gla_a_bias_fwd, v_gla_a_bias_fwd),
        gla_a_bias_bwd=(gla_a_bias_bwd, m_gla_a_bias_bwd, v_gla_a_bias_bwd),
        hgrn_lb_logits=(hgrn_lb_logits, m_hgrn_lb_logits, v_hgrn_lb_logits),
        gla_norm_gain=(gla_norm_gain, m_gla_norm_gain, v_gla_norm_gain),
        hgrn_norm_gain=(hgrn_norm_gain, m_hgrn_norm_gain, v_hgrn_norm_gain),
        w_mix_out=(w_mix_out, m_w_mix_out, v_w_mix_out), ffn2_w_in=(ffn2_w_in, m_ffn2_w_in, v_ffn2_w_in),
        ffn2_w_out=(ffn2_w_out, m_ffn2_w_out, v_ffn2_w_out))
    order = list(params.keys())
    big_names = ("w_ada",) + BIG
    upd = {}
    for k in big_names:
        w_, m_, v_ = params[k]
        s2 = w_.shape[-2:]
        d_, nm_, nv_ = _adamw(w_.reshape(s2), grads[k].reshape(s2), m_.reshape(s2), v_.reshape(s2), "adamw_" + k)
        upd[k] = (d_.reshape(w_.shape), nm_.reshape(w_.shape), nv_.reshape(w_.shape))
    small_names = [k for k in order if k not in big_names]
    sizes = [int(np.prod(params[k][0].shape)) for k in small_names]
    tot_n = sum(sizes)
    npad = -(-tot_n // 1024) * 1024

    def packed(get):
        flat_ = jnp.concatenate([get(k).reshape(-1) for k in small_names])
        return jnp.pad(flat_, (0, npad - tot_n)).reshape(8, npad // 8)

    d_s, nm_s, nv_s = _adamw(packed(lambda k: params[k][0]), packed(lambda k: grads[k]),
                             packed(lambda k: params[k][1]), packed(lambda k: params[k][2]), "adamw_small")
    o = 0
    for k, n in zip(small_names, sizes):
        shp = params[k][0].shape
        upd[k] = tuple(a.reshape(-1)[o:o + n].reshape(shp) for a in (d_s, nm_s, nv_s))
        o += n

    return (loss, grad_x, *[grads[k].reshape(params[k][0].shape) for k in order], *[upd[k][0] for k in order],
            *[upd[k][1] for k in order], *[upd[k][2] for k in order])
```

```python
import functools

import numpy as np
import jax
import jax.numpy as jnp
from jax import lax
from jax.experimental import pallas as pl
from jax.experimental.pallas import tpu as pltpu

F32 = jnp.float32
BF16 = jnp.bfloat16
MESH = pl.DeviceIdType.MESH

D = 1024
DFF = 2816
TM = 256
CH = 32
NCB = TM // CH
SB = 128
CSB = SB // CH
NSB = TM // SB
HP = 4
HD = 128
NH = 8
LN_EPS = 1e-5
NORM_EPS = 1e-6
ALPHA = 2.0 ** 0.25
GATE_NORM = 16.0
GLA_DK = 64
N_MOD = 9
VMEM_LIMIT = 52 * 1024 * 1024

MIXP = 5120
GG, RG, GQ, GK, GV, RQ, RFF, RFB, RI, LR = 0, 512, 1024, 1536, 2048, 2560, 3072, 3584, 4096, 4608
IN_SPLITS = (256, 256, 512, 512, 16, 16, 512, 512, 512, 512, 512)

ADAM_LR, ADAM_B1, ADAM_B2, ADAM_EPS, ADAM_WD, ADAM_STEP = 0.001, 0.9, 0.999, 1e-08, 0.01, 10


def _cp(n_axes):
    return pltpu.CompilerParams(dimension_semantics=("arbitrary",) * n_axes, vmem_limit_bytes=VMEM_LIMIT)


def _rowmap(stride, off):
    return lambda b, i: (b * stride + off + i, 0)


def _mmap(comb):
    if comb:
        return lambda b, i: (b, jnp.minimum(i, 1), 0, 0)
    return lambda b, i: (b, 1, 0, 0)


def _ln(x):
    mu = jnp.mean(x, axis=-1, keepdims=True)
    xc = x - mu
    var = jnp.mean(xc * xc, axis=-1, keepdims=True)
    r = lax.rsqrt(var + LN_EPS)
    return xc * r, r


def _ln_bwd(dxh, xh, r):
    return r * (dxh - jnp.mean(dxh, axis=-1, keepdims=True) - xh * jnp.mean(dxh * xh, axis=-1, keepdims=True))


def _sigmoid(x):
    return 1.0 / (1.0 + jnp.exp(-x))


def _rsum(x):
    return jnp.sum(x, axis=0, keepdims=True)


def _dot(a, b):
    return jnp.dot(a, b, preferred_element_type=F32)


def _dot_nt(a, b):
    return lax.dot_general(a, b, (((1,), (1,)), ((), ())), preferred_element_type=F32)


def _dot_tn(a, b):
    return lax.dot_general(a, b, (((0,), (0,)), ((), ())), preferred_element_type=F32)


def _modulate(xv, m_ref, sub):
    xh, _ = _ln(xv)
    sh = m_ref[0, 0, 3 * sub:3 * sub + 1, :]
    sc = m_ref[0, 0, 3 * sub + 1:3 * sub + 2, :]
    return (xh * (1.0 + sh) + sc).astype(BF16)


def _embed_lnmod(x, ctx, pos, mvec):
    B, T, _ = x.shape
    nt = 1 + T // TM

    def body(x_ref, c_ref, p_ref, m_ref, o_ref, h_ref):
        i = pl.program_id(1)

        @pl.when(i == 0)
        def _():
            o_ref[...] = c_ref[0]

        @pl.when(i > 0)
        def _():
            o_ref[...] = x_ref[0] + p_ref[...]

        h_ref[...] = _modulate(o_ref[...], m_ref, 0)

    rows = pl.BlockSpec((TM, D), lambda b, i: (b * nt + i, 0))
    return pl.pallas_call(
        body, name="embed_lnmod0", grid=(B, nt),
        in_specs=[pl.BlockSpec((1, TM, D), lambda b, i: (b, jnp.maximum(i - 1, 0), 0)),
                  pl.BlockSpec((1, TM, D), lambda b, i: (b, 0, 0)),
                  pl.BlockSpec((TM, D), lambda b, i: (jnp.maximum(i - 1, 0), 0)),
                  pl.BlockSpec((1, 1, N_MOD, D), _mmap(True))],
        out_specs=[rows, rows],
        out_shape=[jax.ShapeDtypeStruct((B * nt * TM, D), F32), jax.ShapeDtypeStruct((B * nt * TM, D), BF16)],
        compiler_params=_cp(2))(x, ctx, pos, mvec)


def _lnmod0_bwd(dh, X, mvec, dres, x_shape, B, nt):
    def body(dh_ref, x_ref, m_ref, dr_ref, dx_ref, dm_ref):
        i = pl.program_id(1)
        xh, r = _ln(x_ref[...])
        sh = m_ref[0, 0, 0:1, :]
        dhv = dh_ref[...]

        @pl.when((i == 0) | (i == 1))
        def _():
            dm_ref[...] = jnp.zeros_like(dm_ref)

        dm_ref[0, 0, 0:1, :] += _rsum(dhv * xh)
        dm_ref[0, 0, 1:2, :] += _rsum(dhv)

        @pl.when(i > 0)
        def _():
            dx_ref[0] = _ln_bwd(dhv * (1.0 + sh), xh, r) + dr_ref[...]

    rows = pl.BlockSpec((TM, D), _rowmap(nt, 0))
    return pl.pallas_call(
        body, name="lnmod0_bwd", grid=(B, nt),
        in_specs=[rows, rows, pl.BlockSpec((1, 1, N_MOD, D), _mmap(True)), rows],
        out_specs=[pl.BlockSpec((1, TM, D), lambda b, i: (b, jnp.maximum(i - 1, 0), 0)),
                   pl.BlockSpec((1, 1, 2, D), _mmap(True))],
        out_shape=[jax.ShapeDtypeStruct(x_shape, F32), jax.ShapeDtypeStruct((B, 2, 2, D), F32)],
        compiler_params=_cp(2))(dh, X, mvec, dres)


def _resid_fwd(x_ref, y_ref, m_ref, gb_ref, sub, w):
    wg = w * m_ref[0, 0, 3 * sub + 2:3 * sub + 3, :]
    y = y_ref[...]
    zh, r = _ln(ALPHA * x_ref[...] + wg * y)
    return y, wg, zh, r


def _resid_grads(do, y, wg, zh, r, w, gb_ref, comb, dx_ref, dy_ref, dg_ref, dgb_ref):
    b_, i = pl.program_id(0), pl.program_id(1)
    dz = _ln_bwd(do * gb_ref[0:1, :], zh, r)
    dx_ref[...] = ALPHA * dz
    dy_ref[...] = (wg * dz).astype(BF16)

    @pl.when((b_ == 0) & (i == 0))
    def _():
        dgb_ref[...] = jnp.zeros_like(dgb_ref)

    dgb_ref[0:1, :] += _rsum(do * zh)
    dgb_ref[1:2, :] += _rsum(do)

    init = (i == 0) | (i == 1) if comb else (i == 0)

    @pl.when(init)
    def _():
        dg_ref[...] = jnp.zeros_like(dg_ref)

    dg_ref[0, 0] += w * _rsum(dz * y)


def _resid_lnmod(X, lx, Y, mvec, comb, sub, w, gb, B, nt, name):
    def body(x_ref, y_ref, m_ref, gb_ref, o_ref, h_ref):
        _, _, zh, _ = _resid_fwd(x_ref, y_ref, m_ref, gb_ref, sub, w)
        xn = zh * gb_ref[0:1, :] + gb_ref[1:2, :]
        o_ref[...] = xn
        h_ref[...] = _modulate(xn, m_ref, sub + 1)

    rows = pl.BlockSpec((TM, D), _rowmap(nt, 0))
    return pl.pallas_call(
        body, name=name, grid=(B, nt),
        in_specs=[pl.BlockSpec((TM, D), _rowmap(*lx)), rows,
                  pl.BlockSpec((1, 1, N_MOD, D), _mmap(comb)), pl.BlockSpec((2, D), lambda b, i: (0, 0))],
        out_specs=[rows, rows],
        out_shape=[jax.ShapeDtypeStruct((B * nt * TM, D), F32), jax.ShapeDtypeStruct((B * nt * TM, D), BF16)],
        compiler_params=_cp(2))(X, Y, mvec, gb)


def _resid_out_shapes(B, nt, comb):
    rows = pl.BlockSpec((TM, D), _rowmap(nt, 0))
    specs = [rows, rows, pl.BlockSpec((1, 1, 1, D), _mmap(comb)), pl.BlockSpec((2, D), lambda b, i: (0, 0))]
    shapes = [jax.ShapeDtypeStruct((B * nt * TM, D), F32), jax.ShapeDtypeStruct((B * nt * TM, D), BF16),
              jax.ShapeDtypeStruct((B, 2, 1, D), F32), jax.ShapeDtypeStruct((2, D), F32)]
    return specs, shapes


def _tail(X, Y, mvec, gb, tgt, sub, w, B, nt):
    def body(x_ref, y_ref, m_ref, gb_ref, t_ref, dx_ref, dy_ref, dg_ref, dgb_ref, l_ref):
        y, wg, zh, r = _resid_fwd(x_ref, y_ref, m_ref, gb_ref, sub, w)
        e = (zh * gb_ref[0:1, :] + gb_ref[1:2, :]) - t_ref[0]

        @pl.when((pl.program_id(0) == 0) & (pl.program_id(1) == 0))
        def _():
            l_ref[...] = jnp.zeros_like(l_ref)

        l_ref[...] += _rsum(e * e)
        _resid_grads(e * (1.0 / D), y, wg, zh, r, w, gb_ref, False, dx_ref, dy_ref, dg_ref, dgb_ref)

    rows = pl.BlockSpec((TM, D), _rowmap(nt, 0))
    specs, shapes = _resid_out_shapes(B, nt, False)
    return pl.pallas_call(
        body, name="resid2_loss_bwd", grid=(B, nt),
        in_specs=[rows, rows, pl.BlockSpec((1, 1, N_MOD, D), _mmap(False)), pl.BlockSpec((2, D), lambda b, i: (0, 0)),
                  pl.BlockSpec((1, TM, D), lambda b, i: (b, i, 0))],
        out_specs=specs + [pl.BlockSpec((1, D), lambda b, i: (0, 0))],
        out_shape=shapes + [jax.ShapeDtypeStruct((1, D), F32)],
        compiler_params=_cp(2))(X, Y, mvec, gb, tgt)


def _lnmod_resid_bwd(dh, Xi, lxi, dres, Xp, lxp, Yp, mvec, comb, sub, w, gb, B, nt, name):
    ntl = nt - 1 if comb else nt

    def body(dh_ref, xi_ref, dr_ref, xp_ref, yp_ref, m_ref, gb_ref, dx_ref, dy_ref, dg_ref, dgb_ref, dm_ref):
        i = pl.program_id(1)
        xh, r = _ln(xi_ref[...])
        sh = m_ref[0, 0, 3 * sub:3 * sub + 1, :]
        dhv = dh_ref[...]
        dr = dr_ref[...]
        if comb:
            dr = jnp.where(i > 0, dr, 0.0)
        do = _ln_bwd(dhv * (1.0 + sh), xh, r) + dr

        init = (i == 0) | (i == 1) if comb else (i == 0)

        @pl.when(init)
        def _():
            dm_ref[...] = jnp.zeros_like(dm_ref)

        dm_ref[0, 0, 0:1, :] += _rsum(dhv * xh)
        dm_ref[0, 0, 1:2, :] += _rsum(dhv)
        y, wg, zh, r2 = _resid_fwd(xp_ref, yp_ref, m_ref, gb_ref, sub - 1, w)
        _resid_grads(do, y, wg, zh, r2, w, gb_ref, comb, dx_ref, dy_ref, dg_ref, dgb_ref)

    rows = pl.BlockSpec((TM, D), _rowmap(nt, 0))
    if comb:
        dres_spec = pl.BlockSpec((TM, D), lambda b, i: (b * ntl + jnp.maximum(i - 1, 0), 0))
    else:
        dres_spec = rows
    specs, shapes = _resid_out_shapes(B, nt, comb)
    return pl.pallas_call(
        body, name=name, grid=(B, nt),
        in_specs=[rows, pl.BlockSpec((TM, D), _rowmap(*lxi)), dres_spec, pl.BlockSpec((TM, D), _rowmap(*lxp)), rows,
                  pl.BlockSpec((1, 1, N_MOD, D), _mmap(comb)), pl.BlockSpec((2, D), lambda b, i: (0, 0))],
        out_specs=specs + [pl.BlockSpec((1, 1, 2, D), _mmap(comb))],
        out_shape=shapes + [jax.ShapeDtypeStruct((B, 2, 2, D), F32)],
        compiler_params=_cp(2))(dh, Xi, dres, Xp, Yp, mvec, gb)


def _ffn_out_dx(dy, W, u, name, dep=None):
    M = dy.shape[0]
    half = DFF // 2
    deps = [] if dep is None else [dep]

    def body(dy_ref, w_ref, u_ref, *rest):
        du_ref = rest[-1]
        dyv = dy_ref[...]
        for j in range(2):
            lo, hi = j * half, (j + 1) * half
            da = _dot_nt(dyv, w_ref[lo:hi, :])
            g = u_ref[:, lo:hi].astype(F32)
            up = u_ref[:, DFF + lo:DFF + hi].astype(F32)
            s = _sigmoid(g)
            du_ref[:, lo:hi] = (da * up * (s * (1.0 + g * (1.0 - s)))).astype(BF16)
            du_ref[:, DFF + lo:DFF + hi] = (da * (g * s)).astype(BF16)

    return pl.pallas_call(
        body, name=name, grid=(M // TM,),
        in_specs=[pl.BlockSpec((TM, D), lambda i: (i, 0)), _wspec(W, 1), pl.BlockSpec((TM, 2 * DFF), lambda i: (i, 0))]
        + [_wspec(d, 1) for d in deps],
        out_specs=pl.BlockSpec((TM, 2 * DFF), lambda i: (i, 0)),
        out_shape=jax.ShapeDtypeStruct((M, 2 * DFF), BF16), compiler_params=_cp(1))(dy, W, u, *deps)


def _wspec(W, nidx):
    zeros = (0,) * W.ndim
    if nidx == 1:
        return pl.BlockSpec(W.shape, lambda i: zeros)
    return pl.BlockSpec(W.shape, lambda b, i: zeros)


def _mm_nn(A, la, W, B, nt, out_dtype, name):
    K, N = W.shape

    def body(a_ref, w_ref, o_ref):
        o_ref[...] = _dot(a_ref[...], w_ref[...]).astype(out_dtype)

    return pl.pallas_call(
        body, name=name, grid=(B, nt),
        in_specs=[pl.BlockSpec((TM, K), _rowmap(*la)), _wspec(W, 2)],
        out_specs=pl.BlockSpec((TM, N), _rowmap(nt, 0)),
        out_shape=jax.ShapeDtypeStruct((B * nt * TM, N), out_dtype), compiler_params=_cp(2))(A, W)


def _ffn_in(A, la, W3, B, nt, name):
    K, n = W3.shape[1:]

    def body(a_ref, w_ref, u_ref, s_ref):
        a = a_ref[...]
        for j in range(2):
            g = _dot(a, w_ref[j])
            up = _dot(a, w_ref[j + 2])
            u_ref[:, j * n:(j + 1) * n] = g.astype(BF16)
            u_ref[:, (j + 2) * n:(j + 3) * n] = up.astype(BF16)
            s_ref[:, j * n:(j + 1) * n] = (g * _sigmoid(g) * up).astype(BF16)

    rows = B * nt * TM
    return pl.pallas_call(
        body, name=name, grid=(B, nt),
        in_specs=[pl.BlockSpec((TM, K), _rowmap(*la)), _wspec(W3, 2)],
        out_specs=[pl.BlockSpec((TM, 4 * n), _rowmap(nt, 0)), pl.BlockSpec((TM, 2 * n), _rowmap(nt, 0))],
        out_shape=[jax.ShapeDtypeStruct((rows, 4 * n), BF16), jax.ShapeDtypeStruct((rows, 2 * n), BF16)],
        compiler_params=_cp(2))(A, W3)


def _mm_nt(A, W, name, dep=None):
    M, N = A.shape
    K = W.shape[-2]

    def body(a_ref, w_ref, *rest):
        o_ref = rest[-1]
        if W.ndim == 3:
            n = W.shape[-1]
            acc = _dot_nt(a_ref[:, 0:n], w_ref[0])
            for j in range(1, 4):
                acc = acc + _dot_nt(a_ref[:, j * n:(j + 1) * n], w_ref[j])
            o_ref[...] = acc
        else:
            o_ref[...] = _dot_nt(a_ref[...], w_ref[...])

    deps = [] if dep is None else [dep]
    return pl.pallas_call(
        body, name=name, grid=(M // TM,),
        in_specs=[pl.BlockSpec((TM, N), lambda i: (i, 0)), _wspec(W, 1)] + [_wspec(d, 1) for d in deps],
        out_specs=pl.BlockSpec((TM, K), lambda i: (i, 0)),
        out_shape=jax.ShapeDtypeStruct((M, K), F32), compiler_params=_cp(1))(A, W, *deps)


def _mm_tn(A, G, name, tn=512, shards=None):
    M, K = A.shape
    N = G.shape[1]
    tk = 1024 if M % 1024 == 0 else 512
    if shards:
        tn = N // shards

    def body(a_ref, g_ref, o_ref):
        @pl.when(pl.program_id(1) == 0)
        def _():
            o_ref[...] = jnp.zeros_like(o_ref)

        upd = _dot_tn(a_ref[...], g_ref[...])
        if shards:
            o_ref[0] += upd
        else:
            o_ref[...] += upd

    if shards:
        out_spec = pl.BlockSpec((1, K, tn), lambda n, k: (n, 0, 0))
        out_shape = jax.ShapeDtypeStruct((shards, K, tn), F32)
    else:
        out_spec = pl.BlockSpec((K, tn), lambda n, k: (0, n))
        out_shape = jax.ShapeDtypeStruct((K, N), F32)
    return pl.pallas_call(
        body, name=name, grid=(N // tn, M // tk),
        in_specs=[pl.BlockSpec((tk, K), lambda n, k: (k, 0)), pl.BlockSpec((tk, tn), lambda n, k: (k, n))],
        out_specs=out_spec, out_shape=out_shape, compiler_params=_cp(2))(A, G)


def _logsig(z):
    return jnp.minimum(z, 0.0) - jnp.log(1.0 + jnp.exp(-jnp.abs(z)))


def _features(F, a2p, biasp, lbp, B):
    nt = F.shape[0] // (B * TM)
    nj = nt + 1
    R2 = nj * TM

    def body(f_ref, a2_ref, bias_ref, lb_ref, q_ref, k_ref, v_ref, g_ref):
        lr = f_ref[:, LR:LR + HD].astype(BF16)
        lane = lax.broadcasted_iota(jnp.int32, (1, 4 * HD), 1)
        keep = (lane & (HD - 1)) < GLA_DK
        for d in range(2):
            z = _dot(lr, a2_ref[d]) + bias_ref[d:d + 1, :]
            gl = jnp.where(keep, _logsig(z) * (1.0 / GATE_NORM), 0.0)
            for h in range(4):
                g_ref[d, 0, h] = gl[:, h * HD:(h + 1) * HD]
        for h in range(4):
            q_ref[0, h] = f_ref[:, GQ + h * HD:GQ + (h + 1) * HD] * (GLA_DK ** -0.5)
            kk = f_ref[:, GK + h * HD:GK + (h + 1) * HD]
            k_ref[0, 0, h] = kk
            k_ref[1, 0, h] = kk
            v_ref[0, h] = f_ref[:, GV + h * HD:GV + (h + 1) * HD].astype(BF16)
        for h in range(4):
            sl = slice(h * HD, (h + 1) * HD)
            rq = f_ref[:, RQ + h * HD:RQ + (h + 1) * HD]
            q_ref[0, 4 + h] = rq * _sigmoid(rq) * (HD ** -0.5)
            v_ref[0, 4 + h] = f_ref[:, RI + h * HD:RI + (h + 1) * HD].astype(BF16)
            for d, off in ((0, RFF), (1, RFB)):
                lb = lb_ref[d:d + 1, sl]
                f = lb + (1.0 - lb) * _sigmoid(f_ref[:, off + h * HD:off + (h + 1) * HD])
                g_ref[d, 0, 4 + h] = jnp.log(f)
                k_ref[d, 0, 4 + h] = 1.0 - f

    fmap = lambda b, j: (b * nt + jnp.where(j == nt, 0, j), 0)
    one = pl.BlockSpec((1, NH, TM, HD), lambda b, j: (b, 0, j, 0))
    two = pl.BlockSpec((2, 1, NH, TM, HD), lambda b, j: (0, b, 0, j, 0))
    s1 = jax.ShapeDtypeStruct((B, NH, R2, HD), F32)
    s2 = jax.ShapeDtypeStruct((2, B, NH, R2, HD), F32)
    return pl.pallas_call(
        body, name="mix_features", grid=(B, nj),
        in_specs=[pl.BlockSpec((TM, MIXP), fmap), pl.BlockSpec((2, HD, 4 * HD), lambda b, j: (0, 0, 0)),
                  pl.BlockSpec((2, 4 * HD), lambda b, j: (0, 0)), pl.BlockSpec((2, 4 * HD), lambda b, j: (0, 0))],
        out_specs=[one, two, one, two], out_shape=[s1, s2, jax.ShapeDtypeStruct(s1.shape, BF16), s2],
        compiler_params=_cp(2))(F, a2p, biasp, lbp)


def _features_bwd(F, a2p, biasp, lbp, dQ0, dQ1, dK0, dK1, dV0, dV1, dG0, dG1, dgates, B):
    nt = F.shape[0] // (B * TM)

    def body(f_ref, a2_ref, bias_ref, lb_ref, dq0, dq1, dk0, dk1, dv0, dv1, dg0, dg1, dgt_ref,
             df_ref, da2_ref, dbias_ref, dlb_ref):
        b_, i = pl.program_id(0), pl.program_id(1)

        @pl.when((b_ == 0) & (i == 0))
        def _():
            da2_ref[...] = jnp.zeros_like(da2_ref)
            dbias_ref[...] = jnp.zeros_like(dbias_ref)
            dlb_ref[...] = jnp.zeros_like(dlb_ref)

        df_ref[:, 0:2 * 4 * HD] = jnp.where(i > 0, dgt_ref[...], 0.0).astype(BF16)
        df_ref[:, LR + HD:] = jnp.zeros((TM, MIXP - LR - HD), BF16)
        lr = f_ref[:, LR:LR + HD].astype(BF16)
        lane = lax.broadcasted_iota(jnp.int32, (1, 4 * HD), 1)
        keep = (lane & (HD - 1)) < GLA_DK
        dlr = jnp.zeros((TM, HD), F32)
        dgs = (dg0, dg1)
        dks = (dk0, dk1)
        rd = lambda ref, h: ref[0, h].astype(F32)
        for d in range(2):
            z = _dot(lr, a2_ref[d]) + bias_ref[d:d + 1, :]
            dgl = jnp.concatenate([rd(dgs[d], h) for h in range(4)], axis=1)
            dz = jnp.where(keep, dgl * (1.0 / GATE_NORM) * (1.0 - _sigmoid(z)), 0.0)
            dzb = dz.astype(BF16)
            dlr = dlr + _dot_nt(dzb, a2_ref[d])
            da2_ref[d] += _dot_tn(lr, dzb)
            dbias_ref[d:d + 1, :] += _rsum(dz)
        df_ref[:, LR:LR + HD] = dlr.astype(BF16)
        for h in range(4):
            df_ref[:, GQ + h * HD:GQ + (h + 1) * HD] = ((rd(dq0, h) + rd(dq1, h)) * (GLA_DK ** -0.5)).astype(BF16)
            df_ref[:, GK + h * HD:GK + (h + 1) * HD] = (rd(dk0, h) + rd(dk1, h)).astype(BF16)
            df_ref[:, GV + h * HD:GV + (h + 1) * HD] = (rd(dv0, h) + rd(dv1, h)).astype(BF16)
        for h in range(4):
            sl = slice(h * HD, (h + 1) * HD)
            rq = f_ref[:, RQ + h * HD:RQ + (h + 1) * HD]
            s = _sigmoid(rq)
            dqh = rd(dq0, 4 + h) + rd(dq1, 4 + h)
            df_ref[:, RQ + h * HD:RQ + (h + 1) * HD] = (dqh * (HD ** -0.5) * (s * (1.0 + rq * (1.0 - s)))).astype(BF16)
            df_ref[:, RI + h * HD:RI + (h + 1) * HD] = (rd(dv0, 4 + h) + rd(dv1, 4 + h)).astype(BF16)
            for d, off in ((0, RFF), (1, RFB)):
                lb = lb_ref[d:d + 1, sl]
                sg = _sigmoid(f_ref[:, off + h * HD:off + (h + 1) * HD])
                f = lb + (1.0 - lb) * sg
                dff = rd(dgs[d], 4 + h) / f - rd(dks[d], 4 + h)
                df_ref[:, off + h * HD:off + (h + 1) * HD] = (dff * (1.0 - lb) * sg * (1.0 - sg)).astype(BF16)
                dlb_ref[d:d + 1, sl] += _rsum(dff * (1.0 - sg))

    m0 = lambda b, i: (b, 0, i, 0)
    m1 = lambda b, i: (b, 0, jnp.where(i == 0, nt, i), 0)
    one = lambda m: pl.BlockSpec((1, NH, TM, HD), m)
    return pl.pallas_call(
        body, name="mix_features_bwd", grid=(B, nt),
        in_specs=[pl.BlockSpec((TM, MIXP), _rowmap(nt, 0)), pl.BlockSpec((2, HD, 4 * HD), lambda b, i: (0, 0, 0)),
                  pl.BlockSpec((2, 4 * HD), lambda b, i: (0, 0)), pl.BlockSpec((2, 4 * HD), lambda b, i: (0, 0)),
                  one(m0), one(m1), one(m0), one(m1), one(m0), one(m1), one(m0), one(m1),
                  pl.BlockSpec((TM, D), lambda b, i: (b * (nt - 1) + jnp.maximum(i - 1, 0), 0))],
        out_specs=[pl.BlockSpec((TM, MIXP), _rowmap(nt, 0)), pl.BlockSpec((2, HD, 4 * HD), lambda b, i: (0, 0, 0)),
                   pl.BlockSpec((2, 4 * HD), lambda b, i: (0, 0)), pl.BlockSpec((2, 4 * HD), lambda b, i: (0, 0))],
        out_shape=[jax.ShapeDtypeStruct((B * nt * TM, MIXP), BF16), jax.ShapeDtypeStruct((2, HD, 4 * HD), F32),
                   jax.ShapeDtypeStruct((2, 4 * HD), F32), jax.ShapeDtypeStruct((2, 4 * HD), F32)],
        compiler_params=_cp(2))(F, a2p, biasp, lbp, dQ0, dQ1, dK0, dK1, dV0, dV1, dG0, dG1, dgates)


def _chunk_scan(x, rin, fwd):
    acc = x
    sft = 1
    while sft < CH:
        if fwd:
            acc = acc + jnp.where(rin >= sft, pltpu.roll(acc, sft, 0), 0.0)
        else:
            acc = acc + jnp.where(rin < CH - sft, pltpu.roll(acc, TM - sft, 0), 0.0)
        sft *= 2
    return acc


def _chunk_total(x):
    t = jnp.sum(x.reshape(NCB, CH, HD), axis=1, keepdims=True)
    return jnp.broadcast_to(t, (NCB, CH, HD)).reshape(TM, HD)


def _scan_masks(rev):
    rin = lax.broadcasted_iota(jnp.int32, (TM, HD), 0) & (CH - 1)
    ri = lax.broadcasted_iota(jnp.int32, (SB, SB), 0)
    ci = lax.broadcasted_iota(jnp.int32, (SB, SB), 1)
    same = (ri >> 5) == (ci >> 5)
    lo = same & (ri >= ci)
    up = same & (ri <= ci)
    mask, maskT = (up, lo) if rev else (lo, up)
    re = lax.broadcasted_iota(jnp.int32, (SB, CSB * HD), 0) >> 5
    ce = lax.broadcasted_iota(jnp.int32, (SB, CSB * HD), 1) >> 7
    return rin, mask, maskT, re == ce


def _scan_decay(q, k, g, rin, rev):
    b = _chunk_scan(g, rin, not rev)
    xx = _chunk_total(g) - b
    eb = jnp.exp(b)
    return b, xx, eb, q * eb, k * jnp.exp(-b), k * jnp.exp(xx)


def _sub(x, s):
    return x[s * SB:(s + 1) * SB]


def _expand(xb, mexp):
    return jnp.where(mexp, jnp.concatenate([xb] * CSB, axis=1), jnp.zeros((), xb.dtype))


def _own(x, mexp):
    xm = jnp.where(mexp, x, 0.0)
    acc = xm[:, 0:HD]
    for n in range(1, CSB):
        acc = acc + xm[:, n * HD:(n + 1) * HD]
    return acc


def _stack(per_chunk, s):
    return jnp.concatenate(per_chunk[s * CSB:(s + 1) * CSB], axis=1)


def _state_pass(s0, eb, uts, rev):
    order = range(NCB - 1, -1, -1) if rev else range(NCB)
    states = [None] * NCB
    s = s0
    for n in order:
        row = n * CH if rev else n * CH + CH - 1
        states[n] = s
        s = eb[row:row + 1, :] * s + uts[n // CSB][:, (n % CSB) * HD:(n % CSB + 1) * HD]
    return states, s


def _scan_fwd(Q, K, V, G, rev, B):
    nb = Q.shape[2] // TM - 1
    d = 1 if rev else 0
    rmap = (lambda s: nb - s) if rev else (lambda s: s)

    def body(q_ref, k_ref, v_ref, g_ref, o_ref, st_ref, s_scr):
        @pl.when(pl.program_id(2) == 0)
        def _():
            s_scr[...] = jnp.zeros_like(s_scr)

        rin, mask, _, mexp = _scan_masks(rev)
        for p in range(HP):
            s0 = s_scr[p]
            st_ref[0, p, 0] = s0
            _, _, eb, qd, ki, kt = _scan_decay(q_ref[0, p], k_ref[0, 0, p], g_ref[0, 0, p], rin, rev)
            qb, kib, ktb, vb = qd.astype(BF16), ki.astype(BF16), kt.astype(BF16), v_ref[0, p]
            uts = [_dot_tn(_sub(vb, s), _expand(_sub(ktb, s), mexp)) for s in range(NSB)]
            states, s_new = _state_pass(s0, eb, uts, rev)
            s_scr[p] = s_new
            for s in range(NSB):
                a = jnp.where(mask, _dot_nt(_sub(qb, s), _sub(kib, s)), 0.0)
                o_ref[0, p, s * SB:(s + 1) * SB, :] = (
                    _dot(a.astype(BF16), _sub(vb, s))
                    + _dot_nt(_expand(_sub(qb, s), mexp), _stack(states, s).astype(BF16)))

    one = pl.BlockSpec((1, HP, TM, HD), lambda b, h, s: (b, h, rmap(s), 0))
    two = pl.BlockSpec((1, 1, HP, TM, HD), lambda b, h, s: (d, b, h, rmap(s), 0))
    return pl.pallas_call(
        body, name="scan_fwd_rev" if rev else "scan_fwd", grid=(B, NH // HP, nb),
        in_specs=[one, two, one, two],
        out_specs=[one, pl.BlockSpec((1, HP, 1, HD, HD), lambda b, h, s: (b, h, s, 0, 0))],
        out_shape=[jax.ShapeDtypeStruct(Q.shape, F32), jax.ShapeDtypeStruct((B, NH, nb, HD, HD), F32)],
        scratch_shapes=[pltpu.VMEM((HP, HD, HD), F32)],
        compiler_params=_cp(3))(Q, K, V, G)


def _scan_bwd(Q, K, V, G, St, dO, rev, B):
    nb = Q.shape[2] // TM - 1
    d = 1 if rev else 0
    smap = lambda t: nb - 1 - t
    rmap = (lambda t: nb - smap(t)) if rev else smap

    def body(q_ref, k_ref, v_ref, g_ref, st_ref, do_ref, dq_ref, dk_ref, dv_ref, dg_ref, ds_scr):
        t = pl.program_id(2)

        @pl.when(t == 0)
        def _():
            ds_scr[...] = jnp.zeros_like(ds_scr)

        is_lat = smap(t) >= 1
        rin, mask, maskT, mexp = _scan_masks(rev)
        for p in range(HP):
            b, xx, eb, qd, ki, kt = _scan_decay(q_ref[0, p], k_ref[0, 0, p], g_ref[0, 0, p], rin, rev)
            qb, kib, ktb, vb = qd.astype(BF16), ki.astype(BF16), kt.astype(BF16), v_ref[0, p]
            dob = jnp.where(is_lat, do_ref[0, p], 0.0).astype(BF16)
            kt_exps = [_expand(_sub(ktb, s), mexp) for s in range(NSB)]
            uts = [_dot_tn(_sub(vb, s), kt_exps[s]) for s in range(NSB)]
            states, _ = _state_pass(st_ref[0, p, 0], eb, uts, rev)
            gts = [_dot_tn(_sub(dob, s), _expand(_sub(qb, s), mexp)) for s in range(NSB)]
            order = range(NCB) if rev else range(NCB - 1, -1, -1)
            dsp = [None] * NCB
            t2 = [None] * NCB
            dsc = ds_scr[p]
            for n in order:
                row = n * CH if rev else n * CH + CH - 1
                ebl = eb[row:row + 1, :]
                dsp[n] = dsc
                t2[n] = jnp.broadcast_to(ebl * _rsum(states[n] * dsc), (CH, HD))
                dsc = gts[n // CSB][:, (n % CSB) * HD:(n % CSB + 1) * HD] + ebl * dsc
            ds_scr[p] = dsc
            dqds, dkis, dkts = [], [], []
            for s in range(NSB):
                q_s, ki_s, v_s, do_s = _sub(qb, s), _sub(kib, s), _sub(vb, s), _sub(dob, s)
                dspb = _stack(dsp, s).astype(BF16)
                da = jnp.where(mask, _dot_nt(do_s, v_s), 0.0).astype(BF16)
                dat = jnp.where(maskT, _dot_nt(v_s, do_s), 0.0).astype(BF16)
                at = jnp.where(maskT, _dot_nt(ki_s, q_s), 0.0).astype(BF16)
                dqds.append(_dot(da, ki_s) + _own(_dot(do_s, _stack(states, s).astype(BF16)), mexp))
                dkis.append(_dot(dat, q_s))
                dv_ref[0, p, s * SB:(s + 1) * SB, :] = (_dot(at, do_s) + _dot_nt(kt_exps[s], dspb)).astype(BF16)
                dkts.append(_own(_dot(v_s, dspb), mexp))
            dqd, dki, dkt = (jnp.concatenate(parts, axis=0) for parts in (dqds, dkis, dkts))
            z = dkt * kt
            db = dqd * qd - dki * ki
            dq_ref[0, p] = (dqd * eb).astype(BF16)
            dk_ref[0, p] = (dki * jnp.exp(-b) + dkt * jnp.exp(xx)).astype(BF16)
            dg_ref[0, p] = (_chunk_total(db) + (db - z) + _chunk_scan(z - db, rin, not rev)
                            + jnp.concatenate(t2, axis=0)).astype(BF16)

    one = pl.BlockSpec((1, HP, TM, HD), lambda b, h, t: (b, h, rmap(t), 0))
    two = pl.BlockSpec((1, 1, HP, TM, HD), lambda b, h, t: (d, b, h, rmap(t), 0))
    lat = pl.BlockSpec((1, HP, TM, HD), lambda b, h, t: (b, h, jnp.clip(rmap(t) - 1, 0, nb - 2), 0))
    shp = jax.ShapeDtypeStruct(Q.shape, BF16)
    return pl.pallas_call(
        body, name="scan_bwd_rev" if rev else "scan_bwd", grid=(B, NH // HP, nb),
        in_specs=[one, two, one, two, pl.BlockSpec((1, HP, 1, HD, HD), lambda b, h, t: (b, h, smap(t), 0, 0)), lat],
        out_specs=[one, one, one, one], out_shape=[shp, shp, shp, shp],
        scratch_shapes=[pltpu.VMEM((HP, HD, HD), F32)],
        compiler_params=_cp(3))(Q, K, V, G, St, dO)


def _gnorm(O0, O1, F, gains, B, ntl):
    nt = ntl + 1

    def body(o0_ref, o1_ref, f_ref, gn_ref, m_ref):
        for h in range(NH):
            o = o0_ref[0, h] + o1_ref[0, h]
            r = lax.rsqrt(jnp.mean(o * o, axis=-1, keepdims=True) + NORM_EPS)
            gn = gn_ref[0:1, :] if h < 4 else gn_ref[1:2, :]
            gt = f_ref[:, h * HD:(h + 1) * HD]
            m_ref[:, h * HD:(h + 1) * HD] = (o * r * gn * (gt * _sigmoid(gt))).astype(BF16)

    ospec = pl.BlockSpec((1, NH, TM, HD), lambda b, i: (b, 0, i + 1, 0))
    return pl.pallas_call(
        body, name="gated_norm", grid=(B, ntl),
        in_specs=[ospec, ospec, pl.BlockSpec((TM, D), lambda b, i: (b * nt + 1 + i, 0)),
                  pl.BlockSpec((2, HD), lambda b, i: (0, 0))],
        out_specs=pl.BlockSpec((TM, D), _rowmap(ntl, 0)),
        out_shape=jax.ShapeDtypeStruct((B * ntl * TM, D), BF16), compiler_params=_cp(2))(O0, O1, F, gains)


def _gnorm_bwd(dM, O0, O1, F, gains, B, ntl):
    nt = ntl + 1

    def body(dm_ref, o0_ref, o1_ref, f_ref, gn_ref, do_ref, dgt_ref, dgn_ref):
        b_, i = pl.program_id(0), pl.program_id(1)

        @pl.when((b_ == 0) & (i == 0))
        def _():
            dgn_ref[...] = jnp.zeros_like(dgn_ref)

        for h in range(NH):
            o = o0_ref[0, h] + o1_ref[0, h]
            r = lax.rsqrt(jnp.mean(o * o, axis=-1, keepdims=True) + NORM_EPS)
            y = o * r
            gn = gn_ref[0:1, :] if h < 4 else gn_ref[1:2, :]
            gt = f_ref[:, h * HD:(h + 1) * HD]
            s = _sigmoid(gt)
            dm = dm_ref[:, h * HD:(h + 1) * HD]
            don = dm * (gt * s)
            dgt_ref[:, h * HD:(h + 1) * HD] = dm * (y * gn) * (s * (1.0 + gt * (1.0 - s)))
            row = 0 if h < 4 else 1
            dgn_ref[row:row + 1, :] += _rsum(don * y)
            dy = don * gn
            do_ref[0, h] = r * (dy - y * jnp.mean(dy * y, axis=-1, keepdims=True))

    ospec = pl.BlockSpec((1, NH, TM, HD), lambda b, i: (b, 0, i + 1, 0))
    return pl.pallas_call(
        body, name="gated_norm_bwd", grid=(B, ntl),
        in_specs=[pl.BlockSpec((TM, D), _rowmap(ntl, 0)), ospec, ospec,
                  pl.BlockSpec((TM, D), lambda b, i: (b * nt + 1 + i, 0)), pl.BlockSpec((2, HD), lambda b, i: (0, 0))],
        out_specs=[pl.BlockSpec((1, NH, TM, HD), lambda b, i: (b, 0, i, 0)), pl.BlockSpec((TM, D), _rowmap(ntl, 0)),
                   pl.BlockSpec((2, HD), lambda b, i: (0, 0))],
        out_shape=[jax.ShapeDtypeStruct((B, NH, ntl * TM, HD), F32), jax.ShapeDtypeStruct((B * ntl * TM, D), F32),
                   jax.ShapeDtypeStruct((2, HD), F32)],
        compiler_params=_cp(2))(dM, O0, O1, F, gains)


def _sincos_2d(rows, width, dim):
    quarter = dim // 4
    omega = 1.0 / 10000.0 ** (jnp.arange(quarter, dtype=F32) / quarter)

    def emb(n):
        a = jnp.arange(n).astype(F32)[:, None] * omega[None, :]
        return jnp.concatenate([jnp.sin(a), jnp.cos(a)], axis=-1)

    er = jnp.broadcast_to(emb(rows)[:, None, :], (rows, width, dim // 2))
    ec = jnp.broadcast_to(emb(width)[None, :, :], (rows, width, dim // 2))
    return jnp.concatenate([er, ec], axis=-1).reshape(rows * width, dim)


def _pad_heads(w):
    k = w.shape[0]
    return jnp.pad(w.reshape(k, 4, GLA_DK), ((0, 0), (0, 0), (0, HD - GLA_DK))).reshape(k, 4 * HD)


def _unpad_heads(w):
    k = w.shape[0]
    return w.reshape(k, 4, HD)[:, :, :GLA_DK].reshape(k, 4 * GLA_DK)


MIX_N = 1032
MIX_NP = 1152
_SEGS = ([(64 * h, 64, GQ + HD * h) for h in range(4)] + [(256 + 64 * h, 64, GK + HD * h) for h in range(4)]
         + [(512, 512, GV), (1024, 512, GG), (1536, 32, LR), (1568, 512, RQ), (2080, 512, RFF), (2592, 512, RFB),
            (3104, 512, RI), (3616, 512, RG)])


def _mix_in_to_padded(ps):
    k = ps.shape[1]
    parts, pos = [], 0
    for g0, ln, s0 in sorted(_SEGS, key=lambda s: s[2]):
        if s0 > pos:
            parts.append(jnp.zeros((k, s0 - pos), ps.dtype))
        for j in range(4):
            lo, hi = max(g0, j * MIX_N), min(g0 + ln, (j + 1) * MIX_N)
            if lo < hi:
                parts.append(ps[j][:, lo - j * MIX_N:hi - j * MIX_N])
        pos = s0 + ln
    parts.append(jnp.zeros((k, MIXP - pos), ps.dtype))
    return jnp.concatenate(parts, axis=1)


def _mix_in_from_padded(g):
    k = g.shape[0]
    shards = []
    for j in range(4):
        parts = []
        for g0, ln, s0 in sorted(_SEGS):
            lo, hi = max(g0, j * MIX_N), min(g0 + ln, (j + 1) * MIX_N)
            if lo < hi:
                parts.append(g[:, s0 + lo - g0:s0 + hi - g0])
        parts.append(jnp.zeros((k, MIX_NP - MIX_N), g.dtype))
        shards.append(jnp.concatenate(parts, axis=1))
    return jnp.stack(shards)


def _local_step(x, ctx, tgt, mvec, weights_for, ln_gain, ln_bias, a2f, a2b, abf, abb, lb, gng, gnh, on_grads, on_sent):
    B, T, _ = x.shape
    assert ctx.shape[1] == TM and T % TM == 0
    ntl = T // TM
    nt = ntl + 1
    C, L, CL = (nt, 0), (ntl, 0), (nt, 1)
    pos = _sincos_2d(T // 64, 64, D)
    gbs = [jnp.stack([ln_gain[i], ln_bias[i]]) for i in range(3)]
    a2p = jnp.zeros((2, HD, 4 * HD), F32)
    a2p = a2p.at[0, 0:16].set(_pad_heads(a2f)).at[1, 16:32].set(_pad_heads(a2b)).astype(BF16)
    biasp = jnp.concatenate([_pad_heads(abf.reshape(1, -1)), _pad_heads(abb.reshape(1, -1))], axis=0)
    gains = jnp.concatenate([gng.reshape(1, HD), gnh.reshape(1, HD)], axis=0)

    X0, h0 = _embed_lnmod(x, ctx, pos, mvec)
    w1i, w1o = weights_for("ffn1", h0)
    u0, a0 = _ffn_in(h0, C, w1i, B, nt, "ffn1_in")
    wmp, wmo = weights_for("mix", a0)
    y0 = _mm_nn(a0, C, w1o, B, nt, F32, "ffn1_out")
    X1, h1 = _resid_lnmod(X0, C, y0, mvec, True, 0, 0.5, gbs[0], B, nt, "resid0_lnmod1")
    Fm = _mm_nn(h1, C, wmp, B, nt, F32, "mix_in")
    w2i, w2o = weights_for("ffn2", Fm)
    Q, K, V, G = _features(Fm, a2p, biasp, lb, B)
    O0, S0 = _scan_fwd(Q, K, V, G, False, B)
    O1, S1 = _scan_fwd(Q, K, V, G, True, B)
    merged = _gnorm(O0, O1, Fm, gains, B, ntl)
    y1 = _mm_nn(merged, L, wmo, B, ntl, F32, "mix_out")
    X2, h2 = _resid_lnmod(X1, CL, y1, mvec, False, 1, 1.0, gbs[1], B, ntl, "resid1_lnmod2")
    u2, a2 = _ffn_in(h2, L, w2i, B, ntl, "ffn2_in")
    y2 = _mm_nn(a2, L, w2o, B, ntl, F32, "ffn2_out")

    dx2r, dy2, dgate2, dgb2, lsum = _tail(X2, y2, mvec, gbs[2], tgt, 2, 0.5, B, ntl)
    loss = (0.5 / D) * jnp.sum(lsum)
    du2 = _ffn_out_dx(dy2, w2o, u2, "ffn2_out_dx")
    g_w2o = _mm_tn(a2, dy2, "ffn2_out_dw")
    dh2 = _mm_nt(du2, w2i, "ffn2_in_dx")
    g_w2i = _mm_tn(h2, du2, "ffn2_in_dw", shards=4)
    tok = on_grads("ffn2", (g_w2i, g_w2o))
    dx1r, dy1, dgate1, dgb1, dss2 = _lnmod_resid_bwd(dh2, X2, L, dx2r, X1, CL, y1, mvec, False, 2, 1.0,
                                                     gbs[1] + tok[0, 0], B, ntl, "lnmod2_resid1_bwd")
    tok = on_sent("ffn2", dy1)
    dmerged = _mm_nt(dy1, wmo, "mix_out_dx", dep=tok)
    g_wmo = _mm_tn(merged, dy1, "mix_out_dw", tn=D)
    dO, dgates, dgains = _gnorm_bwd(dmerged, O0, O1, Fm, gains, B, ntl)
    dQ0, dK0, dV0, dG0 = _scan_bwd(Q, K, V, G, S0, dO, False, B)
    dQ1, dK1, dV1, dG1 = _scan_bwd(Q, K, V, G, S1, dO, True, B)
    dF, da2p, dbiasp, dlb = _features_bwd(Fm, a2p, biasp, lb, dQ0, dQ1, dK0, dK1, dV0, dV1, dG0, dG1, dgates, B)
    dh1 = _mm_nt(dF, wmp, "mix_in_dx")
    g_wmp = _mm_tn(h1, dF, "mix_in_dw", tn=MIXP // 4)
    tok = on_grads("mix", (g_wmp, g_wmo))
    dx0r, dy0, dgate0, dgb0, dss1 = _lnmod_resid_bwd(dh1, X1, C, dx1r, X0, C, y0, mvec, True, 1, 0.5,
                                                     gbs[0] + tok[0, 0], B, nt, "lnmod1_resid0_bwd")
    tok = on_sent("mix", dy0)
    du0 = _ffn_out_dx(dy0, w1o, u0, "ffn1_out_dx", dep=tok)
    g_w1o = _mm_tn(a0, dy0, "ffn1_out_dw")
    g_w1i = _mm_tn(h0, du0, "ffn1_in_dw", shards=4)
    tok = on_grads("ffn1", (g_w1i, g_w1o))
    dh0 = _mm_nt(du0, w1i, "ffn1_in_dx", dep=tok)
    tok = on_sent("ffn1", dh0)
    grad_x, dss0 = _lnmod0_bwd(dh0, X0, mvec + tok[0, 0], dx0r, x.shape, B, nt)

    zero_ctx = lambda a: a.at[:, 0].set(0.0)
    dm = jnp.concatenate([dss0, dgate0, dss1, zero_ctx(dgate1), zero_ctx(dss2), zero_ctx(dgate2)], axis=2)
    small = dict(
        ln_gain=jnp.stack([dgb0[0], dgb1[0], dgb2[0]]), ln_bias=jnp.stack([dgb0[1], dgb1[1], dgb2[1]]),
        a2f=_unpad_heads(da2p[0, 0:16]), a2b=_unpad_heads(da2p[1, 16:32]),
        abf=_unpad_heads(dbiasp[0:1]), abb=_unpad_heads(dbiasp[1:2]), lb=dlb, gng=dgains[0], gnh=dgains[1])
    return loss, grad_x, dm, small


def _small_allgather(xs, name):
    r, n = xs.shape

    def body(x_ref, out_ref, send_sems, recv_sems, local_sem):
        x, y, c = lax.axis_index("x"), lax.axis_index("y"), lax.axis_index("c")
        me, sibling = (x, y, c), (x, y, 1 - c)
        chips = [(1 - x, y), (x, 1 - y), (1 - x, 1 - y)]

        def rows(px, py, pc):
            return out_ref.at[pl.ds((4 * px + 2 * py + pc) * r, r), :]

        def copy(k, block, to, src=None):
            return pltpu.make_async_remote_copy(
                src_ref=rows(*block) if src is None else src, dst_ref=rows(*block),
                send_sem=send_sems.at[k], recv_sem=recv_sems.at[k], device_id=to, device_id_type=MESH)

        mine = pltpu.make_async_copy(x_ref, rows(*me), local_sem)
        mine.start()
        first = [copy(0, me, sibling, src=x_ref)]
        first += [copy(1 + j, me, (*chip, c), src=x_ref) for j, chip in enumerate(chips)]
        for cp in first:
            cp.start()
        passed = [copy(4 + j, (*chip, c), sibling) for j, chip in enumerate(chips)]
        for j, chip in enumerate(chips):
            copy(1 + j, (*chip, c), me).wait_recv()
            passed[j].start()
        copy(0, sibling, me).wait_recv()
        for j, chip in enumerate(chips):
            copy(4 + j, (*chip, 1 - c), me).wait_recv()
        for cp in first + passed:
            cp.wait_send()
        mine.wait()

    out = pl.pallas_call(
        body, name=name,
        out_shape=jax.ShapeDtypeStruct((8 * r, n), xs.dtype),
        in_specs=[pl.BlockSpec(memory_space=pltpu.VMEM)],
        out_specs=pl.BlockSpec(memory_space=pltpu.VMEM),
        scratch_shapes=[pltpu.SemaphoreType.DMA((7,)), pltpu.SemaphoreType.DMA((7,)), pltpu.SemaphoreType.DMA],
        compiler_params=pltpu.CompilerParams(vmem_limit_bytes=VMEM_LIMIT))(xs)
    return out.reshape(8, r, n)


def _gather_flat(v, name):
    n = v.shape[0]
    npad = -(-n // 1024) * 1024
    g = _small_allgather(jnp.pad(v, (0, npad - n)).reshape(8, npad // 8), name)
    return g.reshape(8, npad)[:, :n]


HBM_SPEC = pl.BlockSpec(memory_space=pltpu.HBM)
SEM_SPEC = pl.BlockSpec(memory_space=pltpu.SEMAPHORE)
DATAFLOW = pltpu.SideEffectType.DATAFLOW_SIDE_EFFECTING


def _gather_copies(xs, outs, send_sems, recv_sems):
    x, y, c = lax.axis_index("x"), lax.axis_index("y"), lax.axis_index("c")
    dests = [(x, y, 1 - c), (1 - x, y, c), (x, 1 - y, c), (1 - x, 1 - y, c)]
    return [pltpu.make_async_remote_copy(
        src_ref=xs[w], dst_ref=outs[w].at[4 * x + 2 * y + c], send_sem=send_sems[4 * w + k],
        recv_sem=recv_sems[4 * w + k], device_id=dests[k], device_id_type=MESH)
        for w in range(len(xs)) for k in range(4)]


def _split_start(copies, per_w, srcs, land_lead, after, name):
    n = len(srcs)
    m = per_w * n
    lands = [lax.empty((land_lead,) + s.shape[-2:], s.dtype) for s in srcs]
    deps = [] if after is None else [after]

    def body(*refs):
        xs, ls, outs = refs[:n], refs[n:2 * n], refs[2 * n + len(deps):]
        for cp in copies(xs, ls, outs[:m], outs[m:2 * m]):
            cp.start()
        token = outs[2 * m + 2 * n]
        token[...] = jnp.zeros_like(token)

    outs = pl.pallas_call(
        body, name=name,
        out_shape=([pltpu.SemaphoreType.DMA(())] * (2 * m) + [pltpu.HBM(a.shape, a.dtype) for a in srcs + lands]
                   + [jax.ShapeDtypeStruct((8, 128), F32)]),
        in_specs=[HBM_SPEC] * (2 * n) + [pl.BlockSpec(memory_space=pl.ANY)] * len(deps),
        out_specs=[SEM_SPEC] * (2 * m) + [HBM_SPEC] * (2 * n) + [pl.BlockSpec(memory_space=pltpu.VMEM)],
        input_output_aliases={w: 2 * m + w for w in range(2 * n)},
        compiler_params=pltpu.CompilerParams(has_side_effects=DATAFLOW),
    )(*[pltpu.with_memory_space_constraint(a, pltpu.HBM) for a in srcs + lands], *deps)
    return outs[:2 * m], outs[2 * m:2 * m + n], outs[2 * m + n:2 * m + 2 * n], outs[2 * m + 2 * n]


def _split_wait(copies, per_w, sems, x_thru, l_thru, after, name):
    n = len(x_thru)
    m = per_w * n

    def body(*refs):
        xs, ls, ss = refs[:n], refs[n:2 * n], refs[2 * n:2 * n + 2 * m]
        for cp in copies(xs, ls, ss[:m], ss[m:]):
            cp.wait_send()
            cp.wait_recv()

    outs = pl.pallas_call(
        body, name=name,
        out_shape=[pltpu.HBM(a.shape, a.dtype) for a in list(x_thru) + list(l_thru)],
        in_specs=[HBM_SPEC] * (2 * n) + [SEM_SPEC] * (2 * m) + [pl.BlockSpec(memory_space=pl.ANY)],
        out_specs=[HBM_SPEC] * (2 * n),
        input_output_aliases={w: w for w in range(2 * n)},
        compiler_params=pltpu.CompilerParams(has_side_effects=DATAFLOW),
    )(*x_thru, *l_thru, *sems, after)
    return outs[:n], outs[n:]


def _gather_forward(gathered, name):
    n = len(gathered)

    def body(*refs):
        outs = refs[n:2 * n]
        send_sems, recv_sems = refs[2 * n:]
        x, y, c = lax.axis_index("x"), lax.axis_index("y"), lax.axis_index("c")
        chips = [(1 - x, y), (x, 1 - y), (1 - x, 1 - y), (x, y)]

        def copy(w, j, pc):
            px, py = chips[j]
            slot = outs[w].at[4 * px + 2 * py + pc]
            return pltpu.make_async_remote_copy(
                src_ref=slot, dst_ref=slot, send_sem=send_sems.at[4 * w + j], recv_sem=recv_sems.at[4 * w + j],
                device_id=(x, y, 1 - c), device_id_type=MESH)

        sends = [copy(w, j, 1 - c if j == 3 else c) for w in range(n) for j in range(4)]
        for cp in sends:
            cp.start()
        for w in range(n):
            for j in range(4):
                copy(w, j, c if j == 3 else 1 - c).wait_recv()
        for cp in sends:
            cp.wait_send()

    any_spec = pl.BlockSpec(memory_space=pl.ANY)
    return pl.pallas_call(
        body, name=name,
        out_shape=[jax.ShapeDtypeStruct(g.shape, g.dtype) for g in gathered],
        in_specs=[any_spec] * n, out_specs=[any_spec] * n,
        input_output_aliases={w: w for w in range(n)},
        scratch_shapes=[pltpu.SemaphoreType.DMA((4 * n,)), pltpu.SemaphoreType.DMA((4 * n,))],
    )(*gathered)


def _pair_copies(gs, ls, send_sems, recv_sems):
    x, y, c = lax.axis_index("x"), lax.axis_index("y"), lax.axis_index("c")
    return [pltpu.make_async_remote_copy(
        src_ref=gs[w].at[2 * j + 1 - c], dst_ref=ls[w].at[j], send_sem=send_sems[4 * w + j],
        recv_sem=recv_sems[4 * w + j], device_id=(x, y, 1 - c), device_id_type=MESH)
        for w in range(len(gs)) for j in range(4)]


def _chip_copies(hs, ls, send_sems, recv_sems):
    x, y, c = lax.axis_index("x"), lax.axis_index("y"), lax.axis_index("c")
    chips = [(1 - x, y), (x, 1 - y), (1 - x, 1 - y)]
    return [pltpu.make_async_remote_copy(
        src_ref=hs[w].at[2 * px + py], dst_ref=ls[w].at[k], send_sem=send_sems[3 * w + k],
        recv_sem=recv_sems[3 * w + k], device_id=(px, py, c), device_id_type=MESH)
        for w in range(len(hs)) for k, (px, py) in enumerate(chips)]


def _rs_pair_share(fins, name):
    n = len(fins)

    def body(*refs):
        outs = refs[n:2 * n]
        send_sems, recv_sems = refs[2 * n:]
        x, y, c = lax.axis_index("x"), lax.axis_index("y"), lax.axis_index("c")

        def copy(w, slot):
            return pltpu.make_async_remote_copy(
                src_ref=outs[w].at[slot], dst_ref=outs[w].at[slot], send_sem=send_sems.at[w], recv_sem=recv_sems.at[w],
                device_id=(x, y, 1 - c), device_id_type=MESH)

        sends = [copy(w, c) for w in range(n)]
        for cp in sends:
            cp.start()
        for w in range(n):
            copy(w, 1 - c).wait_recv()
        for cp in sends:
            cp.wait_send()

    any_spec = pl.BlockSpec(memory_space=pl.ANY)
    return pl.pallas_call(
        body, name=name,
        out_shape=[jax.ShapeDtypeStruct(f.shape, f.dtype) for f in fins],
        in_specs=[any_spec] * n, out_specs=[any_spec] * n,
        input_output_aliases={w: w for w in range(n)},
        scratch_shapes=[pltpu.SemaphoreType.DMA((n,)), pltpu.SemaphoreType.DMA((n,))])(*fins)


def _rs_add_pair(g8, r4, c, name):
    R, n = g8.shape[1:]
    rb = R // 2

    def body(c_ref, g_ref, r_ref, o_ref):
        o_ref[...] = (g_ref[...] + r_ref[...]).astype(BF16)

    spec = pl.BlockSpec((1, rb, n), lambda j, i, c_ref: (j, i, 0))
    return pl.pallas_call(
        body, name=name,
        grid_spec=pltpu.PrefetchScalarGridSpec(
            num_scalar_prefetch=1, grid=(4, R // rb),
            in_specs=[pl.BlockSpec((1, rb, n), lambda j, i, c_ref: (2 * j + c_ref[0], i, 0)), spec],
            out_specs=spec),
        out_shape=jax.ShapeDtypeStruct((4, R, n), BF16), compiler_params=_cp(2))(c, g8, r4)


def _rs_add_chips(g8, r4, r3, cj, name):
    R, n = g8.shape[1:]
    rb = R // 2

    def body(cj_ref, g_ref, p_ref, r_ref, o_ref):
        own = g_ref[0] + p_ref[0]
        o_ref[0] = ((own + r_ref[0].astype(F32)) + r_ref[1].astype(F32)) + r_ref[2].astype(F32)

    return pl.pallas_call(
        body, name=name,
        grid_spec=pltpu.PrefetchScalarGridSpec(
            num_scalar_prefetch=1, grid=(R // rb,),
            in_specs=[pl.BlockSpec((1, rb, n), lambda i, cj_ref: (2 * cj_ref[1] + cj_ref[0], i, 0)),
                      pl.BlockSpec((1, rb, n), lambda i, cj_ref: (cj_ref[1], i, 0)),
                      pl.BlockSpec((3, rb, n), lambda i, cj_ref: (0, i, 0))],
            out_specs=pl.BlockSpec((1, rb, n), lambda i, cj_ref: (cj_ref[0], i, 0))),
        out_shape=jax.ShapeDtypeStruct((2, R, n), F32), compiler_params=_cp(1))(cj, g8, r4, r3)


def _sum8(g):
    n = g.shape[1]

    def body(g_ref, o_ref):
        acc = g_ref[0:1, :]
        for k in range(1, 8):
            acc = acc + g_ref[k:k + 1, :]
        o_ref[...] = acc

    return pl.pallas_call(body, name="sum_devices", out_shape=jax.ShapeDtypeStruct((1, n), F32),
                          compiler_params=pltpu.CompilerParams(vmem_limit_bytes=VMEM_LIMIT))(g)


ADA_ROWS = 64


def _ada_fwd(cs, w, b):
    n = w.shape[1]

    def body(c_ref, w_ref, b_ref, o_ref):
        cv = c_ref[...]
        s = (cv * _sigmoid(cv)).astype(BF16)
        o_ref[...] = _dot(s, w_ref[...].astype(BF16)) + b_ref[...]

    return pl.pallas_call(body, name="ada_fwd", out_shape=jax.ShapeDtypeStruct((ADA_ROWS, n), F32),
                          compiler_params=pltpu.CompilerParams(vmem_limit_bytes=VMEM_LIMIT))(cs, w, b)


def _ada_bwd(cs, w, dm):
    n = w.shape[1]

    def body(c_ref, w_ref, dm_ref, gw_ref, dc_ref):
        cv = c_ref[...]
        s = (cv * _sigmoid(cv)).astype(BF16)
        gw_ref[...] = _dot_tn(s, dm_ref[...].astype(BF16))
        dc_ref[...] = _dot_nt(dm_ref[32:40, :].astype(BF16), w_ref[...].astype(BF16))

    return pl.pallas_call(
        body, name="ada_bwd",
        out_shape=[jax.ShapeDtypeStruct((D, n), F32), jax.ShapeDtypeStruct((8, D), F32)],
        compiler_params=pltpu.CompilerParams(vmem_limit_bytes=VMEM_LIMIT))(cs, w, dm)


def _adamw(w, g, m, v, name):
    r, c = w.shape
    rb = r
    if r % 8 == 0 and r * c * 4 > (1 << 20):
        rb = 8
        for cand in range(8, r, 8):
            if r % cand == 0 and cand * c * 4 <= (1 << 20):
                rb = cand

    def body(w_ref, g_ref, m_ref, v_ref, d_ref, nm_ref, nv_ref):
        gv = g_ref[...]
        mn = ADAM_B1 * m_ref[...] + (1.0 - ADAM_B1) * gv
        vn = ADAM_B2 * v_ref[...] + (1.0 - ADAM_B2) * (gv * gv)
        m_hat = mn / (1.0 - ADAM_B1 ** ADAM_STEP)
        v_hat = vn / (1.0 - ADAM_B2 ** ADAM_STEP)
        d_ref[...] = -ADAM_LR * (m_hat / (jnp.sqrt(v_hat) + ADAM_EPS) + ADAM_WD * w_ref[...])
        nm_ref[...] = mn
        nv_ref[...] = vn

    spec = pl.BlockSpec((rb, c), lambda i: (i, 0))
    shp = jax.ShapeDtypeStruct((r, c), F32)
    return pl.pallas_call(body, name=name, grid=(r // rb,), in_specs=[spec] * 4, out_specs=[spec] * 3,
                          out_shape=[shp] * 3, compiler_params=_cp(1))(w, g, m, v)


BIG = ("ffn1_w_in", "ffn1_w_out", "w_mix_in", "w_mix_out", "ffn2_w_in", "ffn2_w_out")


def _half_rows(w, c):
    half = w.shape[0] // 2
    return lax.dynamic_slice_in_dim(w, c * half, half, axis=0)


def _lower_bounds(logits):
    return jnp.cumsum(jax.nn.softmax(logits.astype(F32), axis=1), axis=1)[:, 0]


def kernel(x, c, ctx, c_ctx, w_ada, b_ada, ln_gain, ln_bias, ffn1_w_in, ffn1_w_out, w_mix_in, gla_a2_fwd, gla_a2_bwd, gla_a_bias_fwd, gla_a_bias_bwd, hgrn_lb_logits, gla_norm_gain, hgrn_norm_gain, w_mix_out, ffn2_w_in, ffn2_w_out, loss_target, m_c_ctx, m_w_ada, m_b_ada, m_ln_gain, m_ln_bias, m_ffn1_w_in, m_ffn1_w_out, m_w_mix_in, m_gla_a2_fwd, m_gla_a2_bwd, m_gla_a_bias_fwd, m_gla_a_bias_bwd, m_hgrn_lb_logits, m_gla_norm_gain, m_hgrn_norm_gain, m_w_mix_out, m_ffn2_w_in, m_ffn2_w_out, v_c_ctx, v_w_ada, v_b_ada, v_ln_gain, v_ln_bias, v_ffn1_w_in, v_ffn1_w_out, v_w_mix_in, v_gla_a2_fwd, v_gla_a2_bwd, v_gla_a_bias_fwd, v_gla_a_bias_bwd, v_hgrn_lb_logits, v_gla_norm_gain, v_hgrn_norm_gain, v_w_mix_out, v_ffn2_w_in, v_ffn2_w_out):
    xi, yi, ci = lax.axis_index("x"), lax.axis_index("y"), lax.axis_index("c")
    chip = 2 * xi + yi
    dev = 2 * chip + ci
    B = x.shape[0]
    weights = dict(ffn1_w_in=ffn1_w_in[0], ffn1_w_out=ffn1_w_out[0], w_mix_in=w_mix_in[0], w_mix_out=w_mix_out[0],
                   ffn2_w_in=ffn2_w_in[0], ffn2_w_out=ffn2_w_out[0])

    mine = jnp.concatenate([c.reshape(-1), ln_gain.reshape(-1), ln_bias.reshape(-1), gla_a2_fwd.reshape(-1),
                            gla_a2_bwd.reshape(-1), hgrn_lb_logits.reshape(-1)])
    g1 = _gather_flat(mine, "gather_cond")
    nc = B * D
    c_all = g1[:, :nc].reshape(8 * B, D)
    per_chip = g1[0::2, nc:]
    o = 0

    def take(shape, axis):
        nonlocal o
        n = int(np.prod(shape))
        parts = per_chip[:, o:o + n].reshape((4,) + shape)
        o += n
        return jnp.concatenate([parts[j] for j in range(4)], axis=axis)

    ln_gain_f = take((3, 256), 1)
    ln_bias_f = take((3, 256), 1)
    a2f_f = take((16, 64), 1)
    a2b_f = take((16, 64), 1)
    lbl_f = take((2, 2, 128), 2)
    lb, lb_vjp = jax.vjp(_lower_bounds, lbl_f)

    cs = jnp.concatenate([c_all, c_ctx.reshape(1, D), jnp.zeros((ADA_ROWS - 8 * B - 1, D), F32)], axis=0)
    ncol = w_ada.shape[2]
    b_cols = lax.dynamic_slice_in_dim(b_ada, chip * ncol, ncol, axis=1)
    m_cols = _ada_fwd(cs, w_ada[0], b_cols)
    g2 = _small_allgather(m_cols, "gather_mod")[0::2]
    m_all = jnp.concatenate([g2[j] for j in range(4)], axis=1)
    m_lat = lax.dynamic_slice_in_dim(m_all, dev * B, B, axis=0).reshape(B, 1, N_MOD, D)
    m_ctx = jnp.broadcast_to(m_all[8 * B].reshape(1, 1, N_MOD, D), (B, 1, N_MOD, D))

    groups = dict(ffn2=("ffn2_w_in", "ffn2_w_out"), mix=("w_mix_in", "w_mix_out"), ffn1=("ffn1_w_in", "ffn1_w_out"))
    shards = dict(weights, w_mix_in=jnp.pad(weights["w_mix_in"], ((0, 0), (0, MIX_NP - MIX_N))))
    blks = {k: _half_rows(shards[k], ci).astype(BF16) for k in BIG}
    gathering = {}
    token = m_all
    for group in ("ffn1", "mix", "ffn2"):
        sems, x_thru, l_thru, token = _split_start(_gather_copies, 4, [blks[k] for k in groups[group]], 8, token,
                                                   "weight_gather_start_" + group)
        gathering[group] = (sems, x_thru, l_thru)
    mvec = jnp.concatenate([m_ctx, m_lat], axis=1) + token[0, 0]

    def weights_for(group, after):
        names = groups[group]
        _, got = _split_wait(_gather_copies, 4, *gathering[group], after, "weight_gather_wait_" + group)
        w_in, w_out = _gather_forward(got, "weight_gather_forward_" + group)
        if group == "mix":
            return _mix_in_to_padded(w_in.reshape(4, D, MIX_NP)), w_out.reshape(-1, D)
        return w_in.reshape((4,) + shards[names[0]].shape), w_out.reshape(-1, D)

    cvec = ci.reshape(1).astype(jnp.int32)
    cjvec = jnp.stack([ci, chip]).astype(jnp.int32)
    in_flight = {}

    def on_grads(group, gs):
        names = groups[group]
        if group == "mix":
            gs = (_mix_in_from_padded(gs[0]), gs[1])
        g8s = [g.reshape((8, shards[k].shape[0] // 2, shards[k].shape[1])) for k, g in zip(names, gs)]
        sems, g_thru, l_thru, token = _split_start(_pair_copies, 4, g8s, 4, None, "grad_pair_start_" + group)
        in_flight[group] = (sems, g_thru, l_thru)
        return token

    def on_sent(group, after):
        g8s, r4s = _split_wait(_pair_copies, 4, *in_flight[group], after, "grad_pair_wait_" + group)
        h4s = [_rs_add_pair(g, r, cvec, "grad_pair_add_" + k) for k, g, r in zip(groups[group], g8s, r4s)]
        sems, h_thru, l_thru, token = _split_start(_chip_copies, 3, h4s, 3, None, "grad_chip_start_" + group)
        in_flight[group] = (g8s, r4s, sems, h_thru, l_thru)
        return token

    loss_l, grad_x, dm, small = _local_step(
        x, ctx, loss_target, mvec, weights_for, ln_gain_f, ln_bias_f, a2f_f, a2b_f,
        gla_a_bias_fwd, gla_a_bias_bwd, lb, gla_norm_gain, hgrn_norm_gain, on_grads, on_sent)
    loss = lax.psum(loss_l, ("x", "y", "c"))

    dm_lat = dm[:, 1].reshape(B, N_MOD * D)
    dm_ctx = jnp.sum(dm[:, 0], axis=0).reshape(N_MOD * D)
    keys = ("ln_gain", "ln_bias", "a2f", "a2b", "abf", "abb", "lb", "gng", "gnh")
    flat = jnp.concatenate([dm_lat.reshape(-1), dm_ctx] + [small[k].reshape(-1) for k in keys])
    g3 = _gather_flat(flat, "gather_small_grads")
    nlat = B * N_MOD * D
    dm_all = g3[:, :nlat].reshape(8 * B, N_MOD * D)
    tot = _sum8(g3[:, nlat:])[0]
    dmc_tot = tot[:N_MOD * D]
    o = N_MOD * D
    sg = {}
    for k in keys:
        n = int(np.prod(small[k].shape))
        sg[k] = tot[o:o + n].reshape(small[k].shape)
        o += n
    dm_rows = jnp.concatenate([dm_all, dmc_tot.reshape(1, -1), jnp.zeros((ADA_ROWS - 8 * B - 1, N_MOD * D), F32)], axis=0)
    g_b_ada = (jnp.sum(dm_all, axis=0) + dmc_tot).reshape(1, N_MOD * D)
    g_w_ada, dcc = _ada_bwd(cs, w_ada[0], lax.dynamic_slice_in_dim(dm_rows, chip * ncol, ncol, axis=1))
    g4 = _gather_flat(dcc[0], "gather_cctx")
    dsilu = ((g4[0] + g4[2]) + g4[4]) + g4[6]
    sc = _sigmoid(c_ctx)
    g_c_ctx = dsilu * (sc * (1.0 + c_ctx * (1.0 - sc)))
    (g_lbl,) = lb_vjp(sg["lb"])

    def cols(a, n, axis):
        return lax.dynamic_slice_in_dim(a, chip * n, n, axis=axis)

    fin = {}
    for group in ("ffn2", "mix", "ffn1"):
        g8s, r4s, sems, h_thru, l_thru = in_flight[group]
        _, r3s = _split_wait(_chip_copies, 3, sems, h_thru, l_thru, g_w_ada, "grad_chip_wait_" + group)
        for k, g, r4, r3 in zip(groups[group], g8s, r4s, r3s):
            fin[k] = _rs_add_chips(g, r4, r3, cjvec, "grad_chip_add_" + k)
    fins = [fin[k] for k in BIG]
    gsh = {k: both.reshape(shards[k].shape)[:, :weights[k].shape[1]]
           for k, both in zip(BIG, _rs_pair_share(fins, "grad_pair_share"))}

    grads = dict(
        c_ctx=g_c_ctx, w_ada=g_w_ada[None], b_ada=g_b_ada, ln_gain=cols(sg["ln_gain"], 256, 1)[None],
        ln_bias=cols(sg["ln_bias"], 256, 1)[None], ffn1_w_in=gsh["ffn1_w_in"][None], ffn1_w_out=gsh["ffn1_w_out"][None],
        w_mix_in=gsh["w_mix_in"][None], gla_a2_fwd=cols(sg["a2f"], 64, 1)[None], gla_a2_bwd=cols(sg["a2b"], 64, 1)[None],
        gla_a_bias_fwd=sg["abf"], gla_a_bias_bwd=sg["abb"], hgrn_lb_logits=cols(g_lbl, 128, 2),
        gla_norm_gain=sg["gng"].reshape(1, HD), hgrn_norm_gain=sg["gnh"].reshape(1, HD),
        w_mix_out=gsh["w_mix_out"][None], ffn2_w_in=gsh["ffn2_w_in"][None], ffn2_w_out=gsh["ffn2_w_out"][None])
    params = dict(
        c_ctx=(c_ctx, m_c_ctx, v_c_ctx), w_ada=(w_ada, m_w_ada, v_w_ada), b_ada=(b_ada, m_b_ada, v_b_ada),
        ln_gain=(ln_gain, m_ln_gain, v_ln_gain), ln_bias=(ln_bias, m_ln_bias, v_ln_bias),
        ffn1_w_in=(ffn1_w_in, m_ffn1_w_in, v_ffn1_w_in), ffn1_w_out=(ffn1_w_out, m_ffn1_w_out, v_ffn1_w_out),
        w_mix_in=(w_mix_in, m_w_mix_in, v_w_mix_in), gla_a2_fwd=(gla_a2_fwd, m_gla_a2_fwd, v_gla_a2_fwd),
        gla_a2_bwd=(gla_a2_bwd, m_gla_a2_bwd, v_gla_a2_bwd),
        gla_a_bias_fwd=(gla_a_bias_fwd, m_gla_a_bias_fwd, v_gla_a_bias_fwd),
        gla_a_bias_bwd=(gla_a_bias_bwd, m_gla_a_bias_bwd, v_gla_a_bias_bwd),
        hgrn_lb_logits=(hgrn_lb_logits, m_hgrn_lb_logits, v_hgrn_lb_logits),
        gla_norm_gain=(gla_norm_gain, m_gla_norm_gain, v_gla_norm_gain),
        hgrn_norm_gain=(hgrn_norm_gain, m_hgrn_norm_gain, v_hgrn_norm_gain),
        w_mix_out=(w_mix_out, m_w_mix_out, v_w_mix_out), ffn2_w_in=(ffn2_w_in, m_ffn2_w_in, v_ffn2_w_in),
        ffn2_w_out=(ffn2_w_out, m_ffn2_w_out, v_ffn2_w_out))
    order = list(params.keys())
    big_names = ("w_ada",) + BIG
    upd = {}
    for k in big_names:
        w_, m_, v_ = params[k]
        s2 = w_.shape[-2:]
        d_, nm_, nv_ = _adamw(w_.reshape(s2), grads[k].reshape(s2), m_.reshape(s2), v_.reshape(s2), "adamw_" + k)
        upd[k] = (d_.reshape(w_.shape), nm_.reshape(w_.shape), nv_.reshape(w_.shape))
    small_names = [k for k in order if k not in big_names]
    sizes = [int(np.prod(params[k][0].shape)) for k in small_names]
    tot_n = sum(sizes)
    npad = -(-tot_n // 1024) * 1024

    def packed(get):
        flat_ = jnp.concatenate([get(k).reshape(-1) for k in small_names])
        return jnp.pad(flat_, (0, npad - tot_n)).reshape(8, npad // 8)

    d_s, nm_s, nv_s = _adamw(packed(lambda k: params[k][0]), packed(lambda k: grads[k]),
                             packed(lambda k: params[k][1]), packed(lambda k: params[k][2]), "adamw_small")
    o = 0
    for k, n in zip(small_names, sizes):
        shp = params[k][0].shape
        upd[k] = tuple(a.reshape(-1)[o:o + n].reshape(shp) for a in (d_s, nm_s, nv_s))
        o += n

    return (loss, grad_x, *[grads[k].reshape(params[k][0].shape) for k in order], *[upd[k][0] for k in order],
            *[upd[k][1] for k in order], *[upd[k][2] for k in order])
```

```python
import functools

import numpy as np
import jax
import jax.numpy as jnp
from jax import lax
from jax.experimental import pallas as pl
from jax.experimental.pallas import tpu as pltpu

F32 = jnp.float32
BF16 = jnp.bfloat16
MESH = pl.DeviceIdType.MESH

D = 1024
DFF = 2816
TM = 256
CH = 32
NCB = TM // CH
SB = 128
CSB = SB // CH
NSB = TM // SB
HP = 8
HD = 128
NH = 8
LN_EPS = 1e-5
NORM_EPS = 1e-6
ALPHA = 2.0 ** 0.25
GATE_NORM = 16.0
GLA_DK = 64
N_MOD = 9
VMEM_LIMIT = 52 * 1024 * 1024

MIXP = 5120
GG, RG, GQ, GK, GV, RQ, RFF, RFB, RI, LR = 0, 512, 1024, 1536, 2048, 2560, 3072, 3584, 4096, 4608
IN_SPLITS = (256, 256, 512, 512, 16, 16, 512, 512, 512, 512, 512)

ADAM_LR, ADAM_B1, ADAM_B2, ADAM_EPS, ADAM_WD, ADAM_STEP = 0.001, 0.9, 0.999, 1e-08, 0.01, 10


def _cp(n_axes):
    return pltpu.CompilerParams(dimension_semantics=("arbitrary",) * n_axes, vmem_limit_bytes=VMEM_LIMIT)


def _rowmap(stride, off):
    return lambda b, i: (b * stride + off + i, 0)


def _mmap(comb):
    if comb:
        return lambda b, i: (b, jnp.minimum(i, 1), 0, 0)
    return lambda b, i: (b, 1, 0, 0)


def _ln(x):
    mu = jnp.mean(x, axis=-1, keepdims=True)
    xc = x - mu
    var = jnp.mean(xc * xc, axis=-1, keepdims=True)
    r = lax.rsqrt(var + LN_EPS)
    return xc * r, r


def _ln_bwd(dxh, xh, r):
    return r * (dxh - jnp.mean(dxh, axis=-1, keepdims=True) - xh * jnp.mean(dxh * xh, axis=-1, keepdims=True))


def _sigmoid(x):
    return 1.0 / (1.0 + jnp.exp(-x))


def _rsum(x):
    return jnp.sum(x, axis=0, keepdims=True)


def _dot(a, b):
    return jnp.dot(a, b, preferred_element_type=F32)


def _dot_nt(a, b):
    return lax.dot_general(a, b, (((1,), (1,)), ((), ())), preferred_element_type=F32)


def _dot_tn(a, b):
    return lax.dot_general(a, b, (((0,), (0,)), ((), ())), preferred_element_type=F32)


def _modulate(xv, m_ref, sub):
    xh, _ = _ln(xv)
    sh = m_ref[0, 0, 3 * sub:3 * sub + 1, :]
    sc = m_ref[0, 0, 3 * sub + 1:3 * sub + 2, :]
    return (xh * (1.0 + sh) + sc).astype(BF16)


def _embed_lnmod(x, ctx, pos, mvec):
    B, T, _ = x.shape
    nt = 1 + T // TM

    def body(x_ref, c_ref, p_ref, m_ref, o_ref, h_ref):
        i = pl.program_id(1)

        @pl.when(i == 0)
        def _():
            o_ref[...] = c_ref[0]

        @pl.when(i > 0)
        def _():
            o_ref[...] = x_ref[0] + p_ref[...]

        h_ref[...] = _modulate(o_ref[...], m_ref, 0)

    rows = pl.BlockSpec((TM, D), lambda b, i: (b * nt + i, 0))
    return pl.pallas_call(
        body, name="embed_lnmod0", grid=(B, nt),
        in_specs=[pl.BlockSpec((1, TM, D), lambda b, i: (b, jnp.maximum(i - 1, 0), 0)),
                  pl.BlockSpec((1, TM, D), lambda b, i: (b, 0, 0)),
                  pl.BlockSpec((TM, D), lambda b, i: (jnp.maximum(i - 1, 0), 0)),
                  pl.BlockSpec((1, 1, N_MOD, D), _mmap(True))],
        out_specs=[rows, rows],
        out_shape=[jax.ShapeDtypeStruct((B * nt * TM, D), F32), jax.ShapeDtypeStruct((B * nt * TM, D), BF16)],
        compiler_params=_cp(2))(x, ctx, pos, mvec)


def _lnmod0_bwd(dh, X, mvec, dres, x_shape, B, nt):
    def body(dh_ref, x_ref, m_ref, dr_ref, dx_ref, dm_ref):
        i = pl.program_id(1)
        xh, r = _ln(x_ref[...])
        sh = m_ref[0, 0, 0:1, :]
        dhv = dh_ref[...]

        @pl.when((i == 0) | (i == 1))
        def _():
            dm_ref[...] = jnp.zeros_like(dm_ref)

        dm_ref[0, 0, 0:1, :] += _rsum(dhv * xh)
        dm_ref[0, 0, 1:2, :] += _rsum(dhv)

        @pl.when(i > 0)
        def _():
            dx_ref[0] = _ln_bwd(dhv * (1.0 + sh), xh, r) + dr_ref[...]

    rows = pl.BlockSpec((TM, D), _rowmap(nt, 0))
    return pl.pallas_call(
        body, name="lnmod0_bwd", grid=(B, nt),
        in_specs=[rows, rows, pl.BlockSpec((1, 1, N_MOD, D), _mmap(True)), rows],
        out_specs=[pl.BlockSpec((1, TM, D), lambda b, i: (b, jnp.maximum(i - 1, 0), 0)),
                   pl.BlockSpec((1, 1, 2, D), _mmap(True))],
        out_shape=[jax.ShapeDtypeStruct(x_shape, F32), jax.ShapeDtypeStruct((B, 2, 2, D), F32)],
        compiler_params=_cp(2))(dh, X, mvec, dres)


def _resid_fwd(x_ref, y_ref, m_ref, gb_ref, sub, w):
    wg = w * m_ref[0, 0, 3 * sub + 2:3 * sub + 3, :]
    y = y_ref[...]
    zh, r = _ln(ALPHA * x_ref[...] + wg * y)
    return y, wg, zh, r


def _resid_grads(do, y, wg, zh, r, w, gb_ref, comb, dx_ref, dy_ref, dg_ref, dgb_ref):
    b_, i = pl.program_id(0), pl.program_id(1)
    dz = _ln_bwd(do * gb_ref[0:1, :], zh, r)
    dx_ref[...] = ALPHA * dz
    dy_ref[...] = (wg * dz).astype(BF16)

    @pl.when((b_ == 0) & (i == 0))
    def _():
        dgb_ref[...] = jnp.zeros_like(dgb_ref)

    dgb_ref[0:1, :] += _rsum(do * zh)
    dgb_ref[1:2, :] += _rsum(do)

    init = (i == 0) | (i == 1) if comb else (i == 0)

    @pl.when(init)
    def _():
        dg_ref[...] = jnp.zeros_like(dg_ref)

    dg_ref[0, 0] += w * _rsum(dz * y)


def _resid_lnmod(X, lx, Y, mvec, comb, sub, w, gb, B, nt, name):
    def body(x_ref, y_ref, m_ref, gb_ref, o_ref, h_ref):
        _, _, zh, _ = _resid_fwd(x_ref, y_ref, m_ref, gb_ref, sub, w)
        xn = zh * gb_ref[0:1, :] + gb_ref[1:2, :]
        o_ref[...] = xn
        h_ref[...] = _modulate(xn, m_ref, sub + 1)

    rows = pl.BlockSpec((TM, D), _rowmap(nt, 0))
    return pl.pallas_call(
        body, name=name, grid=(B, nt),
        in_specs=[pl.BlockSpec((TM, D), _rowmap(*lx)), rows,
                  pl.BlockSpec((1, 1, N_MOD, D), _mmap(comb)), pl.BlockSpec((2, D), lambda b, i: (0, 0))],
        out_specs=[rows, rows],
        out_shape=[jax.ShapeDtypeStruct((B * nt * TM, D), F32), jax.ShapeDtypeStruct((B * nt * TM, D), BF16)],
        compiler_params=_cp(2))(X, Y, mvec, gb)


def _resid_out_shapes(B, nt, comb):
    rows = pl.BlockSpec((TM, D), _rowmap(nt, 0))
    specs = [rows, rows, pl.BlockSpec((1, 1, 1, D), _mmap(comb)), pl.BlockSpec((2, D), lambda b, i: (0, 0))]
    shapes = [jax.ShapeDtypeStruct((B * nt * TM, D), F32), jax.ShapeDtypeStruct((B * nt * TM, D), BF16),
              jax.ShapeDtypeStruct((B, 2, 1, D), F32), jax.ShapeDtypeStruct((2, D), F32)]
    return specs, shapes


def _tail(X, Y, mvec, gb, tgt, sub, w, B, nt):
    def body(x_ref, y_ref, m_ref, gb_ref, t_ref, dx_ref, dy_ref, dg_ref, dgb_ref, l_ref):
        y, wg, zh, r = _resid_fwd(x_ref, y_ref, m_ref, gb_ref, sub, w)
        e = (zh * gb_ref[0:1, :] + gb_ref[1:2, :]) - t_ref[0]

        @pl.when((pl.program_id(0) == 0) & (pl.program_id(1) == 0))
        def _():
            l_ref[...] = jnp.zeros_like(l_ref)

        l_ref[...] += _rsum(e * e)
        _resid_grads(e * (1.0 / D), y, wg, zh, r, w, gb_ref, False, dx_ref, dy_ref, dg_ref, dgb_ref)

    rows = pl.BlockSpec((TM, D), _rowmap(nt, 0))
    specs, shapes = _resid_out_shapes(B, nt, False)
    return pl.pallas_call(
        body, name="resid2_loss_bwd", grid=(B, nt),
        in_specs=[rows, rows, pl.BlockSpec((1, 1, N_MOD, D), _mmap(False)), pl.BlockSpec((2, D), lambda b, i: (0, 0)),
                  pl.BlockSpec((1, TM, D), lambda b, i: (b, i, 0))],
        out_specs=specs + [pl.BlockSpec((1, D), lambda b, i: (0, 0))],
        out_shape=shapes + [jax.ShapeDtypeStruct((1, D), F32)],
        compiler_params=_cp(2))(X, Y, mvec, gb, tgt)


def _lnmod_resid_bwd(dh, Xi, lxi, dres, Xp, lxp, Yp, mvec, comb, sub, w, gb, B, nt, name):
    ntl = nt - 1 if comb else nt

    def body(dh_ref, xi_ref, dr_ref, xp_ref, yp_ref, m_ref, gb_ref, dx_ref, dy_ref, dg_ref, dgb_ref, dm_ref):
        i = pl.program_id(1)
        xh, r = _ln(xi_ref[...])
        sh = m_ref[0, 0, 3 * sub:3 * sub + 1, :]
        dhv = dh_ref[...]
        dr = dr_ref[...]
        if comb:
            dr = jnp.where(i > 0, dr, 0.0)
        do = _ln_bwd(dhv * (1.0 + sh), xh, r) + dr

        init = (i == 0) | (i == 1) if comb else (i == 0)

        @pl.when(init)
        def _():
            dm_ref[...] = jnp.zeros_like(dm_ref)

        dm_ref[0, 0, 0:1, :] += _rsum(dhv * xh)
        dm_ref[0, 0, 1:2, :] += _rsum(dhv)
        y, wg, zh, r2 = _resid_fwd(xp_ref, yp_ref, m_ref, gb_ref, sub - 1, w)
        _resid_grads(do, y, wg, zh, r2, w, gb_ref, comb, dx_ref, dy_ref, dg_ref, dgb_ref)

    rows = pl.BlockSpec((TM, D), _rowmap(nt, 0))
    if comb:
        dres_spec = pl.BlockSpec((TM, D), lambda b, i: (b * ntl + jnp.maximum(i - 1, 0), 0))
    else:
        dres_spec = rows
    specs, shapes = _resid_out_shapes(B, nt, comb)
    return pl.pallas_call(
        body, name=name, grid=(B, nt),
        in_specs=[rows, pl.BlockSpec((TM, D), _rowmap(*lxi)), dres_spec, pl.BlockSpec((TM, D), _rowmap(*lxp)), rows,
                  pl.BlockSpec((1, 1, N_MOD, D), _mmap(comb)), pl.BlockSpec((2, D), lambda b, i: (0, 0))],
        out_specs=specs + [pl.BlockSpec((1, 1, 2, D), _mmap(comb))],
        out_shape=shapes + [jax.ShapeDtypeStruct((B, 2, 2, D), F32)],
        compiler_params=_cp(2))(dh, Xi, dres, Xp, Yp, mvec, gb)


def _ffn_out_dx(dy, W, u, name, dep=None):
    M = dy.shape[0]
    half = DFF // 2
    deps = [] if dep is None else [dep]

    def body(dy_ref, w_ref, u_ref, *rest):
        du_ref = rest[-1]
        dyv = dy_ref[...]
        for j in range(2):
            lo, hi = j * half, (j + 1) * half
            da = _dot_nt(dyv, w_ref[lo:hi, :])
            g = u_ref[:, lo:hi].astype(F32)
            up = u_ref[:, DFF + lo:DFF + hi].astype(F32)
            s = _sigmoid(g)
            du_ref[:, lo:hi] = (da * up * (s * (1.0 + g * (1.0 - s)))).astype(BF16)
            du_ref[:, DFF + lo:DFF + hi] = (da * (g * s)).astype(BF16)

    return pl.pallas_call(
        body, name=name, grid=(M // TM,),
        in_specs=[pl.BlockSpec((TM, D), lambda i: (i, 0)), _wspec(W, 1), pl.BlockSpec((TM, 2 * DFF), lambda i: (i, 0))]
        + [_wspec(d, 1) for d in deps],
        out_specs=pl.BlockSpec((TM, 2 * DFF), lambda i: (i, 0)),
        out_shape=jax.ShapeDtypeStruct((M, 2 * DFF), BF16), compiler_params=_cp(1))(dy, W, u, *deps)


def _wspec(W, nidx):
    zeros = (0,) * W.ndim
    if nidx == 1:
        return pl.BlockSpec(W.shape, lambda i: zeros)
    return pl.BlockSpec(W.shape, lambda b, i: zeros)


def _mm_nn(A, la, W, B, nt, out_dtype, name):
    K, N = W.shape

    def body(a_ref, w_ref, o_ref):
        o_ref[...] = _dot(a_ref[...], w_ref[...]).astype(out_dtype)

    return pl.pallas_call(
        body, name=name, grid=(B, nt),
        in_specs=[pl.BlockSpec((TM, K), _rowmap(*la)), _wspec(W, 2)],
        out_specs=pl.BlockSpec((TM, N), _rowmap(nt, 0)),
        out_shape=jax.ShapeDtypeStruct((B * nt * TM, N), out_dtype), compiler_params=_cp(2))(A, W)


def _ffn_in(A, la, W3, B, nt, name):
    K, n = W3.shape[1:]

    def body(a_ref, w_ref, u_ref, s_ref):
        a = a_ref[...]
        for j in range(2):
            g = _dot(a, w_ref[j])
            up = _dot(a, w_ref[j + 2])
            u_ref[:, j * n:(j + 1) * n] = g.astype(BF16)
            u_ref[:, (j + 2) * n:(j + 3) * n] = up.astype(BF16)
            s_ref[:, j * n:(j + 1) * n] = (g * _sigmoid(g) * up).astype(BF16)

    rows = B * nt * TM
    return pl.pallas_call(
        body, name=name, grid=(B, nt),
        in_specs=[pl.BlockSpec((TM, K), _rowmap(*la)), _wspec(W3, 2)],
        out_specs=[pl.BlockSpec((TM, 4 * n), _rowmap(nt, 0)), pl.BlockSpec((TM, 2 * n), _rowmap(nt, 0))],
        out_shape=[jax.ShapeDtypeStruct((rows, 4 * n), BF16), jax.ShapeDtypeStruct((rows, 2 * n), BF16)],
        compiler_params=_cp(2))(A, W3)


def _mm_nt(A, W, name, dep=None):
    M, N = A.shape
    K = W.shape[-2]

    def body(a_ref, w_ref, *rest):
        o_ref = rest[-1]
        if W.ndim == 3:
            n = W.shape[-1]
            acc = _dot_nt(a_ref[:, 0:n], w_ref[0])
            for j in range(1, 4):
                acc = acc + _dot_nt(a_ref[:, j * n:(j + 1) * n], w_ref[j])
            o_ref[...] = acc
        else:
            o_ref[...] = _dot_nt(a_ref[...], w_ref[...])

    deps = [] if dep is None else [dep]
    return pl.pallas_call(
        body, name=name, grid=(M // TM,),
        in_specs=[pl.BlockSpec((TM, N), lambda i: (i, 0)), _wspec(W, 1)] + [_wspec(d, 1) for d in deps],
        out_specs=pl.BlockSpec((TM, K), lambda i: (i, 0)),
        out_shape=jax.ShapeDtypeStruct((M, K), F32), compiler_params=_cp(1))(A, W, *deps)


def _mm_tn(A, G, name, tn=512, shards=None):
    M, K = A.shape
    N = G.shape[1]
    tk = 1024 if M % 1024 == 0 else 512
    if shards:
        tn = N // shards

    def body(a_ref, g_ref, o_ref):
        @pl.when(pl.program_id(1) == 0)
        def _():
            o_ref[...] = jnp.zeros_like(o_ref)

        upd = _dot_tn(a_ref[...], g_ref[...])
        if shards:
            o_ref[0] += upd
        else:
            o_ref[...] += upd

    if shards:
        out_spec = pl.BlockSpec((1, K, tn), lambda n, k: (n, 0, 0))
        out_shape = jax.ShapeDtypeStruct((shards, K, tn), F32)
    else:
        out_spec = pl.BlockSpec((K, tn), lambda n, k: (0, n))
        out_shape = jax.ShapeDtypeStruct((K, N), F32)
    return pl.pallas_call(
        body, name=name, grid=(N // tn, M // tk),
        in_specs=[pl.BlockSpec((tk, K), lambda n, k: (k, 0)), pl.BlockSpec((tk, tn), lambda n, k: (k, n))],
        out_specs=out_spec, out_shape=out_shape, compiler_params=_cp(2))(A, G)


def _logsig(z):
    return jnp.minimum(z, 0.0) - jnp.log(1.0 + jnp.exp(-jnp.abs(z)))


def _features(F, a2p, biasp, lbp, B):
    nt = F.shape[0] // (B * TM)
    nj = nt + 1
    R2 = nj * TM

    def body(f_ref, a2_ref, bias_ref, lb_ref, q_ref, k_ref, v_ref, g_ref):
        lr = f_ref[:, LR:LR + HD].astype(BF16)
        lane = lax.broadcasted_iota(jnp.int32, (1, 4 * HD), 1)
        keep = (lane & (HD - 1)) < GLA_DK
        for d in range(2):
            z = _dot(lr, a2_ref[d]) + bias_ref[d:d + 1, :]
            gl = jnp.where(keep, _logsig(z) * (1.0 / GATE_NORM), 0.0)
            for h in range(4):
                g_ref[d, 0, h] = gl[:, h * HD:(h + 1) * HD]
        for h in range(4):
            q_ref[0, h] = f_ref[:, GQ + h * HD:GQ + (h + 1) * HD] * (GLA_DK ** -0.5)
            kk = f_ref[:, GK + h * HD:GK + (h + 1) * HD]
            k_ref[0, 0, h] = kk
            k_ref[1, 0, h] = kk
            v_ref[0, h] = f_ref[:, GV + h * HD:GV + (h + 1) * HD].astype(BF16)
        for h in range(4):
            sl = slice(h * HD, (h + 1) * HD)
            rq = f_ref[:, RQ + h * HD:RQ + (h + 1) * HD]
            q_ref[0, 4 + h] = rq * _sigmoid(rq) * (HD ** -0.5)
            v_ref[0, 4 + h] = f_ref[:, RI + h * HD:RI + (h + 1) * HD].astype(BF16)
            for d, off in ((0, RFF), (1, RFB)):
                lb = lb_ref[d:d + 1, sl]
                f = lb + (1.0 - lb) * _sigmoid(f_ref[:, off + h * HD:off + (h + 1) * HD])
                g_ref[d, 0, 4 + h] = jnp.log(f)
                k_ref[d, 0, 4 + h] = 1.0 - f

    fmap = lambda b, j: (b * nt + jnp.where(j == nt, 0, j), 0)
    one = pl.BlockSpec((1, NH, TM, HD), lambda b, j: (b, 0, j, 0))
    two = pl.BlockSpec((2, 1, NH, TM, HD), lambda b, j: (0, b, 0, j, 0))
    s1 = jax.ShapeDtypeStruct((B, NH, R2, HD), F32)
    s2 = jax.ShapeDtypeStruct((2, B, NH, R2, HD), F32)
    return pl.pallas_call(
        body, name="mix_features", grid=(B, nj),
        in_specs=[pl.BlockSpec((TM, MIXP), fmap), pl.BlockSpec((2, HD, 4 * HD), lambda b, j: (0, 0, 0)),
                  pl.BlockSpec((2, 4 * HD), lambda b, j: (0, 0)), pl.BlockSpec((2, 4 * HD), lambda b, j: (0, 0))],
        out_specs=[one, two, one, two], out_shape=[s1, s2, jax.ShapeDtypeStruct(s1.shape, BF16), s2],
        compiler_params=_cp(2))(F, a2p, biasp, lbp)


def _features_bwd(F, a2p, biasp, lbp, dQ0, dQ1, dK0, dK1, dV0, dV1, dG0, dG1, dgates, B):
    nt = F.shape[0] // (B * TM)

    def body(f_ref, a2_ref, bias_ref, lb_ref, dq0, dq1, dk0, dk1, dv0, dv1, dg0, dg1, dgt_ref,
             df_ref, da2_ref, dbias_ref, dlb_ref):
        b_, i = pl.program_id(0), pl.program_id(1)

        @pl.when((b_ == 0) & (i == 0))
        def _():
            da2_ref[...] = jnp.zeros_like(da2_ref)
            dbias_ref[...] = jnp.zeros_like(dbias_ref)
            dlb_ref[...] = jnp.zeros_like(dlb_ref)

        df_ref[:, 0:2 * 4 * HD] = jnp.where(i > 0, dgt_ref[...], 0.0).astype(BF16)
        df_ref[:, LR + HD:] = jnp.zeros((TM, MIXP - LR - HD), BF16)
        lr = f_ref[:, LR:LR + HD].astype(BF16)
        lane = lax.broadcasted_iota(jnp.int32, (1, 4 * HD), 1)
        keep = (lane & (HD - 1)) < GLA_DK
        dlr = jnp.zeros((TM, HD), F32)
        dgs = (dg0, dg1)
        dks = (dk0, dk1)
        rd = lambda ref, h: ref[0, h].astype(F32)
        for d in range(2):
            z = _dot(lr, a2_ref[d]) + bias_ref[d:d + 1, :]
            dgl = jnp.concatenate([rd(dgs[d], h) for h in range(4)], axis=1)
            dz = jnp.where(keep, dgl * (1.0 / GATE_NORM) * (1.0 - _sigmoid(z)), 0.0)
            dzb = dz.astype(BF16)
            dlr = dlr + _dot_nt(dzb, a2_ref[d])
            da2_ref[d] += _dot_tn(lr, dzb)
            dbias_ref[d:d + 1, :] += _rsum(dz)
        df_ref[:, LR:LR + HD] = dlr.astype(BF16)
        for h in range(4):
            df_ref[:, GQ + h * HD:GQ + (h + 1) * HD] = ((rd(dq0, h) + rd(dq1, h)) * (GLA_DK ** -0.5)).astype(BF16)
            df_ref[:, GK + h * HD:GK + (h + 1) * HD] = (rd(dk0, h) + rd(dk1, h)).astype(BF16)
            df_ref[:, GV + h * HD:GV + (h + 1) * HD] = (rd(dv0, h) + rd(dv1, h)).astype(BF16)
        for h in range(4):
            sl = slice(h * HD, (h + 1) * HD)
            rq = f_ref[:, RQ + h * HD:RQ + (h + 1) * HD]
            s = _sigmoid(rq)
            dqh = rd(dq0, 4 + h) + rd(dq1, 4 + h)
            df_ref[:, RQ + h * HD:RQ + (h + 1) * HD] = (dqh * (HD ** -0.5) * (s * (1.0 + rq * (1.0 - s)))).astype(BF16)
            df_ref[:, RI + h * HD:RI + (h + 1) * HD] = (rd(dv0, 4 + h) + rd(dv1, 4 + h)).astype(BF16)
            for d, off in ((0, RFF), (1, RFB)):
                lb = lb_ref[d:d + 1, sl]
                sg = _sigmoid(f_ref[:, off + h * HD:off + (h + 1) * HD])
                f = lb + (1.0 - lb) * sg
                dff = rd(dgs[d], 4 + h) / f - rd(dks[d], 4 + h)
                df_ref[:, off + h * HD:off + (h + 1) * HD] = (dff * (1.0 - lb) * sg * (1.0 - sg)).astype(BF16)
                dlb_ref[d:d + 1, sl] += _rsum(dff * (1.0 - sg))

    m0 = lambda b, i: (b, 0, i, 0)
    m1 = lambda b, i: (b, 0, jnp.where(i == 0, nt, i), 0)
    one = lambda m: pl.BlockSpec((1, NH, TM, HD), m)
    return pl.pallas_call(
        body, name="mix_features_bwd", grid=(B, nt),
        in_specs=[pl.BlockSpec((TM, MIXP), _rowmap(nt, 0)), pl.BlockSpec((2, HD, 4 * HD), lambda b, i: (0, 0, 0)),
                  pl.BlockSpec((2, 4 * HD), lambda b, i: (0, 0)), pl.BlockSpec((2, 4 * HD), lambda b, i: (0, 0)),
                  one(m0), one(m1), one(m0), one(m1), one(m0), one(m1), one(m0), one(m1),
                  pl.BlockSpec((TM, D), lambda b, i: (b * (nt - 1) + jnp.maximum(i - 1, 0), 0))],
        out_specs=[pl.BlockSpec((TM, MIXP), _rowmap(nt, 0)), pl.BlockSpec((2, HD, 4 * HD), lambda b, i: (0, 0, 0)),
                   pl.BlockSpec((2, 4 * HD), lambda b, i: (0, 0)), pl.BlockSpec((2, 4 * HD), lambda b, i: (0, 0))],
        out_shape=[jax.ShapeDtypeStruct((B * nt * TM, MIXP), BF16), jax.ShapeDtypeStruct((2, HD, 4 * HD), F32),
                   jax.ShapeDtypeStruct((2, 4 * HD), F32), jax.ShapeDtypeStruct((2, 4 * HD), F32)],
        compiler_params=_cp(2))(F, a2p, biasp, lbp, dQ0, dQ1, dK0, dK1, dV0, dV1, dG0, dG1, dgates)


def _chunk_scan(x, rin, fwd):
    acc = x
    sft = 1
    while sft < CH:
        if fwd:
            acc = acc + jnp.where(rin >= sft, pltpu.roll(acc, sft, 0), 0.0)
        else:
            acc = acc + jnp.where(rin < CH - sft, pltpu.roll(acc, TM - sft, 0), 0.0)
        sft *= 2
    return acc


def _chunk_total(x):
    t = jnp.sum(x.reshape(NCB, CH, HD), axis=1, keepdims=True)
    return jnp.broadcast_to(t, (NCB, CH, HD)).reshape(TM, HD)


def _scan_masks(rev):
    rin = lax.broadcasted_iota(jnp.int32, (TM, HD), 0) & (CH - 1)
    ri = lax.broadcasted_iota(jnp.int32, (SB, SB), 0)
    ci = lax.broadcasted_iota(jnp.int32, (SB, SB), 1)
    same = (ri >> 5) == (ci >> 5)
    lo = same & (ri >= ci)
    up = same & (ri <= ci)
    mask, maskT = (up, lo) if rev else (lo, up)
    re = lax.broadcasted_iota(jnp.int32, (SB, CSB * HD), 0) >> 5
    ce = lax.broadcasted_iota(jnp.int32, (SB, CSB * HD), 1) >> 7
    return rin, mask, maskT, re == ce


def _scan_decay(q, k, g, rin, rev):
    b = _chunk_scan(g, rin, not rev)
    xx = _chunk_total(g) - b
    eb = jnp.exp(b)
    return b, xx, eb, q * eb, k * jnp.exp(-b), k * jnp.exp(xx)


def _sub(x, s):
    return x[s * SB:(s + 1) * SB]


def _expand(xb, mexp):
    return jnp.where(mexp, jnp.concatenate([xb] * CSB, axis=1), jnp.zeros((), xb.dtype))


def _own(x, mexp):
    xm = jnp.where(mexp, x, 0.0)
    acc = xm[:, 0:HD]
    for n in range(1, CSB):
        acc = acc + xm[:, n * HD:(n + 1) * HD]
    return acc


def _stack(per_chunk, s):
    return jnp.concatenate(per_chunk[s * CSB:(s + 1) * CSB], axis=1)


def _state_pass(s0, eb, uts, rev):
    order = range(NCB - 1, -1, -1) if rev else range(NCB)
    states = [None] * NCB
    s = s0
    for n in order:
        row = n * CH if rev else n * CH + CH - 1
        states[n] = s
        s = eb[row:row + 1, :] * s + uts[n // CSB][:, (n % CSB) * HD:(n % CSB + 1) * HD]
    return states, s


def _scan_fwd(Q, K, V, G, rev, B):
    nb = Q.shape[2] // TM - 1
    d = 1 if rev else 0
    rmap = (lambda s: nb - s) if rev else (lambda s: s)

    def body(q_ref, k_ref, v_ref, g_ref, o_ref, st_ref, s_scr):
        @pl.when(pl.program_id(2) == 0)
        def _():
            s_scr[...] = jnp.zeros_like(s_scr)

        rin, mask, _, mexp = _scan_masks(rev)

        def head(p, readout):
            s0 = s_scr[p]
            st_ref[0, p, 0] = s0
            _, _, eb, qd, ki, kt = _scan_decay(q_ref[0, p], k_ref[0, 0, p], g_ref[0, 0, p], rin, rev)
            ktb, vb = kt.astype(BF16), v_ref[0, p]
            uts = [_dot_tn(_sub(vb, s), _expand(_sub(ktb, s), mexp)) for s in range(NSB)]
            states, s_new = _state_pass(s0, eb, uts, rev)
            s_scr[p] = s_new
            if not readout:
                return
            qb, kib = qd.astype(BF16), ki.astype(BF16)
            for s in range(NSB):
                a = jnp.where(mask, _dot_nt(_sub(qb, s), _sub(kib, s)), 0.0)
                o_ref[0, p, s * SB:(s + 1) * SB, :] = (
                    _dot(a.astype(BF16), _sub(vb, s))
                    + _dot_nt(_expand(_sub(qb, s), mexp), _stack(states, s).astype(BF16)))

        @pl.when(pl.program_id(2) >= 1)
        def _():
            for p in range(HP):
                head(p, True)

        @pl.when(pl.program_id(2) == 0)
        def _():
            for p in range(HP):
                head(p, False)

    one = pl.BlockSpec((1, HP, TM, HD), lambda b, h, s: (b, h, rmap(s), 0))
    two = pl.BlockSpec((1, 1, HP, TM, HD), lambda b, h, s: (d, b, h, rmap(s), 0))
    return pl.pallas_call(
        body, name="scan_fwd_rev" if rev else "scan_fwd", grid=(B, NH // HP, nb),
        in_specs=[one, two, one, two],
        out_specs=[one, pl.BlockSpec((1, HP, 1, HD, HD), lambda b, h, s: (b, h, s, 0, 0))],
        out_shape=[jax.ShapeDtypeStruct(Q.shape, F32), jax.ShapeDtypeStruct((B, NH, nb, HD, HD), F32)],
        scratch_shapes=[pltpu.VMEM((HP, HD, HD), F32)],
        compiler_params=_cp(3))(Q, K, V, G)


def _scan_bwd(Q, K, V, G, St, dO, rev, B):
    nb = Q.shape[2] // TM - 1
    d = 1 if rev else 0
    smap = lambda t: nb - 1 - t
    rmap = (lambda t: nb - smap(t)) if rev else smap

    def body(q_ref, k_ref, v_ref, g_ref, st_ref, do_ref, dq_ref, dk_ref, dv_ref, dg_ref, ds_scr):
        t = pl.program_id(2)

        @pl.when(t == 0)
        def _():
            ds_scr[...] = jnp.zeros_like(ds_scr)

        rin, mask, maskT, mexp = _scan_masks(rev)

        def head(p, readout):
            b, xx, eb, qd, ki, kt = _scan_decay(q_ref[0, p], k_ref[0, 0, p], g_ref[0, 0, p], rin, rev)
            qb, kib, ktb, vb = qd.astype(BF16), ki.astype(BF16), kt.astype(BF16), v_ref[0, p]
            kt_exps = [_expand(_sub(ktb, s), mexp) for s in range(NSB)]
            uts = [_dot_tn(_sub(vb, s), kt_exps[s]) for s in range(NSB)]
            states, _ = _state_pass(st_ref[0, p, 0], eb, uts, rev)
            if readout:
                dob = do_ref[0, p].astype(BF16)
                gts = [_dot_tn(_sub(dob, s), _expand(_sub(qb, s), mexp)) for s in range(NSB)]
            order = range(NCB) if rev else range(NCB - 1, -1, -1)
            dsp = [None] * NCB
            t2 = [None] * NCB
            dsc = ds_scr[p]
            for n in order:
                row = n * CH if rev else n * CH + CH - 1
                ebl = eb[row:row + 1, :]
                dsp[n] = dsc
                t2[n] = jnp.broadcast_to(ebl * _rsum(states[n] * dsc), (CH, HD))
                dsc = ebl * dsc
                if readout:
                    dsc = gts[n // CSB][:, (n % CSB) * HD:(n % CSB + 1) * HD] + dsc
            ds_scr[p] = dsc
            dqds, dkis, dkts = [], [], []
            for s in range(NSB):
                v_s = _sub(vb, s)
                dspb = _stack(dsp, s).astype(BF16)
                dv = _dot_nt(kt_exps[s], dspb)
                if readout:
                    q_s, ki_s, do_s = _sub(qb, s), _sub(kib, s), _sub(dob, s)
                    da = jnp.where(mask, _dot_nt(do_s, v_s), 0.0).astype(BF16)
                    dat = jnp.where(maskT, _dot_nt(v_s, do_s), 0.0).astype(BF16)
                    at = jnp.where(maskT, _dot_nt(ki_s, q_s), 0.0).astype(BF16)
                    dqds.append(_dot(da, ki_s) + _own(_dot(do_s, _stack(states, s).astype(BF16)), mexp))
                    dkis.append(_dot(dat, q_s))
                    dv = _dot(at, do_s) + dv
                dv_ref[0, p, s * SB:(s + 1) * SB, :] = dv.astype(BF16)
                dkts.append(_own(_dot(v_s, dspb), mexp))
            dkt = jnp.concatenate(dkts, axis=0)
            z = dkt * kt
            if readout:
                dqd, dki = jnp.concatenate(dqds, axis=0), jnp.concatenate(dkis, axis=0)
                db = dqd * qd - dki * ki
                dq_ref[0, p] = (dqd * eb).astype(BF16)
                dk_ref[0, p] = (dki * jnp.exp(-b) + dkt * jnp.exp(xx)).astype(BF16)
                w = z - db
                dg = _chunk_total(db) - w
            else:
                dq_ref[0, p] = jnp.zeros((TM, HD), BF16)
                dk_ref[0, p] = (dkt * jnp.exp(xx)).astype(BF16)
                w = z
                dg = -z
            dg_ref[0, p] = (dg + _chunk_scan(w, rin, not rev) + jnp.concatenate(t2, axis=0)).astype(BF16)

        @pl.when(smap(t) >= 1)
        def _():
            for p in range(HP):
                head(p, True)

        @pl.when(smap(t) == 0)
        def _():
            for p in range(HP):
                head(p, False)

    one = pl.BlockSpec((1, HP, TM, HD), lambda b, h, t: (b, h, rmap(t), 0))
    two = pl.BlockSpec((1, 1, HP, TM, HD), lambda b, h, t: (d, b, h, rmap(t), 0))
    lat = pl.BlockSpec((1, HP, TM, HD), lambda b, h, t: (b, h, jnp.clip(rmap(t) - 1, 0, nb - 2), 0))
    shp = jax.ShapeDtypeStruct(Q.shape, BF16)
    return pl.pallas_call(
        body, name="scan_bwd_rev" if rev else "scan_bwd", grid=(B, NH // HP, nb),
        in_specs=[one, two, one, two, pl.BlockSpec((1, HP, 1, HD, HD), lambda b, h, t: (b, h, smap(t), 0, 0)), lat],
        out_specs=[one, one, one, one], out_shape=[shp, shp, shp, shp],
        scratch_shapes=[pltpu.VMEM((HP, HD, HD), F32)],
        compiler_params=_cp(3))(Q, K, V, G, St, dO)


def _gnorm(O0, O1, F, gains, B, ntl):
    nt = ntl + 1

    def body(o0_ref, o1_ref, f_ref, gn_ref, m_ref):
        for h in range(NH):
            o = o0_ref[0, h] + o1_ref[0, h]
            r = lax.rsqrt(jnp.mean(o * o, axis=-1, keepdims=True) + NORM_EPS)
            gn = gn_ref[0:1, :] if h < 4 else gn_ref[1:2, :]
            gt = f_ref[:, h * HD:(h + 1) * HD]
            m_ref[:, h * HD:(h + 1) * HD] = (o * r * gn * (gt * _sigmoid(gt))).astype(BF16)

    ospec = pl.BlockSpec((1, NH, TM, HD), lambda b, i: (b, 0, i + 1, 0))
    return pl.pallas_call(
        body, name="gated_norm", grid=(B, ntl),
        in_specs=[ospec, ospec, pl.BlockSpec((TM, D), lambda b, i: (b * nt + 1 + i, 0)),
                  pl.BlockSpec((2, HD), lambda b, i: (0, 0))],
        out_specs=pl.BlockSpec((TM, D), _rowmap(ntl, 0)),
        out_shape=jax.ShapeDtypeStruct((B * ntl * TM, D), BF16), compiler_params=_cp(2))(O0, O1, F, gains)


def _gnorm_bwd(dM, O0, O1, F, gains, B, ntl):
    nt = ntl + 1

    def body(dm_ref, o0_ref, o1_ref, f_ref, gn_ref, do_ref, dgt_ref, dgn_ref):
        b_, i = pl.program_id(0), pl.program_id(1)

        @pl.when((b_ == 0) & (i == 0))
        def _():
            dgn_ref[...] = jnp.zeros_like(dgn_ref)

        for h in range(NH):
            o = o0_ref[0, h] + o1_ref[0, h]
            r = lax.rsqrt(jnp.mean(o * o, axis=-1, keepdims=True) + NORM_EPS)
            y = o * r
            gn = gn_ref[0:1, :] if h < 4 else gn_ref[1:2, :]
            gt = f_ref[:, h * HD:(h + 1) * HD]
            s = _sigmoid(gt)
            dm = dm_ref[:, h * HD:(h + 1) * HD]
            don = dm * (gt * s)
            dgt_ref[:, h * HD:(h + 1) * HD] = dm * (y * gn) * (s * (1.0 + gt * (1.0 - s)))
            row = 0 if h < 4 else 1
            dgn_ref[row:row + 1, :] += _rsum(don * y)
            dy = don * gn
            do_ref[0, h] = r * (dy - y * jnp.mean(dy * y, axis=-1, keepdims=True))

    ospec = pl.BlockSpec((1, NH, TM, HD), lambda b, i: (b, 0, i + 1, 0))
    return pl.pallas_call(
        body, name="gated_norm_bwd", grid=(B, ntl),
        in_specs=[pl.BlockSpec((TM, D), _rowmap(ntl, 0)), ospec, ospec,
                  pl.BlockSpec((TM, D), lambda b, i: (b * nt + 1 + i, 0)), pl.BlockSpec((2, HD), lambda b, i: (0, 0))],
        out_specs=[pl.BlockSpec((1, NH, TM, HD), lambda b, i: (b, 0, i, 0)), pl.BlockSpec((TM, D), _rowmap(ntl, 0)),
                   pl.BlockSpec((2, HD), lambda b, i: (0, 0))],
        out_shape=[jax.ShapeDtypeStruct((B, NH, ntl * TM, HD), F32), jax.ShapeDtypeStruct((B * ntl * TM, D), F32),
                   jax.ShapeDtypeStruct((2, HD), F32)],
        compiler_params=_cp(2))(dM, O0, O1, F, gains)


def _sincos_2d(rows, width, dim):
    quarter = dim // 4
    omega = 1.0 / 10000.0 ** (jnp.arange(quarter, dtype=F32) / quarter)

    def emb(n):
        a = jnp.arange(n).astype(F32)[:, None] * omega[None, :]
        return jnp.concatenate([jnp.sin(a), jnp.cos(a)], axis=-1)

    er = jnp.broadcast_to(emb(rows)[:, None, :], (rows, width, dim // 2))
    ec = jnp.broadcast_to(emb(width)[None, :, :], (rows, width, dim // 2))
    return jnp.concatenate([er, ec], axis=-1).reshape(rows * width, dim)


def _pad_heads(w):
    k = w.shape[0]
    return jnp.pad(w.reshape(k, 4, GLA_DK), ((0, 0), (0, 0), (0, HD - GLA_DK))).reshape(k, 4 * HD)


def _unpad_heads(w):
    k = w.shape[0]
    return w.reshape(k, 4, HD)[:, :, :GLA_DK].reshape(k, 4 * GLA_DK)


MIX_N = 1032
MIX_NP = 1152
_SEGS = ([(64 * h, 64, GQ + HD * h) for h in range(4)] + [(256 + 64 * h, 64, GK + HD * h) for h in range(4)]
         + [(512, 512, GV), (1024, 512, GG), (1536, 32, LR), (1568, 512, RQ), (2080, 512, RFF), (2592, 512, RFB),
            (3104, 512, RI), (3616, 512, RG)])


def _mix_in_to_padded(ps):
    k = ps.shape[1]
    parts, pos = [], 0
    for g0, ln, s0 in sorted(_SEGS, key=lambda s: s[2]):
        if s0 > pos:
            parts.append(jnp.zeros((k, s0 - pos), ps.dtype))
        for j in range(4):
            lo, hi = max(g0, j * MIX_N), min(g0 + ln, (j + 1) * MIX_N)
            if lo < hi:
                parts.append(ps[j][:, lo - j * MIX_N:hi - j * MIX_N])
        pos = s0 + ln
    parts.append(jnp.zeros((k, MIXP - pos), ps.dtype))
    return jnp.concatenate(parts, axis=1)


def _mix_in_from_padded(g):
    k = g.shape[0]
    shards = []
    for j in range(4):
        parts = []
        for g0, ln, s0 in sorted(_SEGS):
            lo, hi = max(g0, j * MIX_N), min(g0 + ln, (j + 1) * MIX_N)
            if lo < hi:
                parts.append(g[:, s0 + lo - g0:s0 + hi - g0])
        parts.append(jnp.zeros((k, MIX_NP - MIX_N), g.dtype))
        shards.append(jnp.concatenate(parts, axis=1))
    return jnp.stack(shards)


def _local_step(x, ctx, tgt, mvec, weights_for, ln_gain, ln_bias, a2f, a2b, abf, abb, lb, gng, gnh, on_grads, on_sent):
    B, T, _ = x.shape
    assert ctx.shape[1] == TM and T % TM == 0
    ntl = T // TM
    nt = ntl + 1
    C, L, CL = (nt, 0), (ntl, 0), (nt, 1)
    pos = _sincos_2d(T // 64, 64, D)
    gbs = [jnp.stack([ln_gain[i], ln_bias[i]]) for i in range(3)]
    a2p = jnp.zeros((2, HD, 4 * HD), F32)
    a2p = a2p.at[0, 0:16].set(_pad_heads(a2f)).at[1, 16:32].set(_pad_heads(a2b)).astype(BF16)
    biasp = jnp.concatenate([_pad_heads(abf.reshape(1, -1)), _pad_heads(abb.reshape(1, -1))], axis=0)
    gains = jnp.concatenate([gng.reshape(1, HD), gnh.reshape(1, HD)], axis=0)

    X0, h0 = _embed_lnmod(x, ctx, pos, mvec)
    w1i, w1o = weights_for("ffn1", h0)
    u0, a0 = _ffn_in(h0, C, w1i, B, nt, "ffn1_in")
    wmp, wmo = weights_for("mix", a0)
    y0 = _mm_nn(a0, C, w1o, B, nt, F32, "ffn1_out")
    X1, h1 = _resid_lnmod(X0, C, y0, mvec, True, 0, 0.5, gbs[0], B, nt, "resid0_lnmod1")
    Fm = _mm_nn(h1, C, wmp, B, nt, F32, "mix_in")
    w2i, w2o = weights_for("ffn2", Fm)
    Q, K, V, G = _features(Fm, a2p, biasp, lb, B)
    O0, S0 = _scan_fwd(Q, K, V, G, False, B)
    O1, S1 = _scan_fwd(Q, K, V, G, True, B)
    merged = _gnorm(O0, O1, Fm, gains, B, ntl)
    y1 = _mm_nn(merged, L, wmo, B, ntl, F32, "mix_out")
    X2, h2 = _resid_lnmod(X1, CL, y1, mvec, False, 1, 1.0, gbs[1], B, ntl, "resid1_lnmod2")
    u2, a2 = _ffn_in(h2, L, w2i, B, ntl, "ffn2_in")
    y2 = _mm_nn(a2, L, w2o, B, ntl, F32, "ffn2_out")

    dx2r, dy2, dgate2, dgb2, lsum = _tail(X2, y2, mvec, gbs[2], tgt, 2, 0.5, B, ntl)
    loss = (0.5 / D) * jnp.sum(lsum)
    du2 = _ffn_out_dx(dy2, w2o, u2, "ffn2_out_dx")
    g_w2o = _mm_tn(a2, dy2, "ffn2_out_dw")
    dh2 = _mm_nt(du2, w2i, "ffn2_in_dx")
    g_w2i = _mm_tn(h2, du2, "ffn2_in_dw", shards=4)
    tok = on_grads("ffn2", (g_w2i, g_w2o))
    dx1r, dy1, dgate1, dgb1, dss2 = _lnmod_resid_bwd(dh2, X2, L, dx2r, X1, CL, y1, mvec, False, 2, 1.0,
                                                     gbs[1] + tok[0, 0], B, ntl, "lnmod2_resid1_bwd")
    tok = on_sent("ffn2", dy1)
    dmerged = _mm_nt(dy1, wmo, "mix_out_dx", dep=tok)
    g_wmo = _mm_tn(merged, dy1, "mix_out_dw", tn=D)
    dO, dgates, dgains = _gnorm_bwd(dmerged, O0, O1, Fm, gains, B, ntl)
    dQ0, dK0, dV0, dG0 = _scan_bwd(Q, K, V, G, S0, dO, False, B)
    dQ1, dK1, dV1, dG1 = _scan_bwd(Q, K, V, G, S1, dO, True, B)
    dF, da2p, dbiasp, dlb = _features_bwd(Fm, a2p, biasp, lb, dQ0, dQ1, dK0, dK1, dV0, dV1, dG0, dG1, dgates, B)
    dh1 = _mm_nt(dF, wmp, "mix_in_dx")
    g_wmp = _mm_tn(h1, dF, "mix_in_dw", tn=MIXP // 4)
    tok = on_grads("mix", (g_wmp, g_wmo))
    dx0r, dy0, dgate0, dgb0, dss1 = _lnmod_resid_bwd(dh1, X1, C, dx1r, X0, C, y0, mvec, True, 1, 0.5,
                                                     gbs[0] + tok[0, 0], B, nt, "lnmod1_resid0_bwd")
    tok = on_sent("mix", dy0)
    du0 = _ffn_out_dx(dy0, w1o, u0, "ffn1_out_dx", dep=tok)
    g_w1o = _mm_tn(a0, dy0, "ffn1_out_dw")
    g_w1i = _mm_tn(h0, du0, "ffn1_in_dw", shards=4)
    tok = on_grads("ffn1", (g_w1i, g_w1o))
    dh0 = _mm_nt(du0, w1i, "ffn1_in_dx", dep=tok)
    tok = on_sent("ffn1", dh0)
    grad_x, dss0 = _lnmod0_bwd(dh0, X0, mvec + tok[0, 0], dx0r, x.shape, B, nt)

    zero_ctx = lambda a: a.at[:, 0].set(0.0)
    dm = jnp.concatenate([dss0, dgate0, dss1, zero_ctx(dgate1), zero_ctx(dss2), zero_ctx(dgate2)], axis=2)
    small = dict(
        ln_gain=jnp.stack([dgb0[0], dgb1[0], dgb2[0]]), ln_bias=jnp.stack([dgb0[1], dgb1[1], dgb2[1]]),
        a2f=_unpad_heads(da2p[0, 0:16]), a2b=_unpad_heads(da2p[1, 16:32]),
        abf=_unpad_heads(dbiasp[0:1]), abb=_unpad_heads(dbiasp[1:2]), lb=dlb, gng=dgains[0], gnh=dgains[1])
    return loss, grad_x, dm, small


def _small_allgather(xs, name):
    r, n = xs.shape

    def body(x_ref, out_ref, send_sems, recv_sems, local_sem):
        x, y, c = lax.axis_index("x"), lax.axis_index("y"), lax.axis_index("c")
        me, sibling = (x, y, c), (x, y, 1 - c)
        chips = [(1 - x, y), (x, 1 - y), (1 - x, 1 - y)]

        def rows(px, py, pc):
            return out_ref.at[pl.ds((4 * px + 2 * py + pc) * r, r), :]

        def copy(k, block, to, src=None):
            return pltpu.make_async_remote_copy(
                src_ref=rows(*block) if src is None else src, dst_ref=rows(*block),
                send_sem=send_sems.at[k], recv_sem=recv_sems.at[k], device_id=to, device_id_type=MESH)

        mine = pltpu.make_async_copy(x_ref, rows(*me), local_sem)
        mine.start()
        first = [copy(0, me, sibling, src=x_ref)]
        first += [copy(1 + j, me, (*chip, c), src=x_ref) for j, chip in enumerate(chips)]
        for cp in first:
            cp.start()
        passed = [copy(4 + j, (*chip, c), sibling) for j, chip in enumerate(chips)]
        for j, chip in enumerate(chips):
            copy(1 + j, (*chip, c), me).wait_recv()
            passed[j].start()
        copy(0, sibling, me).wait_recv()
        for j, chip in enumerate(chips):
            copy(4 + j, (*chip, 1 - c), me).wait_recv()
        for cp in first + passed:
            cp.wait_send()
        mine.wait()

    out = pl.pallas_call(
        body, name=name,
        out_shape=jax.ShapeDtypeStruct((8 * r, n), xs.dtype),
        in_specs=[pl.BlockSpec(memory_space=pltpu.VMEM)],
        out_specs=pl.BlockSpec(memory_space=pltpu.VMEM),
        scratch_shapes=[pltpu.SemaphoreType.DMA((7,)), pltpu.SemaphoreType.DMA((7,)), pltpu.SemaphoreType.DMA],
        compiler_params=pltpu.CompilerParams(vmem_limit_bytes=VMEM_LIMIT))(xs)
    return out.reshape(8, r, n)


def _gather_flat(v, name):
    n = v.shape[0]
    npad = -(-n // 1024) * 1024
    g = _small_allgather(jnp.pad(v, (0, npad - n)).reshape(8, npad // 8), name)
    return g.reshape(8, npad)[:, :n]


HBM_SPEC = pl.BlockSpec(memory_space=pltpu.HBM)
SEM_SPEC = pl.BlockSpec(memory_space=pltpu.SEMAPHORE)
DATAFLOW = pltpu.SideEffectType.DATAFLOW_SIDE_EFFECTING


def _gather_copies(xs, outs, send_sems, recv_sems):
    x, y, c = lax.axis_index("x"), lax.axis_index("y"), lax.axis_index("c")
    dests = [(x, y, 1 - c), (1 - x, y, c), (x, 1 - y, c), (1 - x, 1 - y, c)]
    return [pltpu.make_async_remote_copy(
        src_ref=xs[w], dst_ref=outs[w].at[4 * x + 2 * y + c], send_sem=send_sems[4 * w + k],
        recv_sem=recv_sems[4 * w + k], device_id=dests[k], device_id_type=MESH)
        for w in range(len(xs)) for k in range(4)]


def _split_start(copies, per_w, srcs, land_lead, after, name):
    n = len(srcs)
    m = per_w * n
    lands = [lax.empty((land_lead,) + s.shape[-2:], s.dtype) for s in srcs]
    deps = [] if after is None else [after]

    def body(*refs):
        xs, ls, outs = refs[:n], refs[n:2 * n], refs[2 * n + len(deps):]
        for cp in copies(xs, ls, outs[:m], outs[m:2 * m]):
            cp.start()
        token = outs[2 * m + 2 * n]
        token[...] = jnp.zeros_like(token)

    outs = pl.pallas_call(
        body, name=name,
        out_shape=([pltpu.SemaphoreType.DMA(())] * (2 * m) + [pltpu.HBM(a.shape, a.dtype) for a in srcs + lands]
                   + [jax.ShapeDtypeStruct((8, 128), F32)]),
        in_specs=[HBM_SPEC] * (2 * n) + [pl.BlockSpec(memory_space=pl.ANY)] * len(deps),
        out_specs=[SEM_SPEC] * (2 * m) + [HBM_SPEC] * (2 * n) + [pl.BlockSpec(memory_space=pltpu.VMEM)],
        input_output_aliases={w: 2 * m + w for w in range(2 * n)},
        compiler_params=pltpu.CompilerParams(has_side_effects=DATAFLOW),
    )(*[pltpu.with_memory_space_constraint(a, pltpu.HBM) for a in srcs + lands], *deps)
    return outs[:2 * m], outs[2 * m:2 * m + n], outs[2 * m + n:2 * m + 2 * n], outs[2 * m + 2 * n]


def _split_wait(copies, per_w, sems, x_thru, l_thru, after, name):
    n = len(x_thru)
    m = per_w * n

    def body(*refs):
        xs, ls, ss = refs[:n], refs[n:2 * n], refs[2 * n:2 * n + 2 * m]
        for cp in copies(xs, ls, ss[:m], ss[m:]):
            cp.wait_send()
            cp.wait_recv()

    outs = pl.pallas_call(
        body, name=name,
        out_shape=[pltpu.HBM(a.shape, a.dtype) for a in list(x_thru) + list(l_thru)],
        in_specs=[HBM_SPEC] * (2 * n) + [SEM_SPEC] * (2 * m) + [pl.BlockSpec(memory_space=pl.ANY)],
        out_specs=[HBM_SPEC] * (2 * n),
        input_output_aliases={w: w for w in range(2 * n)},
        compiler_params=pltpu.CompilerParams(has_side_effects=DATAFLOW),
    )(*x_thru, *l_thru, *sems, after)
    return outs[:n], outs[n:]


def _gather_forward(gathered, name):
    n = len(gathered)

    def body(*refs):
        outs = refs[n:2 * n]
        send_sems, recv_sems = refs[2 * n:]
        x, y, c = lax.axis_index("x"), lax.axis_index("y"), lax.axis_index("c")
        chips = [(1 - x, y), (x, 1 - y), (1 - x, 1 - y), (x, y)]

        def copy(w, j, pc):
            px, py = chips[j]
            slot = outs[w].at[4 * px + 2 * py + pc]
            return pltpu.make_async_remote_copy(
                src_ref=slot, dst_ref=slot, send_sem=send_sems.at[4 * w + j], recv_sem=recv_sems.at[4 * w + j],
                device_id=(x, y, 1 - c), device_id_type=MESH)

        sends = [copy(w, j, 1 - c if j == 3 else c) for w in range(n) for j in range(4)]
        for cp in sends:
            cp.start()
        for w in range(n):
            for j in range(4):
                copy(w, j, c if j == 3 else 1 - c).wait_recv()
        for cp in sends:
            cp.wait_send()

    any_spec = pl.BlockSpec(memory_space=pl.ANY)
    return pl.pallas_call(
        body, name=name,
        out_shape=[jax.ShapeDtypeStruct(g.shape, g.dtype) for g in gathered],
        in_specs=[any_spec] * n, out_specs=[any_spec] * n,
        input_output_aliases={w: w for w in range(n)},
        scratch_shapes=[pltpu.SemaphoreType.DMA((4 * n,)), pltpu.SemaphoreType.DMA((4 * n,))],
    )(*gathered)


def _pair_copies(gs, ls, send_sems, recv_sems):
    x, y, c = lax.axis_index("x"), lax.axis_index("y"), lax.axis_index("c")
    return [pltpu.make_async_remote_copy(
        src_ref=gs[w].at[2 * j + 1 - c], dst_ref=ls[w].at[j], send_sem=send_sems[4 * w + j],
        recv_sem=recv_sems[4 * w + j], device_id=(x, y, 1 - c), device_id_type=MESH)
        for w in range(len(gs)) for j in range(4)]


def _chip_copies(hs, ls, send_sems, recv_sems):
    x, y, c = lax.axis_index("x"), lax.axis_index("y"), lax.axis_index("c")
    chips = [(1 - x, y), (x, 1 - y), (1 - x, 1 - y)]
    return [pltpu.make_async_remote_copy(
        src_ref=hs[w].at[2 * px + py], dst_ref=ls[w].at[k], send_sem=send_sems[3 * w + k],
        recv_sem=recv_sems[3 * w + k], device_id=(px, py, c), device_id_type=MESH)
        for w in range(len(hs)) for k, (px, py) in enumerate(chips)]


def _rs_pair_share(fins, name):
    n = len(fins)

    def body(*refs):
        outs = refs[n:2 * n]
        send_sems, recv_sems = refs[2 * n:]
        x, y, c = lax.axis_index("x"), lax.axis_index("y"), lax.axis_index("c")

        def copy(w, slot):
            return pltpu.make_async_remote_copy(
                src_ref=outs[w].at[slot], dst_ref=outs[w].at[slot], send_sem=send_sems.at[w], recv_sem=recv_sems.at[w],
                device_id=(x, y, 1 - c), device_id_type=MESH)

        sends = [copy(w, c) for w in range(n)]
        for cp in sends:
            cp.start()
        for w in range(n):
            copy(w, 1 - c).wait_recv()
        for cp in sends:
            cp.wait_send()

    any_spec = pl.BlockSpec(memory_space=pl.ANY)
    return pl.pallas_call(
        body, name=name,
        out_shape=[jax.ShapeDtypeStruct(f.shape, f.dtype) for f in fins],
        in_specs=[any_spec] * n, out_specs=[any_spec] * n,
        input_output_aliases={w: w for w in range(n)},
        scratch_shapes=[pltpu.SemaphoreType.DMA((n,)), pltpu.SemaphoreType.DMA((n,))])(*fins)


def _rs_add_pair(g8, r4, c, name):
    R, n = g8.shape[1:]
    rb = R // 2

    def body(c_ref, g_ref, r_ref, o_ref):
        o_ref[...] = (g_ref[...] + r_ref[...]).astype(BF16)

    spec = pl.BlockSpec((1, rb, n), lambda j, i, c_ref: (j, i, 0))
    return pl.pallas_call(
        body, name=name,
        grid_spec=pltpu.PrefetchScalarGridSpec(
            num_scalar_prefetch=1, grid=(4, R // rb),
            in_specs=[pl.BlockSpec((1, rb, n), lambda j, i, c_ref: (2 * j + c_ref[0], i, 0)), spec],
            out_specs=spec),
        out_shape=jax.ShapeDtypeStruct((4, R, n), BF16), compiler_params=_cp(2))(c, g8, r4)


def _rs_add_chips(g8, r4, r3, cj, name):
    R, n = g8.shape[1:]
    rb = R // 2

    def body(cj_ref, g_ref, p_ref, r_ref, o_ref):
        own = g_ref[0] + p_ref[0]
        o_ref[0] = ((own + r_ref[0].astype(F32)) + r_ref[1].astype(F32)) + r_ref[2].astype(F32)

    return pl.pallas_call(
        body, name=name,
        grid_spec=pltpu.PrefetchScalarGridSpec(
            num_scalar_prefetch=1, grid=(R // rb,),
            in_specs=[pl.BlockSpec((1, rb, n), lambda i, cj_ref: (2 * cj_ref[1] + cj_ref[0], i, 0)),
                      pl.BlockSpec((1, rb, n), lambda i, cj_ref: (cj_ref[1], i, 0)),
                      pl.BlockSpec((3, rb, n), lambda i, cj_ref: (0, i, 0))],
            out_specs=pl.BlockSpec((1, rb, n), lambda i, cj_ref: (cj_ref[0], i, 0))),
        out_shape=jax.ShapeDtypeStruct((2, R, n), F32), compiler_params=_cp(1))(cj, g8, r4, r3)


def _sum8(g):
    n = g.shape[1]

    def body(g_ref, o_ref):
        acc = g_ref[0:1, :]
        for k in range(1, 8):
            acc = acc + g_ref[k:k + 1, :]
        o_ref[...] = acc

    return pl.pallas_call(body, name="sum_devices", out_shape=jax.ShapeDtypeStruct((1, n), F32),
                          compiler_params=pltpu.CompilerParams(vmem_limit_bytes=VMEM_LIMIT))(g)


ADA_ROWS = 64


def _ada_fwd(cs, w, b):
    n = w.shape[1]

    def body(c_ref, w_ref, b_ref, o_ref):
        cv = c_ref[...]
        s = (cv * _sigmoid(cv)).astype(BF16)
        o_ref[...] = _dot(s, w_ref[...].astype(BF16)) + b_ref[...]

    return pl.pallas_call(body, name="ada_fwd", out_shape=jax.ShapeDtypeStruct((ADA_ROWS, n), F32),
                          compiler_params=pltpu.CompilerParams(vmem_limit_bytes=VMEM_LIMIT))(cs, w, b)


def _ada_bwd(cs, w, dm):
    n = w.shape[1]

    def body(c_ref, w_ref, dm_ref, gw_ref, dc_ref):
        cv = c_ref[...]
        s = (cv * _sigmoid(cv)).astype(BF16)
        gw_ref[...] = _dot_tn(s, dm_ref[...].astype(BF16))
        dc_ref[...] = _dot_nt(dm_ref[32:40, :].astype(BF16), w_ref[...].astype(BF16))

    return pl.pallas_call(
        body, name="ada_bwd",
        out_shape=[jax.ShapeDtypeStruct((D, n), F32), jax.ShapeDtypeStruct((8, D), F32)],
        compiler_params=pltpu.CompilerParams(vmem_limit_bytes=VMEM_LIMIT))(cs, w, dm)


def _adamw(w, g, m, v, name):
    r, c = w.shape
    rb = r
    if r % 8 == 0 and r * c * 4 > (1 << 20):
        rb = 8
        for cand in range(8, r, 8):
            if r % cand == 0 and cand * c * 4 <= (1 << 20):
                rb = cand

    def body(w_ref, g_ref, m_ref, v_ref, go_ref, d_ref, nm_ref, nv_ref):
        gv = g_ref[...]
        go_ref[...] = gv
        mn = ADAM_B1 * m_ref[...] + (1.0 - ADAM_B1) * gv
        vn = ADAM_B2 * v_ref[...] + (1.0 - ADAM_B2) * (gv * gv)
        m_hat = mn / (1.0 - ADAM_B1 ** ADAM_STEP)
        v_hat = vn / (1.0 - ADAM_B2 ** ADAM_STEP)
        d_ref[...] = -ADAM_LR * (m_hat / (jnp.sqrt(v_hat) + ADAM_EPS) + ADAM_WD * w_ref[...])
        nm_ref[...] = mn
        nv_ref[...] = vn

    spec = pl.BlockSpec((rb, c), lambda i: (i, 0))
    shp = jax.ShapeDtypeStruct((r, c), F32)
    return pl.pallas_call(body, name=name, grid=(r // rb,), in_specs=[spec] * 4, out_specs=[spec] * 4,
                          out_shape=[shp] * 4, compiler_params=_cp(1))(w, g, m, v)


BIG = ("ffn1_w_in", "ffn1_w_out", "w_mix_in", "w_mix_out", "ffn2_w_in", "ffn2_w_out")


def _half_rows(w, c):
    half = w.shape[0] // 2
    return lax.dynamic_slice_in_dim(w, c * half, half, axis=0)


def _lower_bounds(logits):
    return jnp.cumsum(jax.nn.softmax(logits.astype(F32), axis=1), axis=1)[:, 0]


def kernel(x, c, ctx, c_ctx, w_ada, b_ada, ln_gain, ln_bias, ffn1_w_in, ffn1_w_out, w_mix_in, gla_a2_fwd, gla_a2_bwd, gla_a_bias_fwd, gla_a_bias_bwd, hgrn_lb_logits, gla_norm_gain, hgrn_norm_gain, w_mix_out, ffn2_w_in, ffn2_w_out, loss_target, m_c_ctx, m_w_ada, m_b_ada, m_ln_gain, m_ln_bias, m_ffn1_w_in, m_ffn1_w_out, m_w_mix_in, m_gla_a2_fwd, m_gla_a2_bwd, m_gla_a_bias_fwd, m_gla_a_bias_bwd, m_hgrn_lb_logits, m_gla_norm_gain, m_hgrn_norm_gain, m_w_mix_out, m_ffn2_w_in, m_ffn2_w_out, v_c_ctx, v_w_ada, v_b_ada, v_ln_gain, v_ln_bias, v_ffn1_w_in, v_ffn1_w_out, v_w_mix_in, v_gla_a2_fwd, v_gla_a2_bwd, v_gla_a_bias_fwd, v_gla_a_bias_bwd, v_hgrn_lb_logits, v_gla_norm_gain, v_hgrn_norm_gain, v_w_mix_out, v_ffn2_w_in, v_ffn2_w_out):
    xi, yi, ci = lax.axis_index("x"), lax.axis_index("y"), lax.axis_index("c")
    chip = 2 * xi + yi
    dev = 2 * chip + ci
    B = x.shape[0]
    weights = dict(ffn1_w_in=ffn1_w_in[0], ffn1_w_out=ffn1_w_out[0], w_mix_in=w_mix_in[0], w_mix_out=w_mix_out[0],
                   ffn2_w_in=ffn2_w_in[0], ffn2_w_out=ffn2_w_out[0])

    mine = jnp.concatenate([c.reshape(-1), ln_gain.reshape(-1), ln_bias.reshape(-1), gla_a2_fwd.reshape(-1),
                            gla_a2_bwd.reshape(-1), hgrn_lb_logits.reshape(-1)])
    g1 = _gather_flat(mine, "gather_cond")
    nc = B * D
    c_all = g1[:, :nc].reshape(8 * B, D)
    per_chip = g1[0::2, nc:]
    o = 0

    def take(shape, axis):
        nonlocal o
        n = int(np.prod(shape))
        parts = per_chip[:, o:o + n].reshape((4,) + shape)
        o += n
        return jnp.concatenate([parts[j] for j in range(4)], axis=axis)

    ln_gain_f = take((3, 256), 1)
    ln_bias_f = take((3, 256), 1)
    a2f_f = take((16, 64), 1)
    a2b_f = take((16, 64), 1)
    lbl_f = take((2, 2, 128), 2)
    lb, lb_vjp = jax.vjp(_lower_bounds, lbl_f)

    cs = jnp.concatenate([c_all, c_ctx.reshape(1, D), jnp.zeros((ADA_ROWS - 8 * B - 1, D), F32)], axis=0)
    ncol = w_ada.shape[2]
    b_cols = lax.dynamic_slice_in_dim(b_ada, chip * ncol, ncol, axis=1)
    m_cols = _ada_fwd(cs, w_ada[0], b_cols)
    g2 = _small_allgather(m_cols, "gather_mod")[0::2]
    m_all = jnp.concatenate([g2[j] for j in range(4)], axis=1)
    m_lat = lax.dynamic_slice_in_dim(m_all, dev * B, B, axis=0).reshape(B, 1, N_MOD, D)
    m_ctx = jnp.broadcast_to(m_all[8 * B].reshape(1, 1, N_MOD, D), (B, 1, N_MOD, D))

    groups = dict(ffn2=("ffn2_w_in", "ffn2_w_out"), mix=("w_mix_in", "w_mix_out"), ffn1=("ffn1_w_in", "ffn1_w_out"))
    shards = dict(weights, w_mix_in=jnp.pad(weights["w_mix_in"], ((0, 0), (0, MIX_NP - MIX_N))))
    blks = {k: _half_rows(shards[k], ci).astype(BF16) for k in BIG}
    gathering = {}
    token = m_all
    for group in ("ffn1", "mix", "ffn2"):
        sems, x_thru, l_thru, token = _split_start(_gather_copies, 4, [blks[k] for k in groups[group]], 8, token,
                                                   "weight_gather_start_" + group)
        gathering[group] = (sems, x_thru, l_thru)
    mvec = jnp.concatenate([m_ctx, m_lat], axis=1) + token[0, 0]

    def weights_for(group, after):
        names = groups[group]
        _, got = _split_wait(_gather_copies, 4, *gathering[group], after, "weight_gather_wait_" + group)
        w_in, w_out = _gather_forward(got, "weight_gather_forward_" + group)
        if group == "mix":
            return _mix_in_to_padded(w_in.reshape(4, D, MIX_NP)), w_out.reshape(-1, D)
        return w_in.reshape((4,) + shards[names[0]].shape), w_out.reshape(-1, D)

    cvec = ci.reshape(1).astype(jnp.int32)
    cjvec = jnp.stack([ci, chip]).astype(jnp.int32)
    in_flight = {}

    def on_grads(group, gs):
        names = groups[group]
        if group == "mix":
            gs = (_mix_in_from_padded(gs[0]), gs[1])
        g8s = [g.reshape((8, shards[k].shape[0] // 2, shards[k].shape[1])) for k, g in zip(names, gs)]
        sems, g_thru, l_thru, token = _split_start(_pair_copies, 4, g8s, 4, None, "grad_pair_start_" + group)
        in_flight[group] = (sems, g_thru, l_thru)
        return token

    def on_sent(group, after):
        g8s, r4s = _split_wait(_pair_copies, 4, *in_flight[group], after, "grad_pair_wait_" + group)
        h4s = [_rs_add_pair(g, r, cvec, "grad_pair_add_" + k) for k, g, r in zip(groups[group], g8s, r4s)]
        sems, h_thru, l_thru, token = _split_start(_chip_copies, 3, h4s, 3, None, "grad_chip_start_" + group)
        in_flight[group] = (g8s, r4s, sems, h_thru, l_thru)
        return token

    loss_l, grad_x, dm, small = _local_step(
        x, ctx, loss_target, mvec, weights_for, ln_gain_f, ln_bias_f, a2f_f, a2b_f,
        gla_a_bias_fwd, gla_a_bias_bwd, lb, gla_norm_gain, hgrn_norm_gain, on_grads, on_sent)
    loss = lax.psum(loss_l, ("x", "y", "c"))

    dm_lat = dm[:, 1].reshape(B, N_MOD * D)
    dm_ctx = jnp.sum(dm[:, 0], axis=0).reshape(N_MOD * D)
    keys = ("ln_gain", "ln_bias", "a2f", "a2b", "abf", "abb", "lb", "gng", "gnh")
    flat = jnp.concatenate([dm_lat.reshape(-1), dm_ctx] + [small[k].reshape(-1) for k in keys])
    g3 = _gather_flat(flat, "gather_small_grads")
    nlat = B * N_MOD * D
    dm_all = g3[:, :nlat].reshape(8 * B, N_MOD * D)
    tot = _sum8(g3[:, nlat:])[0]
    dmc_tot = tot[:N_MOD * D]
    o = N_MOD * D
    sg = {}
    for k in keys:
        n = int(np.prod(small[k].shape))
        sg[k] = tot[o:o + n].reshape(small[k].shape)
        o += n
    dm_rows = jnp.concatenate([dm_all, dmc_tot.reshape(1, -1), jnp.zeros((ADA_ROWS - 8 * B - 1, N_MOD * D), F32)], axis=0)
    g_b_ada = (jnp.sum(dm_all, axis=0) + dmc_tot).reshape(1, N_MOD * D)
    g_w_ada, dcc = _ada_bwd(cs, w_ada[0], lax.dynamic_slice_in_dim(dm_rows, chip * ncol, ncol, axis=1))
    g4 = _gather_flat(dcc[0], "gather_cctx")
    dsilu = ((g4[0] + g4[2]) + g4[4]) + g4[6]
    sc = _sigmoid(c_ctx)
    g_c_ctx = dsilu * (sc * (1.0 + c_ctx * (1.0 - sc)))
    (g_lbl,) = lb_vjp(sg["lb"])

    def cols(a, n, axis):
        return lax.dynamic_slice_in_dim(a, chip * n, n, axis=axis)

    fin = {}
    for group in ("ffn2", "mix", "ffn1"):
        g8s, r4s, sems, h_thru, l_thru = in_flight[group]
        _, r3s = _split_wait(_chip_copies, 3, sems, h_thru, l_thru, g_w_ada, "grad_chip_wait_" + group)
        for k, g, r4, r3 in zip(groups[group], g8s, r4s, r3s):
            fin[k] = _rs_add_chips(g, r4, r3, cjvec, "grad_chip_add_" + k)
    fins = [fin[k] for k in BIG]
    gsh = {k: both.reshape(shards[k].shape)[:, :weights[k].shape[1]]
           for k, both in zip(BIG, _rs_pair_share(fins, "grad_pair_share"))}

    grads = dict(
        c_ctx=g_c_ctx, w_ada=g_w_ada[None], b_ada=g_b_ada, ln_gain=cols(sg["ln_gain"], 256, 1)[None],
        ln_bias=cols(sg["ln_bias"], 256, 1)[None], ffn1_w_in=gsh["ffn1_w_in"][None], ffn1_w_out=gsh["ffn1_w_out"][None],
        w_mix_in=gsh["w_mix_in"][None], gla_a2_fwd=cols(sg["a2f"], 64, 1)[None], gla_a2_bwd=cols(sg["a2b"], 64, 1)[None],
        gla_a_bias_fwd=sg["abf"], gla_a_bias_bwd=sg["abb"], hgrn_lb_logits=cols(g_lbl, 128, 2),
        gla_norm_gain=sg["gng"].reshape(1, HD), hgrn_norm_gain=sg["gnh"].reshape(1, HD),
        w_mix_out=gsh["w_mix_out"][None], ffn2_w_in=gsh["ffn2_w_in"][None], ffn2_w_out=gsh["ffn2_w_out"][None])
    params = dict(
        c_ctx=(c_ctx, m_c_ctx, v_c_ctx), w_ada=(w_ada, m_w_ada, v_w_ada), b_ada=(b_ada, m_b_ada, v_b_ada),
        ln_gain=(ln_gain, m_ln_gain, v_ln_gain), ln_bias=(ln_bias, m_ln_bias, v_ln_bias),
        ffn1_w_in=(ffn1_w_in, m_ffn1_w_in, v_ffn1_w_in), ffn1_w_out=(ffn1_w_out, m_ffn1_w_out, v_ffn1_w_out),
        w_mix_in=(w_mix_in, m_w_mix_in, v_w_mix_in), gla_a2_fwd=(gla_a2_fwd, m_gla_a2_fwd, v_gla_a2_fwd),
        gla_a2_bwd=(gla_a2_bwd, m_gla_a2_bwd, v_gla_a2_bwd),
        gla_a_bias_fwd=(gla_a_bias_fwd, m_gla_a_bias_fwd, v_gla_a_bias_fwd),
        gla_a_bias_bwd=(gla_a_bias_bwd, m_gla_a_bias_bwd, v_gla_a_bias_bwd),
        hgrn_lb_logits=(hgrn_lb_logits, m_hgrn_lb_logits, v_hgrn_lb_logits),
        gla_norm_gain=(gla_norm_gain, m_gla_norm_gain, v_gla_norm_gain),
        hgrn_norm_gain=(hgrn_norm_gain, m_hgrn_norm_gain, v_hgrn_norm_gain),
        w_mix_out=(w_mix_out, m_w_mix_out, v_w_mix_out), ffn2_w_in=(ffn2_w_in, m_ffn2_w_in, v_ffn2_w_in),
        ffn2_w_out=(ffn2_w_out, m_ffn2_w_out, v_ffn2_w_out))
    order = list(params.keys())
    big_names = ("w_ada",) + BIG
    upd = {}
    for k in big_names:
        w_, m_, v_ = params[k]
        s2 = w_.shape[-2:]
        g_, d_, nm_, nv_ = _adamw(w_.reshape(s2), grads[k].reshape(s2), m_.reshape(s2), v_.reshape(s2), "adamw_" + k)
        grads[k] = g_
        upd[k] = (d_.reshape(w_.shape), nm_.reshape(w_.shape), nv_.reshape(w_.shape))
    small_names = [k for k in order if k not in big_names]
    sizes = [int(np.prod(params[k][0].shape)) for k in small_names]
    tot_n = sum(sizes)
    npad = -(-tot_n // 1024) * 1024

    def packed(get):
        flat_ = jnp.concatenate([get(k).reshape(-1) for k in small_names])
        return jnp.pad(flat_, (0, npad - tot_n)).reshape(8, npad // 8)

    _, d_s, nm_s, nv_s = _adamw(packed(lambda k: params[k][0]), packed(lambda k: grads[k]),
                                packed(lambda k: params[k][1]), packed(lambda k: params[k][2]), "adamw_small")
    o = 0
    for k, n in zip(small_names, sizes):
        shp = params[k][0].shape
        upd[k] = tuple(a.reshape(-1)[o:o + n].reshape(shp) for a in (d_s, nm_s, nv_s))
        o += n

    return (loss, grad_x, *[grads[k].reshape(params[k][0].shape) for k in order], *[upd[k][0] for k in order],
            *[upd[k][1] for k in order], *[upd[k][2] for k in order])
```

```python
import functools

import numpy as np
import jax
import jax.numpy as jnp
from jax import lax
from jax.experimental import pallas as pl
from jax.experimental.pallas import tpu as pltpu

F32 = jnp.float32
BF16 = jnp.bfloat16
MESH = pl.DeviceIdType.MESH

D = 1024
DFF = 2816
TM = 256
CH = 32
NCB = TM // CH
SB = 128
CSB = SB // CH
NSB = TM // SB
HP = 8
HPB = 4
HD = 128
NH = 8
LN_EPS = 1e-5
NORM_EPS = 1e-6
ALPHA = 2.0 ** 0.25
GATE_NORM = 16.0
GLA_DK = 64
N_MOD = 9
VMEM_LIMIT = 52 * 1024 * 1024

MIXP = 5120
GG, RG, GQ, GK, GV, RQ, RFF, RFB, RI, LR = 0, 512, 1024, 1536, 2048, 2560, 3072, 3584, 4096, 4608
IN_SPLITS = (256, 256, 512, 512, 16, 16, 512, 512, 512, 512, 512)

ADAM_LR, ADAM_B1, ADAM_B2, ADAM_EPS, ADAM_WD, ADAM_STEP = 0.001, 0.9, 0.999, 1e-08, 0.01, 10


def _cp(n_axes):
    return pltpu.CompilerParams(dimension_semantics=("arbitrary",) * n_axes, vmem_limit_bytes=VMEM_LIMIT)


def _rowmap(stride, off):
    return lambda b, i: (b * stride + off + i, 0)


def _mmap(comb):
    if comb:
        return lambda b, i: (b, jnp.minimum(i, 1), 0, 0)
    return lambda b, i: (b, 1, 0, 0)


def _ln(x):
    mu = jnp.mean(x, axis=-1, keepdims=True)
    xc = x - mu
    var = jnp.mean(xc * xc, axis=-1, keepdims=True)
    r = lax.rsqrt(var + LN_EPS)
    return xc * r, r


def _ln_bwd(dxh, xh, r):
    return r * (dxh - jnp.mean(dxh, axis=-1, keepdims=True) - xh * jnp.mean(dxh * xh, axis=-1, keepdims=True))


def _sigmoid(x):
    return 1.0 / (1.0 + jnp.exp(-x))


def _rsum(x):
    return jnp.sum(x, axis=0, keepdims=True)


def _dot(a, b):
    return jnp.dot(a, b, preferred_element_type=F32)


def _dot_nt(a, b):
    return lax.dot_general(a, b, (((1,), (1,)), ((), ())), preferred_element_type=F32)


def _dot_tn(a, b):
    return lax.dot_general(a, b, (((0,), (0,)), ((), ())), preferred_element_type=F32)


def _modulate(xv, m_ref, sub):
    xh, _ = _ln(xv)
    sh = m_ref[0, 0, 3 * sub:3 * sub + 1, :]
    sc = m_ref[0, 0, 3 * sub + 1:3 * sub + 2, :]
    return (xh * (1.0 + sh) + sc).astype(BF16)


def _embed_lnmod(x, ctx, pos, mvec):
    B, T, _ = x.shape
    nt = 1 + T // TM

    def body(x_ref, c_ref, p_ref, m_ref, o_ref, h_ref):
        i = pl.program_id(1)

        @pl.when(i == 0)
        def _():
            o_ref[...] = c_ref[0]

        @pl.when(i > 0)
        def _():
            o_ref[...] = x_ref[0] + p_ref[...]

        h_ref[...] = _modulate(o_ref[...], m_ref, 0)

    rows = pl.BlockSpec((TM, D), lambda b, i: (b * nt + i, 0))
    return pl.pallas_call(
        body, name="embed_lnmod0", grid=(B, nt),
        in_specs=[pl.BlockSpec((1, TM, D), lambda b, i: (b, jnp.maximum(i - 1, 0), 0)),
                  pl.BlockSpec((1, TM, D), lambda b, i: (b, 0, 0)),
                  pl.BlockSpec((TM, D), lambda b, i: (jnp.maximum(i - 1, 0), 0)),
                  pl.BlockSpec((1, 1, N_MOD, D), _mmap(True))],
        out_specs=[rows, rows],
        out_shape=[jax.ShapeDtypeStruct((B * nt * TM, D), F32), jax.ShapeDtypeStruct((B * nt * TM, D), BF16)],
        compiler_params=_cp(2))(x, ctx, pos, mvec)


def _lnmod0_bwd(dh, X, mvec, dres, x_shape, B, nt):
    def body(dh_ref, x_ref, m_ref, dr_ref, dx_ref, dm_ref):
        i = pl.program_id(1)
        xh, r = _ln(x_ref[...])
        sh = m_ref[0, 0, 0:1, :]
        dhv = dh_ref[...]

        @pl.when((i == 0) | (i == 1))
        def _():
            dm_ref[...] = jnp.zeros_like(dm_ref)

        dm_ref[0, 0, 0:1, :] += _rsum(dhv * xh)
        dm_ref[0, 0, 1:2, :] += _rsum(dhv)

        @pl.when(i > 0)
        def _():
            dx_ref[0] = _ln_bwd(dhv * (1.0 + sh), xh, r) + dr_ref[...]

    rows = pl.BlockSpec((TM, D), _rowmap(nt, 0))
    return pl.pallas_call(
        body, name="lnmod0_bwd", grid=(B, nt),
        in_specs=[rows, rows, pl.BlockSpec((1, 1, N_MOD, D), _mmap(True)), rows],
        out_specs=[pl.BlockSpec((1, TM, D), lambda b, i: (b, jnp.maximum(i - 1, 0), 0)),
                   pl.BlockSpec((1, 1, 2, D), _mmap(True))],
        out_shape=[jax.ShapeDtypeStruct(x_shape, F32), jax.ShapeDtypeStruct((B, 2, 2, D), F32)],
        compiler_params=_cp(2))(dh, X, mvec, dres)


def _resid_fwd(x_ref, y_ref, m_ref, gb_ref, sub, w):
    wg = w * m_ref[0, 0, 3 * sub + 2:3 * sub + 3, :]
    y = y_ref[...]
    zh, r = _ln(ALPHA * x_ref[...] + wg * y)
    return y, wg, zh, r


def _resid_grads(do, y, wg, zh, r, w, gb_ref, comb, dx_ref, dy_ref, dg_ref, dgb_ref):
    b_, i = pl.program_id(0), pl.program_id(1)
    dz = _ln_bwd(do * gb_ref[0:1, :], zh, r)
    dx_ref[...] = ALPHA * dz
    dy_ref[...] = (wg * dz).astype(BF16)

    @pl.when((b_ == 0) & (i == 0))
    def _():
        dgb_ref[...] = jnp.zeros_like(dgb_ref)

    dgb_ref[0:1, :] += _rsum(do * zh)
    dgb_ref[1:2, :] += _rsum(do)

    init = (i == 0) | (i == 1) if comb else (i == 0)

    @pl.when(init)
    def _():
        dg_ref[...] = jnp.zeros_like(dg_ref)

    dg_ref[0, 0] += w * _rsum(dz * y)


def _resid_lnmod(X, lx, Y, mvec, comb, sub, w, gb, B, nt, name):
    def body(x_ref, y_ref, m_ref, gb_ref, o_ref, h_ref):
        _, _, zh, _ = _resid_fwd(x_ref, y_ref, m_ref, gb_ref, sub, w)
        xn = zh * gb_ref[0:1, :] + gb_ref[1:2, :]
        o_ref[...] = xn
        h_ref[...] = _modulate(xn, m_ref, sub + 1)

    rows = pl.BlockSpec((TM, D), _rowmap(nt, 0))
    return pl.pallas_call(
        body, name=name, grid=(B, nt),
        in_specs=[pl.BlockSpec((TM, D), _rowmap(*lx)), rows,
                  pl.BlockSpec((1, 1, N_MOD, D), _mmap(comb)), pl.BlockSpec((2, D), lambda b, i: (0, 0))],
        out_specs=[rows, rows],
        out_shape=[jax.ShapeDtypeStruct((B * nt * TM, D), F32), jax.ShapeDtypeStruct((B * nt * TM, D), BF16)],
        compiler_params=_cp(2))(X, Y, mvec, gb)


def _resid_out_shapes(B, nt, comb):
    rows = pl.BlockSpec((TM, D), _rowmap(nt, 0))
    specs = [rows, rows, pl.BlockSpec((1, 1, 1, D), _mmap(comb)), pl.BlockSpec((2, D), lambda b, i: (0, 0))]
    shapes = [jax.ShapeDtypeStruct((B * nt * TM, D), F32), jax.ShapeDtypeStruct((B * nt * TM, D), BF16),
              jax.ShapeDtypeStruct((B, 2, 1, D), F32), jax.ShapeDtypeStruct((2, D), F32)]
    return specs, shapes


def _tail(X, Y, mvec, gb, tgt, sub, w, B, nt):
    def body(x_ref, y_ref, m_ref, gb_ref, t_ref, dx_ref, dy_ref, dg_ref, dgb_ref, l_ref):
        y, wg, zh, r = _resid_fwd(x_ref, y_ref, m_ref, gb_ref, sub, w)
        e = (zh * gb_ref[0:1, :] + gb_ref[1:2, :]) - t_ref[0]

        @pl.when((pl.program_id(0) == 0) & (pl.program_id(1) == 0))
        def _():
            l_ref[...] = jnp.zeros_like(l_ref)

        l_ref[...] += _rsum(e * e)
        _resid_grads(e * (1.0 / D), y, wg, zh, r, w, gb_ref, False, dx_ref, dy_ref, dg_ref, dgb_ref)

    rows = pl.BlockSpec((TM, D), _rowmap(nt, 0))
    specs, shapes = _resid_out_shapes(B, nt, False)
    return pl.pallas_call(
        body, name="resid2_loss_bwd", grid=(B, nt),
        in_specs=[rows, rows, pl.BlockSpec((1, 1, N_MOD, D), _mmap(False)), pl.BlockSpec((2, D), lambda b, i: (0, 0)),
                  pl.BlockSpec((1, TM, D), lambda b, i: (b, i, 0))],
        out_specs=specs + [pl.BlockSpec((1, D), lambda b, i: (0, 0))],
        out_shape=shapes + [jax.ShapeDtypeStruct((1, D), F32)],
        compiler_params=_cp(2))(X, Y, mvec, gb, tgt)


def _lnmod_resid_bwd(dh, Xi, lxi, dres, Xp, lxp, Yp, mvec, comb, sub, w, gb, B, nt, name):
    ntl = nt - 1 if comb else nt

    def body(dh_ref, xi_ref, dr_ref, xp_ref, yp_ref, m_ref, gb_ref, dx_ref, dy_ref, dg_ref, dgb_ref, dm_ref):
        i = pl.program_id(1)
        xh, r = _ln(xi_ref[...])
        sh = m_ref[0, 0, 3 * sub:3 * sub + 1, :]
        dhv = dh_ref[...]
        dr = dr_ref[...]
        if comb:
            dr = jnp.where(i > 0, dr, 0.0)
        do = _ln_bwd(dhv * (1.0 + sh), xh, r) + dr

        init = (i == 0) | (i == 1) if comb else (i == 0)

        @pl.when(init)
        def _():
            dm_ref[...] = jnp.zeros_like(dm_ref)

        dm_ref[0, 0, 0:1, :] += _rsum(dhv * xh)
        dm_ref[0, 0, 1:2, :] += _rsum(dhv)
        y, wg, zh, r2 = _resid_fwd(xp_ref, yp_ref, m_ref, gb_ref, sub - 1, w)
        _resid_grads(do, y, wg, zh, r2, w, gb_ref, comb, dx_ref, dy_ref, dg_ref, dgb_ref)

    rows = pl.BlockSpec((TM, D), _rowmap(nt, 0))
    if comb:
        dres_spec = pl.BlockSpec((TM, D), lambda b, i: (b * ntl + jnp.maximum(i - 1, 0), 0))
    else:
        dres_spec = rows
    specs, shapes = _resid_out_shapes(B, nt, comb)
    return pl.pallas_call(
        body, name=name, grid=(B, nt),
        in_specs=[rows, pl.BlockSpec((TM, D), _rowmap(*lxi)), dres_spec, pl.BlockSpec((TM, D), _rowmap(*lxp)), rows,
                  pl.BlockSpec((1, 1, N_MOD, D), _mmap(comb)), pl.BlockSpec((2, D), lambda b, i: (0, 0))],
        out_specs=specs + [pl.BlockSpec((1, 1, 2, D), _mmap(comb))],
        out_shape=shapes + [jax.ShapeDtypeStruct((B, 2, 2, D), F32)],
        compiler_params=_cp(2))(dh, Xi, dres, Xp, Yp, mvec, gb)


def _ffn_out_dx(dy, W, u, name, dep=None):
    M = dy.shape[0]
    half = DFF // 2
    deps = [] if dep is None else [dep]

    def body(dy_ref, w_ref, u_ref, *rest):
        du_ref = rest[-1]
        dyv = dy_ref[...]
        for j in range(2):
            lo, hi = j * half, (j + 1) * half
            da = _dot_nt(dyv, w_ref[lo:hi, :])
            g = u_ref[:, lo:hi].astype(F32)
            up = u_ref[:, DFF + lo:DFF + hi].astype(F32)
            s = _sigmoid(g)
            du_ref[:, lo:hi] = (da * up * (s * (1.0 + g * (1.0 - s)))).astype(BF16)
            du_ref[:, DFF + lo:DFF + hi] = (da * (g * s)).astype(BF16)

    return pl.pallas_call(
        body, name=name, grid=(M // TM,),
        in_specs=[pl.BlockSpec((TM, D), lambda i: (i, 0)), _wspec(W, 1), pl.BlockSpec((TM, 2 * DFF), lambda i: (i, 0))]
        + [_wspec(d, 1) for d in deps],
        out_specs=pl.BlockSpec((TM, 2 * DFF), lambda i: (i, 0)),
        out_shape=jax.ShapeDtypeStruct((M, 2 * DFF), BF16), compiler_params=_cp(1))(dy, W, u, *deps)


def _wspec(W, nidx):
    zeros = (0,) * W.ndim
    if nidx == 1:
        return pl.BlockSpec(W.shape, lambda i: zeros)
    return pl.BlockSpec(W.shape, lambda b, i: zeros)


def _mm_nn(A, la, W, B, nt, out_dtype, name):
    K, N = W.shape

    def body(a_ref, w_ref, o_ref):
        o_ref[...] = _dot(a_ref[...], w_ref[...]).astype(out_dtype)

    return pl.pallas_call(
        body, name=name, grid=(B, nt),
        in_specs=[pl.BlockSpec((TM, K), _rowmap(*la)), _wspec(W, 2)],
        out_specs=pl.BlockSpec((TM, N), _rowmap(nt, 0)),
        out_shape=jax.ShapeDtypeStruct((B * nt * TM, N), out_dtype), compiler_params=_cp(2))(A, W)


def _ffn_in(A, la, W3, B, nt, name):
    K, n = W3.shape[1:]

    def body(a_ref, w_ref, u_ref, s_ref):
        a = a_ref[...]
        for j in range(2):
            g = _dot(a, w_ref[j])
            up = _dot(a, w_ref[j + 2])
            u_ref[:, j * n:(j + 1) * n] = g.astype(BF16)
            u_ref[:, (j + 2) * n:(j + 3) * n] = up.astype(BF16)
            s_ref[:, j * n:(j + 1) * n] = (g * _sigmoid(g) * up).astype(BF16)

    rows = B * nt * TM
    return pl.pallas_call(
        body, name=name, grid=(B, nt),
        in_specs=[pl.BlockSpec((TM, K), _rowmap(*la)), _wspec(W3, 2)],
        out_specs=[pl.BlockSpec((TM, 4 * n), _rowmap(nt, 0)), pl.BlockSpec((TM, 2 * n), _rowmap(nt, 0))],
        out_shape=[jax.ShapeDtypeStruct((rows, 4 * n), BF16), jax.ShapeDtypeStruct((rows, 2 * n), BF16)],
        compiler_params=_cp(2))(A, W3)


def _mm_nt(A, W, name, dep=None):
    M, N = A.shape
    K = W.shape[-2]

    def body(a_ref, w_ref, *rest):
        o_ref = rest[-1]
        if W.ndim == 3:
            n = W.shape[-1]
            acc = _dot_nt(a_ref[:, 0:n], w_ref[0])
            for j in range(1, 4):
                acc = acc + _dot_nt(a_ref[:, j * n:(j + 1) * n], w_ref[j])
            o_ref[...] = acc
        else:
            o_ref[...] = _dot_nt(a_ref[...], w_ref[...])

    deps = [] if dep is None else [dep]
    return pl.pallas_call(
        body, name=name, grid=(M // TM,),
        in_specs=[pl.BlockSpec((TM, N), lambda i: (i, 0)), _wspec(W, 1)] + [_wspec(d, 1) for d in deps],
        out_specs=pl.BlockSpec((TM, K), lambda i: (i, 0)),
        out_shape=jax.ShapeDtypeStruct((M, K), F32), compiler_params=_cp(1))(A, W, *deps)


def _mm_tn(A, G, name, tn=512, shards=None):
    M, K = A.shape
    N = G.shape[1]
    tk = 1024 if M % 1024 == 0 else 512
    if shards:
        tn = N // shards

    def body(a_ref, g_ref, o_ref):
        @pl.when(pl.program_id(1) == 0)
        def _():
            o_ref[...] = jnp.zeros_like(o_ref)

        upd = _dot_tn(a_ref[...], g_ref[...])
        if shards:
            o_ref[0] += upd
        else:
            o_ref[...] += upd

    if shards:
        out_spec = pl.BlockSpec((1, K, tn), lambda n, k: (n, 0, 0))
        out_shape = jax.ShapeDtypeStruct((shards, K, tn), F32)
    else:
        out_spec = pl.BlockSpec((K, tn), lambda n, k: (0, n))
        out_shape = jax.ShapeDtypeStruct((K, N), F32)
    return pl.pallas_call(
        body, name=name, grid=(N // tn, M // tk),
        in_specs=[pl.BlockSpec((tk, K), lambda n, k: (k, 0)), pl.BlockSpec((tk, tn), lambda n, k: (k, n))],
        out_specs=out_spec, out_shape=out_shape, compiler_params=_cp(2))(A, G)


def _logsig(z):
    return jnp.minimum(z, 0.0) - jnp.log(1.0 + jnp.exp(-jnp.abs(z)))


ZW = 2688
ZRQ, ZRFF, ZRFB, ZLR = 1024, 1536, 2048, 2560


def _mix_in_features(H, W, a2p, biasp, lbp, B, nt):
    K = W.shape[0]

    def body(h_ref, w_ref, a2_ref, bias_ref, lb_ref, q_ref, k_ref, v_ref, g_ref, z_ref):
        hv = h_ref[...]
        proj = lambda lo, hi: _dot(hv, w_ref[:, lo:hi])
        z_ref[:, 0:2 * 4 * HD] = proj(GG, GG + 2 * 4 * HD)
        lrf = proj(LR, LR + HD)
        z_ref[:, ZLR:ZLR + HD] = lrf
        lr = lrf.astype(BF16)
        lane = lax.broadcasted_iota(jnp.int32, (1, 4 * HD), 1)
        keep = (lane & (HD - 1)) < GLA_DK
        for d in range(2):
            z = _dot(lr, a2_ref[d]) + bias_ref[d:d + 1, :]
            gl = jnp.where(keep, _logsig(z) * (1.0 / GATE_NORM), 0.0)
            for h in range(4):
                g_ref[d, 0, h] = gl[:, h * HD:(h + 1) * HD]
        gq, gk, gv = proj(GQ, GQ + 4 * HD), proj(GK, GK + 4 * HD), proj(GV, GV + 4 * HD)
        for h in range(4):
            sl = slice(h * HD, (h + 1) * HD)
            q_ref[0, h] = gq[:, sl] * (GLA_DK ** -0.5)
            k_ref[0, 0, h] = gk[:, sl]
            k_ref[1, 0, h] = gk[:, sl]
            v_ref[0, h] = gv[:, sl].astype(BF16)
        rqs, ri = proj(RQ, RQ + 4 * HD), proj(RI, RI + 4 * HD)
        z_ref[:, ZRQ:ZRQ + 4 * HD] = rqs
        for h in range(4):
            sl = slice(h * HD, (h + 1) * HD)
            rq = rqs[:, sl]
            q_ref[0, 4 + h] = rq * _sigmoid(rq) * (HD ** -0.5)
            v_ref[0, 4 + h] = ri[:, sl].astype(BF16)
        for d, off, zoff in ((0, RFF, ZRFF), (1, RFB, ZRFB)):
            rf = proj(off, off + 4 * HD)
            z_ref[:, zoff:zoff + 4 * HD] = rf
            for h in range(4):
                sl = slice(h * HD, (h + 1) * HD)
                lb = lb_ref[d:d + 1, sl]
                f = lb + (1.0 - lb) * _sigmoid(rf[:, sl])
                g_ref[d, 0, 4 + h] = jnp.log(f)
                k_ref[d, 0, 4 + h] = 1.0 - f

    one = pl.BlockSpec((1, NH, TM, HD), lambda b, i: (b, 0, i, 0))
    two = pl.BlockSpec((2, 1, NH, TM, HD), lambda b, i: (0, b, 0, i, 0))
    s1 = jax.ShapeDtypeStruct((B, NH, nt * TM, HD), F32)
    s2 = jax.ShapeDtypeStruct((2, B, NH, nt * TM, HD), F32)
    return pl.pallas_call(
        body, name="mix_in_features", grid=(B, nt),
        in_specs=[pl.BlockSpec((TM, K), _rowmap(nt, 0)), _wspec(W, 2), pl.BlockSpec((2, HD, 4 * HD), lambda b, i: (0, 0, 0)),
                  pl.BlockSpec((2, 4 * HD), lambda b, i: (0, 0)), pl.BlockSpec((2, 4 * HD), lambda b, i: (0, 0))],
        out_specs=[one, two, one, two, pl.BlockSpec((TM, ZW), _rowmap(nt, 0))],
        out_shape=[s1, s2, jax.ShapeDtypeStruct(s1.shape, BF16), s2, jax.ShapeDtypeStruct((B * nt * TM, ZW), F32)],
        compiler_params=_cp(2))(H, W, a2p, biasp, lbp)


def _features_bwd(Z, a2p, biasp, lbp, dQ0, dQ1, dK0, dK1, dV0, dV1, dG0, dG1, dgates, B):
    nt = Z.shape[0] // (B * TM)

    def body(z_ref, a2_ref, bias_ref, lb_ref, dq0, dq1, dk0, dk1, dv0, dv1, dg0, dg1, dgt_ref,
             df_ref, da2_ref, dbias_ref, dlb_ref):
        b_, i = pl.program_id(0), pl.program_id(1)

        @pl.when((b_ == 0) & (i == 0))
        def _():
            da2_ref[...] = jnp.zeros_like(da2_ref)
            dbias_ref[...] = jnp.zeros_like(dbias_ref)
            dlb_ref[...] = jnp.zeros_like(dlb_ref)

        df_ref[:, 0:2 * 4 * HD] = jnp.where(i > 0, dgt_ref[...], 0.0).astype(BF16)
        df_ref[:, LR + HD:] = jnp.zeros((TM, MIXP - LR - HD), BF16)
        lr = z_ref[:, ZLR:ZLR + HD].astype(BF16)
        lane = lax.broadcasted_iota(jnp.int32, (1, 4 * HD), 1)
        keep = (lane & (HD - 1)) < GLA_DK
        dlr = jnp.zeros((TM, HD), F32)
        dgs = (dg0, dg1)
        dks = (dk0, dk1)
        rd = lambda ref, h: ref[0, h].astype(F32)
        for d in range(2):
            z = _dot(lr, a2_ref[d]) + bias_ref[d:d + 1, :]
            dgl = jnp.concatenate([rd(dgs[d], h) for h in range(4)], axis=1)
            dz = jnp.where(keep, dgl * (1.0 / GATE_NORM) * (1.0 - _sigmoid(z)), 0.0)
            dzb = dz.astype(BF16)
            dlr = dlr + _dot_nt(dzb, a2_ref[d])
            da2_ref[d] += _dot_tn(lr, dzb)
            dbias_ref[d:d + 1, :] += _rsum(dz)
        df_ref[:, LR:LR + HD] = dlr.astype(BF16)
        for h in range(4):
            df_ref[:, GQ + h * HD:GQ + (h + 1) * HD] = ((rd(dq0, h) + rd(dq1, h)) * (GLA_DK ** -0.5)).astype(BF16)
            df_ref[:, GK + h * HD:GK + (h + 1) * HD] = (rd(dk0, h) + rd(dk1, h)).astype(BF16)
            df_ref[:, GV + h * HD:GV + (h + 1) * HD] = (rd(dv0, h) + rd(dv1, h)).astype(BF16)
        for h in range(4):
            sl = slice(h * HD, (h + 1) * HD)
            rq = z_ref[:, ZRQ + h * HD:ZRQ + (h + 1) * HD]
            s = _sigmoid(rq)
            dqh = rd(dq0, 4 + h) + rd(dq1, 4 + h)
            df_ref[:, RQ + h * HD:RQ + (h + 1) * HD] = (dqh * (HD ** -0.5) * (s * (1.0 + rq * (1.0 - s)))).astype(BF16)
            df_ref[:, RI + h * HD:RI + (h + 1) * HD] = (rd(dv0, 4 + h) + rd(dv1, 4 + h)).astype(BF16)
            for d, off, zoff in ((0, RFF, ZRFF), (1, RFB, ZRFB)):
                lb = lb_ref[d:d + 1, sl]
                sg = _sigmoid(z_ref[:, zoff + h * HD:zoff + (h + 1) * HD])
                f = lb + (1.0 - lb) * sg
                dff = rd(dgs[d], 4 + h) / f - rd(dks[d], 4 + h)
                df_ref[:, off + h * HD:off + (h + 1) * HD] = (dff * (1.0 - lb) * sg * (1.0 - sg)).astype(BF16)
                dlb_ref[d:d + 1, sl] += _rsum(dff * (1.0 - sg))

    m0 = lambda b, i: (b, 0, i, 0)
    one = lambda m: pl.BlockSpec((1, NH, TM, HD), m)
    return pl.pallas_call(
        body, name="mix_features_bwd", grid=(B, nt),
        in_specs=[pl.BlockSpec((TM, ZW), _rowmap(nt, 0)), pl.BlockSpec((2, HD, 4 * HD), lambda b, i: (0, 0, 0)),
                  pl.BlockSpec((2, 4 * HD), lambda b, i: (0, 0)), pl.BlockSpec((2, 4 * HD), lambda b, i: (0, 0)),
                  one(m0), one(m0), one(m0), one(m0), one(m0), one(m0), one(m0), one(m0),
                  pl.BlockSpec((TM, D), lambda b, i: (b * (nt - 1) + jnp.maximum(i - 1, 0), 0))],
        out_specs=[pl.BlockSpec((TM, MIXP), _rowmap(nt, 0)), pl.BlockSpec((2, HD, 4 * HD), lambda b, i: (0, 0, 0)),
                   pl.BlockSpec((2, 4 * HD), lambda b, i: (0, 0)), pl.BlockSpec((2, 4 * HD), lambda b, i: (0, 0))],
        out_shape=[jax.ShapeDtypeStruct((B * nt * TM, MIXP), BF16), jax.ShapeDtypeStruct((2, HD, 4 * HD), F32),
                   jax.ShapeDtypeStruct((2, 4 * HD), F32), jax.ShapeDtypeStruct((2, 4 * HD), F32)],
        compiler_params=_cp(2))(Z, a2p, biasp, lbp, dQ0, dQ1, dK0, dK1, dV0, dV1, dG0, dG1, dgates)


def _chunk_scan(x, rin, fwd):
    acc = x
    sft = 1
    while sft < CH:
        if fwd:
            acc = acc + jnp.where(rin >= sft, pltpu.roll(acc, sft, 0), 0.0)
        else:
            acc = acc + jnp.where(rin < CH - sft, pltpu.roll(acc, TM - sft, 0), 0.0)
        sft *= 2
    return acc


def _chunk_total(x):
    t = jnp.sum(x.reshape(NCB, CH, HD), axis=1, keepdims=True)
    return jnp.broadcast_to(t, (NCB, CH, HD)).reshape(TM, HD)


def _scan_masks(rev):
    rin = lax.broadcasted_iota(jnp.int32, (TM, HD), 0) & (CH - 1)
    ri = lax.broadcasted_iota(jnp.int32, (SB, SB), 0)
    ci = lax.broadcasted_iota(jnp.int32, (SB, SB), 1)
    same = (ri >> 5) == (ci >> 5)
    lo = same & (ri >= ci)
    up = same & (ri <= ci)
    mask, maskT = (up, lo) if rev else (lo, up)
    re = lax.broadcasted_iota(jnp.int32, (SB, CSB * HD), 0) >> 5
    ce = lax.broadcasted_iota(jnp.int32, (SB, CSB * HD), 1) >> 7
    return rin, mask, maskT, re == ce


def _scan_decay(q, k, g, rin, rev):
    b = _chunk_scan(g, rin, not rev)
    xx = _chunk_total(g) - b
    eb = jnp.exp(b)
    return b, xx, eb, q * eb, k * jnp.exp(-b), k * jnp.exp(xx)


def _sub(x, s):
    return x[s * SB:(s + 1) * SB]


def _expand(xb, mexp):
    return jnp.where(mexp, jnp.concatenate([xb] * CSB, axis=1), jnp.zeros((), xb.dtype))


def _own(x, mexp):
    xm = jnp.where(mexp, x, 0.0)
    acc = xm[:, 0:HD]
    for n in range(1, CSB):
        acc = acc + xm[:, n * HD:(n + 1) * HD]
    return acc


def _stack(per_chunk, s):
    return jnp.concatenate(per_chunk[s * CSB:(s + 1) * CSB], axis=1)


def _state_pass(s0, eb, uts, rev):
    order = range(NCB - 1, -1, -1) if rev else range(NCB)
    states = [None] * NCB
    s = s0
    for n in order:
        row = n * CH if rev else n * CH + CH - 1
        states[n] = s
        s = eb[row:row + 1, :] * s + uts[n // CSB][:, (n % CSB) * HD:(n % CSB + 1) * HD]
    return states, s


def _scan_fwd(Q, K, V, G, rev, B):
    nb = Q.shape[2] // TM
    d = 1 if rev else 0
    rmap = (lambda s: jnp.where(s == 0, 0, nb - s)) if rev else (lambda s: s)

    def body(q_ref, k_ref, v_ref, g_ref, o_ref, st_ref, s_scr):
        @pl.when(pl.program_id(2) == 0)
        def _():
            s_scr[...] = jnp.zeros_like(s_scr)

        rin, mask, _, mexp = _scan_masks(rev)

        def head(p, readout):
            s0 = s_scr[p]
            st_ref[0, p, 0] = s0
            _, _, eb, qd, ki, kt = _scan_decay(q_ref[0, p], k_ref[0, 0, p], g_ref[0, 0, p], rin, rev)
            ktb, vb = kt.astype(BF16), v_ref[0, p]
            uts = [_dot_tn(_sub(vb, s), _expand(_sub(ktb, s), mexp)) for s in range(NSB)]
            states, s_new = _state_pass(s0, eb, uts, rev)
            s_scr[p] = s_new
            if not readout:
                return
            qb, kib = qd.astype(BF16), ki.astype(BF16)
            for s in range(NSB):
                a = jnp.where(mask, _dot_nt(_sub(qb, s), _sub(kib, s)), 0.0)
                o_ref[0, p, s * SB:(s + 1) * SB, :] = (
                    _dot(a.astype(BF16), _sub(vb, s))
                    + _dot_nt(_expand(_sub(qb, s), mexp), _stack(states, s).astype(BF16)))

        @pl.when(pl.program_id(2) >= 1)
        def _():
            for p in range(HP):
                head(p, True)

        @pl.when(pl.program_id(2) == 0)
        def _():
            for p in range(HP):
                head(p, False)

    one = pl.BlockSpec((1, HP, TM, HD), lambda b, h, s: (b, h, rmap(s), 0))
    two = pl.BlockSpec((1, 1, HP, TM, HD), lambda b, h, s: (d, b, h, rmap(s), 0))
    return pl.pallas_call(
        body, name="scan_fwd_rev" if rev else "scan_fwd", grid=(B, NH // HP, nb),
        in_specs=[one, two, one, two],
        out_specs=[one, pl.BlockSpec((1, HP, 1, HD, HD), lambda b, h, s: (b, h, s, 0, 0))],
        out_shape=[jax.ShapeDtypeStruct(Q.shape, F32), jax.ShapeDtypeStruct((B, NH, nb, HD, HD), F32)],
        scratch_shapes=[pltpu.VMEM((HP, HD, HD), F32)],
        compiler_params=_cp(3))(Q, K, V, G)


def _scan_bwd(Q, K, V, G, St, dO, rev, B):
    nb = Q.shape[2] // TM
    d = 1 if rev else 0
    smap = lambda t: nb - 1 - t
    rmap = (lambda t: jnp.where(smap(t) == 0, 0, nb - smap(t))) if rev else smap

    def body(q_ref, k_ref, v_ref, g_ref, st_ref, do_ref, dq_ref, dk_ref, dv_ref, dg_ref, ds_scr):
        t = pl.program_id(2)

        @pl.when(t == 0)
        def _():
            ds_scr[...] = jnp.zeros_like(ds_scr)

        rin, mask, maskT, mexp = _scan_masks(rev)

        def head(p, readout):
            b, xx, eb, qd, ki, kt = _scan_decay(q_ref[0, p], k_ref[0, 0, p], g_ref[0, 0, p], rin, rev)
            qb, kib, ktb, vb = qd.astype(BF16), ki.astype(BF16), kt.astype(BF16), v_ref[0, p]
            kt_exps = [_expand(_sub(ktb, s), mexp) for s in range(NSB)]
            uts = [_dot_tn(_sub(vb, s), kt_exps[s]) for s in range(NSB)]
            states, _ = _state_pass(st_ref[0, p, 0], eb, uts, rev)
            if readout:
                dob = do_ref[0, p].astype(BF16)
                gts = [_dot_tn(_sub(dob, s), _expand(_sub(qb, s), mexp)) for s in range(NSB)]
            order = range(NCB) if rev else range(NCB - 1, -1, -1)
            dsp = [None] * NCB
            t2 = [None] * NCB
            dsc = ds_scr[p]
            for n in order:
                row = n * CH if rev else n * CH + CH - 1
                ebl = eb[row:row + 1, :]
                dsp[n] = dsc
                t2[n] = jnp.broadcast_to(ebl * _rsum(states[n] * dsc), (CH, HD))
                dsc = ebl * dsc
                if readout:
                    dsc = gts[n // CSB][:, (n % CSB) * HD:(n % CSB + 1) * HD] + dsc
            ds_scr[p] = dsc
            dqds, dkis, dkts = [], [], []
            for s in range(NSB):
                v_s = _sub(vb, s)
                dspb = _stack(dsp, s).astype(BF16)
                dv = _dot_nt(kt_exps[s], dspb)
                if readout:
                    q_s, ki_s, do_s = _sub(qb, s), _sub(kib, s), _sub(dob, s)
                    da = jnp.where(mask, _dot_nt(do_s, v_s), 0.0).astype(BF16)
                    dat = jnp.where(maskT, _dot_nt(v_s, do_s), 0.0).astype(BF16)
                    at = jnp.where(maskT, _dot_nt(ki_s, q_s), 0.0).astype(BF16)
                    dqds.append(_dot(da, ki_s) + _own(_dot(do_s, _stack(states, s).astype(BF16)), mexp))
                    dkis.append(_dot(dat, q_s))
                    dv = _dot(at, do_s) + dv
                dv_ref[0, p, s * SB:(s + 1) * SB, :] = dv.astype(BF16)
                dkts.append(_own(_dot(v_s, dspb), mexp))
            dkt = jnp.concatenate(dkts, axis=0)
            z = dkt * kt
            if readout:
                dqd, dki = jnp.concatenate(dqds, axis=0), jnp.concatenate(dkis, axis=0)
                db = dqd * qd - dki * ki
                dq_ref[0, p] = (dqd * eb).astype(BF16)
                dk_ref[0, p] = (dki * jnp.exp(-b) + dkt * jnp.exp(xx)).astype(BF16)
                w = z - db
                dg = _chunk_total(db) - w
            else:
                dq_ref[0, p] = jnp.zeros((TM, HD), BF16)
                dk_ref[0, p] = (dkt * jnp.exp(xx)).astype(BF16)
                w = z
                dg = -z
            dg_ref[0, p] = (dg + _chunk_scan(w, rin, not rev) + jnp.concatenate(t2, axis=0)).astype(BF16)

        @pl.when(smap(t) >= 1)
        def _():
            for p in range(HPB):
                head(p, True)

        @pl.when(smap(t) == 0)
        def _():
            for p in range(HPB):
                head(p, False)

    one = pl.BlockSpec((1, HPB, TM, HD), lambda b, h, t: (b, h, rmap(t), 0))
    two = pl.BlockSpec((1, 1, HPB, TM, HD), lambda b, h, t: (d, b, h, rmap(t), 0))
    lat = pl.BlockSpec((1, HPB, TM, HD), lambda b, h, t: (b, h, jnp.clip(rmap(t) - 1, 0, nb - 2), 0))
    shp = jax.ShapeDtypeStruct(Q.shape, BF16)
    return pl.pallas_call(
        body, name="scan_bwd_rev" if rev else "scan_bwd", grid=(B, NH // HPB, nb),
        in_specs=[one, two, one, two, pl.BlockSpec((1, HPB, 1, HD, HD), lambda b, h, t: (b, h, smap(t), 0, 0)), lat],
        out_specs=[one, one, one, one], out_shape=[shp, shp, shp, shp],
        scratch_shapes=[pltpu.VMEM((HPB, HD, HD), F32)],
        compiler_params=_cp(3))(Q, K, V, G, St, dO)


def _gnorm(O0, O1, F, gains, B, ntl):
    nt = ntl + 1

    def body(o0_ref, o1_ref, f_ref, gn_ref, m_ref):
        for h in range(NH):
            o = o0_ref[0, h] + o1_ref[0, h]
            r = lax.rsqrt(jnp.mean(o * o, axis=-1, keepdims=True) + NORM_EPS)
            gn = gn_ref[0:1, :] if h < 4 else gn_ref[1:2, :]
            gt = f_ref[:, h * HD:(h + 1) * HD]
            m_ref[:, h * HD:(h + 1) * HD] = (o * r * gn * (gt * _sigmoid(gt))).astype(BF16)

    ospec = pl.BlockSpec((1, NH, TM, HD), lambda b, i: (b, 0, i + 1, 0))
    return pl.pallas_call(
        body, name="gated_norm", grid=(B, ntl),
        in_specs=[ospec, ospec, pl.BlockSpec((TM, D), lambda b, i: (b * nt + 1 + i, 0)),
                  pl.BlockSpec((2, HD), lambda b, i: (0, 0))],
        out_specs=pl.BlockSpec((TM, D), _rowmap(ntl, 0)),
        out_shape=jax.ShapeDtypeStruct((B * ntl * TM, D), BF16), compiler_params=_cp(2))(O0, O1, F, gains)


def _gnorm_bwd(dM, O0, O1, F, gains, B, ntl):
    nt = ntl + 1

    def body(dm_ref, o0_ref, o1_ref, f_ref, gn_ref, do_ref, dgt_ref, dgn_ref):
        b_, i = pl.program_id(0), pl.program_id(1)

        @pl.when((b_ == 0) & (i == 0))
        def _():
            dgn_ref[...] = jnp.zeros_like(dgn_ref)

        for h in range(NH):
            o = o0_ref[0, h] + o1_ref[0, h]
            r = lax.rsqrt(jnp.mean(o * o, axis=-1, keepdims=True) + NORM_EPS)
            y = o * r
            gn = gn_ref[0:1, :] if h < 4 else gn_ref[1:2, :]
            gt = f_ref[:, h * HD:(h + 1) * HD]
            s = _sigmoid(gt)
            dm = dm_ref[:, h * HD:(h + 1) * HD]
            don = dm * (gt * s)
            dgt_ref[:, h * HD:(h + 1) * HD] = dm * (y * gn) * (s * (1.0 + gt * (1.0 - s)))
            row = 0 if h < 4 else 1
            dgn_ref[row:row + 1, :] += _rsum(don * y)
            dy = don * gn
            do_ref[0, h] = r * (dy - y * jnp.mean(dy * y, axis=-1, keepdims=True))

    ospec = pl.BlockSpec((1, NH, TM, HD), lambda b, i: (b, 0, i + 1, 0))
    return pl.pallas_call(
        body, name="gated_norm_bwd", grid=(B, ntl),
        in_specs=[pl.BlockSpec((TM, D), _rowmap(ntl, 0)), ospec, ospec,
                  pl.BlockSpec((TM, D), lambda b, i: (b * nt + 1 + i, 0)), pl.BlockSpec((2, HD), lambda b, i: (0, 0))],
        out_specs=[pl.BlockSpec((1, NH, TM, HD), lambda b, i: (b, 0, i, 0)), pl.BlockSpec((TM, D), _rowmap(ntl, 0)),
                   pl.BlockSpec((2, HD), lambda b, i: (0, 0))],
        out_shape=[jax.ShapeDtypeStruct((B, NH, ntl * TM, HD), F32), jax.ShapeDtypeStruct((B * ntl * TM, D), F32),
                   jax.ShapeDtypeStruct((2, HD), F32)],
        compiler_params=_cp(2))(dM, O0, O1, F, gains)


def _sincos_2d(rows, width, dim):
    quarter = dim // 4
    omega = 1.0 / 10000.0 ** (jnp.arange(quarter, dtype=F32) / quarter)

    def emb(n):
        a = jnp.arange(n).astype(F32)[:, None] * omega[None, :]
        return jnp.concatenate([jnp.sin(a), jnp.cos(a)], axis=-1)

    er = jnp.broadcast_to(emb(rows)[:, None, :], (rows, width, dim // 2))
    ec = jnp.broadcast_to(emb(width)[None, :, :], (rows, width, dim // 2))
    return jnp.concatenate([er, ec], axis=-1).reshape(rows * width, dim)


def _pad_heads(w):
    k = w.shape[0]
    return jnp.pad(w.reshape(k, 4, GLA_DK), ((0, 0), (0, 0), (0, HD - GLA_DK))).reshape(k, 4 * HD)


def _unpad_heads(w):
    k = w.shape[0]
    return w.reshape(k, 4, HD)[:, :, :GLA_DK].reshape(k, 4 * GLA_DK)


MIX_N = 1032
MIX_NP = 1152
_SEGS = ([(64 * h, 64, GQ + HD * h) for h in range(4)] + [(256 + 64 * h, 64, GK + HD * h) for h in range(4)]
         + [(512, 512, GV), (1024, 512, GG), (1536, 32, LR), (1568, 512, RQ), (2080, 512, RFF), (2592, 512, RFB),
            (3104, 512, RI), (3616, 512, RG)])


def _mix_in_to_padded(ps):
    k = ps.shape[1]
    parts, pos = [], 0
    for g0, ln, s0 in sorted(_SEGS, key=lambda s: s[2]):
        if s0 > pos:
            parts.append(jnp.zeros((k, s0 - pos), ps.dtype))
        for j in range(4):
            lo, hi = max(g0, j * MIX_N), min(g0 + ln, (j + 1) * MIX_N)
            if lo < hi:
                parts.append(ps[j][:, lo - j * MIX_N:hi - j * MIX_N])
        pos = s0 + ln
    parts.append(jnp.zeros((k, MIXP - pos), ps.dtype))
    return jnp.concatenate(parts, axis=1)


def _mix_in_from_padded(g):
    k = g.shape[0]
    shards = []
    for j in range(4):
        parts = []
        for g0, ln, s0 in sorted(_SEGS):
            lo, hi = max(g0, j * MIX_N), min(g0 + ln, (j + 1) * MIX_N)
            if lo < hi:
                parts.append(g[:, s0 + lo - g0:s0 + hi - g0])
        parts.append(jnp.zeros((k, MIX_NP - MIX_N), g.dtype))
        shards.append(jnp.concatenate(parts, axis=1))
    return jnp.stack(shards)


def _local_step(x, ctx, tgt, mvec, weights_for, ln_gain, ln_bias, a2f, a2b, abf, abb, lb, gng, gnh, on_grads, on_sent):
    B, T, _ = x.shape
    assert ctx.shape[1] == TM and T % TM == 0
    ntl = T // TM
    nt = ntl + 1
    C, L, CL = (nt, 0), (ntl, 0), (nt, 1)
    pos = _sincos_2d(T // 64, 64, D)
    gbs = [jnp.stack([ln_gain[i], ln_bias[i]]) for i in range(3)]
    a2p = jnp.zeros((2, HD, 4 * HD), F32)
    a2p = a2p.at[0, 0:16].set(_pad_heads(a2f)).at[1, 16:32].set(_pad_heads(a2b)).astype(BF16)
    biasp = jnp.concatenate([_pad_heads(abf.reshape(1, -1)), _pad_heads(abb.reshape(1, -1))], axis=0)
    gains = jnp.concatenate([gng.reshape(1, HD), gnh.reshape(1, HD)], axis=0)

    X0, h0 = _embed_lnmod(x, ctx, pos, mvec)
    w1i, w1o = weights_for("ffn1", h0)
    u0, a0 = _ffn_in(h0, C, w1i, B, nt, "ffn1_in")
    wmp, wmo = weights_for("mix", a0)
    y0 = _mm_nn(a0, C, w1o, B, nt, F32, "ffn1_out")
    X1, h1 = _resid_lnmod(X0, C, y0, mvec, True, 0, 0.5, gbs[0], B, nt, "resid0_lnmod1")
    Q, K, V, G, Zm = _mix_in_features(h1, wmp, a2p, biasp, lb, B, nt)
    w2i, w2o = weights_for("ffn2", Zm)
    O0, S0 = _scan_fwd(Q, K, V, G, False, B)
    O1, S1 = _scan_fwd(Q, K, V, G, True, B)
    merged = _gnorm(O0, O1, Zm, gains, B, ntl)
    y1 = _mm_nn(merged, L, wmo, B, ntl, F32, "mix_out")
    X2, h2 = _resid_lnmod(X1, CL, y1, mvec, False, 1, 1.0, gbs[1], B, ntl, "resid1_lnmod2")
    u2, a2 = _ffn_in(h2, L, w2i, B, ntl, "ffn2_in")
    y2 = _mm_nn(a2, L, w2o, B, ntl, F32, "ffn2_out")

    dx2r, dy2, dgate2, dgb2, lsum = _tail(X2, y2, mvec, gbs[2], tgt, 2, 0.5, B, ntl)
    loss = (0.5 / D) * jnp.sum(lsum)
    du2 = _ffn_out_dx(dy2, w2o, u2, "ffn2_out_dx")
    g_w2o = _mm_tn(a2, dy2, "ffn2_out_dw")
    dh2 = _mm_nt(du2, w2i, "ffn2_in_dx")
    g_w2i = _mm_tn(h2, du2, "ffn2_in_dw", shards=4)
    tok = on_grads("ffn2", (g_w2i, g_w2o))
    dx1r, dy1, dgate1, dgb1, dss2 = _lnmod_resid_bwd(dh2, X2, L, dx2r, X1, CL, y1, mvec, False, 2, 1.0,
                                                     gbs[1] + tok[0, 0], B, ntl, "lnmod2_resid1_bwd")
    tok = on_sent("ffn2", dy1)
    dmerged = _mm_nt(dy1, wmo, "mix_out_dx", dep=tok)
    g_wmo = _mm_tn(merged, dy1, "mix_out_dw", tn=D)
    dO, dgates, dgains = _gnorm_bwd(dmerged, O0, O1, Zm, gains, B, ntl)
    dQ0, dK0, dV0, dG0 = _scan_bwd(Q, K, V, G, S0, dO, False, B)
    dQ1, dK1, dV1, dG1 = _scan_bwd(Q, K, V, G, S1, dO, True, B)
    dF, da2p, dbiasp, dlb = _features_bwd(Zm, a2p, biasp, lb, dQ0, dQ1, dK0, dK1, dV0, dV1, dG0, dG1, dgates, B)
    dh1 = _mm_nt(dF, wmp, "mix_in_dx")
    g_wmp = _mm_tn(h1, dF, "mix_in_dw", tn=MIXP // 4)
    tok = on_grads("mix", (g_wmp, g_wmo))
    dx0r, dy0, dgate0, dgb0, dss1 = _lnmod_resid_bwd(dh1, X1, C, dx1r, X0, C, y0, mvec, True, 1, 0.5,
                                                     gbs[0] + tok[0, 0], B, nt, "lnmod1_resid0_bwd")
    tok = on_sent("mix", dy0)
    du0 = _ffn_out_dx(dy0, w1o, u0, "ffn1_out_dx", dep=tok)
    g_w1o = _mm_tn(a0, dy0, "ffn1_out_dw")
    g_w1i = _mm_tn(h0, du0, "ffn1_in_dw", shards=4)
    tok = on_grads("ffn1", (g_w1i, g_w1o))
    dh0 = _mm_nt(du0, w1i, "ffn1_in_dx", dep=tok)
    tok = on_sent("ffn1", dh0)
    grad_x, dss0 = _lnmod0_bwd(dh0, X0, mvec + tok[0, 0], dx0r, x.shape, B, nt)

    zero_ctx = lambda a: a.at[:, 0].set(0.0)
    dm = jnp.concatenate([dss0, dgate0, dss1, zero_ctx(dgate1), zero_ctx(dss2), zero_ctx(dgate2)], axis=2)
    small = dict(
        ln_gain=jnp.stack([dgb0[0], dgb1[0], dgb2[0]]), ln_bias=jnp.stack([dgb0[1], dgb1[1], dgb2[1]]),
        a2f=_unpad_heads(da2p[0, 0:16]), a2b=_unpad_heads(da2p[1, 16:32]),
        abf=_unpad_heads(dbiasp[0:1]), abb=_unpad_heads(dbiasp[1:2]), lb=dlb, gng=dgains[0], gnh=dgains[1])
    return loss, grad_x, dm, small


def _small_allgather(xs, name):
    r, n = xs.shape

    def body(x_ref, out_ref, send_sems, recv_sems, local_sem):
        x, y, c = lax.axis_index("x"), lax.axis_index("y"), lax.axis_index("c")
        me, sibling = (x, y, c), (x, y, 1 - c)
        chips = [(1 - x, y), (x, 1 - y), (1 - x, 1 - y)]

        def rows(px, py, pc):
            return out_ref.at[pl.ds((4 * px + 2 * py + pc) * r, r), :]

        def copy(k, block, to, src=None):
            return pltpu.make_async_remote_copy(
                src_ref=rows(*block) if src is None else src, dst_ref=rows(*block),
                send_sem=send_sems.at[k], recv_sem=recv_sems.at[k], device_id=to, device_id_type=MESH)

        mine = pltpu.make_async_copy(x_ref, rows(*me), local_sem)
        mine.start()
        first = [copy(0, me, sibling, src=x_ref)]
        first += [copy(1 + j, me, (*chip, c), src=x_ref) for j, chip in enumerate(chips)]
        for cp in first:
            cp.start()
        passed = [copy(4 + j, (*chip, c), sibling) for j, chip in enumerate(chips)]
        for j, chip in enumerate(chips):
            copy(1 + j, (*chip, c), me).wait_recv()
            passed[j].start()
        copy(0, sibling, me).wait_recv()
        for j, chip in enumerate(chips):
            copy(4 + j, (*chip, 1 - c), me).wait_recv()
        for cp in first + passed:
            cp.wait_send()
        mine.wait()

    out = pl.pallas_call(
        body, name=name,
        out_shape=jax.ShapeDtypeStruct((8 * r, n), xs.dtype),
        in_specs=[pl.BlockSpec(memory_space=pltpu.VMEM)],
        out_specs=pl.BlockSpec(memory_space=pltpu.VMEM),
        scratch_shapes=[pltpu.SemaphoreType.DMA((7,)), pltpu.SemaphoreType.DMA((7,)), pltpu.SemaphoreType.DMA],
        compiler_params=pltpu.CompilerParams(vmem_limit_bytes=VMEM_LIMIT))(xs)
    return out.reshape(8, r, n)


def _gather_flat(v, name):
    n = v.shape[0]
    npad = -(-n // 1024) * 1024
    g = _small_allgather(jnp.pad(v, (0, npad - n)).reshape(8, npad // 8), name)
    return g.reshape(8, npad)[:, :n]


HBM_SPEC = pl.BlockSpec(memory_space=pltpu.HBM)
SEM_SPEC = pl.BlockSpec(memory_space=pltpu.SEMAPHORE)
DATAFLOW = pltpu.SideEffectType.DATAFLOW_SIDE_EFFECTING


def _gather_copies(xs, outs, send_sems, recv_sems):
    x, y, c = lax.axis_index("x"), lax.axis_index("y"), lax.axis_index("c")
    dests = [(x, y, 1 - c), (1 - x, y, c), (x, 1 - y, c), (1 - x, 1 - y, c)]
    return [pltpu.make_async_remote_copy(
        src_ref=xs[w], dst_ref=outs[w].at[4 * x + 2 * y + c], send_sem=send_sems[4 * w + k],
        recv_sem=recv_sems[4 * w + k], device_id=dests[k], device_id_type=MESH)
        for w in range(len(xs)) for k in range(4)]


def _split_start(copies, per_w, srcs, land_lead, after, name):
    n = len(srcs)
    m = per_w * n
    lands = [lax.empty((land_lead,) + s.shape[-2:], s.dtype) for s in srcs]
    deps = [] if after is None else [after]

    def body(*refs):
        xs, ls, outs = refs[:n], refs[n:2 * n], refs[2 * n + len(deps):]
        for cp in copies(xs, ls, outs[:m], outs[m:2 * m]):
            cp.start()
        token = outs[2 * m + 2 * n]
        token[...] = jnp.zeros_like(token)

    outs = pl.pallas_call(
        body, name=name,
        out_shape=([pltpu.SemaphoreType.DMA(())] * (2 * m) + [pltpu.HBM(a.shape, a.dtype) for a in srcs + lands]
                   + [jax.ShapeDtypeStruct((8, 128), F32)]),
        in_specs=[HBM_SPEC] * (2 * n) + [pl.BlockSpec(memory_space=pl.ANY)] * len(deps),
        out_specs=[SEM_SPEC] * (2 * m) + [HBM_SPEC] * (2 * n) + [pl.BlockSpec(memory_space=pltpu.VMEM)],
        input_output_aliases={w: 2 * m + w for w in range(2 * n)},
        compiler_params=pltpu.CompilerParams(has_side_effects=DATAFLOW),
    )(*[pltpu.with_memory_space_constraint(a, pltpu.HBM) for a in srcs + lands], *deps)
    return outs[:2 * m], outs[2 * m:2 * m + n], outs[2 * m + n:2 * m + 2 * n], outs[2 * m + 2 * n]


def _split_wait(copies, per_w, sems, x_thru, l_thru, after, name):
    n = len(x_thru)
    m = per_w * n

    def body(*refs):
        xs, ls, ss = refs[:n], refs[n:2 * n], refs[2 * n:2 * n + 2 * m]
        for cp in copies(xs, ls, ss[:m], ss[m:]):
            cp.wait_send()
            cp.wait_recv()

    outs = pl.pallas_call(
        body, name=name,
        out_shape=[pltpu.HBM(a.shape, a.dtype) for a in list(x_thru) + list(l_thru)],
        in_specs=[HBM_SPEC] * (2 * n) + [SEM_SPEC] * (2 * m) + [pl.BlockSpec(memory_space=pl.ANY)],
        out_specs=[HBM_SPEC] * (2 * n),
        input_output_aliases={w: w for w in range(2 * n)},
        compiler_params=pltpu.CompilerParams(has_side_effects=DATAFLOW),
    )(*x_thru, *l_thru, *sems, after)
    return outs[:n], outs[n:]


def _gather_forward(gathered, name):
    n = len(gathered)

    def body(*refs):
        outs = refs[n:2 * n]
        send_sems, recv_sems = refs[2 * n:]
        x, y, c = lax.axis_index("x"), lax.axis_index("y"), lax.axis_index("c")
        chips = [(1 - x, y), (x, 1 - y), (1 - x, 1 - y), (x, y)]

        def copy(w, j, pc):
            px, py = chips[j]
            slot = outs[w].at[4 * px + 2 * py + pc]
            return pltpu.make_async_remote_copy(
                src_ref=slot, dst_ref=slot, send_sem=send_sems.at[4 * w + j], recv_sem=recv_sems.at[4 * w + j],
                device_id=(x, y, 1 - c), device_id_type=MESH)

        sends = [copy(w, j, 1 - c if j == 3 else c) for w in range(n) for j in range(4)]
        for cp in sends:
            cp.start()
        for w in range(n):
            for j in range(4):
                copy(w, j, c if j == 3 else 1 - c).wait_recv()
        for cp in sends:
            cp.wait_send()

    any_spec = pl.BlockSpec(memory_space=pl.ANY)
    return pl.pallas_call(
        body, name=name,
        out_shape=[jax.ShapeDtypeStruct(g.shape, g.dtype) for g in gathered],
        in_specs=[any_spec] * n, out_specs=[any_spec] * n,
        input_output_aliases={w: w for w in range(n)},
        scratch_shapes=[pltpu.SemaphoreType.DMA((4 * n,)), pltpu.SemaphoreType.DMA((4 * n,))],
    )(*gathered)


def _pair_copies(gs, ls, send_sems, recv_sems):
    x, y, c = lax.axis_index("x"), lax.axis_index("y"), lax.axis_index("c")
    return [pltpu.make_async_remote_copy(
        src_ref=gs[w].at[2 * j + 1 - c], dst_ref=ls[w].at[j], send_sem=send_sems[4 * w + j],
        recv_sem=recv_sems[4 * w + j], device_id=(x, y, 1 - c), device_id_type=MESH)
        for w in range(len(gs)) for j in range(4)]


def _chip_copies(hs, ls, send_sems, recv_sems):
    x, y, c = lax.axis_index("x"), lax.axis_index("y"), lax.axis_index("c")
    chips = [(1 - x, y), (x, 1 - y), (1 - x, 1 - y)]
    return [pltpu.make_async_remote_copy(
        src_ref=hs[w].at[2 * px + py], dst_ref=ls[w].at[k], send_sem=send_sems[3 * w + k],
        recv_sem=recv_sems[3 * w + k], device_id=(px, py, c), device_id_type=MESH)
        for w in range(len(hs)) for k, (px, py) in enumerate(chips)]


def _rs_pair_share(fins, name):
    n = len(fins)

    def body(*refs):
        outs = refs[n:2 * n]
        send_sems, recv_sems = refs[2 * n:]
        x, y, c = lax.axis_index("x"), lax.axis_index("y"), lax.axis_index("c")

        def copy(w, slot):
            return pltpu.make_async_remote_copy(
                src_ref=outs[w].at[slot], dst_ref=outs[w].at[slot], send_sem=send_sems.at[w], recv_sem=recv_sems.at[w],
                device_id=(x, y, 1 - c), device_id_type=MESH)

        sends = [copy(w, c) for w in range(n)]
        for cp in sends:
            cp.start()
        for w in range(n):
            copy(w, 1 - c).wait_recv()
        for cp in sends:
            cp.wait_send()

    any_spec = pl.BlockSpec(memory_space=pl.ANY)
    return pl.pallas_call(
        body, name=name,
        out_shape=[jax.ShapeDtypeStruct(f.shape, f.dtype) for f in fins],
        in_specs=[any_spec] * n, out_specs=[any_spec] * n,
        input_output_aliases={w: w for w in range(n)},
        scratch_shapes=[pltpu.SemaphoreType.DMA((n,)), pltpu.SemaphoreType.DMA((n,))])(*fins)


def _rs_add_pair(g8, r4, c, name):
    R, n = g8.shape[1:]
    rb = R // 2

    def body(c_ref, g_ref, r_ref, o_ref):
        o_ref[...] = (g_ref[...] + r_ref[...]).astype(BF16)

    spec = pl.BlockSpec((1, rb, n), lambda j, i, c_ref: (j, i, 0))
    return pl.pallas_call(
        body, name=name,
        grid_spec=pltpu.PrefetchScalarGridSpec(
            num_scalar_prefetch=1, grid=(4, R // rb),
            in_specs=[pl.BlockSpec((1, rb, n), lambda j, i, c_ref: (2 * j + c_ref[0], i, 0)), spec],
            out_specs=spec),
        out_shape=jax.ShapeDtypeStruct((4, R, n), BF16), compiler_params=_cp(2))(c, g8, r4)


def _rs_add_chips(g8, r4, r3, cj, name):
    R, n = g8.shape[1:]
    rb = R // 2

    def body(cj_ref, g_ref, p_ref, r_ref, o_ref):
        own = g_ref[0] + p_ref[0]
        o_ref[0] = ((own + r_ref[0].astype(F32)) + r_ref[1].astype(F32)) + r_ref[2].astype(F32)

    return pl.pallas_call(
        body, name=name,
        grid_spec=pltpu.PrefetchScalarGridSpec(
            num_scalar_prefetch=1, grid=(R // rb,),
            in_specs=[pl.BlockSpec((1, rb, n), lambda i, cj_ref: (2 * cj_ref[1] + cj_ref[0], i, 0)),
                      pl.BlockSpec((1, rb, n), lambda i, cj_ref: (cj_ref[1], i, 0)),
                      pl.BlockSpec((3, rb, n), lambda i, cj_ref: (0, i, 0))],
            out_specs=pl.BlockSpec((1, rb, n), lambda i, cj_ref: (cj_ref[0], i, 0))),
        out_shape=jax.ShapeDtypeStruct((2, R, n), F32), compiler_params=_cp(1))(cj, g8, r4, r3)


def _sum8(g):
    n = g.shape[1]

    def body(g_ref, o_ref):
        acc = g_ref[0:1, :]
        for k in range(1, 8):
            acc = acc + g_ref[k:k + 1, :]
        o_ref[...] = acc

    return pl.pallas_call(body, name="sum_devices", out_shape=jax.ShapeDtypeStruct((1, n), F32),
                          compiler_params=pltpu.CompilerParams(vmem_limit_bytes=VMEM_LIMIT))(g)


ADA_ROWS = 64


def _ada_fwd(cs, w, b):
    n = w.shape[1]

    def body(c_ref, w_ref, b_ref, o_ref):
        cv = c_ref[...]
        s = (cv * _sigmoid(cv)).astype(BF16)
        o_ref[...] = _dot(s, w_ref[...].astype(BF16)) + b_ref[...]

    return pl.pallas_call(body, name="ada_fwd", out_shape=jax.ShapeDtypeStruct((ADA_ROWS, n), F32),
                          compiler_params=pltpu.CompilerParams(vmem_limit_bytes=VMEM_LIMIT))(cs, w, b)


def _ada_bwd(cs, w, dm):
    n = w.shape[1]

    def body(c_ref, w_ref, dm_ref, gw_ref, dc_ref):
        cv = c_ref[...]
        s = (cv * _sigmoid(cv)).astype(BF16)
        gw_ref[...] = _dot_tn(s, dm_ref[...].astype(BF16))
        dc_ref[...] = _dot_nt(dm_ref[32:40, :].astype(BF16), w_ref[...].astype(BF16))

    return pl.pallas_call(
        body, name="ada_bwd",
        out_shape=[jax.ShapeDtypeStruct((D, n), F32), jax.ShapeDtypeStruct((8, D), F32)],
        compiler_params=pltpu.CompilerParams(vmem_limit_bytes=VMEM_LIMIT))(cs, w, dm)


def _adamw(w, g, m, v, name):
    r, c = w.shape
    rb = r
    if r % 8 == 0 and r * c * 4 > (1 << 20):
        rb = 8
        for cand in range(8, r, 8):
            if r % cand == 0 and cand * c * 4 <= (1 << 20):
                rb = cand

    def body(w_ref, g_ref, m_ref, v_ref, go_ref, d_ref, nm_ref, nv_ref):
        gv = g_ref[...]
        go_ref[...] = gv
        mn = ADAM_B1 * m_ref[...] + (1.0 - ADAM_B1) * gv
        vn = ADAM_B2 * v_ref[...] + (1.0 - ADAM_B2) * (gv * gv)
        m_hat = mn / (1.0 - ADAM_B1 ** ADAM_STEP)
        v_hat = vn / (1.0 - ADAM_B2 ** ADAM_STEP)
        d_ref[...] = -ADAM_LR * (m_hat / (jnp.sqrt(v_hat) + ADAM_EPS) + ADAM_WD * w_ref[...])
        nm_ref[...] = mn
        nv_ref[...] = vn

    spec = pl.BlockSpec((rb, c), lambda i: (i, 0))
    shp = jax.ShapeDtypeStruct((r, c), F32)
    return pl.pallas_call(body, name=name, grid=(r // rb,), in_specs=[spec] * 4, out_specs=[spec] * 4,
                          out_shape=[shp] * 4, compiler_params=_cp(1))(w, g, m, v)


BIG = ("ffn1_w_in", "ffn1_w_out", "w_mix_in", "w_mix_out", "ffn2_w_in", "ffn2_w_out")


def _half_rows(w, c):
    half = w.shape[0] // 2
    return lax.dynamic_slice_in_dim(w, c * half, half, axis=0)


def _lower_bounds(logits):
    return jnp.cumsum(jax.nn.softmax(logits.astype(F32), axis=1), axis=1)[:, 0]


def kernel(x, c, ctx, c_ctx, w_ada, b_ada, ln_gain, ln_bias, ffn1_w_in, ffn1_w_out, w_mix_in, gla_a2_fwd, gla_a2_bwd, gla_a_bias_fwd, gla_a_bias_bwd, hgrn_lb_logits, gla_norm_gain, hgrn_norm_gain, w_mix_out, ffn2_w_in, ffn2_w_out, loss_target, m_c_ctx, m_w_ada, m_b_ada, m_ln_gain, m_ln_bias, m_ffn1_w_in, m_ffn1_w_out, m_w_mix_in, m_gla_a2_fwd, m_gla_a2_bwd, m_gla_a_bias_fwd, m_gla_a_bias_bwd, m_hgrn_lb_logits, m_gla_norm_gain, m_hgrn_norm_gain, m_w_mix_out, m_ffn2_w_in, m_ffn2_w_out, v_c_ctx, v_w_ada, v_b_ada, v_ln_gain, v_ln_bias, v_ffn1_w_in, v_ffn1_w_out, v_w_mix_in, v_gla_a2_fwd, v_gla_a2_bwd, v_gla_a_bias_fwd, v_gla_a_bias_bwd, v_hgrn_lb_logits, v_gla_norm_gain, v_hgrn_norm_gain, v_w_mix_out, v_ffn2_w_in, v_ffn2_w_out):
    xi, yi, ci = lax.axis_index("x"), lax.axis_index("y"), lax.axis_index("c")
    chip = 2 * xi + yi
    dev = 2 * chip + ci
    B = x.shape[0]
    weights = dict(ffn1_w_in=ffn1_w_in[0], ffn1_w_out=ffn1_w_out[0], w_mix_in=w_mix_in[0], w_mix_out=w_mix_out[0],
                   ffn2_w_in=ffn2_w_in[0], ffn2_w_out=ffn2_w_out[0])

    mine = jnp.concatenate([c.reshape(-1), ln_gain.reshape(-1), ln_bias.reshape(-1), gla_a2_fwd.reshape(-1),
                            gla_a2_bwd.reshape(-1), hgrn_lb_logits.reshape(-1)])
    g1 = _gather_flat(mine, "gather_cond")
    nc = B * D
    c_all = g1[:, :nc].reshape(8 * B, D)
    per_chip = g1[0::2, nc:]
    o = 0

    def take(shape, axis):
        nonlocal o
        n = int(np.prod(shape))
        parts = per_chip[:, o:o + n].reshape((4,) + shape)
        o += n
        return jnp.concatenate([parts[j] for j in range(4)], axis=axis)

    ln_gain_f = take((3, 256), 1)
    ln_bias_f = take((3, 256), 1)
    a2f_f = take((16, 64), 1)
    a2b_f = take((16, 64), 1)
    lbl_f = take((2, 2, 128), 2)
    lb, lb_vjp = jax.vjp(_lower_bounds, lbl_f)

    cs = jnp.concatenate([c_all, c_ctx.reshape(1, D), jnp.zeros((ADA_ROWS - 8 * B - 1, D), F32)], axis=0)
    ncol = w_ada.shape[2]
    b_cols = lax.dynamic_slice_in_dim(b_ada, chip * ncol, ncol, axis=1)
    m_cols = _ada_fwd(cs, w_ada[0], b_cols)
    g2 = _small_allgather(m_cols, "gather_mod")[0::2]
    m_all = jnp.concatenate([g2[j] for j in range(4)], axis=1)
    m_lat = lax.dynamic_slice_in_dim(m_all, dev * B, B, axis=0).reshape(B, 1, N_MOD, D)
    m_ctx = jnp.broadcast_to(m_all[8 * B].reshape(1, 1, N_MOD, D), (B, 1, N_MOD, D))

    groups = dict(ffn2=("ffn2_w_in", "ffn2_w_out"), mix=("w_mix_in", "w_mix_out"), ffn1=("ffn1_w_in", "ffn1_w_out"))
    shards = dict(weights, w_mix_in=jnp.pad(weights["w_mix_in"], ((0, 0), (0, MIX_NP - MIX_N))))
    blks = {k: _half_rows(shards[k], ci).astype(BF16) for k in BIG}
    gathering = {}
    token = m_all
    for group in ("ffn1", "mix", "ffn2"):
        sems, x_thru, l_thru, token = _split_start(_gather_copies, 4, [blks[k] for k in groups[group]], 8, token,
                                                   "weight_gather_start_" + group)
        gathering[group] = (sems, x_thru, l_thru)
    mvec = jnp.concatenate([m_ctx, m_lat], axis=1) + token[0, 0]

    def weights_for(group, after):
        names = groups[group]
        _, got = _split_wait(_gather_copies, 4, *gathering[group], after, "weight_gather_wait_" + group)
        w_in, w_out = _gather_forward(got, "weight_gather_forward_" + group)
        if group == "mix":
            return _mix_in_to_padded(w_in.reshape(4, D, MIX_NP)), w_out.reshape(-1, D)
        return w_in.reshape((4,) + shards[names[0]].shape), w_out.reshape(-1, D)

    cvec = ci.reshape(1).astype(jnp.int32)
    cjvec = jnp.stack([ci, chip]).astype(jnp.int32)
    in_flight = {}

    def on_grads(group, gs):
        names = groups[group]
        if group == "mix":
            gs = (_mix_in_from_padded(gs[0]), gs[1])
        g8s = [g.reshape((8, shards[k].shape[0] // 2, shards[k].shape[1])) for k, g in zip(names, gs)]
        sems, g_thru, l_thru, token = _split_start(_pair_copies, 4, g8s, 4, None, "grad_pair_start_" + group)
        in_flight[group] = (sems, g_thru, l_thru)
        return token

    def on_sent(group, after):
        g8s, r4s = _split_wait(_pair_copies, 4, *in_flight[group], after, "grad_pair_wait_" + group)
        h4s = [_rs_add_pair(g, r, cvec, "grad_pair_add_" + k) for k, g, r in zip(groups[group], g8s, r4s)]
        sems, h_thru, l_thru, token = _split_start(_chip_copies, 3, h4s, 3, None, "grad_chip_start_" + group)
        in_flight[group] = (g8s, r4s, sems, h_thru, l_thru)
        return token

    loss_l, grad_x, dm, small = _local_step(
        x, ctx, loss_target, mvec, weights_for, ln_gain_f, ln_bias_f, a2f_f, a2b_f,
        gla_a_bias_fwd, gla_a_bias_bwd, lb, gla_norm_gain, hgrn_norm_gain, on_grads, on_sent)
    loss = lax.psum(loss_l, ("x", "y", "c"))

    dm_lat = dm[:, 1].reshape(B, N_MOD * D)
    dm_ctx = jnp.sum(dm[:, 0], axis=0).reshape(N_MOD * D)
    keys = ("ln_gain", "ln_bias", "a2f", "a2b", "abf", "abb", "lb", "gng", "gnh")
    flat = jnp.concatenate([dm_lat.reshape(-1), dm_ctx] + [small[k].reshape(-1) for k in keys])
    g3 = _gather_flat(flat, "gather_small_grads")
    nlat = B * N_MOD * D
    dm_all = g3[:, :nlat].reshape(8 * B, N_MOD * D)
    tot = _sum8(g3[:, nlat:])[0]
    dmc_tot = tot[:N_MOD * D]
    o = N_MOD * D
    sg = {}
    for k in keys:
        n = int(np.prod(small[k].shape))
        sg[k] = tot[o:o + n].reshape(small[k].shape)
        o += n
    dm_rows = jnp.concatenate([dm_all, dmc_tot.reshape(1, -1), jnp.zeros((ADA_ROWS - 8 * B - 1, N_MOD * D), F32)], axis=0)
    g_b_ada = (jnp.sum(dm_all, axis=0) + dmc_tot).reshape(1, N_MOD * D)
    g_w_ada, dcc = _ada_bwd(cs, w_ada[0], lax.dynamic_slice_in_dim(dm_rows, chip * ncol, ncol, axis=1))
    g4 = _gather_flat(dcc[0], "gather_cctx")
    dsilu = ((g4[0] + g4[2]) + g4[4]) + g4[6]
    sc = _sigmoid(c_ctx)
    g_c_ctx = dsilu * (sc * (1.0 + c_ctx * (1.0 - sc)))
    (g_lbl,) = lb_vjp(sg["lb"])

    def cols(a, n, axis):
        return lax.dynamic_slice_in_dim(a, chip * n, n, axis=axis)

    fin = {}
    for group in ("ffn2", "mix", "ffn1"):
        g8s, r4s, sems, h_thru, l_thru = in_flight[group]
        _, r3s = _split_wait(_chip_copies, 3, sems, h_thru, l_thru, g_w_ada, "grad_chip_wait_" + group)
        for k, g, r4, r3 in zip(groups[group], g8s, r4s, r3s):
            fin[k] = _rs_add_chips(g, r4, r3, cjvec, "grad_chip_add_" + k)
    fins = [fin[k] for k in BIG]
    gsh = {k: both.reshape(shards[k].shape)[:, :weights[k].shape[1]]
           for k, both in zip(BIG, _rs_pair_share(fins, "grad_pair_share"))}

    grads = dict(
        c_ctx=g_c_ctx, w_ada=g_w_ada[None], b_ada=g_b_ada, ln_gain=cols(sg["ln_gain"], 256, 1)[None],
        ln_bias=cols(sg["ln_bias"], 256, 1)[None], ffn1_w_in=gsh["ffn1_w_in"][None], ffn1_w_out=gsh["ffn1_w_out"][None],
        w_mix_in=gsh["w_mix_in"][None], gla_a2_fwd=cols(sg["a2f"], 64, 1)[None], gla_a2_bwd=cols(sg["a2b"], 64, 1)[None],
        gla_a_bias_fwd=sg["abf"], gla_a_bias_bwd=sg["abb"], hgrn_lb_logits=cols(g_lbl, 128, 2),
        gla_norm_gain=sg["gng"].reshape(1, HD), hgrn_norm_gain=sg["gnh"].reshape(1, HD),
        w_mix_out=gsh["w_mix_out"][None], ffn2_w_in=gsh["ffn2_w_in"][None], ffn2_w_out=gsh["ffn2_w_out"][None])
    params = dict(
        c_ctx=(c_ctx, m_c_ctx, v_c_ctx), w_ada=(w_ada, m_w_ada, v_w_ada), b_ada=(b_ada, m_b_ada, v_b_ada),
        ln_gain=(ln_gain, m_ln_gain, v_ln_gain), ln_bias=(ln_bias, m_ln_bias, v_ln_bias),
        ffn1_w_in=(ffn1_w_in, m_ffn1_w_in, v_ffn1_w_in), ffn1_w_out=(ffn1_w_out, m_ffn1_w_out, v_ffn1_w_out),
        w_mix_in=(w_mix_in, m_w_mix_in, v_w_mix_in), gla_a2_fwd=(gla_a2_fwd, m_gla_a2_fwd, v_gla_a2_fwd),
        gla_a2_bwd=(gla_a2_bwd, m_gla_a2_bwd, v_gla_a2_bwd),
        gla_a_bias_fwd=(gla_a_bias_fwd, m_gla_a_bias_fwd, v_gla_a_bias_fwd),
        gla_a_bias_bwd=(gla_a_bias_bwd, m_gla_a_bias_bwd, v_gla_a_bias_bwd),
        hgrn_lb_logits=(hgrn_lb_logits, m_hgrn_lb_logits, v_hgrn_lb_logits),
        gla_norm_gain=(gla_norm_gain, m_gla_norm_gain, v_gla_norm_gain),
        hgrn_norm_gain=(hgrn_norm_gain, m_hgrn_norm_gain, v_hgrn_norm_gain),
        w_mix_out=(w_mix_out, m_w_mix_out, v_w_mix_out), ffn2_w_in=(ffn2_w_in, m_ffn2_w_in, v_ffn2_w_in),
        ffn2_w_out=(ffn2_w_out, m_ffn2_w_out, v_ffn2_w_out))
    order = list(params.keys())
    big_names = ("w_ada",) + BIG
    upd = {}
    for k in big_names:
        w_, m_, v_ = params[k]
        s2 = w_.shape[-2:]
        g_, d_, nm_, nv_ = _adamw(w_.reshape(s2), grads[k].reshape(s2), m_.reshape(s2), v_.reshape(s2), "adamw_" + k)
        grads[k] = g_
        upd[k] = (d_.reshape(w_.shape), nm_.reshape(w_.shape), nv_.reshape(w_.shape))
    small_names = [k for k in order if k not in big_names]
    sizes = [int(np.prod(params[k][0].shape)) for k in small_names]
    tot_n = sum(sizes)
    npad = -(-tot_n // 1024) * 1024

    def packed(get):
        flat_ = jnp.concatenate([get(k).reshape(-1) for k in small_names])
        return jnp.pad(flat_, (0, npad - tot_n)).reshape(8, npad // 8)

    _, d_s, nm_s, nv_s = _adamw(packed(lambda k: params[k][0]), packed(lambda k: grads[k]),
                                packed(lambda k: params[k][1]), packed(lambda k: params[k][2]), "adamw_small")
    o = 0
    for k, n in zip(small_names, sizes):
        shp = params[k][0].shape
        upd[k] = tuple(a.reshape(-1)[o:o + n].reshape(shp) for a in (d_s, nm_s, nv_s))
        o += n

    return (loss, grad_x, *[grads[k].reshape(params[k][0].shape) for k in order], *[upd[k][0] for k in order],
            *[upd[k][1] for k in order], *[upd[k][2] for k in order])
```

```python
import functools

import numpy as np
import jax
import jax.numpy as jnp
from jax import lax
from jax.experimental import pallas as pl
from jax.experimental.pallas import tpu as pltpu

F32 = jnp.float32
BF16 = jnp.bfloat16
MESH = pl.DeviceIdType.MESH

D = 1024
DFF = 2816
TM = 256
CH = 32
NCB = TM // CH
SB = 128
CSB = SB // CH
NSB = TM // SB
HP = 8
HPB = 8
HD = 128
NH = 8
LN_EPS = 1e-5
NORM_EPS = 1e-6
ALPHA = 2.0 ** 0.25
GATE_NORM = 16.0
GLA_DK = 64
N_MOD = 9
VMEM_LIMIT = 52 * 1024 * 1024

MIXP = 5120
GG, RG, GQ, GK, GV, RQ, RFF, RFB, RI, LR = 0, 512, 1024, 1536, 2048, 2560, 3072, 3584, 4096, 4608
IN_SPLITS = (256, 256, 512, 512, 16, 16, 512, 512, 512, 512, 512)

ADAM_LR, ADAM_B1, ADAM_B2, ADAM_EPS, ADAM_WD, ADAM_STEP = 0.001, 0.9, 0.999, 1e-08, 0.01, 10


def _cp(n_axes):
    return pltpu.CompilerParams(dimension_semantics=("arbitrary",) * n_axes, vmem_limit_bytes=VMEM_LIMIT)


def _rowmap(stride, off):
    return lambda b, i: (b * stride + off + i, 0)


def _mmap(comb):
    if comb:
        return lambda b, i: (b, jnp.minimum(i, 1), 0, 0)
    return lambda b, i: (b, 1, 0, 0)


def _ln(x):
    mu = jnp.mean(x, axis=-1, keepdims=True)
    xc = x - mu
    var = jnp.mean(xc * xc, axis=-1, keepdims=True)
    r = lax.rsqrt(var + LN_EPS)
    return xc * r, r


def _ln_bwd(dxh, xh, r):
    return r * (dxh - jnp.mean(dxh, axis=-1, keepdims=True) - xh * jnp.mean(dxh * xh, axis=-1, keepdims=True))


def _sigmoid(x):
    return 1.0 / (1.0 + jnp.exp(-x))


def _rsum(x):
    return jnp.sum(x, axis=0, keepdims=True)


def _dot(a, b):
    return jnp.dot(a, b, preferred_element_type=F32)


def _dot_nt(a, b):
    return lax.dot_general(a, b, (((1,), (1,)), ((), ())), preferred_element_type=F32)


def _dot_tn(a, b):
    return lax.dot_general(a, b, (((0,), (0,)), ((), ())), preferred_element_type=F32)


def _modulate(xv, m_ref, sub):
    xh, _ = _ln(xv)
    sh = m_ref[0, 0, 3 * sub:3 * sub + 1, :]
    sc = m_ref[0, 0, 3 * sub + 1:3 * sub + 2, :]
    return (xh * (1.0 + sh) + sc).astype(BF16)


def _embed_lnmod(x, ctx, pos, mvec):
    B, T, _ = x.shape
    nt = 1 + T // TM

    def body(x_ref, c_ref, p_ref, m_ref, o_ref, h_ref):
        i = pl.program_id(1)

        @pl.when(i == 0)
        def _():
            o_ref[...] = c_ref[0]

        @pl.when(i > 0)
        def _():
            o_ref[...] = x_ref[0] + p_ref[...]

        h_ref[...] = _modulate(o_ref[...], m_ref, 0)

    rows = pl.BlockSpec((TM, D), lambda b, i: (b * nt + i, 0))
    return pl.pallas_call(
        body, name="embed_lnmod0", grid=(B, nt),
        in_specs=[pl.BlockSpec((1, TM, D), lambda b, i: (b, jnp.maximum(i - 1, 0), 0)),
                  pl.BlockSpec((1, TM, D), lambda b, i: (b, 0, 0)),
                  pl.BlockSpec((TM, D), lambda b, i: (jnp.maximum(i - 1, 0), 0)),
                  pl.BlockSpec((1, 1, N_MOD, D), _mmap(True))],
        out_specs=[rows, rows],
        out_shape=[jax.ShapeDtypeStruct((B * nt * TM, D), F32), jax.ShapeDtypeStruct((B * nt * TM, D), BF16)],
        compiler_params=_cp(2))(x, ctx, pos, mvec)


def _lnmod0_bwd(dh, X, mvec, dres, x_shape, B, nt):
    def body(dh_ref, x_ref, m_ref, dr_ref, dx_ref, dm_ref):
        i = pl.program_id(1)
        xh, r = _ln(x_ref[...])
        sh = m_ref[0, 0, 0:1, :]
        dhv = dh_ref[...]

        @pl.when((i == 0) | (i == 1))
        def _():
            dm_ref[...] = jnp.zeros_like(dm_ref)

        dm_ref[0, 0, 0:1, :] += _rsum(dhv * xh)
        dm_ref[0, 0, 1:2, :] += _rsum(dhv)

        @pl.when(i > 0)
        def _():
            dx_ref[0] = _ln_bwd(dhv * (1.0 + sh), xh, r) + dr_ref[...]

    rows = pl.BlockSpec((TM, D), _rowmap(nt, 0))
    return pl.pallas_call(
        body, name="lnmod0_bwd", grid=(B, nt),
        in_specs=[rows, rows, pl.BlockSpec((1, 1, N_MOD, D), _mmap(True)), rows],
        out_specs=[pl.BlockSpec((1, TM, D), lambda b, i: (b, jnp.maximum(i - 1, 0), 0)),
                   pl.BlockSpec((1, 1, 2, D), _mmap(True))],
        out_shape=[jax.ShapeDtypeStruct(x_shape, F32), jax.ShapeDtypeStruct((B, 2, 2, D), F32)],
        compiler_params=_cp(2))(dh, X, mvec, dres)


def _resid_fwd(x_ref, y_ref, m_ref, gb_ref, sub, w):
    wg = w * m_ref[0, 0, 3 * sub + 2:3 * sub + 3, :]
    y = y_ref[...]
    zh, r = _ln(ALPHA * x_ref[...] + wg * y)
    return y, wg, zh, r


def _resid_grads(do, y, wg, zh, r, w, gb_ref, comb, dx_ref, dy_ref, dg_ref, dgb_ref):
    b_, i = pl.program_id(0), pl.program_id(1)
    dz = _ln_bwd(do * gb_ref[0:1, :], zh, r)
    dx_ref[...] = ALPHA * dz
    dy_ref[...] = (wg * dz).astype(BF16)

    @pl.when((b_ == 0) & (i == 0))
    def _():
        dgb_ref[...] = jnp.zeros_like(dgb_ref)

    dgb_ref[0:1, :] += _rsum(do * zh)
    dgb_ref[1:2, :] += _rsum(do)

    init = (i == 0) | (i == 1) if comb else (i == 0)

    @pl.when(init)
    def _():
        dg_ref[...] = jnp.zeros_like(dg_ref)

    dg_ref[0, 0] += w * _rsum(dz * y)


def _resid_lnmod(X, lx, Y, mvec, comb, sub, w, gb, B, nt, name):
    def body(x_ref, y_ref, m_ref, gb_ref, o_ref, h_ref):
        _, _, zh, _ = _resid_fwd(x_ref, y_ref, m_ref, gb_ref, sub, w)
        xn = zh * gb_ref[0:1, :] + gb_ref[1:2, :]
        o_ref[...] = xn
        h_ref[...] = _modulate(xn, m_ref, sub + 1)

    rows = pl.BlockSpec((TM, D), _rowmap(nt, 0))
    return pl.pallas_call(
        body, name=name, grid=(B, nt),
        in_specs=[pl.BlockSpec((TM, D), _rowmap(*lx)), rows,
                  pl.BlockSpec((1, 1, N_MOD, D), _mmap(comb)), pl.BlockSpec((2, D), lambda b, i: (0, 0))],
        out_specs=[rows, rows],
        out_shape=[jax.ShapeDtypeStruct((B * nt * TM, D), F32), jax.ShapeDtypeStruct((B * nt * TM, D), BF16)],
        compiler_params=_cp(2))(X, Y, mvec, gb)


def _resid_out_shapes(B, nt, comb):
    rows = pl.BlockSpec((TM, D), _rowmap(nt, 0))
    specs = [rows, rows, pl.BlockSpec((1, 1, 1, D), _mmap(comb)), pl.BlockSpec((2, D), lambda b, i: (0, 0))]
    shapes = [jax.ShapeDtypeStruct((B * nt * TM, D), F32), jax.ShapeDtypeStruct((B * nt * TM, D), BF16),
              jax.ShapeDtypeStruct((B, 2, 1, D), F32), jax.ShapeDtypeStruct((2, D), F32)]
    return specs, shapes


def _tail(X, Y, mvec, gb, tgt, sub, w, B, nt):
    def body(x_ref, y_ref, m_ref, gb_ref, t_ref, dx_ref, dy_ref, dg_ref, dgb_ref, l_ref):
        y, wg, zh, r = _resid_fwd(x_ref, y_ref, m_ref, gb_ref, sub, w)
        e = (zh * gb_ref[0:1, :] + gb_ref[1:2, :]) - t_ref[0]

        @pl.when((pl.program_id(0) == 0) & (pl.program_id(1) == 0))
        def _():
            l_ref[...] = jnp.zeros_like(l_ref)

        l_ref[...] += _rsum(e * e)
        _resid_grads(e * (1.0 / D), y, wg, zh, r, w, gb_ref, False, dx_ref, dy_ref, dg_ref, dgb_ref)

    rows = pl.BlockSpec((TM, D), _rowmap(nt, 0))
    specs, shapes = _resid_out_shapes(B, nt, False)
    return pl.pallas_call(
        body, name="resid2_loss_bwd", grid=(B, nt),
        in_specs=[rows, rows, pl.BlockSpec((1, 1, N_MOD, D), _mmap(False)), pl.BlockSpec((2, D), lambda b, i: (0, 0)),
                  pl.BlockSpec((1, TM, D), lambda b, i: (b, i, 0))],
        out_specs=specs + [pl.BlockSpec((1, D), lambda b, i: (0, 0))],
        out_shape=shapes + [jax.ShapeDtypeStruct((1, D), F32)],
        compiler_params=_cp(2))(X, Y, mvec, gb, tgt)


def _lnmod_resid_bwd(dh, Xi, lxi, dres, Xp, lxp, Yp, mvec, comb, sub, w, gb, B, nt, name):
    ntl = nt - 1 if comb else nt

    def body(dh_ref, xi_ref, dr_ref, xp_ref, yp_ref, m_ref, gb_ref, dx_ref, dy_ref, dg_ref, dgb_ref, dm_ref):
        i = pl.program_id(1)
        xh, r = _ln(xi_ref[...])
        sh = m_ref[0, 0, 3 * sub:3 * sub + 1, :]
        dhv = dh_ref[...]
        dr = dr_ref[...]
        if comb:
            dr = jnp.where(i > 0, dr, 0.0)
        do = _ln_bwd(dhv * (1.0 + sh), xh, r) + dr

        init = (i == 0) | (i == 1) if comb else (i == 0)

        @pl.when(init)
        def _():
            dm_ref[...] = jnp.zeros_like(dm_ref)

        dm_ref[0, 0, 0:1, :] += _rsum(dhv * xh)
        dm_ref[0, 0, 1:2, :] += _rsum(dhv)
        y, wg, zh, r2 = _resid_fwd(xp_ref, yp_ref, m_ref, gb_ref, sub - 1, w)
        _resid_grads(do, y, wg, zh, r2, w, gb_ref, comb, dx_ref, dy_ref, dg_ref, dgb_ref)

    rows = pl.BlockSpec((TM, D), _rowmap(nt, 0))
    if comb:
        dres_spec = pl.BlockSpec((TM, D), lambda b, i: (b * ntl + jnp.maximum(i - 1, 0), 0))
    else:
        dres_spec = rows
    specs, shapes = _resid_out_shapes(B, nt, comb)
    return pl.pallas_call(
        body, name=name, grid=(B, nt),
        in_specs=[rows, pl.BlockSpec((TM, D), _rowmap(*lxi)), dres_spec, pl.BlockSpec((TM, D), _rowmap(*lxp)), rows,
                  pl.BlockSpec((1, 1, N_MOD, D), _mmap(comb)), pl.BlockSpec((2, D), lambda b, i: (0, 0))],
        out_specs=specs + [pl.BlockSpec((1, 1, 2, D), _mmap(comb))],
        out_shape=shapes + [jax.ShapeDtypeStruct((B, 2, 2, D), F32)],
        compiler_params=_cp(2))(dh, Xi, dres, Xp, Yp, mvec, gb)


def _ffn_out_dx(dy, W, u, name, dep=None):
    M = dy.shape[0]
    half = DFF // 2
    deps = [] if dep is None else [dep]

    def body(dy_ref, w_ref, u_ref, *rest):
        du_ref = rest[-1]
        dyv = dy_ref[...]
        for j in range(2):
            lo, hi = j * half, (j + 1) * half
            da = _dot_nt(dyv, w_ref[lo:hi, :])
            g = u_ref[:, lo:hi].astype(F32)
            up = u_ref[:, DFF + lo:DFF + hi].astype(F32)
            s = _sigmoid(g)
            du_ref[:, lo:hi] = (da * up * (s * (1.0 + g * (1.0 - s)))).astype(BF16)
            du_ref[:, DFF + lo:DFF + hi] = (da * (g * s)).astype(BF16)

    return pl.pallas_call(
        body, name=name, grid=(M // TM,),
        in_specs=[pl.BlockSpec((TM, D), lambda i: (i, 0)), _wspec(W, 1), pl.BlockSpec((TM, 2 * DFF), lambda i: (i, 0))]
        + [_wspec(d, 1) for d in deps],
        out_specs=pl.BlockSpec((TM, 2 * DFF), lambda i: (i, 0)),
        out_shape=jax.ShapeDtypeStruct((M, 2 * DFF), BF16), compiler_params=_cp(1))(dy, W, u, *deps)


def _wspec(W, nidx):
    zeros = (0,) * W.ndim
    if nidx == 1:
        return pl.BlockSpec(W.shape, lambda i: zeros)
    return pl.BlockSpec(W.shape, lambda b, i: zeros)


def _mm_nn(A, la, W, B, nt, out_dtype, name):
    K, N = W.shape

    def body(a_ref, w_ref, o_ref):
        o_ref[...] = _dot(a_ref[...], w_ref[...]).astype(out_dtype)

    return pl.pallas_call(
        body, name=name, grid=(B, nt),
        in_specs=[pl.BlockSpec((TM, K), _rowmap(*la)), _wspec(W, 2)],
        out_specs=pl.BlockSpec((TM, N), _rowmap(nt, 0)),
        out_shape=jax.ShapeDtypeStruct((B * nt * TM, N), out_dtype), compiler_params=_cp(2))(A, W)


def _ffn_in(A, la, W3, B, nt, name):
    K, n = W3.shape[1:]

    def body(a_ref, w_ref, u_ref, s_ref):
        a = a_ref[...]
        for j in range(2):
            g = _dot(a, w_ref[j])
            up = _dot(a, w_ref[j + 2])
            u_ref[:, j * n:(j + 1) * n] = g.astype(BF16)
            u_ref[:, (j + 2) * n:(j + 3) * n] = up.astype(BF16)
            s_ref[:, j * n:(j + 1) * n] = (g * _sigmoid(g) * up).astype(BF16)

    rows = B * nt * TM
    return pl.pallas_call(
        body, name=name, grid=(B, nt),
        in_specs=[pl.BlockSpec((TM, K), _rowmap(*la)), _wspec(W3, 2)],
        out_specs=[pl.BlockSpec((TM, 4 * n), _rowmap(nt, 0)), pl.BlockSpec((TM, 2 * n), _rowmap(nt, 0))],
        out_shape=[jax.ShapeDtypeStruct((rows, 4 * n), BF16), jax.ShapeDtypeStruct((rows, 2 * n), BF16)],
        compiler_params=_cp(2))(A, W3)


def _mm_nt(A, W, name, dep=None):
    M, N = A.shape
    K = W.shape[-2]

    def body(a_ref, w_ref, *rest):
        o_ref = rest[-1]
        if W.ndim == 3:
            n = W.shape[-1]
            acc = _dot_nt(a_ref[:, 0:n], w_ref[0])
            for j in range(1, 4):
                acc = acc + _dot_nt(a_ref[:, j * n:(j + 1) * n], w_ref[j])
            o_ref[...] = acc
        else:
            o_ref[...] = _dot_nt(a_ref[...], w_ref[...])

    deps = [] if dep is None else [dep]
    return pl.pallas_call(
        body, name=name, grid=(M // TM,),
        in_specs=[pl.BlockSpec((TM, N), lambda i: (i, 0)), _wspec(W, 1)] + [_wspec(d, 1) for d in deps],
        out_specs=pl.BlockSpec((TM, K), lambda i: (i, 0)),
        out_shape=jax.ShapeDtypeStruct((M, K), F32), compiler_params=_cp(1))(A, W, *deps)


def _mm_tn(A, G, name, tn=512, shards=None):
    M, K = A.shape
    N = G.shape[1]
    tk = next(t for t in ((2048, 1536, 1024, 512) if K <= D else (1024, 512)) if M % t == 0)
    if shards:
        tn = N // shards

    def body(a_ref, g_ref, o_ref):
        @pl.when(pl.program_id(1) == 0)
        def _():
            o_ref[...] = jnp.zeros_like(o_ref)

        upd = _dot_tn(a_ref[...], g_ref[...])
        if shards:
            o_ref[0] += upd
        else:
            o_ref[...] += upd

    if shards:
        out_spec = pl.BlockSpec((1, K, tn), lambda n, k: (n, 0, 0))
        out_shape = jax.ShapeDtypeStruct((shards, K, tn), F32)
    else:
        out_spec = pl.BlockSpec((K, tn), lambda n, k: (0, n))
        out_shape = jax.ShapeDtypeStruct((K, N), F32)
    return pl.pallas_call(
        body, name=name, grid=(N // tn, M // tk),
        in_specs=[pl.BlockSpec((tk, K), lambda n, k: (k, 0)), pl.BlockSpec((tk, tn), lambda n, k: (k, n))],
        out_specs=out_spec, out_shape=out_shape, compiler_params=_cp(2))(A, G)


def _logsig(z):
    return jnp.minimum(z, 0.0) - jnp.log(1.0 + jnp.exp(-jnp.abs(z)))


ZW = 2688
ZRQ, ZRFF, ZRFB, ZLR = 1024, 1536, 2048, 2560


def _mix_in_features(H, W, a2p, biasp, lbp, B, nt):
    K = W.shape[0]

    def body(h_ref, w_ref, a2_ref, bias_ref, lb_ref, q_ref, k_ref, v_ref, g_ref, z_ref):
        hv = h_ref[...]
        proj = lambda lo, hi: _dot(hv, w_ref[:, lo:hi])
        z_ref[:, 0:2 * 4 * HD] = proj(GG, GG + 2 * 4 * HD)
        lrf = proj(LR, LR + HD)
        z_ref[:, ZLR:ZLR + HD] = lrf
        lr = lrf.astype(BF16)
        lane = lax.broadcasted_iota(jnp.int32, (1, 4 * HD), 1)
        keep = (lane & (HD - 1)) < GLA_DK
        for d in range(2):
            z = _dot(lr, a2_ref[d]) + bias_ref[d:d + 1, :]
            gl = jnp.where(keep, _logsig(z) * (1.0 / GATE_NORM), 0.0)
            for h in range(4):
                g_ref[d, 0, h] = gl[:, h * HD:(h + 1) * HD]
        gq, gk, gv = proj(GQ, GQ + 4 * HD), proj(GK, GK + 4 * HD), proj(GV, GV + 4 * HD)
        for h in range(4):
            sl = slice(h * HD, (h + 1) * HD)
            q_ref[0, h] = gq[:, sl] * (GLA_DK ** -0.5)
            k_ref[0, 0, h] = gk[:, sl]
            k_ref[1, 0, h] = gk[:, sl]
            v_ref[0, h] = gv[:, sl].astype(BF16)
        rqs, ri = proj(RQ, RQ + 4 * HD), proj(RI, RI + 4 * HD)
        z_ref[:, ZRQ:ZRQ + 4 * HD] = rqs
        for h in range(4):
            sl = slice(h * HD, (h + 1) * HD)
            rq = rqs[:, sl]
            q_ref[0, 4 + h] = rq * _sigmoid(rq) * (HD ** -0.5)
            v_ref[0, 4 + h] = ri[:, sl].astype(BF16)
        for d, off, zoff in ((0, RFF, ZRFF), (1, RFB, ZRFB)):
            rf = proj(off, off + 4 * HD)
            z_ref[:, zoff:zoff + 4 * HD] = rf
            for h in range(4):
                sl = slice(h * HD, (h + 1) * HD)
                lb = lb_ref[d:d + 1, sl]
                f = lb + (1.0 - lb) * _sigmoid(rf[:, sl])
                g_ref[d, 0, 4 + h] = jnp.log(f)
                k_ref[d, 0, 4 + h] = 1.0 - f

    one = pl.BlockSpec((1, NH, TM, HD), lambda b, i: (b, 0, i, 0))
    two = pl.BlockSpec((2, 1, NH, TM, HD), lambda b, i: (0, b, 0, i, 0))
    s1 = jax.ShapeDtypeStruct((B, NH, nt * TM, HD), F32)
    s2 = jax.ShapeDtypeStruct((2, B, NH, nt * TM, HD), F32)
    return pl.pallas_call(
        body, name="mix_in_features", grid=(B, nt),
        in_specs=[pl.BlockSpec((TM, K), _rowmap(nt, 0)), _wspec(W, 2), pl.BlockSpec((2, HD, 4 * HD), lambda b, i: (0, 0, 0)),
                  pl.BlockSpec((2, 4 * HD), lambda b, i: (0, 0)), pl.BlockSpec((2, 4 * HD), lambda b, i: (0, 0))],
        out_specs=[one, two, one, two, pl.BlockSpec((TM, ZW), _rowmap(nt, 0))],
        out_shape=[s1, s2, jax.ShapeDtypeStruct(s1.shape, BF16), s2, jax.ShapeDtypeStruct((B * nt * TM, ZW), F32)],
        compiler_params=_cp(2))(H, W, a2p, biasp, lbp)


def _features_bwd(Z, a2p, biasp, lbp, dQ0, dQ1, dK0, dK1, dV0, dV1, dG0, dG1, dgates, B):
    nt = Z.shape[0] // (B * TM)

    def body(z_ref, a2_ref, bias_ref, lb_ref, dq0, dq1, dk0, dk1, dv0, dv1, dg0, dg1, dgt_ref,
             df_ref, da2_ref, dbias_ref, dlb_ref):
        b_, i = pl.program_id(0), pl.program_id(1)

        @pl.when((b_ == 0) & (i == 0))
        def _():
            da2_ref[...] = jnp.zeros_like(da2_ref)
            dbias_ref[...] = jnp.zeros_like(dbias_ref)
            dlb_ref[...] = jnp.zeros_like(dlb_ref)

        df_ref[:, 0:2 * 4 * HD] = jnp.where(i > 0, dgt_ref[...], 0.0).astype(BF16)
        df_ref[:, LR + HD:] = jnp.zeros((TM, MIXP - LR - HD), BF16)
        lr = z_ref[:, ZLR:ZLR + HD].astype(BF16)
        lane = lax.broadcasted_iota(jnp.int32, (1, 4 * HD), 1)
        keep = (lane & (HD - 1)) < GLA_DK
        dlr = jnp.zeros((TM, HD), F32)
        dgs = (dg0, dg1)
        dks = (dk0, dk1)
        rd = lambda ref, h: ref[0, h].astype(F32)
        for d in range(2):
            z = _dot(lr, a2_ref[d]) + bias_ref[d:d + 1, :]
            dgl = jnp.concatenate([rd(dgs[d], h) for h in range(4)], axis=1)
            dz = jnp.where(keep, dgl * (1.0 / GATE_NORM) * (1.0 - _sigmoid(z)), 0.0)
            dzb = dz.astype(BF16)
            dlr = dlr + _dot_nt(dzb, a2_ref[d])
            da2_ref[d] += _dot_tn(lr, dzb)
            dbias_ref[d:d + 1, :] += _rsum(dz)
        df_ref[:, LR:LR + HD] = dlr.astype(BF16)
        for h in range(4):
            df_ref[:, GQ + h * HD:GQ + (h + 1) * HD] = ((rd(dq0, h) + rd(dq1, h)) * (GLA_DK ** -0.5)).astype(BF16)
            df_ref[:, GK + h * HD:GK + (h + 1) * HD] = (rd(dk0, h) + rd(dk1, h)).astype(BF16)
            df_ref[:, GV + h * HD:GV + (h + 1) * HD] = (rd(dv0, h) + rd(dv1, h)).astype(BF16)
        for h in range(4):
            sl = slice(h * HD, (h + 1) * HD)
            rq = z_ref[:, ZRQ + h * HD:ZRQ + (h + 1) * HD]
            s = _sigmoid(rq)
            dqh = rd(dq0, 4 + h) + rd(dq1, 4 + h)
            df_ref[:, RQ + h * HD:RQ + (h + 1) * HD] = (dqh * (HD ** -0.5) * (s * (1.0 + rq * (1.0 - s)))).astype(BF16)
            df_ref[:, RI + h * HD:RI + (h + 1) * HD] = (rd(dv0, 4 + h) + rd(dv1, 4 + h)).astype(BF16)
            for d, off, zoff in ((0, RFF, ZRFF), (1, RFB, ZRFB)):
                lb = lb_ref[d:d + 1, sl]
                sg = _sigmoid(z_ref[:, zoff + h * HD:zoff + (h + 1) * HD])
                f = lb + (1.0 - lb) * sg
                dff = rd(dgs[d], 4 + h) / f - rd(dks[d], 4 + h)
                df_ref[:, off + h * HD:off + (h + 1) * HD] = (dff * (1.0 - lb) * sg * (1.0 - sg)).astype(BF16)
                dlb_ref[d:d + 1, sl] += _rsum(dff * (1.0 - sg))

    m0 = lambda b, i: (b, 0, i, 0)
    one = lambda m: pl.BlockSpec((1, NH, TM, HD), m)
    return pl.pallas_call(
        body, name="mix_features_bwd", grid=(B, nt),
        in_specs=[pl.BlockSpec((TM, ZW), _rowmap(nt, 0)), pl.BlockSpec((2, HD, 4 * HD), lambda b, i: (0, 0, 0)),
                  pl.BlockSpec((2, 4 * HD), lambda b, i: (0, 0)), pl.BlockSpec((2, 4 * HD), lambda b, i: (0, 0)),
                  one(m0), one(m0), one(m0), one(m0), one(m0), one(m0), one(m0), one(m0),
                  pl.BlockSpec((TM, D), lambda b, i: (b * (nt - 1) + jnp.maximum(i - 1, 0), 0))],
        out_specs=[pl.BlockSpec((TM, MIXP), _rowmap(nt, 0)), pl.BlockSpec((2, HD, 4 * HD), lambda b, i: (0, 0, 0)),
                   pl.BlockSpec((2, 4 * HD), lambda b, i: (0, 0)), pl.BlockSpec((2, 4 * HD), lambda b, i: (0, 0))],
        out_shape=[jax.ShapeDtypeStruct((B * nt * TM, MIXP), BF16), jax.ShapeDtypeStruct((2, HD, 4 * HD), F32),
                   jax.ShapeDtypeStruct((2, 4 * HD), F32), jax.ShapeDtypeStruct((2, 4 * HD), F32)],
        compiler_params=_cp(2))(Z, a2p, biasp, lbp, dQ0, dQ1, dK0, dK1, dV0, dV1, dG0, dG1, dgates)


def _chunk_scan(x, rin, fwd):
    acc = x
    sft = 1
    while sft < CH:
        if fwd:
            acc = acc + jnp.where(rin >= sft, pltpu.roll(acc, sft, 0), 0.0)
        else:
            acc = acc + jnp.where(rin < CH - sft, pltpu.roll(acc, TM - sft, 0), 0.0)
        sft *= 2
    return acc


def _chunk_total(x):
    t = jnp.sum(x.reshape(NCB, CH, HD), axis=1, keepdims=True)
    return jnp.broadcast_to(t, (NCB, CH, HD)).reshape(TM, HD)


def _scan_masks(rev):
    rin = lax.broadcasted_iota(jnp.int32, (TM, HD), 0) & (CH - 1)
    ri = lax.broadcasted_iota(jnp.int32, (SB, SB), 0)
    ci = lax.broadcasted_iota(jnp.int32, (SB, SB), 1)
    same = (ri >> 5) == (ci >> 5)
    lo = same & (ri >= ci)
    up = same & (ri <= ci)
    mask, maskT = (up, lo) if rev else (lo, up)
    re = lax.broadcasted_iota(jnp.int32, (SB, CSB * HD), 0) >> 5
    ce = lax.broadcasted_iota(jnp.int32, (SB, CSB * HD), 1) >> 7
    return rin, mask, maskT, re == ce


def _scan_decay(q, k, g, rin, rev):
    b = _chunk_scan(g, rin, not rev)
    xx = _chunk_total(g) - b
    eb = jnp.exp(b)
    return b, xx, eb, q * eb, k * jnp.exp(-b), k * jnp.exp(xx)


def _sub(x, s):
    return x[s * SB:(s + 1) * SB]


def _expand(xb, mexp):
    return jnp.where(mexp, jnp.concatenate([xb] * CSB, axis=1), jnp.zeros((), xb.dtype))


def _own(x, mexp):
    xm = jnp.where(mexp, x, 0.0)
    acc = xm[:, 0:HD]
    for n in range(1, CSB):
        acc = acc + xm[:, n * HD:(n + 1) * HD]
    return acc


def _stack(per_chunk, s):
    return jnp.concatenate(per_chunk[s * CSB:(s + 1) * CSB], axis=1)


def _state_pass(s0, eb, uts, rev):
    order = range(NCB - 1, -1, -1) if rev else range(NCB)
    states = [None] * NCB
    s = s0
    for n in order:
        row = n * CH if rev else n * CH + CH - 1
        states[n] = s
        s = eb[row:row + 1, :] * s + uts[n // CSB][:, (n % CSB) * HD:(n % CSB + 1) * HD]
    return states, s


def _scan_fwd(Q, K, V, G, rev, B):
    nb = Q.shape[2] // TM
    d = 1 if rev else 0
    rmap = (lambda s: jnp.where(s == 0, 0, nb - s)) if rev else (lambda s: s)

    def body(q_ref, k_ref, v_ref, g_ref, o_ref, st_ref, s_scr):
        @pl.when(pl.program_id(2) == 0)
        def _():
            s_scr[...] = jnp.zeros_like(s_scr)

        rin, mask, _, mexp = _scan_masks(rev)

        def head(p, readout):
            s0 = s_scr[p]
            st_ref[0, p, 0] = s0
            _, _, eb, qd, ki, kt = _scan_decay(q_ref[0, p], k_ref[0, 0, p], g_ref[0, 0, p], rin, rev)
            ktb, vb = kt.astype(BF16), v_ref[0, p]
            uts = [_dot_tn(_sub(vb, s), _expand(_sub(ktb, s), mexp)) for s in range(NSB)]
            states, s_new = _state_pass(s0, eb, uts, rev)
            s_scr[p] = s_new
            if not readout:
                return
            qb, kib = qd.astype(BF16), ki.astype(BF16)
            for s in range(NSB):
                a = jnp.where(mask, _dot_nt(_sub(qb, s), _sub(kib, s)), 0.0)
                o_ref[0, p, s * SB:(s + 1) * SB, :] = (
                    _dot(a.astype(BF16), _sub(vb, s))
                    + _dot_nt(_expand(_sub(qb, s), mexp), _stack(states, s).astype(BF16)))

        @pl.when(pl.program_id(2) >= 1)
        def _():
            for p in range(HP):
                head(p, True)

        @pl.when(pl.program_id(2) == 0)
        def _():
            for p in range(HP):
                head(p, False)

    one = pl.BlockSpec((1, HP, TM, HD), lambda b, h, s: (b, h, rmap(s), 0))
    two = pl.BlockSpec((1, 1, HP, TM, HD), lambda b, h, s: (d, b, h, rmap(s), 0))
    return pl.pallas_call(
        body, name="scan_fwd_rev" if rev else "scan_fwd", grid=(B, NH // HP, nb),
        in_specs=[one, two, one, two],
        out_specs=[one, pl.BlockSpec((1, HP, 1, HD, HD), lambda b, h, s: (b, h, s, 0, 0))],
        out_shape=[jax.ShapeDtypeStruct(Q.shape, F32), jax.ShapeDtypeStruct((B, NH, nb, HD, HD), F32)],
        scratch_shapes=[pltpu.VMEM((HP, HD, HD), F32)],
        compiler_params=_cp(3))(Q, K, V, G)


def _scan_bwd(Q, K, V, G, St, dO, rev, B):
    nb = Q.shape[2] // TM
    d = 1 if rev else 0
    smap = lambda t: nb - 1 - t
    rmap = (lambda t: jnp.where(smap(t) == 0, 0, nb - smap(t))) if rev else smap

    def body(q_ref, k_ref, v_ref, g_ref, st_ref, do_ref, dq_ref, dk_ref, dv_ref, dg_ref, ds_scr):
        t = pl.program_id(2)

        @pl.when(t == 0)
        def _():
            ds_scr[...] = jnp.zeros_like(ds_scr)

        is_lat = smap(t) >= 1
        rin, mask, maskT, mexp = _scan_masks(rev)
        for p in range(HPB):
            b, xx, eb, qd, ki, kt = _scan_decay(q_ref[0, p], k_ref[0, 0, p], g_ref[0, 0, p], rin, rev)
            qb, kib, ktb, vb = qd.astype(BF16), ki.astype(BF16), kt.astype(BF16), v_ref[0, p]
            dob = jnp.where(is_lat, do_ref[0, p], 0.0).astype(BF16)
            kt_exps = [_expand(_sub(ktb, s), mexp) for s in range(NSB)]
            uts = [_dot_tn(_sub(vb, s), kt_exps[s]) for s in range(NSB)]
            states, _ = _state_pass(st_ref[0, p, 0], eb, uts, rev)
            gts = [_dot_tn(_sub(dob, s), _expand(_sub(qb, s), mexp)) for s in range(NSB)]
            order = range(NCB) if rev else range(NCB - 1, -1, -1)
            dsp = [None] * NCB
            t2 = [None] * NCB
            dsc = ds_scr[p]
            for n in order:
                row = n * CH if rev else n * CH + CH - 1
                ebl = eb[row:row + 1, :]
                dsp[n] = dsc
                t2[n] = jnp.broadcast_to(ebl * _rsum(states[n] * dsc), (CH, HD))
                dsc = gts[n // CSB][:, (n % CSB) * HD:(n % CSB + 1) * HD] + ebl * dsc
            ds_scr[p] = dsc
            dqds, dkis, dkts = [], [], []
            for s in range(NSB):
                q_s, ki_s, v_s, do_s = _sub(qb, s), _sub(kib, s), _sub(vb, s), _sub(dob, s)
                dspb = _stack(dsp, s).astype(BF16)
                da = jnp.where(mask, _dot_nt(do_s, v_s), 0.0).astype(BF16)
                dat = jnp.where(maskT, _dot_nt(v_s, do_s), 0.0).astype(BF16)
                at = jnp.where(maskT, _dot_nt(ki_s, q_s), 0.0).astype(BF16)
                dqds.append(_dot(da, ki_s) + _own(_dot(do_s, _stack(states, s).astype(BF16)), mexp))
                dkis.append(_dot(dat, q_s))
                dv_ref[0, p, s * SB:(s + 1) * SB, :] = (_dot(at, do_s) + _dot_nt(kt_exps[s], dspb)).astype(BF16)
                dkts.append(_own(_dot(v_s, dspb), mexp))
            dqd, dki, dkt = (jnp.concatenate(parts, axis=0) for parts in (dqds, dkis, dkts))
            z = dkt * kt
            db = dqd * qd - dki * ki
            dq_ref[0, p] = (dqd * eb).astype(BF16)
            dk_ref[0, p] = (dki * jnp.exp(-b) + dkt * jnp.exp(xx)).astype(BF16)
            dg_ref[0, p] = (_chunk_total(db) + (db - z) + _chunk_scan(z - db, rin, not rev)
                            + jnp.concatenate(t2, axis=0)).astype(BF16)

    one = pl.BlockSpec((1, HPB, TM, HD), lambda b, h, t: (b, h, rmap(t), 0))
    two = pl.BlockSpec((1, 1, HPB, TM, HD), lambda b, h, t: (d, b, h, rmap(t), 0))
    lat = pl.BlockSpec((1, HPB, TM, HD), lambda b, h, t: (b, h, jnp.clip(rmap(t) - 1, 0, nb - 2), 0))
    shp = jax.ShapeDtypeStruct(Q.shape, BF16)
    return pl.pallas_call(
        body, name="scan_bwd_rev" if rev else "scan_bwd", grid=(B, NH // HPB, nb),
        in_specs=[one, two, one, two, pl.BlockSpec((1, HPB, 1, HD, HD), lambda b, h, t: (b, h, smap(t), 0, 0)), lat],
        out_specs=[one, one, one, one], out_shape=[shp, shp, shp, shp],
        scratch_shapes=[pltpu.VMEM((HPB, HD, HD), F32)],
        compiler_params=_cp(3))(Q, K, V, G, St, dO)


def _gnorm(O0, O1, F, gains, B, ntl):
    nt = ntl + 1

    def body(o0_ref, o1_ref, f_ref, gn_ref, m_ref):
        for h in range(NH):
            o = o0_ref[0, h] + o1_ref[0, h]
            r = lax.rsqrt(jnp.mean(o * o, axis=-1, keepdims=True) + NORM_EPS)
            gn = gn_ref[0:1, :] if h < 4 else gn_ref[1:2, :]
            gt = f_ref[:, h * HD:(h + 1) * HD]
            m_ref[:, h * HD:(h + 1) * HD] = (o * r * gn * (gt * _sigmoid(gt))).astype(BF16)

    ospec = pl.BlockSpec((1, NH, TM, HD), lambda b, i: (b, 0, i + 1, 0))
    return pl.pallas_call(
        body, name="gated_norm", grid=(B, ntl),
        in_specs=[ospec, ospec, pl.BlockSpec((TM, D), lambda b, i: (b * nt + 1 + i, 0)),
                  pl.BlockSpec((2, HD), lambda b, i: (0, 0))],
        out_specs=pl.BlockSpec((TM, D), _rowmap(ntl, 0)),
        out_shape=jax.ShapeDtypeStruct((B * ntl * TM, D), BF16), compiler_params=_cp(2))(O0, O1, F, gains)


def _gnorm_bwd(dM, O0, O1, F, gains, B, ntl):
    nt = ntl + 1

    def body(dm_ref, o0_ref, o1_ref, f_ref, gn_ref, do_ref, dgt_ref, dgn_ref):
        b_, i = pl.program_id(0), pl.program_id(1)

        @pl.when((b_ == 0) & (i == 0))
        def _():
            dgn_ref[...] = jnp.zeros_like(dgn_ref)

        for h in range(NH):
            o = o0_ref[0, h] + o1_ref[0, h]
            r = lax.rsqrt(jnp.mean(o * o, axis=-1, keepdims=True) + NORM_EPS)
            y = o * r
            gn = gn_ref[0:1, :] if h < 4 else gn_ref[1:2, :]
            gt = f_ref[:, h * HD:(h + 1) * HD]
            s = _sigmoid(gt)
            dm = dm_ref[:, h * HD:(h + 1) * HD]
            don = dm * (gt * s)
            dgt_ref[:, h * HD:(h + 1) * HD] = dm * (y * gn) * (s * (1.0 + gt * (1.0 - s)))
            row = 0 if h < 4 else 1
            dgn_ref[row:row + 1, :] += _rsum(don * y)
            dy = don * gn
            do_ref[0, h] = r * (dy - y * jnp.mean(dy * y, axis=-1, keepdims=True))

    ospec = pl.BlockSpec((1, NH, TM, HD), lambda b, i: (b, 0, i + 1, 0))
    return pl.pallas_call(
        body, name="gated_norm_bwd", grid=(B, ntl),
        in_specs=[pl.BlockSpec((TM, D), _rowmap(ntl, 0)), ospec, ospec,
                  pl.BlockSpec((TM, D), lambda b, i: (b * nt + 1 + i, 0)), pl.BlockSpec((2, HD), lambda b, i: (0, 0))],
        out_specs=[pl.BlockSpec((1, NH, TM, HD), lambda b, i: (b, 0, i, 0)), pl.BlockSpec((TM, D), _rowmap(ntl, 0)),
                   pl.BlockSpec((2, HD), lambda b, i: (0, 0))],
        out_shape=[jax.ShapeDtypeStruct((B, NH, ntl * TM, HD), F32), jax.ShapeDtypeStruct((B * ntl * TM, D), F32),
                   jax.ShapeDtypeStruct((2, HD), F32)],
        compiler_params=_cp(2))(dM, O0, O1, F, gains)


def _sincos_2d(rows, width, dim):
    quarter = dim // 4
    omega = 1.0 / 10000.0 ** (jnp.arange(quarter, dtype=F32) / quarter)

    def emb(n):
        a = jnp.arange(n).astype(F32)[:, None] * omega[None, :]
        return jnp.concatenate([jnp.sin(a), jnp.cos(a)], axis=-1)

    er = jnp.broadcast_to(emb(rows)[:, None, :], (rows, width, dim // 2))
    ec = jnp.broadcast_to(emb(width)[None, :, :], (rows, width, dim // 2))
    return jnp.concatenate([er, ec], axis=-1).reshape(rows * width, dim)


def _pad_heads(w):
    k = w.shape[0]
    return jnp.pad(w.reshape(k, 4, GLA_DK), ((0, 0), (0, 0), (0, HD - GLA_DK))).reshape(k, 4 * HD)


def _unpad_heads(w):
    k = w.shape[0]
    return w.reshape(k, 4, HD)[:, :, :GLA_DK].reshape(k, 4 * GLA_DK)


MIX_N = 1032
MIX_NP = 1152
_SEGS = ([(64 * h, 64, GQ + HD * h) for h in range(4)] + [(256 + 64 * h, 64, GK + HD * h) for h in range(4)]
         + [(512, 512, GV), (1024, 512, GG), (1536, 32, LR), (1568, 512, RQ), (2080, 512, RFF), (2592, 512, RFB),
            (3104, 512, RI), (3616, 512, RG)])


def _mix_in_to_padded(ps):
    k = ps.shape[1]
    parts, pos = [], 0
    for g0, ln, s0 in sorted(_SEGS, key=lambda s: s[2]):
        if s0 > pos:
            parts.append(jnp.zeros((k, s0 - pos), ps.dtype))
        for j in range(4):
            lo, hi = max(g0, j * MIX_N), min(g0 + ln, (j + 1) * MIX_N)
            if lo < hi:
                parts.append(ps[j][:, lo - j * MIX_N:hi - j * MIX_N])
        pos = s0 + ln
    parts.append(jnp.zeros((k, MIXP - pos), ps.dtype))
    return jnp.concatenate(parts, axis=1)


def _mix_in_from_padded(g):
    k = g.shape[0]
    shards = []
    for j in range(4):
        parts = []
        for g0, ln, s0 in sorted(_SEGS):
            lo, hi = max(g0, j * MIX_N), min(g0 + ln, (j + 1) * MIX_N)
            if lo < hi:
                parts.append(g[:, s0 + lo - g0:s0 + hi - g0])
        parts.append(jnp.zeros((k, MIX_NP - MIX_N), g.dtype))
        shards.append(jnp.concatenate(parts, axis=1))
    return jnp.stack(shards)


def _local_step(x, ctx, tgt, mvec, weights_for, ln_gain, ln_bias, a2f, a2b, abf, abb, lb, gng, gnh, on_grads, on_sent):
    B, T, _ = x.shape
    assert ctx.shape[1] == TM and T % TM == 0
    ntl = T // TM
    nt = ntl + 1
    C, L, CL = (nt, 0), (ntl, 0), (nt, 1)
    pos = _sincos_2d(T // 64, 64, D)
    gbs = [jnp.stack([ln_gain[i], ln_bias[i]]) for i in range(3)]
    a2p = jnp.zeros((2, HD, 4 * HD), F32)
    a2p = a2p.at[0, 0:16].set(_pad_heads(a2f)).at[1, 16:32].set(_pad_heads(a2b)).astype(BF16)
    biasp = jnp.concatenate([_pad_heads(abf.reshape(1, -1)), _pad_heads(abb.reshape(1, -1))], axis=0)
    gains = jnp.concatenate([gng.reshape(1, HD), gnh.reshape(1, HD)], axis=0)

    X0, h0 = _embed_lnmod(x, ctx, pos, mvec)
    w1i, w1o = weights_for("ffn1", h0)
    u0, a0 = _ffn_in(h0, C, w1i, B, nt, "ffn1_in")
    wmp, wmo = weights_for("mix", a0)
    y0 = _mm_nn(a0, C, w1o, B, nt, F32, "ffn1_out")
    X1, h1 = _resid_lnmod(X0, C, y0, mvec, True, 0, 0.5, gbs[0], B, nt, "resid0_lnmod1")
    Q, K, V, G, Zm = _mix_in_features(h1, wmp, a2p, biasp, lb, B, nt)
    w2i, w2o = weights_for("ffn2", Zm)
    O0, S0 = _scan_fwd(Q, K, V, G, False, B)
    O1, S1 = _scan_fwd(Q, K, V, G, True, B)
    merged = _gnorm(O0, O1, Zm, gains, B, ntl)
    y1 = _mm_nn(merged, L, wmo, B, ntl, F32, "mix_out")
    X2, h2 = _resid_lnmod(X1, CL, y1, mvec, False, 1, 1.0, gbs[1], B, ntl, "resid1_lnmod2")
    u2, a2 = _ffn_in(h2, L, w2i, B, ntl, "ffn2_in")
    y2 = _mm_nn(a2, L, w2o, B, ntl, F32, "ffn2_out")

    dx2r, dy2, dgate2, dgb2, lsum = _tail(X2, y2, mvec, gbs[2], tgt, 2, 0.5, B, ntl)
    loss = (0.5 / D) * jnp.sum(lsum)
    du2 = _ffn_out_dx(dy2, w2o, u2, "ffn2_out_dx")
    g_w2o = _mm_tn(a2, dy2, "ffn2_out_dw")
    dh2 = _mm_nt(du2, w2i, "ffn2_in_dx")
    g_w2i = _mm_tn(h2, du2, "ffn2_in_dw", shards=4)
    tok = on_grads("ffn2", (g_w2i, g_w2o))
    dx1r, dy1, dgate1, dgb1, dss2 = _lnmod_resid_bwd(dh2, X2, L, dx2r, X1, CL, y1, mvec, False, 2, 1.0,
                                                     gbs[1] + tok[0, 0], B, ntl, "lnmod2_resid1_bwd")
    tok = on_sent("ffn2", dy1)
    dmerged = _mm_nt(dy1, wmo, "mix_out_dx", dep=tok)
    g_wmo = _mm_tn(merged, dy1, "mix_out_dw", tn=D)
    dO, dgates, dgains = _gnorm_bwd(dmerged, O0, O1, Zm, gains, B, ntl)
    dQ0, dK0, dV0, dG0 = _scan_bwd(Q, K, V, G, S0, dO, False, B)
    dQ1, dK1, dV1, dG1 = _scan_bwd(Q, K, V, G, S1, dO, True, B)
    dF, da2p, dbiasp, dlb = _features_bwd(Zm, a2p, biasp, lb, dQ0, dQ1, dK0, dK1, dV0, dV1, dG0, dG1, dgates, B)
    dh1 = _mm_nt(dF, wmp, "mix_in_dx")
    g_wmp = _mm_tn(h1, dF, "mix_in_dw", tn=MIXP // 4)
    tok = on_grads("mix", (g_wmp, g_wmo))
    dx0r, dy0, dgate0, dgb0, dss1 = _lnmod_resid_bwd(dh1, X1, C, dx1r, X0, C, y0, mvec, True, 1, 0.5,
                                                     gbs[0] + tok[0, 0], B, nt, "lnmod1_resid0_bwd")
    tok = on_sent("mix", dy0)
    du0 = _ffn_out_dx(dy0, w1o, u0, "ffn1_out_dx", dep=tok)
    g_w1o = _mm_tn(a0, dy0, "ffn1_out_dw")
    g_w1i = _mm_tn(h0, du0, "ffn1_in_dw", shards=4)
    tok = on_grads("ffn1", (g_w1i, g_w1o))
    dh0 = _mm_nt(du0, w1i, "ffn1_in_dx", dep=tok)
    tok = on_sent("ffn1", dh0)
    grad_x, dss0 = _lnmod0_bwd(dh0, X0, mvec + tok[0, 0], dx0r, x.shape, B, nt)

    zero_ctx = lambda a: a.at[:, 0].set(0.0)
    dm = jnp.concatenate([dss0, dgate0, dss1, zero_ctx(dgate1), zero_ctx(dss2), zero_ctx(dgate2)], axis=2)
    small = dict(
        ln_gain=jnp.stack([dgb0[0], dgb1[0], dgb2[0]]), ln_bias=jnp.stack([dgb0[1], dgb1[1], dgb2[1]]),
        a2f=_unpad_heads(da2p[0, 0:16]), a2b=_unpad_heads(da2p[1, 16:32]),
        abf=_unpad_heads(dbiasp[0:1]), abb=_unpad_heads(dbiasp[1:2]), lb=dlb, gng=dgains[0], gnh=dgains[1])
    return loss, grad_x, dm, small


def _small_allgather(xs, name):
    r, n = xs.shape

    def body(x_ref, out_ref, send_sems, recv_sems, local_sem):
        x, y, c = lax.axis_index("x"), lax.axis_index("y"), lax.axis_index("c")
        me, sibling = (x, y, c), (x, y, 1 - c)
        chips = [(1 - x, y), (x, 1 - y), (1 - x, 1 - y)]

        def rows(px, py, pc):
            return out_ref.at[pl.ds((4 * px + 2 * py + pc) * r, r), :]

        def copy(k, block, to, src=None):
            return pltpu.make_async_remote_copy(
                src_ref=rows(*block) if src is None else src, dst_ref=rows(*block),
                send_sem=send_sems.at[k], recv_sem=recv_sems.at[k], device_id=to, device_id_type=MESH)

        mine = pltpu.make_async_copy(x_ref, rows(*me), local_sem)
        mine.start()
        first = [copy(0, me, sibling, src=x_ref)]
        first += [copy(1 + j, me, (*chip, c), src=x_ref) for j, chip in enumerate(chips)]
        for cp in first:
            cp.start()
        passed = [copy(4 + j, (*chip, c), sibling) for j, chip in enumerate(chips)]
        for j, chip in enumerate(chips):
            copy(1 + j, (*chip, c), me).wait_recv()
            passed[j].start()
        copy(0, sibling, me).wait_recv()
        for j, chip in enumerate(chips):
            copy(4 + j, (*chip, 1 - c), me).wait_recv()
        for cp in first + passed:
            cp.wait_send()
        mine.wait()

    out = pl.pallas_call(
        body, name=name,
        out_shape=jax.ShapeDtypeStruct((8 * r, n), xs.dtype),
        in_specs=[pl.BlockSpec(memory_space=pltpu.VMEM)],
        out_specs=pl.BlockSpec(memory_space=pltpu.VMEM),
        scratch_shapes=[pltpu.SemaphoreType.DMA((7,)), pltpu.SemaphoreType.DMA((7,)), pltpu.SemaphoreType.DMA],
        compiler_params=pltpu.CompilerParams(vmem_limit_bytes=VMEM_LIMIT))(xs)
    return out.reshape(8, r, n)


def _gather_flat(v, name):
    n = v.shape[0]
    npad = -(-n // 1024) * 1024
    g = _small_allgather(jnp.pad(v, (0, npad - n)).reshape(8, npad // 8), name)
    return g.reshape(8, npad)[:, :n]


HBM_SPEC = pl.BlockSpec(memory_space=pltpu.HBM)
SEM_SPEC = pl.BlockSpec(memory_space=pltpu.SEMAPHORE)
DATAFLOW = pltpu.SideEffectType.DATAFLOW_SIDE_EFFECTING


def _gather_copies(xs, outs, send_sems, recv_sems):
    x, y, c = lax.axis_index("x"), lax.axis_index("y"), lax.axis_index("c")
    dests = [(x, y, 1 - c), (1 - x, y, c), (x, 1 - y, c), (1 - x, 1 - y, c)]
    return [pltpu.make_async_remote_copy(
        src_ref=xs[w], dst_ref=outs[w].at[4 * x + 2 * y + c], send_sem=send_sems[4 * w + k],
        recv_sem=recv_sems[4 * w + k], device_id=dests[k], device_id_type=MESH)
        for w in range(len(xs)) for k in range(4)]


def _split_start(copies, per_w, srcs, land_lead, after, name):
    n = len(srcs)
    m = per_w * n
    lands = [lax.empty((land_lead,) + s.shape[-2:], s.dtype) for s in srcs]
    deps = [] if after is None else [after]

    def body(*refs):
        xs, ls, outs = refs[:n], refs[n:2 * n], refs[2 * n + len(deps):]
        for cp in copies(xs, ls, outs[:m], outs[m:2 * m]):
            cp.start()
        token = outs[2 * m + 2 * n]
        token[...] = jnp.zeros_like(token)

    outs = pl.pallas_call(
        body, name=name,
        out_shape=([pltpu.SemaphoreType.DMA(())] * (2 * m) + [pltpu.HBM(a.shape, a.dtype) for a in srcs + lands]
                   + [jax.ShapeDtypeStruct((8, 128), F32)]),
        in_specs=[HBM_SPEC] * (2 * n) + [pl.BlockSpec(memory_space=pl.ANY)] * len(deps),
        out_specs=[SEM_SPEC] * (2 * m) + [HBM_SPEC] * (2 * n) + [pl.BlockSpec(memory_space=pltpu.VMEM)],
        input_output_aliases={w: 2 * m + w for w in range(2 * n)},
        compiler_params=pltpu.CompilerParams(has_side_effects=DATAFLOW),
    )(*[pltpu.with_memory_space_constraint(a, pltpu.HBM) for a in srcs + lands], *deps)
    return outs[:2 * m], outs[2 * m:2 * m + n], outs[2 * m + n:2 * m + 2 * n], outs[2 * m + 2 * n]


def _split_wait(copies, per_w, sems, x_thru, l_thru, after, name):
    n = len(x_thru)
    m = per_w * n

    def body(*refs):
        xs, ls, ss = refs[:n], refs[n:2 * n], refs[2 * n:2 * n + 2 * m]
        for cp in copies(xs, ls, ss[:m], ss[m:]):
            cp.wait_send()
            cp.wait_recv()

    outs = pl.pallas_call(
        body, name=name,
        out_shape=[pltpu.HBM(a.shape, a.dtype) for a in list(x_thru) + list(l_thru)],
        in_specs=[HBM_SPEC] * (2 * n) + [SEM_SPEC] * (2 * m) + [pl.BlockSpec(memory_space=pl.ANY)],
        out_specs=[HBM_SPEC] * (2 * n),
        input_output_aliases={w: w for w in range(2 * n)},
        compiler_params=pltpu.CompilerParams(has_side_effects=DATAFLOW),
    )(*x_thru, *l_thru, *sems, after)
    return outs[:n], outs[n:]


def _gather_forward(gathered, name):
    n = len(gathered)

    def body(*refs):
        outs = refs[n:2 * n]
        send_sems, recv_sems = refs[2 * n:]
        x, y, c = lax.axis_index("x"), lax.axis_index("y"), lax.axis_index("c")
        chips = [(1 - x, y), (x, 1 - y), (1 - x, 1 - y), (x, y)]

        def copy(w, j, pc):
            px, py = chips[j]
            slot = outs[w].at[4 * px + 2 * py + pc]
            return pltpu.make_async_remote_copy(
                src_ref=slot, dst_ref=slot, send_sem=send_sems.at[4 * w + j], recv_sem=recv_sems.at[4 * w + j],
                device_id=(x, y, 1 - c), device_id_type=MESH)

        sends = [copy(w, j, 1 - c if j == 3 else c) for w in range(n) for j in range(4)]
        for cp in sends:
            cp.start()
        for w in range(n):
            for j in range(4):
                copy(w, j, c if j == 3 else 1 - c).wait_recv()
        for cp in sends:
            cp.wait_send()

    any_spec = pl.BlockSpec(memory_space=pl.ANY)
    return pl.pallas_call(
        body, name=name,
        out_shape=[jax.ShapeDtypeStruct(g.shape, g.dtype) for g in gathered],
        in_specs=[any_spec] * n, out_specs=[any_spec] * n,
        input_output_aliases={w: w for w in range(n)},
        scratch_shapes=[pltpu.SemaphoreType.DMA((4 * n,)), pltpu.SemaphoreType.DMA((4 * n,))],
    )(*gathered)


def _pair_copies(gs, ls, send_sems, recv_sems):
    x, y, c = lax.axis_index("x"), lax.axis_index("y"), lax.axis_index("c")
    return [pltpu.make_async_remote_copy(
        src_ref=gs[w].at[2 * j + 1 - c], dst_ref=ls[w].at[j], send_sem=send_sems[4 * w + j],
        recv_sem=recv_sems[4 * w + j], device_id=(x, y, 1 - c), device_id_type=MESH)
        for w in range(len(gs)) for j in range(4)]


def _chip_copies(hs, ls, send_sems, recv_sems):
    x, y, c = lax.axis_index("x"), lax.axis_index("y"), lax.axis_index("c")
    chips = [(1 - x, y), (x, 1 - y), (1 - x, 1 - y)]
    return [pltpu.make_async_remote_copy(
        src_ref=hs[w].at[2 * px + py], dst_ref=ls[w].at[k], send_sem=send_sems[3 * w + k],
        recv_sem=recv_sems[3 * w + k], device_id=(px, py, c), device_id_type=MESH)
        for w in range(len(hs)) for k, (px, py) in enumerate(chips)]


def _rs_pair_share(fins, name):
    n = len(fins)

    def body(*refs):
        outs = refs[n:2 * n]
        send_sems, recv_sems = refs[2 * n:]
        x, y, c = lax.axis_index("x"), lax.axis_index("y"), lax.axis_index("c")

        def copy(w, slot):
            return pltpu.make_async_remote_copy(
                src_ref=outs[w].at[slot], dst_ref=outs[w].at[slot], send_sem=send_sems.at[w], recv_sem=recv_sems.at[w],
                device_id=(x, y, 1 - c), device_id_type=MESH)

        sends = [copy(w, c) for w in range(n)]
        for cp in sends:
            cp.start()
        for w in range(n):
            copy(w, 1 - c).wait_recv()
        for cp in sends:
            cp.wait_send()

    any_spec = pl.BlockSpec(memory_space=pl.ANY)
    return pl.pallas_call(
        body, name=name,
        out_shape=[jax.ShapeDtypeStruct(f.shape, f.dtype) for f in fins],
        in_specs=[any_spec] * n, out_specs=[any_spec] * n,
        input_output_aliases={w: w for w in range(n)},
        scratch_shapes=[pltpu.SemaphoreType.DMA((n,)), pltpu.SemaphoreType.DMA((n,))])(*fins)


def _rs_add_pair(g8, r4, c, name):
    R, n = g8.shape[1:]
    rb = R // 2

    def body(c_ref, g_ref, r_ref, o_ref):
        o_ref[...] = (g_ref[...] + r_ref[...]).astype(BF16)

    spec = pl.BlockSpec((1, rb, n), lambda j, i, c_ref: (j, i, 0))
    return pl.pallas_call(
        body, name=name,
        grid_spec=pltpu.PrefetchScalarGridSpec(
            num_scalar_prefetch=1, grid=(4, R // rb),
            in_specs=[pl.BlockSpec((1, rb, n), lambda j, i, c_ref: (2 * j + c_ref[0], i, 0)), spec],
            out_specs=spec),
        out_shape=jax.ShapeDtypeStruct((4, R, n), BF16), compiler_params=_cp(2))(c, g8, r4)


def _rs_add_chips(g8, r4, r3, cj, name):
    R, n = g8.shape[1:]
    rb = R // 2

    def body(cj_ref, g_ref, p_ref, r_ref, o_ref):
        own = g_ref[0] + p_ref[0]
        o_ref[0] = ((own + r_ref[0].astype(F32)) + r_ref[1].astype(F32)) + r_ref[2].astype(F32)

    return pl.pallas_call(
        body, name=name,
        grid_spec=pltpu.PrefetchScalarGridSpec(
            num_scalar_prefetch=1, grid=(R // rb,),
            in_specs=[pl.BlockSpec((1, rb, n), lambda i, cj_ref: (2 * cj_ref[1] + cj_ref[0], i, 0)),
                      pl.BlockSpec((1, rb, n), lambda i, cj_ref: (cj_ref[1], i, 0)),
                      pl.BlockSpec((3, rb, n), lambda i, cj_ref: (0, i, 0))],
            out_specs=pl.BlockSpec((1, rb, n), lambda i, cj_ref: (cj_ref[0], i, 0))),
        out_shape=jax.ShapeDtypeStruct((2, R, n), F32), compiler_params=_cp(1))(cj, g8, r4, r3)


def _sum8(g):
    n = g.shape[1]

    def body(g_ref, o_ref):
        acc = g_ref[0:1, :]
        for k in range(1, 8):
            acc = acc + g_ref[k:k + 1, :]
        o_ref[...] = acc

    return pl.pallas_call(body, name="sum_devices", out_shape=jax.ShapeDtypeStruct((1, n), F32),
                          compiler_params=pltpu.CompilerParams(vmem_limit_bytes=VMEM_LIMIT))(g)


ADA_ROWS = 64


def _ada_fwd(cs, w, b):
    n = w.shape[1]

    def body(c_ref, w_ref, b_ref, o_ref):
        cv = c_ref[...]
        s = (cv * _sigmoid(cv)).astype(BF16)
        o_ref[...] = _dot(s, w_ref[...].astype(BF16)) + b_ref[...]

    return pl.pallas_call(body, name="ada_fwd", out_shape=jax.ShapeDtypeStruct((ADA_ROWS, n), F32),
                          compiler_params=pltpu.CompilerParams(vmem_limit_bytes=VMEM_LIMIT))(cs, w, b)


def _ada_bwd(cs, w, dm):
    n = w.shape[1]

    def body(c_ref, w_ref, dm_ref, gw_ref, dc_ref):
        cv = c_ref[...]
        s = (cv * _sigmoid(cv)).astype(BF16)
        gw_ref[...] = _dot_tn(s, dm_ref[...].astype(BF16))
        dc_ref[...] = _dot_nt(dm_ref[32:40, :].astype(BF16), w_ref[...].astype(BF16))

    return pl.pallas_call(
        body, name="ada_bwd",
        out_shape=[jax.ShapeDtypeStruct((D, n), F32), jax.ShapeDtypeStruct((8, D), F32)],
        compiler_params=pltpu.CompilerParams(vmem_limit_bytes=VMEM_LIMIT))(cs, w, dm)


def _adamw(w, g, m, v, name):
    r, c = w.shape
    rb = r
    if r % 8 == 0 and r * c * 4 > (1 << 20):
        rb = 8
        for cand in range(8, r, 8):
            if r % cand == 0 and cand * c * 4 <= (1 << 20):
                rb = cand

    def body(w_ref, g_ref, m_ref, v_ref, go_ref, d_ref, nm_ref, nv_ref):
        gv = g_ref[...]
        go_ref[...] = gv
        mn = ADAM_B1 * m_ref[...] + (1.0 - ADAM_B1) * gv
        vn = ADAM_B2 * v_ref[...] + (1.0 - ADAM_B2) * (gv * gv)
        m_hat = mn / (1.0 - ADAM_B1 ** ADAM_STEP)
        v_hat = vn / (1.0 - ADAM_B2 ** ADAM_STEP)
        d_ref[...] = -ADAM_LR * (m_hat / (jnp.sqrt(v_hat) + ADAM_EPS) + ADAM_WD * w_ref[...])
        nm_ref[...] = mn
        nv_ref[...] = vn

    spec = pl.BlockSpec((rb, c), lambda i: (i, 0))
    shp = jax.ShapeDtypeStruct((r, c), F32)
    return pl.pallas_call(body, name=name, grid=(r // rb,), in_specs=[spec] * 4, out_specs=[spec] * 4,
                          out_shape=[shp] * 4, compiler_params=_cp(1))(w, g, m, v)


BIG = ("ffn1_w_in", "ffn1_w_out", "w_mix_in", "w_mix_out", "ffn2_w_in", "ffn2_w_out")


def _half_rows(w, c):
    half = w.shape[0] // 2
    return lax.dynamic_slice_in_dim(w, c * half, half, axis=0)


def _lower_bounds(logits):
    return jnp.cumsum(jax.nn.softmax(logits.astype(F32), axis=1), axis=1)[:, 0]


def kernel(x, c, ctx, c_ctx, w_ada, b_ada, ln_gain, ln_bias, ffn1_w_in, ffn1_w_out, w_mix_in, gla_a2_fwd, gla_a2_bwd, gla_a_bias_fwd, gla_a_bias_bwd, hgrn_lb_logits, gla_norm_gain, hgrn_norm_gain, w_mix_out, ffn2_w_in, ffn2_w_out, loss_target, m_c_ctx, m_w_ada, m_b_ada, m_ln_gain, m_ln_bias, m_ffn1_w_in, m_ffn1_w_out, m_w_mix_in, m_gla_a2_fwd, m_gla_a2_bwd, m_gla_a_bias_fwd, m_gla_a_bias_bwd, m_hgrn_lb_logits, m_gla_norm_gain, m_hgrn_norm_gain, m_w_mix_out, m_ffn2_w_in, m_ffn2_w_out, v_c_ctx, v_w_ada, v_b_ada, v_ln_gain, v_ln_bias, v_ffn1_w_in, v_ffn1_w_out, v_w_mix_in, v_gla_a2_fwd, v_gla_a2_bwd, v_gla_a_bias_fwd, v_gla_a_bias_bwd, v_hgrn_lb_logits, v_gla_norm_gain, v_hgrn_norm_gain, v_w_mix_out, v_ffn2_w_in, v_ffn2_w_out):
    xi, yi, ci = lax.axis_index("x"), lax.axis_index("y"), lax.axis_index("c")
    chip = 2 * xi + yi
    dev = 2 * chip + ci
    B = x.shape[0]
    weights = dict(ffn1_w_in=ffn1_w_in[0], ffn1_w_out=ffn1_w_out[0], w_mix_in=w_mix_in[0], w_mix_out=w_mix_out[0],
                   ffn2_w_in=ffn2_w_in[0], ffn2_w_out=ffn2_w_out[0])

    mine = jnp.concatenate([c.reshape(-1), ln_gain.reshape(-1), ln_bias.reshape(-1), gla_a2_fwd.reshape(-1),
                            gla_a2_bwd.reshape(-1), hgrn_lb_logits.reshape(-1)])
    g1 = _gather_flat(mine, "gather_cond")
    nc = B * D
    c_all = g1[:, :nc].reshape(8 * B, D)
    per_chip = g1[0::2, nc:]
    o = 0

    def take(shape, axis):
        nonlocal o
        n = int(np.prod(shape))
        parts = per_chip[:, o:o + n].reshape((4,) + shape)
        o += n
        return jnp.concatenate([parts[j] for j in range(4)], axis=axis)

    ln_gain_f = take((3, 256), 1)
    ln_bias_f = take((3, 256), 1)
    a2f_f = take((16, 64), 1)
    a2b_f = take((16, 64), 1)
    lbl_f = take((2, 2, 128), 2)
    lb, lb_vjp = jax.vjp(_lower_bounds, lbl_f)

    cs = jnp.concatenate([c_all, c_ctx.reshape(1, D), jnp.zeros((ADA_ROWS - 8 * B - 1, D), F32)], axis=0)
    ncol = w_ada.shape[2]
    b_cols = lax.dynamic_slice_in_dim(b_ada, chip * ncol, ncol, axis=1)
    m_cols = _ada_fwd(cs, w_ada[0], b_cols)
    g2 = _small_allgather(m_cols, "gather_mod")[0::2]
    m_all = jnp.concatenate([g2[j] for j in range(4)], axis=1)
    m_lat = lax.dynamic_slice_in_dim(m_all, dev * B, B, axis=0).reshape(B, 1, N_MOD, D)
    m_ctx = jnp.broadcast_to(m_all[8 * B].reshape(1, 1, N_MOD, D), (B, 1, N_MOD, D))

    groups = dict(ffn2=("ffn2_w_in", "ffn2_w_out"), mix=("w_mix_in", "w_mix_out"), ffn1=("ffn1_w_in", "ffn1_w_out"))
    shards = dict(weights, w_mix_in=jnp.pad(weights["w_mix_in"], ((0, 0), (0, MIX_NP - MIX_N))))
    blks = {k: _half_rows(shards[k], ci).astype(BF16) for k in BIG}
    gathering = {}
    token = m_all
    for group in ("ffn1", "mix", "ffn2"):
        sems, x_thru, l_thru, token = _split_start(_gather_copies, 4, [blks[k] for k in groups[group]], 8, token,
                                                   "weight_gather_start_" + group)
        gathering[group] = (sems, x_thru, l_thru)
    mvec = jnp.concatenate([m_ctx, m_lat], axis=1) + token[0, 0]

    def weights_for(group, after):
        names = groups[group]
        _, got = _split_wait(_gather_copies, 4, *gathering[group], after, "weight_gather_wait_" + group)
        w_in, w_out = _gather_forward(got, "weight_gather_forward_" + group)
        if group == "mix":
            return _mix_in_to_padded(w_in.reshape(4, D, MIX_NP)), w_out.reshape(-1, D)
        return w_in.reshape((4,) + shards[names[0]].shape), w_out.reshape(-1, D)

    cvec = ci.reshape(1).astype(jnp.int32)
    cjvec = jnp.stack([ci, chip]).astype(jnp.int32)
    in_flight = {}

    def on_grads(group, gs):
        names = groups[group]
        if group == "mix":
            gs = (_mix_in_from_padded(gs[0]), gs[1])
        g8s = [g.reshape((8, shards[k].shape[0] // 2, shards[k].shape[1])) for k, g in zip(names, gs)]
        sems, g_thru, l_thru, token = _split_start(_pair_copies, 4, g8s, 4, None, "grad_pair_start_" + group)
        in_flight[group] = (sems, g_thru, l_thru)
        return token

    def on_sent(group, after):
        g8s, r4s = _split_wait(_pair_copies, 4, *in_flight[group], after, "grad_pair_wait_" + group)
        h4s = [_rs_add_pair(g, r, cvec, "grad_pair_add_" + k) for k, g, r in zip(groups[group], g8s, r4s)]
        sems, h_thru, l_thru, token = _split_start(_chip_copies, 3, h4s, 3, None, "grad_chip_start_" + group)
        in_flight[group] = (g8s, r4s, sems, h_thru, l_thru)
        return token

    loss_l, grad_x, dm, small = _local_step(
        x, ctx, loss_target, mvec, weights_for, ln_gain_f, ln_bias_f, a2f_f, a2b_f,
        gla_a_bias_fwd, gla_a_bias_bwd, lb, gla_norm_gain, hgrn_norm_gain, on_grads, on_sent)
    loss = lax.psum(loss_l, ("x", "y", "c"))

    dm_lat = dm[:, 1].reshape(B, N_MOD * D)
    dm_ctx = jnp.sum(dm[:, 0], axis=0).reshape(N_MOD * D)
    keys = ("ln_gain", "ln_bias", "a2f", "a2b", "abf", "abb", "lb", "gng", "gnh")
    flat = jnp.concatenate([dm_lat.reshape(-1), dm_ctx] + [small[k].reshape(-1) for k in keys])
    g3 = _gather_flat(flat, "gather_small_grads")
    nlat = B * N_MOD * D
    dm_all = g3[:, :nlat].reshape(8 * B, N_MOD * D)
    tot = _sum8(g3[:, nlat:])[0]
    dmc_tot = tot[:N_MOD * D]
    o = N_MOD * D
    sg = {}
    for k in keys:
        n = int(np.prod(small[k].shape))
        sg[k] = tot[o:o + n].reshape(small[k].shape)
        o += n
    dm_rows = jnp.concatenate([dm_all, dmc_tot.reshape(1, -1), jnp.zeros((ADA_ROWS - 8 * B - 1, N_MOD * D), F32)], axis=0)
    g_b_ada = (jnp.sum(dm_all, axis=0) + dmc_tot).reshape(1, N_MOD * D)
    g_w_ada, dcc = _ada_bwd(cs, w_ada[0], lax.dynamic_slice_in_dim(dm_rows, chip * ncol, ncol, axis=1))
    g4 = _gather_flat(dcc[0], "gather_cctx")
    dsilu = ((g4[0] + g4[2]) + g4[4]) + g4[6]
    sc = _sigmoid(c_ctx)
    g_c_ctx = dsilu * (sc * (1.0 + c_ctx * (1.0 - sc)))
    (g_lbl,) = lb_vjp(sg["lb"])

    def cols(a, n, axis):
        return lax.dynamic_slice_in_dim(a, chip * n, n, axis=axis)

    fin = {}
    for group in ("ffn2", "mix", "ffn1"):
        g8s, r4s, sems, h_thru, l_thru = in_flight[group]
        _, r3s = _split_wait(_chip_copies, 3, sems, h_thru, l_thru, g_w_ada, "grad_chip_wait_" + group)
        for k, g, r4, r3 in zip(groups[group], g8s, r4s, r3s):
            fin[k] = _rs_add_chips(g, r4, r3, cjvec, "grad_chip_add_" + k)
    fins = [fin[k] for k in BIG]
    gsh = {k: both.reshape(shards[k].shape)[:, :weights[k].shape[1]]
           for k, both in zip(BIG, _rs_pair_share(fins, "grad_pair_share"))}

    grads = dict(
        c_ctx=g_c_ctx, w_ada=g_w_ada[None], b_ada=g_b_ada, ln_gain=cols(sg["ln_gain"], 256, 1)[None],
        ln_bias=cols(sg["ln_bias"], 256, 1)[None], ffn1_w_in=gsh["ffn1_w_in"][None], ffn1_w_out=gsh["ffn1_w_out"][None],
        w_mix_in=gsh["w_mix_in"][None], gla_a2_fwd=cols(sg["a2f"], 64, 1)[None], gla_a2_bwd=cols(sg["a2b"], 64, 1)[None],
        gla_a_bias_fwd=sg["abf"], gla_a_bias_bwd=sg["abb"], hgrn_lb_logits=cols(g_lbl, 128, 2),
        gla_norm_gain=sg["gng"].reshape(1, HD), hgrn_norm_gain=sg["gnh"].reshape(1, HD),
        w_mix_out=gsh["w_mix_out"][None], ffn2_w_in=gsh["ffn2_w_in"][None], ffn2_w_out=gsh["ffn2_w_out"][None])
    params = dict(
        c_ctx=(c_ctx, m_c_ctx, v_c_ctx), w_ada=(w_ada, m_w_ada, v_w_ada), b_ada=(b_ada, m_b_ada, v_b_ada),
        ln_gain=(ln_gain, m_ln_gain, v_ln_gain), ln_bias=(ln_bias, m_ln_bias, v_ln_bias),
        ffn1_w_in=(ffn1_w_in, m_ffn1_w_in, v_ffn1_w_in), ffn1_w_out=(ffn1_w_out, m_ffn1_w_out, v_ffn1_w_out),
        w_mix_in=(w_mix_in, m_w_mix_in, v_w_mix_in), gla_a2_fwd=(gla_a2_fwd, m_gla_a2_fwd, v_gla_a2_fwd),
        gla_a2_bwd=(gla_a2_bwd, m_gla_a2_bwd, v_gla_a2_bwd),
        gla_a_bias_fwd=(gla_a_bias_fwd, m_gla_a_bias_fwd, v_gla_a_bias_fwd),
        gla_a_bias_bwd=(gla_a_bias_bwd, m_gla_a_bias_bwd, v_gla_a_bias_bwd),
        hgrn_lb_logits=(hgrn_lb_logits, m_hgrn_lb_logits, v_hgrn_lb_logits),
        gla_norm_gain=(gla_norm_gain, m_gla_norm_gain, v_gla_norm_gain),
        hgrn_norm_gain=(hgrn_norm_gain, m_hgrn_norm_gain, v_hgrn_norm_gain),
        w_mix_out=(w_mix_out, m_w_mix_out, v_w_mix_out), ffn2_w_in=(ffn2_w_in, m_ffn2_w_in, v_ffn2_w_in),
        ffn2_w_out=(ffn2_w_out, m_ffn2_w_out, v_ffn2_w_out))
    order = list(params.keys())
    big_names = ("w_ada",) + BIG
    upd = {}
    for k in big_names:
        w_, m_, v_ = params[k]
        s2 = w_.shape[-2:]
        g_, d_, nm_, nv_ = _adamw(w_.reshape(s2), grads[k].reshape(s2), m_.reshape(s2), v_.reshape(s2), "adamw_" + k)
        grads[k] = g_
        upd[k] = (d_.reshape(w_.shape), nm_.reshape(w_.shape), nv_.reshape(w_.shape))
    small_names = [k for k in order if k not in big_names]
    sizes = [int(np.prod(params[k][0].shape)) for k in small_names]
    tot_n = sum(sizes)
    npad = -(-tot_n // 1024) * 1024

    def packed(get):
        flat_ = jnp.concatenate([get(k).reshape(-1) for k in small_names])
        return jnp.pad(flat_, (0, npad - tot_n)).reshape(8, npad // 8)

    _, d_s, nm_s, nv_s = _adamw(packed(lambda k: params[k][0]), packed(lambda k: grads[k]),
                                packed(lambda k: params[k][1]), packed(lambda k: params[k][2]), "adamw_small")
    o = 0
    for k, n in zip(small_names, sizes):
        shp = params[k][0].shape
        upd[k] = tuple(a.reshape(-1)[o:o + n].reshape(shp) for a in (d_s, nm_s, nv_s))
        o += n

    return (loss, grad_x, *[grads[k].reshape(params[k][0].shape) for k in order], *[upd[k][0] for k in order],
            *[upd[k][1] for k in order], *[upd[k][2] for k in order])
```

```python
import functools

import numpy as np
import jax
import jax.numpy as jnp
from jax import lax
from jax.experimental import pallas as pl
from jax.experimental.pallas import tpu as pltpu

F32 = jnp.float32
BF16 = jnp.bfloat16
MESH = pl.DeviceIdType.MESH

D = 1024
DFF = 2816
TM = 256
CH = 32
NCB = TM // CH
SB = 128
CSB = SB // CH
NSB = TM // SB
HP = 8
HPB = 8
HD = 128
NH = 8
LN_EPS = 1e-5
NORM_EPS = 1e-6
ALPHA = 2.0 ** 0.25
GATE_NORM = 16.0
GLA_DK = 64
N_MOD = 9
VMEM_LIMIT = 52 * 1024 * 1024

MIXP = 5120
GG, RG, GQ, GK, GV, RQ, RFF, RFB, RI, LR = 0, 512, 1024, 1536, 2048, 2560, 3072, 3584, 4096, 4608
IN_SPLITS = (256, 256, 512, 512, 16, 16, 512, 512, 512, 512, 512)

ADAM_LR, ADAM_B1, ADAM_B2, ADAM_EPS, ADAM_WD, ADAM_STEP = 0.001, 0.9, 0.999, 1e-08, 0.01, 10


def _cp(n_axes):
    return pltpu.CompilerParams(dimension_semantics=("arbitrary",) * n_axes, vmem_limit_bytes=VMEM_LIMIT)


def _rowmap(stride, off):
    return lambda b, i: (b * stride + off + i, 0)


def _mmap(comb):
    if comb:
        return lambda b, i: (b, jnp.minimum(i, 1), 0, 0)
    return lambda b, i: (b, 1, 0, 0)


def _ln(x):
    mu = jnp.mean(x, axis=-1, keepdims=True)
    xc = x - mu
    var = jnp.mean(xc * xc, axis=-1, keepdims=True)
    r = lax.rsqrt(var + LN_EPS)
    return xc * r, r


def _ln_bwd(dxh, xh, r):
    return r * (dxh - jnp.mean(dxh, axis=-1, keepdims=True) - xh * jnp.mean(dxh * xh, axis=-1, keepdims=True))


def _sigmoid(x):
    return 1.0 / (1.0 + jnp.exp(-x))


def _rsum(x):
    return jnp.sum(x, axis=0, keepdims=True)


def _dot(a, b):
    return jnp.dot(a, b, preferred_element_type=F32)


def _dot_nt(a, b):
    return lax.dot_general(a, b, (((1,), (1,)), ((), ())), preferred_element_type=F32)


def _dot_tn(a, b):
    return lax.dot_general(a, b, (((0,), (0,)), ((), ())), preferred_element_type=F32)


def _modulate(xv, m_ref, sub):
    xh, _ = _ln(xv)
    sh = m_ref[0, 0, 3 * sub:3 * sub + 1, :]
    sc = m_ref[0, 0, 3 * sub + 1:3 * sub + 2, :]
    return (xh * (1.0 + sh) + sc).astype(BF16)


def _embed_lnmod(x, ctx, pos, mvec):
    B, T, _ = x.shape
    nt = 1 + T // TM

    def body(x_ref, c_ref, p_ref, m_ref, o_ref, h_ref):
        i = pl.program_id(1)

        @pl.when(i == 0)
        def _():
            o_ref[...] = c_ref[0]

        @pl.when(i > 0)
        def _():
            o_ref[...] = x_ref[0] + p_ref[...]

        h_ref[...] = _modulate(o_ref[...], m_ref, 0)

    rows = pl.BlockSpec((TM, D), lambda b, i: (b * nt + i, 0))
    return pl.pallas_call(
        body, name="embed_lnmod0", grid=(B, nt),
        in_specs=[pl.BlockSpec((1, TM, D), lambda b, i: (b, jnp.maximum(i - 1, 0), 0)),
                  pl.BlockSpec((1, TM, D), lambda b, i: (b, 0, 0)),
                  pl.BlockSpec((TM, D), lambda b, i: (jnp.maximum(i - 1, 0), 0)),
                  pl.BlockSpec((1, 1, N_MOD, D), _mmap(True))],
        out_specs=[rows, rows],
        out_shape=[jax.ShapeDtypeStruct((B * nt * TM, D), F32), jax.ShapeDtypeStruct((B * nt * TM, D), BF16)],
        compiler_params=_cp(2))(x, ctx, pos, mvec)


def _lnmod0_bwd(dh, X, mvec, dres, x_shape, B, nt):
    def body(dh_ref, x_ref, m_ref, dr_ref, dx_ref, dm_ref):
        i = pl.program_id(1)
        xh, r = _ln(x_ref[...])
        sh = m_ref[0, 0, 0:1, :]
        dhv = dh_ref[...]

        @pl.when((i == 0) | (i == 1))
        def _():
            dm_ref[...] = jnp.zeros_like(dm_ref)

        dm_ref[0, 0, 0:1, :] += _rsum(dhv * xh)
        dm_ref[0, 0, 1:2, :] += _rsum(dhv)

        @pl.when(i > 0)
        def _():
            dx_ref[0] = _ln_bwd(dhv * (1.0 + sh), xh, r) + dr_ref[...]

    rows = pl.BlockSpec((TM, D), _rowmap(nt, 0))
    return pl.pallas_call(
        body, name="lnmod0_bwd", grid=(B, nt),
        in_specs=[rows, rows, pl.BlockSpec((1, 1, N_MOD, D), _mmap(True)), rows],
        out_specs=[pl.BlockSpec((1, TM, D), lambda b, i: (b, jnp.maximum(i - 1, 0), 0)),
                   pl.BlockSpec((1, 1, 2, D), _mmap(True))],
        out_shape=[jax.ShapeDtypeStruct(x_shape, F32), jax.ShapeDtypeStruct((B, 2, 2, D), F32)],
        compiler_params=_cp(2))(dh, X, mvec, dres)


def _resid_fwd(x_ref, y_ref, m_ref, gb_ref, sub, w):
    wg = w * m_ref[0, 0, 3 * sub + 2:3 * sub + 3, :]
    y = y_ref[...]
    zh, r = _ln(ALPHA * x_ref[...] + wg * y)
    return y, wg, zh, r


def _resid_grads(do, y, wg, zh, r, w, gb_ref, comb, dx_ref, dy_ref, dg_ref, dgb_ref):
    b_, i = pl.program_id(0), pl.program_id(1)
    dz = _ln_bwd(do * gb_ref[0:1, :], zh, r)
    dx_ref[...] = ALPHA * dz
    dy_ref[...] = (wg * dz).astype(BF16)

    @pl.when((b_ == 0) & (i == 0))
    def _():
        dgb_ref[...] = jnp.zeros_like(dgb_ref)

    dgb_ref[0:1, :] += _rsum(do * zh)
    dgb_ref[1:2, :] += _rsum(do)

    init = (i == 0) | (i == 1) if comb else (i == 0)

    @pl.when(init)
    def _():
        dg_ref[...] = jnp.zeros_like(dg_ref)

    dg_ref[0, 0] += w * _rsum(dz * y)


def _resid_lnmod(X, lx, Y, mvec, comb, sub, w, gb, B, nt, name):
    def body(x_ref, y_ref, m_ref, gb_ref, o_ref, h_ref):
        _, _, zh, _ = _resid_fwd(x_ref, y_ref, m_ref, gb_ref, sub, w)
        xn = zh * gb_ref[0:1, :] + gb_ref[1:2, :]
        o_ref[...] = xn
        h_ref[...] = _modulate(xn, m_ref, sub + 1)

    rows = pl.BlockSpec((TM, D), _rowmap(nt, 0))
    return pl.pallas_call(
        body, name=name, grid=(B, nt),
        in_specs=[pl.BlockSpec((TM, D), _rowmap(*lx)), rows,
                  pl.BlockSpec((1, 1, N_MOD, D), _mmap(comb)), pl.BlockSpec((2, D), lambda b, i: (0, 0))],
        out_specs=[rows, rows],
        out_shape=[jax.ShapeDtypeStruct((B * nt * TM, D), F32), jax.ShapeDtypeStruct((B * nt * TM, D), BF16)],
        compiler_params=_cp(2))(X, Y, mvec, gb)


def _resid_out_shapes(B, nt, comb):
    rows = pl.BlockSpec((TM, D), _rowmap(nt, 0))
    specs = [rows, rows, pl.BlockSpec((1, 1, 1, D), _mmap(comb)), pl.BlockSpec((2, D), lambda b, i: (0, 0))]
    shapes = [jax.ShapeDtypeStruct((B * nt * TM, D), F32), jax.ShapeDtypeStruct((B * nt * TM, D), BF16),
              jax.ShapeDtypeStruct((B, 2, 1, D), F32), jax.ShapeDtypeStruct((2, D), F32)]
    return specs, shapes


def _tail(X, Y, mvec, gb, tgt, sub, w, B, nt):
    def body(x_ref, y_ref, m_ref, gb_ref, t_ref, dx_ref, dy_ref, dg_ref, dgb_ref, l_ref):
        y, wg, zh, r = _resid_fwd(x_ref, y_ref, m_ref, gb_ref, sub, w)
        e = (zh * gb_ref[0:1, :] + gb_ref[1:2, :]) - t_ref[0]

        @pl.when((pl.program_id(0) == 0) & (pl.program_id(1) == 0))
        def _():
            l_ref[...] = jnp.zeros_like(l_ref)

        l_ref[...] += _rsum(e * e)
        _resid_grads(e * (1.0 / D), y, wg, zh, r, w, gb_ref, False, dx_ref, dy_ref, dg_ref, dgb_ref)

    rows = pl.BlockSpec((TM, D), _rowmap(nt, 0))
    specs, shapes = _resid_out_shapes(B, nt, False)
    return pl.pallas_call(
        body, name="resid2_loss_bwd", grid=(B, nt),
        in_specs=[rows, rows, pl.BlockSpec((1, 1, N_MOD, D), _mmap(False)), pl.BlockSpec((2, D), lambda b, i: (0, 0)),
                  pl.BlockSpec((1, TM, D), lambda b, i: (b, i, 0))],
        out_specs=specs + [pl.BlockSpec((1, D), lambda b, i: (0, 0))],
        out_shape=shapes + [jax.ShapeDtypeStruct((1, D), F32)],
        compiler_params=_cp(2))(X, Y, mvec, gb, tgt)


def _lnmod_resid_bwd(dh, Xi, lxi, dres, Xp, lxp, Yp, mvec, comb, sub, w, gb, B, nt, name):
    ntl = nt - 1 if comb else nt

    def body(dh_ref, xi_ref, dr_ref, xp_ref, yp_ref, m_ref, gb_ref, dx_ref, dy_ref, dg_ref, dgb_ref, dm_ref):
        i = pl.program_id(1)
        xh, r = _ln(xi_ref[...])
        sh = m_ref[0, 0, 3 * sub:3 * sub + 1, :]
        dhv = dh_ref[...]
        dr = dr_ref[...]
        if comb:
            dr = jnp.where(i > 0, dr, 0.0)
        do = _ln_bwd(dhv * (1.0 + sh), xh, r) + dr

        init = (i == 0) | (i == 1) if comb else (i == 0)

        @pl.when(init)
        def _():
            dm_ref[...] = jnp.zeros_like(dm_ref)

        dm_ref[0, 0, 0:1, :] += _rsum(dhv * xh)
        dm_ref[0, 0, 1:2, :] += _rsum(dhv)
        y, wg, zh, r2 = _resid_fwd(xp_ref, yp_ref, m_ref, gb_ref, sub - 1, w)
        _resid_grads(do, y, wg, zh, r2, w, gb_ref, comb, dx_ref, dy_ref, dg_ref, dgb_ref)

    rows = pl.BlockSpec((TM, D), _rowmap(nt, 0))
    if comb:
        dres_spec = pl.BlockSpec((TM, D), lambda b, i: (b * ntl + jnp.maximum(i - 1, 0), 0))
    else:
        dres_spec = rows
    specs, shapes = _resid_out_shapes(B, nt, comb)
    return pl.pallas_call(
        body, name=name, grid=(B, nt),
        in_specs=[rows, pl.BlockSpec((TM, D), _rowmap(*lxi)), dres_spec, pl.BlockSpec((TM, D), _rowmap(*lxp)), rows,
                  pl.BlockSpec((1, 1, N_MOD, D), _mmap(comb)), pl.BlockSpec((2, D), lambda b, i: (0, 0))],
        out_specs=specs + [pl.BlockSpec((1, 1, 2, D), _mmap(comb))],
        out_shape=shapes + [jax.ShapeDtypeStruct((B, 2, 2, D), F32)],
        compiler_params=_cp(2))(dh, Xi, dres, Xp, Yp, mvec, gb)


def _ffn_out_dx(dy, W, u, name, dep=None):
    M = dy.shape[0]
    half = DFF // 2
    deps = [] if dep is None else [dep]

    def body(dy_ref, w_ref, u_ref, *rest):
        du_ref = rest[-1]
        dyv = dy_ref[...]
        for j in range(2):
            lo, hi = j * half, (j + 1) * half
            da = _dot_nt(dyv, w_ref[lo:hi, :])
            g = u_ref[:, lo:hi].astype(F32)
            up = u_ref[:, DFF + lo:DFF + hi].astype(F32)
            s = _sigmoid(g)
            du_ref[:, lo:hi] = (da * up * (s * (1.0 + g * (1.0 - s)))).astype(BF16)
            du_ref[:, DFF + lo:DFF + hi] = (da * (g * s)).astype(BF16)

    return pl.pallas_call(
        body, name=name, grid=(M // TM,),
        in_specs=[pl.BlockSpec((TM, D), lambda i: (i, 0)), _wspec(W, 1), pl.BlockSpec((TM, 2 * DFF), lambda i: (i, 0))]
        + [_wspec(d, 1) for d in deps],
        out_specs=pl.BlockSpec((TM, 2 * DFF), lambda i: (i, 0)),
        out_shape=jax.ShapeDtypeStruct((M, 2 * DFF), BF16), compiler_params=_cp(1))(dy, W, u, *deps)


def _wspec(W, nidx):
    zeros = (0,) * W.ndim
    if nidx == 1:
        return pl.BlockSpec(W.shape, lambda i: zeros)
    return pl.BlockSpec(W.shape, lambda b, i: zeros)


def _mm_nn(A, la, W, B, nt, out_dtype, name):
    K, N = W.shape

    def body(a_ref, w_ref, o_ref):
        o_ref[...] = _dot(a_ref[...], w_ref[...]).astype(out_dtype)

    return pl.pallas_call(
        body, name=name, grid=(B, nt),
        in_specs=[pl.BlockSpec((TM, K), _rowmap(*la)), _wspec(W, 2)],
        out_specs=pl.BlockSpec((TM, N), _rowmap(nt, 0)),
        out_shape=jax.ShapeDtypeStruct((B * nt * TM, N), out_dtype), compiler_params=_cp(2))(A, W)


def _ffn_in(A, la, W3, B, nt, name):
    K, n = W3.shape[1:]

    def body(a_ref, w_ref, u_ref, s_ref):
        a = a_ref[...]
        for j in range(2):
            g = _dot(a, w_ref[j])
            up = _dot(a, w_ref[j + 2])
            u_ref[:, j * n:(j + 1) * n] = g.astype(BF16)
            u_ref[:, (j + 2) * n:(j + 3) * n] = up.astype(BF16)
            s_ref[:, j * n:(j + 1) * n] = (g * _sigmoid(g) * up).astype(BF16)

    rows = B * nt * TM
    return pl.pallas_call(
        body, name=name, grid=(B, nt),
        in_specs=[pl.BlockSpec((TM, K), _rowmap(*la)), _wspec(W3, 2)],
        out_specs=[pl.BlockSpec((TM, 4 * n), _rowmap(nt, 0)), pl.BlockSpec((TM, 2 * n), _rowmap(nt, 0))],
        out_shape=[jax.ShapeDtypeStruct((rows, 4 * n), BF16), jax.ShapeDtypeStruct((rows, 2 * n), BF16)],
        compiler_params=_cp(2))(A, W3)


def _mm_nt(A, W, name, dep=None):
    M, N = A.shape
    K = W.shape[-2]

    def body(a_ref, w_ref, *rest):
        o_ref = rest[-1]
        if W.ndim == 3:
            n = W.shape[-1]
            acc = _dot_nt(a_ref[:, 0:n], w_ref[0])
            for j in range(1, 4):
                acc = acc + _dot_nt(a_ref[:, j * n:(j + 1) * n], w_ref[j])
            o_ref[...] = acc
        else:
            o_ref[...] = _dot_nt(a_ref[...], w_ref[...])

    deps = [] if dep is None else [dep]
    return pl.pallas_call(
        body, name=name, grid=(M // TM,),
        in_specs=[pl.BlockSpec((TM, N), lambda i: (i, 0)), _wspec(W, 1)] + [_wspec(d, 1) for d in deps],
        out_specs=pl.BlockSpec((TM, K), lambda i: (i, 0)),
        out_shape=jax.ShapeDtypeStruct((M, K), F32), compiler_params=_cp(1))(A, W, *deps)


def _mm_tn(A, G, name, tn=512, shards=None):
    M, K = A.shape
    N = G.shape[1]
    tk = next(t for t in ((2048, 1536, 1024, 512) if K <= D else (1024, 512)) if M % t == 0)
    if shards:
        tn = N // shards

    def body(a_ref, g_ref, o_ref):
        @pl.when(pl.program_id(1) == 0)
        def _():
            o_ref[...] = jnp.zeros_like(o_ref)

        upd = _dot_tn(a_ref[...], g_ref[...])
        if shards:
            o_ref[0] += upd
        else:
            o_ref[...] += upd

    if shards:
        out_spec = pl.BlockSpec((1, K, tn), lambda n, k: (n, 0, 0))
        out_shape = jax.ShapeDtypeStruct((shards, K, tn), F32)
    else:
        out_spec = pl.BlockSpec((K, tn), lambda n, k: (0, n))
        out_shape = jax.ShapeDtypeStruct((K, N), F32)
    return pl.pallas_call(
        body, name=name, grid=(N // tn, M // tk),
        in_specs=[pl.BlockSpec((tk, K), lambda n, k: (k, 0)), pl.BlockSpec((tk, tn), lambda n, k: (k, n))],
        out_specs=out_spec, out_shape=out_shape, compiler_params=_cp(2))(A, G)


def _logsig(z):
    return jnp.minimum(z, 0.0) - jnp.log(1.0 + jnp.exp(-jnp.abs(z)))


ZW = 2688
ZRQ, ZRFF, ZRFB, ZLR = 1024, 1536, 2048, 2560


def _mix_in_features(H, W, a2p, biasp, lbp, B, nt):
    K = W.shape[0]

    def body(h_ref, w_ref, a2_ref, bias_ref, lb_ref, q_ref, k_ref, v_ref, g_ref, z_ref):
        hv = h_ref[...]
        proj = lambda lo, hi: _dot(hv, w_ref[:, lo:hi])
        z_ref[:, 0:2 * 4 * HD] = proj(GG, GG + 2 * 4 * HD)
        lrf = proj(LR, LR + HD)
        z_ref[:, ZLR:ZLR + HD] = lrf
        lr = lrf.astype(BF16)
        lane = lax.broadcasted_iota(jnp.int32, (1, 4 * HD), 1)
        keep = (lane & (HD - 1)) < GLA_DK
        for d in range(2):
            z = _dot(lr, a2_ref[d]) + bias_ref[d:d + 1, :]
            gl = jnp.where(keep, _logsig(z) * (1.0 / GATE_NORM), 0.0)
            for h in range(4):
                g_ref[d, 0, h] = gl[:, h * HD:(h + 1) * HD]
        gq, gk, gv = proj(GQ, GQ + 4 * HD), proj(GK, GK + 4 * HD), proj(GV, GV + 4 * HD)
        for h in range(4):
            sl = slice(h * HD, (h + 1) * HD)
            q_ref[0, h] = gq[:, sl] * (GLA_DK ** -0.5)
            k_ref[0, 0, h] = gk[:, sl]
            k_ref[1, 0, h] = gk[:, sl]
            v_ref[0, h] = gv[:, sl].astype(BF16)
        rqs, ri = proj(RQ, RQ + 4 * HD), proj(RI, RI + 4 * HD)
        z_ref[:, ZRQ:ZRQ + 4 * HD] = rqs
        for h in range(4):
            sl = slice(h * HD, (h + 1) * HD)
            rq = rqs[:, sl]
            q_ref[0, 4 + h] = rq * _sigmoid(rq) * (HD ** -0.5)
            v_ref[0, 4 + h] = ri[:, sl].astype(BF16)
        for d, off, zoff in ((0, RFF, ZRFF), (1, RFB, ZRFB)):
            rf = proj(off, off + 4 * HD)
            z_ref[:, zoff:zoff + 4 * HD] = rf
            for h in range(4):
                sl = slice(h * HD, (h + 1) * HD)
                lb = lb_ref[d:d + 1, sl]
                f = lb + (1.0 - lb) * _sigmoid(rf[:, sl])
                g_ref[d, 0, 4 + h] = jnp.log(f)
                k_ref[d, 0, 4 + h] = 1.0 - f

    one = pl.BlockSpec((1, NH, TM, HD), lambda b, i: (b, 0, i, 0))
    two = pl.BlockSpec((2, 1, NH, TM, HD), lambda b, i: (0, b, 0, i, 0))
    s1 = jax.ShapeDtypeStruct((B, NH, nt * TM, HD), F32)
    s2 = jax.ShapeDtypeStruct((2, B, NH, nt * TM, HD), F32)
    return pl.pallas_call(
        body, name="mix_in_features", grid=(B, nt),
        in_specs=[pl.BlockSpec((TM, K), _rowmap(nt, 0)), _wspec(W, 2), pl.BlockSpec((2, HD, 4 * HD), lambda b, i: (0, 0, 0)),
                  pl.BlockSpec((2, 4 * HD), lambda b, i: (0, 0)), pl.BlockSpec((2, 4 * HD), lambda b, i: (0, 0))],
        out_specs=[one, two, one, two, pl.BlockSpec((TM, ZW), _rowmap(nt, 0))],
        out_shape=[s1, s2, jax.ShapeDtypeStruct(s1.shape, BF16), s2, jax.ShapeDtypeStruct((B * nt * TM, ZW), F32)],
        compiler_params=_cp(2))(H, W, a2p, biasp, lbp)


def _features_bwd(Z, a2p, biasp, lbp, dQ0, dQ1, dK0, dK1, dV0, dV1, dG0, dG1, dgates, B):
    nt = Z.shape[0] // (B * TM)

    def body(z_ref, a2_ref, bias_ref, lb_ref, dq0, dq1, dk0, dk1, dv0, dv1, dg0, dg1, dgt_ref,
             df_ref, da2_ref, dbias_ref, dlb_ref):
        b_, i = pl.program_id(0), pl.program_id(1)

        @pl.when((b_ == 0) & (i == 0))
        def _():
            da2_ref[...] = jnp.zeros_like(da2_ref)
            dbias_ref[...] = jnp.zeros_like(dbias_ref)
            dlb_ref[...] = jnp.zeros_like(dlb_ref)

        df_ref[:, 0:2 * 4 * HD] = jnp.where(i > 0, dgt_ref[...], 0.0).astype(BF16)
        df_ref[:, LR + HD:] = jnp.zeros((TM, MIXP - LR - HD), BF16)
        lr = z_ref[:, ZLR:ZLR + HD].astype(BF16)
        lane = lax.broadcasted_iota(jnp.int32, (1, 4 * HD), 1)
        keep = (lane & (HD - 1)) < GLA_DK
        dlr = jnp.zeros((TM, HD), F32)
        dgs = (dg0, dg1)
        dks = (dk0, dk1)
        rd = lambda ref, h: ref[0, h].astype(F32)
        for d in range(2):
            z = _dot(lr, a2_ref[d]) + bias_ref[d:d + 1, :]
            dgl = jnp.concatenate([rd(dgs[d], h) for h in range(4)], axis=1)
            dz = jnp.where(keep, dgl * (1.0 / GATE_NORM) * (1.0 - _sigmoid(z)), 0.0)
            dzb = dz.astype(BF16)
            dlr = dlr + _dot_nt(dzb, a2_ref[d])
            da2_ref[d] += _dot_tn(lr, dzb)
            dbias_ref[d:d + 1, :] += _rsum(dz)
        df_ref[:, LR:LR + HD] = dlr.astype(BF16)
        for h in range(4):
            df_ref[:, GQ + h * HD:GQ + (h + 1) * HD] = ((rd(dq0, h) + rd(dq1, h)) * (GLA_DK ** -0.5)).astype(BF16)
            df_ref[:, GK + h * HD:GK + (h + 1) * HD] = (rd(dk0, h) + rd(dk1, h)).astype(BF16)
            df_ref[:, GV + h * HD:GV + (h + 1) * HD] = (rd(dv0, h) + rd(dv1, h)).astype(BF16)
        for h in range(4):
            sl = slice(h * HD, (h + 1) * HD)
            rq = z_ref[:, ZRQ + h * HD:ZRQ + (h + 1) * HD]
            s = _sigmoid(rq)
            dqh = rd(dq0, 4 + h) + rd(dq1, 4 + h)
            df_ref[:, RQ + h * HD:RQ + (h + 1) * HD] = (dqh * (HD ** -0.5) * (s * (1.0 + rq * (1.0 - s)))).astype(BF16)
            df_ref[:, RI + h * HD:RI + (h + 1) * HD] = (rd(dv0, 4 + h) + rd(dv1, 4 + h)).astype(BF16)
            for d, off, zoff in ((0, RFF, ZRFF), (1, RFB, ZRFB)):
                lb = lb_ref[d:d + 1, sl]
                sg = _sigmoid(z_ref[:, zoff + h * HD:zoff + (h + 1) * HD])
                f = lb + (1.0 - lb) * sg
                dff = rd(dgs[d], 4 + h) / f - rd(dks[d], 4 + h)
                df_ref[:, off + h * HD:off + (h + 1) * HD] = (dff * (1.0 - lb) * sg * (1.0 - sg)).astype(BF16)
                dlb_ref[d:d + 1, sl] += _rsum(dff * (1.0 - sg))

    m0 = lambda b, i: (b, 0, i, 0)
    one = lambda m: pl.BlockSpec((1, NH, TM, HD), m)
    return pl.pallas_call(
        body, name="mix_features_bwd", grid=(B, nt),
        in_specs=[pl.BlockSpec((TM, ZW), _rowmap(nt, 0)), pl.BlockSpec((2, HD, 4 * HD), lambda b, i: (0, 0, 0)),
                  pl.BlockSpec((2, 4 * HD), lambda b, i: (0, 0)), pl.BlockSpec((2, 4 * HD), lambda b, i: (0, 0)),
                  one(m0), one(m0), one(m0), one(m0), one(m0), one(m0), one(m0), one(m0),
                  pl.BlockSpec((TM, D), lambda b, i: (b * (nt - 1) + jnp.maximum(i - 1, 0), 0))],
        out_specs=[pl.BlockSpec((TM, MIXP), _rowmap(nt, 0)), pl.BlockSpec((2, HD, 4 * HD), lambda b, i: (0, 0, 0)),
                   pl.BlockSpec((2, 4 * HD), lambda b, i: (0, 0)), pl.BlockSpec((2, 4 * HD), lambda b, i: (0, 0))],
        out_shape=[jax.ShapeDtypeStruct((B * nt * TM, MIXP), BF16), jax.ShapeDtypeStruct((2, HD, 4 * HD), F32),
                   jax.ShapeDtypeStruct((2, 4 * HD), F32), jax.ShapeDtypeStruct((2, 4 * HD), F32)],
        compiler_params=_cp(2))(Z, a2p, biasp, lbp, dQ0, dQ1, dK0, dK1, dV0, dV1, dG0, dG1, dgates)


def _chunk_scan(x, rin, fwd):
    acc = x
    sft = 1
    while sft < CH:
        if fwd:
            acc = acc + jnp.where(rin >= sft, pltpu.roll(acc, sft, 0), 0.0)
        else:
            acc = acc + jnp.where(rin < CH - sft, pltpu.roll(acc, TM - sft, 0), 0.0)
        sft *= 2
    return acc


def _chunk_total(x):
    t = jnp.sum(x.reshape(NCB, CH, HD), axis=1, keepdims=True)
    return jnp.broadcast_to(t, (NCB, CH, HD)).reshape(TM, HD)


def _scan_masks(rev):
    rin = lax.broadcasted_iota(jnp.int32, (TM, HD), 0) & (CH - 1)
    ri = lax.broadcasted_iota(jnp.int32, (SB, SB), 0)
    ci = lax.broadcasted_iota(jnp.int32, (SB, SB), 1)
    same = (ri >> 5) == (ci >> 5)
    lo = same & (ri >= ci)
    up = same & (ri <= ci)
    mask, maskT = (up, lo) if rev else (lo, up)
    re = lax.broadcasted_iota(jnp.int32, (SB, CSB * HD), 0) >> 5
    ce = lax.broadcasted_iota(jnp.int32, (SB, CSB * HD), 1) >> 7
    return rin, mask, maskT, re == ce


def _scan_decay(q, k, g, rin, rev):
    b = _chunk_scan(g, rin, not rev)
    xx = _chunk_total(g) - b
    eb = jnp.exp(b)
    return b, xx, eb, q * eb, k * jnp.exp(-b), k * jnp.exp(xx)


def _sub(x, s):
    return x[s * SB:(s + 1) * SB]


def _expand(xb, mexp):
    return jnp.where(mexp, jnp.concatenate([xb] * CSB, axis=1), jnp.zeros((), xb.dtype))


def _own(x, mexp):
    xm = jnp.where(mexp, x, 0.0)
    acc = xm[:, 0:HD]
    for n in range(1, CSB):
        acc = acc + xm[:, n * HD:(n + 1) * HD]
    return acc


def _stack(per_chunk, s):
    return jnp.concatenate(per_chunk[s * CSB:(s + 1) * CSB], axis=1)


def _state_pass(s0, eb, uts, rev):
    order = range(NCB - 1, -1, -1) if rev else range(NCB)
    states = [None] * NCB
    s = s0
    for n in order:
        row = n * CH if rev else n * CH + CH - 1
        states[n] = s
        s = eb[row:row + 1, :] * s + uts[n // CSB][:, (n % CSB) * HD:(n % CSB + 1) * HD]
    return states, s


def _scan_fwd(Q, K, V, G, rev, B):
    nb = Q.shape[2] // TM
    d = 1 if rev else 0
    rmap = (lambda s: jnp.where(s == 0, 0, nb - s)) if rev else (lambda s: s)

    def body(q_ref, k_ref, v_ref, g_ref, o_ref, st_ref, s_scr):
        @pl.when(pl.program_id(2) == 0)
        def _():
            s_scr[...] = jnp.zeros_like(s_scr)

        rin, mask, _, mexp = _scan_masks(rev)

        def head(p, readout):
            s0 = s_scr[p]
            st_ref[0, p, 0] = s0
            _, _, eb, qd, ki, kt = _scan_decay(q_ref[0, p], k_ref[0, 0, p], g_ref[0, 0, p], rin, rev)
            ktb, vb = kt.astype(BF16), v_ref[0, p]
            uts = [_dot_tn(_sub(vb, s), _expand(_sub(ktb, s), mexp)) for s in range(NSB)]
            states, s_new = _state_pass(s0, eb, uts, rev)
            s_scr[p] = s_new
            if not readout:
                return
            qb, kib = qd.astype(BF16), ki.astype(BF16)
            for s in range(NSB):
                a = jnp.where(mask, _dot_nt(_sub(qb, s), _sub(kib, s)), 0.0)
                o_ref[0, p, s * SB:(s + 1) * SB, :] = (
                    _dot(a.astype(BF16), _sub(vb, s))
                    + _dot_nt(_expand(_sub(qb, s), mexp), _stack(states, s).astype(BF16)))

        @pl.when(pl.program_id(2) >= 1)
        def _():
            for p in range(HP):
                head(p, True)

        @pl.when(pl.program_id(2) == 0)
        def _():
            for p in range(HP):
                head(p, False)

    one = pl.BlockSpec((1, HP, TM, HD), lambda b, h, s: (b, h, rmap(s), 0))
    two = pl.BlockSpec((1, 1, HP, TM, HD), lambda b, h, s: (d, b, h, rmap(s), 0))
    return pl.pallas_call(
        body, name="scan_fwd_rev" if rev else "scan_fwd", grid=(B, NH // HP, nb),
        in_specs=[one, two, one, two],
        out_specs=[one, pl.BlockSpec((1, HP, 1, HD, HD), lambda b, h, s: (b, h, s, 0, 0))],
        out_shape=[jax.ShapeDtypeStruct(Q.shape, F32), jax.ShapeDtypeStruct((B, NH, nb, HD, HD), F32)],
        scratch_shapes=[pltpu.VMEM((HP, HD, HD), F32)],
        compiler_params=_cp(3))(Q, K, V, G)


def _scan_bwd(Q, K, V, G, St, dO, rev, B):
    nb = Q.shape[2] // TM
    d = 1 if rev else 0
    smap = lambda t: nb - 1 - t
    rmap = (lambda t: jnp.where(smap(t) == 0, 0, nb - smap(t))) if rev else smap

    def body(q_ref, k_ref, v_ref, g_ref, st_ref, do_ref, dq_ref, dk_ref, dv_ref, dg_ref, ds_scr):
        t = pl.program_id(2)

        @pl.when(t == 0)
        def _():
            ds_scr[...] = jnp.zeros_like(ds_scr)

        is_lat = smap(t) >= 1
        rin, mask, maskT, mexp = _scan_masks(rev)
        for p in range(HPB):
            b, xx, eb, qd, ki, kt = _scan_decay(q_ref[0, p], k_ref[0, 0, p], g_ref[0, 0, p], rin, rev)
            qb, kib, ktb, vb = qd.astype(BF16), ki.astype(BF16), kt.astype(BF16), v_ref[0, p]
            dob = jnp.where(is_lat, do_ref[0, p], 0.0).astype(BF16)
            kt_exps = [_expand(_sub(ktb, s), mexp) for s in range(NSB)]
            uts = [_dot_tn(_sub(vb, s), kt_exps[s]) for s in range(NSB)]
            states, _ = _state_pass(st_ref[0, p, 0], eb, uts, rev)
            gts = [_dot_tn(_sub(dob, s), _expand(_sub(qb, s), mexp)) for s in range(NSB)]
            order = range(NCB) if rev else range(NCB - 1, -1, -1)
            dsp = [None] * NCB
            t2 = [None] * NCB
            dsc = ds_scr[p]
            for n in order:
                row = n * CH if rev else n * CH + CH - 1
                ebl = eb[row:row + 1, :]
                dsp[n] = dsc
                t2[n] = jnp.broadcast_to(ebl * _rsum(states[n] * dsc), (CH, HD))
                dsc = gts[n // CSB][:, (n % CSB) * HD:(n % CSB + 1) * HD] + ebl * dsc
            ds_scr[p] = dsc
            dqds, dkis, dkts = [], [], []
            for s in range(NSB):
                q_s, ki_s, v_s, do_s = _sub(qb, s), _sub(kib, s), _sub(vb, s), _sub(dob, s)
                dspb = _stack(dsp, s).astype(BF16)
                da = jnp.where(mask, _dot_nt(do_s, v_s), 0.0).astype(BF16)
                dat = jnp.where(maskT, _dot_nt(v_s, do_s), 0.0).astype(BF16)
                at = jnp.where(maskT, _dot_nt(ki_s, q_s), 0.0).astype(BF16)
                dqds.append(_dot(da, ki_s) + _own(_dot(do_s, _stack(states, s).astype(BF16)), mexp))
                dkis.append(_dot(dat, q_s))
                dv_ref[0, p, s * SB:(s + 1) * SB, :] = (_dot(at, do_s) + _dot_nt(kt_exps[s], dspb)).astype(BF16)
                dkts.append(_own(_dot(v_s, dspb), mexp))
            dqd, dki, dkt = (jnp.concatenate(parts, axis=0) for parts in (dqds, dkis, dkts))
            z = dkt * kt
            db = dqd * qd - dki * ki
            dq_ref[0, p] = (dqd * eb).astype(BF16)
            dk_ref[0, p] = (dki * jnp.exp(-b) + dkt * jnp.exp(xx)).astype(BF16)
            dg_ref[0, p] = (_chunk_total(db) + (db - z) + _chunk_scan(z - db, rin, not rev)
                            + jnp.concatenate(t2, axis=0)).astype(BF16)

    one = pl.BlockSpec((1, HPB, TM, HD), lambda b, h, t: (b, h, rmap(t), 0))
    two = pl.BlockSpec((1, 1, HPB, TM, HD), lambda b, h, t: (d, b, h, rmap(t), 0))
    lat = pl.BlockSpec((1, HPB, TM, HD), lambda b, h, t: (b, h, jnp.clip(rmap(t) - 1, 0, nb - 2), 0))
    shp = jax.ShapeDtypeStruct(Q.shape, BF16)
    return pl.pallas_call(
        body, name="scan_bwd_rev" if rev else "scan_bwd", grid=(B, NH // HPB, nb),
        in_specs=[one, two, one, two, pl.BlockSpec((1, HPB, 1, HD, HD), lambda b, h, t: (b, h, smap(t), 0, 0)), lat],
        out_specs=[one, one, one, one], out_shape=[shp, shp, shp, shp],
        scratch_shapes=[pltpu.VMEM((HPB, HD, HD), F32)],
        compiler_params=_cp(3))(Q, K, V, G, St, dO)


def _gnorm(O0, O1, F, gains, B, ntl):
    nt = ntl + 1

    def body(o0_ref, o1_ref, f_ref, gn_ref, m_ref):
        for h in range(NH):
            o = o0_ref[0, h] + o1_ref[0, h]
            r = lax.rsqrt(jnp.mean(o * o, axis=-1, keepdims=True) + NORM_EPS)
            gn = gn_ref[0:1, :] if h < 4 else gn_ref[1:2, :]
            gt = f_ref[:, h * HD:(h + 1) * HD]
            m_ref[:, h * HD:(h + 1) * HD] = (o * r * gn * (gt * _sigmoid(gt))).astype(BF16)

    ospec = pl.BlockSpec((1, NH, TM, HD), lambda b, i: (b, 0, i + 1, 0))
    return pl.pallas_call(
        body, name="gated_norm", grid=(B, ntl),
        in_specs=[ospec, ospec, pl.BlockSpec((TM, D), lambda b, i: (b * nt + 1 + i, 0)),
                  pl.BlockSpec((2, HD), lambda b, i: (0, 0))],
        out_specs=pl.BlockSpec((TM, D), _rowmap(ntl, 0)),
        out_shape=jax.ShapeDtypeStruct((B * ntl * TM, D), BF16), compiler_params=_cp(2))(O0, O1, F, gains)


def _gnorm_bwd(dM, O0, O1, F, gains, B, ntl):
    nt = ntl + 1

    def body(dm_ref, o0_ref, o1_ref, f_ref, gn_ref, do_ref, dgt_ref, dgn_ref):
        b_, i = pl.program_id(0), pl.program_id(1)

        @pl.when((b_ == 0) & (i == 0))
        def _():
            dgn_ref[...] = jnp.zeros_like(dgn_ref)

        for h in range(NH):
            o = o0_ref[0, h] + o1_ref[0, h]
            r = lax.rsqrt(jnp.mean(o * o, axis=-1, keepdims=True) + NORM_EPS)
            y = o * r
            gn = gn_ref[0:1, :] if h < 4 else gn_ref[1:2, :]
            gt = f_ref[:, h * HD:(h + 1) * HD]
            s = _sigmoid(gt)
            dm = dm_ref[:, h * HD:(h + 1) * HD]
            don = dm * (gt * s)
            dgt_ref[:, h * HD:(h + 1) * HD] = dm * (y * gn) * (s * (1.0 + gt * (1.0 - s)))
            row = 0 if h < 4 else 1
            dgn_ref[row:row + 1, :] += _rsum(don * y)
            dy = don * gn
            do_ref[0, h] = r * (dy - y * jnp.mean(dy * y, axis=-1, keepdims=True))

    ospec = pl.BlockSpec((1, NH, TM, HD), lambda b, i: (b, 0, i + 1, 0))
    return pl.pallas_call(
        body, name="gated_norm_bwd", grid=(B, ntl),
        in_specs=[pl.BlockSpec((TM, D), _rowmap(ntl, 0)), ospec, ospec,
                  pl.BlockSpec((TM, D), lambda b, i: (b * nt + 1 + i, 0)), pl.BlockSpec((2, HD), lambda b, i: (0, 0))],
        out_specs=[pl.BlockSpec((1, NH, TM, HD), lambda b, i: (b, 0, i, 0)), pl.BlockSpec((TM, D), _rowmap(ntl, 0)),
                   pl.BlockSpec((2, HD), lambda b, i: (0, 0))],
        out_shape=[jax.ShapeDtypeStruct((B, NH, ntl * TM, HD), F32), jax.ShapeDtypeStruct((B * ntl * TM, D), F32),
                   jax.ShapeDtypeStruct((2, HD), F32)],
        compiler_params=_cp(2))(dM, O0, O1, F, gains)


def _sincos_2d(rows, width, dim):
    quarter = dim // 4
    omega = 1.0 / 10000.0 ** (jnp.arange(quarter, dtype=F32) / quarter)

    def emb(n):
        a = jnp.arange(n).astype(F32)[:, None] * omega[None, :]
        return jnp.concatenate([jnp.sin(a), jnp.cos(a)], axis=-1)

    er = jnp.broadcast_to(emb(rows)[:, None, :], (rows, width, dim // 2))
    ec = jnp.broadcast_to(emb(width)[None, :, :], (rows, width, dim // 2))
    return jnp.concatenate([er, ec], axis=-1).reshape(rows * width, dim)


def _pad_heads(w):
    k = w.shape[0]
    return jnp.pad(w.reshape(k, 4, GLA_DK), ((0, 0), (0, 0), (0, HD - GLA_DK))).reshape(k, 4 * HD)


def _unpad_heads(w):
    k = w.shape[0]
    return w.reshape(k, 4, HD)[:, :, :GLA_DK].reshape(k, 4 * GLA_DK)


MIX_N = 1032
MIX_NP = 1152
_SEGS = ([(64 * h, 64, GQ + HD * h) for h in range(4)] + [(256 + 64 * h, 64, GK + HD * h) for h in range(4)]
         + [(512, 512, GV), (1024, 512, GG), (1536, 32, LR), (1568, 512, RQ), (2080, 512, RFF), (2592, 512, RFB),
            (3104, 512, RI), (3616, 512, RG)])


def _mix_in_to_padded(ps):
    k = ps.shape[1]
    parts, pos = [], 0
    for g0, ln, s0 in sorted(_SEGS, key=lambda s: s[2]):
        if s0 > pos:
            parts.append(jnp.zeros((k, s0 - pos), ps.dtype))
        for j in range(4):
            lo, hi = max(g0, j * MIX_N), min(g0 + ln, (j + 1) * MIX_N)
            if lo < hi:
                parts.append(ps[j][:, lo - j * MIX_N:hi - j * MIX_N])
        pos = s0 + ln
    parts.append(jnp.zeros((k, MIXP - pos), ps.dtype))
    return jnp.concatenate(parts, axis=1)


def _mix_in_from_padded(g):
    k = g.shape[0]
    shards = []
    for j in range(4):
        parts = []
        for g0, ln, s0 in sorted(_SEGS):
            lo, hi = max(g0, j * MIX_N), min(g0 + ln, (j + 1) * MIX_N)
            if lo < hi:
                parts.append(g[:, s0 + lo - g0:s0 + hi - g0])
        parts.append(jnp.zeros((k, MIX_NP - MIX_N), g.dtype))
        shards.append(jnp.concatenate(parts, axis=1))
    return jnp.stack(shards)


def _local_step(x, ctx, tgt, mvec, weights_for, ln_gain, ln_bias, a2f, a2b, abf, abb, lb, gng, gnh, on_grads, on_sent):
    B, T, _ = x.shape
    assert ctx.shape[1] == TM and T % TM == 0
    ntl = T // TM
    nt = ntl + 1
    C, L, CL = (nt, 0), (ntl, 0), (nt, 1)
    pos = _sincos_2d(T // 64, 64, D)
    gbs = [jnp.stack([ln_gain[i], ln_bias[i]]) for i in range(3)]
    a2p = jnp.zeros((2, HD, 4 * HD), F32)
    a2p = a2p.at[0, 0:16].set(_pad_heads(a2f)).at[1, 16:32].set(_pad_heads(a2b)).astype(BF16)
    biasp = jnp.concatenate([_pad_heads(abf.reshape(1, -1)), _pad_heads(abb.reshape(1, -1))], axis=0)
    gains = jnp.concatenate([gng.reshape(1, HD), gnh.reshape(1, HD)], axis=0)

    X0, h0 = _embed_lnmod(x, ctx, pos, mvec)
    w1i, w1o = weights_for("ffn1", h0)
    u0, a0 = _ffn_in(h0, C, w1i, B, nt, "ffn1_in")
    wmp, wmo = weights_for("mix", a0)
    y0 = _mm_nn(a0, C, w1o, B, nt, F32, "ffn1_out")
    X1, h1 = _resid_lnmod(X0, C, y0, mvec, True, 0, 0.5, gbs[0], B, nt, "resid0_lnmod1")
    Q, K, V, G, Zm = _mix_in_features(h1, wmp, a2p, biasp, lb, B, nt)
    w2i, w2o = weights_for("ffn2", Zm)
    O0, S0 = _scan_fwd(Q, K, V, G, False, B)
    O1, S1 = _scan_fwd(Q, K, V, G, True, B)
    merged = _gnorm(O0, O1, Zm, gains, B, ntl)
    y1 = _mm_nn(merged, L, wmo, B, ntl, F32, "mix_out")
    X2, h2 = _resid_lnmod(X1, CL, y1, mvec, False, 1, 1.0, gbs[1], B, ntl, "resid1_lnmod2")
    u2, a2 = _ffn_in(h2, L, w2i, B, ntl, "ffn2_in")
    y2 = _mm_nn(a2, L, w2o, B, ntl, F32, "ffn2_out")

    dx2r, dy2, dgate2, dgb2, lsum = _tail(X2, y2, mvec, gbs[2], tgt, 2, 0.5, B, ntl)
    loss = (0.5 / D) * jnp.sum(lsum)
    du2 = _ffn_out_dx(dy2, w2o, u2, "ffn2_out_dx")
    g_w2o = _mm_tn(a2, dy2, "ffn2_out_dw")
    dh2 = _mm_nt(du2, w2i, "ffn2_in_dx")
    g_w2i = _mm_tn(h2, du2, "ffn2_in_dw", shards=4)
    tok = on_grads("ffn2", (g_w2i, g_w2o))
    dx1r, dy1, dgate1, dgb1, dss2 = _lnmod_resid_bwd(dh2, X2, L, dx2r, X1, CL, y1, mvec, False, 2, 1.0,
                                                     gbs[1] + tok[0, 0], B, ntl, "lnmod2_resid1_bwd")
    tok = on_sent("ffn2", dy1)
    dmerged = _mm_nt(dy1, wmo, "mix_out_dx", dep=tok)
    g_wmo = _mm_tn(merged, dy1, "mix_out_dw", tn=D)
    dO, dgates, dgains = _gnorm_bwd(dmerged, O0, O1, Zm, gains, B, ntl)
    dQ0, dK0, dV0, dG0 = _scan_bwd(Q, K, V, G, S0, dO, False, B)
    dQ1, dK1, dV1, dG1 = _scan_bwd(Q, K, V, G, S1, dO, True, B)
    dF, da2p, dbiasp, dlb = _features_bwd(Zm, a2p, biasp, lb, dQ0, dQ1, dK0, dK1, dV0, dV1, dG0, dG1, dgates, B)
    dh1 = _mm_nt(dF, wmp, "mix_in_dx")
    g_wmp = _mm_tn(h1, dF, "mix_in_dw", tn=MIXP // 4)
    tok = on_grads("mix", (g_wmp, g_wmo))
    dx0r, dy0, dgate0, dgb0, dss1 = _lnmod_resid_bwd(dh1, X1, C, dx1r, X0, C, y0, mvec, True, 1, 0.5,
                                                     gbs[0] + tok[0, 0], B, nt, "lnmod1_resid0_bwd")
    tok = on_sent("mix", dy0)
    du0 = _ffn_out_dx(dy0, w1o, u0, "ffn1_out_dx", dep=tok)
    g_w1o = _mm_tn(a0, dy0, "ffn1_out_dw")
    g_w1i = _mm_tn(h0, du0, "ffn1_in_dw", shards=4)
    tok = on_grads("ffn1", (g_w1i, g_w1o))
    dh0 = _mm_nt(du0, w1i, "ffn1_in_dx", dep=tok)
    tok = on_sent("ffn1", dh0)
    grad_x, dss0 = _lnmod0_bwd(dh0, X0, mvec + tok[0, 0], dx0r, x.shape, B, nt)

    zero_ctx = lambda a: a.at[:, 0].set(0.0)
    dm = jnp.concatenate([dss0, dgate0, dss1, zero_ctx(dgate1), zero_ctx(dss2), zero_ctx(dgate2)], axis=2)
    small = dict(
        ln_gain=jnp.stack([dgb0[0], dgb1[0], dgb2[0]]), ln_bias=jnp.stack([dgb0[1], dgb1[1], dgb2[1]]),
        a2f=_unpad_heads(da2p[0, 0:16]), a2b=_unpad_heads(da2p[1, 16:32]),
        abf=_unpad_heads(dbiasp[0:1]), abb=_unpad_heads(dbiasp[1:2]), lb=dlb, gng=dgains[0], gnh=dgains[1])
    return loss, grad_x, dm, small


def _small_allgather(xs, name):
    r, n = xs.shape

    def body(x_ref, out_ref, send_sems, recv_sems, local_sem):
        x, y, c = lax.axis_index("x"), lax.axis_index("y"), lax.axis_index("c")
        me, sibling = (x, y, c), (x, y, 1 - c)
        chips = [(1 - x, y), (x, 1 - y), (1 - x, 1 - y)]

        def rows(px, py, pc):
            return out_ref.at[pl.ds((4 * px + 2 * py + pc) * r, r), :]

        def copy(k, block, to, src=None):
            return pltpu.make_async_remote_copy(
                src_ref=rows(*block) if src is None else src, dst_ref=rows(*block),
                send_sem=send_sems.at[k], recv_sem=recv_sems.at[k], device_id=to, device_id_type=MESH)

        mine = pltpu.make_async_copy(x_ref, rows(*me), local_sem)
        mine.start()
        first = [copy(0, me, sibling, src=x_ref)]
        first += [copy(1 + j, me, (*chip, c), src=x_ref) for j, chip in enumerate(chips)]
        for cp in first:
            cp.start()
        passed = [copy(4 + j, (*chip, c), sibling) for j, chip in enumerate(chips)]
        for j, chip in enumerate(chips):
            copy(1 + j, (*chip, c), me).wait_recv()
            passed[j].start()
        copy(0, sibling, me).wait_recv()
        for j, chip in enumerate(chips):
            copy(4 + j, (*chip, 1 - c), me).wait_recv()
        for cp in first + passed:
            cp.wait_send()
        mine.wait()

    out = pl.pallas_call(
        body, name=name,
        out_shape=jax.ShapeDtypeStruct((8 * r, n), xs.dtype),
        in_specs=[pl.BlockSpec(memory_space=pltpu.VMEM)],
        out_specs=pl.BlockSpec(memory_space=pltpu.VMEM),
        scratch_shapes=[pltpu.SemaphoreType.DMA((7,)), pltpu.SemaphoreType.DMA((7,)), pltpu.SemaphoreType.DMA],
        compiler_params=pltpu.CompilerParams(vmem_limit_bytes=VMEM_LIMIT))(xs)
    return out.reshape(8, r, n)


def _gather_flat(v, name):
    n = v.shape[0]
    npad = -(-n // 1024) * 1024
    g = _small_allgather(jnp.pad(v, (0, npad - n)).reshape(8, npad // 8), name)
    return g.reshape(8, npad)[:, :n]


HBM_SPEC = pl.BlockSpec(memory_space=pltpu.HBM)
SEM_SPEC = pl.BlockSpec(memory_space=pltpu.SEMAPHORE)
DATAFLOW = pltpu.SideEffectType.DATAFLOW_SIDE_EFFECTING


def _gather_copies(xs, outs, send_sems, recv_sems):
    x, y, c = lax.axis_index("x"), lax.axis_index("y"), lax.axis_index("c")
    dests = [(x, y, 1 - c), (1 - x, y, c), (x, 1 - y, c), (1 - x, 1 - y, c)]
    return [pltpu.make_async_remote_copy(
        src_ref=xs[w], dst_ref=outs[w].at[4 * x + 2 * y + c], send_sem=send_sems[4 * w + k],
        recv_sem=recv_sems[4 * w + k], device_id=dests[k], device_id_type=MESH)
        for w in range(len(xs)) for k in range(4)]


def _split_start(copies, per_w, srcs, land_lead, after, name):
    n = len(srcs)
    m = per_w * n
    lands = [lax.empty((land_lead,) + s.shape[-2:], s.dtype) for s in srcs]
    deps = [] if after is None else [after]

    def body(*refs):
        xs, ls, outs = refs[:n], refs[n:2 * n], refs[2 * n + len(deps):]
        for cp in copies(xs, ls, outs[:m], outs[m:2 * m]):
            cp.start()
        token = outs[2 * m + 2 * n]
        token[...] = jnp.zeros_like(token)

    outs = pl.pallas_call(
        body, name=name,
        out_shape=([pltpu.SemaphoreType.DMA(())] * (2 * m) + [pltpu.HBM(a.shape, a.dtype) for a in srcs + lands]
                   + [jax.ShapeDtypeStruct((8, 128), F32)]),
        in_specs=[HBM_SPEC] * (2 * n) + [pl.BlockSpec(memory_space=pl.ANY)] * len(deps),
        out_specs=[SEM_SPEC] * (2 * m) + [HBM_SPEC] * (2 * n) + [pl.BlockSpec(memory_space=pltpu.VMEM)],
        input_output_aliases={w: 2 * m + w for w in range(2 * n)},
        compiler_params=pltpu.CompilerParams(has_side_effects=DATAFLOW),
    )(*[pltpu.with_memory_space_constraint(a, pltpu.HBM) for a in srcs + lands], *deps)
    return outs[:2 * m], outs[2 * m:2 * m + n], outs[2 * m + n:2 * m + 2 * n], outs[2 * m + 2 * n]


def _split_wait(copies, per_w, sems, x_thru, l_thru, after, name):
    n = len(x_thru)
    m = per_w * n

    def body(*refs):
        xs, ls, ss = refs[:n], refs[n:2 * n], refs[2 * n:2 * n + 2 * m]
        for cp in copies(xs, ls, ss[:m], ss[m:]):
            cp.wait_send()
            cp.wait_recv()

    outs = pl.pallas_call(
        body, name=name,
        out_shape=[pltpu.HBM(a.shape, a.dtype) for a in list(x_thru) + list(l_thru)],
        in_specs=[HBM_SPEC] * (2 * n) + [SEM_SPEC] * (2 * m) + [pl.BlockSpec(memory_space=pl.ANY)],
        out_specs=[HBM_SPEC] * (2 * n),
        input_output_aliases={w: w for w in range(2 * n)},
        compiler_params=pltpu.CompilerParams(has_side_effects=DATAFLOW),
    )(*x_thru, *l_thru, *sems, after)
    return outs[:n], outs[n:]


def _gather_forward(gathered, name):
    n = len(gathered)

    def body(*refs):
        outs = refs[n:2 * n]
        send_sems, recv_sems = refs[2 * n:]
        x, y, c = lax.axis_index("x"), lax.axis_index("y"), lax.axis_index("c")
        chips = [(1 - x, y), (x, 1 - y), (1 - x, 1 - y), (x, y)]

        def copy(w, j, pc):
            px, py = chips[j]
            slot = outs[w].at[4 * px + 2 * py + pc]
            return pltpu.make_async_remote_copy(
                src_ref=slot, dst_ref=slot, send_sem=send_sems.at[4 * w + j], recv_sem=recv_sems.at[4 * w + j],
                device_id=(x, y, 1 - c), device_id_type=MESH)

        sends = [copy(w, j, 1 - c if j == 3 else c) for w in range(n) for j in range(4)]
        for cp in sends:
            cp.start()
        for w in range(n):
            for j in range(4):
                copy(w, j, c if j == 3 else 1 - c).wait_recv()
        for cp in sends:
            cp.wait_send()

    any_spec = pl.BlockSpec(memory_space=pl.ANY)
    return pl.pallas_call(
        body, name=name,
        out_shape=[jax.ShapeDtypeStruct(g.shape, g.dtype) for g in gathered],
        in_specs=[any_spec] * n, out_specs=[any_spec] * n,
        input_output_aliases={w: w for w in range(n)},
        scratch_shapes=[pltpu.SemaphoreType.DMA((4 * n,)), pltpu.SemaphoreType.DMA((4 * n,))],
    )(*gathered)


def _pair_copies(gs, ls, send_sems, recv_sems):
    x, y, c = lax.axis_index("x"), lax.axis_index("y"), lax.axis_index("c")
    return [pltpu.make_async_remote_copy(
        src_ref=gs[w].at[2 * j + 1 - c], dst_ref=ls[w].at[j], send_sem=send_sems[4 * w + j],
        recv_sem=recv_sems[4 * w + j], device_id=(x, y, 1 - c), device_id_type=MESH)
        for w in range(len(gs)) for j in range(4)]


def _all_copies(xs, ls, send_sems, recv_sems):
    x, y, c = lax.axis_index("x"), lax.axis_index("y"), lax.axis_index("c")
    flips = [(a, b, e) for a in (0, 1) for b in (0, 1) for e in (0, 1)][1:]
    return [pltpu.make_async_remote_copy(
        src_ref=xs[0], dst_ref=ls[0].at[4 * x + 2 * y + c], send_sem=send_sems[k], recv_sem=recv_sems[k],
        device_id=(1 - x if a else x, 1 - y if b else y, 1 - c if e else c), device_id_type=MESH)
        for k, (a, b, e) in enumerate(flips)]


def _share_copies(fs, ls, send_sems, recv_sems):
    x, y, c = lax.axis_index("x"), lax.axis_index("y"), lax.axis_index("c")
    return [pltpu.make_async_remote_copy(
        src_ref=fs[w].at[c], dst_ref=fs[w].at[c], send_sem=send_sems[w], recv_sem=recv_sems[w],
        device_id=(x, y, 1 - c), device_id_type=MESH) for w in range(len(fs))]


def _chip_copies(hs, ls, send_sems, recv_sems):
    x, y, c = lax.axis_index("x"), lax.axis_index("y"), lax.axis_index("c")
    chips = [(1 - x, y), (x, 1 - y), (1 - x, 1 - y)]
    return [pltpu.make_async_remote_copy(
        src_ref=hs[w].at[2 * px + py], dst_ref=ls[w].at[k], send_sem=send_sems[3 * w + k],
        recv_sem=recv_sems[3 * w + k], device_id=(px, py, c), device_id_type=MESH)
        for w in range(len(hs)) for k, (px, py) in enumerate(chips)]


def _rs_add_pair(g8, r4, c, name):
    R, n = g8.shape[1:]
    rb = R // 2

    def body(c_ref, g_ref, r_ref, o_ref):
        o_ref[...] = (g_ref[...] + r_ref[...]).astype(BF16)

    spec = pl.BlockSpec((1, rb, n), lambda j, i, c_ref: (j, i, 0))
    return pl.pallas_call(
        body, name=name,
        grid_spec=pltpu.PrefetchScalarGridSpec(
            num_scalar_prefetch=1, grid=(4, R // rb),
            in_specs=[pl.BlockSpec((1, rb, n), lambda j, i, c_ref: (2 * j + c_ref[0], i, 0)), spec],
            out_specs=spec),
        out_shape=jax.ShapeDtypeStruct((4, R, n), BF16), compiler_params=_cp(2))(c, g8, r4)


def _rs_add_chips(g8, r4, r3, cj, name):
    R, n = g8.shape[1:]
    rb = R // 2

    def body(cj_ref, g_ref, p_ref, r_ref, o_ref):
        own = g_ref[0] + p_ref[0]
        o_ref[0] = ((own + r_ref[0].astype(F32)) + r_ref[1].astype(F32)) + r_ref[2].astype(F32)

    return pl.pallas_call(
        body, name=name,
        grid_spec=pltpu.PrefetchScalarGridSpec(
            num_scalar_prefetch=1, grid=(R // rb,),
            in_specs=[pl.BlockSpec((1, rb, n), lambda i, cj_ref: (2 * cj_ref[1] + cj_ref[0], i, 0)),
                      pl.BlockSpec((1, rb, n), lambda i, cj_ref: (cj_ref[1], i, 0)),
                      pl.BlockSpec((3, rb, n), lambda i, cj_ref: (0, i, 0))],
            out_specs=pl.BlockSpec((1, rb, n), lambda i, cj_ref: (cj_ref[0], i, 0))),
        out_shape=jax.ShapeDtypeStruct((2, R, n), F32), compiler_params=_cp(1))(cj, g8, r4, r3)


def _sum8(g):
    n = g.shape[1]

    def body(g_ref, o_ref):
        acc = g_ref[0:1, :]
        for k in range(1, 8):
            acc = acc + g_ref[k:k + 1, :]
        o_ref[...] = acc

    return pl.pallas_call(body, name="sum_devices", out_shape=jax.ShapeDtypeStruct((1, n), F32),
                          compiler_params=pltpu.CompilerParams(vmem_limit_bytes=VMEM_LIMIT))(g)


ADA_ROWS = 64


def _ada_fwd(cs, w, b):
    n = w.shape[1]

    def body(c_ref, w_ref, b_ref, o_ref):
        cv = c_ref[...]
        s = (cv * _sigmoid(cv)).astype(BF16)
        o_ref[...] = _dot(s, w_ref[...].astype(BF16)) + b_ref[...]

    return pl.pallas_call(body, name="ada_fwd", out_shape=jax.ShapeDtypeStruct((ADA_ROWS, n), F32),
                          compiler_params=pltpu.CompilerParams(vmem_limit_bytes=VMEM_LIMIT))(cs, w, b)


def _ada_bwd(cs, w, dm):
    n = w.shape[1]

    def body(c_ref, w_ref, dm_ref, gw_ref, dc_ref):
        cv = c_ref[...]
        s = (cv * _sigmoid(cv)).astype(BF16)
        gw_ref[...] = _dot_tn(s, dm_ref[...].astype(BF16))
        dc_ref[...] = _dot_nt(dm_ref[32:40, :].astype(BF16), w_ref[...].astype(BF16))

    return pl.pallas_call(
        body, name="ada_bwd",
        out_shape=[jax.ShapeDtypeStruct((D, n), F32), jax.ShapeDtypeStruct((8, D), F32)],
        compiler_params=pltpu.CompilerParams(vmem_limit_bytes=VMEM_LIMIT))(cs, w, dm)


def _adamw(w, g, m, v, name):
    r, c = w.shape
    rb = r
    if r % 8 == 0 and r * c * 4 > (1 << 20):
        rb = 8
        for cand in range(8, r, 8):
            if r % cand == 0 and cand * c * 4 <= (1 << 20):
                rb = cand

    def body(w_ref, g_ref, m_ref, v_ref, go_ref, d_ref, nm_ref, nv_ref):
        gv = g_ref[...]
        go_ref[...] = gv
        mn = ADAM_B1 * m_ref[...] + (1.0 - ADAM_B1) * gv
        vn = ADAM_B2 * v_ref[...] + (1.0 - ADAM_B2) * (gv * gv)
        m_hat = mn / (1.0 - ADAM_B1 ** ADAM_STEP)
        v_hat = vn / (1.0 - ADAM_B2 ** ADAM_STEP)
        d_ref[...] = -ADAM_LR * (m_hat / (jnp.sqrt(v_hat) + ADAM_EPS) + ADAM_WD * w_ref[...])
        nm_ref[...] = mn
        nv_ref[...] = vn

    spec = pl.BlockSpec((rb, c), lambda i: (i, 0))
    shp = jax.ShapeDtypeStruct((r, c), F32)
    return pl.pallas_call(body, name=name, grid=(r // rb,), in_specs=[spec] * 4, out_specs=[spec] * 4,
                          out_shape=[shp] * 4, compiler_params=_cp(1))(w, g, m, v)


BIG = ("ffn1_w_in", "ffn1_w_out", "w_mix_in", "w_mix_out", "ffn2_w_in", "ffn2_w_out")


def _half_rows(w, c):
    half = w.shape[0] // 2
    return lax.dynamic_slice_in_dim(w, c * half, half, axis=0)


def _lower_bounds(logits):
    return jnp.cumsum(jax.nn.softmax(logits.astype(F32), axis=1), axis=1)[:, 0]


def kernel(x, c, ctx, c_ctx, w_ada, b_ada, ln_gain, ln_bias, ffn1_w_in, ffn1_w_out, w_mix_in, gla_a2_fwd, gla_a2_bwd, gla_a_bias_fwd, gla_a_bias_bwd, hgrn_lb_logits, gla_norm_gain, hgrn_norm_gain, w_mix_out, ffn2_w_in, ffn2_w_out, loss_target, m_c_ctx, m_w_ada, m_b_ada, m_ln_gain, m_ln_bias, m_ffn1_w_in, m_ffn1_w_out, m_w_mix_in, m_gla_a2_fwd, m_gla_a2_bwd, m_gla_a_bias_fwd, m_gla_a_bias_bwd, m_hgrn_lb_logits, m_gla_norm_gain, m_hgrn_norm_gain, m_w_mix_out, m_ffn2_w_in, m_ffn2_w_out, v_c_ctx, v_w_ada, v_b_ada, v_ln_gain, v_ln_bias, v_ffn1_w_in, v_ffn1_w_out, v_w_mix_in, v_gla_a2_fwd, v_gla_a2_bwd, v_gla_a_bias_fwd, v_gla_a_bias_bwd, v_hgrn_lb_logits, v_gla_norm_gain, v_hgrn_norm_gain, v_w_mix_out, v_ffn2_w_in, v_ffn2_w_out):
    xi, yi, ci = lax.axis_index("x"), lax.axis_index("y"), lax.axis_index("c")
    chip = 2 * xi + yi
    dev = 2 * chip + ci
    B = x.shape[0]
    weights = dict(ffn1_w_in=ffn1_w_in[0], ffn1_w_out=ffn1_w_out[0], w_mix_in=w_mix_in[0], w_mix_out=w_mix_out[0],
                   ffn2_w_in=ffn2_w_in[0], ffn2_w_out=ffn2_w_out[0])

    mine = jnp.concatenate([c.reshape(-1), ln_gain.reshape(-1), ln_bias.reshape(-1), gla_a2_fwd.reshape(-1),
                            gla_a2_bwd.reshape(-1), hgrn_lb_logits.reshape(-1)])
    g1 = _gather_flat(mine, "gather_cond")
    nc = B * D
    c_all = g1[:, :nc].reshape(8 * B, D)
    per_chip = g1[0::2, nc:]
    o = 0

    def take(shape, axis):
        nonlocal o
        n = int(np.prod(shape))
        parts = per_chip[:, o:o + n].reshape((4,) + shape)
        o += n
        return jnp.concatenate([parts[j] for j in range(4)], axis=axis)

    ln_gain_f = take((3, 256), 1)
    ln_bias_f = take((3, 256), 1)
    a2f_f = take((16, 64), 1)
    a2b_f = take((16, 64), 1)
    lbl_f = take((2, 2, 128), 2)
    lb, lb_vjp = jax.vjp(_lower_bounds, lbl_f)

    cs = jnp.concatenate([c_all, c_ctx.reshape(1, D), jnp.zeros((ADA_ROWS - 8 * B - 1, D), F32)], axis=0)
    ncol = w_ada.shape[2]
    b_cols = lax.dynamic_slice_in_dim(b_ada, chip * ncol, ncol, axis=1)
    m_cols = _ada_fwd(cs, w_ada[0], b_cols)
    g2 = _small_allgather(m_cols, "gather_mod")[0::2]
    m_all = jnp.concatenate([g2[j] for j in range(4)], axis=1)
    m_lat = lax.dynamic_slice_in_dim(m_all, dev * B, B, axis=0).reshape(B, 1, N_MOD, D)
    m_ctx = jnp.broadcast_to(m_all[8 * B].reshape(1, 1, N_MOD, D), (B, 1, N_MOD, D))

    groups = dict(ffn2=("ffn2_w_in", "ffn2_w_out"), mix=("w_mix_in", "w_mix_out"), ffn1=("ffn1_w_in", "ffn1_w_out"))
    shards = dict(weights, w_mix_in=jnp.pad(weights["w_mix_in"], ((0, 0), (0, MIX_NP - MIX_N))))
    blks = {k: _half_rows(shards[k], ci).astype(BF16) for k in BIG}
    gathering = {}
    token = m_all
    for group in ("ffn1", "mix", "ffn2"):
        sems, x_thru, l_thru, token = _split_start(_gather_copies, 4, [blks[k] for k in groups[group]], 8, token,
                                                   "weight_gather_start_" + group)
        gathering[group] = (sems, x_thru, l_thru)
    mvec = jnp.concatenate([m_ctx, m_lat], axis=1) + token[0, 0]

    def weights_for(group, after):
        names = groups[group]
        _, got = _split_wait(_gather_copies, 4, *gathering[group], after, "weight_gather_wait_" + group)
        w_in, w_out = _gather_forward(got, "weight_gather_forward_" + group)
        if group == "mix":
            return _mix_in_to_padded(w_in.reshape(4, D, MIX_NP)), w_out.reshape(-1, D)
        return w_in.reshape((4,) + shards[names[0]].shape), w_out.reshape(-1, D)

    cvec = ci.reshape(1).astype(jnp.int32)
    cjvec = jnp.stack([ci, chip]).astype(jnp.int32)
    in_flight = {}

    def on_grads(group, gs):
        names = groups[group]
        if group == "mix":
            gs = (_mix_in_from_padded(gs[0]), gs[1])
        g8s = [g.reshape((8, shards[k].shape[0] // 2, shards[k].shape[1])) for k, g in zip(names, gs)]
        sems, g_thru, l_thru, token = _split_start(_pair_copies, 4, g8s, 4, None, "grad_pair_start_" + group)
        in_flight[group] = (sems, g_thru, l_thru)
        return token

    def on_sent(group, after):
        g8s, r4s = _split_wait(_pair_copies, 4, *in_flight[group], after, "grad_pair_wait_" + group)
        h4s = [_rs_add_pair(g, r, cvec, "grad_pair_add_" + k) for k, g, r in zip(groups[group], g8s, r4s)]
        sems, h_thru, l_thru, token = _split_start(_chip_copies, 3, h4s, 3, None, "grad_chip_start_" + group)
        in_flight[group] = (g8s, r4s, sems, h_thru, l_thru)
        return token

    loss_l, grad_x, dm, small = _local_step(
        x, ctx, loss_target, mvec, weights_for, ln_gain_f, ln_bias_f, a2f_f, a2b_f,
        gla_a_bias_fwd, gla_a_bias_bwd, lb, gla_norm_gain, hgrn_norm_gain, on_grads, on_sent)
    loss = lax.psum(loss_l, ("x", "y", "c"))

    dm_lat = dm[:, 1].reshape(B, N_MOD * D)
    dm_ctx = jnp.sum(dm[:, 0], axis=0).reshape(N_MOD * D)
    keys = ("ln_gain", "ln_bias", "a2f", "a2b", "abf", "abb", "lb", "gng", "gnh")
    flat = jnp.concatenate([dm_lat.reshape(-1), dm_ctx] + [small[k].reshape(-1) for k in keys])
    nflat = flat.shape[0]
    npad = -(-nflat // 1024) * 1024
    flat2 = jnp.pad(flat, (0, npad - nflat)).reshape(8, npad // 8)
    s_sems, s_src, s_land, tok = _split_start(_all_copies, 7, [flat2], 8, None, "small_grads_start")

    fin = {}
    for group in ("ffn2", "mix", "ffn1"):
        g8s, r4s, sems, h_thru, l_thru = in_flight[group]
        _, r3s = _split_wait(_chip_copies, 3, sems, h_thru, l_thru, tok, "grad_chip_wait_" + group)
        for k, g, r4, r3 in zip(groups[group], g8s, r4s, r3s):
            fin[k] = _rs_add_chips(g, r4, r3, cjvec, "grad_chip_add_" + k)
    p_sems, p_src, p_land, tok = _split_start(_share_copies, 1, [fin[k] for k in BIG], 1, None, "grad_pair_share_start")

    _, (land,) = _split_wait(_all_copies, 7, s_sems, s_src, s_land, tok, "small_grads_wait")
    g3 = lax.dynamic_update_index_in_dim(land, flat2, dev, 0).reshape(8, npad)[:, :nflat]
    nlat = B * N_MOD * D
    dm_all = g3[:, :nlat].reshape(8 * B, N_MOD * D)
    tot = _sum8(g3[:, nlat:])[0]
    dmc_tot = tot[:N_MOD * D]
    o = N_MOD * D
    sg = {}
    for k in keys:
        n = int(np.prod(small[k].shape))
        sg[k] = tot[o:o + n].reshape(small[k].shape)
        o += n
    dm_rows = jnp.concatenate([dm_all, dmc_tot.reshape(1, -1), jnp.zeros((ADA_ROWS - 8 * B - 1, N_MOD * D), F32)], axis=0)
    g_b_ada = (jnp.sum(dm_all, axis=0) + dmc_tot).reshape(1, N_MOD * D)
    g_w_ada, dcc = _ada_bwd(cs, w_ada[0], lax.dynamic_slice_in_dim(dm_rows, chip * ncol, ncol, axis=1))
    g4 = _gather_flat(dcc[0], "gather_cctx")
    dsilu = ((g4[0] + g4[2]) + g4[4]) + g4[6]
    sc = _sigmoid(c_ctx)
    g_c_ctx = dsilu * (sc * (1.0 + c_ctx * (1.0 - sc)))
    (g_lbl,) = lb_vjp(sg["lb"])

    def cols(a, n, axis):
        return lax.dynamic_slice_in_dim(a, chip * n, n, axis=axis)

    shared, _ = _split_wait(_share_copies, 1, p_sems, p_src, p_land, g_w_ada, "grad_pair_share_wait")
    gsh = {k: both.reshape(shards[k].shape)[:, :weights[k].shape[1]] for k, both in zip(BIG, shared)}

    grads = dict(
        c_ctx=g_c_ctx, w_ada=g_w_ada[None], b_ada=g_b_ada, ln_gain=cols(sg["ln_gain"], 256, 1)[None],
        ln_bias=cols(sg["ln_bias"], 256, 1)[None], ffn1_w_in=gsh["ffn1_w_in"][None], ffn1_w_out=gsh["ffn1_w_out"][None],
        w_mix_in=gsh["w_mix_in"][None], gla_a2_fwd=cols(sg["a2f"], 64, 1)[None], gla_a2_bwd=cols(sg["a2b"], 64, 1)[None],
        gla_a_bias_fwd=sg["abf"], gla_a_bias_bwd=sg["abb"], hgrn_lb_logits=cols(g_lbl, 128, 2),
        gla_norm_gain=sg["gng"].reshape(1, HD), hgrn_norm_gain=sg["gnh"].reshape(1, HD),
        w_mix_out=gsh["w_mix_out"][None], ffn2_w_in=gsh["ffn2_w_in"][None], ffn2_w_out=gsh["ffn2_w_out"][None])
    params = dict(
        c_ctx=(c_ctx, m_c_ctx, v_c_ctx), w_ada=(w_ada, m_w_ada, v_w_ada), b_ada=(b_ada, m_b_ada, v_b_ada),
        ln_gain=(ln_gain, m_ln_gain, v_ln_gain), ln_bias=(ln_bias, m_ln_bias, v_ln_bias),
        ffn1_w_in=(ffn1_w_in, m_ffn1_w_in, v_ffn1_w_in), ffn1_w_out=(ffn1_w_out, m_ffn1_w_out, v_ffn1_w_out),
        w_mix_in=(w_mix_in, m_w_mix_in, v_w_mix_in), gla_a2_fwd=(gla_a2_fwd, m_gla_a2_fwd, v_gla_a2_fwd),
        gla_a2_bwd=(gla_a2_bwd, m_gla_a2_bwd, v_gla_a2_bwd),
        gla_a_bias_fwd=(gla_a_bias_fwd, m_gla_a_bias_fwd, v_gla_a_bias_fwd),
        gla_a_bias_bwd=(gla_a_bias_bwd, m_gla_a_bias_bwd, v_gla_a_bias_bwd),
        hgrn_lb_logits=(hgrn_lb_logits, m_hgrn_lb_logits, v_hgrn_lb_logits),
        gla_norm_gain=(gla_norm_gain, m_gla_norm_gain, v_gla_norm_gain),
        hgrn_norm_gain=(hgrn_norm_gain, m_hgrn_norm_gain, v_hgrn_norm_gain),
        w_mix_out=(w_mix_out, m_w_mix_out, v_w_mix_out), ffn2_w_in=(ffn2_w_in, m_ffn2_w_in, v_ffn2_w_in),
        ffn2_w_out=(ffn2_w_out, m_ffn2_w_out, v_ffn2_w_out))
    order = list(params.keys())
    big_names = ("w_ada",) + BIG
    upd = {}
    for k in big_names:
        w_, m_, v_ = params[k]
        s2 = w_.shape[-2:]
        g_, d_, nm_, nv_ = _adamw(w_.reshape(s2), grads[k].reshape(s2), m_.reshape(s2), v_.reshape(s2), "adamw_" + k)
        grads[k] = g_
        upd[k] = (d_.reshape(w_.shape), nm_.reshape(w_.shape), nv_.reshape(w_.shape))
    small_names = [k for k in order if k not in big_names]
    sizes = [int(np.prod(params[k][0].shape)) for k in small_names]
    tot_n = sum(sizes)
    npad = -(-tot_n // 1024) * 1024

    def packed(get):
        flat_ = jnp.concatenate([get(k).reshape(-1) for k in small_names])
        return jnp.pad(flat_, (0, npad - tot_n)).reshape(8, npad // 8)

    _, d_s, nm_s, nv_s = _adamw(packed(lambda k: params[k][0]), packed(lambda k: grads[k]),
                                packed(lambda k: params[k][1]), packed(lambda k: params[k][2]), "adamw_small")
    o = 0
    for k, n in zip(small_names, sizes):
        shp = params[k][0].shape
        upd[k] = tuple(a.reshape(-1)[o:o + n].reshape(shp) for a in (d_s, nm_s, nv_s))
        o += n

    return (loss, grad_x, *[grads[k].reshape(params[k][0].shape) for k in order], *[upd[k][0] for k in order],
            *[upd[k][1] for k in order], *[upd[k][2] for k in order])
```

```python
import functools

import numpy as np
import jax
import jax.numpy as jnp
from jax import lax
from jax.experimental import pallas as pl
from jax.experimental.pallas import tpu as pltpu

F32 = jnp.float32
BF16 = jnp.bfloat16
MESH = pl.DeviceIdType.MESH

D = 1024
DFF = 2816
TM = 256
CH = 32
NCB = TM // CH
SB = 128
CSB = SB // CH
NSB = TM // SB
HP = 8
HPB = 8
HD = 128
NH = 8
LN_EPS = 1e-5
NORM_EPS = 1e-6
ALPHA = 2.0 ** 0.25
GATE_NORM = 16.0
GLA_DK = 64
N_MOD = 9
VMEM_LIMIT = 52 * 1024 * 1024

MIXP = 5120
GG, RG, GQ, GK, GV, RQ, RFF, RFB, RI, LR = 0, 512, 1024, 1536, 2048, 2560, 3072, 3584, 4096, 4608
IN_SPLITS = (256, 256, 512, 512, 16, 16, 512, 512, 512, 512, 512)

ADAM_LR, ADAM_B1, ADAM_B2, ADAM_EPS, ADAM_WD, ADAM_STEP = 0.001, 0.9, 0.999, 1e-08, 0.01, 10


def _cp(n_axes):
    return pltpu.CompilerParams(dimension_semantics=("arbitrary",) * n_axes, vmem_limit_bytes=VMEM_LIMIT)


def _rowmap(stride, off):
    return lambda b, i: (b * stride + off + i, 0)


def _mmap(comb):
    if comb:
        return lambda b, i: (b, jnp.minimum(i, 1), 0, 0)
    return lambda b, i: (b, 1, 0, 0)


def _ln(x):
    mu = jnp.mean(x, axis=-1, keepdims=True)
    xc = x - mu
    var = jnp.mean(xc * xc, axis=-1, keepdims=True)
    r = lax.rsqrt(var + LN_EPS)
    return xc * r, r


def _ln_bwd(dxh, xh, r):
    return r * (dxh - jnp.mean(dxh, axis=-1, keepdims=True) - xh * jnp.mean(dxh * xh, axis=-1, keepdims=True))


def _sigmoid(x):
    return 1.0 / (1.0 + jnp.exp(-x))


def _rsum(x):
    return jnp.sum(x, axis=0, keepdims=True)


def _dot(a, b):
    return jnp.dot(a, b, preferred_element_type=F32)


def _dot_nt(a, b):
    return lax.dot_general(a, b, (((1,), (1,)), ((), ())), preferred_element_type=F32)


def _dot_tn(a, b):
    return lax.dot_general(a, b, (((0,), (0,)), ((), ())), preferred_element_type=F32)


def _modulate(xv, m_ref, sub):
    xh, _ = _ln(xv)
    sh = m_ref[0, 0, 3 * sub:3 * sub + 1, :]
    sc = m_ref[0, 0, 3 * sub + 1:3 * sub + 2, :]
    return (xh * (1.0 + sh) + sc).astype(BF16)


def _embed_lnmod(x, ctx, pos, mvec):
    B, T, _ = x.shape
    nt = 1 + T // TM

    def body(x_ref, c_ref, p_ref, m_ref, o_ref, h_ref):
        i = pl.program_id(1)

        @pl.when(i == 0)
        def _():
            o_ref[...] = c_ref[0]

        @pl.when(i > 0)
        def _():
            o_ref[...] = x_ref[0] + p_ref[...]

        h_ref[...] = _modulate(o_ref[...], m_ref, 0)

    rows = pl.BlockSpec((TM, D), lambda b, i: (b * nt + i, 0))
    return pl.pallas_call(
        body, name="embed_lnmod0", grid=(B, nt),
        in_specs=[pl.BlockSpec((1, TM, D), lambda b, i: (b, jnp.maximum(i - 1, 0), 0)),
                  pl.BlockSpec((1, TM, D), lambda b, i: (b, 0, 0)),
                  pl.BlockSpec((TM, D), lambda b, i: (jnp.maximum(i - 1, 0), 0)),
                  pl.BlockSpec((1, 1, N_MOD, D), _mmap(True))],
        out_specs=[rows, rows],
        out_shape=[jax.ShapeDtypeStruct((B * nt * TM, D), F32), jax.ShapeDtypeStruct((B * nt * TM, D), BF16)],
        compiler_params=_cp(2))(x, ctx, pos, mvec)


def _lnmod0_bwd(dh, X, mvec, dres, x_shape, B, nt):
    def body(dh_ref, x_ref, m_ref, dr_ref, dx_ref, dm_ref):
        i = pl.program_id(1)
        xh, r = _ln(x_ref[...])
        sh = m_ref[0, 0, 0:1, :]
        dhv = dh_ref[...].astype(F32)

        @pl.when((i == 0) | (i == 1))
        def _():
            dm_ref[...] = jnp.zeros_like(dm_ref)

        dm_ref[0, 0, 0:1, :] += _rsum(dhv * xh)
        dm_ref[0, 0, 1:2, :] += _rsum(dhv)

        @pl.when(i > 0)
        def _():
            dx_ref[0] = _ln_bwd(dhv * (1.0 + sh), xh, r) + dr_ref[...]

    rows = pl.BlockSpec((TM, D), _rowmap(nt, 0))
    return pl.pallas_call(
        body, name="lnmod0_bwd", grid=(B, nt),
        in_specs=[rows, rows, pl.BlockSpec((1, 1, N_MOD, D), _mmap(True)), rows],
        out_specs=[pl.BlockSpec((1, TM, D), lambda b, i: (b, jnp.maximum(i - 1, 0), 0)),
                   pl.BlockSpec((1, 1, 2, D), _mmap(True))],
        out_shape=[jax.ShapeDtypeStruct(x_shape, F32), jax.ShapeDtypeStruct((B, 2, 2, D), F32)],
        compiler_params=_cp(2))(dh, X, mvec, dres)


def _resid_fwd(x_ref, y_ref, m_ref, gb_ref, sub, w):
    wg = w * m_ref[0, 0, 3 * sub + 2:3 * sub + 3, :]
    y = y_ref[...].astype(F32)
    zh, r = _ln(ALPHA * x_ref[...] + wg * y)
    return y, wg, zh, r


def _resid_grads(do, y, wg, zh, r, w, gb_ref, comb, dx_ref, dy_ref, dg_ref, dgb_ref):
    b_, i = pl.program_id(0), pl.program_id(1)
    dz = _ln_bwd(do * gb_ref[0:1, :], zh, r)
    dx_ref[...] = ALPHA * dz
    dy_ref[...] = (wg * dz).astype(BF16)

    @pl.when((b_ == 0) & (i == 0))
    def _():
        dgb_ref[...] = jnp.zeros_like(dgb_ref)

    dgb_ref[0:1, :] += _rsum(do * zh)
    dgb_ref[1:2, :] += _rsum(do)

    init = (i == 0) | (i == 1) if comb else (i == 0)

    @pl.when(init)
    def _():
        dg_ref[...] = jnp.zeros_like(dg_ref)

    dg_ref[0, 0] += w * _rsum(dz * y)


def _resid_lnmod(X, lx, Y, mvec, comb, sub, w, gb, B, nt, name):
    def body(x_ref, y_ref, m_ref, gb_ref, o_ref, h_ref):
        _, _, zh, _ = _resid_fwd(x_ref, y_ref, m_ref, gb_ref, sub, w)
        xn = zh * gb_ref[0:1, :] + gb_ref[1:2, :]
        o_ref[...] = xn
        h_ref[...] = _modulate(xn, m_ref, sub + 1)

    rows = pl.BlockSpec((TM, D), _rowmap(nt, 0))
    return pl.pallas_call(
        body, name=name, grid=(B, nt),
        in_specs=[pl.BlockSpec((TM, D), _rowmap(*lx)), rows,
                  pl.BlockSpec((1, 1, N_MOD, D), _mmap(comb)), pl.BlockSpec((2, D), lambda b, i: (0, 0))],
        out_specs=[rows, rows],
        out_shape=[jax.ShapeDtypeStruct((B * nt * TM, D), F32), jax.ShapeDtypeStruct((B * nt * TM, D), BF16)],
        compiler_params=_cp(2))(X, Y, mvec, gb)


def _resid_out_shapes(B, nt, comb):
    rows = pl.BlockSpec((TM, D), _rowmap(nt, 0))
    specs = [rows, rows, pl.BlockSpec((1, 1, 1, D), _mmap(comb)), pl.BlockSpec((2, D), lambda b, i: (0, 0))]
    shapes = [jax.ShapeDtypeStruct((B * nt * TM, D), F32), jax.ShapeDtypeStruct((B * nt * TM, D), BF16),
              jax.ShapeDtypeStruct((B, 2, 1, D), F32), jax.ShapeDtypeStruct((2, D), F32)]
    return specs, shapes


def _tail(X, Y, mvec, gb, tgt, sub, w, B, nt):
    def body(x_ref, y_ref, m_ref, gb_ref, t_ref, dx_ref, dy_ref, dg_ref, dgb_ref, l_ref):
        y, wg, zh, r = _resid_fwd(x_ref, y_ref, m_ref, gb_ref, sub, w)
        e = (zh * gb_ref[0:1, :] + gb_ref[1:2, :]) - t_ref[0]

        @pl.when((pl.program_id(0) == 0) & (pl.program_id(1) == 0))
        def _():
            l_ref[...] = jnp.zeros_like(l_ref)

        l_ref[...] += _rsum(e * e)
        _resid_grads(e * (1.0 / D), y, wg, zh, r, w, gb_ref, False, dx_ref, dy_ref, dg_ref, dgb_ref)

    rows = pl.BlockSpec((TM, D), _rowmap(nt, 0))
    specs, shapes = _resid_out_shapes(B, nt, False)
    return pl.pallas_call(
        body, name="resid2_loss_bwd", grid=(B, nt),
        in_specs=[rows, rows, pl.BlockSpec((1, 1, N_MOD, D), _mmap(False)), pl.BlockSpec((2, D), lambda b, i: (0, 0)),
                  pl.BlockSpec((1, TM, D), lambda b, i: (b, i, 0))],
        out_specs=specs + [pl.BlockSpec((1, D), lambda b, i: (0, 0))],
        out_shape=shapes + [jax.ShapeDtypeStruct((1, D), F32)],
        compiler_params=_cp(2))(X, Y, mvec, gb, tgt)


def _lnmod_resid_bwd(dh, Xi, lxi, dres, Xp, lxp, Yp, mvec, comb, sub, w, gb, B, nt, name):
    ntl = nt - 1 if comb else nt

    def body(dh_ref, xi_ref, dr_ref, xp_ref, yp_ref, m_ref, gb_ref, dx_ref, dy_ref, dg_ref, dgb_ref, dm_ref):
        i = pl.program_id(1)
        xh, r = _ln(xi_ref[...])
        sh = m_ref[0, 0, 3 * sub:3 * sub + 1, :]
        dhv = dh_ref[...].astype(F32)
        dr = dr_ref[...]
        if comb:
            dr = jnp.where(i > 0, dr, 0.0)
        do = _ln_bwd(dhv * (1.0 + sh), xh, r) + dr

        init = (i == 0) | (i == 1) if comb else (i == 0)

        @pl.when(init)
        def _():
            dm_ref[...] = jnp.zeros_like(dm_ref)

        dm_ref[0, 0, 0:1, :] += _rsum(dhv * xh)
        dm_ref[0, 0, 1:2, :] += _rsum(dhv)
        y, wg, zh, r2 = _resid_fwd(xp_ref, yp_ref, m_ref, gb_ref, sub - 1, w)
        _resid_grads(do, y, wg, zh, r2, w, gb_ref, comb, dx_ref, dy_ref, dg_ref, dgb_ref)

    rows = pl.BlockSpec((TM, D), _rowmap(nt, 0))
    if comb:
        dres_spec = pl.BlockSpec((TM, D), lambda b, i: (b * ntl + jnp.maximum(i - 1, 0), 0))
    else:
        dres_spec = rows
    specs, shapes = _resid_out_shapes(B, nt, comb)
    return pl.pallas_call(
        body, name=name, grid=(B, nt),
        in_specs=[rows, pl.BlockSpec((TM, D), _rowmap(*lxi)), dres_spec, pl.BlockSpec((TM, D), _rowmap(*lxp)), rows,
                  pl.BlockSpec((1, 1, N_MOD, D), _mmap(comb)), pl.BlockSpec((2, D), lambda b, i: (0, 0))],
        out_specs=specs + [pl.BlockSpec((1, 1, 2, D), _mmap(comb))],
        out_shape=shapes + [jax.ShapeDtypeStruct((B, 2, 2, D), F32)],
        compiler_params=_cp(2))(dh, Xi, dres, Xp, Yp, mvec, gb)


def _ffn_out_dx(dy, W, u, name, dep=None):
    M = dy.shape[0]
    half = DFF // 2
    deps = [] if dep is None else [dep]

    def body(dy_ref, w_ref, u_ref, *rest):
        du_ref = rest[-1]
        dyv = dy_ref[...]
        for j in range(2):
            lo, hi = j * half, (j + 1) * half
            da = _dot_nt(dyv, w_ref[lo:hi, :])
            g = u_ref[:, lo:hi].astype(F32)
            up = u_ref[:, DFF + lo:DFF + hi].astype(F32)
            s = _sigmoid(g)
            du_ref[:, lo:hi] = (da * up * (s * (1.0 + g * (1.0 - s)))).astype(BF16)
            du_ref[:, DFF + lo:DFF + hi] = (da * (g * s)).astype(BF16)

    return pl.pallas_call(
        body, name=name, grid=(M // TM,),
        in_specs=[pl.BlockSpec((TM, D), lambda i: (i, 0)), _wspec(W, 1), pl.BlockSpec((TM, 2 * DFF), lambda i: (i, 0))]
        + [_wspec(d, 1) for d in deps],
        out_specs=pl.BlockSpec((TM, 2 * DFF), lambda i: (i, 0)),
        out_shape=jax.ShapeDtypeStruct((M, 2 * DFF), BF16), compiler_params=_cp(1))(dy, W, u, *deps)


def _wspec(W, nidx):
    zeros = (0,) * W.ndim
    if nidx == 1:
        return pl.BlockSpec(W.shape, lambda i: zeros)
    return pl.BlockSpec(W.shape, lambda b, i: zeros)


def _mm_nn(A, la, W, B, nt, out_dtype, name):
    K, N = W.shape

    def body(a_ref, w_ref, o_ref):
        o_ref[...] = _dot(a_ref[...], w_ref[...]).astype(out_dtype)

    return pl.pallas_call(
        body, name=name, grid=(B, nt),
        in_specs=[pl.BlockSpec((TM, K), _rowmap(*la)), _wspec(W, 2)],
        out_specs=pl.BlockSpec((TM, N), _rowmap(nt, 0)),
        out_shape=jax.ShapeDtypeStruct((B * nt * TM, N), out_dtype), compiler_params=_cp(2))(A, W)


def _ffn_in(A, la, W3, B, nt, name):
    K, n = W3.shape[1:]

    def body(a_ref, w_ref, u_ref, s_ref):
        a = a_ref[...]
        for j in range(2):
            g = _dot(a, w_ref[j])
            up = _dot(a, w_ref[j + 2])
            u_ref[:, j * n:(j + 1) * n] = g.astype(BF16)
            u_ref[:, (j + 2) * n:(j + 3) * n] = up.astype(BF16)
            s_ref[:, j * n:(j + 1) * n] = (g * _sigmoid(g) * up).astype(BF16)

    rows = B * nt * TM
    return pl.pallas_call(
        body, name=name, grid=(B, nt),
        in_specs=[pl.BlockSpec((TM, K), _rowmap(*la)), _wspec(W3, 2)],
        out_specs=[pl.BlockSpec((TM, 4 * n), _rowmap(nt, 0)), pl.BlockSpec((TM, 2 * n), _rowmap(nt, 0))],
        out_shape=[jax.ShapeDtypeStruct((rows, 4 * n), BF16), jax.ShapeDtypeStruct((rows, 2 * n), BF16)],
        compiler_params=_cp(2))(A, W3)


def _mm_nt(A, W, name, dep=None, out_dtype=F32):
    M, N = A.shape
    K = W.shape[-2]

    def body(a_ref, w_ref, *rest):
        o_ref = rest[-1]
        if W.ndim == 3:
            n = W.shape[-1]
            acc = _dot_nt(a_ref[:, 0:n], w_ref[0])
            for j in range(1, 4):
                acc = acc + _dot_nt(a_ref[:, j * n:(j + 1) * n], w_ref[j])
            o_ref[...] = acc.astype(out_dtype)
        else:
            o_ref[...] = _dot_nt(a_ref[...], w_ref[...]).astype(out_dtype)

    deps = [] if dep is None else [dep]
    return pl.pallas_call(
        body, name=name, grid=(M // TM,),
        in_specs=[pl.BlockSpec((TM, N), lambda i: (i, 0)), _wspec(W, 1)] + [_wspec(d, 1) for d in deps],
        out_specs=pl.BlockSpec((TM, K), lambda i: (i, 0)),
        out_shape=jax.ShapeDtypeStruct((M, K), out_dtype), compiler_params=_cp(1))(A, W, *deps)


def _mm_tn(A, G, name, tn=512, shards=None):
    M, K = A.shape
    N = G.shape[1]
    tk = next(t for t in ((2048, 1536, 1024, 512) if K <= D else (1024, 512)) if M % t == 0)
    if shards:
        tn = N // shards

    def body(a_ref, g_ref, o_ref):
        @pl.when(pl.program_id(1) == 0)
        def _():
            o_ref[...] = jnp.zeros_like(o_ref)

        upd = _dot_tn(a_ref[...], g_ref[...])
        if shards:
            o_ref[0] += upd
        else:
            o_ref[...] += upd

    if shards:
        out_spec = pl.BlockSpec((1, K, tn), lambda n, k: (n, 0, 0))
        out_shape = jax.ShapeDtypeStruct((shards, K, tn), F32)
    else:
        out_spec = pl.BlockSpec((K, tn), lambda n, k: (0, n))
        out_shape = jax.ShapeDtypeStruct((K, N), F32)
    return pl.pallas_call(
        body, name=name, grid=(N // tn, M // tk),
        in_specs=[pl.BlockSpec((tk, K), lambda n, k: (k, 0)), pl.BlockSpec((tk, tn), lambda n, k: (k, n))],
        out_specs=out_spec, out_shape=out_shape, compiler_params=_cp(2))(A, G)


def _logsig(z):
    return jnp.minimum(z, 0.0) - jnp.log(1.0 + jnp.exp(-jnp.abs(z)))


ZW = 2688
ZRQ, ZRFF, ZRFB, ZLR = 1024, 1536, 2048, 2560


def _mix_in_features(H, W, a2p, biasp, lbp, B, nt):
    K = W.shape[0]

    def body(h_ref, w_ref, a2_ref, bias_ref, lb_ref, q_ref, k_ref, v_ref, g_ref, z_ref):
        hv = h_ref[...]
        proj = lambda lo, hi: _dot(hv, w_ref[:, lo:hi])
        z_ref[:, 0:2 * 4 * HD] = proj(GG, GG + 2 * 4 * HD)
        lrf = proj(LR, LR + HD)
        z_ref[:, ZLR:ZLR + HD] = lrf
        lr = lrf.astype(BF16)
        lane = lax.broadcasted_iota(jnp.int32, (1, 4 * HD), 1)
        keep = (lane & (HD - 1)) < GLA_DK
        for d in range(2):
            z = _dot(lr, a2_ref[d]) + bias_ref[d:d + 1, :]
            gl = jnp.where(keep, _logsig(z) * (1.0 / GATE_NORM), 0.0)
            for h in range(4):
                g_ref[d, 0, h] = gl[:, h * HD:(h + 1) * HD]
        gq, gk, gv = proj(GQ, GQ + 4 * HD), proj(GK, GK + 4 * HD), proj(GV, GV + 4 * HD)
        for h in range(4):
            sl = slice(h * HD, (h + 1) * HD)
            q_ref[0, h] = gq[:, sl] * (GLA_DK ** -0.5)
            k_ref[0, 0, h] = gk[:, sl]
            k_ref[1, 0, h] = gk[:, sl]
            v_ref[0, h] = gv[:, sl].astype(BF16)
        rqs, ri = proj(RQ, RQ + 4 * HD), proj(RI, RI + 4 * HD)
        z_ref[:, ZRQ:ZRQ + 4 * HD] = rqs
        for h in range(4):
            sl = slice(h * HD, (h + 1) * HD)
            rq = rqs[:, sl]
            q_ref[0, 4 + h] = rq * _sigmoid(rq) * (HD ** -0.5)
            v_ref[0, 4 + h] = ri[:, sl].astype(BF16)
        for d, off, zoff in ((0, RFF, ZRFF), (1, RFB, ZRFB)):
            rf = proj(off, off + 4 * HD)
            z_ref[:, zoff:zoff + 4 * HD] = rf
            for h in range(4):
                sl = slice(h * HD, (h + 1) * HD)
                lb = lb_ref[d:d + 1, sl]
                f = lb + (1.0 - lb) * _sigmoid(rf[:, sl])
                g_ref[d, 0, 4 + h] = jnp.log(f)
                k_ref[d, 0, 4 + h] = 1.0 - f

    one = pl.BlockSpec((1, NH, TM, HD), lambda b, i: (b, 0, i, 0))
    two = pl.BlockSpec((2, 1, NH, TM, HD), lambda b, i: (0, b, 0, i, 0))
    s1 = jax.ShapeDtypeStruct((B, NH, nt * TM, HD), F32)
    s2 = jax.ShapeDtypeStruct((2, B, NH, nt * TM, HD), F32)
    return pl.pallas_call(
        body, name="mix_in_features", grid=(B, nt),
        in_specs=[pl.BlockSpec((TM, K), _rowmap(nt, 0)), _wspec(W, 2), pl.BlockSpec((2, HD, 4 * HD), lambda b, i: (0, 0, 0)),
                  pl.BlockSpec((2, 4 * HD), lambda b, i: (0, 0)), pl.BlockSpec((2, 4 * HD), lambda b, i: (0, 0))],
        out_specs=[one, two, one, two, pl.BlockSpec((TM, ZW), _rowmap(nt, 0))],
        out_shape=[s1, s2, jax.ShapeDtypeStruct(s1.shape, BF16), s2, jax.ShapeDtypeStruct((B * nt * TM, ZW), F32)],
        compiler_params=_cp(2))(H, W, a2p, biasp, lbp)


def _features_bwd(Z, a2p, biasp, lbp, dQ0, dQ1, dK0, dK1, dV0, dV1, dG0, dG1, dgates, B):
    nt = Z.shape[0] // (B * TM)

    def body(z_ref, a2_ref, bias_ref, lb_ref, dq0, dq1, dk0, dk1, dv0, dv1, dg0, dg1, dgt_ref,
             df_ref, da2_ref, dbias_ref, dlb_ref):
        b_, i = pl.program_id(0), pl.program_id(1)

        @pl.when((b_ == 0) & (i == 0))
        def _():
            da2_ref[...] = jnp.zeros_like(da2_ref)
            dbias_ref[...] = jnp.zeros_like(dbias_ref)
            dlb_ref[...] = jnp.zeros_like(dlb_ref)

        df_ref[:, 0:2 * 4 * HD] = jnp.where(i > 0, dgt_ref[...], 0.0).astype(BF16)
        df_ref[:, LR + HD:] = jnp.zeros((TM, MIXP - LR - HD), BF16)
        lr = z_ref[:, ZLR:ZLR + HD].astype(BF16)
        lane = lax.broadcasted_iota(jnp.int32, (1, 4 * HD), 1)
        keep = (lane & (HD - 1)) < GLA_DK
        dlr = jnp.zeros((TM, HD), F32)
        dgs = (dg0, dg1)
        dks = (dk0, dk1)
        rd = lambda ref, h: ref[0, h].astype(F32)
        for d in range(2):
            z = _dot(lr, a2_ref[d]) + bias_ref[d:d + 1, :]
            dgl = jnp.concatenate([rd(dgs[d], h) for h in range(4)], axis=1)
            dz = jnp.where(keep, dgl * (1.0 / GATE_NORM) * (1.0 - _sigmoid(z)), 0.0)
            dzb = dz.astype(BF16)
            dlr = dlr + _dot_nt(dzb, a2_ref[d])
            da2_ref[d] += _dot_tn(lr, dzb)
            dbias_ref[d:d + 1, :] += _rsum(dz)
        df_ref[:, LR:LR + HD] = dlr.astype(BF16)
        for h in range(4):
            df_ref[:, GQ + h * HD:GQ + (h + 1) * HD] = ((rd(dq0, h) + rd(dq1, h)) * (GLA_DK ** -0.5)).astype(BF16)
            df_ref[:, GK + h * HD:GK + (h + 1) * HD] = (rd(dk0, h) + rd(dk1, h)).astype(BF16)
            df_ref[:, GV + h * HD:GV + (h + 1) * HD] = (rd(dv0, h) + rd(dv1, h)).astype(BF16)
        for h in range(4):
            sl = slice(h * HD, (h + 1) * HD)
            rq = z_ref[:, ZRQ + h * HD:ZRQ + (h + 1) * HD]
            s = _sigmoid(rq)
            dqh = rd(dq0, 4 + h) + rd(dq1, 4 + h)
            df_ref[:, RQ + h * HD:RQ + (h + 1) * HD] = (dqh * (HD ** -0.5) * (s * (1.0 + rq * (1.0 - s)))).astype(BF16)
            df_ref[:, RI + h * HD:RI + (h + 1) * HD] = (rd(dv0, 4 + h) + rd(dv1, 4 + h)).astype(BF16)
            for d, off, zoff in ((0, RFF, ZRFF), (1, RFB, ZRFB)):
                lb = lb_ref[d:d + 1, sl]
                sg = _sigmoid(z_ref[:, zoff + h * HD:zoff + (h + 1) * HD])
                f = lb + (1.0 - lb) * sg
                dff = rd(dgs[d], 4 + h) / f - rd(dks[d], 4 + h)
                df_ref[:, off + h * HD:off + (h + 1) * HD] = (dff * (1.0 - lb) * sg * (1.0 - sg)).astype(BF16)
                dlb_ref[d:d + 1, sl] += _rsum(dff * (1.0 - sg))

    m0 = lambda b, i: (b, 0, i, 0)
    one = lambda m: pl.BlockSpec((1, NH, TM, HD), m)
    return pl.pallas_call(
        body, name="mix_features_bwd", grid=(B, nt),
        in_specs=[pl.BlockSpec((TM, ZW), _rowmap(nt, 0)), pl.BlockSpec((2, HD, 4 * HD), lambda b, i: (0, 0, 0)),
                  pl.BlockSpec((2, 4 * HD), lambda b, i: (0, 0)), pl.BlockSpec((2, 4 * HD), lambda b, i: (0, 0)),
                  one(m0), one(m0), one(m0), one(m0), one(m0), one(m0), one(m0), one(m0),
                  pl.BlockSpec((TM, D), lambda b, i: (b * (nt - 1) + jnp.maximum(i - 1, 0), 0))],
        out_specs=[pl.BlockSpec((TM, MIXP), _rowmap(nt, 0)), pl.BlockSpec((2, HD, 4 * HD), lambda b, i: (0, 0, 0)),
                   pl.BlockSpec((2, 4 * HD), lambda b, i: (0, 0)), pl.BlockSpec((2, 4 * HD), lambda b, i: (0, 0))],
        out_shape=[jax.ShapeDtypeStruct((B * nt * TM, MIXP), BF16), jax.ShapeDtypeStruct((2, HD, 4 * HD), F32),
                   jax.ShapeDtypeStruct((2, 4 * HD), F32), jax.ShapeDtypeStruct((2, 4 * HD), F32)],
        compiler_params=_cp(2))(Z, a2p, biasp, lbp, dQ0, dQ1, dK0, dK1, dV0, dV1, dG0, dG1, dgates)


def _chunk_scan(x, rin, fwd):
    acc = x
    sft = 1
    while sft < CH:
        if fwd:
            acc = acc + jnp.where(rin >= sft, pltpu.roll(acc, sft, 0), 0.0)
        else:
            acc = acc + jnp.where(rin < CH - sft, pltpu.roll(acc, TM - sft, 0), 0.0)
        sft *= 2
    return acc


def _chunk_total(x):
    t = jnp.sum(x.reshape(NCB, CH, HD), axis=1, keepdims=True)
    return jnp.broadcast_to(t, (NCB, CH, HD)).reshape(TM, HD)


def _scan_masks(rev):
    rin = lax.broadcasted_iota(jnp.int32, (TM, HD), 0) & (CH - 1)
    ri = lax.broadcasted_iota(jnp.int32, (SB, SB), 0)
    ci = lax.broadcasted_iota(jnp.int32, (SB, SB), 1)
    same = (ri >> 5) == (ci >> 5)
    lo = same & (ri >= ci)
    up = same & (ri <= ci)
    mask, maskT = (up, lo) if rev else (lo, up)
    re = lax.broadcasted_iota(jnp.int32, (SB, CSB * HD), 0) >> 5
    ce = lax.broadcasted_iota(jnp.int32, (SB, CSB * HD), 1) >> 7
    return rin, mask, maskT, re == ce


def _scan_decay(q, k, g, rin, rev):
    b = _chunk_scan(g, rin, not rev)
    xx = _chunk_total(g) - b
    eb = jnp.exp(b)
    return b, xx, eb, q * eb, k * jnp.exp(-b), k * jnp.exp(xx)


def _sub(x, s):
    return x[s * SB:(s + 1) * SB]


def _expand(xb, mexp):
    return jnp.where(mexp, jnp.concatenate([xb] * CSB, axis=1), jnp.zeros((), xb.dtype))


def _own(x, mexp):
    xm = jnp.where(mexp, x, 0.0)
    acc = xm[:, 0:HD]
    for n in range(1, CSB):
        acc = acc + xm[:, n * HD:(n + 1) * HD]
    return acc


def _stack(per_chunk, s):
    return jnp.concatenate(per_chunk[s * CSB:(s + 1) * CSB], axis=1)


def _state_pass(s0, eb, uts, rev):
    order = range(NCB - 1, -1, -1) if rev else range(NCB)
    states = [None] * NCB
    s = s0
    for n in order:
        row = n * CH if rev else n * CH + CH - 1
        states[n] = s
        s = eb[row:row + 1, :] * s + uts[n // CSB][:, (n % CSB) * HD:(n % CSB + 1) * HD]
    return states, s


def _scan_fwd(Q, K, V, G, rev, B):
    nb = Q.shape[2] // TM
    d = 1 if rev else 0
    rmap = (lambda s: jnp.where(s == 0, 0, nb - s)) if rev else (lambda s: s)

    def body(q_ref, k_ref, v_ref, g_ref, o_ref, st_ref, s_scr):
        @pl.when(pl.program_id(2) == 0)
        def _():
            s_scr[...] = jnp.zeros_like(s_scr)

        rin, mask, _, mexp = _scan_masks(rev)

        def head(p, readout):
            s0 = s_scr[p]
            st_ref[0, p, 0] = s0
            _, _, eb, qd, ki, kt = _scan_decay(q_ref[0, p], k_ref[0, 0, p], g_ref[0, 0, p], rin, rev)
            ktb, vb = kt.astype(BF16), v_ref[0, p]
            uts = [_dot_tn(_sub(vb, s), _expand(_sub(ktb, s), mexp)) for s in range(NSB)]
            states, s_new = _state_pass(s0, eb, uts, rev)
            s_scr[p] = s_new
            if not readout:
                return
            qb, kib = qd.astype(BF16), ki.astype(BF16)
            for s in range(NSB):
                a = jnp.where(mask, _dot_nt(_sub(qb, s), _sub(kib, s)), 0.0)
                o_ref[0, p, s * SB:(s + 1) * SB, :] = (
                    _dot(a.astype(BF16), _sub(vb, s))
                    + _dot_nt(_expand(_sub(qb, s), mexp), _stack(states, s).astype(BF16)))

        @pl.when(pl.program_id(2) >= 1)
        def _():
            for p in range(HP):
                head(p, True)

        @pl.when(pl.program_id(2) == 0)
        def _():
            for p in range(HP):
                head(p, False)

    one = pl.BlockSpec((1, HP, TM, HD), lambda b, h, s: (b, h, rmap(s), 0))
    two = pl.BlockSpec((1, 1, HP, TM, HD), lambda b, h, s: (d, b, h, rmap(s), 0))
    return pl.pallas_call(
        body, name="scan_fwd_rev" if rev else "scan_fwd", grid=(B, NH // HP, nb),
        in_specs=[one, two, one, two],
        out_specs=[one, pl.BlockSpec((1, HP, 1, HD, HD), lambda b, h, s: (b, h, s, 0, 0))],
        out_shape=[jax.ShapeDtypeStruct(Q.shape, F32), jax.ShapeDtypeStruct((B, NH, nb, HD, HD), F32)],
        scratch_shapes=[pltpu.VMEM((HP, HD, HD), F32)],
        compiler_params=_cp(3))(Q, K, V, G)


def _scan_bwd(Q, K, V, G, St, dO, rev, B):
    nb = Q.shape[2] // TM
    d = 1 if rev else 0
    smap = lambda t: nb - 1 - t
    rmap = (lambda t: jnp.where(smap(t) == 0, 0, nb - smap(t))) if rev else smap

    def body(q_ref, k_ref, v_ref, g_ref, st_ref, do_ref, dq_ref, dk_ref, dv_ref, dg_ref, ds_scr):
        t = pl.program_id(2)

        @pl.when(t == 0)
        def _():
            ds_scr[...] = jnp.zeros_like(ds_scr)

        is_lat = smap(t) >= 1
        rin, mask, maskT, mexp = _scan_masks(rev)
        for p in range(HPB):
            b, xx, eb, qd, ki, kt = _scan_decay(q_ref[0, p], k_ref[0, 0, p], g_ref[0, 0, p], rin, rev)
            qb, kib, ktb, vb = qd.astype(BF16), ki.astype(BF16), kt.astype(BF16), v_ref[0, p]
            dob = jnp.where(is_lat, do_ref[0, p], 0.0).astype(BF16)
            kt_exps = [_expand(_sub(ktb, s), mexp) for s in range(NSB)]
            uts = [_dot_tn(_sub(vb, s), kt_exps[s]) for s in range(NSB)]
            states, _ = _state_pass(st_ref[0, p, 0], eb, uts, rev)
            gts = [_dot_tn(_sub(dob, s), _expand(_sub(qb, s), mexp)) for s in range(NSB)]
            order = range(NCB) if rev else range(NCB - 1, -1, -1)
            dsp = [None] * NCB
            t2 = [None] * NCB
            dsc = ds_scr[p]
            for n in order:
                row = n * CH if rev else n * CH + CH - 1
                ebl = eb[row:row + 1, :]
                dsp[n] = dsc
                t2[n] = jnp.broadcast_to(ebl * _rsum(states[n] * dsc), (CH, HD))
                dsc = gts[n // CSB][:, (n % CSB) * HD:(n % CSB + 1) * HD] + ebl * dsc
            ds_scr[p] = dsc
            dqds, dkis, dkts = [], [], []
            for s in range(NSB):
                q_s, ki_s, v_s, do_s = _sub(qb, s), _sub(kib, s), _sub(vb, s), _sub(dob, s)
                dspb = _stack(dsp, s).astype(BF16)
                da = jnp.where(mask, _dot_nt(do_s, v_s), 0.0).astype(BF16)
                dat = jnp.where(maskT, _dot_nt(v_s, do_s), 0.0).astype(BF16)
                at = jnp.where(maskT, _dot_nt(ki_s, q_s), 0.0).astype(BF16)
                dqds.append(_dot(da, ki_s) + _own(_dot(do_s, _stack(states, s).astype(BF16)), mexp))
                dkis.append(_dot(dat, q_s))
                dv_ref[0, p, s * SB:(s + 1) * SB, :] = (_dot(at, do_s) + _dot_nt(kt_exps[s], dspb)).astype(BF16)
                dkts.append(_own(_dot(v_s, dspb), mexp))
            dqd, dki, dkt = (jnp.concatenate(parts, axis=0) for parts in (dqds, dkis, dkts))
            z = dkt * kt
            db = dqd * qd - dki * ki
            dq_ref[0, p] = (dqd * eb).astype(BF16)
            dk_ref[0, p] = (dki * jnp.exp(-b) + dkt * jnp.exp(xx)).astype(BF16)
            dg_ref[0, p] = (_chunk_total(db) + (db - z) + _chunk_scan(z - db, rin, not rev)
                            + jnp.concatenate(t2, axis=0)).astype(BF16)

    one = pl.BlockSpec((1, HPB, TM, HD), lambda b, h, t: (b, h, rmap(t), 0))
    two = pl.BlockSpec((1, 1, HPB, TM, HD), lambda b, h, t: (d, b, h, rmap(t), 0))
    lat = pl.BlockSpec((1, HPB, TM, HD), lambda b, h, t: (b, h, jnp.clip(rmap(t) - 1, 0, nb - 2), 0))
    shp = jax.ShapeDtypeStruct(Q.shape, BF16)
    return pl.pallas_call(
        body, name="scan_bwd_rev" if rev else "scan_bwd", grid=(B, NH // HPB, nb),
        in_specs=[one, two, one, two, pl.BlockSpec((1, HPB, 1, HD, HD), lambda b, h, t: (b, h, smap(t), 0, 0)), lat],
        out_specs=[one, one, one, one], out_shape=[shp, shp, shp, shp],
        scratch_shapes=[pltpu.VMEM((HPB, HD, HD), F32)],
        compiler_params=_cp(3))(Q, K, V, G, St, dO)


def _gnorm(O0, O1, F, gains, B, ntl):
    nt = ntl + 1

    def body(o0_ref, o1_ref, f_ref, gn_ref, m_ref):
        for h in range(NH):
            o = o0_ref[0, h] + o1_ref[0, h]
            r = lax.rsqrt(jnp.mean(o * o, axis=-1, keepdims=True) + NORM_EPS)
            gn = gn_ref[0:1, :] if h < 4 else gn_ref[1:2, :]
            gt = f_ref[:, h * HD:(h + 1) * HD]
            m_ref[:, h * HD:(h + 1) * HD] = (o * r * gn * (gt * _sigmoid(gt))).astype(BF16)

    ospec = pl.BlockSpec((1, NH, TM, HD), lambda b, i: (b, 0, i + 1, 0))
    return pl.pallas_call(
        body, name="gated_norm", grid=(B, ntl),
        in_specs=[ospec, ospec, pl.BlockSpec((TM, D), lambda b, i: (b * nt + 1 + i, 0)),
                  pl.BlockSpec((2, HD), lambda b, i: (0, 0))],
        out_specs=pl.BlockSpec((TM, D), _rowmap(ntl, 0)),
        out_shape=jax.ShapeDtypeStruct((B * ntl * TM, D), BF16), compiler_params=_cp(2))(O0, O1, F, gains)


def _gnorm_bwd(dM, O0, O1, F, gains, B, ntl):
    nt = ntl + 1

    def body(dm_ref, o0_ref, o1_ref, f_ref, gn_ref, do_ref, dgt_ref, dgn_ref):
        b_, i = pl.program_id(0), pl.program_id(1)

        @pl.when((b_ == 0) & (i == 0))
        def _():
            dgn_ref[...] = jnp.zeros_like(dgn_ref)

        for h in range(NH):
            o = o0_ref[0, h] + o1_ref[0, h]
            r = lax.rsqrt(jnp.mean(o * o, axis=-1, keepdims=True) + NORM_EPS)
            y = o * r
            gn = gn_ref[0:1, :] if h < 4 else gn_ref[1:2, :]
            gt = f_ref[:, h * HD:(h + 1) * HD]
            s = _sigmoid(gt)
            dm = dm_ref[:, h * HD:(h + 1) * HD].astype(F32)
            don = dm * (gt * s)
            dgt_ref[:, h * HD:(h + 1) * HD] = (dm * (y * gn) * (s * (1.0 + gt * (1.0 - s)))).astype(BF16)
            row = 0 if h < 4 else 1
            dgn_ref[row:row + 1, :] += _rsum(don * y)
            dy = don * gn
            do_ref[0, h] = (r * (dy - y * jnp.mean(dy * y, axis=-1, keepdims=True))).astype(BF16)

    ospec = pl.BlockSpec((1, NH, TM, HD), lambda b, i: (b, 0, i + 1, 0))
    return pl.pallas_call(
        body, name="gated_norm_bwd", grid=(B, ntl),
        in_specs=[pl.BlockSpec((TM, D), _rowmap(ntl, 0)), ospec, ospec,
                  pl.BlockSpec((TM, D), lambda b, i: (b * nt + 1 + i, 0)), pl.BlockSpec((2, HD), lambda b, i: (0, 0))],
        out_specs=[pl.BlockSpec((1, NH, TM, HD), lambda b, i: (b, 0, i, 0)), pl.BlockSpec((TM, D), _rowmap(ntl, 0)),
                   pl.BlockSpec((2, HD), lambda b, i: (0, 0))],
        out_shape=[jax.ShapeDtypeStruct((B, NH, ntl * TM, HD), BF16), jax.ShapeDtypeStruct((B * ntl * TM, D), BF16),
                   jax.ShapeDtypeStruct((2, HD), F32)],
        compiler_params=_cp(2))(dM, O0, O1, F, gains)


def _sincos_2d(rows, width, dim):
    quarter = dim // 4
    omega = 1.0 / 10000.0 ** (jnp.arange(quarter, dtype=F32) / quarter)

    def emb(n):
        a = jnp.arange(n).astype(F32)[:, None] * omega[None, :]
        return jnp.concatenate([jnp.sin(a), jnp.cos(a)], axis=-1)

    er = jnp.broadcast_to(emb(rows)[:, None, :], (rows, width, dim // 2))
    ec = jnp.broadcast_to(emb(width)[None, :, :], (rows, width, dim // 2))
    return jnp.concatenate([er, ec], axis=-1).reshape(rows * width, dim)


def _pad_heads(w):
    k = w.shape[0]
    return jnp.pad(w.reshape(k, 4, GLA_DK), ((0, 0), (0, 0), (0, HD - GLA_DK))).reshape(k, 4 * HD)


def _unpad_heads(w):
    k = w.shape[0]
    return w.reshape(k, 4, HD)[:, :, :GLA_DK].reshape(k, 4 * GLA_DK)


MIX_N = 1032
MIX_NP = 1152
_SEGS = ([(64 * h, 64, GQ + HD * h) for h in range(4)] + [(256 + 64 * h, 64, GK + HD * h) for h in range(4)]
         + [(512, 512, GV), (1024, 512, GG), (1536, 32, LR), (1568, 512, RQ), (2080, 512, RFF), (2592, 512, RFB),
            (3104, 512, RI), (3616, 512, RG)])


def _mix_in_to_padded(ps):
    k = ps.shape[1]
    parts, pos = [], 0
    for g0, ln, s0 in sorted(_SEGS, key=lambda s: s[2]):
        if s0 > pos:
            parts.append(jnp.zeros((k, s0 - pos), ps.dtype))
        for j in range(4):
            lo, hi = max(g0, j * MIX_N), min(g0 + ln, (j + 1) * MIX_N)
            if lo < hi:
                parts.append(ps[j][:, lo - j * MIX_N:hi - j * MIX_N])
        pos = s0 + ln
    parts.append(jnp.zeros((k, MIXP - pos), ps.dtype))
    return jnp.concatenate(parts, axis=1)


def _mix_in_from_padded(g):
    k = g.shape[0]
    shards = []
    for j in range(4):
        parts = []
        for g0, ln, s0 in sorted(_SEGS):
            lo, hi = max(g0, j * MIX_N), min(g0 + ln, (j + 1) * MIX_N)
            if lo < hi:
                parts.append(g[:, s0 + lo - g0:s0 + hi - g0])
        parts.append(jnp.zeros((k, MIX_NP - MIX_N), g.dtype))
        shards.append(jnp.concatenate(parts, axis=1))
    return jnp.stack(shards)


def _local_step(x, ctx, tgt, mvec, weights_for, ln_gain, ln_bias, a2f, a2b, abf, abb, lb, gng, gnh, on_grads, on_sent):
    B, T, _ = x.shape
    assert ctx.shape[1] == TM and T % TM == 0
    ntl = T // TM
    nt = ntl + 1
    C, L, CL = (nt, 0), (ntl, 0), (nt, 1)
    pos = _sincos_2d(T // 64, 64, D)
    gbs = [jnp.stack([ln_gain[i], ln_bias[i]]) for i in range(3)]
    a2p = jnp.zeros((2, HD, 4 * HD), F32)
    a2p = a2p.at[0, 0:16].set(_pad_heads(a2f)).at[1, 16:32].set(_pad_heads(a2b)).astype(BF16)
    biasp = jnp.concatenate([_pad_heads(abf.reshape(1, -1)), _pad_heads(abb.reshape(1, -1))], axis=0)
    gains = jnp.concatenate([gng.reshape(1, HD), gnh.reshape(1, HD)], axis=0)

    X0, h0 = _embed_lnmod(x, ctx, pos, mvec)
    w1i, w1o = weights_for("ffn1", h0)
    u0, a0 = _ffn_in(h0, C, w1i, B, nt, "ffn1_in")
    wmp, wmo = weights_for("mix", a0)
    y0 = _mm_nn(a0, C, w1o, B, nt, BF16, "ffn1_out")
    X1, h1 = _resid_lnmod(X0, C, y0, mvec, True, 0, 0.5, gbs[0], B, nt, "resid0_lnmod1")
    Q, K, V, G, Zm = _mix_in_features(h1, wmp, a2p, biasp, lb, B, nt)
    w2i, w2o = weights_for("ffn2", Zm)
    O0, S0 = _scan_fwd(Q, K, V, G, False, B)
    O1, S1 = _scan_fwd(Q, K, V, G, True, B)
    merged = _gnorm(O0, O1, Zm, gains, B, ntl)
    y1 = _mm_nn(merged, L, wmo, B, ntl, BF16, "mix_out")
    X2, h2 = _resid_lnmod(X1, CL, y1, mvec, False, 1, 1.0, gbs[1], B, ntl, "resid1_lnmod2")
    u2, a2 = _ffn_in(h2, L, w2i, B, ntl, "ffn2_in")
    y2 = _mm_nn(a2, L, w2o, B, ntl, BF16, "ffn2_out")

    dx2r, dy2, dgate2, dgb2, lsum = _tail(X2, y2, mvec, gbs[2], tgt, 2, 0.5, B, ntl)
    loss = (0.5 / D) * jnp.sum(lsum)
    du2 = _ffn_out_dx(dy2, w2o, u2, "ffn2_out_dx")
    g_w2o = _mm_tn(a2, dy2, "ffn2_out_dw")
    dh2 = _mm_nt(du2, w2i, "ffn2_in_dx", out_dtype=BF16)
    g_w2i = _mm_tn(h2, du2, "ffn2_in_dw", shards=4)
    tok = on_grads("ffn2", (g_w2i, g_w2o))
    dx1r, dy1, dgate1, dgb1, dss2 = _lnmod_resid_bwd(dh2, X2, L, dx2r, X1, CL, y1, mvec, False, 2, 1.0,
                                                     gbs[1] + tok[0, 0], B, ntl, "lnmod2_resid1_bwd")
    tok = on_sent("ffn2", dy1)
    dmerged = _mm_nt(dy1, wmo, "mix_out_dx", dep=tok, out_dtype=BF16)
    g_wmo = _mm_tn(merged, dy1, "mix_out_dw", tn=D)
    dO, dgates, dgains = _gnorm_bwd(dmerged, O0, O1, Zm, gains, B, ntl)
    dQ0, dK0, dV0, dG0 = _scan_bwd(Q, K, V, G, S0, dO, False, B)
    dQ1, dK1, dV1, dG1 = _scan_bwd(Q, K, V, G, S1, dO, True, B)
    dF, da2p, dbiasp, dlb = _features_bwd(Zm, a2p, biasp, lb, dQ0, dQ1, dK0, dK1, dV0, dV1, dG0, dG1, dgates, B)
    dh1 = _mm_nt(dF, wmp, "mix_in_dx", out_dtype=BF16)
    g_wmp = _mm_tn(h1, dF, "mix_in_dw", tn=MIXP // 4)
    tok = on_grads("mix", (g_wmp, g_wmo))
    dx0r, dy0, dgate0, dgb0, dss1 = _lnmod_resid_bwd(dh1, X1, C, dx1r, X0, C, y0, mvec, True, 1, 0.5,
                                                     gbs[0] + tok[0, 0], B, nt, "lnmod1_resid0_bwd")
    tok = on_sent("mix", dy0)
    du0 = _ffn_out_dx(dy0, w1o, u0, "ffn1_out_dx", dep=tok)
    g_w1o = _mm_tn(a0, dy0, "ffn1_out_dw")
    g_w1i = _mm_tn(h0, du0, "ffn1_in_dw", shards=4)
    tok = on_grads("ffn1", (g_w1i, g_w1o))
    dh0 = _mm_nt(du0, w1i, "ffn1_in_dx", dep=tok, out_dtype=BF16)
    tok = on_sent("ffn1", dh0)
    grad_x, dss0 = _lnmod0_bwd(dh0, X0, mvec + tok[0, 0], dx0r, x.shape, B, nt)

    zero_ctx = lambda a: a.at[:, 0].set(0.0)
    dm = jnp.concatenate([dss0, dgate0, dss1, zero_ctx(dgate1), zero_ctx(dss2), zero_ctx(dgate2)], axis=2)
    small = dict(
        ln_gain=jnp.stack([dgb0[0], dgb1[0], dgb2[0]]), ln_bias=jnp.stack([dgb0[1], dgb1[1], dgb2[1]]),
        a2f=_unpad_heads(da2p[0, 0:16]), a2b=_unpad_heads(da2p[1, 16:32]),
        abf=_unpad_heads(dbiasp[0:1]), abb=_unpad_heads(dbiasp[1:2]), lb=dlb, gng=dgains[0], gnh=dgains[1])
    return loss, grad_x, dm, small


def _small_allgather(xs, name):
    r, n = xs.shape

    def body(x_ref, out_ref, send_sems, recv_sems, local_sem):
        x, y, c = lax.axis_index("x"), lax.axis_index("y"), lax.axis_index("c")
        me, sibling = (x, y, c), (x, y, 1 - c)
        chips = [(1 - x, y), (x, 1 - y), (1 - x, 1 - y)]

        def rows(px, py, pc):
            return out_ref.at[pl.ds((4 * px + 2 * py + pc) * r, r), :]

        def copy(k, block, to, src=None):
            return pltpu.make_async_remote_copy(
                src_ref=rows(*block) if src is None else src, dst_ref=rows(*block),
                send_sem=send_sems.at[k], recv_sem=recv_sems.at[k], device_id=to, device_id_type=MESH)

        mine = pltpu.make_async_copy(x_ref, rows(*me), local_sem)
        mine.start()
        first = [copy(0, me, sibling, src=x_ref)]
        first += [copy(1 + j, me, (*chip, c), src=x_ref) for j, chip in enumerate(chips)]
        for cp in first:
            cp.start()
        passed = [copy(4 + j, (*chip, c), sibling) for j, chip in enumerate(chips)]
        for j, chip in enumerate(chips):
            copy(1 + j, (*chip, c), me).wait_recv()
            passed[j].start()
        copy(0, sibling, me).wait_recv()
        for j, chip in enumerate(chips):
            copy(4 + j, (*chip, 1 - c), me).wait_recv()
        for cp in first + passed:
            cp.wait_send()
        mine.wait()

    out = pl.pallas_call(
        body, name=name,
        out_shape=jax.ShapeDtypeStruct((8 * r, n), xs.dtype),
        in_specs=[pl.BlockSpec(memory_space=pltpu.VMEM)],
        out_specs=pl.BlockSpec(memory_space=pltpu.VMEM),
        scratch_shapes=[pltpu.SemaphoreType.DMA((7,)), pltpu.SemaphoreType.DMA((7,)), pltpu.SemaphoreType.DMA],
        compiler_params=pltpu.CompilerParams(vmem_limit_bytes=VMEM_LIMIT))(xs)
    return out.reshape(8, r, n)


def _gather_flat(v, name):
    n = v.shape[0]
    npad = -(-n // 1024) * 1024
    g = _small_allgather(jnp.pad(v, (0, npad - n)).reshape(8, npad // 8), name)
    return g.reshape(8, npad)[:, :n]


HBM_SPEC = pl.BlockSpec(memory_space=pltpu.HBM)
SEM_SPEC = pl.BlockSpec(memory_space=pltpu.SEMAPHORE)
DATAFLOW = pltpu.SideEffectType.DATAFLOW_SIDE_EFFECTING


def _gather_copies(xs, outs, send_sems, recv_sems):
    x, y, c = lax.axis_index("x"), lax.axis_index("y"), lax.axis_index("c")
    dests = [(x, y, 1 - c), (1 - x, y, c), (x, 1 - y, c), (1 - x, 1 - y, c)]
    return [pltpu.make_async_remote_copy(
        src_ref=xs[w], dst_ref=outs[w].at[4 * x + 2 * y + c], send_sem=send_sems[4 * w + k],
        recv_sem=recv_sems[4 * w + k], device_id=dests[k], device_id_type=MESH)
        for w in range(len(xs)) for k in range(4)]


def _split_start(copies, per_w, srcs, land_lead, after, name):
    n = len(srcs)
    m = per_w * n
    lands = [lax.empty((land_lead,) + s.shape[-2:], s.dtype) for s in srcs]
    deps = [] if after is None else [after]

    def body(*refs):
        xs, ls, outs = refs[:n], refs[n:2 * n], refs[2 * n + len(deps):]
        for cp in copies(xs, ls, outs[:m], outs[m:2 * m]):
            cp.start()
        token = outs[2 * m + 2 * n]
        token[...] = jnp.zeros_like(token)

    outs = pl.pallas_call(
        body, name=name,
        out_shape=([pltpu.SemaphoreType.DMA(())] * (2 * m) + [pltpu.HBM(a.shape, a.dtype) for a in srcs + lands]
                   + [jax.ShapeDtypeStruct((8, 128), F32)]),
        in_specs=[HBM_SPEC] * (2 * n) + [pl.BlockSpec(memory_space=pl.ANY)] * len(deps),
        out_specs=[SEM_SPEC] * (2 * m) + [HBM_SPEC] * (2 * n) + [pl.BlockSpec(memory_space=pltpu.VMEM)],
        input_output_aliases={w: 2 * m + w for w in range(2 * n)},
        compiler_params=pltpu.CompilerParams(has_side_effects=DATAFLOW),
    )(*[pltpu.with_memory_space_constraint(a, pltpu.HBM) for a in srcs + lands], *deps)
    return outs[:2 * m], outs[2 * m:2 * m + n], outs[2 * m + n:2 * m + 2 * n], outs[2 * m + 2 * n]


def _split_wait(copies, per_w, sems, x_thru, l_thru, after, name):
    n = len(x_thru)
    m = per_w * n

    def body(*refs):
        xs, ls, ss = refs[:n], refs[n:2 * n], refs[2 * n:2 * n + 2 * m]
        for cp in copies(xs, ls, ss[:m], ss[m:]):
            cp.wait_send()
            cp.wait_recv()

    outs = pl.pallas_call(
        body, name=name,
        out_shape=[pltpu.HBM(a.shape, a.dtype) for a in list(x_thru) + list(l_thru)],
        in_specs=[HBM_SPEC] * (2 * n) + [SEM_SPEC] * (2 * m) + [pl.BlockSpec(memory_space=pl.ANY)],
        out_specs=[HBM_SPEC] * (2 * n),
        input_output_aliases={w: w for w in range(2 * n)},
        compiler_params=pltpu.CompilerParams(has_side_effects=DATAFLOW),
    )(*x_thru, *l_thru, *sems, after)
    return outs[:n], outs[n:]


def _gather_forward(gathered, name):
    n = len(gathered)

    def body(*refs):
        outs = refs[n:2 * n]
        send_sems, recv_sems = refs[2 * n:]
        x, y, c = lax.axis_index("x"), lax.axis_index("y"), lax.axis_index("c")
        chips = [(1 - x, y), (x, 1 - y), (1 - x, 1 - y), (x, y)]

        def copy(w, j, pc):
            px, py = chips[j]
            slot = outs[w].at[4 * px + 2 * py + pc]
            return pltpu.make_async_remote_copy(
                src_ref=slot, dst_ref=slot, send_sem=send_sems.at[4 * w + j], recv_sem=recv_sems.at[4 * w + j],
                device_id=(x, y, 1 - c), device_id_type=MESH)

        sends = [copy(w, j, 1 - c if j == 3 else c) for w in range(n) for j in range(4)]
        for cp in sends:
            cp.start()
        for w in range(n):
            for j in range(4):
                copy(w, j, c if j == 3 else 1 - c).wait_recv()
        for cp in sends:
            cp.wait_send()

    any_spec = pl.BlockSpec(memory_space=pl.ANY)
    return pl.pallas_call(
        body, name=name,
        out_shape=[jax.ShapeDtypeStruct(g.shape, g.dtype) for g in gathered],
        in_specs=[any_spec] * n, out_specs=[any_spec] * n,
        input_output_aliases={w: w for w in range(n)},
        scratch_shapes=[pltpu.SemaphoreType.DMA((4 * n,)), pltpu.SemaphoreType.DMA((4 * n,))],
    )(*gathered)


def _pair_copies(gs, ls, send_sems, recv_sems):
    x, y, c = lax.axis_index("x"), lax.axis_index("y"), lax.axis_index("c")
    return [pltpu.make_async_remote_copy(
        src_ref=gs[w].at[2 * j + 1 - c], dst_ref=ls[w].at[j], send_sem=send_sems[4 * w + j],
        recv_sem=recv_sems[4 * w + j], device_id=(x, y, 1 - c), device_id_type=MESH)
        for w in range(len(gs)) for j in range(4)]


def _all_copies(xs, ls, send_sems, recv_sems):
    x, y, c = lax.axis_index("x"), lax.axis_index("y"), lax.axis_index("c")
    flips = [(a, b, e) for a in (0, 1) for b in (0, 1) for e in (0, 1)][1:]
    return [pltpu.make_async_remote_copy(
        src_ref=xs[0], dst_ref=ls[0].at[4 * x + 2 * y + c], send_sem=send_sems[k], recv_sem=recv_sems[k],
        device_id=(1 - x if a else x, 1 - y if b else y, 1 - c if e else c), device_id_type=MESH)
        for k, (a, b, e) in enumerate(flips)]


def _share_copies(fs, ls, send_sems, recv_sems):
    x, y, c = lax.axis_index("x"), lax.axis_index("y"), lax.axis_index("c")
    return [pltpu.make_async_remote_copy(
        src_ref=fs[w].at[c], dst_ref=fs[w].at[c], send_sem=send_sems[w], recv_sem=recv_sems[w],
        device_id=(x, y, 1 - c), device_id_type=MESH) for w in range(len(fs))]


def _chip_copies(hs, ls, send_sems, recv_sems):
    x, y, c = lax.axis_index("x"), lax.axis_index("y"), lax.axis_index("c")
    chips = [(1 - x, y), (x, 1 - y), (1 - x, 1 - y)]
    return [pltpu.make_async_remote_copy(
        src_ref=hs[w].at[2 * px + py], dst_ref=ls[w].at[k], send_sem=send_sems[3 * w + k],
        recv_sem=recv_sems[3 * w + k], device_id=(px, py, c), device_id_type=MESH)
        for w in range(len(hs)) for k, (px, py) in enumerate(chips)]


def _rs_add_pair(g8, r4, c, name):
    R, n = g8.shape[1:]
    rb = R // 2

    def body(c_ref, g_ref, r_ref, o_ref):
        o_ref[...] = (g_ref[...] + r_ref[...]).astype(BF16)

    spec = pl.BlockSpec((1, rb, n), lambda j, i, c_ref: (j, i, 0))
    return pl.pallas_call(
        body, name=name,
        grid_spec=pltpu.PrefetchScalarGridSpec(
            num_scalar_prefetch=1, grid=(4, R // rb),
            in_specs=[pl.BlockSpec((1, rb, n), lambda j, i, c_ref: (2 * j + c_ref[0], i, 0)), spec],
            out_specs=spec),
        out_shape=jax.ShapeDtypeStruct((4, R, n), BF16), compiler_params=_cp(2))(c, g8, r4)


def _rs_add_chips(g8, r4, r3, cj, name):
    R, n = g8.shape[1:]
    rb = R // 2

    def body(cj_ref, g_ref, p_ref, r_ref, o_ref):
        own = g_ref[0] + p_ref[0]
        o_ref[0] = ((own + r_ref[0].astype(F32)) + r_ref[1].astype(F32)) + r_ref[2].astype(F32)

    return pl.pallas_call(
        body, name=name,
        grid_spec=pltpu.PrefetchScalarGridSpec(
            num_scalar_prefetch=1, grid=(R // rb,),
            in_specs=[pl.BlockSpec((1, rb, n), lambda i, cj_ref: (2 * cj_ref[1] + cj_ref[0], i, 0)),
                      pl.BlockSpec((1, rb, n), lambda i, cj_ref: (cj_ref[1], i, 0)),
                      pl.BlockSpec((3, rb, n), lambda i, cj_ref: (0, i, 0))],
            out_specs=pl.BlockSpec((1, rb, n), lambda i, cj_ref: (cj_ref[0], i, 0))),
        out_shape=jax.ShapeDtypeStruct((2, R, n), F32), compiler_params=_cp(1))(cj, g8, r4, r3)


def _sum8(g):
    n = g.shape[1]

    def body(g_ref, o_ref):
        acc = g_ref[0:1, :]
        for k in range(1, 8):
            acc = acc + g_ref[k:k + 1, :]
        o_ref[...] = acc

    return pl.pallas_call(body, name="sum_devices", out_shape=jax.ShapeDtypeStruct((1, n), F32),
                          compiler_params=pltpu.CompilerParams(vmem_limit_bytes=VMEM_LIMIT))(g)


ADA_ROWS = 64


def _ada_fwd(cs, w, b):
    n = w.shape[1]

    def body(c_ref, w_ref, b_ref, o_ref):
        cv = c_ref[...]
        s = (cv * _sigmoid(cv)).astype(BF16)
        o_ref[...] = _dot(s, w_ref[...].astype(BF16)) + b_ref[...]

    return pl.pallas_call(body, name="ada_fwd", out_shape=jax.ShapeDtypeStruct((ADA_ROWS, n), F32),
                          compiler_params=pltpu.CompilerParams(vmem_limit_bytes=VMEM_LIMIT))(cs, w, b)


def _ada_bwd(cs, w, dm):
    n = w.shape[1]

    def body(c_ref, w_ref, dm_ref, gw_ref, dc_ref):
        cv = c_ref[...]
        s = (cv * _sigmoid(cv)).astype(BF16)
        gw_ref[...] = _dot_tn(s, dm_ref[...].astype(BF16))
        dc_ref[...] = _dot_nt(dm_ref[32:40, :].astype(BF16), w_ref[...].astype(BF16))

    return pl.pallas_call(
        body, name="ada_bwd",
        out_shape=[jax.ShapeDtypeStruct((D, n), F32), jax.ShapeDtypeStruct((8, D), F32)],
        compiler_params=pltpu.CompilerParams(vmem_limit_bytes=VMEM_LIMIT))(cs, w, dm)


def _adamw(w, g, m, v, name):
    r, c = w.shape
    rb = r
    if r % 8 == 0 and r * c * 4 > (1 << 20):
        rb = 8
        for cand in range(8, r, 8):
            if r % cand == 0 and cand * c * 4 <= (1 << 20):
                rb = cand

    def body(w_ref, g_ref, m_ref, v_ref, go_ref, d_ref, nm_ref, nv_ref):
        gv = g_ref[...]
        go_ref[...] = gv
        mn = ADAM_B1 * m_ref[...] + (1.0 - ADAM_B1) * gv
        vn = ADAM_B2 * v_ref[...] + (1.0 - ADAM_B2) * (gv * gv)
        m_hat = mn / (1.0 - ADAM_B1 ** ADAM_STEP)
        v_hat = vn / (1.0 - ADAM_B2 ** ADAM_STEP)
        d_ref[...] = -ADAM_LR * (m_hat / (jnp.sqrt(v_hat) + ADAM_EPS) + ADAM_WD * w_ref[...])
        nm_ref[...] = mn
        nv_ref[...] = vn

    spec = pl.BlockSpec((rb, c), lambda i: (i, 0))
    shp = jax.ShapeDtypeStruct((r, c), F32)
    return pl.pallas_call(body, name=name, grid=(r // rb,), in_specs=[spec] * 4, out_specs=[spec] * 4,
                          out_shape=[shp] * 4, compiler_params=_cp(1))(w, g, m, v)


BIG = ("ffn1_w_in", "ffn1_w_out", "w_mix_in", "w_mix_out", "ffn2_w_in", "ffn2_w_out")


def _half_rows(w, c):
    half = w.shape[0] // 2
    return lax.dynamic_slice_in_dim(w, c * half, half, axis=0)


def _lower_bounds(logits):
    return jnp.cumsum(jax.nn.softmax(logits.astype(F32), axis=1), axis=1)[:, 0]


def kernel(x, c, ctx, c_ctx, w_ada, b_ada, ln_gain, ln_bias, ffn1_w_in, ffn1_w_out, w_mix_in, gla_a2_fwd, gla_a2_bwd, gla_a_bias_fwd, gla_a_bias_bwd, hgrn_lb_logits, gla_norm_gain, hgrn_norm_gain, w_mix_out, ffn2_w_in, ffn2_w_out, loss_target, m_c_ctx, m_w_ada, m_b_ada, m_ln_gain, m_ln_bias, m_ffn1_w_in, m_ffn1_w_out, m_w_mix_in, m_gla_a2_fwd, m_gla_a2_bwd, m_gla_a_bias_fwd, m_gla_a_bias_bwd, m_hgrn_lb_logits, m_gla_norm_gain, m_hgrn_norm_gain, m_w_mix_out, m_ffn2_w_in, m_ffn2_w_out, v_c_ctx, v_w_ada, v_b_ada, v_ln_gain, v_ln_bias, v_ffn1_w_in, v_ffn1_w_out, v_w_mix_in, v_gla_a2_fwd, v_gla_a2_bwd, v_gla_a_bias_fwd, v_gla_a_bias_bwd, v_hgrn_lb_logits, v_gla_norm_gain, v_hgrn_norm_gain, v_w_mix_out, v_ffn2_w_in, v_ffn2_w_out):
    xi, yi, ci = lax.axis_index("x"), lax.axis_index("y"), lax.axis_index("c")
    chip = 2 * xi + yi
    dev = 2 * chip + ci
    B = x.shape[0]
    weights = dict(ffn1_w_in=ffn1_w_in[0], ffn1_w_out=ffn1_w_out[0], w_mix_in=w_mix_in[0], w_mix_out=w_mix_out[0],
                   ffn2_w_in=ffn2_w_in[0], ffn2_w_out=ffn2_w_out[0])

    mine = jnp.concatenate([c.reshape(-1), ln_gain.reshape(-1), ln_bias.reshape(-1), gla_a2_fwd.reshape(-1),
                            gla_a2_bwd.reshape(-1), hgrn_lb_logits.reshape(-1)])
    g1 = _gather_flat(mine, "gather_cond")
    nc = B * D
    c_all = g1[:, :nc].reshape(8 * B, D)
    per_chip = g1[0::2, nc:]
    o = 0

    def take(shape, axis):
        nonlocal o
        n = int(np.prod(shape))
        parts = per_chip[:, o:o + n].reshape((4,) + shape)
        o += n
        return jnp.concatenate([parts[j] for j in range(4)], axis=axis)

    ln_gain_f = take((3, 256), 1)
    ln_bias_f = take((3, 256), 1)
    a2f_f = take((16, 64), 1)
    a2b_f = take((16, 64), 1)
    lbl_f = take((2, 2, 128), 2)
    lb, lb_vjp = jax.vjp(_lower_bounds, lbl_f)

    cs = jnp.concatenate([c_all, c_ctx.reshape(1, D), jnp.zeros((ADA_ROWS - 8 * B - 1, D), F32)], axis=0)
    ncol = w_ada.shape[2]
    b_cols = lax.dynamic_slice_in_dim(b_ada, chip * ncol, ncol, axis=1)
    m_cols = _ada_fwd(cs, w_ada[0], b_cols)
    g2 = _small_allgather(m_cols, "gather_mod")[0::2]
    m_all = jnp.concatenate([g2[j] for j in range(4)], axis=1)
    m_lat = lax.dynamic_slice_in_dim(m_all, dev * B, B, axis=0).reshape(B, 1, N_MOD, D)
    m_ctx = jnp.broadcast_to(m_all[8 * B].reshape(1, 1, N_MOD, D), (B, 1, N_MOD, D))

    groups = dict(ffn2=("ffn2_w_in", "ffn2_w_out"), mix=("w_mix_in", "w_mix_out"), ffn1=("ffn1_w_in", "ffn1_w_out"))
    shards = dict(weights, w_mix_in=jnp.pad(weights["w_mix_in"], ((0, 0), (0, MIX_NP - MIX_N))))
    blks = {k: _half_rows(shards[k], ci).astype(BF16) for k in BIG}
    gathering = {}
    token = m_all
    for group in ("ffn1", "mix", "ffn2"):
        sems, x_thru, l_thru, token = _split_start(_gather_copies, 4, [blks[k] for k in groups[group]], 8, token,
                                                   "weight_gather_start_" + group)
        gathering[group] = (sems, x_thru, l_thru)
    mvec = jnp.concatenate([m_ctx, m_lat], axis=1) + token[0, 0]

    def weights_for(group, after):
        names = groups[group]
        _, got = _split_wait(_gather_copies, 4, *gathering[group], after, "weight_gather_wait_" + group)
        w_in, w_out = _gather_forward(got, "weight_gather_forward_" + group)
        if group == "mix":
            return _mix_in_to_padded(w_in.reshape(4, D, MIX_NP)), w_out.reshape(-1, D)
        return w_in.reshape((4,) + shards[names[0]].shape), w_out.reshape(-1, D)

    cvec = ci.reshape(1).astype(jnp.int32)
    cjvec = jnp.stack([ci, chip]).astype(jnp.int32)
    in_flight = {}

    def on_grads(group, gs):
        names = groups[group]
        if group == "mix":
            gs = (_mix_in_from_padded(gs[0]), gs[1])
        g8s = [g.reshape((8, shards[k].shape[0] // 2, shards[k].shape[1])) for k, g in zip(names, gs)]
        sems, g_thru, l_thru, token = _split_start(_pair_copies, 4, g8s, 4, None, "grad_pair_start_" + group)
        in_flight[group] = (sems, g_thru, l_thru)
        return token

    def on_sent(group, after):
        g8s, r4s = _split_wait(_pair_copies, 4, *in_flight[group], after, "grad_pair_wait_" + group)
        h4s = [_rs_add_pair(g, r, cvec, "grad_pair_add_" + k) for k, g, r in zip(groups[group], g8s, r4s)]
        sems, h_thru, l_thru, token = _split_start(_chip_copies, 3, h4s, 3, None, "grad_chip_start_" + group)
        in_flight[group] = (g8s, r4s, sems, h_thru, l_thru)
        return token

    loss_l, grad_x, dm, small = _local_step(
        x, ctx, loss_target, mvec, weights_for, ln_gain_f, ln_bias_f, a2f_f, a2b_f,
        gla_a_bias_fwd, gla_a_bias_bwd, lb, gla_norm_gain, hgrn_norm_gain, on_grads, on_sent)
    loss = lax.psum(loss_l, ("x", "y", "c"))

    dm_lat = dm[:, 1].reshape(B, N_MOD * D)
    dm_ctx = jnp.sum(dm[:, 0], axis=0).reshape(N_MOD * D)
    keys = ("ln_gain", "ln_bias", "a2f", "a2b", "abf", "abb", "lb", "gng", "gnh")
    flat = jnp.concatenate([dm_lat.reshape(-1), dm_ctx] + [small[k].reshape(-1) for k in keys])
    nflat = flat.shape[0]
    npad = -(-nflat // 1024) * 1024
    flat2 = jnp.pad(flat, (0, npad - nflat)).reshape(8, npad // 8)
    s_sems, s_src, s_land, tok = _split_start(_all_copies, 7, [flat2], 8, None, "small_grads_start")

    fin = {}
    for group in ("ffn2", "mix", "ffn1"):
        g8s, r4s, sems, h_thru, l_thru = in_flight[group]
        _, r3s = _split_wait(_chip_copies, 3, sems, h_thru, l_thru, tok, "grad_chip_wait_" + group)
        for k, g, r4, r3 in zip(groups[group], g8s, r4s, r3s):
            fin[k] = _rs_add_chips(g, r4, r3, cjvec, "grad_chip_add_" + k)
    p_sems, p_src, p_land, tok = _split_start(_share_copies, 1, [fin[k] for k in BIG], 1, None, "grad_pair_share_start")

    _, (land,) = _split_wait(_all_copies, 7, s_sems, s_src, s_land, tok, "small_grads_wait")
    g3 = lax.dynamic_update_index_in_dim(land, flat2, dev, 0).reshape(8, npad)[:, :nflat]
    nlat = B * N_MOD * D
    dm_all = g3[:, :nlat].reshape(8 * B, N_MOD * D)
    tot = _sum8(g3[:, nlat:])[0]
    dmc_tot = tot[:N_MOD * D]
    o = N_MOD * D
    sg = {}
    for k in keys:
        n = int(np.prod(small[k].shape))
        sg[k] = tot[o:o + n].reshape(small[k].shape)
        o += n
    dm_rows = jnp.concatenate([dm_all, dmc_tot.reshape(1, -1), jnp.zeros((ADA_ROWS - 8 * B - 1, N_MOD * D), F32)], axis=0)
    g_b_ada = (jnp.sum(dm_all, axis=0) + dmc_tot).reshape(1, N_MOD * D)
    g_w_ada, dcc = _ada_bwd(cs, w_ada[0], lax.dynamic_slice_in_dim(dm_rows, chip * ncol, ncol, axis=1))
    g4 = _gather_flat(dcc[0], "gather_cctx")
    dsilu = ((g4[0] + g4[2]) + g4[4]) + g4[6]
    sc = _sigmoid(c_ctx)
    g_c_ctx = dsilu * (sc * (1.0 + c_ctx * (1.0 - sc)))
    (g_lbl,) = lb_vjp(sg["lb"])

    def cols(a, n, axis):
        return lax.dynamic_slice_in_dim(a, chip * n, n, axis=axis)

    shared, _ = _split_wait(_share_copies, 1, p_sems, p_src, p_land, g_w_ada, "grad_pair_share_wait")
    gsh = {k: both.reshape(shards[k].shape)[:, :weights[k].shape[1]] for k, both in zip(BIG, shared)}

    grads = dict(
        c_ctx=g_c_ctx, w_ada=g_w_ada[None], b_ada=g_b_ada, ln_gain=cols(sg["ln_gain"], 256, 1)[None],
        ln_bias=cols(sg["ln_bias"], 256, 1)[None], ffn1_w_in=gsh["ffn1_w_in"][None], ffn1_w_out=gsh["ffn1_w_out"][None],
        w_mix_in=gsh["w_mix_in"][None], gla_a2_fwd=cols(sg["a2f"], 64, 1)[None], gla_a2_bwd=cols(sg["a2b"], 64, 1)[None],
        gla_a_bias_fwd=sg["abf"], gla_a_bias_bwd=sg["abb"], hgrn_lb_logits=cols(g_lbl, 128, 2),
        gla_norm_gain=sg["gng"].reshape(1, HD), hgrn_norm_gain=sg["gnh"].reshape(1, HD),
        w_mix_out=gsh["w_mix_out"][None], ffn2_w_in=gsh["ffn2_w_in"][None], ffn2_w_out=gsh["ffn2_w_out"][None])
    params = dict(
        c_ctx=(c_ctx, m_c_ctx, v_c_ctx), w_ada=(w_ada, m_w_ada, v_w_ada), b_ada=(b_ada, m_b_ada, v_b_ada),
        ln_gain=(ln_gain, m_ln_gain, v_ln_gain), ln_bias=(ln_bias, m_ln_bias, v_ln_bias),
        ffn1_w_in=(ffn1_w_in, m_ffn1_w_in, v_ffn1_w_in), ffn1_w_out=(ffn1_w_out, m_ffn1_w_out, v_ffn1_w_out),
        w_mix_in=(w_mix_in, m_w_mix_in, v_w_mix_in), gla_a2_fwd=(gla_a2_fwd, m_gla_a2_fwd, v_gla_a2_fwd),
        gla_a2_bwd=(gla_a2_bwd, m_gla_a2_bwd, v_gla_a2_bwd),
        gla_a_bias_fwd=(gla_a_bias_fwd, m_gla_a_bias_fwd, v_gla_a_bias_fwd),
        gla_a_bias_bwd=(gla_a_bias_bwd, m_gla_a_bias_bwd, v_gla_a_bias_bwd),
        hgrn_lb_logits=(hgrn_lb_logits, m_hgrn_lb_logits, v_hgrn_lb_logits),
        gla_norm_gain=(gla_norm_gain, m_gla_norm_gain, v_gla_norm_gain),
        hgrn_norm_gain=(hgrn_norm_gain, m_hgrn_norm_gain, v_hgrn_norm_gain),
        w_mix_out=(w_mix_out, m_w_mix_out, v_w_mix_out), ffn2_w_in=(ffn2_w_in, m_ffn2_w_in, v_ffn2_w_in),
        ffn2_w_out=(ffn2_w_out, m_ffn2_w_out, v_ffn2_w_out))
    order = list(params.keys())
    big_names = ("w_ada",) + BIG
    upd = {}
    for k in big_names:
        w_, m_, v_ = params[k]
        s2 = w_.shape[-2:]
        g_, d_, nm_, nv_ = _adamw(w_.reshape(s2), grads[k].reshape(s2), m_.reshape(s2), v_.reshape(s2), "adamw_" + k)
        grads[k] = g_
        upd[k] = (d_.reshape(w_.shape), nm_.reshape(w_.shape), nv_.reshape(w_.shape))
    small_names = [k for k in order if k not in big_names]
    sizes = [int(np.prod(params[k][0].shape)) for k in small_names]
    tot_n = sum(sizes)
    npad = -(-tot_n // 1024) * 1024

    def packed(get):
        flat_ = jnp.concatenate([get(k).reshape(-1) for k in small_names])
        return jnp.pad(flat_, (0, npad - tot_n)).reshape(8, npad // 8)

    _, d_s, nm_s, nv_s = _adamw(packed(lambda k: params[k][0]), packed(lambda k: grads[k]),
                                packed(lambda k: params[k][1]), packed(lambda k: params[k][2]), "adamw_small")
    o = 0
    for k, n in zip(small_names, sizes):
        shp = params[k][0].shape
        upd[k] = tuple(a.reshape(-1)[o:o + n].reshape(shp) for a in (d_s, nm_s, nv_s))
        o += n

    return (loss, grad_x, *[grads[k].reshape(params[k][0].shape) for k in order], *[upd[k][0] for k in order],
            *[upd[k][1] for k in order], *[upd[k][2] for k in order])
```

```python
import functools

import numpy as np
import jax
import jax.numpy as jnp
from jax import lax
from jax.experimental import pallas as pl
from jax.experimental.pallas import tpu as pltpu

F32 = jnp.float32
BF16 = jnp.bfloat16
MESH = pl.DeviceIdType.MESH

D = 1024
DFF = 2816
TM = 256
CH = 32
NCB = TM // CH
SB = 128
CSB = SB // CH
NSB = TM // SB
HP = 8
HPB = 8
HD = 128
NH = 8
LN_EPS = 1e-5
NORM_EPS = 1e-6
ALPHA = 2.0 ** 0.25
GATE_NORM = 16.0
GLA_DK = 64
N_MOD = 9
VMEM_LIMIT = 52 * 1024 * 1024

MIXP = 5120
GG, RG, GQ, GK, GV, RQ, RFF, RFB, RI, LR = 0, 512, 1024, 1536, 2048, 2560, 3072, 3584, 4096, 4608
IN_SPLITS = (256, 256, 512, 512, 16, 16, 512, 512, 512, 512, 512)

ADAM_LR, ADAM_B1, ADAM_B2, ADAM_EPS, ADAM_WD, ADAM_STEP = 0.001, 0.9, 0.999, 1e-08, 0.01, 10


def _cp(n_axes):
    return pltpu.CompilerParams(dimension_semantics=("arbitrary",) * n_axes, vmem_limit_bytes=VMEM_LIMIT)


def _rowmap(stride, off):
    return lambda b, i: (b * stride + off + i, 0)


def _mmap(comb):
    if comb:
        return lambda b, i: (b, jnp.minimum(i, 1), 0, 0)
    return lambda b, i: (b, 1, 0, 0)


def _ln(x):
    mu = jnp.mean(x, axis=-1, keepdims=True)
    xc = x - mu
    var = jnp.mean(xc * xc, axis=-1, keepdims=True)
    r = lax.rsqrt(var + LN_EPS)
    return xc * r, r


def _ln_bwd(dxh, xh, r):
    return r * (dxh - jnp.mean(dxh, axis=-1, keepdims=True) - xh * jnp.mean(dxh * xh, axis=-1, keepdims=True))


def _sigmoid(x):
    return 1.0 / (1.0 + jnp.exp(-x))


def _rsum(x):
    return jnp.sum(x, axis=0, keepdims=True)


def _dot(a, b):
    return jnp.dot(a, b, preferred_element_type=F32)


def _dot_nt(a, b):
    return lax.dot_general(a, b, (((1,), (1,)), ((), ())), preferred_element_type=F32)


def _dot_tn(a, b):
    return lax.dot_general(a, b, (((0,), (0,)), ((), ())), preferred_element_type=F32)


def _modulate(xv, m_ref, sub):
    xh, _ = _ln(xv)
    sh = m_ref[0, 0, 3 * sub:3 * sub + 1, :]
    sc = m_ref[0, 0, 3 * sub + 1:3 * sub + 2, :]
    return (xh * (1.0 + sh) + sc).astype(BF16)


def _embed_lnmod(x, ctx, pos, mvec):
    B, T, _ = x.shape
    nt = 1 + T // TM

    def body(x_ref, c_ref, p_ref, m_ref, o_ref, h_ref):
        i = pl.program_id(1)

        @pl.when(i == 0)
        def _():
            o_ref[...] = c_ref[0]

        @pl.when(i > 0)
        def _():
            o_ref[...] = x_ref[0] + p_ref[...]

        h_ref[...] = _modulate(o_ref[...], m_ref, 0)

    rows = pl.BlockSpec((TM, D), lambda b, i: (b * nt + i, 0))
    return pl.pallas_call(
        body, name="embed_lnmod0", grid=(B, nt),
        in_specs=[pl.BlockSpec((1, TM, D), lambda b, i: (b, jnp.maximum(i - 1, 0), 0)),
                  pl.BlockSpec((1, TM, D), lambda b, i: (b, 0, 0)),
                  pl.BlockSpec((TM, D), lambda b, i: (jnp.maximum(i - 1, 0), 0)),
                  pl.BlockSpec((1, 1, N_MOD, D), _mmap(True))],
        out_specs=[rows, rows],
        out_shape=[jax.ShapeDtypeStruct((B * nt * TM, D), F32), jax.ShapeDtypeStruct((B * nt * TM, D), BF16)],
        compiler_params=_cp(2))(x, ctx, pos, mvec)


def _lnmod0_bwd(dh, X, mvec, dres, x_shape, B, nt):
    def body(dh_ref, x_ref, m_ref, dr_ref, dx_ref, dm_ref):
        i = pl.program_id(1)
        xh, r = _ln(x_ref[...])
        sh = m_ref[0, 0, 0:1, :]
        dhv = dh_ref[...].astype(F32)

        @pl.when((i == 0) | (i == 1))
        def _():
            dm_ref[...] = jnp.zeros_like(dm_ref)

        dm_ref[0, 0, 0:1, :] += _rsum(dhv * xh)
        dm_ref[0, 0, 1:2, :] += _rsum(dhv)

        @pl.when(i > 0)
        def _():
            dx_ref[0] = _ln_bwd(dhv * (1.0 + sh), xh, r) + dr_ref[...]

    rows = pl.BlockSpec((TM, D), _rowmap(nt, 0))
    return pl.pallas_call(
        body, name="lnmod0_bwd", grid=(B, nt),
        in_specs=[rows, rows, pl.BlockSpec((1, 1, N_MOD, D), _mmap(True)), rows],
        out_specs=[pl.BlockSpec((1, TM, D), lambda b, i: (b, jnp.maximum(i - 1, 0), 0)),
                   pl.BlockSpec((1, 1, 2, D), _mmap(True))],
        out_shape=[jax.ShapeDtypeStruct(x_shape, F32), jax.ShapeDtypeStruct((B, 2, 2, D), F32)],
        compiler_params=_cp(2))(dh, X, mvec, dres)


def _resid_fwd(x_ref, y_ref, m_ref, gb_ref, sub, w):
    wg = w * m_ref[0, 0, 3 * sub + 2:3 * sub + 3, :]
    y = y_ref[...].astype(F32)
    zh, r = _ln(ALPHA * x_ref[...] + wg * y)
    return y, wg, zh, r


def _resid_grads(do, y, wg, zh, r, w, gb_ref, comb, dx_ref, dy_ref, dg_ref, dgb_ref):
    b_, i = pl.program_id(0), pl.program_id(1)
    dz = _ln_bwd(do * gb_ref[0:1, :], zh, r)
    dx_ref[...] = ALPHA * dz
    dy_ref[...] = (wg * dz).astype(BF16)

    @pl.when((b_ == 0) & (i == 0))
    def _():
        dgb_ref[...] = jnp.zeros_like(dgb_ref)

    dgb_ref[0:1, :] += _rsum(do * zh)
    dgb_ref[1:2, :] += _rsum(do)

    init = (i == 0) | (i == 1) if comb else (i == 0)

    @pl.when(init)
    def _():
        dg_ref[...] = jnp.zeros_like(dg_ref)

    dg_ref[0, 0] += w * _rsum(dz * y)


def _resid_lnmod(X, lx, Y, mvec, comb, sub, w, gb, B, nt, name):
    def body(x_ref, y_ref, m_ref, gb_ref, o_ref, h_ref):
        _, _, zh, _ = _resid_fwd(x_ref, y_ref, m_ref, gb_ref, sub, w)
        xn = zh * gb_ref[0:1, :] + gb_ref[1:2, :]
        o_ref[...] = xn
        h_ref[...] = _modulate(xn, m_ref, sub + 1)

    rows = pl.BlockSpec((TM, D), _rowmap(nt, 0))
    return pl.pallas_call(
        body, name=name, grid=(B, nt),
        in_specs=[pl.BlockSpec((TM, D), _rowmap(*lx)), rows,
                  pl.BlockSpec((1, 1, N_MOD, D), _mmap(comb)), pl.BlockSpec((2, D), lambda b, i: (0, 0))],
        out_specs=[rows, rows],
        out_shape=[jax.ShapeDtypeStruct((B * nt * TM, D), F32), jax.ShapeDtypeStruct((B * nt * TM, D), BF16)],
        compiler_params=_cp(2))(X, Y, mvec, gb)


def _resid_out_shapes(B, nt, comb):
    rows = pl.BlockSpec((TM, D), _rowmap(nt, 0))
    specs = [rows, rows, pl.BlockSpec((1, 1, 1, D), _mmap(comb)), pl.BlockSpec((2, D), lambda b, i: (0, 0))]
    shapes = [jax.ShapeDtypeStruct((B * nt * TM, D), F32), jax.ShapeDtypeStruct((B * nt * TM, D), BF16),
              jax.ShapeDtypeStruct((B, 2, 1, D), F32), jax.ShapeDtypeStruct((2, D), F32)]
    return specs, shapes


def _tail(X, Y, mvec, gb, tgt, sub, w, B, nt):
    def body(x_ref, y_ref, m_ref, gb_ref, t_ref, dx_ref, dy_ref, dg_ref, dgb_ref, l_ref):
        y, wg, zh, r = _resid_fwd(x_ref, y_ref, m_ref, gb_ref, sub, w)
        e = (zh * gb_ref[0:1, :] + gb_ref[1:2, :]) - t_ref[0]

        @pl.when((pl.program_id(0) == 0) & (pl.program_id(1) == 0))
        def _():
            l_ref[...] = jnp.zeros_like(l_ref)

        l_ref[...] += _rsum(e * e)
        _resid_grads(e * (1.0 / D), y, wg, zh, r, w, gb_ref, False, dx_ref, dy_ref, dg_ref, dgb_ref)

    rows = pl.BlockSpec((TM, D), _rowmap(nt, 0))
    specs, shapes = _resid_out_shapes(B, nt, False)
    return pl.pallas_call(
        body, name="resid2_loss_bwd", grid=(B, nt),
        in_specs=[rows, rows, pl.BlockSpec((1, 1, N_MOD, D), _mmap(False)), pl.BlockSpec((2, D), lambda b, i: (0, 0)),
                  pl.BlockSpec((1, TM, D), lambda b, i: (b, i, 0))],
        out_specs=specs + [pl.BlockSpec((1, D), lambda b, i: (0, 0))],
        out_shape=shapes + [jax.ShapeDtypeStruct((1, D), F32)],
        compiler_params=_cp(2))(X, Y, mvec, gb, tgt)


def _lnmod_resid_bwd(dh, Xi, lxi, dres, Xp, lxp, Yp, mvec, comb, sub, w, gb, B, nt, name):
    ntl = nt - 1 if comb else nt

    def body(dh_ref, xi_ref, dr_ref, xp_ref, yp_ref, m_ref, gb_ref, dx_ref, dy_ref, dg_ref, dgb_ref, dm_ref):
        i = pl.program_id(1)
        xh, r = _ln(xi_ref[...])
        sh = m_ref[0, 0, 3 * sub:3 * sub + 1, :]
        dhv = dh_ref[...].astype(F32)
        dr = dr_ref[...]
        if comb:
            dr = jnp.where(i > 0, dr, 0.0)
        do = _ln_bwd(dhv * (1.0 + sh), xh, r) + dr

        init = (i == 0) | (i == 1) if comb else (i == 0)

        @pl.when(init)
        def _():
            dm_ref[...] = jnp.zeros_like(dm_ref)

        dm_ref[0, 0, 0:1, :] += _rsum(dhv * xh)
        dm_ref[0, 0, 1:2, :] += _rsum(dhv)
        y, wg, zh, r2 = _resid_fwd(xp_ref, yp_ref, m_ref, gb_ref, sub - 1, w)
        _resid_grads(do, y, wg, zh, r2, w, gb_ref, comb, dx_ref, dy_ref, dg_ref, dgb_ref)

    rows = pl.BlockSpec((TM, D), _rowmap(nt, 0))
    if comb:
        dres_spec = pl.BlockSpec((TM, D), lambda b, i: (b * ntl + jnp.maximum(i - 1, 0), 0))
    else:
        dres_spec = rows
    specs, shapes = _resid_out_shapes(B, nt, comb)
    return pl.pallas_call(
        body, name=name, grid=(B, nt),
        in_specs=[rows, pl.BlockSpec((TM, D), _rowmap(*lxi)), dres_spec, pl.BlockSpec((TM, D), _rowmap(*lxp)), rows,
                  pl.BlockSpec((1, 1, N_MOD, D), _mmap(comb)), pl.BlockSpec((2, D), lambda b, i: (0, 0))],
        out_specs=specs + [pl.BlockSpec((1, 1, 2, D), _mmap(comb))],
        out_shape=shapes + [jax.ShapeDtypeStruct((B, 2, 2, D), F32)],
        compiler_params=_cp(2))(dh, Xi, dres, Xp, Yp, mvec, gb)


def _ffn_out_dx(dy, W, u, name, dep=None):
    M = dy.shape[0]
    half = DFF // 2
    deps = [] if dep is None else [dep]

    def body(dy_ref, w_ref, u_ref, *rest):
        du_ref = rest[-1]
        dyv = dy_ref[...]
        for j in range(2):
            lo, hi = j * half, (j + 1) * half
            da = _dot_nt(dyv, w_ref[lo:hi, :])
            g = u_ref[:, lo:hi].astype(F32)
            up = u_ref[:, DFF + lo:DFF + hi].astype(F32)
            s = _sigmoid(g)
            du_ref[:, lo:hi] = (da * up * (s * (1.0 + g * (1.0 - s)))).astype(BF16)
            du_ref[:, DFF + lo:DFF + hi] = (da * (g * s)).astype(BF16)

    return pl.pallas_call(
        body, name=name, grid=(M // TM,),
        in_specs=[pl.BlockSpec((TM, D), lambda i: (i, 0)), _wspec(W, 1), pl.BlockSpec((TM, 2 * DFF), lambda i: (i, 0))]
        + [_wspec(d, 1) for d in deps],
        out_specs=pl.BlockSpec((TM, 2 * DFF), lambda i: (i, 0)),
        out_shape=jax.ShapeDtypeStruct((M, 2 * DFF), BF16), compiler_params=_cp(1))(dy, W, u, *deps)


def _wspec(W, nidx):
    zeros = (0,) * W.ndim
    if nidx == 1:
        return pl.BlockSpec(W.shape, lambda i: zeros)
    return pl.BlockSpec(W.shape, lambda b, i: zeros)


def _mm_nn(A, la, W, B, nt, out_dtype, name):
    K, N = W.shape

    def body(a_ref, w_ref, o_ref):
        o_ref[...] = _dot(a_ref[...], w_ref[...]).astype(out_dtype)

    return pl.pallas_call(
        body, name=name, grid=(B, nt),
        in_specs=[pl.BlockSpec((TM, K), _rowmap(*la)), _wspec(W, 2)],
        out_specs=pl.BlockSpec((TM, N), _rowmap(nt, 0)),
        out_shape=jax.ShapeDtypeStruct((B * nt * TM, N), out_dtype), compiler_params=_cp(2))(A, W)


def _ffn_in(A, la, W3, B, nt, name):
    K, n = W3.shape[1:]

    def body(a_ref, w_ref, u_ref, s_ref):
        a = a_ref[...]
        for j in range(2):
            g = _dot(a, w_ref[j])
            up = _dot(a, w_ref[j + 2])
            u_ref[:, j * n:(j + 1) * n] = g.astype(BF16)
            u_ref[:, (j + 2) * n:(j + 3) * n] = up.astype(BF16)
            s_ref[:, j * n:(j + 1) * n] = (g * _sigmoid(g) * up).astype(BF16)

    rows = B * nt * TM
    return pl.pallas_call(
        body, name=name, grid=(B, nt),
        in_specs=[pl.BlockSpec((TM, K), _rowmap(*la)), _wspec(W3, 2)],
        out_specs=[pl.BlockSpec((TM, 4 * n), _rowmap(nt, 0)), pl.BlockSpec((TM, 2 * n), _rowmap(nt, 0))],
        out_shape=[jax.ShapeDtypeStruct((rows, 4 * n), BF16), jax.ShapeDtypeStruct((rows, 2 * n), BF16)],
        compiler_params=_cp(2))(A, W3)


def _mm_nt(A, W, name, dep=None, out_dtype=F32):
    M, N = A.shape
    K = W.shape[-2]

    def body(a_ref, w_ref, *rest):
        o_ref = rest[-1]
        if W.ndim == 3:
            n = W.shape[-1]
            acc = _dot_nt(a_ref[:, 0:n], w_ref[0])
            for j in range(1, 4):
                acc = acc + _dot_nt(a_ref[:, j * n:(j + 1) * n], w_ref[j])
            o_ref[...] = acc.astype(out_dtype)
        else:
            o_ref[...] = _dot_nt(a_ref[...], w_ref[...]).astype(out_dtype)

    deps = [] if dep is None else [dep]
    return pl.pallas_call(
        body, name=name, grid=(M // TM,),
        in_specs=[pl.BlockSpec((TM, N), lambda i: (i, 0)), _wspec(W, 1)] + [_wspec(d, 1) for d in deps],
        out_specs=pl.BlockSpec((TM, K), lambda i: (i, 0)),
        out_shape=jax.ShapeDtypeStruct((M, K), out_dtype), compiler_params=_cp(1))(A, W, *deps)


def _mm_tn(A, G, name, tn=512, shards=None):
    M, K = A.shape
    N = G.shape[1]
    tk = next(t for t in ((2048, 1536, 1024, 512) if K <= D else (1024, 512)) if M % t == 0)
    if shards:
        tn = N // shards

    def body(a_ref, g_ref, o_ref):
        @pl.when(pl.program_id(1) == 0)
        def _():
            o_ref[...] = jnp.zeros_like(o_ref)

        upd = _dot_tn(a_ref[...], g_ref[...])
        if shards:
            o_ref[0] += upd
        else:
            o_ref[...] += upd

    if shards:
        out_spec = pl.BlockSpec((1, K, tn), lambda n, k: (n, 0, 0))
        out_shape = jax.ShapeDtypeStruct((shards, K, tn), F32)
    else:
        out_spec = pl.BlockSpec((K, tn), lambda n, k: (0, n))
        out_shape = jax.ShapeDtypeStruct((K, N), F32)
    return pl.pallas_call(
        body, name=name, grid=(N // tn, M // tk),
        in_specs=[pl.BlockSpec((tk, K), lambda n, k: (k, 0)), pl.BlockSpec((tk, tn), lambda n, k: (k, n))],
        out_specs=out_spec, out_shape=out_shape, compiler_params=_cp(2))(A, G)


def _logsig(z):
    return jnp.minimum(z, 0.0) - jnp.log(1.0 + jnp.exp(-jnp.abs(z)))


ZW = 2688
ZRQ, ZRFF, ZRFB, ZLR = 1024, 1536, 2048, 2560


def _mix_in_features(H, W, a2p, biasp, lbp, B, nt):
    K = W.shape[0]

    def body(h_ref, w_ref, a2_ref, bias_ref, lb_ref, q_ref, k_ref, v_ref, g_ref, z_ref):
        hv = h_ref[...]
        proj = lambda lo, hi: _dot(hv, w_ref[:, lo:hi])
        z_ref[:, 0:2 * 4 * HD] = proj(GG, GG + 2 * 4 * HD)
        lrf = proj(LR, LR + HD)
        z_ref[:, ZLR:ZLR + HD] = lrf
        lr = lrf.astype(BF16)
        lane = lax.broadcasted_iota(jnp.int32, (1, 4 * HD), 1)
        keep = (lane & (HD - 1)) < GLA_DK
        for d in range(2):
            z = _dot(lr, a2_ref[d]) + bias_ref[d:d + 1, :]
            gl = jnp.where(keep, _logsig(z) * (1.0 / GATE_NORM), 0.0)
            for h in range(4):
                g_ref[d, 0, h] = gl[:, h * HD:(h + 1) * HD]
        gq, gk, gv = proj(GQ, GQ + 4 * HD), proj(GK, GK + 4 * HD), proj(GV, GV + 4 * HD)
        for h in range(4):
            sl = slice(h * HD, (h + 1) * HD)
            q_ref[0, h] = gq[:, sl] * (GLA_DK ** -0.5)
            k_ref[0, 0, h] = gk[:, sl]
            k_ref[1, 0, h] = gk[:, sl]
            v_ref[0, h] = gv[:, sl].astype(BF16)
        rqs, ri = proj(RQ, RQ + 4 * HD), proj(RI, RI + 4 * HD)
        z_ref[:, ZRQ:ZRQ + 4 * HD] = rqs
        for h in range(4):
            sl = slice(h * HD, (h + 1) * HD)
            rq = rqs[:, sl]
            q_ref[0, 4 + h] = rq * _sigmoid(rq) * (HD ** -0.5)
            v_ref[0, 4 + h] = ri[:, sl].astype(BF16)
        for d, off, zoff in ((0, RFF, ZRFF), (1, RFB, ZRFB)):
            rf = proj(off, off + 4 * HD)
            z_ref[:, zoff:zoff + 4 * HD] = rf
            for h in range(4):
                sl = slice(h * HD, (h + 1) * HD)
                lb = lb_ref[d:d + 1, sl]
                f = lb + (1.0 - lb) * _sigmoid(rf[:, sl])
                g_ref[d, 0, 4 + h] = jnp.log(f)
                k_ref[d, 0, 4 + h] = 1.0 - f

    one = pl.BlockSpec((1, NH, TM, HD), lambda b, i: (b, 0, i, 0))
    two = pl.BlockSpec((2, 1, NH, TM, HD), lambda b, i: (0, b, 0, i, 0))
    s1 = jax.ShapeDtypeStruct((B, NH, nt * TM, HD), F32)
    s2 = jax.ShapeDtypeStruct((2, B, NH, nt * TM, HD), F32)
    return pl.pallas_call(
        body, name="mix_in_features", grid=(B, nt),
        in_specs=[pl.BlockSpec((TM, K), _rowmap(nt, 0)), _wspec(W, 2), pl.BlockSpec((2, HD, 4 * HD), lambda b, i: (0, 0, 0)),
                  pl.BlockSpec((2, 4 * HD), lambda b, i: (0, 0)), pl.BlockSpec((2, 4 * HD), lambda b, i: (0, 0))],
        out_specs=[one, two, one, two, pl.BlockSpec((TM, ZW), _rowmap(nt, 0))],
        out_shape=[s1, s2, jax.ShapeDtypeStruct(s1.shape, BF16), s2, jax.ShapeDtypeStruct((B * nt * TM, ZW), F32)],
        compiler_params=_cp(2))(H, W, a2p, biasp, lbp)


def _features_bwd(Z, a2p, biasp, lbp, dQ0, dQ1, dK0, dK1, dV0, dV1, dG0, dG1, dgates, B):
    nt = Z.shape[0] // (B * TM)

    def body(z_ref, a2_ref, bias_ref, lb_ref, dq0, dq1, dk0, dk1, dv0, dv1, dg0, dg1, dgt_ref,
             df_ref, da2_ref, dbias_ref, dlb_ref):
        b_, i = pl.program_id(0), pl.program_id(1)

        @pl.when((b_ == 0) & (i == 0))
        def _():
            da2_ref[...] = jnp.zeros_like(da2_ref)
            dbias_ref[...] = jnp.zeros_like(dbias_ref)
            dlb_ref[...] = jnp.zeros_like(dlb_ref)

        df_ref[:, 0:2 * 4 * HD] = jnp.where(i > 0, dgt_ref[...], 0.0).astype(BF16)
        df_ref[:, LR + HD:] = jnp.zeros((TM, MIXP - LR - HD), BF16)
        lr = z_ref[:, ZLR:ZLR + HD].astype(BF16)
        lane = lax.broadcasted_iota(jnp.int32, (1, 4 * HD), 1)
        keep = (lane & (HD - 1)) < GLA_DK
        dlr = jnp.zeros((TM, HD), F32)
        dgs = (dg0, dg1)
        dks = (dk0, dk1)
        rd = lambda ref, h: ref[0, h].astype(F32)
        for d in range(2):
            z = _dot(lr, a2_ref[d]) + bias_ref[d:d + 1, :]
            dgl = jnp.concatenate([rd(dgs[d], h) for h in range(4)], axis=1)
            dz = jnp.where(keep, dgl * (1.0 / GATE_NORM) * (1.0 - _sigmoid(z)), 0.0)
            dzb = dz.astype(BF16)
            dlr = dlr + _dot_nt(dzb, a2_ref[d])
            da2_ref[d] += _dot_tn(lr, dzb)
            dbias_ref[d:d + 1, :] += _rsum(dz)
        df_ref[:, LR:LR + HD] = dlr.astype(BF16)
        for h in range(4):
            df_ref[:, GQ + h * HD:GQ + (h + 1) * HD] = ((rd(dq0, h) + rd(dq1, h)) * (GLA_DK ** -0.5)).astype(BF16)
            df_ref[:, GK + h * HD:GK + (h + 1) * HD] = (rd(dk0, h) + rd(dk1, h)).astype(BF16)
            df_ref[:, GV + h * HD:GV + (h + 1) * HD] = (rd(dv0, h) + rd(dv1, h)).astype(BF16)
        for h in range(4):
            sl = slice(h * HD, (h + 1) * HD)
            rq = z_ref[:, ZRQ + h * HD:ZRQ + (h + 1) * HD]
            s = _sigmoid(rq)
            dqh = rd(dq0, 4 + h) + rd(dq1, 4 + h)
            df_ref[:, RQ + h * HD:RQ + (h + 1) * HD] = (dqh * (HD ** -0.5) * (s * (1.0 + rq * (1.0 - s)))).astype(BF16)
            df_ref[:, RI + h * HD:RI + (h + 1) * HD] = (rd(dv0, 4 + h) + rd(dv1, 4 + h)).astype(BF16)
            for d, off, zoff in ((0, RFF, ZRFF), (1, RFB, ZRFB)):
                lb = lb_ref[d:d + 1, sl]
                sg = _sigmoid(z_ref[:, zoff + h * HD:zoff + (h + 1) * HD])
                f = lb + (1.0 - lb) * sg
                dff = rd(dgs[d], 4 + h) / f - rd(dks[d], 4 + h)
                df_ref[:, off + h * HD:off + (h + 1) * HD] = (dff * (1.0 - lb) * sg * (1.0 - sg)).astype(BF16)
                dlb_ref[d:d + 1, sl] += _rsum(dff * (1.0 - sg))

    m0 = lambda b, i: (b, 0, i, 0)
    one = lambda m: pl.BlockSpec((1, NH, TM, HD), m)
    return pl.pallas_call(
        body, name="mix_features_bwd", grid=(B, nt),
        in_specs=[pl.BlockSpec((TM, ZW), _rowmap(nt, 0)), pl.BlockSpec((2, HD, 4 * HD), lambda b, i: (0, 0, 0)),
                  pl.BlockSpec((2, 4 * HD), lambda b, i: (0, 0)), pl.BlockSpec((2, 4 * HD), lambda b, i: (0, 0)),
                  one(m0), one(m0), one(m0), one(m0), one(m0), one(m0), one(m0), one(m0),
                  pl.BlockSpec((TM, D), lambda b, i: (b * (nt - 1) + jnp.maximum(i - 1, 0), 0))],
        out_specs=[pl.BlockSpec((TM, MIXP), _rowmap(nt, 0)), pl.BlockSpec((2, HD, 4 * HD), lambda b, i: (0, 0, 0)),
                   pl.BlockSpec((2, 4 * HD), lambda b, i: (0, 0)), pl.BlockSpec((2, 4 * HD), lambda b, i: (0, 0))],
        out_shape=[jax.ShapeDtypeStruct((B * nt * TM, MIXP), BF16), jax.ShapeDtypeStruct((2, HD, 4 * HD), F32),
                   jax.ShapeDtypeStruct((2, 4 * HD), F32), jax.ShapeDtypeStruct((2, 4 * HD), F32)],
        compiler_params=_cp(2))(Z, a2p, biasp, lbp, dQ0, dQ1, dK0, dK1, dV0, dV1, dG0, dG1, dgates)


def _chunk_scan(x, rin, fwd):
    acc = x
    sft = 1
    while sft < CH:
        if fwd:
            acc = acc + jnp.where(rin >= sft, pltpu.roll(acc, sft, 0), 0.0)
        else:
            acc = acc + jnp.where(rin < CH - sft, pltpu.roll(acc, TM - sft, 0), 0.0)
        sft *= 2
    return acc


def _chunk_total(x):
    t = jnp.sum(x.reshape(NCB, CH, HD), axis=1, keepdims=True)
    return jnp.broadcast_to(t, (NCB, CH, HD)).reshape(TM, HD)


def _scan_masks(rev):
    rin = lax.broadcasted_iota(jnp.int32, (TM, HD), 0) & (CH - 1)
    ri = lax.broadcasted_iota(jnp.int32, (SB, SB), 0)
    ci = lax.broadcasted_iota(jnp.int32, (SB, SB), 1)
    same = (ri >> 5) == (ci >> 5)
    lo = same & (ri >= ci)
    up = same & (ri <= ci)
    mask, maskT = (up, lo) if rev else (lo, up)
    re = lax.broadcasted_iota(jnp.int32, (SB, CSB * HD), 0) >> 5
    ce = lax.broadcasted_iota(jnp.int32, (SB, CSB * HD), 1) >> 7
    return rin, mask, maskT, re == ce


def _scan_decay(q, k, g, rin, rev):
    b = _chunk_scan(g, rin, not rev)
    xx = _chunk_total(g) - b
    eb = jnp.exp(b)
    return b, xx, eb, q * eb, k * jnp.exp(-b), k * jnp.exp(xx)


def _sub(x, s):
    return x[s * SB:(s + 1) * SB]


def _expand(xb, mexp):
    return jnp.where(mexp, jnp.concatenate([xb] * CSB, axis=1), jnp.zeros((), xb.dtype))


def _own(x, mexp):
    xm = jnp.where(mexp, x, 0.0)
    acc = xm[:, 0:HD]
    for n in range(1, CSB):
        acc = acc + xm[:, n * HD:(n + 1) * HD]
    return acc


def _stack(per_chunk, s):
    return jnp.concatenate(per_chunk[s * CSB:(s + 1) * CSB], axis=1)


def _state_pass(s0, eb, uts, rev):
    order = range(NCB - 1, -1, -1) if rev else range(NCB)
    states = [None] * NCB
    s = s0
    for n in order:
        row = n * CH if rev else n * CH + CH - 1
        states[n] = s
        s = eb[row:row + 1, :] * s + uts[n // CSB][:, (n % CSB) * HD:(n % CSB + 1) * HD]
    return states, s


def _scan_fwd(Q, K, V, G, rev, B):
    nb = Q.shape[2] // TM
    d = 1 if rev else 0
    rmap = (lambda s: jnp.where(s == 0, 0, nb - s)) if rev else (lambda s: s)

    def body(q_ref, k_ref, v_ref, g_ref, o_ref, st_ref, s_scr):
        @pl.when(pl.program_id(2) == 0)
        def _():
            s_scr[...] = jnp.zeros_like(s_scr)

        rin, mask, _, mexp = _scan_masks(rev)

        def head(p, readout):
            s0 = s_scr[p]
            st_ref[0, p, 0] = s0
            _, _, eb, qd, ki, kt = _scan_decay(q_ref[0, p], k_ref[0, 0, p], g_ref[0, 0, p], rin, rev)
            ktb, vb = kt.astype(BF16), v_ref[0, p]
            uts = [_dot_tn(_sub(vb, s), _expand(_sub(ktb, s), mexp)) for s in range(NSB)]
            states, s_new = _state_pass(s0, eb, uts, rev)
            s_scr[p] = s_new
            if not readout:
                return
            qb, kib = qd.astype(BF16), ki.astype(BF16)
            for s in range(NSB):
                a = jnp.where(mask, _dot_nt(_sub(qb, s), _sub(kib, s)), 0.0)
                o_ref[0, p, s * SB:(s + 1) * SB, :] = (
                    _dot(a.astype(BF16), _sub(vb, s))
                    + _dot_nt(_expand(_sub(qb, s), mexp), _stack(states, s).astype(BF16)))

        @pl.when(pl.program_id(2) >= 1)
        def _():
            for p in range(HP):
                head(p, True)

        @pl.when(pl.program_id(2) == 0)
        def _():
            for p in range(HP):
                head(p, False)

    one = pl.BlockSpec((1, HP, TM, HD), lambda b, h, s: (b, h, rmap(s), 0))
    two = pl.BlockSpec((1, 1, HP, TM, HD), lambda b, h, s: (d, b, h, rmap(s), 0))
    return pl.pallas_call(
        body, name="scan_fwd_rev" if rev else "scan_fwd", grid=(B, NH // HP, nb),
        in_specs=[one, two, one, two],
        out_specs=[one, pl.BlockSpec((1, HP, 1, HD, HD), lambda b, h, s: (b, h, s, 0, 0))],
        out_shape=[jax.ShapeDtypeStruct(Q.shape, F32), jax.ShapeDtypeStruct((B, NH, nb, HD, HD), F32)],
        scratch_shapes=[pltpu.VMEM((HP, HD, HD), F32)],
        compiler_params=_cp(3))(Q, K, V, G)


def _scan_bwd(Q, K, V, G, St, dO, rev, B):
    nb = Q.shape[2] // TM
    d = 1 if rev else 0
    smap = lambda t: nb - 1 - t
    rmap = (lambda t: jnp.where(smap(t) == 0, 0, nb - smap(t))) if rev else smap

    def body(q_ref, k_ref, v_ref, g_ref, st_ref, do_ref, dq_ref, dk_ref, dv_ref, dg_ref, ds_scr):
        t = pl.program_id(2)

        @pl.when(t == 0)
        def _():
            ds_scr[...] = jnp.zeros_like(ds_scr)

        is_lat = smap(t) >= 1
        rin, mask, maskT, mexp = _scan_masks(rev)
        for p in range(HPB):
            b, xx, eb, qd, ki, kt = _scan_decay(q_ref[0, p], k_ref[0, 0, p], g_ref[0, 0, p], rin, rev)
            qb, kib, ktb, vb = qd.astype(BF16), ki.astype(BF16), kt.astype(BF16), v_ref[0, p]
            dob = jnp.where(is_lat, do_ref[0, p], 0.0).astype(BF16)
            kt_exps = [_expand(_sub(ktb, s), mexp) for s in range(NSB)]
            uts = [_dot_tn(_sub(vb, s), kt_exps[s]) for s in range(NSB)]
            states, _ = _state_pass(st_ref[0, p, 0], eb, uts, rev)
            gts = [_dot_tn(_sub(dob, s), _expand(_sub(qb, s), mexp)) for s in range(NSB)]
            order = range(NCB) if rev else range(NCB - 1, -1, -1)
            dsp = [None] * NCB
            t2 = [None] * NCB
            dsc = ds_scr[p]
            for n in order:
                row = n * CH if rev else n * CH + CH - 1
                ebl = eb[row:row + 1, :]
                dsp[n] = dsc
                t2[n] = jnp.broadcast_to(ebl * _rsum(states[n] * dsc), (CH, HD))
                dsc = gts[n // CSB][:, (n % CSB) * HD:(n % CSB + 1) * HD] + ebl * dsc
            ds_scr[p] = dsc
            dqds, dkis, dkts = [], [], []
            for s in range(NSB):
                q_s, ki_s, v_s, do_s = _sub(qb, s), _sub(kib, s), _sub(vb, s), _sub(dob, s)
                dspb = _stack(dsp, s).astype(BF16)
                da = jnp.where(mask, _dot_nt(do_s, v_s), 0.0).astype(BF16)
                dat = jnp.where(maskT, _dot_nt(v_s, do_s), 0.0).astype(BF16)
                at = jnp.where(maskT, _dot_nt(ki_s, q_s), 0.0).astype(BF16)
                dqds.append(_dot(da, ki_s) + _own(_dot(do_s, _stack(states, s).astype(BF16)), mexp))
                dkis.append(_dot(dat, q_s))
                dv_ref[0, p, s * SB:(s + 1) * SB, :] = (_dot(at, do_s) + _dot_nt(kt_exps[s], dspb)).astype(BF16)
                dkts.append(_own(_dot(v_s, dspb), mexp))
            dqd, dki, dkt = (jnp.concatenate(parts, axis=0) for parts in (dqds, dkis, dkts))
            z = dkt * kt
            db = dqd * qd - dki * ki
            dq_ref[0, p] = (dqd * eb).astype(BF16)
            dk_ref[0, p] = (dki * jnp.exp(-b) + dkt * jnp.exp(xx)).astype(BF16)
            dg_ref[0, p] = (_chunk_total(db) + (db - z) + _chunk_scan(z - db, rin, not rev)
                            + jnp.concatenate(t2, axis=0)).astype(BF16)

    one = pl.BlockSpec((1, HPB, TM, HD), lambda b, h, t: (b, h, rmap(t), 0))
    two = pl.BlockSpec((1, 1, HPB, TM, HD), lambda b, h, t: (d, b, h, rmap(t), 0))
    lat = pl.BlockSpec((1, HPB, TM, HD), lambda b, h, t: (b, h, jnp.clip(rmap(t) - 1, 0, nb - 2), 0))
    shp = jax.ShapeDtypeStruct(Q.shape, BF16)
    return pl.pallas_call(
        body, name="scan_bwd_rev" if rev else "scan_bwd", grid=(B, NH // HPB, nb),
        in_specs=[one, two, one, two, pl.BlockSpec((1, HPB, 1, HD, HD), lambda b, h, t: (b, h, smap(t), 0, 0)), lat],
        out_specs=[one, one, one, one], out_shape=[shp, shp, shp, shp],
        scratch_shapes=[pltpu.VMEM((HPB, HD, HD), F32)],
        compiler_params=_cp(3))(Q, K, V, G, St, dO)


def _gnorm(O0, O1, F, gains, B, ntl):
    nt = ntl + 1

    def body(o0_ref, o1_ref, f_ref, gn_ref, m_ref):
        for h in range(NH):
            o = o0_ref[0, h] + o1_ref[0, h]
            r = lax.rsqrt(jnp.mean(o * o, axis=-1, keepdims=True) + NORM_EPS)
            gn = gn_ref[0:1, :] if h < 4 else gn_ref[1:2, :]
            gt = f_ref[:, h * HD:(h + 1) * HD]
            m_ref[:, h * HD:(h + 1) * HD] = (o * r * gn * (gt * _sigmoid(gt))).astype(BF16)

    ospec = pl.BlockSpec((1, NH, TM, HD), lambda b, i: (b, 0, i + 1, 0))
    return pl.pallas_call(
        body, name="gated_norm", grid=(B, ntl),
        in_specs=[ospec, ospec, pl.BlockSpec((TM, D), lambda b, i: (b * nt + 1 + i, 0)),
                  pl.BlockSpec((2, HD), lambda b, i: (0, 0))],
        out_specs=pl.BlockSpec((TM, D), _rowmap(ntl, 0)),
        out_shape=jax.ShapeDtypeStruct((B * ntl * TM, D), BF16), compiler_params=_cp(2))(O0, O1, F, gains)


def _gnorm_bwd(dM, O0, O1, F, gains, B, ntl):
    nt = ntl + 1

    def body(dm_ref, o0_ref, o1_ref, f_ref, gn_ref, do_ref, dgt_ref, dgn_ref):
        b_, i = pl.program_id(0), pl.program_id(1)

        @pl.when((b_ == 0) & (i == 0))
        def _():
            dgn_ref[...] = jnp.zeros_like(dgn_ref)

        for h in range(NH):
            o = o0_ref[0, h] + o1_ref[0, h]
            r = lax.rsqrt(jnp.mean(o * o, axis=-1, keepdims=True) + NORM_EPS)
            y = o * r
            gn = gn_ref[0:1, :] if h < 4 else gn_ref[1:2, :]
            gt = f_ref[:, h * HD:(h + 1) * HD]
            s = _sigmoid(gt)
            dm = dm_ref[:, h * HD:(h + 1) * HD].astype(F32)
            don = dm * (gt * s)
            dgt_ref[:, h * HD:(h + 1) * HD] = (dm * (y * gn) * (s * (1.0 + gt * (1.0 - s)))).astype(BF16)
            row = 0 if h < 4 else 1
            dgn_ref[row:row + 1, :] += _rsum(don * y)
            dy = don * gn
            do_ref[0, h] = (r * (dy - y * jnp.mean(dy * y, axis=-1, keepdims=True))).astype(BF16)

    ospec = pl.BlockSpec((1, NH, TM, HD), lambda b, i: (b, 0, i + 1, 0))
    return pl.pallas_call(
        body, name="gated_norm_bwd", grid=(B, ntl),
        in_specs=[pl.BlockSpec((TM, D), _rowmap(ntl, 0)), ospec, ospec,
                  pl.BlockSpec((TM, D), lambda b, i: (b * nt + 1 + i, 0)), pl.BlockSpec((2, HD), lambda b, i: (0, 0))],
        out_specs=[pl.BlockSpec((1, NH, TM, HD), lambda b, i: (b, 0, i, 0)), pl.BlockSpec((TM, D), _rowmap(ntl, 0)),
                   pl.BlockSpec((2, HD), lambda b, i: (0, 0))],
        out_shape=[jax.ShapeDtypeStruct((B, NH, ntl * TM, HD), BF16), jax.ShapeDtypeStruct((B * ntl * TM, D), BF16),
                   jax.ShapeDtypeStruct((2, HD), F32)],
        compiler_params=_cp(2))(dM, O0, O1, F, gains)


def _sincos_2d(rows, width, dim):
    quarter = dim // 4
    omega = 1.0 / 10000.0 ** (jnp.arange(quarter, dtype=F32) / quarter)

    def emb(n):
        a = jnp.arange(n).astype(F32)[:, None] * omega[None, :]
        return jnp.concatenate([jnp.sin(a), jnp.cos(a)], axis=-1)

    er = jnp.broadcast_to(emb(rows)[:, None, :], (rows, width, dim // 2))
    ec = jnp.broadcast_to(emb(width)[None, :, :], (rows, width, dim // 2))
    return jnp.concatenate([er, ec], axis=-1).reshape(rows * width, dim)


def _pad_heads(w):
    k = w.shape[0]
    return jnp.pad(w.reshape(k, 4, GLA_DK), ((0, 0), (0, 0), (0, HD - GLA_DK))).reshape(k, 4 * HD)


def _unpad_heads(w):
    k = w.shape[0]
    return w.reshape(k, 4, HD)[:, :, :GLA_DK].reshape(k, 4 * GLA_DK)


MIX_N = 1032
MIX_NP = 1152
_SEGS = ([(64 * h, 64, GQ + HD * h) for h in range(4)] + [(256 + 64 * h, 64, GK + HD * h) for h in range(4)]
         + [(512, 512, GV), (1024, 512, GG), (1536, 32, LR), (1568, 512, RQ), (2080, 512, RFF), (2592, 512, RFB),
            (3104, 512, RI), (3616, 512, RG)])


def _mix_in_to_padded(ps):
    k = ps.shape[1]
    parts, pos = [], 0
    for g0, ln, s0 in sorted(_SEGS, key=lambda s: s[2]):
        if s0 > pos:
            parts.append(jnp.zeros((k, s0 - pos), ps.dtype))
        for j in range(4):
            lo, hi = max(g0, j * MIX_N), min(g0 + ln, (j + 1) * MIX_N)
            if lo < hi:
                parts.append(ps[j][:, lo - j * MIX_N:hi - j * MIX_N])
        pos = s0 + ln
    parts.append(jnp.zeros((k, MIXP - pos), ps.dtype))
    return jnp.concatenate(parts, axis=1)


def _mix_in_from_padded(g):
    k = g.shape[0]
    shards = []
    for j in range(4):
        parts = []
        for g0, ln, s0 in sorted(_SEGS):
            lo, hi = max(g0, j * MIX_N), min(g0 + ln, (j + 1) * MIX_N)
            if lo < hi:
                parts.append(g[:, s0 + lo - g0:s0 + hi - g0])
        parts.append(jnp.zeros((k, MIX_NP - MIX_N), g.dtype))
        shards.append(jnp.concatenate(parts, axis=1))
    return jnp.stack(shards)


def _local_step(x, ctx, tgt, mvec, weights_for, ln_gain, ln_bias, a2f, a2b, abf, abb, lb, gng, gnh, on_grads, on_sent):
    B, T, _ = x.shape
    assert ctx.shape[1] == TM and T % TM == 0
    ntl = T // TM
    nt = ntl + 1
    C, L, CL = (nt, 0), (ntl, 0), (nt, 1)
    pos = _sincos_2d(T // 64, 64, D)
    gbs = [jnp.stack([ln_gain[i], ln_bias[i]]) for i in range(3)]
    a2p = jnp.zeros((2, HD, 4 * HD), F32)
    a2p = a2p.at[0, 0:16].set(_pad_heads(a2f)).at[1, 16:32].set(_pad_heads(a2b)).astype(BF16)
    biasp = jnp.concatenate([_pad_heads(abf.reshape(1, -1)), _pad_heads(abb.reshape(1, -1))], axis=0)
    gains = jnp.concatenate([gng.reshape(1, HD), gnh.reshape(1, HD)], axis=0)

    X0, h0 = _embed_lnmod(x, ctx, pos, mvec)
    w1i, w1o = weights_for("ffn1", h0)
    u0, a0 = _ffn_in(h0, C, w1i, B, nt, "ffn1_in")
    wmp, wmo = weights_for("mix", a0)
    y0 = _mm_nn(a0, C, w1o, B, nt, BF16, "ffn1_out")
    X1, h1 = _resid_lnmod(X0, C, y0, mvec, True, 0, 0.5, gbs[0], B, nt, "resid0_lnmod1")
    Q, K, V, G, Zm = _mix_in_features(h1, wmp, a2p, biasp, lb, B, nt)
    w2i, w2o = weights_for("ffn2", Zm)
    O0, S0 = _scan_fwd(Q, K, V, G, False, B)
    O1, S1 = _scan_fwd(Q, K, V, G, True, B)
    merged = _gnorm(O0, O1, Zm, gains, B, ntl)
    y1 = _mm_nn(merged, L, wmo, B, ntl, BF16, "mix_out")
    X2, h2 = _resid_lnmod(X1, CL, y1, mvec, False, 1, 1.0, gbs[1], B, ntl, "resid1_lnmod2")
    u2, a2 = _ffn_in(h2, L, w2i, B, ntl, "ffn2_in")
    y2 = _mm_nn(a2, L, w2o, B, ntl, BF16, "ffn2_out")

    dx2r, dy2, dgate2, dgb2, lsum = _tail(X2, y2, mvec, gbs[2], tgt, 2, 0.5, B, ntl)
    loss = (0.5 / D) * jnp.sum(lsum)
    du2 = _ffn_out_dx(dy2, w2o, u2, "ffn2_out_dx")
    g_w2o = _mm_tn(a2, dy2, "ffn2_out_dw")
    dh2 = _mm_nt(du2, w2i, "ffn2_in_dx", out_dtype=BF16)
    g_w2i = _mm_tn(h2, du2, "ffn2_in_dw", shards=4)
    tok = on_grads("ffn2", (g_w2i, g_w2o))
    dx1r, dy1, dgate1, dgb1, dss2 = _lnmod_resid_bwd(dh2, X2, L, dx2r, X1, CL, y1, mvec, False, 2, 1.0,
                                                     gbs[1] + tok[0, 0], B, ntl, "lnmod2_resid1_bwd")
    tok = on_sent("ffn2", dy1)
    dmerged = _mm_nt(dy1, wmo, "mix_out_dx", dep=tok, out_dtype=BF16)
    g_wmo = _mm_tn(merged, dy1, "mix_out_dw", tn=D)
    dO, dgates, dgains = _gnorm_bwd(dmerged, O0, O1, Zm, gains, B, ntl)
    dQ0, dK0, dV0, dG0 = _scan_bwd(Q, K, V, G, S0, dO, False, B)
    dQ1, dK1, dV1, dG1 = _scan_bwd(Q, K, V, G, S1, dO, True, B)
    dF, da2p, dbiasp, dlb = _features_bwd(Zm, a2p, biasp, lb, dQ0, dQ1, dK0, dK1, dV0, dV1, dG0, dG1, dgates, B)
    dh1 = _mm_nt(dF, wmp, "mix_in_dx", out_dtype=BF16)
    g_wmp = _mm_tn(h1, dF, "mix_in_dw", tn=MIXP // 4)
    tok = on_grads("mix", (g_wmp, g_wmo))
    dx0r, dy0, dgate0, dgb0, dss1 = _lnmod_resid_bwd(dh1, X1, C, dx1r, X0, C, y0, mvec, True, 1, 0.5,
                                                     gbs[0] + tok[0, 0], B, nt, "lnmod1_resid0_bwd")
    tok = on_sent("mix", dy0)
    du0 = _ffn_out_dx(dy0, w1o, u0, "ffn1_out_dx", dep=tok)
    g_w1o = _mm_tn(a0, dy0, "ffn1_out_dw")
    g_w1i = _mm_tn(h0, du0, "ffn1_in_dw", shards=4)
    tok = on_grads("ffn1", (g_w1i, g_w1o))
    dh0 = _mm_nt(du0, w1i, "ffn1_in_dx", dep=tok, out_dtype=BF16)
    tok = on_sent("ffn1", dh0)
    grad_x, dss0 = _lnmod0_bwd(dh0, X0, mvec + tok[0, 0], dx0r, x.shape, B, nt)

    zero_ctx = lambda a: a.at[:, 0].set(0.0)
    dm = jnp.concatenate([dss0, dgate0, dss1, zero_ctx(dgate1), zero_ctx(dss2), zero_ctx(dgate2)], axis=2)
    small = dict(
        ln_gain=jnp.stack([dgb0[0], dgb1[0], dgb2[0]]), ln_bias=jnp.stack([dgb0[1], dgb1[1], dgb2[1]]),
        a2f=_unpad_heads(da2p[0, 0:16]), a2b=_unpad_heads(da2p[1, 16:32]),
        abf=_unpad_heads(dbiasp[0:1]), abb=_unpad_heads(dbiasp[1:2]), lb=dlb, gng=dgains[0], gnh=dgains[1])
    return loss, grad_x, dm, small


def _small_allgather(xs, name):
    r, n = xs.shape

    def body(x_ref, out_ref, send_sems, recv_sems, local_sem):
        x, y, c = lax.axis_index("x"), lax.axis_index("y"), lax.axis_index("c")
        me, sibling = (x, y, c), (x, y, 1 - c)
        chips = [(1 - x, y), (x, 1 - y), (1 - x, 1 - y)]

        def rows(px, py, pc):
            return out_ref.at[pl.ds((4 * px + 2 * py + pc) * r, r), :]

        def copy(k, block, to, src=None):
            return pltpu.make_async_remote_copy(
                src_ref=rows(*block) if src is None else src, dst_ref=rows(*block),
                send_sem=send_sems.at[k], recv_sem=recv_sems.at[k], device_id=to, device_id_type=MESH)

        mine = pltpu.make_async_copy(x_ref, rows(*me), local_sem)
        mine.start()
        first = [copy(0, me, sibling, src=x_ref)]
        first += [copy(1 + j, me, (*chip, c), src=x_ref) for j, chip in enumerate(chips)]
        for cp in first:
            cp.start()
        passed = [copy(4 + j, (*chip, c), sibling) for j, chip in enumerate(chips)]
        for j, chip in enumerate(chips):
            copy(1 + j, (*chip, c), me).wait_recv()
            passed[j].start()
        copy(0, sibling, me).wait_recv()
        for j, chip in enumerate(chips):
            copy(4 + j, (*chip, 1 - c), me).wait_recv()
        for cp in first + passed:
            cp.wait_send()
        mine.wait()

    out = pl.pallas_call(
        body, name=name,
        out_shape=jax.ShapeDtypeStruct((8 * r, n), xs.dtype),
        in_specs=[pl.BlockSpec(memory_space=pltpu.VMEM)],
        out_specs=pl.BlockSpec(memory_space=pltpu.VMEM),
        scratch_shapes=[pltpu.SemaphoreType.DMA((7,)), pltpu.SemaphoreType.DMA((7,)), pltpu.SemaphoreType.DMA],
        compiler_params=pltpu.CompilerParams(vmem_limit_bytes=VMEM_LIMIT))(xs)
    return out.reshape(8, r, n)


def _gather_flat(v, name):
    n = v.shape[0]
    npad = -(-n // 1024) * 1024
    g = _small_allgather(jnp.pad(v, (0, npad - n)).reshape(8, npad // 8), name)
    return g.reshape(8, npad)[:, :n]


HBM_SPEC = pl.BlockSpec(memory_space=pltpu.HBM)
SEM_SPEC = pl.BlockSpec(memory_space=pltpu.SEMAPHORE)
DATAFLOW = pltpu.SideEffectType.DATAFLOW_SIDE_EFFECTING


def _gather_copies(xs, outs, send_sems, recv_sems):
    x, y, c = lax.axis_index("x"), lax.axis_index("y"), lax.axis_index("c")
    dests = [(x, y, 1 - c), (1 - x, y, c), (x, 1 - y, c), (1 - x, 1 - y, c)]
    return [pltpu.make_async_remote_copy(
        src_ref=xs[w], dst_ref=outs[w].at[4 * x + 2 * y + c], send_sem=send_sems[4 * w + k],
        recv_sem=recv_sems[4 * w + k], device_id=dests[k], device_id_type=MESH)
        for w in range(len(xs)) for k in range(4)]


def _split_start(copies, per_w, srcs, land_lead, after, name):
    n = len(srcs)
    m = per_w * n
    lands = [lax.empty((land_lead,) + s.shape[-2:], s.dtype) for s in srcs]
    deps = [] if after is None else [after]

    def body(*refs):
        xs, ls, outs = refs[:n], refs[n:2 * n], refs[2 * n + len(deps):]
        for cp in copies(xs, ls, outs[:m], outs[m:2 * m]):
            cp.start()
        token = outs[2 * m + 2 * n]
        token[...] = jnp.zeros_like(token)

    outs = pl.pallas_call(
        body, name=name,
        out_shape=([pltpu.SemaphoreType.DMA(())] * (2 * m) + [pltpu.HBM(a.shape, a.dtype) for a in srcs + lands]
                   + [jax.ShapeDtypeStruct((8, 128), F32)]),
        in_specs=[HBM_SPEC] * (2 * n) + [pl.BlockSpec(memory_space=pl.ANY)] * len(deps),
        out_specs=[SEM_SPEC] * (2 * m) + [HBM_SPEC] * (2 * n) + [pl.BlockSpec(memory_space=pltpu.VMEM)],
        input_output_aliases={w: 2 * m + w for w in range(2 * n)},
        compiler_params=pltpu.CompilerParams(has_side_effects=DATAFLOW),
    )(*[pltpu.with_memory_space_constraint(a, pltpu.HBM) for a in srcs + lands], *deps)
    return outs[:2 * m], outs[2 * m:2 * m + n], outs[2 * m + n:2 * m + 2 * n], outs[2 * m + 2 * n]


def _split_wait(copies, per_w, sems, x_thru, l_thru, after, name):
    n = len(x_thru)
    m = per_w * n

    def body(*refs):
        xs, ls, ss = refs[:n], refs[n:2 * n], refs[2 * n:2 * n + 2 * m]
        for cp in copies(xs, ls, ss[:m], ss[m:]):
            cp.wait_send()
            cp.wait_recv()

    outs = pl.pallas_call(
        body, name=name,
        out_shape=[pltpu.HBM(a.shape, a.dtype) for a in list(x_thru) + list(l_thru)],
        in_specs=[HBM_SPEC] * (2 * n) + [SEM_SPEC] * (2 * m) + [pl.BlockSpec(memory_space=pl.ANY)],
        out_specs=[HBM_SPEC] * (2 * n),
        input_output_aliases={w: w for w in range(2 * n)},
        compiler_params=pltpu.CompilerParams(has_side_effects=DATAFLOW),
    )(*x_thru, *l_thru, *sems, after)
    return outs[:n], outs[n:]


def _gather_forward(gathered, name):
    n = len(gathered)

    def body(*refs):
        outs = refs[n:2 * n]
        send_sems, recv_sems = refs[2 * n:]
        x, y, c = lax.axis_index("x"), lax.axis_index("y"), lax.axis_index("c")
        chips = [(1 - x, y), (x, 1 - y), (1 - x, 1 - y), (x, y)]

        def copy(w, j, pc):
            px, py = chips[j]
            slot = outs[w].at[4 * px + 2 * py + pc]
            return pltpu.make_async_remote_copy(
                src_ref=slot, dst_ref=slot, send_sem=send_sems.at[4 * w + j], recv_sem=recv_sems.at[4 * w + j],
                device_id=(x, y, 1 - c), device_id_type=MESH)

        sends = [copy(w, j, 1 - c if j == 3 else c) for w in range(n) for j in range(4)]
        for cp in sends:
            cp.start()
        for w in range(n):
            for j in range(4):
                copy(w, j, c if j == 3 else 1 - c).wait_recv()
        for cp in sends:
            cp.wait_send()

    any_spec = pl.BlockSpec(memory_space=pl.ANY)
    return pl.pallas_call(
        body, name=name,
        out_shape=[jax.ShapeDtypeStruct(g.shape, g.dtype) for g in gathered],
        in_specs=[any_spec] * n, out_specs=[any_spec] * n,
        input_output_aliases={w: w for w in range(n)},
        scratch_shapes=[pltpu.SemaphoreType.DMA((4 * n,)), pltpu.SemaphoreType.DMA((4 * n,))],
    )(*gathered)


def _pair_copies(gs, ls, send_sems, recv_sems):
    x, y, c = lax.axis_index("x"), lax.axis_index("y"), lax.axis_index("c")
    return [pltpu.make_async_remote_copy(
        src_ref=gs[w].at[2 * j + 1 - c], dst_ref=ls[w].at[j], send_sem=send_sems[4 * w + j],
        recv_sem=recv_sems[4 * w + j], device_id=(x, y, 1 - c), device_id_type=MESH)
        for w in range(len(gs)) for j in range(4)]


def _all_copies(xs, ls, send_sems, recv_sems):
    x, y, c = lax.axis_index("x"), lax.axis_index("y"), lax.axis_index("c")
    flips = [(a, b, e) for a in (0, 1) for b in (0, 1) for e in (0, 1)][1:]
    return [pltpu.make_async_remote_copy(
        src_ref=xs[0], dst_ref=ls[0].at[4 * x + 2 * y + c], send_sem=send_sems[k], recv_sem=recv_sems[k],
        device_id=(1 - x if a else x, 1 - y if b else y, 1 - c if e else c), device_id_type=MESH)
        for k, (a, b, e) in enumerate(flips)]


def _share_copies(fs, ls, send_sems, recv_sems):
    x, y, c = lax.axis_index("x"), lax.axis_index("y"), lax.axis_index("c")
    return [pltpu.make_async_remote_copy(
        src_ref=fs[w].at[c], dst_ref=fs[w].at[c], send_sem=send_sems[w], recv_sem=recv_sems[w],
        device_id=(x, y, 1 - c), device_id_type=MESH) for w in range(len(fs))]


def _chip_copies(hs, ls, send_sems, recv_sems):
    x, y, c = lax.axis_index("x"), lax.axis_index("y"), lax.axis_index("c")
    chips = [(1 - x, y), (x, 1 - y), (1 - x, 1 - y)]
    return [pltpu.make_async_remote_copy(
        src_ref=hs[w].at[2 * px + py], dst_ref=ls[w].at[k], send_sem=send_sems[3 * w + k],
        recv_sem=recv_sems[3 * w + k], device_id=(px, py, c), device_id_type=MESH)
        for w in range(len(hs)) for k, (px, py) in enumerate(chips)]


def _rs_add_pair(g8, r4, c, name):
    R, n = g8.shape[1:]
    rb = R // 2

    def body(c_ref, g_ref, r_ref, o_ref):
        o_ref[...] = (g_ref[...] + r_ref[...]).astype(BF16)

    spec = pl.BlockSpec((1, rb, n), lambda j, i, c_ref: (j, i, 0))
    return pl.pallas_call(
        body, name=name,
        grid_spec=pltpu.PrefetchScalarGridSpec(
            num_scalar_prefetch=1, grid=(4, R // rb),
            in_specs=[pl.BlockSpec((1, rb, n), lambda j, i, c_ref: (2 * j + c_ref[0], i, 0)), spec],
            out_specs=spec),
        out_shape=jax.ShapeDtypeStruct((4, R, n), BF16), compiler_params=_cp(2))(c, g8, r4)


def _rs_add_chips(g8, r4, r3, cj, name):
    R, n = g8.shape[1:]
    rb = R // 2

    def body(cj_ref, g_ref, p_ref, r_ref, o_ref):
        own = g_ref[0] + p_ref[0]
        o_ref[0] = ((own + r_ref[0].astype(F32)) + r_ref[1].astype(F32)) + r_ref[2].astype(F32)

    return pl.pallas_call(
        body, name=name,
        grid_spec=pltpu.PrefetchScalarGridSpec(
            num_scalar_prefetch=1, grid=(R // rb,),
            in_specs=[pl.BlockSpec((1, rb, n), lambda i, cj_ref: (2 * cj_ref[1] + cj_ref[0], i, 0)),
                      pl.BlockSpec((1, rb, n), lambda i, cj_ref: (cj_ref[1], i, 0)),
                      pl.BlockSpec((3, rb, n), lambda i, cj_ref: (0, i, 0))],
            out_specs=pl.BlockSpec((1, rb, n), lambda i, cj_ref: (cj_ref[0], i, 0))),
        out_shape=jax.ShapeDtypeStruct((2, R, n), F32), compiler_params=_cp(1))(cj, g8, r4, r3)


def _sum8(g):
    n = g.shape[1]

    def body(g_ref, o_ref):
        acc = g_ref[0:1, :]
        for k in range(1, 8):
            acc = acc + g_ref[k:k + 1, :]
        o_ref[...] = acc

    return pl.pallas_call(body, name="sum_devices", out_shape=jax.ShapeDtypeStruct((1, n), F32),
                          compiler_params=pltpu.CompilerParams(vmem_limit_bytes=VMEM_LIMIT))(g)


ADA_ROWS = 64


def _ada_fwd(cs, w, b):
    n = w.shape[1]

    def body(c_ref, w_ref, b_ref, o_ref):
        cv = c_ref[...]
        s = (cv * _sigmoid(cv)).astype(BF16)
        o_ref[...] = _dot(s, w_ref[...].astype(BF16)) + b_ref[...]

    return pl.pallas_call(body, name="ada_fwd", out_shape=jax.ShapeDtypeStruct((ADA_ROWS, n), F32),
                          compiler_params=pltpu.CompilerParams(vmem_limit_bytes=VMEM_LIMIT))(cs, w, b)


def _mod_exchange(m_cols):
    n = m_cols.shape[1]

    def body(m_ref, out_ref, send_sems, recv_sems, local_sem):
        x, y, c = lax.axis_index("x"), lax.axis_index("y"), lax.axis_index("c")
        chips = [(1 - x, y), (x, 1 - y), (1 - x, 1 - y)]

        def group(px, py):
            return m_ref.at[pl.ds(pl.multiple_of(8 * (4 * px + 2 * py + c), 8), 8), :]

        mine = pltpu.make_async_copy(group(x, y), out_ref.at[2 * x + y], local_sem)
        mine.start()
        cps = [pltpu.make_async_remote_copy(
            src_ref=group(px, py), dst_ref=out_ref.at[2 * x + y], send_sem=send_sems.at[k], recv_sem=recv_sems.at[k],
            device_id=(px, py, c), device_id_type=MESH) for k, (px, py) in enumerate(chips)]
        for cp in cps:
            cp.start()
        for cp in cps:
            cp.wait_recv()
        for cp in cps:
            cp.wait_send()
        mine.wait()

    return pl.pallas_call(
        body, name="mod_exchange", out_shape=jax.ShapeDtypeStruct((4, 8, n), F32),
        in_specs=[pl.BlockSpec(memory_space=pltpu.VMEM)], out_specs=pl.BlockSpec(memory_space=pltpu.VMEM),
        scratch_shapes=[pltpu.SemaphoreType.DMA((3,)), pltpu.SemaphoreType.DMA((3,)), pltpu.SemaphoreType.DMA],
        compiler_params=pltpu.CompilerParams(vmem_limit_bytes=VMEM_LIMIT))(m_cols)


def _ada_bwd(cs, w, dm):
    n = w.shape[1]

    def body(c_ref, w_ref, dm_ref, gw_ref, dc_ref):
        cv = c_ref[...]
        s = (cv * _sigmoid(cv)).astype(BF16)
        gw_ref[...] = _dot_tn(s, dm_ref[...].astype(BF16))
        dc_ref[...] = _dot_nt(dm_ref[0:8, :].astype(BF16), w_ref[...].astype(BF16))

    return pl.pallas_call(
        body, name="ada_bwd",
        out_shape=[jax.ShapeDtypeStruct((D, n), F32), jax.ShapeDtypeStruct((8, D), F32)],
        compiler_params=pltpu.CompilerParams(vmem_limit_bytes=VMEM_LIMIT))(cs, w, dm)


def _adamw(w, g, m, v, name):
    r, c = w.shape
    rb = r
    if r % 8 == 0 and r * c * 4 > (1 << 20):
        rb = 8
        for cand in range(8, r, 8):
            if r % cand == 0 and cand * c * 4 <= (1 << 20):
                rb = cand

    def body(w_ref, g_ref, m_ref, v_ref, go_ref, d_ref, nm_ref, nv_ref):
        gv = g_ref[...]
        go_ref[...] = gv
        mn = ADAM_B1 * m_ref[...] + (1.0 - ADAM_B1) * gv
        vn = ADAM_B2 * v_ref[...] + (1.0 - ADAM_B2) * (gv * gv)
        m_hat = mn / (1.0 - ADAM_B1 ** ADAM_STEP)
        v_hat = vn / (1.0 - ADAM_B2 ** ADAM_STEP)
        d_ref[...] = -ADAM_LR * (m_hat / (jnp.sqrt(v_hat) + ADAM_EPS) + ADAM_WD * w_ref[...])
        nm_ref[...] = mn
        nv_ref[...] = vn

    spec = pl.BlockSpec((rb, c), lambda i: (i, 0))
    shp = jax.ShapeDtypeStruct((r, c), F32)
    return pl.pallas_call(body, name=name, grid=(r // rb,), in_specs=[spec] * 4, out_specs=[spec] * 4,
                          out_shape=[shp] * 4, compiler_params=_cp(1))(w, g, m, v)


BIG = ("ffn1_w_in", "ffn1_w_out", "w_mix_in", "w_mix_out", "ffn2_w_in", "ffn2_w_out")


def _half_rows(w, c):
    half = w.shape[0] // 2
    return lax.dynamic_slice_in_dim(w, c * half, half, axis=0)


def _lower_bounds(logits):
    return jnp.cumsum(jax.nn.softmax(logits.astype(F32), axis=1), axis=1)[:, 0]


def kernel(x, c, ctx, c_ctx, w_ada, b_ada, ln_gain, ln_bias, ffn1_w_in, ffn1_w_out, w_mix_in, gla_a2_fwd, gla_a2_bwd, gla_a_bias_fwd, gla_a_bias_bwd, hgrn_lb_logits, gla_norm_gain, hgrn_norm_gain, w_mix_out, ffn2_w_in, ffn2_w_out, loss_target, m_c_ctx, m_w_ada, m_b_ada, m_ln_gain, m_ln_bias, m_ffn1_w_in, m_ffn1_w_out, m_w_mix_in, m_gla_a2_fwd, m_gla_a2_bwd, m_gla_a_bias_fwd, m_gla_a_bias_bwd, m_hgrn_lb_logits, m_gla_norm_gain, m_hgrn_norm_gain, m_w_mix_out, m_ffn2_w_in, m_ffn2_w_out, v_c_ctx, v_w_ada, v_b_ada, v_ln_gain, v_ln_bias, v_ffn1_w_in, v_ffn1_w_out, v_w_mix_in, v_gla_a2_fwd, v_gla_a2_bwd, v_gla_a_bias_fwd, v_gla_a_bias_bwd, v_hgrn_lb_logits, v_gla_norm_gain, v_hgrn_norm_gain, v_w_mix_out, v_ffn2_w_in, v_ffn2_w_out):
    xi, yi, ci = lax.axis_index("x"), lax.axis_index("y"), lax.axis_index("c")
    chip = 2 * xi + yi
    dev = 2 * chip + ci
    B = x.shape[0]
    weights = dict(ffn1_w_in=ffn1_w_in[0], ffn1_w_out=ffn1_w_out[0], w_mix_in=w_mix_in[0], w_mix_out=w_mix_out[0],
                   ffn2_w_in=ffn2_w_in[0], ffn2_w_out=ffn2_w_out[0])

    mine = jnp.concatenate([c.reshape(-1), ln_gain.reshape(-1), ln_bias.reshape(-1), gla_a2_fwd.reshape(-1),
                            gla_a2_bwd.reshape(-1), hgrn_lb_logits.reshape(-1)])
    g1 = _gather_flat(mine, "gather_cond")
    nc = B * D
    c_all = g1[:, :nc].reshape(8 * B, D)
    per_chip = g1[0::2, nc:]
    o = 0

    def take(shape, axis):
        nonlocal o
        n = int(np.prod(shape))
        parts = per_chip[:, o:o + n].reshape((4,) + shape)
        o += n
        return jnp.concatenate([parts[j] for j in range(4)], axis=axis)

    ln_gain_f = take((3, 256), 1)
    ln_bias_f = take((3, 256), 1)
    a2f_f = take((16, 64), 1)
    a2b_f = take((16, 64), 1)
    lbl_f = take((2, 2, 128), 2)
    lb, lb_vjp = jax.vjp(_lower_bounds, lbl_f)

    assert B + 1 <= 8
    cs = jnp.concatenate([c_all.reshape(8, B, D), jnp.broadcast_to(c_ctx.reshape(1, 1, D), (8, 1, D)),
                          jnp.zeros((8, 7 - B, D), F32)], axis=1).reshape(ADA_ROWS, D)
    ncol = w_ada.shape[2]
    b_cols = lax.dynamic_slice_in_dim(b_ada, chip * ncol, ncol, axis=1)
    m4 = _mod_exchange(_ada_fwd(cs, w_ada[0], b_cols))
    m_all = jnp.concatenate([m4[j] for j in range(4)], axis=1)
    m_lat = m_all[:B].reshape(B, 1, N_MOD, D)
    m_ctx = jnp.broadcast_to(m_all[B].reshape(1, 1, N_MOD, D), (B, 1, N_MOD, D))

    groups = dict(ffn2=("ffn2_w_in", "ffn2_w_out"), mix=("w_mix_in", "w_mix_out"), ffn1=("ffn1_w_in", "ffn1_w_out"))
    shards = dict(weights, w_mix_in=jnp.pad(weights["w_mix_in"], ((0, 0), (0, MIX_NP - MIX_N))))
    blks = {k: _half_rows(shards[k], ci).astype(BF16) for k in BIG}
    gathering = {}
    token = m_all
    for group in ("ffn1", "mix", "ffn2"):
        sems, x_thru, l_thru, token = _split_start(_gather_copies, 4, [blks[k] for k in groups[group]], 8, token,
                                                   "weight_gather_start_" + group)
        gathering[group] = (sems, x_thru, l_thru)
    mvec = jnp.concatenate([m_ctx, m_lat], axis=1) + token[0, 0]

    def weights_for(group, after):
        names = groups[group]
        _, got = _split_wait(_gather_copies, 4, *gathering[group], after, "weight_gather_wait_" + group)
        w_in, w_out = _gather_forward(got, "weight_gather_forward_" + group)
        if group == "mix":
            return _mix_in_to_padded(w_in.reshape(4, D, MIX_NP)), w_out.reshape(-1, D)
        return w_in.reshape((4,) + shards[names[0]].shape), w_out.reshape(-1, D)

    cvec = ci.reshape(1).astype(jnp.int32)
    cjvec = jnp.stack([ci, chip]).astype(jnp.int32)
    in_flight = {}

    def on_grads(group, gs):
        names = groups[group]
        if group == "mix":
            gs = (_mix_in_from_padded(gs[0]), gs[1])
        g8s = [g.reshape((8, shards[k].shape[0] // 2, shards[k].shape[1])) for k, g in zip(names, gs)]
        sems, g_thru, l_thru, token = _split_start(_pair_copies, 4, g8s, 4, None, "grad_pair_start_" + group)
        in_flight[group] = (sems, g_thru, l_thru)
        return token

    def on_sent(group, after):
        g8s, r4s = _split_wait(_pair_copies, 4, *in_flight[group], after, "grad_pair_wait_" + group)
        h4s = [_rs_add_pair(g, r, cvec, "grad_pair_add_" + k) for k, g, r in zip(groups[group], g8s, r4s)]
        sems, h_thru, l_thru, token = _split_start(_chip_copies, 3, h4s, 3, None, "grad_chip_start_" + group)
        in_flight[group] = (g8s, r4s, sems, h_thru, l_thru)
        return token

    loss_l, grad_x, dm, small = _local_step(
        x, ctx, loss_target, mvec, weights_for, ln_gain_f, ln_bias_f, a2f_f, a2b_f,
        gla_a_bias_fwd, gla_a_bias_bwd, lb, gla_norm_gain, hgrn_norm_gain, on_grads, on_sent)
    loss = lax.psum(loss_l, ("x", "y", "c"))

    dm_lat = dm[:, 1].reshape(B, N_MOD * D)
    dm_ctx = jnp.sum(dm[:, 0], axis=0).reshape(N_MOD * D)
    keys = ("ln_gain", "ln_bias", "a2f", "a2b", "abf", "abb", "lb", "gng", "gnh")
    flat = jnp.concatenate([dm_lat.reshape(-1), dm_ctx] + [small[k].reshape(-1) for k in keys])
    nflat = flat.shape[0]
    npad = -(-nflat // 1024) * 1024
    flat2 = jnp.pad(flat, (0, npad - nflat)).reshape(8, npad // 8)
    s_sems, s_src, s_land, tok = _split_start(_all_copies, 7, [flat2], 8, None, "small_grads_start")

    fin = {}
    for group in ("ffn2", "mix", "ffn1"):
        g8s, r4s, sems, h_thru, l_thru = in_flight[group]
        _, r3s = _split_wait(_chip_copies, 3, sems, h_thru, l_thru, tok, "grad_chip_wait_" + group)
        for k, g, r4, r3 in zip(groups[group], g8s, r4s, r3s):
            fin[k] = _rs_add_chips(g, r4, r3, cjvec, "grad_chip_add_" + k)
    p_sems, p_src, p_land, tok = _split_start(_share_copies, 1, [fin[k] for k in BIG], 1, None, "grad_pair_share_start")

    _, (land,) = _split_wait(_all_copies, 7, s_sems, s_src, s_land, tok, "small_grads_wait")
    g3 = lax.dynamic_update_index_in_dim(land, flat2, dev, 0).reshape(8, npad)[:, :nflat]
    nlat = B * N_MOD * D
    dm_all = g3[:, :nlat].reshape(8 * B, N_MOD * D)
    tot = _sum8(g3[:, nlat:])[0]
    dmc_tot = tot[:N_MOD * D]
    o = N_MOD * D
    sg = {}
    for k in keys:
        n = int(np.prod(small[k].shape))
        sg[k] = tot[o:o + n].reshape(small[k].shape)
        o += n
    ctx_rows = jnp.zeros((8, 1, N_MOD * D), F32).at[0, 0].set(dmc_tot)
    dm_rows = jnp.concatenate([dm_all.reshape(8, B, -1), ctx_rows, jnp.zeros((8, 7 - B, N_MOD * D), F32)],
                              axis=1).reshape(ADA_ROWS, N_MOD * D)
    g_b_ada = (jnp.sum(dm_all, axis=0) + dmc_tot).reshape(1, N_MOD * D)
    g_w_ada, dcc = _ada_bwd(cs, w_ada[0], lax.dynamic_slice_in_dim(dm_rows, chip * ncol, ncol, axis=1))
    g4 = _gather_flat(dcc[B], "gather_cctx")
    dsilu = ((g4[0] + g4[2]) + g4[4]) + g4[6]
    sc = _sigmoid(c_ctx)
    g_c_ctx = dsilu * (sc * (1.0 + c_ctx * (1.0 - sc)))
    (g_lbl,) = lb_vjp(sg["lb"])

    def cols(a, n, axis):
        return lax.dynamic_slice_in_dim(a, chip * n, n, axis=axis)

    shared, _ = _split_wait(_share_copies, 1, p_sems, p_src, p_land, g_w_ada, "grad_pair_share_wait")
    gsh = {k: both.reshape(shards[k].shape)[:, :weights[k].shape[1]] for k, both in zip(BIG, shared)}

    grads = dict(
        c_ctx=g_c_ctx, w_ada=g_w_ada[None], b_ada=g_b_ada, ln_gain=cols(sg["ln_gain"], 256, 1)[None],
        ln_bias=cols(sg["ln_bias"], 256, 1)[None], ffn1_w_in=gsh["ffn1_w_in"][None], ffn1_w_out=gsh["ffn1_w_out"][None],
        w_mix_in=gsh["w_mix_in"][None], gla_a2_fwd=cols(sg["a2f"], 64, 1)[None], gla_a2_bwd=cols(sg["a2b"], 64, 1)[None],
        gla_a_bias_fwd=sg["abf"], gla_a_bias_bwd=sg["abb"], hgrn_lb_logits=cols(g_lbl, 128, 2),
        gla_norm_gain=sg["gng"].reshape(1, HD), hgrn_norm_gain=sg["gnh"].reshape(1, HD),
        w_mix_out=gsh["w_mix_out"][None], ffn2_w_in=gsh["ffn2_w_in"][None], ffn2_w_out=gsh["ffn2_w_out"][None])
    params = dict(
        c_ctx=(c_ctx, m_c_ctx, v_c_ctx), w_ada=(w_ada, m_w_ada, v_w_ada), b_ada=(b_ada, m_b_ada, v_b_ada),
        ln_gain=(ln_gain, m_ln_gain, v_ln_gain), ln_bias=(ln_bias, m_ln_bias, v_ln_bias),
        ffn1_w_in=(ffn1_w_in, m_ffn1_w_in, v_ffn1_w_in), ffn1_w_out=(ffn1_w_out, m_ffn1_w_out, v_ffn1_w_out),
        w_mix_in=(w_mix_in, m_w_mix_in, v_w_mix_in), gla_a2_fwd=(gla_a2_fwd, m_gla_a2_fwd, v_gla_a2_fwd),
        gla_a2_bwd=(gla_a2_bwd, m_gla_a2_bwd, v_gla_a2_bwd),
        gla_a_bias_fwd=(gla_a_bias_fwd, m_gla_a_bias_fwd, v_gla_a_bias_fwd),
        gla_a_bias_bwd=(gla_a_bias_bwd, m_gla_a_bias_bwd, v_gla_a_bias_bwd),
        hgrn_lb_logits=(hgrn_lb_logits, m_hgrn_lb_logits, v_hgrn_lb_logits),
        gla_norm_gain=(gla_norm_gain, m_gla_norm_gain, v_gla_norm_gain),
        hgrn_norm_gain=(hgrn_norm_gain, m_hgrn_norm_gain, v_hgrn_norm_gain),
        w_mix_out=(w_mix_out, m_w_mix_out, v_w_mix_out), ffn2_w_in=(ffn2_w_in, m_ffn2_w_in, v_ffn2_w_in),
        ffn2_w_out=(ffn2_w_out, m_ffn2_w_out, v_ffn2_w_out))
    order = list(params.keys())
    big_names = ("w_ada",) + BIG
    upd = {}
    for k in big_names:
        w_, m_, v_ = params[k]
        s2 = w_.shape[-2:]
        g_, d_, nm_, nv_ = _adamw(w_.reshape(s2), grads[k].reshape(s2), m_.reshape(s2), v_.reshape(s2), "adamw_" + k)
        grads[k] = g_
        upd[k] = (d_.reshape(w_.shape), nm_.reshape(w_.shape), nv_.reshape(w_.shape))
    small_names = [k for k in order if k not in big_names]
    sizes = [int(np.prod(params[k][0].shape)) for k in small_names]
    tot_n = sum(sizes)
    npad = -(-tot_n // 1024) * 1024

    def packed(get):
        flat_ = jnp.concatenate([get(k).reshape(-1) for k in small_names])
        return jnp.pad(flat_, (0, npad - tot_n)).reshape(8, npad // 8)

    _, d_s, nm_s, nv_s = _adamw(packed(lambda k: params[k][0]), packed(lambda k: grads[k]),
                                packed(lambda k: params[k][1]), packed(lambda k: params[k][2]), "adamw_small")
    o = 0
    for k, n in zip(small_names, sizes):
        shp = params[k][0].shape
        upd[k] = tuple(a.reshape(-1)[o:o + n].reshape(shp) for a in (d_s, nm_s, nv_s))
        o += n

    return (loss, grad_x, *[grads[k].reshape(params[k][0].shape) for k in order], *[upd[k][0] for k in order],
            *[upd[k][1] for k in order], *[upd[k][2] for k in order])
```

```python
import functools

import numpy as np
import jax
import jax.numpy as jnp
from jax import lax
from jax.experimental import pallas as pl
from jax.experimental.pallas import tpu as pltpu

F32 = jnp.float32
BF16 = jnp.bfloat16
MESH = pl.DeviceIdType.MESH

D = 1024
DFF = 2816
TM = 256
CH = 32
NCB = TM // CH
SB = 128
CSB = SB // CH
NSB = TM // SB
HP = 8
HPB = 8
HD = 128
NH = 8
LN_EPS = 1e-5
NORM_EPS = 1e-6
ALPHA = 2.0 ** 0.25
GATE_NORM = 16.0
GLA_DK = 64
N_MOD = 9
VMEM_LIMIT = 52 * 1024 * 1024

MIXP = 5120
GG, RG, GQ, GK, GV, RQ, RFF, RFB, RI, LR = 0, 512, 1024, 1536, 2048, 2560, 3072, 3584, 4096, 4608
IN_SPLITS = (256, 256, 512, 512, 16, 16, 512, 512, 512, 512, 512)

ADAM_LR, ADAM_B1, ADAM_B2, ADAM_EPS, ADAM_WD, ADAM_STEP = 0.001, 0.9, 0.999, 1e-08, 0.01, 10


def _cp(n_axes):
    return pltpu.CompilerParams(dimension_semantics=("arbitrary",) * n_axes, vmem_limit_bytes=VMEM_LIMIT)


def _rowmap(stride, off):
    return lambda b, i: (b * stride + off + i, 0)


def _mmap(comb):
    if comb:
        return lambda b, i: (b, jnp.minimum(i, 1), 0, 0)
    return lambda b, i: (b, 1, 0, 0)


def _ln(x):
    mu = jnp.mean(x, axis=-1, keepdims=True)
    xc = x - mu
    var = jnp.mean(xc * xc, axis=-1, keepdims=True)
    r = lax.rsqrt(var + LN_EPS)
    return xc * r, r


def _ln_bwd(dxh, xh, r):
    return r * (dxh - jnp.mean(dxh, axis=-1, keepdims=True) - xh * jnp.mean(dxh * xh, axis=-1, keepdims=True))


def _sigmoid(x):
    return 1.0 / (1.0 + jnp.exp(-x))


def _rsum(x):
    return jnp.sum(x, axis=0, keepdims=True)


def _dot(a, b):
    return jnp.dot(a, b, preferred_element_type=F32)


def _dot_nt(a, b):
    return lax.dot_general(a, b, (((1,), (1,)), ((), ())), preferred_element_type=F32)


def _dot_tn(a, b):
    return lax.dot_general(a, b, (((0,), (0,)), ((), ())), preferred_element_type=F32)


def _modulate(xv, m_ref, sub):
    xh, _ = _ln(xv)
    sh = m_ref[0, 0, 3 * sub:3 * sub + 1, :]
    sc = m_ref[0, 0, 3 * sub + 1:3 * sub + 2, :]
    return (xh * (1.0 + sh) + sc).astype(BF16)


def _embed_lnmod(x, ctx, pos, mvec):
    B, T, _ = x.shape
    nt = 1 + T // TM

    def body(x_ref, c_ref, p_ref, m_ref, o_ref, h_ref):
        i = pl.program_id(1)

        @pl.when(i == 0)
        def _():
            o_ref[...] = c_ref[0]

        @pl.when(i > 0)
        def _():
            o_ref[...] = x_ref[0] + p_ref[...]

        h_ref[...] = _modulate(o_ref[...], m_ref, 0)

    rows = pl.BlockSpec((TM, D), lambda b, i: (b * nt + i, 0))
    return pl.pallas_call(
        body, name="embed_lnmod0", grid=(B, nt),
        in_specs=[pl.BlockSpec((1, TM, D), lambda b, i: (b, jnp.maximum(i - 1, 0), 0)),
                  pl.BlockSpec((1, TM, D), lambda b, i: (b, 0, 0)),
                  pl.BlockSpec((TM, D), lambda b, i: (jnp.maximum(i - 1, 0), 0)),
                  pl.BlockSpec((1, 1, N_MOD, D), _mmap(True))],
        out_specs=[rows, rows],
        out_shape=[jax.ShapeDtypeStruct((B * nt * TM, D), F32), jax.ShapeDtypeStruct((B * nt * TM, D), BF16)],
        compiler_params=_cp(2))(x, ctx, pos, mvec)


def _lnmod0_bwd(dh, X, mvec, dres, x_shape, B, nt):
    def body(dh_ref, x_ref, m_ref, dr_ref, dx_ref, dm_ref):
        i = pl.program_id(1)
        xh, r = _ln(x_ref[...])
        sh = m_ref[0, 0, 0:1, :]
        dhv = dh_ref[...].astype(F32)

        @pl.when((i == 0) | (i == 1))
        def _():
            dm_ref[...] = jnp.zeros_like(dm_ref)

        dm_ref[0, 0, 0:1, :] += _rsum(dhv * xh)
        dm_ref[0, 0, 1:2, :] += _rsum(dhv)

        @pl.when(i > 0)
        def _():
            dx_ref[0] = _ln_bwd(dhv * (1.0 + sh), xh, r) + dr_ref[...]

    rows = pl.BlockSpec((TM, D), _rowmap(nt, 0))
    return pl.pallas_call(
        body, name="lnmod0_bwd", grid=(B, nt),
        in_specs=[rows, rows, pl.BlockSpec((1, 1, N_MOD, D), _mmap(True)), rows],
        out_specs=[pl.BlockSpec((1, TM, D), lambda b, i: (b, jnp.maximum(i - 1, 0), 0)),
                   pl.BlockSpec((1, 1, 2, D), _mmap(True))],
        out_shape=[jax.ShapeDtypeStruct(x_shape, F32), jax.ShapeDtypeStruct((B, 2, 2, D), F32)],
        compiler_params=_cp(2))(dh, X, mvec, dres)


def _resid_fwd(x_ref, y_ref, m_ref, gb_ref, sub, w):
    wg = w * m_ref[0, 0, 3 * sub + 2:3 * sub + 3, :]
    y = y_ref[...].astype(F32)
    zh, r = _ln(ALPHA * x_ref[...] + wg * y)
    return y, wg, zh, r


def _resid_grads(do, y, wg, zh, r, w, gb_ref, comb, dx_ref, dy_ref, dg_ref, dgb_ref):
    b_, i = pl.program_id(0), pl.program_id(1)
    dz = _ln_bwd(do * gb_ref[0:1, :], zh, r)
    dx_ref[...] = ALPHA * dz
    dy_ref[...] = (wg * dz).astype(BF16)

    @pl.when((b_ == 0) & (i == 0))
    def _():
        dgb_ref[...] = jnp.zeros_like(dgb_ref)

    dgb_ref[0:1, :] += _rsum(do * zh)
    dgb_ref[1:2, :] += _rsum(do)

    init = (i == 0) | (i == 1) if comb else (i == 0)

    @pl.when(init)
    def _():
        dg_ref[...] = jnp.zeros_like(dg_ref)

    dg_ref[0, 0] += w * _rsum(dz * y)


def _resid_lnmod(X, lx, Y, mvec, comb, sub, w, gb, B, nt, name):
    def body(x_ref, y_ref, m_ref, gb_ref, o_ref, h_ref):
        _, _, zh, _ = _resid_fwd(x_ref, y_ref, m_ref, gb_ref, sub, w)
        xn = zh * gb_ref[0:1, :] + gb_ref[1:2, :]
        o_ref[...] = xn
        h_ref[...] = _modulate(xn, m_ref, sub + 1)

    rows = pl.BlockSpec((TM, D), _rowmap(nt, 0))
    return pl.pallas_call(
        body, name=name, grid=(B, nt),
        in_specs=[pl.BlockSpec((TM, D), _rowmap(*lx)), rows,
                  pl.BlockSpec((1, 1, N_MOD, D), _mmap(comb)), pl.BlockSpec((2, D), lambda b, i: (0, 0))],
        out_specs=[rows, rows],
        out_shape=[jax.ShapeDtypeStruct((B * nt * TM, D), F32), jax.ShapeDtypeStruct((B * nt * TM, D), BF16)],
        compiler_params=_cp(2))(X, Y, mvec, gb)


def _resid_out_shapes(B, nt, comb):
    rows = pl.BlockSpec((TM, D), _rowmap(nt, 0))
    specs = [rows, rows, pl.BlockSpec((1, 1, 1, D), _mmap(comb)), pl.BlockSpec((2, D), lambda b, i: (0, 0))]
    shapes = [jax.ShapeDtypeStruct((B * nt * TM, D), F32), jax.ShapeDtypeStruct((B * nt * TM, D), BF16),
              jax.ShapeDtypeStruct((B, 2, 1, D), F32), jax.ShapeDtypeStruct((2, D), F32)]
    return specs, shapes


def _tail(X, Y, mvec, gb, tgt, sub, w, B, nt):
    def body(x_ref, y_ref, m_ref, gb_ref, t_ref, dx_ref, dy_ref, dg_ref, dgb_ref, l_ref):
        y, wg, zh, r = _resid_fwd(x_ref, y_ref, m_ref, gb_ref, sub, w)
        e = (zh * gb_ref[0:1, :] + gb_ref[1:2, :]) - t_ref[0]

        @pl.when((pl.program_id(0) == 0) & (pl.program_id(1) == 0))
        def _():
            l_ref[...] = jnp.zeros_like(l_ref)

        l_ref[...] += _rsum(e * e)
        _resid_grads(e * (1.0 / D), y, wg, zh, r, w, gb_ref, False, dx_ref, dy_ref, dg_ref, dgb_ref)

    rows = pl.BlockSpec((TM, D), _rowmap(nt, 0))
    specs, shapes = _resid_out_shapes(B, nt, False)
    return pl.pallas_call(
        body, name="resid2_loss_bwd", grid=(B, nt),
        in_specs=[rows, rows, pl.BlockSpec((1, 1, N_MOD, D), _mmap(False)), pl.BlockSpec((2, D), lambda b, i: (0, 0)),
                  pl.BlockSpec((1, TM, D), lambda b, i: (b, i, 0))],
        out_specs=specs + [pl.BlockSpec((1, D), lambda b, i: (0, 0))],
        out_shape=shapes + [jax.ShapeDtypeStruct((1, D), F32)],
        compiler_params=_cp(2))(X, Y, mvec, gb, tgt)


def _lnmod_resid_bwd(dh, Xi, lxi, dres, Xp, lxp, Yp, mvec, comb, sub, w, gb, B, nt, name):
    ntl = nt - 1 if comb else nt

    def body(dh_ref, xi_ref, dr_ref, xp_ref, yp_ref, m_ref, gb_ref, dx_ref, dy_ref, dg_ref, dgb_ref, dm_ref):
        i = pl.program_id(1)
        xh, r = _ln(xi_ref[...])
        sh = m_ref[0, 0, 3 * sub:3 * sub + 1, :]
        dhv = dh_ref[...].astype(F32)
        dr = dr_ref[...]
        if comb:
            dr = jnp.where(i > 0, dr, 0.0)
        do = _ln_bwd(dhv * (1.0 + sh), xh, r) + dr

        init = (i == 0) | (i == 1) if comb else (i == 0)

        @pl.when(init)
        def _():
            dm_ref[...] = jnp.zeros_like(dm_ref)

        dm_ref[0, 0, 0:1, :] += _rsum(dhv * xh)
        dm_ref[0, 0, 1:2, :] += _rsum(dhv)
        y, wg, zh, r2 = _resid_fwd(xp_ref, yp_ref, m_ref, gb_ref, sub - 1, w)
        _resid_grads(do, y, wg, zh, r2, w, gb_ref, comb, dx_ref, dy_ref, dg_ref, dgb_ref)

    rows = pl.BlockSpec((TM, D), _rowmap(nt, 0))
    if comb:
        dres_spec = pl.BlockSpec((TM, D), lambda b, i: (b * ntl + jnp.maximum(i - 1, 0), 0))
    else:
        dres_spec = rows
    specs, shapes = _resid_out_shapes(B, nt, comb)
    return pl.pallas_call(
        body, name=name, grid=(B, nt),
        in_specs=[rows, pl.BlockSpec((TM, D), _rowmap(*lxi)), dres_spec, pl.BlockSpec((TM, D), _rowmap(*lxp)), rows,
                  pl.BlockSpec((1, 1, N_MOD, D), _mmap(comb)), pl.BlockSpec((2, D), lambda b, i: (0, 0))],
        out_specs=specs + [pl.BlockSpec((1, 1, 2, D), _mmap(comb))],
        out_shape=shapes + [jax.ShapeDtypeStruct((B, 2, 2, D), F32)],
        compiler_params=_cp(2))(dh, Xi, dres, Xp, Yp, mvec, gb)


def _ffn_out_dx(dy, W, u, name, dep=None):
    M = dy.shape[0]
    half = DFF // 2
    deps = [] if dep is None else [dep]

    def body(dy_ref, w_ref, u_ref, *rest):
        du_ref = rest[-1]
        dyv = dy_ref[...]
        for j in range(2):
            lo, hi = j * half, (j + 1) * half
            da = _dot_nt(dyv, w_ref[lo:hi, :])
            dsg = u_ref[:, lo:hi].astype(F32)
            sg = u_ref[:, DFF + lo:DFF + hi].astype(F32)
            up = u_ref[:, 2 * DFF + lo:2 * DFF + hi].astype(F32)
            du_ref[:, lo:hi] = (da * up * dsg).astype(BF16)
            du_ref[:, DFF + lo:DFF + hi] = (da * sg).astype(BF16)

    return pl.pallas_call(
        body, name=name, grid=(M // TM,),
        in_specs=[pl.BlockSpec((TM, D), lambda i: (i, 0)), _wspec(W, 1), pl.BlockSpec((TM, 3 * DFF), lambda i: (i, 0))]
        + [_wspec(d, 1) for d in deps],
        out_specs=pl.BlockSpec((TM, 2 * DFF), lambda i: (i, 0)),
        out_shape=jax.ShapeDtypeStruct((M, 2 * DFF), BF16), compiler_params=_cp(1))(dy, W, u, *deps)


def _wspec(W, nidx):
    zeros = (0,) * W.ndim
    if nidx == 1:
        return pl.BlockSpec(W.shape, lambda i: zeros)
    return pl.BlockSpec(W.shape, lambda b, i: zeros)


def _mm_nn(A, la, W, B, nt, out_dtype, name):
    K, N = W.shape

    def body(a_ref, w_ref, o_ref):
        o_ref[...] = _dot(a_ref[...], w_ref[...]).astype(out_dtype)

    return pl.pallas_call(
        body, name=name, grid=(B, nt),
        in_specs=[pl.BlockSpec((TM, K), _rowmap(*la)), _wspec(W, 2)],
        out_specs=pl.BlockSpec((TM, N), _rowmap(nt, 0)),
        out_shape=jax.ShapeDtypeStruct((B * nt * TM, N), out_dtype), compiler_params=_cp(2))(A, W)


def _ffn_in(A, la, W3, B, nt, name):
    K, n = W3.shape[1:]

    def body(a_ref, w_ref, u_ref, s_ref):
        a = a_ref[...]
        for j in range(2):
            g = _dot(a, w_ref[j])
            up = _dot(a, w_ref[j + 2])
            s = _sigmoid(g)
            sg = g * s
            u_ref[:, j * n:(j + 1) * n] = (s * (1.0 + g * (1.0 - s))).astype(BF16)
            u_ref[:, (j + 2) * n:(j + 3) * n] = sg.astype(BF16)
            u_ref[:, (j + 4) * n:(j + 5) * n] = up.astype(BF16)
            s_ref[:, j * n:(j + 1) * n] = (sg * up).astype(BF16)

    rows = B * nt * TM
    return pl.pallas_call(
        body, name=name, grid=(B, nt),
        in_specs=[pl.BlockSpec((TM, K), _rowmap(*la)), _wspec(W3, 2)],
        out_specs=[pl.BlockSpec((TM, 6 * n), _rowmap(nt, 0)), pl.BlockSpec((TM, 2 * n), _rowmap(nt, 0))],
        out_shape=[jax.ShapeDtypeStruct((rows, 6 * n), BF16), jax.ShapeDtypeStruct((rows, 2 * n), BF16)],
        compiler_params=_cp(2))(A, W3)


def _mm_nt(A, W, name, dep=None, out_dtype=F32):
    M, N = A.shape
    K = W.shape[-2]

    def body(a_ref, w_ref, *rest):
        o_ref = rest[-1]
        if W.ndim == 3:
            n = W.shape[-1]
            acc = _dot_nt(a_ref[:, 0:n], w_ref[0])
            for j in range(1, 4):
                acc = acc + _dot_nt(a_ref[:, j * n:(j + 1) * n], w_ref[j])
            o_ref[...] = acc.astype(out_dtype)
        else:
            o_ref[...] = _dot_nt(a_ref[...], w_ref[...]).astype(out_dtype)

    deps = [] if dep is None else [dep]
    return pl.pallas_call(
        body, name=name, grid=(M // TM,),
        in_specs=[pl.BlockSpec((TM, N), lambda i: (i, 0)), _wspec(W, 1)] + [_wspec(d, 1) for d in deps],
        out_specs=pl.BlockSpec((TM, K), lambda i: (i, 0)),
        out_shape=jax.ShapeDtypeStruct((M, K), out_dtype), compiler_params=_cp(1))(A, W, *deps)


def _mm_tn(A, G, name, tn=512, shards=None):
    M, K = A.shape
    N = G.shape[1]
    tk = next(t for t in ((2048, 1536, 1024, 512) if K <= D else (1024, 512)) if M % t == 0)
    if shards:
        tn = N // shards

    def body(a_ref, g_ref, o_ref):
        @pl.when(pl.program_id(1) == 0)
        def _():
            o_ref[...] = jnp.zeros_like(o_ref)

        upd = _dot_tn(a_ref[...], g_ref[...])
        if shards:
            o_ref[0] += upd
        else:
            o_ref[...] += upd

    if shards:
        out_spec = pl.BlockSpec((1, K, tn), lambda n, k: (n, 0, 0))
        out_shape = jax.ShapeDtypeStruct((shards, K, tn), F32)
    else:
        out_spec = pl.BlockSpec((K, tn), lambda n, k: (0, n))
        out_shape = jax.ShapeDtypeStruct((K, N), F32)
    return pl.pallas_call(
        body, name=name, grid=(N // tn, M // tk),
        in_specs=[pl.BlockSpec((tk, K), lambda n, k: (k, 0)), pl.BlockSpec((tk, tn), lambda n, k: (k, n))],
        out_specs=out_spec, out_shape=out_shape, compiler_params=_cp(2))(A, G)


def _logsig(z):
    return jnp.minimum(z, 0.0) - jnp.log(1.0 + jnp.exp(-jnp.abs(z)))


ZW = 2688
ZRQ, ZRFF, ZRFB, ZLR = 1024, 1536, 2048, 2560


def _mix_in_features(H, W, a2p, biasp, lbp, B, nt):
    K = W.shape[0]

    def body(h_ref, w_ref, a2_ref, bias_ref, lb_ref, q_ref, k_ref, v_ref, g_ref, z_ref):
        hv = h_ref[...]
        proj = lambda lo, hi: _dot(hv, w_ref[:, lo:hi])
        z_ref[:, 0:2 * 4 * HD] = proj(GG, GG + 2 * 4 * HD)
        lrf = proj(LR, LR + HD)
        z_ref[:, ZLR:ZLR + HD] = lrf
        lr = lrf.astype(BF16)
        lane = lax.broadcasted_iota(jnp.int32, (1, 4 * HD), 1)
        keep = (lane & (HD - 1)) < GLA_DK
        for d in range(2):
            z = _dot(lr, a2_ref[d]) + bias_ref[d:d + 1, :]
            gl = jnp.where(keep, _logsig(z) * (1.0 / GATE_NORM), 0.0)
            for h in range(4):
                g_ref[d, 0, h] = gl[:, h * HD:(h + 1) * HD]
        gq, gk, gv = proj(GQ, GQ + 4 * HD), proj(GK, GK + 4 * HD), proj(GV, GV + 4 * HD)
        for h in range(4):
            sl = slice(h * HD, (h + 1) * HD)
            q_ref[0, h] = gq[:, sl] * (GLA_DK ** -0.5)
            k_ref[0, 0, h] = gk[:, sl]
            k_ref[1, 0, h] = gk[:, sl]
            v_ref[0, h] = gv[:, sl].astype(BF16)
        rqs, ri = proj(RQ, RQ + 4 * HD), proj(RI, RI + 4 * HD)
        z_ref[:, ZRQ:ZRQ + 4 * HD] = rqs
        for h in range(4):
            sl = slice(h * HD, (h + 1) * HD)
            rq = rqs[:, sl]
            q_ref[0, 4 + h] = rq * _sigmoid(rq) * (HD ** -0.5)
            v_ref[0, 4 + h] = ri[:, sl].astype(BF16)
        for d, off, zoff in ((0, RFF, ZRFF), (1, RFB, ZRFB)):
            rf = proj(off, off + 4 * HD)
            z_ref[:, zoff:zoff + 4 * HD] = rf
            for h in range(4):
                sl = slice(h * HD, (h + 1) * HD)
                lb = lb_ref[d:d + 1, sl]
                f = lb + (1.0 - lb) * _sigmoid(rf[:, sl])
                g_ref[d, 0, 4 + h] = jnp.log(f)
                k_ref[d, 0, 4 + h] = 1.0 - f

    one = pl.BlockSpec((1, NH, TM, HD), lambda b, i: (b, 0, i, 0))
    two = pl.BlockSpec((2, 1, NH, TM, HD), lambda b, i: (0, b, 0, i, 0))
    s1 = jax.ShapeDtypeStruct((B, NH, nt * TM, HD), F32)
    s2 = jax.ShapeDtypeStruct((2, B, NH, nt * TM, HD), F32)
    return pl.pallas_call(
        body, name="mix_in_features", grid=(B, nt),
        in_specs=[pl.BlockSpec((TM, K), _rowmap(nt, 0)), _wspec(W, 2), pl.BlockSpec((2, HD, 4 * HD), lambda b, i: (0, 0, 0)),
                  pl.BlockSpec((2, 4 * HD), lambda b, i: (0, 0)), pl.BlockSpec((2, 4 * HD), lambda b, i: (0, 0))],
        out_specs=[one, two, one, two, pl.BlockSpec((TM, ZW), _rowmap(nt, 0))],
        out_shape=[s1, s2, jax.ShapeDtypeStruct(s1.shape, BF16), s2, jax.ShapeDtypeStruct((B * nt * TM, ZW), F32)],
        compiler_params=_cp(2))(H, W, a2p, biasp, lbp)


def _features_bwd(Z, a2p, biasp, lbp, dQ0, dQ1, dK0, dK1, dV0, dV1, dG0, dG1, dgates, B):
    nt = Z.shape[0] // (B * TM)

    def body(z_ref, a2_ref, bias_ref, lb_ref, dq0, dq1, dk0, dk1, dv0, dv1, dg0, dg1, dgt_ref,
             df_ref, da2_ref, dbias_ref, dlb_ref):
        b_, i = pl.program_id(0), pl.program_id(1)

        @pl.when((b_ == 0) & (i == 0))
        def _():
            da2_ref[...] = jnp.zeros_like(da2_ref)
            dbias_ref[...] = jnp.zeros_like(dbias_ref)
            dlb_ref[...] = jnp.zeros_like(dlb_ref)

        df_ref[:, 0:2 * 4 * HD] = jnp.where(i > 0, dgt_ref[...], 0.0).astype(BF16)
        df_ref[:, LR + HD:] = jnp.zeros((TM, MIXP - LR - HD), BF16)
        lr = z_ref[:, ZLR:ZLR + HD].astype(BF16)
        lane = lax.broadcasted_iota(jnp.int32, (1, 4 * HD), 1)
        keep = (lane & (HD - 1)) < GLA_DK
        dlr = jnp.zeros((TM, HD), F32)
        dgs = (dg0, dg1)
        dks = (dk0, dk1)
        rd = lambda ref, h: ref[0, h].astype(F32)
        for d in range(2):
            z = _dot(lr, a2_ref[d]) + bias_ref[d:d + 1, :]
            dgl = jnp.concatenate([rd(dgs[d], h) for h in range(4)], axis=1)
            dz = jnp.where(keep, dgl * (1.0 / GATE_NORM) * (1.0 - _sigmoid(z)), 0.0)
            dzb = dz.astype(BF16)
            dlr = dlr + _dot_nt(dzb, a2_ref[d])
            da2_ref[d] += _dot_tn(lr, dzb)
            dbias_ref[d:d + 1, :] += _rsum(dz)
        df_ref[:, LR:LR + HD] = dlr.astype(BF16)
        for h in range(4):
            df_ref[:, GQ + h * HD:GQ + (h + 1) * HD] = ((rd(dq0, h) + rd(dq1, h)) * (GLA_DK ** -0.5)).astype(BF16)
            df_ref[:, GK + h * HD:GK + (h + 1) * HD] = (rd(dk0, h) + rd(dk1, h)).astype(BF16)
            df_ref[:, GV + h * HD:GV + (h + 1) * HD] = (rd(dv0, h) + rd(dv1, h)).astype(BF16)
        for h in range(4):
            sl = slice(h * HD, (h + 1) * HD)
            rq = z_ref[:, ZRQ + h * HD:ZRQ + (h + 1) * HD]
            s = _sigmoid(rq)
            dqh = rd(dq0, 4 + h) + rd(dq1, 4 + h)
            df_ref[:, RQ + h * HD:RQ + (h + 1) * HD] = (dqh * (HD ** -0.5) * (s * (1.0 + rq * (1.0 - s)))).astype(BF16)
            df_ref[:, RI + h * HD:RI + (h + 1) * HD] = (rd(dv0, 4 + h) + rd(dv1, 4 + h)).astype(BF16)
            for d, off, zoff in ((0, RFF, ZRFF), (1, RFB, ZRFB)):
                lb = lb_ref[d:d + 1, sl]
                sg = _sigmoid(z_ref[:, zoff + h * HD:zoff + (h + 1) * HD])
                f = lb + (1.0 - lb) * sg
                dff = rd(dgs[d], 4 + h) / f - rd(dks[d], 4 + h)
                df_ref[:, off + h * HD:off + (h + 1) * HD] = (dff * (1.0 - lb) * sg * (1.0 - sg)).astype(BF16)
                dlb_ref[d:d + 1, sl] += _rsum(dff * (1.0 - sg))

    m0 = lambda b, i: (b, 0, i, 0)
    one = lambda m: pl.BlockSpec((1, NH, TM, HD), m)
    return pl.pallas_call(
        body, name="mix_features_bwd", grid=(B, nt),
        in_specs=[pl.BlockSpec((TM, ZW), _rowmap(nt, 0)), pl.BlockSpec((2, HD, 4 * HD), lambda b, i: (0, 0, 0)),
                  pl.BlockSpec((2, 4 * HD), lambda b, i: (0, 0)), pl.BlockSpec((2, 4 * HD), lambda b, i: (0, 0)),
                  one(m0), one(m0), one(m0), one(m0), one(m0), one(m0), one(m0), one(m0),
                  pl.BlockSpec((TM, D), lambda b, i: (b * (nt - 1) + jnp.maximum(i - 1, 0), 0))],
        out_specs=[pl.BlockSpec((TM, MIXP), _rowmap(nt, 0)), pl.BlockSpec((2, HD, 4 * HD), lambda b, i: (0, 0, 0)),
                   pl.BlockSpec((2, 4 * HD), lambda b, i: (0, 0)), pl.BlockSpec((2, 4 * HD), lambda b, i: (0, 0))],
        out_shape=[jax.ShapeDtypeStruct((B * nt * TM, MIXP), BF16), jax.ShapeDtypeStruct((2, HD, 4 * HD), F32),
                   jax.ShapeDtypeStruct((2, 4 * HD), F32), jax.ShapeDtypeStruct((2, 4 * HD), F32)],
        compiler_params=_cp(2))(Z, a2p, biasp, lbp, dQ0, dQ1, dK0, dK1, dV0, dV1, dG0, dG1, dgates)


def _chunk_scan(x, rin, fwd):
    acc = x
    sft = 1
    while sft < CH:
        if fwd:
            acc = acc + jnp.where(rin >= sft, pltpu.roll(acc, sft, 0), 0.0)
        else:
            acc = acc + jnp.where(rin < CH - sft, pltpu.roll(acc, TM - sft, 0), 0.0)
        sft *= 2
    return acc


def _chunk_total(x):
    t = jnp.sum(x.reshape(NCB, CH, HD), axis=1, keepdims=True)
    return jnp.broadcast_to(t, (NCB, CH, HD)).reshape(TM, HD)


def _scan_masks(rev):
    rin = lax.broadcasted_iota(jnp.int32, (TM, HD), 0) & (CH - 1)
    ri = lax.broadcasted_iota(jnp.int32, (SB, SB), 0)
    ci = lax.broadcasted_iota(jnp.int32, (SB, SB), 1)
    same = (ri >> 5) == (ci >> 5)
    lo = same & (ri >= ci)
    up = same & (ri <= ci)
    mask, maskT = (up, lo) if rev else (lo, up)
    re = lax.broadcasted_iota(jnp.int32, (SB, CSB * HD), 0) >> 5
    ce = lax.broadcasted_iota(jnp.int32, (SB, CSB * HD), 1) >> 7
    return rin, mask, maskT, re == ce


def _scan_decay(q, k, g, rin, rev):
    b = _chunk_scan(g, rin, not rev)
    xx = _chunk_total(g) - b
    eb = jnp.exp(b)
    return b, xx, eb, q * eb, k * jnp.exp(-b), k * jnp.exp(xx)


def _sub(x, s):
    return x[s * SB:(s + 1) * SB]


def _expand(xb, mexp):
    return jnp.where(mexp, jnp.concatenate([xb] * CSB, axis=1), jnp.zeros((), xb.dtype))


def _own(x, mexp):
    xm = jnp.where(mexp, x, 0.0)
    acc = xm[:, 0:HD]
    for n in range(1, CSB):
        acc = acc + xm[:, n * HD:(n + 1) * HD]
    return acc


def _stack(per_chunk, s):
    return jnp.concatenate(per_chunk[s * CSB:(s + 1) * CSB], axis=1)


def _state_pass(s0, eb, uts, rev):
    order = range(NCB - 1, -1, -1) if rev else range(NCB)
    states = [None] * NCB
    s = s0
    for n in order:
        row = n * CH if rev else n * CH + CH - 1
        states[n] = s
        s = eb[row:row + 1, :] * s + uts[n // CSB][:, (n % CSB) * HD:(n % CSB + 1) * HD]
    return states, s


def _scan_fwd(Q, K, V, G, rev, B):
    nb = Q.shape[2] // TM
    d = 1 if rev else 0
    rmap = (lambda s: jnp.where(s == 0, 0, nb - s)) if rev else (lambda s: s)

    def body(q_ref, k_ref, v_ref, g_ref, o_ref, st_ref, s_scr):
        @pl.when(pl.program_id(2) == 0)
        def _():
            s_scr[...] = jnp.zeros_like(s_scr)

        rin, mask, _, mexp = _scan_masks(rev)

        def head(p, readout):
            s0 = s_scr[p]
            st_ref[0, p, 0] = s0
            _, _, eb, qd, ki, kt = _scan_decay(q_ref[0, p], k_ref[0, 0, p], g_ref[0, 0, p], rin, rev)
            ktb, vb = kt.astype(BF16), v_ref[0, p]
            uts = [_dot_tn(_sub(vb, s), _expand(_sub(ktb, s), mexp)) for s in range(NSB)]
            states, s_new = _state_pass(s0, eb, uts, rev)
            s_scr[p] = s_new
            if not readout:
                return
            qb, kib = qd.astype(BF16), ki.astype(BF16)
            for s in range(NSB):
                a = jnp.where(mask, _dot_nt(_sub(qb, s), _sub(kib, s)), 0.0)
                o_ref[0, p, s * SB:(s + 1) * SB, :] = (
                    _dot(a.astype(BF16), _sub(vb, s))
                    + _dot_nt(_expand(_sub(qb, s), mexp), _stack(states, s).astype(BF16)))

        @pl.when(pl.program_id(2) >= 1)
        def _():
            for p in range(HP):
                head(p, True)

        @pl.when(pl.program_id(2) == 0)
        def _():
            for p in range(HP):
                head(p, False)

    one = pl.BlockSpec((1, HP, TM, HD), lambda b, h, s: (b, h, rmap(s), 0))
    two = pl.BlockSpec((1, 1, HP, TM, HD), lambda b, h, s: (d, b, h, rmap(s), 0))
    return pl.pallas_call(
        body, name="scan_fwd_rev" if rev else "scan_fwd", grid=(B, NH // HP, nb),
        in_specs=[one, two, one, two],
        out_specs=[one, pl.BlockSpec((1, HP, 1, HD, HD), lambda b, h, s: (b, h, s, 0, 0))],
        out_shape=[jax.ShapeDtypeStruct(Q.shape, F32), jax.ShapeDtypeStruct((B, NH, nb, HD, HD), F32)],
        scratch_shapes=[pltpu.VMEM((HP, HD, HD), F32)],
        compiler_params=_cp(3))(Q, K, V, G)


def _scan_bwd(Q, K, V, G, St, dO, rev, B):
    nb = Q.shape[2] // TM
    d = 1 if rev else 0
    smap = lambda t: nb - 1 - t
    rmap = (lambda t: jnp.where(smap(t) == 0, 0, nb - smap(t))) if rev else smap

    def body(q_ref, k_ref, v_ref, g_ref, st_ref, do_ref, dq_ref, dk_ref, dv_ref, dg_ref, ds_scr):
        t = pl.program_id(2)

        @pl.when(t == 0)
        def _():
            ds_scr[...] = jnp.zeros_like(ds_scr)

        is_lat = smap(t) >= 1
        rin, mask, maskT, mexp = _scan_masks(rev)
        for p in range(HPB):
            b, xx, eb, qd, ki, kt = _scan_decay(q_ref[0, p], k_ref[0, 0, p], g_ref[0, 0, p], rin, rev)
            qb, kib, ktb, vb = qd.astype(BF16), ki.astype(BF16), kt.astype(BF16), v_ref[0, p]
            dob = jnp.where(is_lat, do_ref[0, p], 0.0).astype(BF16)
            kt_exps = [_expand(_sub(ktb, s), mexp) for s in range(NSB)]
            uts = [_dot_tn(_sub(vb, s), kt_exps[s]) for s in range(NSB)]
            states, _ = _state_pass(st_ref[0, p, 0], eb, uts, rev)
            gts = [_dot_tn(_sub(dob, s), _expand(_sub(qb, s), mexp)) for s in range(NSB)]
            order = range(NCB) if rev else range(NCB - 1, -1, -1)
            dsp = [None] * NCB
            t2 = [None] * NCB
            dsc = ds_scr[p]
            for n in order:
                row = n * CH if rev else n * CH + CH - 1
                ebl = eb[row:row + 1, :]
                dsp[n] = dsc
                t2[n] = jnp.broadcast_to(ebl * _rsum(states[n] * dsc), (CH, HD))
                dsc = gts[n // CSB][:, (n % CSB) * HD:(n % CSB + 1) * HD] + ebl * dsc
            ds_scr[p] = dsc
            dqds, dkis, dkts = [], [], []
            for s in range(NSB):
                q_s, ki_s, v_s, do_s = _sub(qb, s), _sub(kib, s), _sub(vb, s), _sub(dob, s)
                dspb = _stack(dsp, s).astype(BF16)
                da = jnp.where(mask, _dot_nt(do_s, v_s), 0.0).astype(BF16)
                dat = jnp.where(maskT, _dot_nt(v_s, do_s), 0.0).astype(BF16)
                at = jnp.where(maskT, _dot_nt(ki_s, q_s), 0.0).astype(BF16)
                dqds.append(_dot(da, ki_s) + _own(_dot(do_s, _stack(states, s).astype(BF16)), mexp))
                dkis.append(_dot(dat, q_s))
                dv_ref[0, p, s * SB:(s + 1) * SB, :] = (_dot(at, do_s) + _dot_nt(kt_exps[s], dspb)).astype(BF16)
                dkts.append(_own(_dot(v_s, dspb), mexp))
            dqd, dki, dkt = (jnp.concatenate(parts, axis=0) for parts in (dqds, dkis, dkts))
            z = dkt * kt
            db = dqd * qd - dki * ki
            dq_ref[0, p] = (dqd * eb).astype(BF16)
            dk_ref[0, p] = (dki * jnp.exp(-b) + dkt * jnp.exp(xx)).astype(BF16)
            dg_ref[0, p] = (_chunk_total(db) + (db - z) + _chunk_scan(z - db, rin, not rev)
                            + jnp.concatenate(t2, axis=0)).astype(BF16)

    one = pl.BlockSpec((1, HPB, TM, HD), lambda b, h, t: (b, h, rmap(t), 0))
    two = pl.BlockSpec((1, 1, HPB, TM, HD), lambda b, h, t: (d, b, h, rmap(t), 0))
    lat = pl.BlockSpec((1, HPB, TM, HD), lambda b, h, t: (b, h, jnp.clip(rmap(t) - 1, 0, nb - 2), 0))
    shp = jax.ShapeDtypeStruct(Q.shape, BF16)
    return pl.pallas_call(
        body, name="scan_bwd_rev" if rev else "scan_bwd", grid=(B, NH // HPB, nb),
        in_specs=[one, two, one, two, pl.BlockSpec((1, HPB, 1, HD, HD), lambda b, h, t: (b, h, smap(t), 0, 0)), lat],
        out_specs=[one, one, one, one], out_shape=[shp, shp, shp, shp],
        scratch_shapes=[pltpu.VMEM((HPB, HD, HD), F32)],
        compiler_params=_cp(3))(Q, K, V, G, St, dO)


def _gnorm(O0, O1, F, gains, B, ntl):
    nt = ntl + 1

    def body(o0_ref, o1_ref, f_ref, gn_ref, m_ref):
        for h in range(NH):
            o = o0_ref[0, h] + o1_ref[0, h]
            r = lax.rsqrt(jnp.mean(o * o, axis=-1, keepdims=True) + NORM_EPS)
            gn = gn_ref[0:1, :] if h < 4 else gn_ref[1:2, :]
            gt = f_ref[:, h * HD:(h + 1) * HD]
            m_ref[:, h * HD:(h + 1) * HD] = (o * r * gn * (gt * _sigmoid(gt))).astype(BF16)

    ospec = pl.BlockSpec((1, NH, TM, HD), lambda b, i: (b, 0, i + 1, 0))
    return pl.pallas_call(
        body, name="gated_norm", grid=(B, ntl),
        in_specs=[ospec, ospec, pl.BlockSpec((TM, D), lambda b, i: (b * nt + 1 + i, 0)),
                  pl.BlockSpec((2, HD), lambda b, i: (0, 0))],
        out_specs=pl.BlockSpec((TM, D), _rowmap(ntl, 0)),
        out_shape=jax.ShapeDtypeStruct((B * ntl * TM, D), BF16), compiler_params=_cp(2))(O0, O1, F, gains)


def _gnorm_bwd(dM, O0, O1, F, gains, B, ntl):
    nt = ntl + 1

    def body(dm_ref, o0_ref, o1_ref, f_ref, gn_ref, do_ref, dgt_ref, dgn_ref):
        b_, i = pl.program_id(0), pl.program_id(1)

        @pl.when((b_ == 0) & (i == 0))
        def _():
            dgn_ref[...] = jnp.zeros_like(dgn_ref)

        for h in range(NH):
            o = o0_ref[0, h] + o1_ref[0, h]
            r = lax.rsqrt(jnp.mean(o * o, axis=-1, keepdims=True) + NORM_EPS)
            y = o * r
            gn = gn_ref[0:1, :] if h < 4 else gn_ref[1:2, :]
            gt = f_ref[:, h * HD:(h + 1) * HD]
            s = _sigmoid(gt)
            dm = dm_ref[:, h * HD:(h + 1) * HD].astype(F32)
            don = dm * (gt * s)
            dgt_ref[:, h * HD:(h + 1) * HD] = (dm * (y * gn) * (s * (1.0 + gt * (1.0 - s)))).astype(BF16)
            row = 0 if h < 4 else 1
            dgn_ref[row:row + 1, :] += _rsum(don * y)
            dy = don * gn
            do_ref[0, h] = (r * (dy - y * jnp.mean(dy * y, axis=-1, keepdims=True))).astype(BF16)

    ospec = pl.BlockSpec((1, NH, TM, HD), lambda b, i: (b, 0, i + 1, 0))
    return pl.pallas_call(
        body, name="gated_norm_bwd", grid=(B, ntl),
        in_specs=[pl.BlockSpec((TM, D), _rowmap(ntl, 0)), ospec, ospec,
                  pl.BlockSpec((TM, D), lambda b, i: (b * nt + 1 + i, 0)), pl.BlockSpec((2, HD), lambda b, i: (0, 0))],
        out_specs=[pl.BlockSpec((1, NH, TM, HD), lambda b, i: (b, 0, i, 0)), pl.BlockSpec((TM, D), _rowmap(ntl, 0)),
                   pl.BlockSpec((2, HD), lambda b, i: (0, 0))],
        out_shape=[jax.ShapeDtypeStruct((B, NH, ntl * TM, HD), BF16), jax.ShapeDtypeStruct((B * ntl * TM, D), BF16),
                   jax.ShapeDtypeStruct((2, HD), F32)],
        compiler_params=_cp(2))(dM, O0, O1, F, gains)


def _sincos_2d(rows, width, dim):
    quarter = dim // 4
    omega = 1.0 / 10000.0 ** (jnp.arange(quarter, dtype=F32) / quarter)

    def emb(n):
        a = jnp.arange(n).astype(F32)[:, None] * omega[None, :]
        return jnp.concatenate([jnp.sin(a), jnp.cos(a)], axis=-1)

    er = jnp.broadcast_to(emb(rows)[:, None, :], (rows, width, dim // 2))
    ec = jnp.broadcast_to(emb(width)[None, :, :], (rows, width, dim // 2))
    return jnp.concatenate([er, ec], axis=-1).reshape(rows * width, dim)


def _pad_heads(w):
    k = w.shape[0]
    return jnp.pad(w.reshape(k, 4, GLA_DK), ((0, 0), (0, 0), (0, HD - GLA_DK))).reshape(k, 4 * HD)


def _unpad_heads(w):
    k = w.shape[0]
    return w.reshape(k, 4, HD)[:, :, :GLA_DK].reshape(k, 4 * GLA_DK)


MIX_N = 1032
MIX_NP = 1152
_SEGS = ([(64 * h, 64, GQ + HD * h) for h in range(4)] + [(256 + 64 * h, 64, GK + HD * h) for h in range(4)]
         + [(512, 512, GV), (1024, 512, GG), (1536, 32, LR), (1568, 512, RQ), (2080, 512, RFF), (2592, 512, RFB),
            (3104, 512, RI), (3616, 512, RG)])


def _mix_in_to_padded(ps):
    k = ps.shape[1]
    parts, pos = [], 0
    for g0, ln, s0 in sorted(_SEGS, key=lambda s: s[2]):
        if s0 > pos:
            parts.append(jnp.zeros((k, s0 - pos), ps.dtype))
        for j in range(4):
            lo, hi = max(g0, j * MIX_N), min(g0 + ln, (j + 1) * MIX_N)
            if lo < hi:
                parts.append(ps[j][:, lo - j * MIX_N:hi - j * MIX_N])
        pos = s0 + ln
    parts.append(jnp.zeros((k, MIXP - pos), ps.dtype))
    return jnp.concatenate(parts, axis=1)


def _mix_in_from_padded(g):
    k = g.shape[0]
    shards = []
    for j in range(4):
        parts = []
        for g0, ln, s0 in sorted(_SEGS):
            lo, hi = max(g0, j * MIX_N), min(g0 + ln, (j + 1) * MIX_N)
            if lo < hi:
                parts.append(g[:, s0 + lo - g0:s0 + hi - g0])
        parts.append(jnp.zeros((k, MIX_NP - MIX_N), g.dtype))
        shards.append(jnp.concatenate(parts, axis=1))
    return jnp.stack(shards)


def _local_step(x, ctx, tgt, mvec, weights_for, ln_gain, ln_bias, a2f, a2b, abf, abb, lb, gng, gnh, on_grads, on_sent):
    B, T, _ = x.shape
    assert ctx.shape[1] == TM and T % TM == 0
    ntl = T // TM
    nt = ntl + 1
    C, L, CL = (nt, 0), (ntl, 0), (nt, 1)
    pos = _sincos_2d(T // 64, 64, D)
    gbs = [jnp.stack([ln_gain[i], ln_bias[i]]) for i in range(3)]
    a2p = jnp.zeros((2, HD, 4 * HD), F32)
    a2p = a2p.at[0, 0:16].set(_pad_heads(a2f)).at[1, 16:32].set(_pad_heads(a2b)).astype(BF16)
    biasp = jnp.concatenate([_pad_heads(abf.reshape(1, -1)), _pad_heads(abb.reshape(1, -1))], axis=0)
    gains = jnp.concatenate([gng.reshape(1, HD), gnh.reshape(1, HD)], axis=0)

    X0, h0 = _embed_lnmod(x, ctx, pos, mvec)
    w1i, w1o = weights_for("ffn1", h0)
    u0, a0 = _ffn_in(h0, C, w1i, B, nt, "ffn1_in")
    wmp, wmo = weights_for("mix", a0)
    y0 = _mm_nn(a0, C, w1o, B, nt, BF16, "ffn1_out")
    X1, h1 = _resid_lnmod(X0, C, y0, mvec, True, 0, 0.5, gbs[0], B, nt, "resid0_lnmod1")
    Q, K, V, G, Zm = _mix_in_features(h1, wmp, a2p, biasp, lb, B, nt)
    w2i, w2o = weights_for("ffn2", Zm)
    O0, S0 = _scan_fwd(Q, K, V, G, False, B)
    O1, S1 = _scan_fwd(Q, K, V, G, True, B)
    merged = _gnorm(O0, O1, Zm, gains, B, ntl)
    y1 = _mm_nn(merged, L, wmo, B, ntl, BF16, "mix_out")
    X2, h2 = _resid_lnmod(X1, CL, y1, mvec, False, 1, 1.0, gbs[1], B, ntl, "resid1_lnmod2")
    u2, a2 = _ffn_in(h2, L, w2i, B, ntl, "ffn2_in")
    y2 = _mm_nn(a2, L, w2o, B, ntl, BF16, "ffn2_out")

    dx2r, dy2, dgate2, dgb2, lsum = _tail(X2, y2, mvec, gbs[2], tgt, 2, 0.5, B, ntl)
    loss = (0.5 / D) * jnp.sum(lsum)
    du2 = _ffn_out_dx(dy2, w2o, u2, "ffn2_out_dx")
    g_w2o = _mm_tn(a2, dy2, "ffn2_out_dw")
    dh2 = _mm_nt(du2, w2i, "ffn2_in_dx", out_dtype=BF16)
    g_w2i = _mm_tn(h2, du2, "ffn2_in_dw", shards=4)
    tok = on_grads("ffn2", (g_w2i, g_w2o))
    dx1r, dy1, dgate1, dgb1, dss2 = _lnmod_resid_bwd(dh2, X2, L, dx2r, X1, CL, y1, mvec, False, 2, 1.0,
                                                     gbs[1] + tok[0, 0], B, ntl, "lnmod2_resid1_bwd")
    tok = on_sent("ffn2", dy1)
    dmerged = _mm_nt(dy1, wmo, "mix_out_dx", dep=tok, out_dtype=BF16)
    g_wmo = _mm_tn(merged, dy1, "mix_out_dw", tn=D)
    dO, dgates, dgains = _gnorm_bwd(dmerged, O0, O1, Zm, gains, B, ntl)
    dQ0, dK0, dV0, dG0 = _scan_bwd(Q, K, V, G, S0, dO, False, B)
    dQ1, dK1, dV1, dG1 = _scan_bwd(Q, K, V, G, S1, dO, True, B)
    dF, da2p, dbiasp, dlb = _features_bwd(Zm, a2p, biasp, lb, dQ0, dQ1, dK0, dK1, dV0, dV1, dG0, dG1, dgates, B)
    dh1 = _mm_nt(dF, wmp, "mix_in_dx", out_dtype=BF16)
    g_wmp = _mm_tn(h1, dF, "mix_in_dw", tn=MIXP // 4)
    tok = on_grads("mix", (g_wmp, g_wmo))
    dx0r, dy0, dgate0, dgb0, dss1 = _lnmod_resid_bwd(dh1, X1, C, dx1r, X0, C, y0, mvec, True, 1, 0.5,
                                                     gbs[0] + tok[0, 0], B, nt, "lnmod1_resid0_bwd")
    tok = on_sent("mix", dy0)
    du0 = _ffn_out_dx(dy0, w1o, u0, "ffn1_out_dx", dep=tok)
    g_w1o = _mm_tn(a0, dy0, "ffn1_out_dw")
    g_w1i = _mm_tn(h0, du0, "ffn1_in_dw", shards=4)
    tok = on_grads("ffn1", (g_w1i, g_w1o))
    dh0 = _mm_nt(du0, w1i, "ffn1_in_dx", dep=tok, out_dtype=BF16)
    tok = on_sent("ffn1", dh0)
    grad_x, dss0 = _lnmod0_bwd(dh0, X0, mvec + tok[0, 0], dx0r, x.shape, B, nt)

    zero_ctx = lambda a: a.at[:, 0].set(0.0)
    dm = jnp.concatenate([dss0, dgate0, dss1, zero_ctx(dgate1), zero_ctx(dss2), zero_ctx(dgate2)], axis=2)
    small = dict(
        ln_gain=jnp.stack([dgb0[0], dgb1[0], dgb2[0]]), ln_bias=jnp.stack([dgb0[1], dgb1[1], dgb2[1]]),
        a2f=_unpad_heads(da2p[0, 0:16]), a2b=_unpad_heads(da2p[1, 16:32]),
        abf=_unpad_heads(dbiasp[0:1]), abb=_unpad_heads(dbiasp[1:2]), lb=dlb, gng=dgains[0], gnh=dgains[1])
    return loss, grad_x, dm, small


def _small_allgather(xs, name):
    r, n = xs.shape

    def body(x_ref, out_ref, send_sems, recv_sems, local_sem):
        x, y, c = lax.axis_index("x"), lax.axis_index("y"), lax.axis_index("c")
        me, sibling = (x, y, c), (x, y, 1 - c)
        chips = [(1 - x, y), (x, 1 - y), (1 - x, 1 - y)]

        def rows(px, py, pc):
            return out_ref.at[pl.ds((4 * px + 2 * py + pc) * r, r), :]

        def copy(k, block, to, src=None):
            return pltpu.make_async_remote_copy(
                src_ref=rows(*block) if src is None else src, dst_ref=rows(*block),
                send_sem=send_sems.at[k], recv_sem=recv_sems.at[k], device_id=to, device_id_type=MESH)

        mine = pltpu.make_async_copy(x_ref, rows(*me), local_sem)
        mine.start()
        first = [copy(0, me, sibling, src=x_ref)]
        first += [copy(1 + j, me, (*chip, c), src=x_ref) for j, chip in enumerate(chips)]
        for cp in first:
            cp.start()
        passed = [copy(4 + j, (*chip, c), sibling) for j, chip in enumerate(chips)]
        for j, chip in enumerate(chips):
            copy(1 + j, (*chip, c), me).wait_recv()
            passed[j].start()
        copy(0, sibling, me).wait_recv()
        for j, chip in enumerate(chips):
            copy(4 + j, (*chip, 1 - c), me).wait_recv()
        for cp in first + passed:
            cp.wait_send()
        mine.wait()

    out = pl.pallas_call(
        body, name=name,
        out_shape=jax.ShapeDtypeStruct((8 * r, n), xs.dtype),
        in_specs=[pl.BlockSpec(memory_space=pltpu.VMEM)],
        out_specs=pl.BlockSpec(memory_space=pltpu.VMEM),
        scratch_shapes=[pltpu.SemaphoreType.DMA((7,)), pltpu.SemaphoreType.DMA((7,)), pltpu.SemaphoreType.DMA],
        compiler_params=pltpu.CompilerParams(vmem_limit_bytes=VMEM_LIMIT))(xs)
    return out.reshape(8, r, n)


def _gather_flat(v, name):
    n = v.shape[0]
    npad = -(-n // 1024) * 1024
    g = _small_allgather(jnp.pad(v, (0, npad - n)).reshape(8, npad // 8), name)
    return g.reshape(8, npad)[:, :n]


HBM_SPEC = pl.BlockSpec(memory_space=pltpu.HBM)
SEM_SPEC = pl.BlockSpec(memory_space=pltpu.SEMAPHORE)
DATAFLOW = pltpu.SideEffectType.DATAFLOW_SIDE_EFFECTING


def _gather_copies(xs, outs, send_sems, recv_sems):
    x, y, c = lax.axis_index("x"), lax.axis_index("y"), lax.axis_index("c")
    dests = [(x, y, 1 - c), (1 - x, y, c), (x, 1 - y, c), (1 - x, 1 - y, c)]
    return [pltpu.make_async_remote_copy(
        src_ref=xs[w], dst_ref=outs[w].at[4 * x + 2 * y + c], send_sem=send_sems[4 * w + k],
        recv_sem=recv_sems[4 * w + k], device_id=dests[k], device_id_type=MESH)
        for w in range(len(xs)) for k in range(4)]


def _split_start(copies, per_w, srcs, land_lead, after, name):
    n = len(srcs)
    m = per_w * n
    lands = [lax.empty((land_lead,) + s.shape[-2:], s.dtype) for s in srcs]
    deps = [] if after is None else [after]

    def body(*refs):
        xs, ls, outs = refs[:n], refs[n:2 * n], refs[2 * n + len(deps):]
        for cp in copies(xs, ls, outs[:m], outs[m:2 * m]):
            cp.start()
        token = outs[2 * m + 2 * n]
        token[...] = jnp.zeros_like(token)

    outs = pl.pallas_call(
        body, name=name,
        out_shape=([pltpu.SemaphoreType.DMA(())] * (2 * m) + [pltpu.HBM(a.shape, a.dtype) for a in srcs + lands]
                   + [jax.ShapeDtypeStruct((8, 128), F32)]),
        in_specs=[HBM_SPEC] * (2 * n) + [pl.BlockSpec(memory_space=pl.ANY)] * len(deps),
        out_specs=[SEM_SPEC] * (2 * m) + [HBM_SPEC] * (2 * n) + [pl.BlockSpec(memory_space=pltpu.VMEM)],
        input_output_aliases={w: 2 * m + w for w in range(2 * n)},
        compiler_params=pltpu.CompilerParams(has_side_effects=DATAFLOW),
    )(*[pltpu.with_memory_space_constraint(a, pltpu.HBM) for a in srcs + lands], *deps)
    return outs[:2 * m], outs[2 * m:2 * m + n], outs[2 * m + n:2 * m + 2 * n], outs[2 * m + 2 * n]


def _split_wait(copies, per_w, sems, x_thru, l_thru, after, name):
    n = len(x_thru)
    m = per_w * n

    def body(*refs):
        xs, ls, ss = refs[:n], refs[n:2 * n], refs[2 * n:2 * n + 2 * m]
        for cp in copies(xs, ls, ss[:m], ss[m:]):
            cp.wait_send()
            cp.wait_recv()

    outs = pl.pallas_call(
        body, name=name,
        out_shape=[pltpu.HBM(a.shape, a.dtype) for a in list(x_thru) + list(l_thru)],
        in_specs=[HBM_SPEC] * (2 * n) + [SEM_SPEC] * (2 * m) + [pl.BlockSpec(memory_space=pl.ANY)],
        out_specs=[HBM_SPEC] * (2 * n),
        input_output_aliases={w: w for w in range(2 * n)},
        compiler_params=pltpu.CompilerParams(has_side_effects=DATAFLOW),
    )(*x_thru, *l_thru, *sems, after)
    return outs[:n], outs[n:]


def _gather_forward(gathered, name):
    n = len(gathered)

    def body(*refs):
        outs = refs[n:2 * n]
        send_sems, recv_sems = refs[2 * n:]
        x, y, c = lax.axis_index("x"), lax.axis_index("y"), lax.axis_index("c")
        chips = [(1 - x, y), (x, 1 - y), (1 - x, 1 - y), (x, y)]

        def copy(w, j, pc):
            px, py = chips[j]
            slot = outs[w].at[4 * px + 2 * py + pc]
            return pltpu.make_async_remote_copy(
                src_ref=slot, dst_ref=slot, send_sem=send_sems.at[4 * w + j], recv_sem=recv_sems.at[4 * w + j],
                device_id=(x, y, 1 - c), device_id_type=MESH)

        sends = [copy(w, j, 1 - c if j == 3 else c) for w in range(n) for j in range(4)]
        for cp in sends:
            cp.start()
        for w in range(n):
            for j in range(4):
                copy(w, j, c if j == 3 else 1 - c).wait_recv()
        for cp in sends:
            cp.wait_send()

    any_spec = pl.BlockSpec(memory_space=pl.ANY)
    return pl.pallas_call(
        body, name=name,
        out_shape=[jax.ShapeDtypeStruct(g.shape, g.dtype) for g in gathered],
        in_specs=[any_spec] * n, out_specs=[any_spec] * n,
        input_output_aliases={w: w for w in range(n)},
        scratch_shapes=[pltpu.SemaphoreType.DMA((4 * n,)), pltpu.SemaphoreType.DMA((4 * n,))],
    )(*gathered)


def _pair_copies(gs, ls, send_sems, recv_sems):
    x, y, c = lax.axis_index("x"), lax.axis_index("y"), lax.axis_index("c")
    return [pltpu.make_async_remote_copy(
        src_ref=gs[w].at[2 * j + 1 - c], dst_ref=ls[w].at[j], send_sem=send_sems[4 * w + j],
        recv_sem=recv_sems[4 * w + j], device_id=(x, y, 1 - c), device_id_type=MESH)
        for w in range(len(gs)) for j in range(4)]


def _all_copies(xs, ls, send_sems, recv_sems):
    x, y, c = lax.axis_index("x"), lax.axis_index("y"), lax.axis_index("c")
    flips = [(a, b, e) for a in (0, 1) for b in (0, 1) for e in (0, 1)][1:]
    return [pltpu.make_async_remote_copy(
        src_ref=xs[0], dst_ref=ls[0].at[4 * x + 2 * y + c], send_sem=send_sems[k], recv_sem=recv_sems[k],
        device_id=(1 - x if a else x, 1 - y if b else y, 1 - c if e else c), device_id_type=MESH)
        for k, (a, b, e) in enumerate(flips)]


def _share_copies(fs, ls, send_sems, recv_sems):
    x, y, c = lax.axis_index("x"), lax.axis_index("y"), lax.axis_index("c")
    return [pltpu.make_async_remote_copy(
        src_ref=fs[w].at[c], dst_ref=fs[w].at[c], send_sem=send_sems[w], recv_sem=recv_sems[w],
        device_id=(x, y, 1 - c), device_id_type=MESH) for w in range(len(fs))]


def _chip_copies(hs, ls, send_sems, recv_sems):
    x, y, c = lax.axis_index("x"), lax.axis_index("y"), lax.axis_index("c")
    chips = [(1 - x, y), (x, 1 - y), (1 - x, 1 - y)]
    return [pltpu.make_async_remote_copy(
        src_ref=hs[w].at[2 * px + py], dst_ref=ls[w].at[k], send_sem=send_sems[3 * w + k],
        recv_sem=recv_sems[3 * w + k], device_id=(px, py, c), device_id_type=MESH)
        for w in range(len(hs)) for k, (px, py) in enumerate(chips)]


def _rs_add_pair(g8, r4, c, name):
    R, n = g8.shape[1:]
    rb = R // 2

    def body(c_ref, g_ref, r_ref, o_ref):
        o_ref[...] = (g_ref[...] + r_ref[...]).astype(BF16)

    spec = pl.BlockSpec((1, rb, n), lambda j, i, c_ref: (j, i, 0))
    return pl.pallas_call(
        body, name=name,
        grid_spec=pltpu.PrefetchScalarGridSpec(
            num_scalar_prefetch=1, grid=(4, R // rb),
            in_specs=[pl.BlockSpec((1, rb, n), lambda j, i, c_ref: (2 * j + c_ref[0], i, 0)), spec],
            out_specs=spec),
        out_shape=jax.ShapeDtypeStruct((4, R, n), BF16), compiler_params=_cp(2))(c, g8, r4)


def _rs_add_chips(g8, r4, r3, cj, name):
    R, n = g8.shape[1:]
    rb = R // 2

    def body(cj_ref, g_ref, p_ref, r_ref, o_ref):
        own = g_ref[0] + p_ref[0]
        o_ref[0] = ((own + r_ref[0].astype(F32)) + r_ref[1].astype(F32)) + r_ref[2].astype(F32)

    return pl.pallas_call(
        body, name=name,
        grid_spec=pltpu.PrefetchScalarGridSpec(
            num_scalar_prefetch=1, grid=(R // rb,),
            in_specs=[pl.BlockSpec((1, rb, n), lambda i, cj_ref: (2 * cj_ref[1] + cj_ref[0], i, 0)),
                      pl.BlockSpec((1, rb, n), lambda i, cj_ref: (cj_ref[1], i, 0)),
                      pl.BlockSpec((3, rb, n), lambda i, cj_ref: (0, i, 0))],
            out_specs=pl.BlockSpec((1, rb, n), lambda i, cj_ref: (cj_ref[0], i, 0))),
        out_shape=jax.ShapeDtypeStruct((2, R, n), F32), compiler_params=_cp(1))(cj, g8, r4, r3)


def _sum8(g):
    n = g.shape[1]

    def body(g_ref, o_ref):
        acc = g_ref[0:1, :]
        for k in range(1, 8):
            acc = acc + g_ref[k:k + 1, :]
        o_ref[...] = acc

    return pl.pallas_call(body, name="sum_devices", out_shape=jax.ShapeDtypeStruct((1, n), F32),
                          compiler_params=pltpu.CompilerParams(vmem_limit_bytes=VMEM_LIMIT))(g)


ADA_ROWS = 64


def _ada_fwd(cs, w, b):
    n = w.shape[1]

    def body(c_ref, w_ref, b_ref, o_ref):
        cv = c_ref[...]
        s = (cv * _sigmoid(cv)).astype(BF16)
        o_ref[...] = _dot(s, w_ref[...].astype(BF16)) + b_ref[...]

    return pl.pallas_call(body, name="ada_fwd", out_shape=jax.ShapeDtypeStruct((ADA_ROWS, n), F32),
                          compiler_params=pltpu.CompilerParams(vmem_limit_bytes=VMEM_LIMIT))(cs, w, b)


def _mod_exchange(m_cols):
    n = m_cols.shape[1]

    def body(m_ref, out_ref, send_sems, recv_sems, local_sem):
        x, y, c = lax.axis_index("x"), lax.axis_index("y"), lax.axis_index("c")
        chips = [(1 - x, y), (x, 1 - y), (1 - x, 1 - y)]

        def group(px, py):
            return m_ref.at[pl.ds(pl.multiple_of(8 * (4 * px + 2 * py + c), 8), 8), :]

        mine = pltpu.make_async_copy(group(x, y), out_ref.at[2 * x + y], local_sem)
        mine.start()
        cps = [pltpu.make_async_remote_copy(
            src_ref=group(px, py), dst_ref=out_ref.at[2 * x + y], send_sem=send_sems.at[k], recv_sem=recv_sems.at[k],
            device_id=(px, py, c), device_id_type=MESH) for k, (px, py) in enumerate(chips)]
        for cp in cps:
            cp.start()
        for cp in cps:
            cp.wait_recv()
        for cp in cps:
            cp.wait_send()
        mine.wait()

    return pl.pallas_call(
        body, name="mod_exchange", out_shape=jax.ShapeDtypeStruct((4, 8, n), F32),
        in_specs=[pl.BlockSpec(memory_space=pltpu.VMEM)], out_specs=pl.BlockSpec(memory_space=pltpu.VMEM),
        scratch_shapes=[pltpu.SemaphoreType.DMA((3,)), pltpu.SemaphoreType.DMA((3,)), pltpu.SemaphoreType.DMA],
        compiler_params=pltpu.CompilerParams(vmem_limit_bytes=VMEM_LIMIT))(m_cols)


def _ada_bwd(cs, w, dm):
    n = w.shape[1]

    def body(c_ref, w_ref, dm_ref, gw_ref, dc_ref):
        cv = c_ref[...]
        s = (cv * _sigmoid(cv)).astype(BF16)
        gw_ref[...] = _dot_tn(s, dm_ref[...].astype(BF16))
        dc_ref[...] = _dot_nt(dm_ref[0:8, :].astype(BF16), w_ref[...].astype(BF16))

    return pl.pallas_call(
        body, name="ada_bwd",
        out_shape=[jax.ShapeDtypeStruct((D, n), F32), jax.ShapeDtypeStruct((8, D), F32)],
        compiler_params=pltpu.CompilerParams(vmem_limit_bytes=VMEM_LIMIT))(cs, w, dm)


def _adamw(w, g, m, v, name):
    r, c = w.shape
    rb = r
    if r % 8 == 0 and r * c * 4 > (1 << 20):
        rb = 8
        for cand in range(8, r, 8):
            if r % cand == 0 and cand * c * 4 <= (1 << 20):
                rb = cand

    def body(w_ref, g_ref, m_ref, v_ref, go_ref, d_ref, nm_ref, nv_ref):
        gv = g_ref[...]
        go_ref[...] = gv
        mn = ADAM_B1 * m_ref[...] + (1.0 - ADAM_B1) * gv
        vn = ADAM_B2 * v_ref[...] + (1.0 - ADAM_B2) * (gv * gv)
        m_hat = mn / (1.0 - ADAM_B1 ** ADAM_STEP)
        v_hat = vn / (1.0 - ADAM_B2 ** ADAM_STEP)
        d_ref[...] = -ADAM_LR * (m_hat / (jnp.sqrt(v_hat) + ADAM_EPS) + ADAM_WD * w_ref[...])
        nm_ref[...] = mn
        nv_ref[...] = vn

    spec = pl.BlockSpec((rb, c), lambda i: (i, 0))
    shp = jax.ShapeDtypeStruct((r, c), F32)
    return pl.pallas_call(body, name=name, grid=(r // rb,), in_specs=[spec] * 4, out_specs=[spec] * 4,
                          out_shape=[shp] * 4, compiler_params=_cp(1))(w, g, m, v)


BIG = ("ffn1_w_in", "ffn1_w_out", "w_mix_in", "w_mix_out", "ffn2_w_in", "ffn2_w_out")


def _half_rows(w, c):
    half = w.shape[0] // 2
    return lax.dynamic_slice_in_dim(w, c * half, half, axis=0)


def _lower_bounds(logits):
    return jnp.cumsum(jax.nn.softmax(logits.astype(F32), axis=1), axis=1)[:, 0]


def kernel(x, c, ctx, c_ctx, w_ada, b_ada, ln_gain, ln_bias, ffn1_w_in, ffn1_w_out, w_mix_in, gla_a2_fwd, gla_a2_bwd, gla_a_bias_fwd, gla_a_bias_bwd, hgrn_lb_logits, gla_norm_gain, hgrn_norm_gain, w_mix_out, ffn2_w_in, ffn2_w_out, loss_target, m_c_ctx, m_w_ada, m_b_ada, m_ln_gain, m_ln_bias, m_ffn1_w_in, m_ffn1_w_out, m_w_mix_in, m_gla_a2_fwd, m_gla_a2_bwd, m_gla_a_bias_fwd, m_gla_a_bias_bwd, m_hgrn_lb_logits, m_gla_norm_gain, m_hgrn_norm_gain, m_w_mix_out, m_ffn2_w_in, m_ffn2_w_out, v_c_ctx, v_w_ada, v_b_ada, v_ln_gain, v_ln_bias, v_ffn1_w_in, v_ffn1_w_out, v_w_mix_in, v_gla_a2_fwd, v_gla_a2_bwd, v_gla_a_bias_fwd, v_gla_a_bias_bwd, v_hgrn_lb_logits, v_gla_norm_gain, v_hgrn_norm_gain, v_w_mix_out, v_ffn2_w_in, v_ffn2_w_out):
    xi, yi, ci = lax.axis_index("x"), lax.axis_index("y"), lax.axis_index("c")
    chip = 2 * xi + yi
    dev = 2 * chip + ci
    B = x.shape[0]
    weights = dict(ffn1_w_in=ffn1_w_in[0], ffn1_w_out=ffn1_w_out[0], w_mix_in=w_mix_in[0], w_mix_out=w_mix_out[0],
                   ffn2_w_in=ffn2_w_in[0], ffn2_w_out=ffn2_w_out[0])

    mine = jnp.concatenate([c.reshape(-1), ln_gain.reshape(-1), ln_bias.reshape(-1), gla_a2_fwd.reshape(-1),
                            gla_a2_bwd.reshape(-1), hgrn_lb_logits.reshape(-1)])
    g1 = _gather_flat(mine, "gather_cond")
    nc = B * D
    c_all = g1[:, :nc].reshape(8 * B, D)
    per_chip = g1[0::2, nc:]
    o = 0

    def take(shape, axis):
        nonlocal o
        n = int(np.prod(shape))
        parts = per_chip[:, o:o + n].reshape((4,) + shape)
        o += n
        return jnp.concatenate([parts[j] for j in range(4)], axis=axis)

    ln_gain_f = take((3, 256), 1)
    ln_bias_f = take((3, 256), 1)
    a2f_f = take((16, 64), 1)
    a2b_f = take((16, 64), 1)
    lbl_f = take((2, 2, 128), 2)
    lb, lb_vjp = jax.vjp(_lower_bounds, lbl_f)

    assert B + 1 <= 8
    cs = jnp.concatenate([c_all.reshape(8, B, D), jnp.broadcast_to(c_ctx.reshape(1, 1, D), (8, 1, D)),
                          jnp.zeros((8, 7 - B, D), F32)], axis=1).reshape(ADA_ROWS, D)
    ncol = w_ada.shape[2]
    b_cols = lax.dynamic_slice_in_dim(b_ada, chip * ncol, ncol, axis=1)
    m4 = _mod_exchange(_ada_fwd(cs, w_ada[0], b_cols))
    m_all = jnp.concatenate([m4[j] for j in range(4)], axis=1)
    m_lat = m_all[:B].reshape(B, 1, N_MOD, D)
    m_ctx = jnp.broadcast_to(m_all[B].reshape(1, 1, N_MOD, D), (B, 1, N_MOD, D))

    groups = dict(ffn2=("ffn2_w_in", "ffn2_w_out"), mix=("w_mix_in", "w_mix_out"), ffn1=("ffn1_w_in", "ffn1_w_out"))
    shards = dict(weights, w_mix_in=jnp.pad(weights["w_mix_in"], ((0, 0), (0, MIX_NP - MIX_N))))
    blks = {k: _half_rows(shards[k], ci).astype(BF16) for k in BIG}
    gathering = {}
    token = m_all
    for group in ("ffn1", "mix", "ffn2"):
        sems, x_thru, l_thru, token = _split_start(_gather_copies, 4, [blks[k] for k in groups[group]], 8, token,
                                                   "weight_gather_start_" + group)
        gathering[group] = (sems, x_thru, l_thru)
    mvec = jnp.concatenate([m_ctx, m_lat], axis=1) + token[0, 0]

    def weights_for(group, after):
        names = groups[group]
        _, got = _split_wait(_gather_copies, 4, *gathering[group], after, "weight_gather_wait_" + group)
        w_in, w_out = _gather_forward(got, "weight_gather_forward_" + group)
        if group == "mix":
            return _mix_in_to_padded(w_in.reshape(4, D, MIX_NP)), w_out.reshape(-1, D)
        return w_in.reshape((4,) + shards[names[0]].shape), w_out.reshape(-1, D)

    cvec = ci.reshape(1).astype(jnp.int32)
    cjvec = jnp.stack([ci, chip]).astype(jnp.int32)
    in_flight = {}

    def on_grads(group, gs):
        names = groups[group]
        if group == "mix":
            gs = (_mix_in_from_padded(gs[0]), gs[1])
        g8s = [g.reshape((8, shards[k].shape[0] // 2, shards[k].shape[1])) for k, g in zip(names, gs)]
        sems, g_thru, l_thru, token = _split_start(_pair_copies, 4, g8s, 4, None, "grad_pair_start_" + group)
        in_flight[group] = (sems, g_thru, l_thru)
        return token

    def on_sent(group, after):
        g8s, r4s = _split_wait(_pair_copies, 4, *in_flight[group], after, "grad_pair_wait_" + group)
        h4s = [_rs_add_pair(g, r, cvec, "grad_pair_add_" + k) for k, g, r in zip(groups[group], g8s, r4s)]
        sems, h_thru, l_thru, token = _split_start(_chip_copies, 3, h4s, 3, None, "grad_chip_start_" + group)
        in_flight[group] = (g8s, r4s, sems, h_thru, l_thru)
        return token

    loss_l, grad_x, dm, small = _local_step(
        x, ctx, loss_target, mvec, weights_for, ln_gain_f, ln_bias_f, a2f_f, a2b_f,
        gla_a_bias_fwd, gla_a_bias_bwd, lb, gla_norm_gain, hgrn_norm_gain, on_grads, on_sent)
    loss = lax.psum(loss_l, ("x", "y", "c"))

    dm_lat = dm[:, 1].reshape(B, N_MOD * D)
    dm_ctx = jnp.sum(dm[:, 0], axis=0).reshape(N_MOD * D)
    keys = ("ln_gain", "ln_bias", "a2f", "a2b", "abf", "abb", "lb", "gng", "gnh")
    flat = jnp.concatenate([dm_lat.reshape(-1), dm_ctx] + [small[k].reshape(-1) for k in keys])
    nflat = flat.shape[0]
    npad = -(-nflat // 1024) * 1024
    flat2 = jnp.pad(flat, (0, npad - nflat)).reshape(8, npad // 8)
    s_sems, s_src, s_land, tok = _split_start(_all_copies, 7, [flat2], 8, None, "small_grads_start")

    fin = {}
    for group in ("ffn2", "mix", "ffn1"):
        g8s, r4s, sems, h_thru, l_thru = in_flight[group]
        _, r3s = _split_wait(_chip_copies, 3, sems, h_thru, l_thru, tok, "grad_chip_wait_" + group)
        for k, g, r4, r3 in zip(groups[group], g8s, r4s, r3s):
            fin[k] = _rs_add_chips(g, r4, r3, cjvec, "grad_chip_add_" + k)
    p_sems, p_src, p_land, tok = _split_start(_share_copies, 1, [fin[k] for k in BIG], 1, None, "grad_pair_share_start")

    _, (land,) = _split_wait(_all_copies, 7, s_sems, s_src, s_land, tok, "small_grads_wait")
    g3 = lax.dynamic_update_index_in_dim(land, flat2, dev, 0).reshape(8, npad)[:, :nflat]
    nlat = B * N_MOD * D
    dm_all = g3[:, :nlat].reshape(8 * B, N_MOD * D)
    tot = _sum8(g3[:, nlat:])[0]
    dmc_tot = tot[:N_MOD * D]
    o = N_MOD * D
    sg = {}
    for k in keys:
        n = int(np.prod(small[k].shape))
        sg[k] = tot[o:o + n].reshape(small[k].shape)
        o += n
    ctx_rows = jnp.zeros((8, 1, N_MOD * D), F32).at[0, 0].set(dmc_tot)
    dm_rows = jnp.concatenate([dm_all.reshape(8, B, -1), ctx_rows, jnp.zeros((8, 7 - B, N_MOD * D), F32)],
                              axis=1).reshape(ADA_ROWS, N_MOD * D)
    g_b_ada = (jnp.sum(dm_all, axis=0) + dmc_tot).reshape(1, N_MOD * D)
    g_w_ada, dcc = _ada_bwd(cs, w_ada[0], lax.dynamic_slice_in_dim(dm_rows, chip * ncol, ncol, axis=1))
    g4 = _gather_flat(dcc[B], "gather_cctx")
    dsilu = ((g4[0] + g4[2]) + g4[4]) + g4[6]
    sc = _sigmoid(c_ctx)
    g_c_ctx = dsilu * (sc * (1.0 + c_ctx * (1.0 - sc)))
    (g_lbl,) = lb_vjp(sg["lb"])

    def cols(a, n, axis):
        return lax.dynamic_slice_in_dim(a, chip * n, n, axis=axis)

    shared, _ = _split_wait(_share_copies, 1, p_sems, p_src, p_land, g_w_ada, "grad_pair_share_wait")
    gsh = {k: both.reshape(shards[k].shape)[:, :weights[k].shape[1]] for k, both in zip(BIG, shared)}

    grads = dict(
        c_ctx=g_c_ctx, w_ada=g_w_ada[None], b_ada=g_b_ada, ln_gain=cols(sg["ln_gain"], 256, 1)[None],
        ln_bias=cols(sg["ln_bias"], 256, 1)[None], ffn1_w_in=gsh["ffn1_w_in"][None], ffn1_w_out=gsh["ffn1_w_out"][None],
        w_mix_in=gsh["w_mix_in"][None], gla_a2_fwd=cols(sg["a2f"], 64, 1)[None], gla_a2_bwd=cols(sg["a2b"], 64, 1)[None],
        gla_a_bias_fwd=sg["abf"], gla_a_bias_bwd=sg["abb"], hgrn_lb_logits=cols(g_lbl, 128, 2),
        gla_norm_gain=sg["gng"].reshape(1, HD), hgrn_norm_gain=sg["gnh"].reshape(1, HD),
        w_mix_out=gsh["w_mix_out"][None], ffn2_w_in=gsh["ffn2_w_in"][None], ffn2_w_out=gsh["ffn2_w_out"][None])
    params = dict(
        c_ctx=(c_ctx, m_c_ctx, v_c_ctx), w_ada=(w_ada, m_w_ada, v_w_ada), b_ada=(b_ada, m_b_ada, v_b_ada),
        ln_gain=(ln_gain, m_ln_gain, v_ln_gain), ln_bias=(ln_bias, m_ln_bias, v_ln_bias),
        ffn1_w_in=(ffn1_w_in, m_ffn1_w_in, v_ffn1_w_in), ffn1_w_out=(ffn1_w_out, m_ffn1_w_out, v_ffn1_w_out),
        w_mix_in=(w_mix_in, m_w_mix_in, v_w_mix_in), gla_a2_fwd=(gla_a2_fwd, m_gla_a2_fwd, v_gla_a2_fwd),
        gla_a2_bwd=(gla_a2_bwd, m_gla_a2_bwd, v_gla_a2_bwd),
        gla_a_bias_fwd=(gla_a_bias_fwd, m_gla_a_bias_fwd, v_gla_a_bias_fwd),
        gla_a_bias_bwd=(gla_a_bias_bwd, m_gla_a_bias_bwd, v_gla_a_bias_bwd),
        hgrn_lb_logits=(hgrn_lb_logits, m_hgrn_lb_logits, v_hgrn_lb_logits),
        gla_norm_gain=(gla_norm_gain, m_gla_norm_gain, v_gla_norm_gain),
        hgrn_norm_gain=(hgrn_norm_gain, m_hgrn_norm_gain, v_hgrn_norm_gain),
        w_mix_out=(w_mix_out, m_w_mix_out, v_w_mix_out), ffn2_w_in=(ffn2_w_in, m_ffn2_w_in, v_ffn2_w_in),
        ffn2_w_out=(ffn2_w_out, m_ffn2_w_out, v_ffn2_w_out))
    order = list(params.keys())
    big_names = ("w_ada",) + BIG
    upd = {}
    for k in big_names:
        w_, m_, v_ = params[k]
        s2 = w_.shape[-2:]
        g_, d_, nm_, nv_ = _adamw(w_.reshape(s2), grads[k].reshape(s2), m_.reshape(s2), v_.reshape(s2), "adamw_" + k)
        grads[k] = g_
        upd[k] = (d_.reshape(w_.shape), nm_.reshape(w_.shape), nv_.reshape(w_.shape))
    small_names = [k for k in order if k not in big_names]
    sizes = [int(np.prod(params[k][0].shape)) for k in small_names]
    tot_n = sum(sizes)
    npad = -(-tot_n // 1024) * 1024

    def packed(get):
        flat_ = jnp.concatenate([get(k).reshape(-1) for k in small_names])
        return jnp.pad(flat_, (0, npad - tot_n)).reshape(8, npad // 8)

    _, d_s, nm_s, nv_s = _adamw(packed(lambda k: params[k][0]), packed(lambda k: grads[k]),
                                packed(lambda k: params[k][1]), packed(lambda k: params[k][2]), "adamw_small")
    o = 0
    for k, n in zip(small_names, sizes):
        shp = params[k][0].shape
        upd[k] = tuple(a.reshape(-1)[o:o + n].reshape(shp) for a in (d_s, nm_s, nv_s))
        o += n

    return (loss, grad_x, *[grads[k].reshape(params[k][0].shape) for k in order], *[upd[k][0] for k in order],
            *[upd[k][1] for k in order], *[upd[k][2] for k in order])
```

```python
import functools

import numpy as np
import jax
import jax.numpy as jnp
from jax import lax
from jax.experimental import pallas as pl
from jax.experimental.pallas import tpu as pltpu

F32 = jnp.float32
BF16 = jnp.bfloat16
MESH = pl.DeviceIdType.MESH

D = 1024
DFF = 2816
TM = 256
CH = 32
NCB = TM // CH
SB = 128
CSB = SB // CH
NSB = TM // SB
HP = 8
HPB = 8
HD = 128
NH = 8
LN_EPS = 1e-5
NORM_EPS = 1e-6
ALPHA = 2.0 ** 0.25
GATE_NORM = 16.0
GLA_DK = 64
N_MOD = 9
VMEM_LIMIT = 52 * 1024 * 1024

MIXP = 5120
GG, RG, GQ, GK, GV, RQ, RFF, RFB, RI, LR = 0, 512, 1024, 1536, 2048, 2560, 3072, 3584, 4096, 4608
IN_SPLITS = (256, 256, 512, 512, 16, 16, 512, 512, 512, 512, 512)

ADAM_LR, ADAM_B1, ADAM_B2, ADAM_EPS, ADAM_WD, ADAM_STEP = 0.001, 0.9, 0.999, 1e-08, 0.01, 10


def _cp(n_axes):
    return pltpu.CompilerParams(dimension_semantics=("arbitrary",) * n_axes, vmem_limit_bytes=VMEM_LIMIT)


def _rowmap(stride, off):
    return lambda b, i: (b * stride + off + i, 0)


def _mmap(comb):
    if comb:
        return lambda b, i: (b, jnp.minimum(i, 1), 0, 0)
    return lambda b, i: (b, 1, 0, 0)


def _ln(x):
    mu = jnp.mean(x, axis=-1, keepdims=True)
    xc = x - mu
    var = jnp.mean(xc * xc, axis=-1, keepdims=True)
    r = lax.rsqrt(var + LN_EPS)
    return xc * r, r


def _ln_bwd(dxh, xh, r):
    return r * (dxh - jnp.mean(dxh, axis=-1, keepdims=True) - xh * jnp.mean(dxh * xh, axis=-1, keepdims=True))


def _sigmoid(x):
    return 1.0 / (1.0 + jnp.exp(-x))


def _rsum(x):
    return jnp.sum(x, axis=0, keepdims=True)


def _dot(a, b):
    return jnp.dot(a, b, preferred_element_type=F32)


def _dot_nt(a, b):
    return lax.dot_general(a, b, (((1,), (1,)), ((), ())), preferred_element_type=F32)


def _dot_tn(a, b):
    return lax.dot_general(a, b, (((0,), (0,)), ((), ())), preferred_element_type=F32)


def _modulate(xv, m_ref, sub):
    xh, _ = _ln(xv)
    sh = m_ref[0, 0, 3 * sub:3 * sub + 1, :]
    sc = m_ref[0, 0, 3 * sub + 1:3 * sub + 2, :]
    return (xh * (1.0 + sh) + sc).astype(BF16)


def _embed_lnmod(x, ctx, pos, mvec):
    B, T, _ = x.shape
    nt = 1 + T // TM

    def body(x_ref, c_ref, p_ref, m_ref, o_ref, h_ref):
        i = pl.program_id(1)

        @pl.when(i == 0)
        def _():
            o_ref[...] = c_ref[0]

        @pl.when(i > 0)
        def _():
            o_ref[...] = x_ref[0] + p_ref[...]

        h_ref[...] = _modulate(o_ref[...], m_ref, 0)

    rows = pl.BlockSpec((TM, D), lambda b, i: (b * nt + i, 0))
    return pl.pallas_call(
        body, name="embed_lnmod0", grid=(B, nt),
        in_specs=[pl.BlockSpec((1, TM, D), lambda b, i: (b, jnp.maximum(i - 1, 0), 0)),
                  pl.BlockSpec((1, TM, D), lambda b, i: (b, 0, 0)),
                  pl.BlockSpec((TM, D), lambda b, i: (jnp.maximum(i - 1, 0), 0)),
                  pl.BlockSpec((1, 1, N_MOD, D), _mmap(True))],
        out_specs=[rows, rows],
        out_shape=[jax.ShapeDtypeStruct((B * nt * TM, D), F32), jax.ShapeDtypeStruct((B * nt * TM, D), BF16)],
        compiler_params=_cp(2))(x, ctx, pos, mvec)


def _lnmod0_bwd(dh, X, mvec, dres, x_shape, B, nt):
    def body(dh_ref, x_ref, m_ref, dr_ref, dx_ref, dm_ref):
        i = pl.program_id(1)
        xh, r = _ln(x_ref[...])
        sh = m_ref[0, 0, 0:1, :]
        dhv = dh_ref[...].astype(F32)

        @pl.when((i == 0) | (i == 1))
        def _():
            dm_ref[...] = jnp.zeros_like(dm_ref)

        dm_ref[0, 0, 0:1, :] += _rsum(dhv * xh)
        dm_ref[0, 0, 1:2, :] += _rsum(dhv)

        @pl.when(i > 0)
        def _():
            dx_ref[0] = _ln_bwd(dhv * (1.0 + sh), xh, r) + dr_ref[...]

    rows = pl.BlockSpec((TM, D), _rowmap(nt, 0))
    return pl.pallas_call(
        body, name="lnmod0_bwd", grid=(B, nt),
        in_specs=[rows, rows, pl.BlockSpec((1, 1, N_MOD, D), _mmap(True)), rows],
        out_specs=[pl.BlockSpec((1, TM, D), lambda b, i: (b, jnp.maximum(i - 1, 0), 0)),
                   pl.BlockSpec((1, 1, 2, D), _mmap(True))],
        out_shape=[jax.ShapeDtypeStruct(x_shape, F32), jax.ShapeDtypeStruct((B, 2, 2, D), F32)],
        compiler_params=_cp(2))(dh, X, mvec, dres)


def _resid_fwd(x_ref, y_ref, m_ref, gb_ref, sub, w):
    wg = w * m_ref[0, 0, 3 * sub + 2:3 * sub + 3, :]
    y = y_ref[...].astype(F32)
    zh, r = _ln(ALPHA * x_ref[...] + wg * y)
    return y, wg, zh, r


def _resid_grads(do, y, wg, zh, r, w, gb_ref, comb, dx_ref, dy_ref, dg_ref, dgb_ref):
    b_, i = pl.program_id(0), pl.program_id(1)
    dz = _ln_bwd(do * gb_ref[0:1, :], zh, r)
    dx_ref[...] = ALPHA * dz
    dy_ref[...] = (wg * dz).astype(BF16)

    @pl.when((b_ == 0) & (i == 0))
    def _():
        dgb_ref[...] = jnp.zeros_like(dgb_ref)

    dgb_ref[0:1, :] += _rsum(do * zh)
    dgb_ref[1:2, :] += _rsum(do)

    init = (i == 0) | (i == 1) if comb else (i == 0)

    @pl.when(init)
    def _():
        dg_ref[...] = jnp.zeros_like(dg_ref)

    dg_ref[0, 0] += w * _rsum(dz * y)


def _resid_lnmod(X, lx, Y, mvec, comb, sub, w, gb, B, nt, name):
    def body(x_ref, y_ref, m_ref, gb_ref, o_ref, h_ref):
        _, _, zh, _ = _resid_fwd(x_ref, y_ref, m_ref, gb_ref, sub, w)
        xn = zh * gb_ref[0:1, :] + gb_ref[1:2, :]
        o_ref[...] = xn
        h_ref[...] = _modulate(xn, m_ref, sub + 1)

    rows = pl.BlockSpec((TM, D), _rowmap(nt, 0))
    return pl.pallas_call(
        body, name=name, grid=(B, nt),
        in_specs=[pl.BlockSpec((TM, D), _rowmap(*lx)), rows,
                  pl.BlockSpec((1, 1, N_MOD, D), _mmap(comb)), pl.BlockSpec((2, D), lambda b, i: (0, 0))],
        out_specs=[rows, rows],
        out_shape=[jax.ShapeDtypeStruct((B * nt * TM, D), F32), jax.ShapeDtypeStruct((B * nt * TM, D), BF16)],
        compiler_params=_cp(2))(X, Y, mvec, gb)


def _resid_out_shapes(B, nt, comb):
    rows = pl.BlockSpec((TM, D), _rowmap(nt, 0))
    specs = [rows, rows, pl.BlockSpec((1, 1, 1, D), _mmap(comb)), pl.BlockSpec((2, D), lambda b, i: (0, 0))]
    shapes = [jax.ShapeDtypeStruct((B * nt * TM, D), F32), jax.ShapeDtypeStruct((B * nt * TM, D), BF16),
              jax.ShapeDtypeStruct((B, 2, 1, D), F32), jax.ShapeDtypeStruct((2, D), F32)]
    return specs, shapes


def _tail(X, Y, mvec, gb, tgt, sub, w, B, nt):
    def body(x_ref, y_ref, m_ref, gb_ref, t_ref, dx_ref, dy_ref, dg_ref, dgb_ref, l_ref):
        y, wg, zh, r = _resid_fwd(x_ref, y_ref, m_ref, gb_ref, sub, w)
        e = (zh * gb_ref[0:1, :] + gb_ref[1:2, :]) - t_ref[0]

        @pl.when((pl.program_id(0) == 0) & (pl.program_id(1) == 0))
        def _():
            l_ref[...] = jnp.zeros_like(l_ref)

        l_ref[...] += _rsum(e * e)
        _resid_grads(e * (1.0 / D), y, wg, zh, r, w, gb_ref, False, dx_ref, dy_ref, dg_ref, dgb_ref)

    rows = pl.BlockSpec((TM, D), _rowmap(nt, 0))
    specs, shapes = _resid_out_shapes(B, nt, False)
    return pl.pallas_call(
        body, name="resid2_loss_bwd", grid=(B, nt),
        in_specs=[rows, rows, pl.BlockSpec((1, 1, N_MOD, D), _mmap(False)), pl.BlockSpec((2, D), lambda b, i: (0, 0)),
                  pl.BlockSpec((1, TM, D), lambda b, i: (b, i, 0))],
        out_specs=specs + [pl.BlockSpec((1, D), lambda b, i: (0, 0))],
        out_shape=shapes + [jax.ShapeDtypeStruct((1, D), F32)],
        compiler_params=_cp(2))(X, Y, mvec, gb, tgt)


def _lnmod_resid_bwd(dh, Xi, lxi, dres, Xp, lxp, Yp, mvec, comb, sub, w, gb, B, nt, name):
    ntl = nt - 1 if comb else nt

    def body(dh_ref, xi_ref, dr_ref, xp_ref, yp_ref, m_ref, gb_ref, dx_ref, dy_ref, dg_ref, dgb_ref, dm_ref):
        i = pl.program_id(1)
        xh, r = _ln(xi_ref[...])
        sh = m_ref[0, 0, 3 * sub:3 * sub + 1, :]
        dhv = dh_ref[...].astype(F32)
        dr = dr_ref[...]
        if comb:
            dr = jnp.where(i > 0, dr, 0.0)
        do = _ln_bwd(dhv * (1.0 + sh), xh, r) + dr

        init = (i == 0) | (i == 1) if comb else (i == 0)

        @pl.when(init)
        def _():
            dm_ref[...] = jnp.zeros_like(dm_ref)

        dm_ref[0, 0, 0:1, :] += _rsum(dhv * xh)
        dm_ref[0, 0, 1:2, :] += _rsum(dhv)
        y, wg, zh, r2 = _resid_fwd(xp_ref, yp_ref, m_ref, gb_ref, sub - 1, w)
        _resid_grads(do, y, wg, zh, r2, w, gb_ref, comb, dx_ref, dy_ref, dg_ref, dgb_ref)

    rows = pl.BlockSpec((TM, D), _rowmap(nt, 0))
    if comb:
        dres_spec = pl.BlockSpec((TM, D), lambda b, i: (b * ntl + jnp.maximum(i - 1, 0), 0))
    else:
        dres_spec = rows
    specs, shapes = _resid_out_shapes(B, nt, comb)
    return pl.pallas_call(
        body, name=name, grid=(B, nt),
        in_specs=[rows, pl.BlockSpec((TM, D), _rowmap(*lxi)), dres_spec, pl.BlockSpec((TM, D), _rowmap(*lxp)), rows,
                  pl.BlockSpec((1, 1, N_MOD, D), _mmap(comb)), pl.BlockSpec((2, D), lambda b, i: (0, 0))],
        out_specs=specs + [pl.BlockSpec((1, 1, 2, D), _mmap(comb))],
        out_shape=shapes + [jax.ShapeDtypeStruct((B, 2, 2, D), F32)],
        compiler_params=_cp(2))(dh, Xi, dres, Xp, Yp, mvec, gb)


def _ffn_out_dx(dy, W, u, name, dep=None):
    M = dy.shape[0]
    half = DFF // 2
    deps = [] if dep is None else [dep]

    def body(dy_ref, w_ref, u_ref, *rest):
        du_ref = rest[-1]
        dyv = dy_ref[...]
        for j in range(2):
            lo, hi = j * half, (j + 1) * half
            da = _dot_nt(dyv, w_ref[lo:hi, :])
            g = u_ref[:, lo:hi].astype(F32)
            up = u_ref[:, DFF + lo:DFF + hi].astype(F32)
            s = _sigmoid(g)
            du_ref[:, lo:hi] = (da * up * (s * (1.0 + g * (1.0 - s)))).astype(BF16)
            du_ref[:, DFF + lo:DFF + hi] = (da * (g * s)).astype(BF16)

    return pl.pallas_call(
        body, name=name, grid=(M // TM,),
        in_specs=[pl.BlockSpec((TM, D), lambda i: (i, 0)), _wspec(W, 1), pl.BlockSpec((TM, 2 * DFF), lambda i: (i, 0))]
        + [_wspec(d, 1) for d in deps],
        out_specs=pl.BlockSpec((TM, 2 * DFF), lambda i: (i, 0)),
        out_shape=jax.ShapeDtypeStruct((M, 2 * DFF), BF16), compiler_params=_cp(1))(dy, W, u, *deps)


def _wspec(W, nidx):
    zeros = (0,) * W.ndim
    if nidx == 1:
        return pl.BlockSpec(W.shape, lambda i: zeros)
    return pl.BlockSpec(W.shape, lambda b, i: zeros)


def _mm_nn(A, la, W, B, nt, out_dtype, name):
    K, N = W.shape

    def body(a_ref, w_ref, o_ref):
        o_ref[...] = _dot(a_ref[...], w_ref[...]).astype(out_dtype)

    return pl.pallas_call(
        body, name=name, grid=(B, nt),
        in_specs=[pl.BlockSpec((TM, K), _rowmap(*la)), _wspec(W, 2)],
        out_specs=pl.BlockSpec((TM, N), _rowmap(nt, 0)),
        out_shape=jax.ShapeDtypeStruct((B * nt * TM, N), out_dtype), compiler_params=_cp(2))(A, W)


def _ffn_in(A, la, W3, B, nt, name):
    K, n = W3.shape[1:]

    def body(a_ref, w_ref, u_ref, s_ref):
        a = a_ref[...]
        for j in range(2):
            g = _dot(a, w_ref[j])
            up = _dot(a, w_ref[j + 2])
            u_ref[:, j * n:(j + 1) * n] = g.astype(BF16)
            u_ref[:, (j + 2) * n:(j + 3) * n] = up.astype(BF16)
            s_ref[:, j * n:(j + 1) * n] = (g * _sigmoid(g) * up).astype(BF16)

    rows = B * nt * TM
    return pl.pallas_call(
        body, name=name, grid=(B, nt),
        in_specs=[pl.BlockSpec((TM, K), _rowmap(*la)), _wspec(W3, 2)],
        out_specs=[pl.BlockSpec((TM, 4 * n), _rowmap(nt, 0)), pl.BlockSpec((TM, 2 * n), _rowmap(nt, 0))],
        out_shape=[jax.ShapeDtypeStruct((rows, 4 * n), BF16), jax.ShapeDtypeStruct((rows, 2 * n), BF16)],
        compiler_params=_cp(2))(A, W3)


def _mm_nt(A, W, name, dep=None, out_dtype=F32):
    M, N = A.shape
    K = W.shape[-2]

    def body(a_ref, w_ref, *rest):
        o_ref = rest[-1]
        if W.ndim == 3:
            n = W.shape[-1]
            acc = _dot_nt(a_ref[:, 0:n], w_ref[0])
            for j in range(1, 4):
                acc = acc + _dot_nt(a_ref[:, j * n:(j + 1) * n], w_ref[j])
            o_ref[...] = acc.astype(out_dtype)
        else:
            o_ref[...] = _dot_nt(a_ref[...], w_ref[...]).astype(out_dtype)

    deps = [] if dep is None else [dep]
    return pl.pallas_call(
        body, name=name, grid=(M // TM,),
        in_specs=[pl.BlockSpec((TM, N), lambda i: (i, 0)), _wspec(W, 1)] + [_wspec(d, 1) for d in deps],
        out_specs=pl.BlockSpec((TM, K), lambda i: (i, 0)),
        out_shape=jax.ShapeDtypeStruct((M, K), out_dtype), compiler_params=_cp(1))(A, W, *deps)


def _mm_tn(A, G, name, tn=512, shards=None):
    M, K = A.shape
    N = G.shape[1]
    tk = next(t for t in ((2048, 1536, 1024, 512) if K <= D else (1024, 512)) if M % t == 0)
    if shards:
        tn = N // shards

    def body(a_ref, g_ref, o_ref):
        @pl.when(pl.program_id(1) == 0)
        def _():
            o_ref[...] = jnp.zeros_like(o_ref)

        upd = _dot_tn(a_ref[...], g_ref[...])
        if shards:
            o_ref[0] += upd
        else:
            o_ref[...] += upd

    if shards:
        out_spec = pl.BlockSpec((1, K, tn), lambda n, k: (n, 0, 0))
        out_shape = jax.ShapeDtypeStruct((shards, K, tn), F32)
    else:
        out_spec = pl.BlockSpec((K, tn), lambda n, k: (0, n))
        out_shape = jax.ShapeDtypeStruct((K, N), F32)
    return pl.pallas_call(
        body, name=name, grid=(N // tn, M // tk),
        in_specs=[pl.BlockSpec((tk, K), lambda n, k: (k, 0)), pl.BlockSpec((tk, tn), lambda n, k: (k, n))],
        out_specs=out_spec, out_shape=out_shape, compiler_params=_cp(2))(A, G)


def _logsig(z):
    return jnp.minimum(z, 0.0) - jnp.log(1.0 + jnp.exp(-jnp.abs(z)))


ZW = 2688
ZRQ, ZRFF, ZRFB, ZLR = 1024, 1536, 2048, 2560


def _mix_in_features(H, W, a2p, biasp, lbp, B, nt):
    K = W.shape[0]

    def body(h_ref, w_ref, a2_ref, bias_ref, lb_ref, q_ref, k_ref, v_ref, g_ref, z_ref):
        hv = h_ref[...]
        proj = lambda lo, hi: _dot(hv, w_ref[:, lo:hi])
        z_ref[:, 0:2 * 4 * HD] = proj(GG, GG + 2 * 4 * HD)
        lrf = proj(LR, LR + HD)
        z_ref[:, ZLR:ZLR + HD] = lrf
        lr = lrf.astype(BF16)
        lane = lax.broadcasted_iota(jnp.int32, (1, 4 * HD), 1)
        keep = (lane & (HD - 1)) < GLA_DK
        for d in range(2):
            z = _dot(lr, a2_ref[d]) + bias_ref[d:d + 1, :]
            gl = jnp.where(keep, _logsig(z) * (1.0 / GATE_NORM), 0.0)
            for h in range(4):
                g_ref[d, 0, h] = gl[:, h * HD:(h + 1) * HD]
        gq, gk, gv = proj(GQ, GQ + 4 * HD), proj(GK, GK + 4 * HD), proj(GV, GV + 4 * HD)
        for h in range(4):
            sl = slice(h * HD, (h + 1) * HD)
            q_ref[0, h] = gq[:, sl] * (GLA_DK ** -0.5)
            k_ref[0, 0, h] = gk[:, sl]
            k_ref[1, 0, h] = gk[:, sl]
            v_ref[0, h] = gv[:, sl].astype(BF16)
        rqs, ri = proj(RQ, RQ + 4 * HD), proj(RI, RI + 4 * HD)
        z_ref[:, ZRQ:ZRQ + 4 * HD] = rqs
        for h in range(4):
            sl = slice(h * HD, (h + 1) * HD)
            rq = rqs[:, sl]
            q_ref[0, 4 + h] = rq * _sigmoid(rq) * (HD ** -0.5)
            v_ref[0, 4 + h] = ri[:, sl].astype(BF16)
        for d, off, zoff in ((0, RFF, ZRFF), (1, RFB, ZRFB)):
            rf = proj(off, off + 4 * HD)
            z_ref[:, zoff:zoff + 4 * HD] = rf
            for h in range(4):
                sl = slice(h * HD, (h + 1) * HD)
                lb = lb_ref[d:d + 1, sl]
                f = lb + (1.0 - lb) * _sigmoid(rf[:, sl])
                g_ref[d, 0, 4 + h] = jnp.log(f)
                k_ref[d, 0, 4 + h] = 1.0 - f

    one = pl.BlockSpec((1, NH, TM, HD), lambda b, i: (b, 0, i, 0))
    two = pl.BlockSpec((2, 1, NH, TM, HD), lambda b, i: (0, b, 0, i, 0))
    s1 = jax.ShapeDtypeStruct((B, NH, nt * TM, HD), F32)
    s2 = jax.ShapeDtypeStruct((2, B, NH, nt * TM, HD), F32)
    return pl.pallas_call(
        body, name="mix_in_features", grid=(B, nt),
        in_specs=[pl.BlockSpec((TM, K), _rowmap(nt, 0)), _wspec(W, 2), pl.BlockSpec((2, HD, 4 * HD), lambda b, i: (0, 0, 0)),
                  pl.BlockSpec((2, 4 * HD), lambda b, i: (0, 0)), pl.BlockSpec((2, 4 * HD), lambda b, i: (0, 0))],
        out_specs=[one, two, one, two, pl.BlockSpec((TM, ZW), _rowmap(nt, 0))],
        out_shape=[s1, s2, jax.ShapeDtypeStruct(s1.shape, BF16), s2, jax.ShapeDtypeStruct((B * nt * TM, ZW), F32)],
        compiler_params=_cp(2))(H, W, a2p, biasp, lbp)


def _features_bwd(Z, a2p, biasp, lbp, dQ0, dQ1, dK0, dK1, dV0, dV1, dG0, dG1, dgates, B):
    nt = Z.shape[0] // (B * TM)

    def body(z_ref, a2_ref, bias_ref, lb_ref, dq0, dq1, dk0, dk1, dv0, dv1, dg0, dg1, dgt_ref,
             df_ref, da2_ref, dbias_ref, dlb_ref):
        b_, i = pl.program_id(0), pl.program_id(1)

        @pl.when((b_ == 0) & (i == 0))
        def _():
            da2_ref[...] = jnp.zeros_like(da2_ref)
            dbias_ref[...] = jnp.zeros_like(dbias_ref)
            dlb_ref[...] = jnp.zeros_like(dlb_ref)

        df_ref[:, 0:2 * 4 * HD] = jnp.where(i > 0, dgt_ref[...], 0.0).astype(BF16)
        df_ref[:, LR + HD:] = jnp.zeros((TM, MIXP - LR - HD), BF16)
        lr = z_ref[:, ZLR:ZLR + HD].astype(BF16)
        lane = lax.broadcasted_iota(jnp.int32, (1, 4 * HD), 1)
        keep = (lane & (HD - 1)) < GLA_DK
        dlr = jnp.zeros((TM, HD), F32)
        dgs = (dg0, dg1)
        dks = (dk0, dk1)
        rd = lambda ref, h: ref[0, h].astype(F32)
        for d in range(2):
            z = _dot(lr, a2_ref[d]) + bias_ref[d:d + 1, :]
            dgl = jnp.concatenate([rd(dgs[d], h) for h in range(4)], axis=1)
            dz = jnp.where(keep, dgl * (1.0 / GATE_NORM) * (1.0 - _sigmoid(z)), 0.0)
            dzb = dz.astype(BF16)
            dlr = dlr + _dot_nt(dzb, a2_ref[d])
            da2_ref[d] += _dot_tn(lr, dzb)
            dbias_ref[d:d + 1, :] += _rsum(dz)
        df_ref[:, LR:LR + HD] = dlr.astype(BF16)
        for h in range(4):
            df_ref[:, GQ + h * HD:GQ + (h + 1) * HD] = ((rd(dq0, h) + rd(dq1, h)) * (GLA_DK ** -0.5)).astype(BF16)
            df_ref[:, GK + h * HD:GK + (h + 1) * HD] = (rd(dk0, h) + rd(dk1, h)).astype(BF16)
            df_ref[:, GV + h * HD:GV + (h + 1) * HD] = (rd(dv0, h) + rd(dv1, h)).astype(BF16)
        for h in range(4):
            sl = slice(h * HD, (h + 1) * HD)
            rq = z_ref[:, ZRQ + h * HD:ZRQ + (h + 1) * HD]
            s = _sigmoid(rq)
            dqh = rd(dq0, 4 + h) + rd(dq1, 4 + h)
            df_ref[:, RQ + h * HD:RQ + (h + 1) * HD] = (dqh * (HD ** -0.5) * (s * (1.0 + rq * (1.0 - s)))).astype(BF16)
            df_ref[:, RI + h * HD:RI + (h + 1) * HD] = (rd(dv0, 4 + h) + rd(dv1, 4 + h)).astype(BF16)
            for d, off, zoff in ((0, RFF, ZRFF), (1, RFB, ZRFB)):
                lb = lb_ref[d:d + 1, sl]
                sg = _sigmoid(z_ref[:, zoff + h * HD:zoff + (h + 1) * HD])
                f = lb + (1.0 - lb) * sg
                dff = rd(dgs[d], 4 + h) / f - rd(dks[d], 4 + h)
                df_ref[:, off + h * HD:off + (h + 1) * HD] = (dff * (1.0 - lb) * sg * (1.0 - sg)).astype(BF16)
                dlb_ref[d:d + 1, sl] += _rsum(dff * (1.0 - sg))

    m0 = lambda b, i: (b, 0, i, 0)
    one = lambda m: pl.BlockSpec((1, NH, TM, HD), m)
    return pl.pallas_call(
        body, name="mix_features_bwd", grid=(B, nt),
        in_specs=[pl.BlockSpec((TM, ZW), _rowmap(nt, 0)), pl.BlockSpec((2, HD, 4 * HD), lambda b, i: (0, 0, 0)),
                  pl.BlockSpec((2, 4 * HD), lambda b, i: (0, 0)), pl.BlockSpec((2, 4 * HD), lambda b, i: (0, 0)),
                  one(m0), one(m0), one(m0), one(m0), one(m0), one(m0), one(m0), one(m0),
                  pl.BlockSpec((TM, D), lambda b, i: (b * (nt - 1) + jnp.maximum(i - 1, 0), 0))],
        out_specs=[pl.BlockSpec((TM, MIXP), _rowmap(nt, 0)), pl.BlockSpec((2, HD, 4 * HD), lambda b, i: (0, 0, 0)),
                   pl.BlockSpec((2, 4 * HD), lambda b, i: (0, 0)), pl.BlockSpec((2, 4 * HD), lambda b, i: (0, 0))],
        out_shape=[jax.ShapeDtypeStruct((B * nt * TM, MIXP), BF16), jax.ShapeDtypeStruct((2, HD, 4 * HD), F32),
                   jax.ShapeDtypeStruct((2, 4 * HD), F32), jax.ShapeDtypeStruct((2, 4 * HD), F32)],
        compiler_params=_cp(2))(Z, a2p, biasp, lbp, dQ0, dQ1, dK0, dK1, dV0, dV1, dG0, dG1, dgates)


def _chunk_scan(x, rin, fwd):
    acc = x
    sft = 1
    while sft < CH:
        if fwd:
            acc = acc + jnp.where(rin >= sft, pltpu.roll(acc, sft, 0), 0.0)
        else:
            acc = acc + jnp.where(rin < CH - sft, pltpu.roll(acc, TM - sft, 0), 0.0)
        sft *= 2
    return acc


def _chunk_total(x):
    t = jnp.sum(x.reshape(NCB, CH, HD), axis=1, keepdims=True)
    return jnp.broadcast_to(t, (NCB, CH, HD)).reshape(TM, HD)


def _scan_masks(rev):
    rin = lax.broadcasted_iota(jnp.int32, (TM, HD), 0) & (CH - 1)
    ri = lax.broadcasted_iota(jnp.int32, (SB, SB), 0)
    ci = lax.broadcasted_iota(jnp.int32, (SB, SB), 1)
    same = (ri >> 5) == (ci >> 5)
    lo = same & (ri >= ci)
    up = same & (ri <= ci)
    mask, maskT = (up, lo) if rev else (lo, up)
    re = lax.broadcasted_iota(jnp.int32, (SB, CSB * HD), 0) >> 5
    ce = lax.broadcasted_iota(jnp.int32, (SB, CSB * HD), 1) >> 7
    return rin, mask, maskT, re == ce


def _scan_decay(q, k, g, rin, rev):
    b = _chunk_scan(g, rin, not rev)
    xx = _chunk_total(g) - b
    eb = jnp.exp(b)
    return b, xx, eb, q * eb, k * jnp.exp(-b), k * jnp.exp(xx)


def _sub(x, s):
    return x[s * SB:(s + 1) * SB]


def _expand(xb, mexp):
    return jnp.where(mexp, jnp.concatenate([xb] * CSB, axis=1), jnp.zeros((), xb.dtype))


def _own(x, mexp):
    xm = jnp.where(mexp, x, 0.0)
    acc = xm[:, 0:HD]
    for n in range(1, CSB):
        acc = acc + xm[:, n * HD:(n + 1) * HD]
    return acc


def _stack(per_chunk, s):
    return jnp.concatenate(per_chunk[s * CSB:(s + 1) * CSB], axis=1)


def _state_pass(s0, eb, uts, rev):
    order = range(NCB - 1, -1, -1) if rev else range(NCB)
    states = [None] * NCB
    s = s0
    for n in order:
        row = n * CH if rev else n * CH + CH - 1
        states[n] = s
        s = eb[row:row + 1, :] * s + uts[n // CSB][:, (n % CSB) * HD:(n % CSB + 1) * HD]
    return states, s


def _scan_fwd(Q, K, V, G, rev, B):
    nb = Q.shape[2] // TM
    d = 1 if rev else 0
    rmap = (lambda s: jnp.where(s == 0, 0, nb - s)) if rev else (lambda s: s)

    def body(q_ref, k_ref, v_ref, g_ref, o_ref, st_ref, s_scr):
        @pl.when(pl.program_id(2) == 0)
        def _():
            s_scr[...] = jnp.zeros_like(s_scr)

        rin, mask, _, mexp = _scan_masks(rev)

        def head(p, readout):
            s0 = s_scr[p]
            st_ref[0, p, 0] = s0
            _, _, eb, qd, ki, kt = _scan_decay(q_ref[0, p], k_ref[0, 0, p], g_ref[0, 0, p], rin, rev)
            ktb, vb = kt.astype(BF16), v_ref[0, p]
            uts = [_dot_tn(_sub(vb, s), _expand(_sub(ktb, s), mexp)) for s in range(NSB)]
            states, s_new = _state_pass(s0, eb, uts, rev)
            s_scr[p] = s_new
            if not readout:
                return
            qb, kib = qd.astype(BF16), ki.astype(BF16)
            for s in range(NSB):
                a = jnp.where(mask, _dot_nt(_sub(qb, s), _sub(kib, s)), 0.0)
                o_ref[0, p, s * SB:(s + 1) * SB, :] = (
                    _dot(a.astype(BF16), _sub(vb, s))
                    + _dot_nt(_expand(_sub(qb, s), mexp), _stack(states, s).astype(BF16)))

        @pl.when(pl.program_id(2) >= 1)
        def _():
            for p in range(HP):
                head(p, True)

        @pl.when(pl.program_id(2) == 0)
        def _():
            for p in range(HP):
                head(p, False)

    one = pl.BlockSpec((1, HP, TM, HD), lambda b, h, s: (b, h, rmap(s), 0))
    two = pl.BlockSpec((1, 1, HP, TM, HD), lambda b, h, s: (d, b, h, rmap(s), 0))
    return pl.pallas_call(
        body, name="scan_fwd_rev" if rev else "scan_fwd", grid=(B, NH // HP, nb),
        in_specs=[one, two, one, two],
        out_specs=[one, pl.BlockSpec((1, HP, 1, HD, HD), lambda b, h, s: (b, h, s, 0, 0))],
        out_shape=[jax.ShapeDtypeStruct(Q.shape, F32), jax.ShapeDtypeStruct((B, NH, nb, HD, HD), F32)],
        scratch_shapes=[pltpu.VMEM((HP, HD, HD), F32)],
        compiler_params=_cp(3))(Q, K, V, G)


def _scan_bwd(Q, K, V, G, St, dO, rev, B):
    nb = Q.shape[2] // TM
    d = 1 if rev else 0
    smap = lambda t: nb - 1 - t
    rmap = (lambda t: jnp.where(smap(t) == 0, 0, nb - smap(t))) if rev else smap

    def body(q_ref, k_ref, v_ref, g_ref, st_ref, do_ref, dq_ref, dk_ref, dv_ref, dg_ref, ds_scr):
        t = pl.program_id(2)

        @pl.when(t == 0)
        def _():
            ds_scr[...] = jnp.zeros_like(ds_scr)

        is_lat = smap(t) >= 1
        rin, mask, maskT, mexp = _scan_masks(rev)
        for p in range(HPB):
            b, xx, eb, qd, ki, kt = _scan_decay(q_ref[0, p], k_ref[0, 0, p], g_ref[0, 0, p], rin, rev)
            qb, kib, ktb, vb = qd.astype(BF16), ki.astype(BF16), kt.astype(BF16), v_ref[0, p]
            dob = jnp.where(is_lat, do_ref[0, p], 0.0).astype(BF16)
            kt_exps = [_expand(_sub(ktb, s), mexp) for s in range(NSB)]
            uts = [_dot_tn(_sub(vb, s), kt_exps[s]) for s in range(NSB)]
            states, _ = _state_pass(st_ref[0, p, 0], eb, uts, rev)
            gts = [_dot_tn(_sub(dob, s), _expand(_sub(qb, s), mexp)) for s in range(NSB)]
            order = range(NCB) if rev else range(NCB - 1, -1, -1)
            dsp = [None] * NCB
            t2 = [None] * NCB
            dsc = ds_scr[p]
            for n in order:
                row = n * CH if rev else n * CH + CH - 1
                ebl = eb[row:row + 1, :]
                dsp[n] = dsc
                t2[n] = jnp.broadcast_to(ebl * _rsum(states[n] * dsc), (CH, HD))
                dsc = gts[n // CSB][:, (n % CSB) * HD:(n % CSB + 1) * HD] + ebl * dsc
            ds_scr[p] = dsc
            dqds, dkis, dkts = [], [], []
            for s in range(NSB):
                q_s, ki_s, v_s, do_s = _sub(qb, s), _sub(kib, s), _sub(vb, s), _sub(dob, s)
                dspb = _stack(dsp, s).astype(BF16)
                da = jnp.where(mask, _dot_nt(do_s, v_s), 0.0).astype(BF16)
                dat = jnp.where(maskT, _dot_nt(v_s, do_s), 0.0).astype(BF16)
                at = jnp.where(maskT, _dot_nt(ki_s, q_s), 0.0).astype(BF16)
                dqds.append(_dot(da, ki_s) + _own(_dot(do_s, _stack(states, s).astype(BF16)), mexp))
                dkis.append(_dot(dat, q_s))
                dv_ref[0, p, s * SB:(s + 1) * SB, :] = (_dot(at, do_s) + _dot_nt(kt_exps[s], dspb)).astype(BF16)
                dkts.append(_own(_dot(v_s, dspb), mexp))
            dqd, dki, dkt = (jnp.concatenate(parts, axis=0) for parts in (dqds, dkis, dkts))
            z = dkt * kt
            db = dqd * qd - dki * ki
            dq_ref[0, p] = (dqd * eb).astype(BF16)
            dk_ref[0, p] = (dki * jnp.exp(-b) + dkt * jnp.exp(xx)).astype(BF16)
            dg_ref[0, p] = (_chunk_total(db) + (db - z) + _chunk_scan(z - db, rin, not rev)
                            + jnp.concatenate(t2, axis=0)).astype(BF16)

    one = pl.BlockSpec((1, HPB, TM, HD), lambda b, h, t: (b, h, rmap(t), 0))
    two = pl.BlockSpec((1, 1, HPB, TM, HD), lambda b, h, t: (d, b, h, rmap(t), 0))
    lat = pl.BlockSpec((1, HPB, TM, HD), lambda b, h, t: (b, h, jnp.clip(rmap(t) - 1, 0, nb - 2), 0))
    shp = jax.ShapeDtypeStruct(Q.shape, BF16)
    return pl.pallas_call(
        body, name="scan_bwd_rev" if rev else "scan_bwd", grid=(B, NH // HPB, nb),
        in_specs=[one, two, one, two, pl.BlockSpec((1, HPB, 1, HD, HD), lambda b, h, t: (b, h, smap(t), 0, 0)), lat],
        out_specs=[one, one, one, one], out_shape=[shp, shp, shp, shp],
        scratch_shapes=[pltpu.VMEM((HPB, HD, HD), F32)],
        compiler_params=_cp(3))(Q, K, V, G, St, dO)


def _gnorm_mix_out(O0, O1, F, gains, W, B, ntl):
    nt = ntl + 1

    def body(o0_ref, o1_ref, f_ref, gn_ref, w_ref, m_ref, y_ref):
        for h in range(NH):
            o = o0_ref[0, h] + o1_ref[0, h]
            r = lax.rsqrt(jnp.mean(o * o, axis=-1, keepdims=True) + NORM_EPS)
            gn = gn_ref[0:1, :] if h < 4 else gn_ref[1:2, :]
            gt = f_ref[:, h * HD:(h + 1) * HD]
            m_ref[:, h * HD:(h + 1) * HD] = (o * r * gn * (gt * _sigmoid(gt))).astype(BF16)
        y_ref[...] = _dot(m_ref[...], w_ref[...]).astype(BF16)

    ospec = pl.BlockSpec((1, NH, TM, HD), lambda b, i: (b, 0, i + 1, 0))
    rows = pl.BlockSpec((TM, D), _rowmap(ntl, 0))
    shp = jax.ShapeDtypeStruct((B * ntl * TM, D), BF16)
    return pl.pallas_call(
        body, name="gated_norm_mix_out", grid=(B, ntl),
        in_specs=[ospec, ospec, pl.BlockSpec((TM, D), lambda b, i: (b * nt + 1 + i, 0)),
                  pl.BlockSpec((2, HD), lambda b, i: (0, 0)), _wspec(W, 2)],
        out_specs=[rows, rows], out_shape=[shp, shp], compiler_params=_cp(2))(O0, O1, F, gains, W)


def _mix_out_dx_gnorm_bwd(dY, W, O0, O1, F, gains, dep, B, ntl):
    nt = ntl + 1

    def body(dy_ref, w_ref, o0_ref, o1_ref, f_ref, gn_ref, dep_ref, do_ref, dgt_ref, dgn_ref):
        b_, i = pl.program_id(0), pl.program_id(1)

        @pl.when((b_ == 0) & (i == 0))
        def _():
            dgn_ref[...] = jnp.zeros_like(dgn_ref)

        dyv = dy_ref[...]
        for h in range(NH):
            o = o0_ref[0, h] + o1_ref[0, h]
            r = lax.rsqrt(jnp.mean(o * o, axis=-1, keepdims=True) + NORM_EPS)
            y = o * r
            gn = gn_ref[0:1, :] if h < 4 else gn_ref[1:2, :]
            gt = f_ref[:, h * HD:(h + 1) * HD]
            s = _sigmoid(gt)
            dm = _dot_nt(dyv, w_ref[h * HD:(h + 1) * HD, :])
            don = dm * (gt * s)
            dgt_ref[:, h * HD:(h + 1) * HD] = (dm * (y * gn) * (s * (1.0 + gt * (1.0 - s)))).astype(BF16)
            row = 0 if h < 4 else 1
            dgn_ref[row:row + 1, :] += _rsum(don * y)
            dy = don * gn
            do_ref[0, h] = (r * (dy - y * jnp.mean(dy * y, axis=-1, keepdims=True))).astype(BF16)

    ospec = pl.BlockSpec((1, NH, TM, HD), lambda b, i: (b, 0, i + 1, 0))
    return pl.pallas_call(
        body, name="mix_out_dx_gated_norm_bwd", grid=(B, ntl),
        in_specs=[pl.BlockSpec((TM, D), _rowmap(ntl, 0)), _wspec(W, 2), ospec, ospec,
                  pl.BlockSpec((TM, D), lambda b, i: (b * nt + 1 + i, 0)), pl.BlockSpec((2, HD), lambda b, i: (0, 0)),
                  _wspec(dep, 2)],
        out_specs=[pl.BlockSpec((1, NH, TM, HD), lambda b, i: (b, 0, i, 0)), pl.BlockSpec((TM, D), _rowmap(ntl, 0)),
                   pl.BlockSpec((2, HD), lambda b, i: (0, 0))],
        out_shape=[jax.ShapeDtypeStruct((B, NH, ntl * TM, HD), BF16), jax.ShapeDtypeStruct((B * ntl * TM, D), BF16),
                   jax.ShapeDtypeStruct((2, HD), F32)],
        compiler_params=_cp(2))(dY, W, O0, O1, F, gains, dep)


def _sincos_2d(rows, width, dim):
    quarter = dim // 4
    omega = 1.0 / 10000.0 ** (jnp.arange(quarter, dtype=F32) / quarter)

    def emb(n):
        a = jnp.arange(n).astype(F32)[:, None] * omega[None, :]
        return jnp.concatenate([jnp.sin(a), jnp.cos(a)], axis=-1)

    er = jnp.broadcast_to(emb(rows)[:, None, :], (rows, width, dim // 2))
    ec = jnp.broadcast_to(emb(width)[None, :, :], (rows, width, dim // 2))
    return jnp.concatenate([er, ec], axis=-1).reshape(rows * width, dim)


def _pad_heads(w):
    k = w.shape[0]
    return jnp.pad(w.reshape(k, 4, GLA_DK), ((0, 0), (0, 0), (0, HD - GLA_DK))).reshape(k, 4 * HD)


def _unpad_heads(w):
    k = w.shape[0]
    return w.reshape(k, 4, HD)[:, :, :GLA_DK].reshape(k, 4 * GLA_DK)


MIX_N = 1032
MIX_NP = 1152
_SEGS = ([(64 * h, 64, GQ + HD * h) for h in range(4)] + [(256 + 64 * h, 64, GK + HD * h) for h in range(4)]
         + [(512, 512, GV), (1024, 512, GG), (1536, 32, LR), (1568, 512, RQ), (2080, 512, RFF), (2592, 512, RFB),
            (3104, 512, RI), (3616, 512, RG)])


def _mix_in_to_padded(ps):
    k = ps.shape[1]
    parts, pos = [], 0
    for g0, ln, s0 in sorted(_SEGS, key=lambda s: s[2]):
        if s0 > pos:
            parts.append(jnp.zeros((k, s0 - pos), ps.dtype))
        for j in range(4):
            lo, hi = max(g0, j * MIX_N), min(g0 + ln, (j + 1) * MIX_N)
            if lo < hi:
                parts.append(ps[j][:, lo - j * MIX_N:hi - j * MIX_N])
        pos = s0 + ln
    parts.append(jnp.zeros((k, MIXP - pos), ps.dtype))
    return jnp.concatenate(parts, axis=1)


def _mix_in_from_padded(g):
    k = g.shape[0]
    shards = []
    for j in range(4):
        parts = []
        for g0, ln, s0 in sorted(_SEGS):
            lo, hi = max(g0, j * MIX_N), min(g0 + ln, (j + 1) * MIX_N)
            if lo < hi:
                parts.append(g[:, s0 + lo - g0:s0 + hi - g0])
        parts.append(jnp.zeros((k, MIX_NP - MIX_N), g.dtype))
        shards.append(jnp.concatenate(parts, axis=1))
    return jnp.stack(shards)


def _local_step(x, ctx, tgt, mvec, weights_for, prefetch, ln_gain, ln_bias, a2f, a2b, abf, abb, lb, gng, gnh, on_grads, on_sent):
    B, T, _ = x.shape
    assert ctx.shape[1] == TM and T % TM == 0
    ntl = T // TM
    nt = ntl + 1
    C, L, CL = (nt, 0), (ntl, 0), (nt, 1)
    pos = _sincos_2d(T // 64, 64, D)
    gbs = [jnp.stack([ln_gain[i], ln_bias[i]]) for i in range(3)]
    a2p = jnp.zeros((2, HD, 4 * HD), F32)
    a2p = a2p.at[0, 0:16].set(_pad_heads(a2f)).at[1, 16:32].set(_pad_heads(a2b)).astype(BF16)
    biasp = jnp.concatenate([_pad_heads(abf.reshape(1, -1)), _pad_heads(abb.reshape(1, -1))], axis=0)
    gains = jnp.concatenate([gng.reshape(1, HD), gnh.reshape(1, HD)], axis=0)

    X0, h0 = _embed_lnmod(x, ctx, pos, mvec)
    w1i, w1o = weights_for("ffn1", h0)
    u0, a0 = _ffn_in(h0, C, w1i, B, nt, "ffn1_in")
    tok = prefetch("mix", a0)
    y0 = _mm_nn(a0, C, w1o, B, nt, BF16, "ffn1_out")
    X1, h1 = _resid_lnmod(X0, C, y0, mvec, True, 0, 0.5, gbs[0] + tok[0, 0], B, nt, "resid0_lnmod1")
    wmp, wmo = weights_for("mix", h1)
    Q, K, V, G, Zm = _mix_in_features(h1, wmp, a2p, biasp, lb, B, nt)
    prefetch("ffn2", Zm)
    O0, S0 = _scan_fwd(Q, K, V, G, False, B)
    O1, S1 = _scan_fwd(Q, K, V, G, True, B)
    merged, y1 = _gnorm_mix_out(O0, O1, Zm, gains, wmo, B, ntl)
    X2, h2 = _resid_lnmod(X1, CL, y1, mvec, False, 1, 1.0, gbs[1], B, ntl, "resid1_lnmod2")
    w2i, w2o = weights_for("ffn2", h2)
    u2, a2 = _ffn_in(h2, L, w2i, B, ntl, "ffn2_in")
    y2 = _mm_nn(a2, L, w2o, B, ntl, BF16, "ffn2_out")

    dx2r, dy2, dgate2, dgb2, lsum = _tail(X2, y2, mvec, gbs[2], tgt, 2, 0.5, B, ntl)
    loss = (0.5 / D) * jnp.sum(lsum)
    du2 = _ffn_out_dx(dy2, w2o, u2, "ffn2_out_dx")
    g_w2o = _mm_tn(a2, dy2, "ffn2_out_dw")
    dh2 = _mm_nt(du2, w2i, "ffn2_in_dx", out_dtype=BF16)
    g_w2i = _mm_tn(h2, du2, "ffn2_in_dw", shards=4)
    tok = on_grads("ffn2", (g_w2i, g_w2o))
    dx1r, dy1, dgate1, dgb1, dss2 = _lnmod_resid_bwd(dh2, X2, L, dx2r, X1, CL, y1, mvec, False, 2, 1.0,
                                                     gbs[1] + tok[0, 0], B, ntl, "lnmod2_resid1_bwd")
    tok = on_sent("ffn2", dy1)
    g_wmo = _mm_tn(merged, dy1, "mix_out_dw", tn=D)
    dO, dgates, dgains = _mix_out_dx_gnorm_bwd(dy1, wmo, O0, O1, Zm, gains, tok, B, ntl)
    dQ0, dK0, dV0, dG0 = _scan_bwd(Q, K, V, G, S0, dO, False, B)
    dQ1, dK1, dV1, dG1 = _scan_bwd(Q, K, V, G, S1, dO, True, B)
    dF, da2p, dbiasp, dlb = _features_bwd(Zm, a2p, biasp, lb, dQ0, dQ1, dK0, dK1, dV0, dV1, dG0, dG1, dgates, B)
    dh1 = _mm_nt(dF, wmp, "mix_in_dx", out_dtype=BF16)
    g_wmp = _mm_tn(h1, dF, "mix_in_dw", tn=MIXP // 4)
    tok = on_grads("mix", (g_wmp, g_wmo))
    dx0r, dy0, dgate0, dgb0, dss1 = _lnmod_resid_bwd(dh1, X1, C, dx1r, X0, C, y0, mvec, True, 1, 0.5,
                                                     gbs[0] + tok[0, 0], B, nt, "lnmod1_resid0_bwd")
    tok = on_sent("mix", dy0)
    du0 = _ffn_out_dx(dy0, w1o, u0, "ffn1_out_dx", dep=tok)
    g_w1o = _mm_tn(a0, dy0, "ffn1_out_dw")
    g_w1i = _mm_tn(h0, du0, "ffn1_in_dw", shards=4)
    tok = on_grads("ffn1", (g_w1i, g_w1o))
    dh0 = _mm_nt(du0, w1i, "ffn1_in_dx", dep=tok, out_dtype=BF16)
    tok = on_sent("ffn1", dh0)
    grad_x, dss0 = _lnmod0_bwd(dh0, X0, mvec + tok[0, 0], dx0r, x.shape, B, nt)

    zero_ctx = lambda a: a.at[:, 0].set(0.0)
    dm = jnp.concatenate([dss0, dgate0, dss1, zero_ctx(dgate1), zero_ctx(dss2), zero_ctx(dgate2)], axis=2)
    small = dict(
        ln_gain=jnp.stack([dgb0[0], dgb1[0], dgb2[0]]), ln_bias=jnp.stack([dgb0[1], dgb1[1], dgb2[1]]),
        a2f=_unpad_heads(da2p[0, 0:16]), a2b=_unpad_heads(da2p[1, 16:32]),
        abf=_unpad_heads(dbiasp[0:1]), abb=_unpad_heads(dbiasp[1:2]), lb=dlb, gng=dgains[0], gnh=dgains[1])
    return loss, grad_x, dm, small


def _small_allgather(xs, name):
    r, n = xs.shape

    def body(x_ref, out_ref, send_sems, recv_sems, local_sem):
        x, y, c = lax.axis_index("x"), lax.axis_index("y"), lax.axis_index("c")
        me, sibling = (x, y, c), (x, y, 1 - c)
        chips = [(1 - x, y), (x, 1 - y), (1 - x, 1 - y)]

        def rows(px, py, pc):
            return out_ref.at[pl.ds((4 * px + 2 * py + pc) * r, r), :]

        def copy(k, block, to, src=None):
            return pltpu.make_async_remote_copy(
                src_ref=rows(*block) if src is None else src, dst_ref=rows(*block),
                send_sem=send_sems.at[k], recv_sem=recv_sems.at[k], device_id=to, device_id_type=MESH)

        mine = pltpu.make_async_copy(x_ref, rows(*me), local_sem)
        mine.start()
        first = [copy(0, me, sibling, src=x_ref)]
        first += [copy(1 + j, me, (*chip, c), src=x_ref) for j, chip in enumerate(chips)]
        for cp in first:
            cp.start()
        passed = [copy(4 + j, (*chip, c), sibling) for j, chip in enumerate(chips)]
        for j, chip in enumerate(chips):
            copy(1 + j, (*chip, c), me).wait_recv()
            passed[j].start()
        copy(0, sibling, me).wait_recv()
        for j, chip in enumerate(chips):
            copy(4 + j, (*chip, 1 - c), me).wait_recv()
        for cp in first + passed:
            cp.wait_send()
        mine.wait()

    out = pl.pallas_call(
        body, name=name,
        out_shape=jax.ShapeDtypeStruct((8 * r, n), xs.dtype),
        in_specs=[pl.BlockSpec(memory_space=pltpu.VMEM)],
        out_specs=pl.BlockSpec(memory_space=pltpu.VMEM),
        scratch_shapes=[pltpu.SemaphoreType.DMA((7,)), pltpu.SemaphoreType.DMA((7,)), pltpu.SemaphoreType.DMA],
        compiler_params=pltpu.CompilerParams(vmem_limit_bytes=VMEM_LIMIT))(xs)
    return out.reshape(8, r, n)


def _gather_flat(v, name):
    n = v.shape[0]
    npad = -(-n // 1024) * 1024
    g = _small_allgather(jnp.pad(v, (0, npad - n)).reshape(8, npad // 8), name)
    return g.reshape(8, npad)[:, :n]


HBM_SPEC = pl.BlockSpec(memory_space=pltpu.HBM)
SEM_SPEC = pl.BlockSpec(memory_space=pltpu.SEMAPHORE)
DATAFLOW = pltpu.SideEffectType.DATAFLOW_SIDE_EFFECTING


def _gather_copies(xs, outs, send_sems, recv_sems):
    x, y, c = lax.axis_index("x"), lax.axis_index("y"), lax.axis_index("c")
    dests = [(x, y, 1 - c), (1 - x, y, c), (x, 1 - y, c), (1 - x, 1 - y, c)]
    return [pltpu.make_async_remote_copy(
        src_ref=xs[w], dst_ref=outs[w].at[4 * x + 2 * y + c], send_sem=send_sems[4 * w + k],
        recv_sem=recv_sems[4 * w + k], device_id=dests[k], device_id_type=MESH)
        for w in range(len(xs)) for k in range(4)]


def _split_start(copies, per_w, srcs, land_lead, after, name):
    n = len(srcs)
    m = per_w * n
    lands = [lax.empty((land_lead,) + s.shape[-2:], s.dtype) for s in srcs]
    deps = [] if after is None else [after]

    def body(*refs):
        xs, ls, outs = refs[:n], refs[n:2 * n], refs[2 * n + len(deps):]
        for cp in copies(xs, ls, outs[:m], outs[m:2 * m]):
            cp.start()
        token = outs[2 * m + 2 * n]
        token[...] = jnp.zeros_like(token)

    outs = pl.pallas_call(
        body, name=name,
        out_shape=([pltpu.SemaphoreType.DMA(())] * (2 * m) + [pltpu.HBM(a.shape, a.dtype) for a in srcs + lands]
                   + [jax.ShapeDtypeStruct((8, 128), F32)]),
        in_specs=[HBM_SPEC] * (2 * n) + [pl.BlockSpec(memory_space=pl.ANY)] * len(deps),
        out_specs=[SEM_SPEC] * (2 * m) + [HBM_SPEC] * (2 * n) + [pl.BlockSpec(memory_space=pltpu.VMEM)],
        input_output_aliases={w: 2 * m + w for w in range(2 * n)},
        compiler_params=pltpu.CompilerParams(has_side_effects=DATAFLOW),
    )(*[pltpu.with_memory_space_constraint(a, pltpu.HBM) for a in srcs + lands], *deps)
    return outs[:2 * m], outs[2 * m:2 * m + n], outs[2 * m + n:2 * m + 2 * n], outs[2 * m + 2 * n]


def _split_wait(copies, per_w, sems, x_thru, l_thru, after, name):
    n = len(x_thru)
    m = per_w * n

    def body(*refs):
        xs, ls, ss = refs[:n], refs[n:2 * n], refs[2 * n:2 * n + 2 * m]
        for cp in copies(xs, ls, ss[:m], ss[m:]):
            cp.wait_send()
            cp.wait_recv()

    outs = pl.pallas_call(
        body, name=name,
        out_shape=[pltpu.HBM(a.shape, a.dtype) for a in list(x_thru) + list(l_thru)],
        in_specs=[HBM_SPEC] * (2 * n) + [SEM_SPEC] * (2 * m) + [pl.BlockSpec(memory_space=pl.ANY)],
        out_specs=[HBM_SPEC] * (2 * n),
        input_output_aliases={w: w for w in range(2 * n)},
        compiler_params=pltpu.CompilerParams(has_side_effects=DATAFLOW),
    )(*x_thru, *l_thru, *sems, after)
    return outs[:n], outs[n:]


def _gather_forward(gathered, name):
    n = len(gathered)

    def body(*refs):
        outs = refs[n:2 * n]
        send_sems, recv_sems = refs[2 * n:]
        x, y, c = lax.axis_index("x"), lax.axis_index("y"), lax.axis_index("c")
        chips = [(1 - x, y), (x, 1 - y), (1 - x, 1 - y), (x, y)]

        def copy(w, j, pc):
            px, py = chips[j]
            slot = outs[w].at[4 * px + 2 * py + pc]
            return pltpu.make_async_remote_copy(
                src_ref=slot, dst_ref=slot, send_sem=send_sems.at[4 * w + j], recv_sem=recv_sems.at[4 * w + j],
                device_id=(x, y, 1 - c), device_id_type=MESH)

        sends = [copy(w, j, 1 - c if j == 3 else c) for w in range(n) for j in range(4)]
        for cp in sends:
            cp.start()
        for w in range(n):
            for j in range(4):
                copy(w, j, c if j == 3 else 1 - c).wait_recv()
        for cp in sends:
            cp.wait_send()

    any_spec = pl.BlockSpec(memory_space=pl.ANY)
    return pl.pallas_call(
        body, name=name,
        out_shape=[jax.ShapeDtypeStruct(g.shape, g.dtype) for g in gathered],
        in_specs=[any_spec] * n, out_specs=[any_spec] * n,
        input_output_aliases={w: w for w in range(n)},
        scratch_shapes=[pltpu.SemaphoreType.DMA((4 * n,)), pltpu.SemaphoreType.DMA((4 * n,))],
    )(*gathered)


def _forward_copies(gs, ls, send_sems, recv_sems):
    x, y, c = lax.axis_index("x"), lax.axis_index("y"), lax.axis_index("c")
    chips = [(1 - x, y), (x, 1 - y), (1 - x, 1 - y), (x, y)]
    cps = []
    for w in range(len(gs)):
        for j, (px, py) in enumerate(chips):
            slot = gs[w].at[4 * px + 2 * py + (1 - c if j == 3 else c)]
            cps.append(pltpu.make_async_remote_copy(
                src_ref=slot, dst_ref=slot, send_sem=send_sems[4 * w + j], recv_sem=recv_sems[4 * w + j],
                device_id=(x, y, 1 - c), device_id_type=MESH))
    return cps


def _pair_copies(gs, ls, send_sems, recv_sems):
    x, y, c = lax.axis_index("x"), lax.axis_index("y"), lax.axis_index("c")
    return [pltpu.make_async_remote_copy(
        src_ref=gs[w].at[2 * j + 1 - c], dst_ref=ls[w].at[j], send_sem=send_sems[4 * w + j],
        recv_sem=recv_sems[4 * w + j], device_id=(x, y, 1 - c), device_id_type=MESH)
        for w in range(len(gs)) for j in range(4)]


def _all_copies(xs, ls, send_sems, recv_sems):
    x, y, c = lax.axis_index("x"), lax.axis_index("y"), lax.axis_index("c")
    flips = [(a, b, e) for a in (0, 1) for b in (0, 1) for e in (0, 1)][1:]
    return [pltpu.make_async_remote_copy(
        src_ref=xs[0], dst_ref=ls[0].at[4 * x + 2 * y + c], send_sem=send_sems[k], recv_sem=recv_sems[k],
        device_id=(1 - x if a else x, 1 - y if b else y, 1 - c if e else c), device_id_type=MESH)
        for k, (a, b, e) in enumerate(flips)]


def _share_copies(fs, ls, send_sems, recv_sems):
    x, y, c = lax.axis_index("x"), lax.axis_index("y"), lax.axis_index("c")
    return [pltpu.make_async_remote_copy(
        src_ref=fs[w].at[c], dst_ref=fs[w].at[c], send_sem=send_sems[w], recv_sem=recv_sems[w],
        device_id=(x, y, 1 - c), device_id_type=MESH) for w in range(len(fs))]


def _chip_copies(hs, ls, send_sems, recv_sems):
    x, y, c = lax.axis_index("x"), lax.axis_index("y"), lax.axis_index("c")
    chips = [(1 - x, y), (x, 1 - y), (1 - x, 1 - y)]
    return [pltpu.make_async_remote_copy(
        src_ref=hs[w].at[2 * px + py], dst_ref=ls[w].at[k], send_sem=send_sems[3 * w + k],
        recv_sem=recv_sems[3 * w + k], device_id=(px, py, c), device_id_type=MESH)
        for w in range(len(hs)) for k, (px, py) in enumerate(chips)]


def _rs_add_pair(g8, r4, c, name):
    R, n = g8.shape[1:]
    rb = R // 2

    def body(c_ref, g_ref, r_ref, o_ref):
        o_ref[...] = (g_ref[...] + r_ref[...]).astype(BF16)

    spec = pl.BlockSpec((1, rb, n), lambda j, i, c_ref: (j, i, 0))
    return pl.pallas_call(
        body, name=name,
        grid_spec=pltpu.PrefetchScalarGridSpec(
            num_scalar_prefetch=1, grid=(4, R // rb),
            in_specs=[pl.BlockSpec((1, rb, n), lambda j, i, c_ref: (2 * j + c_ref[0], i, 0)), spec],
            out_specs=spec),
        out_shape=jax.ShapeDtypeStruct((4, R, n), BF16), compiler_params=_cp(2))(c, g8, r4)


def _rs_add_chips(g8, r4, r3, cj, name):
    R, n = g8.shape[1:]
    rb = R // 2

    def body(cj_ref, g_ref, p_ref, r_ref, o_ref):
        own = g_ref[0] + p_ref[0]
        o_ref[0] = ((own + r_ref[0].astype(F32)) + r_ref[1].astype(F32)) + r_ref[2].astype(F32)

    return pl.pallas_call(
        body, name=name,
        grid_spec=pltpu.PrefetchScalarGridSpec(
            num_scalar_prefetch=1, grid=(R // rb,),
            in_specs=[pl.BlockSpec((1, rb, n), lambda i, cj_ref: (2 * cj_ref[1] + cj_ref[0], i, 0)),
                      pl.BlockSpec((1, rb, n), lambda i, cj_ref: (cj_ref[1], i, 0)),
                      pl.BlockSpec((3, rb, n), lambda i, cj_ref: (0, i, 0))],
            out_specs=pl.BlockSpec((1, rb, n), lambda i, cj_ref: (cj_ref[0], i, 0))),
        out_shape=jax.ShapeDtypeStruct((2, R, n), F32), compiler_params=_cp(1))(cj, g8, r4, r3)


def _sum8(g):
    n = g.shape[1]

    def body(g_ref, o_ref):
        acc = g_ref[0:1, :]
        for k in range(1, 8):
            acc = acc + g_ref[k:k + 1, :]
        o_ref[...] = acc

    return pl.pallas_call(body, name="sum_devices", out_shape=jax.ShapeDtypeStruct((1, n), F32),
                          compiler_params=pltpu.CompilerParams(vmem_limit_bytes=VMEM_LIMIT))(g)


ADA_ROWS = 64


def _ada_fwd(cs, w, b):
    n = w.shape[1]

    def body(c_ref, w_ref, b_ref, o_ref):
        cv = c_ref[...]
        s = (cv * _sigmoid(cv)).astype(BF16)
        o_ref[...] = _dot(s, w_ref[...].astype(BF16)) + b_ref[...]

    return pl.pallas_call(body, name="ada_fwd", out_shape=jax.ShapeDtypeStruct((ADA_ROWS, n), F32),
                          compiler_params=pltpu.CompilerParams(vmem_limit_bytes=VMEM_LIMIT))(cs, w, b)


def _mod_exchange(m_cols):
    n = m_cols.shape[1]

    def body(m_ref, out_ref, send_sems, recv_sems, local_sem):
        x, y, c = lax.axis_index("x"), lax.axis_index("y"), lax.axis_index("c")
        chips = [(1 - x, y), (x, 1 - y), (1 - x, 1 - y)]

        def group(px, py):
            return m_ref.at[pl.ds(pl.multiple_of(8 * (4 * px + 2 * py + c), 8), 8), :]

        mine = pltpu.make_async_copy(group(x, y), out_ref.at[2 * x + y], local_sem)
        mine.start()
        cps = [pltpu.make_async_remote_copy(
            src_ref=group(px, py), dst_ref=out_ref.at[2 * x + y], send_sem=send_sems.at[k], recv_sem=recv_sems.at[k],
            device_id=(px, py, c), device_id_type=MESH) for k, (px, py) in enumerate(chips)]
        for cp in cps:
            cp.start()
        for cp in cps:
            cp.wait_recv()
        for cp in cps:
            cp.wait_send()
        mine.wait()

    return pl.pallas_call(
        body, name="mod_exchange", out_shape=jax.ShapeDtypeStruct((4, 8, n), F32),
        in_specs=[pl.BlockSpec(memory_space=pltpu.VMEM)], out_specs=pl.BlockSpec(memory_space=pltpu.VMEM),
        scratch_shapes=[pltpu.SemaphoreType.DMA((3,)), pltpu.SemaphoreType.DMA((3,)), pltpu.SemaphoreType.DMA],
        compiler_params=pltpu.CompilerParams(vmem_limit_bytes=VMEM_LIMIT))(m_cols)


def _ada_bwd(cs, w, dm):
    n = w.shape[1]

    def body(c_ref, w_ref, dm_ref, gw_ref, dc_ref):
        cv = c_ref[...]
        s = (cv * _sigmoid(cv)).astype(BF16)
        gw_ref[...] = _dot_tn(s, dm_ref[...].astype(BF16))
        dc_ref[...] = _dot_nt(dm_ref[0:8, :].astype(BF16), w_ref[...].astype(BF16))

    return pl.pallas_call(
        body, name="ada_bwd",
        out_shape=[jax.ShapeDtypeStruct((D, n), F32), jax.ShapeDtypeStruct((8, D), F32)],
        compiler_params=pltpu.CompilerParams(vmem_limit_bytes=VMEM_LIMIT))(cs, w, dm)


def _adamw(w, g, m, v, name):
    r, c = w.shape
    rb = r
    if r % 8 == 0 and r * c * 4 > (1 << 20):
        rb = 8
        for cand in range(8, r, 8):
            if r % cand == 0 and cand * c * 4 <= (1 << 20):
                rb = cand

    def body(w_ref, g_ref, m_ref, v_ref, go_ref, d_ref, nm_ref, nv_ref):
        gv = g_ref[...]
        go_ref[...] = gv
        mn = ADAM_B1 * m_ref[...] + (1.0 - ADAM_B1) * gv
        vn = ADAM_B2 * v_ref[...] + (1.0 - ADAM_B2) * (gv * gv)
        m_hat = mn / (1.0 - ADAM_B1 ** ADAM_STEP)
        v_hat = vn / (1.0 - ADAM_B2 ** ADAM_STEP)
        d_ref[...] = -ADAM_LR * (m_hat / (jnp.sqrt(v_hat) + ADAM_EPS) + ADAM_WD * w_ref[...])
        nm_ref[...] = mn
        nv_ref[...] = vn

    spec = pl.BlockSpec((rb, c), lambda i: (i, 0))
    shp = jax.ShapeDtypeStruct((r, c), F32)
    return pl.pallas_call(body, name=name, grid=(r // rb,), in_specs=[spec] * 4, out_specs=[spec] * 4,
                          out_shape=[shp] * 4, compiler_params=_cp(1))(w, g, m, v)


BIG = ("ffn1_w_in", "ffn1_w_out", "w_mix_in", "w_mix_out", "ffn2_w_in", "ffn2_w_out")


def _half_rows(w, c):
    half = w.shape[0] // 2
    return lax.dynamic_slice_in_dim(w, c * half, half, axis=0)


def _lower_bounds(logits):
    return jnp.cumsum(jax.nn.softmax(logits.astype(F32), axis=1), axis=1)[:, 0]


def kernel(x, c, ctx, c_ctx, w_ada, b_ada, ln_gain, ln_bias, ffn1_w_in, ffn1_w_out, w_mix_in, gla_a2_fwd, gla_a2_bwd, gla_a_bias_fwd, gla_a_bias_bwd, hgrn_lb_logits, gla_norm_gain, hgrn_norm_gain, w_mix_out, ffn2_w_in, ffn2_w_out, loss_target, m_c_ctx, m_w_ada, m_b_ada, m_ln_gain, m_ln_bias, m_ffn1_w_in, m_ffn1_w_out, m_w_mix_in, m_gla_a2_fwd, m_gla_a2_bwd, m_gla_a_bias_fwd, m_gla_a_bias_bwd, m_hgrn_lb_logits, m_gla_norm_gain, m_hgrn_norm_gain, m_w_mix_out, m_ffn2_w_in, m_ffn2_w_out, v_c_ctx, v_w_ada, v_b_ada, v_ln_gain, v_ln_bias, v_ffn1_w_in, v_ffn1_w_out, v_w_mix_in, v_gla_a2_fwd, v_gla_a2_bwd, v_gla_a_bias_fwd, v_gla_a_bias_bwd, v_hgrn_lb_logits, v_gla_norm_gain, v_hgrn_norm_gain, v_w_mix_out, v_ffn2_w_in, v_ffn2_w_out):
    xi, yi, ci = lax.axis_index("x"), lax.axis_index("y"), lax.axis_index("c")
    chip = 2 * xi + yi
    dev = 2 * chip + ci
    B = x.shape[0]
    weights = dict(ffn1_w_in=ffn1_w_in[0], ffn1_w_out=ffn1_w_out[0], w_mix_in=w_mix_in[0], w_mix_out=w_mix_out[0],
                   ffn2_w_in=ffn2_w_in[0], ffn2_w_out=ffn2_w_out[0])

    mine = jnp.concatenate([c.reshape(-1), ln_gain.reshape(-1), ln_bias.reshape(-1), gla_a2_fwd.reshape(-1),
                            gla_a2_bwd.reshape(-1), hgrn_lb_logits.reshape(-1)])
    g1 = _gather_flat(mine, "gather_cond")
    nc = B * D
    c_all = g1[:, :nc].reshape(8 * B, D)
    per_chip = g1[0::2, nc:]
    o = 0

    def take(shape, axis):
        nonlocal o
        n = int(np.prod(shape))
        parts = per_chip[:, o:o + n].reshape((4,) + shape)
        o += n
        return jnp.concatenate([parts[j] for j in range(4)], axis=axis)

    ln_gain_f = take((3, 256), 1)
    ln_bias_f = take((3, 256), 1)
    a2f_f = take((16, 64), 1)
    a2b_f = take((16, 64), 1)
    lbl_f = take((2, 2, 128), 2)
    lb, lb_vjp = jax.vjp(_lower_bounds, lbl_f)

    assert B + 1 <= 8
    cs = jnp.concatenate([c_all.reshape(8, B, D), jnp.broadcast_to(c_ctx.reshape(1, 1, D), (8, 1, D)),
                          jnp.zeros((8, 7 - B, D), F32)], axis=1).reshape(ADA_ROWS, D)
    ncol = w_ada.shape[2]
    b_cols = lax.dynamic_slice_in_dim(b_ada, chip * ncol, ncol, axis=1)
    m4 = _mod_exchange(_ada_fwd(cs, w_ada[0], b_cols))
    m_all = jnp.concatenate([m4[j] for j in range(4)], axis=1)
    m_lat = m_all[:B].reshape(B, 1, N_MOD, D)
    m_ctx = jnp.broadcast_to(m_all[B].reshape(1, 1, N_MOD, D), (B, 1, N_MOD, D))

    groups = dict(ffn2=("ffn2_w_in", "ffn2_w_out"), mix=("w_mix_in", "w_mix_out"), ffn1=("ffn1_w_in", "ffn1_w_out"))
    shards = dict(weights, w_mix_in=jnp.pad(weights["w_mix_in"], ((0, 0), (0, MIX_NP - MIX_N))))
    blks = {k: _half_rows(shards[k], ci).astype(BF16) for k in BIG}
    gathering = {}
    token = m_all
    for group in ("ffn1", "mix", "ffn2"):
        sems, x_thru, l_thru, token = _split_start(_gather_copies, 4, [blks[k] for k in groups[group]], 8, token,
                                                   "weight_gather_start_" + group)
        gathering[group] = (sems, x_thru, l_thru)
    mvec = jnp.concatenate([m_ctx, m_lat], axis=1) + token[0, 0]

    forwarding = {}

    def prefetch(group, after):
        _, got = _split_wait(_gather_copies, 4, *gathering[group], after, "weight_gather_wait_" + group)
        forwarding[group] = _split_start(_forward_copies, 4, list(got), 1, None, "weight_gather_forward_start_" + group)
        return forwarding[group][3]

    def weights_for(group, after):
        names = groups[group]
        if group in forwarding:
            sems, g_thru, l_thru, _ = forwarding[group]
            (w_in, w_out), _ = _split_wait(_forward_copies, 4, sems, g_thru, l_thru, after,
                                           "weight_gather_forward_wait_" + group)
        else:
            _, got = _split_wait(_gather_copies, 4, *gathering[group], after, "weight_gather_wait_" + group)
            w_in, w_out = _gather_forward(got, "weight_gather_forward_" + group)
        if group == "mix":
            return _mix_in_to_padded(w_in.reshape(4, D, MIX_NP)), w_out.reshape(-1, D)
        return w_in.reshape((4,) + shards[names[0]].shape), w_out.reshape(-1, D)

    cvec = ci.reshape(1).astype(jnp.int32)
    cjvec = jnp.stack([ci, chip]).astype(jnp.int32)
    in_flight = {}

    def on_grads(group, gs):
        names = groups[group]
        if group == "mix":
            gs = (_mix_in_from_padded(gs[0]), gs[1])
        g8s = [g.reshape((8, shards[k].shape[0] // 2, shards[k].shape[1])) for k, g in zip(names, gs)]
        sems, g_thru, l_thru, token = _split_start(_pair_copies, 4, g8s, 4, None, "grad_pair_start_" + group)
        in_flight[group] = (sems, g_thru, l_thru)
        return token

    def on_sent(group, after):
        g8s, r4s = _split_wait(_pair_copies, 4, *in_flight[group], after, "grad_pair_wait_" + group)
        h4s = [_rs_add_pair(g, r, cvec, "grad_pair_add_" + k) for k, g, r in zip(groups[group], g8s, r4s)]
        sems, h_thru, l_thru, token = _split_start(_chip_copies, 3, h4s, 3, None, "grad_chip_start_" + group)
        in_flight[group] = (g8s, r4s, sems, h_thru, l_thru)
        return token

    loss_l, grad_x, dm, small = _local_step(
        x, ctx, loss_target, mvec, weights_for, prefetch, ln_gain_f, ln_bias_f, a2f_f, a2b_f,
        gla_a_bias_fwd, gla_a_bias_bwd, lb, gla_norm_gain, hgrn_norm_gain, on_grads, on_sent)
    loss = lax.psum(loss_l, ("x", "y", "c"))

    dm_lat = dm[:, 1].reshape(B, N_MOD * D)
    dm_ctx = jnp.sum(dm[:, 0], axis=0).reshape(N_MOD * D)
    keys = ("ln_gain", "ln_bias", "a2f", "a2b", "abf", "abb", "lb", "gng", "gnh")
    flat = jnp.concatenate([dm_lat.reshape(-1), dm_ctx] + [small[k].reshape(-1) for k in keys])
    nflat = flat.shape[0]
    npad = -(-nflat // 1024) * 1024
    flat2 = jnp.pad(flat, (0, npad - nflat)).reshape(8, npad // 8)
    s_sems, s_src, s_land, tok = _split_start(_all_copies, 7, [flat2], 8, None, "small_grads_start")

    fin = {}
    for group in ("ffn2", "mix", "ffn1"):
        g8s, r4s, sems, h_thru, l_thru = in_flight[group]
        _, r3s = _split_wait(_chip_copies, 3, sems, h_thru, l_thru, tok, "grad_chip_wait_" + group)
        for k, g, r4, r3 in zip(groups[group], g8s, r4s, r3s):
            fin[k] = _rs_add_chips(g, r4, r3, cjvec, "grad_chip_add_" + k)
    p_sems, p_src, p_land, tok = _split_start(_share_copies, 1, [fin[k] for k in BIG], 1, None, "grad_pair_share_start")

    _, (land,) = _split_wait(_all_copies, 7, s_sems, s_src, s_land, tok, "small_grads_wait")
    g3 = lax.dynamic_update_index_in_dim(land, flat2, dev, 0).reshape(8, npad)[:, :nflat]
    nlat = B * N_MOD * D
    dm_all = g3[:, :nlat].reshape(8 * B, N_MOD * D)
    tot = _sum8(g3[:, nlat:])[0]
    dmc_tot = tot[:N_MOD * D]
    o = N_MOD * D
    sg = {}
    for k in keys:
        n = int(np.prod(small[k].shape))
        sg[k] = tot[o:o + n].reshape(small[k].shape)
        o += n
    ctx_rows = jnp.zeros((8, 1, N_MOD * D), F32).at[0, 0].set(dmc_tot)
    dm_rows = jnp.concatenate([dm_all.reshape(8, B, -1), ctx_rows, jnp.zeros((8, 7 - B, N_MOD * D), F32)],
                              axis=1).reshape(ADA_ROWS, N_MOD * D)
    g_b_ada = (jnp.sum(dm_all, axis=0) + dmc_tot).reshape(1, N_MOD * D)
    g_w_ada, dcc = _ada_bwd(cs, w_ada[0], lax.dynamic_slice_in_dim(dm_rows, chip * ncol, ncol, axis=1))
    g4 = _gather_flat(dcc[B], "gather_cctx")
    dsilu = ((g4[0] + g4[2]) + g4[4]) + g4[6]
    sc = _sigmoid(c_ctx)
    g_c_ctx = dsilu * (sc * (1.0 + c_ctx * (1.0 - sc)))
    (g_lbl,) = lb_vjp(sg["lb"])

    def cols(a, n, axis):
        return lax.dynamic_slice_in_dim(a, chip * n, n, axis=axis)

    shared, _ = _split_wait(_share_copies, 1, p_sems, p_src, p_land, g_w_ada, "grad_pair_share_wait")
    gsh = {k: both.reshape(shards[k].shape)[:, :weights[k].shape[1]] for k, both in zip(BIG, shared)}

    grads = dict(
        c_ctx=g_c_ctx, w_ada=g_w_ada[None], b_ada=g_b_ada, ln_gain=cols(sg["ln_gain"], 256, 1)[None],
        ln_bias=cols(sg["ln_bias"], 256, 1)[None], ffn1_w_in=gsh["ffn1_w_in"][None], ffn1_w_out=gsh["ffn1_w_out"][None],
        w_mix_in=gsh["w_mix_in"][None], gla_a2_fwd=cols(sg["a2f"], 64, 1)[None], gla_a2_bwd=cols(sg["a2b"], 64, 1)[None],
        gla_a_bias_fwd=sg["abf"], gla_a_bias_bwd=sg["abb"], hgrn_lb_logits=cols(g_lbl, 128, 2),
        gla_norm_gain=sg["gng"].reshape(1, HD), hgrn_norm_gain=sg["gnh"].reshape(1, HD),
        w_mix_out=gsh["w_mix_out"][None], ffn2_w_in=gsh["ffn2_w_in"][None], ffn2_w_out=gsh["ffn2_w_out"][None])
    params = dict(
        c_ctx=(c_ctx, m_c_ctx, v_c_ctx), w_ada=(w_ada, m_w_ada, v_w_ada), b_ada=(b_ada, m_b_ada, v_b_ada),
        ln_gain=(ln_gain, m_ln_gain, v_ln_gain), ln_bias=(ln_bias, m_ln_bias, v_ln_bias),
        ffn1_w_in=(ffn1_w_in, m_ffn1_w_in, v_ffn1_w_in), ffn1_w_out=(ffn1_w_out, m_ffn1_w_out, v_ffn1_w_out),
        w_mix_in=(w_mix_in, m_w_mix_in, v_w_mix_in), gla_a2_fwd=(gla_a2_fwd, m_gla_a2_fwd, v_gla_a2_fwd),
        gla_a2_bwd=(gla_a2_bwd, m_gla_a2_bwd, v_gla_a2_bwd),
        gla_a_bias_fwd=(gla_a_bias_fwd, m_gla_a_bias_fwd, v_gla_a_bias_fwd),
        gla_a_bias_bwd=(gla_a_bias_bwd, m_gla_a_bias_bwd, v_gla_a_bias_bwd),
        hgrn_lb_logits=(hgrn_lb_logits, m_hgrn_lb_logits, v_hgrn_lb_logits),
        gla_norm_gain=(gla_norm_gain, m_gla_norm_gain, v_gla_norm_gain),
        hgrn_norm_gain=(hgrn_norm_gain, m_hgrn_norm_gain, v_hgrn_norm_gain),
        w_mix_out=(w_mix_out, m_w_mix_out, v_w_mix_out), ffn2_w_in=(ffn2_w_in, m_ffn2_w_in, v_ffn2_w_in),
        ffn2_w_out=(ffn2_w_out, m_ffn2_w_out, v_ffn2_w_out))
    order = list(params.keys())
    big_names = ("w_ada",) + BIG
    upd = {}
    for k in big_names:
        w_, m_, v_ = params[k]
        s2 = w_.shape[-2:]
        g_, d_, nm_, nv_ = _adamw(w_.reshape(s2), grads[k].reshape(s2), m_.reshape(s2), v_.reshape(s2), "adamw_" + k)
        grads[k] = g_
        upd[k] = (d_.reshape(w_.shape), nm_.reshape(w_.shape), nv_.reshape(w_.shape))
    small_names = [k for k in order if k not in big_names]
    sizes = [int(np.prod(params[k][0].shape)) for k in small_names]
    tot_n = sum(sizes)
    npad = -(-tot_n // 1024) * 1024

    def packed(get):
        flat_ = jnp.concatenate([get(k).reshape(-1) for k in small_names])
        return jnp.pad(flat_, (0, npad - tot_n)).reshape(8, npad // 8)

    _, d_s, nm_s, nv_s = _adamw(packed(lambda k: params[k][0]), packed(lambda k: grads[k]),
                                packed(lambda k: params[k][1]), packed(lambda k: params[k][2]), "adamw_small")
    o = 0
    for k, n in zip(small_names, sizes):
        shp = params[k][0].shape
        upd[k] = tuple(a.reshape(-1)[o:o + n].reshape(shp) for a in (d_s, nm_s, nv_s))
        o += n

    return (loss, grad_x, *[grads[k].reshape(params[k][0].shape) for k in order], *[upd[k][0] for k in order],
            *[upd[k][1] for k in order], *[upd[k][2] for k in order])
```

```python
import functools

import numpy as np
import jax
import jax.numpy as jnp
from jax import lax
from jax.experimental import pallas as pl
from jax.experimental.pallas import tpu as pltpu

F32 = jnp.float32
BF16 = jnp.bfloat16
MESH = pl.DeviceIdType.MESH

D = 1024
DFF = 2816
TM = 256
CH = 32
NCB = TM // CH
SB = 128
CSB = SB // CH
NSB = TM // SB
HP = 8
HPB = 8
HD = 128
NH = 8
LN_EPS = 1e-5
NORM_EPS = 1e-6
ALPHA = 2.0 ** 0.25
GATE_NORM = 16.0
GLA_DK = 64
N_MOD = 9
VMEM_LIMIT = 52 * 1024 * 1024

MIXP = 5120
GG, RG, GQ, GK, GV, RQ, RFF, RFB, RI, LR = 0, 512, 1024, 1536, 2048, 2560, 3072, 3584, 4096, 4608
IN_SPLITS = (256, 256, 512, 512, 16, 16, 512, 512, 512, 512, 512)

ADAM_LR, ADAM_B1, ADAM_B2, ADAM_EPS, ADAM_WD, ADAM_STEP = 0.001, 0.9, 0.999, 1e-08, 0.01, 10


def _cp(n_axes):
    return pltpu.CompilerParams(dimension_semantics=("arbitrary",) * n_axes, vmem_limit_bytes=VMEM_LIMIT)


def _rowmap(stride, off):
    return lambda b, i: (b * stride + off + i, 0)


def _mmap(comb):
    if comb:
        return lambda b, i: (b, jnp.minimum(i, 1), 0, 0)
    return lambda b, i: (b, 1, 0, 0)


def _ln(x):
    mu = jnp.mean(x, axis=-1, keepdims=True)
    xc = x - mu
    var = jnp.mean(xc * xc, axis=-1, keepdims=True)
    r = lax.rsqrt(var + LN_EPS)
    return xc * r, r


def _ln_bwd(dxh, xh, r):
    return r * (dxh - jnp.mean(dxh, axis=-1, keepdims=True) - xh * jnp.mean(dxh * xh, axis=-1, keepdims=True))


def _sigmoid(x):
    return 1.0 / (1.0 + jnp.exp(-x))


def _rsum(x):
    return jnp.sum(x, axis=0, keepdims=True)


def _dot(a, b):
    return jnp.dot(a, b, preferred_element_type=F32)


def _dot_nt(a, b):
    return lax.dot_general(a, b, (((1,), (1,)), ((), ())), preferred_element_type=F32)


def _dot_tn(a, b):
    return lax.dot_general(a, b, (((0,), (0,)), ((), ())), preferred_element_type=F32)


def _modulate(xv, m_ref, sub):
    xh, _ = _ln(xv)
    sh = m_ref[0, 0, 3 * sub:3 * sub + 1, :]
    sc = m_ref[0, 0, 3 * sub + 1:3 * sub + 2, :]
    return (xh * (1.0 + sh) + sc).astype(BF16)


def _embed_lnmod(x, ctx, pos, mvec):
    B, T, _ = x.shape
    nt = 1 + T // TM

    def body(x_ref, c_ref, p_ref, m_ref, o_ref, h_ref):
        i = pl.program_id(1)

        @pl.when(i == 0)
        def _():
            o_ref[...] = c_ref[0]

        @pl.when(i > 0)
        def _():
            o_ref[...] = x_ref[0] + p_ref[...]

        h_ref[...] = _modulate(o_ref[...], m_ref, 0)

    rows = pl.BlockSpec((TM, D), lambda b, i: (b * nt + i, 0))
    return pl.pallas_call(
        body, name="embed_lnmod0", grid=(B, nt),
        in_specs=[pl.BlockSpec((1, TM, D), lambda b, i: (b, jnp.maximum(i - 1, 0), 0)),
                  pl.BlockSpec((1, TM, D), lambda b, i: (b, 0, 0)),
                  pl.BlockSpec((TM, D), lambda b, i: (jnp.maximum(i - 1, 0), 0)),
                  pl.BlockSpec((1, 1, N_MOD, D), _mmap(True))],
        out_specs=[rows, rows],
        out_shape=[jax.ShapeDtypeStruct((B * nt * TM, D), F32), jax.ShapeDtypeStruct((B * nt * TM, D), BF16)],
        compiler_params=_cp(2))(x, ctx, pos, mvec)


def _lnmod0_bwd(dh, X, mvec, dres, x_shape, B, nt):
    def body(dh_ref, x_ref, m_ref, dr_ref, dx_ref, dm_ref):
        i = pl.program_id(1)
        xh, r = _ln(x_ref[...])
        sh = m_ref[0, 0, 0:1, :]
        dhv = dh_ref[...].astype(F32)

        @pl.when((i == 0) | (i == 1))
        def _():
            dm_ref[...] = jnp.zeros_like(dm_ref)

        dm_ref[0, 0, 0:1, :] += _rsum(dhv * xh)
        dm_ref[0, 0, 1:2, :] += _rsum(dhv)

        @pl.when(i > 0)
        def _():
            dx_ref[0] = _ln_bwd(dhv * (1.0 + sh), xh, r) + dr_ref[...]

    rows = pl.BlockSpec((TM, D), _rowmap(nt, 0))
    return pl.pallas_call(
        body, name="lnmod0_bwd", grid=(B, nt),
        in_specs=[rows, rows, pl.BlockSpec((1, 1, N_MOD, D), _mmap(True)), rows],
        out_specs=[pl.BlockSpec((1, TM, D), lambda b, i: (b, jnp.maximum(i - 1, 0), 0)),
                   pl.BlockSpec((1, 1, 2, D), _mmap(True))],
        out_shape=[jax.ShapeDtypeStruct(x_shape, F32), jax.ShapeDtypeStruct((B, 2, 2, D), F32)],
        compiler_params=_cp(2))(dh, X, mvec, dres)


def _resid_fwd(x_ref, y_ref, m_ref, gb_ref, sub, w):
    wg = w * m_ref[0, 0, 3 * sub + 2:3 * sub + 3, :]
    y = y_ref[...].astype(F32)
    zh, r = _ln(ALPHA * x_ref[...] + wg * y)
    return y, wg, zh, r


def _resid_grads(do, y, wg, zh, r, w, gb_ref, comb, dx_ref, dy_ref, dg_ref, dgb_ref):
    b_, i = pl.program_id(0), pl.program_id(1)
    dz = _ln_bwd(do * gb_ref[0:1, :], zh, r)
    dx_ref[...] = ALPHA * dz
    dy_ref[...] = (wg * dz).astype(BF16)

    @pl.when((b_ == 0) & (i == 0))
    def _():
        dgb_ref[...] = jnp.zeros_like(dgb_ref)

    dgb_ref[0:1, :] += _rsum(do * zh)
    dgb_ref[1:2, :] += _rsum(do)

    init = (i == 0) | (i == 1) if comb else (i == 0)

    @pl.when(init)
    def _():
        dg_ref[...] = jnp.zeros_like(dg_ref)

    dg_ref[0, 0] += w * _rsum(dz * y)


def _resid_lnmod(X, lx, Y, mvec, comb, sub, w, gb, B, nt, name):
    def body(x_ref, y_ref, m_ref, gb_ref, o_ref, h_ref):
        _, _, zh, _ = _resid_fwd(x_ref, y_ref, m_ref, gb_ref, sub, w)
        xn = zh * gb_ref[0:1, :] + gb_ref[1:2, :]
        o_ref[...] = xn
        h_ref[...] = _modulate(xn, m_ref, sub + 1)

    rows = pl.BlockSpec((TM, D), _rowmap(nt, 0))
    return pl.pallas_call(
        body, name=name, grid=(B, nt),
        in_specs=[pl.BlockSpec((TM, D), _rowmap(*lx)), rows,
                  pl.BlockSpec((1, 1, N_MOD, D), _mmap(comb)), pl.BlockSpec((2, D), lambda b, i: (0, 0))],
        out_specs=[rows, rows],
        out_shape=[jax.ShapeDtypeStruct((B * nt * TM, D), F32), jax.ShapeDtypeStruct((B * nt * TM, D), BF16)],
        compiler_params=_cp(2))(X, Y, mvec, gb)


def _resid_out_shapes(B, nt, comb):
    rows = pl.BlockSpec((TM, D), _rowmap(nt, 0))
    specs = [rows, rows, pl.BlockSpec((1, 1, 1, D), _mmap(comb)), pl.BlockSpec((2, D), lambda b, i: (0, 0))]
    shapes = [jax.ShapeDtypeStruct((B * nt * TM, D), F32), jax.ShapeDtypeStruct((B * nt * TM, D), BF16),
              jax.ShapeDtypeStruct((B, 2, 1, D), F32), jax.ShapeDtypeStruct((2, D), F32)]
    return specs, shapes


def _tail(X, Y, mvec, gb, tgt, sub, w, B, nt):
    def body(x_ref, y_ref, m_ref, gb_ref, t_ref, dx_ref, dy_ref, dg_ref, dgb_ref, l_ref):
        y, wg, zh, r = _resid_fwd(x_ref, y_ref, m_ref, gb_ref, sub, w)
        e = (zh * gb_ref[0:1, :] + gb_ref[1:2, :]) - t_ref[0]

        @pl.when((pl.program_id(0) == 0) & (pl.program_id(1) == 0))
        def _():
            l_ref[...] = jnp.zeros_like(l_ref)

        l_ref[...] += _rsum(e * e)
        _resid_grads(e * (1.0 / D), y, wg, zh, r, w, gb_ref, False, dx_ref, dy_ref, dg_ref, dgb_ref)

    rows = pl.BlockSpec((TM, D), _rowmap(nt, 0))
    specs, shapes = _resid_out_shapes(B, nt, False)
    return pl.pallas_call(
        body, name="resid2_loss_bwd", grid=(B, nt),
        in_specs=[rows, rows, pl.BlockSpec((1, 1, N_MOD, D), _mmap(False)), pl.BlockSpec((2, D), lambda b, i: (0, 0)),
                  pl.BlockSpec((1, TM, D), lambda b, i: (b, i, 0))],
        out_specs=specs + [pl.BlockSpec((1, D), lambda b, i: (0, 0))],
        out_shape=shapes + [jax.ShapeDtypeStruct((1, D), F32)],
        compiler_params=_cp(2))(X, Y, mvec, gb, tgt)


def _lnmod_resid_bwd(dh, Xi, lxi, dres, Xp, lxp, Yp, mvec, comb, sub, w, gb, B, nt, name):
    ntl = nt - 1 if comb else nt

    def body(dh_ref, xi_ref, dr_ref, xp_ref, yp_ref, m_ref, gb_ref, dx_ref, dy_ref, dg_ref, dgb_ref, dm_ref):
        i = pl.program_id(1)
        xh, r = _ln(xi_ref[...])
        sh = m_ref[0, 0, 3 * sub:3 * sub + 1, :]
        dhv = dh_ref[...].astype(F32)
        dr = dr_ref[...]
        if comb:
            dr = jnp.where(i > 0, dr, 0.0)
        do = _ln_bwd(dhv * (1.0 + sh), xh, r) + dr

        init = (i == 0) | (i == 1) if comb else (i == 0)

        @pl.when(init)
        def _():
            dm_ref[...] = jnp.zeros_like(dm_ref)

        dm_ref[0, 0, 0:1, :] += _rsum(dhv * xh)
        dm_ref[0, 0, 1:2, :] += _rsum(dhv)
        y, wg, zh, r2 = _resid_fwd(xp_ref, yp_ref, m_ref, gb_ref, sub - 1, w)
        _resid_grads(do, y, wg, zh, r2, w, gb_ref, comb, dx_ref, dy_ref, dg_ref, dgb_ref)

    rows = pl.BlockSpec((TM, D), _rowmap(nt, 0))
    if comb:
        dres_spec = pl.BlockSpec((TM, D), lambda b, i: (b * ntl + jnp.maximum(i - 1, 0), 0))
    else:
        dres_spec = rows
    specs, shapes = _resid_out_shapes(B, nt, comb)
    return pl.pallas_call(
        body, name=name, grid=(B, nt),
        in_specs=[rows, pl.BlockSpec((TM, D), _rowmap(*lxi)), dres_spec, pl.BlockSpec((TM, D), _rowmap(*lxp)), rows,
                  pl.BlockSpec((1, 1, N_MOD, D), _mmap(comb)), pl.BlockSpec((2, D), lambda b, i: (0, 0))],
        out_specs=specs + [pl.BlockSpec((1, 1, 2, D), _mmap(comb))],
        out_shape=shapes + [jax.ShapeDtypeStruct((B, 2, 2, D), F32)],
        compiler_params=_cp(2))(dh, Xi, dres, Xp, Yp, mvec, gb)


def _ffn_out_dx(dy, W, u, name, dep=None):
    M = dy.shape[0]
    half = DFF // 2
    deps = [] if dep is None else [dep]

    def body(dy_ref, w_ref, u_ref, *rest):
        du_ref = rest[-1]
        dyv = dy_ref[...]
        for j in range(2):
            lo, hi = j * half, (j + 1) * half
            da = _dot_nt(dyv, w_ref[lo:hi, :])
            g = u_ref[:, lo:hi].astype(F32)
            up = u_ref[:, DFF + lo:DFF + hi].astype(F32)
            s = _sigmoid(g)
            du_ref[:, lo:hi] = (da * up * (s * (1.0 + g * (1.0 - s)))).astype(BF16)
            du_ref[:, DFF + lo:DFF + hi] = (da * (g * s)).astype(BF16)

    return pl.pallas_call(
        body, name=name, grid=(M // TM,),
        in_specs=[pl.BlockSpec((TM, D), lambda i: (i, 0)), _wspec(W, 1), pl.BlockSpec((TM, 2 * DFF), lambda i: (i, 0))]
        + [_wspec(d, 1) for d in deps],
        out_specs=pl.BlockSpec((TM, 2 * DFF), lambda i: (i, 0)),
        out_shape=jax.ShapeDtypeStruct((M, 2 * DFF), BF16), compiler_params=_cp(1))(dy, W, u, *deps)


def _wspec(W, nidx):
    zeros = (0,) * W.ndim
    if nidx == 1:
        return pl.BlockSpec(W.shape, lambda i: zeros)
    return pl.BlockSpec(W.shape, lambda b, i: zeros)


def _mm_nn(A, la, W, B, nt, out_dtype, name):
    K, N = W.shape

    def body(a_ref, w_ref, o_ref):
        o_ref[...] = _dot(a_ref[...], w_ref[...]).astype(out_dtype)

    return pl.pallas_call(
        body, name=name, grid=(B, nt),
        in_specs=[pl.BlockSpec((TM, K), _rowmap(*la)), _wspec(W, 2)],
        out_specs=pl.BlockSpec((TM, N), _rowmap(nt, 0)),
        out_shape=jax.ShapeDtypeStruct((B * nt * TM, N), out_dtype), compiler_params=_cp(2))(A, W)


def _ffn_in(A, la, W3, B, nt, name):
    K, n = W3.shape[1:]

    def body(a_ref, w_ref, u_ref, s_ref):
        a = a_ref[...]
        for j in range(2):
            g = _dot(a, w_ref[j])
            up = _dot(a, w_ref[j + 2])
            u_ref[:, j * n:(j + 1) * n] = g.astype(BF16)
            u_ref[:, (j + 2) * n:(j + 3) * n] = up.astype(BF16)
            s_ref[:, j * n:(j + 1) * n] = (g * _sigmoid(g) * up).astype(BF16)

    rows = B * nt * TM
    return pl.pallas_call(
        body, name=name, grid=(B, nt),
        in_specs=[pl.BlockSpec((TM, K), _rowmap(*la)), _wspec(W3, 2)],
        out_specs=[pl.BlockSpec((TM, 4 * n), _rowmap(nt, 0)), pl.BlockSpec((TM, 2 * n), _rowmap(nt, 0))],
        out_shape=[jax.ShapeDtypeStruct((rows, 4 * n), BF16), jax.ShapeDtypeStruct((rows, 2 * n), BF16)],
        compiler_params=_cp(2))(A, W3)


def _mm_nt(A, W, name, dep=None, out_dtype=F32):
    M, N = A.shape
    K = W.shape[-2]

    def body(a_ref, w_ref, *rest):
        o_ref = rest[-1]
        if W.ndim == 3:
            n = W.shape[-1]
            acc = _dot_nt(a_ref[:, 0:n], w_ref[0])
            for j in range(1, 4):
                acc = acc + _dot_nt(a_ref[:, j * n:(j + 1) * n], w_ref[j])
            o_ref[...] = acc.astype(out_dtype)
        else:
            o_ref[...] = _dot_nt(a_ref[...], w_ref[...]).astype(out_dtype)

    deps = [] if dep is None else [dep]
    return pl.pallas_call(
        body, name=name, grid=(M // TM,),
        in_specs=[pl.BlockSpec((TM, N), lambda i: (i, 0)), _wspec(W, 1)] + [_wspec(d, 1) for d in deps],
        out_specs=pl.BlockSpec((TM, K), lambda i: (i, 0)),
        out_shape=jax.ShapeDtypeStruct((M, K), out_dtype), compiler_params=_cp(1))(A, W, *deps)


def _mm_tn(A, G, name, tn=512, shards=None):
    M, K = A.shape
    N = G.shape[1]
    tk = next(t for t in ((2048, 1536, 1024, 512) if K <= D else (1024, 512)) if M % t == 0)
    if shards:
        tn = N // shards

    def body(a_ref, g_ref, o_ref):
        @pl.when(pl.program_id(1) == 0)
        def _():
            o_ref[...] = jnp.zeros_like(o_ref)

        upd = _dot_tn(a_ref[...], g_ref[...])
        if shards:
            o_ref[0] += upd
        else:
            o_ref[...] += upd

    if shards:
        out_spec = pl.BlockSpec((1, K, tn), lambda n, k: (n, 0, 0))
        out_shape = jax.ShapeDtypeStruct((shards, K, tn), F32)
    else:
        out_spec = pl.BlockSpec((K, tn), lambda n, k: (0, n))
        out_shape = jax.ShapeDtypeStruct((K, N), F32)
    return pl.pallas_call(
        body, name=name, grid=(N // tn, M // tk),
        in_specs=[pl.BlockSpec((tk, K), lambda n, k: (k, 0)), pl.BlockSpec((tk, tn), lambda n, k: (k, n))],
        out_specs=out_spec, out_shape=out_shape, compiler_params=_cp(2))(A, G)


def _logsig(z):
    return jnp.minimum(z, 0.0) - jnp.log(1.0 + jnp.exp(-jnp.abs(z)))


ZW = 2688
ZRQ, ZRFF, ZRFB, ZLR = 1024, 1536, 2048, 2560


def _mix_in_features(H, W, a2p, biasp, lbp, B, nt):
    K = W.shape[0]

    def body(h_ref, w_ref, a2_ref, bias_ref, lb_ref, q_ref, k_ref, v_ref, g_ref, z_ref):
        hv = h_ref[...]
        proj = lambda lo, hi: _dot(hv, w_ref[:, lo:hi])
        z_ref[:, 0:2 * 4 * HD] = proj(GG, GG + 2 * 4 * HD)
        lrf = proj(LR, LR + HD)
        z_ref[:, ZLR:ZLR + HD] = lrf
        lr = lrf.astype(BF16)
        lane = lax.broadcasted_iota(jnp.int32, (1, 4 * HD), 1)
        keep = (lane & (HD - 1)) < GLA_DK
        for d in range(2):
            z = _dot(lr, a2_ref[d]) + bias_ref[d:d + 1, :]
            gl = jnp.where(keep, _logsig(z) * (1.0 / GATE_NORM), 0.0)
            for h in range(4):
                g_ref[d, 0, h] = gl[:, h * HD:(h + 1) * HD]
        gq, gk, gv = proj(GQ, GQ + 4 * HD), proj(GK, GK + 4 * HD), proj(GV, GV + 4 * HD)
        for h in range(4):
            sl = slice(h * HD, (h + 1) * HD)
            q_ref[0, h] = gq[:, sl] * (GLA_DK ** -0.5)
            k_ref[0, 0, h] = gk[:, sl]
            k_ref[1, 0, h] = gk[:, sl]
            v_ref[0, h] = gv[:, sl].astype(BF16)
        rqs, ri = proj(RQ, RQ + 4 * HD), proj(RI, RI + 4 * HD)
        z_ref[:, ZRQ:ZRQ + 4 * HD] = rqs
        for h in range(4):
            sl = slice(h * HD, (h + 1) * HD)
            rq = rqs[:, sl]
            q_ref[0, 4 + h] = rq * _sigmoid(rq) * (HD ** -0.5)
            v_ref[0, 4 + h] = ri[:, sl].astype(BF16)
        for d, off, zoff in ((0, RFF, ZRFF), (1, RFB, ZRFB)):
            rf = proj(off, off + 4 * HD)
            z_ref[:, zoff:zoff + 4 * HD] = rf
            for h in range(4):
                sl = slice(h * HD, (h + 1) * HD)
                lb = lb_ref[d:d + 1, sl]
                f = lb + (1.0 - lb) * _sigmoid(rf[:, sl])
                g_ref[d, 0, 4 + h] = jnp.log(f)
                k_ref[d, 0, 4 + h] = 1.0 - f

    one = pl.BlockSpec((1, NH, TM, HD), lambda b, i: (b, 0, i, 0))
    two = pl.BlockSpec((2, 1, NH, TM, HD), lambda b, i: (0, b, 0, i, 0))
    s1 = jax.ShapeDtypeStruct((B, NH, nt * TM, HD), F32)
    s2 = jax.ShapeDtypeStruct((2, B, NH, nt * TM, HD), F32)
    return pl.pallas_call(
        body, name="mix_in_features", grid=(B, nt),
        in_specs=[pl.BlockSpec((TM, K), _rowmap(nt, 0)), _wspec(W, 2), pl.BlockSpec((2, HD, 4 * HD), lambda b, i: (0, 0, 0)),
                  pl.BlockSpec((2, 4 * HD), lambda b, i: (0, 0)), pl.BlockSpec((2, 4 * HD), lambda b, i: (0, 0))],
        out_specs=[one, two, one, two, pl.BlockSpec((TM, ZW), _rowmap(nt, 0))],
        out_shape=[s1, s2, jax.ShapeDtypeStruct(s1.shape, BF16), s2, jax.ShapeDtypeStruct((B * nt * TM, ZW), F32)],
        compiler_params=_cp(2))(H, W, a2p, biasp, lbp)


def _features_bwd(Z, a2p, biasp, lbp, dQ0, dQ1, dK0, dK1, dV0, dV1, dG0, dG1, dgates, W, B):
    nt = Z.shape[0] // (B * TM)

    def body(z_ref, a2_ref, bias_ref, lb_ref, dq0, dq1, dk0, dk1, dv0, dv1, dg0, dg1, dgt_ref, w_ref,
             df_ref, dh_ref, da2_ref, dbias_ref, dlb_ref):
        b_, i = pl.program_id(0), pl.program_id(1)

        @pl.when((b_ == 0) & (i == 0))
        def _():
            da2_ref[...] = jnp.zeros_like(da2_ref)
            dbias_ref[...] = jnp.zeros_like(dbias_ref)
            dlb_ref[...] = jnp.zeros_like(dlb_ref)

        df_ref[:, 0:2 * 4 * HD] = jnp.where(i > 0, dgt_ref[...], 0.0).astype(BF16)
        df_ref[:, LR + HD:] = jnp.zeros((TM, MIXP - LR - HD), BF16)
        lr = z_ref[:, ZLR:ZLR + HD].astype(BF16)
        lane = lax.broadcasted_iota(jnp.int32, (1, 4 * HD), 1)
        keep = (lane & (HD - 1)) < GLA_DK
        dlr = jnp.zeros((TM, HD), F32)
        dgs = (dg0, dg1)
        dks = (dk0, dk1)
        rd = lambda ref, h: ref[0, h].astype(F32)
        for d in range(2):
            z = _dot(lr, a2_ref[d]) + bias_ref[d:d + 1, :]
            dgl = jnp.concatenate([rd(dgs[d], h) for h in range(4)], axis=1)
            dz = jnp.where(keep, dgl * (1.0 / GATE_NORM) * (1.0 - _sigmoid(z)), 0.0)
            dzb = dz.astype(BF16)
            dlr = dlr + _dot_nt(dzb, a2_ref[d])
            da2_ref[d] += _dot_tn(lr, dzb)
            dbias_ref[d:d + 1, :] += _rsum(dz)
        df_ref[:, LR:LR + HD] = dlr.astype(BF16)
        for h in range(4):
            df_ref[:, GQ + h * HD:GQ + (h + 1) * HD] = ((rd(dq0, h) + rd(dq1, h)) * (GLA_DK ** -0.5)).astype(BF16)
            df_ref[:, GK + h * HD:GK + (h + 1) * HD] = (rd(dk0, h) + rd(dk1, h)).astype(BF16)
            df_ref[:, GV + h * HD:GV + (h + 1) * HD] = (rd(dv0, h) + rd(dv1, h)).astype(BF16)
        for h in range(4):
            sl = slice(h * HD, (h + 1) * HD)
            rq = z_ref[:, ZRQ + h * HD:ZRQ + (h + 1) * HD]
            s = _sigmoid(rq)
            dqh = rd(dq0, 4 + h) + rd(dq1, 4 + h)
            df_ref[:, RQ + h * HD:RQ + (h + 1) * HD] = (dqh * (HD ** -0.5) * (s * (1.0 + rq * (1.0 - s)))).astype(BF16)
            df_ref[:, RI + h * HD:RI + (h + 1) * HD] = (rd(dv0, 4 + h) + rd(dv1, 4 + h)).astype(BF16)
            for d, off, zoff in ((0, RFF, ZRFF), (1, RFB, ZRFB)):
                lb = lb_ref[d:d + 1, sl]
                sg = _sigmoid(z_ref[:, zoff + h * HD:zoff + (h + 1) * HD])
                f = lb + (1.0 - lb) * sg
                dff = rd(dgs[d], 4 + h) / f - rd(dks[d], 4 + h)
                df_ref[:, off + h * HD:off + (h + 1) * HD] = (dff * (1.0 - lb) * sg * (1.0 - sg)).astype(BF16)
                dlb_ref[d:d + 1, sl] += _rsum(dff * (1.0 - sg))
        dh_ref[...] = _dot_nt(df_ref[...], w_ref[...]).astype(BF16)

    m0 = lambda b, i: (b, 0, i, 0)
    one = lambda m: pl.BlockSpec((1, NH, TM, HD), m)
    return pl.pallas_call(
        body, name="mix_features_bwd", grid=(B, nt),
        in_specs=[pl.BlockSpec((TM, ZW), _rowmap(nt, 0)), pl.BlockSpec((2, HD, 4 * HD), lambda b, i: (0, 0, 0)),
                  pl.BlockSpec((2, 4 * HD), lambda b, i: (0, 0)), pl.BlockSpec((2, 4 * HD), lambda b, i: (0, 0)),
                  one(m0), one(m0), one(m0), one(m0), one(m0), one(m0), one(m0), one(m0),
                  pl.BlockSpec((TM, D), lambda b, i: (b * (nt - 1) + jnp.maximum(i - 1, 0), 0)), _wspec(W, 2)],
        out_specs=[pl.BlockSpec((TM, MIXP), _rowmap(nt, 0)), pl.BlockSpec((TM, D), _rowmap(nt, 0)),
                   pl.BlockSpec((2, HD, 4 * HD), lambda b, i: (0, 0, 0)),
                   pl.BlockSpec((2, 4 * HD), lambda b, i: (0, 0)), pl.BlockSpec((2, 4 * HD), lambda b, i: (0, 0))],
        out_shape=[jax.ShapeDtypeStruct((B * nt * TM, MIXP), BF16), jax.ShapeDtypeStruct((B * nt * TM, D), BF16),
                   jax.ShapeDtypeStruct((2, HD, 4 * HD), F32),
                   jax.ShapeDtypeStruct((2, 4 * HD), F32), jax.ShapeDtypeStruct((2, 4 * HD), F32)],
        compiler_params=_cp(2))(Z, a2p, biasp, lbp, dQ0, dQ1, dK0, dK1, dV0, dV1, dG0, dG1, dgates, W)


def _chunk_scan(x, rin, fwd):
    acc = x
    sft = 1
    while sft < CH:
        if fwd:
            acc = acc + jnp.where(rin >= sft, pltpu.roll(acc, sft, 0), 0.0)
        else:
            acc = acc + jnp.where(rin < CH - sft, pltpu.roll(acc, TM - sft, 0), 0.0)
        sft *= 2
    return acc


def _chunk_total(x):
    t = jnp.sum(x.reshape(NCB, CH, HD), axis=1, keepdims=True)
    return jnp.broadcast_to(t, (NCB, CH, HD)).reshape(TM, HD)


def _scan_masks(rev):
    rin = lax.broadcasted_iota(jnp.int32, (TM, HD), 0) & (CH - 1)
    ri = lax.broadcasted_iota(jnp.int32, (SB, SB), 0)
    ci = lax.broadcasted_iota(jnp.int32, (SB, SB), 1)
    same = (ri >> 5) == (ci >> 5)
    lo = same & (ri >= ci)
    up = same & (ri <= ci)
    mask, maskT = (up, lo) if rev else (lo, up)
    re = lax.broadcasted_iota(jnp.int32, (SB, CSB * HD), 0) >> 5
    ce = lax.broadcasted_iota(jnp.int32, (SB, CSB * HD), 1) >> 7
    return rin, mask, maskT, re == ce


def _scan_decay(q, k, g, rin, rev):
    b = _chunk_scan(g, rin, not rev)
    xx = _chunk_total(g) - b
    eb = jnp.exp(b)
    return b, xx, eb, q * eb, k * jnp.exp(-b), k * jnp.exp(xx)


def _sub(x, s):
    return x[s * SB:(s + 1) * SB]


def _expand(xb, mexp):
    return jnp.where(mexp, jnp.concatenate([xb] * CSB, axis=1), jnp.zeros((), xb.dtype))


def _own(x, mexp):
    xm = jnp.where(mexp, x, 0.0)
    acc = xm[:, 0:HD]
    for n in range(1, CSB):
        acc = acc + xm[:, n * HD:(n + 1) * HD]
    return acc


def _stack(per_chunk, s):
    return jnp.concatenate(per_chunk[s * CSB:(s + 1) * CSB], axis=1)


def _state_pass(s0, eb, uts, rev):
    order = range(NCB - 1, -1, -1) if rev else range(NCB)
    states = [None] * NCB
    s = s0
    for n in order:
        row = n * CH if rev else n * CH + CH - 1
        states[n] = s
        s = eb[row:row + 1, :] * s + uts[n // CSB][:, (n % CSB) * HD:(n % CSB + 1) * HD]
    return states, s


def _scan_fwd(Q, K, V, G, rev, B):
    nb = Q.shape[2] // TM
    d = 1 if rev else 0
    rmap = (lambda s: jnp.where(s == 0, 0, nb - s)) if rev else (lambda s: s)

    def body(q_ref, k_ref, v_ref, g_ref, o_ref, st_ref, s_scr):
        @pl.when(pl.program_id(2) == 0)
        def _():
            s_scr[...] = jnp.zeros_like(s_scr)

        rin, mask, _, mexp = _scan_masks(rev)

        def head(p, readout):
            s0 = s_scr[p]
            st_ref[0, p, 0] = s0
            _, _, eb, qd, ki, kt = _scan_decay(q_ref[0, p], k_ref[0, 0, p], g_ref[0, 0, p], rin, rev)
            ktb, vb = kt.astype(BF16), v_ref[0, p]
            uts = [_dot_tn(_sub(vb, s), _expand(_sub(ktb, s), mexp)) for s in range(NSB)]
            states, s_new = _state_pass(s0, eb, uts, rev)
            s_scr[p] = s_new
            if not readout:
                return
            qb, kib = qd.astype(BF16), ki.astype(BF16)
            for s in range(NSB):
                a = jnp.where(mask, _dot_nt(_sub(qb, s), _sub(kib, s)), 0.0)
                o_ref[0, p, s * SB:(s + 1) * SB, :] = (
                    _dot(a.astype(BF16), _sub(vb, s))
                    + _dot_nt(_expand(_sub(qb, s), mexp), _stack(states, s).astype(BF16)))

        @pl.when(pl.program_id(2) >= 1)
        def _():
            for p in range(HP):
                head(p, True)

        @pl.when(pl.program_id(2) == 0)
        def _():
            for p in range(HP):
                head(p, False)

    one = pl.BlockSpec((1, HP, TM, HD), lambda b, h, s: (b, h, rmap(s), 0))
    two = pl.BlockSpec((1, 1, HP, TM, HD), lambda b, h, s: (d, b, h, rmap(s), 0))
    return pl.pallas_call(
        body, name="scan_fwd_rev" if rev else "scan_fwd", grid=(B, NH // HP, nb),
        in_specs=[one, two, one, two],
        out_specs=[one, pl.BlockSpec((1, HP, 1, HD, HD), lambda b, h, s: (b, h, s, 0, 0))],
        out_shape=[jax.ShapeDtypeStruct(Q.shape, F32), jax.ShapeDtypeStruct((B, NH, nb, HD, HD), F32)],
        scratch_shapes=[pltpu.VMEM((HP, HD, HD), F32)],
        compiler_params=_cp(3))(Q, K, V, G)


def _scan_bwd(Q, K, V, G, St, dO, rev, B):
    nb = Q.shape[2] // TM
    d = 1 if rev else 0
    smap = lambda t: nb - 1 - t
    rmap = (lambda t: jnp.where(smap(t) == 0, 0, nb - smap(t))) if rev else smap

    def body(q_ref, k_ref, v_ref, g_ref, st_ref, do_ref, dq_ref, dk_ref, dv_ref, dg_ref, ds_scr):
        t = pl.program_id(2)

        @pl.when(t == 0)
        def _():
            ds_scr[...] = jnp.zeros_like(ds_scr)

        is_lat = smap(t) >= 1
        rin, mask, maskT, mexp = _scan_masks(rev)
        for p in range(HPB):
            b, xx, eb, qd, ki, kt = _scan_decay(q_ref[0, p], k_ref[0, 0, p], g_ref[0, 0, p], rin, rev)
            qb, kib, ktb, vb = qd.astype(BF16), ki.astype(BF16), kt.astype(BF16), v_ref[0, p]
            dob = jnp.where(is_lat, do_ref[0, p], 0.0).astype(BF16)
            kt_exps = [_expand(_sub(ktb, s), mexp) for s in range(NSB)]
            uts = [_dot_tn(_sub(vb, s), kt_exps[s]) for s in range(NSB)]
            states, _ = _state_pass(st_ref[0, p, 0], eb, uts, rev)
            gts = [_dot_tn(_sub(dob, s), _expand(_sub(qb, s), mexp)) for s in range(NSB)]
            order = range(NCB) if rev else range(NCB - 1, -1, -1)
            dsp = [None] * NCB
            t2 = [None] * NCB
            dsc = ds_scr[p]
            for n in order:
                row = n * CH if rev else n * CH + CH - 1
                ebl = eb[row:row + 1, :]
                dsp[n] = dsc
                t2[n] = jnp.broadcast_to(ebl * _rsum(states[n] * dsc), (CH, HD))
                dsc = gts[n // CSB][:, (n % CSB) * HD:(n % CSB + 1) * HD] + ebl * dsc
            ds_scr[p] = dsc
            dqds, dkis, dkts = [], [], []
            for s in range(NSB):
                q_s, ki_s, v_s, do_s = _sub(qb, s), _sub(kib, s), _sub(vb, s), _sub(dob, s)
                dspb = _stack(dsp, s).astype(BF16)
                da = jnp.where(mask, _dot_nt(do_s, v_s), 0.0).astype(BF16)
                dat = jnp.where(maskT, _dot_nt(v_s, do_s), 0.0).astype(BF16)
                at = jnp.where(maskT, _dot_nt(ki_s, q_s), 0.0).astype(BF16)
                dqds.append(_dot(da, ki_s) + _own(_dot(do_s, _stack(states, s).astype(BF16)), mexp))
                dkis.append(_dot(dat, q_s))
                dv_ref[0, p, s * SB:(s + 1) * SB, :] = (_dot(at, do_s) + _dot_nt(kt_exps[s], dspb)).astype(BF16)
                dkts.append(_own(_dot(v_s, dspb), mexp))
            dqd, dki, dkt = (jnp.concatenate(parts, axis=0) for parts in (dqds, dkis, dkts))
            z = dkt * kt
            db = dqd * qd - dki * ki
            dq_ref[0, p] = (dqd * eb).astype(BF16)
            dk_ref[0, p] = (dki * jnp.exp(-b) + dkt * jnp.exp(xx)).astype(BF16)
            dg_ref[0, p] = (_chunk_total(db) + (db - z) + _chunk_scan(z - db, rin, not rev)
                            + jnp.concatenate(t2, axis=0)).astype(BF16)

    one = pl.BlockSpec((1, HPB, TM, HD), lambda b, h, t: (b, h, rmap(t), 0))
    two = pl.BlockSpec((1, 1, HPB, TM, HD), lambda b, h, t: (d, b, h, rmap(t), 0))
    lat = pl.BlockSpec((1, HPB, TM, HD), lambda b, h, t: (b, h, jnp.clip(rmap(t) - 1, 0, nb - 2), 0))
    shp = jax.ShapeDtypeStruct(Q.shape, BF16)
    return pl.pallas_call(
        body, name="scan_bwd_rev" if rev else "scan_bwd", grid=(B, NH // HPB, nb),
        in_specs=[one, two, one, two, pl.BlockSpec((1, HPB, 1, HD, HD), lambda b, h, t: (b, h, smap(t), 0, 0)), lat],
        out_specs=[one, one, one, one], out_shape=[shp, shp, shp, shp],
        scratch_shapes=[pltpu.VMEM((HPB, HD, HD), F32)],
        compiler_params=_cp(3))(Q, K, V, G, St, dO)


def _gnorm_mix_out(O0, O1, F, gains, W, B, ntl):
    nt = ntl + 1

    def body(o0_ref, o1_ref, f_ref, gn_ref, w_ref, m_ref, y_ref):
        for h in range(NH):
            o = o0_ref[0, h] + o1_ref[0, h]
            r = lax.rsqrt(jnp.mean(o * o, axis=-1, keepdims=True) + NORM_EPS)
            gn = gn_ref[0:1, :] if h < 4 else gn_ref[1:2, :]
            gt = f_ref[:, h * HD:(h + 1) * HD]
            m_ref[:, h * HD:(h + 1) * HD] = (o * r * gn * (gt * _sigmoid(gt))).astype(BF16)
        y_ref[...] = _dot(m_ref[...], w_ref[...]).astype(BF16)

    ospec = pl.BlockSpec((1, NH, TM, HD), lambda b, i: (b, 0, i + 1, 0))
    rows = pl.BlockSpec((TM, D), _rowmap(ntl, 0))
    shp = jax.ShapeDtypeStruct((B * ntl * TM, D), BF16)
    return pl.pallas_call(
        body, name="gated_norm_mix_out", grid=(B, ntl),
        in_specs=[ospec, ospec, pl.BlockSpec((TM, D), lambda b, i: (b * nt + 1 + i, 0)),
                  pl.BlockSpec((2, HD), lambda b, i: (0, 0)), _wspec(W, 2)],
        out_specs=[rows, rows], out_shape=[shp, shp], compiler_params=_cp(2))(O0, O1, F, gains, W)


def _mix_out_dx_gnorm_bwd(dY, W, O0, O1, F, gains, dep, B, ntl):
    nt = ntl + 1

    def body(dy_ref, w_ref, o0_ref, o1_ref, f_ref, gn_ref, dep_ref, do_ref, dgt_ref, dgn_ref):
        b_, i = pl.program_id(0), pl.program_id(1)

        @pl.when((b_ == 0) & (i == 0))
        def _():
            dgn_ref[...] = jnp.zeros_like(dgn_ref)

        dyv = dy_ref[...]
        for h in range(NH):
            o = o0_ref[0, h] + o1_ref[0, h]
            r = lax.rsqrt(jnp.mean(o * o, axis=-1, keepdims=True) + NORM_EPS)
            y = o * r
            gn = gn_ref[0:1, :] if h < 4 else gn_ref[1:2, :]
            gt = f_ref[:, h * HD:(h + 1) * HD]
            s = _sigmoid(gt)
            dm = _dot_nt(dyv, w_ref[h * HD:(h + 1) * HD, :])
            don = dm * (gt * s)
            dgt_ref[:, h * HD:(h + 1) * HD] = (dm * (y * gn) * (s * (1.0 + gt * (1.0 - s)))).astype(BF16)
            row = 0 if h < 4 else 1
            dgn_ref[row:row + 1, :] += _rsum(don * y)
            dy = don * gn
            do_ref[0, h] = (r * (dy - y * jnp.mean(dy * y, axis=-1, keepdims=True))).astype(BF16)

    ospec = pl.BlockSpec((1, NH, TM, HD), lambda b, i: (b, 0, i + 1, 0))
    return pl.pallas_call(
        body, name="mix_out_dx_gated_norm_bwd", grid=(B, ntl),
        in_specs=[pl.BlockSpec((TM, D), _rowmap(ntl, 0)), _wspec(W, 2), ospec, ospec,
                  pl.BlockSpec((TM, D), lambda b, i: (b * nt + 1 + i, 0)), pl.BlockSpec((2, HD), lambda b, i: (0, 0)),
                  _wspec(dep, 2)],
        out_specs=[pl.BlockSpec((1, NH, TM, HD), lambda b, i: (b, 0, i, 0)), pl.BlockSpec((TM, D), _rowmap(ntl, 0)),
                   pl.BlockSpec((2, HD), lambda b, i: (0, 0))],
        out_shape=[jax.ShapeDtypeStruct((B, NH, ntl * TM, HD), BF16), jax.ShapeDtypeStruct((B * ntl * TM, D), BF16),
                   jax.ShapeDtypeStruct((2, HD), F32)],
        compiler_params=_cp(2))(dY, W, O0, O1, F, gains, dep)


def _sincos_2d(rows, width, dim):
    quarter = dim // 4
    omega = 1.0 / 10000.0 ** (jnp.arange(quarter, dtype=F32) / quarter)

    def emb(n):
        a = jnp.arange(n).astype(F32)[:, None] * omega[None, :]
        return jnp.concatenate([jnp.sin(a), jnp.cos(a)], axis=-1)

    er = jnp.broadcast_to(emb(rows)[:, None, :], (rows, width, dim // 2))
    ec = jnp.broadcast_to(emb(width)[None, :, :], (rows, width, dim // 2))
    return jnp.concatenate([er, ec], axis=-1).reshape(rows * width, dim)


def _pad_heads(w):
    k = w.shape[0]
    return jnp.pad(w.reshape(k, 4, GLA_DK), ((0, 0), (0, 0), (0, HD - GLA_DK))).reshape(k, 4 * HD)


def _unpad_heads(w):
    k = w.shape[0]
    return w.reshape(k, 4, HD)[:, :, :GLA_DK].reshape(k, 4 * GLA_DK)


MIX_N = 1032
MIX_NP = 1152
_SEGS = ([(64 * h, 64, GQ + HD * h) for h in range(4)] + [(256 + 64 * h, 64, GK + HD * h) for h in range(4)]
         + [(512, 512, GV), (1024, 512, GG), (1536, 32, LR), (1568, 512, RQ), (2080, 512, RFF), (2592, 512, RFB),
            (3104, 512, RI), (3616, 512, RG)])


def _mix_in_to_padded(ps):
    k = ps.shape[1]
    parts, pos = [], 0
    for g0, ln, s0 in sorted(_SEGS, key=lambda s: s[2]):
        if s0 > pos:
            parts.append(jnp.zeros((k, s0 - pos), ps.dtype))
        for j in range(4):
            lo, hi = max(g0, j * MIX_N), min(g0 + ln, (j + 1) * MIX_N)
            if lo < hi:
                parts.append(ps[j][:, lo - j * MIX_N:hi - j * MIX_N])
        pos = s0 + ln
    parts.append(jnp.zeros((k, MIXP - pos), ps.dtype))
    return jnp.concatenate(parts, axis=1)


def _mix_in_from_padded(g):
    k = g.shape[0]
    shards = []
    for j in range(4):
        parts = []
        for g0, ln, s0 in sorted(_SEGS):
            lo, hi = max(g0, j * MIX_N), min(g0 + ln, (j + 1) * MIX_N)
            if lo < hi:
                parts.append(g[:, s0 + lo - g0:s0 + hi - g0])
        parts.append(jnp.zeros((k, MIX_NP - MIX_N), g.dtype))
        shards.append(jnp.concatenate(parts, axis=1))
    return jnp.stack(shards)


def _local_step(x, ctx, tgt, mvec, weights_for, prefetch, ln_gain, ln_bias, a2f, a2b, abf, abb, lb, gng, gnh, on_grads, on_sent):
    B, T, _ = x.shape
    assert ctx.shape[1] == TM and T % TM == 0
    ntl = T // TM
    nt = ntl + 1
    C, L, CL = (nt, 0), (ntl, 0), (nt, 1)
    pos = _sincos_2d(T // 64, 64, D)
    gbs = [jnp.stack([ln_gain[i], ln_bias[i]]) for i in range(3)]
    a2p = jnp.zeros((2, HD, 4 * HD), F32)
    a2p = a2p.at[0, 0:16].set(_pad_heads(a2f)).at[1, 16:32].set(_pad_heads(a2b)).astype(BF16)
    biasp = jnp.concatenate([_pad_heads(abf.reshape(1, -1)), _pad_heads(abb.reshape(1, -1))], axis=0)
    gains = jnp.concatenate([gng.reshape(1, HD), gnh.reshape(1, HD)], axis=0)

    X0, h0 = _embed_lnmod(x, ctx, pos, mvec)
    w1i, w1o = weights_for("ffn1", h0)
    u0, a0 = _ffn_in(h0, C, w1i, B, nt, "ffn1_in")
    tok = prefetch("mix", a0)
    y0 = _mm_nn(a0, C, w1o, B, nt, BF16, "ffn1_out")
    X1, h1 = _resid_lnmod(X0, C, y0, mvec, True, 0, 0.5, gbs[0] + tok[0, 0], B, nt, "resid0_lnmod1")
    wmp, wmo = weights_for("mix", h1)
    Q, K, V, G, Zm = _mix_in_features(h1, wmp, a2p, biasp, lb, B, nt)
    prefetch("ffn2", Zm)
    O0, S0 = _scan_fwd(Q, K, V, G, False, B)
    O1, S1 = _scan_fwd(Q, K, V, G, True, B)
    merged, y1 = _gnorm_mix_out(O0, O1, Zm, gains, wmo, B, ntl)
    X2, h2 = _resid_lnmod(X1, CL, y1, mvec, False, 1, 1.0, gbs[1], B, ntl, "resid1_lnmod2")
    w2i, w2o = weights_for("ffn2", h2)
    u2, a2 = _ffn_in(h2, L, w2i, B, ntl, "ffn2_in")
    y2 = _mm_nn(a2, L, w2o, B, ntl, BF16, "ffn2_out")

    dx2r, dy2, dgate2, dgb2, lsum = _tail(X2, y2, mvec, gbs[2], tgt, 2, 0.5, B, ntl)
    loss = (0.5 / D) * jnp.sum(lsum)
    du2 = _ffn_out_dx(dy2, w2o, u2, "ffn2_out_dx")
    g_w2o = _mm_tn(a2, dy2, "ffn2_out_dw")
    dh2 = _mm_nt(du2, w2i, "ffn2_in_dx", out_dtype=BF16)
    g_w2i = _mm_tn(h2, du2, "ffn2_in_dw", shards=4)
    tok = on_grads("ffn2", (g_w2i, g_w2o))
    dx1r, dy1, dgate1, dgb1, dss2 = _lnmod_resid_bwd(dh2, X2, L, dx2r, X1, CL, y1, mvec, False, 2, 1.0,
                                                     gbs[1] + tok[0, 0], B, ntl, "lnmod2_resid1_bwd")
    tok = on_sent("ffn2", dy1)
    g_wmo = _mm_tn(merged, dy1, "mix_out_dw", tn=D)
    dO, dgates, dgains = _mix_out_dx_gnorm_bwd(dy1, wmo, O0, O1, Zm, gains, tok, B, ntl)
    dQ0, dK0, dV0, dG0 = _scan_bwd(Q, K, V, G, S0, dO, False, B)
    dQ1, dK1, dV1, dG1 = _scan_bwd(Q, K, V, G, S1, dO, True, B)
    dF, dh1, da2p, dbiasp, dlb = _features_bwd(Zm, a2p, biasp, lb, dQ0, dQ1, dK0, dK1, dV0, dV1, dG0, dG1, dgates,
                                               wmp, B)
    g_wmp = _mm_tn(h1, dF, "mix_in_dw", tn=MIXP // 4)
    tok = on_grads("mix", (g_wmp, g_wmo))
    dx0r, dy0, dgate0, dgb0, dss1 = _lnmod_resid_bwd(dh1, X1, C, dx1r, X0, C, y0, mvec, True, 1, 0.5,
                                                     gbs[0] + tok[0, 0], B, nt, "lnmod1_resid0_bwd")
    tok = on_sent("mix", dy0)
    du0 = _ffn_out_dx(dy0, w1o, u0, "ffn1_out_dx", dep=tok)
    g_w1o = _mm_tn(a0, dy0, "ffn1_out_dw")
    g_w1i = _mm_tn(h0, du0, "ffn1_in_dw", shards=4)
    tok = on_grads("ffn1", (g_w1i, g_w1o))
    dh0 = _mm_nt(du0, w1i, "ffn1_in_dx", dep=tok, out_dtype=BF16)
    tok = on_sent("ffn1", dh0)
    grad_x, dss0 = _lnmod0_bwd(dh0, X0, mvec + tok[0, 0], dx0r, x.shape, B, nt)

    zero_ctx = lambda a: a.at[:, 0].set(0.0)
    dm = jnp.concatenate([dss0, dgate0, dss1, zero_ctx(dgate1), zero_ctx(dss2), zero_ctx(dgate2)], axis=2)
    small = dict(
        ln_gain=jnp.stack([dgb0[0], dgb1[0], dgb2[0]]), ln_bias=jnp.stack([dgb0[1], dgb1[1], dgb2[1]]),
        a2f=_unpad_heads(da2p[0, 0:16]), a2b=_unpad_heads(da2p[1, 16:32]),
        abf=_unpad_heads(dbiasp[0:1]), abb=_unpad_heads(dbiasp[1:2]), lb=dlb, gng=dgains[0], gnh=dgains[1])
    return loss, grad_x, dm, small


def _small_allgather(xs, name):
    r, n = xs.shape

    def body(x_ref, out_ref, send_sems, recv_sems, local_sem):
        x, y, c = lax.axis_index("x"), lax.axis_index("y"), lax.axis_index("c")
        me, sibling = (x, y, c), (x, y, 1 - c)
        chips = [(1 - x, y), (x, 1 - y), (1 - x, 1 - y)]

        def rows(px, py, pc):
            return out_ref.at[pl.ds((4 * px + 2 * py + pc) * r, r), :]

        def copy(k, block, to, src=None):
            return pltpu.make_async_remote_copy(
                src_ref=rows(*block) if src is None else src, dst_ref=rows(*block),
                send_sem=send_sems.at[k], recv_sem=recv_sems.at[k], device_id=to, device_id_type=MESH)

        mine = pltpu.make_async_copy(x_ref, rows(*me), local_sem)
        mine.start()
        first = [copy(0, me, sibling, src=x_ref)]
        first += [copy(1 + j, me, (*chip, c), src=x_ref) for j, chip in enumerate(chips)]
        for cp in first:
            cp.start()
        passed = [copy(4 + j, (*chip, c), sibling) for j, chip in enumerate(chips)]
        for j, chip in enumerate(chips):
            copy(1 + j, (*chip, c), me).wait_recv()
            passed[j].start()
        copy(0, sibling, me).wait_recv()
        for j, chip in enumerate(chips):
            copy(4 + j, (*chip, 1 - c), me).wait_recv()
        for cp in first + passed:
            cp.wait_send()
        mine.wait()

    out = pl.pallas_call(
        body, name=name,
        out_shape=jax.ShapeDtypeStruct((8 * r, n), xs.dtype),
        in_specs=[pl.BlockSpec(memory_space=pltpu.VMEM)],
        out_specs=pl.BlockSpec(memory_space=pltpu.VMEM),
        scratch_shapes=[pltpu.SemaphoreType.DMA((7,)), pltpu.SemaphoreType.DMA((7,)), pltpu.SemaphoreType.DMA],
        compiler_params=pltpu.CompilerParams(vmem_limit_bytes=VMEM_LIMIT))(xs)
    return out.reshape(8, r, n)


def _gather_flat(v, name):
    n = v.shape[0]
    npad = -(-n // 1024) * 1024
    g = _small_allgather(jnp.pad(v, (0, npad - n)).reshape(8, npad // 8), name)
    return g.reshape(8, npad)[:, :n]


HBM_SPEC = pl.BlockSpec(memory_space=pltpu.HBM)
SEM_SPEC = pl.BlockSpec(memory_space=pltpu.SEMAPHORE)
DATAFLOW = pltpu.SideEffectType.DATAFLOW_SIDE_EFFECTING


def _gather_copies(xs, outs, send_sems, recv_sems):
    x, y, c = lax.axis_index("x"), lax.axis_index("y"), lax.axis_index("c")
    dests = [(x, y, 1 - c), (1 - x, y, c), (x, 1 - y, c), (1 - x, 1 - y, c)]
    return [pltpu.make_async_remote_copy(
        src_ref=xs[w], dst_ref=outs[w].at[4 * x + 2 * y + c], send_sem=send_sems[4 * w + k],
        recv_sem=recv_sems[4 * w + k], device_id=dests[k], device_id_type=MESH)
        for w in range(len(xs)) for k in range(4)]


def _split_start(copies, per_w, srcs, land_lead, after, name):
    n = len(srcs)
    m = per_w * n
    lands = [lax.empty((land_lead,) + s.shape[-2:], s.dtype) for s in srcs]
    deps = [] if after is None else [after]

    def body(*refs):
        xs, ls, outs = refs[:n], refs[n:2 * n], refs[2 * n + len(deps):]
        for cp in copies(xs, ls, outs[:m], outs[m:2 * m]):
            cp.start()
        token = outs[2 * m + 2 * n]
        token[...] = jnp.zeros_like(token)

    outs = pl.pallas_call(
        body, name=name,
        out_shape=([pltpu.SemaphoreType.DMA(())] * (2 * m) + [pltpu.HBM(a.shape, a.dtype) for a in srcs + lands]
                   + [jax.ShapeDtypeStruct((8, 128), F32)]),
        in_specs=[HBM_SPEC] * (2 * n) + [pl.BlockSpec(memory_space=pl.ANY)] * len(deps),
        out_specs=[SEM_SPEC] * (2 * m) + [HBM_SPEC] * (2 * n) + [pl.BlockSpec(memory_space=pltpu.VMEM)],
        input_output_aliases={w: 2 * m + w for w in range(2 * n)},
        compiler_params=pltpu.CompilerParams(has_side_effects=DATAFLOW),
    )(*[pltpu.with_memory_space_constraint(a, pltpu.HBM) for a in srcs + lands], *deps)
    return outs[:2 * m], outs[2 * m:2 * m + n], outs[2 * m + n:2 * m + 2 * n], outs[2 * m + 2 * n]


def _split_wait(copies, per_w, sems, x_thru, l_thru, after, name):
    n = len(x_thru)
    m = per_w * n

    def body(*refs):
        xs, ls, ss = refs[:n], refs[n:2 * n], refs[2 * n:2 * n + 2 * m]
        for cp in copies(xs, ls, ss[:m], ss[m:]):
            cp.wait_send()
            cp.wait_recv()

    outs = pl.pallas_call(
        body, name=name,
        out_shape=[pltpu.HBM(a.shape, a.dtype) for a in list(x_thru) + list(l_thru)],
        in_specs=[HBM_SPEC] * (2 * n) + [SEM_SPEC] * (2 * m) + [pl.BlockSpec(memory_space=pl.ANY)],
        out_specs=[HBM_SPEC] * (2 * n),
        input_output_aliases={w: w for w in range(2 * n)},
        compiler_params=pltpu.CompilerParams(has_side_effects=DATAFLOW),
    )(*x_thru, *l_thru, *sems, after)
    return outs[:n], outs[n:]


def _gather_forward(gathered, name):
    n = len(gathered)

    def body(*refs):
        outs = refs[n:2 * n]
        send_sems, recv_sems = refs[2 * n:]
        x, y, c = lax.axis_index("x"), lax.axis_index("y"), lax.axis_index("c")
        chips = [(1 - x, y), (x, 1 - y), (1 - x, 1 - y), (x, y)]

        def copy(w, j, pc):
            px, py = chips[j]
            slot = outs[w].at[4 * px + 2 * py + pc]
            return pltpu.make_async_remote_copy(
                src_ref=slot, dst_ref=slot, send_sem=send_sems.at[4 * w + j], recv_sem=recv_sems.at[4 * w + j],
                device_id=(x, y, 1 - c), device_id_type=MESH)

        sends = [copy(w, j, 1 - c if j == 3 else c) for w in range(n) for j in range(4)]
        for cp in sends:
            cp.start()
        for w in range(n):
            for j in range(4):
                copy(w, j, c if j == 3 else 1 - c).wait_recv()
        for cp in sends:
            cp.wait_send()

    any_spec = pl.BlockSpec(memory_space=pl.ANY)
    return pl.pallas_call(
        body, name=name,
        out_shape=[jax.ShapeDtypeStruct(g.shape, g.dtype) for g in gathered],
        in_specs=[any_spec] * n, out_specs=[any_spec] * n,
        input_output_aliases={w: w for w in range(n)},
        scratch_shapes=[pltpu.SemaphoreType.DMA((4 * n,)), pltpu.SemaphoreType.DMA((4 * n,))],
    )(*gathered)


def _forward_copies(gs, ls, send_sems, recv_sems):
    x, y, c = lax.axis_index("x"), lax.axis_index("y"), lax.axis_index("c")
    chips = [(1 - x, y), (x, 1 - y), (1 - x, 1 - y), (x, y)]
    cps = []
    for w in range(len(gs)):
        for j, (px, py) in enumerate(chips):
            slot = gs[w].at[4 * px + 2 * py + (1 - c if j == 3 else c)]
            cps.append(pltpu.make_async_remote_copy(
                src_ref=slot, dst_ref=slot, send_sem=send_sems[4 * w + j], recv_sem=recv_sems[4 * w + j],
                device_id=(x, y, 1 - c), device_id_type=MESH))
    return cps


def _pair_copies(gs, ls, send_sems, recv_sems):
    x, y, c = lax.axis_index("x"), lax.axis_index("y"), lax.axis_index("c")
    return [pltpu.make_async_remote_copy(
        src_ref=gs[w].at[2 * j + 1 - c], dst_ref=ls[w].at[j], send_sem=send_sems[4 * w + j],
        recv_sem=recv_sems[4 * w + j], device_id=(x, y, 1 - c), device_id_type=MESH)
        for w in range(len(gs)) for j in range(4)]


def _all_copies(xs, ls, send_sems, recv_sems):
    x, y, c = lax.axis_index("x"), lax.axis_index("y"), lax.axis_index("c")
    flips = [(a, b, e) for a in (0, 1) for b in (0, 1) for e in (0, 1)][1:]
    return [pltpu.make_async_remote_copy(
        src_ref=xs[0], dst_ref=ls[0].at[4 * x + 2 * y + c], send_sem=send_sems[k], recv_sem=recv_sems[k],
        device_id=(1 - x if a else x, 1 - y if b else y, 1 - c if e else c), device_id_type=MESH)
        for k, (a, b, e) in enumerate(flips)]


def _share_copies(fs, ls, send_sems, recv_sems):
    x, y, c = lax.axis_index("x"), lax.axis_index("y"), lax.axis_index("c")
    return [pltpu.make_async_remote_copy(
        src_ref=fs[w].at[c], dst_ref=fs[w].at[c], send_sem=send_sems[w], recv_sem=recv_sems[w],
        device_id=(x, y, 1 - c), device_id_type=MESH) for w in range(len(fs))]


def _chip_copies(hs, ls, send_sems, recv_sems):
    x, y, c = lax.axis_index("x"), lax.axis_index("y"), lax.axis_index("c")
    chips = [(1 - x, y), (x, 1 - y), (1 - x, 1 - y)]
    return [pltpu.make_async_remote_copy(
        src_ref=hs[w].at[2 * px + py], dst_ref=ls[w].at[k], send_sem=send_sems[3 * w + k],
        recv_sem=recv_sems[3 * w + k], device_id=(px, py, c), device_id_type=MESH)
        for w in range(len(hs)) for k, (px, py) in enumerate(chips)]


def _rs_add_pair(g8, r4, c, name):
    R, n = g8.shape[1:]
    rb = R // 2

    def body(c_ref, g_ref, r_ref, o_ref):
        o_ref[...] = (g_ref[...] + r_ref[...]).astype(BF16)

    spec = pl.BlockSpec((1, rb, n), lambda j, i, c_ref: (j, i, 0))
    return pl.pallas_call(
        body, name=name,
        grid_spec=pltpu.PrefetchScalarGridSpec(
            num_scalar_prefetch=1, grid=(4, R // rb),
            in_specs=[pl.BlockSpec((1, rb, n), lambda j, i, c_ref: (2 * j + c_ref[0], i, 0)), spec],
            out_specs=spec),
        out_shape=jax.ShapeDtypeStruct((4, R, n), BF16), compiler_params=_cp(2))(c, g8, r4)


def _rs_add_chips(g8, r4, r3, cj, name):
    R, n = g8.shape[1:]
    rb = R // 2

    def body(cj_ref, g_ref, p_ref, r_ref, o_ref):
        own = g_ref[0] + p_ref[0]
        o_ref[0] = ((own + r_ref[0].astype(F32)) + r_ref[1].astype(F32)) + r_ref[2].astype(F32)

    return pl.pallas_call(
        body, name=name,
        grid_spec=pltpu.PrefetchScalarGridSpec(
            num_scalar_prefetch=1, grid=(R // rb,),
            in_specs=[pl.BlockSpec((1, rb, n), lambda i, cj_ref: (2 * cj_ref[1] + cj_ref[0], i, 0)),
                      pl.BlockSpec((1, rb, n), lambda i, cj_ref: (cj_ref[1], i, 0)),
                      pl.BlockSpec((3, rb, n), lambda i, cj_ref: (0, i, 0))],
            out_specs=pl.BlockSpec((1, rb, n), lambda i, cj_ref: (cj_ref[0], i, 0))),
        out_shape=jax.ShapeDtypeStruct((2, R, n), F32), compiler_params=_cp(1))(cj, g8, r4, r3)


def _sum8(g):
    n = g.shape[1]

    def body(g_ref, o_ref):
        acc = g_ref[0:1, :]
        for k in range(1, 8):
            acc = acc + g_ref[k:k + 1, :]
        o_ref[...] = acc

    return pl.pallas_call(body, name="sum_devices", out_shape=jax.ShapeDtypeStruct((1, n), F32),
                          compiler_params=pltpu.CompilerParams(vmem_limit_bytes=VMEM_LIMIT))(g)


ADA_ROWS = 64


def _ada_fwd(cs, w, b):
    n = w.shape[1]

    def body(c_ref, w_ref, b_ref, o_ref):
        cv = c_ref[...]
        s = (cv * _sigmoid(cv)).astype(BF16)
        o_ref[...] = _dot(s, w_ref[...].astype(BF16)) + b_ref[...]

    return pl.pallas_call(body, name="ada_fwd", out_shape=jax.ShapeDtypeStruct((ADA_ROWS, n), F32),
                          compiler_params=pltpu.CompilerParams(vmem_limit_bytes=VMEM_LIMIT))(cs, w, b)


def _mod_exchange(m_cols):
    n = m_cols.shape[1]

    def body(m_ref, out_ref, send_sems, recv_sems, local_sem):
        x, y, c = lax.axis_index("x"), lax.axis_index("y"), lax.axis_index("c")
        chips = [(1 - x, y), (x, 1 - y), (1 - x, 1 - y)]

        def group(px, py):
            return m_ref.at[pl.ds(pl.multiple_of(8 * (4 * px + 2 * py + c), 8), 8), :]

        mine = pltpu.make_async_copy(group(x, y), out_ref.at[2 * x + y], local_sem)
        mine.start()
        cps = [pltpu.make_async_remote_copy(
            src_ref=group(px, py), dst_ref=out_ref.at[2 * x + y], send_sem=send_sems.at[k], recv_sem=recv_sems.at[k],
            device_id=(px, py, c), device_id_type=MESH) for k, (px, py) in enumerate(chips)]
        for cp in cps:
            cp.start()
        for cp in cps:
            cp.wait_recv()
        for cp in cps:
            cp.wait_send()
        mine.wait()

    return pl.pallas_call(
        body, name="mod_exchange", out_shape=jax.ShapeDtypeStruct((4, 8, n), F32),
        in_specs=[pl.BlockSpec(memory_space=pltpu.VMEM)], out_specs=pl.BlockSpec(memory_space=pltpu.VMEM),
        scratch_shapes=[pltpu.SemaphoreType.DMA((3,)), pltpu.SemaphoreType.DMA((3,)), pltpu.SemaphoreType.DMA],
        compiler_params=pltpu.CompilerParams(vmem_limit_bytes=VMEM_LIMIT))(m_cols)


def _ada_bwd(cs, w, dm):
    n = w.shape[1]

    def body(c_ref, w_ref, dm_ref, gw_ref, dc_ref):
        cv = c_ref[...]
        s = (cv * _sigmoid(cv)).astype(BF16)
        gw_ref[...] = _dot_tn(s, dm_ref[...].astype(BF16))
        dc_ref[...] = _dot_nt(dm_ref[0:8, :].astype(BF16), w_ref[...].astype(BF16))

    return pl.pallas_call(
        body, name="ada_bwd",
        out_shape=[jax.ShapeDtypeStruct((D, n), F32), jax.ShapeDtypeStruct((8, D), F32)],
        compiler_params=pltpu.CompilerParams(vmem_limit_bytes=VMEM_LIMIT))(cs, w, dm)


def _adamw(w, g, m, v, name):
    r, c = w.shape
    rb = r
    if r % 8 == 0 and r * c * 4 > (1 << 20):
        rb = 8
        for cand in range(8, r, 8):
            if r % cand == 0 and cand * c * 4 <= (1 << 20):
                rb = cand

    def body(w_ref, g_ref, m_ref, v_ref, go_ref, d_ref, nm_ref, nv_ref):
        gv = g_ref[...]
        go_ref[...] = gv
        mn = ADAM_B1 * m_ref[...] + (1.0 - ADAM_B1) * gv
        vn = ADAM_B2 * v_ref[...] + (1.0 - ADAM_B2) * (gv * gv)
        m_hat = mn / (1.0 - ADAM_B1 ** ADAM_STEP)
        v_hat = vn / (1.0 - ADAM_B2 ** ADAM_STEP)
        d_ref[...] = -ADAM_LR * (m_hat / (jnp.sqrt(v_hat) + ADAM_EPS) + ADAM_WD * w_ref[...])
        nm_ref[...] = mn
        nv_ref[...] = vn

    spec = pl.BlockSpec((rb, c), lambda i: (i, 0))
    shp = jax.ShapeDtypeStruct((r, c), F32)
    return pl.pallas_call(body, name=name, grid=(r // rb,), in_specs=[spec] * 4, out_specs=[spec] * 4,
                          out_shape=[shp] * 4, compiler_params=_cp(1))(w, g, m, v)


BIG = ("ffn1_w_in", "ffn1_w_out", "w_mix_in", "w_mix_out", "ffn2_w_in", "ffn2_w_out")


def _half_rows(w, c):
    half = w.shape[0] // 2
    return lax.dynamic_slice_in_dim(w, c * half, half, axis=0)


def _lower_bounds(logits):
    return jnp.cumsum(jax.nn.softmax(logits.astype(F32), axis=1), axis=1)[:, 0]


def kernel(x, c, ctx, c_ctx, w_ada, b_ada, ln_gain, ln_bias, ffn1_w_in, ffn1_w_out, w_mix_in, gla_a2_fwd, gla_a2_bwd, gla_a_bias_fwd, gla_a_bias_bwd, hgrn_lb_logits, gla_norm_gain, hgrn_norm_gain, w_mix_out, ffn2_w_in, ffn2_w_out, loss_target, m_c_ctx, m_w_ada, m_b_ada, m_ln_gain, m_ln_bias, m_ffn1_w_in, m_ffn1_w_out, m_w_mix_in, m_gla_a2_fwd, m_gla_a2_bwd, m_gla_a_bias_fwd, m_gla_a_bias_bwd, m_hgrn_lb_logits, m_gla_norm_gain, m_hgrn_norm_gain, m_w_mix_out, m_ffn2_w_in, m_ffn2_w_out, v_c_ctx, v_w_ada, v_b_ada, v_ln_gain, v_ln_bias, v_ffn1_w_in, v_ffn1_w_out, v_w_mix_in, v_gla_a2_fwd, v_gla_a2_bwd, v_gla_a_bias_fwd, v_gla_a_bias_bwd, v_hgrn_lb_logits, v_gla_norm_gain, v_hgrn_norm_gain, v_w_mix_out, v_ffn2_w_in, v_ffn2_w_out):
    xi, yi, ci = lax.axis_index("x"), lax.axis_index("y"), lax.axis_index("c")
    chip = 2 * xi + yi
    dev = 2 * chip + ci
    B = x.shape[0]
    weights = dict(ffn1_w_in=ffn1_w_in[0], ffn1_w_out=ffn1_w_out[0], w_mix_in=w_mix_in[0], w_mix_out=w_mix_out[0],
                   ffn2_w_in=ffn2_w_in[0], ffn2_w_out=ffn2_w_out[0])

    mine = jnp.concatenate([c.reshape(-1), ln_gain.reshape(-1), ln_bias.reshape(-1), gla_a2_fwd.reshape(-1),
                            gla_a2_bwd.reshape(-1), hgrn_lb_logits.reshape(-1)])
    g1 = _gather_flat(mine, "gather_cond")
    nc = B * D
    c_all = g1[:, :nc].reshape(8 * B, D)
    per_chip = g1[0::2, nc:]
    o = 0

    def take(shape, axis):
        nonlocal o
        n = int(np.prod(shape))
        parts = per_chip[:, o:o + n].reshape((4,) + shape)
        o += n
        return jnp.concatenate([parts[j] for j in range(4)], axis=axis)

    ln_gain_f = take((3, 256), 1)
    ln_bias_f = take((3, 256), 1)
    a2f_f = take((16, 64), 1)
    a2b_f = take((16, 64), 1)
    lbl_f = take((2, 2, 128), 2)
    lb, lb_vjp = jax.vjp(_lower_bounds, lbl_f)

    assert B + 1 <= 8
    cs = jnp.concatenate([c_all.reshape(8, B, D), jnp.broadcast_to(c_ctx.reshape(1, 1, D), (8, 1, D)),
                          jnp.zeros((8, 7 - B, D), F32)], axis=1).reshape(ADA_ROWS, D)
    ncol = w_ada.shape[2]
    b_cols = lax.dynamic_slice_in_dim(b_ada, chip * ncol, ncol, axis=1)
    m4 = _mod_exchange(_ada_fwd(cs, w_ada[0], b_cols))
    m_all = jnp.concatenate([m4[j] for j in range(4)], axis=1)
    m_lat = m_all[:B].reshape(B, 1, N_MOD, D)
    m_ctx = jnp.broadcast_to(m_all[B].reshape(1, 1, N_MOD, D), (B, 1, N_MOD, D))

    groups = dict(ffn2=("ffn2_w_in", "ffn2_w_out"), mix=("w_mix_in", "w_mix_out"), ffn1=("ffn1_w_in", "ffn1_w_out"))
    shards = dict(weights, w_mix_in=jnp.pad(weights["w_mix_in"], ((0, 0), (0, MIX_NP - MIX_N))))
    blks = {k: _half_rows(shards[k], ci).astype(BF16) for k in BIG}
    gathering = {}
    token = m_all
    for group in ("ffn1", "mix", "ffn2"):
        sems, x_thru, l_thru, token = _split_start(_gather_copies, 4, [blks[k] for k in groups[group]], 8, token,
                                                   "weight_gather_start_" + group)
        gathering[group] = (sems, x_thru, l_thru)
    mvec = jnp.concatenate([m_ctx, m_lat], axis=1) + token[0, 0]

    forwarding = {}

    def prefetch(group, after):
        _, got = _split_wait(_gather_copies, 4, *gathering[group], after, "weight_gather_wait_" + group)
        forwarding[group] = _split_start(_forward_copies, 4, list(got), 1, None, "weight_gather_forward_start_" + group)
        return forwarding[group][3]

    def weights_for(group, after):
        names = groups[group]
        if group in forwarding:
            sems, g_thru, l_thru, _ = forwarding[group]
            (w_in, w_out), _ = _split_wait(_forward_copies, 4, sems, g_thru, l_thru, after,
                                           "weight_gather_forward_wait_" + group)
        else:
            _, got = _split_wait(_gather_copies, 4, *gathering[group], after, "weight_gather_wait_" + group)
            w_in, w_out = _gather_forward(got, "weight_gather_forward_" + group)
        if group == "mix":
            return _mix_in_to_padded(w_in.reshape(4, D, MIX_NP)), w_out.reshape(-1, D)
        return w_in.reshape((4,) + shards[names[0]].shape), w_out.reshape(-1, D)

    cvec = ci.reshape(1).astype(jnp.int32)
    cjvec = jnp.stack([ci, chip]).astype(jnp.int32)
    in_flight = {}

    def on_grads(group, gs):
        names = groups[group]
        if group == "mix":
            gs = (_mix_in_from_padded(gs[0]), gs[1])
        g8s = [g.reshape((8, shards[k].shape[0] // 2, shards[k].shape[1])) for k, g in zip(names, gs)]
        sems, g_thru, l_thru, token = _split_start(_pair_copies, 4, g8s, 4, None, "grad_pair_start_" + group)
        in_flight[group] = (sems, g_thru, l_thru)
        return token

    def on_sent(group, after):
        g8s, r4s = _split_wait(_pair_copies, 4, *in_flight[group], after, "grad_pair_wait_" + group)
        h4s = [_rs_add_pair(g, r, cvec, "grad_pair_add_" + k) for k, g, r in zip(groups[group], g8s, r4s)]
        sems, h_thru, l_thru, token = _split_start(_chip_copies, 3, h4s, 3, None, "grad_chip_start_" + group)
        in_flight[group] = (g8s, r4s, sems, h_thru, l_thru)
        return token

    loss_l, grad_x, dm, small = _local_step(
        x, ctx, loss_target, mvec, weights_for, prefetch, ln_gain_f, ln_bias_f, a2f_f, a2b_f,
        gla_a_bias_fwd, gla_a_bias_bwd, lb, gla_norm_gain, hgrn_norm_gain, on_grads, on_sent)
    loss = lax.psum(loss_l, ("x", "y", "c"))

    dm_lat = dm[:, 1].reshape(B, N_MOD * D)
    dm_ctx = jnp.sum(dm[:, 0], axis=0).reshape(N_MOD * D)
    keys = ("ln_gain", "ln_bias", "a2f", "a2b", "abf", "abb", "lb", "gng", "gnh")
    flat = jnp.concatenate([dm_lat.reshape(-1), dm_ctx] + [small[k].reshape(-1) for k in keys])
    nflat = flat.shape[0]
    npad = -(-nflat // 1024) * 1024
    flat2 = jnp.pad(flat, (0, npad - nflat)).reshape(8, npad // 8)
    s_sems, s_src, s_land, tok = _split_start(_all_copies, 7, [flat2], 8, None, "small_grads_start")

    fin = {}
    for group in ("ffn2", "mix", "ffn1"):
        g8s, r4s, sems, h_thru, l_thru = in_flight[group]
        _, r3s = _split_wait(_chip_copies, 3, sems, h_thru, l_thru, tok, "grad_chip_wait_" + group)
        for k, g, r4, r3 in zip(groups[group], g8s, r4s, r3s):
            fin[k] = _rs_add_chips(g, r4, r3, cjvec, "grad_chip_add_" + k)
    p_sems, p_src, p_land, tok = _split_start(_share_copies, 1, [fin[k] for k in BIG], 1, None, "grad_pair_share_start")

    _, (land,) = _split_wait(_all_copies, 7, s_sems, s_src, s_land, tok, "small_grads_wait")
    g3 = lax.dynamic_update_index_in_dim(land, flat2, dev, 0).reshape(8, npad)[:, :nflat]
    nlat = B * N_MOD * D
    dm_all = g3[:, :nlat].reshape(8 * B, N_MOD * D)
    tot = _sum8(g3[:, nlat:])[0]
    dmc_tot = tot[:N_MOD * D]
    o = N_MOD * D
    sg = {}
    for k in keys:
        n = int(np.prod(small[k].shape))
        sg[k] = tot[o:o + n].reshape(small[k].shape)
        o += n
    ctx_rows = jnp.zeros((8, 1, N_MOD * D), F32).at[0, 0].set(dmc_tot)
    dm_rows = jnp.concatenate([dm_all.reshape(8, B, -1), ctx_rows, jnp.zeros((8, 7 - B, N_MOD * D), F32)],
                              axis=1).reshape(ADA_ROWS, N_MOD * D)
    g_b_ada = (jnp.sum(dm_all, axis=0) + dmc_tot).reshape(1, N_MOD * D)
    g_w_ada, dcc = _ada_bwd(cs, w_ada[0], lax.dynamic_slice_in_dim(dm_rows, chip * ncol, ncol, axis=1))
    g4 = _gather_flat(dcc[B], "gather_cctx")
    dsilu = ((g4[0] + g4[2]) + g4[4]) + g4[6]
    sc = _sigmoid(c_ctx)
    g_c_ctx = dsilu * (sc * (1.0 + c_ctx * (1.0 - sc)))
    (g_lbl,) = lb_vjp(sg["lb"])

    def cols(a, n, axis):
        return lax.dynamic_slice_in_dim(a, chip * n, n, axis=axis)

    shared, _ = _split_wait(_share_copies, 1, p_sems, p_src, p_land, g_w_ada, "grad_pair_share_wait")
    gsh = {k: both.reshape(shards[k].shape)[:, :weights[k].shape[1]] for k, both in zip(BIG, shared)}

    grads = dict(
        c_ctx=g_c_ctx, w_ada=g_w_ada[None], b_ada=g_b_ada, ln_gain=cols(sg["ln_gain"], 256, 1)[None],
        ln_bias=cols(sg["ln_bias"], 256, 1)[None], ffn1_w_in=gsh["ffn1_w_in"][None], ffn1_w_out=gsh["ffn1_w_out"][None],
        w_mix_in=gsh["w_mix_in"][None], gla_a2_fwd=cols(sg["a2f"], 64, 1)[None], gla_a2_bwd=cols(sg["a2b"], 64, 1)[None],
        gla_a_bias_fwd=sg["abf"], gla_a_bias_bwd=sg["abb"], hgrn_lb_logits=cols(g_lbl, 128, 2),
        gla_norm_gain=sg["gng"].reshape(1, HD), hgrn_norm_gain=sg["gnh"].reshape(1, HD),
        w_mix_out=gsh["w_mix_out"][None], ffn2_w_in=gsh["ffn2_w_in"][None], ffn2_w_out=gsh["ffn2_w_out"][None])
    params = dict(
        c_ctx=(c_ctx, m_c_ctx, v_c_ctx), w_ada=(w_ada, m_w_ada, v_w_ada), b_ada=(b_ada, m_b_ada, v_b_ada),
        ln_gain=(ln_gain, m_ln_gain, v_ln_gain), ln_bias=(ln_bias, m_ln_bias, v_ln_bias),
        ffn1_w_in=(ffn1_w_in, m_ffn1_w_in, v_ffn1_w_in), ffn1_w_out=(ffn1_w_out, m_ffn1_w_out, v_ffn1_w_out),
        w_mix_in=(w_mix_in, m_w_mix_in, v_w_mix_in), gla_a2_fwd=(gla_a2_fwd, m_gla_a2_fwd, v_gla_a2_fwd),
        gla_a2_bwd=(gla_a2_bwd, m_gla_a2_bwd, v_gla_a2_bwd),
        gla_a_bias_fwd=(gla_a_bias_fwd, m_gla_a_bias_fwd, v_gla_a_bias_fwd),
        gla_a_bias_bwd=(gla_a_bias_bwd, m_gla_a_bias_bwd, v_gla_a_bias_bwd),
        hgrn_lb_logits=(hgrn_lb_logits, m_hgrn_lb_logits, v_hgrn_lb_logits),
        gla_norm_gain=(gla_norm_gain, m_gla_norm_gain, v_gla_norm_gain),
        hgrn_norm_gain=(hgrn_norm_gain, m_hgrn_norm_gain, v_hgrn_norm_gain),
        w_mix_out=(w_mix_out, m_w_mix_out, v_w_mix_out), ffn2_w_in=(ffn2_w_in, m_ffn2_w_in, v_ffn2_w_in),
        ffn2_w_out=(ffn2_w_out, m_ffn2_w_out, v_ffn2_w_out))
    order = list(params.keys())
    big_names = ("w_ada",) + BIG
    upd = {}
    for k in big_names:
        w_, m_, v_ = params[k]
        s2 = w_.shape[-2:]
        g_, d_, nm_, nv_ = _adamw(w_.reshape(s2), grads[k].reshape(s2), m_.reshape(s2), v_.reshape(s2), "adamw_" + k)
        grads[k] = g_
        upd[k] = (d_.reshape(w_.shape), nm_.reshape(w_.shape), nv_.reshape(w_.shape))
    small_names = [k for k in order if k not in big_names]
    sizes = [int(np.prod(params[k][0].shape)) for k in small_names]
    tot_n = sum(sizes)
    npad = -(-tot_n // 1024) * 1024

    def packed(get):
        flat_ = jnp.concatenate([get(k).reshape(-1) for k in small_names])
        return jnp.pad(flat_, (0, npad - tot_n)).reshape(8, npad // 8)

    _, d_s, nm_s, nv_s = _adamw(packed(lambda k: params[k][0]), packed(lambda k: grads[k]),
                                packed(lambda k: params[k][1]), packed(lambda k: params[k][2]), "adamw_small")
    o = 0
    for k, n in zip(small_names, sizes):
        shp = params[k][0].shape
        upd[k] = tuple(a.reshape(-1)[o:o + n].reshape(shp) for a in (d_s, nm_s, nv_s))
        o += n

    return (loss, grad_x, *[grads[k].reshape(params[k][0].shape) for k in order], *[upd[k][0] for k in order],
            *[upd[k][1] for k in order], *[upd[k][2] for k in order])
```

```python
import functools

import numpy as np
import jax
import jax.numpy as jnp
from jax import lax
from jax.experimental import pallas as pl
from jax.experimental.pallas import tpu as pltpu

F32 = jnp.float32
BF16 = jnp.bfloat16
MESH = pl.DeviceIdType.MESH

D = 1024
DFF = 2816
TM = 256
CH = 32
NCB = TM // CH
SB = 128
CSB = SB // CH
NSB = TM // SB
HP = 8
HPB = 8
HD = 128
NH = 8
LN_EPS = 1e-5
NORM_EPS = 1e-6
ALPHA = 2.0 ** 0.25
GATE_NORM = 16.0
GLA_DK = 64
N_MOD = 9
VMEM_LIMIT = 52 * 1024 * 1024

MIXP = 5120
GG, RG, GQ, GK, GV, RQ, RFF, RFB, RI, LR = 0, 512, 1024, 1536, 2048, 2560, 3072, 3584, 4096, 4608
IN_SPLITS = (256, 256, 512, 512, 16, 16, 512, 512, 512, 512, 512)

ADAM_LR, ADAM_B1, ADAM_B2, ADAM_EPS, ADAM_WD, ADAM_STEP = 0.001, 0.9, 0.999, 1e-08, 0.01, 10


def _cp(n_axes):
    return pltpu.CompilerParams(dimension_semantics=("arbitrary",) * n_axes, vmem_limit_bytes=VMEM_LIMIT)


def _rowmap(stride, off):
    return lambda b, i: (b * stride + off + i, 0)


def _mmap(comb):
    if comb:
        return lambda b, i: (b, jnp.minimum(i, 1), 0, 0)
    return lambda b, i: (b, 1, 0, 0)


def _ln(x):
    mu = jnp.mean(x, axis=-1, keepdims=True)
    xc = x - mu
    var = jnp.mean(xc * xc, axis=-1, keepdims=True)
    r = lax.rsqrt(var + LN_EPS)
    return xc * r, r


def _ln_bwd(dxh, xh, r):
    return r * (dxh - jnp.mean(dxh, axis=-1, keepdims=True) - xh * jnp.mean(dxh * xh, axis=-1, keepdims=True))


def _sigmoid(x):
    return 1.0 / (1.0 + jnp.exp(-x))


def _rsum(x):
    return jnp.sum(x, axis=0, keepdims=True)


def _dot(a, b):
    return jnp.dot(a, b, preferred_element_type=F32)


def _dot_nt(a, b):
    return lax.dot_general(a, b, (((1,), (1,)), ((), ())), preferred_element_type=F32)


def _dot_tn(a, b):
    return lax.dot_general(a, b, (((0,), (0,)), ((), ())), preferred_element_type=F32)


def _modulate(xv, m_ref, sub):
    xh, _ = _ln(xv)
    sh = m_ref[0, 0, 3 * sub:3 * sub + 1, :]
    sc = m_ref[0, 0, 3 * sub + 1:3 * sub + 2, :]
    return (xh * (1.0 + sh) + sc).astype(BF16)


def _embed_lnmod(x, ctx, pos, mvec):
    B, T, _ = x.shape
    nt = 1 + T // TM

    def body(x_ref, c_ref, p_ref, m_ref, o_ref, h_ref):
        i = pl.program_id(1)

        @pl.when(i == 0)
        def _():
            o_ref[...] = c_ref[0]

        @pl.when(i > 0)
        def _():
            o_ref[...] = x_ref[0] + p_ref[...]

        h_ref[...] = _modulate(o_ref[...], m_ref, 0)

    rows = pl.BlockSpec((TM, D), lambda b, i: (b * nt + i, 0))
    return pl.pallas_call(
        body, name="embed_lnmod0", grid=(B, nt),
        in_specs=[pl.BlockSpec((1, TM, D), lambda b, i: (b, jnp.maximum(i - 1, 0), 0)),
                  pl.BlockSpec((1, TM, D), lambda b, i: (b, 0, 0)),
                  pl.BlockSpec((TM, D), lambda b, i: (jnp.maximum(i - 1, 0), 0)),
                  pl.BlockSpec((1, 1, N_MOD, D), _mmap(True))],
        out_specs=[rows, rows],
        out_shape=[jax.ShapeDtypeStruct((B * nt * TM, D), F32), jax.ShapeDtypeStruct((B * nt * TM, D), BF16)],
        compiler_params=_cp(2))(x, ctx, pos, mvec)


def _lnmod0_bwd(dh, X, mvec, dres, x_shape, B, nt):
    def body(dh_ref, x_ref, m_ref, dr_ref, dx_ref, dm_ref):
        i = pl.program_id(1)
        xh, r = _ln(x_ref[...])
        sh = m_ref[0, 0, 0:1, :]
        dhv = dh_ref[...].astype(F32)

        @pl.when((i == 0) | (i == 1))
        def _():
            dm_ref[...] = jnp.zeros_like(dm_ref)

        dm_ref[0, 0, 0:1, :] += _rsum(dhv * xh)
        dm_ref[0, 0, 1:2, :] += _rsum(dhv)

        @pl.when(i > 0)
        def _():
            dx_ref[0] = _ln_bwd(dhv * (1.0 + sh), xh, r) + dr_ref[...]

    rows = pl.BlockSpec((TM, D), _rowmap(nt, 0))
    return pl.pallas_call(
        body, name="lnmod0_bwd", grid=(B, nt),
        in_specs=[rows, rows, pl.BlockSpec((1, 1, N_MOD, D), _mmap(True)), rows],
        out_specs=[pl.BlockSpec((1, TM, D), lambda b, i: (b, jnp.maximum(i - 1, 0), 0)),
                   pl.BlockSpec((1, 1, 2, D), _mmap(True))],
        out_shape=[jax.ShapeDtypeStruct(x_shape, F32), jax.ShapeDtypeStruct((B, 2, 2, D), F32)],
        compiler_params=_cp(2))(dh, X, mvec, dres)


def _resid_fwd(x_ref, y_ref, m_ref, gb_ref, sub, w):
    wg = w * m_ref[0, 0, 3 * sub + 2:3 * sub + 3, :]
    y = y_ref[...].astype(F32)
    zh, r = _ln(ALPHA * x_ref[...] + wg * y)
    return y, wg, zh, r


def _resid_grads(do, y, wg, zh, r, w, gb_ref, comb, dx_ref, dy_ref, dg_ref, dgb_ref):
    b_, i = pl.program_id(0), pl.program_id(1)
    dz = _ln_bwd(do * gb_ref[0:1, :], zh, r)
    dx_ref[...] = ALPHA * dz
    dy_ref[...] = (wg * dz).astype(BF16)

    @pl.when((b_ == 0) & (i == 0))
    def _():
        dgb_ref[...] = jnp.zeros_like(dgb_ref)

    dgb_ref[0:1, :] += _rsum(do * zh)
    dgb_ref[1:2, :] += _rsum(do)

    init = (i == 0) | (i == 1) if comb else (i == 0)

    @pl.when(init)
    def _():
        dg_ref[...] = jnp.zeros_like(dg_ref)

    dg_ref[0, 0] += w * _rsum(dz * y)


def _resid_lnmod(X, lx, Y, mvec, comb, sub, w, gb, B, nt, name):
    def body(x_ref, y_ref, m_ref, gb_ref, o_ref, h_ref):
        _, _, zh, _ = _resid_fwd(x_ref, y_ref, m_ref, gb_ref, sub, w)
        xn = zh * gb_ref[0:1, :] + gb_ref[1:2, :]
        o_ref[...] = xn
        h_ref[...] = _modulate(xn, m_ref, sub + 1)

    rows = pl.BlockSpec((TM, D), _rowmap(nt, 0))
    return pl.pallas_call(
        body, name=name, grid=(B, nt),
        in_specs=[pl.BlockSpec((TM, D), _rowmap(*lx)), rows,
                  pl.BlockSpec((1, 1, N_MOD, D), _mmap(comb)), pl.BlockSpec((2, D), lambda b, i: (0, 0))],
        out_specs=[rows, rows],
        out_shape=[jax.ShapeDtypeStruct((B * nt * TM, D), F32), jax.ShapeDtypeStruct((B * nt * TM, D), BF16)],
        compiler_params=_cp(2))(X, Y, mvec, gb)


def _resid_out_shapes(B, nt, comb):
    rows = pl.BlockSpec((TM, D), _rowmap(nt, 0))
    specs = [rows, rows, pl.BlockSpec((1, 1, 1, D), _mmap(comb)), pl.BlockSpec((2, D), lambda b, i: (0, 0))]
    shapes = [jax.ShapeDtypeStruct((B * nt * TM, D), F32), jax.ShapeDtypeStruct((B * nt * TM, D), BF16),
              jax.ShapeDtypeStruct((B, 2, 1, D), F32), jax.ShapeDtypeStruct((2, D), F32)]
    return specs, shapes


def _tail(X, Y, mvec, gb, tgt, sub, w, B, nt):
    def body(x_ref, y_ref, m_ref, gb_ref, t_ref, dx_ref, dy_ref, dg_ref, dgb_ref, l_ref):
        y, wg, zh, r = _resid_fwd(x_ref, y_ref, m_ref, gb_ref, sub, w)
        e = (zh * gb_ref[0:1, :] + gb_ref[1:2, :]) - t_ref[0]

        @pl.when((pl.program_id(0) == 0) & (pl.program_id(1) == 0))
        def _():
            l_ref[...] = jnp.zeros_like(l_ref)

        l_ref[...] += _rsum(e * e)
        _resid_grads(e * (1.0 / D), y, wg, zh, r, w, gb_ref, False, dx_ref, dy_ref, dg_ref, dgb_ref)

    rows = pl.BlockSpec((TM, D), _rowmap(nt, 0))
    specs, shapes = _resid_out_shapes(B, nt, False)
    return pl.pallas_call(
        body, name="resid2_loss_bwd", grid=(B, nt),
        in_specs=[rows, rows, pl.BlockSpec((1, 1, N_MOD, D), _mmap(False)), pl.BlockSpec((2, D), lambda b, i: (0, 0)),
                  pl.BlockSpec((1, TM, D), lambda b, i: (b, i, 0))],
        out_specs=specs + [pl.BlockSpec((1, D), lambda b, i: (0, 0))],
        out_shape=shapes + [jax.ShapeDtypeStruct((1, D), F32)],
        compiler_params=_cp(2))(X, Y, mvec, gb, tgt)


def _lnmod_resid_bwd(dh, Xi, lxi, dres, Xp, lxp, Yp, mvec, comb, sub, w, gb, B, nt, name):
    ntl = nt - 1 if comb else nt

    def body(dh_ref, xi_ref, dr_ref, xp_ref, yp_ref, m_ref, gb_ref, dx_ref, dy_ref, dg_ref, dgb_ref, dm_ref):
        i = pl.program_id(1)
        xh, r = _ln(xi_ref[...])
        sh = m_ref[0, 0, 3 * sub:3 * sub + 1, :]
        dhv = dh_ref[...].astype(F32)
        dr = dr_ref[...]
        if comb:
            dr = jnp.where(i > 0, dr, 0.0)
        do = _ln_bwd(dhv * (1.0 + sh), xh, r) + dr

        init = (i == 0) | (i == 1) if comb else (i == 0)

        @pl.when(init)
        def _():
            dm_ref[...] = jnp.zeros_like(dm_ref)

        dm_ref[0, 0, 0:1, :] += _rsum(dhv * xh)
        dm_ref[0, 0, 1:2, :] += _rsum(dhv)
        y, wg, zh, r2 = _resid_fwd(xp_ref, yp_ref, m_ref, gb_ref, sub - 1, w)
        _resid_grads(do, y, wg, zh, r2, w, gb_ref, comb, dx_ref, dy_ref, dg_ref, dgb_ref)

    rows = pl.BlockSpec((TM, D), _rowmap(nt, 0))
    if comb:
        dres_spec = pl.BlockSpec((TM, D), lambda b, i: (b * ntl + jnp.maximum(i - 1, 0), 0))
    else:
        dres_spec = rows
    specs, shapes = _resid_out_shapes(B, nt, comb)
    return pl.pallas_call(
        body, name=name, grid=(B, nt),
        in_specs=[rows, pl.BlockSpec((TM, D), _rowmap(*lxi)), dres_spec, pl.BlockSpec((TM, D), _rowmap(*lxp)), rows,
                  pl.BlockSpec((1, 1, N_MOD, D), _mmap(comb)), pl.BlockSpec((2, D), lambda b, i: (0, 0))],
        out_specs=specs + [pl.BlockSpec((1, 1, 2, D), _mmap(comb))],
        out_shape=shapes + [jax.ShapeDtypeStruct((B, 2, 2, D), F32)],
        compiler_params=_cp(2))(dh, Xi, dres, Xp, Yp, mvec, gb)


def _ffn_out_dx(dy, W, u, name, dep=None):
    M = dy.shape[0]
    half = DFF // 2
    deps = [] if dep is None else [dep]

    def body(dy_ref, w_ref, u_ref, *rest):
        du_ref = rest[-1]
        dyv = dy_ref[...]
        for j in range(2):
            lo, hi = j * half, (j + 1) * half
            da = _dot_nt(dyv, w_ref[lo:hi, :])
            g = u_ref[:, lo:hi].astype(F32)
            up = u_ref[:, DFF + lo:DFF + hi].astype(F32)
            s = _sigmoid(g)
            du_ref[:, lo:hi] = (da * up * (s * (1.0 + g * (1.0 - s)))).astype(BF16)
            du_ref[:, DFF + lo:DFF + hi] = (da * (g * s)).astype(BF16)

    return pl.pallas_call(
        body, name=name, grid=(M // TM,),
        in_specs=[pl.BlockSpec((TM, D), lambda i: (i, 0)), _wspec(W, 1), pl.BlockSpec((TM, 2 * DFF), lambda i: (i, 0))]
        + [_wspec(d, 1) for d in deps],
        out_specs=pl.BlockSpec((TM, 2 * DFF), lambda i: (i, 0)),
        out_shape=jax.ShapeDtypeStruct((M, 2 * DFF), BF16), compiler_params=_cp(1))(dy, W, u, *deps)


def _wspec(W, nidx):
    zeros = (0,) * W.ndim
    if nidx == 1:
        return pl.BlockSpec(W.shape, lambda i: zeros)
    return pl.BlockSpec(W.shape, lambda b, i: zeros)


def _mm_nn(A, la, W, B, nt, out_dtype, name):
    K, N = W.shape

    def body(a_ref, w_ref, o_ref):
        o_ref[...] = _dot(a_ref[...], w_ref[...]).astype(out_dtype)

    return pl.pallas_call(
        body, name=name, grid=(B, nt),
        in_specs=[pl.BlockSpec((TM, K), _rowmap(*la)), _wspec(W, 2)],
        out_specs=pl.BlockSpec((TM, N), _rowmap(nt, 0)),
        out_shape=jax.ShapeDtypeStruct((B * nt * TM, N), out_dtype), compiler_params=_cp(2))(A, W)


def _ffn_in(A, la, W3, B, nt, name):
    K, n = W3.shape[1:]

    def body(a_ref, w_ref, u_ref, s_ref):
        a = a_ref[...]
        for j in range(2):
            g = _dot(a, w_ref[j])
            up = _dot(a, w_ref[j + 2])
            u_ref[:, j * n:(j + 1) * n] = g.astype(BF16)
            u_ref[:, (j + 2) * n:(j + 3) * n] = up.astype(BF16)
            s_ref[:, j * n:(j + 1) * n] = (g * _sigmoid(g) * up).astype(BF16)

    rows = B * nt * TM
    return pl.pallas_call(
        body, name=name, grid=(B, nt),
        in_specs=[pl.BlockSpec((TM, K), _rowmap(*la)), _wspec(W3, 2)],
        out_specs=[pl.BlockSpec((TM, 4 * n), _rowmap(nt, 0)), pl.BlockSpec((TM, 2 * n), _rowmap(nt, 0))],
        out_shape=[jax.ShapeDtypeStruct((rows, 4 * n), BF16), jax.ShapeDtypeStruct((rows, 2 * n), BF16)],
        compiler_params=_cp(2))(A, W3)


def _mm_nt(A, W, name, dep=None, out_dtype=F32):
    M, N = A.shape
    K = W.shape[-2]

    def body(a_ref, w_ref, *rest):
        o_ref = rest[-1]
        if W.ndim == 3:
            n = W.shape[-1]
            acc = _dot_nt(a_ref[:, 0:n], w_ref[0])
            for j in range(1, 4):
                acc = acc + _dot_nt(a_ref[:, j * n:(j + 1) * n], w_ref[j])
            o_ref[...] = acc.astype(out_dtype)
        else:
            o_ref[...] = _dot_nt(a_ref[...], w_ref[...]).astype(out_dtype)

    deps = [] if dep is None else [dep]
    return pl.pallas_call(
        body, name=name, grid=(M // TM,),
        in_specs=[pl.BlockSpec((TM, N), lambda i: (i, 0)), _wspec(W, 1)] + [_wspec(d, 1) for d in deps],
        out_specs=pl.BlockSpec((TM, K), lambda i: (i, 0)),
        out_shape=jax.ShapeDtypeStruct((M, K), out_dtype), compiler_params=_cp(1))(A, W, *deps)


def _mm_tn(A, G, name, tn=512, shards=None):
    M, K = A.shape
    N = G.shape[1]
    tk = next(t for t in ((2048, 1536, 1024, 512) if K <= D else (1024, 512)) if M % t == 0)
    if shards:
        tn = N // shards

    def body(a_ref, g_ref, o_ref):
        @pl.when(pl.program_id(1) == 0)
        def _():
            o_ref[...] = jnp.zeros_like(o_ref)

        upd = _dot_tn(a_ref[...], g_ref[...])
        if shards:
            o_ref[0] += upd
        else:
            o_ref[...] += upd

    if shards:
        out_spec = pl.BlockSpec((1, K, tn), lambda n, k: (n, 0, 0))
        out_shape = jax.ShapeDtypeStruct((shards, K, tn), F32)
    else:
        out_spec = pl.BlockSpec((K, tn), lambda n, k: (0, n))
        out_shape = jax.ShapeDtypeStruct((K, N), F32)
    return pl.pallas_call(
        body, name=name, grid=(N // tn, M // tk),
        in_specs=[pl.BlockSpec((tk, K), lambda n, k: (k, 0)), pl.BlockSpec((tk, tn), lambda n, k: (k, n))],
        out_specs=out_spec, out_shape=out_shape, compiler_params=_cp(2))(A, G)


def _logsig(z):
    return jnp.minimum(z, 0.0) - jnp.log(1.0 + jnp.exp(-jnp.abs(z)))


ZW = 2688
ZRQ, ZRFF, ZRFB, ZLR = 1024, 1536, 2048, 2560


def _mix_in_features(H, W, a2p, biasp, lbp, B, nt):
    K = W.shape[0]

    def body(h_ref, w_ref, a2_ref, bias_ref, lb_ref, q_ref, k_ref, v_ref, g_ref, z_ref):
        hv = h_ref[...]
        proj = lambda lo, hi: _dot(hv, w_ref[:, lo:hi])
        z_ref[:, 0:2 * 4 * HD] = proj(GG, GG + 2 * 4 * HD)
        lrf = proj(LR, LR + HD)
        z_ref[:, ZLR:ZLR + HD] = lrf
        lr = lrf.astype(BF16)
        lane = lax.broadcasted_iota(jnp.int32, (1, 4 * HD), 1)
        keep = (lane & (HD - 1)) < GLA_DK
        for d in range(2):
            z = _dot(lr, a2_ref[d]) + bias_ref[d:d + 1, :]
            gl = jnp.where(keep, _logsig(z) * (1.0 / GATE_NORM), 0.0)
            for h in range(4):
                g_ref[d, 0, h] = gl[:, h * HD:(h + 1) * HD]
        gq, gk, gv = proj(GQ, GQ + 4 * HD), proj(GK, GK + 4 * HD), proj(GV, GV + 4 * HD)
        for h in range(4):
            sl = slice(h * HD, (h + 1) * HD)
            q_ref[0, h] = gq[:, sl] * (GLA_DK ** -0.5)
            k_ref[0, 0, h] = gk[:, sl]
            k_ref[1, 0, h] = gk[:, sl]
            v_ref[0, h] = gv[:, sl].astype(BF16)
        rqs, ri = proj(RQ, RQ + 4 * HD), proj(RI, RI + 4 * HD)
        z_ref[:, ZRQ:ZRQ + 4 * HD] = rqs
        for h in range(4):
            sl = slice(h * HD, (h + 1) * HD)
            rq = rqs[:, sl]
            q_ref[0, 4 + h] = rq * _sigmoid(rq) * (HD ** -0.5)
            v_ref[0, 4 + h] = ri[:, sl].astype(BF16)
        for d, off, zoff in ((0, RFF, ZRFF), (1, RFB, ZRFB)):
            rf = proj(off, off + 4 * HD)
            z_ref[:, zoff:zoff + 4 * HD] = rf
            for h in range(4):
                sl = slice(h * HD, (h + 1) * HD)
                lb = lb_ref[d:d + 1, sl]
                f = lb + (1.0 - lb) * _sigmoid(rf[:, sl])
                g_ref[d, 0, 4 + h] = jnp.log(f)
                k_ref[d, 0, 4 + h] = 1.0 - f

    one = pl.BlockSpec((1, NH, TM, HD), lambda b, i: (b, 0, i, 0))
    two = pl.BlockSpec((2, 1, NH, TM, HD), lambda b, i: (0, b, 0, i, 0))
    s1 = jax.ShapeDtypeStruct((B, NH, nt * TM, HD), F32)
    s2 = jax.ShapeDtypeStruct((2, B, NH, nt * TM, HD), F32)
    return pl.pallas_call(
        body, name="mix_in_features", grid=(B, nt),
        in_specs=[pl.BlockSpec((TM, K), _rowmap(nt, 0)), _wspec(W, 2), pl.BlockSpec((2, HD, 4 * HD), lambda b, i: (0, 0, 0)),
                  pl.BlockSpec((2, 4 * HD), lambda b, i: (0, 0)), pl.BlockSpec((2, 4 * HD), lambda b, i: (0, 0))],
        out_specs=[one, two, one, two, pl.BlockSpec((TM, ZW), _rowmap(nt, 0))],
        out_shape=[s1, s2, jax.ShapeDtypeStruct(s1.shape, BF16), s2, jax.ShapeDtypeStruct((B * nt * TM, ZW), F32)],
        compiler_params=_cp(2))(H, W, a2p, biasp, lbp)


def _features_bwd(Z, a2p, biasp, lbp, dQ0, dQ1, dK0, dK1, dV0, dV1, dG0, dG1, dgates, W, B):
    nt = Z.shape[0] // (B * TM)

    def body(z_ref, a2_ref, bias_ref, lb_ref, dq0, dq1, dk0, dk1, dv0, dv1, dg0, dg1, dgt_ref, w_ref,
             df_ref, dh_ref, da2_ref, dbias_ref, dlb_ref):
        b_, i = pl.program_id(0), pl.program_id(1)

        @pl.when((b_ == 0) & (i == 0))
        def _():
            da2_ref[...] = jnp.zeros_like(da2_ref)
            dbias_ref[...] = jnp.zeros_like(dbias_ref)
            dlb_ref[...] = jnp.zeros_like(dlb_ref)

        df_ref[:, 0:2 * 4 * HD] = jnp.where(i > 0, dgt_ref[...], 0.0).astype(BF16)
        df_ref[:, LR + HD:] = jnp.zeros((TM, MIXP - LR - HD), BF16)
        lr = z_ref[:, ZLR:ZLR + HD].astype(BF16)
        lane = lax.broadcasted_iota(jnp.int32, (1, 4 * HD), 1)
        keep = (lane & (HD - 1)) < GLA_DK
        dlr = jnp.zeros((TM, HD), F32)
        dgs = (dg0, dg1)
        dks = (dk0, dk1)
        rd = lambda ref, h: ref[0, h].astype(F32)
        for d in range(2):
            z = _dot(lr, a2_ref[d]) + bias_ref[d:d + 1, :]
            dgl = jnp.concatenate([rd(dgs[d], h) for h in range(4)], axis=1)
            dz = jnp.where(keep, dgl * (1.0 / GATE_NORM) * (1.0 - _sigmoid(z)), 0.0)
            dzb = dz.astype(BF16)
            dlr = dlr + _dot_nt(dzb, a2_ref[d])
            da2_ref[d] += _dot_tn(lr, dzb)
            dbias_ref[d:d + 1, :] += _rsum(dz)
        df_ref[:, LR:LR + HD] = dlr.astype(BF16)
        for h in range(4):
            df_ref[:, GQ + h * HD:GQ + (h + 1) * HD] = ((rd(dq0, h) + rd(dq1, h)) * (GLA_DK ** -0.5)).astype(BF16)
            df_ref[:, GK + h * HD:GK + (h + 1) * HD] = (rd(dk0, h) + rd(dk1, h)).astype(BF16)
            df_ref[:, GV + h * HD:GV + (h + 1) * HD] = (rd(dv0, h) + rd(dv1, h)).astype(BF16)
        for h in range(4):
            sl = slice(h * HD, (h + 1) * HD)
            rq = z_ref[:, ZRQ + h * HD:ZRQ + (h + 1) * HD]
            s = _sigmoid(rq)
            dqh = rd(dq0, 4 + h) + rd(dq1, 4 + h)
            df_ref[:, RQ + h * HD:RQ + (h + 1) * HD] = (dqh * (HD ** -0.5) * (s * (1.0 + rq * (1.0 - s)))).astype(BF16)
            df_ref[:, RI + h * HD:RI + (h + 1) * HD] = (rd(dv0, 4 + h) + rd(dv1, 4 + h)).astype(BF16)
            for d, off, zoff in ((0, RFF, ZRFF), (1, RFB, ZRFB)):
                lb = lb_ref[d:d + 1, sl]
                sg = _sigmoid(z_ref[:, zoff + h * HD:zoff + (h + 1) * HD])
                f = lb + (1.0 - lb) * sg
                dff = rd(dgs[d], 4 + h) / f - rd(dks[d], 4 + h)
                df_ref[:, off + h * HD:off + (h + 1) * HD] = (dff * (1.0 - lb) * sg * (1.0 - sg)).astype(BF16)
                dlb_ref[d:d + 1, sl] += _rsum(dff * (1.0 - sg))
        dh_ref[...] = _dot_nt(df_ref[...], w_ref[...]).astype(BF16)

    m0 = lambda b, i: (b, 0, i, 0)
    one = lambda m: pl.BlockSpec((1, NH, TM, HD), m)
    return pl.pallas_call(
        body, name="mix_features_bwd", grid=(B, nt),
        in_specs=[pl.BlockSpec((TM, ZW), _rowmap(nt, 0)), pl.BlockSpec((2, HD, 4 * HD), lambda b, i: (0, 0, 0)),
                  pl.BlockSpec((2, 4 * HD), lambda b, i: (0, 0)), pl.BlockSpec((2, 4 * HD), lambda b, i: (0, 0)),
                  one(m0), one(m0), one(m0), one(m0), one(m0), one(m0), one(m0), one(m0),
                  pl.BlockSpec((TM, D), lambda b, i: (b * (nt - 1) + jnp.maximum(i - 1, 0), 0)), _wspec(W, 2)],
        out_specs=[pl.BlockSpec((TM, MIXP), _rowmap(nt, 0)), pl.BlockSpec((TM, D), _rowmap(nt, 0)),
                   pl.BlockSpec((2, HD, 4 * HD), lambda b, i: (0, 0, 0)),
                   pl.BlockSpec((2, 4 * HD), lambda b, i: (0, 0)), pl.BlockSpec((2, 4 * HD), lambda b, i: (0, 0))],
        out_shape=[jax.ShapeDtypeStruct((B * nt * TM, MIXP), BF16), jax.ShapeDtypeStruct((B * nt * TM, D), BF16),
                   jax.ShapeDtypeStruct((2, HD, 4 * HD), F32),
                   jax.ShapeDtypeStruct((2, 4 * HD), F32), jax.ShapeDtypeStruct((2, 4 * HD), F32)],
        compiler_params=_cp(2))(Z, a2p, biasp, lbp, dQ0, dQ1, dK0, dK1, dV0, dV1, dG0, dG1, dgates, W)


def _chunk_scan(x, rin, fwd):
    acc = x
    sft = 1
    while sft < CH:
        if fwd:
            acc = acc + jnp.where(rin >= sft, pltpu.roll(acc, sft, 0), 0.0)
        else:
            acc = acc + jnp.where(rin < CH - sft, pltpu.roll(acc, TM - sft, 0), 0.0)
        sft *= 2
    return acc


def _chunk_total(x):
    t = jnp.sum(x.reshape(NCB, CH, HD), axis=1, keepdims=True)
    return jnp.broadcast_to(t, (NCB, CH, HD)).reshape(TM, HD)


def _scan_masks(rev):
    rin = lax.broadcasted_iota(jnp.int32, (TM, HD), 0) & (CH - 1)
    ri = lax.broadcasted_iota(jnp.int32, (SB, SB), 0)
    ci = lax.broadcasted_iota(jnp.int32, (SB, SB), 1)
    same = (ri >> 5) == (ci >> 5)
    lo = same & (ri >= ci)
    up = same & (ri <= ci)
    mask, maskT = (up, lo) if rev else (lo, up)
    re = lax.broadcasted_iota(jnp.int32, (SB, CSB * HD), 0) >> 5
    ce = lax.broadcasted_iota(jnp.int32, (SB, CSB * HD), 1) >> 7
    return rin, mask, maskT, re == ce


def _scan_decay(q, k, g, rin, rev):
    b = _chunk_scan(g, rin, not rev)
    xx = _chunk_total(g) - b
    eb = jnp.exp(b)
    return b, xx, eb, q * eb, k * jnp.exp(-b), k * jnp.exp(xx)


def _sub(x, s):
    return x[s * SB:(s + 1) * SB]


def _expand(xb, mexp):
    return jnp.where(mexp, jnp.concatenate([xb] * CSB, axis=1), jnp.zeros((), xb.dtype))


def _own(x, mexp):
    xm = jnp.where(mexp, x, 0.0)
    acc = xm[:, 0:HD]
    for n in range(1, CSB):
        acc = acc + xm[:, n * HD:(n + 1) * HD]
    return acc


def _stack(per_chunk, s):
    return jnp.concatenate(per_chunk[s * CSB:(s + 1) * CSB], axis=1)


def _state_pass(s0, eb, uts, rev):
    order = range(NCB - 1, -1, -1) if rev else range(NCB)
    states = [None] * NCB
    s = s0
    for n in order:
        row = n * CH if rev else n * CH + CH - 1
        states[n] = s
        s = eb[row:row + 1, :] * s + uts[n // CSB][:, (n % CSB) * HD:(n % CSB + 1) * HD]
    return states, s


def _scan_fwd(Q, K, V, G, rev, B):
    nb = Q.shape[2] // TM
    d = 1 if rev else 0
    rmap = (lambda s: jnp.where(s == 0, 0, nb - s)) if rev else (lambda s: s)

    def body(q_ref, k_ref, v_ref, g_ref, o_ref, st_ref, s_scr):
        @pl.when(pl.program_id(2) == 0)
        def _():
            s_scr[...] = jnp.zeros_like(s_scr)

        rin, mask, _, mexp = _scan_masks(rev)

        def head(p, readout):
            s0 = s_scr[p]
            st_ref[0, p, 0] = s0
            _, _, eb, qd, ki, kt = _scan_decay(q_ref[0, p], k_ref[0, 0, p], g_ref[0, 0, p], rin, rev)
            ktb, vb = kt.astype(BF16), v_ref[0, p]
            uts = [_dot_tn(_sub(vb, s), _expand(_sub(ktb, s), mexp)) for s in range(NSB)]
            states, s_new = _state_pass(s0, eb, uts, rev)
            s_scr[p] = s_new
            if not readout:
                return
            qb, kib = qd.astype(BF16), ki.astype(BF16)
            for s in range(NSB):
                a = jnp.where(mask, _dot_nt(_sub(qb, s), _sub(kib, s)), 0.0)
                o_ref[0, p, s * SB:(s + 1) * SB, :] = (
                    _dot(a.astype(BF16), _sub(vb, s))
                    + _dot_nt(_expand(_sub(qb, s), mexp), _stack(states, s).astype(BF16)))

        @pl.when(pl.program_id(2) >= 1)
        def _():
            for p in range(HP):
                head(p, True)

        @pl.when(pl.program_id(2) == 0)
        def _():
            for p in range(HP):
                head(p, False)

    one = pl.BlockSpec((1, HP, TM, HD), lambda b, h, s: (b, h, rmap(s), 0))
    two = pl.BlockSpec((1, 1, HP, TM, HD), lambda b, h, s: (d, b, h, rmap(s), 0))
    return pl.pallas_call(
        body, name="scan_fwd_rev" if rev else "scan_fwd", grid=(B, NH // HP, nb),
        in_specs=[one, two, one, two],
        out_specs=[one, pl.BlockSpec((1, HP, 1, HD, HD), lambda b, h, s: (b, h, s, 0, 0))],
        out_shape=[jax.ShapeDtypeStruct(Q.shape, F32), jax.ShapeDtypeStruct((B, NH, nb, HD, HD), F32)],
        scratch_shapes=[pltpu.VMEM((HP, HD, HD), F32)],
        compiler_params=_cp(3))(Q, K, V, G)


def _scan_bwd(Q, K, V, G, St, dO, rev, B):
    nb = Q.shape[2] // TM
    d = 1 if rev else 0
    smap = lambda t: nb - 1 - t
    rmap = (lambda t: jnp.where(smap(t) == 0, 0, nb - smap(t))) if rev else smap

    def body(q_ref, k_ref, v_ref, g_ref, st_ref, do_ref, dq_ref, dk_ref, dv_ref, dg_ref, ds_scr):
        t = pl.program_id(2)

        @pl.when(t == 0)
        def _():
            ds_scr[...] = jnp.zeros_like(ds_scr)

        is_lat = smap(t) >= 1
        rin, mask, maskT, mexp = _scan_masks(rev)
        for p in range(HPB):
            b, xx, eb, qd, ki, kt = _scan_decay(q_ref[0, p], k_ref[0, 0, p], g_ref[0, 0, p], rin, rev)
            qb, kib, ktb, vb = qd.astype(BF16), ki.astype(BF16), kt.astype(BF16), v_ref[0, p]
            dob = jnp.where(is_lat, do_ref[0, p], 0.0).astype(BF16)
            kt_exps = [_expand(_sub(ktb, s), mexp) for s in range(NSB)]
            uts = [_dot_tn(_sub(vb, s), kt_exps[s]) for s in range(NSB)]
            states, _ = _state_pass(st_ref[0, p, 0], eb, uts, rev)
            gts = [_dot_tn(_sub(dob, s), _expand(_sub(qb, s), mexp)) for s in range(NSB)]
            order = range(NCB) if rev else range(NCB - 1, -1, -1)
            dsp = [None] * NCB
            t2 = [None] * NCB
            dsc = ds_scr[p]
            for n in order:
                row = n * CH if rev else n * CH + CH - 1
                ebl = eb[row:row + 1, :]
                dsp[n] = dsc
                t2[n] = jnp.broadcast_to(ebl * _rsum(states[n] * dsc), (CH, HD))
                dsc = gts[n // CSB][:, (n % CSB) * HD:(n % CSB + 1) * HD] + ebl * dsc
            ds_scr[p] = dsc
            dqds, dkis, dkts = [], [], []
            for s in range(NSB):
                q_s, ki_s, v_s, do_s = _sub(qb, s), _sub(kib, s), _sub(vb, s), _sub(dob, s)
                dspb = _stack(dsp, s).astype(BF16)
                da = jnp.where(mask, _dot_nt(do_s, v_s), 0.0).astype(BF16)
                dat = jnp.where(maskT, _dot_nt(v_s, do_s), 0.0).astype(BF16)
                at = jnp.where(maskT, _dot_nt(ki_s, q_s), 0.0).astype(BF16)
                dqds.append(_dot(da, ki_s) + _own(_dot(do_s, _stack(states, s).astype(BF16)), mexp))
                dkis.append(_dot(dat, q_s))
                dv_ref[0, p, s * SB:(s + 1) * SB, :] = (_dot(at, do_s) + _dot_nt(kt_exps[s], dspb)).astype(BF16)
                dkts.append(_own(_dot(v_s, dspb), mexp))
            dqd, dki, dkt = (jnp.concatenate(parts, axis=0) for parts in (dqds, dkis, dkts))
            z = dkt * kt
            db = dqd * qd - dki * ki
            dq_ref[0, p] = (dqd * eb).astype(BF16)
            dk_ref[0, p] = (dki * jnp.exp(-b) + dkt * jnp.exp(xx)).astype(BF16)
            dg_ref[0, p] = (_chunk_total(db) + (db - z) + _chunk_scan(z - db, rin, not rev)
                            + jnp.concatenate(t2, axis=0)).astype(BF16)

    one = pl.BlockSpec((1, HPB, TM, HD), lambda b, h, t: (b, h, rmap(t), 0))
    two = pl.BlockSpec((1, 1, HPB, TM, HD), lambda b, h, t: (d, b, h, rmap(t), 0))
    lat = pl.BlockSpec((1, HPB, TM, HD), lambda b, h, t: (b, h, jnp.clip(rmap(t) - 1, 0, nb - 2), 0))
    shp = jax.ShapeDtypeStruct(Q.shape, BF16)
    return pl.pallas_call(
        body, name="scan_bwd_rev" if rev else "scan_bwd", grid=(B, NH // HPB, nb),
        in_specs=[one, two, one, two, pl.BlockSpec((1, HPB, 1, HD, HD), lambda b, h, t: (b, h, smap(t), 0, 0)), lat],
        out_specs=[one, one, one, one], out_shape=[shp, shp, shp, shp],
        scratch_shapes=[pltpu.VMEM((HPB, HD, HD), F32)],
        compiler_params=_cp(3))(Q, K, V, G, St, dO)


def _gnorm_mix_out(O0, O1, F, gains, W, B, ntl):
    nt = ntl + 1

    def body(o0_ref, o1_ref, f_ref, gn_ref, w_ref, m_ref, y_ref):
        for h in range(NH):
            o = o0_ref[0, h] + o1_ref[0, h]
            r = lax.rsqrt(jnp.mean(o * o, axis=-1, keepdims=True) + NORM_EPS)
            gn = gn_ref[0:1, :] if h < 4 else gn_ref[1:2, :]
            gt = f_ref[:, h * HD:(h + 1) * HD]
            m_ref[:, h * HD:(h + 1) * HD] = (o * r * gn * (gt * _sigmoid(gt))).astype(BF16)
        y_ref[...] = _dot(m_ref[...], w_ref[...]).astype(BF16)

    ospec = pl.BlockSpec((1, NH, TM, HD), lambda b, i: (b, 0, i + 1, 0))
    rows = pl.BlockSpec((TM, D), _rowmap(ntl, 0))
    shp = jax.ShapeDtypeStruct((B * ntl * TM, D), BF16)
    return pl.pallas_call(
        body, name="gated_norm_mix_out", grid=(B, ntl),
        in_specs=[ospec, ospec, pl.BlockSpec((TM, D), lambda b, i: (b * nt + 1 + i, 0)),
                  pl.BlockSpec((2, HD), lambda b, i: (0, 0)), _wspec(W, 2)],
        out_specs=[rows, rows], out_shape=[shp, shp], compiler_params=_cp(2))(O0, O1, F, gains, W)


def _mix_out_dx_gnorm_bwd(dY, W, O0, O1, F, gains, dep, B, ntl):
    nt = ntl + 1

    def body(dy_ref, w_ref, o0_ref, o1_ref, f_ref, gn_ref, dep_ref, do_ref, dgt_ref, dgn_ref):
        b_, i = pl.program_id(0), pl.program_id(1)

        @pl.when((b_ == 0) & (i == 0))
        def _():
            dgn_ref[...] = jnp.zeros_like(dgn_ref)

        dyv = dy_ref[...]
        for h in range(NH):
            o = o0_ref[0, h] + o1_ref[0, h]
            r = lax.rsqrt(jnp.mean(o * o, axis=-1, keepdims=True) + NORM_EPS)
            y = o * r
            gn = gn_ref[0:1, :] if h < 4 else gn_ref[1:2, :]
            gt = f_ref[:, h * HD:(h + 1) * HD]
            s = _sigmoid(gt)
            dm = _dot_nt(dyv, w_ref[h * HD:(h + 1) * HD, :])
            don = dm * (gt * s)
            dgt_ref[:, h * HD:(h + 1) * HD] = (dm * (y * gn) * (s * (1.0 + gt * (1.0 - s)))).astype(BF16)
            row = 0 if h < 4 else 1
            dgn_ref[row:row + 1, :] += _rsum(don * y)
            dy = don * gn
            do_ref[0, h] = (r * (dy - y * jnp.mean(dy * y, axis=-1, keepdims=True))).astype(BF16)

    ospec = pl.BlockSpec((1, NH, TM, HD), lambda b, i: (b, 0, i + 1, 0))
    return pl.pallas_call(
        body, name="mix_out_dx_gated_norm_bwd", grid=(B, ntl),
        in_specs=[pl.BlockSpec((TM, D), _rowmap(ntl, 0)), _wspec(W, 2), ospec, ospec,
                  pl.BlockSpec((TM, D), lambda b, i: (b * nt + 1 + i, 0)), pl.BlockSpec((2, HD), lambda b, i: (0, 0)),
                  _wspec(dep, 2)],
        out_specs=[pl.BlockSpec((1, NH, TM, HD), lambda b, i: (b, 0, i, 0)), pl.BlockSpec((TM, D), _rowmap(ntl, 0)),
                   pl.BlockSpec((2, HD), lambda b, i: (0, 0))],
        out_shape=[jax.ShapeDtypeStruct((B, NH, ntl * TM, HD), BF16), jax.ShapeDtypeStruct((B * ntl * TM, D), BF16),
                   jax.ShapeDtypeStruct((2, HD), F32)],
        compiler_params=_cp(2))(dY, W, O0, O1, F, gains, dep)


def _sincos_2d(rows, width, dim):
    quarter = dim // 4
    omega = 1.0 / 10000.0 ** (jnp.arange(quarter, dtype=F32) / quarter)

    def emb(n):
        a = jnp.arange(n).astype(F32)[:, None] * omega[None, :]
        return jnp.concatenate([jnp.sin(a), jnp.cos(a)], axis=-1)

    er = jnp.broadcast_to(emb(rows)[:, None, :], (rows, width, dim // 2))
    ec = jnp.broadcast_to(emb(width)[None, :, :], (rows, width, dim // 2))
    return jnp.concatenate([er, ec], axis=-1).reshape(rows * width, dim)


def _pad_heads(w):
    k = w.shape[0]
    return jnp.pad(w.reshape(k, 4, GLA_DK), ((0, 0), (0, 0), (0, HD - GLA_DK))).reshape(k, 4 * HD)


def _unpad_heads(w):
    k = w.shape[0]
    return w.reshape(k, 4, HD)[:, :, :GLA_DK].reshape(k, 4 * GLA_DK)


MIX_N = 1032
MIX_NP = 1152
_SEGS = ([(64 * h, 64, GQ + HD * h) for h in range(4)] + [(256 + 64 * h, 64, GK + HD * h) for h in range(4)]
         + [(512, 512, GV), (1024, 512, GG), (1536, 32, LR), (1568, 512, RQ), (2080, 512, RFF), (2592, 512, RFB),
            (3104, 512, RI), (3616, 512, RG)])


def _mix_in_to_padded(ps):
    k = ps.shape[1]
    parts, pos = [], 0
    for g0, ln, s0 in sorted(_SEGS, key=lambda s: s[2]):
        if s0 > pos:
            parts.append(jnp.zeros((k, s0 - pos), ps.dtype))
        for j in range(4):
            lo, hi = max(g0, j * MIX_N), min(g0 + ln, (j + 1) * MIX_N)
            if lo < hi:
                parts.append(ps[j][:, lo - j * MIX_N:hi - j * MIX_N])
        pos = s0 + ln
    parts.append(jnp.zeros((k, MIXP - pos), ps.dtype))
    return jnp.concatenate(parts, axis=1)


def _mix_in_from_padded(g):
    k = g.shape[0]
    shards = []
    for j in range(4):
        parts = []
        for g0, ln, s0 in sorted(_SEGS):
            lo, hi = max(g0, j * MIX_N), min(g0 + ln, (j + 1) * MIX_N)
            if lo < hi:
                parts.append(g[:, s0 + lo - g0:s0 + hi - g0])
        parts.append(jnp.zeros((k, MIX_NP - MIX_N), g.dtype))
        shards.append(jnp.concatenate(parts, axis=1))
    return jnp.stack(shards)


def _local_step(x, ctx, tgt, mvec, weights_for, prefetch, ln_gain, ln_bias, a2f, a2b, abf, abb, lb, gng, gnh, on_grads, on_sent):
    B, T, _ = x.shape
    assert ctx.shape[1] == TM and T % TM == 0
    ntl = T // TM
    nt = ntl + 1
    C, L, CL = (nt, 0), (ntl, 0), (nt, 1)
    pos = _sincos_2d(T // 64, 64, D)
    gbs = [jnp.stack([ln_gain[i], ln_bias[i]]) for i in range(3)]
    a2p = jnp.zeros((2, HD, 4 * HD), F32)
    a2p = a2p.at[0, 0:16].set(_pad_heads(a2f)).at[1, 16:32].set(_pad_heads(a2b)).astype(BF16)
    biasp = jnp.concatenate([_pad_heads(abf.reshape(1, -1)), _pad_heads(abb.reshape(1, -1))], axis=0)
    gains = jnp.concatenate([gng.reshape(1, HD), gnh.reshape(1, HD)], axis=0)

    X0, h0 = _embed_lnmod(x, ctx, pos, mvec)
    (w1i,) = weights_for("ffn1_in", h0)
    prefetch("ffn1_out", w1i)
    u0, a0 = _ffn_in(h0, C, w1i, B, nt, "ffn1_in")
    (w1o,) = weights_for("ffn1_out", a0)
    tok = prefetch("mix", a0)
    y0 = _mm_nn(a0, C, w1o, B, nt, BF16, "ffn1_out")
    X1, h1 = _resid_lnmod(X0, C, y0, mvec, True, 0, 0.5, gbs[0] + tok[0, 0], B, nt, "resid0_lnmod1")
    wmp, wmo = weights_for("mix", h1)
    Q, K, V, G, Zm = _mix_in_features(h1, wmp, a2p, biasp, lb, B, nt)
    prefetch("ffn2", Zm)
    O0, S0 = _scan_fwd(Q, K, V, G, False, B)
    O1, S1 = _scan_fwd(Q, K, V, G, True, B)
    merged, y1 = _gnorm_mix_out(O0, O1, Zm, gains, wmo, B, ntl)
    X2, h2 = _resid_lnmod(X1, CL, y1, mvec, False, 1, 1.0, gbs[1], B, ntl, "resid1_lnmod2")
    w2i, w2o = weights_for("ffn2", h2)
    u2, a2 = _ffn_in(h2, L, w2i, B, ntl, "ffn2_in")
    y2 = _mm_nn(a2, L, w2o, B, ntl, BF16, "ffn2_out")

    dx2r, dy2, dgate2, dgb2, lsum = _tail(X2, y2, mvec, gbs[2], tgt, 2, 0.5, B, ntl)
    loss = (0.5 / D) * jnp.sum(lsum)
    du2 = _ffn_out_dx(dy2, w2o, u2, "ffn2_out_dx")
    g_w2o = _mm_tn(a2, dy2, "ffn2_out_dw")
    dh2 = _mm_nt(du2, w2i, "ffn2_in_dx", out_dtype=BF16)
    g_w2i = _mm_tn(h2, du2, "ffn2_in_dw", shards=4)
    tok = on_grads("ffn2", (g_w2i, g_w2o))
    dx1r, dy1, dgate1, dgb1, dss2 = _lnmod_resid_bwd(dh2, X2, L, dx2r, X1, CL, y1, mvec, False, 2, 1.0,
                                                     gbs[1] + tok[0, 0], B, ntl, "lnmod2_resid1_bwd")
    tok = on_sent("ffn2", dy1)
    g_wmo = _mm_tn(merged, dy1, "mix_out_dw", tn=D)
    dO, dgates, dgains = _mix_out_dx_gnorm_bwd(dy1, wmo, O0, O1, Zm, gains, tok, B, ntl)
    dQ0, dK0, dV0, dG0 = _scan_bwd(Q, K, V, G, S0, dO, False, B)
    dQ1, dK1, dV1, dG1 = _scan_bwd(Q, K, V, G, S1, dO, True, B)
    dF, dh1, da2p, dbiasp, dlb = _features_bwd(Zm, a2p, biasp, lb, dQ0, dQ1, dK0, dK1, dV0, dV1, dG0, dG1, dgates,
                                               wmp, B)
    g_wmp = _mm_tn(h1, dF, "mix_in_dw", tn=MIXP // 4)
    tok = on_grads("mix", (g_wmp, g_wmo))
    dx0r, dy0, dgate0, dgb0, dss1 = _lnmod_resid_bwd(dh1, X1, C, dx1r, X0, C, y0, mvec, True, 1, 0.5,
                                                     gbs[0] + tok[0, 0], B, nt, "lnmod1_resid0_bwd")
    tok = on_sent("mix", dy0)
    du0 = _ffn_out_dx(dy0, w1o, u0, "ffn1_out_dx", dep=tok)
    g_w1o = _mm_tn(a0, dy0, "ffn1_out_dw")
    g_w1i = _mm_tn(h0, du0, "ffn1_in_dw", shards=4)
    tok = on_grads("ffn1", (g_w1i, g_w1o))
    dh0 = _mm_nt(du0, w1i, "ffn1_in_dx", dep=tok, out_dtype=BF16)
    tok = on_sent("ffn1", dh0)
    grad_x, dss0 = _lnmod0_bwd(dh0, X0, mvec + tok[0, 0], dx0r, x.shape, B, nt)

    zero_ctx = lambda a: a.at[:, 0].set(0.0)
    dm = jnp.concatenate([dss0, dgate0, dss1, zero_ctx(dgate1), zero_ctx(dss2), zero_ctx(dgate2)], axis=2)
    small = dict(
        ln_gain=jnp.stack([dgb0[0], dgb1[0], dgb2[0]]), ln_bias=jnp.stack([dgb0[1], dgb1[1], dgb2[1]]),
        a2f=_unpad_heads(da2p[0, 0:16]), a2b=_unpad_heads(da2p[1, 16:32]),
        abf=_unpad_heads(dbiasp[0:1]), abb=_unpad_heads(dbiasp[1:2]), lb=dlb, gng=dgains[0], gnh=dgains[1])
    return loss, grad_x, dm, small


def _small_allgather(xs, name):
    r, n = xs.shape

    def body(x_ref, out_ref, send_sems, recv_sems, local_sem):
        x, y, c = lax.axis_index("x"), lax.axis_index("y"), lax.axis_index("c")
        me, sibling = (x, y, c), (x, y, 1 - c)
        chips = [(1 - x, y), (x, 1 - y), (1 - x, 1 - y)]

        def rows(px, py, pc):
            return out_ref.at[pl.ds((4 * px + 2 * py + pc) * r, r), :]

        def copy(k, block, to, src=None):
            return pltpu.make_async_remote_copy(
                src_ref=rows(*block) if src is None else src, dst_ref=rows(*block),
                send_sem=send_sems.at[k], recv_sem=recv_sems.at[k], device_id=to, device_id_type=MESH)

        mine = pltpu.make_async_copy(x_ref, rows(*me), local_sem)
        mine.start()
        first = [copy(0, me, sibling, src=x_ref)]
        first += [copy(1 + j, me, (*chip, c), src=x_ref) for j, chip in enumerate(chips)]
        for cp in first:
            cp.start()
        passed = [copy(4 + j, (*chip, c), sibling) for j, chip in enumerate(chips)]
        for j, chip in enumerate(chips):
            copy(1 + j, (*chip, c), me).wait_recv()
            passed[j].start()
        copy(0, sibling, me).wait_recv()
        for j, chip in enumerate(chips):
            copy(4 + j, (*chip, 1 - c), me).wait_recv()
        for cp in first + passed:
            cp.wait_send()
        mine.wait()

    out = pl.pallas_call(
        body, name=name,
        out_shape=jax.ShapeDtypeStruct((8 * r, n), xs.dtype),
        in_specs=[pl.BlockSpec(memory_space=pltpu.VMEM)],
        out_specs=pl.BlockSpec(memory_space=pltpu.VMEM),
        scratch_shapes=[pltpu.SemaphoreType.DMA((7,)), pltpu.SemaphoreType.DMA((7,)), pltpu.SemaphoreType.DMA],
        compiler_params=pltpu.CompilerParams(vmem_limit_bytes=VMEM_LIMIT))(xs)
    return out.reshape(8, r, n)


def _gather_flat(v, name):
    n = v.shape[0]
    npad = -(-n // 1024) * 1024
    g = _small_allgather(jnp.pad(v, (0, npad - n)).reshape(8, npad // 8), name)
    return g.reshape(8, npad)[:, :n]


HBM_SPEC = pl.BlockSpec(memory_space=pltpu.HBM)
SEM_SPEC = pl.BlockSpec(memory_space=pltpu.SEMAPHORE)
DATAFLOW = pltpu.SideEffectType.DATAFLOW_SIDE_EFFECTING


def _gather_copies(xs, outs, send_sems, recv_sems):
    x, y, c = lax.axis_index("x"), lax.axis_index("y"), lax.axis_index("c")
    dests = [(x, y, 1 - c), (1 - x, y, c), (x, 1 - y, c), (1 - x, 1 - y, c)]
    return [pltpu.make_async_remote_copy(
        src_ref=xs[w], dst_ref=outs[w].at[4 * x + 2 * y + c], send_sem=send_sems[4 * w + k],
        recv_sem=recv_sems[4 * w + k], device_id=dests[k], device_id_type=MESH)
        for w in range(len(xs)) for k in range(4)]


def _split_start(copies, per_w, srcs, land_lead, after, name):
    n = len(srcs)
    m = per_w * n
    lands = [lax.empty((land_lead,) + s.shape[-2:], s.dtype) for s in srcs]
    deps = [] if after is None else [after]

    def body(*refs):
        xs, ls, outs = refs[:n], refs[n:2 * n], refs[2 * n + len(deps):]
        for cp in copies(xs, ls, outs[:m], outs[m:2 * m]):
            cp.start()
        token = outs[2 * m + 2 * n]
        token[...] = jnp.zeros_like(token)

    outs = pl.pallas_call(
        body, name=name,
        out_shape=([pltpu.SemaphoreType.DMA(())] * (2 * m) + [pltpu.HBM(a.shape, a.dtype) for a in srcs + lands]
                   + [jax.ShapeDtypeStruct((8, 128), F32)]),
        in_specs=[HBM_SPEC] * (2 * n) + [pl.BlockSpec(memory_space=pl.ANY)] * len(deps),
        out_specs=[SEM_SPEC] * (2 * m) + [HBM_SPEC] * (2 * n) + [pl.BlockSpec(memory_space=pltpu.VMEM)],
        input_output_aliases={w: 2 * m + w for w in range(2 * n)},
        compiler_params=pltpu.CompilerParams(has_side_effects=DATAFLOW),
    )(*[pltpu.with_memory_space_constraint(a, pltpu.HBM) for a in srcs + lands], *deps)
    return outs[:2 * m], outs[2 * m:2 * m + n], outs[2 * m + n:2 * m + 2 * n], outs[2 * m + 2 * n]


def _split_wait(copies, per_w, sems, x_thru, l_thru, after, name):
    n = len(x_thru)
    m = per_w * n

    def body(*refs):
        xs, ls, ss = refs[:n], refs[n:2 * n], refs[2 * n:2 * n + 2 * m]
        for cp in copies(xs, ls, ss[:m], ss[m:]):
            cp.wait_send()
            cp.wait_recv()

    outs = pl.pallas_call(
        body, name=name,
        out_shape=[pltpu.HBM(a.shape, a.dtype) for a in list(x_thru) + list(l_thru)],
        in_specs=[HBM_SPEC] * (2 * n) + [SEM_SPEC] * (2 * m) + [pl.BlockSpec(memory_space=pl.ANY)],
        out_specs=[HBM_SPEC] * (2 * n),
        input_output_aliases={w: w for w in range(2 * n)},
        compiler_params=pltpu.CompilerParams(has_side_effects=DATAFLOW),
    )(*x_thru, *l_thru, *sems, after)
    return outs[:n], outs[n:]


def _gather_forward(gathered, name):
    n = len(gathered)

    def body(*refs):
        outs = refs[n:2 * n]
        send_sems, recv_sems = refs[2 * n:]
        x, y, c = lax.axis_index("x"), lax.axis_index("y"), lax.axis_index("c")
        chips = [(1 - x, y), (x, 1 - y), (1 - x, 1 - y), (x, y)]

        def copy(w, j, pc):
            px, py = chips[j]
            slot = outs[w].at[4 * px + 2 * py + pc]
            return pltpu.make_async_remote_copy(
                src_ref=slot, dst_ref=slot, send_sem=send_sems.at[4 * w + j], recv_sem=recv_sems.at[4 * w + j],
                device_id=(x, y, 1 - c), device_id_type=MESH)

        sends = [copy(w, j, 1 - c if j == 3 else c) for w in range(n) for j in range(4)]
        for cp in sends:
            cp.start()
        for w in range(n):
            for j in range(4):
                copy(w, j, c if j == 3 else 1 - c).wait_recv()
        for cp in sends:
            cp.wait_send()

    any_spec = pl.BlockSpec(memory_space=pl.ANY)
    return pl.pallas_call(
        body, name=name,
        out_shape=[jax.ShapeDtypeStruct(g.shape, g.dtype) for g in gathered],
        in_specs=[any_spec] * n, out_specs=[any_spec] * n,
        input_output_aliases={w: w for w in range(n)},
        scratch_shapes=[pltpu.SemaphoreType.DMA((4 * n,)), pltpu.SemaphoreType.DMA((4 * n,))],
    )(*gathered)


def _forward_copies(gs, ls, send_sems, recv_sems):
    x, y, c = lax.axis_index("x"), lax.axis_index("y"), lax.axis_index("c")
    chips = [(1 - x, y), (x, 1 - y), (1 - x, 1 - y), (x, y)]
    cps = []
    for w in range(len(gs)):
        for j, (px, py) in enumerate(chips):
            slot = gs[w].at[4 * px + 2 * py + (1 - c if j == 3 else c)]
            cps.append(pltpu.make_async_remote_copy(
                src_ref=slot, dst_ref=slot, send_sem=send_sems[4 * w + j], recv_sem=recv_sems[4 * w + j],
                device_id=(x, y, 1 - c), device_id_type=MESH))
    return cps


def _pair_copies(gs, ls, send_sems, recv_sems):
    x, y, c = lax.axis_index("x"), lax.axis_index("y"), lax.axis_index("c")
    return [pltpu.make_async_remote_copy(
        src_ref=gs[w].at[2 * j + 1 - c], dst_ref=ls[w].at[j], send_sem=send_sems[4 * w + j],
        recv_sem=recv_sems[4 * w + j], device_id=(x, y, 1 - c), device_id_type=MESH)
        for w in range(len(gs)) for j in range(4)]


def _all_copies(xs, ls, send_sems, recv_sems):
    x, y, c = lax.axis_index("x"), lax.axis_index("y"), lax.axis_index("c")
    flips = [(a, b, e) for a in (0, 1) for b in (0, 1) for e in (0, 1)][1:]
    return [pltpu.make_async_remote_copy(
        src_ref=xs[0], dst_ref=ls[0].at[4 * x + 2 * y + c], send_sem=send_sems[k], recv_sem=recv_sems[k],
        device_id=(1 - x if a else x, 1 - y if b else y, 1 - c if e else c), device_id_type=MESH)
        for k, (a, b, e) in enumerate(flips)]


def _share_copies(fs, ls, send_sems, recv_sems):
    x, y, c = lax.axis_index("x"), lax.axis_index("y"), lax.axis_index("c")
    return [pltpu.make_async_remote_copy(
        src_ref=fs[w].at[c], dst_ref=fs[w].at[c], send_sem=send_sems[w], recv_sem=recv_sems[w],
        device_id=(x, y, 1 - c), device_id_type=MESH) for w in range(len(fs))]


def _chip_copies(hs, ls, send_sems, recv_sems):
    x, y, c = lax.axis_index("x"), lax.axis_index("y"), lax.axis_index("c")
    chips = [(1 - x, y), (x, 1 - y), (1 - x, 1 - y)]
    return [pltpu.make_async_remote_copy(
        src_ref=hs[w].at[2 * px + py], dst_ref=ls[w].at[k], send_sem=send_sems[3 * w + k],
        recv_sem=recv_sems[3 * w + k], device_id=(px, py, c), device_id_type=MESH)
        for w in range(len(hs)) for k, (px, py) in enumerate(chips)]


def _rs_add_pair(g8, r4, c, name):
    R, n = g8.shape[1:]
    rb = R // 2

    def body(c_ref, g_ref, r_ref, o_ref):
        o_ref[...] = (g_ref[...] + r_ref[...]).astype(BF16)

    spec = pl.BlockSpec((1, rb, n), lambda j, i, c_ref: (j, i, 0))
    return pl.pallas_call(
        body, name=name,
        grid_spec=pltpu.PrefetchScalarGridSpec(
            num_scalar_prefetch=1, grid=(4, R // rb),
            in_specs=[pl.BlockSpec((1, rb, n), lambda j, i, c_ref: (2 * j + c_ref[0], i, 0)), spec],
            out_specs=spec),
        out_shape=jax.ShapeDtypeStruct((4, R, n), BF16), compiler_params=_cp(2))(c, g8, r4)


def _rs_add_chips(g8, r4, r3, cj, name):
    R, n = g8.shape[1:]
    rb = R // 2

    def body(cj_ref, g_ref, p_ref, r_ref, o_ref):
        own = g_ref[0] + p_ref[0]
        o_ref[0] = ((own + r_ref[0].astype(F32)) + r_ref[1].astype(F32)) + r_ref[2].astype(F32)

    return pl.pallas_call(
        body, name=name,
        grid_spec=pltpu.PrefetchScalarGridSpec(
            num_scalar_prefetch=1, grid=(R // rb,),
            in_specs=[pl.BlockSpec((1, rb, n), lambda i, cj_ref: (2 * cj_ref[1] + cj_ref[0], i, 0)),
                      pl.BlockSpec((1, rb, n), lambda i, cj_ref: (cj_ref[1], i, 0)),
                      pl.BlockSpec((3, rb, n), lambda i, cj_ref: (0, i, 0))],
            out_specs=pl.BlockSpec((1, rb, n), lambda i, cj_ref: (cj_ref[0], i, 0))),
        out_shape=jax.ShapeDtypeStruct((2, R, n), F32), compiler_params=_cp(1))(cj, g8, r4, r3)


def _sum8(g):
    n = g.shape[1]

    def body(g_ref, o_ref):
        acc = g_ref[0:1, :]
        for k in range(1, 8):
            acc = acc + g_ref[k:k + 1, :]
        o_ref[...] = acc

    return pl.pallas_call(body, name="sum_devices", out_shape=jax.ShapeDtypeStruct((1, n), F32),
                          compiler_params=pltpu.CompilerParams(vmem_limit_bytes=VMEM_LIMIT))(g)


ADA_ROWS = 64


def _ada_fwd(cs, w, b):
    n = w.shape[1]

    def body(c_ref, w_ref, b_ref, o_ref):
        cv = c_ref[...]
        s = (cv * _sigmoid(cv)).astype(BF16)
        o_ref[...] = _dot(s, w_ref[...].astype(BF16)) + b_ref[...]

    return pl.pallas_call(body, name="ada_fwd", out_shape=jax.ShapeDtypeStruct((ADA_ROWS, n), F32),
                          compiler_params=pltpu.CompilerParams(vmem_limit_bytes=VMEM_LIMIT))(cs, w, b)


def _mod_exchange(m_cols):
    n = m_cols.shape[1]

    def body(m_ref, out_ref, send_sems, recv_sems, local_sem):
        x, y, c = lax.axis_index("x"), lax.axis_index("y"), lax.axis_index("c")
        chips = [(1 - x, y), (x, 1 - y), (1 - x, 1 - y)]

        def group(px, py):
            return m_ref.at[pl.ds(pl.multiple_of(8 * (4 * px + 2 * py + c), 8), 8), :]

        mine = pltpu.make_async_copy(group(x, y), out_ref.at[2 * x + y], local_sem)
        mine.start()
        cps = [pltpu.make_async_remote_copy(
            src_ref=group(px, py), dst_ref=out_ref.at[2 * x + y], send_sem=send_sems.at[k], recv_sem=recv_sems.at[k],
            device_id=(px, py, c), device_id_type=MESH) for k, (px, py) in enumerate(chips)]
        for cp in cps:
            cp.start()
        for cp in cps:
            cp.wait_recv()
        for cp in cps:
            cp.wait_send()
        mine.wait()

    return pl.pallas_call(
        body, name="mod_exchange", out_shape=jax.ShapeDtypeStruct((4, 8, n), F32),
        in_specs=[pl.BlockSpec(memory_space=pltpu.VMEM)], out_specs=pl.BlockSpec(memory_space=pltpu.VMEM),
        scratch_shapes=[pltpu.SemaphoreType.DMA((3,)), pltpu.SemaphoreType.DMA((3,)), pltpu.SemaphoreType.DMA],
        compiler_params=pltpu.CompilerParams(vmem_limit_bytes=VMEM_LIMIT))(m_cols)


def _ada_bwd(cs, w, dm):
    n = w.shape[1]

    def body(c_ref, w_ref, dm_ref, gw_ref, dc_ref):
        cv = c_ref[...]
        s = (cv * _sigmoid(cv)).astype(BF16)
        gw_ref[...] = _dot_tn(s, dm_ref[...].astype(BF16))
        dc_ref[...] = _dot_nt(dm_ref[0:8, :].astype(BF16), w_ref[...].astype(BF16))

    return pl.pallas_call(
        body, name="ada_bwd",
        out_shape=[jax.ShapeDtypeStruct((D, n), F32), jax.ShapeDtypeStruct((8, D), F32)],
        compiler_params=pltpu.CompilerParams(vmem_limit_bytes=VMEM_LIMIT))(cs, w, dm)


def _adamw(w, g, m, v, name):
    r, c = w.shape
    rb = r
    if r % 8 == 0 and r * c * 4 > (1 << 20):
        rb = 8
        for cand in range(8, r, 8):
            if r % cand == 0 and cand * c * 4 <= (1 << 20):
                rb = cand

    def body(w_ref, g_ref, m_ref, v_ref, go_ref, d_ref, nm_ref, nv_ref):
        gv = g_ref[...]
        go_ref[...] = gv
        mn = ADAM_B1 * m_ref[...] + (1.0 - ADAM_B1) * gv
        vn = ADAM_B2 * v_ref[...] + (1.0 - ADAM_B2) * (gv * gv)
        m_hat = mn / (1.0 - ADAM_B1 ** ADAM_STEP)
        v_hat = vn / (1.0 - ADAM_B2 ** ADAM_STEP)
        d_ref[...] = -ADAM_LR * (m_hat / (jnp.sqrt(v_hat) + ADAM_EPS) + ADAM_WD * w_ref[...])
        nm_ref[...] = mn
        nv_ref[...] = vn

    spec = pl.BlockSpec((rb, c), lambda i: (i, 0))
    shp = jax.ShapeDtypeStruct((r, c), F32)
    return pl.pallas_call(body, name=name, grid=(r // rb,), in_specs=[spec] * 4, out_specs=[spec] * 4,
                          out_shape=[shp] * 4, compiler_params=_cp(1))(w, g, m, v)


BIG = ("ffn1_w_in", "ffn1_w_out", "w_mix_in", "w_mix_out", "ffn2_w_in", "ffn2_w_out")


def _half_rows(w, c):
    half = w.shape[0] // 2
    return lax.dynamic_slice_in_dim(w, c * half, half, axis=0)


def _lower_bounds(logits):
    return jnp.cumsum(jax.nn.softmax(logits.astype(F32), axis=1), axis=1)[:, 0]


def kernel(x, c, ctx, c_ctx, w_ada, b_ada, ln_gain, ln_bias, ffn1_w_in, ffn1_w_out, w_mix_in, gla_a2_fwd, gla_a2_bwd, gla_a_bias_fwd, gla_a_bias_bwd, hgrn_lb_logits, gla_norm_gain, hgrn_norm_gain, w_mix_out, ffn2_w_in, ffn2_w_out, loss_target, m_c_ctx, m_w_ada, m_b_ada, m_ln_gain, m_ln_bias, m_ffn1_w_in, m_ffn1_w_out, m_w_mix_in, m_gla_a2_fwd, m_gla_a2_bwd, m_gla_a_bias_fwd, m_gla_a_bias_bwd, m_hgrn_lb_logits, m_gla_norm_gain, m_hgrn_norm_gain, m_w_mix_out, m_ffn2_w_in, m_ffn2_w_out, v_c_ctx, v_w_ada, v_b_ada, v_ln_gain, v_ln_bias, v_ffn1_w_in, v_ffn1_w_out, v_w_mix_in, v_gla_a2_fwd, v_gla_a2_bwd, v_gla_a_bias_fwd, v_gla_a_bias_bwd, v_hgrn_lb_logits, v_gla_norm_gain, v_hgrn_norm_gain, v_w_mix_out, v_ffn2_w_in, v_ffn2_w_out):
    xi, yi, ci = lax.axis_index("x"), lax.axis_index("y"), lax.axis_index("c")
    chip = 2 * xi + yi
    dev = 2 * chip + ci
    B = x.shape[0]
    weights = dict(ffn1_w_in=ffn1_w_in[0], ffn1_w_out=ffn1_w_out[0], w_mix_in=w_mix_in[0], w_mix_out=w_mix_out[0],
                   ffn2_w_in=ffn2_w_in[0], ffn2_w_out=ffn2_w_out[0])

    mine = jnp.concatenate([c.reshape(-1), ln_gain.reshape(-1), ln_bias.reshape(-1), gla_a2_fwd.reshape(-1),
                            gla_a2_bwd.reshape(-1), hgrn_lb_logits.reshape(-1)])
    g1 = _gather_flat(mine, "gather_cond")
    nc = B * D
    c_all = g1[:, :nc].reshape(8 * B, D)
    per_chip = g1[0::2, nc:]
    o = 0

    def take(shape, axis):
        nonlocal o
        n = int(np.prod(shape))
        parts = per_chip[:, o:o + n].reshape((4,) + shape)
        o += n
        return jnp.concatenate([parts[j] for j in range(4)], axis=axis)

    ln_gain_f = take((3, 256), 1)
    ln_bias_f = take((3, 256), 1)
    a2f_f = take((16, 64), 1)
    a2b_f = take((16, 64), 1)
    lbl_f = take((2, 2, 128), 2)
    lb, lb_vjp = jax.vjp(_lower_bounds, lbl_f)

    assert B + 1 <= 8
    cs = jnp.concatenate([c_all.reshape(8, B, D), jnp.broadcast_to(c_ctx.reshape(1, 1, D), (8, 1, D)),
                          jnp.zeros((8, 7 - B, D), F32)], axis=1).reshape(ADA_ROWS, D)
    ncol = w_ada.shape[2]
    b_cols = lax.dynamic_slice_in_dim(b_ada, chip * ncol, ncol, axis=1)
    m4 = _mod_exchange(_ada_fwd(cs, w_ada[0], b_cols))
    m_all = jnp.concatenate([m4[j] for j in range(4)], axis=1)
    m_lat = m_all[:B].reshape(B, 1, N_MOD, D)
    m_ctx = jnp.broadcast_to(m_all[B].reshape(1, 1, N_MOD, D), (B, 1, N_MOD, D))

    groups = dict(ffn2=("ffn2_w_in", "ffn2_w_out"), mix=("w_mix_in", "w_mix_out"), ffn1=("ffn1_w_in", "ffn1_w_out"))
    shards = dict(weights, w_mix_in=jnp.pad(weights["w_mix_in"], ((0, 0), (0, MIX_NP - MIX_N))))
    blks = {k: _half_rows(shards[k], ci).astype(BF16) for k in BIG}
    gather_groups = dict(ffn1_in=("ffn1_w_in",), ffn1_out=("ffn1_w_out",), mix=groups["mix"], ffn2=groups["ffn2"])
    gathering = {}
    token = m_all
    for group, names in gather_groups.items():
        sems, x_thru, l_thru, token = _split_start(_gather_copies, 4, [blks[k] for k in names], 8, token,
                                                   "weight_gather_start_" + group)
        gathering[group] = (sems, x_thru, l_thru)
    mvec = jnp.concatenate([m_ctx, m_lat], axis=1) + token[0, 0]

    forwarding = {}

    def prefetch(group, after):
        _, got = _split_wait(_gather_copies, 4, *gathering[group], after, "weight_gather_wait_" + group)
        forwarding[group] = _split_start(_forward_copies, 4, list(got), 1, None, "weight_gather_forward_start_" + group)
        return forwarding[group][3]

    def weights_for(group, after):
        if group in forwarding:
            sems, g_thru, l_thru, _ = forwarding[group]
            got, _ = _split_wait(_forward_copies, 4, sems, g_thru, l_thru, after, "weight_gather_forward_wait_" + group)
        else:
            _, got = _split_wait(_gather_copies, 4, *gathering[group], after, "weight_gather_wait_" + group)
            got = _gather_forward(got, "weight_gather_forward_" + group)
        full = []
        for k, g in zip(gather_groups[group], got):
            if k == "w_mix_in":
                full.append(_mix_in_to_padded(g.reshape(4, D, MIX_NP)))
            elif k.endswith("w_in"):
                full.append(g.reshape((4,) + shards[k].shape))
            else:
                full.append(g.reshape(-1, D))
        return full

    cvec = ci.reshape(1).astype(jnp.int32)
    cjvec = jnp.stack([ci, chip]).astype(jnp.int32)
    in_flight = {}

    def on_grads(group, gs):
        names = groups[group]
        if group == "mix":
            gs = (_mix_in_from_padded(gs[0]), gs[1])
        g8s = [g.reshape((8, shards[k].shape[0] // 2, shards[k].shape[1])) for k, g in zip(names, gs)]
        sems, g_thru, l_thru, token = _split_start(_pair_copies, 4, g8s, 4, None, "grad_pair_start_" + group)
        in_flight[group] = (sems, g_thru, l_thru)
        return token

    def on_sent(group, after):
        g8s, r4s = _split_wait(_pair_copies, 4, *in_flight[group], after, "grad_pair_wait_" + group)
        h4s = [_rs_add_pair(g, r, cvec, "grad_pair_add_" + k) for k, g, r in zip(groups[group], g8s, r4s)]
        sems, h_thru, l_thru, token = _split_start(_chip_copies, 3, h4s, 3, None, "grad_chip_start_" + group)
        in_flight[group] = (g8s, r4s, sems, h_thru, l_thru)
        return token

    loss_l, grad_x, dm, small = _local_step(
        x, ctx, loss_target, mvec, weights_for, prefetch, ln_gain_f, ln_bias_f, a2f_f, a2b_f,
        gla_a_bias_fwd, gla_a_bias_bwd, lb, gla_norm_gain, hgrn_norm_gain, on_grads, on_sent)
    loss = lax.psum(loss_l, ("x", "y", "c"))

    dm_lat = dm[:, 1].reshape(B, N_MOD * D)
    dm_ctx = jnp.sum(dm[:, 0], axis=0).reshape(N_MOD * D)
    keys = ("ln_gain", "ln_bias", "a2f", "a2b", "abf", "abb", "lb", "gng", "gnh")
    flat = jnp.concatenate([dm_lat.reshape(-1), dm_ctx] + [small[k].reshape(-1) for k in keys])
    nflat = flat.shape[0]
    npad = -(-nflat // 1024) * 1024
    flat2 = jnp.pad(flat, (0, npad - nflat)).reshape(8, npad // 8)
    s_sems, s_src, s_land, tok = _split_start(_all_copies, 7, [flat2], 8, None, "small_grads_start")

    fin = {}
    for group in ("ffn2", "mix", "ffn1"):
        g8s, r4s, sems, h_thru, l_thru = in_flight[group]
        _, r3s = _split_wait(_chip_copies, 3, sems, h_thru, l_thru, tok, "grad_chip_wait_" + group)
        for k, g, r4, r3 in zip(groups[group], g8s, r4s, r3s):
            fin[k] = _rs_add_chips(g, r4, r3, cjvec, "grad_chip_add_" + k)
    p_sems, p_src, p_land, tok = _split_start(_share_copies, 1, [fin[k] for k in BIG], 1, None, "grad_pair_share_start")

    _, (land,) = _split_wait(_all_copies, 7, s_sems, s_src, s_land, tok, "small_grads_wait")
    g3 = lax.dynamic_update_index_in_dim(land, flat2, dev, 0).reshape(8, npad)[:, :nflat]
    nlat = B * N_MOD * D
    dm_all = g3[:, :nlat].reshape(8 * B, N_MOD * D)
    tot = _sum8(g3[:, nlat:])[0]
    dmc_tot = tot[:N_MOD * D]
    o = N_MOD * D
    sg = {}
    for k in keys:
        n = int(np.prod(small[k].shape))
        sg[k] = tot[o:o + n].reshape(small[k].shape)
        o += n
    ctx_rows = jnp.zeros((8, 1, N_MOD * D), F32).at[0, 0].set(dmc_tot)
    dm_rows = jnp.concatenate([dm_all.reshape(8, B, -1), ctx_rows, jnp.zeros((8, 7 - B, N_MOD * D), F32)],
                              axis=1).reshape(ADA_ROWS, N_MOD * D)
    g_b_ada = (jnp.sum(dm_all, axis=0) + dmc_tot).reshape(1, N_MOD * D)
    g_w_ada, dcc = _ada_bwd(cs, w_ada[0], lax.dynamic_slice_in_dim(dm_rows, chip * ncol, ncol, axis=1))
    g4 = _gather_flat(dcc[B], "gather_cctx")
    dsilu = ((g4[0] + g4[2]) + g4[4]) + g4[6]
    sc = _sigmoid(c_ctx)
    g_c_ctx = dsilu * (sc * (1.0 + c_ctx * (1.0 - sc)))
    (g_lbl,) = lb_vjp(sg["lb"])

    def cols(a, n, axis):
        return lax.dynamic_slice_in_dim(a, chip * n, n, axis=axis)

    shared, _ = _split_wait(_share_copies, 1, p_sems, p_src, p_land, g_w_ada, "grad_pair_share_wait")
    gsh = {k: both.reshape(shards[k].shape)[:, :weights[k].shape[1]] for k, both in zip(BIG, shared)}

    grads = dict(
        c_ctx=g_c_ctx, w_ada=g_w_ada[None], b_ada=g_b_ada, ln_gain=cols(sg["ln_gain"], 256, 1)[None],
        ln_bias=cols(sg["ln_bias"], 256, 1)[None], ffn1_w_in=gsh["ffn1_w_in"][None], ffn1_w_out=gsh["ffn1_w_out"][None],
        w_mix_in=gsh["w_mix_in"][None], gla_a2_fwd=cols(sg["a2f"], 64, 1)[None], gla_a2_bwd=cols(sg["a2b"], 64, 1)[None],
        gla_a_bias_fwd=sg["abf"], gla_a_bias_bwd=sg["abb"], hgrn_lb_logits=cols(g_lbl, 128, 2),
        gla_norm_gain=sg["gng"].reshape(1, HD), hgrn_norm_gain=sg["gnh"].reshape(1, HD),
        w_mix_out=gsh["w_mix_out"][None], ffn2_w_in=gsh["ffn2_w_in"][None], ffn2_w_out=gsh["ffn2_w_out"][None])
    params = dict(
        c_ctx=(c_ctx, m_c_ctx, v_c_ctx), w_ada=(w_ada, m_w_ada, v_w_ada), b_ada=(b_ada, m_b_ada, v_b_ada),
        ln_gain=(ln_gain, m_ln_gain, v_ln_gain), ln_bias=(ln_bias, m_ln_bias, v_ln_bias),
        ffn1_w_in=(ffn1_w_in, m_ffn1_w_in, v_ffn1_w_in), ffn1_w_out=(ffn1_w_out, m_ffn1_w_out, v_ffn1_w_out),
        w_mix_in=(w_mix_in, m_w_mix_in, v_w_mix_in), gla_a2_fwd=(gla_a2_fwd, m_gla_a2_fwd, v_gla_a2_fwd),
        gla_a2_bwd=(gla_a2_bwd, m_gla_a2_bwd, v_gla_a2_bwd),
        gla_a_bias_fwd=(gla_a_bias_fwd, m_gla_a_bias_fwd, v_gla_a_bias_fwd),
        gla_a_bias_bwd=(gla_a_bias_bwd, m_gla_a_bias_bwd, v_gla_a_bias_bwd),
        hgrn_lb_logits=(hgrn_lb_logits, m_hgrn_lb_logits, v_hgrn_lb_logits),
        gla_norm_gain=(gla_norm_gain, m_gla_norm_gain, v_gla_norm_gain),
        hgrn_norm_gain=(hgrn_norm_gain, m_hgrn_norm_gain, v_hgrn_norm_gain),
        w_mix_out=(w_mix_out, m_w_mix_out, v_w_mix_out), ffn2_w_in=(ffn2_w_in, m_ffn2_w_in, v_ffn2_w_in),
        ffn2_w_out=(ffn2_w_out, m_ffn2_w_out, v_ffn2_w_out))
    order = list(params.keys())
    big_names = ("w_ada",) + BIG
    upd = {}
    for k in big_names:
        w_, m_, v_ = params[k]
        s2 = w_.shape[-2:]
        g_, d_, nm_, nv_ = _adamw(w_.reshape(s2), grads[k].reshape(s2), m_.reshape(s2), v_.reshape(s2), "adamw_" + k)
        grads[k] = g_
        upd[k] = (d_.reshape(w_.shape), nm_.reshape(w_.shape), nv_.reshape(w_.shape))
    small_names = [k for k in order if k not in big_names]
    sizes = [int(np.prod(params[k][0].shape)) for k in small_names]
    tot_n = sum(sizes)
    npad = -(-tot_n // 1024) * 1024

    def packed(get):
        flat_ = jnp.concatenate([get(k).reshape(-1) for k in small_names])
        return jnp.pad(flat_, (0, npad - tot_n)).reshape(8, npad // 8)

    _, d_s, nm_s, nv_s = _adamw(packed(lambda k: params[k][0]), packed(lambda k: grads[k]),
                                packed(lambda k: params[k][1]), packed(lambda k: params[k][2]), "adamw_small")
    o = 0
    for k, n in zip(small_names, sizes):
        shp = params[k][0].shape
        upd[k] = tuple(a.reshape(-1)[o:o + n].reshape(shp) for a in (d_s, nm_s, nv_s))
        o += n

    return (loss, grad_x, *[grads[k].reshape(params[k][0].shape) for k in order], *[upd[k][0] for k in order],
            *[upd[k][1] for k in order], *[upd[k][2] for k in order])
```

```python
import functools

import numpy as np
import jax
import jax.numpy as jnp
from jax import lax
from jax.experimental import pallas as pl
from jax.experimental.pallas import tpu as pltpu

F32 = jnp.float32
BF16 = jnp.bfloat16
MESH = pl.DeviceIdType.MESH

D = 1024
DFF = 2816
TM = 256
CH = 32
NCB = TM // CH
SB = 128
CSB = SB // CH
NSB = TM // SB
HP = 8
HPB = 8
HD = 128
NH = 8
LN_EPS = 1e-5
NORM_EPS = 1e-6
ALPHA = 2.0 ** 0.25
GATE_NORM = 16.0
GLA_DK = 64
N_MOD = 9
VMEM_LIMIT = 52 * 1024 * 1024

MIXP = 5120
GG, RG, GQ, GK, GV, RQ, RFF, RFB, RI, LR = 0, 512, 1024, 1536, 2048, 2560, 3072, 3584, 4096, 4608
IN_SPLITS = (256, 256, 512, 512, 16, 16, 512, 512, 512, 512, 512)

ADAM_LR, ADAM_B1, ADAM_B2, ADAM_EPS, ADAM_WD, ADAM_STEP = 0.001, 0.9, 0.999, 1e-08, 0.01, 10


def _cp(n_axes):
    return pltpu.CompilerParams(dimension_semantics=("arbitrary",) * n_axes, vmem_limit_bytes=VMEM_LIMIT)


def _rowmap(stride, off):
    return lambda b, i: (b * stride + off + i, 0)


def _mmap(comb):
    if comb:
        return lambda b, i: (b, jnp.minimum(i, 1), 0, 0)
    return lambda b, i: (b, 1, 0, 0)


def _ln(x):
    mu = jnp.mean(x, axis=-1, keepdims=True)
    xc = x - mu
    var = jnp.mean(xc * xc, axis=-1, keepdims=True)
    r = lax.rsqrt(var + LN_EPS)
    return xc * r, r


def _ln_bwd(dxh, xh, r):
    return r * (dxh - jnp.mean(dxh, axis=-1, keepdims=True) - xh * jnp.mean(dxh * xh, axis=-1, keepdims=True))


def _sigmoid(x):
    return 1.0 / (1.0 + jnp.exp(-x))


def _rsum(x):
    return jnp.sum(x, axis=0, keepdims=True)


def _dot(a, b):
    return jnp.dot(a, b, preferred_element_type=F32)


def _dot_nt(a, b):
    return lax.dot_general(a, b, (((1,), (1,)), ((), ())), preferred_element_type=F32)


def _dot_tn(a, b):
    return lax.dot_general(a, b, (((0,), (0,)), ((), ())), preferred_element_type=F32)


def _modulate(xv, m_ref, sub):
    xh, _ = _ln(xv)
    sh = m_ref[0, 0, 3 * sub:3 * sub + 1, :]
    sc = m_ref[0, 0, 3 * sub + 1:3 * sub + 2, :]
    return (xh * (1.0 + sh) + sc).astype(BF16)


def _embed_lnmod(x, ctx, pos, mvec):
    B, T, _ = x.shape
    nt = 1 + T // TM

    def body(x_ref, c_ref, p_ref, m_ref, o_ref, h_ref):
        i = pl.program_id(1)

        @pl.when(i == 0)
        def _():
            o_ref[...] = c_ref[0]

        @pl.when(i > 0)
        def _():
            o_ref[...] = x_ref[0] + p_ref[...]

        h_ref[...] = _modulate(o_ref[...], m_ref, 0)

    rows = pl.BlockSpec((TM, D), lambda b, i: (b * nt + i, 0))
    return pl.pallas_call(
        body, name="embed_lnmod0", grid=(B, nt),
        in_specs=[pl.BlockSpec((1, TM, D), lambda b, i: (b, jnp.maximum(i - 1, 0), 0)),
                  pl.BlockSpec((1, TM, D), lambda b, i: (b, 0, 0)),
                  pl.BlockSpec((TM, D), lambda b, i: (jnp.maximum(i - 1, 0), 0)),
                  pl.BlockSpec((1, 1, N_MOD, D), _mmap(True))],
        out_specs=[rows, rows],
        out_shape=[jax.ShapeDtypeStruct((B * nt * TM, D), F32), jax.ShapeDtypeStruct((B * nt * TM, D), BF16)],
        compiler_params=_cp(2))(x, ctx, pos, mvec)


def _lnmod0_bwd(dh, X, mvec, dres, x_shape, B, nt):
    def body(dh_ref, x_ref, m_ref, dr_ref, dx_ref, dm_ref):
        i = pl.program_id(1)
        xh, r = _ln(x_ref[...])
        sh = m_ref[0, 0, 0:1, :]
        dhv = dh_ref[...].astype(F32)

        @pl.when((i == 0) | (i == 1))
        def _():
            dm_ref[...] = jnp.zeros_like(dm_ref)

        dm_ref[0, 0, 0:1, :] += _rsum(dhv * xh)
        dm_ref[0, 0, 1:2, :] += _rsum(dhv)

        @pl.when(i > 0)
        def _():
            dx_ref[0] = _ln_bwd(dhv * (1.0 + sh), xh, r) + dr_ref[...]

    rows = pl.BlockSpec((TM, D), _rowmap(nt, 0))
    return pl.pallas_call(
        body, name="lnmod0_bwd", grid=(B, nt),
        in_specs=[rows, rows, pl.BlockSpec((1, 1, N_MOD, D), _mmap(True)), rows],
        out_specs=[pl.BlockSpec((1, TM, D), lambda b, i: (b, jnp.maximum(i - 1, 0), 0)),
                   pl.BlockSpec((1, 1, 2, D), _mmap(True))],
        out_shape=[jax.ShapeDtypeStruct(x_shape, F32), jax.ShapeDtypeStruct((B, 2, 2, D), F32)],
        compiler_params=_cp(2))(dh, X, mvec, dres)


def _resid_fwd(x_ref, y_ref, m_ref, gb_ref, sub, w):
    wg = w * m_ref[0, 0, 3 * sub + 2:3 * sub + 3, :]
    y = y_ref[...].astype(F32)
    zh, r = _ln(ALPHA * x_ref[...] + wg * y)
    return y, wg, zh, r


def _resid_modulate(x_ref, y_ref, m_ref, gb_ref, sub, w, xn_ref, hn_ref):
    _, _, zh, _ = _resid_fwd(x_ref, y_ref, m_ref, gb_ref, sub, w)
    xn = zh * gb_ref[0:1, :] + gb_ref[1:2, :]
    xn_ref[...] = xn
    hn = _modulate(xn, m_ref, sub + 1)
    hn_ref[...] = hn
    return hn


def _pre_specs(pre, B, nt):
    rows = pl.BlockSpec((TM, D), _rowmap(nt, 0))
    in_specs = [pl.BlockSpec((TM, D), _rowmap(*pre["lx"])), rows, pl.BlockSpec((1, 1, N_MOD, D), _mmap(pre["comb"])),
                pl.BlockSpec((2, D), lambda b, i: (0, 0))]
    shapes = [jax.ShapeDtypeStruct((B * nt * TM, D), F32), jax.ShapeDtypeStruct((B * nt * TM, D), BF16)]
    return in_specs, [pre["X"], pre["Y"], pre["mvec"], pre["gb"]], [rows, rows], shapes


def _resid_grads(do, y, wg, zh, r, w, gb_ref, comb, dx_ref, dy_ref, dg_ref, dgb_ref):
    b_, i = pl.program_id(0), pl.program_id(1)
    dz = _ln_bwd(do * gb_ref[0:1, :], zh, r)
    dx_ref[...] = ALPHA * dz
    dy_ref[...] = (wg * dz).astype(BF16)

    @pl.when((b_ == 0) & (i == 0))
    def _():
        dgb_ref[...] = jnp.zeros_like(dgb_ref)

    dgb_ref[0:1, :] += _rsum(do * zh)
    dgb_ref[1:2, :] += _rsum(do)

    init = (i == 0) | (i == 1) if comb else (i == 0)

    @pl.when(init)
    def _():
        dg_ref[...] = jnp.zeros_like(dg_ref)

    dg_ref[0, 0] += w * _rsum(dz * y)


def _resid_out_shapes(B, nt, comb):
    rows = pl.BlockSpec((TM, D), _rowmap(nt, 0))
    specs = [rows, rows, pl.BlockSpec((1, 1, 1, D), _mmap(comb)), pl.BlockSpec((2, D), lambda b, i: (0, 0))]
    shapes = [jax.ShapeDtypeStruct((B * nt * TM, D), F32), jax.ShapeDtypeStruct((B * nt * TM, D), BF16),
              jax.ShapeDtypeStruct((B, 2, 1, D), F32), jax.ShapeDtypeStruct((2, D), F32)]
    return specs, shapes


def _tail(X, Y, mvec, gb, tgt, sub, w, B, nt):
    def body(x_ref, y_ref, m_ref, gb_ref, t_ref, dx_ref, dy_ref, dg_ref, dgb_ref, l_ref):
        y, wg, zh, r = _resid_fwd(x_ref, y_ref, m_ref, gb_ref, sub, w)
        e = (zh * gb_ref[0:1, :] + gb_ref[1:2, :]) - t_ref[0]

        @pl.when((pl.program_id(0) == 0) & (pl.program_id(1) == 0))
        def _():
            l_ref[...] = jnp.zeros_like(l_ref)

        l_ref[...] += _rsum(e * e)
        _resid_grads(e * (1.0 / D), y, wg, zh, r, w, gb_ref, False, dx_ref, dy_ref, dg_ref, dgb_ref)

    rows = pl.BlockSpec((TM, D), _rowmap(nt, 0))
    specs, shapes = _resid_out_shapes(B, nt, False)
    return pl.pallas_call(
        body, name="resid2_loss_bwd", grid=(B, nt),
        in_specs=[rows, rows, pl.BlockSpec((1, 1, N_MOD, D), _mmap(False)), pl.BlockSpec((2, D), lambda b, i: (0, 0)),
                  pl.BlockSpec((1, TM, D), lambda b, i: (b, i, 0))],
        out_specs=specs + [pl.BlockSpec((1, D), lambda b, i: (0, 0))],
        out_shape=shapes + [jax.ShapeDtypeStruct((1, D), F32)],
        compiler_params=_cp(2))(X, Y, mvec, gb, tgt)


def _lnmod_resid_bwd(dh, Xi, lxi, dres, Xp, lxp, Yp, mvec, comb, sub, w, gb, B, nt, name):
    ntl = nt - 1 if comb else nt

    def body(dh_ref, xi_ref, dr_ref, xp_ref, yp_ref, m_ref, gb_ref, dx_ref, dy_ref, dg_ref, dgb_ref, dm_ref):
        i = pl.program_id(1)
        xh, r = _ln(xi_ref[...])
        sh = m_ref[0, 0, 3 * sub:3 * sub + 1, :]
        dhv = dh_ref[...].astype(F32)
        dr = dr_ref[...]
        if comb:
            dr = jnp.where(i > 0, dr, 0.0)
        do = _ln_bwd(dhv * (1.0 + sh), xh, r) + dr

        init = (i == 0) | (i == 1) if comb else (i == 0)

        @pl.when(init)
        def _():
            dm_ref[...] = jnp.zeros_like(dm_ref)

        dm_ref[0, 0, 0:1, :] += _rsum(dhv * xh)
        dm_ref[0, 0, 1:2, :] += _rsum(dhv)
        y, wg, zh, r2 = _resid_fwd(xp_ref, yp_ref, m_ref, gb_ref, sub - 1, w)
        _resid_grads(do, y, wg, zh, r2, w, gb_ref, comb, dx_ref, dy_ref, dg_ref, dgb_ref)

    rows = pl.BlockSpec((TM, D), _rowmap(nt, 0))
    if comb:
        dres_spec = pl.BlockSpec((TM, D), lambda b, i: (b * ntl + jnp.maximum(i - 1, 0), 0))
    else:
        dres_spec = rows
    specs, shapes = _resid_out_shapes(B, nt, comb)
    return pl.pallas_call(
        body, name=name, grid=(B, nt),
        in_specs=[rows, pl.BlockSpec((TM, D), _rowmap(*lxi)), dres_spec, pl.BlockSpec((TM, D), _rowmap(*lxp)), rows,
                  pl.BlockSpec((1, 1, N_MOD, D), _mmap(comb)), pl.BlockSpec((2, D), lambda b, i: (0, 0))],
        out_specs=specs + [pl.BlockSpec((1, 1, 2, D), _mmap(comb))],
        out_shape=shapes + [jax.ShapeDtypeStruct((B, 2, 2, D), F32)],
        compiler_params=_cp(2))(dh, Xi, dres, Xp, Yp, mvec, gb)


def _ffn_out_dx(dy, W, u, name, dep=None):
    M = dy.shape[0]
    half = DFF // 2
    deps = [] if dep is None else [dep]

    def body(dy_ref, w_ref, u_ref, *rest):
        du_ref = rest[-1]
        dyv = dy_ref[...]
        for j in range(2):
            lo, hi = j * half, (j + 1) * half
            da = _dot_nt(dyv, w_ref[lo:hi, :])
            g = u_ref[:, lo:hi].astype(F32)
            up = u_ref[:, DFF + lo:DFF + hi].astype(F32)
            s = _sigmoid(g)
            du_ref[:, lo:hi] = (da * up * (s * (1.0 + g * (1.0 - s)))).astype(BF16)
            du_ref[:, DFF + lo:DFF + hi] = (da * (g * s)).astype(BF16)

    return pl.pallas_call(
        body, name=name, grid=(M // TM,),
        in_specs=[pl.BlockSpec((TM, D), lambda i: (i, 0)), _wspec(W, 1), pl.BlockSpec((TM, 2 * DFF), lambda i: (i, 0))]
        + [_wspec(d, 1) for d in deps],
        out_specs=pl.BlockSpec((TM, 2 * DFF), lambda i: (i, 0)),
        out_shape=jax.ShapeDtypeStruct((M, 2 * DFF), BF16), compiler_params=_cp(1))(dy, W, u, *deps)


def _wspec(W, nidx):
    zeros = (0,) * W.ndim
    if nidx == 1:
        return pl.BlockSpec(W.shape, lambda i: zeros)
    return pl.BlockSpec(W.shape, lambda b, i: zeros)


def _mm_nn(A, la, W, B, nt, out_dtype, name):
    K, N = W.shape

    def body(a_ref, w_ref, o_ref):
        o_ref[...] = _dot(a_ref[...], w_ref[...]).astype(out_dtype)

    return pl.pallas_call(
        body, name=name, grid=(B, nt),
        in_specs=[pl.BlockSpec((TM, K), _rowmap(*la)), _wspec(W, 2)],
        out_specs=pl.BlockSpec((TM, N), _rowmap(nt, 0)),
        out_shape=jax.ShapeDtypeStruct((B * nt * TM, N), out_dtype), compiler_params=_cp(2))(A, W)


def _ffn_in(A, la, W3, B, nt, name, pre=None):
    K, n = W3.shape[1:]

    def body(*refs):
        if pre is None:
            a_ref, w_ref, u_ref, s_ref = refs
            a = a_ref[...]
        else:
            x_ref, y_ref, m_ref, gb_ref, w_ref, u_ref, s_ref, xn_ref, hn_ref = refs
            a = _resid_modulate(x_ref, y_ref, m_ref, gb_ref, pre["sub"], pre["w"], xn_ref, hn_ref)
        for j in range(2):
            g = _dot(a, w_ref[j])
            up = _dot(a, w_ref[j + 2])
            u_ref[:, j * n:(j + 1) * n] = g.astype(BF16)
            u_ref[:, (j + 2) * n:(j + 3) * n] = up.astype(BF16)
            s_ref[:, j * n:(j + 1) * n] = (g * _sigmoid(g) * up).astype(BF16)

    rows = B * nt * TM
    if pre is None:
        in_specs, args, xo_specs, xo_shapes = [pl.BlockSpec((TM, K), _rowmap(*la))], [A], [], []
    else:
        in_specs, args, xo_specs, xo_shapes = _pre_specs(pre, B, nt)
    return pl.pallas_call(
        body, name=name, grid=(B, nt),
        in_specs=in_specs + [_wspec(W3, 2)],
        out_specs=[pl.BlockSpec((TM, 4 * n), _rowmap(nt, 0)), pl.BlockSpec((TM, 2 * n), _rowmap(nt, 0))] + xo_specs,
        out_shape=[jax.ShapeDtypeStruct((rows, 4 * n), BF16), jax.ShapeDtypeStruct((rows, 2 * n), BF16)] + xo_shapes,
        compiler_params=_cp(2))(*args, W3)


def _mm_nt(A, W, name, dep=None, out_dtype=F32):
    M, N = A.shape
    K = W.shape[-2]

    def body(a_ref, w_ref, *rest):
        o_ref = rest[-1]
        if W.ndim == 3:
            n = W.shape[-1]
            acc = _dot_nt(a_ref[:, 0:n], w_ref[0])
            for j in range(1, 4):
                acc = acc + _dot_nt(a_ref[:, j * n:(j + 1) * n], w_ref[j])
            o_ref[...] = acc.astype(out_dtype)
        else:
            o_ref[...] = _dot_nt(a_ref[...], w_ref[...]).astype(out_dtype)

    deps = [] if dep is None else [dep]
    return pl.pallas_call(
        body, name=name, grid=(M // TM,),
        in_specs=[pl.BlockSpec((TM, N), lambda i: (i, 0)), _wspec(W, 1)] + [_wspec(d, 1) for d in deps],
        out_specs=pl.BlockSpec((TM, K), lambda i: (i, 0)),
        out_shape=jax.ShapeDtypeStruct((M, K), out_dtype), compiler_params=_cp(1))(A, W, *deps)


def _mm_tn(A, G, name, tn=512, shards=None):
    M, K = A.shape
    N = G.shape[1]
    tk = next(t for t in ((2048, 1536, 1024, 512) if K <= D else (1024, 512)) if M % t == 0)
    if shards:
        tn = N // shards

    def body(a_ref, g_ref, o_ref):
        @pl.when(pl.program_id(1) == 0)
        def _():
            o_ref[...] = jnp.zeros_like(o_ref)

        upd = _dot_tn(a_ref[...], g_ref[...])
        if shards:
            o_ref[0] += upd
        else:
            o_ref[...] += upd

    if shards:
        out_spec = pl.BlockSpec((1, K, tn), lambda n, k: (n, 0, 0))
        out_shape = jax.ShapeDtypeStruct((shards, K, tn), F32)
    else:
        out_spec = pl.BlockSpec((K, tn), lambda n, k: (0, n))
        out_shape = jax.ShapeDtypeStruct((K, N), F32)
    return pl.pallas_call(
        body, name=name, grid=(N // tn, M // tk),
        in_specs=[pl.BlockSpec((tk, K), lambda n, k: (k, 0)), pl.BlockSpec((tk, tn), lambda n, k: (k, n))],
        out_specs=out_spec, out_shape=out_shape, compiler_params=_cp(2))(A, G)


def _logsig(z):
    return jnp.minimum(z, 0.0) - jnp.log(1.0 + jnp.exp(-jnp.abs(z)))


ZW = 2688
ZRQ, ZRFF, ZRFB, ZLR = 1024, 1536, 2048, 2560


def _mix_in_features(H, W, a2p, biasp, lbp, B, nt, pre=None):
    K = W.shape[0]

    def body(*refs):
        if pre is None:
            h_ref, w_ref, a2_ref, bias_ref, lb_ref, q_ref, k_ref, v_ref, g_ref, z_ref = refs
            hv = h_ref[...]
        else:
            (x_ref, y_ref, m_ref, gb_ref, w_ref, a2_ref, bias_ref, lb_ref, q_ref, k_ref, v_ref, g_ref, z_ref,
             xn_ref, hn_ref) = refs
            hv = _resid_modulate(x_ref, y_ref, m_ref, gb_ref, pre["sub"], pre["w"], xn_ref, hn_ref)
        proj = lambda lo, hi: _dot(hv, w_ref[:, lo:hi])
        z_ref[:, 0:2 * 4 * HD] = proj(GG, GG + 2 * 4 * HD)
        lrf = proj(LR, LR + HD)
        z_ref[:, ZLR:ZLR + HD] = lrf
        lr = lrf.astype(BF16)
        lane = lax.broadcasted_iota(jnp.int32, (1, 4 * HD), 1)
        keep = (lane & (HD - 1)) < GLA_DK
        for d in range(2):
            z = _dot(lr, a2_ref[d]) + bias_ref[d:d + 1, :]
            gl = jnp.where(keep, _logsig(z) * (1.0 / GATE_NORM), 0.0)
            for h in range(4):
                g_ref[d, 0, h] = gl[:, h * HD:(h + 1) * HD]
        gq, gk, gv = proj(GQ, GQ + 4 * HD), proj(GK, GK + 4 * HD), proj(GV, GV + 4 * HD)
        for h in range(4):
            sl = slice(h * HD, (h + 1) * HD)
            q_ref[0, h] = gq[:, sl] * (GLA_DK ** -0.5)
            k_ref[0, 0, h] = gk[:, sl]
            k_ref[1, 0, h] = gk[:, sl]
            v_ref[0, h] = gv[:, sl].astype(BF16)
        rqs, ri = proj(RQ, RQ + 4 * HD), proj(RI, RI + 4 * HD)
        z_ref[:, ZRQ:ZRQ + 4 * HD] = rqs
        for h in range(4):
            sl = slice(h * HD, (h + 1) * HD)
            rq = rqs[:, sl]
            q_ref[0, 4 + h] = rq * _sigmoid(rq) * (HD ** -0.5)
            v_ref[0, 4 + h] = ri[:, sl].astype(BF16)
        for d, off, zoff in ((0, RFF, ZRFF), (1, RFB, ZRFB)):
            rf = proj(off, off + 4 * HD)
            z_ref[:, zoff:zoff + 4 * HD] = rf
            for h in range(4):
                sl = slice(h * HD, (h + 1) * HD)
                lb = lb_ref[d:d + 1, sl]
                f = lb + (1.0 - lb) * _sigmoid(rf[:, sl])
                g_ref[d, 0, 4 + h] = jnp.log(f)
                k_ref[d, 0, 4 + h] = 1.0 - f

    one = pl.BlockSpec((1, NH, TM, HD), lambda b, i: (b, 0, i, 0))
    two = pl.BlockSpec((2, 1, NH, TM, HD), lambda b, i: (0, b, 0, i, 0))
    s1 = jax.ShapeDtypeStruct((B, NH, nt * TM, HD), F32)
    s2 = jax.ShapeDtypeStruct((2, B, NH, nt * TM, HD), F32)
    if pre is None:
        in_specs, args, xo_specs, xo_shapes = [pl.BlockSpec((TM, K), _rowmap(nt, 0))], [H], [], []
    else:
        in_specs, args, xo_specs, xo_shapes = _pre_specs(pre, B, nt)
    return pl.pallas_call(
        body, name="mix_in_features", grid=(B, nt),
        in_specs=in_specs + [_wspec(W, 2), pl.BlockSpec((2, HD, 4 * HD), lambda b, i: (0, 0, 0)),
                             pl.BlockSpec((2, 4 * HD), lambda b, i: (0, 0)), pl.BlockSpec((2, 4 * HD), lambda b, i: (0, 0))],
        out_specs=[one, two, one, two, pl.BlockSpec((TM, ZW), _rowmap(nt, 0))] + xo_specs,
        out_shape=[s1, s2, jax.ShapeDtypeStruct(s1.shape, BF16), s2,
                   jax.ShapeDtypeStruct((B * nt * TM, ZW), F32)] + xo_shapes,
        compiler_params=_cp(2))(*args, W, a2p, biasp, lbp)


def _features_bwd(Z, a2p, biasp, lbp, dQ0, dQ1, dK0, dK1, dV0, dV1, dG0, dG1, dgates, W, B):
    nt = Z.shape[0] // (B * TM)

    def body(z_ref, a2_ref, bias_ref, lb_ref, dq0, dq1, dk0, dk1, dv0, dv1, dg0, dg1, dgt_ref, w_ref,
             df_ref, dh_ref, da2_ref, dbias_ref, dlb_ref):
        b_, i = pl.program_id(0), pl.program_id(1)

        @pl.when((b_ == 0) & (i == 0))
        def _():
            da2_ref[...] = jnp.zeros_like(da2_ref)
            dbias_ref[...] = jnp.zeros_like(dbias_ref)
            dlb_ref[...] = jnp.zeros_like(dlb_ref)

        df_ref[:, 0:2 * 4 * HD] = jnp.where(i > 0, dgt_ref[...], 0.0).astype(BF16)
        df_ref[:, LR + HD:] = jnp.zeros((TM, MIXP - LR - HD), BF16)
        lr = z_ref[:, ZLR:ZLR + HD].astype(BF16)
        lane = lax.broadcasted_iota(jnp.int32, (1, 4 * HD), 1)
        keep = (lane & (HD - 1)) < GLA_DK
        dlr = jnp.zeros((TM, HD), F32)
        dgs = (dg0, dg1)
        dks = (dk0, dk1)
        rd = lambda ref, h: ref[0, h].astype(F32)
        for d in range(2):
            z = _dot(lr, a2_ref[d]) + bias_ref[d:d + 1, :]
            dgl = jnp.concatenate([rd(dgs[d], h) for h in range(4)], axis=1)
            dz = jnp.where(keep, dgl * (1.0 / GATE_NORM) * (1.0 - _sigmoid(z)), 0.0)
            dzb = dz.astype(BF16)
            dlr = dlr + _dot_nt(dzb, a2_ref[d])
            da2_ref[d] += _dot_tn(lr, dzb)
            dbias_ref[d:d + 1, :] += _rsum(dz)
        df_ref[:, LR:LR + HD] = dlr.astype(BF16)
        for h in range(4):
            df_ref[:, GQ + h * HD:GQ + (h + 1) * HD] = ((rd(dq0, h) + rd(dq1, h)) * (GLA_DK ** -0.5)).astype(BF16)
            df_ref[:, GK + h * HD:GK + (h + 1) * HD] = (rd(dk0, h) + rd(dk1, h)).astype(BF16)
            df_ref[:, GV + h * HD:GV + (h + 1) * HD] = (rd(dv0, h) + rd(dv1, h)).astype(BF16)
        for h in range(4):
            sl = slice(h * HD, (h + 1) * HD)
            rq = z_ref[:, ZRQ + h * HD:ZRQ + (h + 1) * HD]
            s = _sigmoid(rq)
            dqh = rd(dq0, 4 + h) + rd(dq1, 4 + h)
            df_ref[:, RQ + h * HD:RQ + (h + 1) * HD] = (dqh * (HD ** -0.5) * (s * (1.0 + rq * (1.0 - s)))).astype(BF16)
            df_ref[:, RI + h * HD:RI + (h + 1) * HD] = (rd(dv0, 4 + h) + rd(dv1, 4 + h)).astype(BF16)
            for d, off, zoff in ((0, RFF, ZRFF), (1, RFB, ZRFB)):
                lb = lb_ref[d:d + 1, sl]
                sg = _sigmoid(z_ref[:, zoff + h * HD:zoff + (h + 1) * HD])
                f = lb + (1.0 - lb) * sg
                dff = rd(dgs[d], 4 + h) / f - rd(dks[d], 4 + h)
                df_ref[:, off + h * HD:off + (h + 1) * HD] = (dff * (1.0 - lb) * sg * (1.0 - sg)).astype(BF16)
                dlb_ref[d:d + 1, sl] += _rsum(dff * (1.0 - sg))
        dh_ref[...] = _dot_nt(df_ref[...], w_ref[...]).astype(BF16)

    m0 = lambda b, i: (b, 0, i, 0)
    one = lambda m: pl.BlockSpec((1, NH, TM, HD), m)
    return pl.pallas_call(
        body, name="mix_features_bwd", grid=(B, nt),
        in_specs=[pl.BlockSpec((TM, ZW), _rowmap(nt, 0)), pl.BlockSpec((2, HD, 4 * HD), lambda b, i: (0, 0, 0)),
                  pl.BlockSpec((2, 4 * HD), lambda b, i: (0, 0)), pl.BlockSpec((2, 4 * HD), lambda b, i: (0, 0)),
                  one(m0), one(m0), one(m0), one(m0), one(m0), one(m0), one(m0), one(m0),
                  pl.BlockSpec((TM, D), lambda b, i: (b * (nt - 1) + jnp.maximum(i - 1, 0), 0)), _wspec(W, 2)],
        out_specs=[pl.BlockSpec((TM, MIXP), _rowmap(nt, 0)), pl.BlockSpec((TM, D), _rowmap(nt, 0)),
                   pl.BlockSpec((2, HD, 4 * HD), lambda b, i: (0, 0, 0)),
                   pl.BlockSpec((2, 4 * HD), lambda b, i: (0, 0)), pl.BlockSpec((2, 4 * HD), lambda b, i: (0, 0))],
        out_shape=[jax.ShapeDtypeStruct((B * nt * TM, MIXP), BF16), jax.ShapeDtypeStruct((B * nt * TM, D), BF16),
                   jax.ShapeDtypeStruct((2, HD, 4 * HD), F32),
                   jax.ShapeDtypeStruct((2, 4 * HD), F32), jax.ShapeDtypeStruct((2, 4 * HD), F32)],
        compiler_params=_cp(2))(Z, a2p, biasp, lbp, dQ0, dQ1, dK0, dK1, dV0, dV1, dG0, dG1, dgates, W)


def _chunk_scan(x, rin, fwd):
    acc = x
    sft = 1
    while sft < CH:
        if fwd:
            acc = acc + jnp.where(rin >= sft, pltpu.roll(acc, sft, 0), 0.0)
        else:
            acc = acc + jnp.where(rin < CH - sft, pltpu.roll(acc, TM - sft, 0), 0.0)
        sft *= 2
    return acc


def _chunk_total(x):
    t = jnp.sum(x.reshape(NCB, CH, HD), axis=1, keepdims=True)
    return jnp.broadcast_to(t, (NCB, CH, HD)).reshape(TM, HD)


def _scan_masks(rev):
    rin = lax.broadcasted_iota(jnp.int32, (TM, HD), 0) & (CH - 1)
    ri = lax.broadcasted_iota(jnp.int32, (SB, SB), 0)
    ci = lax.broadcasted_iota(jnp.int32, (SB, SB), 1)
    same = (ri >> 5) == (ci >> 5)
    lo = same & (ri >= ci)
    up = same & (ri <= ci)
    mask, maskT = (up, lo) if rev else (lo, up)
    re = lax.broadcasted_iota(jnp.int32, (SB, CSB * HD), 0) >> 5
    ce = lax.broadcasted_iota(jnp.int32, (SB, CSB * HD), 1) >> 7
    return rin, mask, maskT, re == ce


def _scan_decay(q, k, g, rin, rev):
    b = _chunk_scan(g, rin, not rev)
    xx = _chunk_total(g) - b
    eb = jnp.exp(b)
    return b, xx, eb, q * eb, k * jnp.exp(-b), k * jnp.exp(xx)


def _sub(x, s):
    return x[s * SB:(s + 1) * SB]


def _expand(xb, mexp):
    return jnp.where(mexp, jnp.concatenate([xb] * CSB, axis=1), jnp.zeros((), xb.dtype))


def _own(x, mexp):
    xm = jnp.where(mexp, x, 0.0)
    acc = xm[:, 0:HD]
    for n in range(1, CSB):
        acc = acc + xm[:, n * HD:(n + 1) * HD]
    return acc


def _stack(per_chunk, s):
    return jnp.concatenate(per_chunk[s * CSB:(s + 1) * CSB], axis=1)


def _state_pass(s0, eb, uts, rev):
    order = range(NCB - 1, -1, -1) if rev else range(NCB)
    states = [None] * NCB
    s = s0
    for n in order:
        row = n * CH if rev else n * CH + CH - 1
        states[n] = s
        s = eb[row:row + 1, :] * s + uts[n // CSB][:, (n % CSB) * HD:(n % CSB + 1) * HD]
    return states, s


def _scan_fwd(Q, K, V, G, rev, B):
    nb = Q.shape[2] // TM
    d = 1 if rev else 0
    rmap = (lambda s: jnp.where(s == 0, 0, nb - s)) if rev else (lambda s: s)

    def body(q_ref, k_ref, v_ref, g_ref, o_ref, st_ref, s_scr):
        @pl.when(pl.program_id(2) == 0)
        def _():
            s_scr[...] = jnp.zeros_like(s_scr)

        rin, mask, _, mexp = _scan_masks(rev)

        def head(p, readout):
            s0 = s_scr[p]
            st_ref[0, p, 0] = s0
            _, _, eb, qd, ki, kt = _scan_decay(q_ref[0, p], k_ref[0, 0, p], g_ref[0, 0, p], rin, rev)
            ktb, vb = kt.astype(BF16), v_ref[0, p]
            uts = [_dot_tn(_sub(vb, s), _expand(_sub(ktb, s), mexp)) for s in range(NSB)]
            states, s_new = _state_pass(s0, eb, uts, rev)
            s_scr[p] = s_new
            if not readout:
                return
            qb, kib = qd.astype(BF16), ki.astype(BF16)
            for s in range(NSB):
                a = jnp.where(mask, _dot_nt(_sub(qb, s), _sub(kib, s)), 0.0)
                o_ref[0, p, s * SB:(s + 1) * SB, :] = (
                    _dot(a.astype(BF16), _sub(vb, s))
                    + _dot_nt(_expand(_sub(qb, s), mexp), _stack(states, s).astype(BF16)))

        @pl.when(pl.program_id(2) >= 1)
        def _():
            for p in range(HP):
                head(p, True)

        @pl.when(pl.program_id(2) == 0)
        def _():
            for p in range(HP):
                head(p, False)

    one = pl.BlockSpec((1, HP, TM, HD), lambda b, h, s: (b, h, rmap(s), 0))
    two = pl.BlockSpec((1, 1, HP, TM, HD), lambda b, h, s: (d, b, h, rmap(s), 0))
    return pl.pallas_call(
        body, name="scan_fwd_rev" if rev else "scan_fwd", grid=(B, NH // HP, nb),
        in_specs=[one, two, one, two],
        out_specs=[one, pl.BlockSpec((1, HP, 1, HD, HD), lambda b, h, s: (b, h, s, 0, 0))],
        out_shape=[jax.ShapeDtypeStruct(Q.shape, F32), jax.ShapeDtypeStruct((B, NH, nb, HD, HD), F32)],
        scratch_shapes=[pltpu.VMEM((HP, HD, HD), F32)],
        compiler_params=_cp(3))(Q, K, V, G)


def _scan_bwd(Q, K, V, G, St, dO, rev, B):
    nb = Q.shape[2] // TM
    d = 1 if rev else 0
    smap = lambda t: nb - 1 - t
    rmap = (lambda t: jnp.where(smap(t) == 0, 0, nb - smap(t))) if rev else smap

    def body(q_ref, k_ref, v_ref, g_ref, st_ref, do_ref, dq_ref, dk_ref, dv_ref, dg_ref, ds_scr):
        t = pl.program_id(2)

        @pl.when(t == 0)
        def _():
            ds_scr[...] = jnp.zeros_like(ds_scr)

        is_lat = smap(t) >= 1
        rin, mask, maskT, mexp = _scan_masks(rev)
        for p in range(HPB):
            b, xx, eb, qd, ki, kt = _scan_decay(q_ref[0, p], k_ref[0, 0, p], g_ref[0, 0, p], rin, rev)
            qb, kib, ktb, vb = qd.astype(BF16), ki.astype(BF16), kt.astype(BF16), v_ref[0, p]
            dob = jnp.where(is_lat, do_ref[0, p], 0.0).astype(BF16)
            kt_exps = [_expand(_sub(ktb, s), mexp) for s in range(NSB)]
            uts = [_dot_tn(_sub(vb, s), kt_exps[s]) for s in range(NSB)]
            states, _ = _state_pass(st_ref[0, p, 0], eb, uts, rev)
            gts = [_dot_tn(_sub(dob, s), _expand(_sub(qb, s), mexp)) for s in range(NSB)]
            order = range(NCB) if rev else range(NCB - 1, -1, -1)
            dsp = [None] * NCB
            t2 = [None] * NCB
            dsc = ds_scr[p]
            for n in order:
                row = n * CH if rev else n * CH + CH - 1
                ebl = eb[row:row + 1, :]
                dsp[n] = dsc
                t2[n] = jnp.broadcast_to(ebl * _rsum(states[n] * dsc), (CH, HD))
                dsc = gts[n // CSB][:, (n % CSB) * HD:(n % CSB + 1) * HD] + ebl * dsc
            ds_scr[p] = dsc
            dqds, dkis, dkts = [], [], []
            for s in range(NSB):
                q_s, ki_s, v_s, do_s = _sub(qb, s), _sub(kib, s), _sub(vb, s), _sub(dob, s)
                dspb = _stack(dsp, s).astype(BF16)
                da = jnp.where(mask, _dot_nt(do_s, v_s), 0.0).astype(BF16)
                dat = jnp.where(maskT, _dot_nt(v_s, do_s), 0.0).astype(BF16)
                at = jnp.where(maskT, _dot_nt(ki_s, q_s), 0.0).astype(BF16)
                dqds.append(_dot(da, ki_s) + _own(_dot(do_s, _stack(states, s).astype(BF16)), mexp))
                dkis.append(_dot(dat, q_s))
                dv_ref[0, p, s * SB:(s + 1) * SB, :] = (_dot(at, do_s) + _dot_nt(kt_exps[s], dspb)).astype(BF16)
                dkts.append(_own(_dot(v_s, dspb), mexp))
            dqd, dki, dkt = (jnp.concatenate(parts, axis=0) for parts in (dqds, dkis, dkts))
            z = dkt * kt
            db = dqd * qd - dki * ki
            dq_ref[0, p] = (dqd * eb).astype(BF16)
            dk_ref[0, p] = (dki * jnp.exp(-b) + dkt * jnp.exp(xx)).astype(BF16)
            dg_ref[0, p] = (_chunk_total(db) + (db - z) + _chunk_scan(z - db, rin, not rev)
                            + jnp.concatenate(t2, axis=0)).astype(BF16)

    one = pl.BlockSpec((1, HPB, TM, HD), lambda b, h, t: (b, h, rmap(t), 0))
    two = pl.BlockSpec((1, 1, HPB, TM, HD), lambda b, h, t: (d, b, h, rmap(t), 0))
    lat = pl.BlockSpec((1, HPB, TM, HD), lambda b, h, t: (b, h, jnp.clip(rmap(t) - 1, 0, nb - 2), 0))
    shp = jax.ShapeDtypeStruct(Q.shape, BF16)
    return pl.pallas_call(
        body, name="scan_bwd_rev" if rev else "scan_bwd", grid=(B, NH // HPB, nb),
        in_specs=[one, two, one, two, pl.BlockSpec((1, HPB, 1, HD, HD), lambda b, h, t: (b, h, smap(t), 0, 0)), lat],
        out_specs=[one, one, one, one], out_shape=[shp, shp, shp, shp],
        scratch_shapes=[pltpu.VMEM((HPB, HD, HD), F32)],
        compiler_params=_cp(3))(Q, K, V, G, St, dO)


def _gnorm_mix_out(O0, O1, F, gains, W, B, ntl):
    nt = ntl + 1

    def body(o0_ref, o1_ref, f_ref, gn_ref, w_ref, m_ref, y_ref):
        for h in range(NH):
            o = o0_ref[0, h] + o1_ref[0, h]
            r = lax.rsqrt(jnp.mean(o * o, axis=-1, keepdims=True) + NORM_EPS)
            gn = gn_ref[0:1, :] if h < 4 else gn_ref[1:2, :]
            gt = f_ref[:, h * HD:(h + 1) * HD]
            m_ref[:, h * HD:(h + 1) * HD] = (o * r * gn * (gt * _sigmoid(gt))).astype(BF16)
        y_ref[...] = _dot(m_ref[...], w_ref[...]).astype(BF16)

    ospec = pl.BlockSpec((1, NH, TM, HD), lambda b, i: (b, 0, i + 1, 0))
    rows = pl.BlockSpec((TM, D), _rowmap(ntl, 0))
    shp = jax.ShapeDtypeStruct((B * ntl * TM, D), BF16)
    return pl.pallas_call(
        body, name="gated_norm_mix_out", grid=(B, ntl),
        in_specs=[ospec, ospec, pl.BlockSpec((TM, D), lambda b, i: (b * nt + 1 + i, 0)),
                  pl.BlockSpec((2, HD), lambda b, i: (0, 0)), _wspec(W, 2)],
        out_specs=[rows, rows], out_shape=[shp, shp], compiler_params=_cp(2))(O0, O1, F, gains, W)


def _mix_out_dx_gnorm_bwd(dY, W, O0, O1, F, gains, dep, B, ntl):
    nt = ntl + 1

    def body(dy_ref, w_ref, o0_ref, o1_ref, f_ref, gn_ref, dep_ref, do_ref, dgt_ref, dgn_ref):
        b_, i = pl.program_id(0), pl.program_id(1)

        @pl.when((b_ == 0) & (i == 0))
        def _():
            dgn_ref[...] = jnp.zeros_like(dgn_ref)

        dyv = dy_ref[...]
        for h in range(NH):
            o = o0_ref[0, h] + o1_ref[0, h]
            r = lax.rsqrt(jnp.mean(o * o, axis=-1, keepdims=True) + NORM_EPS)
            y = o * r
            gn = gn_ref[0:1, :] if h < 4 else gn_ref[1:2, :]
            gt = f_ref[:, h * HD:(h + 1) * HD]
            s = _sigmoid(gt)
            dm = _dot_nt(dyv, w_ref[h * HD:(h + 1) * HD, :])
            don = dm * (gt * s)
            dgt_ref[:, h * HD:(h + 1) * HD] = (dm * (y * gn) * (s * (1.0 + gt * (1.0 - s)))).astype(BF16)
            row = 0 if h < 4 else 1
            dgn_ref[row:row + 1, :] += _rsum(don * y)
            dy = don * gn
            do_ref[0, h] = (r * (dy - y * jnp.mean(dy * y, axis=-1, keepdims=True))).astype(BF16)

    ospec = pl.BlockSpec((1, NH, TM, HD), lambda b, i: (b, 0, i + 1, 0))
    return pl.pallas_call(
        body, name="mix_out_dx_gated_norm_bwd", grid=(B, ntl),
        in_specs=[pl.BlockSpec((TM, D), _rowmap(ntl, 0)), _wspec(W, 2), ospec, ospec,
                  pl.BlockSpec((TM, D), lambda b, i: (b * nt + 1 + i, 0)), pl.BlockSpec((2, HD), lambda b, i: (0, 0)),
                  _wspec(dep, 2)],
        out_specs=[pl.BlockSpec((1, NH, TM, HD), lambda b, i: (b, 0, i, 0)), pl.BlockSpec((TM, D), _rowmap(ntl, 0)),
                   pl.BlockSpec((2, HD), lambda b, i: (0, 0))],
        out_shape=[jax.ShapeDtypeStruct((B, NH, ntl * TM, HD), BF16), jax.ShapeDtypeStruct((B * ntl * TM, D), BF16),
                   jax.ShapeDtypeStruct((2, HD), F32)],
        compiler_params=_cp(2))(dY, W, O0, O1, F, gains, dep)


def _sincos_2d(rows, width, dim):
    quarter = dim // 4
    omega = 1.0 / 10000.0 ** (jnp.arange(quarter, dtype=F32) / quarter)

    def emb(n):
        a = jnp.arange(n).astype(F32)[:, None] * omega[None, :]
        return jnp.concatenate([jnp.sin(a), jnp.cos(a)], axis=-1)

    er = jnp.broadcast_to(emb(rows)[:, None, :], (rows, width, dim // 2))
    ec = jnp.broadcast_to(emb(width)[None, :, :], (rows, width, dim // 2))
    return jnp.concatenate([er, ec], axis=-1).reshape(rows * width, dim)


def _pad_heads(w):
    k = w.shape[0]
    return jnp.pad(w.reshape(k, 4, GLA_DK), ((0, 0), (0, 0), (0, HD - GLA_DK))).reshape(k, 4 * HD)


def _unpad_heads(w):
    k = w.shape[0]
    return w.reshape(k, 4, HD)[:, :, :GLA_DK].reshape(k, 4 * GLA_DK)


MIX_N = 1032
MIX_NP = 1152
_SEGS = ([(64 * h, 64, GQ + HD * h) for h in range(4)] + [(256 + 64 * h, 64, GK + HD * h) for h in range(4)]
         + [(512, 512, GV), (1024, 512, GG), (1536, 32, LR), (1568, 512, RQ), (2080, 512, RFF), (2592, 512, RFB),
            (3104, 512, RI), (3616, 512, RG)])


def _mix_in_to_padded(ps):
    k = ps.shape[1]
    parts, pos = [], 0
    for g0, ln, s0 in sorted(_SEGS, key=lambda s: s[2]):
        if s0 > pos:
            parts.append(jnp.zeros((k, s0 - pos), ps.dtype))
        for j in range(4):
            lo, hi = max(g0, j * MIX_N), min(g0 + ln, (j + 1) * MIX_N)
            if lo < hi:
                parts.append(ps[j][:, lo - j * MIX_N:hi - j * MIX_N])
        pos = s0 + ln
    parts.append(jnp.zeros((k, MIXP - pos), ps.dtype))
    return jnp.concatenate(parts, axis=1)


def _mix_in_from_padded(g):
    k = g.shape[0]
    shards = []
    for j in range(4):
        parts = []
        for g0, ln, s0 in sorted(_SEGS):
            lo, hi = max(g0, j * MIX_N), min(g0 + ln, (j + 1) * MIX_N)
            if lo < hi:
                parts.append(g[:, s0 + lo - g0:s0 + hi - g0])
        parts.append(jnp.zeros((k, MIX_NP - MIX_N), g.dtype))
        shards.append(jnp.concatenate(parts, axis=1))
    return jnp.stack(shards)


def _local_step(x, ctx, tgt, mvec, weights_for, prefetch, ln_gain, ln_bias, a2f, a2b, abf, abb, lb, gng, gnh, on_grads, on_sent):
    B, T, _ = x.shape
    assert ctx.shape[1] == TM and T % TM == 0
    ntl = T // TM
    nt = ntl + 1
    C, L, CL = (nt, 0), (ntl, 0), (nt, 1)
    pos = _sincos_2d(T // 64, 64, D)
    gbs = [jnp.stack([ln_gain[i], ln_bias[i]]) for i in range(3)]
    a2p = jnp.zeros((2, HD, 4 * HD), F32)
    a2p = a2p.at[0, 0:16].set(_pad_heads(a2f)).at[1, 16:32].set(_pad_heads(a2b)).astype(BF16)
    biasp = jnp.concatenate([_pad_heads(abf.reshape(1, -1)), _pad_heads(abb.reshape(1, -1))], axis=0)
    gains = jnp.concatenate([gng.reshape(1, HD), gnh.reshape(1, HD)], axis=0)

    X0, h0 = _embed_lnmod(x, ctx, pos, mvec)
    w1i, w1o = weights_for("ffn1", h0)
    u0, a0 = _ffn_in(h0, C, w1i, B, nt, "ffn1_in")
    tok = prefetch("mix", a0)
    y0 = _mm_nn(a0, C, w1o, B, nt, BF16, "ffn1_out")
    wmp, wmo = weights_for("mix", y0)
    Q, K, V, G, Zm, X1, h1 = _mix_in_features(None, wmp, a2p, biasp, lb, B, nt, pre=dict(
        X=X0, lx=C, Y=y0, mvec=mvec, comb=True, sub=0, w=0.5, gb=gbs[0] + tok[0, 0]))
    prefetch("ffn2", Zm)
    O0, S0 = _scan_fwd(Q, K, V, G, False, B)
    O1, S1 = _scan_fwd(Q, K, V, G, True, B)
    merged, y1 = _gnorm_mix_out(O0, O1, Zm, gains, wmo, B, ntl)
    w2i, w2o = weights_for("ffn2", y1)
    u2, a2, X2, h2 = _ffn_in(None, L, w2i, B, ntl, "ffn2_in", pre=dict(
        X=X1, lx=CL, Y=y1, mvec=mvec, comb=False, sub=1, w=1.0, gb=gbs[1]))
    y2 = _mm_nn(a2, L, w2o, B, ntl, BF16, "ffn2_out")

    dx2r, dy2, dgate2, dgb2, lsum = _tail(X2, y2, mvec, gbs[2], tgt, 2, 0.5, B, ntl)
    loss = (0.5 / D) * jnp.sum(lsum)
    du2 = _ffn_out_dx(dy2, w2o, u2, "ffn2_out_dx")
    g_w2o = _mm_tn(a2, dy2, "ffn2_out_dw")
    dh2 = _mm_nt(du2, w2i, "ffn2_in_dx", out_dtype=BF16)
    g_w2i = _mm_tn(h2, du2, "ffn2_in_dw", shards=4)
    tok = on_grads("ffn2", (g_w2i, g_w2o))
    dx1r, dy1, dgate1, dgb1, dss2 = _lnmod_resid_bwd(dh2, X2, L, dx2r, X1, CL, y1, mvec, False, 2, 1.0,
                                                     gbs[1] + tok[0, 0], B, ntl, "lnmod2_resid1_bwd")
    tok = on_sent("ffn2", dy1)
    g_wmo = _mm_tn(merged, dy1, "mix_out_dw", tn=D)
    dO, dgates, dgains = _mix_out_dx_gnorm_bwd(dy1, wmo, O0, O1, Zm, gains, tok, B, ntl)
    dQ0, dK0, dV0, dG0 = _scan_bwd(Q, K, V, G, S0, dO, False, B)
    dQ1, dK1, dV1, dG1 = _scan_bwd(Q, K, V, G, S1, dO, True, B)
    dF, dh1, da2p, dbiasp, dlb = _features_bwd(Zm, a2p, biasp, lb, dQ0, dQ1, dK0, dK1, dV0, dV1, dG0, dG1, dgates,
                                               wmp, B)
    g_wmp = _mm_tn(h1, dF, "mix_in_dw", tn=MIXP // 4)
    tok = on_grads("mix", (g_wmp, g_wmo))
    dx0r, dy0, dgate0, dgb0, dss1 = _lnmod_resid_bwd(dh1, X1, C, dx1r, X0, C, y0, mvec, True, 1, 0.5,
                                                     gbs[0] + tok[0, 0], B, nt, "lnmod1_resid0_bwd")
    tok = on_sent("mix", dy0)
    du0 = _ffn_out_dx(dy0, w1o, u0, "ffn1_out_dx", dep=tok)
    g_w1o = _mm_tn(a0, dy0, "ffn1_out_dw")
    g_w1i = _mm_tn(h0, du0, "ffn1_in_dw", shards=4)
    tok = on_grads("ffn1", (g_w1i, g_w1o))
    dh0 = _mm_nt(du0, w1i, "ffn1_in_dx", dep=tok, out_dtype=BF16)
    tok = on_sent("ffn1", dh0)
    grad_x, dss0 = _lnmod0_bwd(dh0, X0, mvec + tok[0, 0], dx0r, x.shape, B, nt)

    zero_ctx = lambda a: a.at[:, 0].set(0.0)
    dm = jnp.concatenate([dss0, dgate0, dss1, zero_ctx(dgate1), zero_ctx(dss2), zero_ctx(dgate2)], axis=2)
    small = dict(
        ln_gain=jnp.stack([dgb0[0], dgb1[0], dgb2[0]]), ln_bias=jnp.stack([dgb0[1], dgb1[1], dgb2[1]]),
        a2f=_unpad_heads(da2p[0, 0:16]), a2b=_unpad_heads(da2p[1, 16:32]),
        abf=_unpad_heads(dbiasp[0:1]), abb=_unpad_heads(dbiasp[1:2]), lb=dlb, gng=dgains[0], gnh=dgains[1])
    return loss, grad_x, dm, small


def _small_allgather(xs, name):
    r, n = xs.shape

    def body(x_ref, out_ref, send_sems, recv_sems, local_sem):
        x, y, c = lax.axis_index("x"), lax.axis_index("y"), lax.axis_index("c")
        me, sibling = (x, y, c), (x, y, 1 - c)
        chips = [(1 - x, y), (x, 1 - y), (1 - x, 1 - y)]

        def rows(px, py, pc):
            return out_ref.at[pl.ds((4 * px + 2 * py + pc) * r, r), :]

        def copy(k, block, to, src=None):
            return pltpu.make_async_remote_copy(
                src_ref=rows(*block) if src is None else src, dst_ref=rows(*block),
                send_sem=send_sems.at[k], recv_sem=recv_sems.at[k], device_id=to, device_id_type=MESH)

        mine = pltpu.make_async_copy(x_ref, rows(*me), local_sem)
        mine.start()
        first = [copy(0, me, sibling, src=x_ref)]
        first += [copy(1 + j, me, (*chip, c), src=x_ref) for j, chip in enumerate(chips)]
        for cp in first:
            cp.start()
        passed = [copy(4 + j, (*chip, c), sibling) for j, chip in enumerate(chips)]
        for j, chip in enumerate(chips):
            copy(1 + j, (*chip, c), me).wait_recv()
            passed[j].start()
        copy(0, sibling, me).wait_recv()
        for j, chip in enumerate(chips):
            copy(4 + j, (*chip, 1 - c), me).wait_recv()
        for cp in first + passed:
            cp.wait_send()
        mine.wait()

    out = pl.pallas_call(
        body, name=name,
        out_shape=jax.ShapeDtypeStruct((8 * r, n), xs.dtype),
        in_specs=[pl.BlockSpec(memory_space=pltpu.VMEM)],
        out_specs=pl.BlockSpec(memory_space=pltpu.VMEM),
        scratch_shapes=[pltpu.SemaphoreType.DMA((7,)), pltpu.SemaphoreType.DMA((7,)), pltpu.SemaphoreType.DMA],
        compiler_params=pltpu.CompilerParams(vmem_limit_bytes=VMEM_LIMIT))(xs)
    return out.reshape(8, r, n)


def _gather_flat(v, name):
    n = v.shape[0]
    npad = -(-n // 1024) * 1024
    g = _small_allgather(jnp.pad(v, (0, npad - n)).reshape(8, npad // 8), name)
    return g.reshape(8, npad)[:, :n]


HBM_SPEC = pl.BlockSpec(memory_space=pltpu.HBM)
SEM_SPEC = pl.BlockSpec(memory_space=pltpu.SEMAPHORE)
DATAFLOW = pltpu.SideEffectType.DATAFLOW_SIDE_EFFECTING


def _gather_copies(xs, outs, send_sems, recv_sems):
    x, y, c = lax.axis_index("x"), lax.axis_index("y"), lax.axis_index("c")
    dests = [(x, y, 1 - c), (1 - x, y, c), (x, 1 - y, c), (1 - x, 1 - y, c)]
    return [pltpu.make_async_remote_copy(
        src_ref=xs[w], dst_ref=outs[w].at[4 * x + 2 * y + c], send_sem=send_sems[4 * w + k],
        recv_sem=recv_sems[4 * w + k], device_id=dests[k], device_id_type=MESH)
        for w in range(len(xs)) for k in range(4)]


def _split_start(copies, per_w, srcs, land_lead, after, name):
    n = len(srcs)
    m = per_w * n
    lands = [lax.empty((land_lead,) + s.shape[-2:], s.dtype) for s in srcs]
    deps = [] if after is None else [after]

    def body(*refs):
        xs, ls, outs = refs[:n], refs[n:2 * n], refs[2 * n + len(deps):]
        for cp in copies(xs, ls, outs[:m], outs[m:2 * m]):
            cp.start()
        token = outs[2 * m + 2 * n]
        token[...] = jnp.zeros_like(token)

    outs = pl.pallas_call(
        body, name=name,
        out_shape=([pltpu.SemaphoreType.DMA(())] * (2 * m) + [pltpu.HBM(a.shape, a.dtype) for a in srcs + lands]
                   + [jax.ShapeDtypeStruct((8, 128), F32)]),
        in_specs=[HBM_SPEC] * (2 * n) + [pl.BlockSpec(memory_space=pl.ANY)] * len(deps),
        out_specs=[SEM_SPEC] * (2 * m) + [HBM_SPEC] * (2 * n) + [pl.BlockSpec(memory_space=pltpu.VMEM)],
        input_output_aliases={w: 2 * m + w for w in range(2 * n)},
        compiler_params=pltpu.CompilerParams(has_side_effects=DATAFLOW),
    )(*[pltpu.with_memory_space_constraint(a, pltpu.HBM) for a in srcs + lands], *deps)
    return outs[:2 * m], outs[2 * m:2 * m + n], outs[2 * m + n:2 * m + 2 * n], outs[2 * m + 2 * n]


def _split_wait(copies, per_w, sems, x_thru, l_thru, after, name):
    n = len(x_thru)
    m = per_w * n

    def body(*refs):
        xs, ls, ss = refs[:n], refs[n:2 * n], refs[2 * n:2 * n + 2 * m]
        for cp in copies(xs, ls, ss[:m], ss[m:]):
            cp.wait_send()
            cp.wait_recv()

    outs = pl.pallas_call(
        body, name=name,
        out_shape=[pltpu.HBM(a.shape, a.dtype) for a in list(x_thru) + list(l_thru)],
        in_specs=[HBM_SPEC] * (2 * n) + [SEM_SPEC] * (2 * m) + [pl.BlockSpec(memory_space=pl.ANY)],
        out_specs=[HBM_SPEC] * (2 * n),
        input_output_aliases={w: w for w in range(2 * n)},
        compiler_params=pltpu.CompilerParams(has_side_effects=DATAFLOW),
    )(*x_thru, *l_thru, *sems, after)
    return outs[:n], outs[n:]


def _gather_forward(gathered, name):
    n = len(gathered)

    def body(*refs):
        outs = refs[n:2 * n]
        send_sems, recv_sems = refs[2 * n:]
        x, y, c = lax.axis_index("x"), lax.axis_index("y"), lax.axis_index("c")
        chips = [(1 - x, y), (x, 1 - y), (1 - x, 1 - y), (x, y)]

        def copy(w, j, pc):
            px, py = chips[j]
            slot = outs[w].at[4 * px + 2 * py + pc]
            return pltpu.make_async_remote_copy(
                src_ref=slot, dst_ref=slot, send_sem=send_sems.at[4 * w + j], recv_sem=recv_sems.at[4 * w + j],
                device_id=(x, y, 1 - c), device_id_type=MESH)

        sends = [copy(w, j, 1 - c if j == 3 else c) for w in range(n) for j in range(4)]
        for cp in sends:
            cp.start()
        for w in range(n):
            for j in range(4):
                copy(w, j, c if j == 3 else 1 - c).wait_recv()
        for cp in sends:
            cp.wait_send()

    any_spec = pl.BlockSpec(memory_space=pl.ANY)
    return pl.pallas_call(
        body, name=name,
        out_shape=[jax.ShapeDtypeStruct(g.shape, g.dtype) for g in gathered],
        in_specs=[any_spec] * n, out_specs=[any_spec] * n,
        input_output_aliases={w: w for w in range(n)},
        scratch_shapes=[pltpu.SemaphoreType.DMA((4 * n,)), pltpu.SemaphoreType.DMA((4 * n,))],
    )(*gathered)


def _forward_copies(gs, ls, send_sems, recv_sems):
    x, y, c = lax.axis_index("x"), lax.axis_index("y"), lax.axis_index("c")
    chips = [(1 - x, y), (x, 1 - y), (1 - x, 1 - y), (x, y)]
    cps = []
    for w in range(len(gs)):
        for j, (px, py) in enumerate(chips):
            slot = gs[w].at[4 * px + 2 * py + (1 - c if j == 3 else c)]
            cps.append(pltpu.make_async_remote_copy(
                src_ref=slot, dst_ref=slot, send_sem=send_sems[4 * w + j], recv_sem=recv_sems[4 * w + j],
                device_id=(x, y, 1 - c), device_id_type=MESH))
    return cps


def _pair_copies(gs, ls, send_sems, recv_sems):
    x, y, c = lax.axis_index("x"), lax.axis_index("y"), lax.axis_index("c")
    return [pltpu.make_async_remote_copy(
        src_ref=gs[w].at[2 * j + 1 - c], dst_ref=ls[w].at[j], send_sem=send_sems[4 * w + j],
        recv_sem=recv_sems[4 * w + j], device_id=(x, y, 1 - c), device_id_type=MESH)
        for w in range(len(gs)) for j in range(4)]


def _all_copies(xs, ls, send_sems, recv_sems):
    x, y, c = lax.axis_index("x"), lax.axis_index("y"), lax.axis_index("c")
    flips = [(a, b, e) for a in (0, 1) for b in (0, 1) for e in (0, 1)][1:]
    return [pltpu.make_async_remote_copy(
        src_ref=xs[0], dst_ref=ls[0].at[4 * x + 2 * y + c], send_sem=send_sems[k], recv_sem=recv_sems[k],
        device_id=(1 - x if a else x, 1 - y if b else y, 1 - c if e else c), device_id_type=MESH)
        for k, (a, b, e) in enumerate(flips)]


def _share_copies(fs, ls, send_sems, recv_sems):
    x, y, c = lax.axis_index("x"), lax.axis_index("y"), lax.axis_index("c")
    return [pltpu.make_async_remote_copy(
        src_ref=fs[w].at[c], dst_ref=fs[w].at[c], send_sem=send_sems[w], recv_sem=recv_sems[w],
        device_id=(x, y, 1 - c), device_id_type=MESH) for w in range(len(fs))]


def _chip_copies(hs, ls, send_sems, recv_sems):
    x, y, c = lax.axis_index("x"), lax.axis_index("y"), lax.axis_index("c")
    chips = [(1 - x, y), (x, 1 - y), (1 - x, 1 - y)]
    return [pltpu.make_async_remote_copy(
        src_ref=hs[w].at[2 * px + py], dst_ref=ls[w].at[k], send_sem=send_sems[3 * w + k],
        recv_sem=recv_sems[3 * w + k], device_id=(px, py, c), device_id_type=MESH)
        for w in range(len(hs)) for k, (px, py) in enumerate(chips)]


def _rs_add_pair(g8, r4, c, name):
    R, n = g8.shape[1:]
    rb = R // 2

    def body(c_ref, g_ref, r_ref, o_ref):
        o_ref[...] = (g_ref[...] + r_ref[...]).astype(BF16)

    spec = pl.BlockSpec((1, rb, n), lambda j, i, c_ref: (j, i, 0))
    return pl.pallas_call(
        body, name=name,
        grid_spec=pltpu.PrefetchScalarGridSpec(
            num_scalar_prefetch=1, grid=(4, R // rb),
            in_specs=[pl.BlockSpec((1, rb, n), lambda j, i, c_ref: (2 * j + c_ref[0], i, 0)), spec],
            out_specs=spec),
        out_shape=jax.ShapeDtypeStruct((4, R, n), BF16), compiler_params=_cp(2))(c, g8, r4)


def _rs_add_chips(g8, r4, r3, cj, name):
    R, n = g8.shape[1:]
    rb = R // 2

    def body(cj_ref, g_ref, p_ref, r_ref, o_ref):
        own = g_ref[0] + p_ref[0]
        o_ref[0] = ((own + r_ref[0].astype(F32)) + r_ref[1].astype(F32)) + r_ref[2].astype(F32)

    return pl.pallas_call(
        body, name=name,
        grid_spec=pltpu.PrefetchScalarGridSpec(
            num_scalar_prefetch=1, grid=(R // rb,),
            in_specs=[pl.BlockSpec((1, rb, n), lambda i, cj_ref: (2 * cj_ref[1] + cj_ref[0], i, 0)),
                      pl.BlockSpec((1, rb, n), lambda i, cj_ref: (cj_ref[1], i, 0)),
                      pl.BlockSpec((3, rb, n), lambda i, cj_ref: (0, i, 0))],
            out_specs=pl.BlockSpec((1, rb, n), lambda i, cj_ref: (cj_ref[0], i, 0))),
        out_shape=jax.ShapeDtypeStruct((2, R, n), F32), compiler_params=_cp(1))(cj, g8, r4, r3)


def _sum8(g):
    n = g.shape[1]

    def body(g_ref, o_ref):
        acc = g_ref[0:1, :]
        for k in range(1, 8):
            acc = acc + g_ref[k:k + 1, :]
        o_ref[...] = acc

    return pl.pallas_call(body, name="sum_devices", out_shape=jax.ShapeDtypeStruct((1, n), F32),
                          compiler_params=pltpu.CompilerParams(vmem_limit_bytes=VMEM_LIMIT))(g)


ADA_ROWS = 64


def _ada_fwd(cs, w, b):
    n = w.shape[1]

    def body(c_ref, w_ref, b_ref, o_ref):
        cv = c_ref[...]
        s = (cv * _sigmoid(cv)).astype(BF16)
        o_ref[...] = _dot(s, w_ref[...].astype(BF16)) + b_ref[...]

    return pl.pallas_call(body, name="ada_fwd", out_shape=jax.ShapeDtypeStruct((ADA_ROWS, n), F32),
                          compiler_params=pltpu.CompilerParams(vmem_limit_bytes=VMEM_LIMIT))(cs, w, b)


def _mod_exchange(m_cols):
    n = m_cols.shape[1]

    def body(m_ref, out_ref, send_sems, recv_sems, local_sem):
        x, y, c = lax.axis_index("x"), lax.axis_index("y"), lax.axis_index("c")
        chips = [(1 - x, y), (x, 1 - y), (1 - x, 1 - y)]

        def group(px, py):
            return m_ref.at[pl.ds(pl.multiple_of(8 * (4 * px + 2 * py + c), 8), 8), :]

        mine = pltpu.make_async_copy(group(x, y), out_ref.at[2 * x + y], local_sem)
        mine.start()
        cps = [pltpu.make_async_remote_copy(
            src_ref=group(px, py), dst_ref=out_ref.at[2 * x + y], send_sem=send_sems.at[k], recv_sem=recv_sems.at[k],
            device_id=(px, py, c), device_id_type=MESH) for k, (px, py) in enumerate(chips)]
        for cp in cps:
            cp.start()
        for cp in cps:
            cp.wait_recv()
        for cp in cps:
            cp.wait_send()
        mine.wait()

    return pl.pallas_call(
        body, name="mod_exchange", out_shape=jax.ShapeDtypeStruct((4, 8, n), F32),
        in_specs=[pl.BlockSpec(memory_space=pltpu.VMEM)], out_specs=pl.BlockSpec(memory_space=pltpu.VMEM),
        scratch_shapes=[pltpu.SemaphoreType.DMA((3,)), pltpu.SemaphoreType.DMA((3,)), pltpu.SemaphoreType.DMA],
        compiler_params=pltpu.CompilerParams(vmem_limit_bytes=VMEM_LIMIT))(m_cols)


def _ada_bwd(cs, w, dm):
    n = w.shape[1]

    def body(c_ref, w_ref, dm_ref, gw_ref, dc_ref):
        cv = c_ref[...]
        s = (cv * _sigmoid(cv)).astype(BF16)
        gw_ref[...] = _dot_tn(s, dm_ref[...].astype(BF16))
        dc_ref[...] = _dot_nt(dm_ref[0:8, :].astype(BF16), w_ref[...].astype(BF16))

    return pl.pallas_call(
        body, name="ada_bwd",
        out_shape=[jax.ShapeDtypeStruct((D, n), F32), jax.ShapeDtypeStruct((8, D), F32)],
        compiler_params=pltpu.CompilerParams(vmem_limit_bytes=VMEM_LIMIT))(cs, w, dm)


def _adamw(w, g, m, v, name):
    r, c = w.shape
    rb = r
    if r % 8 == 0 and r * c * 4 > (1 << 20):
        rb = 8
        for cand in range(8, r, 8):
            if r % cand == 0 and cand * c * 4 <= (1 << 20):
                rb = cand

    def body(w_ref, g_ref, m_ref, v_ref, go_ref, d_ref, nm_ref, nv_ref):
        gv = g_ref[...]
        go_ref[...] = gv
        mn = ADAM_B1 * m_ref[...] + (1.0 - ADAM_B1) * gv
        vn = ADAM_B2 * v_ref[...] + (1.0 - ADAM_B2) * (gv * gv)
        m_hat = mn / (1.0 - ADAM_B1 ** ADAM_STEP)
        v_hat = vn / (1.0 - ADAM_B2 ** ADAM_STEP)
        d_ref[...] = -ADAM_LR * (m_hat / (jnp.sqrt(v_hat) + ADAM_EPS) + ADAM_WD * w_ref[...])
        nm_ref[...] = mn
        nv_ref[...] = vn

    spec = pl.BlockSpec((rb, c), lambda i: (i, 0))
    shp = jax.ShapeDtypeStruct((r, c), F32)
    return pl.pallas_call(body, name=name, grid=(r // rb,), in_specs=[spec] * 4, out_specs=[spec] * 4,
                          out_shape=[shp] * 4, compiler_params=_cp(1))(w, g, m, v)


BIG = ("ffn1_w_in", "ffn1_w_out", "w_mix_in", "w_mix_out", "ffn2_w_in", "ffn2_w_out")


def _half_rows(w, c):
    half = w.shape[0] // 2
    return lax.dynamic_slice_in_dim(w, c * half, half, axis=0)


def _lower_bounds(logits):
    return jnp.cumsum(jax.nn.softmax(logits.astype(F32), axis=1), axis=1)[:, 0]


def kernel(x, c, ctx, c_ctx, w_ada, b_ada, ln_gain, ln_bias, ffn1_w_in, ffn1_w_out, w_mix_in, gla_a2_fwd, gla_a2_bwd, gla_a_bias_fwd, gla_a_bias_bwd, hgrn_lb_logits, gla_norm_gain, hgrn_norm_gain, w_mix_out, ffn2_w_in, ffn2_w_out, loss_target, m_c_ctx, m_w_ada, m_b_ada, m_ln_gain, m_ln_bias, m_ffn1_w_in, m_ffn1_w_out, m_w_mix_in, m_gla_a2_fwd, m_gla_a2_bwd, m_gla_a_bias_fwd, m_gla_a_bias_bwd, m_hgrn_lb_logits, m_gla_norm_gain, m_hgrn_norm_gain, m_w_mix_out, m_ffn2_w_in, m_ffn2_w_out, v_c_ctx, v_w_ada, v_b_ada, v_ln_gain, v_ln_bias, v_ffn1_w_in, v_ffn1_w_out, v_w_mix_in, v_gla_a2_fwd, v_gla_a2_bwd, v_gla_a_bias_fwd, v_gla_a_bias_bwd, v_hgrn_lb_logits, v_gla_norm_gain, v_hgrn_norm_gain, v_w_mix_out, v_ffn2_w_in, v_ffn2_w_out):
    xi, yi, ci = lax.axis_index("x"), lax.axis_index("y"), lax.axis_index("c")
    chip = 2 * xi + yi
    dev = 2 * chip + ci
    B = x.shape[0]
    weights = dict(ffn1_w_in=ffn1_w_in[0], ffn1_w_out=ffn1_w_out[0], w_mix_in=w_mix_in[0], w_mix_out=w_mix_out[0],
                   ffn2_w_in=ffn2_w_in[0], ffn2_w_out=ffn2_w_out[0])

    mine = jnp.concatenate([c.reshape(-1), ln_gain.reshape(-1), ln_bias.reshape(-1), gla_a2_fwd.reshape(-1),
                            gla_a2_bwd.reshape(-1), hgrn_lb_logits.reshape(-1)])
    g1 = _gather_flat(mine, "gather_cond")
    nc = B * D
    c_all = g1[:, :nc].reshape(8 * B, D)
    per_chip = g1[0::2, nc:]
    o = 0

    def take(shape, axis):
        nonlocal o
        n = int(np.prod(shape))
        parts = per_chip[:, o:o + n].reshape((4,) + shape)
        o += n
        return jnp.concatenate([parts[j] for j in range(4)], axis=axis)

    ln_gain_f = take((3, 256), 1)
    ln_bias_f = take((3, 256), 1)
    a2f_f = take((16, 64), 1)
    a2b_f = take((16, 64), 1)
    lbl_f = take((2, 2, 128), 2)
    lb, lb_vjp = jax.vjp(_lower_bounds, lbl_f)

    assert B + 1 <= 8
    cs = jnp.concatenate([c_all.reshape(8, B, D), jnp.broadcast_to(c_ctx.reshape(1, 1, D), (8, 1, D)),
                          jnp.zeros((8, 7 - B, D), F32)], axis=1).reshape(ADA_ROWS, D)
    ncol = w_ada.shape[2]
    b_cols = lax.dynamic_slice_in_dim(b_ada, chip * ncol, ncol, axis=1)
    m4 = _mod_exchange(_ada_fwd(cs, w_ada[0], b_cols))
    m_all = jnp.concatenate([m4[j] for j in range(4)], axis=1)
    m_lat = m_all[:B].reshape(B, 1, N_MOD, D)
    m_ctx = jnp.broadcast_to(m_all[B].reshape(1, 1, N_MOD, D), (B, 1, N_MOD, D))

    groups = dict(ffn2=("ffn2_w_in", "ffn2_w_out"), mix=("w_mix_in", "w_mix_out"), ffn1=("ffn1_w_in", "ffn1_w_out"))
    shards = dict(weights, w_mix_in=jnp.pad(weights["w_mix_in"], ((0, 0), (0, MIX_NP - MIX_N))))
    blks = {k: _half_rows(shards[k], ci).astype(BF16) for k in BIG}
    gathering = {}
    token = m_all
    for group in ("ffn1", "mix", "ffn2"):
        sems, x_thru, l_thru, token = _split_start(_gather_copies, 4, [blks[k] for k in groups[group]], 8, token,
                                                   "weight_gather_start_" + group)
        gathering[group] = (sems, x_thru, l_thru)
    mvec = jnp.concatenate([m_ctx, m_lat], axis=1) + token[0, 0]

    forwarding = {}

    def prefetch(group, after):
        _, got = _split_wait(_gather_copies, 4, *gathering[group], after, "weight_gather_wait_" + group)
        forwarding[group] = _split_start(_forward_copies, 4, list(got), 1, None, "weight_gather_forward_start_" + group)
        return forwarding[group][3]

    def weights_for(group, after):
        names = groups[group]
        if group in forwarding:
            sems, g_thru, l_thru, _ = forwarding[group]
            (w_in, w_out), _ = _split_wait(_forward_copies, 4, sems, g_thru, l_thru, after,
                                           "weight_gather_forward_wait_" + group)
        else:
            _, got = _split_wait(_gather_copies, 4, *gathering[group], after, "weight_gather_wait_" + group)
            w_in, w_out = _gather_forward(got, "weight_gather_forward_" + group)
        if group == "mix":
            return _mix_in_to_padded(w_in.reshape(4, D, MIX_NP)), w_out.reshape(-1, D)
        return w_in.reshape((4,) + shards[names[0]].shape), w_out.reshape(-1, D)

    cvec = ci.reshape(1).astype(jnp.int32)
    cjvec = jnp.stack([ci, chip]).astype(jnp.int32)
    in_flight = {}

    def on_grads(group, gs):
        names = groups[group]
        if group == "mix":
            gs = (_mix_in_from_padded(gs[0]), gs[1])
        g8s = [g.reshape((8, shards[k].shape[0] // 2, shards[k].shape[1])) for k, g in zip(names, gs)]
        sems, g_thru, l_thru, token = _split_start(_pair_copies, 4, g8s, 4, None, "grad_pair_start_" + group)
        in_flight[group] = (sems, g_thru, l_thru)
        return token

    def on_sent(group, after):
        g8s, r4s = _split_wait(_pair_copies, 4, *in_flight[group], after, "grad_pair_wait_" + group)
        h4s = [_rs_add_pair(g, r, cvec, "grad_pair_add_" + k) for k, g, r in zip(groups[group], g8s, r4s)]
        sems, h_thru, l_thru, token = _split_start(_chip_copies, 3, h4s, 3, None, "grad_chip_start_" + group)
        in_flight[group] = (g8s, r4s, sems, h_thru, l_thru)
        return token

    loss_l, grad_x, dm, small = _local_step(
        x, ctx, loss_target, mvec, weights_for, prefetch, ln_gain_f, ln_bias_f, a2f_f, a2b_f,
        gla_a_bias_fwd, gla_a_bias_bwd, lb, gla_norm_gain, hgrn_norm_gain, on_grads, on_sent)
    loss = lax.psum(loss_l, ("x", "y", "c"))

    dm_lat = dm[:, 1].reshape(B, N_MOD * D)
    dm_ctx = jnp.sum(dm[:, 0], axis=0).reshape(N_MOD * D)
    keys = ("ln_gain", "ln_bias", "a2f", "a2b", "abf", "abb", "lb", "gng", "gnh")
    flat = jnp.concatenate([dm_lat.reshape(-1), dm_ctx] + [small[k].reshape(-1) for k in keys])
    nflat = flat.shape[0]
    npad = -(-nflat // 1024) * 1024
    flat2 = jnp.pad(flat, (0, npad - nflat)).reshape(8, npad // 8)
    s_sems, s_src, s_land, tok = _split_start(_all_copies, 7, [flat2], 8, None, "small_grads_start")

    fin = {}
    for group in ("ffn2", "mix", "ffn1"):
        g8s, r4s, sems, h_thru, l_thru = in_flight[group]
        _, r3s = _split_wait(_chip_copies, 3, sems, h_thru, l_thru, tok, "grad_chip_wait_" + group)
        for k, g, r4, r3 in zip(groups[group], g8s, r4s, r3s):
            fin[k] = _rs_add_chips(g, r4, r3, cjvec, "grad_chip_add_" + k)
    p_sems, p_src, p_land, tok = _split_start(_share_copies, 1, [fin[k] for k in BIG], 1, None, "grad_pair_share_start")

    _, (land,) = _split_wait(_all_copies, 7, s_sems, s_src, s_land, tok, "small_grads_wait")
    g3 = lax.dynamic_update_index_in_dim(land, flat2, dev, 0).reshape(8, npad)[:, :nflat]
    nlat = B * N_MOD * D
    dm_all = g3[:, :nlat].reshape(8 * B, N_MOD * D)
    tot = _sum8(g3[:, nlat:])[0]
    dmc_tot = tot[:N_MOD * D]
    o = N_MOD * D
    sg = {}
    for k in keys:
        n = int(np.prod(small[k].shape))
        sg[k] = tot[o:o + n].reshape(small[k].shape)
        o += n
    ctx_rows = jnp.zeros((8, 1, N_MOD * D), F32).at[0, 0].set(dmc_tot)
    dm_rows = jnp.concatenate([dm_all.reshape(8, B, -1), ctx_rows, jnp.zeros((8, 7 - B, N_MOD * D), F32)],
                              axis=1).reshape(ADA_ROWS, N_MOD * D)
    g_b_ada = (jnp.sum(dm_all, axis=0) + dmc_tot).reshape(1, N_MOD * D)
    g_w_ada, dcc = _ada_bwd(cs, w_ada[0], lax.dynamic_slice_in_dim(dm_rows, chip * ncol, ncol, axis=1))
    g4 = _gather_flat(dcc[B], "gather_cctx")
    dsilu = ((g4[0] + g4[2]) + g4[4]) + g4[6]
    sc = _sigmoid(c_ctx)
    g_c_ctx = dsilu * (sc * (1.0 + c_ctx * (1.0 - sc)))
    (g_lbl,) = lb_vjp(sg["lb"])

    def cols(a, n, axis):
        return lax.dynamic_slice_in_dim(a, chip * n, n, axis=axis)

    shared, _ = _split_wait(_share_copies, 1, p_sems, p_src, p_land, g_w_ada, "grad_pair_share_wait")
    gsh = {k: both.reshape(shards[k].shape)[:, :weights[k].shape[1]] for k, both in zip(BIG, shared)}

    grads = dict(
        c_ctx=g_c_ctx, w_ada=g_w_ada[None], b_ada=g_b_ada, ln_gain=cols(sg["ln_gain"], 256, 1)[None],
        ln_bias=cols(sg["ln_bias"], 256, 1)[None], ffn1_w_in=gsh["ffn1_w_in"][None], ffn1_w_out=gsh["ffn1_w_out"][None],
        w_mix_in=gsh["w_mix_in"][None], gla_a2_fwd=cols(sg["a2f"], 64, 1)[None], gla_a2_bwd=cols(sg["a2b"], 64, 1)[None],
        gla_a_bias_fwd=sg["abf"], gla_a_bias_bwd=sg["abb"], hgrn_lb_logits=cols(g_lbl, 128, 2),
        gla_norm_gain=sg["gng"].reshape(1, HD), hgrn_norm_gain=sg["gnh"].reshape(1, HD),
        w_mix_out=gsh["w_mix_out"][None], ffn2_w_in=gsh["ffn2_w_in"][None], ffn2_w_out=gsh["ffn2_w_out"][None])
    params = dict(
        c_ctx=(c_ctx, m_c_ctx, v_c_ctx), w_ada=(w_ada, m_w_ada, v_w_ada), b_ada=(b_ada, m_b_ada, v_b_ada),
        ln_gain=(ln_gain, m_ln_gain, v_ln_gain), ln_bias=(ln_bias, m_ln_bias, v_ln_bias),
        ffn1_w_in=(ffn1_w_in, m_ffn1_w_in, v_ffn1_w_in), ffn1_w_out=(ffn1_w_out, m_ffn1_w_out, v_ffn1_w_out),
        w_mix_in=(w_mix_in, m_w_mix_in, v_w_mix_in), gla_a2_fwd=(gla_a2_fwd, m_gla_a2_fwd, v_gla_a2_fwd),
        gla_a2_bwd=(gla_a2_bwd, m_gla_a2_bwd, v_gla_a2_bwd),
        gla_a_bias_fwd=(gla_a_bias_fwd, m_gla_a_bias_fwd, v_gla_a_bias_fwd),
        gla_a_bias_bwd=(gla_a_bias_bwd, m_gla_a_bias_bwd, v_gla_a_bias_bwd),
        hgrn_lb_logits=(hgrn_lb_logits, m_hgrn_lb_logits, v_hgrn_lb_logits),
        gla_norm_gain=(gla_norm_gain, m_gla_norm_gain, v_gla_norm_gain),
        hgrn_norm_gain=(hgrn_norm_gain, m_hgrn_norm_gain, v_hgrn_norm_gain),
        w_mix_out=(w_mix_out, m_w_mix_out, v_w_mix_out), ffn2_w_in=(ffn2_w_in, m_ffn2_w_in, v_ffn2_w_in),
        ffn2_w_out=(ffn2_w_out, m_ffn2_w_out, v_ffn2_w_out))
    order = list(params.keys())
    big_names = ("w_ada",) + BIG
    upd = {}
    for k in big_names:
        w_, m_, v_ = params[k]
        s2 = w_.shape[-2:]
        g_, d_, nm_, nv_ = _adamw(w_.reshape(s2), grads[k].reshape(s2), m_.reshape(s2), v_.reshape(s2), "adamw_" + k)
        grads[k] = g_
        upd[k] = (d_.reshape(w_.shape), nm_.reshape(w_.shape), nv_.reshape(w_.shape))
    small_names = [k for k in order if k not in big_names]
    sizes = [int(np.prod(params[k][0].shape)) for k in small_names]
    tot_n = sum(sizes)
    npad = -(-tot_n // 1024) * 1024

    def packed(get):
        flat_ = jnp.concatenate([get(k).reshape(-1) for k in small_names])
        return jnp.pad(flat_, (0, npad - tot_n)).reshape(8, npad // 8)

    _, d_s, nm_s, nv_s = _adamw(packed(lambda k: params[k][0]), packed(lambda k: grads[k]),
                                packed(lambda k: params[k][1]), packed(lambda k: params[k][2]), "adamw_small")
    o = 0
    for k, n in zip(small_names, sizes):
        shp = params[k][0].shape
        upd[k] = tuple(a.reshape(-1)[o:o + n].reshape(shp) for a in (d_s, nm_s, nv_s))
        o += n

    return (loss, grad_x, *[grads[k].reshape(params[k][0].shape) for k in order], *[upd[k][0] for k in order],
            *[upd[k][1] for k in order], *[upd[k][2] for k in order])
```

```python
import functools

import numpy as np
import jax
import jax.numpy as jnp
from jax import lax
from jax.experimental import pallas as pl
from jax.experimental.pallas import tpu as pltpu

F32 = jnp.float32
BF16 = jnp.bfloat16
MESH = pl.DeviceIdType.MESH

D = 1024
DFF = 2816
TM = 256
CH = 32
NCB = TM // CH
SB = 128
CSB = SB // CH
NSB = TM // SB
HP = 8
HPB = 8
HD = 128
NH = 8
LN_EPS = 1e-5
NORM_EPS = 1e-6
ALPHA = 2.0 ** 0.25
GATE_NORM = 16.0
GLA_DK = 64
N_MOD = 9
VMEM_LIMIT = 52 * 1024 * 1024

MIXP = 5120
GG, RG, GQ, GK, GV, RQ, RFF, RFB, RI, LR = 0, 512, 1024, 1536, 2048, 2560, 3072, 3584, 4096, 4608
IN_SPLITS = (256, 256, 512, 512, 16, 16, 512, 512, 512, 512, 512)

ADAM_LR, ADAM_B1, ADAM_B2, ADAM_EPS, ADAM_WD, ADAM_STEP = 0.001, 0.9, 0.999, 1e-08, 0.01, 10


def _cp(n_axes):
    return pltpu.CompilerParams(dimension_semantics=("arbitrary",) * n_axes, vmem_limit_bytes=VMEM_LIMIT)


def _rowmap(stride, off):
    return lambda b, i: (b * stride + off + i, 0)


def _mmap(comb):
    if comb:
        return lambda b, i: (b, jnp.minimum(i, 1), 0, 0)
    return lambda b, i: (b, 1, 0, 0)


def _ln(x):
    mu = jnp.mean(x, axis=-1, keepdims=True)
    xc = x - mu
    var = jnp.mean(xc * xc, axis=-1, keepdims=True)
    r = lax.rsqrt(var + LN_EPS)
    return xc * r, r


def _ln_bwd(dxh, xh, r):
    return r * (dxh - jnp.mean(dxh, axis=-1, keepdims=True) - xh * jnp.mean(dxh * xh, axis=-1, keepdims=True))


def _sigmoid(x):
    return 1.0 / (1.0 + jnp.exp(-x))


def _rsum(x):
    return jnp.sum(x, axis=0, keepdims=True)


def _dot(a, b):
    return jnp.dot(a, b, preferred_element_type=F32)


def _dot_nt(a, b):
    return lax.dot_general(a, b, (((1,), (1,)), ((), ())), preferred_element_type=F32)


def _dot_tn(a, b):
    return lax.dot_general(a, b, (((0,), (0,)), ((), ())), preferred_element_type=F32)


def _modulate(xv, m_ref, sub):
    xh, _ = _ln(xv)
    sh = m_ref[0, 0, 3 * sub:3 * sub + 1, :]
    sc = m_ref[0, 0, 3 * sub + 1:3 * sub + 2, :]
    return (xh * (1.0 + sh) + sc).astype(BF16)


def _embed_lnmod(x, ctx, pos, mvec):
    B, T, _ = x.shape
    nt = 1 + T // TM

    def body(x_ref, c_ref, p_ref, m_ref, o_ref, h_ref):
        i = pl.program_id(1)

        @pl.when(i == 0)
        def _():
            o_ref[...] = c_ref[0]

        @pl.when(i > 0)
        def _():
            o_ref[...] = x_ref[0] + p_ref[...]

        h_ref[...] = _modulate(o_ref[...], m_ref, 0)

    rows = pl.BlockSpec((TM, D), lambda b, i: (b * nt + i, 0))
    return pl.pallas_call(
        body, name="embed_lnmod0", grid=(B, nt),
        in_specs=[pl.BlockSpec((1, TM, D), lambda b, i: (b, jnp.maximum(i - 1, 0), 0)),
                  pl.BlockSpec((1, TM, D), lambda b, i: (b, 0, 0)),
                  pl.BlockSpec((TM, D), lambda b, i: (jnp.maximum(i - 1, 0), 0)),
                  pl.BlockSpec((1, 1, N_MOD, D), _mmap(True))],
        out_specs=[rows, rows],
        out_shape=[jax.ShapeDtypeStruct((B * nt * TM, D), F32), jax.ShapeDtypeStruct((B * nt * TM, D), BF16)],
        compiler_params=_cp(2))(x, ctx, pos, mvec)


def _lnmod0_bwd(dh, X, mvec, dres, x_shape, B, nt):
    def body(dh_ref, x_ref, m_ref, dr_ref, dx_ref, dm_ref):
        i = pl.program_id(1)
        xh, r = _ln(x_ref[...])
        sh = m_ref[0, 0, 0:1, :]
        dhv = dh_ref[...].astype(F32)

        @pl.when((i == 0) | (i == 1))
        def _():
            dm_ref[...] = jnp.zeros_like(dm_ref)

        dm_ref[0, 0, 0:1, :] += _rsum(dhv * xh)
        dm_ref[0, 0, 1:2, :] += _rsum(dhv)

        @pl.when(i > 0)
        def _():
            dx_ref[0] = _ln_bwd(dhv * (1.0 + sh), xh, r) + dr_ref[...]

    rows = pl.BlockSpec((TM, D), _rowmap(nt, 0))
    return pl.pallas_call(
        body, name="lnmod0_bwd", grid=(B, nt),
        in_specs=[rows, rows, pl.BlockSpec((1, 1, N_MOD, D), _mmap(True)), rows],
        out_specs=[pl.BlockSpec((1, TM, D), lambda b, i: (b, jnp.maximum(i - 1, 0), 0)),
                   pl.BlockSpec((1, 1, 2, D), _mmap(True))],
        out_shape=[jax.ShapeDtypeStruct(x_shape, F32), jax.ShapeDtypeStruct((B, 2, 2, D), F32)],
        compiler_params=_cp(2))(dh, X, mvec, dres)


def _resid_fwd(x_ref, y_ref, m_ref, gb_ref, sub, w):
    wg = w * m_ref[0, 0, 3 * sub + 2:3 * sub + 3, :]
    y = y_ref[...].astype(F32)
    zh, r = _ln(ALPHA * x_ref[...] + wg * y)
    return y, wg, zh, r


def _resid_modulate(x_ref, y_ref, m_ref, gb_ref, sub, w, xn_ref, hn_ref):
    _, _, zh, _ = _resid_fwd(x_ref, y_ref, m_ref, gb_ref, sub, w)
    xn = zh * gb_ref[0:1, :] + gb_ref[1:2, :]
    xn_ref[...] = xn
    hn = _modulate(xn, m_ref, sub + 1)
    hn_ref[...] = hn
    return hn


def _pre_specs(pre, B, nt):
    rows = pl.BlockSpec((TM, D), _rowmap(nt, 0))
    in_specs = [pl.BlockSpec((TM, D), _rowmap(*pre["lx"])), rows, pl.BlockSpec((1, 1, N_MOD, D), _mmap(pre["comb"])),
                pl.BlockSpec((2, D), lambda b, i: (0, 0))]
    shapes = [jax.ShapeDtypeStruct((B * nt * TM, D), F32), jax.ShapeDtypeStruct((B * nt * TM, D), BF16)]
    return in_specs, [pre["X"], pre["Y"], pre["mvec"], pre["gb"]], [rows, rows], shapes


def _resid_grads(do, y, wg, zh, r, w, gb_ref, comb, dx_ref, dy_ref, dg_ref, dgb_ref):
    b_, i = pl.program_id(0), pl.program_id(1)
    dz = _ln_bwd(do * gb_ref[0:1, :], zh, r)
    dx_ref[...] = ALPHA * dz
    dy_ref[...] = (wg * dz).astype(BF16)

    @pl.when((b_ == 0) & (i == 0))
    def _():
        dgb_ref[...] = jnp.zeros_like(dgb_ref)

    dgb_ref[0:1, :] += _rsum(do * zh)
    dgb_ref[1:2, :] += _rsum(do)

    init = (i == 0) | (i == 1) if comb else (i == 0)

    @pl.when(init)
    def _():
        dg_ref[...] = jnp.zeros_like(dg_ref)

    dg_ref[0, 0] += w * _rsum(dz * y)


def _resid_out_shapes(B, nt, comb):
    rows = pl.BlockSpec((TM, D), _rowmap(nt, 0))
    specs = [rows, rows, pl.BlockSpec((1, 1, 1, D), _mmap(comb)), pl.BlockSpec((2, D), lambda b, i: (0, 0))]
    shapes = [jax.ShapeDtypeStruct((B * nt * TM, D), F32), jax.ShapeDtypeStruct((B * nt * TM, D), BF16),
              jax.ShapeDtypeStruct((B, 2, 1, D), F32), jax.ShapeDtypeStruct((2, D), F32)]
    return specs, shapes


def _swiglu_dx(dyv, w_ref, u_ref, du_ref):
    half = DFF // 2
    for j in range(2):
        lo, hi = j * half, (j + 1) * half
        da = _dot_nt(dyv, w_ref[lo:hi, :])
        g = u_ref[:, lo:hi].astype(F32)
        up = u_ref[:, DFF + lo:DFF + hi].astype(F32)
        s = _sigmoid(g)
        du_ref[:, lo:hi] = (da * up * (s * (1.0 + g * (1.0 - s)))).astype(BF16)
        du_ref[:, DFF + lo:DFF + hi] = (da * (g * s)).astype(BF16)


def _tail(X, Y, mvec, gb, tgt, sub, w, Wo, U, B, nt):
    def body(x_ref, y_ref, m_ref, gb_ref, t_ref, w_ref, u_ref, dx_ref, dy_ref, dg_ref, dgb_ref, l_ref, du_ref):
        y, wg, zh, r = _resid_fwd(x_ref, y_ref, m_ref, gb_ref, sub, w)
        e = (zh * gb_ref[0:1, :] + gb_ref[1:2, :]) - t_ref[0]

        @pl.when((pl.program_id(0) == 0) & (pl.program_id(1) == 0))
        def _():
            l_ref[...] = jnp.zeros_like(l_ref)

        l_ref[...] += _rsum(e * e)
        _resid_grads(e * (1.0 / D), y, wg, zh, r, w, gb_ref, False, dx_ref, dy_ref, dg_ref, dgb_ref)
        _swiglu_dx(dy_ref[...], w_ref, u_ref, du_ref)

    rows = pl.BlockSpec((TM, D), _rowmap(nt, 0))
    wide = pl.BlockSpec((TM, 2 * DFF), _rowmap(nt, 0))
    specs, shapes = _resid_out_shapes(B, nt, False)
    return pl.pallas_call(
        body, name="resid2_loss_bwd", grid=(B, nt),
        in_specs=[rows, rows, pl.BlockSpec((1, 1, N_MOD, D), _mmap(False)), pl.BlockSpec((2, D), lambda b, i: (0, 0)),
                  pl.BlockSpec((1, TM, D), lambda b, i: (b, i, 0)), _wspec(Wo, 2), wide],
        out_specs=specs + [pl.BlockSpec((1, D), lambda b, i: (0, 0)), wide],
        out_shape=shapes + [jax.ShapeDtypeStruct((1, D), F32), jax.ShapeDtypeStruct((B * nt * TM, 2 * DFF), BF16)],
        compiler_params=_cp(2))(X, Y, mvec, gb, tgt, Wo, U)


def _lnmod_resid_bwd(dh, Xi, lxi, dres, Xp, lxp, Yp, mvec, comb, sub, w, gb, B, nt, name):
    ntl = nt - 1 if comb else nt

    def body(dh_ref, xi_ref, dr_ref, xp_ref, yp_ref, m_ref, gb_ref, dx_ref, dy_ref, dg_ref, dgb_ref, dm_ref):
        i = pl.program_id(1)
        xh, r = _ln(xi_ref[...])
        sh = m_ref[0, 0, 3 * sub:3 * sub + 1, :]
        dhv = dh_ref[...].astype(F32)
        dr = dr_ref[...]
        if comb:
            dr = jnp.where(i > 0, dr, 0.0)
        do = _ln_bwd(dhv * (1.0 + sh), xh, r) + dr

        init = (i == 0) | (i == 1) if comb else (i == 0)

        @pl.when(init)
        def _():
            dm_ref[...] = jnp.zeros_like(dm_ref)

        dm_ref[0, 0, 0:1, :] += _rsum(dhv * xh)
        dm_ref[0, 0, 1:2, :] += _rsum(dhv)
        y, wg, zh, r2 = _resid_fwd(xp_ref, yp_ref, m_ref, gb_ref, sub - 1, w)
        _resid_grads(do, y, wg, zh, r2, w, gb_ref, comb, dx_ref, dy_ref, dg_ref, dgb_ref)

    rows = pl.BlockSpec((TM, D), _rowmap(nt, 0))
    if comb:
        dres_spec = pl.BlockSpec((TM, D), lambda b, i: (b * ntl + jnp.maximum(i - 1, 0), 0))
    else:
        dres_spec = rows
    specs, shapes = _resid_out_shapes(B, nt, comb)
    return pl.pallas_call(
        body, name=name, grid=(B, nt),
        in_specs=[rows, pl.BlockSpec((TM, D), _rowmap(*lxi)), dres_spec, pl.BlockSpec((TM, D), _rowmap(*lxp)), rows,
                  pl.BlockSpec((1, 1, N_MOD, D), _mmap(comb)), pl.BlockSpec((2, D), lambda b, i: (0, 0))],
        out_specs=specs + [pl.BlockSpec((1, 1, 2, D), _mmap(comb))],
        out_shape=shapes + [jax.ShapeDtypeStruct((B, 2, 2, D), F32)],
        compiler_params=_cp(2))(dh, Xi, dres, Xp, Yp, mvec, gb)


def _ffn_out_dx(dy, W, u, name, dep=None):
    M = dy.shape[0]
    deps = [] if dep is None else [dep]

    def body(dy_ref, w_ref, u_ref, *rest):
        _swiglu_dx(dy_ref[...], w_ref, u_ref, rest[-1])

    return pl.pallas_call(
        body, name=name, grid=(M // TM,),
        in_specs=[pl.BlockSpec((TM, D), lambda i: (i, 0)), _wspec(W, 1), pl.BlockSpec((TM, 2 * DFF), lambda i: (i, 0))]
        + [_wspec(d, 1) for d in deps],
        out_specs=pl.BlockSpec((TM, 2 * DFF), lambda i: (i, 0)),
        out_shape=jax.ShapeDtypeStruct((M, 2 * DFF), BF16), compiler_params=_cp(1))(dy, W, u, *deps)


def _wspec(W, nidx):
    zeros = (0,) * W.ndim
    if nidx == 1:
        return pl.BlockSpec(W.shape, lambda i: zeros)
    return pl.BlockSpec(W.shape, lambda b, i: zeros)


def _mm_nn(A, la, W, B, nt, out_dtype, name):
    K, N = W.shape

    def body(a_ref, w_ref, o_ref):
        o_ref[...] = _dot(a_ref[...], w_ref[...]).astype(out_dtype)

    return pl.pallas_call(
        body, name=name, grid=(B, nt),
        in_specs=[pl.BlockSpec((TM, K), _rowmap(*la)), _wspec(W, 2)],
        out_specs=pl.BlockSpec((TM, N), _rowmap(nt, 0)),
        out_shape=jax.ShapeDtypeStruct((B * nt * TM, N), out_dtype), compiler_params=_cp(2))(A, W)


def _ffn_in(A, la, W3, B, nt, name, pre=None):
    K, n = W3.shape[1:]

    def body(*refs):
        if pre is None:
            a_ref, w_ref, u_ref, s_ref = refs
            a = a_ref[...]
        else:
            x_ref, y_ref, m_ref, gb_ref, w_ref, u_ref, s_ref, xn_ref, hn_ref = refs
            a = _resid_modulate(x_ref, y_ref, m_ref, gb_ref, pre["sub"], pre["w"], xn_ref, hn_ref)
        for j in range(2):
            g = _dot(a, w_ref[j])
            up = _dot(a, w_ref[j + 2])
            u_ref[:, j * n:(j + 1) * n] = g.astype(BF16)
            u_ref[:, (j + 2) * n:(j + 3) * n] = up.astype(BF16)
            s_ref[:, j * n:(j + 1) * n] = (g * _sigmoid(g) * up).astype(BF16)

    rows = B * nt * TM
    if pre is None:
        in_specs, args, xo_specs, xo_shapes = [pl.BlockSpec((TM, K), _rowmap(*la))], [A], [], []
    else:
        in_specs, args, xo_specs, xo_shapes = _pre_specs(pre, B, nt)
    return pl.pallas_call(
        body, name=name, grid=(B, nt),
        in_specs=in_specs + [_wspec(W3, 2)],
        out_specs=[pl.BlockSpec((TM, 4 * n), _rowmap(nt, 0)), pl.BlockSpec((TM, 2 * n), _rowmap(nt, 0))] + xo_specs,
        out_shape=[jax.ShapeDtypeStruct((rows, 4 * n), BF16), jax.ShapeDtypeStruct((rows, 2 * n), BF16)] + xo_shapes,
        compiler_params=_cp(2))(*args, W3)


def _mm_nt(A, W, name, dep=None, out_dtype=F32):
    M, N = A.shape
    K = W.shape[-2]

    def body(a_ref, w_ref, *rest):
        o_ref = rest[-1]
        if W.ndim == 3:
            n = W.shape[-1]
            acc = _dot_nt(a_ref[:, 0:n], w_ref[0])
            for j in range(1, 4):
                acc = acc + _dot_nt(a_ref[:, j * n:(j + 1) * n], w_ref[j])
            o_ref[...] = acc.astype(out_dtype)
        else:
            o_ref[...] = _dot_nt(a_ref[...], w_ref[...]).astype(out_dtype)

    deps = [] if dep is None else [dep]
    return pl.pallas_call(
        body, name=name, grid=(M // TM,),
        in_specs=[pl.BlockSpec((TM, N), lambda i: (i, 0)), _wspec(W, 1)] + [_wspec(d, 1) for d in deps],
        out_specs=pl.BlockSpec((TM, K), lambda i: (i, 0)),
        out_shape=jax.ShapeDtypeStruct((M, K), out_dtype), compiler_params=_cp(1))(A, W, *deps)


def _mm_tn(A, G, name, tn=512, shards=None):
    M, K = A.shape
    N = G.shape[1]
    tk = next(t for t in ((2048, 1536, 1024, 512) if K <= D else (1024, 512)) if M % t == 0)
    if shards:
        tn = N // shards

    def body(a_ref, g_ref, o_ref):
        @pl.when(pl.program_id(1) == 0)
        def _():
            o_ref[...] = jnp.zeros_like(o_ref)

        upd = _dot_tn(a_ref[...], g_ref[...])
        if shards:
            o_ref[0] += upd
        else:
            o_ref[...] += upd

    if shards:
        out_spec = pl.BlockSpec((1, K, tn), lambda n, k: (n, 0, 0))
        out_shape = jax.ShapeDtypeStruct((shards, K, tn), F32)
    else:
        out_spec = pl.BlockSpec((K, tn), lambda n, k: (0, n))
        out_shape = jax.ShapeDtypeStruct((K, N), F32)
    return pl.pallas_call(
        body, name=name, grid=(N // tn, M // tk),
        in_specs=[pl.BlockSpec((tk, K), lambda n, k: (k, 0)), pl.BlockSpec((tk, tn), lambda n, k: (k, n))],
        out_specs=out_spec, out_shape=out_shape, compiler_params=_cp(2))(A, G)


def _logsig(z):
    return jnp.minimum(z, 0.0) - jnp.log(1.0 + jnp.exp(-jnp.abs(z)))


ZW = 2688
ZRQ, ZRFF, ZRFB, ZLR = 1024, 1536, 2048, 2560


def _mix_in_features(H, W, a2p, biasp, lbp, B, nt, pre=None):
    K = W.shape[0]

    def body(*refs):
        if pre is None:
            h_ref, w_ref, a2_ref, bias_ref, lb_ref, q_ref, k_ref, v_ref, g_ref, z_ref = refs
            hv = h_ref[...]
        else:
            (x_ref, y_ref, m_ref, gb_ref, w_ref, a2_ref, bias_ref, lb_ref, q_ref, k_ref, v_ref, g_ref, z_ref,
             xn_ref, hn_ref) = refs
            hv = _resid_modulate(x_ref, y_ref, m_ref, gb_ref, pre["sub"], pre["w"], xn_ref, hn_ref)
        proj = lambda lo, hi: _dot(hv, w_ref[:, lo:hi])
        z_ref[:, 0:2 * 4 * HD] = proj(GG, GG + 2 * 4 * HD)
        lrf = proj(LR, LR + HD)
        z_ref[:, ZLR:ZLR + HD] = lrf
        lr = lrf.astype(BF16)
        lane = lax.broadcasted_iota(jnp.int32, (1, 4 * HD), 1)
        keep = (lane & (HD - 1)) < GLA_DK
        for d in range(2):
            z = _dot(lr, a2_ref[d]) + bias_ref[d:d + 1, :]
            gl = jnp.where(keep, _logsig(z) * (1.0 / GATE_NORM), 0.0)
            for h in range(4):
                g_ref[d, 0, h] = gl[:, h * HD:(h + 1) * HD]
        gq, gk, gv = proj(GQ, GQ + 4 * HD), proj(GK, GK + 4 * HD), proj(GV, GV + 4 * HD)
        for h in range(4):
            sl = slice(h * HD, (h + 1) * HD)
            q_ref[0, h] = gq[:, sl] * (GLA_DK ** -0.5)
            k_ref[0, 0, h] = gk[:, sl]
            k_ref[1, 0, h] = gk[:, sl]
            v_ref[0, h] = gv[:, sl].astype(BF16)
        rqs, ri = proj(RQ, RQ + 4 * HD), proj(RI, RI + 4 * HD)
        z_ref[:, ZRQ:ZRQ + 4 * HD] = rqs
        for h in range(4):
            sl = slice(h * HD, (h + 1) * HD)
            rq = rqs[:, sl]
            q_ref[0, 4 + h] = rq * _sigmoid(rq) * (HD ** -0.5)
            v_ref[0, 4 + h] = ri[:, sl].astype(BF16)
        for d, off, zoff in ((0, RFF, ZRFF), (1, RFB, ZRFB)):
            rf = proj(off, off + 4 * HD)
            z_ref[:, zoff:zoff + 4 * HD] = rf
            for h in range(4):
                sl = slice(h * HD, (h + 1) * HD)
                lb = lb_ref[d:d + 1, sl]
                f = lb + (1.0 - lb) * _sigmoid(rf[:, sl])
                g_ref[d, 0, 4 + h] = jnp.log(f)
                k_ref[d, 0, 4 + h] = 1.0 - f

    one = pl.BlockSpec((1, NH, TM, HD), lambda b, i: (b, 0, i, 0))
    two = pl.BlockSpec((2, 1, NH, TM, HD), lambda b, i: (0, b, 0, i, 0))
    s1 = jax.ShapeDtypeStruct((B, NH, nt * TM, HD), F32)
    s2 = jax.ShapeDtypeStruct((2, B, NH, nt * TM, HD), F32)
    if pre is None:
        in_specs, args, xo_specs, xo_shapes = [pl.BlockSpec((TM, K), _rowmap(nt, 0))], [H], [], []
    else:
        in_specs, args, xo_specs, xo_shapes = _pre_specs(pre, B, nt)
    return pl.pallas_call(
        body, name="mix_in_features", grid=(B, nt),
        in_specs=in_specs + [_wspec(W, 2), pl.BlockSpec((2, HD, 4 * HD), lambda b, i: (0, 0, 0)),
                             pl.BlockSpec((2, 4 * HD), lambda b, i: (0, 0)), pl.BlockSpec((2, 4 * HD), lambda b, i: (0, 0))],
        out_specs=[one, two, one, two, pl.BlockSpec((TM, ZW), _rowmap(nt, 0))] + xo_specs,
        out_shape=[s1, s2, jax.ShapeDtypeStruct(s1.shape, BF16), s2,
                   jax.ShapeDtypeStruct((B * nt * TM, ZW), F32)] + xo_shapes,
        compiler_params=_cp(2))(*args, W, a2p, biasp, lbp)


def _features_bwd(Z, a2p, biasp, lbp, dQ0, dQ1, dK0, dK1, dV0, dV1, dG0, dG1, dgates, W, B):
    nt = Z.shape[0] // (B * TM)

    def body(z_ref, a2_ref, bias_ref, lb_ref, dq0, dq1, dk0, dk1, dv0, dv1, dg0, dg1, dgt_ref, w_ref,
             df_ref, dh_ref, da2_ref, dbias_ref, dlb_ref):
        b_, i = pl.program_id(0), pl.program_id(1)

        @pl.when((b_ == 0) & (i == 0))
        def _():
            da2_ref[...] = jnp.zeros_like(da2_ref)
            dbias_ref[...] = jnp.zeros_like(dbias_ref)
            dlb_ref[...] = jnp.zeros_like(dlb_ref)

        df_ref[:, 0:2 * 4 * HD] = jnp.where(i > 0, dgt_ref[...], 0.0).astype(BF16)
        df_ref[:, LR + HD:] = jnp.zeros((TM, MIXP - LR - HD), BF16)
        lr = z_ref[:, ZLR:ZLR + HD].astype(BF16)
        lane = lax.broadcasted_iota(jnp.int32, (1, 4 * HD), 1)
        keep = (lane & (HD - 1)) < GLA_DK
        dlr = jnp.zeros((TM, HD), F32)
        dgs = (dg0, dg1)
        dks = (dk0, dk1)
        rd = lambda ref, h: ref[0, h].astype(F32)
        for d in range(2):
            z = _dot(lr, a2_ref[d]) + bias_ref[d:d + 1, :]
            dgl = jnp.concatenate([rd(dgs[d], h) for h in range(4)], axis=1)
            dz = jnp.where(keep, dgl * (1.0 / GATE_NORM) * (1.0 - _sigmoid(z)), 0.0)
            dzb = dz.astype(BF16)
            dlr = dlr + _dot_nt(dzb, a2_ref[d])
            da2_ref[d] += _dot_tn(lr, dzb)
            dbias_ref[d:d + 1, :] += _rsum(dz)
        df_ref[:, LR:LR + HD] = dlr.astype(BF16)
        for h in range(4):
            df_ref[:, GQ + h * HD:GQ + (h + 1) * HD] = ((rd(dq0, h) + rd(dq1, h)) * (GLA_DK ** -0.5)).astype(BF16)
            df_ref[:, GK + h * HD:GK + (h + 1) * HD] = (rd(dk0, h) + rd(dk1, h)).astype(BF16)
            df_ref[:, GV + h * HD:GV + (h + 1) * HD] = (rd(dv0, h) + rd(dv1, h)).astype(BF16)
        for h in range(4):
            sl = slice(h * HD, (h + 1) * HD)
            rq = z_ref[:, ZRQ + h * HD:ZRQ + (h + 1) * HD]
            s = _sigmoid(rq)
            dqh = rd(dq0, 4 + h) + rd(dq1, 4 + h)
            df_ref[:, RQ + h * HD:RQ + (h + 1) * HD] = (dqh * (HD ** -0.5) * (s * (1.0 + rq * (1.0 - s)))).astype(BF16)
            df_ref[:, RI + h * HD:RI + (h + 1) * HD] = (rd(dv0, 4 + h) + rd(dv1, 4 + h)).astype(BF16)
            for d, off, zoff in ((0, RFF, ZRFF), (1, RFB, ZRFB)):
                lb = lb_ref[d:d + 1, sl]
                sg = _sigmoid(z_ref[:, zoff + h * HD:zoff + (h + 1) * HD])
                f = lb + (1.0 - lb) * sg
                dff = rd(dgs[d], 4 + h) / f - rd(dks[d], 4 + h)
                df_ref[:, off + h * HD:off + (h + 1) * HD] = (dff * (1.0 - lb) * sg * (1.0 - sg)).astype(BF16)
                dlb_ref[d:d + 1, sl] += _rsum(dff * (1.0 - sg))
        dh_ref[...] = _dot_nt(df_ref[...], w_ref[...]).astype(BF16)

    m0 = lambda b, i: (b, 0, i, 0)
    one = lambda m: pl.BlockSpec((1, NH, TM, HD), m)
    return pl.pallas_call(
        body, name="mix_features_bwd", grid=(B, nt),
        in_specs=[pl.BlockSpec((TM, ZW), _rowmap(nt, 0)), pl.BlockSpec((2, HD, 4 * HD), lambda b, i: (0, 0, 0)),
                  pl.BlockSpec((2, 4 * HD), lambda b, i: (0, 0)), pl.BlockSpec((2, 4 * HD), lambda b, i: (0, 0)),
                  one(m0), one(m0), one(m0), one(m0), one(m0), one(m0), one(m0), one(m0),
                  pl.BlockSpec((TM, D), lambda b, i: (b * (nt - 1) + jnp.maximum(i - 1, 0), 0)), _wspec(W, 2)],
        out_specs=[pl.BlockSpec((TM, MIXP), _rowmap(nt, 0)), pl.BlockSpec((TM, D), _rowmap(nt, 0)),
                   pl.BlockSpec((2, HD, 4 * HD), lambda b, i: (0, 0, 0)),
                   pl.BlockSpec((2, 4 * HD), lambda b, i: (0, 0)), pl.BlockSpec((2, 4 * HD), lambda b, i: (0, 0))],
        out_shape=[jax.ShapeDtypeStruct((B * nt * TM, MIXP), BF16), jax.ShapeDtypeStruct((B * nt * TM, D), BF16),
                   jax.ShapeDtypeStruct((2, HD, 4 * HD), F32),
                   jax.ShapeDtypeStruct((2, 4 * HD), F32), jax.ShapeDtypeStruct((2, 4 * HD), F32)],
        compiler_params=_cp(2))(Z, a2p, biasp, lbp, dQ0, dQ1, dK0, dK1, dV0, dV1, dG0, dG1, dgates, W)


def _chunk_scan(x, rin, fwd):
    acc = x
    sft = 1
    while sft < CH:
        if fwd:
            acc = acc + jnp.where(rin >= sft, pltpu.roll(acc, sft, 0), 0.0)
        else:
            acc = acc + jnp.where(rin < CH - sft, pltpu.roll(acc, TM - sft, 0), 0.0)
        sft *= 2
    return acc


def _chunk_total(x):
    t = jnp.sum(x.reshape(NCB, CH, HD), axis=1, keepdims=True)
    return jnp.broadcast_to(t, (NCB, CH, HD)).reshape(TM, HD)


def _scan_masks(rev):
    rin = lax.broadcasted_iota(jnp.int32, (TM, HD), 0) & (CH - 1)
    ri = lax.broadcasted_iota(jnp.int32, (SB, SB), 0)
    ci = lax.broadcasted_iota(jnp.int32, (SB, SB), 1)
    same = (ri >> 5) == (ci >> 5)
    lo = same & (ri >= ci)
    up = same & (ri <= ci)
    mask, maskT = (up, lo) if rev else (lo, up)
    re = lax.broadcasted_iota(jnp.int32, (SB, CSB * HD), 0) >> 5
    ce = lax.broadcasted_iota(jnp.int32, (SB, CSB * HD), 1) >> 7
    return rin, mask, maskT, re == ce


def _scan_decay(q, k, g, rin, rev):
    b = _chunk_scan(g, rin, not rev)
    xx = _chunk_total(g) - b
    eb = jnp.exp(b)
    return b, xx, eb, q * eb, k * jnp.exp(-b), k * jnp.exp(xx)


def _sub(x, s):
    return x[s * SB:(s + 1) * SB]


def _expand(xb, mexp):
    return jnp.where(mexp, jnp.concatenate([xb] * CSB, axis=1), jnp.zeros((), xb.dtype))


def _own(x, mexp):
    xm = jnp.where(mexp, x, 0.0)
    acc = xm[:, 0:HD]
    for n in range(1, CSB):
        acc = acc + xm[:, n * HD:(n + 1) * HD]
    return acc


def _stack(per_chunk, s):
    return jnp.concatenate(per_chunk[s * CSB:(s + 1) * CSB], axis=1)


def _state_pass(s0, eb, uts, rev):
    order = range(NCB - 1, -1, -1) if rev else range(NCB)
    states = [None] * NCB
    s = s0
    for n in order:
        row = n * CH if rev else n * CH + CH - 1
        states[n] = s
        s = eb[row:row + 1, :] * s + uts[n // CSB][:, (n % CSB) * HD:(n % CSB + 1) * HD]
    return states, s


def _scan_fwd(Q, K, V, G, rev, B):
    nb = Q.shape[2] // TM
    d = 1 if rev else 0
    rmap = (lambda s: jnp.where(s == 0, 0, nb - s)) if rev else (lambda s: s)

    def body(q_ref, k_ref, v_ref, g_ref, o_ref, st_ref, s_scr):
        @pl.when(pl.program_id(2) == 0)
        def _():
            s_scr[...] = jnp.zeros_like(s_scr)

        rin, mask, _, mexp = _scan_masks(rev)

        def head(p, readout):
            s0 = s_scr[p]
            st_ref[0, p, 0] = s0
            _, _, eb, qd, ki, kt = _scan_decay(q_ref[0, p], k_ref[0, 0, p], g_ref[0, 0, p], rin, rev)
            ktb, vb = kt.astype(BF16), v_ref[0, p]
            uts = [_dot_tn(_sub(vb, s), _expand(_sub(ktb, s), mexp)) for s in range(NSB)]
            states, s_new = _state_pass(s0, eb, uts, rev)
            s_scr[p] = s_new
            if not readout:
                return
            qb, kib = qd.astype(BF16), ki.astype(BF16)
            for s in range(NSB):
                a = jnp.where(mask, _dot_nt(_sub(qb, s), _sub(kib, s)), 0.0)
                o_ref[0, p, s * SB:(s + 1) * SB, :] = (
                    _dot(a.astype(BF16), _sub(vb, s))
                    + _dot_nt(_expand(_sub(qb, s), mexp), _stack(states, s).astype(BF16)))

        @pl.when(pl.program_id(2) >= 1)
        def _():
            for p in range(HP):
                head(p, True)

        @pl.when(pl.program_id(2) == 0)
        def _():
            for p in range(HP):
                head(p, False)

    one = pl.BlockSpec((1, HP, TM, HD), lambda b, h, s: (b, h, rmap(s), 0))
    two = pl.BlockSpec((1, 1, HP, TM, HD), lambda b, h, s: (d, b, h, rmap(s), 0))
    return pl.pallas_call(
        body, name="scan_fwd_rev" if rev else "scan_fwd", grid=(B, NH // HP, nb),
        in_specs=[one, two, one, two],
        out_specs=[one, pl.BlockSpec((1, HP, 1, HD, HD), lambda b, h, s: (b, h, s, 0, 0))],
        out_shape=[jax.ShapeDtypeStruct(Q.shape, F32), jax.ShapeDtypeStruct((B, NH, nb, HD, HD), F32)],
        scratch_shapes=[pltpu.VMEM((HP, HD, HD), F32)],
        compiler_params=_cp(3))(Q, K, V, G)


def _scan_bwd(Q, K, V, G, St, dO, rev, B):
    nb = Q.shape[2] // TM
    d = 1 if rev else 0
    smap = lambda t: nb - 1 - t
    rmap = (lambda t: jnp.where(smap(t) == 0, 0, nb - smap(t))) if rev else smap

    def body(q_ref, k_ref, v_ref, g_ref, st_ref, do_ref, dq_ref, dk_ref, dv_ref, dg_ref, ds_scr):
        t = pl.program_id(2)

        @pl.when(t == 0)
        def _():
            ds_scr[...] = jnp.zeros_like(ds_scr)

        is_lat = smap(t) >= 1
        rin, mask, maskT, mexp = _scan_masks(rev)
        for p in range(HPB):
            b, xx, eb, qd, ki, kt = _scan_decay(q_ref[0, p], k_ref[0, 0, p], g_ref[0, 0, p], rin, rev)
            qb, kib, ktb, vb = qd.astype(BF16), ki.astype(BF16), kt.astype(BF16), v_ref[0, p]
            dob = jnp.where(is_lat, do_ref[0, p], 0.0).astype(BF16)
            kt_exps = [_expand(_sub(ktb, s), mexp) for s in range(NSB)]
            uts = [_dot_tn(_sub(vb, s), kt_exps[s]) for s in range(NSB)]
            states, _ = _state_pass(st_ref[0, p, 0], eb, uts, rev)
            gts = [_dot_tn(_sub(dob, s), _expand(_sub(qb, s), mexp)) for s in range(NSB)]
            order = range(NCB) if rev else range(NCB - 1, -1, -1)
            dsp = [None] * NCB
            t2 = [None] * NCB
            dsc = ds_scr[p]
            for n in order:
                row = n * CH if rev else n * CH + CH - 1
                ebl = eb[row:row + 1, :]
                dsp[n] = dsc
                t2[n] = jnp.broadcast_to(ebl * _rsum(states[n] * dsc), (CH, HD))
                dsc = gts[n // CSB][:, (n % CSB) * HD:(n % CSB + 1) * HD] + ebl * dsc
            ds_scr[p] = dsc
            dqds, dkis, dkts = [], [], []
            for s in range(NSB):
                q_s, ki_s, v_s, do_s = _sub(qb, s), _sub(kib, s), _sub(vb, s), _sub(dob, s)
                dspb = _stack(dsp, s).astype(BF16)
                da = jnp.where(mask, _dot_nt(do_s, v_s), 0.0).astype(BF16)
                dat = jnp.where(maskT, _dot_nt(v_s, do_s), 0.0).astype(BF16)
                at = jnp.where(maskT, _dot_nt(ki_s, q_s), 0.0).astype(BF16)
                dqds.append(_dot(da, ki_s) + _own(_dot(do_s, _stack(states, s).astype(BF16)), mexp))
                dkis.append(_dot(dat, q_s))
                dv_ref[0, p, s * SB:(s + 1) * SB, :] = (_dot(at, do_s) + _dot_nt(kt_exps[s], dspb)).astype(BF16)
                dkts.append(_own(_dot(v_s, dspb), mexp))
            dqd, dki, dkt = (jnp.concatenate(parts, axis=0) for parts in (dqds, dkis, dkts))
            z = dkt * kt
            db = dqd * qd - dki * ki
            dq_ref[0, p] = (dqd * eb).astype(BF16)
            dk_ref[0, p] = (dki * jnp.exp(-b) + dkt * jnp.exp(xx)).astype(BF16)
            dg_ref[0, p] = (_chunk_total(db) + (db - z) + _chunk_scan(z - db, rin, not rev)
                            + jnp.concatenate(t2, axis=0)).astype(BF16)

    one = pl.BlockSpec((1, HPB, TM, HD), lambda b, h, t: (b, h, rmap(t), 0))
    two = pl.BlockSpec((1, 1, HPB, TM, HD), lambda b, h, t: (d, b, h, rmap(t), 0))
    lat = pl.BlockSpec((1, HPB, TM, HD), lambda b, h, t: (b, h, jnp.clip(rmap(t) - 1, 0, nb - 2), 0))
    shp = jax.ShapeDtypeStruct(Q.shape, BF16)
    return pl.pallas_call(
        body, name="scan_bwd_rev" if rev else "scan_bwd", grid=(B, NH // HPB, nb),
        in_specs=[one, two, one, two, pl.BlockSpec((1, HPB, 1, HD, HD), lambda b, h, t: (b, h, smap(t), 0, 0)), lat],
        out_specs=[one, one, one, one], out_shape=[shp, shp, shp, shp],
        scratch_shapes=[pltpu.VMEM((HPB, HD, HD), F32)],
        compiler_params=_cp(3))(Q, K, V, G, St, dO)


def _gnorm_mix_out(O0, O1, F, gains, W, B, ntl):
    nt = ntl + 1

    def body(o0_ref, o1_ref, f_ref, gn_ref, w_ref, m_ref, y_ref):
        for h in range(NH):
            o = o0_ref[0, h] + o1_ref[0, h]
            r = lax.rsqrt(jnp.mean(o * o, axis=-1, keepdims=True) + NORM_EPS)
            gn = gn_ref[0:1, :] if h < 4 else gn_ref[1:2, :]
            gt = f_ref[:, h * HD:(h + 1) * HD]
            m_ref[:, h * HD:(h + 1) * HD] = (o * r * gn * (gt * _sigmoid(gt))).astype(BF16)
        y_ref[...] = _dot(m_ref[...], w_ref[...]).astype(BF16)

    ospec = pl.BlockSpec((1, NH, TM, HD), lambda b, i: (b, 0, i + 1, 0))
    rows = pl.BlockSpec((TM, D), _rowmap(ntl, 0))
    shp = jax.ShapeDtypeStruct((B * ntl * TM, D), BF16)
    return pl.pallas_call(
        body, name="gated_norm_mix_out", grid=(B, ntl),
        in_specs=[ospec, ospec, pl.BlockSpec((TM, D), lambda b, i: (b * nt + 1 + i, 0)),
                  pl.BlockSpec((2, HD), lambda b, i: (0, 0)), _wspec(W, 2)],
        out_specs=[rows, rows], out_shape=[shp, shp], compiler_params=_cp(2))(O0, O1, F, gains, W)


def _mix_out_dx_gnorm_bwd(dY, W, O0, O1, F, gains, dep, B, ntl):
    nt = ntl + 1

    def body(dy_ref, w_ref, o0_ref, o1_ref, f_ref, gn_ref, dep_ref, do_ref, dgt_ref, dgn_ref):
        b_, i = pl.program_id(0), pl.program_id(1)

        @pl.when((b_ == 0) & (i == 0))
        def _():
            dgn_ref[...] = jnp.zeros_like(dgn_ref)

        dyv = dy_ref[...]
        for h in range(NH):
            o = o0_ref[0, h] + o1_ref[0, h]
            r = lax.rsqrt(jnp.mean(o * o, axis=-1, keepdims=True) + NORM_EPS)
            y = o * r
            gn = gn_ref[0:1, :] if h < 4 else gn_ref[1:2, :]
            gt = f_ref[:, h * HD:(h + 1) * HD]
            s = _sigmoid(gt)
            dm = _dot_nt(dyv, w_ref[h * HD:(h + 1) * HD, :])
            don = dm * (gt * s)
            dgt_ref[:, h * HD:(h + 1) * HD] = (dm * (y * gn) * (s * (1.0 + gt * (1.0 - s)))).astype(BF16)
            row = 0 if h < 4 else 1
            dgn_ref[row:row + 1, :] += _rsum(don * y)
            dy = don * gn
            do_ref[0, h] = (r * (dy - y * jnp.mean(dy * y, axis=-1, keepdims=True))).astype(BF16)

    ospec = pl.BlockSpec((1, NH, TM, HD), lambda b, i: (b, 0, i + 1, 0))
    return pl.pallas_call(
        body, name="mix_out_dx_gated_norm_bwd", grid=(B, ntl),
        in_specs=[pl.BlockSpec((TM, D), _rowmap(ntl, 0)), _wspec(W, 2), ospec, ospec,
                  pl.BlockSpec((TM, D), lambda b, i: (b * nt + 1 + i, 0)), pl.BlockSpec((2, HD), lambda b, i: (0, 0)),
                  _wspec(dep, 2)],
        out_specs=[pl.BlockSpec((1, NH, TM, HD), lambda b, i: (b, 0, i, 0)), pl.BlockSpec((TM, D), _rowmap(ntl, 0)),
                   pl.BlockSpec((2, HD), lambda b, i: (0, 0))],
        out_shape=[jax.ShapeDtypeStruct((B, NH, ntl * TM, HD), BF16), jax.ShapeDtypeStruct((B * ntl * TM, D), BF16),
                   jax.ShapeDtypeStruct((2, HD), F32)],
        compiler_params=_cp(2))(dY, W, O0, O1, F, gains, dep)


def _sincos_2d(rows, width, dim):
    quarter = dim // 4
    omega = 1.0 / 10000.0 ** (jnp.arange(quarter, dtype=F32) / quarter)

    def emb(n):
        a = jnp.arange(n).astype(F32)[:, None] * omega[None, :]
        return jnp.concatenate([jnp.sin(a), jnp.cos(a)], axis=-1)

    er = jnp.broadcast_to(emb(rows)[:, None, :], (rows, width, dim // 2))
    ec = jnp.broadcast_to(emb(width)[None, :, :], (rows, width, dim // 2))
    return jnp.concatenate([er, ec], axis=-1).reshape(rows * width, dim)


def _pad_heads(w):
    k = w.shape[0]
    return jnp.pad(w.reshape(k, 4, GLA_DK), ((0, 0), (0, 0), (0, HD - GLA_DK))).reshape(k, 4 * HD)


def _unpad_heads(w):
    k = w.shape[0]
    return w.reshape(k, 4, HD)[:, :, :GLA_DK].reshape(k, 4 * GLA_DK)


MIX_N = 1032
MIX_NP = 1152
_SEGS = ([(64 * h, 64, GQ + HD * h) for h in range(4)] + [(256 + 64 * h, 64, GK + HD * h) for h in range(4)]
         + [(512, 512, GV), (1024, 512, GG), (1536, 32, LR), (1568, 512, RQ), (2080, 512, RFF), (2592, 512, RFB),
            (3104, 512, RI), (3616, 512, RG)])


def _mix_in_to_padded(ps):
    k = ps.shape[1]
    parts, pos = [], 0
    for g0, ln, s0 in sorted(_SEGS, key=lambda s: s[2]):
        if s0 > pos:
            parts.append(jnp.zeros((k, s0 - pos), ps.dtype))
        for j in range(4):
            lo, hi = max(g0, j * MIX_N), min(g0 + ln, (j + 1) * MIX_N)
            if lo < hi:
                parts.append(ps[j][:, lo - j * MIX_N:hi - j * MIX_N])
        pos = s0 + ln
    parts.append(jnp.zeros((k, MIXP - pos), ps.dtype))
    return jnp.concatenate(parts, axis=1)


def _mix_in_from_padded(g):
    k = g.shape[0]
    shards = []
    for j in range(4):
        parts = []
        for g0, ln, s0 in sorted(_SEGS):
            lo, hi = max(g0, j * MIX_N), min(g0 + ln, (j + 1) * MIX_N)
            if lo < hi:
                parts.append(g[:, s0 + lo - g0:s0 + hi - g0])
        parts.append(jnp.zeros((k, MIX_NP - MIX_N), g.dtype))
        shards.append(jnp.concatenate(parts, axis=1))
    return jnp.stack(shards)


def _local_step(x, ctx, tgt, mvec, weights_for, prefetch, ln_gain, ln_bias, a2f, a2b, abf, abb, lb, gng, gnh, on_grads, on_sent):
    B, T, _ = x.shape
    assert ctx.shape[1] == TM and T % TM == 0
    ntl = T // TM
    nt = ntl + 1
    C, L, CL = (nt, 0), (ntl, 0), (nt, 1)
    pos = _sincos_2d(T // 64, 64, D)
    gbs = [jnp.stack([ln_gain[i], ln_bias[i]]) for i in range(3)]
    a2p = jnp.zeros((2, HD, 4 * HD), F32)
    a2p = a2p.at[0, 0:16].set(_pad_heads(a2f)).at[1, 16:32].set(_pad_heads(a2b)).astype(BF16)
    biasp = jnp.concatenate([_pad_heads(abf.reshape(1, -1)), _pad_heads(abb.reshape(1, -1))], axis=0)
    gains = jnp.concatenate([gng.reshape(1, HD), gnh.reshape(1, HD)], axis=0)

    X0, h0 = _embed_lnmod(x, ctx, pos, mvec)
    w1i, w1o = weights_for("ffn1", h0)
    u0, a0 = _ffn_in(h0, C, w1i, B, nt, "ffn1_in")
    tok = prefetch("mix", a0)
    y0 = _mm_nn(a0, C, w1o, B, nt, BF16, "ffn1_out")
    wmp, wmo = weights_for("mix", y0)
    Q, K, V, G, Zm, X1, h1 = _mix_in_features(None, wmp, a2p, biasp, lb, B, nt, pre=dict(
        X=X0, lx=C, Y=y0, mvec=mvec, comb=True, sub=0, w=0.5, gb=gbs[0] + tok[0, 0]))
    prefetch("ffn2", Zm)
    O0, S0 = _scan_fwd(Q, K, V, G, False, B)
    O1, S1 = _scan_fwd(Q, K, V, G, True, B)
    merged, y1 = _gnorm_mix_out(O0, O1, Zm, gains, wmo, B, ntl)
    w2i, w2o = weights_for("ffn2", y1)
    u2, a2, X2, h2 = _ffn_in(None, L, w2i, B, ntl, "ffn2_in", pre=dict(
        X=X1, lx=CL, Y=y1, mvec=mvec, comb=False, sub=1, w=1.0, gb=gbs[1]))
    y2 = _mm_nn(a2, L, w2o, B, ntl, BF16, "ffn2_out")

    dx2r, dy2, dgate2, dgb2, lsum, du2 = _tail(X2, y2, mvec, gbs[2], tgt, 2, 0.5, w2o, u2, B, ntl)
    loss = (0.5 / D) * jnp.sum(lsum)
    g_w2o = _mm_tn(a2, dy2, "ffn2_out_dw")
    dh2 = _mm_nt(du2, w2i, "ffn2_in_dx", out_dtype=BF16)
    g_w2i = _mm_tn(h2, du2, "ffn2_in_dw", shards=4)
    tok = on_grads("ffn2", (g_w2i, g_w2o))
    dx1r, dy1, dgate1, dgb1, dss2 = _lnmod_resid_bwd(dh2, X2, L, dx2r, X1, CL, y1, mvec, False, 2, 1.0,
                                                     gbs[1] + tok[0, 0], B, ntl, "lnmod2_resid1_bwd")
    tok = on_sent("ffn2", dy1)
    g_wmo = _mm_tn(merged, dy1, "mix_out_dw", tn=D)
    dO, dgates, dgains = _mix_out_dx_gnorm_bwd(dy1, wmo, O0, O1, Zm, gains, tok, B, ntl)
    dQ0, dK0, dV0, dG0 = _scan_bwd(Q, K, V, G, S0, dO, False, B)
    dQ1, dK1, dV1, dG1 = _scan_bwd(Q, K, V, G, S1, dO, True, B)
    dF, dh1, da2p, dbiasp, dlb = _features_bwd(Zm, a2p, biasp, lb, dQ0, dQ1, dK0, dK1, dV0, dV1, dG0, dG1, dgates,
                                               wmp, B)
    g_wmp = _mm_tn(h1, dF, "mix_in_dw", tn=MIXP // 4)
    tok = on_grads("mix", (g_wmp, g_wmo))
    dx0r, dy0, dgate0, dgb0, dss1 = _lnmod_resid_bwd(dh1, X1, C, dx1r, X0, C, y0, mvec, True, 1, 0.5,
                                                     gbs[0] + tok[0, 0], B, nt, "lnmod1_resid0_bwd")
    tok = on_sent("mix", dy0)
    du0 = _ffn_out_dx(dy0, w1o, u0, "ffn1_out_dx", dep=tok)
    g_w1o = _mm_tn(a0, dy0, "ffn1_out_dw")
    g_w1i = _mm_tn(h0, du0, "ffn1_in_dw", shards=4)
    tok = on_grads("ffn1", (g_w1i, g_w1o))
    dh0 = _mm_nt(du0, w1i, "ffn1_in_dx", dep=tok, out_dtype=BF16)
    tok = on_sent("ffn1", dh0)
    grad_x, dss0 = _lnmod0_bwd(dh0, X0, mvec + tok[0, 0], dx0r, x.shape, B, nt)

    zero_ctx = lambda a: a.at[:, 0].set(0.0)
    dm = jnp.concatenate([dss0, dgate0, dss1, zero_ctx(dgate1), zero_ctx(dss2), zero_ctx(dgate2)], axis=2)
    small = dict(
        ln_gain=jnp.stack([dgb0[0], dgb1[0], dgb2[0]]), ln_bias=jnp.stack([dgb0[1], dgb1[1], dgb2[1]]),
        a2f=_unpad_heads(da2p[0, 0:16]), a2b=_unpad_heads(da2p[1, 16:32]),
        abf=_unpad_heads(dbiasp[0:1]), abb=_unpad_heads(dbiasp[1:2]), lb=dlb, gng=dgains[0], gnh=dgains[1])
    return loss, grad_x, dm, small


def _small_allgather(xs, name):
    r, n = xs.shape

    def body(x_ref, out_ref, send_sems, recv_sems, local_sem):
        x, y, c = lax.axis_index("x"), lax.axis_index("y"), lax.axis_index("c")
        me, sibling = (x, y, c), (x, y, 1 - c)
        chips = [(1 - x, y), (x, 1 - y), (1 - x, 1 - y)]

        def rows(px, py, pc):
            return out_ref.at[pl.ds((4 * px + 2 * py + pc) * r, r), :]

        def copy(k, block, to, src=None):
            return pltpu.make_async_remote_copy(
                src_ref=rows(*block) if src is None else src, dst_ref=rows(*block),
                send_sem=send_sems.at[k], recv_sem=recv_sems.at[k], device_id=to, device_id_type=MESH)

        mine = pltpu.make_async_copy(x_ref, rows(*me), local_sem)
        mine.start()
        first = [copy(0, me, sibling, src=x_ref)]
        first += [copy(1 + j, me, (*chip, c), src=x_ref) for j, chip in enumerate(chips)]
        for cp in first:
            cp.start()
        passed = [copy(4 + j, (*chip, c), sibling) for j, chip in enumerate(chips)]
        for j, chip in enumerate(chips):
            copy(1 + j, (*chip, c), me).wait_recv()
            passed[j].start()
        copy(0, sibling, me).wait_recv()
        for j, chip in enumerate(chips):
            copy(4 + j, (*chip, 1 - c), me).wait_recv()
        for cp in first + passed:
            cp.wait_send()
        mine.wait()

    out = pl.pallas_call(
        body, name=name,
        out_shape=jax.ShapeDtypeStruct((8 * r, n), xs.dtype),
        in_specs=[pl.BlockSpec(memory_space=pltpu.VMEM)],
        out_specs=pl.BlockSpec(memory_space=pltpu.VMEM),
        scratch_shapes=[pltpu.SemaphoreType.DMA((7,)), pltpu.SemaphoreType.DMA((7,)), pltpu.SemaphoreType.DMA],
        compiler_params=pltpu.CompilerParams(vmem_limit_bytes=VMEM_LIMIT))(xs)
    return out.reshape(8, r, n)


def _gather_flat(v, name):
    n = v.shape[0]
    npad = -(-n // 1024) * 1024
    g = _small_allgather(jnp.pad(v, (0, npad - n)).reshape(8, npad // 8), name)
    return g.reshape(8, npad)[:, :n]


HBM_SPEC = pl.BlockSpec(memory_space=pltpu.HBM)
SEM_SPEC = pl.BlockSpec(memory_space=pltpu.SEMAPHORE)
DATAFLOW = pltpu.SideEffectType.DATAFLOW_SIDE_EFFECTING


def _gather_copies(xs, outs, send_sems, recv_sems):
    x, y, c = lax.axis_index("x"), lax.axis_index("y"), lax.axis_index("c")
    dests = [(x, y, 1 - c), (1 - x, y, c), (x, 1 - y, c), (1 - x, 1 - y, c)]
    return [pltpu.make_async_remote_copy(
        src_ref=xs[w], dst_ref=outs[w].at[4 * x + 2 * y + c], send_sem=send_sems[4 * w + k],
        recv_sem=recv_sems[4 * w + k], device_id=dests[k], device_id_type=MESH)
        for w in range(len(xs)) for k in range(4)]


def _split_start(copies, per_w, srcs, land_lead, after, name):
    n = len(srcs)
    m = per_w * n
    lands = [lax.empty((land_lead,) + s.shape[-2:], s.dtype) for s in srcs]
    deps = [] if after is None else [after]

    def body(*refs):
        xs, ls, outs = refs[:n], refs[n:2 * n], refs[2 * n + len(deps):]
        for cp in copies(xs, ls, outs[:m], outs[m:2 * m]):
            cp.start()
        token = outs[2 * m + 2 * n]
        token[...] = jnp.zeros_like(token)

    outs = pl.pallas_call(
        body, name=name,
        out_shape=([pltpu.SemaphoreType.DMA(())] * (2 * m) + [pltpu.HBM(a.shape, a.dtype) for a in srcs + lands]
                   + [jax.ShapeDtypeStruct((8, 128), F32)]),
        in_specs=[HBM_SPEC] * (2 * n) + [pl.BlockSpec(memory_space=pl.ANY)] * len(deps),
        out_specs=[SEM_SPEC] * (2 * m) + [HBM_SPEC] * (2 * n) + [pl.BlockSpec(memory_space=pltpu.VMEM)],
        input_output_aliases={w: 2 * m + w for w in range(2 * n)},
        compiler_params=pltpu.CompilerParams(has_side_effects=DATAFLOW),
    )(*[pltpu.with_memory_space_constraint(a, pltpu.HBM) for a in srcs + lands], *deps)
    return outs[:2 * m], outs[2 * m:2 * m + n], outs[2 * m + n:2 * m + 2 * n], outs[2 * m + 2 * n]


def _split_wait(copies, per_w, sems, x_thru, l_thru, after, name):
    n = len(x_thru)
    m = per_w * n

    def body(*refs):
        xs, ls, ss = refs[:n], refs[n:2 * n], refs[2 * n:2 * n + 2 * m]
        for cp in copies(xs, ls, ss[:m], ss[m:]):
            cp.wait_send()
            cp.wait_recv()

    outs = pl.pallas_call(
        body, name=name,
        out_shape=[pltpu.HBM(a.shape, a.dtype) for a in list(x_thru) + list(l_thru)],
        in_specs=[HBM_SPEC] * (2 * n) + [SEM_SPEC] * (2 * m) + [pl.BlockSpec(memory_space=pl.ANY)],
        out_specs=[HBM_SPEC] * (2 * n),
        input_output_aliases={w: w for w in range(2 * n)},
        compiler_params=pltpu.CompilerParams(has_side_effects=DATAFLOW),
    )(*x_thru, *l_thru, *sems, after)
    return outs[:n], outs[n:]


def _gather_forward(gathered, name):
    n = len(gathered)

    def body(*refs):
        outs = refs[n:2 * n]
        send_sems, recv_sems = refs[2 * n:]
        x, y, c = lax.axis_index("x"), lax.axis_index("y"), lax.axis_index("c")
        chips = [(1 - x, y), (x, 1 - y), (1 - x, 1 - y), (x, y)]

        def copy(w, j, pc):
            px, py = chips[j]
            slot = outs[w].at[4 * px + 2 * py + pc]
            return pltpu.make_async_remote_copy(
                src_ref=slot, dst_ref=slot, send_sem=send_sems.at[4 * w + j], recv_sem=recv_sems.at[4 * w + j],
                device_id=(x, y, 1 - c), device_id_type=MESH)

        sends = [copy(w, j, 1 - c if j == 3 else c) for w in range(n) for j in range(4)]
        for cp in sends:
            cp.start()
        for w in range(n):
            for j in range(4):
                copy(w, j, c if j == 3 else 1 - c).wait_recv()
        for cp in sends:
            cp.wait_send()

    any_spec = pl.BlockSpec(memory_space=pl.ANY)
    return pl.pallas_call(
        body, name=name,
        out_shape=[jax.ShapeDtypeStruct(g.shape, g.dtype) for g in gathered],
        in_specs=[any_spec] * n, out_specs=[any_spec] * n,
        input_output_aliases={w: w for w in range(n)},
        scratch_shapes=[pltpu.SemaphoreType.DMA((4 * n,)), pltpu.SemaphoreType.DMA((4 * n,))],
    )(*gathered)


def _forward_copies(gs, ls, send_sems, recv_sems):
    x, y, c = lax.axis_index("x"), lax.axis_index("y"), lax.axis_index("c")
    chips = [(1 - x, y), (x, 1 - y), (1 - x, 1 - y), (x, y)]
    cps = []
    for w in range(len(gs)):
        for j, (px, py) in enumerate(chips):
            slot = gs[w].at[4 * px + 2 * py + (1 - c if j == 3 else c)]
            cps.append(pltpu.make_async_remote_copy(
                src_ref=slot, dst_ref=slot, send_sem=send_sems[4 * w + j], recv_sem=recv_sems[4 * w + j],
                device_id=(x, y, 1 - c), device_id_type=MESH))
    return cps


def _pair_copies(gs, ls, send_sems, recv_sems):
    x, y, c = lax.axis_index("x"), lax.axis_index("y"), lax.axis_index("c")
    return [pltpu.make_async_remote_copy(
        src_ref=gs[w].at[2 * j + 1 - c], dst_ref=ls[w].at[j], send_sem=send_sems[4 * w + j],
        recv_sem=recv_sems[4 * w + j], device_id=(x, y, 1 - c), device_id_type=MESH)
        for w in range(len(gs)) for j in range(4)]


def _all_copies(xs, ls, send_sems, recv_sems):
    x, y, c = lax.axis_index("x"), lax.axis_index("y"), lax.axis_index("c")
    flips = [(a, b, e) for a in (0, 1) for b in (0, 1) for e in (0, 1)][1:]
    return [pltpu.make_async_remote_copy(
        src_ref=xs[0], dst_ref=ls[0].at[4 * x + 2 * y + c], send_sem=send_sems[k], recv_sem=recv_sems[k],
        device_id=(1 - x if a else x, 1 - y if b else y, 1 - c if e else c), device_id_type=MESH)
        for k, (a, b, e) in enumerate(flips)]


def _share_copies(fs, ls, send_sems, recv_sems):
    x, y, c = lax.axis_index("x"), lax.axis_index("y"), lax.axis_index("c")
    return [pltpu.make_async_remote_copy(
        src_ref=fs[w].at[c], dst_ref=fs[w].at[c], send_sem=send_sems[w], recv_sem=recv_sems[w],
        device_id=(x, y, 1 - c), device_id_type=MESH) for w in range(len(fs))]


def _chip_copies(hs, ls, send_sems, recv_sems):
    x, y, c = lax.axis_index("x"), lax.axis_index("y"), lax.axis_index("c")
    chips = [(1 - x, y), (x, 1 - y), (1 - x, 1 - y)]
    return [pltpu.make_async_remote_copy(
        src_ref=hs[w].at[2 * px + py], dst_ref=ls[w].at[k], send_sem=send_sems[3 * w + k],
        recv_sem=recv_sems[3 * w + k], device_id=(px, py, c), device_id_type=MESH)
        for w in range(len(hs)) for k, (px, py) in enumerate(chips)]


def _rs_add_pair(g8, r4, c, name):
    R, n = g8.shape[1:]
    rb = R // 2

    def body(c_ref, g_ref, r_ref, o_ref):
        o_ref[...] = (g_ref[...] + r_ref[...]).astype(BF16)

    spec = pl.BlockSpec((1, rb, n), lambda j, i, c_ref: (j, i, 0))
    return pl.pallas_call(
        body, name=name,
        grid_spec=pltpu.PrefetchScalarGridSpec(
            num_scalar_prefetch=1, grid=(4, R // rb),
            in_specs=[pl.BlockSpec((1, rb, n), lambda j, i, c_ref: (2 * j + c_ref[0], i, 0)), spec],
            out_specs=spec),
        out_shape=jax.ShapeDtypeStruct((4, R, n), BF16), compiler_params=_cp(2))(c, g8, r4)


def _rs_add_chips(g8, r4, r3, cj, name):
    R, n = g8.shape[1:]
    rb = R // 2

    def body(cj_ref, g_ref, p_ref, r_ref, o_ref):
        own = g_ref[0] + p_ref[0]
        o_ref[0] = ((own + r_ref[0].astype(F32)) + r_ref[1].astype(F32)) + r_ref[2].astype(F32)

    return pl.pallas_call(
        body, name=name,
        grid_spec=pltpu.PrefetchScalarGridSpec(
            num_scalar_prefetch=1, grid=(R // rb,),
            in_specs=[pl.BlockSpec((1, rb, n), lambda i, cj_ref: (2 * cj_ref[1] + cj_ref[0], i, 0)),
                      pl.BlockSpec((1, rb, n), lambda i, cj_ref: (cj_ref[1], i, 0)),
                      pl.BlockSpec((3, rb, n), lambda i, cj_ref: (0, i, 0))],
            out_specs=pl.BlockSpec((1, rb, n), lambda i, cj_ref: (cj_ref[0], i, 0))),
        out_shape=jax.ShapeDtypeStruct((2, R, n), F32), compiler_params=_cp(1))(cj, g8, r4, r3)


def _sum8(g):
    n = g.shape[1]

    def body(g_ref, o_ref):
        acc = g_ref[0:1, :]
        for k in range(1, 8):
            acc = acc + g_ref[k:k + 1, :]
        o_ref[...] = acc

    return pl.pallas_call(body, name="sum_devices", out_shape=jax.ShapeDtypeStruct((1, n), F32),
                          compiler_params=pltpu.CompilerParams(vmem_limit_bytes=VMEM_LIMIT))(g)


ADA_ROWS = 64


def _ada_fwd(cs, w, b):
    n = w.shape[1]

    def body(c_ref, w_ref, b_ref, o_ref):
        cv = c_ref[...]
        s = (cv * _sigmoid(cv)).astype(BF16)
        o_ref[...] = _dot(s, w_ref[...].astype(BF16)) + b_ref[...]

    return pl.pallas_call(body, name="ada_fwd", out_shape=jax.ShapeDtypeStruct((ADA_ROWS, n), F32),
                          compiler_params=pltpu.CompilerParams(vmem_limit_bytes=VMEM_LIMIT))(cs, w, b)


def _mod_exchange(m_cols):
    n = m_cols.shape[1]

    def body(m_ref, out_ref, send_sems, recv_sems, local_sem):
        x, y, c = lax.axis_index("x"), lax.axis_index("y"), lax.axis_index("c")
        chips = [(1 - x, y), (x, 1 - y), (1 - x, 1 - y)]

        def group(px, py):
            return m_ref.at[pl.ds(pl.multiple_of(8 * (4 * px + 2 * py + c), 8), 8), :]

        mine = pltpu.make_async_copy(group(x, y), out_ref.at[2 * x + y], local_sem)
        mine.start()
        cps = [pltpu.make_async_remote_copy(
            src_ref=group(px, py), dst_ref=out_ref.at[2 * x + y], send_sem=send_sems.at[k], recv_sem=recv_sems.at[k],
            device_id=(px, py, c), device_id_type=MESH) for k, (px, py) in enumerate(chips)]
        for cp in cps:
            cp.start()
        for cp in cps:
            cp.wait_recv()
        for cp in cps:
            cp.wait_send()
        mine.wait()

    return pl.pallas_call(
        body, name="mod_exchange", out_shape=jax.ShapeDtypeStruct((4, 8, n), F32),
        in_specs=[pl.BlockSpec(memory_space=pltpu.VMEM)], out_specs=pl.BlockSpec(memory_space=pltpu.VMEM),
        scratch_shapes=[pltpu.SemaphoreType.DMA((3,)), pltpu.SemaphoreType.DMA((3,)), pltpu.SemaphoreType.DMA],
        compiler_params=pltpu.CompilerParams(vmem_limit_bytes=VMEM_LIMIT))(m_cols)


def _ada_bwd(cs, w, dm):
    n = w.shape[1]

    def body(c_ref, w_ref, dm_ref, gw_ref, dc_ref):
        cv = c_ref[...]
        s = (cv * _sigmoid(cv)).astype(BF16)
        gw_ref[...] = _dot_tn(s, dm_ref[...].astype(BF16))
        dc_ref[...] = _dot_nt(dm_ref[0:8, :].astype(BF16), w_ref[...].astype(BF16))

    return pl.pallas_call(
        body, name="ada_bwd",
        out_shape=[jax.ShapeDtypeStruct((D, n), F32), jax.ShapeDtypeStruct((8, D), F32)],
        compiler_params=pltpu.CompilerParams(vmem_limit_bytes=VMEM_LIMIT))(cs, w, dm)


def _adamw(w, g, m, v, name):
    r, c = w.shape
    rb = r
    if r % 8 == 0 and r * c * 4 > (1 << 20):
        rb = 8
        for cand in range(8, r, 8):
            if r % cand == 0 and cand * c * 4 <= (1 << 20):
                rb = cand

    def body(w_ref, g_ref, m_ref, v_ref, go_ref, d_ref, nm_ref, nv_ref):
        gv = g_ref[...]
        go_ref[...] = gv
        mn = ADAM_B1 * m_ref[...] + (1.0 - ADAM_B1) * gv
        vn = ADAM_B2 * v_ref[...] + (1.0 - ADAM_B2) * (gv * gv)
        m_hat = mn / (1.0 - ADAM_B1 ** ADAM_STEP)
        v_hat = vn / (1.0 - ADAM_B2 ** ADAM_STEP)
        d_ref[...] = -ADAM_LR * (m_hat / (jnp.sqrt(v_hat) + ADAM_EPS) + ADAM_WD * w_ref[...])
        nm_ref[...] = mn
        nv_ref[...] = vn

    spec = pl.BlockSpec((rb, c), lambda i: (i, 0))
    shp = jax.ShapeDtypeStruct((r, c), F32)
    return pl.pallas_call(body, name=name, grid=(r // rb,), in_specs=[spec] * 4, out_specs=[spec] * 4,
                          out_shape=[shp] * 4, compiler_params=_cp(1))(w, g, m, v)


BIG = ("ffn1_w_in", "ffn1_w_out", "w_mix_in", "w_mix_out", "ffn2_w_in", "ffn2_w_out")


def _half_rows(w, c):
    half = w.shape[0] // 2
    return lax.dynamic_slice_in_dim(w, c * half, half, axis=0)


def _lower_bounds(logits):
    return jnp.cumsum(jax.nn.softmax(logits.astype(F32), axis=1), axis=1)[:, 0]


def kernel(x, c, ctx, c_ctx, w_ada, b_ada, ln_gain, ln_bias, ffn1_w_in, ffn1_w_out, w_mix_in, gla_a2_fwd, gla_a2_bwd, gla_a_bias_fwd, gla_a_bias_bwd, hgrn_lb_logits, gla_norm_gain, hgrn_norm_gain, w_mix_out, ffn2_w_in, ffn2_w_out, loss_target, m_c_ctx, m_w_ada, m_b_ada, m_ln_gain, m_ln_bias, m_ffn1_w_in, m_ffn1_w_out, m_w_mix_in, m_gla_a2_fwd, m_gla_a2_bwd, m_gla_a_bias_fwd, m_gla_a_bias_bwd, m_hgrn_lb_logits, m_gla_norm_gain, m_hgrn_norm_gain, m_w_mix_out, m_ffn2_w_in, m_ffn2_w_out, v_c_ctx, v_w_ada, v_b_ada, v_ln_gain, v_ln_bias, v_ffn1_w_in, v_ffn1_w_out, v_w_mix_in, v_gla_a2_fwd, v_gla_a2_bwd, v_gla_a_bias_fwd, v_gla_a_bias_bwd, v_hgrn_lb_logits, v_gla_norm_gain, v_hgrn_norm_gain, v_w_mix_out, v_ffn2_w_in, v_ffn2_w_out):
    xi, yi, ci = lax.axis_index("x"), lax.axis_index("y"), lax.axis_index("c")
    chip = 2 * xi + yi
    dev = 2 * chip + ci
    B = x.shape[0]
    weights = dict(ffn1_w_in=ffn1_w_in[0], ffn1_w_out=ffn1_w_out[0], w_mix_in=w_mix_in[0], w_mix_out=w_mix_out[0],
                   ffn2_w_in=ffn2_w_in[0], ffn2_w_out=ffn2_w_out[0])

    mine = jnp.concatenate([c.reshape(-1), ln_gain.reshape(-1), ln_bias.reshape(-1), gla_a2_fwd.reshape(-1),
                            gla_a2_bwd.reshape(-1), hgrn_lb_logits.reshape(-1)])
    g1 = _gather_flat(mine, "gather_cond")
    nc = B * D
    c_all = g1[:, :nc].reshape(8 * B, D)
    per_chip = g1[0::2, nc:]
    o = 0

    def take(shape, axis):
        nonlocal o
        n = int(np.prod(shape))
        parts = per_chip[:, o:o + n].reshape((4,) + shape)
        o += n
        return jnp.concatenate([parts[j] for j in range(4)], axis=axis)

    ln_gain_f = take((3, 256), 1)
    ln_bias_f = take((3, 256), 1)
    a2f_f = take((16, 64), 1)
    a2b_f = take((16, 64), 1)
    lbl_f = take((2, 2, 128), 2)
    lb, lb_vjp = jax.vjp(_lower_bounds, lbl_f)

    assert B + 1 <= 8
    cs = jnp.concatenate([c_all.reshape(8, B, D), jnp.broadcast_to(c_ctx.reshape(1, 1, D), (8, 1, D)),
                          jnp.zeros((8, 7 - B, D), F32)], axis=1).reshape(ADA_ROWS, D)
    ncol = w_ada.shape[2]
    b_cols = lax.dynamic_slice_in_dim(b_ada, chip * ncol, ncol, axis=1)
    m4 = _mod_exchange(_ada_fwd(cs, w_ada[0], b_cols))
    m_all = jnp.concatenate([m4[j] for j in range(4)], axis=1)
    m_lat = m_all[:B].reshape(B, 1, N_MOD, D)
    m_ctx = jnp.broadcast_to(m_all[B].reshape(1, 1, N_MOD, D), (B, 1, N_MOD, D))

    groups = dict(ffn2=("ffn2_w_in", "ffn2_w_out"), mix=("w_mix_in", "w_mix_out"), ffn1=("ffn1_w_in", "ffn1_w_out"))
    shards = dict(weights, w_mix_in=jnp.pad(weights["w_mix_in"], ((0, 0), (0, MIX_NP - MIX_N))))
    blks = {k: _half_rows(shards[k], ci).astype(BF16) for k in BIG}
    gathering = {}
    token = m_all
    for group in ("ffn1", "mix", "ffn2"):
        sems, x_thru, l_thru, token = _split_start(_gather_copies, 4, [blks[k] for k in groups[group]], 8, token,
                                                   "weight_gather_start_" + group)
        gathering[group] = (sems, x_thru, l_thru)
    mvec = jnp.concatenate([m_ctx, m_lat], axis=1) + token[0, 0]

    forwarding = {}

    def prefetch(group, after):
        _, got = _split_wait(_gather_copies, 4, *gathering[group], after, "weight_gather_wait_" + group)
        forwarding[group] = _split_start(_forward_copies, 4, list(got), 1, None, "weight_gather_forward_start_" + group)
        return forwarding[group][3]

    def weights_for(group, after):
        names = groups[group]
        if group in forwarding:
            sems, g_thru, l_thru, _ = forwarding[group]
            (w_in, w_out), _ = _split_wait(_forward_copies, 4, sems, g_thru, l_thru, after,
                                           "weight_gather_forward_wait_" + group)
        else:
            _, got = _split_wait(_gather_copies, 4, *gathering[group], after, "weight_gather_wait_" + group)
            w_in, w_out = _gather_forward(got, "weight_gather_forward_" + group)
        if group == "mix":
            return _mix_in_to_padded(w_in.reshape(4, D, MIX_NP)), w_out.reshape(-1, D)
        return w_in.reshape((4,) + shards[names[0]].shape), w_out.reshape(-1, D)

    cvec = ci.reshape(1).astype(jnp.int32)
    cjvec = jnp.stack([ci, chip]).astype(jnp.int32)
    in_flight = {}

    def on_grads(group, gs):
        names = groups[group]
        if group == "mix":
            gs = (_mix_in_from_padded(gs[0]), gs[1])
        g8s = [g.reshape((8, shards[k].shape[0] // 2, shards[k].shape[1])) for k, g in zip(names, gs)]
        sems, g_thru, l_thru, token = _split_start(_pair_copies, 4, g8s, 4, None, "grad_pair_start_" + group)
        in_flight[group] = (sems, g_thru, l_thru)
        return token

    def on_sent(group, after):
        g8s, r4s = _split_wait(_pair_copies, 4, *in_flight[group], after, "grad_pair_wait_" + group)
        h4s = [_rs_add_pair(g, r, cvec, "grad_pair_add_" + k) for k, g, r in zip(groups[group], g8s, r4s)]
        sems, h_thru, l_thru, token = _split_start(_chip_copies, 3, h4s, 3, None, "grad_chip_start_" + group)
        in_flight[group] = (g8s, r4s, sems, h_thru, l_thru)
        return token

    loss_l, grad_x, dm, small = _local_step(
        x, ctx, loss_target, mvec, weights_for, prefetch, ln_gain_f, ln_bias_f, a2f_f, a2b_f,
        gla_a_bias_fwd, gla_a_bias_bwd, lb, gla_norm_gain, hgrn_norm_gain, on_grads, on_sent)
    loss = lax.psum(loss_l, ("x", "y", "c"))

    dm_lat = dm[:, 1].reshape(B, N_MOD * D)
    dm_ctx = jnp.sum(dm[:, 0], axis=0).reshape(N_MOD * D)
    keys = ("ln_gain", "ln_bias", "a2f", "a2b", "abf", "abb", "lb", "gng", "gnh")
    flat = jnp.concatenate([dm_lat.reshape(-1), dm_ctx] + [small[k].reshape(-1) for k in keys])
    nflat = flat.shape[0]
    npad = -(-nflat // 1024) * 1024
    flat2 = jnp.pad(flat, (0, npad - nflat)).reshape(8, npad // 8)
    s_sems, s_src, s_land, tok = _split_start(_all_copies, 7, [flat2], 8, None, "small_grads_start")

    fin = {}
    for group in ("ffn2", "mix", "ffn1"):
        g8s, r4s, sems, h_thru, l_thru = in_flight[group]
        _, r3s = _split_wait(_chip_copies, 3, sems, h_thru, l_thru, tok, "grad_chip_wait_" + group)
        for k, g, r4, r3 in zip(groups[group], g8s, r4s, r3s):
            fin[k] = _rs_add_chips(g, r4, r3, cjvec, "grad_chip_add_" + k)
    p_sems, p_src, p_land, tok = _split_start(_share_copies, 1, [fin[k] for k in BIG], 1, None, "grad_pair_share_start")

    _, (land,) = _split_wait(_all_copies, 7, s_sems, s_src, s_land, tok, "small_grads_wait")
    g3 = lax.dynamic_update_index_in_dim(land, flat2, dev, 0).reshape(8, npad)[:, :nflat]
    nlat = B * N_MOD * D
    dm_all = g3[:, :nlat].reshape(8 * B, N_MOD * D)
    tot = _sum8(g3[:, nlat:])[0]
    dmc_tot = tot[:N_MOD * D]
    o = N_MOD * D
    sg = {}
    for k in keys:
        n = int(np.prod(small[k].shape))
        sg[k] = tot[o:o + n].reshape(small[k].shape)
        o += n
    ctx_rows = jnp.zeros((8, 1, N_MOD * D), F32).at[0, 0].set(dmc_tot)
    dm_rows = jnp.concatenate([dm_all.reshape(8, B, -1), ctx_rows, jnp.zeros((8, 7 - B, N_MOD * D), F32)],
                              axis=1).reshape(ADA_ROWS, N_MOD * D)
    g_b_ada = (jnp.sum(dm_all, axis=0) + dmc_tot).reshape(1, N_MOD * D)
    g_w_ada, dcc = _ada_bwd(cs, w_ada[0], lax.dynamic_slice_in_dim(dm_rows, chip * ncol, ncol, axis=1))
    g4 = _gather_flat(dcc[B], "gather_cctx")
    dsilu = ((g4[0] + g4[2]) + g4[4]) + g4[6]
    sc = _sigmoid(c_ctx)
    g_c_ctx = dsilu * (sc * (1.0 + c_ctx * (1.0 - sc)))
    (g_lbl,) = lb_vjp(sg["lb"])

    def cols(a, n, axis):
        return lax.dynamic_slice_in_dim(a, chip * n, n, axis=axis)

    shared, _ = _split_wait(_share_copies, 1, p_sems, p_src, p_land, g_w_ada, "grad_pair_share_wait")
    gsh = {k: both.reshape(shards[k].shape)[:, :weights[k].shape[1]] for k, both in zip(BIG, shared)}

    grads = dict(
        c_ctx=g_c_ctx, w_ada=g_w_ada[None], b_ada=g_b_ada, ln_gain=cols(sg["ln_gain"], 256, 1)[None],
        ln_bias=cols(sg["ln_bias"], 256, 1)[None], ffn1_w_in=gsh["ffn1_w_in"][None], ffn1_w_out=gsh["ffn1_w_out"][None],
        w_mix_in=gsh["w_mix_in"][None], gla_a2_fwd=cols(sg["a2f"], 64, 1)[None], gla_a2_bwd=cols(sg["a2b"], 64, 1)[None],
        gla_a_bias_fwd=sg["abf"], gla_a_bias_bwd=sg["abb"], hgrn_lb_logits=cols(g_lbl, 128, 2),
        gla_norm_gain=sg["gng"].reshape(1, HD), hgrn_norm_gain=sg["gnh"].reshape(1, HD),
        w_mix_out=gsh["w_mix_out"][None], ffn2_w_in=gsh["ffn2_w_in"][None], ffn2_w_out=gsh["ffn2_w_out"][None])
    params = dict(
        c_ctx=(c_ctx, m_c_ctx, v_c_ctx), w_ada=(w_ada, m_w_ada, v_w_ada), b_ada=(b_ada, m_b_ada, v_b_ada),
        ln_gain=(ln_gain, m_ln_gain, v_ln_gain), ln_bias=(ln_bias, m_ln_bias, v_ln_bias),
        ffn1_w_in=(ffn1_w_in, m_ffn1_w_in, v_ffn1_w_in), ffn1_w_out=(ffn1_w_out, m_ffn1_w_out, v_ffn1_w_out),
        w_mix_in=(w_mix_in, m_w_mix_in, v_w_mix_in), gla_a2_fwd=(gla_a2_fwd, m_gla_a2_fwd, v_gla_a2_fwd),
        gla_a2_bwd=(gla_a2_bwd, m_gla_a2_bwd, v_gla_a2_bwd),
        gla_a_bias_fwd=(gla_a_bias_fwd, m_gla_a_bias_fwd, v_gla_a_bias_fwd),
        gla_a_bias_bwd=(gla_a_bias_bwd, m_gla_a_bias_bwd, v_gla_a_bias_bwd),
        hgrn_lb_logits=(hgrn_lb_logits, m_hgrn_lb_logits, v_hgrn_lb_logits),
        gla_norm_gain=(gla_norm_gain, m_gla_norm_gain, v_gla_norm_gain),
        hgrn_norm_gain=(hgrn_norm_gain, m_hgrn_norm_gain, v_hgrn_norm_gain),
        w_mix_out=(w_mix_out, m_w_mix_out, v_w_mix_out), ffn2_w_in=(ffn2_w_in, m_ffn2_w_in, v_ffn2_w_in),
        ffn2_w_out=(ffn2_w_out, m_ffn2_w_out, v_ffn2_w_out))
    order = list(params.keys())
    big_names = ("w_ada",) + BIG
    upd = {}
    for k in big_names:
        w_, m_, v_ = params[k]
        s2 = w_.shape[-2:]
        g_, d_, nm_, nv_ = _adamw(w_.reshape(s2), grads[k].reshape(s2), m_.reshape(s2), v_.reshape(s2), "adamw_" + k)
        grads[k] = g_
        upd[k] = (d_.reshape(w_.shape), nm_.reshape(w_.shape), nv_.reshape(w_.shape))
    small_names = [k for k in order if k not in big_names]
    sizes = [int(np.prod(params[k][0].shape)) for k in small_names]
    tot_n = sum(sizes)
    npad = -(-tot_n // 1024) * 1024

    def packed(get):
        flat_ = jnp.concatenate([get(k).reshape(-1) for k in small_names])
        return jnp.pad(flat_, (0, npad - tot_n)).reshape(8, npad // 8)

    _, d_s, nm_s, nv_s = _adamw(packed(lambda k: params[k][0]), packed(lambda k: grads[k]),
                                packed(lambda k: params[k][1]), packed(lambda k: params[k][2]), "adamw_small")
    o = 0
    for k, n in zip(small_names, sizes):
        shp = params[k][0].shape
        upd[k] = tuple(a.reshape(-1)[o:o + n].reshape(shp) for a in (d_s, nm_s, nv_s))
        o += n

    return (loss, grad_x, *[grads[k].reshape(params[k][0].shape) for k in order], *[upd[k][0] for k in order],
            *[upd[k][1] for k in order], *[upd[k][2] for k in order])
```

```python
import functools

import numpy as np
import jax
import jax.numpy as jnp
from jax import lax
from jax.experimental import pallas as pl
from jax.experimental.pallas import tpu as pltpu

F32 = jnp.float32
BF16 = jnp.bfloat16
MESH = pl.DeviceIdType.MESH

D = 1024
DFF = 2816
TM = 256
CH = 32
NCB = TM // CH
SB = 128
CSB = SB // CH
NSB = TM // SB
HP = 8
HPB = 8
HD = 128
NH = 8
LN_EPS = 1e-5
NORM_EPS = 1e-6
ALPHA = 2.0 ** 0.25
GATE_NORM = 16.0
GLA_DK = 64
N_MOD = 9
VMEM_LIMIT = 52 * 1024 * 1024

MIXP = 5120
GG, RG, GQ, GK, GV, RQ, RFF, RFB, RI, LR = 0, 512, 1024, 1536, 2048, 2560, 3072, 3584, 4096, 4608
IN_SPLITS = (256, 256, 512, 512, 16, 16, 512, 512, 512, 512, 512)

ADAM_LR, ADAM_B1, ADAM_B2, ADAM_EPS, ADAM_WD, ADAM_STEP = 0.001, 0.9, 0.999, 1e-08, 0.01, 10


def _cp(n_axes):
    return pltpu.CompilerParams(dimension_semantics=("arbitrary",) * n_axes, vmem_limit_bytes=VMEM_LIMIT)


def _rowmap(stride, off):
    return lambda b, i: (b * stride + off + i, 0)


def _mmap(comb):
    if comb:
        return lambda b, i: (b, jnp.minimum(i, 1), 0, 0)
    return lambda b, i: (b, 1, 0, 0)


def _ln(x):
    mu = jnp.mean(x, axis=-1, keepdims=True)
    xc = x - mu
    var = jnp.mean(xc * xc, axis=-1, keepdims=True)
    r = lax.rsqrt(var + LN_EPS)
    return xc * r, r


def _ln_bwd(dxh, xh, r):
    return r * (dxh - jnp.mean(dxh, axis=-1, keepdims=True) - xh * jnp.mean(dxh * xh, axis=-1, keepdims=True))


def _sigmoid(x):
    return 1.0 / (1.0 + jnp.exp(-x))


def _rsum(x):
    return jnp.sum(x, axis=0, keepdims=True)


def _dot(a, b):
    return jnp.dot(a, b, preferred_element_type=F32)


def _dot_nt(a, b):
    return lax.dot_general(a, b, (((1,), (1,)), ((), ())), preferred_element_type=F32)


def _dot_tn(a, b):
    return lax.dot_general(a, b, (((0,), (0,)), ((), ())), preferred_element_type=F32)


def _modulate(xv, m_ref, sub):
    xh, _ = _ln(xv)
    sh = m_ref[0, 0, 3 * sub:3 * sub + 1, :]
    sc = m_ref[0, 0, 3 * sub + 1:3 * sub + 2, :]
    return (xh * (1.0 + sh) + sc).astype(BF16)


def _embed_lnmod(x, ctx, pos, mvec):
    B, T, _ = x.shape
    nt = 1 + T // TM

    def body(x_ref, c_ref, p_ref, m_ref, o_ref, h_ref):
        i = pl.program_id(1)

        @pl.when(i == 0)
        def _():
            o_ref[...] = c_ref[0]

        @pl.when(i > 0)
        def _():
            o_ref[...] = x_ref[0] + p_ref[...]

        h_ref[...] = _modulate(o_ref[...], m_ref, 0)

    rows = pl.BlockSpec((TM, D), lambda b, i: (b * nt + i, 0))
    return pl.pallas_call(
        body, name="embed_lnmod0", grid=(B, nt),
        in_specs=[pl.BlockSpec((1, TM, D), lambda b, i: (b, jnp.maximum(i - 1, 0), 0)),
                  pl.BlockSpec((1, TM, D), lambda b, i: (b, 0, 0)),
                  pl.BlockSpec((TM, D), lambda b, i: (jnp.maximum(i - 1, 0), 0)),
                  pl.BlockSpec((1, 1, N_MOD, D), _mmap(True))],
        out_specs=[rows, rows],
        out_shape=[jax.ShapeDtypeStruct((B * nt * TM, D), F32), jax.ShapeDtypeStruct((B * nt * TM, D), BF16)],
        compiler_params=_cp(2))(x, ctx, pos, mvec)


def _lnmod0_bwd(dh, X, mvec, dres, x_shape, B, nt):
    def body(dh_ref, x_ref, m_ref, dr_ref, dx_ref, dm_ref):
        i = pl.program_id(1)
        xh, r = _ln(x_ref[...])
        sh = m_ref[0, 0, 0:1, :]
        dhv = dh_ref[...].astype(F32)

        @pl.when((i == 0) | (i == 1))
        def _():
            dm_ref[...] = jnp.zeros_like(dm_ref)

        dm_ref[0, 0, 0:1, :] += _rsum(dhv * xh)
        dm_ref[0, 0, 1:2, :] += _rsum(dhv)

        @pl.when(i > 0)
        def _():
            dx_ref[0] = _ln_bwd(dhv * (1.0 + sh), xh, r) + dr_ref[...]

    rows = pl.BlockSpec((TM, D), _rowmap(nt, 0))
    return pl.pallas_call(
        body, name="lnmod0_bwd", grid=(B, nt),
        in_specs=[rows, rows, pl.BlockSpec((1, 1, N_MOD, D), _mmap(True)), rows],
        out_specs=[pl.BlockSpec((1, TM, D), lambda b, i: (b, jnp.maximum(i - 1, 0), 0)),
                   pl.BlockSpec((1, 1, 2, D), _mmap(True))],
        out_shape=[jax.ShapeDtypeStruct(x_shape, F32), jax.ShapeDtypeStruct((B, 2, 2, D), F32)],
        compiler_params=_cp(2))(dh, X, mvec, dres)


def _resid_fwd(x_ref, y_ref, m_ref, gb_ref, sub, w):
    wg = w * m_ref[0, 0, 3 * sub + 2:3 * sub + 3, :]
    y = y_ref[...].astype(F32)
    zh, r = _ln(ALPHA * x_ref[...] + wg * y)
    return y, wg, zh, r


def _resid_modulate(x_ref, y_ref, m_ref, gb_ref, sub, w, xn_ref, hn_ref):
    _, _, zh, _ = _resid_fwd(x_ref, y_ref, m_ref, gb_ref, sub, w)
    xn = zh * gb_ref[0:1, :] + gb_ref[1:2, :]
    xn_ref[...] = xn
    hn = _modulate(xn, m_ref, sub + 1)
    hn_ref[...] = hn
    return hn


def _pre_specs(pre, B, nt):
    rows = pl.BlockSpec((TM, D), _rowmap(nt, 0))
    in_specs = [pl.BlockSpec((TM, D), _rowmap(*pre["lx"])), rows, pl.BlockSpec((1, 1, N_MOD, D), _mmap(pre["comb"])),
                pl.BlockSpec((2, D), lambda b, i: (0, 0))]
    shapes = [jax.ShapeDtypeStruct((B * nt * TM, D), F32), jax.ShapeDtypeStruct((B * nt * TM, D), BF16)]
    return in_specs, [pre["X"], pre["Y"], pre["mvec"], pre["gb"]], [rows, rows], shapes


def _resid_grads(do, y, wg, zh, r, w, gb_ref, comb, dx_ref, dy_ref, dg_ref, dgb_ref):
    b_, i = pl.program_id(0), pl.program_id(1)
    dz = _ln_bwd(do * gb_ref[0:1, :], zh, r)
    dx_ref[...] = ALPHA * dz
    dy_ref[...] = (wg * dz).astype(BF16)

    @pl.when((b_ == 0) & (i == 0))
    def _():
        dgb_ref[...] = jnp.zeros_like(dgb_ref)

    dgb_ref[0:1, :] += _rsum(do * zh)
    dgb_ref[1:2, :] += _rsum(do)

    init = (i == 0) | (i == 1) if comb else (i == 0)

    @pl.when(init)
    def _():
        dg_ref[...] = jnp.zeros_like(dg_ref)

    dg_ref[0, 0] += w * _rsum(dz * y)


def _resid_out_shapes(B, nt, comb):
    rows = pl.BlockSpec((TM, D), _rowmap(nt, 0))
    specs = [rows, rows, pl.BlockSpec((1, 1, 1, D), _mmap(comb)), pl.BlockSpec((2, D), lambda b, i: (0, 0))]
    shapes = [jax.ShapeDtypeStruct((B * nt * TM, D), F32), jax.ShapeDtypeStruct((B * nt * TM, D), BF16),
              jax.ShapeDtypeStruct((B, 2, 1, D), F32), jax.ShapeDtypeStruct((2, D), F32)]
    return specs, shapes


def _swiglu_dx(dyv, w_ref, u_ref, du_ref):
    half = DFF // 2
    for j in range(2):
        lo, hi = j * half, (j + 1) * half
        da = _dot_nt(dyv, w_ref[lo:hi, :])
        g = u_ref[:, lo:hi].astype(F32)
        up = u_ref[:, DFF + lo:DFF + hi].astype(F32)
        s = _sigmoid(g)
        du_ref[:, lo:hi] = (da * up * (s * (1.0 + g * (1.0 - s)))).astype(BF16)
        du_ref[:, DFF + lo:DFF + hi] = (da * (g * s)).astype(BF16)


def _tail(X, Y, mvec, gb, tgt, sub, w, Wo, U, B, nt):
    def body(x_ref, y_ref, m_ref, gb_ref, t_ref, w_ref, u_ref, dx_ref, dy_ref, dg_ref, dgb_ref, l_ref, du_ref):
        y, wg, zh, r = _resid_fwd(x_ref, y_ref, m_ref, gb_ref, sub, w)
        e = (zh * gb_ref[0:1, :] + gb_ref[1:2, :]) - t_ref[0]

        @pl.when((pl.program_id(0) == 0) & (pl.program_id(1) == 0))
        def _():
            l_ref[...] = jnp.zeros_like(l_ref)

        l_ref[...] += _rsum(e * e)
        _resid_grads(e * (1.0 / D), y, wg, zh, r, w, gb_ref, False, dx_ref, dy_ref, dg_ref, dgb_ref)
        _swiglu_dx(dy_ref[...], w_ref, u_ref, du_ref)

    rows = pl.BlockSpec((TM, D), _rowmap(nt, 0))
    wide = pl.BlockSpec((TM, 2 * DFF), _rowmap(nt, 0))
    specs, shapes = _resid_out_shapes(B, nt, False)
    return pl.pallas_call(
        body, name="resid2_loss_bwd", grid=(B, nt),
        in_specs=[rows, rows, pl.BlockSpec((1, 1, N_MOD, D), _mmap(False)), pl.BlockSpec((2, D), lambda b, i: (0, 0)),
                  pl.BlockSpec((1, TM, D), lambda b, i: (b, i, 0)), _wspec(Wo, 2), wide],
        out_specs=specs + [pl.BlockSpec((1, D), lambda b, i: (0, 0)), wide],
        out_shape=shapes + [jax.ShapeDtypeStruct((1, D), F32), jax.ShapeDtypeStruct((B * nt * TM, 2 * DFF), BF16)],
        compiler_params=_cp(2))(X, Y, mvec, gb, tgt, Wo, U)


def _lnmod_resid_bwd(dh, Xi, lxi, dres, Xp, lxp, Yp, mvec, comb, sub, w, gb, B, nt, name, ffn=None):
    ntl = nt - 1 if comb else nt

    def body(dh_ref, xi_ref, dr_ref, xp_ref, yp_ref, m_ref, gb_ref, *rest):
        w_ref, u_ref = rest[:2] if ffn else (None, None)
        dx_ref, dy_ref, dg_ref, dgb_ref, dm_ref = rest[2:7] if ffn else rest
        i = pl.program_id(1)
        xh, r = _ln(xi_ref[...])
        sh = m_ref[0, 0, 3 * sub:3 * sub + 1, :]
        dhv = dh_ref[...].astype(F32)
        dr = dr_ref[...]
        if comb:
            dr = jnp.where(i > 0, dr, 0.0)
        do = _ln_bwd(dhv * (1.0 + sh), xh, r) + dr

        init = (i == 0) | (i == 1) if comb else (i == 0)

        @pl.when(init)
        def _():
            dm_ref[...] = jnp.zeros_like(dm_ref)

        dm_ref[0, 0, 0:1, :] += _rsum(dhv * xh)
        dm_ref[0, 0, 1:2, :] += _rsum(dhv)
        y, wg, zh, r2 = _resid_fwd(xp_ref, yp_ref, m_ref, gb_ref, sub - 1, w)
        _resid_grads(do, y, wg, zh, r2, w, gb_ref, comb, dx_ref, dy_ref, dg_ref, dgb_ref)
        if ffn:
            _swiglu_dx(dy_ref[...], w_ref, u_ref, rest[7])

    rows = pl.BlockSpec((TM, D), _rowmap(nt, 0))
    wide = pl.BlockSpec((TM, 2 * DFF), _rowmap(nt, 0))
    if comb:
        dres_spec = pl.BlockSpec((TM, D), lambda b, i: (b * ntl + jnp.maximum(i - 1, 0), 0))
    else:
        dres_spec = rows
    specs, shapes = _resid_out_shapes(B, nt, comb)
    return pl.pallas_call(
        body, name=name, grid=(B, nt),
        in_specs=[rows, pl.BlockSpec((TM, D), _rowmap(*lxi)), dres_spec, pl.BlockSpec((TM, D), _rowmap(*lxp)), rows,
                  pl.BlockSpec((1, 1, N_MOD, D), _mmap(comb)), pl.BlockSpec((2, D), lambda b, i: (0, 0))]
        + ([_wspec(ffn[0], 2), wide] if ffn else []),
        out_specs=specs + [pl.BlockSpec((1, 1, 2, D), _mmap(comb))] + ([wide] if ffn else []),
        out_shape=shapes + [jax.ShapeDtypeStruct((B, 2, 2, D), F32)]
        + ([jax.ShapeDtypeStruct((B * nt * TM, 2 * DFF), BF16)] if ffn else []),
        compiler_params=_cp(2))(dh, Xi, dres, Xp, Yp, mvec, gb, *(ffn or ()))


def _wspec(W, nidx):
    zeros = (0,) * W.ndim
    if nidx == 1:
        return pl.BlockSpec(W.shape, lambda i: zeros)
    return pl.BlockSpec(W.shape, lambda b, i: zeros)


def _mm_nn(A, la, W, B, nt, out_dtype, name):
    K, N = W.shape

    def body(a_ref, w_ref, o_ref):
        o_ref[...] = _dot(a_ref[...], w_ref[...]).astype(out_dtype)

    return pl.pallas_call(
        body, name=name, grid=(B, nt),
        in_specs=[pl.BlockSpec((TM, K), _rowmap(*la)), _wspec(W, 2)],
        out_specs=pl.BlockSpec((TM, N), _rowmap(nt, 0)),
        out_shape=jax.ShapeDtypeStruct((B * nt * TM, N), out_dtype), compiler_params=_cp(2))(A, W)


def _ffn_in(A, la, W3, B, nt, name, pre=None):
    K, n = W3.shape[1:]

    def body(*refs):
        if pre is None:
            a_ref, w_ref, u_ref, s_ref = refs
            a = a_ref[...]
        else:
            x_ref, y_ref, m_ref, gb_ref, w_ref, u_ref, s_ref, xn_ref, hn_ref = refs
            a = _resid_modulate(x_ref, y_ref, m_ref, gb_ref, pre["sub"], pre["w"], xn_ref, hn_ref)
        for j in range(2):
            g = _dot(a, w_ref[j])
            up = _dot(a, w_ref[j + 2])
            u_ref[:, j * n:(j + 1) * n] = g.astype(BF16)
            u_ref[:, (j + 2) * n:(j + 3) * n] = up.astype(BF16)
            s_ref[:, j * n:(j + 1) * n] = (g * _sigmoid(g) * up).astype(BF16)

    rows = B * nt * TM
    if pre is None:
        in_specs, args, xo_specs, xo_shapes = [pl.BlockSpec((TM, K), _rowmap(*la))], [A], [], []
    else:
        in_specs, args, xo_specs, xo_shapes = _pre_specs(pre, B, nt)
    return pl.pallas_call(
        body, name=name, grid=(B, nt),
        in_specs=in_specs + [_wspec(W3, 2)],
        out_specs=[pl.BlockSpec((TM, 4 * n), _rowmap(nt, 0)), pl.BlockSpec((TM, 2 * n), _rowmap(nt, 0))] + xo_specs,
        out_shape=[jax.ShapeDtypeStruct((rows, 4 * n), BF16), jax.ShapeDtypeStruct((rows, 2 * n), BF16)] + xo_shapes,
        compiler_params=_cp(2))(*args, W3)


def _mm_nt(A, W, name, dep=None, out_dtype=F32):
    M, N = A.shape
    K = W.shape[-2]

    def body(a_ref, w_ref, *rest):
        o_ref = rest[-1]
        if W.ndim == 3:
            n = W.shape[-1]
            acc = _dot_nt(a_ref[:, 0:n], w_ref[0])
            for j in range(1, 4):
                acc = acc + _dot_nt(a_ref[:, j * n:(j + 1) * n], w_ref[j])
            o_ref[...] = acc.astype(out_dtype)
        else:
            o_ref[...] = _dot_nt(a_ref[...], w_ref[...]).astype(out_dtype)

    deps = [] if dep is None else [dep]
    return pl.pallas_call(
        body, name=name, grid=(M // TM,),
        in_specs=[pl.BlockSpec((TM, N), lambda i: (i, 0)), _wspec(W, 1)] + [_wspec(d, 1) for d in deps],
        out_specs=pl.BlockSpec((TM, K), lambda i: (i, 0)),
        out_shape=jax.ShapeDtypeStruct((M, K), out_dtype), compiler_params=_cp(1))(A, W, *deps)


def _mm_tn(A, G, name, tn=512, shards=None):
    M, K = A.shape
    N = G.shape[1]
    tk = next(t for t in ((2048, 1536, 1024, 512) if K <= D else (1024, 512)) if M % t == 0)
    if shards:
        tn = N // shards

    def body(a_ref, g_ref, o_ref):
        @pl.when(pl.program_id(1) == 0)
        def _():
            o_ref[...] = jnp.zeros_like(o_ref)

        upd = _dot_tn(a_ref[...], g_ref[...])
        if shards:
            o_ref[0] += upd
        else:
            o_ref[...] += upd

    if shards:
        out_spec = pl.BlockSpec((1, K, tn), lambda n, k: (n, 0, 0))
        out_shape = jax.ShapeDtypeStruct((shards, K, tn), F32)
    else:
        out_spec = pl.BlockSpec((K, tn), lambda n, k: (0, n))
        out_shape = jax.ShapeDtypeStruct((K, N), F32)
    return pl.pallas_call(
        body, name=name, grid=(N // tn, M // tk),
        in_specs=[pl.BlockSpec((tk, K), lambda n, k: (k, 0)), pl.BlockSpec((tk, tn), lambda n, k: (k, n))],
        out_specs=out_spec, out_shape=out_shape, compiler_params=_cp(2))(A, G)


def _logsig(z):
    return jnp.minimum(z, 0.0) - jnp.log(1.0 + jnp.exp(-jnp.abs(z)))


ZW = 2688
ZRQ, ZRFF, ZRFB, ZLR = 1024, 1536, 2048, 2560


def _mix_in_features(H, W, a2p, biasp, lbp, B, nt, pre=None):
    K = W.shape[0]

    def body(*refs):
        if pre is None:
            h_ref, w_ref, a2_ref, bias_ref, lb_ref, q_ref, k_ref, v_ref, g_ref, z_ref = refs
            hv = h_ref[...]
        else:
            (x_ref, y_ref, m_ref, gb_ref, w_ref, a2_ref, bias_ref, lb_ref, q_ref, k_ref, v_ref, g_ref, z_ref,
             xn_ref, hn_ref) = refs
            hv = _resid_modulate(x_ref, y_ref, m_ref, gb_ref, pre["sub"], pre["w"], xn_ref, hn_ref)
        proj = lambda lo, hi: _dot(hv, w_ref[:, lo:hi])
        z_ref[:, 0:2 * 4 * HD] = proj(GG, GG + 2 * 4 * HD)
        lrf = proj(LR, LR + HD)
        z_ref[:, ZLR:ZLR + HD] = lrf
        lr = lrf.astype(BF16)
        lane = lax.broadcasted_iota(jnp.int32, (1, 4 * HD), 1)
        keep = (lane & (HD - 1)) < GLA_DK
        for d in range(2):
            z = _dot(lr, a2_ref[d]) + bias_ref[d:d + 1, :]
            gl = jnp.where(keep, _logsig(z) * (1.0 / GATE_NORM), 0.0)
            for h in range(4):
                g_ref[d, 0, h] = gl[:, h * HD:(h + 1) * HD]
        gq, gk, gv = proj(GQ, GQ + 4 * HD), proj(GK, GK + 4 * HD), proj(GV, GV + 4 * HD)
        for h in range(4):
            sl = slice(h * HD, (h + 1) * HD)
            q_ref[0, h] = gq[:, sl] * (GLA_DK ** -0.5)
            k_ref[0, 0, h] = gk[:, sl]
            k_ref[1, 0, h] = gk[:, sl]
            v_ref[0, h] = gv[:, sl].astype(BF16)
        rqs, ri = proj(RQ, RQ + 4 * HD), proj(RI, RI + 4 * HD)
        z_ref[:, ZRQ:ZRQ + 4 * HD] = rqs
        for h in range(4):
            sl = slice(h * HD, (h + 1) * HD)
            rq = rqs[:, sl]
            q_ref[0, 4 + h] = rq * _sigmoid(rq) * (HD ** -0.5)
            v_ref[0, 4 + h] = ri[:, sl].astype(BF16)
        for d, off, zoff in ((0, RFF, ZRFF), (1, RFB, ZRFB)):
            rf = proj(off, off + 4 * HD)
            z_ref[:, zoff:zoff + 4 * HD] = rf
            for h in range(4):
                sl = slice(h * HD, (h + 1) * HD)
                lb = lb_ref[d:d + 1, sl]
                f = lb + (1.0 - lb) * _sigmoid(rf[:, sl])
                g_ref[d, 0, 4 + h] = jnp.log(f)
                k_ref[d, 0, 4 + h] = 1.0 - f

    one = pl.BlockSpec((1, NH, TM, HD), lambda b, i: (b, 0, i, 0))
    two = pl.BlockSpec((2, 1, NH, TM, HD), lambda b, i: (0, b, 0, i, 0))
    s1 = jax.ShapeDtypeStruct((B, NH, nt * TM, HD), F32)
    s2 = jax.ShapeDtypeStruct((2, B, NH, nt * TM, HD), F32)
    if pre is None:
        in_specs, args, xo_specs, xo_shapes = [pl.BlockSpec((TM, K), _rowmap(nt, 0))], [H], [], []
    else:
        in_specs, args, xo_specs, xo_shapes = _pre_specs(pre, B, nt)
    return pl.pallas_call(
        body, name="mix_in_features", grid=(B, nt),
        in_specs=in_specs + [_wspec(W, 2), pl.BlockSpec((2, HD, 4 * HD), lambda b, i: (0, 0, 0)),
                             pl.BlockSpec((2, 4 * HD), lambda b, i: (0, 0)), pl.BlockSpec((2, 4 * HD), lambda b, i: (0, 0))],
        out_specs=[one, two, one, two, pl.BlockSpec((TM, ZW), _rowmap(nt, 0))] + xo_specs,
        out_shape=[s1, s2, jax.ShapeDtypeStruct(s1.shape, BF16), s2,
                   jax.ShapeDtypeStruct((B * nt * TM, ZW), F32)] + xo_shapes,
        compiler_params=_cp(2))(*args, W, a2p, biasp, lbp)


def _features_bwd(Z, a2p, biasp, lbp, dQ0, dQ1, dK0, dK1, dV0, dV1, dG0, dG1, dgates, W, B):
    nt = Z.shape[0] // (B * TM)

    def body(z_ref, a2_ref, bias_ref, lb_ref, dq0, dq1, dk0, dk1, dv0, dv1, dg0, dg1, dgt_ref, w_ref,
             df_ref, dh_ref, da2_ref, dbias_ref, dlb_ref):
        b_, i = pl.program_id(0), pl.program_id(1)

        @pl.when((b_ == 0) & (i == 0))
        def _():
            da2_ref[...] = jnp.zeros_like(da2_ref)
            dbias_ref[...] = jnp.zeros_like(dbias_ref)
            dlb_ref[...] = jnp.zeros_like(dlb_ref)

        df_ref[:, 0:2 * 4 * HD] = jnp.where(i > 0, dgt_ref[...], 0.0).astype(BF16)
        df_ref[:, LR + HD:] = jnp.zeros((TM, MIXP - LR - HD), BF16)
        lr = z_ref[:, ZLR:ZLR + HD].astype(BF16)
        lane = lax.broadcasted_iota(jnp.int32, (1, 4 * HD), 1)
        keep = (lane & (HD - 1)) < GLA_DK
        dlr = jnp.zeros((TM, HD), F32)
        dgs = (dg0, dg1)
        dks = (dk0, dk1)
        rd = lambda ref, h: ref[0, h].astype(F32)
        for d in range(2):
            z = _dot(lr, a2_ref[d]) + bias_ref[d:d + 1, :]
            dgl = jnp.concatenate([rd(dgs[d], h) for h in range(4)], axis=1)
            dz = jnp.where(keep, dgl * (1.0 / GATE_NORM) * (1.0 - _sigmoid(z)), 0.0)
            dzb = dz.astype(BF16)
            dlr = dlr + _dot_nt(dzb, a2_ref[d])
            da2_ref[d] += _dot_tn(lr, dzb)
            dbias_ref[d:d + 1, :] += _rsum(dz)
        df_ref[:, LR:LR + HD] = dlr.astype(BF16)
        for h in range(4):
            df_ref[:, GQ + h * HD:GQ + (h + 1) * HD] = ((rd(dq0, h) + rd(dq1, h)) * (GLA_DK ** -0.5)).astype(BF16)
            df_ref[:, GK + h * HD:GK + (h + 1) * HD] = (rd(dk0, h) + rd(dk1, h)).astype(BF16)
            df_ref[:, GV + h * HD:GV + (h + 1) * HD] = (rd(dv0, h) + rd(dv1, h)).astype(BF16)
        for h in range(4):
            sl = slice(h * HD, (h + 1) * HD)
            rq = z_ref[:, ZRQ + h * HD:ZRQ + (h + 1) * HD]
            s = _sigmoid(rq)
            dqh = rd(dq0, 4 + h) + rd(dq1, 4 + h)
            df_ref[:, RQ + h * HD:RQ + (h + 1) * HD] = (dqh * (HD ** -0.5) * (s * (1.0 + rq * (1.0 - s)))).astype(BF16)
            df_ref[:, RI + h * HD:RI + (h + 1) * HD] = (rd(dv0, 4 + h) + rd(dv1, 4 + h)).astype(BF16)
            for d, off, zoff in ((0, RFF, ZRFF), (1, RFB, ZRFB)):
                lb = lb_ref[d:d + 1, sl]
                sg = _sigmoid(z_ref[:, zoff + h * HD:zoff + (h + 1) * HD])
                f = lb + (1.0 - lb) * sg
                dff = rd(dgs[d], 4 + h) / f - rd(dks[d], 4 + h)
                df_ref[:, off + h * HD:off + (h + 1) * HD] = (dff * (1.0 - lb) * sg * (1.0 - sg)).astype(BF16)
                dlb_ref[d:d + 1, sl] += _rsum(dff * (1.0 - sg))
        dh_ref[...] = _dot_nt(df_ref[...], w_ref[...]).astype(BF16)

    m0 = lambda b, i: (b, 0, i, 0)
    one = lambda m: pl.BlockSpec((1, NH, TM, HD), m)
    return pl.pallas_call(
        body, name="mix_features_bwd", grid=(B, nt),
        in_specs=[pl.BlockSpec((TM, ZW), _rowmap(nt, 0)), pl.BlockSpec((2, HD, 4 * HD), lambda b, i: (0, 0, 0)),
                  pl.BlockSpec((2, 4 * HD), lambda b, i: (0, 0)), pl.BlockSpec((2, 4 * HD), lambda b, i: (0, 0)),
                  one(m0), one(m0), one(m0), one(m0), one(m0), one(m0), one(m0), one(m0),
                  pl.BlockSpec((TM, D), lambda b, i: (b * (nt - 1) + jnp.maximum(i - 1, 0), 0)), _wspec(W, 2)],
        out_specs=[pl.BlockSpec((TM, MIXP), _rowmap(nt, 0)), pl.BlockSpec((TM, D), _rowmap(nt, 0)),
                   pl.BlockSpec((2, HD, 4 * HD), lambda b, i: (0, 0, 0)),
                   pl.BlockSpec((2, 4 * HD), lambda b, i: (0, 0)), pl.BlockSpec((2, 4 * HD), lambda b, i: (0, 0))],
        out_shape=[jax.ShapeDtypeStruct((B * nt * TM, MIXP), BF16), jax.ShapeDtypeStruct((B * nt * TM, D), BF16),
                   jax.ShapeDtypeStruct((2, HD, 4 * HD), F32),
                   jax.ShapeDtypeStruct((2, 4 * HD), F32), jax.ShapeDtypeStruct((2, 4 * HD), F32)],
        compiler_params=_cp(2))(Z, a2p, biasp, lbp, dQ0, dQ1, dK0, dK1, dV0, dV1, dG0, dG1, dgates, W)


def _chunk_scan(x, rin, fwd):
    acc = x
    sft = 1
    while sft < CH:
        if fwd:
            acc = acc + jnp.where(rin >= sft, pltpu.roll(acc, sft, 0), 0.0)
        else:
            acc = acc + jnp.where(rin < CH - sft, pltpu.roll(acc, TM - sft, 0), 0.0)
        sft *= 2
    return acc


def _chunk_total(x):
    t = jnp.sum(x.reshape(NCB, CH, HD), axis=1, keepdims=True)
    return jnp.broadcast_to(t, (NCB, CH, HD)).reshape(TM, HD)


def _scan_masks(rev):
    rin = lax.broadcasted_iota(jnp.int32, (TM, HD), 0) & (CH - 1)
    ri = lax.broadcasted_iota(jnp.int32, (SB, SB), 0)
    ci = lax.broadcasted_iota(jnp.int32, (SB, SB), 1)
    same = (ri >> 5) == (ci >> 5)
    lo = same & (ri >= ci)
    up = same & (ri <= ci)
    mask, maskT = (up, lo) if rev else (lo, up)
    re = lax.broadcasted_iota(jnp.int32, (SB, CSB * HD), 0) >> 5
    ce = lax.broadcasted_iota(jnp.int32, (SB, CSB * HD), 1) >> 7
    return rin, mask, maskT, re == ce


def _scan_decay(q, k, g, rin, rev):
    b = _chunk_scan(g, rin, not rev)
    xx = _chunk_total(g) - b
    eb = jnp.exp(b)
    return b, xx, eb, q * eb, k * jnp.exp(-b), k * jnp.exp(xx)


def _sub(x, s):
    return x[s * SB:(s + 1) * SB]


def _expand(xb, mexp):
    return jnp.where(mexp, jnp.concatenate([xb] * CSB, axis=1), jnp.zeros((), xb.dtype))


def _own(x, mexp):
    xm = jnp.where(mexp, x, 0.0)
    acc = xm[:, 0:HD]
    for n in range(1, CSB):
        acc = acc + xm[:, n * HD:(n + 1) * HD]
    return acc


def _stack(per_chunk, s):
    return jnp.concatenate(per_chunk[s * CSB:(s + 1) * CSB], axis=1)


def _state_pass(s0, eb, uts, rev):
    order = range(NCB - 1, -1, -1) if rev else range(NCB)
    states = [None] * NCB
    s = s0
    for n in order:
        row = n * CH if rev else n * CH + CH - 1
        states[n] = s
        s = eb[row:row + 1, :] * s + uts[n // CSB][:, (n % CSB) * HD:(n % CSB + 1) * HD]
    return states, s


def _scan_fwd(Q, K, V, G, rev, B):
    nb = Q.shape[2] // TM
    d = 1 if rev else 0
    rmap = (lambda s: jnp.where(s == 0, 0, nb - s)) if rev else (lambda s: s)

    def body(q_ref, k_ref, v_ref, g_ref, o_ref, st_ref, s_scr):
        @pl.when(pl.program_id(2) == 0)
        def _():
            s_scr[...] = jnp.zeros_like(s_scr)

        rin, mask, _, mexp = _scan_masks(rev)

        def head(p, readout):
            s0 = s_scr[p]
            st_ref[0, p, 0] = s0
            _, _, eb, qd, ki, kt = _scan_decay(q_ref[0, p], k_ref[0, 0, p], g_ref[0, 0, p], rin, rev)
            ktb, vb = kt.astype(BF16), v_ref[0, p]
            uts = [_dot_tn(_sub(vb, s), _expand(_sub(ktb, s), mexp)) for s in range(NSB)]
            states, s_new = _state_pass(s0, eb, uts, rev)
            s_scr[p] = s_new
            if not readout:
                return
            qb, kib = qd.astype(BF16), ki.astype(BF16)
            for s in range(NSB):
                a = jnp.where(mask, _dot_nt(_sub(qb, s), _sub(kib, s)), 0.0)
                o_ref[0, p, s * SB:(s + 1) * SB, :] = (
                    _dot(a.astype(BF16), _sub(vb, s))
                    + _dot_nt(_expand(_sub(qb, s), mexp), _stack(states, s).astype(BF16)))

        @pl.when(pl.program_id(2) >= 1)
        def _():
            for p in range(HP):
                head(p, True)

        @pl.when(pl.program_id(2) == 0)
        def _():
            for p in range(HP):
                head(p, False)

    one = pl.BlockSpec((1, HP, TM, HD), lambda b, h, s: (b, h, rmap(s), 0))
    two = pl.BlockSpec((1, 1, HP, TM, HD), lambda b, h, s: (d, b, h, rmap(s), 0))
    return pl.pallas_call(
        body, name="scan_fwd_rev" if rev else "scan_fwd", grid=(B, NH // HP, nb),
        in_specs=[one, two, one, two],
        out_specs=[one, pl.BlockSpec((1, HP, 1, HD, HD), lambda b, h, s: (b, h, s, 0, 0))],
        out_shape=[jax.ShapeDtypeStruct(Q.shape, F32), jax.ShapeDtypeStruct((B, NH, nb, HD, HD), F32)],
        scratch_shapes=[pltpu.VMEM((HP, HD, HD), F32)],
        compiler_params=_cp(3))(Q, K, V, G)


def _scan_bwd(Q, K, V, G, St, dO, rev, B):
    nb = Q.shape[2] // TM
    d = 1 if rev else 0
    smap = lambda t: nb - 1 - t
    rmap = (lambda t: jnp.where(smap(t) == 0, 0, nb - smap(t))) if rev else smap

    def body(q_ref, k_ref, v_ref, g_ref, st_ref, do_ref, dq_ref, dk_ref, dv_ref, dg_ref, ds_scr):
        t = pl.program_id(2)

        @pl.when(t == 0)
        def _():
            ds_scr[...] = jnp.zeros_like(ds_scr)

        is_lat = smap(t) >= 1
        rin, mask, maskT, mexp = _scan_masks(rev)
        for p in range(HPB):
            b, xx, eb, qd, ki, kt = _scan_decay(q_ref[0, p], k_ref[0, 0, p], g_ref[0, 0, p], rin, rev)
            qb, kib, ktb, vb = qd.astype(BF16), ki.astype(BF16), kt.astype(BF16), v_ref[0, p]
            dob = jnp.where(is_lat, do_ref[0, p], 0.0).astype(BF16)
            kt_exps = [_expand(_sub(ktb, s), mexp) for s in range(NSB)]
            uts = [_dot_tn(_sub(vb, s), kt_exps[s]) for s in range(NSB)]
            states, _ = _state_pass(st_ref[0, p, 0], eb, uts, rev)
            gts = [_dot_tn(_sub(dob, s), _expand(_sub(qb, s), mexp)) for s in range(NSB)]
            order = range(NCB) if rev else range(NCB - 1, -1, -1)
            dsp = [None] * NCB
            t2 = [None] * NCB
            dsc = ds_scr[p]
            for n in order:
                row = n * CH if rev else n * CH + CH - 1
                ebl = eb[row:row + 1, :]
                dsp[n] = dsc
                t2[n] = jnp.broadcast_to(ebl * _rsum(states[n] * dsc), (CH, HD))
                dsc = gts[n // CSB][:, (n % CSB) * HD:(n % CSB + 1) * HD] + ebl * dsc
            ds_scr[p] = dsc
            dqds, dkis, dkts = [], [], []
            for s in range(NSB):
                q_s, ki_s, v_s, do_s = _sub(qb, s), _sub(kib, s), _sub(vb, s), _sub(dob, s)
                dspb = _stack(dsp, s).astype(BF16)
                da = jnp.where(mask, _dot_nt(do_s, v_s), 0.0).astype(BF16)
                dat = jnp.where(maskT, _dot_nt(v_s, do_s), 0.0).astype(BF16)
                at = jnp.where(maskT, _dot_nt(ki_s, q_s), 0.0).astype(BF16)
                dqds.append(_dot(da, ki_s) + _own(_dot(do_s, _stack(states, s).astype(BF16)), mexp))
                dkis.append(_dot(dat, q_s))
                dv_ref[0, p, s * SB:(s + 1) * SB, :] = (_dot(at, do_s) + _dot_nt(kt_exps[s], dspb)).astype(BF16)
                dkts.append(_own(_dot(v_s, dspb), mexp))
            dqd, dki, dkt = (jnp.concatenate(parts, axis=0) for parts in (dqds, dkis, dkts))
            z = dkt * kt
            db = dqd * qd - dki * ki
            dq_ref[0, p] = (dqd * eb).astype(BF16)
            dk_ref[0, p] = (dki * jnp.exp(-b) + dkt * jnp.exp(xx)).astype(BF16)
            dg_ref[0, p] = (_chunk_total(db) + (db - z) + _chunk_scan(z - db, rin, not rev)
                            + jnp.concatenate(t2, axis=0)).astype(BF16)

    one = pl.BlockSpec((1, HPB, TM, HD), lambda b, h, t: (b, h, rmap(t), 0))
    two = pl.BlockSpec((1, 1, HPB, TM, HD), lambda b, h, t: (d, b, h, rmap(t), 0))
    lat = pl.BlockSpec((1, HPB, TM, HD), lambda b, h, t: (b, h, jnp.clip(rmap(t) - 1, 0, nb - 2), 0))
    shp = jax.ShapeDtypeStruct(Q.shape, BF16)
    return pl.pallas_call(
        body, name="scan_bwd_rev" if rev else "scan_bwd", grid=(B, NH // HPB, nb),
        in_specs=[one, two, one, two, pl.BlockSpec((1, HPB, 1, HD, HD), lambda b, h, t: (b, h, smap(t), 0, 0)), lat],
        out_specs=[one, one, one, one], out_shape=[shp, shp, shp, shp],
        scratch_shapes=[pltpu.VMEM((HPB, HD, HD), F32)],
        compiler_params=_cp(3))(Q, K, V, G, St, dO)


def _gnorm_mix_out(O0, O1, F, gains, W, B, ntl):
    nt = ntl + 1

    def body(o0_ref, o1_ref, f_ref, gn_ref, w_ref, m_ref, y_ref):
        for h in range(NH):
            o = o0_ref[0, h] + o1_ref[0, h]
            r = lax.rsqrt(jnp.mean(o * o, axis=-1, keepdims=True) + NORM_EPS)
            gn = gn_ref[0:1, :] if h < 4 else gn_ref[1:2, :]
            gt = f_ref[:, h * HD:(h + 1) * HD]
            m_ref[:, h * HD:(h + 1) * HD] = (o * r * gn * (gt * _sigmoid(gt))).astype(BF16)
        y_ref[...] = _dot(m_ref[...], w_ref[...]).astype(BF16)

    ospec = pl.BlockSpec((1, NH, TM, HD), lambda b, i: (b, 0, i + 1, 0))
    rows = pl.BlockSpec((TM, D), _rowmap(ntl, 0))
    shp = jax.ShapeDtypeStruct((B * ntl * TM, D), BF16)
    return pl.pallas_call(
        body, name="gated_norm_mix_out", grid=(B, ntl),
        in_specs=[ospec, ospec, pl.BlockSpec((TM, D), lambda b, i: (b * nt + 1 + i, 0)),
                  pl.BlockSpec((2, HD), lambda b, i: (0, 0)), _wspec(W, 2)],
        out_specs=[rows, rows], out_shape=[shp, shp], compiler_params=_cp(2))(O0, O1, F, gains, W)


def _mix_out_dx_gnorm_bwd(dY, W, O0, O1, F, gains, dep, B, ntl):
    nt = ntl + 1

    def body(dy_ref, w_ref, o0_ref, o1_ref, f_ref, gn_ref, dep_ref, do_ref, dgt_ref, dgn_ref):
        b_, i = pl.program_id(0), pl.program_id(1)

        @pl.when((b_ == 0) & (i == 0))
        def _():
            dgn_ref[...] = jnp.zeros_like(dgn_ref)

        dyv = dy_ref[...]
        for h in range(NH):
            o = o0_ref[0, h] + o1_ref[0, h]
            r = lax.rsqrt(jnp.mean(o * o, axis=-1, keepdims=True) + NORM_EPS)
            y = o * r
            gn = gn_ref[0:1, :] if h < 4 else gn_ref[1:2, :]
            gt = f_ref[:, h * HD:(h + 1) * HD]
            s = _sigmoid(gt)
            dm = _dot_nt(dyv, w_ref[h * HD:(h + 1) * HD, :])
            don = dm * (gt * s)
            dgt_ref[:, h * HD:(h + 1) * HD] = (dm * (y * gn) * (s * (1.0 + gt * (1.0 - s)))).astype(BF16)
            row = 0 if h < 4 else 1
            dgn_ref[row:row + 1, :] += _rsum(don * y)
            dy = don * gn
            do_ref[0, h] = (r * (dy - y * jnp.mean(dy * y, axis=-1, keepdims=True))).astype(BF16)

    ospec = pl.BlockSpec((1, NH, TM, HD), lambda b, i: (b, 0, i + 1, 0))
    return pl.pallas_call(
        body, name="mix_out_dx_gated_norm_bwd", grid=(B, ntl),
        in_specs=[pl.BlockSpec((TM, D), _rowmap(ntl, 0)), _wspec(W, 2), ospec, ospec,
                  pl.BlockSpec((TM, D), lambda b, i: (b * nt + 1 + i, 0)), pl.BlockSpec((2, HD), lambda b, i: (0, 0)),
                  _wspec(dep, 2)],
        out_specs=[pl.BlockSpec((1, NH, TM, HD), lambda b, i: (b, 0, i, 0)), pl.BlockSpec((TM, D), _rowmap(ntl, 0)),
                   pl.BlockSpec((2, HD), lambda b, i: (0, 0))],
        out_shape=[jax.ShapeDtypeStruct((B, NH, ntl * TM, HD), BF16), jax.ShapeDtypeStruct((B * ntl * TM, D), BF16),
                   jax.ShapeDtypeStruct((2, HD), F32)],
        compiler_params=_cp(2))(dY, W, O0, O1, F, gains, dep)


def _sincos_2d(rows, width, dim):
    quarter = dim // 4
    omega = 1.0 / 10000.0 ** (jnp.arange(quarter, dtype=F32) / quarter)

    def emb(n):
        a = jnp.arange(n).astype(F32)[:, None] * omega[None, :]
        return jnp.concatenate([jnp.sin(a), jnp.cos(a)], axis=-1)

    er = jnp.broadcast_to(emb(rows)[:, None, :], (rows, width, dim // 2))
    ec = jnp.broadcast_to(emb(width)[None, :, :], (rows, width, dim // 2))
    return jnp.concatenate([er, ec], axis=-1).reshape(rows * width, dim)


def _pad_heads(w):
    k = w.shape[0]
    return jnp.pad(w.reshape(k, 4, GLA_DK), ((0, 0), (0, 0), (0, HD - GLA_DK))).reshape(k, 4 * HD)


def _unpad_heads(w):
    k = w.shape[0]
    return w.reshape(k, 4, HD)[:, :, :GLA_DK].reshape(k, 4 * GLA_DK)


MIX_N = 1032
MIX_NP = 1152
_SEGS = ([(64 * h, 64, GQ + HD * h) for h in range(4)] + [(256 + 64 * h, 64, GK + HD * h) for h in range(4)]
         + [(512, 512, GV), (1024, 512, GG), (1536, 32, LR), (1568, 512, RQ), (2080, 512, RFF), (2592, 512, RFB),
            (3104, 512, RI), (3616, 512, RG)])


def _mix_in_to_padded(ps):
    k = ps.shape[1]
    parts, pos = [], 0
    for g0, ln, s0 in sorted(_SEGS, key=lambda s: s[2]):
        if s0 > pos:
            parts.append(jnp.zeros((k, s0 - pos), ps.dtype))
        for j in range(4):
            lo, hi = max(g0, j * MIX_N), min(g0 + ln, (j + 1) * MIX_N)
            if lo < hi:
                parts.append(ps[j][:, lo - j * MIX_N:hi - j * MIX_N])
        pos = s0 + ln
    parts.append(jnp.zeros((k, MIXP - pos), ps.dtype))
    return jnp.concatenate(parts, axis=1)


def _mix_in_from_padded(g):
    k = g.shape[0]
    shards = []
    for j in range(4):
        parts = []
        for g0, ln, s0 in sorted(_SEGS):
            lo, hi = max(g0, j * MIX_N), min(g0 + ln, (j + 1) * MIX_N)
            if lo < hi:
                parts.append(g[:, s0 + lo - g0:s0 + hi - g0])
        parts.append(jnp.zeros((k, MIX_NP - MIX_N), g.dtype))
        shards.append(jnp.concatenate(parts, axis=1))
    return jnp.stack(shards)


def _local_step(x, ctx, tgt, mvec, weights_for, prefetch, ln_gain, ln_bias, a2f, a2b, abf, abb, lb, gng, gnh, on_grads, on_sent):
    B, T, _ = x.shape
    assert ctx.shape[1] == TM and T % TM == 0
    ntl = T // TM
    nt = ntl + 1
    C, L, CL = (nt, 0), (ntl, 0), (nt, 1)
    pos = _sincos_2d(T // 64, 64, D)
    gbs = [jnp.stack([ln_gain[i], ln_bias[i]]) for i in range(3)]
    a2p = jnp.zeros((2, HD, 4 * HD), F32)
    a2p = a2p.at[0, 0:16].set(_pad_heads(a2f)).at[1, 16:32].set(_pad_heads(a2b)).astype(BF16)
    biasp = jnp.concatenate([_pad_heads(abf.reshape(1, -1)), _pad_heads(abb.reshape(1, -1))], axis=0)
    gains = jnp.concatenate([gng.reshape(1, HD), gnh.reshape(1, HD)], axis=0)

    X0, h0 = _embed_lnmod(x, ctx, pos, mvec)
    w1i, w1o = weights_for("ffn1", h0)
    u0, a0 = _ffn_in(h0, C, w1i, B, nt, "ffn1_in")
    tok = prefetch("mix", a0)
    y0 = _mm_nn(a0, C, w1o, B, nt, BF16, "ffn1_out")
    wmp, wmo = weights_for("mix", y0)
    Q, K, V, G, Zm, X1, h1 = _mix_in_features(None, wmp, a2p, biasp, lb, B, nt, pre=dict(
        X=X0, lx=C, Y=y0, mvec=mvec, comb=True, sub=0, w=0.5, gb=gbs[0] + tok[0, 0]))
    prefetch("ffn2", Zm)
    O0, S0 = _scan_fwd(Q, K, V, G, False, B)
    O1, S1 = _scan_fwd(Q, K, V, G, True, B)
    merged, y1 = _gnorm_mix_out(O0, O1, Zm, gains, wmo, B, ntl)
    w2i, w2o = weights_for("ffn2", y1)
    u2, a2, X2, h2 = _ffn_in(None, L, w2i, B, ntl, "ffn2_in", pre=dict(
        X=X1, lx=CL, Y=y1, mvec=mvec, comb=False, sub=1, w=1.0, gb=gbs[1]))
    y2 = _mm_nn(a2, L, w2o, B, ntl, BF16, "ffn2_out")

    dx2r, dy2, dgate2, dgb2, lsum, du2 = _tail(X2, y2, mvec, gbs[2], tgt, 2, 0.5, w2o, u2, B, ntl)
    loss = (0.5 / D) * jnp.sum(lsum)
    g_w2o = _mm_tn(a2, dy2, "ffn2_out_dw")
    dh2 = _mm_nt(du2, w2i, "ffn2_in_dx", out_dtype=BF16)
    g_w2i = _mm_tn(h2, du2, "ffn2_in_dw", shards=4)
    tok = on_grads("ffn2", (g_w2i, g_w2o))
    dx1r, dy1, dgate1, dgb1, dss2 = _lnmod_resid_bwd(dh2, X2, L, dx2r, X1, CL, y1, mvec, False, 2, 1.0,
                                                     gbs[1] + tok[0, 0], B, ntl, "lnmod2_resid1_bwd")
    tok = on_sent("ffn2", dy1)
    g_wmo = _mm_tn(merged, dy1, "mix_out_dw", tn=D)
    dO, dgates, dgains = _mix_out_dx_gnorm_bwd(dy1, wmo, O0, O1, Zm, gains, tok, B, ntl)
    dQ0, dK0, dV0, dG0 = _scan_bwd(Q, K, V, G, S0, dO, False, B)
    dQ1, dK1, dV1, dG1 = _scan_bwd(Q, K, V, G, S1, dO, True, B)
    dF, dh1, da2p, dbiasp, dlb = _features_bwd(Zm, a2p, biasp, lb, dQ0, dQ1, dK0, dK1, dV0, dV1, dG0, dG1, dgates,
                                               wmp, B)
    g_wmp = _mm_tn(h1, dF, "mix_in_dw", tn=MIXP // 4)
    tok = on_grads("mix", (g_wmp, g_wmo))
    dx0r, dy0, dgate0, dgb0, dss1, du0 = _lnmod_resid_bwd(dh1, X1, C, dx1r, X0, C, y0, mvec, True, 1, 0.5,
                                                          gbs[0] + tok[0, 0], B, nt, "lnmod1_resid0_bwd", ffn=(w1o, u0))
    on_sent("mix", du0)
    g_w1o = _mm_tn(a0, dy0, "ffn1_out_dw")
    g_w1i = _mm_tn(h0, du0, "ffn1_in_dw", shards=4)
    tok = on_grads("ffn1", (g_w1i, g_w1o))
    dh0 = _mm_nt(du0, w1i, "ffn1_in_dx", dep=tok, out_dtype=BF16)
    tok = on_sent("ffn1", dh0)
    grad_x, dss0 = _lnmod0_bwd(dh0, X0, mvec + tok[0, 0], dx0r, x.shape, B, nt)

    zero_ctx = lambda a: a.at[:, 0].set(0.0)
    dm = jnp.concatenate([dss0, dgate0, dss1, zero_ctx(dgate1), zero_ctx(dss2), zero_ctx(dgate2)], axis=2)
    small = dict(
        ln_gain=jnp.stack([dgb0[0], dgb1[0], dgb2[0]]), ln_bias=jnp.stack([dgb0[1], dgb1[1], dgb2[1]]),
        a2f=_unpad_heads(da2p[0, 0:16]), a2b=_unpad_heads(da2p[1, 16:32]),
        abf=_unpad_heads(dbiasp[0:1]), abb=_unpad_heads(dbiasp[1:2]), lb=dlb, gng=dgains[0], gnh=dgains[1])
    return loss, grad_x, dm, small


def _small_allgather(xs, name):
    r, n = xs.shape

    def body(x_ref, out_ref, send_sems, recv_sems, local_sem):
        x, y, c = lax.axis_index("x"), lax.axis_index("y"), lax.axis_index("c")
        me, sibling = (x, y, c), (x, y, 1 - c)
        chips = [(1 - x, y), (x, 1 - y), (1 - x, 1 - y)]

        def rows(px, py, pc):
            return out_ref.at[pl.ds((4 * px + 2 * py + pc) * r, r), :]

        def copy(k, block, to, src=None):
            return pltpu.make_async_remote_copy(
                src_ref=rows(*block) if src is None else src, dst_ref=rows(*block),
                send_sem=send_sems.at[k], recv_sem=recv_sems.at[k], device_id=to, device_id_type=MESH)

        mine = pltpu.make_async_copy(x_ref, rows(*me), local_sem)
        mine.start()
        first = [copy(0, me, sibling, src=x_ref)]
        first += [copy(1 + j, me, (*chip, c), src=x_ref) for j, chip in enumerate(chips)]
        for cp in first:
            cp.start()
        passed = [copy(4 + j, (*chip, c), sibling) for j, chip in enumerate(chips)]
        for j, chip in enumerate(chips):
            copy(1 + j, (*chip, c), me).wait_recv()
            passed[j].start()
        copy(0, sibling, me).wait_recv()
        for j, chip in enumerate(chips):
            copy(4 + j, (*chip, 1 - c), me).wait_recv()
        for cp in first + passed:
            cp.wait_send()
        mine.wait()

    out = pl.pallas_call(
        body, name=name,
        out_shape=jax.ShapeDtypeStruct((8 * r, n), xs.dtype),
        in_specs=[pl.BlockSpec(memory_space=pltpu.VMEM)],
        out_specs=pl.BlockSpec(memory_space=pltpu.VMEM),
        scratch_shapes=[pltpu.SemaphoreType.DMA((7,)), pltpu.SemaphoreType.DMA((7,)), pltpu.SemaphoreType.DMA],
        compiler_params=pltpu.CompilerParams(vmem_limit_bytes=VMEM_LIMIT))(xs)
    return out.reshape(8, r, n)


def _gather_flat(v, name):
    n = v.shape[0]
    npad = -(-n // 1024) * 1024
    g = _small_allgather(jnp.pad(v, (0, npad - n)).reshape(8, npad // 8), name)
    return g.reshape(8, npad)[:, :n]


HBM_SPEC = pl.BlockSpec(memory_space=pltpu.HBM)
SEM_SPEC = pl.BlockSpec(memory_space=pltpu.SEMAPHORE)
DATAFLOW = pltpu.SideEffectType.DATAFLOW_SIDE_EFFECTING


def _gather_copies(xs, outs, send_sems, recv_sems):
    x, y, c = lax.axis_index("x"), lax.axis_index("y"), lax.axis_index("c")
    dests = [(x, y, 1 - c), (1 - x, y, c), (x, 1 - y, c), (1 - x, 1 - y, c)]
    return [pltpu.make_async_remote_copy(
        src_ref=xs[w], dst_ref=outs[w].at[4 * x + 2 * y + c], send_sem=send_sems[4 * w + k],
        recv_sem=recv_sems[4 * w + k], device_id=dests[k], device_id_type=MESH)
        for w in range(len(xs)) for k in range(4)]


def _split_start(copies, per_w, srcs, land_lead, after, name):
    n = len(srcs)
    m = per_w * n
    lands = [lax.empty((land_lead,) + s.shape[-2:], s.dtype) for s in srcs]
    deps = [] if after is None else [after]

    def body(*refs):
        xs, ls, outs = refs[:n], refs[n:2 * n], refs[2 * n + len(deps):]
        for cp in copies(xs, ls, outs[:m], outs[m:2 * m]):
            cp.start()
        token = outs[2 * m + 2 * n]
        token[...] = jnp.zeros_like(token)

    outs = pl.pallas_call(
        body, name=name,
        out_shape=([pltpu.SemaphoreType.DMA(())] * (2 * m) + [pltpu.HBM(a.shape, a.dtype) for a in srcs + lands]
                   + [jax.ShapeDtypeStruct((8, 128), F32)]),
        in_specs=[HBM_SPEC] * (2 * n) + [pl.BlockSpec(memory_space=pl.ANY)] * len(deps),
        out_specs=[SEM_SPEC] * (2 * m) + [HBM_SPEC] * (2 * n) + [pl.BlockSpec(memory_space=pltpu.VMEM)],
        input_output_aliases={w: 2 * m + w for w in range(2 * n)},
        compiler_params=pltpu.CompilerParams(has_side_effects=DATAFLOW),
    )(*[pltpu.with_memory_space_constraint(a, pltpu.HBM) for a in srcs + lands], *deps)
    return outs[:2 * m], outs[2 * m:2 * m + n], outs[2 * m + n:2 * m + 2 * n], outs[2 * m + 2 * n]


def _split_wait(copies, per_w, sems, x_thru, l_thru, after, name):
    n = len(x_thru)
    m = per_w * n

    def body(*refs):
        xs, ls, ss = refs[:n], refs[n:2 * n], refs[2 * n:2 * n + 2 * m]
        for cp in copies(xs, ls, ss[:m], ss[m:]):
            cp.wait_send()
            cp.wait_recv()

    outs = pl.pallas_call(
        body, name=name,
        out_shape=[pltpu.HBM(a.shape, a.dtype) for a in list(x_thru) + list(l_thru)],
        in_specs=[HBM_SPEC] * (2 * n) + [SEM_SPEC] * (2 * m) + [pl.BlockSpec(memory_space=pl.ANY)],
        out_specs=[HBM_SPEC] * (2 * n),
        input_output_aliases={w: w for w in range(2 * n)},
        compiler_params=pltpu.CompilerParams(has_side_effects=DATAFLOW),
    )(*x_thru, *l_thru, *sems, after)
    return outs[:n], outs[n:]


def _gather_forward(gathered, name):
    n = len(gathered)

    def body(*refs):
        outs = refs[n:2 * n]
        send_sems, recv_sems = refs[2 * n:]
        x, y, c = lax.axis_index("x"), lax.axis_index("y"), lax.axis_index("c")
        chips = [(1 - x, y), (x, 1 - y), (1 - x, 1 - y), (x, y)]

        def copy(w, j, pc):
            px, py = chips[j]
            slot = outs[w].at[4 * px + 2 * py + pc]
            return pltpu.make_async_remote_copy(
                src_ref=slot, dst_ref=slot, send_sem=send_sems.at[4 * w + j], recv_sem=recv_sems.at[4 * w + j],
                device_id=(x, y, 1 - c), device_id_type=MESH)

        sends = [copy(w, j, 1 - c if j == 3 else c) for w in range(n) for j in range(4)]
        for cp in sends:
            cp.start()
        for w in range(n):
            for j in range(4):
                copy(w, j, c if j == 3 else 1 - c).wait_recv()
        for cp in sends:
            cp.wait_send()

    any_spec = pl.BlockSpec(memory_space=pl.ANY)
    return pl.pallas_call(
        body, name=name,
        out_shape=[jax.ShapeDtypeStruct(g.shape, g.dtype) for g in gathered],
        in_specs=[any_spec] * n, out_specs=[any_spec] * n,
        input_output_aliases={w: w for w in range(n)},
        scratch_shapes=[pltpu.SemaphoreType.DMA((4 * n,)), pltpu.SemaphoreType.DMA((4 * n,))],
    )(*gathered)


def _forward_copies(gs, ls, send_sems, recv_sems):
    x, y, c = lax.axis_index("x"), lax.axis_index("y"), lax.axis_index("c")
    chips = [(1 - x, y), (x, 1 - y), (1 - x, 1 - y), (x, y)]
    cps = []
    for w in range(len(gs)):
        for j, (px, py) in enumerate(chips):
            slot = gs[w].at[4 * px + 2 * py + (1 - c if j == 3 else c)]
            cps.append(pltpu.make_async_remote_copy(
                src_ref=slot, dst_ref=slot, send_sem=send_sems[4 * w + j], recv_sem=recv_sems[4 * w + j],
                device_id=(x, y, 1 - c), device_id_type=MESH))
    return cps


def _pair_copies(gs, ls, send_sems, recv_sems):
    x, y, c = lax.axis_index("x"), lax.axis_index("y"), lax.axis_index("c")
    return [pltpu.make_async_remote_copy(
        src_ref=gs[w].at[2 * j + 1 - c], dst_ref=ls[w].at[j], send_sem=send_sems[4 * w + j],
        recv_sem=recv_sems[4 * w + j], device_id=(x, y, 1 - c), device_id_type=MESH)
        for w in range(len(gs)) for j in range(4)]


def _all_copies(xs, ls, send_sems, recv_sems):
    x, y, c = lax.axis_index("x"), lax.axis_index("y"), lax.axis_index("c")
    flips = [(a, b, e) for a in (0, 1) for b in (0, 1) for e in (0, 1)][1:]
    return [pltpu.make_async_remote_copy(
        src_ref=xs[0], dst_ref=ls[0].at[4 * x + 2 * y + c], send_sem=send_sems[k], recv_sem=recv_sems[k],
        device_id=(1 - x if a else x, 1 - y if b else y, 1 - c if e else c), device_id_type=MESH)
        for k, (a, b, e) in enumerate(flips)]


def _share_copies(fs, ls, send_sems, recv_sems):
    x, y, c = lax.axis_index("x"), lax.axis_index("y"), lax.axis_index("c")
    return [pltpu.make_async_remote_copy(
        src_ref=fs[w].at[c], dst_ref=fs[w].at[c], send_sem=send_sems[w], recv_sem=recv_sems[w],
        device_id=(x, y, 1 - c), device_id_type=MESH) for w in range(len(fs))]


def _chip_copies(hs, ls, send_sems, recv_sems):
    x, y, c = lax.axis_index("x"), lax.axis_index("y"), lax.axis_index("c")
    chips = [(1 - x, y), (x, 1 - y), (1 - x, 1 - y)]
    return [pltpu.make_async_remote_copy(
        src_ref=hs[w].at[2 * px + py], dst_ref=ls[w].at[k], send_sem=send_sems[3 * w + k],
        recv_sem=recv_sems[3 * w + k], device_id=(px, py, c), device_id_type=MESH)
        for w in range(len(hs)) for k, (px, py) in enumerate(chips)]


def _rs_add_pair(g8, r4, c, name):
    R, n = g8.shape[1:]
    rb = R // 2

    def body(c_ref, g_ref, r_ref, o_ref):
        o_ref[...] = (g_ref[...] + r_ref[...]).astype(BF16)

    spec = pl.BlockSpec((1, rb, n), lambda j, i, c_ref: (j, i, 0))
    return pl.pallas_call(
        body, name=name,
        grid_spec=pltpu.PrefetchScalarGridSpec(
            num_scalar_prefetch=1, grid=(4, R // rb),
            in_specs=[pl.BlockSpec((1, rb, n), lambda j, i, c_ref: (2 * j + c_ref[0], i, 0)), spec],
            out_specs=spec),
        out_shape=jax.ShapeDtypeStruct((4, R, n), BF16), compiler_params=_cp(2))(c, g8, r4)


def _rs_add_chips(g8, r4, r3, cj, name):
    R, n = g8.shape[1:]
    rb = R // 2

    def body(cj_ref, g_ref, p_ref, r_ref, o_ref):
        own = g_ref[0] + p_ref[0]
        o_ref[0] = ((own + r_ref[0].astype(F32)) + r_ref[1].astype(F32)) + r_ref[2].astype(F32)

    return pl.pallas_call(
        body, name=name,
        grid_spec=pltpu.PrefetchScalarGridSpec(
            num_scalar_prefetch=1, grid=(R // rb,),
            in_specs=[pl.BlockSpec((1, rb, n), lambda i, cj_ref: (2 * cj_ref[1] + cj_ref[0], i, 0)),
                      pl.BlockSpec((1, rb, n), lambda i, cj_ref: (cj_ref[1], i, 0)),
                      pl.BlockSpec((3, rb, n), lambda i, cj_ref: (0, i, 0))],
            out_specs=pl.BlockSpec((1, rb, n), lambda i, cj_ref: (cj_ref[0], i, 0))),
        out_shape=jax.ShapeDtypeStruct((2, R, n), F32), compiler_params=_cp(1))(cj, g8, r4, r3)


def _sum8(g):
    n = g.shape[1]

    def body(g_ref, o_ref):
        acc = g_ref[0:1, :]
        for k in range(1, 8):
            acc = acc + g_ref[k:k + 1, :]
        o_ref[...] = acc

    return pl.pallas_call(body, name="sum_devices", out_shape=jax.ShapeDtypeStruct((1, n), F32),
                          compiler_params=pltpu.CompilerParams(vmem_limit_bytes=VMEM_LIMIT))(g)


ADA_ROWS = 64


def _ada_fwd(cs, w, b):
    n = w.shape[1]

    def body(c_ref, w_ref, b_ref, o_ref):
        cv = c_ref[...]
        s = (cv * _sigmoid(cv)).astype(BF16)
        o_ref[...] = _dot(s, w_ref[...].astype(BF16)) + b_ref[...]

    return pl.pallas_call(body, name="ada_fwd", out_shape=jax.ShapeDtypeStruct((ADA_ROWS, n), F32),
                          compiler_params=pltpu.CompilerParams(vmem_limit_bytes=VMEM_LIMIT))(cs, w, b)


def _mod_exchange(m_cols):
    n = m_cols.shape[1]

    def body(m_ref, out_ref, send_sems, recv_sems, local_sem):
        x, y, c = lax.axis_index("x"), lax.axis_index("y"), lax.axis_index("c")
        chips = [(1 - x, y), (x, 1 - y), (1 - x, 1 - y)]

        def group(px, py):
            return m_ref.at[pl.ds(pl.multiple_of(8 * (4 * px + 2 * py + c), 8), 8), :]

        mine = pltpu.make_async_copy(group(x, y), out_ref.at[2 * x + y], local_sem)
        mine.start()
        cps = [pltpu.make_async_remote_copy(
            src_ref=group(px, py), dst_ref=out_ref.at[2 * x + y], send_sem=send_sems.at[k], recv_sem=recv_sems.at[k],
            device_id=(px, py, c), device_id_type=MESH) for k, (px, py) in enumerate(chips)]
        for cp in cps:
            cp.start()
        for cp in cps:
            cp.wait_recv()
        for cp in cps:
            cp.wait_send()
        mine.wait()

    return pl.pallas_call(
        body, name="mod_exchange", out_shape=jax.ShapeDtypeStruct((4, 8, n), F32),
        in_specs=[pl.BlockSpec(memory_space=pltpu.VMEM)], out_specs=pl.BlockSpec(memory_space=pltpu.VMEM),
        scratch_shapes=[pltpu.SemaphoreType.DMA((3,)), pltpu.SemaphoreType.DMA((3,)), pltpu.SemaphoreType.DMA],
        compiler_params=pltpu.CompilerParams(vmem_limit_bytes=VMEM_LIMIT))(m_cols)


def _ada_bwd(cs, w, dm):
    n = w.shape[1]

    def body(c_ref, w_ref, dm_ref, gw_ref, dc_ref):
        cv = c_ref[...]
        s = (cv * _sigmoid(cv)).astype(BF16)
        gw_ref[...] = _dot_tn(s, dm_ref[...].astype(BF16))
        dc_ref[...] = _dot_nt(dm_ref[0:8, :].astype(BF16), w_ref[...].astype(BF16))

    return pl.pallas_call(
        body, name="ada_bwd",
        out_shape=[jax.ShapeDtypeStruct((D, n), F32), jax.ShapeDtypeStruct((8, D), F32)],
        compiler_params=pltpu.CompilerParams(vmem_limit_bytes=VMEM_LIMIT))(cs, w, dm)


def _adamw(w, g, m, v, name):
    r, c = w.shape
    rb = r
    if r % 8 == 0 and r * c * 4 > (1 << 20):
        rb = 8
        for cand in range(8, r, 8):
            if r % cand == 0 and cand * c * 4 <= (1 << 20):
                rb = cand

    def body(w_ref, g_ref, m_ref, v_ref, go_ref, d_ref, nm_ref, nv_ref):
        gv = g_ref[...]
        go_ref[...] = gv
        mn = ADAM_B1 * m_ref[...] + (1.0 - ADAM_B1) * gv
        vn = ADAM_B2 * v_ref[...] + (1.0 - ADAM_B2) * (gv * gv)
        m_hat = mn / (1.0 - ADAM_B1 ** ADAM_STEP)
        v_hat = vn / (1.0 - ADAM_B2 ** ADAM_STEP)
        d_ref[...] = -ADAM_LR * (m_hat / (jnp.sqrt(v_hat) + ADAM_EPS) + ADAM_WD * w_ref[...])
        nm_ref[...] = mn
        nv_ref[...] = vn

    spec = pl.BlockSpec((rb, c), lambda i: (i, 0))
    shp = jax.ShapeDtypeStruct((r, c), F32)
    return pl.pallas_call(body, name=name, grid=(r // rb,), in_specs=[spec] * 4, out_specs=[spec] * 4,
                          out_shape=[shp] * 4, compiler_params=_cp(1))(w, g, m, v)


BIG = ("ffn1_w_in", "ffn1_w_out", "w_mix_in", "w_mix_out", "ffn2_w_in", "ffn2_w_out")


def _half_rows(w, c):
    half = w.shape[0] // 2
    return lax.dynamic_slice_in_dim(w, c * half, half, axis=0)


def _lower_bounds(logits):
    return jnp.cumsum(jax.nn.softmax(logits.astype(F32), axis=1), axis=1)[:, 0]


def kernel(x, c, ctx, c_ctx, w_ada, b_ada, ln_gain, ln_bias, ffn1_w_in, ffn1_w_out, w_mix_in, gla_a2_fwd, gla_a2_bwd, gla_a_bias_fwd, gla_a_bias_bwd, hgrn_lb_logits, gla_norm_gain, hgrn_norm_gain, w_mix_out, ffn2_w_in, ffn2_w_out, loss_target, m_c_ctx, m_w_ada, m_b_ada, m_ln_gain, m_ln_bias, m_ffn1_w_in, m_ffn1_w_out, m_w_mix_in, m_gla_a2_fwd, m_gla_a2_bwd, m_gla_a_bias_fwd, m_gla_a_bias_bwd, m_hgrn_lb_logits, m_gla_norm_gain, m_hgrn_norm_gain, m_w_mix_out, m_ffn2_w_in, m_ffn2_w_out, v_c_ctx, v_w_ada, v_b_ada, v_ln_gain, v_ln_bias, v_ffn1_w_in, v_ffn1_w_out, v_w_mix_in, v_gla_a2_fwd, v_gla_a2_bwd, v_gla_a_bias_fwd, v_gla_a_bias_bwd, v_hgrn_lb_logits, v_gla_norm_gain, v_hgrn_norm_gain, v_w_mix_out, v_ffn2_w_in, v_ffn2_w_out):
    xi, yi, ci = lax.axis_index("x"), lax.axis_index("y"), lax.axis_index("c")
    chip = 2 * xi + yi
    dev = 2 * chip + ci
    B = x.shape[0]
    weights = dict(ffn1_w_in=ffn1_w_in[0], ffn1_w_out=ffn1_w_out[0], w_mix_in=w_mix_in[0], w_mix_out=w_mix_out[0],
                   ffn2_w_in=ffn2_w_in[0], ffn2_w_out=ffn2_w_out[0])

    mine = jnp.concatenate([c.reshape(-1), ln_gain.reshape(-1), ln_bias.reshape(-1), gla_a2_fwd.reshape(-1),
                            gla_a2_bwd.reshape(-1), hgrn_lb_logits.reshape(-1)])
    g1 = _gather_flat(mine, "gather_cond")
    nc = B * D
    c_all = g1[:, :nc].reshape(8 * B, D)
    per_chip = g1[0::2, nc:]
    o = 0

    def take(shape, axis):
        nonlocal o
        n = int(np.prod(shape))
        parts = per_chip[:, o:o + n].reshape((4,) + shape)
        o += n
        return jnp.concatenate([parts[j] for j in range(4)], axis=axis)

    ln_gain_f = take((3, 256), 1)
    ln_bias_f = take((3, 256), 1)
    a2f_f = take((16, 64), 1)
    a2b_f = take((16, 64), 1)
    lbl_f = take((2, 2, 128), 2)
    lb, lb_vjp = jax.vjp(_lower_bounds, lbl_f)

    assert B + 1 <= 8
    cs = jnp.concatenate([c_all.reshape(8, B, D), jnp.broadcast_to(c_ctx.reshape(1, 1, D), (8, 1, D)),
                          jnp.zeros((8, 7 - B, D), F32)], axis=1).reshape(ADA_ROWS, D)
    ncol = w_ada.shape[2]
    b_cols = lax.dynamic_slice_in_dim(b_ada, chip * ncol, ncol, axis=1)
    m4 = _mod_exchange(_ada_fwd(cs, w_ada[0], b_cols))
    m_all = jnp.concatenate([m4[j] for j in range(4)], axis=1)
    m_lat = m_all[:B].reshape(B, 1, N_MOD, D)
    m_ctx = jnp.broadcast_to(m_all[B].reshape(1, 1, N_MOD, D), (B, 1, N_MOD, D))

    groups = dict(ffn2=("ffn2_w_in", "ffn2_w_out"), mix=("w_mix_in", "w_mix_out"), ffn1=("ffn1_w_in", "ffn1_w_out"))
    shards = dict(weights, w_mix_in=jnp.pad(weights["w_mix_in"], ((0, 0), (0, MIX_NP - MIX_N))))
    blks = {k: _half_rows(shards[k], ci).astype(BF16) for k in BIG}
    gathering = {}
    token = m_all
    for group in ("ffn1", "mix", "ffn2"):
        sems, x_thru, l_thru, token = _split_start(_gather_copies, 4, [blks[k] for k in groups[group]], 8, token,
                                                   "weight_gather_start_" + group)
        gathering[group] = (sems, x_thru, l_thru)
    mvec = jnp.concatenate([m_ctx, m_lat], axis=1) + token[0, 0]

    forwarding = {}

    def prefetch(group, after):
        _, got = _split_wait(_gather_copies, 4, *gathering[group], after, "weight_gather_wait_" + group)
        forwarding[group] = _split_start(_forward_copies, 4, list(got), 1, None, "weight_gather_forward_start_" + group)
        return forwarding[group][3]

    def weights_for(group, after):
        names = groups[group]
        if group in forwarding:
            sems, g_thru, l_thru, _ = forwarding[group]
            (w_in, w_out), _ = _split_wait(_forward_copies, 4, sems, g_thru, l_thru, after,
                                           "weight_gather_forward_wait_" + group)
        else:
            _, got = _split_wait(_gather_copies, 4, *gathering[group], after, "weight_gather_wait_" + group)
            w_in, w_out = _gather_forward(got, "weight_gather_forward_" + group)
        if group == "mix":
            return _mix_in_to_padded(w_in.reshape(4, D, MIX_NP)), w_out.reshape(-1, D)
        return w_in.reshape((4,) + shards[names[0]].shape), w_out.reshape(-1, D)

    cvec = ci.reshape(1).astype(jnp.int32)
    cjvec = jnp.stack([ci, chip]).astype(jnp.int32)
    in_flight = {}

    def on_grads(group, gs):
        names = groups[group]
        if group == "mix":
            gs = (_mix_in_from_padded(gs[0]), gs[1])
        g8s = [g.reshape((8, shards[k].shape[0] // 2, shards[k].shape[1])) for k, g in zip(names, gs)]
        sems, g_thru, l_thru, token = _split_start(_pair_copies, 4, g8s, 4, None, "grad_pair_start_" + group)
        in_flight[group] = (sems, g_thru, l_thru)
        return token

    def on_sent(group, after):
        g8s, r4s = _split_wait(_pair_copies, 4, *in_flight[group], after, "grad_pair_wait_" + group)
        h4s = [_rs_add_pair(g, r, cvec, "grad_pair_add_" + k) for k, g, r in zip(groups[group], g8s, r4s)]
        sems, h_thru, l_thru, token = _split_start(_chip_copies, 3, h4s, 3, None, "grad_chip_start_" + group)
        in_flight[group] = (g8s, r4s, sems, h_thru, l_thru)
        return token

    loss_l, grad_x, dm, small = _local_step(
        x, ctx, loss_target, mvec, weights_for, prefetch, ln_gain_f, ln_bias_f, a2f_f, a2b_f,
        gla_a_bias_fwd, gla_a_bias_bwd, lb, gla_norm_gain, hgrn_norm_gain, on_grads, on_sent)
    loss = lax.psum(loss_l, ("x", "y", "c"))

    dm_lat = dm[:, 1].reshape(B, N_MOD * D)
    dm_ctx = jnp.sum(dm[:, 0], axis=0).reshape(N_MOD * D)
    keys = ("ln_gain", "ln_bias", "a2f", "a2b", "abf", "abb", "lb", "gng", "gnh")
    flat = jnp.concatenate([dm_lat.reshape(-1), dm_ctx] + [small[k].reshape(-1) for k in keys])
    nflat = flat.shape[0]
    npad = -(-nflat // 1024) * 1024
    flat2 = jnp.pad(flat, (0, npad - nflat)).reshape(8, npad // 8)
    s_sems, s_src, s_land, tok = _split_start(_all_copies, 7, [flat2], 8, None, "small_grads_start")

    fin = {}
    for group in ("ffn2", "mix", "ffn1"):
        g8s, r4s, sems, h_thru, l_thru = in_flight[group]
        _, r3s = _split_wait(_chip_copies, 3, sems, h_thru, l_thru, tok, "grad_chip_wait_" + group)
        for k, g, r4, r3 in zip(groups[group], g8s, r4s, r3s):
            fin[k] = _rs_add_chips(g, r4, r3, cjvec, "grad_chip_add_" + k)
    p_sems, p_src, p_land, tok = _split_start(_share_copies, 1, [fin[k] for k in BIG], 1, None, "grad_pair_share_start")

    _, (land,) = _split_wait(_all_copies, 7, s_sems, s_src, s_land, tok, "small_grads_wait")
    g3 = lax.dynamic_update_index_in_dim(land, flat2, dev, 0).reshape(8, npad)[:, :nflat]
    nlat = B * N_MOD * D
    dm_all = g3[:, :nlat].reshape(8 * B, N_MOD * D)
    tot = _sum8(g3[:, nlat:])[0]
    dmc_tot = tot[:N_MOD * D]
    o = N_MOD * D
    sg = {}
    for k in keys:
        n = int(np.prod(small[k].shape))
        sg[k] = tot[o:o + n].reshape(small[k].shape)
        o += n
    ctx_rows = jnp.zeros((8, 1, N_MOD * D), F32).at[0, 0].set(dmc_tot)
    dm_rows = jnp.concatenate([dm_all.reshape(8, B, -1), ctx_rows, jnp.zeros((8, 7 - B, N_MOD * D), F32)],
                              axis=1).reshape(ADA_ROWS, N_MOD * D)
    g_b_ada = (jnp.sum(dm_all, axis=0) + dmc_tot).reshape(1, N_MOD * D)
    g_w_ada, dcc = _ada_bwd(cs, w_ada[0], lax.dynamic_slice_in_dim(dm_rows, chip * ncol, ncol, axis=1))
    g4 = _gather_flat(dcc[B], "gather_cctx")
    dsilu = ((g4[0] + g4[2]) + g4[4]) + g4[6]
    sc = _sigmoid(c_ctx)
    g_c_ctx = dsilu * (sc * (1.0 + c_ctx * (1.0 - sc)))
    (g_lbl,) = lb_vjp(sg["lb"])

    def cols(a, n, axis):
        return lax.dynamic_slice_in_dim(a, chip * n, n, axis=axis)

    shared, _ = _split_wait(_share_copies, 1, p_sems, p_src, p_land, g_w_ada, "grad_pair_share_wait")
    gsh = {k: both.reshape(shards[k].shape)[:, :weights[k].shape[1]] for k, both in zip(BIG, shared)}

    grads = dict(
        c_ctx=g_c_ctx, w_ada=g_w_ada[None], b_ada=g_b_ada, ln_gain=cols(sg["ln_gain"], 256, 1)[None],
        ln_bias=cols(sg["ln_bias"], 256, 1)[None], ffn1_w_in=gsh["ffn1_w_in"][None], ffn1_w_out=gsh["ffn1_w_out"][None],
        w_mix_in=gsh["w_mix_in"][None], gla_a2_fwd=cols(sg["a2f"], 64, 1)[None], gla_a2_bwd=cols(sg["a2b"], 64, 1)[None],
        gla_a_bias_fwd=sg["abf"], gla_a_bias_bwd=sg["abb"], hgrn_lb_logits=cols(g_lbl, 128, 2),
        gla_norm_gain=sg["gng"].reshape(1, HD), hgrn_norm_gain=sg["gnh"].reshape(1, HD),
        w_mix_out=gsh["w_mix_out"][None], ffn2_w_in=gsh["ffn2_w_in"][None], ffn2_w_out=gsh["ffn2_w_out"][None])
    params = dict(
        c_ctx=(c_ctx, m_c_ctx, v_c_ctx), w_ada=(w_ada, m_w_ada, v_w_ada), b_ada=(b_ada, m_b_ada, v_b_ada),
        ln_gain=(ln_gain, m_ln_gain, v_ln_gain), ln_bias=(ln_bias, m_ln_bias, v_ln_bias),
        ffn1_w_in=(ffn1_w_in, m_ffn1_w_in, v_ffn1_w_in), ffn1_w_out=(ffn1_w_out, m_ffn1_w_out, v_ffn1_w_out),
        w_mix_in=(w_mix_in, m_w_mix_in, v_w_mix_in), gla_a2_fwd=(gla_a2_fwd, m_gla_a2_fwd, v_gla_a2_fwd),
        gla_a2_bwd=(gla_a2_bwd, m_gla_a2_bwd, v_gla_a2_bwd),
        gla_a_bias_fwd=(gla_a_bias_fwd, m_gla_a_bias_fwd, v_gla_a_bias_fwd),
        gla_a_bias_bwd=(gla_a_bias_bwd, m_gla_a_bias_bwd, v_gla_a_bias_bwd),
        hgrn_lb_logits=(hgrn_lb_logits, m_hgrn_lb_logits, v_hgrn_lb_logits),
        gla_norm_gain=(gla_norm_gain, m_gla_norm_gain, v_gla_norm_gain),
        hgrn_norm_gain=(hgrn_norm_gain, m_hgrn_norm_gain, v_hgrn_norm_gain),
        w_mix_out=(w_mix_out, m_w_mix_out, v_w_mix_out), ffn2_w_in=(ffn2_w_in, m_ffn2_w_in, v_ffn2_w_in),
        ffn2_w_out=(ffn2_w_out, m_ffn2_w_out, v_ffn2_w_out))
    order = list(params.keys())
    big_names = ("w_ada",) + BIG
    upd = {}
    for k in big_names:
        w_, m_, v_ = params[k]
        s2 = w_.shape[-2:]
        g_, d_, nm_, nv_ = _adamw(w_.reshape(s2), grads[k].reshape(s2), m_.reshape(s2), v_.reshape(s2), "adamw_" + k)
        grads[k] = g_
        upd[k] = (d_.reshape(w_.shape), nm_.reshape(w_.shape), nv_.reshape(w_.shape))
    small_names = [k for k in order if k not in big_names]
    sizes = [int(np.prod(params[k][0].shape)) for k in small_names]
    tot_n = sum(sizes)
    npad = -(-tot_n // 1024) * 1024

    def packed(get):
        flat_ = jnp.concatenate([get(k).reshape(-1) for k in small_names])
        return jnp.pad(flat_, (0, npad - tot_n)).reshape(8, npad // 8)

    _, d_s, nm_s, nv_s = _adamw(packed(lambda k: params[k][0]), packed(lambda k: grads[k]),
                                packed(lambda k: params[k][1]), packed(lambda k: params[k][2]), "adamw_small")
    o = 0
    for k, n in zip(small_names, sizes):
        shp = params[k][0].shape
        upd[k] = tuple(a.reshape(-1)[o:o + n].reshape(shp) for a in (d_s, nm_s, nv_s))
        o += n

    return (loss, grad_x, *[grads[k].reshape(params[k][0].shape) for k in order], *[upd[k][0] for k in order],
            *[upd[k][1] for k in order], *[upd[k][2] for k in order])
```

```python
import functools

import numpy as np
import jax
import jax.numpy as jnp
from jax import lax
from jax.experimental import pallas as pl
from jax.experimental.pallas import tpu as pltpu

F32 = jnp.float32
BF16 = jnp.bfloat16
MESH = pl.DeviceIdType.MESH

D = 1024
DFF = 2816
TM = 256
CH = 32
NCB = TM // CH
SB = 128
CSB = SB // CH
NSB = TM // SB
HP = 8
HPB = 8
HD = 128
NH = 8
LN_EPS = 1e-5
NORM_EPS = 1e-6
ALPHA = 2.0 ** 0.25
GATE_NORM = 16.0
GLA_DK = 64
N_MOD = 9
VMEM_LIMIT = 52 * 1024 * 1024

MIXP = 5120
GG, RG, GQ, GK, GV, RQ, RFF, RFB, RI, LR = 0, 512, 1024, 1536, 2048, 2560, 3072, 3584, 4096, 4608
IN_SPLITS = (256, 256, 512, 512, 16, 16, 512, 512, 512, 512, 512)

ADAM_LR, ADAM_B1, ADAM_B2, ADAM_EPS, ADAM_WD, ADAM_STEP = 0.001, 0.9, 0.999, 1e-08, 0.01, 10
ADAM_NBUF = 3


def _cp(n_axes):
    return pltpu.CompilerParams(dimension_semantics=("arbitrary",) * n_axes, vmem_limit_bytes=VMEM_LIMIT)


def _rowmap(stride, off):
    return lambda b, i: (b * stride + off + i, 0)


def _mmap(comb):
    if comb:
        return lambda b, i: (b, jnp.minimum(i, 1), 0, 0)
    return lambda b, i: (b, 1, 0, 0)


def _ln(x):
    mu = jnp.mean(x, axis=-1, keepdims=True)
    xc = x - mu
    var = jnp.mean(xc * xc, axis=-1, keepdims=True)
    r = lax.rsqrt(var + LN_EPS)
    return xc * r, r


def _ln_bwd(dxh, xh, r):
    return r * (dxh - jnp.mean(dxh, axis=-1, keepdims=True) - xh * jnp.mean(dxh * xh, axis=-1, keepdims=True))


def _sigmoid(x):
    return 1.0 / (1.0 + jnp.exp(-x))


def _rsum(x):
    return jnp.sum(x, axis=0, keepdims=True)


def _dot(a, b):
    return jnp.dot(a, b, preferred_element_type=F32)


def _dot_nt(a, b):
    return lax.dot_general(a, b, (((1,), (1,)), ((), ())), preferred_element_type=F32)


def _dot_tn(a, b):
    return lax.dot_general(a, b, (((0,), (0,)), ((), ())), preferred_element_type=F32)


def _modulate(xv, m_ref, sub):
    xh, _ = _ln(xv)
    sh = m_ref[0, 0, 3 * sub:3 * sub + 1, :]
    sc = m_ref[0, 0, 3 * sub + 1:3 * sub + 2, :]
    return (xh * (1.0 + sh) + sc).astype(BF16)


def _embed_lnmod(x, ctx, pos, mvec):
    B, T, _ = x.shape
    nt = 1 + T // TM

    def body(x_ref, c_ref, p_ref, m_ref, o_ref, h_ref):
        i = pl.program_id(1)

        @pl.when(i == 0)
        def _():
            o_ref[...] = c_ref[0]

        @pl.when(i > 0)
        def _():
            o_ref[...] = x_ref[0] + p_ref[...]

        h_ref[...] = _modulate(o_ref[...], m_ref, 0)

    rows = pl.BlockSpec((TM, D), lambda b, i: (b * nt + i, 0))
    return pl.pallas_call(
        body, name="embed_lnmod0", grid=(B, nt),
        in_specs=[pl.BlockSpec((1, TM, D), lambda b, i: (b, jnp.maximum(i - 1, 0), 0)),
                  pl.BlockSpec((1, TM, D), lambda b, i: (b, 0, 0)),
                  pl.BlockSpec((TM, D), lambda b, i: (jnp.maximum(i - 1, 0), 0)),
                  pl.BlockSpec((1, 1, N_MOD, D), _mmap(True))],
        out_specs=[rows, rows],
        out_shape=[jax.ShapeDtypeStruct((B * nt * TM, D), F32), jax.ShapeDtypeStruct((B * nt * TM, D), BF16)],
        compiler_params=_cp(2))(x, ctx, pos, mvec)


def _lnmod0_bwd(dh, X, mvec, dres, x_shape, B, nt):
    def body(dh_ref, x_ref, m_ref, dr_ref, dx_ref, dm_ref):
        i = pl.program_id(1)
        xh, r = _ln(x_ref[...])
        sh = m_ref[0, 0, 0:1, :]
        dhv = dh_ref[...].astype(F32)

        @pl.when((i == 0) | (i == 1))
        def _():
            dm_ref[...] = jnp.zeros_like(dm_ref)

        dm_ref[0, 0, 0:1, :] += _rsum(dhv * xh)
        dm_ref[0, 0, 1:2, :] += _rsum(dhv)

        @pl.when(i > 0)
        def _():
            dx_ref[0] = _ln_bwd(dhv * (1.0 + sh), xh, r) + dr_ref[...]

    rows = pl.BlockSpec((TM, D), _rowmap(nt, 0))
    return pl.pallas_call(
        body, name="lnmod0_bwd", grid=(B, nt),
        in_specs=[rows, rows, pl.BlockSpec((1, 1, N_MOD, D), _mmap(True)), rows],
        out_specs=[pl.BlockSpec((1, TM, D), lambda b, i: (b, jnp.maximum(i - 1, 0), 0)),
                   pl.BlockSpec((1, 1, 2, D), _mmap(True))],
        out_shape=[jax.ShapeDtypeStruct(x_shape, F32), jax.ShapeDtypeStruct((B, 2, 2, D), F32)],
        compiler_params=_cp(2))(dh, X, mvec, dres)


def _resid_fwd(x_ref, y_ref, m_ref, gb_ref, sub, w):
    wg = w * m_ref[0, 0, 3 * sub + 2:3 * sub + 3, :]
    y = y_ref[...].astype(F32)
    zh, r = _ln(ALPHA * x_ref[...] + wg * y)
    return y, wg, zh, r


def _resid_modulate(x_ref, y_ref, m_ref, gb_ref, sub, w, xn_ref, hn_ref):
    _, _, zh, _ = _resid_fwd(x_ref, y_ref, m_ref, gb_ref, sub, w)
    xn = zh * gb_ref[0:1, :] + gb_ref[1:2, :]
    xn_ref[...] = xn
    hn = _modulate(xn, m_ref, sub + 1)
    hn_ref[...] = hn
    return hn


def _pre_specs(pre, B, nt):
    rows = pl.BlockSpec((TM, D), _rowmap(nt, 0))
    in_specs = [pl.BlockSpec((TM, D), _rowmap(*pre["lx"])), rows, pl.BlockSpec((1, 1, N_MOD, D), _mmap(pre["comb"])),
                pl.BlockSpec((2, D), lambda b, i: (0, 0))]
    shapes = [jax.ShapeDtypeStruct((B * nt * TM, D), F32), jax.ShapeDtypeStruct((B * nt * TM, D), BF16)]
    return in_specs, [pre["X"], pre["Y"], pre["mvec"], pre["gb"]], [rows, rows], shapes


def _resid_grads(do, y, wg, zh, r, w, gb_ref, comb, dx_ref, dy_ref, dg_ref, dgb_ref):
    b_, i = pl.program_id(0), pl.program_id(1)
    dz = _ln_bwd(do * gb_ref[0:1, :], zh, r)
    dx_ref[...] = ALPHA * dz
    dy_ref[...] = (wg * dz).astype(BF16)

    @pl.when((b_ == 0) & (i == 0))
    def _():
        dgb_ref[...] = jnp.zeros_like(dgb_ref)

    dgb_ref[0:1, :] += _rsum(do * zh)
    dgb_ref[1:2, :] += _rsum(do)

    init = (i == 0) | (i == 1) if comb else (i == 0)

    @pl.when(init)
    def _():
        dg_ref[...] = jnp.zeros_like(dg_ref)

    dg_ref[0, 0] += w * _rsum(dz * y)


def _resid_out_shapes(B, nt, comb):
    rows = pl.BlockSpec((TM, D), _rowmap(nt, 0))
    specs = [rows, rows, pl.BlockSpec((1, 1, 1, D), _mmap(comb)), pl.BlockSpec((2, D), lambda b, i: (0, 0))]
    shapes = [jax.ShapeDtypeStruct((B * nt * TM, D), F32), jax.ShapeDtypeStruct((B * nt * TM, D), BF16),
              jax.ShapeDtypeStruct((B, 2, 1, D), F32), jax.ShapeDtypeStruct((2, D), F32)]
    return specs, shapes


def _swiglu_dx(dyv, w_ref, u_ref, du_ref):
    half = DFF // 2
    for j in range(2):
        lo, hi = j * half, (j + 1) * half
        da = _dot_nt(dyv, w_ref[lo:hi, :])
        g = u_ref[:, lo:hi].astype(F32)
        up = u_ref[:, DFF + lo:DFF + hi].astype(F32)
        s = _sigmoid(g)
        du_ref[:, lo:hi] = (da * up * (s * (1.0 + g * (1.0 - s)))).astype(BF16)
        du_ref[:, DFF + lo:DFF + hi] = (da * (g * s)).astype(BF16)


def _tail(X, Y, mvec, gb, tgt, sub, w, Wo, U, B, nt):
    def body(x_ref, y_ref, m_ref, gb_ref, t_ref, w_ref, u_ref, dx_ref, dy_ref, dg_ref, dgb_ref, l_ref, du_ref):
        y, wg, zh, r = _resid_fwd(x_ref, y_ref, m_ref, gb_ref, sub, w)
        e = (zh * gb_ref[0:1, :] + gb_ref[1:2, :]) - t_ref[0]

        @pl.when((pl.program_id(0) == 0) & (pl.program_id(1) == 0))
        def _():
            l_ref[...] = jnp.zeros_like(l_ref)

        l_ref[...] += _rsum(e * e)
        _resid_grads(e * (1.0 / D), y, wg, zh, r, w, gb_ref, False, dx_ref, dy_ref, dg_ref, dgb_ref)
        _swiglu_dx(dy_ref[...], w_ref, u_ref, du_ref)

    rows = pl.BlockSpec((TM, D), _rowmap(nt, 0))
    wide = pl.BlockSpec((TM, 2 * DFF), _rowmap(nt, 0))
    specs, shapes = _resid_out_shapes(B, nt, False)
    return pl.pallas_call(
        body, name="resid2_loss_bwd", grid=(B, nt),
        in_specs=[rows, rows, pl.BlockSpec((1, 1, N_MOD, D), _mmap(False)), pl.BlockSpec((2, D), lambda b, i: (0, 0)),
                  pl.BlockSpec((1, TM, D), lambda b, i: (b, i, 0)), _wspec(Wo, 2), wide],
        out_specs=specs + [pl.BlockSpec((1, D), lambda b, i: (0, 0)), wide],
        out_shape=shapes + [jax.ShapeDtypeStruct((1, D), F32), jax.ShapeDtypeStruct((B * nt * TM, 2 * DFF), BF16)],
        compiler_params=_cp(2))(X, Y, mvec, gb, tgt, Wo, U)


def _lnmod_resid_bwd(dh, Xi, lxi, dres, Xp, lxp, Yp, mvec, comb, sub, w, gb, B, nt, name):
    ntl = nt - 1 if comb else nt

    def body(dh_ref, xi_ref, dr_ref, xp_ref, yp_ref, m_ref, gb_ref, dx_ref, dy_ref, dg_ref, dgb_ref, dm_ref):
        i = pl.program_id(1)
        xh, r = _ln(xi_ref[...])
        sh = m_ref[0, 0, 3 * sub:3 * sub + 1, :]
        dhv = dh_ref[...].astype(F32)
        dr = dr_ref[...]
        if comb:
            dr = jnp.where(i > 0, dr, 0.0)
        do = _ln_bwd(dhv * (1.0 + sh), xh, r) + dr

        init = (i == 0) | (i == 1) if comb else (i == 0)

        @pl.when(init)
        def _():
            dm_ref[...] = jnp.zeros_like(dm_ref)

        dm_ref[0, 0, 0:1, :] += _rsum(dhv * xh)
        dm_ref[0, 0, 1:2, :] += _rsum(dhv)
        y, wg, zh, r2 = _resid_fwd(xp_ref, yp_ref, m_ref, gb_ref, sub - 1, w)
        _resid_grads(do, y, wg, zh, r2, w, gb_ref, comb, dx_ref, dy_ref, dg_ref, dgb_ref)

    rows = pl.BlockSpec((TM, D), _rowmap(nt, 0))
    if comb:
        dres_spec = pl.BlockSpec((TM, D), lambda b, i: (b * ntl + jnp.maximum(i - 1, 0), 0))
    else:
        dres_spec = rows
    specs, shapes = _resid_out_shapes(B, nt, comb)
    return pl.pallas_call(
        body, name=name, grid=(B, nt),
        in_specs=[rows, pl.BlockSpec((TM, D), _rowmap(*lxi)), dres_spec, pl.BlockSpec((TM, D), _rowmap(*lxp)), rows,
                  pl.BlockSpec((1, 1, N_MOD, D), _mmap(comb)), pl.BlockSpec((2, D), lambda b, i: (0, 0))],
        out_specs=specs + [pl.BlockSpec((1, 1, 2, D), _mmap(comb))],
        out_shape=shapes + [jax.ShapeDtypeStruct((B, 2, 2, D), F32)],
        compiler_params=_cp(2))(dh, Xi, dres, Xp, Yp, mvec, gb)


def _ffn_out_dx(dy, W, u, name, dep=None):
    M = dy.shape[0]
    deps = [] if dep is None else [dep]

    def body(dy_ref, w_ref, u_ref, *rest):
        _swiglu_dx(dy_ref[...], w_ref, u_ref, rest[-1])

    return pl.pallas_call(
        body, name=name, grid=(M // TM,),
        in_specs=[pl.BlockSpec((TM, D), lambda i: (i, 0)), _wspec(W, 1), pl.BlockSpec((TM, 2 * DFF), lambda i: (i, 0))]
        + [_wspec(d, 1) for d in deps],
        out_specs=pl.BlockSpec((TM, 2 * DFF), lambda i: (i, 0)),
        out_shape=jax.ShapeDtypeStruct((M, 2 * DFF), BF16), compiler_params=_cp(1))(dy, W, u, *deps)


def _wspec(W, nidx):
    zeros = (0,) * W.ndim
    if nidx == 1:
        return pl.BlockSpec(W.shape, lambda i: zeros)
    return pl.BlockSpec(W.shape, lambda b, i: zeros)


def _mm_nn(A, la, W, B, nt, out_dtype, name):
    K, N = W.shape

    def body(a_ref, w_ref, o_ref):
        o_ref[...] = _dot(a_ref[...], w_ref[...]).astype(out_dtype)

    return pl.pallas_call(
        body, name=name, grid=(B, nt),
        in_specs=[pl.BlockSpec((TM, K), _rowmap(*la)), _wspec(W, 2)],
        out_specs=pl.BlockSpec((TM, N), _rowmap(nt, 0)),
        out_shape=jax.ShapeDtypeStruct((B * nt * TM, N), out_dtype), compiler_params=_cp(2))(A, W)


def _ffn_in(A, la, W3, B, nt, name, pre=None):
    K, n = W3.shape[1:]

    def body(*refs):
        if pre is None:
            a_ref, w_ref, u_ref, s_ref = refs
            a = a_ref[...]
        else:
            x_ref, y_ref, m_ref, gb_ref, w_ref, u_ref, s_ref, xn_ref, hn_ref = refs
            a = _resid_modulate(x_ref, y_ref, m_ref, gb_ref, pre["sub"], pre["w"], xn_ref, hn_ref)
        for j in range(2):
            g = _dot(a, w_ref[j])
            up = _dot(a, w_ref[j + 2])
            u_ref[:, j * n:(j + 1) * n] = g.astype(BF16)
            u_ref[:, (j + 2) * n:(j + 3) * n] = up.astype(BF16)
            s_ref[:, j * n:(j + 1) * n] = (g * _sigmoid(g) * up).astype(BF16)

    rows = B * nt * TM
    if pre is None:
        in_specs, args, xo_specs, xo_shapes = [pl.BlockSpec((TM, K), _rowmap(*la))], [A], [], []
    else:
        in_specs, args, xo_specs, xo_shapes = _pre_specs(pre, B, nt)
    return pl.pallas_call(
        body, name=name, grid=(B, nt),
        in_specs=in_specs + [_wspec(W3, 2)],
        out_specs=[pl.BlockSpec((TM, 4 * n), _rowmap(nt, 0)), pl.BlockSpec((TM, 2 * n), _rowmap(nt, 0))] + xo_specs,
        out_shape=[jax.ShapeDtypeStruct((rows, 4 * n), BF16), jax.ShapeDtypeStruct((rows, 2 * n), BF16)] + xo_shapes,
        compiler_params=_cp(2))(*args, W3)


def _mm_nt(A, W, name, dep=None, out_dtype=F32):
    M, N = A.shape
    K = W.shape[-2]

    def body(a_ref, w_ref, *rest):
        o_ref = rest[-1]
        if W.ndim == 3:
            n = W.shape[-1]
            acc = _dot_nt(a_ref[:, 0:n], w_ref[0])
            for j in range(1, 4):
                acc = acc + _dot_nt(a_ref[:, j * n:(j + 1) * n], w_ref[j])
            o_ref[...] = acc.astype(out_dtype)
        else:
            o_ref[...] = _dot_nt(a_ref[...], w_ref[...]).astype(out_dtype)

    deps = [] if dep is None else [dep]
    return pl.pallas_call(
        body, name=name, grid=(M // TM,),
        in_specs=[pl.BlockSpec((TM, N), lambda i: (i, 0)), _wspec(W, 1)] + [_wspec(d, 1) for d in deps],
        out_specs=pl.BlockSpec((TM, K), lambda i: (i, 0)),
        out_shape=jax.ShapeDtypeStruct((M, K), out_dtype), compiler_params=_cp(1))(A, W, *deps)


def _mm_tn(A, G, name, tn=512, shards=None):
    M, K = A.shape
    N = G.shape[1]
    tk = next(t for t in ((2048, 1536, 1024, 512) if K <= D else (1024, 512)) if M % t == 0)
    if shards:
        tn = N // shards

    def body(a_ref, g_ref, o_ref):
        @pl.when(pl.program_id(1) == 0)
        def _():
            o_ref[...] = jnp.zeros_like(o_ref)

        upd = _dot_tn(a_ref[...], g_ref[...])
        if shards:
            o_ref[0] += upd
        else:
            o_ref[...] += upd

    if shards:
        out_spec = pl.BlockSpec((1, K, tn), lambda n, k: (n, 0, 0))
        out_shape = jax.ShapeDtypeStruct((shards, K, tn), F32)
    else:
        out_spec = pl.BlockSpec((K, tn), lambda n, k: (0, n))
        out_shape = jax.ShapeDtypeStruct((K, N), F32)
    return pl.pallas_call(
        body, name=name, grid=(N // tn, M // tk),
        in_specs=[pl.BlockSpec((tk, K), lambda n, k: (k, 0)), pl.BlockSpec((tk, tn), lambda n, k: (k, n))],
        out_specs=out_spec, out_shape=out_shape, compiler_params=_cp(2))(A, G)


def _logsig(z):
    return jnp.minimum(z, 0.0) - jnp.log(1.0 + jnp.exp(-jnp.abs(z)))


ZW = 2688
ZRQ, ZRFF, ZRFB, ZLR = 1024, 1536, 2048, 2560


def _mix_in_features(H, W, a2p, biasp, lbp, B, nt, pre=None):
    K = W.shape[0]

    def body(*refs):
        if pre is None:
            h_ref, w_ref, a2_ref, bias_ref, lb_ref, q_ref, k_ref, v_ref, g_ref, z_ref = refs
            hv = h_ref[...]
        else:
            (x_ref, y_ref, m_ref, gb_ref, w_ref, a2_ref, bias_ref, lb_ref, q_ref, k_ref, v_ref, g_ref, z_ref,
             xn_ref, hn_ref) = refs
            hv = _resid_modulate(x_ref, y_ref, m_ref, gb_ref, pre["sub"], pre["w"], xn_ref, hn_ref)
        proj = lambda lo, hi: _dot(hv, w_ref[:, lo:hi])
        z_ref[:, 0:2 * 4 * HD] = proj(GG, GG + 2 * 4 * HD)
        lrf = proj(LR, LR + HD)
        z_ref[:, ZLR:ZLR + HD] = lrf
        lr = lrf.astype(BF16)
        lane = lax.broadcasted_iota(jnp.int32, (1, 4 * HD), 1)
        keep = (lane & (HD - 1)) < GLA_DK
        for d in range(2):
            z = _dot(lr, a2_ref[d]) + bias_ref[d:d + 1, :]
            gl = jnp.where(keep, _logsig(z) * (1.0 / GATE_NORM), 0.0)
            for h in range(4):
                g_ref[d, 0, h] = gl[:, h * HD:(h + 1) * HD]
        gq, gk, gv = proj(GQ, GQ + 4 * HD), proj(GK, GK + 4 * HD), proj(GV, GV + 4 * HD)
        for h in range(4):
            sl = slice(h * HD, (h + 1) * HD)
            q_ref[0, h] = gq[:, sl] * (GLA_DK ** -0.5)
            k_ref[0, 0, h] = gk[:, sl]
            k_ref[1, 0, h] = gk[:, sl]
            v_ref[0, h] = gv[:, sl].astype(BF16)
        rqs, ri = proj(RQ, RQ + 4 * HD), proj(RI, RI + 4 * HD)
        z_ref[:, ZRQ:ZRQ + 4 * HD] = rqs
        for h in range(4):
            sl = slice(h * HD, (h + 1) * HD)
            rq = rqs[:, sl]
            q_ref[0, 4 + h] = rq * _sigmoid(rq) * (HD ** -0.5)
            v_ref[0, 4 + h] = ri[:, sl].astype(BF16)
        for d, off, zoff in ((0, RFF, ZRFF), (1, RFB, ZRFB)):
            rf = proj(off, off + 4 * HD)
            z_ref[:, zoff:zoff + 4 * HD] = rf
            for h in range(4):
                sl = slice(h * HD, (h + 1) * HD)
                lb = lb_ref[d:d + 1, sl]
                f = lb + (1.0 - lb) * _sigmoid(rf[:, sl])
                g_ref[d, 0, 4 + h] = jnp.log(f)
                k_ref[d, 0, 4 + h] = 1.0 - f

    one = pl.BlockSpec((1, NH, TM, HD), lambda b, i: (b, 0, i, 0))
    two = pl.BlockSpec((2, 1, NH, TM, HD), lambda b, i: (0, b, 0, i, 0))
    s1 = jax.ShapeDtypeStruct((B, NH, nt * TM, HD), F32)
    s2 = jax.ShapeDtypeStruct((2, B, NH, nt * TM, HD), F32)
    if pre is None:
        in_specs, args, xo_specs, xo_shapes = [pl.BlockSpec((TM, K), _rowmap(nt, 0))], [H], [], []
    else:
        in_specs, args, xo_specs, xo_shapes = _pre_specs(pre, B, nt)
    return pl.pallas_call(
        body, name="mix_in_features", grid=(B, nt),
        in_specs=in_specs + [_wspec(W, 2), pl.BlockSpec((2, HD, 4 * HD), lambda b, i: (0, 0, 0)),
                             pl.BlockSpec((2, 4 * HD), lambda b, i: (0, 0)), pl.BlockSpec((2, 4 * HD), lambda b, i: (0, 0))],
        out_specs=[one, two, one, two, pl.BlockSpec((TM, ZW), _rowmap(nt, 0))] + xo_specs,
        out_shape=[s1, s2, jax.ShapeDtypeStruct(s1.shape, BF16), s2,
                   jax.ShapeDtypeStruct((B * nt * TM, ZW), F32)] + xo_shapes,
        compiler_params=_cp(2))(*args, W, a2p, biasp, lbp)


def _features_bwd(Z, a2p, biasp, lbp, dQ0, dQ1, dK0, dK1, dV0, dV1, dG0, dG1, dgates, W, B):
    nt = Z.shape[0] // (B * TM)

    def body(z_ref, a2_ref, bias_ref, lb_ref, dq0, dq1, dk0, dk1, dv0, dv1, dg0, dg1, dgt_ref, w_ref,
             df_ref, dh_ref, da2_ref, dbias_ref, dlb_ref):
        b_, i = pl.program_id(0), pl.program_id(1)

        @pl.when((b_ == 0) & (i == 0))
        def _():
            da2_ref[...] = jnp.zeros_like(da2_ref)
            dbias_ref[...] = jnp.zeros_like(dbias_ref)
            dlb_ref[...] = jnp.zeros_like(dlb_ref)

        df_ref[:, 0:2 * 4 * HD] = jnp.where(i > 0, dgt_ref[...], 0.0).astype(BF16)
        df_ref[:, LR + HD:] = jnp.zeros((TM, MIXP - LR - HD), BF16)
        lr = z_ref[:, ZLR:ZLR + HD].astype(BF16)
        lane = lax.broadcasted_iota(jnp.int32, (1, 4 * HD), 1)
        keep = (lane & (HD - 1)) < GLA_DK
        dlr = jnp.zeros((TM, HD), F32)
        dgs = (dg0, dg1)
        dks = (dk0, dk1)
        rd = lambda ref, h: ref[0, h].astype(F32)
        for d in range(2):
            z = _dot(lr, a2_ref[d]) + bias_ref[d:d + 1, :]
            dgl = jnp.concatenate([rd(dgs[d], h) for h in range(4)], axis=1)
            dz = jnp.where(keep, dgl * (1.0 / GATE_NORM) * (1.0 - _sigmoid(z)), 0.0)
            dzb = dz.astype(BF16)
            dlr = dlr + _dot_nt(dzb, a2_ref[d])
            da2_ref[d] += _dot_tn(lr, dzb)
            dbias_ref[d:d + 1, :] += _rsum(dz)
        df_ref[:, LR:LR + HD] = dlr.astype(BF16)
        for h in range(4):
            df_ref[:, GQ + h * HD:GQ + (h + 1) * HD] = ((rd(dq0, h) + rd(dq1, h)) * (GLA_DK ** -0.5)).astype(BF16)
            df_ref[:, GK + h * HD:GK + (h + 1) * HD] = (rd(dk0, h) + rd(dk1, h)).astype(BF16)
            df_ref[:, GV + h * HD:GV + (h + 1) * HD] = (rd(dv0, h) + rd(dv1, h)).astype(BF16)
        for h in range(4):
            sl = slice(h * HD, (h + 1) * HD)
            rq = z_ref[:, ZRQ + h * HD:ZRQ + (h + 1) * HD]
            s = _sigmoid(rq)
            dqh = rd(dq0, 4 + h) + rd(dq1, 4 + h)
            df_ref[:, RQ + h * HD:RQ + (h + 1) * HD] = (dqh * (HD ** -0.5) * (s * (1.0 + rq * (1.0 - s)))).astype(BF16)
            df_ref[:, RI + h * HD:RI + (h + 1) * HD] = (rd(dv0, 4 + h) + rd(dv1, 4 + h)).astype(BF16)
            for d, off, zoff in ((0, RFF, ZRFF), (1, RFB, ZRFB)):
                lb = lb_ref[d:d + 1, sl]
                sg = _sigmoid(z_ref[:, zoff + h * HD:zoff + (h + 1) * HD])
                f = lb + (1.0 - lb) * sg
                dff = rd(dgs[d], 4 + h) / f - rd(dks[d], 4 + h)
                df_ref[:, off + h * HD:off + (h + 1) * HD] = (dff * (1.0 - lb) * sg * (1.0 - sg)).astype(BF16)
                dlb_ref[d:d + 1, sl] += _rsum(dff * (1.0 - sg))
        dh_ref[...] = _dot_nt(df_ref[...], w_ref[...]).astype(BF16)

    m0 = lambda b, i: (b, 0, i, 0)
    one = lambda m: pl.BlockSpec((1, NH, TM, HD), m)
    return pl.pallas_call(
        body, name="mix_features_bwd", grid=(B, nt),
        in_specs=[pl.BlockSpec((TM, ZW), _rowmap(nt, 0)), pl.BlockSpec((2, HD, 4 * HD), lambda b, i: (0, 0, 0)),
                  pl.BlockSpec((2, 4 * HD), lambda b, i: (0, 0)), pl.BlockSpec((2, 4 * HD), lambda b, i: (0, 0)),
                  one(m0), one(m0), one(m0), one(m0), one(m0), one(m0), one(m0), one(m0),
                  pl.BlockSpec((TM, D), lambda b, i: (b * (nt - 1) + jnp.maximum(i - 1, 0), 0)), _wspec(W, 2)],
        out_specs=[pl.BlockSpec((TM, MIXP), _rowmap(nt, 0)), pl.BlockSpec((TM, D), _rowmap(nt, 0)),
                   pl.BlockSpec((2, HD, 4 * HD), lambda b, i: (0, 0, 0)),
                   pl.BlockSpec((2, 4 * HD), lambda b, i: (0, 0)), pl.BlockSpec((2, 4 * HD), lambda b, i: (0, 0))],
        out_shape=[jax.ShapeDtypeStruct((B * nt * TM, MIXP), BF16), jax.ShapeDtypeStruct((B * nt * TM, D), BF16),
                   jax.ShapeDtypeStruct((2, HD, 4 * HD), F32),
                   jax.ShapeDtypeStruct((2, 4 * HD), F32), jax.ShapeDtypeStruct((2, 4 * HD), F32)],
        compiler_params=_cp(2))(Z, a2p, biasp, lbp, dQ0, dQ1, dK0, dK1, dV0, dV1, dG0, dG1, dgates, W)


def _chunk_scan(x, rin, fwd):
    acc = x
    sft = 1
    while sft < CH:
        if fwd:
            acc = acc + jnp.where(rin >= sft, pltpu.roll(acc, sft, 0), 0.0)
        else:
            acc = acc + jnp.where(rin < CH - sft, pltpu.roll(acc, TM - sft, 0), 0.0)
        sft *= 2
    return acc


def _chunk_total(x):
    t = jnp.sum(x.reshape(NCB, CH, HD), axis=1, keepdims=True)
    return jnp.broadcast_to(t, (NCB, CH, HD)).reshape(TM, HD)


def _scan_masks(rev):
    rin = lax.broadcasted_iota(jnp.int32, (TM, HD), 0) & (CH - 1)
    ri = lax.broadcasted_iota(jnp.int32, (SB, SB), 0)
    ci = lax.broadcasted_iota(jnp.int32, (SB, SB), 1)
    same = (ri >> 5) == (ci >> 5)
    lo = same & (ri >= ci)
    up = same & (ri <= ci)
    mask, maskT = (up, lo) if rev else (lo, up)
    re = lax.broadcasted_iota(jnp.int32, (SB, CSB * HD), 0) >> 5
    ce = lax.broadcasted_iota(jnp.int32, (SB, CSB * HD), 1) >> 7
    return rin, mask, maskT, re == ce


def _scan_decay(q, k, g, rin, rev):
    b = _chunk_scan(g, rin, not rev)
    xx = _chunk_total(g) - b
    eb = jnp.exp(b)
    return b, xx, eb, q * eb, k * jnp.exp(-b), k * jnp.exp(xx)


def _sub(x, s):
    return x[s * SB:(s + 1) * SB]


def _expand(xb, mexp):
    return jnp.where(mexp, jnp.concatenate([xb] * CSB, axis=1), jnp.zeros((), xb.dtype))


def _own(x, mexp):
    xm = jnp.where(mexp, x, 0.0)
    acc = xm[:, 0:HD]
    for n in range(1, CSB):
        acc = acc + xm[:, n * HD:(n + 1) * HD]
    return acc


def _stack(per_chunk, s):
    return jnp.concatenate(per_chunk[s * CSB:(s + 1) * CSB], axis=1)


def _state_pass(s0, eb, uts, rev):
    order = range(NCB - 1, -1, -1) if rev else range(NCB)
    states = [None] * NCB
    s = s0
    for n in order:
        row = n * CH if rev else n * CH + CH - 1
        states[n] = s
        s = eb[row:row + 1, :] * s + uts[n // CSB][:, (n % CSB) * HD:(n % CSB + 1) * HD]
    return states, s


def _scan_fwd(Q, K, V, G, rev, B):
    nb = Q.shape[2] // TM
    d = 1 if rev else 0
    rmap = (lambda s: jnp.where(s == 0, 0, nb - s)) if rev else (lambda s: s)

    def body(q_ref, k_ref, v_ref, g_ref, o_ref, st_ref, s_scr):
        @pl.when(pl.program_id(2) == 0)
        def _():
            s_scr[...] = jnp.zeros_like(s_scr)

        rin, mask, _, mexp = _scan_masks(rev)

        def head(p, readout):
            s0 = s_scr[p]
            st_ref[0, p, 0] = s0
            _, _, eb, qd, ki, kt = _scan_decay(q_ref[0, p], k_ref[0, 0, p], g_ref[0, 0, p], rin, rev)
            ktb, vb = kt.astype(BF16), v_ref[0, p]
            uts = [_dot_tn(_sub(vb, s), _expand(_sub(ktb, s), mexp)) for s in range(NSB)]
            states, s_new = _state_pass(s0, eb, uts, rev)
            s_scr[p] = s_new
            if not readout:
                return
            qb, kib = qd.astype(BF16), ki.astype(BF16)
            for s in range(NSB):
                a = jnp.where(mask, _dot_nt(_sub(qb, s), _sub(kib, s)), 0.0)
                o_ref[0, p, s * SB:(s + 1) * SB, :] = (
                    _dot(a.astype(BF16), _sub(vb, s))
                    + _dot_nt(_expand(_sub(qb, s), mexp), _stack(states, s).astype(BF16)))

        @pl.when(pl.program_id(2) >= 1)
        def _():
            for p in range(HP):
                head(p, True)

        @pl.when(pl.program_id(2) == 0)
        def _():
            for p in range(HP):
                head(p, False)

    one = pl.BlockSpec((1, HP, TM, HD), lambda b, h, s: (b, h, rmap(s), 0))
    two = pl.BlockSpec((1, 1, HP, TM, HD), lambda b, h, s: (d, b, h, rmap(s), 0))
    return pl.pallas_call(
        body, name="scan_fwd_rev" if rev else "scan_fwd", grid=(B, NH // HP, nb),
        in_specs=[one, two, one, two],
        out_specs=[one, pl.BlockSpec((1, HP, 1, HD, HD), lambda b, h, s: (b, h, s, 0, 0))],
        out_shape=[jax.ShapeDtypeStruct(Q.shape, F32), jax.ShapeDtypeStruct((B, NH, nb, HD, HD), F32)],
        scratch_shapes=[pltpu.VMEM((HP, HD, HD), F32)],
        compiler_params=_cp(3))(Q, K, V, G)


def _scan_bwd(Q, K, V, G, St, dO, rev, B):
    nb = Q.shape[2] // TM
    d = 1 if rev else 0
    smap = lambda t: nb - 1 - t
    rmap = (lambda t: jnp.where(smap(t) == 0, 0, nb - smap(t))) if rev else smap

    def body(q_ref, k_ref, v_ref, g_ref, st_ref, do_ref, dq_ref, dk_ref, dv_ref, dg_ref, ds_scr):
        t = pl.program_id(2)

        @pl.when(t == 0)
        def _():
            ds_scr[...] = jnp.zeros_like(ds_scr)

        is_lat = smap(t) >= 1
        rin, mask, maskT, mexp = _scan_masks(rev)
        for p in range(HPB):
            b, xx, eb, qd, ki, kt = _scan_decay(q_ref[0, p], k_ref[0, 0, p], g_ref[0, 0, p], rin, rev)
            qb, kib, ktb, vb = qd.astype(BF16), ki.astype(BF16), kt.astype(BF16), v_ref[0, p]
            dob = jnp.where(is_lat, do_ref[0, p], 0.0).astype(BF16)
            kt_exps = [_expand(_sub(ktb, s), mexp) for s in range(NSB)]
            uts = [_dot_tn(_sub(vb, s), kt_exps[s]) for s in range(NSB)]
            states, _ = _state_pass(st_ref[0, p, 0], eb, uts, rev)
            gts = [_dot_tn(_sub(dob, s), _expand(_sub(qb, s), mexp)) for s in range(NSB)]
            order = range(NCB) if rev else range(NCB - 1, -1, -1)
            dsp = [None] * NCB
            t2 = [None] * NCB
            dsc = ds_scr[p]
            for n in order:
                row = n * CH if rev else n * CH + CH - 1
                ebl = eb[row:row + 1, :]
                dsp[n] = dsc
                t2[n] = jnp.broadcast_to(ebl * _rsum(states[n] * dsc), (CH, HD))
                dsc = gts[n // CSB][:, (n % CSB) * HD:(n % CSB + 1) * HD] + ebl * dsc
            ds_scr[p] = dsc
            dqds, dkis, dkts = [], [], []
            for s in range(NSB):
                q_s, ki_s, v_s, do_s = _sub(qb, s), _sub(kib, s), _sub(vb, s), _sub(dob, s)
                dspb = _stack(dsp, s).astype(BF16)
                da = jnp.where(mask, _dot_nt(do_s, v_s), 0.0).astype(BF16)
                dat = jnp.where(maskT, _dot_nt(v_s, do_s), 0.0).astype(BF16)
                at = jnp.where(maskT, _dot_nt(ki_s, q_s), 0.0).astype(BF16)
                dqds.append(_dot(da, ki_s) + _own(_dot(do_s, _stack(states, s).astype(BF16)), mexp))
                dkis.append(_dot(dat, q_s))
                dv_ref[0, p, s * SB:(s + 1) * SB, :] = (_dot(at, do_s) + _dot_nt(kt_exps[s], dspb)).astype(BF16)
                dkts.append(_own(_dot(v_s, dspb), mexp))
            dqd, dki, dkt = (jnp.concatenate(parts, axis=0) for parts in (dqds, dkis, dkts))
            z = dkt * kt
            db = dqd * qd - dki * ki
            dq_ref[0, p] = (dqd * eb).astype(BF16)
            dk_ref[0, p] = (dki * jnp.exp(-b) + dkt * jnp.exp(xx)).astype(BF16)
            dg_ref[0, p] = (_chunk_total(db) + (db - z) + _chunk_scan(z - db, rin, not rev)
                            + jnp.concatenate(t2, axis=0)).astype(BF16)

    one = pl.BlockSpec((1, HPB, TM, HD), lambda b, h, t: (b, h, rmap(t), 0))
    two = pl.BlockSpec((1, 1, HPB, TM, HD), lambda b, h, t: (d, b, h, rmap(t), 0))
    lat = pl.BlockSpec((1, HPB, TM, HD), lambda b, h, t: (b, h, jnp.clip(rmap(t) - 1, 0, nb - 2), 0))
    shp = jax.ShapeDtypeStruct(Q.shape, BF16)
    return pl.pallas_call(
        body, name="scan_bwd_rev" if rev else "scan_bwd", grid=(B, NH // HPB, nb),
        in_specs=[one, two, one, two, pl.BlockSpec((1, HPB, 1, HD, HD), lambda b, h, t: (b, h, smap(t), 0, 0)), lat],
        out_specs=[one, one, one, one], out_shape=[shp, shp, shp, shp],
        scratch_shapes=[pltpu.VMEM((HPB, HD, HD), F32)],
        compiler_params=_cp(3))(Q, K, V, G, St, dO)


def _gnorm_mix_out(O0, O1, F, gains, W, B, ntl):
    nt = ntl + 1

    def body(o0_ref, o1_ref, f_ref, gn_ref, w_ref, m_ref, y_ref):
        for h in range(NH):
            o = o0_ref[0, h] + o1_ref[0, h]
            r = lax.rsqrt(jnp.mean(o * o, axis=-1, keepdims=True) + NORM_EPS)
            gn = gn_ref[0:1, :] if h < 4 else gn_ref[1:2, :]
            gt = f_ref[:, h * HD:(h + 1) * HD]
            m_ref[:, h * HD:(h + 1) * HD] = (o * r * gn * (gt * _sigmoid(gt))).astype(BF16)
        y_ref[...] = _dot(m_ref[...], w_ref[...]).astype(BF16)

    ospec = pl.BlockSpec((1, NH, TM, HD), lambda b, i: (b, 0, i + 1, 0))
    rows = pl.BlockSpec((TM, D), _rowmap(ntl, 0))
    shp = jax.ShapeDtypeStruct((B * ntl * TM, D), BF16)
    return pl.pallas_call(
        body, name="gated_norm_mix_out", grid=(B, ntl),
        in_specs=[ospec, ospec, pl.BlockSpec((TM, D), lambda b, i: (b * nt + 1 + i, 0)),
                  pl.BlockSpec((2, HD), lambda b, i: (0, 0)), _wspec(W, 2)],
        out_specs=[rows, rows], out_shape=[shp, shp], compiler_params=_cp(2))(O0, O1, F, gains, W)


def _mix_out_dx_gnorm_bwd(dY, W, O0, O1, F, gains, dep, B, ntl):
    nt = ntl + 1

    def body(dy_ref, w_ref, o0_ref, o1_ref, f_ref, gn_ref, dep_ref, do_ref, dgt_ref, dgn_ref):
        b_, i = pl.program_id(0), pl.program_id(1)

        @pl.when((b_ == 0) & (i == 0))
        def _():
            dgn_ref[...] = jnp.zeros_like(dgn_ref)

        dyv = dy_ref[...]
        for h in range(NH):
            o = o0_ref[0, h] + o1_ref[0, h]
            r = lax.rsqrt(jnp.mean(o * o, axis=-1, keepdims=True) + NORM_EPS)
            y = o * r
            gn = gn_ref[0:1, :] if h < 4 else gn_ref[1:2, :]
            gt = f_ref[:, h * HD:(h + 1) * HD]
            s = _sigmoid(gt)
            dm = _dot_nt(dyv, w_ref[h * HD:(h + 1) * HD, :])
            don = dm * (gt * s)
            dgt_ref[:, h * HD:(h + 1) * HD] = (dm * (y * gn) * (s * (1.0 + gt * (1.0 - s)))).astype(BF16)
            row = 0 if h < 4 else 1
            dgn_ref[row:row + 1, :] += _rsum(don * y)
            dy = don * gn
            do_ref[0, h] = (r * (dy - y * jnp.mean(dy * y, axis=-1, keepdims=True))).astype(BF16)

    ospec = pl.BlockSpec((1, NH, TM, HD), lambda b, i: (b, 0, i + 1, 0))
    return pl.pallas_call(
        body, name="mix_out_dx_gated_norm_bwd", grid=(B, ntl),
        in_specs=[pl.BlockSpec((TM, D), _rowmap(ntl, 0)), _wspec(W, 2), ospec, ospec,
                  pl.BlockSpec((TM, D), lambda b, i: (b * nt + 1 + i, 0)), pl.BlockSpec((2, HD), lambda b, i: (0, 0)),
                  _wspec(dep, 2)],
        out_specs=[pl.BlockSpec((1, NH, TM, HD), lambda b, i: (b, 0, i, 0)), pl.BlockSpec((TM, D), _rowmap(ntl, 0)),
                   pl.BlockSpec((2, HD), lambda b, i: (0, 0))],
        out_shape=[jax.ShapeDtypeStruct((B, NH, ntl * TM, HD), BF16), jax.ShapeDtypeStruct((B * ntl * TM, D), BF16),
                   jax.ShapeDtypeStruct((2, HD), F32)],
        compiler_params=_cp(2))(dY, W, O0, O1, F, gains, dep)


def _sincos_2d(rows, width, dim):
    quarter = dim // 4
    omega = 1.0 / 10000.0 ** (jnp.arange(quarter, dtype=F32) / quarter)

    def emb(n):
        a = jnp.arange(n).astype(F32)[:, None] * omega[None, :]
        return jnp.concatenate([jnp.sin(a), jnp.cos(a)], axis=-1)

    er = jnp.broadcast_to(emb(rows)[:, None, :], (rows, width, dim // 2))
    ec = jnp.broadcast_to(emb(width)[None, :, :], (rows, width, dim // 2))
    return jnp.concatenate([er, ec], axis=-1).reshape(rows * width, dim)


def _pad_heads(w):
    k = w.shape[0]
    return jnp.pad(w.reshape(k, 4, GLA_DK), ((0, 0), (0, 0), (0, HD - GLA_DK))).reshape(k, 4 * HD)


def _unpad_heads(w):
    k = w.shape[0]
    return w.reshape(k, 4, HD)[:, :, :GLA_DK].reshape(k, 4 * GLA_DK)


MIX_N = 1032
MIX_NP = 1152
_SEGS = ([(64 * h, 64, GQ + HD * h) for h in range(4)] + [(256 + 64 * h, 64, GK + HD * h) for h in range(4)]
         + [(512, 512, GV), (1024, 512, GG), (1536, 32, LR), (1568, 512, RQ), (2080, 512, RFF), (2592, 512, RFB),
            (3104, 512, RI), (3616, 512, RG)])


def _mix_in_to_padded(ps):
    k = ps.shape[1]
    parts, pos = [], 0
    for g0, ln, s0 in sorted(_SEGS, key=lambda s: s[2]):
        if s0 > pos:
            parts.append(jnp.zeros((k, s0 - pos), ps.dtype))
        for j in range(4):
            lo, hi = max(g0, j * MIX_N), min(g0 + ln, (j + 1) * MIX_N)
            if lo < hi:
                parts.append(ps[j][:, lo - j * MIX_N:hi - j * MIX_N])
        pos = s0 + ln
    parts.append(jnp.zeros((k, MIXP - pos), ps.dtype))
    return jnp.concatenate(parts, axis=1)


def _mix_in_from_padded(g):
    k = g.shape[0]
    shards = []
    for j in range(4):
        parts = []
        for g0, ln, s0 in sorted(_SEGS):
            lo, hi = max(g0, j * MIX_N), min(g0 + ln, (j + 1) * MIX_N)
            if lo < hi:
                parts.append(g[:, s0 + lo - g0:s0 + hi - g0])
        parts.append(jnp.zeros((k, MIX_NP - MIX_N), g.dtype))
        shards.append(jnp.concatenate(parts, axis=1))
    return jnp.stack(shards)


def _local_step(x, ctx, tgt, mvec, weights_for, prefetch, ln_gain, ln_bias, a2f, a2b, abf, abb, lb, gng, gnh, on_grads, on_sent):
    B, T, _ = x.shape
    assert ctx.shape[1] == TM and T % TM == 0
    ntl = T // TM
    nt = ntl + 1
    C, L, CL = (nt, 0), (ntl, 0), (nt, 1)
    pos = _sincos_2d(T // 64, 64, D)
    gbs = [jnp.stack([ln_gain[i], ln_bias[i]]) for i in range(3)]
    a2p = jnp.zeros((2, HD, 4 * HD), F32)
    a2p = a2p.at[0, 0:16].set(_pad_heads(a2f)).at[1, 16:32].set(_pad_heads(a2b)).astype(BF16)
    biasp = jnp.concatenate([_pad_heads(abf.reshape(1, -1)), _pad_heads(abb.reshape(1, -1))], axis=0)
    gains = jnp.concatenate([gng.reshape(1, HD), gnh.reshape(1, HD)], axis=0)

    X0, h0 = _embed_lnmod(x, ctx, pos, mvec)
    w1i, w1o = weights_for("ffn1", h0)
    u0, a0 = _ffn_in(h0, C, w1i, B, nt, "ffn1_in")
    tok = prefetch("mix", a0)
    y0 = _mm_nn(a0, C, w1o, B, nt, BF16, "ffn1_out")
    wmp, wmo = weights_for("mix", y0)
    Q, K, V, G, Zm, X1, h1 = _mix_in_features(None, wmp, a2p, biasp, lb, B, nt, pre=dict(
        X=X0, lx=C, Y=y0, mvec=mvec, comb=True, sub=0, w=0.5, gb=gbs[0] + tok[0, 0]))
    prefetch("ffn2", Zm)
    O0, S0 = _scan_fwd(Q, K, V, G, False, B)
    O1, S1 = _scan_fwd(Q, K, V, G, True, B)
    merged, y1 = _gnorm_mix_out(O0, O1, Zm, gains, wmo, B, ntl)
    w2i, w2o = weights_for("ffn2", y1)
    u2, a2, X2, h2 = _ffn_in(None, L, w2i, B, ntl, "ffn2_in", pre=dict(
        X=X1, lx=CL, Y=y1, mvec=mvec, comb=False, sub=1, w=1.0, gb=gbs[1]))
    y2 = _mm_nn(a2, L, w2o, B, ntl, BF16, "ffn2_out")

    dx2r, dy2, dgate2, dgb2, lsum, du2 = _tail(X2, y2, mvec, gbs[2], tgt, 2, 0.5, w2o, u2, B, ntl)
    loss = (0.5 / D) * jnp.sum(lsum)
    g_w2o = _mm_tn(a2, dy2, "ffn2_out_dw")
    dh2 = _mm_nt(du2, w2i, "ffn2_in_dx", out_dtype=BF16)
    g_w2i = _mm_tn(h2, du2, "ffn2_in_dw", shards=4)
    tok = on_grads("ffn2", (g_w2i, g_w2o))
    dx1r, dy1, dgate1, dgb1, dss2 = _lnmod_resid_bwd(dh2, X2, L, dx2r, X1, CL, y1, mvec, False, 2, 1.0,
                                                     gbs[1] + tok[0, 0], B, ntl, "lnmod2_resid1_bwd")
    tok = on_sent("ffn2", dy1)
    g_wmo = _mm_tn(merged, dy1, "mix_out_dw", tn=D)
    dO, dgates, dgains = _mix_out_dx_gnorm_bwd(dy1, wmo, O0, O1, Zm, gains, tok, B, ntl)
    dQ0, dK0, dV0, dG0 = _scan_bwd(Q, K, V, G, S0, dO, False, B)
    dQ1, dK1, dV1, dG1 = _scan_bwd(Q, K, V, G, S1, dO, True, B)
    dF, dh1, da2p, dbiasp, dlb = _features_bwd(Zm, a2p, biasp, lb, dQ0, dQ1, dK0, dK1, dV0, dV1, dG0, dG1, dgates,
                                               wmp, B)
    g_wmp = _mm_tn(h1, dF, "mix_in_dw", tn=MIXP // 4)
    tok = on_grads("mix", (g_wmp, g_wmo))
    dx0r, dy0, dgate0, dgb0, dss1 = _lnmod_resid_bwd(dh1, X1, C, dx1r, X0, C, y0, mvec, True, 1, 0.5,
                                                     gbs[0] + tok[0, 0], B, nt, "lnmod1_resid0_bwd")
    tok = on_sent("mix", dy0)
    du0 = _ffn_out_dx(dy0, w1o, u0, "ffn1_out_dx", dep=tok)
    g_w1o = _mm_tn(a0, dy0, "ffn1_out_dw")
    g_w1i = _mm_tn(h0, du0, "ffn1_in_dw", shards=4)
    tok = on_grads("ffn1", (g_w1i, g_w1o))
    dh0 = _mm_nt(du0, w1i, "ffn1_in_dx", dep=tok, out_dtype=BF16)
    tok = on_sent("ffn1", dh0)
    grad_x, dss0 = _lnmod0_bwd(dh0, X0, mvec + tok[0, 0], dx0r, x.shape, B, nt)

    zero_ctx = lambda a: a.at[:, 0].set(0.0)
    dm = jnp.concatenate([dss0, dgate0, dss1, zero_ctx(dgate1), zero_ctx(dss2), zero_ctx(dgate2)], axis=2)
    small = dict(
        ln_gain=jnp.stack([dgb0[0], dgb1[0], dgb2[0]]), ln_bias=jnp.stack([dgb0[1], dgb1[1], dgb2[1]]),
        a2f=_unpad_heads(da2p[0, 0:16]), a2b=_unpad_heads(da2p[1, 16:32]),
        abf=_unpad_heads(dbiasp[0:1]), abb=_unpad_heads(dbiasp[1:2]), lb=dlb, gng=dgains[0], gnh=dgains[1])
    return loss, grad_x, dm, small


def _small_allgather(xs, name):
    r, n = xs.shape

    def body(x_ref, out_ref, send_sems, recv_sems, local_sem):
        x, y, c = lax.axis_index("x"), lax.axis_index("y"), lax.axis_index("c")
        me, sibling = (x, y, c), (x, y, 1 - c)
        chips = [(1 - x, y), (x, 1 - y), (1 - x, 1 - y)]

        def rows(px, py, pc):
            return out_ref.at[pl.ds((4 * px + 2 * py + pc) * r, r), :]

        def copy(k, block, to, src=None):
            return pltpu.make_async_remote_copy(
                src_ref=rows(*block) if src is None else src, dst_ref=rows(*block),
                send_sem=send_sems.at[k], recv_sem=recv_sems.at[k], device_id=to, device_id_type=MESH)

        mine = pltpu.make_async_copy(x_ref, rows(*me), local_sem)
        mine.start()
        first = [copy(0, me, sibling, src=x_ref)]
        first += [copy(1 + j, me, (*chip, c), src=x_ref) for j, chip in enumerate(chips)]
        for cp in first:
            cp.start()
        passed = [copy(4 + j, (*chip, c), sibling) for j, chip in enumerate(chips)]
        for j, chip in enumerate(chips):
            copy(1 + j, (*chip, c), me).wait_recv()
            passed[j].start()
        copy(0, sibling, me).wait_recv()
        for j, chip in enumerate(chips):
            copy(4 + j, (*chip, 1 - c), me).wait_recv()
        for cp in first + passed:
            cp.wait_send()
        mine.wait()

    out = pl.pallas_call(
        body, name=name,
        out_shape=jax.ShapeDtypeStruct((8 * r, n), xs.dtype),
        in_specs=[pl.BlockSpec(memory_space=pltpu.VMEM)],
        out_specs=pl.BlockSpec(memory_space=pltpu.VMEM),
        scratch_shapes=[pltpu.SemaphoreType.DMA((7,)), pltpu.SemaphoreType.DMA((7,)), pltpu.SemaphoreType.DMA],
        compiler_params=pltpu.CompilerParams(vmem_limit_bytes=VMEM_LIMIT))(xs)
    return out.reshape(8, r, n)


def _gather_flat(v, name):
    n = v.shape[0]
    npad = -(-n // 1024) * 1024
    g = _small_allgather(jnp.pad(v, (0, npad - n)).reshape(8, npad // 8), name)
    return g.reshape(8, npad)[:, :n]


HBM_SPEC = pl.BlockSpec(memory_space=pltpu.HBM)
SEM_SPEC = pl.BlockSpec(memory_space=pltpu.SEMAPHORE)
DATAFLOW = pltpu.SideEffectType.DATAFLOW_SIDE_EFFECTING


def _gather_copies(xs, outs, send_sems, recv_sems):
    x, y, c = lax.axis_index("x"), lax.axis_index("y"), lax.axis_index("c")
    dests = [(x, y, 1 - c), (1 - x, y, c), (x, 1 - y, c), (1 - x, 1 - y, c)]
    return [pltpu.make_async_remote_copy(
        src_ref=xs[w], dst_ref=outs[w].at[4 * x + 2 * y + c], send_sem=send_sems[4 * w + k],
        recv_sem=recv_sems[4 * w + k], device_id=dests[k], device_id_type=MESH)
        for w in range(len(xs)) for k in range(4)]


def _split_start(copies, per_w, srcs, land_lead, after, name):
    n = len(srcs)
    m = per_w * n
    lands = [lax.empty((land_lead,) + s.shape[-2:], s.dtype) for s in srcs]
    deps = [] if after is None else [after]

    def body(*refs):
        xs, ls, outs = refs[:n], refs[n:2 * n], refs[2 * n + len(deps):]
        for cp in copies(xs, ls, outs[:m], outs[m:2 * m]):
            cp.start()
        token = outs[2 * m + 2 * n]
        token[...] = jnp.zeros_like(token)

    outs = pl.pallas_call(
        body, name=name,
        out_shape=([pltpu.SemaphoreType.DMA(())] * (2 * m) + [pltpu.HBM(a.shape, a.dtype) for a in srcs + lands]
                   + [jax.ShapeDtypeStruct((8, 128), F32)]),
        in_specs=[HBM_SPEC] * (2 * n) + [pl.BlockSpec(memory_space=pl.ANY)] * len(deps),
        out_specs=[SEM_SPEC] * (2 * m) + [HBM_SPEC] * (2 * n) + [pl.BlockSpec(memory_space=pltpu.VMEM)],
        input_output_aliases={w: 2 * m + w for w in range(2 * n)},
        compiler_params=pltpu.CompilerParams(has_side_effects=DATAFLOW),
    )(*[pltpu.with_memory_space_constraint(a, pltpu.HBM) for a in srcs + lands], *deps)
    return outs[:2 * m], outs[2 * m:2 * m + n], outs[2 * m + n:2 * m + 2 * n], outs[2 * m + 2 * n]


def _split_wait(copies, per_w, sems, x_thru, l_thru, after, name):
    n = len(x_thru)
    m = per_w * n

    def body(*refs):
        xs, ls, ss = refs[:n], refs[n:2 * n], refs[2 * n:2 * n + 2 * m]
        for cp in copies(xs, ls, ss[:m], ss[m:]):
            cp.wait_send()
            cp.wait_recv()

    outs = pl.pallas_call(
        body, name=name,
        out_shape=[pltpu.HBM(a.shape, a.dtype) for a in list(x_thru) + list(l_thru)],
        in_specs=[HBM_SPEC] * (2 * n) + [SEM_SPEC] * (2 * m) + [pl.BlockSpec(memory_space=pl.ANY)],
        out_specs=[HBM_SPEC] * (2 * n),
        input_output_aliases={w: w for w in range(2 * n)},
        compiler_params=pltpu.CompilerParams(has_side_effects=DATAFLOW),
    )(*x_thru, *l_thru, *sems, after)
    return outs[:n], outs[n:]


def _gather_forward(gathered, name):
    n = len(gathered)

    def body(*refs):
        outs = refs[n:2 * n]
        send_sems, recv_sems = refs[2 * n:]
        x, y, c = lax.axis_index("x"), lax.axis_index("y"), lax.axis_index("c")
        chips = [(1 - x, y), (x, 1 - y), (1 - x, 1 - y), (x, y)]

        def copy(w, j, pc):
            px, py = chips[j]
            slot = outs[w].at[4 * px + 2 * py + pc]
            return pltpu.make_async_remote_copy(
                src_ref=slot, dst_ref=slot, send_sem=send_sems.at[4 * w + j], recv_sem=recv_sems.at[4 * w + j],
                device_id=(x, y, 1 - c), device_id_type=MESH)

        sends = [copy(w, j, 1 - c if j == 3 else c) for w in range(n) for j in range(4)]
        for cp in sends:
            cp.start()
        for w in range(n):
            for j in range(4):
                copy(w, j, c if j == 3 else 1 - c).wait_recv()
        for cp in sends:
            cp.wait_send()

    any_spec = pl.BlockSpec(memory_space=pl.ANY)
    return pl.pallas_call(
        body, name=name,
        out_shape=[jax.ShapeDtypeStruct(g.shape, g.dtype) for g in gathered],
        in_specs=[any_spec] * n, out_specs=[any_spec] * n,
        input_output_aliases={w: w for w in range(n)},
        scratch_shapes=[pltpu.SemaphoreType.DMA((4 * n,)), pltpu.SemaphoreType.DMA((4 * n,))],
    )(*gathered)


def _forward_copies(gs, ls, send_sems, recv_sems):
    x, y, c = lax.axis_index("x"), lax.axis_index("y"), lax.axis_index("c")
    chips = [(1 - x, y), (x, 1 - y), (1 - x, 1 - y), (x, y)]
    cps = []
    for w in range(len(gs)):
        for j, (px, py) in enumerate(chips):
            slot = gs[w].at[4 * px + 2 * py + (1 - c if j == 3 else c)]
            cps.append(pltpu.make_async_remote_copy(
                src_ref=slot, dst_ref=slot, send_sem=send_sems[4 * w + j], recv_sem=recv_sems[4 * w + j],
                device_id=(x, y, 1 - c), device_id_type=MESH))
    return cps


def _pair_copies(gs, ls, send_sems, recv_sems):
    x, y, c = lax.axis_index("x"), lax.axis_index("y"), lax.axis_index("c")
    return [pltpu.make_async_remote_copy(
        src_ref=gs[w].at[2 * j + 1 - c], dst_ref=ls[w].at[j], send_sem=send_sems[4 * w + j],
        recv_sem=recv_sems[4 * w + j], device_id=(x, y, 1 - c), device_id_type=MESH)
        for w in range(len(gs)) for j in range(4)]


def _all_copies(xs, ls, send_sems, recv_sems):
    x, y, c = lax.axis_index("x"), lax.axis_index("y"), lax.axis_index("c")
    flips = [(a, b, e) for a in (0, 1) for b in (0, 1) for e in (0, 1)][1:]
    return [pltpu.make_async_remote_copy(
        src_ref=xs[0], dst_ref=ls[0].at[4 * x + 2 * y + c], send_sem=send_sems[k], recv_sem=recv_sems[k],
        device_id=(1 - x if a else x, 1 - y if b else y, 1 - c if e else c), device_id_type=MESH)
        for k, (a, b, e) in enumerate(flips)]


def _share_copies(fs, ls, send_sems, recv_sems):
    x, y, c = lax.axis_index("x"), lax.axis_index("y"), lax.axis_index("c")
    return [pltpu.make_async_remote_copy(
        src_ref=fs[w].at[c], dst_ref=fs[w].at[c], send_sem=send_sems[w], recv_sem=recv_sems[w],
        device_id=(x, y, 1 - c), device_id_type=MESH) for w in range(len(fs))]


def _chip_copies(hs, ls, send_sems, recv_sems):
    x, y, c = lax.axis_index("x"), lax.axis_index("y"), lax.axis_index("c")
    chips = [(1 - x, y), (x, 1 - y), (1 - x, 1 - y)]
    return [pltpu.make_async_remote_copy(
        src_ref=hs[w].at[2 * px + py], dst_ref=ls[w].at[k], send_sem=send_sems[3 * w + k],
        recv_sem=recv_sems[3 * w + k], device_id=(px, py, c), device_id_type=MESH)
        for w in range(len(hs)) for k, (px, py) in enumerate(chips)]


def _rs_add_pair(g8, r4, c, name):
    R, n = g8.shape[1:]
    rb = R // 2

    def body(c_ref, g_ref, r_ref, o_ref):
        o_ref[...] = (g_ref[...] + r_ref[...]).astype(BF16)

    spec = pl.BlockSpec((1, rb, n), lambda j, i, c_ref: (j, i, 0))
    return pl.pallas_call(
        body, name=name,
        grid_spec=pltpu.PrefetchScalarGridSpec(
            num_scalar_prefetch=1, grid=(4, R // rb),
            in_specs=[pl.BlockSpec((1, rb, n), lambda j, i, c_ref: (2 * j + c_ref[0], i, 0)), spec],
            out_specs=spec),
        out_shape=jax.ShapeDtypeStruct((4, R, n), BF16), compiler_params=_cp(2))(c, g8, r4)


def _rs_add_chips(g8, r4, r3, cj, name):
    R, n = g8.shape[1:]
    rb = R // 2

    def body(cj_ref, g_ref, p_ref, r_ref, o_ref):
        own = g_ref[0] + p_ref[0]
        o_ref[0] = ((own + r_ref[0].astype(F32)) + r_ref[1].astype(F32)) + r_ref[2].astype(F32)

    return pl.pallas_call(
        body, name=name,
        grid_spec=pltpu.PrefetchScalarGridSpec(
            num_scalar_prefetch=1, grid=(R // rb,),
            in_specs=[pl.BlockSpec((1, rb, n), lambda i, cj_ref: (2 * cj_ref[1] + cj_ref[0], i, 0)),
                      pl.BlockSpec((1, rb, n), lambda i, cj_ref: (cj_ref[1], i, 0)),
                      pl.BlockSpec((3, rb, n), lambda i, cj_ref: (0, i, 0))],
            out_specs=pl.BlockSpec((1, rb, n), lambda i, cj_ref: (cj_ref[0], i, 0))),
        out_shape=jax.ShapeDtypeStruct((2, R, n), F32), compiler_params=_cp(1))(cj, g8, r4, r3)


def _sum8(g):
    n = g.shape[1]

    def body(g_ref, o_ref):
        acc = g_ref[0:1, :]
        for k in range(1, 8):
            acc = acc + g_ref[k:k + 1, :]
        o_ref[...] = acc

    return pl.pallas_call(body, name="sum_devices", out_shape=jax.ShapeDtypeStruct((1, n), F32),
                          compiler_params=pltpu.CompilerParams(vmem_limit_bytes=VMEM_LIMIT))(g)


ADA_ROWS = 64


def _ada_fwd(cs, w, b):
    n = w.shape[1]

    def body(c_ref, w_ref, b_ref, o_ref):
        cv = c_ref[...]
        s = (cv * _sigmoid(cv)).astype(BF16)
        o_ref[...] = _dot(s, w_ref[...].astype(BF16)) + b_ref[...]

    return pl.pallas_call(body, name="ada_fwd", out_shape=jax.ShapeDtypeStruct((ADA_ROWS, n), F32),
                          compiler_params=pltpu.CompilerParams(vmem_limit_bytes=VMEM_LIMIT))(cs, w, b)


def _mod_exchange(m_cols):
    n = m_cols.shape[1]

    def body(m_ref, out_ref, send_sems, recv_sems, local_sem):
        x, y, c = lax.axis_index("x"), lax.axis_index("y"), lax.axis_index("c")
        chips = [(1 - x, y), (x, 1 - y), (1 - x, 1 - y)]

        def group(px, py):
            return m_ref.at[pl.ds(pl.multiple_of(8 * (4 * px + 2 * py + c), 8), 8), :]

        mine = pltpu.make_async_copy(group(x, y), out_ref.at[2 * x + y], local_sem)
        mine.start()
        cps = [pltpu.make_async_remote_copy(
            src_ref=group(px, py), dst_ref=out_ref.at[2 * x + y], send_sem=send_sems.at[k], recv_sem=recv_sems.at[k],
            device_id=(px, py, c), device_id_type=MESH) for k, (px, py) in enumerate(chips)]
        for cp in cps:
            cp.start()
        for cp in cps:
            cp.wait_recv()
        for cp in cps:
            cp.wait_send()
        mine.wait()

    return pl.pallas_call(
        body, name="mod_exchange", out_shape=jax.ShapeDtypeStruct((4, 8, n), F32),
        in_specs=[pl.BlockSpec(memory_space=pltpu.VMEM)], out_specs=pl.BlockSpec(memory_space=pltpu.VMEM),
        scratch_shapes=[pltpu.SemaphoreType.DMA((3,)), pltpu.SemaphoreType.DMA((3,)), pltpu.SemaphoreType.DMA],
        compiler_params=pltpu.CompilerParams(vmem_limit_bytes=VMEM_LIMIT))(m_cols)


def _ada_bwd(cs, w, dm):
    n = w.shape[1]

    def body(c_ref, w_ref, dm_ref, gw_ref, dc_ref):
        cv = c_ref[...]
        s = (cv * _sigmoid(cv)).astype(BF16)
        gw_ref[...] = _dot_tn(s, dm_ref[...].astype(BF16))
        dc_ref[...] = _dot_nt(dm_ref[0:8, :].astype(BF16), w_ref[...].astype(BF16))

    return pl.pallas_call(
        body, name="ada_bwd",
        out_shape=[jax.ShapeDtypeStruct((D, n), F32), jax.ShapeDtypeStruct((8, D), F32)],
        compiler_params=pltpu.CompilerParams(vmem_limit_bytes=VMEM_LIMIT))(cs, w, dm)


def _adamw(w, g, m, v, name):
    r, c = w.shape
    rb = r
    if r % 8 == 0 and r * c * 4 > (1 << 20):
        rb = 8
        for cand in range(8, r, 8):
            if r % cand == 0 and cand * c * 4 <= (1 << 20):
                rb = cand

    def update(wv, gv, mv, vv):
        mn = ADAM_B1 * mv + (1.0 - ADAM_B1) * gv
        vn = ADAM_B2 * vv + (1.0 - ADAM_B2) * (gv * gv)
        m_hat = mn / (1.0 - ADAM_B1 ** ADAM_STEP)
        v_hat = vn / (1.0 - ADAM_B2 ** ADAM_STEP)
        return gv, -ADAM_LR * (m_hat / (jnp.sqrt(v_hat) + ADAM_EPS) + ADAM_WD * wv), mn, vn

    shp = jax.ShapeDtypeStruct((r, c), F32)
    n = r // rb
    if n < 2 * ADAM_NBUF:
        def body(w_ref, g_ref, m_ref, v_ref, go_ref, d_ref, nm_ref, nv_ref):
            go_ref[...], d_ref[...], nm_ref[...], nv_ref[...] = update(w_ref[...], g_ref[...], m_ref[...], v_ref[...])

        spec = pl.BlockSpec((rb, c), lambda i: (i, 0))
        return pl.pallas_call(body, name=name, grid=(n,), in_specs=[spec] * 4, out_specs=[spec] * 4,
                              out_shape=[shp] * 4, compiler_params=_cp(1))(w, g, m, v)

    nb = ADAM_NBUF

    def body(*refs):
        ins, outs, (ibuf, obuf, isem, osem) = refs[:4], refs[4:8], refs[8:]

        def rows(i):
            return pl.ds(pl.multiple_of(i * rb, 8), rb)

        def load(k, i, slot):
            return pltpu.make_async_copy(ins[k].at[rows(i), :], ibuf.at[k, slot], isem.at[k, slot])

        def store(k, i, slot):
            return pltpu.make_async_copy(obuf.at[k, slot], outs[k].at[rows(i), :], osem.at[k, slot])

        for s in range(nb):
            for k in range(4):
                load(k, s, s).start()

        def step(i, carry):
            slot = i % nb
            for k in range(4):
                load(k, i, slot).wait()

            @pl.when(i >= nb)
            def _():
                for k in range(4):
                    store(k, i - nb, slot).wait()

            res = update(ibuf[0, slot], ibuf[1, slot], ibuf[2, slot], ibuf[3, slot])
            for k in range(4):
                obuf[k, slot] = res[k]
                store(k, i, slot).start()

            @pl.when(i + nb < n)
            def _():
                for k in range(4):
                    load(k, i + nb, slot).start()

            return carry

        lax.fori_loop(0, n, step, 0)
        for s in range(nb):
            for k in range(4):
                store(k, n - nb + s, (n - nb + s) % nb).wait()

    any_spec = pl.BlockSpec(memory_space=pl.ANY)
    return pl.pallas_call(
        body, name=name, in_specs=[any_spec] * 4, out_specs=[any_spec] * 4, out_shape=[shp] * 4,
        scratch_shapes=[pltpu.VMEM((4, nb, rb, c), F32), pltpu.VMEM((4, nb, rb, c), F32),
                        pltpu.SemaphoreType.DMA((4, nb)), pltpu.SemaphoreType.DMA((4, nb))],
        compiler_params=pltpu.CompilerParams(vmem_limit_bytes=VMEM_LIMIT))(w, g, m, v)


BIG = ("ffn1_w_in", "ffn1_w_out", "w_mix_in", "w_mix_out", "ffn2_w_in", "ffn2_w_out")


def _half_rows(w, c):
    half = w.shape[0] // 2
    return lax.dynamic_slice_in_dim(w, c * half, half, axis=0)


def _lower_bounds(logits):
    return jnp.cumsum(jax.nn.softmax(logits.astype(F32), axis=1), axis=1)[:, 0]


def kernel(x, c, ctx, c_ctx, w_ada, b_ada, ln_gain, ln_bias, ffn1_w_in, ffn1_w_out, w_mix_in, gla_a2_fwd, gla_a2_bwd, gla_a_bias_fwd, gla_a_bias_bwd, hgrn_lb_logits, gla_norm_gain, hgrn_norm_gain, w_mix_out, ffn2_w_in, ffn2_w_out, loss_target, m_c_ctx, m_w_ada, m_b_ada, m_ln_gain, m_ln_bias, m_ffn1_w_in, m_ffn1_w_out, m_w_mix_in, m_gla_a2_fwd, m_gla_a2_bwd, m_gla_a_bias_fwd, m_gla_a_bias_bwd, m_hgrn_lb_logits, m_gla_norm_gain, m_hgrn_norm_gain, m_w_mix_out, m_ffn2_w_in, m_ffn2_w_out, v_c_ctx, v_w_ada, v_b_ada, v_ln_gain, v_ln_bias, v_ffn1_w_in, v_ffn1_w_out, v_w_mix_in, v_gla_a2_fwd, v_gla_a2_bwd, v_gla_a_bias_fwd, v_gla_a_bias_bwd, v_hgrn_lb_logits, v_gla_norm_gain, v_hgrn_norm_gain, v_w_mix_out, v_ffn2_w_in, v_ffn2_w_out):
    xi, yi, ci = lax.axis_index("x"), lax.axis_index("y"), lax.axis_index("c")
    chip = 2 * xi + yi
    dev = 2 * chip + ci
    B = x.shape[0]
    weights = dict(ffn1_w_in=ffn1_w_in[0], ffn1_w_out=ffn1_w_out[0], w_mix_in=w_mix_in[0], w_mix_out=w_mix_out[0],
                   ffn2_w_in=ffn2_w_in[0], ffn2_w_out=ffn2_w_out[0])

    mine = jnp.concatenate([c.reshape(-1), ln_gain.reshape(-1), ln_bias.reshape(-1), gla_a2_fwd.reshape(-1),
                            gla_a2_bwd.reshape(-1), hgrn_lb_logits.reshape(-1)])
    g1 = _gather_flat(mine, "gather_cond")
    nc = B * D
    c_all = g1[:, :nc].reshape(8 * B, D)
    per_chip = g1[0::2, nc:]
    o = 0

    def take(shape, axis):
        nonlocal o
        n = int(np.prod(shape))
        parts = per_chip[:, o:o + n].reshape((4,) + shape)
        o += n
        return jnp.concatenate([parts[j] for j in range(4)], axis=axis)

    ln_gain_f = take((3, 256), 1)
    ln_bias_f = take((3, 256), 1)
    a2f_f = take((16, 64), 1)
    a2b_f = take((16, 64), 1)
    lbl_f = take((2, 2, 128), 2)
    lb, lb_vjp = jax.vjp(_lower_bounds, lbl_f)

    assert B + 1 <= 8
    cs = jnp.concatenate([c_all.reshape(8, B, D), jnp.broadcast_to(c_ctx.reshape(1, 1, D), (8, 1, D)),
                          jnp.zeros((8, 7 - B, D), F32)], axis=1).reshape(ADA_ROWS, D)
    ncol = w_ada.shape[2]
    b_cols = lax.dynamic_slice_in_dim(b_ada, chip * ncol, ncol, axis=1)
    m4 = _mod_exchange(_ada_fwd(cs, w_ada[0], b_cols))
    m_all = jnp.concatenate([m4[j] for j in range(4)], axis=1)
    m_lat = m_all[:B].reshape(B, 1, N_MOD, D)
    m_ctx = jnp.broadcast_to(m_all[B].reshape(1, 1, N_MOD, D), (B, 1, N_MOD, D))

    groups = dict(ffn2=("ffn2_w_in", "ffn2_w_out"), mix=("w_mix_in", "w_mix_out"), ffn1=("ffn1_w_in", "ffn1_w_out"))
    shards = dict(weights, w_mix_in=jnp.pad(weights["w_mix_in"], ((0, 0), (0, MIX_NP - MIX_N))))
    blks = {k: _half_rows(shards[k], ci).astype(BF16) for k in BIG}
    gathering = {}
    token = m_all
    for group in ("ffn1", "mix", "ffn2"):
        sems, x_thru, l_thru, token = _split_start(_gather_copies, 4, [blks[k] for k in groups[group]], 8, token,
                                                   "weight_gather_start_" + group)
        gathering[group] = (sems, x_thru, l_thru)
    mvec = jnp.concatenate([m_ctx, m_lat], axis=1) + token[0, 0]

    forwarding = {}

    def prefetch(group, after):
        _, got = _split_wait(_gather_copies, 4, *gathering[group], after, "weight_gather_wait_" + group)
        forwarding[group] = _split_start(_forward_copies, 4, list(got), 1, None, "weight_gather_forward_start_" + group)
        return forwarding[group][3]

    def weights_for(group, after):
        names = groups[group]
        if group in forwarding:
            sems, g_thru, l_thru, _ = forwarding[group]
            (w_in, w_out), _ = _split_wait(_forward_copies, 4, sems, g_thru, l_thru, after,
                                           "weight_gather_forward_wait_" + group)
        else:
            _, got = _split_wait(_gather_copies, 4, *gathering[group], after, "weight_gather_wait_" + group)
            w_in, w_out = _gather_forward(got, "weight_gather_forward_" + group)
        if group == "mix":
            return _mix_in_to_padded(w_in.reshape(4, D, MIX_NP)), w_out.reshape(-1, D)
        return w_in.reshape((4,) + shards[names[0]].shape), w_out.reshape(-1, D)

    cvec = ci.reshape(1).astype(jnp.int32)
    cjvec = jnp.stack([ci, chip]).astype(jnp.int32)
    in_flight = {}

    def on_grads(group, gs):
        names = groups[group]
        if group == "mix":
            gs = (_mix_in_from_padded(gs[0]), gs[1])
        g8s = [g.reshape((8, shards[k].shape[0] // 2, shards[k].shape[1])) for k, g in zip(names, gs)]
        sems, g_thru, l_thru, token = _split_start(_pair_copies, 4, g8s, 4, None, "grad_pair_start_" + group)
        in_flight[group] = (sems, g_thru, l_thru)
        return token

    def on_sent(group, after):
        g8s, r4s = _split_wait(_pair_copies, 4, *in_flight[group], after, "grad_pair_wait_" + group)
        h4s = [_rs_add_pair(g, r, cvec, "grad_pair_add_" + k) for k, g, r in zip(groups[group], g8s, r4s)]
        sems, h_thru, l_thru, token = _split_start(_chip_copies, 3, h4s, 3, None, "grad_chip_start_" + group)
        in_flight[group] = (g8s, r4s, sems, h_thru, l_thru)
        return token

    loss_l, grad_x, dm, small = _local_step(
        x, ctx, loss_target, mvec, weights_for, prefetch, ln_gain_f, ln_bias_f, a2f_f, a2b_f,
        gla_a_bias_fwd, gla_a_bias_bwd, lb, gla_norm_gain, hgrn_norm_gain, on_grads, on_sent)
    loss = lax.psum(loss_l, ("x", "y", "c"))

    dm_lat = dm[:, 1].reshape(B, N_MOD * D)
    dm_ctx = jnp.sum(dm[:, 0], axis=0).reshape(N_MOD * D)
    keys = ("ln_gain", "ln_bias", "a2f", "a2b", "abf", "abb", "lb", "gng", "gnh")
    flat = jnp.concatenate([dm_lat.reshape(-1), dm_ctx] + [small[k].reshape(-1) for k in keys])
    nflat = flat.shape[0]
    npad = -(-nflat // 1024) * 1024
    flat2 = jnp.pad(flat, (0, npad - nflat)).reshape(8, npad // 8)
    s_sems, s_src, s_land, tok = _split_start(_all_copies, 7, [flat2], 8, None, "small_grads_start")

    fin = {}
    for group in ("ffn2", "mix", "ffn1"):
        g8s, r4s, sems, h_thru, l_thru = in_flight[group]
        _, r3s = _split_wait(_chip_copies, 3, sems, h_thru, l_thru, tok, "grad_chip_wait_" + group)
        for k, g, r4, r3 in zip(groups[group], g8s, r4s, r3s):
            fin[k] = _rs_add_chips(g, r4, r3, cjvec, "grad_chip_add_" + k)
    p_sems, p_src, p_land, tok = _split_start(_share_copies, 1, [fin[k] for k in BIG], 1, None, "grad_pair_share_start")

    _, (land,) = _split_wait(_all_copies, 7, s_sems, s_src, s_land, tok, "small_grads_wait")
    g3 = lax.dynamic_update_index_in_dim(land, flat2, dev, 0).reshape(8, npad)[:, :nflat]
    nlat = B * N_MOD * D
    dm_all = g3[:, :nlat].reshape(8 * B, N_MOD * D)
    tot = _sum8(g3[:, nlat:])[0]
    dmc_tot = tot[:N_MOD * D]
    o = N_MOD * D
    sg = {}
    for k in keys:
        n = int(np.prod(small[k].shape))
        sg[k] = tot[o:o + n].reshape(small[k].shape)
        o += n
    ctx_rows = jnp.zeros((8, 1, N_MOD * D), F32).at[0, 0].set(dmc_tot)
    dm_rows = jnp.concatenate([dm_all.reshape(8, B, -1), ctx_rows, jnp.zeros((8, 7 - B, N_MOD * D), F32)],
                              axis=1).reshape(ADA_ROWS, N_MOD * D)
    g_b_ada = (jnp.sum(dm_all, axis=0) + dmc_tot).reshape(1, N_MOD * D)
    g_w_ada, dcc = _ada_bwd(cs, w_ada[0], lax.dynamic_slice_in_dim(dm_rows, chip * ncol, ncol, axis=1))
    g4 = _gather_flat(dcc[B], "gather_cctx")
    dsilu = ((g4[0] + g4[2]) + g4[4]) + g4[6]
    sc = _sigmoid(c_ctx)
    g_c_ctx = dsilu * (sc * (1.0 + c_ctx * (1.0 - sc)))
    (g_lbl,) = lb_vjp(sg["lb"])

    def cols(a, n, axis):
        return lax.dynamic_slice_in_dim(a, chip * n, n, axis=axis)

    shared, _ = _split_wait(_share_copies, 1, p_sems, p_src, p_land, g_w_ada, "grad_pair_share_wait")
    gsh = {k: both.reshape(shards[k].shape)[:, :weights[k].shape[1]] for k, both in zip(BIG, shared)}

    grads = dict(
        c_ctx=g_c_ctx, w_ada=g_w_ada[None], b_ada=g_b_ada, ln_gain=cols(sg["ln_gain"], 256, 1)[None],
        ln_bias=cols(sg["ln_bias"], 256, 1)[None], ffn1_w_in=gsh["ffn1_w_in"][None], ffn1_w_out=gsh["ffn1_w_out"][None],
        w_mix_in=gsh["w_mix_in"][None], gla_a2_fwd=cols(sg["a2f"], 64, 1)[None], gla_a2_bwd=cols(sg["a2b"], 64, 1)[None],
        gla_a_bias_fwd=sg["abf"], gla_a_bias_bwd=sg["abb"], hgrn_lb_logits=cols(g_lbl, 128, 2),
        gla_norm_gain=sg["gng"].reshape(1, HD), hgrn_norm_gain=sg["gnh"].reshape(1, HD),
        w_mix_out=gsh["w_mix_out"][None], ffn2_w_in=gsh["ffn2_w_in"][None], ffn2_w_out=gsh["ffn2_w_out"][None])
    params = dict(
        c_ctx=(c_ctx, m_c_ctx, v_c_ctx), w_ada=(w_ada, m_w_ada, v_w_ada), b_ada=(b_ada, m_b_ada, v_b_ada),
        ln_gain=(ln_gain, m_ln_gain, v_ln_gain), ln_bias=(ln_bias, m_ln_bias, v_ln_bias),
        ffn1_w_in=(ffn1_w_in, m_ffn1_w_in, v_ffn1_w_in), ffn1_w_out=(ffn1_w_out, m_ffn1_w_out, v_ffn1_w_out),
        w_mix_in=(w_mix_in, m_w_mix_in, v_w_mix_in), gla_a2_fwd=(gla_a2_fwd, m_gla_a2_fwd, v_gla_a2_fwd),
        gla_a2_bwd=(gla_a2_bwd, m_gla_a2_bwd, v_gla_a2_bwd),
        gla_a_bias_fwd=(gla_a_bias_fwd, m_gla_a_bias_fwd, v_gla_a_bias_fwd),
        gla_a_bias_bwd=(gla_a_bias_bwd, m_gla_a_bias_bwd, v_gla_a_bias_bwd),
        hgrn_lb_logits=(hgrn_lb_logits, m_hgrn_lb_logits, v_hgrn_lb_logits),
        gla_norm_gain=(gla_norm_gain, m_gla_norm_gain, v_gla_norm_gain),
        hgrn_norm_gain=(hgrn_norm_gain, m_hgrn_norm_gain, v_hgrn_norm_gain),
        w_mix_out=(w_mix_out, m_w_mix_out, v_w_mix_out), ffn2_w_in=(ffn2_w_in, m_ffn2_w_in, v_ffn2_w_in),
        ffn2_w_out=(ffn2_w_out, m_ffn2_w_out, v_ffn2_w_out))
    order = list(params.keys())
    big_names = ("w_ada",) + BIG
    upd = {}
    for k in big_names:
        w_, m_, v_ = params[k]
        s2 = w_.shape[-2:]
        g_, d_, nm_, nv_ = _adamw(w_.reshape(s2), grads[k].reshape(s2), m_.reshape(s2), v_.reshape(s2), "adamw_" + k)
        grads[k] = g_
        upd[k] = (d_.reshape(w_.shape), nm_.reshape(w_.shape), nv_.reshape(w_.shape))
    small_names = [k for k in order if k not in big_names]
    sizes = [int(np.prod(params[k][0].shape)) for k in small_names]
    tot_n = sum(sizes)
    npad = -(-tot_n // 1024) * 1024

    def packed(get):
        flat_ = jnp.concatenate([get(k).reshape(-1) for k in small_names])
        return jnp.pad(flat_, (0, npad - tot_n)).reshape(8, npad // 8)

    _, d_s, nm_s, nv_s = _adamw(packed(lambda k: params[k][0]), packed(lambda k: grads[k]),
                                packed(lambda k: params[k][1]), packed(lambda k: params[k][2]), "adamw_small")
    o = 0
    for k, n in zip(small_names, sizes):
        shp = params[k][0].shape
        upd[k] = tuple(a.reshape(-1)[o:o + n].reshape(shp) for a in (d_s, nm_s, nv_s))
        o += n

    return (loss, grad_x, *[grads[k].reshape(params[k][0].shape) for k in order], *[upd[k][0] for k in order],
            *[upd[k][1] for k in order], *[upd[k][2] for k in order])
```
